```python
import jax, jax.numpy as jnp
from jax import lax
import numpy as np

D_MODEL = 1024
BATCH = 8
SEQ = 4096
DEPTH = 1

MEM_LEN = 256
D_MIX = D_MODEL
POOL_WINDOWS = (2, 4, 8, 16)
N_POOL_GROUPS = len(POOL_WINDOWS)
D_POOL = D_MIX // 4
POOL_GROUP_DIM = D_POOL // N_POOL_GROUPS
D_FOX = D_MIX - D_POOL
FOX_HEAD_DIM = 64
FOX_HEADS = D_FOX // FOX_HEAD_DIM
Q_BLOCK = 128
XA_HEADS = 4
XA_HEAD_DIM = D_MODEL // XA_HEADS
D_FF = 4 * D_MODEL
CONV_WIDTH = 3
NORM_EPS = 1e-6
D_IN = D_POOL + 3 * D_FOX + FOX_HEADS

kernel_name = 'hybrid_pool_fox_memxattn_convffn'


def rms_norm(x, g):
    xf = x.astype(jnp.float32)
    y = xf * lax.rsqrt(jnp.mean(xf * xf, axis=-1, keepdims=True) + NORM_EPS)
    return (y * g.astype(jnp.float32)).astype(x.dtype)


def pool_mixer(u, w_pool, pool_scale):
    b, s, _ = u.shape
    ug = u.astype(jnp.float32).reshape(b, s, N_POOL_GROUPS, POOL_GROUP_DIM)
    csum = jnp.pad(jnp.cumsum(ug, axis=1), ((0, 0), (1, 0), (0, 0), (0, 0)))
    t1 = jnp.arange(1, s + 1, dtype=jnp.float32)
    pooled = []
    for g, w in enumerate(POOL_WINDOWS):
        c = csum[:, :, g]
        lower = jnp.concatenate([jnp.zeros((b, w - 1, POOL_GROUP_DIM), c.dtype), c[:, :s + 1 - w]], axis=1)
        win_sum = c[:, 1:] - lower
        count = jnp.minimum(t1, float(w))[None, :, None]
        pooled.append(win_sum / count)
    pooled = jnp.stack(pooled, axis=2)
    diff = (pooled - ug).astype(u.dtype)
    mixed = jnp.einsum('bsgc,gcd->bsgd', diff, w_pool)
    return (mixed * pool_scale).reshape(b, s, D_POOL)


def forgetting_attention(q, k, v, log_f):
    b, s, h, dh = q.shape
    cum = jnp.cumsum(log_f, axis=1).transpose(0, 2, 1)
    scale = dh ** -0.5
    outs = []
    for i in range(s // Q_BLOCK):
        q0, q1 = i * Q_BLOCK, (i + 1) * Q_BLOCK
        qb, kb, vb = q[:, q0:q1], k[:, :q1], v[:, :q1]
        logits = jnp.einsum('bqhd,bkhd->bhqk', qb, kb).astype(jnp.float32) * scale
        decay = cum[:, :, q0:q1, None] - cum[:, :, None, :q1]
        mask = (q0 + jnp.arange(Q_BLOCK))[:, None] >= jnp.arange(q1)[None, :]
        logits = jnp.where(mask, logits + decay, -jnp.inf)
        probs = jax.nn.softmax(logits, axis=-1)
        outs.append(jnp.einsum('bhqk,bkhd->bqhd', probs.astype(v.dtype), vb))
    return jnp.concatenate(outs, axis=1)


def memory_cross_attention(h, mem_n, w_xq, w_xkv, w_xo):
    b, s, _ = h.shape
    m = mem_n.shape[1]
    q = (h @ w_xq).reshape(b, s, XA_HEADS, XA_HEAD_DIM)
    kv = mem_n @ w_xkv
    k = kv[..., :D_MODEL].reshape(b, m, XA_HEADS, XA_HEAD_DIM)
    v = kv[..., D_MODEL:].reshape(b, m, XA_HEADS, XA_HEAD_DIM)
    logits = jnp.einsum('bshd,bmhd->bhsm', q, k).astype(jnp.float32) * (XA_HEAD_DIM ** -0.5)
    probs = jax.nn.softmax(logits, axis=-1)
    out = jnp.einsum('bhsm,bmhd->bshd', probs.astype(v.dtype), v).reshape(b, s, D_MODEL)
    return out @ w_xo


def conv_gated_mlp(h, w_up, conv_w, conv_b, w_down):
    hid = h @ w_up
    hid = lax.conv_general_dilated(
        hid, conv_w, window_strides=(1,), padding=[(CONV_WIDTH - 1, 0)],
        dimension_numbers=('NWC', 'WIO', 'NWC'), feature_group_count=2 * D_FF) + conv_b
    gate, up = hid[..., :D_FF], hid[..., D_FF:]
    return (jax.nn.gelu(gate, approximate=True) * up) @ w_down


def _fwd_setup_inputs(seed: int = 0) -> dict:
    key = jax.random.key(seed)
    ks = jax.random.split(key, 22)
    nrm = lambda k, shape, s: jax.random.normal(k, shape, jnp.float32) * s
    gain = lambda k: 1.0 + 0.1 * jax.random.normal(k, (DEPTH, D_MODEL), jnp.float32)
    return {
        'x': jax.random.normal(ks[0], (BATCH, SEQ, D_MODEL), jnp.float32),
        'mem': jax.random.normal(ks[1], (BATCH, MEM_LEN, D_MODEL), jnp.float32),
        'norm_mix_pre': gain(ks[2]),
        'norm_mix_post': gain(ks[3]),
        'w_in': nrm(ks[4], (DEPTH, D_MODEL, D_IN), D_MODEL ** -0.5),
        'b_forget': jax.random.uniform(ks[5], (DEPTH, FOX_HEADS), jnp.float32, 1.0, 6.0),
        'w_pool': nrm(ks[6], (DEPTH, N_POOL_GROUPS, POOL_GROUP_DIM, POOL_GROUP_DIM), POOL_GROUP_DIM ** -0.5),
        'pool_scale': 1.0 + 0.1 * jax.random.normal(ks[7], (DEPTH, N_POOL_GROUPS, POOL_GROUP_DIM), jnp.float32),
        'w_mix_out': nrm(ks[8], (DEPTH, D_MIX, D_MODEL), D_MIX ** -0.5),
        'norm_mem': gain(ks[9]),
        'norm_xa_pre': gain(ks[10]),
        'norm_xa_post': gain(ks[11]),
        'w_xq': nrm(ks[12], (DEPTH, D_MODEL, D_MODEL), D_MODEL ** -0.5),
        'w_xkv': nrm(ks[13], (DEPTH, D_MODEL, 2 * D_MODEL), D_MODEL ** -0.5),
        'w_xo': nrm(ks[14], (DEPTH, D_MODEL, D_MODEL), D_MODEL ** -0.5),
        'norm_ffn_pre': gain(ks[15]),
        'norm_ffn_post': gain(ks[16]),
        'w_up': nrm(ks[17], (DEPTH, D_MODEL, 2 * D_FF), D_MODEL ** -0.5),
        'conv_w': nrm(ks[18], (DEPTH, CONV_WIDTH, 1, 2 * D_FF), CONV_WIDTH ** -0.5),
        'conv_b': nrm(ks[19], (DEPTH, 2 * D_FF), 0.02),
        'w_down': nrm(ks[20], (DEPTH, D_FF, D_MODEL), D_FF ** -0.5),
    }


def _fwd_reference(x, mem, norm_mix_pre, norm_mix_post, w_in, b_forget, w_pool, pool_scale, w_mix_out,
              norm_mem, norm_xa_pre, norm_xa_post, w_xq, w_xkv, w_xo,
              norm_ffn_pre, norm_ffn_post, w_up, conv_w, conv_b, w_down):
    b, s, _ = x.shape
    for l in range(DEPTH):
        h = rms_norm(x, norm_mix_pre[l])
        proj = h @ w_in[l]
        u_pool = proj[..., :D_POOL]
        o = D_POOL
        q = proj[..., o:o + D_FOX].reshape(b, s, FOX_HEADS, FOX_HEAD_DIM)
        k = proj[..., o + D_FOX:o + 2 * D_FOX].reshape(b, s, FOX_HEADS, FOX_HEAD_DIM)
        v = proj[..., o + 2 * D_FOX:o + 3 * D_FOX].reshape(b, s, FOX_HEADS, FOX_HEAD_DIM)
        f_logit = proj[..., o + 3 * D_FOX:].astype(jnp.float32) + b_forget[l].astype(jnp.float32)
        log_f = jax.nn.log_sigmoid(f_logit)
        y_pool = pool_mixer(u_pool, w_pool[l], pool_scale[l])
        y_fox = forgetting_attention(q, k, v, log_f).reshape(b, s, D_FOX)
        y = jnp.concatenate([y_pool, y_fox], axis=-1) @ w_mix_out[l]
        x = x + rms_norm(y, norm_mix_post[l])
        h = rms_norm(x, norm_xa_pre[l])
        mem_n = rms_norm(mem, norm_mem[l])
        y = memory_cross_attention(h, mem_n, w_xq[l], w_xkv[l], w_xo[l])
        x = x + rms_norm(y, norm_xa_post[l])
        h = rms_norm(x, norm_ffn_pre[l])
        y = conv_gated_mlp(h, w_up[l], conv_w[l], conv_b[l], w_down[l])
        x = x + rms_norm(y, norm_ffn_post[l])
    return x


import jax as _jax
import jax.numpy as _jnp

TWIN_FORMAT = 'train_step'
FWD_PARAMS = ['x', 'mem', 'norm_mix_pre', 'norm_mix_post', 'w_in', 'b_forget', 'w_pool', 'pool_scale', 'w_mix_out', 'norm_mem', 'norm_xa_pre', 'norm_xa_post', 'w_xq', 'w_xkv', 'w_xo', 'norm_ffn_pre', 'norm_ffn_post', 'w_up', 'conv_w', 'conv_b', 'w_down']
TWIN_WEIGHTS = ['norm_mix_pre', 'norm_mix_post', 'w_in', 'b_forget', 'w_pool', 'pool_scale', 'w_mix_out', 'norm_mem', 'norm_xa_pre', 'norm_xa_post', 'w_xq', 'w_xkv', 'w_xo', 'norm_ffn_pre', 'norm_ffn_post', 'w_up', 'conv_w', 'conv_b', 'w_down']
TWIN_DIFF_INPUT = 'x'
TWIN_INPUTS = ['x', 'mem', 'norm_mix_pre', 'norm_mix_post', 'w_in', 'b_forget', 'w_pool', 'pool_scale', 'w_mix_out', 'norm_mem', 'norm_xa_pre', 'norm_xa_post', 'w_xq', 'w_xkv', 'w_xo', 'norm_ffn_pre', 'norm_ffn_post', 'w_up', 'conv_w', 'conv_b', 'w_down', 'loss_target', 'm_norm_mix_pre', 'm_norm_mix_post', 'm_w_in', 'm_b_forget', 'm_w_pool', 'm_pool_scale', 'm_w_mix_out', 'm_norm_mem', 'm_norm_xa_pre', 'm_norm_xa_post', 'm_w_xq', 'm_w_xkv', 'm_w_xo', 'm_norm_ffn_pre', 'm_norm_ffn_post', 'm_w_up', 'm_conv_w', 'm_conv_b', 'm_w_down', 'v_norm_mix_pre', 'v_norm_mix_post', 'v_w_in', 'v_b_forget', 'v_w_pool', 'v_pool_scale', 'v_w_mix_out', 'v_norm_mem', 'v_norm_xa_pre', 'v_norm_xa_post', 'v_w_xq', 'v_w_xkv', 'v_w_xo', 'v_norm_ffn_pre', 'v_norm_ffn_post', 'v_w_up', 'v_conv_w', 'v_conv_b', 'v_w_down']
TWIN_OUTPUTS = ['loss', 'grad_x', 'grad_norm_mix_pre', 'grad_norm_mix_post', 'grad_w_in', 'grad_b_forget', 'grad_w_pool', 'grad_pool_scale', 'grad_w_mix_out', 'grad_norm_mem', 'grad_norm_xa_pre', 'grad_norm_xa_post', 'grad_w_xq', 'grad_w_xkv', 'grad_w_xo', 'grad_norm_ffn_pre', 'grad_norm_ffn_post', 'grad_w_up', 'grad_conv_w', 'grad_conv_b', 'grad_w_down', 'delta_norm_mix_pre', 'delta_norm_mix_post', 'delta_w_in', 'delta_b_forget', 'delta_w_pool', 'delta_pool_scale', 'delta_w_mix_out', 'delta_norm_mem', 'delta_norm_xa_pre', 'delta_norm_xa_post', 'delta_w_xq', 'delta_w_xkv', 'delta_w_xo', 'delta_norm_ffn_pre', 'delta_norm_ffn_post', 'delta_w_up', 'delta_conv_w', 'delta_conv_b', 'delta_w_down', 'new_m_norm_mix_pre', 'new_m_norm_mix_post', 'new_m_w_in', 'new_m_b_forget', 'new_m_w_pool', 'new_m_pool_scale', 'new_m_w_mix_out', 'new_m_norm_mem', 'new_m_norm_xa_pre', 'new_m_norm_xa_post', 'new_m_w_xq', 'new_m_w_xkv', 'new_m_w_xo', 'new_m_norm_ffn_pre', 'new_m_norm_ffn_post', 'new_m_w_up', 'new_m_conv_w', 'new_m_conv_b', 'new_m_w_down', 'new_v_norm_mix_pre', 'new_v_norm_mix_post', 'new_v_w_in', 'new_v_b_forget', 'new_v_w_pool', 'new_v_pool_scale', 'new_v_w_mix_out', 'new_v_norm_mem', 'new_v_norm_xa_pre', 'new_v_norm_xa_post', 'new_v_w_xq', 'new_v_w_xkv', 'new_v_w_xo', 'new_v_norm_ffn_pre', 'new_v_norm_ffn_post', 'new_v_w_up', 'new_v_conv_w', 'new_v_conv_b', 'new_v_w_down']
TWIN_LEAF_KINDS = {'loss': 'loss', 'grad_x': 'grad_x', 'grad_norm_mix_pre': 'grad_w', 'grad_norm_mix_post': 'grad_w', 'grad_w_in': 'grad_w', 'grad_b_forget': 'grad_w', 'grad_w_pool': 'grad_w', 'grad_pool_scale': 'grad_w', 'grad_w_mix_out': 'grad_w', 'grad_norm_mem': 'grad_w', 'grad_norm_xa_pre': 'grad_w', 'grad_norm_xa_post': 'grad_w', 'grad_w_xq': 'grad_w', 'grad_w_xkv': 'grad_w', 'grad_w_xo': 'grad_w', 'grad_norm_ffn_pre': 'grad_w', 'grad_norm_ffn_post': 'grad_w', 'grad_w_up': 'grad_w', 'grad_conv_w': 'grad_w', 'grad_conv_b': 'grad_w', 'grad_w_down': 'grad_w', 'delta_norm_mix_pre': 'delta_w', 'delta_norm_mix_post': 'delta_w', 'delta_w_in': 'delta_w', 'delta_b_forget': 'delta_w', 'delta_w_pool': 'delta_w', 'delta_pool_scale': 'delta_w', 'delta_w_mix_out': 'delta_w', 'delta_norm_mem': 'delta_w', 'delta_norm_xa_pre': 'delta_w', 'delta_norm_xa_post': 'delta_w', 'delta_w_xq': 'delta_w', 'delta_w_xkv': 'delta_w', 'delta_w_xo': 'delta_w', 'delta_norm_ffn_pre': 'delta_w', 'delta_norm_ffn_post': 'delta_w', 'delta_w_up': 'delta_w', 'delta_conv_w': 'delta_w', 'delta_conv_b': 'delta_w', 'delta_w_down': 'delta_w', 'new_m_norm_mix_pre': 'new_m', 'new_m_norm_mix_post': 'new_m', 'new_m_w_in': 'new_m', 'new_m_b_forget': 'new_m', 'new_m_w_pool': 'new_m', 'new_m_pool_scale': 'new_m', 'new_m_w_mix_out': 'new_m', 'new_m_norm_mem': 'new_m', 'new_m_norm_xa_pre': 'new_m', 'new_m_norm_xa_post': 'new_m', 'new_m_w_xq': 'new_m', 'new_m_w_xkv': 'new_m', 'new_m_w_xo': 'new_m', 'new_m_norm_ffn_pre': 'new_m', 'new_m_norm_ffn_post': 'new_m', 'new_m_w_up': 'new_m', 'new_m_conv_w': 'new_m', 'new_m_conv_b': 'new_m', 'new_m_w_down': 'new_m', 'new_v_norm_mix_pre': 'new_v', 'new_v_norm_mix_post': 'new_v', 'new_v_w_in': 'new_v', 'new_v_b_forget': 'new_v', 'new_v_w_pool': 'new_v', 'new_v_pool_scale': 'new_v', 'new_v_w_mix_out': 'new_v', 'new_v_norm_mem': 'new_v', 'new_v_norm_xa_pre': 'new_v', 'new_v_norm_xa_post': 'new_v', 'new_v_w_xq': 'new_v', 'new_v_w_xkv': 'new_v', 'new_v_w_xo': 'new_v', 'new_v_norm_ffn_pre': 'new_v', 'new_v_norm_ffn_post': 'new_v', 'new_v_w_up': 'new_v', 'new_v_conv_w': 'new_v', 'new_v_conv_b': 'new_v', 'new_v_w_down': 'new_v'}


def _forward(args):
    return _fwd_reference(*[args[k] for k in FWD_PARAMS])


def _output_shape():
    def fwd():
        inp = _fwd_setup_inputs(0)
        return _fwd_reference(*[inp[k] for k in FWD_PARAMS])
    out = _jax.eval_shape(fwd)
    return out.shape, out.dtype

N_MICROBATCH = 1
ADAM_LR = 0.001
ADAM_B1 = 0.9
ADAM_B2 = 0.999
ADAM_EPS = 1e-08
ADAM_WD = 0.01
ADAM_STEP = 10
PER_EXAMPLE_BATCH_AXIS = {'x': 0, 'mem': 0, 'loss_target': 0}
SHARED_INPUTS = []
_WEIGHT_DTYPES = {'norm_mix_pre': _jnp.float32, 'norm_mix_post': _jnp.float32, 'w_in': _jnp.float32, 'b_forget': _jnp.float32, 'w_pool': _jnp.float32, 'pool_scale': _jnp.float32, 'w_mix_out': _jnp.float32, 'norm_mem': _jnp.float32, 'norm_xa_pre': _jnp.float32, 'norm_xa_post': _jnp.float32, 'w_xq': _jnp.float32, 'w_xkv': _jnp.float32, 'w_xo': _jnp.float32, 'norm_ffn_pre': _jnp.float32, 'norm_ffn_post': _jnp.float32, 'w_up': _jnp.float32, 'conv_w': _jnp.float32, 'conv_b': _jnp.float32, 'w_down': _jnp.float32}
MOMENT_SCALE = {'norm_mix_pre': 9.350892e-01, 'norm_mix_post': 3.235758e+01, 'w_in': 6.046165e-01, 'b_forget': 1.588714e+00, 'w_pool': 2.310045e+00, 'pool_scale': 3.330879e+00, 'w_mix_out': 1.364573e+00, 'norm_mem': 2.272438e+00, 'norm_xa_pre': 7.342365e-01, 'norm_xa_post': 3.269725e+01, 'w_xq': 8.390341e-01, 'w_xkv': 1.441401e+00, 'w_xo': 1.954564e+00, 'norm_ffn_pre': 1.280708e+00, 'norm_ffn_post': 3.224637e+01, 'w_up': 4.761459e-01, 'conv_w': 5.707757e-01, 'conv_b': 1.465860e+00, 'w_down': 1.275745e+00}


def _to_microbatches(a, axis):
    t = _jnp.moveaxis(a, axis, 0)
    t = t.reshape((N_MICROBATCH, t.shape[0] // N_MICROBATCH) + t.shape[1:])
    return _jnp.moveaxis(t, 1, axis + 1)


def setup_inputs(seed: int = 0) -> dict:
    inp = _fwd_setup_inputs(seed)
    key = _jax.random.fold_in(_jax.random.key(seed), 7919)
    shape, _ = _output_shape()
    out = dict(inp)
    out["loss_target"] = _jax.random.normal(_jax.random.fold_in(key, 0), shape, _jnp.float32)
    for i, name in enumerate(TWIN_WEIGHTS):
        w = inp[name].astype(_jnp.float32)
        if MOMENT_SCALE is None:
            s = _jnp.sqrt(_jnp.mean(_jnp.square(w)) + 1e-30)
        else:
            s = MOMENT_SCALE[name]
        km, kv = _jax.random.split(_jax.random.fold_in(key, i + 1))
        out[name] = w
        out["m_" + name] = s * _jax.random.normal(km, w.shape, _jnp.float32)
        out["v_" + name] = (s * s) * _jax.random.uniform(kv, w.shape, _jnp.float32, 0.5, 1.5)
    if N_MICROBATCH > 1:
        for name, axis in PER_EXAMPLE_BATCH_AXIS.items():
            out[name] = _to_microbatches(out[name], axis)
    return {'x': out['x'], 'mem': out['mem'], 'norm_mix_pre': out['norm_mix_pre'], 'norm_mix_post': out['norm_mix_post'], 'w_in': out['w_in'], 'b_forget': out['b_forget'], 'w_pool': out['w_pool'], 'pool_scale': out['pool_scale'], 'w_mix_out': out['w_mix_out'], 'norm_mem': out['norm_mem'], 'norm_xa_pre': out['norm_xa_pre'], 'norm_xa_post': out['norm_xa_post'], 'w_xq': out['w_xq'], 'w_xkv': out['w_xkv'], 'w_xo': out['w_xo'], 'norm_ffn_pre': out['norm_ffn_pre'], 'norm_ffn_post': out['norm_ffn_post'], 'w_up': out['w_up'], 'conv_w': out['conv_w'], 'conv_b': out['conv_b'], 'w_down': out['w_down'], 'loss_target': out['loss_target'], 'm_norm_mix_pre': out['m_norm_mix_pre'], 'm_norm_mix_post': out['m_norm_mix_post'], 'm_w_in': out['m_w_in'], 'm_b_forget': out['m_b_forget'], 'm_w_pool': out['m_w_pool'], 'm_pool_scale': out['m_pool_scale'], 'm_w_mix_out': out['m_w_mix_out'], 'm_norm_mem': out['m_norm_mem'], 'm_norm_xa_pre': out['m_norm_xa_pre'], 'm_norm_xa_post': out['m_norm_xa_post'], 'm_w_xq': out['m_w_xq'], 'm_w_xkv': out['m_w_xkv'], 'm_w_xo': out['m_w_xo'], 'm_norm_ffn_pre': out['m_norm_ffn_pre'], 'm_norm_ffn_post': out['m_norm_ffn_post'], 'm_w_up': out['m_w_up'], 'm_conv_w': out['m_conv_w'], 'm_conv_b': out['m_conv_b'], 'm_w_down': out['m_w_down'], 'v_norm_mix_pre': out['v_norm_mix_pre'], 'v_norm_mix_post': out['v_norm_mix_post'], 'v_w_in': out['v_w_in'], 'v_b_forget': out['v_b_forget'], 'v_w_pool': out['v_w_pool'], 'v_pool_scale': out['v_pool_scale'], 'v_w_mix_out': out['v_w_mix_out'], 'v_norm_mem': out['v_norm_mem'], 'v_norm_xa_pre': out['v_norm_xa_pre'], 'v_norm_xa_post': out['v_norm_xa_post'], 'v_w_xq': out['v_w_xq'], 'v_w_xkv': out['v_w_xkv'], 'v_w_xo': out['v_w_xo'], 'v_norm_ffn_pre': out['v_norm_ffn_pre'], 'v_norm_ffn_post': out['v_norm_ffn_post'], 'v_w_up': out['v_w_up'], 'v_conv_w': out['v_conv_w'], 'v_conv_b': out['v_conv_b'], 'v_w_down': out['v_w_down']}


def _loss(weights, diff, rest, loss_target):
    with _jax.named_scope("forward"):
        args = {**rest, TWIN_DIFF_INPUT: diff, **{k: w.astype(_WEIGHT_DTYPES[k]) for k, w in weights.items()}}
        y = _forward(args)
    with _jax.named_scope("loss_head"):
        err = _jnp.square(y.astype(_jnp.float32) - loss_target)
        return 0.5 * _jnp.sum(_jnp.mean(err, axis=-1)) if err.ndim else 0.5 * err


def _adamw(w, g, m, v):
    m = ADAM_B1 * m + (1.0 - ADAM_B1) * g
    v = ADAM_B2 * v + (1.0 - ADAM_B2) * _jnp.square(g)
    m_hat = m / (1.0 - ADAM_B1 ** ADAM_STEP)
    v_hat = v / (1.0 - ADAM_B2 ** ADAM_STEP)
    delta = -ADAM_LR * (m_hat / (_jnp.sqrt(v_hat) + ADAM_EPS) + ADAM_WD * w)
    return delta, m, v


def reference(x, mem, norm_mix_pre, norm_mix_post, w_in, b_forget, w_pool, pool_scale, w_mix_out, norm_mem, norm_xa_pre, norm_xa_post, w_xq, w_xkv, w_xo, norm_ffn_pre, norm_ffn_post, w_up, conv_w, conv_b, w_down, loss_target, m_norm_mix_pre, m_norm_mix_post, m_w_in, m_b_forget, m_w_pool, m_pool_scale, m_w_mix_out, m_norm_mem, m_norm_xa_pre, m_norm_xa_post, m_w_xq, m_w_xkv, m_w_xo, m_norm_ffn_pre, m_norm_ffn_post, m_w_up, m_conv_w, m_conv_b, m_w_down, v_norm_mix_pre, v_norm_mix_post, v_w_in, v_b_forget, v_w_pool, v_pool_scale, v_w_mix_out, v_norm_mem, v_norm_xa_pre, v_norm_xa_post, v_w_xq, v_w_xkv, v_w_xo, v_norm_ffn_pre, v_norm_ffn_post, v_w_up, v_conv_w, v_conv_b, v_w_down):
    given = dict(x=x, mem=mem, norm_mix_pre=norm_mix_pre, norm_mix_post=norm_mix_post, w_in=w_in, b_forget=b_forget, w_pool=w_pool, pool_scale=pool_scale, w_mix_out=w_mix_out, norm_mem=norm_mem, norm_xa_pre=norm_xa_pre, norm_xa_post=norm_xa_post, w_xq=w_xq, w_xkv=w_xkv, w_xo=w_xo, norm_ffn_pre=norm_ffn_pre, norm_ffn_post=norm_ffn_post, w_up=w_up, conv_w=conv_w, conv_b=conv_b, w_down=w_down, loss_target=loss_target, m_norm_mix_pre=m_norm_mix_pre, m_norm_mix_post=m_norm_mix_post, m_w_in=m_w_in, m_b_forget=m_b_forget, m_w_pool=m_w_pool, m_pool_scale=m_pool_scale, m_w_mix_out=m_w_mix_out, m_norm_mem=m_norm_mem, m_norm_xa_pre=m_norm_xa_pre, m_norm_xa_post=m_norm_xa_post, m_w_xq=m_w_xq, m_w_xkv=m_w_xkv, m_w_xo=m_w_xo, m_norm_ffn_pre=m_norm_ffn_pre, m_norm_ffn_post=m_norm_ffn_post, m_w_up=m_w_up, m_conv_w=m_conv_w, m_conv_b=m_conv_b, m_w_down=m_w_down, v_norm_mix_pre=v_norm_mix_pre, v_norm_mix_post=v_norm_mix_post, v_w_in=v_w_in, v_b_forget=v_b_forget, v_w_pool=v_w_pool, v_pool_scale=v_pool_scale, v_w_mix_out=v_w_mix_out, v_norm_mem=v_norm_mem, v_norm_xa_pre=v_norm_xa_pre, v_norm_xa_post=v_norm_xa_post, v_w_xq=v_w_xq, v_w_xkv=v_w_xkv, v_w_xo=v_w_xo, v_norm_ffn_pre=v_norm_ffn_pre, v_norm_ffn_post=v_norm_ffn_post, v_w_up=v_w_up, v_conv_w=v_conv_w, v_conv_b=v_conv_b, v_w_down=v_w_down)
    weights = {n: given[n] for n in TWIN_WEIGHTS}
    shared = {n: given[n] for n in SHARED_INPUTS}
    per_example = {n: given[n] for n in ['x', 'mem']}
    grad_fn = _jax.value_and_grad(_loss, argnums=(0, 1))

    def one_microbatch(ex, loss_target):
        ex = dict(ex)
        diff = ex.pop(TWIN_DIFF_INPUT)
        return grad_fn(weights, diff, {**shared, **ex}, loss_target)

    if N_MICROBATCH == 1:
        loss, (grad_w, grad_x) = one_microbatch(per_example, given["loss_target"])
    else:
        def body(carry, xs):
            loss_sum, grad_sum = carry
            l_k, (gw_k, gx_k) = one_microbatch(xs[0], xs[1])
            with _jax.named_scope("update"):
                return (loss_sum + l_k, _jax.tree.map(_jnp.add, grad_sum, gw_k)), gx_k

        init = (_jnp.zeros((), _jnp.float32), _jax.tree.map(_jnp.zeros_like, weights))
        (loss, grad_w), grad_x = _jax.lax.scan(body, init, (per_example, given["loss_target"]))
    with _jax.named_scope("update"):
        delta_w, new_m, new_v = {}, {}, {}
        for n in TWIN_WEIGHTS:
            delta_w[n], new_m[n], new_v[n] = _adamw(weights[n], grad_w[n], given["m_" + n], given["v_" + n])
    return (loss, grad_x, *[grad_w[n] for n in TWIN_WEIGHTS], *[delta_w[n] for n in TWIN_WEIGHTS],
            *[new_m[n] for n in TWIN_WEIGHTS], *[new_v[n] for n in TWIN_WEIGHTS])
```

```python
import functools
import math

import jax
import jax.numpy as jnp
from jax import lax
from jax.experimental import pallas as pl
from jax.experimental.pallas import tpu as pltpu

F32 = jnp.float32
BF16 = jnp.bfloat16

NDEV = 8
D_MODEL = 1024
D_POOL = 256
D_FOX = 768
FOX_HEADS = 12
HEAD_PAIRS = FOX_HEADS // 2
XA_HEADS = 4
XA_DIM = 256
D_FF = 4096
D_IN_PAD = 2688
F_COL = 2560
POOL_HALO = 16
NORM_EPS = 1e-6
NEG = -1e30

ADAM_LR = 0.001
ADAM_B1 = 0.9
ADAM_B2 = 0.999
ADAM_EPS = 1e-08
ADAM_WD = 0.01
ADAM_STEP = 10

TM = 512
TQ = 512
TN_FF = 512
VMEM_LIMIT = 48 * 1024 * 1024
MESH = pl.DeviceIdType.MESH


def _cp(*sem):
    return pltpu.CompilerParams(dimension_semantics=sem, vmem_limit_bytes=VMEM_LIMIT)


def _dot(a, b, dims):
    return lax.dot_general(a, b, (dims, ((), ())), preferred_element_type=F32)


NN = ((1,), (0,))
NT = ((1,), (1,))
TN = ((0,), (0,))


def _mm(a, b, mode, out_dtype, tm, tn, tk, name, b_cols=None, out_cols=None):
    a_list = list(a) if isinstance(a, (list, tuple)) else [a]
    b_list = list(b) if isinstance(b, (list, tuple)) else [b]
    assert len(a_list) == 1 or len(b_list) == 1
    if mode == "tn":
        K, M = a_list[0].shape
        assert len(a_list) == 1
        Ns = [x.shape[1] for x in b_list]
        N = sum(Ns)
        assert b_cols is None
    else:
        assert len(b_list) == 1
        M = a_list[0].shape[0]
        Ks = [x.shape[1] for x in a_list]
        K = sum(Ks)
        if b_cols is None:
            N = b_list[0].shape[0] if mode == "nt" else b_list[0].shape[1]
        else:
            N = b_list[0].shape[1] if mode == "nt" else NDEV * b_cols
    assert M % tm == 0 and N % tn == 0 and K % tk == 0, (name, M, N, K)
    grid = (M // tm, N // tn, K // tk)
    nk = grid[2]
    dims = {"nn": NN, "nt": NT, "tn": TN}[mode]

    in_specs = []
    if mode == "tn":
        in_specs.append(pl.BlockSpec((tk, tm), lambda i, j, k: (k, i)))
        if len(b_list) == 1:
            in_specs.append(pl.BlockSpec((tk, tn), lambda i, j, k: (k, j)))
        else:
            nj1 = Ns[0] // tn
            in_specs.append(pl.BlockSpec((tk, tn), lambda i, j, k: (k, jnp.minimum(j, nj1 - 1))))
            in_specs.append(pl.BlockSpec((tk, tn), lambda i, j, k: (k, jnp.maximum(j - nj1, 0))))
    else:
        if len(a_list) == 1:
            in_specs.append(pl.BlockSpec((tm, tk), lambda i, j, k: (i, k)))
        else:
            nk1 = Ks[0] // tk
            in_specs.append(pl.BlockSpec((tm, tk), lambda i, j, k: (i, jnp.minimum(k, nk1 - 1))))
            in_specs.append(pl.BlockSpec((tm, tk), lambda i, j, k: (i, jnp.maximum(k - nk1, 0))))
        if b_cols is None:
            if mode == "nn":
                in_specs.append(pl.BlockSpec((tk, tn), lambda i, j, k: (k, j)))
            else:
                in_specs.append(pl.BlockSpec((tn, tk), lambda i, j, k: (j, k)))
        else:
            if mode == "nn":
                per = b_cols // tn
                in_specs.append(pl.BlockSpec((None, tk, tn), lambda i, j, k: (j // per, k, j % per)))
            else:
                per = b_cols // tk
                in_specs.append(pl.BlockSpec((None, tn, tk), lambda i, j, k: (k // per, j, k % per)))
    if out_cols is None:
        out_spec = pl.BlockSpec((tm, tn), lambda i, j, k: (i, j))
        out_shape = jax.ShapeDtypeStruct((M, N), out_dtype)
    else:
        pero = out_cols // tn
        out_spec = pl.BlockSpec((None, tm, tn), lambda i, j, k: (j // pero, i, j % pero))
        out_shape = jax.ShapeDtypeStruct((NDEV, M, out_cols), out_dtype)

    two_a = len(a_list) == 2
    two_b = len(b_list) == 2

    def body(*refs):
        o_ref, acc_ref = refs[-2], refs[-1]
        j = pl.program_id(1)
        k = pl.program_id(2)

        @pl.when(k == 0)
        def _():
            acc_ref[...] = jnp.zeros_like(acc_ref)

        if two_a:
            a1, a2, b1 = refs[0], refs[1], refs[2]
            nk1_ = Ks[0] // tk

            @pl.when(k < nk1_)
            def _():
                acc_ref[...] += _dot(a1[...], b1[...], dims)

            @pl.when(k >= nk1_)
            def _():
                acc_ref[...] += _dot(a2[...], b1[...], dims)
        elif two_b:
            a1, b1, b2 = refs[0], refs[1], refs[2]
            nj1_ = Ns[0] // tn

            @pl.when(j < nj1_)
            def _():
                acc_ref[...] += _dot(a1[...], b1[...], dims)

            @pl.when(j >= nj1_)
            def _():
                acc_ref[...] += _dot(a1[...], b2[...], dims)
        else:
            acc_ref[...] += _dot(refs[0][...], refs[1][...], dims)

        @pl.when(k == nk - 1)
        def _():
            o_ref[...] = acc_ref[...].astype(o_ref.dtype)

    return pl.pallas_call(
        body, name=name, grid=grid, in_specs=in_specs, out_specs=out_spec, out_shape=out_shape,
        scratch_shapes=[pltpu.VMEM((tm, tn), F32)],
        compiler_params=_cp("parallel", "parallel", "arbitrary"),
    )(*a_list, *b_list)


def _rstd(x):
    return lax.rsqrt(jnp.mean(x * x, axis=-1, keepdims=True) + NORM_EPS)


def _norm_bwd_rows(dxn, xn, r):
    return r * (dxn - xn * jnp.mean(dxn * xn, axis=-1, keepdims=True))


def _row_spec(tm, d):
    return pl.BlockSpec((tm, d), lambda i: (i, 0))


def _vec_spec(d):
    return pl.BlockSpec((1, d), lambda i: (0, 0))


def _norm_fwd(x, g, name):
    s, d = x.shape
    tm = min(TM, s)

    def body(x_ref, g_ref, h_ref):
        xv = x_ref[...]
        h_ref[...] = (xv * _rstd(xv) * g_ref[...]).astype(BF16)

    return pl.pallas_call(
        body, name=name, grid=(s // tm,), in_specs=[_row_spec(tm, d), _vec_spec(d)],
        out_specs=_row_spec(tm, d), out_shape=jax.ShapeDtypeStruct((s, d), BF16),
        compiler_params=_cp("parallel"),
    )(x, g)


def _resid_norm_fwd(x_in, y, g_post, g_next, name):
    s, d = x_in.shape

    def body(x_ref, y_ref, gp_ref, gn_ref, xo_ref, h_ref):
        yv = y_ref[...]
        xo = x_ref[...] + yv * _rstd(yv) * gp_ref[...]
        xo_ref[...] = xo
        h_ref[...] = (xo * _rstd(xo) * gn_ref[...]).astype(BF16)

    return pl.pallas_call(
        body, name=name, grid=(s // TM,),
        in_specs=[_row_spec(TM, d), _row_spec(TM, d), _vec_spec(d), _vec_spec(d)],
        out_specs=[_row_spec(TM, d), _row_spec(TM, d)],
        out_shape=[jax.ShapeDtypeStruct((s, d), F32), jax.ShapeDtypeStruct((s, d), BF16)],
        compiler_params=_cp("parallel"),
    )(x_in, y, g_post, g_next)


def _norm_bwd(dh, x, dx_res, g_pre, name, prev=None):
    s, d = x.shape
    tm = min(TM, s)
    has_prev = prev is not None

    def body(*refs):
        if has_prev:
            dh_ref, x_ref, dr_ref, g_ref, y_ref, gp_ref, dx_ref, dg_ref, dy_ref, dgp_ref = refs
        else:
            dh_ref, x_ref, dr_ref, g_ref, dx_ref, dg_ref = refs
        i = pl.program_id(0)
        xv = x_ref[...]
        r = _rstd(xv)
        xn = xv * r
        dhv = dh_ref[...].astype(F32)
        dx = dr_ref[...] + _norm_bwd_rows(dhv * g_ref[...], xn, r)
        dx_ref[...] = dx
        dg = jnp.sum(dhv * xn, axis=0, keepdims=True)

        @pl.when(i == 0)
        def _():
            dg_ref[...] = dg

        @pl.when(i > 0)
        def _():
            dg_ref[...] += dg

        if has_prev:
            yv = y_ref[...]
            r2 = _rstd(yv)
            yn = yv * r2
            dy_ref[...] = _norm_bwd_rows(dx * gp_ref[...], yn, r2).astype(BF16)
            dgp = jnp.sum(dx * yn, axis=0, keepdims=True)

            @pl.when(i == 0)
            def _():
                dgp_ref[...] = dgp

            @pl.when(i > 0)
            def _():
                dgp_ref[...] += dgp

    in_specs = [_row_spec(tm, d), _row_spec(tm, d), _row_spec(tm, d), _vec_spec(d)]
    out_specs = [_row_spec(tm, d), _vec_spec(d)]
    out_shape = [jax.ShapeDtypeStruct((s, d), F32), jax.ShapeDtypeStruct((1, d), F32)]
    args = [dh, x, dx_res, g_pre]
    if has_prev:
        in_specs += [_row_spec(tm, d), _vec_spec(d)]
        out_specs += [_row_spec(tm, d), _vec_spec(d)]
        out_shape += [jax.ShapeDtypeStruct((s, d), BF16), jax.ShapeDtypeStruct((1, d), F32)]
        args += list(prev)
    return pl.pallas_call(
        body, name=name, grid=(s // tm,), in_specs=in_specs, out_specs=out_specs, out_shape=out_shape,
        compiler_params=_cp("arbitrary"),
    )(*args)


def _loss_bwd(x2, y3, g_post, tgt, name):
    s, d = x2.shape

    def body(x_ref, y_ref, g_ref, t_ref, loss_ref, dx_ref, dy_ref, dg_ref):
        i = pl.program_id(0)
        yv = y_ref[...]
        r = _rstd(yv)
        yn = yv * r
        e = x_ref[...] + yn * g_ref[...] - t_ref[...]
        part = 0.5 * jnp.sum(jnp.mean(e * e, axis=-1, keepdims=True), axis=0, keepdims=True)
        dx = e * (1.0 / d)
        dx_ref[...] = dx
        dy_ref[...] = _norm_bwd_rows(dx * g_ref[...], yn, r).astype(BF16)
        dg = jnp.sum(dx * yn, axis=0, keepdims=True)
        part = jnp.broadcast_to(part, (1, 128))

        @pl.when(i == 0)
        def _():
            dg_ref[...] = dg
            loss_ref[...] = part

        @pl.when(i > 0)
        def _():
            dg_ref[...] += dg
            loss_ref[...] += part

    return pl.pallas_call(
        body, name=name, grid=(s // TM,),
        in_specs=[_row_spec(TM, d), _row_spec(TM, d), _vec_spec(d), _row_spec(TM, d)],
        out_specs=[_vec_spec(128), _row_spec(TM, d), _row_spec(TM, d), _vec_spec(d)],
        out_shape=[jax.ShapeDtypeStruct((1, 128), F32), jax.ShapeDtypeStruct((s, d), F32),
                   jax.ShapeDtypeStruct((s, d), BF16), jax.ShapeDtypeStruct((1, d), F32)],
        compiler_params=_cp("arbitrary"),
    )(x2, y3, g_post, tgt)


def _split3(v):
    hi = v.astype(BF16)
    r1 = v - hi.astype(F32)
    mid = r1.astype(BF16)
    lo = (r1 - mid.astype(F32)).astype(BF16)
    return hi, mid, lo


def _tri_dot(tri, v):
    hi, mid, lo = _split3(v)
    return _dot(tri, hi, NN) + _dot(tri, mid, NN) + _dot(tri, lo, NN)


def _gate_cumsum(fraw, b_pad, name):
    s = fraw.shape[0]

    def body(f_ref, b_ref, flog_ref, cum_ref, cumt_ref, carry_ref):
        i = pl.program_id(0)

        @pl.when(i == 0)
        def _():
            carry_ref[...] = jnp.zeros_like(carry_ref)

        flog = f_ref[...] + b_ref[...]
        flog_ref[...] = flog
        lf = jnp.minimum(flog, 0.0) - jnp.log(1.0 + jnp.exp(-jnp.abs(flog)))
        lane = lax.broadcasted_iota(jnp.int32, (1, 128), 1)
        lf = jnp.where(lane < FOX_HEADS, lf, 0.0)
        row = lax.broadcasted_iota(jnp.int32, (TM, TM), 0)
        col = lax.broadcasted_iota(jnp.int32, (TM, TM), 1)
        tri = (row >= col).astype(BF16)
        cum = _tri_dot(tri, lf) + carry_ref[...]
        cum_ref[...] = cum
        cumt_ref[...] = cum.T
        carry_ref[...] = cum[TM - 1:TM, :]

    return pl.pallas_call(
        body, name=name, grid=(s // TM,),
        in_specs=[_row_spec(TM, 128), _vec_spec(128)],
        out_specs=[_row_spec(TM, 128), _row_spec(TM, 128), pl.BlockSpec((128, TM), lambda i: (0, i))],
        out_shape=[jax.ShapeDtypeStruct((s, 128), F32), jax.ShapeDtypeStruct((s, 128), F32),
                   jax.ShapeDtypeStruct((128, s), F32)],
        scratch_shapes=[pltpu.VMEM((1, 128), F32)],
        compiler_params=_cp("arbitrary"),
    )(fraw, b_pad)


def _gate_bwd(dcum, flog, name):
    s = dcum.shape[0]
    n = s // TM

    def body(dc_ref, fl_ref, dp_ref, db_ref, carry_ref):
        i = pl.program_id(0)

        @pl.when(i == 0)
        def _():
            carry_ref[...] = jnp.zeros_like(carry_ref)

        row = lax.broadcasted_iota(jnp.int32, (TM, TM), 0)
        col = lax.broadcasted_iota(jnp.int32, (TM, TM), 1)
        tri = (row <= col).astype(BF16)
        dlf = _tri_dot(tri, dc_ref[...]) + carry_ref[...]
        carry_ref[...] = dlf[0:1, :]
        lane = lax.broadcasted_iota(jnp.int32, (1, 128), 1)
        df = jnp.where(lane < FOX_HEADS, dlf / (1.0 + jnp.exp(fl_ref[...])), 0.0)
        dp_ref[...] = df.astype(BF16)
        db = jnp.sum(df, axis=0, keepdims=True)

        @pl.when(i == 0)
        def _():
            db_ref[...] = db

        @pl.when(i > 0)
        def _():
            db_ref[...] += db

    rev = lambda i: (n - 1 - i, 0)
    return pl.pallas_call(
        body, name=name, grid=(n,),
        in_specs=[pl.BlockSpec((TM, 128), rev), pl.BlockSpec((TM, 128), rev)],
        out_specs=[pl.BlockSpec((TM, 128), rev), _vec_spec(128)],
        out_shape=[jax.ShapeDtypeStruct((s, 128), BF16), jax.ShapeDtypeStruct((1, 128), F32)],
        scratch_shapes=[pltpu.VMEM((1, 128), F32)],
        compiler_params=_cp("arbitrary"),
    )(dcum, flog)


def _pool_consts(i, rows):
    lane = lax.broadcasted_iota(jnp.int32, (rows, D_POOL), 1)
    t1 = lax.broadcasted_iota(jnp.int32, (rows, D_POOL), 0) + i * TM + 1
    win = jnp.where(lane < 64, 2, jnp.where(lane < 128, 4, jnp.where(lane < 192, 8, 16)))
    inv = 1.0 / jnp.minimum(t1, win).astype(F32)
    return lane, inv


def _by_group(lane, s2, s4, s8, s16):
    return jnp.where(lane < 64, s2, jnp.where(lane < 128, s4, jnp.where(lane < 192, s8, s16)))


def _pool_diff(i, u_ref, halo_ref):
    u = u_ref[...].astype(F32)
    halo = jnp.where(i > 0, halo_ref[...].astype(F32), 0.0)
    ext = jnp.concatenate([halo, u], axis=0)
    s2 = ext + pltpu.roll(ext, 1, 0)
    s4 = s2 + pltpu.roll(s2, 2, 0)
    s8 = s4 + pltpu.roll(s4, 4, 0)
    s16 = s8 + pltpu.roll(s8, 8, 0)
    lane, inv = _pool_consts(i, TM)
    sel = _by_group(lane, s2[POOL_HALO:], s4[POOL_HALO:], s8[POOL_HALO:], s16[POOL_HALO:])
    return sel * inv - u


def _pool_fwd(proj, wbd, scale, ycat, name):
    s = proj.shape[0]
    hb = TM // POOL_HALO

    def body(u_ref, halo_ref, w_ref, sc_ref, y_any, y_ref):
        del y_any
        i = pl.program_id(0)
        diff = _pool_diff(i, u_ref, halo_ref)
        mixed = _dot(diff.astype(BF16), w_ref[...], NN)
        y_ref[...] = (mixed * sc_ref[...]).astype(BF16)

    return pl.pallas_call(
        body, name=name, grid=(s // TM,),
        in_specs=[pl.BlockSpec((TM, D_POOL), lambda i: (i, 0)),
                  pl.BlockSpec((POOL_HALO, D_POOL), lambda i: (jnp.maximum(i * hb - 1, 0), 0)),
                  pl.BlockSpec((D_POOL, D_POOL), lambda i: (0, 0)), _vec_spec(D_POOL),
                  pl.BlockSpec(memory_space=pl.ANY)],
        out_specs=pl.BlockSpec((TM, D_POOL), lambda i: (i, 0)),
        out_shape=jax.ShapeDtypeStruct(ycat.shape, ycat.dtype),
        input_output_aliases={4: 0},
        compiler_params=_cp("parallel"),
    )(proj, proj, wbd, scale, ycat)


def _pool_bwd(proj, dycat, wbd, scale, name):
    s = proj.shape[0]
    n = s // TM
    hb = TM // POOL_HALO
    last_halo = s // POOL_HALO - 1

    def body(u_ref, halo_ref, dy_ref, dyp_ref, w_ref, sc_ref, dp_ref, dw_ref, dsc_ref):
        i = pl.program_id(0)
        diff = _pool_diff(i, u_ref, halo_ref)
        diff_b = diff.astype(BF16)
        mixed = _dot(diff_b, w_ref[...], NN)
        dy = dy_ref[...].astype(F32)
        dmix = (dy * sc_ref[...]).astype(BF16)
        dyp = jnp.where(i < n - 1, dyp_ref[...].astype(F32), 0.0)
        dmix_p = (dyp * sc_ref[...]).astype(BF16)
        dd = _dot(dmix, w_ref[...], NT)
        dd_p = _dot(dmix_p, w_ref[...], NT)
        lane, inv = _pool_consts(i, TM)
        _, inv_p = _pool_consts(i + 1, POOL_HALO)
        ext = jnp.concatenate([dd * inv, dd_p * inv_p], axis=0)
        rows = TM + POOL_HALO
        l2 = ext + pltpu.roll(ext, rows - 1, 0)
        l4 = l2 + pltpu.roll(l2, rows - 2, 0)
        l8 = l4 + pltpu.roll(l4, rows - 4, 0)
        l16 = l8 + pltpu.roll(l8, rows - 8, 0)
        du = _by_group(lane, l2[:TM], l4[:TM], l8[:TM], l16[:TM]) - dd
        dp_ref[...] = du.astype(BF16)
        dw = _dot(diff_b, dmix, TN)
        dsc = jnp.sum(dy * mixed, axis=0, keepdims=True)

        @pl.when(i == 0)
        def _():
            dw_ref[...] = dw
            dsc_ref[...] = dsc

        @pl.when(i > 0)
        def _():
            dw_ref[...] += dw
            dsc_ref[...] += dsc

    return pl.pallas_call(
        body, name=name, grid=(n,),
        in_specs=[pl.BlockSpec((TM, D_POOL), lambda i: (i, 0)),
                  pl.BlockSpec((POOL_HALO, D_POOL), lambda i: (jnp.maximum(i * hb - 1, 0), 0)),
                  pl.BlockSpec((TM, D_POOL), lambda i: (i, 0)),
                  pl.BlockSpec((POOL_HALO, D_POOL), lambda i: (jnp.minimum((i + 1) * hb, last_halo), 0)),
                  pl.BlockSpec((D_POOL, D_POOL), lambda i: (0, 0)), _vec_spec(D_POOL)],
        out_specs=[pl.BlockSpec((TM, D_POOL), lambda i: (i, 0)),
                   pl.BlockSpec((D_POOL, D_POOL), lambda i: (0, 0)), _vec_spec(D_POOL)],
        out_shape=[jax.ShapeDtypeStruct((s, D_POOL), BF16),
                   jax.ShapeDtypeStruct((D_POOL, D_POOL), F32), jax.ShapeDtypeStruct((1, D_POOL), F32)],
        compiler_params=_cp("arbitrary"),
    )(proj, proj, dycat, dycat, wbd, scale)


Q_BLK = D_POOL // 128
K_BLK = Q_BLK + D_FOX // 128
V_BLK = K_BLK + D_FOX // 128


def _head_masks():
    lane = lax.broadcasted_iota(jnp.int32, (1, 128), 1)
    return [lane < 64, lane >= 64]


def _fox_scores(qh, k2, cq_col, ck_row, row_off):
    sc = _dot(qh, k2, NT) * 0.125 + cq_col - ck_row
    row = lax.broadcasted_iota(jnp.int32, sc.shape, 0) + row_off
    col = lax.broadcasted_iota(jnp.int32, sc.shape, 1)
    return jnp.where(row >= col, sc, NEG)


def _fox_fwd(proj, cum_c, cum_r, name):
    s = proj.shape[0]
    nq = s // TQ

    def body(q_ref, k_ref, v_ref, cq_ref, ck_ref, o_ref, lse_ref, m_ref, l_ref, acc_ref):
        qi, ki = pl.program_id(1), pl.program_id(2)
        masks = _head_masks()

        @pl.when(ki == 0)
        def _():
            m_ref[...] = jnp.full_like(m_ref, NEG)
            l_ref[...] = jnp.zeros_like(l_ref)
            acc_ref[...] = jnp.zeros_like(acc_ref)

        @pl.when(ki <= qi)
        def _():
            q2, k2, v2 = q_ref[...], k_ref[...], v_ref[...]
            cq, ck = cq_ref[...], ck_ref[...]
            pv = []
            alpha = []
            for hh in range(2):
                qh = jnp.where(masks[hh], q2, jnp.zeros_like(q2))
                vh = jnp.where(masks[hh], v2, jnp.zeros_like(v2))
                sc = _fox_scores(qh, k2, cq[:, hh:hh + 1], ck[hh:hh + 1, :], (qi - ki) * TQ)
                m_prev = m_ref[hh]
                m_new = jnp.maximum(m_prev, jnp.max(sc, axis=1, keepdims=True))
                a = jnp.exp(m_prev - m_new)
                p = jnp.exp(sc - m_new)
                l_ref[hh] = a * l_ref[hh] + jnp.sum(p, axis=1, keepdims=True)
                m_ref[hh] = m_new
                pv.append(_dot(p.astype(BF16), vh, NN))
                alpha.append(a)
            acc_ref[...] = acc_ref[...] * jnp.where(masks[0], alpha[0], alpha[1]) + pv[0] + pv[1]

        @pl.when(ki == qi)
        def _():
            inv = jnp.where(masks[0], 1.0 / l_ref[0], 1.0 / l_ref[1])
            o_ref[...] = (acc_ref[...] * inv).astype(BF16)
            lane = lax.broadcasted_iota(jnp.int32, (1, 128), 1)
            lse_ref[...] = jnp.where(lane == 0, m_ref[0] + jnp.log(l_ref[0]), m_ref[1] + jnp.log(l_ref[1]))

    kv_row = lambda p, qi, ki: jnp.minimum(ki, qi)
    return pl.pallas_call(
        body, name=name, grid=(HEAD_PAIRS, nq, nq),
        in_specs=[pl.BlockSpec((TQ, 128), lambda p, qi, ki: (qi, Q_BLK + p)),
                  pl.BlockSpec((TQ, 128), lambda p, qi, ki: (kv_row(p, qi, ki), K_BLK + p)),
                  pl.BlockSpec((TQ, 128), lambda p, qi, ki: (kv_row(p, qi, ki), V_BLK + p)),
                  pl.BlockSpec((None, TQ, 128), lambda p, qi, ki: (p, qi, 0)),
                  pl.BlockSpec((None, 8, TQ), lambda p, qi, ki: (p, 0, kv_row(p, qi, ki)))],
        out_specs=[pl.BlockSpec((TQ, 128), lambda p, qi, ki: (qi, Q_BLK + p)),
                   pl.BlockSpec((None, TQ, 128), lambda p, qi, ki: (p, qi, 0))],
        out_shape=[jax.ShapeDtypeStruct((s, D_MODEL), BF16), jax.ShapeDtypeStruct((HEAD_PAIRS, s, 128), F32)],
        scratch_shapes=[pltpu.VMEM((2, TQ, 1), F32), pltpu.VMEM((2, TQ, 1), F32), pltpu.VMEM((TQ, 128), F32)],
        compiler_params=_cp("parallel", "parallel", "arbitrary"),
    )(proj, proj, proj, cum_c, cum_r)


def _fox_bwd(proj, ycat, dycat, cum_c, cum_r, lse, name):
    s = proj.shape[0]
    nq = s // TQ

    def body(q_ref, k_ref, v_ref, do_ref, o_ref, cq_ref, ck_ref, lse_ref,
             dq_ref, dk_ref, dv_ref, dc_ref, dcq_ref, dq_acc, dk_acc, dv_acc, dc_acc, dcq_acc):
        ki, qi = pl.program_id(1), pl.program_id(2)
        masks = _head_masks()
        lane = lax.broadcasted_iota(jnp.int32, (1, 128), 1)

        @pl.when(qi == ki)
        def _():
            dk_acc[...] = jnp.zeros_like(dk_acc)
            dv_acc[...] = jnp.zeros_like(dv_acc)
            dc_acc[...] = jnp.zeros_like(dc_acc)

        @pl.when(qi >= ki)
        def _():
            q2, k2, v2, do2 = q_ref[...], k_ref[...], v_ref[...], do_ref[...]
            cq, ck, lse2 = cq_ref[...], ck_ref[...], lse_ref[...]
            dd = do2.astype(F32) * o_ref[...].astype(F32)
            d0 = jnp.sum(jnp.where(masks[0], dd, 0.0), axis=1, keepdims=True)
            drow = [d0, jnp.sum(dd, axis=1, keepdims=True) - d0]
            dq = jnp.zeros((TQ, 128), F32)
            dk = jnp.zeros((TQ, 128), F32)
            dv = jnp.zeros((TQ, 128), F32)
            dcs = []
            rs = []
            for hh in range(2):
                zero = jnp.zeros_like(q2)
                qh = jnp.where(masks[hh], q2, zero)
                kh = jnp.where(masks[hh], k2, zero)
                doh = jnp.where(masks[hh], do2, zero)
                sc = _fox_scores(qh, k2, cq[:, hh:hh + 1], ck[hh:hh + 1, :], (qi - ki) * TQ)
                p = jnp.exp(sc - lse2[:, hh:hh + 1])
                dp = _dot(doh, v2, NT)
                ds = p * (dp - drow[hh])
                dsb = (ds * 0.125).astype(BF16)
                dv = dv + _dot(p.astype(BF16), doh, TN)
                dk = dk + _dot(dsb, qh, TN)
                dq = dq + _dot(dsb, kh, NN)
                dcs.append(-jnp.sum(ds, axis=0, keepdims=True))
                rs.append(jnp.sum(ds, axis=1, keepdims=True))
            dk_acc[...] += dk
            dv_acc[...] += dv
            dc_acc[0:1, :] += dcs[0]
            dc_acc[1:2, :] += dcs[1]
            dcq = jnp.where(lane == 0, rs[0], rs[1])

            @pl.when(ki == 0)
            def _():
                dq_acc[qi] = dq
                dcq_acc[qi] = dcq

            @pl.when(ki > 0)
            def _():
                dq_acc[qi] += dq
                dcq_acc[qi] += dcq

            @pl.when(qi == ki)
            def _():
                rows = pl.ds(pl.multiple_of(qi * TQ, TQ), TQ)
                dq_ref[rows, :] = dq_acc[qi].astype(BF16)
                dcq_ref[rows, :] = dcq_acc[qi]

        @pl.when(qi == nq - 1)
        def _():
            dk_ref[...] = dk_acc[...].astype(BF16)
            dv_ref[...] = dv_acc[...].astype(BF16)
            dc_ref[...] = dc_acc[...]

    q_row = lambda p, ki, qi: jnp.maximum(qi, ki)
    return pl.pallas_call(
        body, name=name, grid=(HEAD_PAIRS, nq, nq),
        in_specs=[pl.BlockSpec((TQ, 128), lambda p, ki, qi: (q_row(p, ki, qi), Q_BLK + p)),
                  pl.BlockSpec((TQ, 128), lambda p, ki, qi: (ki, K_BLK + p)),
                  pl.BlockSpec((TQ, 128), lambda p, ki, qi: (ki, V_BLK + p)),
                  pl.BlockSpec((TQ, 128), lambda p, ki, qi: (q_row(p, ki, qi), Q_BLK + p)),
                  pl.BlockSpec((TQ, 128), lambda p, ki, qi: (q_row(p, ki, qi), Q_BLK + p)),
                  pl.BlockSpec((None, TQ, 128), lambda p, ki, qi: (p, q_row(p, ki, qi), 0)),
                  pl.BlockSpec((None, 8, TQ), lambda p, ki, qi: (p, 0, ki)),
                  pl.BlockSpec((None, TQ, 128), lambda p, ki, qi: (p, q_row(p, ki, qi), 0))],
        out_specs=[pl.BlockSpec((s, 128), lambda p, ki, qi: (0, p)),
                   pl.BlockSpec((TQ, 128), lambda p, ki, qi: (ki, p)),
                   pl.BlockSpec((TQ, 128), lambda p, ki, qi: (ki, p)),
                   pl.BlockSpec((None, 8, TQ), lambda p, ki, qi: (p, 0, ki)),
                   pl.BlockSpec((None, s, 128), lambda p, ki, qi: (p, 0, 0))],
        out_shape=[jax.ShapeDtypeStruct((s, D_FOX), BF16)] * 3 + [jax.ShapeDtypeStruct((HEAD_PAIRS, 8, s), F32),
                                                                 jax.ShapeDtypeStruct((HEAD_PAIRS, s, 128), F32)],
        scratch_shapes=[pltpu.VMEM((nq, TQ, 128), F32), pltpu.VMEM((TQ, 128), F32), pltpu.VMEM((TQ, 128), F32),
                        pltpu.VMEM((8, TQ), F32), pltpu.VMEM((nq, TQ, 128), F32)],
        compiler_params=_cp("arbitrary", "arbitrary", "arbitrary"),
    )(proj, proj, proj, dycat, ycat, cum_c, cum_r, lse)


XA_SCALE = XA_DIM ** -0.5


def _xattn_fwd(q2, kv, name):
    s = q2.shape[0]
    m = kv.shape[0]

    def body(q_ref, kv_ref, o_ref):
        for h in range(XA_HEADS):
            c0 = h * XA_DIM
            sc = _dot(q_ref[:, c0:c0 + XA_DIM], kv_ref[:, c0:c0 + XA_DIM], NT) * XA_SCALE
            e = jnp.exp(sc - jnp.max(sc, axis=1, keepdims=True))
            p = e / jnp.sum(e, axis=1, keepdims=True)
            o_ref[:, c0:c0 + XA_DIM] = _dot(p.astype(BF16), kv_ref[:, D_MODEL + c0:D_MODEL + c0 + XA_DIM], NN).astype(BF16)

    return pl.pallas_call(
        body, name=name, grid=(s // TM,),
        in_specs=[_row_spec(TM, D_MODEL), pl.BlockSpec((m, 2 * D_MODEL), lambda i: (0, 0))],
        out_specs=_row_spec(TM, D_MODEL), out_shape=jax.ShapeDtypeStruct((s, D_MODEL), BF16),
        compiler_params=_cp("parallel"),
    )(q2, kv)


def _xattn_bwd(q2, kv, do2, name):
    s = q2.shape[0]
    m = kv.shape[0]

    def body(q_ref, kv_ref, do_ref, dq_ref, dkv_ref):
        i = pl.program_id(0)

        @pl.when(i == 0)
        def _():
            dkv_ref[...] = jnp.zeros_like(dkv_ref)

        for h in range(XA_HEADS):
            c0 = h * XA_DIM
            v0 = D_MODEL + c0
            qh = q_ref[:, c0:c0 + XA_DIM]
            kh = kv_ref[:, c0:c0 + XA_DIM]
            doh = do_ref[:, c0:c0 + XA_DIM]
            sc = _dot(qh, kh, NT) * XA_SCALE
            e = jnp.exp(sc - jnp.max(sc, axis=1, keepdims=True))
            p = e / jnp.sum(e, axis=1, keepdims=True)
            dp = _dot(doh, kv_ref[:, v0:v0 + XA_DIM], NT)
            ds = p * (dp - jnp.sum(p * dp, axis=1, keepdims=True))
            dsb = (ds * XA_SCALE).astype(BF16)
            dq_ref[:, c0:c0 + XA_DIM] = _dot(dsb, kh, NN).astype(BF16)
            dkv_ref[:, c0:c0 + XA_DIM] += _dot(dsb, qh, TN)
            dkv_ref[:, v0:v0 + XA_DIM] += _dot(p.astype(BF16), doh, TN)

    return pl.pallas_call(
        body, name=name, grid=(s // TM,),
        in_specs=[_row_spec(TM, D_MODEL), pl.BlockSpec((m, 2 * D_MODEL), lambda i: (0, 0)), _row_spec(TM, D_MODEL)],
        out_specs=[_row_spec(TM, D_MODEL), pl.BlockSpec((m, 2 * D_MODEL), lambda i: (0, 0))],
        out_shape=[jax.ShapeDtypeStruct((s, D_MODEL), BF16), jax.ShapeDtypeStruct((m, 2 * D_MODEL), F32)],
        compiler_params=_cp("arbitrary"),
    )(q2, kv, do2)


GELU_C = math.sqrt(2.0 / math.pi)
GELU_A = 0.044715


def _gelu(x):
    return 0.5 * x * (1.0 + jnp.tanh(GELU_C * (x + GELU_A * x * x * x)))


def _gelu_and_grad(x):
    t = jnp.tanh(GELU_C * (x + GELU_A * x * x * x))
    g = 0.5 * x * (1.0 + t)
    dg = 0.5 * (1.0 + t) + 0.5 * x * (1.0 - t * t) * GELU_C * (1.0 + 3.0 * GELU_A * x * x)
    return g, dg


def _shift_down(main, prev8):
    row = lax.broadcasted_iota(jnp.int32, main.shape, 0)
    s1 = jnp.where(row == 0, prev8[7:8, :], pltpu.roll(main, 1, 0))
    s2 = jnp.where(row == 0, prev8[6:7, :], jnp.where(row == 1, prev8[7:8, :], pltpu.roll(main, 2, 0)))
    return s1, s2


def _shift_up(main, next8):
    n = main.shape[0]
    row = lax.broadcasted_iota(jnp.int32, main.shape, 0)
    u1 = jnp.where(row == n - 1, next8[0:1, :], pltpu.roll(main, n - 1, 0))
    u2 = jnp.where(row == n - 2, next8[0:1, :], jnp.where(row == n - 1, next8[1:2, :], pltpu.roll(main, n - 2, 0)))
    return u1, u2


def _conv(h, s1, s2, w_ref, b_ref):
    return w_ref[0:1, :] * s2 + w_ref[1:2, :] * s1 + w_ref[2:3, :] * h + b_ref[...]


def _ffn_fwd(h3, w_up, cw, cb, w_down, name):
    s = h3.shape[0]
    tn = TN_FF
    nj = D_FF // tn
    per = D_MODEL // tn
    hb = TM // 8

    def body(h_ref, halo_ref, wg_ref, wu_ref, cwg_ref, cwu_ref, cbg_ref, cbu_ref, wd_ref,
             hg_ref, hu_ref, a_ref, y_ref):
        i, j = pl.program_id(0), pl.program_id(1)
        h = h_ref[...]
        halo = halo_ref[...]
        halo = jnp.where(i > 0, halo, jnp.zeros_like(halo))
        conv = []
        for w_ref, cw_ref, cb_ref, hid_ref in ((wg_ref, cwg_ref, cbg_ref, hg_ref), (wu_ref, cwu_ref, cbu_ref, hu_ref)):
            hm_b = _dot(h, w_ref[...], NN).astype(BF16)
            hid_ref[...] = hm_b
            hm = hm_b.astype(F32)
            hl = _dot(halo, w_ref[...], NN).astype(BF16).astype(F32)
            s1, s2 = _shift_down(hm, hl)
            conv.append(_conv(hm, s1, s2, cw_ref, cb_ref))
        a = (_gelu(conv[0]) * conv[1]).astype(BF16)
        a_ref[...] = a
        contrib = _dot(a, wd_ref[...], NN)

        @pl.when(j == 0)
        def _():
            y_ref[...] = contrib

        @pl.when(j > 0)
        def _():
            y_ref[...] += contrib

    return pl.pallas_call(
        body, name=name, grid=(s // TM, nj),
        in_specs=[pl.BlockSpec((TM, D_MODEL), lambda i, j: (i, 0)),
                  pl.BlockSpec((8, D_MODEL), lambda i, j: (jnp.maximum(i * hb - 1, 0), 0)),
                  pl.BlockSpec((None, D_MODEL, tn), lambda i, j: (j // per, 0, j % per)),
                  pl.BlockSpec((None, D_MODEL, tn), lambda i, j: (NDEV // 2 + j // per, 0, j % per)),
                  pl.BlockSpec((8, tn), lambda i, j: (0, j)),
                  pl.BlockSpec((8, tn), lambda i, j: (0, nj + j)),
                  pl.BlockSpec((1, tn), lambda i, j: (0, j)),
                  pl.BlockSpec((1, tn), lambda i, j: (0, nj + j)),
                  pl.BlockSpec((tn, D_MODEL), lambda i, j: (j, 0))],
        out_specs=[pl.BlockSpec((TM, tn), lambda i, j: (i, j)), pl.BlockSpec((TM, tn), lambda i, j: (i, j)),
                   pl.BlockSpec((TM, tn), lambda i, j: (i, j)), pl.BlockSpec((TM, D_MODEL), lambda i, j: (i, 0))],
        out_shape=[jax.ShapeDtypeStruct((s, D_FF), BF16), jax.ShapeDtypeStruct((s, D_FF), BF16),
                   jax.ShapeDtypeStruct((s, D_FF), BF16), jax.ShapeDtypeStruct((s, D_MODEL), F32)],
        compiler_params=_cp("parallel", "arbitrary"),
    )(h3, h3, w_up, w_up, cw, cw, cb, cb, w_down)


def _ffn_bwd(dy3, w_down, hid_g, hid_u, cw, cb, name):
    s = dy3.shape[0]
    n = s // TM
    tn = TN_FF
    nj = D_FF // tn
    hb = TM // 8
    last8 = s // 8 - 1

    def body(dy_ref, dyp_ref, wd_ref, hg_ref, hgl_ref, hgn_ref, hu_ref, hul_ref, hun_ref,
             cwg_ref, cwu_ref, cbg_ref, cbu_ref,
             dhg_ref, dhu_ref, dcwg_ref, dcwu_ref, dcbg_ref, dcbu_ref):
        i = pl.program_id(1)
        first, last = i == 0, i == n - 1
        da = _dot(dy_ref[...], wd_ref[...], NT)
        dyp = dyp_ref[...]
        dyp = jnp.where(last, jnp.zeros_like(dyp), dyp)
        da_n = _dot(dyp, wd_ref[...], NT)
        parts = []
        for h_ref, hl_ref, hn_ref, cw_ref, cb_ref in ((hg_ref, hgl_ref, hgn_ref, cwg_ref, cbg_ref),
                                                     (hu_ref, hul_ref, hun_ref, cwu_ref, cbu_ref)):
            hm = h_ref[...].astype(F32)
            hl = jnp.where(first, 0.0, hl_ref[...].astype(F32))
            hn = hn_ref[...].astype(F32)
            s1, s2 = _shift_down(hm, hl)
            c = _conv(hm, s1, s2, cw_ref, cb_ref)
            n1, n2 = _shift_down(hn, hm[TM - 8:, :])
            cn = _conv(hn, n1, n2, cw_ref, cb_ref)
            parts.append((hm, s1, s2, c, cn))
        g, dg = _gelu_and_grad(parts[0][3])
        gn, dgn = _gelu_and_grad(parts[0][4])
        dc_g = da * parts[1][3] * dg
        dc_u = da * g
        dcn_g = da_n * parts[1][4] * dgn
        dcn_u = da_n * gn
        outs = ((dc_g, dcn_g, parts[0], cwg_ref, dhg_ref, dcwg_ref, dcbg_ref),
                (dc_u, dcn_u, parts[1], cwu_ref, dhu_ref, dcwu_ref, dcbu_ref))
        for dc, dcn, (hm, s1, s2, _, _), cw_ref, dh_ref, dcw_ref, dcb_ref in outs:
            u1, u2 = _shift_up(dc, dcn)
            dh_ref[...] = (cw_ref[2:3, :] * dc + cw_ref[1:2, :] * u1 + cw_ref[0:1, :] * u2).astype(BF16)
            dcb = jnp.sum(dc, axis=0, keepdims=True)
            row8 = lax.broadcasted_iota(jnp.int32, (8, tn), 0)
            dcw = jnp.where(row8 == 0, jnp.sum(dc * s2, axis=0, keepdims=True),
                            jnp.where(row8 == 1, jnp.sum(dc * s1, axis=0, keepdims=True),
                                      jnp.where(row8 == 2, jnp.sum(dc * hm, axis=0, keepdims=True), 0.0)))

            @pl.when(first)
            def _():
                dcw_ref[...] = dcw
                dcb_ref[...] = dcb

            @pl.when(i > 0)
            def _():
                dcw_ref[...] += dcw
                dcb_ref[...] += dcb

    prev8 = lambda j, i: (jnp.maximum(i * hb - 1, 0), j)
    next8 = lambda j, i: (jnp.minimum((i + 1) * hb, last8), j)
    blk = lambda j, i: (i, j)
    col = lambda j, i: (0, j)
    colu = lambda j, i: (0, nj + j)
    return pl.pallas_call(
        body, name=name, grid=(nj, n),
        in_specs=[pl.BlockSpec((TM, D_MODEL), lambda j, i: (i, 0)),
                  pl.BlockSpec((8, D_MODEL), lambda j, i: (jnp.minimum((i + 1) * hb, last8), 0)),
                  pl.BlockSpec((tn, D_MODEL), lambda j, i: (j, 0)),
                  pl.BlockSpec((TM, tn), blk), pl.BlockSpec((8, tn), prev8), pl.BlockSpec((8, tn), next8),
                  pl.BlockSpec((TM, tn), blk), pl.BlockSpec((8, tn), prev8), pl.BlockSpec((8, tn), next8),
                  pl.BlockSpec((8, tn), col), pl.BlockSpec((8, tn), colu),
                  pl.BlockSpec((1, tn), col), pl.BlockSpec((1, tn), colu)],
        out_specs=[pl.BlockSpec((TM, tn), blk), pl.BlockSpec((TM, tn), blk),
                   pl.BlockSpec((8, tn), col), pl.BlockSpec((8, tn), col),
                   pl.BlockSpec((1, tn), col), pl.BlockSpec((1, tn), col)],
        out_shape=[jax.ShapeDtypeStruct((s, D_FF), BF16), jax.ShapeDtypeStruct((s, D_FF), BF16),
                   jax.ShapeDtypeStruct((8, D_FF), F32), jax.ShapeDtypeStruct((8, D_FF), F32),
                   jax.ShapeDtypeStruct((1, D_FF), F32), jax.ShapeDtypeStruct((1, D_FF), F32)],
        compiler_params=_cp("parallel", "arbitrary"),
    )(dy3, dy3, w_down, hid_g, hid_g, hid_g, hid_u, hid_u, hid_u, cw, cw, cb, cb)


def _slot(p):
    return 4 * p[0] + 2 * p[1] + p[2]


def _all_gather(shards, name):
    n = len(shards)

    def body(*refs):
        ins, outs = refs[:n], refs[n:2 * n]
        send_sems, recv_sems, local_sems = refs[2 * n:]
        x, y, c = lax.axis_index("x"), lax.axis_index("y"), lax.axis_index("c")
        me, sibling = (x, y, c), (x, y, 1 - c)
        chips = [(1 - x, y), (x, 1 - y), (1 - x, 1 - y)]

        def copy(a, k, block, to, from_input=False):
            dst = outs[a].at[_slot(block)]
            return pltpu.make_async_remote_copy(
                src_ref=ins[a] if from_input else dst, dst_ref=dst,
                send_sem=send_sems.at[a, k], recv_sem=recv_sems.at[a, k],
                device_id=to, device_id_type=MESH)

        mine = [pltpu.make_async_copy(ins[a], outs[a].at[_slot(me)], local_sems.at[a]) for a in range(n)]
        for cp in mine:
            cp.start()
        first = []
        for a in range(n):
            first.append(copy(a, 0, me, sibling, True))
            first += [copy(a, 1 + j, me, (*chip, c), True) for j, chip in enumerate(chips)]
        for cp in first:
            cp.start()
        passed = []
        for j, chip in enumerate(chips):
            for a in range(n):
                copy(a, 1 + j, (*chip, c), me).wait_recv()
                fwd = copy(a, 4 + j, (*chip, c), sibling)
                fwd.start()
                passed.append(fwd)
        for a in range(n):
            copy(a, 0, sibling, me).wait_recv()
            for j, chip in enumerate(chips):
                copy(a, 4 + j, (*chip, 1 - c), me).wait_recv()
        for cp in first + passed:
            cp.wait_send()
        for cp in mine:
            cp.wait()

    any_spec = pl.BlockSpec(memory_space=pl.ANY)
    return pl.pallas_call(
        body, name=name,
        in_specs=[any_spec] * n, out_specs=[any_spec] * n,
        out_shape=[jax.ShapeDtypeStruct((NDEV,) + s.shape, s.dtype) for s in shards],
        scratch_shapes=[pltpu.SemaphoreType.DMA((n, 7)), pltpu.SemaphoreType.DMA((n, 7)),
                        pltpu.SemaphoreType.DMA((n,))],
    )(*shards)


def _scatter_blocks(full, name):
    n = len(full)

    def body(*refs):
        ins, outs = refs[:n], refs[n:2 * n]
        send_sems, recv_sems, local_sems = refs[2 * n:]
        x, y, c = lax.axis_index("x"), lax.axis_index("y"), lax.axis_index("c")
        me = (x, y, c)
        copies = []
        for a in range(n):
            cp = pltpu.make_async_copy(ins[a].at[_slot(me)], outs[a].at[_slot(me)], local_sems.at[a])
            cp.start()
            copies.append(cp)
        remote = []
        for mask in range(1, NDEV):
            peer = (1 - x if mask & 4 else x, 1 - y if mask & 2 else y, 1 - c if mask & 1 else c)
            for a in range(n):
                cp = pltpu.make_async_remote_copy(
                    src_ref=ins[a].at[_slot(peer)], dst_ref=outs[a].at[_slot(me)],
                    send_sem=send_sems.at[a, mask - 1], recv_sem=recv_sems.at[a, mask - 1],
                    device_id=peer, device_id_type=MESH)
                cp.start()
                remote.append(cp)
        for cp in remote:
            cp.wait()
        for cp in copies:
            cp.wait()

    any_spec = pl.BlockSpec(memory_space=pl.ANY)
    return pl.pallas_call(
        body, name=name,
        in_specs=[any_spec] * n, out_specs=[any_spec] * n,
        out_shape=[jax.ShapeDtypeStruct(f.shape, f.dtype) for f in full],
        scratch_shapes=[pltpu.SemaphoreType.DMA((n, 7)), pltpu.SemaphoreType.DMA((n, 7)),
                        pltpu.SemaphoreType.DMA((n,))],
    )(*full)


def _adamw(parts, w, m, v, name):
    r, c = w.shape
    tr = r if r * c <= 160 * 1024 else max(8, (160 * 1024 // c) // 8 * 8)
    while r % tr:
        tr -= 8
    bc1 = 1.0 - ADAM_B1 ** ADAM_STEP
    bc2 = 1.0 - ADAM_B2 ** ADAM_STEP

    def body(p_ref, w_ref, m_ref, v_ref, g_ref, d_ref, mo_ref, vo_ref):
        g = p_ref[0].astype(F32)
        for d in range(1, NDEV):
            g = g + p_ref[d].astype(F32)
        g_ref[...] = g
        mn = ADAM_B1 * m_ref[...] + (1.0 - ADAM_B1) * g
        vn = ADAM_B2 * v_ref[...] + (1.0 - ADAM_B2) * (g * g)
        mo_ref[...] = mn
        vo_ref[...] = vn
        d_ref[...] = -ADAM_LR * ((mn / bc1) / (jnp.sqrt(vn / bc2) + ADAM_EPS) + ADAM_WD * w_ref[...])

    spec = pl.BlockSpec((tr, c), lambda i: (i, 0))
    return pl.pallas_call(
        body, name=name, grid=(r // tr,),
        in_specs=[pl.BlockSpec((NDEV, tr, c), lambda i: (0, i, 0)), spec, spec, spec],
        out_specs=[spec] * 4, out_shape=[jax.ShapeDtypeStruct((r, c), F32)] * 4,
        compiler_params=_cp("parallel"),
    )(parts, w, m, v)


def _pair_cols(a):
    s = a.shape[0]
    t = a[:, :FOX_HEADS].reshape(s, HEAD_PAIRS, 2).transpose(1, 0, 2)
    return jnp.pad(t, ((0, 0), (0, 0), (0, 126)))


def _pair_rows(at):
    s = at.shape[1]
    return jnp.pad(at[:FOX_HEADS].reshape(HEAD_PAIRS, 2, s), ((0, 0), (0, 6), (0, 0)))


def _local_step(x, mem, tgt, gains, b_forget, w_pool, pool_scale, conv_w, conv_b, wts):
    s = x.shape[0]
    w_in, w_mix, w_xq, w_xo = wts["w_in"], wts["w_mix"], wts["w_xq"], wts["w_xo"]
    w_xkv, w_up, w_down = wts["w_xkv"], wts["w_up"], wts["w_down"]

    w_f = w_in[:, F_COL:]
    b_pad = jnp.pad(b_forget, ((0, 0), (0, 128 - FOX_HEADS)))
    wbd = jnp.zeros((D_POOL, D_POOL), F32)
    for g in range(4):
        wbd = wbd.at[64 * g:64 * g + 64, 64 * g:64 * g + 64].set(w_pool[g])
    wbd = wbd.astype(BF16)
    scale = pool_scale.reshape(1, D_POOL)
    cw = jnp.pad(conv_w, ((0, 5), (0, 0)))

    h1 = _norm_fwd(x, gains["mix_pre"], "norm_mix_pre")
    proj = _mm(h1, w_in, "nn", BF16, 1024, 384, 1024, "proj_in")
    fraw = _mm(h1, w_f, "nn", F32, 1024, 128, 1024, "proj_gate")
    flog, cum, cum_t = _gate_cumsum(fraw, b_pad, "gate_cumsum")
    cum_c, cum_r = _pair_cols(cum), _pair_rows(cum_t)
    ycat, lse = _fox_fwd(proj, cum_c, cum_r, "fox_fwd")
    ycat = _pool_fwd(proj, wbd, scale, ycat, "pool_fwd")
    y1 = _mm(ycat, w_mix, "nn", F32, 1024, 1024, 1024, "mix_out")
    x1, h2 = _resid_norm_fwd(x, y1, gains["mix_post"], gains["xa_pre"], "resid_mix")
    q2 = _mm(h2, w_xq, "nn", BF16, 1024, 1024, 1024, "xa_q")
    mem_n = _norm_fwd(mem, gains["mem"], "norm_mem")
    kv = _mm(mem_n, w_xkv, "nn", BF16, mem.shape[0], 256, 1024, "xa_kv", b_cols=256)
    o2 = _xattn_fwd(q2, kv, "xattn_fwd")
    y2 = _mm(o2, w_xo, "nn", F32, 1024, 1024, 1024, "xa_out")
    x2, h3 = _resid_norm_fwd(x1, y2, gains["xa_post"], gains["ffn_pre"], "resid_xa")
    hid_g, hid_u, act, y3 = _ffn_fwd(h3, w_up, cw, conv_b, w_down, "ffn_fwd")

    loss, dx3, dy3, dg_ffn_post = _loss_bwd(x2, y3, gains["ffn_post"], tgt, "loss_bwd")
    dhid_g, dhid_u, dcw_g, dcw_u, dcb_g, dcb_u = _ffn_bwd(dy3, w_down, hid_g, hid_u, cw, conv_b, "ffn_bwd")
    d_w_down = _mm(act, dy3, "tn", BF16, 1024, 1024, 1024, "dw_down")
    d_w_up = _mm(h3, [dhid_g, dhid_u], "tn", BF16, 1024, 1024, 1024, "dw_up", out_cols=1024)
    dh3 = _mm([dhid_g, dhid_u], w_up, "nt", F32, 1024, 1024, 1024, "dh_ffn", b_cols=1024)
    dx2, dg_ffn_pre, dy2, dg_xa_post = _norm_bwd(dh3, x2, dx3, gains["ffn_pre"], "norm_bwd_ffn",
                                                 prev=(y2, gains["xa_post"]))
    do2 = _mm(dy2, w_xo, "nt", BF16, 1024, 1024, 1024, "d_xa_out")
    d_w_xo = _mm(o2, dy2, "tn", BF16, 1024, 1024, 1024, "dw_xo")
    dq2, dkv = _xattn_bwd(q2, kv, do2, "xattn_bwd")
    dkv = dkv.astype(BF16)
    dh2 = _mm(dq2, w_xq, "nt", F32, 1024, 1024, 1024, "dh_xa")
    d_w_xq = _mm(h2, dq2, "tn", BF16, 1024, 1024, 1024, "dw_xq")
    dmem_n = _mm(dkv, w_xkv, "nt", F32, mem.shape[0], 1024, 256, "d_mem", b_cols=256)
    d_w_xkv = _mm(mem_n, dkv, "tn", BF16, 1024, 256, mem.shape[0], "dw_xkv", out_cols=256)
    _, dg_mem = _norm_bwd(dmem_n, mem, jnp.zeros_like(mem), gains["mem"], "norm_bwd_mem")
    dx1, dg_xa_pre, dy1, dg_mix_post = _norm_bwd(dh2, x1, dx2, gains["xa_pre"], "norm_bwd_xa",
                                                 prev=(y1, gains["mix_post"]))
    dycat = _mm(dy1, w_mix, "nt", BF16, 1024, 1024, 1024, "d_mix_out")
    d_w_mix = _mm(ycat, dy1, "tn", BF16, 1024, 1024, 1024, "dw_mix")
    dq, dk, dv, dcum_r, dcum_c = _fox_bwd(proj, ycat, dycat, cum_c, cum_r, lse, "fox_bwd")
    du, d_wbd, d_scale = _pool_bwd(proj, dycat, wbd, scale, "pool_bwd")
    dcum = dcum_r[:, :2, :].reshape(FOX_HEADS, s).T + dcum_c[:, :, :2].transpose(1, 0, 2).reshape(s, FOX_HEADS)
    dcum = jnp.pad(dcum, ((0, 0), (0, 128 - FOX_HEADS)))
    df, db_f = _gate_bwd(dcum, flog, "gate_bwd")
    dproj = jnp.concatenate([du, dq, dk, dv, df], axis=1)
    dh1 = _mm(dproj, w_in, "nt", F32, 1024, 1024, 896, "dh_mix")
    d_w_in = _mm(h1, dproj, "tn", BF16, 1024, 384, 1024, "dw_in")
    grad_x, dg_mix_pre = _norm_bwd(dh1, x, dx1, gains["mix_pre"], "norm_bwd_mix")

    big = dict(w_in=d_w_in, w_mix=d_w_mix, w_xq=d_w_xq, w_xo=d_w_xo, w_xkv=d_w_xkv, w_up=d_w_up, w_down=d_w_down)
    small = dict(
        mix_pre=dg_mix_pre, mix_post=dg_mix_post, mem=dg_mem, xa_pre=dg_xa_pre, xa_post=dg_xa_post,
        ffn_pre=dg_ffn_pre, ffn_post=dg_ffn_post,
        conv_b=jnp.concatenate([dcb_g, dcb_u], axis=1),
        w_pool=jnp.stack([d_wbd[64 * g:64 * g + 64, 64 * g:64 * g + 64] for g in range(4)]),
        pool_scale=d_scale.reshape(4, 64),
        b_forget=db_f[:, :FOX_HEADS],
        conv_w=jnp.concatenate([dcw_g[:3], dcw_u[:3]], axis=1),
    )
    return loss, grad_x, big, small


SMALL_ORDER = ("mix_pre", "mix_post", "mem", "xa_pre", "xa_post", "ffn_pre", "ffn_post", "conv_b",
               "w_pool", "pool_scale", "b_forget")
SMALL_ROWS = 256


def _pack_small(d):
    flat = jnp.concatenate([d[k].reshape(-1).astype(F32) for k in SMALL_ORDER])
    return jnp.pad(flat, (0, SMALL_ROWS * 128 - flat.shape[0])).reshape(SMALL_ROWS, 128)


def _unpack_small(a, like):
    flat = a.reshape(-1)
    out, off = {}, 0
    for k in SMALL_ORDER:
        n = like[k].size
        out[k] = flat[off:off + n].reshape(like[k].shape)
        off += n
    return out


def kernel(x, mem, norm_mix_pre, norm_mix_post, w_in, b_forget, w_pool, pool_scale, w_mix_out, norm_mem, norm_xa_pre, norm_xa_post, w_xq, w_xkv, w_xo, norm_ffn_pre, norm_ffn_post, w_up, conv_w, conv_b, w_down, loss_target, m_norm_mix_pre, m_norm_mix_post, m_w_in, m_b_forget, m_w_pool, m_pool_scale, m_w_mix_out, m_norm_mem, m_norm_xa_pre, m_norm_xa_post, m_w_xq, m_w_xkv, m_w_xo, m_norm_ffn_pre, m_norm_ffn_post, m_w_up, m_conv_w, m_conv_b, m_w_down, v_norm_mix_pre, v_norm_mix_post, v_w_in, v_b_forget, v_w_pool, v_pool_scale, v_w_mix_out, v_norm_mem, v_norm_xa_pre, v_norm_xa_post, v_w_xq, v_w_xkv, v_w_xo, v_norm_ffn_pre, v_norm_ffn_post, v_w_up, v_conv_w, v_conv_b, v_w_down):
    names = ("norm_mix_pre", "norm_mix_post", "w_in", "b_forget", "w_pool", "pool_scale", "w_mix_out", "norm_mem",
             "norm_xa_pre", "norm_xa_post", "w_xq", "w_xkv", "w_xo", "norm_ffn_pre", "norm_ffn_post", "w_up",
             "conv_w", "conv_b", "w_down")
    w = dict(zip(names, (norm_mix_pre, norm_mix_post, w_in, b_forget, w_pool, pool_scale, w_mix_out, norm_mem,
                         norm_xa_pre, norm_xa_post, w_xq, w_xkv, w_xo, norm_ffn_pre, norm_ffn_post, w_up,
                         conv_w, conv_b, w_down)))
    mo = dict(zip(names, (m_norm_mix_pre, m_norm_mix_post, m_w_in, m_b_forget, m_w_pool, m_pool_scale, m_w_mix_out,
                          m_norm_mem, m_norm_xa_pre, m_norm_xa_post, m_w_xq, m_w_xkv, m_w_xo, m_norm_ffn_pre,
                          m_norm_ffn_post, m_w_up, m_conv_w, m_conv_b, m_w_down)))
    vo = dict(zip(names, (v_norm_mix_pre, v_norm_mix_post, v_w_in, v_b_forget, v_w_pool, v_pool_scale, v_w_mix_out,
                          v_norm_mem, v_norm_xa_pre, v_norm_xa_post, v_w_xq, v_w_xkv, v_w_xo, v_norm_ffn_pre,
                          v_norm_ffn_post, v_w_up, v_conv_w, v_conv_b, v_w_down)))

    big_names = ("w_in", "w_mix_out", "w_xq", "w_xo", "w_xkv", "w_up", "w_down")
    shards = []
    for k in big_names:
        sh = w[k][0].astype(BF16)
        if k == "w_in":
            sh = jnp.pad(sh, ((0, 0), (0, D_IN_PAD - sh.shape[1])))
        shards.append(sh)
    conv_w_sh = jnp.pad(conv_w[0, :, 0, :], ((0, 5), (0, 0)))
    gathered = _all_gather(shards + [conv_w_sh], "gather_weights")
    g_in, g_mix, g_xq, g_xo, g_xkv, g_up, g_down, g_cw = gathered
    wts = dict(w_in=g_in.reshape(D_MODEL, D_IN_PAD), w_mix=g_mix.reshape(D_MODEL, D_MODEL),
               w_xq=g_xq.reshape(D_MODEL, D_MODEL), w_xo=g_xo.reshape(D_MODEL, D_MODEL),
               w_xkv=g_xkv, w_up=g_up, w_down=g_down.reshape(D_FF, D_MODEL))
    conv_w_full = g_cw[:, :3, :].transpose(1, 0, 2).reshape(3, 2 * D_FF)

    gains = dict(mix_pre=norm_mix_pre, mix_post=norm_mix_post, mem=norm_mem, xa_pre=norm_xa_pre,
                 xa_post=norm_xa_post, ffn_pre=norm_ffn_pre, ffn_post=norm_ffn_post)
    loss, grad_x, big, small = _local_step(x[0], mem[0], loss_target[0], gains, b_forget, w_pool[0], pool_scale[0],
                                           conv_w_full, conv_b, wts)

    d_cw = jnp.pad(small.pop("conv_w"), ((0, 5), (0, 0))).reshape(8, NDEV, D_MODEL).transpose(1, 0, 2)
    full = [big["w_in"].reshape(NDEV, D_MODEL // NDEV, D_IN_PAD), big["w_mix"].reshape(NDEV, D_MODEL // NDEV, D_MODEL),
            big["w_xq"].reshape(NDEV, D_MODEL // NDEV, D_MODEL), big["w_xo"].reshape(NDEV, D_MODEL // NDEV, D_MODEL),
            big["w_xkv"], big["w_up"], big["w_down"].reshape(NDEV, D_FF // NDEV, D_MODEL), d_cw]
    parts = _scatter_blocks(full, "scatter_grads")
    small_like = dict(mix_pre=norm_mix_pre, mix_post=norm_mix_post, mem=norm_mem, xa_pre=norm_xa_pre,
                      xa_post=norm_xa_post, ffn_pre=norm_ffn_pre, ffn_post=norm_ffn_post, conv_b=conv_b,
                      w_pool=w_pool, pool_scale=pool_scale, b_forget=b_forget)
    small = {k: small[k].reshape(small_like[k].shape) for k in SMALL_ORDER}
    (small_parts,) = _all_gather([_pack_small(small)], "gather_small_grads")

    res = {}
    for k, p in zip(big_names, parts[:7]):
        if k == "w_in":
            p = p[:, :, :w_in.shape[2]]
        res[k] = [a[None] for a in _adamw(p, w[k][0], mo[k][0], vo[k][0], "adamw_" + k)]
    pad_cw = lambda a: jnp.pad(a[0, :, 0, :], ((0, 5), (0, 0)))
    res["conv_w"] = [a[:3][None, :, None, :] for a in
                     _adamw(parts[7], pad_cw(conv_w), pad_cw(m_conv_w), pad_cw(v_conv_w), "adamw_conv_w")]
    key_of = dict(mix_pre="norm_mix_pre", mix_post="norm_mix_post", mem="norm_mem", xa_pre="norm_xa_pre",
                  xa_post="norm_xa_post", ffn_pre="norm_ffn_pre", ffn_post="norm_ffn_post", conv_b="conv_b",
                  w_pool="w_pool", pool_scale="pool_scale", b_forget="b_forget")
    pack_of = lambda src: _pack_small({k: src[key_of[k]] for k in SMALL_ORDER})
    small_out = _adamw(small_parts, pack_of(w), pack_of(mo), pack_of(vo), "adamw_small")
    small_out = [_unpack_small(a, small_like) for a in small_out]
    for k in SMALL_ORDER:
        res[key_of[k]] = [so[k] for so in small_out]

    total = lax.psum(loss[0, 0], ("x", "y", "c"))
    outs = [total, grad_x[None]]
    for idx in range(4):
        outs += [res[k][idx] for k in names]
    return tuple(outs)
```

```python
import functools
import math

import jax
import jax.numpy as jnp
from jax import lax
from jax.experimental import pallas as pl
from jax.experimental.pallas import tpu as pltpu

F32 = jnp.float32
BF16 = jnp.bfloat16

NDEV = 8
D_MODEL = 1024
D_POOL = 256
D_FOX = 768
FOX_HEADS = 12
HEAD_PAIRS = FOX_HEADS // 2
XA_HEADS = 4
XA_DIM = 256
D_FF = 4096
D_IN_PAD = 2688
F_COL = 2560
POOL_HALO = 16
NORM_EPS = 1e-6
NEG = -1e30

ADAM_LR = 0.001
ADAM_B1 = 0.9
ADAM_B2 = 0.999
ADAM_EPS = 1e-08
ADAM_WD = 0.01
ADAM_STEP = 10

TM = 512
TQ = 512
TN_FF = 512
VMEM_LIMIT = 48 * 1024 * 1024
MESH = pl.DeviceIdType.MESH


def _cp(*sem):
    return pltpu.CompilerParams(dimension_semantics=sem, vmem_limit_bytes=VMEM_LIMIT)


def _dot(a, b, dims):
    return lax.dot_general(a, b, (dims, ((), ())), preferred_element_type=F32)


NN = ((1,), (0,))
NT = ((1,), (1,))
TN = ((0,), (0,))


def _mm(a, b, mode, out_dtype, tm, tn, tk, name, b_cols=None, out_cols=None):
    a_list = list(a) if isinstance(a, (list, tuple)) else [a]
    b_list = list(b) if isinstance(b, (list, tuple)) else [b]
    assert len(a_list) == 1 or len(b_list) == 1
    if mode == "tn":
        K, M = a_list[0].shape
        assert len(a_list) == 1
        Ns = [x.shape[1] for x in b_list]
        N = sum(Ns)
        assert b_cols is None
    else:
        assert len(b_list) == 1
        M = a_list[0].shape[0]
        Ks = [x.shape[1] for x in a_list]
        K = sum(Ks)
        if b_cols is None:
            N = b_list[0].shape[0] if mode == "nt" else b_list[0].shape[1]
        else:
            N = b_list[0].shape[1] if mode == "nt" else NDEV * b_cols
    assert M % tm == 0 and N % tn == 0 and K % tk == 0, (name, M, N, K)
    grid = (M // tm, N // tn, K // tk)
    nk = grid[2]
    dims = {"nn": NN, "nt": NT, "tn": TN}[mode]

    in_specs = []
    if mode == "tn":
        in_specs.append(pl.BlockSpec((tk, tm), lambda i, j, k: (k, i)))
        if len(b_list) == 1:
            in_specs.append(pl.BlockSpec((tk, tn), lambda i, j, k: (k, j)))
        else:
            nj1 = Ns[0] // tn
            in_specs.append(pl.BlockSpec((tk, tn), lambda i, j, k: (k, jnp.minimum(j, nj1 - 1))))
            in_specs.append(pl.BlockSpec((tk, tn), lambda i, j, k: (k, jnp.maximum(j - nj1, 0))))
    else:
        if len(a_list) == 1:
            in_specs.append(pl.BlockSpec((tm, tk), lambda i, j, k: (i, k)))
        else:
            nk1 = Ks[0] // tk
            in_specs.append(pl.BlockSpec((tm, tk), lambda i, j, k: (i, jnp.minimum(k, nk1 - 1))))
            in_specs.append(pl.BlockSpec((tm, tk), lambda i, j, k: (i, jnp.maximum(k - nk1, 0))))
        if b_cols is None:
            if mode == "nn":
                in_specs.append(pl.BlockSpec((tk, tn), lambda i, j, k: (k, j)))
            else:
                in_specs.append(pl.BlockSpec((tn, tk), lambda i, j, k: (j, k)))
        else:
            if mode == "nn":
                per = b_cols // tn
                in_specs.append(pl.BlockSpec((None, tk, tn), lambda i, j, k: (j // per, k, j % per)))
            else:
                per = b_cols // tk
                in_specs.append(pl.BlockSpec((None, tn, tk), lambda i, j, k: (k // per, j, k % per)))
    if out_cols is None:
        out_spec = pl.BlockSpec((tm, tn), lambda i, j, k: (i, j))
        out_shape = jax.ShapeDtypeStruct((M, N), out_dtype)
    else:
        pero = out_cols // tn
        out_spec = pl.BlockSpec((None, tm, tn), lambda i, j, k: (j // pero, i, j % pero))
        out_shape = jax.ShapeDtypeStruct((NDEV, M, out_cols), out_dtype)

    two_a = len(a_list) == 2
    two_b = len(b_list) == 2

    def body(*refs):
        o_ref, acc_ref = refs[-2], refs[-1]
        j = pl.program_id(1)
        k = pl.program_id(2)

        @pl.when(k == 0)
        def _():
            acc_ref[...] = jnp.zeros_like(acc_ref)

        if two_a:
            a1, a2, b1 = refs[0], refs[1], refs[2]
            nk1_ = Ks[0] // tk

            @pl.when(k < nk1_)
            def _():
                acc_ref[...] += _dot(a1[...], b1[...], dims)

            @pl.when(k >= nk1_)
            def _():
                acc_ref[...] += _dot(a2[...], b1[...], dims)
        elif two_b:
            a1, b1, b2 = refs[0], refs[1], refs[2]
            nj1_ = Ns[0] // tn

            @pl.when(j < nj1_)
            def _():
                acc_ref[...] += _dot(a1[...], b1[...], dims)

            @pl.when(j >= nj1_)
            def _():
                acc_ref[...] += _dot(a1[...], b2[...], dims)
        else:
            acc_ref[...] += _dot(refs[0][...], refs[1][...], dims)

        @pl.when(k == nk - 1)
        def _():
            o_ref[...] = acc_ref[...].astype(o_ref.dtype)

    return pl.pallas_call(
        body, name=name, grid=grid, in_specs=in_specs, out_specs=out_spec, out_shape=out_shape,
        scratch_shapes=[pltpu.VMEM((tm, tn), F32)],
        compiler_params=_cp("parallel", "parallel", "arbitrary"),
    )(*a_list, *b_list)


def _rstd(x):
    return lax.rsqrt(jnp.mean(x * x, axis=-1, keepdims=True) + NORM_EPS)


def _norm_bwd_rows(dxn, xn, r):
    return r * (dxn - xn * jnp.mean(dxn * xn, axis=-1, keepdims=True))


def _row_spec(tm, d):
    return pl.BlockSpec((tm, d), lambda i: (i, 0))


def _vec_spec(d):
    return pl.BlockSpec((1, d), lambda i: (0, 0))


def _norm_fwd(x, g, name):
    s, d = x.shape
    tm = min(TM, s)

    def body(x_ref, g_ref, h_ref):
        xv = x_ref[...]
        h_ref[...] = (xv * _rstd(xv) * g_ref[...]).astype(BF16)

    return pl.pallas_call(
        body, name=name, grid=(s // tm,), in_specs=[_row_spec(tm, d), _vec_spec(d)],
        out_specs=_row_spec(tm, d), out_shape=jax.ShapeDtypeStruct((s, d), BF16),
        compiler_params=_cp("parallel"),
    )(x, g)


def _resid_norm_fwd(x_in, y, g_post, g_next, name):
    s, d = x_in.shape

    def body(x_ref, y_ref, gp_ref, gn_ref, xo_ref, h_ref):
        yv = y_ref[...]
        xo = x_ref[...] + yv * _rstd(yv) * gp_ref[...]
        xo_ref[...] = xo
        h_ref[...] = (xo * _rstd(xo) * gn_ref[...]).astype(BF16)

    return pl.pallas_call(
        body, name=name, grid=(s // TM,),
        in_specs=[_row_spec(TM, d), _row_spec(TM, d), _vec_spec(d), _vec_spec(d)],
        out_specs=[_row_spec(TM, d), _row_spec(TM, d)],
        out_shape=[jax.ShapeDtypeStruct((s, d), F32), jax.ShapeDtypeStruct((s, d), BF16)],
        compiler_params=_cp("parallel"),
    )(x_in, y, g_post, g_next)


def _norm_bwd(dh, x, dx_res, g_pre, name, prev=None):
    s, d = x.shape
    tm = min(TM, s)
    has_prev = prev is not None

    def body(*refs):
        if has_prev:
            dh_ref, x_ref, dr_ref, g_ref, y_ref, gp_ref, dx_ref, dg_ref, dy_ref, dgp_ref = refs
        else:
            dh_ref, x_ref, dr_ref, g_ref, dx_ref, dg_ref = refs
        i = pl.program_id(0)
        xv = x_ref[...]
        r = _rstd(xv)
        xn = xv * r
        dhv = dh_ref[...].astype(F32)
        dx = dr_ref[...] + _norm_bwd_rows(dhv * g_ref[...], xn, r)
        dx_ref[...] = dx
        dg = jnp.sum(dhv * xn, axis=0, keepdims=True)

        @pl.when(i == 0)
        def _():
            dg_ref[...] = dg

        @pl.when(i > 0)
        def _():
            dg_ref[...] += dg

        if has_prev:
            yv = y_ref[...]
            r2 = _rstd(yv)
            yn = yv * r2
            dy_ref[...] = _norm_bwd_rows(dx * gp_ref[...], yn, r2).astype(BF16)
            dgp = jnp.sum(dx * yn, axis=0, keepdims=True)

            @pl.when(i == 0)
            def _():
                dgp_ref[...] = dgp

            @pl.when(i > 0)
            def _():
                dgp_ref[...] += dgp

    in_specs = [_row_spec(tm, d), _row_spec(tm, d), _row_spec(tm, d), _vec_spec(d)]
    out_specs = [_row_spec(tm, d), _vec_spec(d)]
    out_shape = [jax.ShapeDtypeStruct((s, d), F32), jax.ShapeDtypeStruct((1, d), F32)]
    args = [dh, x, dx_res, g_pre]
    if has_prev:
        in_specs += [_row_spec(tm, d), _vec_spec(d)]
        out_specs += [_row_spec(tm, d), _vec_spec(d)]
        out_shape += [jax.ShapeDtypeStruct((s, d), BF16), jax.ShapeDtypeStruct((1, d), F32)]
        args += list(prev)
    return pl.pallas_call(
        body, name=name, grid=(s // tm,), in_specs=in_specs, out_specs=out_specs, out_shape=out_shape,
        compiler_params=_cp("arbitrary"),
    )(*args)


def _loss_bwd(x2, y3, g_post, tgt, name):
    s, d = x2.shape

    def body(x_ref, y_ref, g_ref, t_ref, loss_ref, dx_ref, dy_ref, dg_ref):
        i = pl.program_id(0)
        yv = y_ref[...]
        r = _rstd(yv)
        yn = yv * r
        e = x_ref[...] + yn * g_ref[...] - t_ref[...]
        part = 0.5 * jnp.sum(jnp.mean(e * e, axis=-1, keepdims=True), axis=0, keepdims=True)
        dx = e * (1.0 / d)
        dx_ref[...] = dx
        dy_ref[...] = _norm_bwd_rows(dx * g_ref[...], yn, r).astype(BF16)
        dg = jnp.sum(dx * yn, axis=0, keepdims=True)
        part = jnp.broadcast_to(part, (1, 128))

        @pl.when(i == 0)
        def _():
            dg_ref[...] = dg
            loss_ref[...] = part

        @pl.when(i > 0)
        def _():
            dg_ref[...] += dg
            loss_ref[...] += part

    return pl.pallas_call(
        body, name=name, grid=(s // TM,),
        in_specs=[_row_spec(TM, d), _row_spec(TM, d), _vec_spec(d), _row_spec(TM, d)],
        out_specs=[_vec_spec(128), _row_spec(TM, d), _row_spec(TM, d), _vec_spec(d)],
        out_shape=[jax.ShapeDtypeStruct((1, 128), F32), jax.ShapeDtypeStruct((s, d), F32),
                   jax.ShapeDtypeStruct((s, d), BF16), jax.ShapeDtypeStruct((1, d), F32)],
        compiler_params=_cp("arbitrary"),
    )(x2, y3, g_post, tgt)


def _split3(v):
    hi = v.astype(BF16)
    r1 = v - hi.astype(F32)
    mid = r1.astype(BF16)
    lo = (r1 - mid.astype(F32)).astype(BF16)
    return hi, mid, lo


def _tri_dot(tri, v):
    hi, mid, lo = _split3(v)
    return _dot(tri, hi, NN) + _dot(tri, mid, NN) + _dot(tri, lo, NN)


def _gate_cumsum(fraw, b_pad, name):
    s = fraw.shape[0]

    def body(f_ref, b_ref, flog_ref, cum_ref, carry_ref):
        i = pl.program_id(0)

        @pl.when(i == 0)
        def _():
            carry_ref[...] = jnp.zeros_like(carry_ref)

        flog = f_ref[...] + b_ref[...]
        flog_ref[...] = flog
        lf = jnp.minimum(flog, 0.0) - jnp.log(1.0 + jnp.exp(-jnp.abs(flog)))
        lane = lax.broadcasted_iota(jnp.int32, (1, 128), 1)
        lf = jnp.where(lane < FOX_HEADS, lf, 0.0)
        row = lax.broadcasted_iota(jnp.int32, (TM, TM), 0)
        col = lax.broadcasted_iota(jnp.int32, (TM, TM), 1)
        tri = (row >= col).astype(BF16)
        cum = _tri_dot(tri, lf) + carry_ref[...]
        cum_ref[...] = cum
        carry_ref[...] = cum[TM - 1:TM, :]

    return pl.pallas_call(
        body, name=name, grid=(s // TM,),
        in_specs=[_row_spec(TM, 128), _vec_spec(128)],
        out_specs=[_row_spec(TM, 128), _row_spec(TM, 128)],
        out_shape=[jax.ShapeDtypeStruct((s, 128), F32), jax.ShapeDtypeStruct((s, 128), F32)],
        scratch_shapes=[pltpu.VMEM((1, 128), F32)],
        compiler_params=_cp("arbitrary"),
    )(fraw, b_pad)


def _gate_bwd(dcum, flog, name):
    s = dcum.shape[0]
    n = s // TM

    def body(dc_ref, fl_ref, dp_ref, db_ref, carry_ref):
        i = pl.program_id(0)

        @pl.when(i == 0)
        def _():
            carry_ref[...] = jnp.zeros_like(carry_ref)

        row = lax.broadcasted_iota(jnp.int32, (TM, TM), 0)
        col = lax.broadcasted_iota(jnp.int32, (TM, TM), 1)
        tri = (row <= col).astype(BF16)
        dlf = _tri_dot(tri, dc_ref[...]) + carry_ref[...]
        carry_ref[...] = dlf[0:1, :]
        lane = lax.broadcasted_iota(jnp.int32, (1, 128), 1)
        df = jnp.where(lane < FOX_HEADS, dlf / (1.0 + jnp.exp(fl_ref[...])), 0.0)
        dp_ref[...] = df.astype(BF16)
        db = jnp.sum(df, axis=0, keepdims=True)

        @pl.when(i == 0)
        def _():
            db_ref[...] = db

        @pl.when(i > 0)
        def _():
            db_ref[...] += db

    rev = lambda i: (n - 1 - i, 0)
    return pl.pallas_call(
        body, name=name, grid=(n,),
        in_specs=[pl.BlockSpec((TM, 128), rev), pl.BlockSpec((TM, 128), rev)],
        out_specs=[pl.BlockSpec((TM, 128), rev), _vec_spec(128)],
        out_shape=[jax.ShapeDtypeStruct((s, 128), BF16), jax.ShapeDtypeStruct((1, 128), F32)],
        scratch_shapes=[pltpu.VMEM((1, 128), F32)],
        compiler_params=_cp("arbitrary"),
    )(dcum, flog)


def _pool_consts(i, rows):
    lane = lax.broadcasted_iota(jnp.int32, (rows, D_POOL), 1)
    t1 = lax.broadcasted_iota(jnp.int32, (rows, D_POOL), 0) + i * TM + 1
    win = jnp.where(lane < 64, 2, jnp.where(lane < 128, 4, jnp.where(lane < 192, 8, 16)))
    inv = 1.0 / jnp.minimum(t1, win).astype(F32)
    return lane, inv


def _by_group(lane, s2, s4, s8, s16):
    return jnp.where(lane < 64, s2, jnp.where(lane < 128, s4, jnp.where(lane < 192, s8, s16)))


def _pool_diff(i, u_ref, halo_ref):
    u = u_ref[...].astype(F32)
    halo = jnp.where(i > 0, halo_ref[...].astype(F32), 0.0)
    ext = jnp.concatenate([halo, u], axis=0)
    s2 = ext + pltpu.roll(ext, 1, 0)
    s4 = s2 + pltpu.roll(s2, 2, 0)
    s8 = s4 + pltpu.roll(s4, 4, 0)
    s16 = s8 + pltpu.roll(s8, 8, 0)
    lane, inv = _pool_consts(i, TM)
    sel = _by_group(lane, s2[POOL_HALO:], s4[POOL_HALO:], s8[POOL_HALO:], s16[POOL_HALO:])
    return sel * inv - u


def _pool_fwd(proj, wbd, scale, ycat, name):
    s = proj.shape[0]
    hb = TM // POOL_HALO

    def body(u_ref, halo_ref, w_ref, sc_ref, y_any, y_ref):
        del y_any
        i = pl.program_id(0)
        diff = _pool_diff(i, u_ref, halo_ref)
        mixed = _dot(diff.astype(BF16), w_ref[...], NN)
        y_ref[...] = (mixed * sc_ref[...]).astype(BF16)

    return pl.pallas_call(
        body, name=name, grid=(s // TM,),
        in_specs=[pl.BlockSpec((TM, D_POOL), lambda i: (i, 0)),
                  pl.BlockSpec((POOL_HALO, D_POOL), lambda i: (jnp.maximum(i * hb - 1, 0), 0)),
                  pl.BlockSpec((D_POOL, D_POOL), lambda i: (0, 0)), _vec_spec(D_POOL),
                  pl.BlockSpec(memory_space=pl.ANY)],
        out_specs=pl.BlockSpec((TM, D_POOL), lambda i: (i, 0)),
        out_shape=jax.ShapeDtypeStruct(ycat.shape, ycat.dtype),
        input_output_aliases={4: 0},
        compiler_params=_cp("parallel"),
    )(proj, proj, wbd, scale, ycat)


def _pool_bwd(proj, dycat, wbd, scale, name):
    s = proj.shape[0]
    n = s // TM
    hb = TM // POOL_HALO
    last_halo = s // POOL_HALO - 1

    def body(u_ref, halo_ref, dy_ref, dyp_ref, w_ref, sc_ref, dp_ref, dw_ref, dsc_ref):
        i = pl.program_id(0)
        diff = _pool_diff(i, u_ref, halo_ref)
        diff_b = diff.astype(BF16)
        mixed = _dot(diff_b, w_ref[...], NN)
        dy = dy_ref[...].astype(F32)
        dmix = (dy * sc_ref[...]).astype(BF16)
        dyp = jnp.where(i < n - 1, dyp_ref[...].astype(F32), 0.0)
        dmix_p = (dyp * sc_ref[...]).astype(BF16)
        dd = _dot(dmix, w_ref[...], NT)
        dd_p = _dot(dmix_p, w_ref[...], NT)
        lane, inv = _pool_consts(i, TM)
        _, inv_p = _pool_consts(i + 1, POOL_HALO)
        ext = jnp.concatenate([dd * inv, dd_p * inv_p], axis=0)
        rows = TM + POOL_HALO
        l2 = ext + pltpu.roll(ext, rows - 1, 0)
        l4 = l2 + pltpu.roll(l2, rows - 2, 0)
        l8 = l4 + pltpu.roll(l4, rows - 4, 0)
        l16 = l8 + pltpu.roll(l8, rows - 8, 0)
        du = _by_group(lane, l2[:TM], l4[:TM], l8[:TM], l16[:TM]) - dd
        dp_ref[...] = du.astype(BF16)
        dw = _dot(diff_b, dmix, TN)
        dsc = jnp.sum(dy * mixed, axis=0, keepdims=True)

        @pl.when(i == 0)
        def _():
            dw_ref[...] = dw
            dsc_ref[...] = dsc

        @pl.when(i > 0)
        def _():
            dw_ref[...] += dw
            dsc_ref[...] += dsc

    return pl.pallas_call(
        body, name=name, grid=(n,),
        in_specs=[pl.BlockSpec((TM, D_POOL), lambda i: (i, 0)),
                  pl.BlockSpec((POOL_HALO, D_POOL), lambda i: (jnp.maximum(i * hb - 1, 0), 0)),
                  pl.BlockSpec((TM, D_POOL), lambda i: (i, 0)),
                  pl.BlockSpec((POOL_HALO, D_POOL), lambda i: (jnp.minimum((i + 1) * hb, last_halo), 0)),
                  pl.BlockSpec((D_POOL, D_POOL), lambda i: (0, 0)), _vec_spec(D_POOL)],
        out_specs=[pl.BlockSpec((TM, D_POOL), lambda i: (i, 0)),
                   pl.BlockSpec((D_POOL, D_POOL), lambda i: (0, 0)), _vec_spec(D_POOL)],
        out_shape=[jax.ShapeDtypeStruct((s, D_POOL), BF16),
                   jax.ShapeDtypeStruct((D_POOL, D_POOL), F32), jax.ShapeDtypeStruct((1, D_POOL), F32)],
        compiler_params=_cp("arbitrary"),
    )(proj, proj, dycat, dycat, wbd, scale)


Q_BLK = D_POOL // 128
K_BLK = Q_BLK + D_FOX // 128
V_BLK = K_BLK + D_FOX // 128


def _head_masks():
    lane = lax.broadcasted_iota(jnp.int32, (1, 128), 1)
    return [lane < 64, lane >= 64]


def _fox_scores(qh, k2, cq_col, ck_row, row_off):
    sc = _dot(qh, k2, NT) * 0.125 + cq_col - ck_row
    row = lax.broadcasted_iota(jnp.int32, sc.shape, 0) + row_off
    col = lax.broadcasted_iota(jnp.int32, sc.shape, 1)
    return jnp.where(row >= col, sc, NEG)


def _fox_fwd(proj, cum_c, cum_r, name):
    s = proj.shape[0]
    nq = s // TQ

    def body(q_ref, k_ref, v_ref, cq_ref, ck_ref, o_ref, lse_ref, m_ref, l_ref, acc_ref):
        qi, ki = pl.program_id(1), pl.program_id(2)
        masks = _head_masks()

        @pl.when(ki == 0)
        def _():
            m_ref[...] = jnp.full_like(m_ref, NEG)
            l_ref[...] = jnp.zeros_like(l_ref)
            acc_ref[...] = jnp.zeros_like(acc_ref)

        @pl.when(ki <= qi)
        def _():
            q2, k2, v2 = q_ref[...], k_ref[...], v_ref[...]
            cq, ck = cq_ref[...], ck_ref[...]
            pv = []
            alpha = []
            for hh in range(2):
                qh = jnp.where(masks[hh], q2, jnp.zeros_like(q2))
                vh = jnp.where(masks[hh], v2, jnp.zeros_like(v2))
                sc = _fox_scores(qh, k2, cq[:, hh:hh + 1], ck[hh:hh + 1, :], (qi - ki) * TQ)
                m_prev = m_ref[hh]
                m_new = jnp.maximum(m_prev, jnp.max(sc, axis=1, keepdims=True))
                a = jnp.exp(m_prev - m_new)
                p = jnp.exp(sc - m_new)
                l_ref[hh] = a * l_ref[hh] + jnp.sum(p, axis=1, keepdims=True)
                m_ref[hh] = m_new
                pv.append(_dot(p.astype(BF16), vh, NN))
                alpha.append(a)
            acc_ref[...] = acc_ref[...] * jnp.where(masks[0], alpha[0], alpha[1]) + pv[0] + pv[1]

        @pl.when(ki == qi)
        def _():
            inv = jnp.where(masks[0], 1.0 / l_ref[0], 1.0 / l_ref[1])
            o_ref[...] = (acc_ref[...] * inv).astype(BF16)
            lane = lax.broadcasted_iota(jnp.int32, (1, 128), 1)
            lse_ref[...] = jnp.where(lane == 0, m_ref[0] + jnp.log(l_ref[0]), m_ref[1] + jnp.log(l_ref[1]))

    kv_row = lambda p, qi, ki: jnp.minimum(ki, qi)
    return pl.pallas_call(
        body, name=name, grid=(HEAD_PAIRS, nq, nq),
        in_specs=[pl.BlockSpec((TQ, 128), lambda p, qi, ki: (qi, Q_BLK + p)),
                  pl.BlockSpec((TQ, 128), lambda p, qi, ki: (kv_row(p, qi, ki), K_BLK + p)),
                  pl.BlockSpec((TQ, 128), lambda p, qi, ki: (kv_row(p, qi, ki), V_BLK + p)),
                  pl.BlockSpec((None, TQ, 128), lambda p, qi, ki: (p, qi, 0)),
                  pl.BlockSpec((None, 8, TQ), lambda p, qi, ki: (p, 0, kv_row(p, qi, ki)))],
        out_specs=[pl.BlockSpec((TQ, 128), lambda p, qi, ki: (qi, Q_BLK + p)),
                   pl.BlockSpec((None, TQ, 128), lambda p, qi, ki: (p, qi, 0))],
        out_shape=[jax.ShapeDtypeStruct((s, D_MODEL), BF16), jax.ShapeDtypeStruct((HEAD_PAIRS, s, 128), F32)],
        scratch_shapes=[pltpu.VMEM((2, TQ, 1), F32), pltpu.VMEM((2, TQ, 1), F32), pltpu.VMEM((TQ, 128), F32)],
        compiler_params=_cp("parallel", "parallel", "arbitrary"),
    )(proj, proj, proj, cum_c, cum_r)


def _fox_bwd(proj, ycat, dycat, cum_c, cum_r, lse, name):
    s = proj.shape[0]
    nq = s // TQ

    def body(q_ref, k_ref, v_ref, do_ref, o_ref, cq_ref, ck_ref, lse_ref,
             dq_ref, dk_ref, dv_ref, dc_ref, dcq_ref, dq_acc, dk_acc, dv_acc, dc_acc, dcq_acc):
        ki, qi = pl.program_id(1), pl.program_id(2)
        masks = _head_masks()
        lane = lax.broadcasted_iota(jnp.int32, (1, 128), 1)

        @pl.when(qi == ki)
        def _():
            dk_acc[...] = jnp.zeros_like(dk_acc)
            dv_acc[...] = jnp.zeros_like(dv_acc)
            dc_acc[...] = jnp.zeros_like(dc_acc)

        @pl.when(qi >= ki)
        def _():
            q2, k2, v2, do2 = q_ref[...], k_ref[...], v_ref[...], do_ref[...]
            cq, ck, lse2 = cq_ref[...], ck_ref[...], lse_ref[...]
            dd = do2.astype(F32) * o_ref[...].astype(F32)
            d0 = jnp.sum(jnp.where(masks[0], dd, 0.0), axis=1, keepdims=True)
            drow = [d0, jnp.sum(dd, axis=1, keepdims=True) - d0]
            dq = jnp.zeros((TQ, 128), F32)
            dk = jnp.zeros((TQ, 128), F32)
            dv = jnp.zeros((TQ, 128), F32)
            dcs = []
            rs = []
            for hh in range(2):
                zero = jnp.zeros_like(q2)
                qh = jnp.where(masks[hh], q2, zero)
                kh = jnp.where(masks[hh], k2, zero)
                doh = jnp.where(masks[hh], do2, zero)
                sc = _fox_scores(qh, k2, cq[:, hh:hh + 1], ck[hh:hh + 1, :], (qi - ki) * TQ)
                p = jnp.exp(sc - lse2[:, hh:hh + 1])
                dp = _dot(doh, v2, NT)
                ds = p * (dp - drow[hh])
                dsb = (ds * 0.125).astype(BF16)
                dv = dv + _dot(p.astype(BF16), doh, TN)
                dk = dk + _dot(dsb, qh, TN)
                dq = dq + _dot(dsb, kh, NN)
                dcs.append(-jnp.sum(ds, axis=0, keepdims=True))
                rs.append(jnp.sum(ds, axis=1, keepdims=True))
            dk_acc[...] += dk
            dv_acc[...] += dv
            dc_acc[0:1, :] += dcs[0]
            dc_acc[1:2, :] += dcs[1]
            dcq = jnp.where(lane == 0, rs[0], rs[1])

            @pl.when(ki == 0)
            def _():
                dq_acc[qi] = dq
                dcq_acc[qi] = dcq

            @pl.when(ki > 0)
            def _():
                dq_acc[qi] += dq
                dcq_acc[qi] += dcq

            @pl.when(qi == ki)
            def _():
                rows = pl.ds(pl.multiple_of(qi * TQ, TQ), TQ)
                dq_ref[rows, :] = dq_acc[qi].astype(BF16)
                dcq_ref[rows, :] = dcq_acc[qi]

        @pl.when(qi == nq - 1)
        def _():
            dk_ref[...] = dk_acc[...].astype(BF16)
            dv_ref[...] = dv_acc[...].astype(BF16)
            dc_ref[...] = dc_acc[...]

    q_row = lambda p, ki, qi: jnp.maximum(qi, ki)
    return pl.pallas_call(
        body, name=name, grid=(HEAD_PAIRS, nq, nq),
        in_specs=[pl.BlockSpec((TQ, 128), lambda p, ki, qi: (q_row(p, ki, qi), Q_BLK + p)),
                  pl.BlockSpec((TQ, 128), lambda p, ki, qi: (ki, K_BLK + p)),
                  pl.BlockSpec((TQ, 128), lambda p, ki, qi: (ki, V_BLK + p)),
                  pl.BlockSpec((TQ, 128), lambda p, ki, qi: (q_row(p, ki, qi), Q_BLK + p)),
                  pl.BlockSpec((TQ, 128), lambda p, ki, qi: (q_row(p, ki, qi), Q_BLK + p)),
                  pl.BlockSpec((None, TQ, 128), lambda p, ki, qi: (p, q_row(p, ki, qi), 0)),
                  pl.BlockSpec((None, 8, TQ), lambda p, ki, qi: (p, 0, ki)),
                  pl.BlockSpec((None, TQ, 128), lambda p, ki, qi: (p, q_row(p, ki, qi), 0))],
        out_specs=[pl.BlockSpec((s, 128), lambda p, ki, qi: (0, p)),
                   pl.BlockSpec((TQ, 128), lambda p, ki, qi: (ki, p)),
                   pl.BlockSpec((TQ, 128), lambda p, ki, qi: (ki, p)),
                   pl.BlockSpec((None, 8, TQ), lambda p, ki, qi: (p, 0, ki)),
                   pl.BlockSpec((None, s, 128), lambda p, ki, qi: (p, 0, 0))],
        out_shape=[jax.ShapeDtypeStruct((s, D_FOX), BF16)] * 3 + [jax.ShapeDtypeStruct((HEAD_PAIRS, 8, s), F32),
                                                                 jax.ShapeDtypeStruct((HEAD_PAIRS, s, 128), F32)],
        scratch_shapes=[pltpu.VMEM((nq, TQ, 128), F32), pltpu.VMEM((TQ, 128), F32), pltpu.VMEM((TQ, 128), F32),
                        pltpu.VMEM((8, TQ), F32), pltpu.VMEM((nq, TQ, 128), F32)],
        compiler_params=_cp("arbitrary", "arbitrary", "arbitrary"),
    )(proj, proj, proj, dycat, ycat, cum_c, cum_r, lse)


def _at_lane(v, lane, width):
    return jnp.pad(v[..., None], [(0, 0)] * v.ndim + [(lane, width - 1 - lane)])


def _pieces(v):
    hi = lax.reduce_precision(v, 8, 7)
    r1 = v - hi
    mid = lax.reduce_precision(r1, 8, 7)
    lo = r1 - mid
    return [hi.astype(BF16), mid.astype(BF16), lo.astype(BF16)]


def _aug(first, second):
    s = first[0].shape[0]
    blk = _at_lane(first[0], 0, 64)
    for j, v in enumerate(list(first[1:]) + list(second)):
        blk = blk + _at_lane(v, j + 1, 64)
    blk = blk.reshape(s, HEAD_PAIRS, 2, 64)[:, :, ::-1, :]
    return blk.reshape(s, HEAD_PAIRS, 128).transpose(1, 0, 2)


def _pair_to_heads(a, lane0, lane1):
    s = a.shape[1]
    both = _at_lane(a[:, :, lane0], 0, 2) + _at_lane(a[:, :, lane1], 1, 2)
    return both.transpose(1, 0, 2).reshape(s, FOX_HEADS)


def _causal_pairs(nq, key_major):
    if key_major:
        pairs = [(q, k) for k in range(nq) for q in range(k, nq)]
    else:
        pairs = [(q, k) for q in range(nq) for k in range(q + 1)]
    return (jnp.asarray([p[0] for p in pairs], jnp.int32), jnp.asarray([p[1] for p in pairs], jnp.int32))


def _diag_mask(sc):
    row = lax.broadcasted_iota(jnp.int32, sc.shape, 0)
    col = lax.broadcasted_iota(jnp.int32, sc.shape, 1)
    return jnp.where(row >= col, sc, NEG)


def _fox_fwd2(proj, aq, ak, name):
    s = proj.shape[0]
    nq = s // TQ
    qi_arr, ki_arr = _causal_pairs(nq, key_major=False)

    def body(qi_ref, ki_ref, q_ref, k_ref, v_ref, aq_ref, ak_ref, o_ref, lse_ref, m_ref, acc_ref, aux_ref):
        t = pl.program_id(1)
        qi, ki = qi_ref[t], ki_ref[t]
        lane = lax.broadcasted_iota(jnp.int32, (1, 128), 1)
        masks = [lane < 64, lane >= 64]
        ones_v = jnp.where((lane & 63) == 8, 1.0, 0.0).astype(BF16)

        @pl.when(ki == 0)
        def _():
            m_ref[...] = jnp.full_like(m_ref, NEG)
            acc_ref[...] = jnp.zeros_like(acc_ref)
            aux_ref[...] = jnp.zeros_like(aux_ref)

        def step(diag):
            q2s = q_ref[...] * 0.125
            k2, v2, aq2, ak2 = k_ref[...], v_ref[...], aq_ref[...], ak_ref[...]
            pv, alpha = [], []
            for hh in range(2):
                qh = jnp.where(masks[hh], q2s, aq2)
                kh = jnp.where(masks[hh], k2, ak2)
                vh = jnp.where(masks[hh], v2, ones_v)
                sc = _dot(qh, kh, NT)
                if diag:
                    sc = _diag_mask(sc)
                m_prev = m_ref[hh]
                m_new = jnp.maximum(m_prev, jnp.max(sc, axis=1, keepdims=True))
                m_ref[hh] = m_new
                alpha.append(jnp.exp(m_prev - m_new))
                pv.append(_dot(jnp.exp(sc - m_new).astype(BF16), vh, NN))
            acc_ref[...] = acc_ref[...] * jnp.where(masks[0], alpha[0], alpha[1]) + jnp.where(masks[0], pv[0], pv[1])
            aux_ref[...] = aux_ref[...] * jnp.where(masks[0], alpha[1], alpha[0]) + jnp.where(masks[0], pv[1], pv[0])

        @pl.when(ki < qi)
        def _():
            step(False)

        @pl.when(ki == qi)
        def _():
            step(True)
            aux = aux_ref[...]
            l0, l1 = aux[:, 72:73], aux[:, 8:9]
            o_ref[...] = (acc_ref[...] * jnp.where(masks[0], 1.0 / l0, 1.0 / l1)).astype(BF16)
            lse_ref[...] = jnp.where(lane == 0, m_ref[0] + jnp.log(l0), m_ref[1] + jnp.log(l1))

    grid_spec = pltpu.PrefetchScalarGridSpec(
        num_scalar_prefetch=2, grid=(HEAD_PAIRS, int(qi_arr.shape[0])),
        in_specs=[pl.BlockSpec((TQ, 128), lambda p, t, qi, ki: (qi[t], Q_BLK + p)),
                  pl.BlockSpec((TQ, 128), lambda p, t, qi, ki: (ki[t], K_BLK + p)),
                  pl.BlockSpec((TQ, 128), lambda p, t, qi, ki: (ki[t], V_BLK + p)),
                  pl.BlockSpec((None, TQ, 128), lambda p, t, qi, ki: (p, qi[t], 0)),
                  pl.BlockSpec((None, TQ, 128), lambda p, t, qi, ki: (p, ki[t], 0))],
        out_specs=[pl.BlockSpec((TQ, 128), lambda p, t, qi, ki: (qi[t], Q_BLK + p)),
                   pl.BlockSpec((None, TQ, 128), lambda p, t, qi, ki: (p, qi[t], 0))],
        scratch_shapes=[pltpu.VMEM((2, TQ, 1), F32), pltpu.VMEM((TQ, 128), F32), pltpu.VMEM((TQ, 128), F32)])
    return pl.pallas_call(
        body, name=name, grid_spec=grid_spec,
        out_shape=[jax.ShapeDtypeStruct((s, D_MODEL), BF16), jax.ShapeDtypeStruct((HEAD_PAIRS, s, 128), F32)],
        compiler_params=_cp("parallel", "arbitrary"),
    )(qi_arr, ki_arr, proj, proj, proj, aq, ak)


def _fox_bwd2(proj, dycat, aqb, ak, ad, name):
    s = proj.shape[0]
    nq = s // TQ
    qi_arr, ki_arr = _causal_pairs(nq, key_major=True)

    def body(qi_ref, ki_ref, q_ref, k_ref, v_ref, do_ref, aq_ref, ak_ref, ad_ref,
             dq_ref, dk_ref, dv_ref, qaux_ref, kaux_ref, dq_acc, qaux_acc, dk_acc, dv_acc, kaux_acc):
        t = pl.program_id(1)
        qi, ki = qi_ref[t], ki_ref[t]
        lane = lax.broadcasted_iota(jnp.int32, (1, 128), 1)
        masks = [lane < 64, lane >= 64]
        ones_v = jnp.where((lane & 63) < 3, 1.0, 0.0).astype(BF16)

        @pl.when(qi == ki)
        def _():
            dk_acc[...] = jnp.zeros_like(dk_acc)
            dv_acc[...] = jnp.zeros_like(dv_acc)
            kaux_acc[...] = jnp.zeros_like(kaux_acc)

        def step(diag):
            q2s = q_ref[...] * 0.125
            k2, v2, do2 = k_ref[...], v_ref[...], do_ref[...]
            aq2, ak2, ad2 = aq_ref[...], ak_ref[...], ad_ref[...]
            dq, dk, dv = [], [], []
            for hh in range(2):
                qh = jnp.where(masks[hh], q2s, aq2)
                kh = jnp.where(masks[hh], k2, ak2)
                doh = jnp.where(masks[hh], do2, ad2)
                vh = jnp.where(masks[hh], v2, ones_v)
                sc = _dot(qh, kh, NT)
                if diag:
                    sc = _diag_mask(sc)
                p = jnp.exp(sc)
                dsb = (p * _dot(doh, vh, NT)).astype(BF16)
                dv.append(_dot(p.astype(BF16), doh, TN))
                dk.append(_dot(dsb, qh, TN))
                dq.append(_dot(dsb, kh, NN))
            dk_acc[...] += jnp.where(masks[0], dk[0], dk[1])
            kaux_acc[...] += jnp.where(masks[0], dk[1], dk[0])
            dv_acc[...] += jnp.where(masks[0], dv[0], dv[1])
            dq_new = jnp.where(masks[0], dq[0], dq[1])
            qaux_new = jnp.where(masks[0], dq[1], dq[0])

            @pl.when(ki == 0)
            def _():
                dq_acc[qi] = dq_new
                qaux_acc[qi] = qaux_new

            @pl.when(ki > 0)
            def _():
                dq_acc[qi] += dq_new
                qaux_acc[qi] += qaux_new

        @pl.when(qi > ki)
        def _():
            step(False)

        @pl.when(qi == ki)
        def _():
            step(True)
            rows = pl.ds(pl.multiple_of(qi * TQ, TQ), TQ)
            dq_ref[rows, :] = (dq_acc[qi] * 0.125).astype(BF16)
            qaux_ref[rows, :] = qaux_acc[qi]

        @pl.when(qi == nq - 1)
        def _():
            dk_ref[...] = dk_acc[...].astype(BF16)
            dv_ref[...] = dv_acc[...].astype(BF16)
            kaux_ref[...] = kaux_acc[...]

    grid_spec = pltpu.PrefetchScalarGridSpec(
        num_scalar_prefetch=2, grid=(HEAD_PAIRS, int(qi_arr.shape[0])),
        in_specs=[pl.BlockSpec((TQ, 128), lambda p, t, qi, ki: (qi[t], Q_BLK + p)),
                  pl.BlockSpec((TQ, 128), lambda p, t, qi, ki: (ki[t], K_BLK + p)),
                  pl.BlockSpec((TQ, 128), lambda p, t, qi, ki: (ki[t], V_BLK + p)),
                  pl.BlockSpec((TQ, 128), lambda p, t, qi, ki: (qi[t], Q_BLK + p)),
                  pl.BlockSpec((None, TQ, 128), lambda p, t, qi, ki: (p, qi[t], 0)),
                  pl.BlockSpec((None, TQ, 128), lambda p, t, qi, ki: (p, ki[t], 0)),
                  pl.BlockSpec((None, TQ, 128), lambda p, t, qi, ki: (p, qi[t], 0))],
        out_specs=[pl.BlockSpec((s, 128), lambda p, t, qi, ki: (0, p)),
                   pl.BlockSpec((TQ, 128), lambda p, t, qi, ki: (ki[t], p)),
                   pl.BlockSpec((TQ, 128), lambda p, t, qi, ki: (ki[t], p)),
                   pl.BlockSpec((None, s, 128), lambda p, t, qi, ki: (p, 0, 0)),
                   pl.BlockSpec((None, TQ, 128), lambda p, t, qi, ki: (p, ki[t], 0))],
        scratch_shapes=[pltpu.VMEM((nq, TQ, 128), F32), pltpu.VMEM((nq, TQ, 128), F32),
                        pltpu.VMEM((TQ, 128), F32), pltpu.VMEM((TQ, 128), F32), pltpu.VMEM((TQ, 128), F32)])
    return pl.pallas_call(
        body, name=name, grid_spec=grid_spec,
        out_shape=[jax.ShapeDtypeStruct((s, D_FOX), BF16)] * 3 + [jax.ShapeDtypeStruct((HEAD_PAIRS, s, 128), F32)] * 2,
        compiler_params=_cp("arbitrary", "arbitrary"),
    )(qi_arr, ki_arr, proj, proj, proj, dycat, aqb, ak, ad)


XA_SCALE = XA_DIM ** -0.5


def _xattn_fwd(q2, kv, name):
    s = q2.shape[0]
    m = kv.shape[0]

    def body(q_ref, kv_ref, o_ref):
        for h in range(XA_HEADS):
            c0 = h * XA_DIM
            sc = _dot(q_ref[:, c0:c0 + XA_DIM], kv_ref[:, c0:c0 + XA_DIM], NT) * XA_SCALE
            e = jnp.exp(sc - jnp.max(sc, axis=1, keepdims=True))
            p = e / jnp.sum(e, axis=1, keepdims=True)
            o_ref[:, c0:c0 + XA_DIM] = _dot(p.astype(BF16), kv_ref[:, D_MODEL + c0:D_MODEL + c0 + XA_DIM], NN).astype(BF16)

    return pl.pallas_call(
        body, name=name, grid=(s // TM,),
        in_specs=[_row_spec(TM, D_MODEL), pl.BlockSpec((m, 2 * D_MODEL), lambda i: (0, 0))],
        out_specs=_row_spec(TM, D_MODEL), out_shape=jax.ShapeDtypeStruct((s, D_MODEL), BF16),
        compiler_params=_cp("parallel"),
    )(q2, kv)


def _xattn_bwd(q2, kv, do2, name):
    s = q2.shape[0]
    m = kv.shape[0]

    def body(q_ref, kv_ref, do_ref, dq_ref, dkv_ref):
        i = pl.program_id(0)

        @pl.when(i == 0)
        def _():
            dkv_ref[...] = jnp.zeros_like(dkv_ref)

        for h in range(XA_HEADS):
            c0 = h * XA_DIM
            v0 = D_MODEL + c0
            qh = q_ref[:, c0:c0 + XA_DIM]
            kh = kv_ref[:, c0:c0 + XA_DIM]
            doh = do_ref[:, c0:c0 + XA_DIM]
            sc = _dot(qh, kh, NT) * XA_SCALE
            e = jnp.exp(sc - jnp.max(sc, axis=1, keepdims=True))
            p = e / jnp.sum(e, axis=1, keepdims=True)
            dp = _dot(doh, kv_ref[:, v0:v0 + XA_DIM], NT)
            ds = p * (dp - jnp.sum(p * dp, axis=1, keepdims=True))
            dsb = (ds * XA_SCALE).astype(BF16)
            dq_ref[:, c0:c0 + XA_DIM] = _dot(dsb, kh, NN).astype(BF16)
            dkv_ref[:, c0:c0 + XA_DIM] += _dot(dsb, qh, TN)
            dkv_ref[:, v0:v0 + XA_DIM] += _dot(p.astype(BF16), doh, TN)

    return pl.pallas_call(
        body, name=name, grid=(s // TM,),
        in_specs=[_row_spec(TM, D_MODEL), pl.BlockSpec((m, 2 * D_MODEL), lambda i: (0, 0)), _row_spec(TM, D_MODEL)],
        out_specs=[_row_spec(TM, D_MODEL), pl.BlockSpec((m, 2 * D_MODEL), lambda i: (0, 0))],
        out_shape=[jax.ShapeDtypeStruct((s, D_MODEL), BF16), jax.ShapeDtypeStruct((m, 2 * D_MODEL), F32)],
        compiler_params=_cp("arbitrary"),
    )(q2, kv, do2)


GELU_C = math.sqrt(2.0 / math.pi)
GELU_A = 0.044715


def _gelu(x):
    return 0.5 * x * (1.0 + jnp.tanh(GELU_C * (x + GELU_A * x * x * x)))


def _gelu_and_grad(x):
    t = jnp.tanh(GELU_C * (x + GELU_A * x * x * x))
    g = 0.5 * x * (1.0 + t)
    dg = 0.5 * (1.0 + t) + 0.5 * x * (1.0 - t * t) * GELU_C * (1.0 + 3.0 * GELU_A * x * x)
    return g, dg


def _shift_down(main, prev8):
    row = lax.broadcasted_iota(jnp.int32, main.shape, 0)
    s1 = jnp.where(row == 0, prev8[7:8, :], pltpu.roll(main, 1, 0))
    s2 = jnp.where(row == 0, prev8[6:7, :], jnp.where(row == 1, prev8[7:8, :], pltpu.roll(main, 2, 0)))
    return s1, s2


def _shift_up(main, next8):
    n = main.shape[0]
    row = lax.broadcasted_iota(jnp.int32, main.shape, 0)
    u1 = jnp.where(row == n - 1, next8[0:1, :], pltpu.roll(main, n - 1, 0))
    u2 = jnp.where(row == n - 2, next8[0:1, :], jnp.where(row == n - 1, next8[1:2, :], pltpu.roll(main, n - 2, 0)))
    return u1, u2


def _conv(h, s1, s2, w_ref, b_ref):
    return w_ref[0:1, :] * s2 + w_ref[1:2, :] * s1 + w_ref[2:3, :] * h + b_ref[...]


def _ffn_fwd(h3, w_up, cw, cb, w_down, name):
    s = h3.shape[0]
    tn = TN_FF
    nj = D_FF // tn
    per = D_MODEL // tn
    hb = TM // 8

    def body(h_ref, halo_ref, wg_ref, wu_ref, cwg_ref, cwu_ref, cbg_ref, cbu_ref, wd_ref,
             hg_ref, hu_ref, a_ref, y_ref):
        i, j = pl.program_id(0), pl.program_id(1)
        h = h_ref[...]
        halo = halo_ref[...]
        halo = jnp.where(i > 0, halo, jnp.zeros_like(halo))
        conv = []
        for w_ref, cw_ref, cb_ref, hid_ref in ((wg_ref, cwg_ref, cbg_ref, hg_ref), (wu_ref, cwu_ref, cbu_ref, hu_ref)):
            hm_b = _dot(h, w_ref[...], NN).astype(BF16)
            hid_ref[...] = hm_b
            hm = hm_b.astype(F32)
            hl = _dot(halo, w_ref[...], NN).astype(BF16).astype(F32)
            s1, s2 = _shift_down(hm, hl)
            conv.append(_conv(hm, s1, s2, cw_ref, cb_ref))
        a = (_gelu(conv[0]) * conv[1]).astype(BF16)
        a_ref[...] = a
        contrib = _dot(a, wd_ref[...], NN)

        @pl.when(j == 0)
        def _():
            y_ref[...] = contrib

        @pl.when(j > 0)
        def _():
            y_ref[...] += contrib

    return pl.pallas_call(
        body, name=name, grid=(s // TM, nj),
        in_specs=[pl.BlockSpec((TM, D_MODEL), lambda i, j: (i, 0)),
                  pl.BlockSpec((8, D_MODEL), lambda i, j: (jnp.maximum(i * hb - 1, 0), 0)),
                  pl.BlockSpec((None, D_MODEL, tn), lambda i, j: (j // per, 0, j % per)),
                  pl.BlockSpec((None, D_MODEL, tn), lambda i, j: (NDEV // 2 + j // per, 0, j % per)),
                  pl.BlockSpec((8, tn), lambda i, j: (0, j)),
                  pl.BlockSpec((8, tn), lambda i, j: (0, nj + j)),
                  pl.BlockSpec((1, tn), lambda i, j: (0, j)),
                  pl.BlockSpec((1, tn), lambda i, j: (0, nj + j)),
                  pl.BlockSpec((tn, D_MODEL), lambda i, j: (j, 0))],
        out_specs=[pl.BlockSpec((TM, tn), lambda i, j: (i, j)), pl.BlockSpec((TM, tn), lambda i, j: (i, j)),
                   pl.BlockSpec((TM, tn), lambda i, j: (i, j)), pl.BlockSpec((TM, D_MODEL), lambda i, j: (i, 0))],
        out_shape=[jax.ShapeDtypeStruct((s, D_FF), BF16), jax.ShapeDtypeStruct((s, D_FF), BF16),
                   jax.ShapeDtypeStruct((s, D_FF), BF16), jax.ShapeDtypeStruct((s, D_MODEL), F32)],
        compiler_params=_cp("parallel", "arbitrary"),
    )(h3, h3, w_up, w_up, cw, cw, cb, cb, w_down)


def _ffn_bwd(dy3, w_down, hid_g, hid_u, cw, cb, name):
    s = dy3.shape[0]
    n = s // TM
    tn = TN_FF
    nj = D_FF // tn
    hb = TM // 8
    last8 = s // 8 - 1

    def body(dy_ref, dyp_ref, wd_ref, hg_ref, hgl_ref, hgn_ref, hu_ref, hul_ref, hun_ref,
             cwg_ref, cwu_ref, cbg_ref, cbu_ref,
             dhg_ref, dhu_ref, dcwg_ref, dcwu_ref, dcbg_ref, dcbu_ref):
        i = pl.program_id(1)
        first, last = i == 0, i == n - 1
        da = _dot(dy_ref[...], wd_ref[...], NT)
        dyp = dyp_ref[...]
        dyp = jnp.where(last, jnp.zeros_like(dyp), dyp)
        da_n = _dot(dyp, wd_ref[...], NT)
        parts = []
        for h_ref, hl_ref, hn_ref, cw_ref, cb_ref in ((hg_ref, hgl_ref, hgn_ref, cwg_ref, cbg_ref),
                                                     (hu_ref, hul_ref, hun_ref, cwu_ref, cbu_ref)):
            hm = h_ref[...].astype(F32)
            hl = jnp.where(first, 0.0, hl_ref[...].astype(F32))
            hn = hn_ref[...].astype(F32)
            s1, s2 = _shift_down(hm, hl)
            c = _conv(hm, s1, s2, cw_ref, cb_ref)
            n1, n2 = _shift_down(hn, hm[TM - 8:, :])
            cn = _conv(hn, n1, n2, cw_ref, cb_ref)
            parts.append((hm, s1, s2, c, cn))
        g, dg = _gelu_and_grad(parts[0][3])
        gn, dgn = _gelu_and_grad(parts[0][4])
        dc_g = da * parts[1][3] * dg
        dc_u = da * g
        dcn_g = da_n * parts[1][4] * dgn
        dcn_u = da_n * gn
        outs = ((dc_g, dcn_g, parts[0], cwg_ref, dhg_ref, dcwg_ref, dcbg_ref),
                (dc_u, dcn_u, parts[1], cwu_ref, dhu_ref, dcwu_ref, dcbu_ref))
        for dc, dcn, (hm, s1, s2, _, _), cw_ref, dh_ref, dcw_ref, dcb_ref in outs:
            u1, u2 = _shift_up(dc, dcn)
            dh_ref[...] = (cw_ref[2:3, :] * dc + cw_ref[1:2, :] * u1 + cw_ref[0:1, :] * u2).astype(BF16)
            dcb = jnp.sum(dc, axis=0, keepdims=True)
            row8 = lax.broadcasted_iota(jnp.int32, (8, tn), 0)
            dcw = jnp.where(row8 == 0, jnp.sum(dc * s2, axis=0, keepdims=True),
                            jnp.where(row8 == 1, jnp.sum(dc * s1, axis=0, keepdims=True),
                                      jnp.where(row8 == 2, jnp.sum(dc * hm, axis=0, keepdims=True), 0.0)))

            @pl.when(first)
            def _():
                dcw_ref[...] = dcw
                dcb_ref[...] = dcb

            @pl.when(i > 0)
            def _():
                dcw_ref[...] += dcw
                dcb_ref[...] += dcb

    prev8 = lambda j, i: (jnp.maximum(i * hb - 1, 0), j)
    next8 = lambda j, i: (jnp.minimum((i + 1) * hb, last8), j)
    blk = lambda j, i: (i, j)
    col = lambda j, i: (0, j)
    colu = lambda j, i: (0, nj + j)
    return pl.pallas_call(
        body, name=name, grid=(nj, n),
        in_specs=[pl.BlockSpec((TM, D_MODEL), lambda j, i: (i, 0)),
                  pl.BlockSpec((8, D_MODEL), lambda j, i: (jnp.minimum((i + 1) * hb, last8), 0)),
                  pl.BlockSpec((tn, D_MODEL), lambda j, i: (j, 0)),
                  pl.BlockSpec((TM, tn), blk), pl.BlockSpec((8, tn), prev8), pl.BlockSpec((8, tn), next8),
                  pl.BlockSpec((TM, tn), blk), pl.BlockSpec((8, tn), prev8), pl.BlockSpec((8, tn), next8),
                  pl.BlockSpec((8, tn), col), pl.BlockSpec((8, tn), colu),
                  pl.BlockSpec((1, tn), col), pl.BlockSpec((1, tn), colu)],
        out_specs=[pl.BlockSpec((TM, tn), blk), pl.BlockSpec((TM, tn), blk),
                   pl.BlockSpec((8, tn), col), pl.BlockSpec((8, tn), col),
                   pl.BlockSpec((1, tn), col), pl.BlockSpec((1, tn), col)],
        out_shape=[jax.ShapeDtypeStruct((s, D_FF), BF16), jax.ShapeDtypeStruct((s, D_FF), BF16),
                   jax.ShapeDtypeStruct((8, D_FF), F32), jax.ShapeDtypeStruct((8, D_FF), F32),
                   jax.ShapeDtypeStruct((1, D_FF), F32), jax.ShapeDtypeStruct((1, D_FF), F32)],
        compiler_params=_cp("parallel", "arbitrary"),
    )(dy3, dy3, w_down, hid_g, hid_g, hid_g, hid_u, hid_u, hid_u, cw, cw, cb, cb)


def _slot(p):
    return 4 * p[0] + 2 * p[1] + p[2]


def _all_gather(shards, name):
    n = len(shards)

    def body(*refs):
        ins, outs = refs[:n], refs[n:2 * n]
        send_sems, recv_sems, local_sems = refs[2 * n:]
        x, y, c = lax.axis_index("x"), lax.axis_index("y"), lax.axis_index("c")
        me, sibling = (x, y, c), (x, y, 1 - c)
        chips = [(1 - x, y), (x, 1 - y), (1 - x, 1 - y)]

        def copy(a, k, block, to, from_input=False):
            dst = outs[a].at[_slot(block)]
            return pltpu.make_async_remote_copy(
                src_ref=ins[a] if from_input else dst, dst_ref=dst,
                send_sem=send_sems.at[a, k], recv_sem=recv_sems.at[a, k],
                device_id=to, device_id_type=MESH)

        mine = [pltpu.make_async_copy(ins[a], outs[a].at[_slot(me)], local_sems.at[a]) for a in range(n)]
        for cp in mine:
            cp.start()
        first = []
        for a in range(n):
            first.append(copy(a, 0, me, sibling, True))
            first += [copy(a, 1 + j, me, (*chip, c), True) for j, chip in enumerate(chips)]
        for cp in first:
            cp.start()
        passed = []
        for j, chip in enumerate(chips):
            for a in range(n):
                copy(a, 1 + j, (*chip, c), me).wait_recv()
                fwd = copy(a, 4 + j, (*chip, c), sibling)
                fwd.start()
                passed.append(fwd)
        for a in range(n):
            copy(a, 0, sibling, me).wait_recv()
            for j, chip in enumerate(chips):
                copy(a, 4 + j, (*chip, 1 - c), me).wait_recv()
        for cp in first + passed:
            cp.wait_send()
        for cp in mine:
            cp.wait()

    any_spec = pl.BlockSpec(memory_space=pl.ANY)
    return pl.pallas_call(
        body, name=name,
        in_specs=[any_spec] * n, out_specs=[any_spec] * n,
        out_shape=[jax.ShapeDtypeStruct((NDEV,) + s.shape, s.dtype) for s in shards],
        scratch_shapes=[pltpu.SemaphoreType.DMA((n, 7)), pltpu.SemaphoreType.DMA((n, 7)),
                        pltpu.SemaphoreType.DMA((n,))],
    )(*shards)


def _scatter_blocks(full, name):
    n = len(full)

    def body(*refs):
        ins, outs = refs[:n], refs[n:2 * n]
        send_sems, recv_sems, local_sems = refs[2 * n:]
        x, y, c = lax.axis_index("x"), lax.axis_index("y"), lax.axis_index("c")
        me = (x, y, c)
        copies = []
        for a in range(n):
            cp = pltpu.make_async_copy(ins[a].at[_slot(me)], outs[a].at[_slot(me)], local_sems.at[a])
            cp.start()
            copies.append(cp)
        remote = []
        for mask in range(1, NDEV):
            peer = (1 - x if mask & 4 else x, 1 - y if mask & 2 else y, 1 - c if mask & 1 else c)
            for a in range(n):
                cp = pltpu.make_async_remote_copy(
                    src_ref=ins[a].at[_slot(peer)], dst_ref=outs[a].at[_slot(me)],
                    send_sem=send_sems.at[a, mask - 1], recv_sem=recv_sems.at[a, mask - 1],
                    device_id=peer, device_id_type=MESH)
                cp.start()
                remote.append(cp)
        for cp in remote:
            cp.wait()
        for cp in copies:
            cp.wait()

    any_spec = pl.BlockSpec(memory_space=pl.ANY)
    return pl.pallas_call(
        body, name=name,
        in_specs=[any_spec] * n, out_specs=[any_spec] * n,
        out_shape=[jax.ShapeDtypeStruct(f.shape, f.dtype) for f in full],
        scratch_shapes=[pltpu.SemaphoreType.DMA((n, 7)), pltpu.SemaphoreType.DMA((n, 7)),
                        pltpu.SemaphoreType.DMA((n,))],
    )(*full)


def _adamw(parts, w, m, v, name):
    r, c = w.shape
    tr = r if r * c <= 160 * 1024 else max(8, (160 * 1024 // c) // 8 * 8)
    while r % tr:
        tr -= 8
    bc1 = 1.0 - ADAM_B1 ** ADAM_STEP
    bc2 = 1.0 - ADAM_B2 ** ADAM_STEP

    def body(p_ref, w_ref, m_ref, v_ref, g_ref, d_ref, mo_ref, vo_ref):
        g = p_ref[0].astype(F32)
        for d in range(1, NDEV):
            g = g + p_ref[d].astype(F32)
        g_ref[...] = g
        mn = ADAM_B1 * m_ref[...] + (1.0 - ADAM_B1) * g
        vn = ADAM_B2 * v_ref[...] + (1.0 - ADAM_B2) * (g * g)
        mo_ref[...] = mn
        vo_ref[...] = vn
        d_ref[...] = -ADAM_LR * ((mn / bc1) / (jnp.sqrt(vn / bc2) + ADAM_EPS) + ADAM_WD * w_ref[...])

    spec = pl.BlockSpec((tr, c), lambda i: (i, 0))
    return pl.pallas_call(
        body, name=name, grid=(r // tr,),
        in_specs=[pl.BlockSpec((NDEV, tr, c), lambda i: (0, i, 0)), spec, spec, spec],
        out_specs=[spec] * 4, out_shape=[jax.ShapeDtypeStruct((r, c), F32)] * 4,
        compiler_params=_cp("parallel"),
    )(parts, w, m, v)


def _pair_cols(a):
    s = a.shape[0]
    t = a[:, :FOX_HEADS].reshape(s, HEAD_PAIRS, 2).transpose(1, 0, 2)
    return jnp.pad(t, ((0, 0), (0, 0), (0, 126)))


def _pair_rows(at):
    s = at.shape[1]
    return jnp.pad(at[:FOX_HEADS].reshape(HEAD_PAIRS, 2, s), ((0, 0), (0, 6), (0, 0)))


def _local_step(x, mem, tgt, gains, b_forget, w_pool, pool_scale, conv_w, conv_b, wts):
    s = x.shape[0]
    w_in, w_mix, w_xq, w_xo = wts["w_in"], wts["w_mix"], wts["w_xq"], wts["w_xo"]
    w_xkv, w_up, w_down = wts["w_xkv"], wts["w_up"], wts["w_down"]

    w_f = w_in[:, F_COL:]
    b_pad = jnp.pad(b_forget, ((0, 0), (0, 128 - FOX_HEADS)))
    wbd = jnp.zeros((D_POOL, D_POOL), F32)
    for g in range(4):
        wbd = wbd.at[64 * g:64 * g + 64, 64 * g:64 * g + 64].set(w_pool[g])
    wbd = wbd.astype(BF16)
    scale = pool_scale.reshape(1, D_POOL)
    cw = jnp.pad(conv_w, ((0, 5), (0, 0)))

    h1 = _norm_fwd(x, gains["mix_pre"], "norm_mix_pre")
    proj = _mm(h1, w_in, "nn", BF16, 1024, 384, 1024, "proj_in")
    fraw = _mm(h1, w_f, "nn", F32, 1024, 128, 1024, "proj_gate")
    flog, cum = _gate_cumsum(fraw, b_pad, "gate_cumsum")
    cum12 = cum[:, :FOX_HEADS]
    ones3 = [jnp.ones((s, FOX_HEADS), BF16)] * 3
    cum_p = _pieces(cum12)
    aq, ak = _aug(cum_p, ones3), _aug(ones3, [-c for c in cum_p])
    ycat, lse = _fox_fwd2(proj, aq, ak, "fox_fwd")
    ycat = _pool_fwd(proj, wbd, scale, ycat, "pool_fwd")
    y1 = _mm(ycat, w_mix, "nn", F32, 1024, 1024, 1024, "mix_out")
    x1, h2 = _resid_norm_fwd(x, y1, gains["mix_post"], gains["xa_pre"], "resid_mix")
    q2 = _mm(h2, w_xq, "nn", BF16, 1024, 1024, 1024, "xa_q")
    mem_n = _norm_fwd(mem, gains["mem"], "norm_mem")
    kv = _mm(mem_n, w_xkv, "nn", BF16, mem.shape[0], 256, 1024, "xa_kv", b_cols=256)
    o2 = _xattn_fwd(q2, kv, "xattn_fwd")
    y2 = _mm(o2, w_xo, "nn", F32, 1024, 1024, 1024, "xa_out")
    x2, h3 = _resid_norm_fwd(x1, y2, gains["xa_post"], gains["ffn_pre"], "resid_xa")
    hid_g, hid_u, act, y3 = _ffn_fwd(h3, w_up, cw, conv_b, w_down, "ffn_fwd")

    loss, dx3, dy3, dg_ffn_post = _loss_bwd(x2, y3, gains["ffn_post"], tgt, "loss_bwd")
    dhid_g, dhid_u, dcw_g, dcw_u, dcb_g, dcb_u = _ffn_bwd(dy3, w_down, hid_g, hid_u, cw, conv_b, "ffn_bwd")
    d_w_down = _mm(act, dy3, "tn", BF16, 1024, 1024, 1024, "dw_down")
    d_w_up = _mm(h3, [dhid_g, dhid_u], "tn", BF16, 1024, 1024, 1024, "dw_up", out_cols=1024)
    dh3 = _mm([dhid_g, dhid_u], w_up, "nt", F32, 1024, 1024, 1024, "dh_ffn", b_cols=1024)
    dx2, dg_ffn_pre, dy2, dg_xa_post = _norm_bwd(dh3, x2, dx3, gains["ffn_pre"], "norm_bwd_ffn",
                                                 prev=(y2, gains["xa_post"]))
    do2 = _mm(dy2, w_xo, "nt", BF16, 1024, 1024, 1024, "d_xa_out")
    d_w_xo = _mm(o2, dy2, "tn", BF16, 1024, 1024, 1024, "dw_xo")
    dq2, dkv = _xattn_bwd(q2, kv, do2, "xattn_bwd")
    dkv = dkv.astype(BF16)
    dh2 = _mm(dq2, w_xq, "nt", F32, 1024, 1024, 1024, "dh_xa")
    d_w_xq = _mm(h2, dq2, "tn", BF16, 1024, 1024, 1024, "dw_xq")
    dmem_n = _mm(dkv, w_xkv, "nt", F32, mem.shape[0], 1024, 256, "d_mem", b_cols=256)
    d_w_xkv = _mm(mem_n, dkv, "tn", BF16, 1024, 256, mem.shape[0], "dw_xkv", out_cols=256)
    _, dg_mem = _norm_bwd(dmem_n, mem, jnp.zeros_like(mem), gains["mem"], "norm_bwd_mem")
    dx1, dg_xa_pre, dy1, dg_mix_post = _norm_bwd(dh2, x1, dx2, gains["xa_pre"], "norm_bwd_xa",
                                                 prev=(y1, gains["mix_post"]))
    dycat = _mm(dy1, w_mix, "nt", BF16, 1024, 1024, 1024, "d_mix_out")
    d_w_mix = _mm(ycat, dy1, "tn", BF16, 1024, 1024, 1024, "dw_mix")
    aqb = _aug(_pieces(cum12 - _pair_to_heads(lse, 0, 1)), ones3)
    d_rows = (dycat[:, D_POOL:].astype(F32) * ycat[:, D_POOL:].astype(F32)).reshape(s, FOX_HEADS, 64).sum(-1)
    ad = _aug([-c for c in _pieces(d_rows)], [jnp.zeros((s, FOX_HEADS), BF16)] * 3)
    dq, dk, dv, qaux, kaux = _fox_bwd2(proj, dycat, aqb, ak, ad, "fox_bwd")
    du, d_wbd, d_scale = _pool_bwd(proj, dycat, wbd, scale, "pool_bwd")
    dcum = _pair_to_heads(qaux, 64, 0) - _pair_to_heads(kaux, 67, 3)
    dcum = jnp.pad(dcum, ((0, 0), (0, 128 - FOX_HEADS)))
    df, db_f = _gate_bwd(dcum, flog, "gate_bwd")
    dproj = jnp.concatenate([du, dq, dk, dv, df], axis=1)
    dh1 = _mm(dproj, w_in, "nt", F32, 1024, 1024, 896, "dh_mix")
    d_w_in = _mm(h1, dproj, "tn", BF16, 1024, 384, 1024, "dw_in")
    grad_x, dg_mix_pre = _norm_bwd(dh1, x, dx1, gains["mix_pre"], "norm_bwd_mix")

    big = dict(w_in=d_w_in, w_mix=d_w_mix, w_xq=d_w_xq, w_xo=d_w_xo, w_xkv=d_w_xkv, w_up=d_w_up, w_down=d_w_down)
    small = dict(
        mix_pre=dg_mix_pre, mix_post=dg_mix_post, mem=dg_mem, xa_pre=dg_xa_pre, xa_post=dg_xa_post,
        ffn_pre=dg_ffn_pre, ffn_post=dg_ffn_post,
        conv_b=jnp.concatenate([dcb_g, dcb_u], axis=1),
        w_pool=jnp.stack([d_wbd[64 * g:64 * g + 64, 64 * g:64 * g + 64] for g in range(4)]),
        pool_scale=d_scale.reshape(4, 64),
        b_forget=db_f[:, :FOX_HEADS],
        conv_w=jnp.concatenate([dcw_g[:3], dcw_u[:3]], axis=1),
    )
    return loss, grad_x, big, small


SMALL_ORDER = ("mix_pre", "mix_post", "mem", "xa_pre", "xa_post", "ffn_pre", "ffn_post", "conv_b",
               "w_pool", "pool_scale", "b_forget")
SMALL_ROWS = 256


def _pack_small(d):
    flat = jnp.concatenate([d[k].reshape(-1).astype(F32) for k in SMALL_ORDER])
    return jnp.pad(flat, (0, SMALL_ROWS * 128 - flat.shape[0])).reshape(SMALL_ROWS, 128)


def _unpack_small(a, like):
    flat = a.reshape(-1)
    out, off = {}, 0
    for k in SMALL_ORDER:
        n = like[k].size
        out[k] = flat[off:off + n].reshape(like[k].shape)
        off += n
    return out


def kernel(x, mem, norm_mix_pre, norm_mix_post, w_in, b_forget, w_pool, pool_scale, w_mix_out, norm_mem, norm_xa_pre, norm_xa_post, w_xq, w_xkv, w_xo, norm_ffn_pre, norm_ffn_post, w_up, conv_w, conv_b, w_down, loss_target, m_norm_mix_pre, m_norm_mix_post, m_w_in, m_b_forget, m_w_pool, m_pool_scale, m_w_mix_out, m_norm_mem, m_norm_xa_pre, m_norm_xa_post, m_w_xq, m_w_xkv, m_w_xo, m_norm_ffn_pre, m_norm_ffn_post, m_w_up, m_conv_w, m_conv_b, m_w_down, v_norm_mix_pre, v_norm_mix_post, v_w_in, v_b_forget, v_w_pool, v_pool_scale, v_w_mix_out, v_norm_mem, v_norm_xa_pre, v_norm_xa_post, v_w_xq, v_w_xkv, v_w_xo, v_norm_ffn_pre, v_norm_ffn_post, v_w_up, v_conv_w, v_conv_b, v_w_down):
    names = ("norm_mix_pre", "norm_mix_post", "w_in", "b_forget", "w_pool", "pool_scale", "w_mix_out", "norm_mem",
             "norm_xa_pre", "norm_xa_post", "w_xq", "w_xkv", "w_xo", "norm_ffn_pre", "norm_ffn_post", "w_up",
             "conv_w", "conv_b", "w_down")
    w = dict(zip(names, (norm_mix_pre, norm_mix_post, w_in, b_forget, w_pool, pool_scale, w_mix_out, norm_mem,
                         norm_xa_pre, norm_xa_post, w_xq, w_xkv, w_xo, norm_ffn_pre, norm_ffn_post, w_up,
                         conv_w, conv_b, w_down)))
    mo = dict(zip(names, (m_norm_mix_pre, m_norm_mix_post, m_w_in, m_b_forget, m_w_pool, m_pool_scale, m_w_mix_out,
                          m_norm_mem, m_norm_xa_pre, m_norm_xa_post, m_w_xq, m_w_xkv, m_w_xo, m_norm_ffn_pre,
                          m_norm_ffn_post, m_w_up, m_conv_w, m_conv_b, m_w_down)))
    vo = dict(zip(names, (v_norm_mix_pre, v_norm_mix_post, v_w_in, v_b_forget, v_w_pool, v_pool_scale, v_w_mix_out,
                          v_norm_mem, v_norm_xa_pre, v_norm_xa_post, v_w_xq, v_w_xkv, v_w_xo, v_norm_ffn_pre,
                          v_norm_ffn_post, v_w_up, v_conv_w, v_conv_b, v_w_down)))

    big_names = ("w_in", "w_mix_out", "w_xq", "w_xo", "w_xkv", "w_up", "w_down")
    shards = []
    for k in big_names:
        sh = w[k][0].astype(BF16)
        if k == "w_in":
            sh = jnp.pad(sh, ((0, 0), (0, D_IN_PAD - sh.shape[1])))
        shards.append(sh)
    conv_w_sh = jnp.pad(conv_w[0, :, 0, :], ((0, 5), (0, 0)))
    gathered = _all_gather(shards + [conv_w_sh], "gather_weights")
    g_in, g_mix, g_xq, g_xo, g_xkv, g_up, g_down, g_cw = gathered
    wts = dict(w_in=g_in.reshape(D_MODEL, D_IN_PAD), w_mix=g_mix.reshape(D_MODEL, D_MODEL),
               w_xq=g_xq.reshape(D_MODEL, D_MODEL), w_xo=g_xo.reshape(D_MODEL, D_MODEL),
               w_xkv=g_xkv, w_up=g_up, w_down=g_down.reshape(D_FF, D_MODEL))
    conv_w_full = g_cw[:, :3, :].transpose(1, 0, 2).reshape(3, 2 * D_FF)

    gains = dict(mix_pre=norm_mix_pre, mix_post=norm_mix_post, mem=norm_mem, xa_pre=norm_xa_pre,
                 xa_post=norm_xa_post, ffn_pre=norm_ffn_pre, ffn_post=norm_ffn_post)
    loss, grad_x, big, small = _local_step(x[0], mem[0], loss_target[0], gains, b_forget, w_pool[0], pool_scale[0],
                                           conv_w_full, conv_b, wts)

    d_cw = jnp.pad(small.pop("conv_w"), ((0, 5), (0, 0))).reshape(8, NDEV, D_MODEL).transpose(1, 0, 2)
    full = [big["w_in"].reshape(NDEV, D_MODEL // NDEV, D_IN_PAD), big["w_mix"].reshape(NDEV, D_MODEL // NDEV, D_MODEL),
            big["w_xq"].reshape(NDEV, D_MODEL // NDEV, D_MODEL), big["w_xo"].reshape(NDEV, D_MODEL // NDEV, D_MODEL),
            big["w_xkv"], big["w_up"], big["w_down"].reshape(NDEV, D_FF // NDEV, D_MODEL), d_cw]
    parts = _scatter_blocks(full, "scatter_grads")
    small_like = dict(mix_pre=norm_mix_pre, mix_post=norm_mix_post, mem=norm_mem, xa_pre=norm_xa_pre,
                      xa_post=norm_xa_post, ffn_pre=norm_ffn_pre, ffn_post=norm_ffn_post, conv_b=conv_b,
                      w_pool=w_pool, pool_scale=pool_scale, b_forget=b_forget)
    small = {k: small[k].reshape(small_like[k].shape) for k in SMALL_ORDER}
    (small_parts,) = _all_gather([_pack_small(small)], "gather_small_grads")

    res = {}
    for k, p in zip(big_names, parts[:7]):
        if k == "w_in":
            p = p[:, :, :w_in.shape[2]]
        res[k] = [a[None] for a in _adamw(p, w[k][0], mo[k][0], vo[k][0], "adamw_" + k)]
    pad_cw = lambda a: jnp.pad(a[0, :, 0, :], ((0, 5), (0, 0)))
    res["conv_w"] = [a[:3][None, :, None, :] for a in
                     _adamw(parts[7], pad_cw(conv_w), pad_cw(m_conv_w), pad_cw(v_conv_w), "adamw_conv_w")]
    key_of = dict(mix_pre="norm_mix_pre", mix_post="norm_mix_post", mem="norm_mem", xa_pre="norm_xa_pre",
                  xa_post="norm_xa_post", ffn_pre="norm_ffn_pre", ffn_post="norm_ffn_post", conv_b="conv_b",
                  w_pool="w_pool", pool_scale="pool_scale", b_forget="b_forget")
    pack_of = lambda src: _pack_small({k: src[key_of[k]] for k in SMALL_ORDER})
    small_out = _adamw(small_parts, pack_of(w), pack_of(mo), pack_of(vo), "adamw_small")
    small_out = [_unpack_small(a, small_like) for a in small_out]
    for k in SMALL_ORDER:
        res[key_of[k]] = [so[k] for so in small_out]

    total = lax.psum(loss[0, 0], ("x", "y", "c"))
    outs = [total, grad_x[None]]
    for idx in range(4):
        outs += [res[k][idx] for k in names]
    return tuple(outs)
```

```python
import functools
import math

import jax
import jax.numpy as jnp
from jax import lax
from jax.experimental import pallas as pl
from jax.experimental.pallas import tpu as pltpu

F32 = jnp.float32
BF16 = jnp.bfloat16

NDEV = 8
D_MODEL = 1024
D_POOL = 256
D_FOX = 768
FOX_HEADS = 12
HEAD_PAIRS = FOX_HEADS // 2
XA_HEADS = 4
XA_DIM = 256
D_FF = 4096
D_IN_PAD = 2688
F_COL = 2560
POOL_HALO = 16
NORM_EPS = 1e-6
NEG = -1e30

ADAM_LR = 0.001
ADAM_B1 = 0.9
ADAM_B2 = 0.999
ADAM_EPS = 1e-08
ADAM_WD = 0.01
ADAM_STEP = 10

TM = 512
TQ = 512
TN_FF = 512
VMEM_LIMIT = 48 * 1024 * 1024
MESH = pl.DeviceIdType.MESH


def _cp(*sem):
    return pltpu.CompilerParams(dimension_semantics=sem, vmem_limit_bytes=VMEM_LIMIT)


def _dot(a, b, dims):
    return lax.dot_general(a, b, (dims, ((), ())), preferred_element_type=F32)


NN = ((1,), (0,))
NT = ((1,), (1,))
TN = ((0,), (0,))


def _mm(a, b, mode, out_dtype, tm, tn, tk, name, b_cols=None, out_cols=None):
    a_list = list(a) if isinstance(a, (list, tuple)) else [a]
    b_list = list(b) if isinstance(b, (list, tuple)) else [b]
    assert len(a_list) == 1 or len(b_list) == 1
    if mode == "tn":
        K, M = a_list[0].shape
        assert len(a_list) == 1
        Ns = [x.shape[1] for x in b_list]
        N = sum(Ns)
        assert b_cols is None
    else:
        assert len(b_list) == 1
        M = a_list[0].shape[0]
        Ks = [x.shape[1] for x in a_list]
        K = sum(Ks)
        if b_cols is None:
            N = b_list[0].shape[0] if mode == "nt" else b_list[0].shape[1]
        else:
            N = b_list[0].shape[1] if mode == "nt" else NDEV * b_cols
    assert M % tm == 0 and N % tn == 0 and K % tk == 0, (name, M, N, K)
    grid = (M // tm, N // tn, K // tk)
    nk = grid[2]
    dims = {"nn": NN, "nt": NT, "tn": TN}[mode]

    in_specs = []
    if mode == "tn":
        in_specs.append(pl.BlockSpec((tk, tm), lambda i, j, k: (k, i)))
        if len(b_list) == 1:
            in_specs.append(pl.BlockSpec((tk, tn), lambda i, j, k: (k, j)))
        else:
            nj1 = Ns[0] // tn
            in_specs.append(pl.BlockSpec((tk, tn), lambda i, j, k: (k, jnp.minimum(j, nj1 - 1))))
            in_specs.append(pl.BlockSpec((tk, tn), lambda i, j, k: (k, jnp.maximum(j - nj1, 0))))
    else:
        if len(a_list) == 1:
            in_specs.append(pl.BlockSpec((tm, tk), lambda i, j, k: (i, k)))
        else:
            nk1 = Ks[0] // tk
            in_specs.append(pl.BlockSpec((tm, tk), lambda i, j, k: (i, jnp.minimum(k, nk1 - 1))))
            in_specs.append(pl.BlockSpec((tm, tk), lambda i, j, k: (i, jnp.maximum(k - nk1, 0))))
        if b_cols is None:
            if mode == "nn":
                in_specs.append(pl.BlockSpec((tk, tn), lambda i, j, k: (k, j)))
            else:
                in_specs.append(pl.BlockSpec((tn, tk), lambda i, j, k: (j, k)))
        else:
            if mode == "nn":
                per = b_cols // tn
                in_specs.append(pl.BlockSpec((None, tk, tn), lambda i, j, k: (j // per, k, j % per)))
            else:
                per = b_cols // tk
                in_specs.append(pl.BlockSpec((None, tn, tk), lambda i, j, k: (k // per, j, k % per)))
    if out_cols is None:
        out_spec = pl.BlockSpec((tm, tn), lambda i, j, k: (i, j))
        out_shape = jax.ShapeDtypeStruct((M, N), out_dtype)
    else:
        pero = out_cols // tn
        out_spec = pl.BlockSpec((None, tm, tn), lambda i, j, k: (j // pero, i, j % pero))
        out_shape = jax.ShapeDtypeStruct((NDEV, M, out_cols), out_dtype)

    two_a = len(a_list) == 2
    two_b = len(b_list) == 2

    def body(*refs):
        o_ref, acc_ref = refs[-2], refs[-1]
        j = pl.program_id(1)
        k = pl.program_id(2)

        @pl.when(k == 0)
        def _():
            acc_ref[...] = jnp.zeros_like(acc_ref)

        if two_a:
            a1, a2, b1 = refs[0], refs[1], refs[2]
            nk1_ = Ks[0] // tk

            @pl.when(k < nk1_)
            def _():
                acc_ref[...] += _dot(a1[...], b1[...], dims)

            @pl.when(k >= nk1_)
            def _():
                acc_ref[...] += _dot(a2[...], b1[...], dims)
        elif two_b:
            a1, b1, b2 = refs[0], refs[1], refs[2]
            nj1_ = Ns[0] // tn

            @pl.when(j < nj1_)
            def _():
                acc_ref[...] += _dot(a1[...], b1[...], dims)

            @pl.when(j >= nj1_)
            def _():
                acc_ref[...] += _dot(a1[...], b2[...], dims)
        else:
            acc_ref[...] += _dot(refs[0][...], refs[1][...], dims)

        @pl.when(k == nk - 1)
        def _():
            o_ref[...] = acc_ref[...].astype(o_ref.dtype)

    return pl.pallas_call(
        body, name=name, grid=grid, in_specs=in_specs, out_specs=out_spec, out_shape=out_shape,
        scratch_shapes=[pltpu.VMEM((tm, tn), F32)],
        compiler_params=_cp("parallel", "parallel", "arbitrary"),
    )(*a_list, *b_list)


def _rstd(x):
    return lax.rsqrt(jnp.mean(x * x, axis=-1, keepdims=True) + NORM_EPS)


def _norm_bwd_rows(dxn, xn, r):
    return r * (dxn - xn * jnp.mean(dxn * xn, axis=-1, keepdims=True))


def _row_spec(tm, d):
    return pl.BlockSpec((tm, d), lambda i: (i, 0))


def _vec_spec(d):
    return pl.BlockSpec((1, d), lambda i: (0, 0))


def _norm_fwd(x, g, name):
    s, d = x.shape
    tm = min(TM, s)

    def body(x_ref, g_ref, h_ref):
        xv = x_ref[...]
        h_ref[...] = (xv * _rstd(xv) * g_ref[...]).astype(BF16)

    return pl.pallas_call(
        body, name=name, grid=(s // tm,), in_specs=[_row_spec(tm, d), _vec_spec(d)],
        out_specs=_row_spec(tm, d), out_shape=jax.ShapeDtypeStruct((s, d), BF16),
        compiler_params=_cp("parallel"),
    )(x, g)


def _resid_norm_fwd(x_in, y, g_post, g_next, name):
    s, d = x_in.shape

    def body(x_ref, y_ref, gp_ref, gn_ref, xo_ref, h_ref):
        yv = y_ref[...]
        xo = x_ref[...] + yv * _rstd(yv) * gp_ref[...]
        xo_ref[...] = xo
        h_ref[...] = (xo * _rstd(xo) * gn_ref[...]).astype(BF16)

    return pl.pallas_call(
        body, name=name, grid=(s // TM,),
        in_specs=[_row_spec(TM, d), _row_spec(TM, d), _vec_spec(d), _vec_spec(d)],
        out_specs=[_row_spec(TM, d), _row_spec(TM, d)],
        out_shape=[jax.ShapeDtypeStruct((s, d), F32), jax.ShapeDtypeStruct((s, d), BF16)],
        compiler_params=_cp("parallel"),
    )(x_in, y, g_post, g_next)


def _norm_bwd(dh, x, dx_res, g_pre, name, prev=None):
    s, d = x.shape
    tm = min(TM, s)
    has_prev = prev is not None

    def body(*refs):
        if has_prev:
            dh_ref, x_ref, dr_ref, g_ref, y_ref, gp_ref, dx_ref, dg_ref, dy_ref, dgp_ref = refs
        else:
            dh_ref, x_ref, dr_ref, g_ref, dx_ref, dg_ref = refs
        i = pl.program_id(0)
        xv = x_ref[...]
        r = _rstd(xv)
        xn = xv * r
        dhv = dh_ref[...].astype(F32)
        dx = dr_ref[...] + _norm_bwd_rows(dhv * g_ref[...], xn, r)
        dx_ref[...] = dx
        dg = jnp.sum(dhv * xn, axis=0, keepdims=True)

        @pl.when(i == 0)
        def _():
            dg_ref[...] = dg

        @pl.when(i > 0)
        def _():
            dg_ref[...] += dg

        if has_prev:
            yv = y_ref[...]
            r2 = _rstd(yv)
            yn = yv * r2
            dy_ref[...] = _norm_bwd_rows(dx * gp_ref[...], yn, r2).astype(BF16)
            dgp = jnp.sum(dx * yn, axis=0, keepdims=True)

            @pl.when(i == 0)
            def _():
                dgp_ref[...] = dgp

            @pl.when(i > 0)
            def _():
                dgp_ref[...] += dgp

    in_specs = [_row_spec(tm, d), _row_spec(tm, d), _row_spec(tm, d), _vec_spec(d)]
    out_specs = [_row_spec(tm, d), _vec_spec(d)]
    out_shape = [jax.ShapeDtypeStruct((s, d), F32), jax.ShapeDtypeStruct((1, d), F32)]
    args = [dh, x, dx_res, g_pre]
    if has_prev:
        in_specs += [_row_spec(tm, d), _vec_spec(d)]
        out_specs += [_row_spec(tm, d), _vec_spec(d)]
        out_shape += [jax.ShapeDtypeStruct((s, d), BF16), jax.ShapeDtypeStruct((1, d), F32)]
        args += list(prev)
    return pl.pallas_call(
        body, name=name, grid=(s // tm,), in_specs=in_specs, out_specs=out_specs, out_shape=out_shape,
        compiler_params=_cp("arbitrary"),
    )(*args)


def _loss_bwd(x2, y3, g_post, tgt, name):
    s, d = x2.shape

    def body(x_ref, y_ref, g_ref, t_ref, loss_ref, dx_ref, dy_ref, dg_ref):
        i = pl.program_id(0)
        yv = y_ref[...]
        r = _rstd(yv)
        yn = yv * r
        e = x_ref[...] + yn * g_ref[...] - t_ref[...]
        part = 0.5 * jnp.sum(jnp.mean(e * e, axis=-1, keepdims=True), axis=0, keepdims=True)
        dx = e * (1.0 / d)
        dx_ref[...] = dx
        dy_ref[...] = _norm_bwd_rows(dx * g_ref[...], yn, r).astype(BF16)
        dg = jnp.sum(dx * yn, axis=0, keepdims=True)
        part = jnp.broadcast_to(part, (1, 128))

        @pl.when(i == 0)
        def _():
            dg_ref[...] = dg
            loss_ref[...] = part

        @pl.when(i > 0)
        def _():
            dg_ref[...] += dg
            loss_ref[...] += part

    return pl.pallas_call(
        body, name=name, grid=(s // TM,),
        in_specs=[_row_spec(TM, d), _row_spec(TM, d), _vec_spec(d), _row_spec(TM, d)],
        out_specs=[_vec_spec(128), _row_spec(TM, d), _row_spec(TM, d), _vec_spec(d)],
        out_shape=[jax.ShapeDtypeStruct((1, 128), F32), jax.ShapeDtypeStruct((s, d), F32),
                   jax.ShapeDtypeStruct((s, d), BF16), jax.ShapeDtypeStruct((1, d), F32)],
        compiler_params=_cp("arbitrary"),
    )(x2, y3, g_post, tgt)


def _split3(v):
    hi = v.astype(BF16)
    r1 = v - hi.astype(F32)
    mid = r1.astype(BF16)
    lo = (r1 - mid.astype(F32)).astype(BF16)
    return hi, mid, lo


def _tri_dot(tri, v):
    hi, mid, lo = _split3(v)
    return _dot(tri, hi, NN) + _dot(tri, mid, NN) + _dot(tri, lo, NN)


def _gate_cumsum(fraw, b_pad, name):
    s = fraw.shape[0]

    def body(f_ref, b_ref, flog_ref, cum_ref, carry_ref):
        i = pl.program_id(0)

        @pl.when(i == 0)
        def _():
            carry_ref[...] = jnp.zeros_like(carry_ref)

        flog = f_ref[...] + b_ref[...]
        flog_ref[...] = flog
        lf = jnp.minimum(flog, 0.0) - jnp.log(1.0 + jnp.exp(-jnp.abs(flog)))
        lane = lax.broadcasted_iota(jnp.int32, (1, 128), 1)
        lf = jnp.where(lane < FOX_HEADS, lf, 0.0)
        row = lax.broadcasted_iota(jnp.int32, (TM, TM), 0)
        col = lax.broadcasted_iota(jnp.int32, (TM, TM), 1)
        tri = (row >= col).astype(BF16)
        cum = _tri_dot(tri, lf) + carry_ref[...]
        cum_ref[...] = cum
        carry_ref[...] = cum[TM - 1:TM, :]

    return pl.pallas_call(
        body, name=name, grid=(s // TM,),
        in_specs=[_row_spec(TM, 128), _vec_spec(128)],
        out_specs=[_row_spec(TM, 128), _row_spec(TM, 128)],
        out_shape=[jax.ShapeDtypeStruct((s, 128), F32), jax.ShapeDtypeStruct((s, 128), F32)],
        scratch_shapes=[pltpu.VMEM((1, 128), F32)],
        compiler_params=_cp("arbitrary"),
    )(fraw, b_pad)


def _gate_bwd(qaux, kaux, flog, name):
    s = flog.shape[0]
    n = s // TM

    def body(qa_ref, ka_ref, fl_ref, dp_ref, db_ref, carry_ref):
        i = pl.program_id(0)

        @pl.when(i == 0)
        def _():
            carry_ref[...] = jnp.zeros_like(carry_ref)

        src = lax.broadcasted_iota(jnp.int32, (128, 128), 0)
        dst = lax.broadcasted_iota(jnp.int32, (128, 128), 1)
        dcum = jnp.zeros((TM, 128), F32)
        for p in range(HEAD_PAIRS):
            for ref, l0, l1, sign in ((qa_ref, 64, 0, 1.0), (ka_ref, 67, 3, -1.0)):
                hit = jnp.logical_or(jnp.logical_and(src == l0, dst == 2 * p),
                                     jnp.logical_and(src == l1, dst == 2 * p + 1))
                sel = jnp.where(hit, sign, 0.0).astype(BF16)
                for piece in _split3(ref[p]):
                    dcum = dcum + _dot(piece, sel, NN)
        row = lax.broadcasted_iota(jnp.int32, (TM, TM), 0)
        col = lax.broadcasted_iota(jnp.int32, (TM, TM), 1)
        tri = (row <= col).astype(BF16)
        dlf = _tri_dot(tri, dcum) + carry_ref[...]
        carry_ref[...] = dlf[0:1, :]
        lane = lax.broadcasted_iota(jnp.int32, (1, 128), 1)
        df = jnp.where(lane < FOX_HEADS, dlf / (1.0 + jnp.exp(fl_ref[...])), 0.0)
        dp_ref[...] = df.astype(BF16)
        db = jnp.sum(df, axis=0, keepdims=True)

        @pl.when(i == 0)
        def _():
            db_ref[...] = db

        @pl.when(i > 0)
        def _():
            db_ref[...] += db

    rev = lambda i: (n - 1 - i, 0)
    return pl.pallas_call(
        body, name=name, grid=(n,),
        in_specs=[pl.BlockSpec((HEAD_PAIRS, TM, 128), lambda i: (0, n - 1 - i, 0)),
                  pl.BlockSpec((HEAD_PAIRS, TM, 128), lambda i: (0, n - 1 - i, 0)), pl.BlockSpec((TM, 128), rev)],
        out_specs=[pl.BlockSpec((TM, 128), rev), _vec_spec(128)],
        out_shape=[jax.ShapeDtypeStruct((s, 128), BF16), jax.ShapeDtypeStruct((1, 128), F32)],
        scratch_shapes=[pltpu.VMEM((1, 128), F32)],
        compiler_params=_cp("arbitrary"),
    )(qaux, kaux, flog)


def _pool_consts(i, rows):
    lane = lax.broadcasted_iota(jnp.int32, (rows, D_POOL), 1)
    t1 = lax.broadcasted_iota(jnp.int32, (rows, D_POOL), 0) + i * TM + 1
    win = jnp.where(lane < 64, 2, jnp.where(lane < 128, 4, jnp.where(lane < 192, 8, 16)))
    inv = 1.0 / jnp.minimum(t1, win).astype(F32)
    return lane, inv


def _by_group(lane, s2, s4, s8, s16):
    return jnp.where(lane < 64, s2, jnp.where(lane < 128, s4, jnp.where(lane < 192, s8, s16)))


def _pool_diff(i, u_ref, halo_ref):
    u = u_ref[...].astype(F32)
    halo = jnp.where(i > 0, halo_ref[...].astype(F32), 0.0)
    ext = jnp.concatenate([halo, u], axis=0)
    s2 = ext + pltpu.roll(ext, 1, 0)
    s4 = s2 + pltpu.roll(s2, 2, 0)
    s8 = s4 + pltpu.roll(s4, 4, 0)
    s16 = s8 + pltpu.roll(s8, 8, 0)
    lane, inv = _pool_consts(i, TM)
    sel = _by_group(lane, s2[POOL_HALO:], s4[POOL_HALO:], s8[POOL_HALO:], s16[POOL_HALO:])
    return sel * inv - u


def _pool_fwd(proj, wbd, scale, ycat, name):
    s = proj.shape[0]
    hb = TM // POOL_HALO

    def body(u_ref, halo_ref, w_ref, sc_ref, y_any, y_ref):
        del y_any
        i = pl.program_id(0)
        diff = _pool_diff(i, u_ref, halo_ref)
        mixed = _dot(diff.astype(BF16), w_ref[...], NN)
        y_ref[...] = (mixed * sc_ref[...]).astype(BF16)

    return pl.pallas_call(
        body, name=name, grid=(s // TM,),
        in_specs=[pl.BlockSpec((TM, D_POOL), lambda i: (i, 0)),
                  pl.BlockSpec((POOL_HALO, D_POOL), lambda i: (jnp.maximum(i * hb - 1, 0), 0)),
                  pl.BlockSpec((D_POOL, D_POOL), lambda i: (0, 0)), _vec_spec(D_POOL),
                  pl.BlockSpec(memory_space=pl.ANY)],
        out_specs=pl.BlockSpec((TM, D_POOL), lambda i: (i, 0)),
        out_shape=jax.ShapeDtypeStruct(ycat.shape, ycat.dtype),
        input_output_aliases={4: 0},
        compiler_params=_cp("parallel"),
    )(proj, proj, wbd, scale, ycat)


def _pool_bwd(proj, dycat, wbd, scale, name):
    s = proj.shape[0]
    n = s // TM
    hb = TM // POOL_HALO
    last_halo = s // POOL_HALO - 1

    def body(u_ref, halo_ref, dy_ref, dyp_ref, w_ref, sc_ref, dp_ref, dw_ref, dsc_ref):
        i = pl.program_id(0)
        diff = _pool_diff(i, u_ref, halo_ref)
        diff_b = diff.astype(BF16)
        mixed = _dot(diff_b, w_ref[...], NN)
        dy = dy_ref[...].astype(F32)
        dmix = (dy * sc_ref[...]).astype(BF16)
        dyp = jnp.where(i < n - 1, dyp_ref[...].astype(F32), 0.0)
        dmix_p = (dyp * sc_ref[...]).astype(BF16)
        dd = _dot(dmix, w_ref[...], NT)
        dd_p = _dot(dmix_p, w_ref[...], NT)
        lane, inv = _pool_consts(i, TM)
        _, inv_p = _pool_consts(i + 1, POOL_HALO)
        ext = jnp.concatenate([dd * inv, dd_p * inv_p], axis=0)
        rows = TM + POOL_HALO
        l2 = ext + pltpu.roll(ext, rows - 1, 0)
        l4 = l2 + pltpu.roll(l2, rows - 2, 0)
        l8 = l4 + pltpu.roll(l4, rows - 4, 0)
        l16 = l8 + pltpu.roll(l8, rows - 8, 0)
        du = _by_group(lane, l2[:TM], l4[:TM], l8[:TM], l16[:TM]) - dd
        dp_ref[...] = du.astype(BF16)
        dw = _dot(diff_b, dmix, TN)
        dsc = jnp.sum(dy * mixed, axis=0, keepdims=True)

        @pl.when(i == 0)
        def _():
            dw_ref[...] = dw
            dsc_ref[...] = dsc

        @pl.when(i > 0)
        def _():
            dw_ref[...] += dw
            dsc_ref[...] += dsc

    return pl.pallas_call(
        body, name=name, grid=(n,),
        in_specs=[pl.BlockSpec((TM, D_POOL), lambda i: (i, 0)),
                  pl.BlockSpec((POOL_HALO, D_POOL), lambda i: (jnp.maximum(i * hb - 1, 0), 0)),
                  pl.BlockSpec((TM, D_POOL), lambda i: (i, 0)),
                  pl.BlockSpec((POOL_HALO, D_POOL), lambda i: (jnp.minimum((i + 1) * hb, last_halo), 0)),
                  pl.BlockSpec((D_POOL, D_POOL), lambda i: (0, 0)), _vec_spec(D_POOL)],
        out_specs=[pl.BlockSpec((TM, D_POOL), lambda i: (i, 0)),
                   pl.BlockSpec((D_POOL, D_POOL), lambda i: (0, 0)), _vec_spec(D_POOL)],
        out_shape=[jax.ShapeDtypeStruct((s, D_POOL), BF16),
                   jax.ShapeDtypeStruct((D_POOL, D_POOL), F32), jax.ShapeDtypeStruct((1, D_POOL), F32)],
        compiler_params=_cp("arbitrary"),
    )(proj, proj, dycat, dycat, wbd, scale)


Q_BLK = D_POOL // 128
K_BLK = Q_BLK + D_FOX // 128
V_BLK = K_BLK + D_FOX // 128


def _head_masks():
    lane = lax.broadcasted_iota(jnp.int32, (1, 128), 1)
    return [lane < 64, lane >= 64]


def _fox_scores(qh, k2, cq_col, ck_row, row_off):
    sc = _dot(qh, k2, NT) * 0.125 + cq_col - ck_row
    row = lax.broadcasted_iota(jnp.int32, sc.shape, 0) + row_off
    col = lax.broadcasted_iota(jnp.int32, sc.shape, 1)
    return jnp.where(row >= col, sc, NEG)


def _fox_fwd(proj, cum_c, cum_r, name):
    s = proj.shape[0]
    nq = s // TQ

    def body(q_ref, k_ref, v_ref, cq_ref, ck_ref, o_ref, lse_ref, m_ref, l_ref, acc_ref):
        qi, ki = pl.program_id(1), pl.program_id(2)
        masks = _head_masks()

        @pl.when(ki == 0)
        def _():
            m_ref[...] = jnp.full_like(m_ref, NEG)
            l_ref[...] = jnp.zeros_like(l_ref)
            acc_ref[...] = jnp.zeros_like(acc_ref)

        @pl.when(ki <= qi)
        def _():
            q2, k2, v2 = q_ref[...], k_ref[...], v_ref[...]
            cq, ck = cq_ref[...], ck_ref[...]
            pv = []
            alpha = []
            for hh in range(2):
                qh = jnp.where(masks[hh], q2, jnp.zeros_like(q2))
                vh = jnp.where(masks[hh], v2, jnp.zeros_like(v2))
                sc = _fox_scores(qh, k2, cq[:, hh:hh + 1], ck[hh:hh + 1, :], (qi - ki) * TQ)
                m_prev = m_ref[hh]
                m_new = jnp.maximum(m_prev, jnp.max(sc, axis=1, keepdims=True))
                a = jnp.exp(m_prev - m_new)
                p = jnp.exp(sc - m_new)
                l_ref[hh] = a * l_ref[hh] + jnp.sum(p, axis=1, keepdims=True)
                m_ref[hh] = m_new
                pv.append(_dot(p.astype(BF16), vh, NN))
                alpha.append(a)
            acc_ref[...] = acc_ref[...] * jnp.where(masks[0], alpha[0], alpha[1]) + pv[0] + pv[1]

        @pl.when(ki == qi)
        def _():
            inv = jnp.where(masks[0], 1.0 / l_ref[0], 1.0 / l_ref[1])
            o_ref[...] = (acc_ref[...] * inv).astype(BF16)
            lane = lax.broadcasted_iota(jnp.int32, (1, 128), 1)
            lse_ref[...] = jnp.where(lane == 0, m_ref[0] + jnp.log(l_ref[0]), m_ref[1] + jnp.log(l_ref[1]))

    kv_row = lambda p, qi, ki: jnp.minimum(ki, qi)
    return pl.pallas_call(
        body, name=name, grid=(HEAD_PAIRS, nq, nq),
        in_specs=[pl.BlockSpec((TQ, 128), lambda p, qi, ki: (qi, Q_BLK + p)),
                  pl.BlockSpec((TQ, 128), lambda p, qi, ki: (kv_row(p, qi, ki), K_BLK + p)),
                  pl.BlockSpec((TQ, 128), lambda p, qi, ki: (kv_row(p, qi, ki), V_BLK + p)),
                  pl.BlockSpec((None, TQ, 128), lambda p, qi, ki: (p, qi, 0)),
                  pl.BlockSpec((None, 8, TQ), lambda p, qi, ki: (p, 0, kv_row(p, qi, ki)))],
        out_specs=[pl.BlockSpec((TQ, 128), lambda p, qi, ki: (qi, Q_BLK + p)),
                   pl.BlockSpec((None, TQ, 128), lambda p, qi, ki: (p, qi, 0))],
        out_shape=[jax.ShapeDtypeStruct((s, D_MODEL), BF16), jax.ShapeDtypeStruct((HEAD_PAIRS, s, 128), F32)],
        scratch_shapes=[pltpu.VMEM((2, TQ, 1), F32), pltpu.VMEM((2, TQ, 1), F32), pltpu.VMEM((TQ, 128), F32)],
        compiler_params=_cp("parallel", "parallel", "arbitrary"),
    )(proj, proj, proj, cum_c, cum_r)


def _fox_bwd(proj, ycat, dycat, cum_c, cum_r, lse, name):
    s = proj.shape[0]
    nq = s // TQ

    def body(q_ref, k_ref, v_ref, do_ref, o_ref, cq_ref, ck_ref, lse_ref,
             dq_ref, dk_ref, dv_ref, dc_ref, dcq_ref, dq_acc, dk_acc, dv_acc, dc_acc, dcq_acc):
        ki, qi = pl.program_id(1), pl.program_id(2)
        masks = _head_masks()
        lane = lax.broadcasted_iota(jnp.int32, (1, 128), 1)

        @pl.when(qi == ki)
        def _():
            dk_acc[...] = jnp.zeros_like(dk_acc)
            dv_acc[...] = jnp.zeros_like(dv_acc)
            dc_acc[...] = jnp.zeros_like(dc_acc)

        @pl.when(qi >= ki)
        def _():
            q2, k2, v2, do2 = q_ref[...], k_ref[...], v_ref[...], do_ref[...]
            cq, ck, lse2 = cq_ref[...], ck_ref[...], lse_ref[...]
            dd = do2.astype(F32) * o_ref[...].astype(F32)
            d0 = jnp.sum(jnp.where(masks[0], dd, 0.0), axis=1, keepdims=True)
            drow = [d0, jnp.sum(dd, axis=1, keepdims=True) - d0]
            dq = jnp.zeros((TQ, 128), F32)
            dk = jnp.zeros((TQ, 128), F32)
            dv = jnp.zeros((TQ, 128), F32)
            dcs = []
            rs = []
            for hh in range(2):
                zero = jnp.zeros_like(q2)
                qh = jnp.where(masks[hh], q2, zero)
                kh = jnp.where(masks[hh], k2, zero)
                doh = jnp.where(masks[hh], do2, zero)
                sc = _fox_scores(qh, k2, cq[:, hh:hh + 1], ck[hh:hh + 1, :], (qi - ki) * TQ)
                p = jnp.exp(sc - lse2[:, hh:hh + 1])
                dp = _dot(doh, v2, NT)
                ds = p * (dp - drow[hh])
                dsb = (ds * 0.125).astype(BF16)
                dv = dv + _dot(p.astype(BF16), doh, TN)
                dk = dk + _dot(dsb, qh, TN)
                dq = dq + _dot(dsb, kh, NN)
                dcs.append(-jnp.sum(ds, axis=0, keepdims=True))
                rs.append(jnp.sum(ds, axis=1, keepdims=True))
            dk_acc[...] += dk
            dv_acc[...] += dv
            dc_acc[0:1, :] += dcs[0]
            dc_acc[1:2, :] += dcs[1]
            dcq = jnp.where(lane == 0, rs[0], rs[1])

            @pl.when(ki == 0)
            def _():
                dq_acc[qi] = dq
                dcq_acc[qi] = dcq

            @pl.when(ki > 0)
            def _():
                dq_acc[qi] += dq
                dcq_acc[qi] += dcq

            @pl.when(qi == ki)
            def _():
                rows = pl.ds(pl.multiple_of(qi * TQ, TQ), TQ)
                dq_ref[rows, :] = dq_acc[qi].astype(BF16)
                dcq_ref[rows, :] = dcq_acc[qi]

        @pl.when(qi == nq - 1)
        def _():
            dk_ref[...] = dk_acc[...].astype(BF16)
            dv_ref[...] = dv_acc[...].astype(BF16)
            dc_ref[...] = dc_acc[...]

    q_row = lambda p, ki, qi: jnp.maximum(qi, ki)
    return pl.pallas_call(
        body, name=name, grid=(HEAD_PAIRS, nq, nq),
        in_specs=[pl.BlockSpec((TQ, 128), lambda p, ki, qi: (q_row(p, ki, qi), Q_BLK + p)),
                  pl.BlockSpec((TQ, 128), lambda p, ki, qi: (ki, K_BLK + p)),
                  pl.BlockSpec((TQ, 128), lambda p, ki, qi: (ki, V_BLK + p)),
                  pl.BlockSpec((TQ, 128), lambda p, ki, qi: (q_row(p, ki, qi), Q_BLK + p)),
                  pl.BlockSpec((TQ, 128), lambda p, ki, qi: (q_row(p, ki, qi), Q_BLK + p)),
                  pl.BlockSpec((None, TQ, 128), lambda p, ki, qi: (p, q_row(p, ki, qi), 0)),
                  pl.BlockSpec((None, 8, TQ), lambda p, ki, qi: (p, 0, ki)),
                  pl.BlockSpec((None, TQ, 128), lambda p, ki, qi: (p, q_row(p, ki, qi), 0))],
        out_specs=[pl.BlockSpec((s, 128), lambda p, ki, qi: (0, p)),
                   pl.BlockSpec((TQ, 128), lambda p, ki, qi: (ki, p)),
                   pl.BlockSpec((TQ, 128), lambda p, ki, qi: (ki, p)),
                   pl.BlockSpec((None, 8, TQ), lambda p, ki, qi: (p, 0, ki)),
                   pl.BlockSpec((None, s, 128), lambda p, ki, qi: (p, 0, 0))],
        out_shape=[jax.ShapeDtypeStruct((s, D_FOX), BF16)] * 3 + [jax.ShapeDtypeStruct((HEAD_PAIRS, 8, s), F32),
                                                                 jax.ShapeDtypeStruct((HEAD_PAIRS, s, 128), F32)],
        scratch_shapes=[pltpu.VMEM((nq, TQ, 128), F32), pltpu.VMEM((TQ, 128), F32), pltpu.VMEM((TQ, 128), F32),
                        pltpu.VMEM((8, TQ), F32), pltpu.VMEM((nq, TQ, 128), F32)],
        compiler_params=_cp("arbitrary", "arbitrary", "arbitrary"),
    )(proj, proj, proj, dycat, ycat, cum_c, cum_r, lse)


def _operand_lanes(v0, v1, ones_off):
    lane = lax.broadcasted_iota(jnp.int32, (1, 128), 1)
    out = jnp.zeros((v0.shape[0], 128), F32)
    if ones_off is not None:
        half = lane & 63
        out = out + jnp.where(jnp.logical_and(half >= ones_off, half < ones_off + 3), 1.0, 0.0)
    for base, v in ((64, v0), (0, v1)):
        for j, piece in enumerate(_split3(v)):
            out = jnp.where(lane == base + j, piece.astype(F32), out)
    return out.astype(BF16)


def _fox_operands(cum, name):
    s = cum.shape[0]
    width = HEAD_PAIRS * 128

    def body(c_ref, aq_ref, ak_ref):
        pieces = _split3(c_ref[...])
        row = lax.broadcasted_iota(jnp.int32, (128, width), 0)
        col = lax.broadcasted_iota(jnp.int32, (128, width), 1)
        base = (row >> 1) * 128 + (1 - (row & 1)) * 64
        half = lax.broadcasted_iota(jnp.int32, (1, width), 1) & 63
        for o_ref, off, sign, ones_off in ((aq_ref, 0, 1.0, 3), (ak_ref, 3, -1.0, 0)):
            out = jnp.where(jnp.logical_and(half >= ones_off, half < ones_off + 3), 1.0, 0.0)
            for j, piece in enumerate(pieces):
                sel = jnp.where(jnp.logical_and(col == base + off + j, row < FOX_HEADS), sign, 0.0).astype(BF16)
                out = out + _dot(piece, sel, NN)
            o_ref[...] = out.astype(BF16)

    return pl.pallas_call(
        body, name=name, grid=(s // TM,), in_specs=[_row_spec(TM, 128)],
        out_specs=[_row_spec(TM, width), _row_spec(TM, width)],
        out_shape=[jax.ShapeDtypeStruct((s, width), BF16)] * 2,
        compiler_params=_cp("parallel"),
    )(cum)


def _fox_do_operand(dycat, ycat, name):
    s = dycat.shape[0]

    def body(do_ref, o_ref, ad_ref):
        lane = lax.broadcasted_iota(jnp.int32, (1, 128), 1)
        dd = do_ref[...].astype(F32) * o_ref[...].astype(F32)
        d0 = jnp.sum(jnp.where(lane < 64, dd, 0.0), axis=1, keepdims=True)
        d1 = jnp.sum(dd, axis=1, keepdims=True) - d0
        ad_ref[...] = _operand_lanes(-d0, -d1, None)

    blk = pl.BlockSpec((TM, 128), lambda i, p: (i, Q_BLK + p))
    return pl.pallas_call(
        body, name=name, grid=(s // TM, HEAD_PAIRS), in_specs=[blk, blk],
        out_specs=pl.BlockSpec((TM, 128), lambda i, p: (i, p)),
        out_shape=jax.ShapeDtypeStruct((s, HEAD_PAIRS * 128), BF16),
        compiler_params=_cp("parallel", "parallel"),
    )(dycat, ycat)


def _causal_pairs(nq, key_major):
    if key_major:
        pairs = [(q, k) for k in range(nq) for q in range(k, nq)]
    else:
        pairs = [(q, k) for q in range(nq) for k in range(q + 1)]
    return (jnp.asarray([p[0] for p in pairs], jnp.int32), jnp.asarray([p[1] for p in pairs], jnp.int32))


def _diag_mask(sc):
    row = lax.broadcasted_iota(jnp.int32, sc.shape, 0)
    col = lax.broadcasted_iota(jnp.int32, sc.shape, 1)
    return jnp.where(row >= col, sc, NEG)


def _fox_fwd2(proj, aq, ak, name):
    s = proj.shape[0]
    nq = s // TQ
    qi_arr, ki_arr = _causal_pairs(nq, key_major=False)

    def body(qi_ref, ki_ref, q_ref, k_ref, v_ref, aq_ref, ak_ref, o_ref, aqb_ref, m_ref, acc_ref, aux_ref):
        t = pl.program_id(1)
        qi, ki = qi_ref[t], ki_ref[t]
        lane = lax.broadcasted_iota(jnp.int32, (1, 128), 1)
        masks = [lane < 64, lane >= 64]
        ones_v = jnp.where((lane & 63) == 8, 1.0, 0.0).astype(BF16)

        @pl.when(ki == 0)
        def _():
            m_ref[...] = jnp.full_like(m_ref, NEG)
            acc_ref[...] = jnp.zeros_like(acc_ref)
            aux_ref[...] = jnp.zeros_like(aux_ref)

        def step(diag):
            q2s = q_ref[...] * 0.125
            k2, v2, aq2, ak2 = k_ref[...], v_ref[...], aq_ref[...], ak_ref[...]
            pv, alpha = [], []
            for hh in range(2):
                qh = jnp.where(masks[hh], q2s, aq2)
                kh = jnp.where(masks[hh], k2, ak2)
                vh = jnp.where(masks[hh], v2, ones_v)
                sc = _dot(qh, kh, NT)
                if diag:
                    sc = _diag_mask(sc)
                m_prev = m_ref[hh]
                m_new = jnp.maximum(m_prev, jnp.max(sc, axis=1, keepdims=True))
                m_ref[hh] = m_new
                alpha.append(jnp.exp(m_prev - m_new))
                pv.append(_dot(jnp.exp(sc - m_new).astype(BF16), vh, NN))
            acc_ref[...] = acc_ref[...] * jnp.where(masks[0], alpha[0], alpha[1]) + jnp.where(masks[0], pv[0], pv[1])
            aux_ref[...] = aux_ref[...] * jnp.where(masks[0], alpha[1], alpha[0]) + jnp.where(masks[0], pv[1], pv[0])

        @pl.when(ki < qi)
        def _():
            step(False)

        @pl.when(ki == qi)
        def _():
            step(True)
            aux = aux_ref[...]
            l0, l1 = aux[:, 72:73], aux[:, 8:9]
            o_ref[...] = (acc_ref[...] * jnp.where(masks[0], 1.0 / l0, 1.0 / l1)).astype(BF16)
            aqf = aq_ref[...].astype(F32)
            cum0 = aqf[:, 64:65] + aqf[:, 65:66] + aqf[:, 66:67]
            cum1 = aqf[:, 0:1] + aqf[:, 1:2] + aqf[:, 2:3]
            aqb_ref[...] = _operand_lanes(cum0 - (m_ref[0] + jnp.log(l0)), cum1 - (m_ref[1] + jnp.log(l1)), 3)

    grid_spec = pltpu.PrefetchScalarGridSpec(
        num_scalar_prefetch=2, grid=(HEAD_PAIRS, int(qi_arr.shape[0])),
        in_specs=[pl.BlockSpec((TQ, 128), lambda p, t, qi, ki: (qi[t], Q_BLK + p)),
                  pl.BlockSpec((TQ, 128), lambda p, t, qi, ki: (ki[t], K_BLK + p)),
                  pl.BlockSpec((TQ, 128), lambda p, t, qi, ki: (ki[t], V_BLK + p)),
                  pl.BlockSpec((TQ, 128), lambda p, t, qi, ki: (qi[t], p)),
                  pl.BlockSpec((TQ, 128), lambda p, t, qi, ki: (ki[t], p))],
        out_specs=[pl.BlockSpec((TQ, 128), lambda p, t, qi, ki: (qi[t], Q_BLK + p)),
                   pl.BlockSpec((TQ, 128), lambda p, t, qi, ki: (qi[t], p))],
        scratch_shapes=[pltpu.VMEM((2, TQ, 1), F32), pltpu.VMEM((TQ, 128), F32), pltpu.VMEM((TQ, 128), F32)])
    return pl.pallas_call(
        body, name=name, grid_spec=grid_spec,
        out_shape=[jax.ShapeDtypeStruct((s, D_MODEL), BF16), jax.ShapeDtypeStruct((s, HEAD_PAIRS * 128), BF16)],
        compiler_params=_cp("parallel", "arbitrary"),
    )(qi_arr, ki_arr, proj, proj, proj, aq, ak)


def _fox_bwd2(proj, dycat, aqb, ak, ad, name):
    s = proj.shape[0]
    nq = s // TQ
    qi_arr, ki_arr = _causal_pairs(nq, key_major=True)

    def body(qi_ref, ki_ref, q_ref, k_ref, v_ref, do_ref, aq_ref, ak_ref, ad_ref,
             dq_ref, dk_ref, dv_ref, qaux_ref, kaux_ref, dq_acc, qaux_acc, dk_acc, dv_acc, kaux_acc):
        t = pl.program_id(1)
        qi, ki = qi_ref[t], ki_ref[t]
        lane = lax.broadcasted_iota(jnp.int32, (1, 128), 1)
        masks = [lane < 64, lane >= 64]
        ones_v = jnp.where((lane & 63) < 3, 1.0, 0.0).astype(BF16)

        @pl.when(qi == ki)
        def _():
            dk_acc[...] = jnp.zeros_like(dk_acc)
            dv_acc[...] = jnp.zeros_like(dv_acc)
            kaux_acc[...] = jnp.zeros_like(kaux_acc)

        def step(diag):
            q2s = q_ref[...] * 0.125
            k2, v2, do2 = k_ref[...], v_ref[...], do_ref[...]
            aq2, ak2, ad2 = aq_ref[...], ak_ref[...], ad_ref[...]
            dq, dk, dv = [], [], []
            for hh in range(2):
                qh = jnp.where(masks[hh], q2s, aq2)
                kh = jnp.where(masks[hh], k2, ak2)
                doh = jnp.where(masks[hh], do2, ad2)
                vh = jnp.where(masks[hh], v2, ones_v)
                sc = _dot(qh, kh, NT)
                if diag:
                    sc = _diag_mask(sc)
                p = jnp.exp(sc)
                dsb = (p * _dot(doh, vh, NT)).astype(BF16)
                dv.append(_dot(p.astype(BF16), doh, TN))
                dk.append(_dot(dsb, qh, TN))
                dq.append(_dot(dsb, kh, NN))
            dk_acc[...] += jnp.where(masks[0], dk[0], dk[1])
            kaux_acc[...] += jnp.where(masks[0], dk[1], dk[0])
            dv_acc[...] += jnp.where(masks[0], dv[0], dv[1])
            dq_new = jnp.where(masks[0], dq[0], dq[1])
            qaux_new = jnp.where(masks[0], dq[1], dq[0])

            @pl.when(ki == 0)
            def _():
                dq_acc[qi] = dq_new
                qaux_acc[qi] = qaux_new

            @pl.when(ki > 0)
            def _():
                dq_acc[qi] += dq_new
                qaux_acc[qi] += qaux_new

        @pl.when(qi > ki)
        def _():
            step(False)

        @pl.when(qi == ki)
        def _():
            step(True)
            rows = pl.ds(pl.multiple_of(qi * TQ, TQ), TQ)
            dq_ref[rows, :] = (dq_acc[qi] * 0.125).astype(BF16)
            qaux_ref[rows, :] = qaux_acc[qi]

        @pl.when(qi == nq - 1)
        def _():
            dk_ref[...] = dk_acc[...].astype(BF16)
            dv_ref[...] = dv_acc[...].astype(BF16)
            kaux_ref[...] = kaux_acc[...]

    grid_spec = pltpu.PrefetchScalarGridSpec(
        num_scalar_prefetch=2, grid=(HEAD_PAIRS, int(qi_arr.shape[0])),
        in_specs=[pl.BlockSpec((TQ, 128), lambda p, t, qi, ki: (qi[t], Q_BLK + p)),
                  pl.BlockSpec((TQ, 128), lambda p, t, qi, ki: (ki[t], K_BLK + p)),
                  pl.BlockSpec((TQ, 128), lambda p, t, qi, ki: (ki[t], V_BLK + p)),
                  pl.BlockSpec((TQ, 128), lambda p, t, qi, ki: (qi[t], Q_BLK + p)),
                  pl.BlockSpec((TQ, 128), lambda p, t, qi, ki: (qi[t], p)),
                  pl.BlockSpec((TQ, 128), lambda p, t, qi, ki: (ki[t], p)),
                  pl.BlockSpec((TQ, 128), lambda p, t, qi, ki: (qi[t], p))],
        out_specs=[pl.BlockSpec((s, 128), lambda p, t, qi, ki: (0, p)),
                   pl.BlockSpec((TQ, 128), lambda p, t, qi, ki: (ki[t], p)),
                   pl.BlockSpec((TQ, 128), lambda p, t, qi, ki: (ki[t], p)),
                   pl.BlockSpec((None, s, 128), lambda p, t, qi, ki: (p, 0, 0)),
                   pl.BlockSpec((None, TQ, 128), lambda p, t, qi, ki: (p, ki[t], 0))],
        scratch_shapes=[pltpu.VMEM((nq, TQ, 128), F32), pltpu.VMEM((nq, TQ, 128), F32),
                        pltpu.VMEM((TQ, 128), F32), pltpu.VMEM((TQ, 128), F32), pltpu.VMEM((TQ, 128), F32)])
    return pl.pallas_call(
        body, name=name, grid_spec=grid_spec,
        out_shape=[jax.ShapeDtypeStruct((s, D_FOX), BF16)] * 3 + [jax.ShapeDtypeStruct((HEAD_PAIRS, s, 128), F32)] * 2,
        compiler_params=_cp("arbitrary", "arbitrary"),
    )(qi_arr, ki_arr, proj, proj, proj, dycat, aqb, ak, ad)


XA_SCALE = XA_DIM ** -0.5


def _xattn_fwd(q2, kv, name):
    s = q2.shape[0]
    m = kv.shape[0]

    def body(q_ref, kv_ref, o_ref):
        for h in range(XA_HEADS):
            c0 = h * XA_DIM
            sc = _dot(q_ref[:, c0:c0 + XA_DIM], kv_ref[:, c0:c0 + XA_DIM], NT) * XA_SCALE
            e = jnp.exp(sc - jnp.max(sc, axis=1, keepdims=True))
            p = e / jnp.sum(e, axis=1, keepdims=True)
            o_ref[:, c0:c0 + XA_DIM] = _dot(p.astype(BF16), kv_ref[:, D_MODEL + c0:D_MODEL + c0 + XA_DIM], NN).astype(BF16)

    return pl.pallas_call(
        body, name=name, grid=(s // TM,),
        in_specs=[_row_spec(TM, D_MODEL), pl.BlockSpec((m, 2 * D_MODEL), lambda i: (0, 0))],
        out_specs=_row_spec(TM, D_MODEL), out_shape=jax.ShapeDtypeStruct((s, D_MODEL), BF16),
        compiler_params=_cp("parallel"),
    )(q2, kv)


def _xattn_bwd(q2, kv, do2, name):
    s = q2.shape[0]
    m = kv.shape[0]

    def body(q_ref, kv_ref, do_ref, dq_ref, dkv_ref):
        i = pl.program_id(0)

        @pl.when(i == 0)
        def _():
            dkv_ref[...] = jnp.zeros_like(dkv_ref)

        for h in range(XA_HEADS):
            c0 = h * XA_DIM
            v0 = D_MODEL + c0
            qh = q_ref[:, c0:c0 + XA_DIM]
            kh = kv_ref[:, c0:c0 + XA_DIM]
            doh = do_ref[:, c0:c0 + XA_DIM]
            sc = _dot(qh, kh, NT) * XA_SCALE
            e = jnp.exp(sc - jnp.max(sc, axis=1, keepdims=True))
            p = e / jnp.sum(e, axis=1, keepdims=True)
            dp = _dot(doh, kv_ref[:, v0:v0 + XA_DIM], NT)
            ds = p * (dp - jnp.sum(p * dp, axis=1, keepdims=True))
            dsb = (ds * XA_SCALE).astype(BF16)
            dq_ref[:, c0:c0 + XA_DIM] = _dot(dsb, kh, NN).astype(BF16)
            dkv_ref[:, c0:c0 + XA_DIM] += _dot(dsb, qh, TN)
            dkv_ref[:, v0:v0 + XA_DIM] += _dot(p.astype(BF16), doh, TN)

    return pl.pallas_call(
        body, name=name, grid=(s // TM,),
        in_specs=[_row_spec(TM, D_MODEL), pl.BlockSpec((m, 2 * D_MODEL), lambda i: (0, 0)), _row_spec(TM, D_MODEL)],
        out_specs=[_row_spec(TM, D_MODEL), pl.BlockSpec((m, 2 * D_MODEL), lambda i: (0, 0))],
        out_shape=[jax.ShapeDtypeStruct((s, D_MODEL), BF16), jax.ShapeDtypeStruct((m, 2 * D_MODEL), F32)],
        compiler_params=_cp("arbitrary"),
    )(q2, kv, do2)


GELU_C = math.sqrt(2.0 / math.pi)
GELU_A = 0.044715


def _gelu(x):
    return 0.5 * x * (1.0 + jnp.tanh(GELU_C * (x + GELU_A * x * x * x)))


def _gelu_and_grad(x):
    t = jnp.tanh(GELU_C * (x + GELU_A * x * x * x))
    g = 0.5 * x * (1.0 + t)
    dg = 0.5 * (1.0 + t) + 0.5 * x * (1.0 - t * t) * GELU_C * (1.0 + 3.0 * GELU_A * x * x)
    return g, dg


def _shift_down(main, prev8):
    row = lax.broadcasted_iota(jnp.int32, main.shape, 0)
    s1 = jnp.where(row == 0, prev8[7:8, :], pltpu.roll(main, 1, 0))
    s2 = jnp.where(row == 0, prev8[6:7, :], jnp.where(row == 1, prev8[7:8, :], pltpu.roll(main, 2, 0)))
    return s1, s2


def _shift_up(main, next8):
    n = main.shape[0]
    row = lax.broadcasted_iota(jnp.int32, main.shape, 0)
    u1 = jnp.where(row == n - 1, next8[0:1, :], pltpu.roll(main, n - 1, 0))
    u2 = jnp.where(row == n - 2, next8[0:1, :], jnp.where(row == n - 1, next8[1:2, :], pltpu.roll(main, n - 2, 0)))
    return u1, u2


def _conv(h, s1, s2, w_ref, b_ref):
    return w_ref[0:1, :] * s2 + w_ref[1:2, :] * s1 + w_ref[2:3, :] * h + b_ref[...]


def _ffn_fwd(h3, w_up, cw, cb, w_down, name):
    s = h3.shape[0]
    tn = TN_FF
    nj = D_FF // tn
    per = D_MODEL // tn
    hb = TM // 8

    def body(h_ref, halo_ref, wg_ref, wu_ref, cwg_ref, cwu_ref, cbg_ref, cbu_ref, wd_ref,
             hg_ref, hu_ref, a_ref, y_ref):
        i, j = pl.program_id(0), pl.program_id(1)
        h = h_ref[...]
        halo = halo_ref[...]
        halo = jnp.where(i > 0, halo, jnp.zeros_like(halo))
        conv = []
        for w_ref, cw_ref, cb_ref, hid_ref in ((wg_ref, cwg_ref, cbg_ref, hg_ref), (wu_ref, cwu_ref, cbu_ref, hu_ref)):
            hm_b = _dot(h, w_ref[...], NN).astype(BF16)
            hid_ref[...] = hm_b
            hm = hm_b.astype(F32)
            hl = _dot(halo, w_ref[...], NN).astype(BF16).astype(F32)
            s1, s2 = _shift_down(hm, hl)
            conv.append(_conv(hm, s1, s2, cw_ref, cb_ref))
        a = (_gelu(conv[0]) * conv[1]).astype(BF16)
        a_ref[...] = a
        contrib = _dot(a, wd_ref[...], NN)

        @pl.when(j == 0)
        def _():
            y_ref[...] = contrib

        @pl.when(j > 0)
        def _():
            y_ref[...] += contrib

    return pl.pallas_call(
        body, name=name, grid=(s // TM, nj),
        in_specs=[pl.BlockSpec((TM, D_MODEL), lambda i, j: (i, 0)),
                  pl.BlockSpec((8, D_MODEL), lambda i, j: (jnp.maximum(i * hb - 1, 0), 0)),
                  pl.BlockSpec((None, D_MODEL, tn), lambda i, j: (j // per, 0, j % per)),
                  pl.BlockSpec((None, D_MODEL, tn), lambda i, j: (NDEV // 2 + j // per, 0, j % per)),
                  pl.BlockSpec((8, tn), lambda i, j: (0, j)),
                  pl.BlockSpec((8, tn), lambda i, j: (0, nj + j)),
                  pl.BlockSpec((1, tn), lambda i, j: (0, j)),
                  pl.BlockSpec((1, tn), lambda i, j: (0, nj + j)),
                  pl.BlockSpec((tn, D_MODEL), lambda i, j: (j, 0))],
        out_specs=[pl.BlockSpec((TM, tn), lambda i, j: (i, j)), pl.BlockSpec((TM, tn), lambda i, j: (i, j)),
                   pl.BlockSpec((TM, tn), lambda i, j: (i, j)), pl.BlockSpec((TM, D_MODEL), lambda i, j: (i, 0))],
        out_shape=[jax.ShapeDtypeStruct((s, D_FF), BF16), jax.ShapeDtypeStruct((s, D_FF), BF16),
                   jax.ShapeDtypeStruct((s, D_FF), BF16), jax.ShapeDtypeStruct((s, D_MODEL), F32)],
        compiler_params=_cp("parallel", "arbitrary"),
    )(h3, h3, w_up, w_up, cw, cw, cb, cb, w_down)


def _ffn_bwd(dy3, w_down, hid_g, hid_u, cw, cb, name):
    s = dy3.shape[0]
    n = s // TM
    tn = TN_FF
    nj = D_FF // tn
    hb = TM // 8
    last8 = s // 8 - 1

    def body(dy_ref, dyp_ref, wd_ref, hg_ref, hgl_ref, hgn_ref, hu_ref, hul_ref, hun_ref,
             cwg_ref, cwu_ref, cbg_ref, cbu_ref,
             dhg_ref, dhu_ref, dcwg_ref, dcwu_ref, dcbg_ref, dcbu_ref):
        i = pl.program_id(1)
        first, last = i == 0, i == n - 1
        da = _dot(dy_ref[...], wd_ref[...], NT)
        dyp = dyp_ref[...]
        dyp = jnp.where(last, jnp.zeros_like(dyp), dyp)
        da_n = _dot(dyp, wd_ref[...], NT)
        parts = []
        for h_ref, hl_ref, hn_ref, cw_ref, cb_ref in ((hg_ref, hgl_ref, hgn_ref, cwg_ref, cbg_ref),
                                                     (hu_ref, hul_ref, hun_ref, cwu_ref, cbu_ref)):
            hm = h_ref[...].astype(F32)
            hl = jnp.where(first, 0.0, hl_ref[...].astype(F32))
            hn = hn_ref[...].astype(F32)
            s1, s2 = _shift_down(hm, hl)
            c = _conv(hm, s1, s2, cw_ref, cb_ref)
            n1, n2 = _shift_down(hn, hm[TM - 8:, :])
            cn = _conv(hn, n1, n2, cw_ref, cb_ref)
            parts.append((hm, s1, s2, c, cn))
        g, dg = _gelu_and_grad(parts[0][3])
        gn, dgn = _gelu_and_grad(parts[0][4])
        dc_g = da * parts[1][3] * dg
        dc_u = da * g
        dcn_g = da_n * parts[1][4] * dgn
        dcn_u = da_n * gn
        outs = ((dc_g, dcn_g, parts[0], cwg_ref, dhg_ref, dcwg_ref, dcbg_ref),
                (dc_u, dcn_u, parts[1], cwu_ref, dhu_ref, dcwu_ref, dcbu_ref))
        for dc, dcn, (hm, s1, s2, _, _), cw_ref, dh_ref, dcw_ref, dcb_ref in outs:
            u1, u2 = _shift_up(dc, dcn)
            dh_ref[...] = (cw_ref[2:3, :] * dc + cw_ref[1:2, :] * u1 + cw_ref[0:1, :] * u2).astype(BF16)
            dcb = jnp.sum(dc, axis=0, keepdims=True)
            row8 = lax.broadcasted_iota(jnp.int32, (8, tn), 0)
            dcw = jnp.where(row8 == 0, jnp.sum(dc * s2, axis=0, keepdims=True),
                            jnp.where(row8 == 1, jnp.sum(dc * s1, axis=0, keepdims=True),
                                      jnp.where(row8 == 2, jnp.sum(dc * hm, axis=0, keepdims=True), 0.0)))

            @pl.when(first)
            def _():
                dcw_ref[...] = dcw
                dcb_ref[...] = dcb

            @pl.when(i > 0)
            def _():
                dcw_ref[...] += dcw
                dcb_ref[...] += dcb

    prev8 = lambda j, i: (jnp.maximum(i * hb - 1, 0), j)
    next8 = lambda j, i: (jnp.minimum((i + 1) * hb, last8), j)
    blk = lambda j, i: (i, j)
    col = lambda j, i: (0, j)
    colu = lambda j, i: (0, nj + j)
    return pl.pallas_call(
        body, name=name, grid=(nj, n),
        in_specs=[pl.BlockSpec((TM, D_MODEL), lambda j, i: (i, 0)),
                  pl.BlockSpec((8, D_MODEL), lambda j, i: (jnp.minimum((i + 1) * hb, last8), 0)),
                  pl.BlockSpec((tn, D_MODEL), lambda j, i: (j, 0)),
                  pl.BlockSpec((TM, tn), blk), pl.BlockSpec((8, tn), prev8), pl.BlockSpec((8, tn), next8),
                  pl.BlockSpec((TM, tn), blk), pl.BlockSpec((8, tn), prev8), pl.BlockSpec((8, tn), next8),
                  pl.BlockSpec((8, tn), col), pl.BlockSpec((8, tn), colu),
                  pl.BlockSpec((1, tn), col), pl.BlockSpec((1, tn), colu)],
        out_specs=[pl.BlockSpec((TM, tn), blk), pl.BlockSpec((TM, tn), blk),
                   pl.BlockSpec((8, tn), col), pl.BlockSpec((8, tn), col),
                   pl.BlockSpec((1, tn), col), pl.BlockSpec((1, tn), col)],
        out_shape=[jax.ShapeDtypeStruct((s, D_FF), BF16), jax.ShapeDtypeStruct((s, D_FF), BF16),
                   jax.ShapeDtypeStruct((8, D_FF), F32), jax.ShapeDtypeStruct((8, D_FF), F32),
                   jax.ShapeDtypeStruct((1, D_FF), F32), jax.ShapeDtypeStruct((1, D_FF), F32)],
        compiler_params=_cp("parallel", "arbitrary"),
    )(dy3, dy3, w_down, hid_g, hid_g, hid_g, hid_u, hid_u, hid_u, cw, cw, cb, cb)


def _slot(p):
    return 4 * p[0] + 2 * p[1] + p[2]


def _all_gather(shards, name):
    n = len(shards)

    def body(*refs):
        ins, outs = refs[:n], refs[n:2 * n]
        send_sems, recv_sems, local_sems = refs[2 * n:]
        x, y, c = lax.axis_index("x"), lax.axis_index("y"), lax.axis_index("c")
        me, sibling = (x, y, c), (x, y, 1 - c)
        chips = [(1 - x, y), (x, 1 - y), (1 - x, 1 - y)]

        def copy(a, k, block, to, from_input=False):
            dst = outs[a].at[_slot(block)]
            return pltpu.make_async_remote_copy(
                src_ref=ins[a] if from_input else dst, dst_ref=dst,
                send_sem=send_sems.at[a, k], recv_sem=recv_sems.at[a, k],
                device_id=to, device_id_type=MESH)

        mine = [pltpu.make_async_copy(ins[a], outs[a].at[_slot(me)], local_sems.at[a]) for a in range(n)]
        for cp in mine:
            cp.start()
        first = []
        for a in range(n):
            first.append(copy(a, 0, me, sibling, True))
            first += [copy(a, 1 + j, me, (*chip, c), True) for j, chip in enumerate(chips)]
        for cp in first:
            cp.start()
        passed = []
        for j, chip in enumerate(chips):
            for a in range(n):
                copy(a, 1 + j, (*chip, c), me).wait_recv()
                fwd = copy(a, 4 + j, (*chip, c), sibling)
                fwd.start()
                passed.append(fwd)
        for a in range(n):
            copy(a, 0, sibling, me).wait_recv()
            for j, chip in enumerate(chips):
                copy(a, 4 + j, (*chip, 1 - c), me).wait_recv()
        for cp in first + passed:
            cp.wait_send()
        for cp in mine:
            cp.wait()

    any_spec = pl.BlockSpec(memory_space=pl.ANY)
    return pl.pallas_call(
        body, name=name,
        in_specs=[any_spec] * n, out_specs=[any_spec] * n,
        out_shape=[jax.ShapeDtypeStruct((NDEV,) + s.shape, s.dtype) for s in shards],
        scratch_shapes=[pltpu.SemaphoreType.DMA((n, 7)), pltpu.SemaphoreType.DMA((n, 7)),
                        pltpu.SemaphoreType.DMA((n,))],
    )(*shards)


def _scatter_blocks(full, name):
    n = len(full)

    def body(*refs):
        ins, outs = refs[:n], refs[n:2 * n]
        send_sems, recv_sems, local_sems = refs[2 * n:]
        x, y, c = lax.axis_index("x"), lax.axis_index("y"), lax.axis_index("c")
        me = (x, y, c)
        copies = []
        for a in range(n):
            cp = pltpu.make_async_copy(ins[a].at[_slot(me)], outs[a].at[_slot(me)], local_sems.at[a])
            cp.start()
            copies.append(cp)
        remote = []
        for mask in range(1, NDEV):
            peer = (1 - x if mask & 4 else x, 1 - y if mask & 2 else y, 1 - c if mask & 1 else c)
            for a in range(n):
                cp = pltpu.make_async_remote_copy(
                    src_ref=ins[a].at[_slot(peer)], dst_ref=outs[a].at[_slot(me)],
                    send_sem=send_sems.at[a, mask - 1], recv_sem=recv_sems.at[a, mask - 1],
                    device_id=peer, device_id_type=MESH)
                cp.start()
                remote.append(cp)
        for cp in remote:
            cp.wait()
        for cp in copies:
            cp.wait()

    any_spec = pl.BlockSpec(memory_space=pl.ANY)
    return pl.pallas_call(
        body, name=name,
        in_specs=[any_spec] * n, out_specs=[any_spec] * n,
        out_shape=[jax.ShapeDtypeStruct(f.shape, f.dtype) for f in full],
        scratch_shapes=[pltpu.SemaphoreType.DMA((n, 7)), pltpu.SemaphoreType.DMA((n, 7)),
                        pltpu.SemaphoreType.DMA((n,))],
    )(*full)


def _adamw(parts, w, m, v, name):
    r, c = w.shape
    tr = r if r * c <= 160 * 1024 else max(8, (160 * 1024 // c) // 8 * 8)
    while r % tr:
        tr -= 8
    bc1 = 1.0 - ADAM_B1 ** ADAM_STEP
    bc2 = 1.0 - ADAM_B2 ** ADAM_STEP

    def body(p_ref, w_ref, m_ref, v_ref, g_ref, d_ref, mo_ref, vo_ref):
        g = p_ref[0].astype(F32)
        for d in range(1, NDEV):
            g = g + p_ref[d].astype(F32)
        g_ref[...] = g
        mn = ADAM_B1 * m_ref[...] + (1.0 - ADAM_B1) * g
        vn = ADAM_B2 * v_ref[...] + (1.0 - ADAM_B2) * (g * g)
        mo_ref[...] = mn
        vo_ref[...] = vn
        d_ref[...] = -ADAM_LR * ((mn / bc1) / (jnp.sqrt(vn / bc2) + ADAM_EPS) + ADAM_WD * w_ref[...])

    spec = pl.BlockSpec((tr, c), lambda i: (i, 0))
    return pl.pallas_call(
        body, name=name, grid=(r // tr,),
        in_specs=[pl.BlockSpec((NDEV, tr, c), lambda i: (0, i, 0)), spec, spec, spec],
        out_specs=[spec] * 4, out_shape=[jax.ShapeDtypeStruct((r, c), F32)] * 4,
        compiler_params=_cp("parallel"),
    )(parts, w, m, v)


def _pair_cols(a):
    s = a.shape[0]
    t = a[:, :FOX_HEADS].reshape(s, HEAD_PAIRS, 2).transpose(1, 0, 2)
    return jnp.pad(t, ((0, 0), (0, 0), (0, 126)))


def _pair_rows(at):
    s = at.shape[1]
    return jnp.pad(at[:FOX_HEADS].reshape(HEAD_PAIRS, 2, s), ((0, 0), (0, 6), (0, 0)))


def _local_step(x, mem, tgt, gains, b_forget, w_pool, pool_scale, conv_w, conv_b, wts):
    s = x.shape[0]
    w_in, w_mix, w_xq, w_xo = wts["w_in"], wts["w_mix"], wts["w_xq"], wts["w_xo"]
    w_xkv, w_up, w_down = wts["w_xkv"], wts["w_up"], wts["w_down"]

    w_f = w_in[:, F_COL:]
    b_pad = jnp.pad(b_forget, ((0, 0), (0, 128 - FOX_HEADS)))
    wbd = jnp.zeros((D_POOL, D_POOL), F32)
    for g in range(4):
        wbd = wbd.at[64 * g:64 * g + 64, 64 * g:64 * g + 64].set(w_pool[g])
    wbd = wbd.astype(BF16)
    scale = pool_scale.reshape(1, D_POOL)
    cw = jnp.pad(conv_w, ((0, 5), (0, 0)))

    h1 = _norm_fwd(x, gains["mix_pre"], "norm_mix_pre")
    proj = _mm(h1, w_in, "nn", BF16, 1024, 384, 1024, "proj_in")
    fraw = _mm(h1, w_f, "nn", F32, 1024, 128, 1024, "proj_gate")
    flog, cum = _gate_cumsum(fraw, b_pad, "gate_cumsum")
    aq, ak = _fox_operands(cum, "fox_operands")
    ycat, aqb = _fox_fwd2(proj, aq, ak, "fox_fwd")
    ycat = _pool_fwd(proj, wbd, scale, ycat, "pool_fwd")
    y1 = _mm(ycat, w_mix, "nn", F32, 1024, 1024, 1024, "mix_out")
    x1, h2 = _resid_norm_fwd(x, y1, gains["mix_post"], gains["xa_pre"], "resid_mix")
    q2 = _mm(h2, w_xq, "nn", BF16, 1024, 1024, 1024, "xa_q")
    mem_n = _norm_fwd(mem, gains["mem"], "norm_mem")
    kv = _mm(mem_n, w_xkv, "nn", BF16, mem.shape[0], 256, 1024, "xa_kv", b_cols=256)
    o2 = _xattn_fwd(q2, kv, "xattn_fwd")
    y2 = _mm(o2, w_xo, "nn", F32, 1024, 1024, 1024, "xa_out")
    x2, h3 = _resid_norm_fwd(x1, y2, gains["xa_post"], gains["ffn_pre"], "resid_xa")
    hid_g, hid_u, act, y3 = _ffn_fwd(h3, w_up, cw, conv_b, w_down, "ffn_fwd")

    loss, dx3, dy3, dg_ffn_post = _loss_bwd(x2, y3, gains["ffn_post"], tgt, "loss_bwd")
    dhid_g, dhid_u, dcw_g, dcw_u, dcb_g, dcb_u = _ffn_bwd(dy3, w_down, hid_g, hid_u, cw, conv_b, "ffn_bwd")
    d_w_down = _mm(act, dy3, "tn", BF16, 1024, 1024, 1024, "dw_down")
    d_w_up = _mm(h3, [dhid_g, dhid_u], "tn", BF16, 1024, 1024, 1024, "dw_up", out_cols=1024)
    dh3 = _mm([dhid_g, dhid_u], w_up, "nt", F32, 1024, 1024, 1024, "dh_ffn", b_cols=1024)
    dx2, dg_ffn_pre, dy2, dg_xa_post = _norm_bwd(dh3, x2, dx3, gains["ffn_pre"], "norm_bwd_ffn",
                                                 prev=(y2, gains["xa_post"]))
    do2 = _mm(dy2, w_xo, "nt", BF16, 1024, 1024, 1024, "d_xa_out")
    d_w_xo = _mm(o2, dy2, "tn", BF16, 1024, 1024, 1024, "dw_xo")
    dq2, dkv = _xattn_bwd(q2, kv, do2, "xattn_bwd")
    dkv = dkv.astype(BF16)
    dh2 = _mm(dq2, w_xq, "nt", F32, 1024, 1024, 1024, "dh_xa")
    d_w_xq = _mm(h2, dq2, "tn", BF16, 1024, 1024, 1024, "dw_xq")
    dmem_n = _mm(dkv, w_xkv, "nt", F32, mem.shape[0], 1024, 256, "d_mem", b_cols=256)
    d_w_xkv = _mm(mem_n, dkv, "tn", BF16, 1024, 256, mem.shape[0], "dw_xkv", out_cols=256)
    _, dg_mem = _norm_bwd(dmem_n, mem, jnp.zeros_like(mem), gains["mem"], "norm_bwd_mem")
    dx1, dg_xa_pre, dy1, dg_mix_post = _norm_bwd(dh2, x1, dx2, gains["xa_pre"], "norm_bwd_xa",
                                                 prev=(y1, gains["mix_post"]))
    dycat = _mm(dy1, w_mix, "nt", BF16, 1024, 1024, 1024, "d_mix_out")
    d_w_mix = _mm(ycat, dy1, "tn", BF16, 1024, 1024, 1024, "dw_mix")
    ad = _fox_do_operand(dycat, ycat, "fox_do_operand")
    dq, dk, dv, qaux, kaux = _fox_bwd2(proj, dycat, aqb, ak, ad, "fox_bwd")
    du, d_wbd, d_scale = _pool_bwd(proj, dycat, wbd, scale, "pool_bwd")
    df, db_f = _gate_bwd(qaux, kaux, flog, "gate_bwd")
    dproj = jnp.concatenate([du, dq, dk, dv, df], axis=1)
    dh1 = _mm(dproj, w_in, "nt", F32, 1024, 1024, 896, "dh_mix")
    d_w_in = _mm(h1, dproj, "tn", BF16, 1024, 384, 1024, "dw_in")
    grad_x, dg_mix_pre = _norm_bwd(dh1, x, dx1, gains["mix_pre"], "norm_bwd_mix")

    big = dict(w_in=d_w_in, w_mix=d_w_mix, w_xq=d_w_xq, w_xo=d_w_xo, w_xkv=d_w_xkv, w_up=d_w_up, w_down=d_w_down)
    small = dict(
        mix_pre=dg_mix_pre, mix_post=dg_mix_post, mem=dg_mem, xa_pre=dg_xa_pre, xa_post=dg_xa_post,
        ffn_pre=dg_ffn_pre, ffn_post=dg_ffn_post,
        conv_b=jnp.concatenate([dcb_g, dcb_u], axis=1),
        w_pool=jnp.stack([d_wbd[64 * g:64 * g + 64, 64 * g:64 * g + 64] for g in range(4)]),
        pool_scale=d_scale.reshape(4, 64),
        b_forget=db_f[:, :FOX_HEADS],
        conv_w=jnp.concatenate([dcw_g[:3], dcw_u[:3]], axis=1),
    )
    return loss, grad_x, big, small


SMALL_ORDER = ("mix_pre", "mix_post", "mem", "xa_pre", "xa_post", "ffn_pre", "ffn_post", "conv_b",
               "w_pool", "pool_scale", "b_forget")
SMALL_ROWS = 256


def _pack_small(d):
    flat = jnp.concatenate([d[k].reshape(-1).astype(F32) for k in SMALL_ORDER])
    return jnp.pad(flat, (0, SMALL_ROWS * 128 - flat.shape[0])).reshape(SMALL_ROWS, 128)


def _unpack_small(a, like):
    flat = a.reshape(-1)
    out, off = {}, 0
    for k in SMALL_ORDER:
        n = like[k].size
        out[k] = flat[off:off + n].reshape(like[k].shape)
        off += n
    return out


def kernel(x, mem, norm_mix_pre, norm_mix_post, w_in, b_forget, w_pool, pool_scale, w_mix_out, norm_mem, norm_xa_pre, norm_xa_post, w_xq, w_xkv, w_xo, norm_ffn_pre, norm_ffn_post, w_up, conv_w, conv_b, w_down, loss_target, m_norm_mix_pre, m_norm_mix_post, m_w_in, m_b_forget, m_w_pool, m_pool_scale, m_w_mix_out, m_norm_mem, m_norm_xa_pre, m_norm_xa_post, m_w_xq, m_w_xkv, m_w_xo, m_norm_ffn_pre, m_norm_ffn_post, m_w_up, m_conv_w, m_conv_b, m_w_down, v_norm_mix_pre, v_norm_mix_post, v_w_in, v_b_forget, v_w_pool, v_pool_scale, v_w_mix_out, v_norm_mem, v_norm_xa_pre, v_norm_xa_post, v_w_xq, v_w_xkv, v_w_xo, v_norm_ffn_pre, v_norm_ffn_post, v_w_up, v_conv_w, v_conv_b, v_w_down):
    names = ("norm_mix_pre", "norm_mix_post", "w_in", "b_forget", "w_pool", "pool_scale", "w_mix_out", "norm_mem",
             "norm_xa_pre", "norm_xa_post", "w_xq", "w_xkv", "w_xo", "norm_ffn_pre", "norm_ffn_post", "w_up",
             "conv_w", "conv_b", "w_down")
    w = dict(zip(names, (norm_mix_pre, norm_mix_post, w_in, b_forget, w_pool, pool_scale, w_mix_out, norm_mem,
                         norm_xa_pre, norm_xa_post, w_xq, w_xkv, w_xo, norm_ffn_pre, norm_ffn_post, w_up,
                         conv_w, conv_b, w_down)))
    mo = dict(zip(names, (m_norm_mix_pre, m_norm_mix_post, m_w_in, m_b_forget, m_w_pool, m_pool_scale, m_w_mix_out,
                          m_norm_mem, m_norm_xa_pre, m_norm_xa_post, m_w_xq, m_w_xkv, m_w_xo, m_norm_ffn_pre,
                          m_norm_ffn_post, m_w_up, m_conv_w, m_conv_b, m_w_down)))
    vo = dict(zip(names, (v_norm_mix_pre, v_norm_mix_post, v_w_in, v_b_forget, v_w_pool, v_pool_scale, v_w_mix_out,
                          v_norm_mem, v_norm_xa_pre, v_norm_xa_post, v_w_xq, v_w_xkv, v_w_xo, v_norm_ffn_pre,
                          v_norm_ffn_post, v_w_up, v_conv_w, v_conv_b, v_w_down)))

    big_names = ("w_in", "w_mix_out", "w_xq", "w_xo", "w_xkv", "w_up", "w_down")
    shards = []
    for k in big_names:
        sh = w[k][0].astype(BF16)
        if k == "w_in":
            sh = jnp.pad(sh, ((0, 0), (0, D_IN_PAD - sh.shape[1])))
        shards.append(sh)
    conv_w_sh = jnp.pad(conv_w[0, :, 0, :], ((0, 5), (0, 0)))
    gathered = _all_gather(shards + [conv_w_sh], "gather_weights")
    g_in, g_mix, g_xq, g_xo, g_xkv, g_up, g_down, g_cw = gathered
    wts = dict(w_in=g_in.reshape(D_MODEL, D_IN_PAD), w_mix=g_mix.reshape(D_MODEL, D_MODEL),
               w_xq=g_xq.reshape(D_MODEL, D_MODEL), w_xo=g_xo.reshape(D_MODEL, D_MODEL),
               w_xkv=g_xkv, w_up=g_up, w_down=g_down.reshape(D_FF, D_MODEL))
    conv_w_full = g_cw[:, :3, :].transpose(1, 0, 2).reshape(3, 2 * D_FF)

    gains = dict(mix_pre=norm_mix_pre, mix_post=norm_mix_post, mem=norm_mem, xa_pre=norm_xa_pre,
                 xa_post=norm_xa_post, ffn_pre=norm_ffn_pre, ffn_post=norm_ffn_post)
    loss, grad_x, big, small = _local_step(x[0], mem[0], loss_target[0], gains, b_forget, w_pool[0], pool_scale[0],
                                           conv_w_full, conv_b, wts)

    d_cw = jnp.pad(small.pop("conv_w"), ((0, 5), (0, 0))).reshape(8, NDEV, D_MODEL).transpose(1, 0, 2)
    full = [big["w_in"].reshape(NDEV, D_MODEL // NDEV, D_IN_PAD), big["w_mix"].reshape(NDEV, D_MODEL // NDEV, D_MODEL),
            big["w_xq"].reshape(NDEV, D_MODEL // NDEV, D_MODEL), big["w_xo"].reshape(NDEV, D_MODEL // NDEV, D_MODEL),
            big["w_xkv"], big["w_up"], big["w_down"].reshape(NDEV, D_FF // NDEV, D_MODEL), d_cw]
    parts = _scatter_blocks(full, "scatter_grads")
    small_like = dict(mix_pre=norm_mix_pre, mix_post=norm_mix_post, mem=norm_mem, xa_pre=norm_xa_pre,
                      xa_post=norm_xa_post, ffn_pre=norm_ffn_pre, ffn_post=norm_ffn_post, conv_b=conv_b,
                      w_pool=w_pool, pool_scale=pool_scale, b_forget=b_forget)
    small = {k: small[k].reshape(small_like[k].shape) for k in SMALL_ORDER}
    (small_parts,) = _all_gather([_pack_small(small)], "gather_small_grads")

    res = {}
    for k, p in zip(big_names, parts[:7]):
        if k == "w_in":
            p = p[:, :, :w_in.shape[2]]
        res[k] = [a[None] for a in _adamw(p, w[k][0], mo[k][0], vo[k][0], "adamw_" + k)]
    pad_cw = lambda a: jnp.pad(a[0, :, 0, :], ((0, 5), (0, 0)))
    res["conv_w"] = [a[:3][None, :, None, :] for a in
                     _adamw(parts[7], pad_cw(conv_w), pad_cw(m_conv_w), pad_cw(v_conv_w), "adamw_conv_w")]
    key_of = dict(mix_pre="norm_mix_pre", mix_post="norm_mix_post", mem="norm_mem", xa_pre="norm_xa_pre",
                  xa_post="norm_xa_post", ffn_pre="norm_ffn_pre", ffn_post="norm_ffn_post", conv_b="conv_b",
                  w_pool="w_pool", pool_scale="pool_scale", b_forget="b_forget")
    pack_of = lambda src: _pack_small({k: src[key_of[k]] for k in SMALL_ORDER})
    small_out = _adamw(small_parts, pack_of(w), pack_of(mo), pack_of(vo), "adamw_small")
    small_out = [_unpack_small(a, small_like) for a in small_out]
    for k in SMALL_ORDER:
        res[key_of[k]] = [so[k] for so in small_out]

    total = lax.psum(loss[0, 0], ("x", "y", "c"))
    outs = [total, grad_x[None]]
    for idx in range(4):
        outs += [res[k][idx] for k in names]
    return tuple(outs)
```

```python
import functools
import math

import jax
import jax.numpy as jnp
from jax import lax
from jax.experimental import pallas as pl
from jax.experimental.pallas import tpu as pltpu

F32 = jnp.float32
BF16 = jnp.bfloat16

NDEV = 8
D_MODEL = 1024
D_POOL = 256
D_FOX = 768
FOX_HEADS = 12
HEAD_PAIRS = FOX_HEADS // 2
XA_HEADS = 4
XA_DIM = 256
D_FF = 4096
D_IN_PAD = 2688
F_COL = 2560
POOL_HALO = 16
NORM_EPS = 1e-6
NEG = -1e30

ADAM_LR = 0.001
ADAM_B1 = 0.9
ADAM_B2 = 0.999
ADAM_EPS = 1e-08
ADAM_WD = 0.01
ADAM_STEP = 10

TM = 512
TQ = 512
TN_FF = 512
VMEM_LIMIT = 48 * 1024 * 1024
MESH = pl.DeviceIdType.MESH


def _cp(*sem):
    return pltpu.CompilerParams(dimension_semantics=sem, vmem_limit_bytes=VMEM_LIMIT)


def _dot(a, b, dims):
    return lax.dot_general(a, b, (dims, ((), ())), preferred_element_type=F32)


NN = ((1,), (0,))
NT = ((1,), (1,))
TN = ((0,), (0,))


def _mm(a, b, mode, out_dtype, tm, tn, tk, name, b_cols=None, out_cols=None):
    a_list = list(a) if isinstance(a, (list, tuple)) else [a]
    b_list = list(b) if isinstance(b, (list, tuple)) else [b]
    assert len(a_list) == 1 or len(b_list) == 1
    if mode == "tn":
        K, M = a_list[0].shape
        assert len(a_list) == 1
        Ns = [x.shape[1] for x in b_list]
        N = sum(Ns)
        assert b_cols is None
    else:
        assert len(b_list) == 1
        M = a_list[0].shape[0]
        Ks = [x.shape[1] for x in a_list]
        K = sum(Ks)
        if b_cols is None:
            N = b_list[0].shape[0] if mode == "nt" else b_list[0].shape[1]
        else:
            N = b_list[0].shape[1] if mode == "nt" else NDEV * b_cols
    assert M % tm == 0 and N % tn == 0 and K % tk == 0, (name, M, N, K)
    grid = (M // tm, N // tn, K // tk)
    nk = grid[2]
    dims = {"nn": NN, "nt": NT, "tn": TN}[mode]

    in_specs = []
    if mode == "tn":
        in_specs.append(pl.BlockSpec((tk, tm), lambda i, j, k: (k, i)))
        if len(b_list) == 1:
            in_specs.append(pl.BlockSpec((tk, tn), lambda i, j, k: (k, j)))
        else:
            nj1 = Ns[0] // tn
            in_specs.append(pl.BlockSpec((tk, tn), lambda i, j, k: (k, jnp.minimum(j, nj1 - 1))))
            in_specs.append(pl.BlockSpec((tk, tn), lambda i, j, k: (k, jnp.maximum(j - nj1, 0))))
    else:
        if len(a_list) == 1:
            in_specs.append(pl.BlockSpec((tm, tk), lambda i, j, k: (i, k)))
        else:
            nk1 = Ks[0] // tk
            in_specs.append(pl.BlockSpec((tm, tk), lambda i, j, k: (i, jnp.minimum(k, nk1 - 1))))
            in_specs.append(pl.BlockSpec((tm, tk), lambda i, j, k: (i, jnp.maximum(k - nk1, 0))))
        if b_cols is None:
            if mode == "nn":
                in_specs.append(pl.BlockSpec((tk, tn), lambda i, j, k: (k, j)))
            else:
                in_specs.append(pl.BlockSpec((tn, tk), lambda i, j, k: (j, k)))
        else:
            if mode == "nn":
                per = b_cols // tn
                in_specs.append(pl.BlockSpec((None, tk, tn), lambda i, j, k: (j // per, k, j % per)))
            else:
                per = b_cols // tk
                in_specs.append(pl.BlockSpec((None, tn, tk), lambda i, j, k: (k // per, j, k % per)))
    if out_cols is None:
        out_spec = pl.BlockSpec((tm, tn), lambda i, j, k: (i, j))
        out_shape = jax.ShapeDtypeStruct((M, N), out_dtype)
    else:
        pero = out_cols // tn
        out_spec = pl.BlockSpec((None, tm, tn), lambda i, j, k: (j // pero, i, j % pero))
        out_shape = jax.ShapeDtypeStruct((NDEV, M, out_cols), out_dtype)

    two_a = len(a_list) == 2
    two_b = len(b_list) == 2

    def body(*refs):
        o_ref, acc_ref = refs[-2], refs[-1]
        j = pl.program_id(1)
        k = pl.program_id(2)

        @pl.when(k == 0)
        def _():
            acc_ref[...] = jnp.zeros_like(acc_ref)

        if two_a:
            a1, a2, b1 = refs[0], refs[1], refs[2]
            nk1_ = Ks[0] // tk

            @pl.when(k < nk1_)
            def _():
                acc_ref[...] += _dot(a1[...], b1[...], dims)

            @pl.when(k >= nk1_)
            def _():
                acc_ref[...] += _dot(a2[...], b1[...], dims)
        elif two_b:
            a1, b1, b2 = refs[0], refs[1], refs[2]
            nj1_ = Ns[0] // tn

            @pl.when(j < nj1_)
            def _():
                acc_ref[...] += _dot(a1[...], b1[...], dims)

            @pl.when(j >= nj1_)
            def _():
                acc_ref[...] += _dot(a1[...], b2[...], dims)
        else:
            acc_ref[...] += _dot(refs[0][...], refs[1][...], dims)

        @pl.when(k == nk - 1)
        def _():
            o_ref[...] = acc_ref[...].astype(o_ref.dtype)

    return pl.pallas_call(
        body, name=name, grid=grid, in_specs=in_specs, out_specs=out_spec, out_shape=out_shape,
        scratch_shapes=[pltpu.VMEM((tm, tn), F32)],
        compiler_params=_cp("parallel", "parallel", "arbitrary"),
    )(*a_list, *b_list)


def _rstd(x):
    return lax.rsqrt(jnp.mean(x * x, axis=-1, keepdims=True) + NORM_EPS)


def _norm_bwd_rows(dxn, xn, r):
    return r * (dxn - xn * jnp.mean(dxn * xn, axis=-1, keepdims=True))


def _row_spec(tm, d):
    return pl.BlockSpec((tm, d), lambda i: (i, 0))


def _vec_spec(d):
    return pl.BlockSpec((1, d), lambda i: (0, 0))


def _norm_fwd(x, g, name):
    s, d = x.shape
    tm = min(TM, s)

    def body(x_ref, g_ref, h_ref):
        xv = x_ref[...]
        h_ref[...] = (xv * _rstd(xv) * g_ref[...]).astype(BF16)

    return pl.pallas_call(
        body, name=name, grid=(s // tm,), in_specs=[_row_spec(tm, d), _vec_spec(d)],
        out_specs=_row_spec(tm, d), out_shape=jax.ShapeDtypeStruct((s, d), BF16),
        compiler_params=_cp("parallel"),
    )(x, g)


def _resid_norm_fwd(x_in, y, g_post, g_next, name):
    s, d = x_in.shape

    def body(x_ref, y_ref, gp_ref, gn_ref, xo_ref, h_ref):
        yv = y_ref[...]
        xo = x_ref[...] + yv * _rstd(yv) * gp_ref[...]
        xo_ref[...] = xo
        h_ref[...] = (xo * _rstd(xo) * gn_ref[...]).astype(BF16)

    return pl.pallas_call(
        body, name=name, grid=(s // TM,),
        in_specs=[_row_spec(TM, d), _row_spec(TM, d), _vec_spec(d), _vec_spec(d)],
        out_specs=[_row_spec(TM, d), _row_spec(TM, d)],
        out_shape=[jax.ShapeDtypeStruct((s, d), F32), jax.ShapeDtypeStruct((s, d), BF16)],
        compiler_params=_cp("parallel"),
    )(x_in, y, g_post, g_next)


def _norm_bwd(dh, x, dx_res, g_pre, name, prev=None):
    s, d = x.shape
    tm = min(TM, s)
    has_prev = prev is not None

    def body(*refs):
        if has_prev:
            dh_ref, x_ref, dr_ref, g_ref, y_ref, gp_ref, dx_ref, dg_ref, dy_ref, dgp_ref = refs
        else:
            dh_ref, x_ref, dr_ref, g_ref, dx_ref, dg_ref = refs
        i = pl.program_id(0)
        xv = x_ref[...]
        r = _rstd(xv)
        xn = xv * r
        dhv = dh_ref[...].astype(F32)
        dx = dr_ref[...] + _norm_bwd_rows(dhv * g_ref[...], xn, r)
        dx_ref[...] = dx
        dg = jnp.sum(dhv * xn, axis=0, keepdims=True)

        @pl.when(i == 0)
        def _():
            dg_ref[...] = dg

        @pl.when(i > 0)
        def _():
            dg_ref[...] += dg

        if has_prev:
            yv = y_ref[...]
            r2 = _rstd(yv)
            yn = yv * r2
            dy_ref[...] = _norm_bwd_rows(dx * gp_ref[...], yn, r2).astype(BF16)
            dgp = jnp.sum(dx * yn, axis=0, keepdims=True)

            @pl.when(i == 0)
            def _():
                dgp_ref[...] = dgp

            @pl.when(i > 0)
            def _():
                dgp_ref[...] += dgp

    in_specs = [_row_spec(tm, d), _row_spec(tm, d), _row_spec(tm, d), _vec_spec(d)]
    out_specs = [_row_spec(tm, d), _vec_spec(d)]
    out_shape = [jax.ShapeDtypeStruct((s, d), F32), jax.ShapeDtypeStruct((1, d), F32)]
    args = [dh, x, dx_res, g_pre]
    if has_prev:
        in_specs += [_row_spec(tm, d), _vec_spec(d)]
        out_specs += [_row_spec(tm, d), _vec_spec(d)]
        out_shape += [jax.ShapeDtypeStruct((s, d), BF16), jax.ShapeDtypeStruct((1, d), F32)]
        args += list(prev)
    return pl.pallas_call(
        body, name=name, grid=(s // tm,), in_specs=in_specs, out_specs=out_specs, out_shape=out_shape,
        compiler_params=_cp("arbitrary"),
    )(*args)


def _loss_bwd(x2, y3, g_post, tgt, name):
    s, d = x2.shape

    def body(x_ref, y_ref, g_ref, t_ref, loss_ref, dx_ref, dy_ref, dg_ref):
        i = pl.program_id(0)
        yv = y_ref[...]
        r = _rstd(yv)
        yn = yv * r
        e = x_ref[...] + yn * g_ref[...] - t_ref[...]
        part = 0.5 * jnp.sum(jnp.mean(e * e, axis=-1, keepdims=True), axis=0, keepdims=True)
        dx = e * (1.0 / d)
        dx_ref[...] = dx
        dy_ref[...] = _norm_bwd_rows(dx * g_ref[...], yn, r).astype(BF16)
        dg = jnp.sum(dx * yn, axis=0, keepdims=True)
        part = jnp.broadcast_to(part, (1, 128))

        @pl.when(i == 0)
        def _():
            dg_ref[...] = dg
            loss_ref[...] = part

        @pl.when(i > 0)
        def _():
            dg_ref[...] += dg
            loss_ref[...] += part

    return pl.pallas_call(
        body, name=name, grid=(s // TM,),
        in_specs=[_row_spec(TM, d), _row_spec(TM, d), _vec_spec(d), _row_spec(TM, d)],
        out_specs=[_vec_spec(128), _row_spec(TM, d), _row_spec(TM, d), _vec_spec(d)],
        out_shape=[jax.ShapeDtypeStruct((1, 128), F32), jax.ShapeDtypeStruct((s, d), F32),
                   jax.ShapeDtypeStruct((s, d), BF16), jax.ShapeDtypeStruct((1, d), F32)],
        compiler_params=_cp("arbitrary"),
    )(x2, y3, g_post, tgt)


def _split3(v):
    hi = v.astype(BF16)
    r1 = v - hi.astype(F32)
    mid = r1.astype(BF16)
    lo = (r1 - mid.astype(F32)).astype(BF16)
    return hi, mid, lo


def _tri_dot(tri, v):
    hi, mid, lo = _split3(v)
    return _dot(tri, hi, NN) + _dot(tri, mid, NN) + _dot(tri, lo, NN)


def _gate_cumsum(fraw, b_pad, name):
    s = fraw.shape[0]

    def body(f_ref, b_ref, flog_ref, cum_ref, carry_ref):
        i = pl.program_id(0)

        @pl.when(i == 0)
        def _():
            carry_ref[...] = jnp.zeros_like(carry_ref)

        flog = f_ref[...] + b_ref[...]
        flog_ref[...] = flog
        lf = jnp.minimum(flog, 0.0) - jnp.log(1.0 + jnp.exp(-jnp.abs(flog)))
        lane = lax.broadcasted_iota(jnp.int32, (1, 128), 1)
        lf = jnp.where(lane < FOX_HEADS, lf, 0.0)
        row = lax.broadcasted_iota(jnp.int32, (TM, TM), 0)
        col = lax.broadcasted_iota(jnp.int32, (TM, TM), 1)
        tri = (row >= col).astype(BF16)
        cum = _tri_dot(tri, lf) + carry_ref[...]
        cum_ref[...] = cum
        carry_ref[...] = cum[TM - 1:TM, :]

    return pl.pallas_call(
        body, name=name, grid=(s // TM,),
        in_specs=[_row_spec(TM, 128), _vec_spec(128)],
        out_specs=[_row_spec(TM, 128), _row_spec(TM, 128)],
        out_shape=[jax.ShapeDtypeStruct((s, 128), F32), jax.ShapeDtypeStruct((s, 128), F32)],
        scratch_shapes=[pltpu.VMEM((1, 128), F32)],
        compiler_params=_cp("arbitrary"),
    )(fraw, b_pad)


def _gate_bwd(qaux, kaux, flog, name):
    s = flog.shape[0]
    n = s // TM

    def body(qa_ref, ka_ref, fl_ref, dp_ref, db_ref, carry_ref):
        i = pl.program_id(0)

        @pl.when(i == 0)
        def _():
            carry_ref[...] = jnp.zeros_like(carry_ref)

        src = lax.broadcasted_iota(jnp.int32, (128, 128), 0)
        dst = lax.broadcasted_iota(jnp.int32, (128, 128), 1)
        dcum = jnp.zeros((TM, 128), F32)
        for p in range(HEAD_PAIRS):
            for ref, l0, l1, sign in ((qa_ref, 64, 0, 1.0), (ka_ref, 67, 3, -1.0)):
                hit = jnp.logical_or(jnp.logical_and(src == l0, dst == 2 * p),
                                     jnp.logical_and(src == l1, dst == 2 * p + 1))
                sel = jnp.where(hit, sign, 0.0).astype(BF16)
                for piece in _split3(ref[p]):
                    dcum = dcum + _dot(piece, sel, NN)
        row = lax.broadcasted_iota(jnp.int32, (TM, TM), 0)
        col = lax.broadcasted_iota(jnp.int32, (TM, TM), 1)
        tri = (row <= col).astype(BF16)
        dlf = _tri_dot(tri, dcum) + carry_ref[...]
        carry_ref[...] = dlf[0:1, :]
        lane = lax.broadcasted_iota(jnp.int32, (1, 128), 1)
        df = jnp.where(lane < FOX_HEADS, dlf / (1.0 + jnp.exp(fl_ref[...])), 0.0)
        dp_ref[...] = df.astype(BF16)
        db = jnp.sum(df, axis=0, keepdims=True)

        @pl.when(i == 0)
        def _():
            db_ref[...] = db

        @pl.when(i > 0)
        def _():
            db_ref[...] += db

    rev = lambda i: (n - 1 - i, 0)
    return pl.pallas_call(
        body, name=name, grid=(n,),
        in_specs=[pl.BlockSpec((HEAD_PAIRS, TM, 128), lambda i: (0, n - 1 - i, 0)),
                  pl.BlockSpec((HEAD_PAIRS, TM, 128), lambda i: (0, n - 1 - i, 0)), pl.BlockSpec((TM, 128), rev)],
        out_specs=[pl.BlockSpec((TM, 128), rev), _vec_spec(128)],
        out_shape=[jax.ShapeDtypeStruct((s, 128), BF16), jax.ShapeDtypeStruct((1, 128), F32)],
        scratch_shapes=[pltpu.VMEM((1, 128), F32)],
        compiler_params=_cp("arbitrary"),
    )(qaux, kaux, flog)


def _pool_consts(i, rows):
    lane = lax.broadcasted_iota(jnp.int32, (rows, D_POOL), 1)
    t1 = lax.broadcasted_iota(jnp.int32, (rows, D_POOL), 0) + i * TM + 1
    win = jnp.where(lane < 64, 2, jnp.where(lane < 128, 4, jnp.where(lane < 192, 8, 16)))
    inv = 1.0 / jnp.minimum(t1, win).astype(F32)
    return lane, inv


def _by_group(lane, s2, s4, s8, s16):
    return jnp.where(lane < 64, s2, jnp.where(lane < 128, s4, jnp.where(lane < 192, s8, s16)))


def _pool_diff(i, u_ref, halo_ref):
    u = u_ref[...].astype(F32)
    halo = jnp.where(i > 0, halo_ref[...].astype(F32), 0.0)
    ext = jnp.concatenate([halo, u], axis=0)
    s2 = ext + pltpu.roll(ext, 1, 0)
    s4 = s2 + pltpu.roll(s2, 2, 0)
    s8 = s4 + pltpu.roll(s4, 4, 0)
    s16 = s8 + pltpu.roll(s8, 8, 0)
    lane, inv = _pool_consts(i, TM)
    sel = _by_group(lane, s2[POOL_HALO:], s4[POOL_HALO:], s8[POOL_HALO:], s16[POOL_HALO:])
    return sel * inv - u


def _pool_fwd(proj, wbd, scale, ycat, name):
    s = proj.shape[0]
    hb = TM // POOL_HALO

    def body(u_ref, halo_ref, w_ref, sc_ref, y_any, y_ref):
        del y_any
        i = pl.program_id(0)
        diff = _pool_diff(i, u_ref, halo_ref)
        mixed = _dot(diff.astype(BF16), w_ref[...], NN)
        y_ref[...] = (mixed * sc_ref[...]).astype(BF16)

    return pl.pallas_call(
        body, name=name, grid=(s // TM,),
        in_specs=[pl.BlockSpec((TM, D_POOL), lambda i: (i, 0)),
                  pl.BlockSpec((POOL_HALO, D_POOL), lambda i: (jnp.maximum(i * hb - 1, 0), 0)),
                  pl.BlockSpec((D_POOL, D_POOL), lambda i: (0, 0)), _vec_spec(D_POOL),
                  pl.BlockSpec(memory_space=pl.ANY)],
        out_specs=pl.BlockSpec((TM, D_POOL), lambda i: (i, 0)),
        out_shape=jax.ShapeDtypeStruct(ycat.shape, ycat.dtype),
        input_output_aliases={4: 0},
        compiler_params=_cp("parallel"),
    )(proj, proj, wbd, scale, ycat)


def _pool_bwd(proj, dycat, wbd, scale, name):
    s = proj.shape[0]
    n = s // TM
    hb = TM // POOL_HALO
    last_halo = s // POOL_HALO - 1

    def body(u_ref, halo_ref, dy_ref, dyp_ref, w_ref, sc_ref, dp_ref, dw_ref, dsc_ref):
        i = pl.program_id(0)
        diff = _pool_diff(i, u_ref, halo_ref)
        diff_b = diff.astype(BF16)
        mixed = _dot(diff_b, w_ref[...], NN)
        dy = dy_ref[...].astype(F32)
        dmix = (dy * sc_ref[...]).astype(BF16)
        dyp = jnp.where(i < n - 1, dyp_ref[...].astype(F32), 0.0)
        dmix_p = (dyp * sc_ref[...]).astype(BF16)
        dd = _dot(dmix, w_ref[...], NT)
        dd_p = _dot(dmix_p, w_ref[...], NT)
        lane, inv = _pool_consts(i, TM)
        _, inv_p = _pool_consts(i + 1, POOL_HALO)
        ext = jnp.concatenate([dd * inv, dd_p * inv_p], axis=0)
        rows = TM + POOL_HALO
        l2 = ext + pltpu.roll(ext, rows - 1, 0)
        l4 = l2 + pltpu.roll(l2, rows - 2, 0)
        l8 = l4 + pltpu.roll(l4, rows - 4, 0)
        l16 = l8 + pltpu.roll(l8, rows - 8, 0)
        du = _by_group(lane, l2[:TM], l4[:TM], l8[:TM], l16[:TM]) - dd
        dp_ref[...] = du.astype(BF16)
        dw = _dot(diff_b, dmix, TN)
        dsc = jnp.sum(dy * mixed, axis=0, keepdims=True)

        @pl.when(i == 0)
        def _():
            dw_ref[...] = dw
            dsc_ref[...] = dsc

        @pl.when(i > 0)
        def _():
            dw_ref[...] += dw
            dsc_ref[...] += dsc

    return pl.pallas_call(
        body, name=name, grid=(n,),
        in_specs=[pl.BlockSpec((TM, D_POOL), lambda i: (i, 0)),
                  pl.BlockSpec((POOL_HALO, D_POOL), lambda i: (jnp.maximum(i * hb - 1, 0), 0)),
                  pl.BlockSpec((TM, D_POOL), lambda i: (i, 0)),
                  pl.BlockSpec((POOL_HALO, D_POOL), lambda i: (jnp.minimum((i + 1) * hb, last_halo), 0)),
                  pl.BlockSpec((D_POOL, D_POOL), lambda i: (0, 0)), _vec_spec(D_POOL)],
        out_specs=[pl.BlockSpec((TM, D_POOL), lambda i: (i, 0)),
                   pl.BlockSpec((D_POOL, D_POOL), lambda i: (0, 0)), _vec_spec(D_POOL)],
        out_shape=[jax.ShapeDtypeStruct((s, D_POOL), BF16),
                   jax.ShapeDtypeStruct((D_POOL, D_POOL), F32), jax.ShapeDtypeStruct((1, D_POOL), F32)],
        compiler_params=_cp("arbitrary"),
    )(proj, proj, dycat, dycat, wbd, scale)


Q_BLK = D_POOL // 128
K_BLK = Q_BLK + D_FOX // 128
V_BLK = K_BLK + D_FOX // 128


def _head_masks():
    lane = lax.broadcasted_iota(jnp.int32, (1, 128), 1)
    return [lane < 64, lane >= 64]


def _fox_scores(qh, k2, cq_col, ck_row, row_off):
    sc = _dot(qh, k2, NT) * 0.125 + cq_col - ck_row
    row = lax.broadcasted_iota(jnp.int32, sc.shape, 0) + row_off
    col = lax.broadcasted_iota(jnp.int32, sc.shape, 1)
    return jnp.where(row >= col, sc, NEG)


def _fox_fwd(proj, cum_c, cum_r, name):
    s = proj.shape[0]
    nq = s // TQ

    def body(q_ref, k_ref, v_ref, cq_ref, ck_ref, o_ref, lse_ref, m_ref, l_ref, acc_ref):
        qi, ki = pl.program_id(1), pl.program_id(2)
        masks = _head_masks()

        @pl.when(ki == 0)
        def _():
            m_ref[...] = jnp.full_like(m_ref, NEG)
            l_ref[...] = jnp.zeros_like(l_ref)
            acc_ref[...] = jnp.zeros_like(acc_ref)

        @pl.when(ki <= qi)
        def _():
            q2, k2, v2 = q_ref[...], k_ref[...], v_ref[...]
            cq, ck = cq_ref[...], ck_ref[...]
            pv = []
            alpha = []
            for hh in range(2):
                qh = jnp.where(masks[hh], q2, jnp.zeros_like(q2))
                vh = jnp.where(masks[hh], v2, jnp.zeros_like(v2))
                sc = _fox_scores(qh, k2, cq[:, hh:hh + 1], ck[hh:hh + 1, :], (qi - ki) * TQ)
                m_prev = m_ref[hh]
                m_new = jnp.maximum(m_prev, jnp.max(sc, axis=1, keepdims=True))
                a = jnp.exp(m_prev - m_new)
                p = jnp.exp(sc - m_new)
                l_ref[hh] = a * l_ref[hh] + jnp.sum(p, axis=1, keepdims=True)
                m_ref[hh] = m_new
                pv.append(_dot(p.astype(BF16), vh, NN))
                alpha.append(a)
            acc_ref[...] = acc_ref[...] * jnp.where(masks[0], alpha[0], alpha[1]) + pv[0] + pv[1]

        @pl.when(ki == qi)
        def _():
            inv = jnp.where(masks[0], 1.0 / l_ref[0], 1.0 / l_ref[1])
            o_ref[...] = (acc_ref[...] * inv).astype(BF16)
            lane = lax.broadcasted_iota(jnp.int32, (1, 128), 1)
            lse_ref[...] = jnp.where(lane == 0, m_ref[0] + jnp.log(l_ref[0]), m_ref[1] + jnp.log(l_ref[1]))

    kv_row = lambda p, qi, ki: jnp.minimum(ki, qi)
    return pl.pallas_call(
        body, name=name, grid=(HEAD_PAIRS, nq, nq),
        in_specs=[pl.BlockSpec((TQ, 128), lambda p, qi, ki: (qi, Q_BLK + p)),
                  pl.BlockSpec((TQ, 128), lambda p, qi, ki: (kv_row(p, qi, ki), K_BLK + p)),
                  pl.BlockSpec((TQ, 128), lambda p, qi, ki: (kv_row(p, qi, ki), V_BLK + p)),
                  pl.BlockSpec((None, TQ, 128), lambda p, qi, ki: (p, qi, 0)),
                  pl.BlockSpec((None, 8, TQ), lambda p, qi, ki: (p, 0, kv_row(p, qi, ki)))],
        out_specs=[pl.BlockSpec((TQ, 128), lambda p, qi, ki: (qi, Q_BLK + p)),
                   pl.BlockSpec((None, TQ, 128), lambda p, qi, ki: (p, qi, 0))],
        out_shape=[jax.ShapeDtypeStruct((s, D_MODEL), BF16), jax.ShapeDtypeStruct((HEAD_PAIRS, s, 128), F32)],
        scratch_shapes=[pltpu.VMEM((2, TQ, 1), F32), pltpu.VMEM((2, TQ, 1), F32), pltpu.VMEM((TQ, 128), F32)],
        compiler_params=_cp("parallel", "parallel", "arbitrary"),
    )(proj, proj, proj, cum_c, cum_r)


def _fox_bwd(proj, ycat, dycat, cum_c, cum_r, lse, name):
    s = proj.shape[0]
    nq = s // TQ

    def body(q_ref, k_ref, v_ref, do_ref, o_ref, cq_ref, ck_ref, lse_ref,
             dq_ref, dk_ref, dv_ref, dc_ref, dcq_ref, dq_acc, dk_acc, dv_acc, dc_acc, dcq_acc):
        ki, qi = pl.program_id(1), pl.program_id(2)
        masks = _head_masks()
        lane = lax.broadcasted_iota(jnp.int32, (1, 128), 1)

        @pl.when(qi == ki)
        def _():
            dk_acc[...] = jnp.zeros_like(dk_acc)
            dv_acc[...] = jnp.zeros_like(dv_acc)
            dc_acc[...] = jnp.zeros_like(dc_acc)

        @pl.when(qi >= ki)
        def _():
            q2, k2, v2, do2 = q_ref[...], k_ref[...], v_ref[...], do_ref[...]
            cq, ck, lse2 = cq_ref[...], ck_ref[...], lse_ref[...]
            dd = do2.astype(F32) * o_ref[...].astype(F32)
            d0 = jnp.sum(jnp.where(masks[0], dd, 0.0), axis=1, keepdims=True)
            drow = [d0, jnp.sum(dd, axis=1, keepdims=True) - d0]
            dq = jnp.zeros((TQ, 128), F32)
            dk = jnp.zeros((TQ, 128), F32)
            dv = jnp.zeros((TQ, 128), F32)
            dcs = []
            rs = []
            for hh in range(2):
                zero = jnp.zeros_like(q2)
                qh = jnp.where(masks[hh], q2, zero)
                kh = jnp.where(masks[hh], k2, zero)
                doh = jnp.where(masks[hh], do2, zero)
                sc = _fox_scores(qh, k2, cq[:, hh:hh + 1], ck[hh:hh + 1, :], (qi - ki) * TQ)
                p = jnp.exp(sc - lse2[:, hh:hh + 1])
                dp = _dot(doh, v2, NT)
                ds = p * (dp - drow[hh])
                dsb = (ds * 0.125).astype(BF16)
                dv = dv + _dot(p.astype(BF16), doh, TN)
                dk = dk + _dot(dsb, qh, TN)
                dq = dq + _dot(dsb, kh, NN)
                dcs.append(-jnp.sum(ds, axis=0, keepdims=True))
                rs.append(jnp.sum(ds, axis=1, keepdims=True))
            dk_acc[...] += dk
            dv_acc[...] += dv
            dc_acc[0:1, :] += dcs[0]
            dc_acc[1:2, :] += dcs[1]
            dcq = jnp.where(lane == 0, rs[0], rs[1])

            @pl.when(ki == 0)
            def _():
                dq_acc[qi] = dq
                dcq_acc[qi] = dcq

            @pl.when(ki > 0)
            def _():
                dq_acc[qi] += dq
                dcq_acc[qi] += dcq

            @pl.when(qi == ki)
            def _():
                rows = pl.ds(pl.multiple_of(qi * TQ, TQ), TQ)
                dq_ref[rows, :] = dq_acc[qi].astype(BF16)
                dcq_ref[rows, :] = dcq_acc[qi]

        @pl.when(qi == nq - 1)
        def _():
            dk_ref[...] = dk_acc[...].astype(BF16)
            dv_ref[...] = dv_acc[...].astype(BF16)
            dc_ref[...] = dc_acc[...]

    q_row = lambda p, ki, qi: jnp.maximum(qi, ki)
    return pl.pallas_call(
        body, name=name, grid=(HEAD_PAIRS, nq, nq),
        in_specs=[pl.BlockSpec((TQ, 128), lambda p, ki, qi: (q_row(p, ki, qi), Q_BLK + p)),
                  pl.BlockSpec((TQ, 128), lambda p, ki, qi: (ki, K_BLK + p)),
                  pl.BlockSpec((TQ, 128), lambda p, ki, qi: (ki, V_BLK + p)),
                  pl.BlockSpec((TQ, 128), lambda p, ki, qi: (q_row(p, ki, qi), Q_BLK + p)),
                  pl.BlockSpec((TQ, 128), lambda p, ki, qi: (q_row(p, ki, qi), Q_BLK + p)),
                  pl.BlockSpec((None, TQ, 128), lambda p, ki, qi: (p, q_row(p, ki, qi), 0)),
                  pl.BlockSpec((None, 8, TQ), lambda p, ki, qi: (p, 0, ki)),
                  pl.BlockSpec((None, TQ, 128), lambda p, ki, qi: (p, q_row(p, ki, qi), 0))],
        out_specs=[pl.BlockSpec((s, 128), lambda p, ki, qi: (0, p)),
                   pl.BlockSpec((TQ, 128), lambda p, ki, qi: (ki, p)),
                   pl.BlockSpec((TQ, 128), lambda p, ki, qi: (ki, p)),
                   pl.BlockSpec((None, 8, TQ), lambda p, ki, qi: (p, 0, ki)),
                   pl.BlockSpec((None, s, 128), lambda p, ki, qi: (p, 0, 0))],
        out_shape=[jax.ShapeDtypeStruct((s, D_FOX), BF16)] * 3 + [jax.ShapeDtypeStruct((HEAD_PAIRS, 8, s), F32),
                                                                 jax.ShapeDtypeStruct((HEAD_PAIRS, s, 128), F32)],
        scratch_shapes=[pltpu.VMEM((nq, TQ, 128), F32), pltpu.VMEM((TQ, 128), F32), pltpu.VMEM((TQ, 128), F32),
                        pltpu.VMEM((8, TQ), F32), pltpu.VMEM((nq, TQ, 128), F32)],
        compiler_params=_cp("arbitrary", "arbitrary", "arbitrary"),
    )(proj, proj, proj, dycat, ycat, cum_c, cum_r, lse)


def _operand_lanes(v0, v1, ones_off):
    lane = lax.broadcasted_iota(jnp.int32, (1, 128), 1)
    out = jnp.zeros((v0.shape[0], 128), F32)
    if ones_off is not None:
        half = lane & 63
        out = out + jnp.where(jnp.logical_and(half >= ones_off, half < ones_off + 3), 1.0, 0.0)
    for base, v in ((64, v0), (0, v1)):
        for j, piece in enumerate(_split3(v)):
            out = jnp.where(lane == base + j, piece.astype(F32), out)
    return out.astype(BF16)


def _fox_operands(cum, name):
    s = cum.shape[0]
    width = HEAD_PAIRS * 128

    def body(c_ref, aq_ref, ak_ref):
        pieces = _split3(c_ref[...])
        row = lax.broadcasted_iota(jnp.int32, (128, width), 0)
        col = lax.broadcasted_iota(jnp.int32, (128, width), 1)
        base = (row >> 1) * 128 + (1 - (row & 1)) * 64
        half = lax.broadcasted_iota(jnp.int32, (1, width), 1) & 63
        for o_ref, off, sign, ones_off in ((aq_ref, 0, 1.0, 3), (ak_ref, 3, -1.0, 0)):
            out = jnp.where(jnp.logical_and(half >= ones_off, half < ones_off + 3), 1.0, 0.0)
            for j, piece in enumerate(pieces):
                sel = jnp.where(jnp.logical_and(col == base + off + j, row < FOX_HEADS), sign, 0.0).astype(BF16)
                out = out + _dot(piece, sel, NN)
            o_ref[...] = out.astype(BF16)

    return pl.pallas_call(
        body, name=name, grid=(s // TM,), in_specs=[_row_spec(TM, 128)],
        out_specs=[_row_spec(TM, width), _row_spec(TM, width)],
        out_shape=[jax.ShapeDtypeStruct((s, width), BF16)] * 2,
        compiler_params=_cp("parallel"),
    )(cum)


def _fox_do_operand(dycat, ycat, name):
    s = dycat.shape[0]

    def body(do_ref, o_ref, ad_ref):
        lane = lax.broadcasted_iota(jnp.int32, (1, 128), 1)
        dd = do_ref[...].astype(F32) * o_ref[...].astype(F32)
        d0 = jnp.sum(jnp.where(lane < 64, dd, 0.0), axis=1, keepdims=True)
        d1 = jnp.sum(dd, axis=1, keepdims=True) - d0
        ad_ref[...] = _operand_lanes(-d0, -d1, None)

    blk = pl.BlockSpec((TM, 128), lambda i, p: (i, Q_BLK + p))
    return pl.pallas_call(
        body, name=name, grid=(s // TM, HEAD_PAIRS), in_specs=[blk, blk],
        out_specs=pl.BlockSpec((TM, 128), lambda i, p: (i, p)),
        out_shape=jax.ShapeDtypeStruct((s, HEAD_PAIRS * 128), BF16),
        compiler_params=_cp("parallel", "parallel"),
    )(dycat, ycat)


def _causal_pairs(nq, key_major):
    if key_major:
        pairs = [(q, k) for k in range(nq) for q in range(k, nq)]
    else:
        pairs = [(q, k) for q in range(nq) for k in range(q + 1)]
    return (jnp.asarray([p[0] for p in pairs], jnp.int32), jnp.asarray([p[1] for p in pairs], jnp.int32))


def _diag_mask(sc):
    row = lax.broadcasted_iota(jnp.int32, sc.shape, 0)
    col = lax.broadcasted_iota(jnp.int32, sc.shape, 1)
    return jnp.where(row >= col, sc, NEG)


def _fox_fwd2(proj, aq, ak, name):
    s = proj.shape[0]
    nq = s // TQ
    qi_arr, ki_arr = _causal_pairs(nq, key_major=False)

    def body(qi_ref, ki_ref, q_ref, k_ref, v_ref, aq_ref, ak_ref, o_ref, aqb_ref, m_ref, acc_ref, aux_ref):
        t = pl.program_id(1)
        qi, ki = qi_ref[t], ki_ref[t]
        lane = lax.broadcasted_iota(jnp.int32, (1, 128), 1)
        masks = [lane < 64, lane >= 64]
        ones_v = jnp.where((lane & 63) == 8, 1.0, 0.0).astype(BF16)

        @pl.when(ki == 0)
        def _():
            m_ref[...] = jnp.full_like(m_ref, NEG)
            acc_ref[...] = jnp.zeros_like(acc_ref)
            aux_ref[...] = jnp.zeros_like(aux_ref)

        def step(diag):
            q2s = q_ref[...] * 0.125
            k2, v2, aq2, ak2 = k_ref[...], v_ref[...], aq_ref[...], ak_ref[...]
            pv, alpha = [], []
            for hh in range(2):
                qh = jnp.where(masks[hh], q2s, aq2)
                kh = jnp.where(masks[hh], k2, ak2)
                vh = jnp.where(masks[hh], v2, ones_v)
                sc = _dot(qh, kh, NT)
                if diag:
                    sc = _diag_mask(sc)
                m_prev = m_ref[hh]
                m_new = jnp.maximum(m_prev, jnp.max(sc, axis=1, keepdims=True))
                m_ref[hh] = m_new
                alpha.append(jnp.exp(m_prev - m_new))
                pv.append(_dot(jnp.exp(sc - m_new).astype(BF16), vh, NN))
            acc_ref[...] = acc_ref[...] * jnp.where(masks[0], alpha[0], alpha[1]) + jnp.where(masks[0], pv[0], pv[1])
            aux_ref[...] = aux_ref[...] * jnp.where(masks[0], alpha[1], alpha[0]) + jnp.where(masks[0], pv[1], pv[0])

        @pl.when(ki < qi)
        def _():
            step(False)

        @pl.when(ki == qi)
        def _():
            step(True)
            aux = aux_ref[...]
            l0, l1 = aux[:, 72:73], aux[:, 8:9]
            o_ref[...] = (acc_ref[...] * jnp.where(masks[0], 1.0 / l0, 1.0 / l1)).astype(BF16)
            aqf = aq_ref[...].astype(F32)
            cum0 = aqf[:, 64:65] + aqf[:, 65:66] + aqf[:, 66:67]
            cum1 = aqf[:, 0:1] + aqf[:, 1:2] + aqf[:, 2:3]
            aqb_ref[...] = _operand_lanes(cum0 - (m_ref[0] + jnp.log(l0)), cum1 - (m_ref[1] + jnp.log(l1)), 3)

    grid_spec = pltpu.PrefetchScalarGridSpec(
        num_scalar_prefetch=2, grid=(HEAD_PAIRS, int(qi_arr.shape[0])),
        in_specs=[pl.BlockSpec((TQ, 128), lambda p, t, qi, ki: (qi[t], Q_BLK + p)),
                  pl.BlockSpec((TQ, 128), lambda p, t, qi, ki: (ki[t], K_BLK + p)),
                  pl.BlockSpec((TQ, 128), lambda p, t, qi, ki: (ki[t], V_BLK + p)),
                  pl.BlockSpec((TQ, 128), lambda p, t, qi, ki: (qi[t], p)),
                  pl.BlockSpec((TQ, 128), lambda p, t, qi, ki: (ki[t], p))],
        out_specs=[pl.BlockSpec((TQ, 128), lambda p, t, qi, ki: (qi[t], Q_BLK + p)),
                   pl.BlockSpec((TQ, 128), lambda p, t, qi, ki: (qi[t], p))],
        scratch_shapes=[pltpu.VMEM((2, TQ, 1), F32), pltpu.VMEM((TQ, 128), F32), pltpu.VMEM((TQ, 128), F32)])
    return pl.pallas_call(
        body, name=name, grid_spec=grid_spec,
        out_shape=[jax.ShapeDtypeStruct((s, D_MODEL), BF16), jax.ShapeDtypeStruct((s, HEAD_PAIRS * 128), BF16)],
        compiler_params=_cp("parallel", "arbitrary"),
    )(qi_arr, ki_arr, proj, proj, proj, aq, ak)


def _fox_bwd2(proj, dycat, aqb, ak, ad, name):
    s = proj.shape[0]
    nq = s // TQ
    qi_arr, ki_arr = _causal_pairs(nq, key_major=True)

    def body(qi_ref, ki_ref, q_ref, k_ref, v_ref, do_ref, aq_ref, ak_ref, ad_ref,
             dq_ref, dk_ref, dv_ref, qaux_ref, kaux_ref, dq_acc, qaux_acc, dk_acc, dv_acc, kaux_acc):
        t = pl.program_id(1)
        qi, ki = qi_ref[t], ki_ref[t]
        lane = lax.broadcasted_iota(jnp.int32, (1, 128), 1)
        masks = [lane < 64, lane >= 64]
        ones_v = jnp.where((lane & 63) < 3, 1.0, 0.0).astype(BF16)

        @pl.when(qi == ki)
        def _():
            dk_acc[...] = jnp.zeros_like(dk_acc)
            dv_acc[...] = jnp.zeros_like(dv_acc)
            kaux_acc[...] = jnp.zeros_like(kaux_acc)

        def step(diag):
            q2s = q_ref[...] * 0.125
            k2, v2, do2 = k_ref[...], v_ref[...], do_ref[...]
            aq2, ak2, ad2 = aq_ref[...], ak_ref[...], ad_ref[...]
            dq, dk, dv = [], [], []
            for hh in range(2):
                qh = jnp.where(masks[hh], q2s, aq2)
                kh = jnp.where(masks[hh], k2, ak2)
                doh = jnp.where(masks[hh], do2, ad2)
                vh = jnp.where(masks[hh], v2, ones_v)
                sc = _dot(qh, kh, NT)
                if diag:
                    sc = _diag_mask(sc)
                p = jnp.exp(sc)
                dsb = (p * _dot(doh, vh, NT)).astype(BF16)
                dv.append(_dot(p.astype(BF16), doh, TN))
                dk.append(_dot(dsb, qh, TN))
                dq.append(_dot(dsb, kh, NN))
            dk_acc[...] += jnp.where(masks[0], dk[0], dk[1])
            kaux_acc[...] += jnp.where(masks[0], dk[1], dk[0])
            dv_acc[...] += jnp.where(masks[0], dv[0], dv[1])
            dq_new = jnp.where(masks[0], dq[0], dq[1])
            qaux_new = jnp.where(masks[0], dq[1], dq[0])

            @pl.when(ki == 0)
            def _():
                dq_acc[qi] = dq_new
                qaux_acc[qi] = qaux_new

            @pl.when(ki > 0)
            def _():
                dq_acc[qi] += dq_new
                qaux_acc[qi] += qaux_new

        @pl.when(qi > ki)
        def _():
            step(False)

        @pl.when(qi == ki)
        def _():
            step(True)
            rows = pl.ds(pl.multiple_of(qi * TQ, TQ), TQ)
            dq_ref[rows, :] = (dq_acc[qi] * 0.125).astype(BF16)
            qaux_ref[rows, :] = qaux_acc[qi]

        @pl.when(qi == nq - 1)
        def _():
            dk_ref[...] = dk_acc[...].astype(BF16)
            dv_ref[...] = dv_acc[...].astype(BF16)
            kaux_ref[...] = kaux_acc[...]

    grid_spec = pltpu.PrefetchScalarGridSpec(
        num_scalar_prefetch=2, grid=(HEAD_PAIRS, int(qi_arr.shape[0])),
        in_specs=[pl.BlockSpec((TQ, 128), lambda p, t, qi, ki: (qi[t], Q_BLK + p)),
                  pl.BlockSpec((TQ, 128), lambda p, t, qi, ki: (ki[t], K_BLK + p)),
                  pl.BlockSpec((TQ, 128), lambda p, t, qi, ki: (ki[t], V_BLK + p)),
                  pl.BlockSpec((TQ, 128), lambda p, t, qi, ki: (qi[t], Q_BLK + p)),
                  pl.BlockSpec((TQ, 128), lambda p, t, qi, ki: (qi[t], p)),
                  pl.BlockSpec((TQ, 128), lambda p, t, qi, ki: (ki[t], p)),
                  pl.BlockSpec((TQ, 128), lambda p, t, qi, ki: (qi[t], p))],
        out_specs=[pl.BlockSpec((s, 128), lambda p, t, qi, ki: (0, p)),
                   pl.BlockSpec((TQ, 128), lambda p, t, qi, ki: (ki[t], p)),
                   pl.BlockSpec((TQ, 128), lambda p, t, qi, ki: (ki[t], p)),
                   pl.BlockSpec((None, s, 128), lambda p, t, qi, ki: (p, 0, 0)),
                   pl.BlockSpec((None, TQ, 128), lambda p, t, qi, ki: (p, ki[t], 0))],
        scratch_shapes=[pltpu.VMEM((nq, TQ, 128), F32), pltpu.VMEM((nq, TQ, 128), F32),
                        pltpu.VMEM((TQ, 128), F32), pltpu.VMEM((TQ, 128), F32), pltpu.VMEM((TQ, 128), F32)])
    return pl.pallas_call(
        body, name=name, grid_spec=grid_spec,
        out_shape=[jax.ShapeDtypeStruct((s, D_FOX), BF16)] * 3 + [jax.ShapeDtypeStruct((HEAD_PAIRS, s, 128), F32)] * 2,
        compiler_params=_cp("arbitrary", "arbitrary"),
    )(qi_arr, ki_arr, proj, proj, proj, dycat, aqb, ak, ad)


XA_SCALE = XA_DIM ** -0.5


def _xattn_fwd(q2, kv, name):
    s = q2.shape[0]
    m = kv.shape[0]

    def body(q_ref, kv_ref, o_ref):
        for h in range(XA_HEADS):
            c0 = h * XA_DIM
            sc = _dot(q_ref[:, c0:c0 + XA_DIM], kv_ref[:, c0:c0 + XA_DIM], NT) * XA_SCALE
            e = jnp.exp(sc - jnp.max(sc, axis=1, keepdims=True))
            p = e / jnp.sum(e, axis=1, keepdims=True)
            o_ref[:, c0:c0 + XA_DIM] = _dot(p.astype(BF16), kv_ref[:, D_MODEL + c0:D_MODEL + c0 + XA_DIM], NN).astype(BF16)

    return pl.pallas_call(
        body, name=name, grid=(s // TM,),
        in_specs=[_row_spec(TM, D_MODEL), pl.BlockSpec((m, 2 * D_MODEL), lambda i: (0, 0))],
        out_specs=_row_spec(TM, D_MODEL), out_shape=jax.ShapeDtypeStruct((s, D_MODEL), BF16),
        compiler_params=_cp("parallel"),
    )(q2, kv)


def _xattn_bwd(q2, kv, do2, name):
    s = q2.shape[0]
    m = kv.shape[0]

    def body(q_ref, kv_ref, do_ref, dq_ref, dkv_ref):
        i = pl.program_id(0)

        @pl.when(i == 0)
        def _():
            dkv_ref[...] = jnp.zeros_like(dkv_ref)

        for h in range(XA_HEADS):
            c0 = h * XA_DIM
            v0 = D_MODEL + c0
            qh = q_ref[:, c0:c0 + XA_DIM]
            kh = kv_ref[:, c0:c0 + XA_DIM]
            doh = do_ref[:, c0:c0 + XA_DIM]
            sc = _dot(qh, kh, NT) * XA_SCALE
            e = jnp.exp(sc - jnp.max(sc, axis=1, keepdims=True))
            p = e / jnp.sum(e, axis=1, keepdims=True)
            dp = _dot(doh, kv_ref[:, v0:v0 + XA_DIM], NT)
            ds = p * (dp - jnp.sum(p * dp, axis=1, keepdims=True))
            dsb = (ds * XA_SCALE).astype(BF16)
            dq_ref[:, c0:c0 + XA_DIM] = _dot(dsb, kh, NN).astype(BF16)
            dkv_ref[:, c0:c0 + XA_DIM] += _dot(dsb, qh, TN)
            dkv_ref[:, v0:v0 + XA_DIM] += _dot(p.astype(BF16), doh, TN)

    return pl.pallas_call(
        body, name=name, grid=(s // TM,),
        in_specs=[_row_spec(TM, D_MODEL), pl.BlockSpec((m, 2 * D_MODEL), lambda i: (0, 0)), _row_spec(TM, D_MODEL)],
        out_specs=[_row_spec(TM, D_MODEL), pl.BlockSpec((m, 2 * D_MODEL), lambda i: (0, 0))],
        out_shape=[jax.ShapeDtypeStruct((s, D_MODEL), BF16), jax.ShapeDtypeStruct((m, 2 * D_MODEL), F32)],
        compiler_params=_cp("arbitrary"),
    )(q2, kv, do2)


GELU_C = math.sqrt(2.0 / math.pi)
GELU_A = 0.044715


def _gelu(x):
    return 0.5 * x * (1.0 + jnp.tanh(GELU_C * (x + GELU_A * x * x * x)))


def _gelu_and_grad(x):
    t = jnp.tanh(GELU_C * (x + GELU_A * x * x * x))
    g = 0.5 * x * (1.0 + t)
    dg = 0.5 * (1.0 + t) + 0.5 * x * (1.0 - t * t) * GELU_C * (1.0 + 3.0 * GELU_A * x * x)
    return g, dg


def _shift_down(main, prev8):
    row = lax.broadcasted_iota(jnp.int32, main.shape, 0)
    s1 = jnp.where(row == 0, prev8[7:8, :], pltpu.roll(main, 1, 0))
    s2 = jnp.where(row == 0, prev8[6:7, :], jnp.where(row == 1, prev8[7:8, :], pltpu.roll(main, 2, 0)))
    return s1, s2


def _shift_up(main, next8):
    n = main.shape[0]
    row = lax.broadcasted_iota(jnp.int32, main.shape, 0)
    u1 = jnp.where(row == n - 1, next8[0:1, :], pltpu.roll(main, n - 1, 0))
    u2 = jnp.where(row == n - 2, next8[0:1, :], jnp.where(row == n - 1, next8[1:2, :], pltpu.roll(main, n - 2, 0)))
    return u1, u2


def _conv(h, s1, s2, w_ref, b_ref):
    return w_ref[0:1, :] * s2 + w_ref[1:2, :] * s1 + w_ref[2:3, :] * h + b_ref[...]


def _ffn_fwd(h3, w_up, cw, cb, w_down, name):
    s = h3.shape[0]
    tn = TN_FF
    nj = D_FF // tn
    per = D_MODEL // tn
    hb = TM // 8

    def body(h_ref, halo_ref, wg_ref, wu_ref, cwg_ref, cwu_ref, cbg_ref, cbu_ref, wd_ref,
             hg_ref, hu_ref, a_ref, y_ref):
        i, j = pl.program_id(0), pl.program_id(1)
        h = h_ref[...]
        halo = halo_ref[...]
        halo = jnp.where(i > 0, halo, jnp.zeros_like(halo))
        conv = []
        for w_ref, cw_ref, cb_ref, hid_ref in ((wg_ref, cwg_ref, cbg_ref, hg_ref), (wu_ref, cwu_ref, cbu_ref, hu_ref)):
            hm_b = _dot(h, w_ref[...], NN).astype(BF16)
            hid_ref[...] = hm_b
            hm = hm_b.astype(F32)
            hl = _dot(halo, w_ref[...], NN).astype(BF16).astype(F32)
            s1, s2 = _shift_down(hm, hl)
            conv.append(_conv(hm, s1, s2, cw_ref, cb_ref))
        a = (_gelu(conv[0]) * conv[1]).astype(BF16)
        a_ref[...] = a
        contrib = _dot(a, wd_ref[...], NN)

        @pl.when(j == 0)
        def _():
            y_ref[...] = contrib

        @pl.when(j > 0)
        def _():
            y_ref[...] += contrib

    return pl.pallas_call(
        body, name=name, grid=(s // TM, nj),
        in_specs=[pl.BlockSpec((TM, D_MODEL), lambda i, j: (i, 0)),
                  pl.BlockSpec((8, D_MODEL), lambda i, j: (jnp.maximum(i * hb - 1, 0), 0)),
                  pl.BlockSpec((None, D_MODEL, tn), lambda i, j: (j // per, 0, j % per)),
                  pl.BlockSpec((None, D_MODEL, tn), lambda i, j: (NDEV // 2 + j // per, 0, j % per)),
                  pl.BlockSpec((8, tn), lambda i, j: (0, j)),
                  pl.BlockSpec((8, tn), lambda i, j: (0, nj + j)),
                  pl.BlockSpec((1, tn), lambda i, j: (0, j)),
                  pl.BlockSpec((1, tn), lambda i, j: (0, nj + j)),
                  pl.BlockSpec((tn, D_MODEL), lambda i, j: (j, 0))],
        out_specs=[pl.BlockSpec((TM, tn), lambda i, j: (i, j)), pl.BlockSpec((TM, tn), lambda i, j: (i, j)),
                   pl.BlockSpec((TM, tn), lambda i, j: (i, j)), pl.BlockSpec((TM, D_MODEL), lambda i, j: (i, 0))],
        out_shape=[jax.ShapeDtypeStruct((s, D_FF), BF16), jax.ShapeDtypeStruct((s, D_FF), BF16),
                   jax.ShapeDtypeStruct((s, D_FF), BF16), jax.ShapeDtypeStruct((s, D_MODEL), F32)],
        compiler_params=_cp("parallel", "arbitrary"),
    )(h3, h3, w_up, w_up, cw, cw, cb, cb, w_down)


def _ffn_bwd(dy3, w_down, hid_g, hid_u, cw, cb, name):
    s = dy3.shape[0]
    n = s // TM
    tn = TN_FF
    nj = D_FF // tn
    hb = TM // 8
    last8 = s // 8 - 1

    def body(dy_ref, dyp_ref, wd_ref, hg_ref, hgl_ref, hgn_ref, hu_ref, hul_ref, hun_ref,
             cwg_ref, cwu_ref, cbg_ref, cbu_ref,
             dhg_ref, dhu_ref, dcwg_ref, dcwu_ref, dcbg_ref, dcbu_ref):
        i = pl.program_id(1)
        first, last = i == 0, i == n - 1
        da = _dot(dy_ref[...], wd_ref[...], NT)
        dyp = dyp_ref[...]
        dyp = jnp.where(last, jnp.zeros_like(dyp), dyp)
        da_n = _dot(dyp, wd_ref[...], NT)
        parts = []
        for h_ref, hl_ref, hn_ref, cw_ref, cb_ref in ((hg_ref, hgl_ref, hgn_ref, cwg_ref, cbg_ref),
                                                     (hu_ref, hul_ref, hun_ref, cwu_ref, cbu_ref)):
            hm = h_ref[...].astype(F32)
            hl = jnp.where(first, 0.0, hl_ref[...].astype(F32))
            hn = hn_ref[...].astype(F32)
            s1, s2 = _shift_down(hm, hl)
            c = _conv(hm, s1, s2, cw_ref, cb_ref)
            n1, n2 = _shift_down(hn, hm[TM - 8:, :])
            cn = _conv(hn, n1, n2, cw_ref, cb_ref)
            parts.append((hm, s1, s2, c, cn))
        g, dg = _gelu_and_grad(parts[0][3])
        gn, dgn = _gelu_and_grad(parts[0][4])
        dc_g = da * parts[1][3] * dg
        dc_u = da * g
        dcn_g = da_n * parts[1][4] * dgn
        dcn_u = da_n * gn
        outs = ((dc_g, dcn_g, parts[0], cwg_ref, dhg_ref, dcwg_ref, dcbg_ref),
                (dc_u, dcn_u, parts[1], cwu_ref, dhu_ref, dcwu_ref, dcbu_ref))
        for dc, dcn, (hm, s1, s2, _, _), cw_ref, dh_ref, dcw_ref, dcb_ref in outs:
            u1, u2 = _shift_up(dc, dcn)
            dh_ref[...] = (cw_ref[2:3, :] * dc + cw_ref[1:2, :] * u1 + cw_ref[0:1, :] * u2).astype(BF16)
            dcb = jnp.sum(dc, axis=0, keepdims=True)
            row8 = lax.broadcasted_iota(jnp.int32, (8, tn), 0)
            dcw = jnp.where(row8 == 0, jnp.sum(dc * s2, axis=0, keepdims=True),
                            jnp.where(row8 == 1, jnp.sum(dc * s1, axis=0, keepdims=True),
                                      jnp.where(row8 == 2, jnp.sum(dc * hm, axis=0, keepdims=True), 0.0)))

            @pl.when(first)
            def _():
                dcw_ref[...] = dcw
                dcb_ref[...] = dcb

            @pl.when(i > 0)
            def _():
                dcw_ref[...] += dcw
                dcb_ref[...] += dcb

    prev8 = lambda j, i: (jnp.maximum(i * hb - 1, 0), j)
    next8 = lambda j, i: (jnp.minimum((i + 1) * hb, last8), j)
    blk = lambda j, i: (i, j)
    col = lambda j, i: (0, j)
    colu = lambda j, i: (0, nj + j)
    return pl.pallas_call(
        body, name=name, grid=(nj, n),
        in_specs=[pl.BlockSpec((TM, D_MODEL), lambda j, i: (i, 0)),
                  pl.BlockSpec((8, D_MODEL), lambda j, i: (jnp.minimum((i + 1) * hb, last8), 0)),
                  pl.BlockSpec((tn, D_MODEL), lambda j, i: (j, 0)),
                  pl.BlockSpec((TM, tn), blk), pl.BlockSpec((8, tn), prev8), pl.BlockSpec((8, tn), next8),
                  pl.BlockSpec((TM, tn), blk), pl.BlockSpec((8, tn), prev8), pl.BlockSpec((8, tn), next8),
                  pl.BlockSpec((8, tn), col), pl.BlockSpec((8, tn), colu),
                  pl.BlockSpec((1, tn), col), pl.BlockSpec((1, tn), colu)],
        out_specs=[pl.BlockSpec((TM, tn), blk), pl.BlockSpec((TM, tn), blk),
                   pl.BlockSpec((8, tn), col), pl.BlockSpec((8, tn), col),
                   pl.BlockSpec((1, tn), col), pl.BlockSpec((1, tn), col)],
        out_shape=[jax.ShapeDtypeStruct((s, D_FF), BF16), jax.ShapeDtypeStruct((s, D_FF), BF16),
                   jax.ShapeDtypeStruct((8, D_FF), F32), jax.ShapeDtypeStruct((8, D_FF), F32),
                   jax.ShapeDtypeStruct((1, D_FF), F32), jax.ShapeDtypeStruct((1, D_FF), F32)],
        compiler_params=_cp("parallel", "arbitrary"),
    )(dy3, dy3, w_down, hid_g, hid_g, hid_g, hid_u, hid_u, hid_u, cw, cw, cb, cb)


def _slot(p):
    return 4 * p[0] + 2 * p[1] + p[2]


def _all_gather(shards, name):
    n = len(shards)

    def body(*refs):
        ins, outs = refs[:n], refs[n:2 * n]
        send_sems, recv_sems, local_sems = refs[2 * n:]
        x, y, c = lax.axis_index("x"), lax.axis_index("y"), lax.axis_index("c")
        me, sibling = (x, y, c), (x, y, 1 - c)
        chips = [(1 - x, y), (x, 1 - y), (1 - x, 1 - y)]

        def copy(a, k, block, to, from_input=False):
            dst = outs[a].at[_slot(block)]
            return pltpu.make_async_remote_copy(
                src_ref=ins[a] if from_input else dst, dst_ref=dst,
                send_sem=send_sems.at[a, k], recv_sem=recv_sems.at[a, k],
                device_id=to, device_id_type=MESH)

        mine = [pltpu.make_async_copy(ins[a], outs[a].at[_slot(me)], local_sems.at[a]) for a in range(n)]
        for cp in mine:
            cp.start()
        first = []
        for a in range(n):
            first.append(copy(a, 0, me, sibling, True))
            first += [copy(a, 1 + j, me, (*chip, c), True) for j, chip in enumerate(chips)]
        for cp in first:
            cp.start()
        passed = []
        for j, chip in enumerate(chips):
            for a in range(n):
                copy(a, 1 + j, (*chip, c), me).wait_recv()
                fwd = copy(a, 4 + j, (*chip, c), sibling)
                fwd.start()
                passed.append(fwd)
        for a in range(n):
            copy(a, 0, sibling, me).wait_recv()
            for j, chip in enumerate(chips):
                copy(a, 4 + j, (*chip, 1 - c), me).wait_recv()
        for cp in first + passed:
            cp.wait_send()
        for cp in mine:
            cp.wait()

    any_spec = pl.BlockSpec(memory_space=pl.ANY)
    return pl.pallas_call(
        body, name=name,
        in_specs=[any_spec] * n, out_specs=[any_spec] * n,
        out_shape=[jax.ShapeDtypeStruct((NDEV,) + s.shape, s.dtype) for s in shards],
        scratch_shapes=[pltpu.SemaphoreType.DMA((n, 7)), pltpu.SemaphoreType.DMA((n, 7)),
                        pltpu.SemaphoreType.DMA((n,))],
    )(*shards)


def _scatter_blocks(full, name):
    n = len(full)

    def body(*refs):
        ins, outs = refs[:n], refs[n:2 * n]
        send_sems, recv_sems, local_sems = refs[2 * n:]
        x, y, c = lax.axis_index("x"), lax.axis_index("y"), lax.axis_index("c")
        me = (x, y, c)
        copies = []
        for a in range(n):
            cp = pltpu.make_async_copy(ins[a].at[_slot(me)], outs[a].at[_slot(me)], local_sems.at[a])
            cp.start()
            copies.append(cp)
        remote = []
        for mask in range(1, NDEV):
            peer = (1 - x if mask & 4 else x, 1 - y if mask & 2 else y, 1 - c if mask & 1 else c)
            for a in range(n):
                cp = pltpu.make_async_remote_copy(
                    src_ref=ins[a].at[_slot(peer)], dst_ref=outs[a].at[_slot(me)],
                    send_sem=send_sems.at[a, mask - 1], recv_sem=recv_sems.at[a, mask - 1],
                    device_id=peer, device_id_type=MESH)
                cp.start()
                remote.append(cp)
        for cp in remote:
            cp.wait()
        for cp in copies:
            cp.wait()

    any_spec = pl.BlockSpec(memory_space=pl.ANY)
    return pl.pallas_call(
        body, name=name,
        in_specs=[any_spec] * n, out_specs=[any_spec] * n,
        out_shape=[jax.ShapeDtypeStruct(f.shape, f.dtype) for f in full],
        scratch_shapes=[pltpu.SemaphoreType.DMA((n, 7)), pltpu.SemaphoreType.DMA((n, 7)),
                        pltpu.SemaphoreType.DMA((n,))],
    )(*full)


def _peer_list(x, y, c):
    return [(1 - x if m & 4 else x, 1 - y if m & 2 else y, 1 - c if m & 1 else c) for m in range(1, NDEV)]


def _exchange_copies(src_refs, land_refs, send_sems, recv_sems, gather):
    x, y, c = lax.axis_index("x"), lax.axis_index("y"), lax.axis_index("c")
    me = (x, y, c)
    copies = []
    for m, peer in enumerate(_peer_list(x, y, c)):
        for a in range(len(src_refs)):
            copies.append(pltpu.make_async_remote_copy(
                src_ref=src_refs[a] if gather else src_refs[a].at[_slot(peer)], dst_ref=land_refs[a].at[_slot(me)],
                send_sem=send_sems.at[a * (NDEV - 1) + m], recv_sem=recv_sems.at[a * (NDEV - 1) + m],
                device_id=peer, device_id_type=MESH))
    return copies


def _exchange_start(srcs, lands, after, gather, name):
    n = len(srcs)
    hbm = pl.BlockSpec(memory_space=pltpu.HBM)

    def body(*refs):
        for cp in _exchange_copies(refs[:n], refs[n:2 * n], refs[2 * n + 1], refs[2 * n + 2], gather):
            cp.start()
        token = refs[-1]
        token[...] = jnp.zeros_like(token)

    outs = pl.pallas_call(
        body, name=name,
        out_shape=(pltpu.SemaphoreType.DMA((n * (NDEV - 1),)), pltpu.SemaphoreType.DMA((n * (NDEV - 1),)),
                   *[pltpu.HBM(a.shape, a.dtype) for a in list(srcs) + list(lands)],
                   jax.ShapeDtypeStruct((8, 128), F32)),
        in_specs=[hbm] * (2 * n) + [pl.BlockSpec(memory_space=pl.ANY)],
        out_specs=(pl.BlockSpec(memory_space=pltpu.SEMAPHORE), pl.BlockSpec(memory_space=pltpu.SEMAPHORE),
                   *[hbm] * (2 * n), pl.BlockSpec(memory_space=pltpu.VMEM)),
        input_output_aliases={i: 2 + i for i in range(2 * n)},
        compiler_params=pltpu.CompilerParams(has_side_effects=pltpu.SideEffectType.DATAFLOW_SIDE_EFFECTING),
    )(*[pltpu.with_memory_space_constraint(a, pltpu.HBM) for a in list(srcs) + list(lands)], after)
    return outs[0], outs[1], outs[2:2 + n], outs[2 + n:2 + 2 * n], outs[-1]


def _exchange_wait(send_sems, recv_sems, srcs, lands, after, gather, name):
    n = len(srcs)
    hbm = pl.BlockSpec(memory_space=pltpu.HBM)

    def body(*refs):
        for cp in _exchange_copies(refs[:n], refs[n:2 * n], refs[2 * n], refs[2 * n + 1], gather):
            cp.wait_send()
            cp.wait_recv()

    outs = pl.pallas_call(
        body, name=name,
        out_shape=tuple(pltpu.HBM(a.shape, a.dtype) for a in list(srcs) + list(lands)),
        in_specs=[hbm] * (2 * n) + [pl.BlockSpec(memory_space=pltpu.SEMAPHORE)] * 2 + [pl.BlockSpec(memory_space=pl.ANY)],
        out_specs=tuple([hbm] * (2 * n)),
        input_output_aliases={i: i for i in range(2 * n)},
        compiler_params=pltpu.CompilerParams(has_side_effects=pltpu.SideEffectType.DATAFLOW_SIDE_EFFECTING),
    )(*srcs, *lands, send_sems, recv_sems, after)
    return outs[n:]


def _own_slot(block):
    me = 4 * lax.axis_index("x") + 2 * lax.axis_index("y") + lax.axis_index("c")
    return lax.dynamic_update_slice(jnp.zeros((NDEV,) + block.shape, block.dtype), block[None], (me, 0, 0))


def _adamw(parts, w, m, v, name):
    r, c = w.shape
    tr = r if r * c <= 160 * 1024 else max(8, (160 * 1024 // c) // 8 * 8)
    while r % tr:
        tr -= 8
    bc1 = 1.0 - ADAM_B1 ** ADAM_STEP
    bc2 = 1.0 - ADAM_B2 ** ADAM_STEP

    def body(p_ref, w_ref, m_ref, v_ref, g_ref, d_ref, mo_ref, vo_ref):
        g = p_ref[0].astype(F32)
        for d in range(1, NDEV):
            g = g + p_ref[d].astype(F32)
        g_ref[...] = g
        mn = ADAM_B1 * m_ref[...] + (1.0 - ADAM_B1) * g
        vn = ADAM_B2 * v_ref[...] + (1.0 - ADAM_B2) * (g * g)
        mo_ref[...] = mn
        vo_ref[...] = vn
        d_ref[...] = -ADAM_LR * ((mn / bc1) / (jnp.sqrt(vn / bc2) + ADAM_EPS) + ADAM_WD * w_ref[...])

    spec = pl.BlockSpec((tr, c), lambda i: (i, 0))
    return pl.pallas_call(
        body, name=name, grid=(r // tr,),
        in_specs=[pl.BlockSpec((NDEV, tr, c), lambda i: (0, i, 0)), spec, spec, spec],
        out_specs=[spec] * 4, out_shape=[jax.ShapeDtypeStruct((r, c), F32)] * 4,
        compiler_params=_cp("parallel"),
    )(parts, w, m, v)


def _pair_cols(a):
    s = a.shape[0]
    t = a[:, :FOX_HEADS].reshape(s, HEAD_PAIRS, 2).transpose(1, 0, 2)
    return jnp.pad(t, ((0, 0), (0, 0), (0, 126)))


def _pair_rows(at):
    s = at.shape[1]
    return jnp.pad(at[:FOX_HEADS].reshape(HEAD_PAIRS, 2, s), ((0, 0), (0, 6), (0, 0)))


def _local_step(x, mem, tgt, gains, b_forget, w_pool, pool_scale, conv_w, conv_b, wts, ffn_weights, send_ffn_grads):
    w_in, w_mix, w_xq, w_xo, w_xkv = wts["w_in"], wts["w_mix"], wts["w_xq"], wts["w_xo"], wts["w_xkv"]

    w_f = w_in[:, F_COL:]
    b_pad = jnp.pad(b_forget, ((0, 0), (0, 128 - FOX_HEADS)))
    wbd = jnp.zeros((D_POOL, D_POOL), F32)
    for g in range(4):
        wbd = wbd.at[64 * g:64 * g + 64, 64 * g:64 * g + 64].set(w_pool[g])
    wbd = wbd.astype(BF16)
    scale = pool_scale.reshape(1, D_POOL)
    cw = jnp.pad(conv_w, ((0, 5), (0, 0)))

    h1 = _norm_fwd(x, gains["mix_pre"], "norm_mix_pre")
    proj = _mm(h1, w_in, "nn", BF16, 1024, 384, 1024, "proj_in")
    fraw = _mm(h1, w_f, "nn", F32, 1024, 128, 1024, "proj_gate")
    flog, cum = _gate_cumsum(fraw, b_pad, "gate_cumsum")
    aq, ak = _fox_operands(cum, "fox_operands")
    ycat, aqb = _fox_fwd2(proj, aq, ak, "fox_fwd")
    ycat = _pool_fwd(proj, wbd, scale, ycat, "pool_fwd")
    y1 = _mm(ycat, w_mix, "nn", F32, 1024, 1024, 1024, "mix_out")
    x1, h2 = _resid_norm_fwd(x, y1, gains["mix_post"], gains["xa_pre"], "resid_mix")
    q2 = _mm(h2, w_xq, "nn", BF16, 1024, 1024, 1024, "xa_q")
    mem_n = _norm_fwd(mem, gains["mem"], "norm_mem")
    kv = _mm(mem_n, w_xkv, "nn", BF16, mem.shape[0], 256, 1024, "xa_kv", b_cols=256)
    o2 = _xattn_fwd(q2, kv, "xattn_fwd")
    y2 = _mm(o2, w_xo, "nn", F32, 1024, 1024, 1024, "xa_out")
    x2, h3 = _resid_norm_fwd(x1, y2, gains["xa_post"], gains["ffn_pre"], "resid_xa")
    w_up, w_down = ffn_weights(h3)
    hid_g, hid_u, act, y3 = _ffn_fwd(h3, w_up, cw, conv_b, w_down, "ffn_fwd")

    loss, dx3, dy3, dg_ffn_post = _loss_bwd(x2, y3, gains["ffn_post"], tgt, "loss_bwd")
    dhid_g, dhid_u, dcw_g, dcw_u, dcb_g, dcb_u = _ffn_bwd(dy3, w_down, hid_g, hid_u, cw, conv_b, "ffn_bwd")
    d_w_down = _mm(act, dy3, "tn", BF16, 1024, 1024, 1024, "dw_down")
    d_w_up = _mm(h3, [dhid_g, dhid_u], "tn", BF16, 1024, 1024, 1024, "dw_up", out_cols=1024)
    sent = send_ffn_grads(d_w_up, d_w_down)
    dh3 = _mm([dhid_g, dhid_u], w_up, "nt", F32, 1024, 1024, 1024, "dh_ffn", b_cols=1024)
    dx2, dg_ffn_pre, dy2, dg_xa_post = _norm_bwd(dh3, x2, dx3, gains["ffn_pre"] + sent, "norm_bwd_ffn",
                                                 prev=(y2, gains["xa_post"]))
    do2 = _mm(dy2, w_xo, "nt", BF16, 1024, 1024, 1024, "d_xa_out")
    d_w_xo = _mm(o2, dy2, "tn", BF16, 1024, 1024, 1024, "dw_xo")
    dq2, dkv = _xattn_bwd(q2, kv, do2, "xattn_bwd")
    dkv = dkv.astype(BF16)
    dh2 = _mm(dq2, w_xq, "nt", F32, 1024, 1024, 1024, "dh_xa")
    d_w_xq = _mm(h2, dq2, "tn", BF16, 1024, 1024, 1024, "dw_xq")
    dmem_n = _mm(dkv, w_xkv, "nt", F32, mem.shape[0], 1024, 256, "d_mem", b_cols=256)
    d_w_xkv = _mm(mem_n, dkv, "tn", BF16, 1024, 256, mem.shape[0], "dw_xkv", out_cols=256)
    _, dg_mem = _norm_bwd(dmem_n, mem, jnp.zeros_like(mem), gains["mem"], "norm_bwd_mem")
    dx1, dg_xa_pre, dy1, dg_mix_post = _norm_bwd(dh2, x1, dx2, gains["xa_pre"], "norm_bwd_xa",
                                                 prev=(y1, gains["mix_post"]))
    dycat = _mm(dy1, w_mix, "nt", BF16, 1024, 1024, 1024, "d_mix_out")
    d_w_mix = _mm(ycat, dy1, "tn", BF16, 1024, 1024, 1024, "dw_mix")
    ad = _fox_do_operand(dycat, ycat, "fox_do_operand")
    dq, dk, dv, qaux, kaux = _fox_bwd2(proj, dycat, aqb, ak, ad, "fox_bwd")
    du, d_wbd, d_scale = _pool_bwd(proj, dycat, wbd, scale, "pool_bwd")
    df, db_f = _gate_bwd(qaux, kaux, flog, "gate_bwd")
    dproj = jnp.concatenate([du, dq, dk, dv, df], axis=1)
    dh1 = _mm(dproj, w_in, "nt", F32, 1024, 1024, 896, "dh_mix")
    d_w_in = _mm(h1, dproj, "tn", BF16, 1024, 384, 1024, "dw_in")
    grad_x, dg_mix_pre = _norm_bwd(dh1, x, dx1, gains["mix_pre"], "norm_bwd_mix")

    big = dict(w_in=d_w_in, w_mix=d_w_mix, w_xq=d_w_xq, w_xo=d_w_xo, w_xkv=d_w_xkv)
    small = dict(
        mix_pre=dg_mix_pre, mix_post=dg_mix_post, mem=dg_mem, xa_pre=dg_xa_pre, xa_post=dg_xa_post,
        ffn_pre=dg_ffn_pre, ffn_post=dg_ffn_post,
        conv_b=jnp.concatenate([dcb_g, dcb_u], axis=1),
        w_pool=jnp.stack([d_wbd[64 * g:64 * g + 64, 64 * g:64 * g + 64] for g in range(4)]),
        pool_scale=d_scale.reshape(4, 64),
        b_forget=db_f[:, :FOX_HEADS],
        conv_w=jnp.concatenate([dcw_g[:3], dcw_u[:3]], axis=1),
    )
    return loss, grad_x, big, small


SMALL_ORDER = ("mix_pre", "mix_post", "mem", "xa_pre", "xa_post", "ffn_pre", "ffn_post", "conv_b",
               "w_pool", "pool_scale", "b_forget")
SMALL_ROWS = 256


def _pack_small(d):
    flat = jnp.concatenate([d[k].reshape(-1).astype(F32) for k in SMALL_ORDER])
    return jnp.pad(flat, (0, SMALL_ROWS * 128 - flat.shape[0])).reshape(SMALL_ROWS, 128)


def _unpack_small(a, like):
    flat = a.reshape(-1)
    out, off = {}, 0
    for k in SMALL_ORDER:
        n = like[k].size
        out[k] = flat[off:off + n].reshape(like[k].shape)
        off += n
    return out


def kernel(x, mem, norm_mix_pre, norm_mix_post, w_in, b_forget, w_pool, pool_scale, w_mix_out, norm_mem, norm_xa_pre, norm_xa_post, w_xq, w_xkv, w_xo, norm_ffn_pre, norm_ffn_post, w_up, conv_w, conv_b, w_down, loss_target, m_norm_mix_pre, m_norm_mix_post, m_w_in, m_b_forget, m_w_pool, m_pool_scale, m_w_mix_out, m_norm_mem, m_norm_xa_pre, m_norm_xa_post, m_w_xq, m_w_xkv, m_w_xo, m_norm_ffn_pre, m_norm_ffn_post, m_w_up, m_conv_w, m_conv_b, m_w_down, v_norm_mix_pre, v_norm_mix_post, v_w_in, v_b_forget, v_w_pool, v_pool_scale, v_w_mix_out, v_norm_mem, v_norm_xa_pre, v_norm_xa_post, v_w_xq, v_w_xkv, v_w_xo, v_norm_ffn_pre, v_norm_ffn_post, v_w_up, v_conv_w, v_conv_b, v_w_down):
    names = ("norm_mix_pre", "norm_mix_post", "w_in", "b_forget", "w_pool", "pool_scale", "w_mix_out", "norm_mem",
             "norm_xa_pre", "norm_xa_post", "w_xq", "w_xkv", "w_xo", "norm_ffn_pre", "norm_ffn_post", "w_up",
             "conv_w", "conv_b", "w_down")
    w = dict(zip(names, (norm_mix_pre, norm_mix_post, w_in, b_forget, w_pool, pool_scale, w_mix_out, norm_mem,
                         norm_xa_pre, norm_xa_post, w_xq, w_xkv, w_xo, norm_ffn_pre, norm_ffn_post, w_up,
                         conv_w, conv_b, w_down)))
    mo = dict(zip(names, (m_norm_mix_pre, m_norm_mix_post, m_w_in, m_b_forget, m_w_pool, m_pool_scale, m_w_mix_out,
                          m_norm_mem, m_norm_xa_pre, m_norm_xa_post, m_w_xq, m_w_xkv, m_w_xo, m_norm_ffn_pre,
                          m_norm_ffn_post, m_w_up, m_conv_w, m_conv_b, m_w_down)))
    vo = dict(zip(names, (v_norm_mix_pre, v_norm_mix_post, v_w_in, v_b_forget, v_w_pool, v_pool_scale, v_w_mix_out,
                          v_norm_mem, v_norm_xa_pre, v_norm_xa_post, v_w_xq, v_w_xkv, v_w_xo, v_norm_ffn_pre,
                          v_norm_ffn_post, v_w_up, v_conv_w, v_conv_b, v_w_down)))

    big_names = ("w_in", "w_mix_out", "w_xq", "w_xo", "w_xkv", "w_up", "w_down")
    shards = {k: w[k][0].astype(BF16) for k in big_names}
    shards["w_in"] = jnp.pad(shards["w_in"], ((0, 0), (0, D_IN_PAD - shards["w_in"].shape[1])))
    conv_w_sh = jnp.pad(conv_w[0, :, 0, :], ((0, 5), (0, 0)))
    early = ("w_in", "w_mix_out", "w_xq", "w_xo", "w_xkv")
    g_in, g_mix, g_xq, g_xo, g_xkv, g_cw = _all_gather([shards[k] for k in early] + [conv_w_sh], "gather_weights")
    wts = dict(w_in=g_in.reshape(D_MODEL, D_IN_PAD), w_mix=g_mix.reshape(D_MODEL, D_MODEL),
               w_xq=g_xq.reshape(D_MODEL, D_MODEL), w_xo=g_xo.reshape(D_MODEL, D_MODEL), w_xkv=g_xkv)
    conv_w_full = g_cw[:, :3, :].transpose(1, 0, 2).reshape(3, 2 * D_FF)
    ffn_srcs = [shards["w_up"], shards["w_down"]]
    w_flight = _exchange_start(ffn_srcs, [_own_slot(a) for a in ffn_srcs], g_cw, True, "gather_ffn_start")

    def ffn_weights(after):
        g_up, g_down = _exchange_wait(*w_flight[:4], after, True, "gather_ffn_wait")
        return g_up, g_down.reshape(D_FF, D_MODEL)

    g_flight = []

    def send_ffn_grads(d_w_up, d_w_down):
        srcs = [d_w_up, d_w_down.reshape(NDEV, D_FF // NDEV, D_MODEL)]
        me = 4 * lax.axis_index("x") + 2 * lax.axis_index("y") + lax.axis_index("c")
        lands = [_own_slot(lax.dynamic_index_in_dim(a, me, 0, keepdims=False)) for a in srcs]
        g_flight.extend(_exchange_start(srcs, lands, d_w_up, False, "scatter_ffn_start"))
        return g_flight[4][0, 0]

    gains = dict(mix_pre=norm_mix_pre + w_flight[4][0, 0], mix_post=norm_mix_post, mem=norm_mem, xa_pre=norm_xa_pre,
                 xa_post=norm_xa_post, ffn_pre=norm_ffn_pre, ffn_post=norm_ffn_post)
    loss, grad_x, big, small = _local_step(x[0], mem[0], loss_target[0], gains, b_forget, w_pool[0], pool_scale[0],
                                           conv_w_full, conv_b, wts, ffn_weights, send_ffn_grads)

    d_cw = jnp.pad(small.pop("conv_w"), ((0, 5), (0, 0))).reshape(8, NDEV, D_MODEL).transpose(1, 0, 2)
    full = [big["w_in"].reshape(NDEV, D_MODEL // NDEV, D_IN_PAD), big["w_mix"].reshape(NDEV, D_MODEL // NDEV, D_MODEL),
            big["w_xq"].reshape(NDEV, D_MODEL // NDEV, D_MODEL), big["w_xo"].reshape(NDEV, D_MODEL // NDEV, D_MODEL),
            big["w_xkv"], d_cw]
    parts = _scatter_blocks(full, "scatter_grads")
    parts = list(parts[:5]) + list(_exchange_wait(*g_flight[:4], parts[0], False, "scatter_ffn_wait")) + [parts[5]]
    small_like = dict(mix_pre=norm_mix_pre, mix_post=norm_mix_post, mem=norm_mem, xa_pre=norm_xa_pre,
                      xa_post=norm_xa_post, ffn_pre=norm_ffn_pre, ffn_post=norm_ffn_post, conv_b=conv_b,
                      w_pool=w_pool, pool_scale=pool_scale, b_forget=b_forget)
    small = {k: small[k].reshape(small_like[k].shape) for k in SMALL_ORDER}
    (small_parts,) = _all_gather([_pack_small(small)], "gather_small_grads")

    res = {}
    for k, p in zip(big_names, parts[:7]):
        if k == "w_in":
            p = p[:, :, :w_in.shape[2]]
        res[k] = [a[None] for a in _adamw(p, w[k][0], mo[k][0], vo[k][0], "adamw_" + k)]
    pad_cw = lambda a: jnp.pad(a[0, :, 0, :], ((0, 5), (0, 0)))
    res["conv_w"] = [a[:3][None, :, None, :] for a in
                     _adamw(parts[7], pad_cw(conv_w), pad_cw(m_conv_w), pad_cw(v_conv_w), "adamw_conv_w")]
    key_of = dict(mix_pre="norm_mix_pre", mix_post="norm_mix_post", mem="norm_mem", xa_pre="norm_xa_pre",
                  xa_post="norm_xa_post", ffn_pre="norm_ffn_pre", ffn_post="norm_ffn_post", conv_b="conv_b",
                  w_pool="w_pool", pool_scale="pool_scale", b_forget="b_forget")
    pack_of = lambda src: _pack_small({k: src[key_of[k]] for k in SMALL_ORDER})
    small_out = _adamw(small_parts, pack_of(w), pack_of(mo), pack_of(vo), "adamw_small")
    small_out = [_unpack_small(a, small_like) for a in small_out]
    for k in SMALL_ORDER:
        res[key_of[k]] = [so[k] for so in small_out]

    total = lax.psum(loss[0, 0], ("x", "y", "c"))
    outs = [total, grad_x[None]]
    for idx in range(4):
        outs += [res[k][idx] for k in names]
    return tuple(outs)
```

```python
import functools
import math

import jax
import jax.numpy as jnp
from jax import lax
from jax.experimental import pallas as pl
from jax.experimental.pallas import tpu as pltpu

F32 = jnp.float32
BF16 = jnp.bfloat16

NDEV = 8
D_MODEL = 1024
D_POOL = 256
D_FOX = 768
FOX_HEADS = 12
HEAD_PAIRS = FOX_HEADS // 2
XA_HEADS = 4
XA_DIM = 256
D_FF = 4096
D_IN_PAD = 2688
F_COL = 2560
POOL_HALO = 16
NORM_EPS = 1e-6
NEG = -1e30

ADAM_LR = 0.001
ADAM_B1 = 0.9
ADAM_B2 = 0.999
ADAM_EPS = 1e-08
ADAM_WD = 0.01
ADAM_STEP = 10

TM = 512
TQ = 512
TN_FF = 512
VMEM_LIMIT = 48 * 1024 * 1024
MESH = pl.DeviceIdType.MESH


def _cp(*sem):
    return pltpu.CompilerParams(dimension_semantics=sem, vmem_limit_bytes=VMEM_LIMIT)


def _dot(a, b, dims):
    return lax.dot_general(a, b, (dims, ((), ())), preferred_element_type=F32)


NN = ((1,), (0,))
NT = ((1,), (1,))
TN = ((0,), (0,))


def _mm(a, b, mode, out_dtype, tm, tn, tk, name, b_cols=None, out_cols=None):
    a_list = list(a) if isinstance(a, (list, tuple)) else [a]
    b_list = list(b) if isinstance(b, (list, tuple)) else [b]
    assert len(a_list) == 1 or len(b_list) == 1
    if mode == "tn":
        K, M = a_list[0].shape
        assert len(a_list) == 1
        Ns = [x.shape[1] for x in b_list]
        N = sum(Ns)
        assert b_cols is None
    else:
        assert len(b_list) == 1
        M = a_list[0].shape[0]
        Ks = [x.shape[1] for x in a_list]
        K = sum(Ks)
        if b_cols is None:
            N = b_list[0].shape[0] if mode == "nt" else b_list[0].shape[1]
        else:
            N = b_list[0].shape[1] if mode == "nt" else NDEV * b_cols
    assert M % tm == 0 and N % tn == 0 and K % tk == 0, (name, M, N, K)
    grid = (M // tm, N // tn, K // tk)
    nk = grid[2]
    dims = {"nn": NN, "nt": NT, "tn": TN}[mode]

    in_specs = []
    if mode == "tn":
        in_specs.append(pl.BlockSpec((tk, tm), lambda i, j, k: (k, i)))
        if len(b_list) == 1:
            in_specs.append(pl.BlockSpec((tk, tn), lambda i, j, k: (k, j)))
        else:
            nj1 = Ns[0] // tn
            in_specs.append(pl.BlockSpec((tk, tn), lambda i, j, k: (k, jnp.minimum(j, nj1 - 1))))
            in_specs.append(pl.BlockSpec((tk, tn), lambda i, j, k: (k, jnp.maximum(j - nj1, 0))))
    else:
        if len(a_list) == 1:
            in_specs.append(pl.BlockSpec((tm, tk), lambda i, j, k: (i, k)))
        else:
            nk1 = Ks[0] // tk
            in_specs.append(pl.BlockSpec((tm, tk), lambda i, j, k: (i, jnp.minimum(k, nk1 - 1))))
            in_specs.append(pl.BlockSpec((tm, tk), lambda i, j, k: (i, jnp.maximum(k - nk1, 0))))
        if b_cols is None:
            if mode == "nn":
                in_specs.append(pl.BlockSpec((tk, tn), lambda i, j, k: (k, j)))
            else:
                in_specs.append(pl.BlockSpec((tn, tk), lambda i, j, k: (j, k)))
        else:
            if mode == "nn":
                per = b_cols // tn
                in_specs.append(pl.BlockSpec((None, tk, tn), lambda i, j, k: (j // per, k, j % per)))
            else:
                per = b_cols // tk
                in_specs.append(pl.BlockSpec((None, tn, tk), lambda i, j, k: (k // per, j, k % per)))
    if out_cols is None:
        out_spec = pl.BlockSpec((tm, tn), lambda i, j, k: (i, j))
        out_shape = jax.ShapeDtypeStruct((M, N), out_dtype)
    else:
        pero = out_cols // tn
        out_spec = pl.BlockSpec((None, tm, tn), lambda i, j, k: (j // pero, i, j % pero))
        out_shape = jax.ShapeDtypeStruct((NDEV, M, out_cols), out_dtype)

    two_a = len(a_list) == 2
    two_b = len(b_list) == 2

    def body(*refs):
        o_ref, acc_ref = refs[-2], refs[-1]
        j = pl.program_id(1)
        k = pl.program_id(2)

        @pl.when(k == 0)
        def _():
            acc_ref[...] = jnp.zeros_like(acc_ref)

        if two_a:
            a1, a2, b1 = refs[0], refs[1], refs[2]
            nk1_ = Ks[0] // tk

            @pl.when(k < nk1_)
            def _():
                acc_ref[...] += _dot(a1[...], b1[...], dims)

            @pl.when(k >= nk1_)
            def _():
                acc_ref[...] += _dot(a2[...], b1[...], dims)
        elif two_b:
            a1, b1, b2 = refs[0], refs[1], refs[2]
            nj1_ = Ns[0] // tn

            @pl.when(j < nj1_)
            def _():
                acc_ref[...] += _dot(a1[...], b1[...], dims)

            @pl.when(j >= nj1_)
            def _():
                acc_ref[...] += _dot(a1[...], b2[...], dims)
        else:
            acc_ref[...] += _dot(refs[0][...], refs[1][...], dims)

        @pl.when(k == nk - 1)
        def _():
            o_ref[...] = acc_ref[...].astype(o_ref.dtype)

    return pl.pallas_call(
        body, name=name, grid=grid, in_specs=in_specs, out_specs=out_spec, out_shape=out_shape,
        scratch_shapes=[pltpu.VMEM((tm, tn), F32)],
        compiler_params=_cp("parallel", "parallel", "arbitrary"),
    )(*a_list, *b_list)


def _rstd(x):
    return lax.rsqrt(jnp.mean(x * x, axis=-1, keepdims=True) + NORM_EPS)


def _norm_bwd_rows(dxn, xn, r):
    return r * (dxn - xn * jnp.mean(dxn * xn, axis=-1, keepdims=True))


def _row_spec(tm, d):
    return pl.BlockSpec((tm, d), lambda i: (i, 0))


def _vec_spec(d):
    return pl.BlockSpec((1, d), lambda i: (0, 0))


def _norm_fwd(x, g, name):
    s, d = x.shape
    tm = min(TM, s)

    def body(x_ref, g_ref, h_ref):
        xv = x_ref[...]
        h_ref[...] = (xv * _rstd(xv) * g_ref[...]).astype(BF16)

    return pl.pallas_call(
        body, name=name, grid=(s // tm,), in_specs=[_row_spec(tm, d), _vec_spec(d)],
        out_specs=_row_spec(tm, d), out_shape=jax.ShapeDtypeStruct((s, d), BF16),
        compiler_params=_cp("parallel"),
    )(x, g)


def _resid_norm_fwd(x_in, y, g_post, g_next, name):
    s, d = x_in.shape

    def body(x_ref, y_ref, gp_ref, gn_ref, xo_ref, h_ref):
        yv = y_ref[...]
        xo = x_ref[...] + yv * _rstd(yv) * gp_ref[...]
        xo_ref[...] = xo
        h_ref[...] = (xo * _rstd(xo) * gn_ref[...]).astype(BF16)

    return pl.pallas_call(
        body, name=name, grid=(s // TM,),
        in_specs=[_row_spec(TM, d), _row_spec(TM, d), _vec_spec(d), _vec_spec(d)],
        out_specs=[_row_spec(TM, d), _row_spec(TM, d)],
        out_shape=[jax.ShapeDtypeStruct((s, d), F32), jax.ShapeDtypeStruct((s, d), BF16)],
        compiler_params=_cp("parallel"),
    )(x_in, y, g_post, g_next)


def _norm_bwd(dh, x, dx_res, g_pre, name, prev=None):
    s, d = x.shape
    tm = min(TM, s)
    has_prev = prev is not None

    def body(*refs):
        if has_prev:
            dh_ref, x_ref, dr_ref, g_ref, y_ref, gp_ref, dx_ref, dg_ref, dy_ref, dgp_ref = refs
        else:
            dh_ref, x_ref, dr_ref, g_ref, dx_ref, dg_ref = refs
        i = pl.program_id(0)
        xv = x_ref[...]
        r = _rstd(xv)
        xn = xv * r
        dhv = dh_ref[...].astype(F32)
        dx = dr_ref[...] + _norm_bwd_rows(dhv * g_ref[...], xn, r)
        dx_ref[...] = dx
        dg = jnp.sum(dhv * xn, axis=0, keepdims=True)

        @pl.when(i == 0)
        def _():
            dg_ref[...] = dg

        @pl.when(i > 0)
        def _():
            dg_ref[...] += dg

        if has_prev:
            yv = y_ref[...]
            r2 = _rstd(yv)
            yn = yv * r2
            dy_ref[...] = _norm_bwd_rows(dx * gp_ref[...], yn, r2).astype(BF16)
            dgp = jnp.sum(dx * yn, axis=0, keepdims=True)

            @pl.when(i == 0)
            def _():
                dgp_ref[...] = dgp

            @pl.when(i > 0)
            def _():
                dgp_ref[...] += dgp

    in_specs = [_row_spec(tm, d), _row_spec(tm, d), _row_spec(tm, d), _vec_spec(d)]
    out_specs = [_row_spec(tm, d), _vec_spec(d)]
    out_shape = [jax.ShapeDtypeStruct((s, d), F32), jax.ShapeDtypeStruct((1, d), F32)]
    args = [dh, x, dx_res, g_pre]
    if has_prev:
        in_specs += [_row_spec(tm, d), _vec_spec(d)]
        out_specs += [_row_spec(tm, d), _vec_spec(d)]
        out_shape += [jax.ShapeDtypeStruct((s, d), BF16), jax.ShapeDtypeStruct((1, d), F32)]
        args += list(prev)
    return pl.pallas_call(
        body, name=name, grid=(s // tm,), in_specs=in_specs, out_specs=out_specs, out_shape=out_shape,
        compiler_params=_cp("arbitrary"),
    )(*args)


def _loss_bwd(x2, y3, g_post, tgt, name):
    s, d = x2.shape

    def body(x_ref, y_ref, g_ref, t_ref, loss_ref, dx_ref, dy_ref, dg_ref):
        i = pl.program_id(0)
        yv = y_ref[...]
        r = _rstd(yv)
        yn = yv * r
        e = x_ref[...] + yn * g_ref[...] - t_ref[...]
        part = 0.5 * jnp.sum(jnp.mean(e * e, axis=-1, keepdims=True), axis=0, keepdims=True)
        dx = e * (1.0 / d)
        dx_ref[...] = dx
        dy_ref[...] = _norm_bwd_rows(dx * g_ref[...], yn, r).astype(BF16)
        dg = jnp.sum(dx * yn, axis=0, keepdims=True)
        part = jnp.broadcast_to(part, (1, 128))

        @pl.when(i == 0)
        def _():
            dg_ref[...] = dg
            loss_ref[...] = part

        @pl.when(i > 0)
        def _():
            dg_ref[...] += dg
            loss_ref[...] += part

    return pl.pallas_call(
        body, name=name, grid=(s // TM,),
        in_specs=[_row_spec(TM, d), _row_spec(TM, d), _vec_spec(d), _row_spec(TM, d)],
        out_specs=[_vec_spec(128), _row_spec(TM, d), _row_spec(TM, d), _vec_spec(d)],
        out_shape=[jax.ShapeDtypeStruct((1, 128), F32), jax.ShapeDtypeStruct((s, d), F32),
                   jax.ShapeDtypeStruct((s, d), BF16), jax.ShapeDtypeStruct((1, d), F32)],
        compiler_params=_cp("arbitrary"),
    )(x2, y3, g_post, tgt)


def _split3(v):
    hi = v.astype(BF16)
    r1 = v - hi.astype(F32)
    mid = r1.astype(BF16)
    lo = (r1 - mid.astype(F32)).astype(BF16)
    return hi, mid, lo


def _tri_dot(tri, v):
    hi, mid, lo = _split3(v)
    return _dot(tri, hi, NN) + _dot(tri, mid, NN) + _dot(tri, lo, NN)


def _gate_cumsum(fraw, b_pad, name):
    s = fraw.shape[0]

    def body(f_ref, b_ref, flog_ref, cum_ref, carry_ref):
        i = pl.program_id(0)

        @pl.when(i == 0)
        def _():
            carry_ref[...] = jnp.zeros_like(carry_ref)

        flog = f_ref[...] + b_ref[...]
        flog_ref[...] = flog
        lf = jnp.minimum(flog, 0.0) - jnp.log(1.0 + jnp.exp(-jnp.abs(flog)))
        lane = lax.broadcasted_iota(jnp.int32, (1, 128), 1)
        lf = jnp.where(lane < FOX_HEADS, lf, 0.0)
        row = lax.broadcasted_iota(jnp.int32, (TM, TM), 0)
        col = lax.broadcasted_iota(jnp.int32, (TM, TM), 1)
        tri = (row >= col).astype(BF16)
        cum = _tri_dot(tri, lf) + carry_ref[...]
        cum_ref[...] = cum
        carry_ref[...] = cum[TM - 1:TM, :]

    return pl.pallas_call(
        body, name=name, grid=(s // TM,),
        in_specs=[_row_spec(TM, 128), _vec_spec(128)],
        out_specs=[_row_spec(TM, 128), _row_spec(TM, 128)],
        out_shape=[jax.ShapeDtypeStruct((s, 128), F32), jax.ShapeDtypeStruct((s, 128), F32)],
        scratch_shapes=[pltpu.VMEM((1, 128), F32)],
        compiler_params=_cp("arbitrary"),
    )(fraw, b_pad)


def _gate_bwd(qaux, kaux, flog, name):
    s = flog.shape[0]
    n = s // TM

    def body(qa_ref, ka_ref, fl_ref, dp_ref, db_ref, carry_ref):
        i = pl.program_id(0)

        @pl.when(i == 0)
        def _():
            carry_ref[...] = jnp.zeros_like(carry_ref)

        src = lax.broadcasted_iota(jnp.int32, (128, 128), 0)
        dst = lax.broadcasted_iota(jnp.int32, (128, 128), 1)
        dcum = jnp.zeros((TM, 128), F32)
        for p in range(HEAD_PAIRS):
            for ref, l0, l1, sign in ((qa_ref, 64, 0, 1.0), (ka_ref, 67, 3, -1.0)):
                hit = jnp.logical_or(jnp.logical_and(src == l0, dst == 2 * p),
                                     jnp.logical_and(src == l1, dst == 2 * p + 1))
                sel = jnp.where(hit, sign, 0.0).astype(BF16)
                for piece in _split3(ref[p]):
                    dcum = dcum + _dot(piece, sel, NN)
        row = lax.broadcasted_iota(jnp.int32, (TM, TM), 0)
        col = lax.broadcasted_iota(jnp.int32, (TM, TM), 1)
        tri = (row <= col).astype(BF16)
        dlf = _tri_dot(tri, dcum) + carry_ref[...]
        carry_ref[...] = dlf[0:1, :]
        lane = lax.broadcasted_iota(jnp.int32, (1, 128), 1)
        df = jnp.where(lane < FOX_HEADS, dlf / (1.0 + jnp.exp(fl_ref[...])), 0.0)
        dp_ref[...] = df.astype(BF16)
        db = jnp.sum(df, axis=0, keepdims=True)

        @pl.when(i == 0)
        def _():
            db_ref[...] = db

        @pl.when(i > 0)
        def _():
            db_ref[...] += db

    rev = lambda i: (n - 1 - i, 0)
    return pl.pallas_call(
        body, name=name, grid=(n,),
        in_specs=[pl.BlockSpec((HEAD_PAIRS, TM, 128), lambda i: (0, n - 1 - i, 0)),
                  pl.BlockSpec((HEAD_PAIRS, TM, 128), lambda i: (0, n - 1 - i, 0)), pl.BlockSpec((TM, 128), rev)],
        out_specs=[pl.BlockSpec((TM, 128), rev), _vec_spec(128)],
        out_shape=[jax.ShapeDtypeStruct((s, 128), BF16), jax.ShapeDtypeStruct((1, 128), F32)],
        scratch_shapes=[pltpu.VMEM((1, 128), F32)],
        compiler_params=_cp("arbitrary"),
    )(qaux, kaux, flog)


def _pool_consts(i, rows):
    lane = lax.broadcasted_iota(jnp.int32, (rows, D_POOL), 1)
    t1 = lax.broadcasted_iota(jnp.int32, (rows, D_POOL), 0) + i * TM + 1
    win = jnp.where(lane < 64, 2, jnp.where(lane < 128, 4, jnp.where(lane < 192, 8, 16)))
    inv = 1.0 / jnp.minimum(t1, win).astype(F32)
    return lane, inv


def _by_group(lane, s2, s4, s8, s16):
    return jnp.where(lane < 64, s2, jnp.where(lane < 128, s4, jnp.where(lane < 192, s8, s16)))


def _pool_diff(i, u_ref, halo_ref):
    u = u_ref[...].astype(F32)
    halo = jnp.where(i > 0, halo_ref[...].astype(F32), 0.0)
    ext = jnp.concatenate([halo, u], axis=0)
    s2 = ext + pltpu.roll(ext, 1, 0)
    s4 = s2 + pltpu.roll(s2, 2, 0)
    s8 = s4 + pltpu.roll(s4, 4, 0)
    s16 = s8 + pltpu.roll(s8, 8, 0)
    lane, inv = _pool_consts(i, TM)
    sel = _by_group(lane, s2[POOL_HALO:], s4[POOL_HALO:], s8[POOL_HALO:], s16[POOL_HALO:])
    return sel * inv - u


def _pool_fwd(proj, wbd, scale, ycat, name):
    s = proj.shape[0]
    hb = TM // POOL_HALO

    def body(u_ref, halo_ref, w_ref, sc_ref, y_any, y_ref):
        del y_any
        i = pl.program_id(0)
        diff = _pool_diff(i, u_ref, halo_ref)
        mixed = _dot(diff.astype(BF16), w_ref[...], NN)
        y_ref[...] = (mixed * sc_ref[...]).astype(BF16)

    return pl.pallas_call(
        body, name=name, grid=(s // TM,),
        in_specs=[pl.BlockSpec((TM, D_POOL), lambda i: (i, 0)),
                  pl.BlockSpec((POOL_HALO, D_POOL), lambda i: (jnp.maximum(i * hb - 1, 0), 0)),
                  pl.BlockSpec((D_POOL, D_POOL), lambda i: (0, 0)), _vec_spec(D_POOL),
                  pl.BlockSpec(memory_space=pl.ANY)],
        out_specs=pl.BlockSpec((TM, D_POOL), lambda i: (i, 0)),
        out_shape=jax.ShapeDtypeStruct(ycat.shape, ycat.dtype),
        input_output_aliases={4: 0},
        compiler_params=_cp("parallel"),
    )(proj, proj, wbd, scale, ycat)


def _pool_bwd(proj, dycat, wbd, scale, name):
    s = proj.shape[0]
    n = s // TM
    hb = TM // POOL_HALO
    last_halo = s // POOL_HALO - 1

    def body(u_ref, halo_ref, dy_ref, dyp_ref, w_ref, sc_ref, dp_ref, dw_ref, dsc_ref):
        i = pl.program_id(0)
        diff = _pool_diff(i, u_ref, halo_ref)
        diff_b = diff.astype(BF16)
        mixed = _dot(diff_b, w_ref[...], NN)
        dy = dy_ref[...].astype(F32)
        dmix = (dy * sc_ref[...]).astype(BF16)
        dyp = jnp.where(i < n - 1, dyp_ref[...].astype(F32), 0.0)
        dmix_p = (dyp * sc_ref[...]).astype(BF16)
        dd = _dot(dmix, w_ref[...], NT)
        dd_p = _dot(dmix_p, w_ref[...], NT)
        lane, inv = _pool_consts(i, TM)
        _, inv_p = _pool_consts(i + 1, POOL_HALO)
        ext = jnp.concatenate([dd * inv, dd_p * inv_p], axis=0)
        rows = TM + POOL_HALO
        l2 = ext + pltpu.roll(ext, rows - 1, 0)
        l4 = l2 + pltpu.roll(l2, rows - 2, 0)
        l8 = l4 + pltpu.roll(l4, rows - 4, 0)
        l16 = l8 + pltpu.roll(l8, rows - 8, 0)
        du = _by_group(lane, l2[:TM], l4[:TM], l8[:TM], l16[:TM]) - dd
        dp_ref[...] = du.astype(BF16)
        dw = _dot(diff_b, dmix, TN)
        dsc = jnp.sum(dy * mixed, axis=0, keepdims=True)

        @pl.when(i == 0)
        def _():
            dw_ref[...] = dw
            dsc_ref[...] = dsc

        @pl.when(i > 0)
        def _():
            dw_ref[...] += dw
            dsc_ref[...] += dsc

    return pl.pallas_call(
        body, name=name, grid=(n,),
        in_specs=[pl.BlockSpec((TM, D_POOL), lambda i: (i, 0)),
                  pl.BlockSpec((POOL_HALO, D_POOL), lambda i: (jnp.maximum(i * hb - 1, 0), 0)),
                  pl.BlockSpec((TM, D_POOL), lambda i: (i, 0)),
                  pl.BlockSpec((POOL_HALO, D_POOL), lambda i: (jnp.minimum((i + 1) * hb, last_halo), 0)),
                  pl.BlockSpec((D_POOL, D_POOL), lambda i: (0, 0)), _vec_spec(D_POOL)],
        out_specs=[pl.BlockSpec((TM, D_POOL), lambda i: (i, 0)),
                   pl.BlockSpec((D_POOL, D_POOL), lambda i: (0, 0)), _vec_spec(D_POOL)],
        out_shape=[jax.ShapeDtypeStruct((s, D_POOL), BF16),
                   jax.ShapeDtypeStruct((D_POOL, D_POOL), F32), jax.ShapeDtypeStruct((1, D_POOL), F32)],
        compiler_params=_cp("arbitrary"),
    )(proj, proj, dycat, dycat, wbd, scale)


Q_BLK = D_POOL // 128
K_BLK = Q_BLK + D_FOX // 128
V_BLK = K_BLK + D_FOX // 128


def _head_masks():
    lane = lax.broadcasted_iota(jnp.int32, (1, 128), 1)
    return [lane < 64, lane >= 64]


def _fox_scores(qh, k2, cq_col, ck_row, row_off):
    sc = _dot(qh, k2, NT) * 0.125 + cq_col - ck_row
    row = lax.broadcasted_iota(jnp.int32, sc.shape, 0) + row_off
    col = lax.broadcasted_iota(jnp.int32, sc.shape, 1)
    return jnp.where(row >= col, sc, NEG)


def _fox_fwd(proj, cum_c, cum_r, name):
    s = proj.shape[0]
    nq = s // TQ

    def body(q_ref, k_ref, v_ref, cq_ref, ck_ref, o_ref, lse_ref, m_ref, l_ref, acc_ref):
        qi, ki = pl.program_id(1), pl.program_id(2)
        masks = _head_masks()

        @pl.when(ki == 0)
        def _():
            m_ref[...] = jnp.full_like(m_ref, NEG)
            l_ref[...] = jnp.zeros_like(l_ref)
            acc_ref[...] = jnp.zeros_like(acc_ref)

        @pl.when(ki <= qi)
        def _():
            q2, k2, v2 = q_ref[...], k_ref[...], v_ref[...]
            cq, ck = cq_ref[...], ck_ref[...]
            pv = []
            alpha = []
            for hh in range(2):
                qh = jnp.where(masks[hh], q2, jnp.zeros_like(q2))
                vh = jnp.where(masks[hh], v2, jnp.zeros_like(v2))
                sc = _fox_scores(qh, k2, cq[:, hh:hh + 1], ck[hh:hh + 1, :], (qi - ki) * TQ)
                m_prev = m_ref[hh]
                m_new = jnp.maximum(m_prev, jnp.max(sc, axis=1, keepdims=True))
                a = jnp.exp(m_prev - m_new)
                p = jnp.exp(sc - m_new)
                l_ref[hh] = a * l_ref[hh] + jnp.sum(p, axis=1, keepdims=True)
                m_ref[hh] = m_new
                pv.append(_dot(p.astype(BF16), vh, NN))
                alpha.append(a)
            acc_ref[...] = acc_ref[...] * jnp.where(masks[0], alpha[0], alpha[1]) + pv[0] + pv[1]

        @pl.when(ki == qi)
        def _():
            inv = jnp.where(masks[0], 1.0 / l_ref[0], 1.0 / l_ref[1])
            o_ref[...] = (acc_ref[...] * inv).astype(BF16)
            lane = lax.broadcasted_iota(jnp.int32, (1, 128), 1)
            lse_ref[...] = jnp.where(lane == 0, m_ref[0] + jnp.log(l_ref[0]), m_ref[1] + jnp.log(l_ref[1]))

    kv_row = lambda p, qi, ki: jnp.minimum(ki, qi)
    return pl.pallas_call(
        body, name=name, grid=(HEAD_PAIRS, nq, nq),
        in_specs=[pl.BlockSpec((TQ, 128), lambda p, qi, ki: (qi, Q_BLK + p)),
                  pl.BlockSpec((TQ, 128), lambda p, qi, ki: (kv_row(p, qi, ki), K_BLK + p)),
                  pl.BlockSpec((TQ, 128), lambda p, qi, ki: (kv_row(p, qi, ki), V_BLK + p)),
                  pl.BlockSpec((None, TQ, 128), lambda p, qi, ki: (p, qi, 0)),
                  pl.BlockSpec((None, 8, TQ), lambda p, qi, ki: (p, 0, kv_row(p, qi, ki)))],
        out_specs=[pl.BlockSpec((TQ, 128), lambda p, qi, ki: (qi, Q_BLK + p)),
                   pl.BlockSpec((None, TQ, 128), lambda p, qi, ki: (p, qi, 0))],
        out_shape=[jax.ShapeDtypeStruct((s, D_MODEL), BF16), jax.ShapeDtypeStruct((HEAD_PAIRS, s, 128), F32)],
        scratch_shapes=[pltpu.VMEM((2, TQ, 1), F32), pltpu.VMEM((2, TQ, 1), F32), pltpu.VMEM((TQ, 128), F32)],
        compiler_params=_cp("parallel", "parallel", "arbitrary"),
    )(proj, proj, proj, cum_c, cum_r)


def _fox_bwd(proj, ycat, dycat, cum_c, cum_r, lse, name):
    s = proj.shape[0]
    nq = s // TQ

    def body(q_ref, k_ref, v_ref, do_ref, o_ref, cq_ref, ck_ref, lse_ref,
             dq_ref, dk_ref, dv_ref, dc_ref, dcq_ref, dq_acc, dk_acc, dv_acc, dc_acc, dcq_acc):
        ki, qi = pl.program_id(1), pl.program_id(2)
        masks = _head_masks()
        lane = lax.broadcasted_iota(jnp.int32, (1, 128), 1)

        @pl.when(qi == ki)
        def _():
            dk_acc[...] = jnp.zeros_like(dk_acc)
            dv_acc[...] = jnp.zeros_like(dv_acc)
            dc_acc[...] = jnp.zeros_like(dc_acc)

        @pl.when(qi >= ki)
        def _():
            q2, k2, v2, do2 = q_ref[...], k_ref[...], v_ref[...], do_ref[...]
            cq, ck, lse2 = cq_ref[...], ck_ref[...], lse_ref[...]
            dd = do2.astype(F32) * o_ref[...].astype(F32)
            d0 = jnp.sum(jnp.where(masks[0], dd, 0.0), axis=1, keepdims=True)
            drow = [d0, jnp.sum(dd, axis=1, keepdims=True) - d0]
            dq = jnp.zeros((TQ, 128), F32)
            dk = jnp.zeros((TQ, 128), F32)
            dv = jnp.zeros((TQ, 128), F32)
            dcs = []
            rs = []
            for hh in range(2):
                zero = jnp.zeros_like(q2)
                qh = jnp.where(masks[hh], q2, zero)
                kh = jnp.where(masks[hh], k2, zero)
                doh = jnp.where(masks[hh], do2, zero)
                sc = _fox_scores(qh, k2, cq[:, hh:hh + 1], ck[hh:hh + 1, :], (qi - ki) * TQ)
                p = jnp.exp(sc - lse2[:, hh:hh + 1])
                dp = _dot(doh, v2, NT)
                ds = p * (dp - drow[hh])
                dsb = (ds * 0.125).astype(BF16)
                dv = dv + _dot(p.astype(BF16), doh, TN)
                dk = dk + _dot(dsb, qh, TN)
                dq = dq + _dot(dsb, kh, NN)
                dcs.append(-jnp.sum(ds, axis=0, keepdims=True))
                rs.append(jnp.sum(ds, axis=1, keepdims=True))
            dk_acc[...] += dk
            dv_acc[...] += dv
            dc_acc[0:1, :] += dcs[0]
            dc_acc[1:2, :] += dcs[1]
            dcq = jnp.where(lane == 0, rs[0], rs[1])

            @pl.when(ki == 0)
            def _():
                dq_acc[qi] = dq
                dcq_acc[qi] = dcq

            @pl.when(ki > 0)
            def _():
                dq_acc[qi] += dq
                dcq_acc[qi] += dcq

            @pl.when(qi == ki)
            def _():
                rows = pl.ds(pl.multiple_of(qi * TQ, TQ), TQ)
                dq_ref[rows, :] = dq_acc[qi].astype(BF16)
                dcq_ref[rows, :] = dcq_acc[qi]

        @pl.when(qi == nq - 1)
        def _():
            dk_ref[...] = dk_acc[...].astype(BF16)
            dv_ref[...] = dv_acc[...].astype(BF16)
            dc_ref[...] = dc_acc[...]

    q_row = lambda p, ki, qi: jnp.maximum(qi, ki)
    return pl.pallas_call(
        body, name=name, grid=(HEAD_PAIRS, nq, nq),
        in_specs=[pl.BlockSpec((TQ, 128), lambda p, ki, qi: (q_row(p, ki, qi), Q_BLK + p)),
                  pl.BlockSpec((TQ, 128), lambda p, ki, qi: (ki, K_BLK + p)),
                  pl.BlockSpec((TQ, 128), lambda p, ki, qi: (ki, V_BLK + p)),
                  pl.BlockSpec((TQ, 128), lambda p, ki, qi: (q_row(p, ki, qi), Q_BLK + p)),
                  pl.BlockSpec((TQ, 128), lambda p, ki, qi: (q_row(p, ki, qi), Q_BLK + p)),
                  pl.BlockSpec((None, TQ, 128), lambda p, ki, qi: (p, q_row(p, ki, qi), 0)),
                  pl.BlockSpec((None, 8, TQ), lambda p, ki, qi: (p, 0, ki)),
                  pl.BlockSpec((None, TQ, 128), lambda p, ki, qi: (p, q_row(p, ki, qi), 0))],
        out_specs=[pl.BlockSpec((s, 128), lambda p, ki, qi: (0, p)),
                   pl.BlockSpec((TQ, 128), lambda p, ki, qi: (ki, p)),
                   pl.BlockSpec((TQ, 128), lambda p, ki, qi: (ki, p)),
                   pl.BlockSpec((None, 8, TQ), lambda p, ki, qi: (p, 0, ki)),
                   pl.BlockSpec((None, s, 128), lambda p, ki, qi: (p, 0, 0))],
        out_shape=[jax.ShapeDtypeStruct((s, D_FOX), BF16)] * 3 + [jax.ShapeDtypeStruct((HEAD_PAIRS, 8, s), F32),
                                                                 jax.ShapeDtypeStruct((HEAD_PAIRS, s, 128), F32)],
        scratch_shapes=[pltpu.VMEM((nq, TQ, 128), F32), pltpu.VMEM((TQ, 128), F32), pltpu.VMEM((TQ, 128), F32),
                        pltpu.VMEM((8, TQ), F32), pltpu.VMEM((nq, TQ, 128), F32)],
        compiler_params=_cp("arbitrary", "arbitrary", "arbitrary"),
    )(proj, proj, proj, dycat, ycat, cum_c, cum_r, lse)


def _operand_lanes(v0, v1, ones_off):
    lane = lax.broadcasted_iota(jnp.int32, (1, 128), 1)
    out = jnp.zeros((v0.shape[0], 128), F32)
    if ones_off is not None:
        half = lane & 63
        out = out + jnp.where(jnp.logical_and(half >= ones_off, half < ones_off + 3), 1.0, 0.0)
    for base, v in ((64, v0), (0, v1)):
        for j, piece in enumerate(_split3(v)):
            out = jnp.where(lane == base + j, piece.astype(F32), out)
    return out.astype(BF16)


def _fox_operands(cum, name):
    s = cum.shape[0]
    width = HEAD_PAIRS * 128

    def body(c_ref, aq_ref, ak_ref):
        pieces = _split3(c_ref[...])
        row = lax.broadcasted_iota(jnp.int32, (128, width), 0)
        col = lax.broadcasted_iota(jnp.int32, (128, width), 1)
        base = (row >> 1) * 128 + (1 - (row & 1)) * 64
        half = lax.broadcasted_iota(jnp.int32, (1, width), 1) & 63
        for o_ref, off, sign, ones_off in ((aq_ref, 0, 1.0, 3), (ak_ref, 3, -1.0, 0)):
            out = jnp.where(jnp.logical_and(half >= ones_off, half < ones_off + 3), 1.0, 0.0)
            for j, piece in enumerate(pieces):
                sel = jnp.where(jnp.logical_and(col == base + off + j, row < FOX_HEADS), sign, 0.0).astype(BF16)
                out = out + _dot(piece, sel, NN)
            o_ref[...] = out.astype(BF16)

    return pl.pallas_call(
        body, name=name, grid=(s // TM,), in_specs=[_row_spec(TM, 128)],
        out_specs=[_row_spec(TM, width), _row_spec(TM, width)],
        out_shape=[jax.ShapeDtypeStruct((s, width), BF16)] * 2,
        compiler_params=_cp("parallel"),
    )(cum)


def _fox_do_operand(dycat, ycat, name):
    s = dycat.shape[0]

    def body(do_ref, o_ref, ad_ref):
        lane = lax.broadcasted_iota(jnp.int32, (1, 128), 1)
        dd = do_ref[...].astype(F32) * o_ref[...].astype(F32)
        d0 = jnp.sum(jnp.where(lane < 64, dd, 0.0), axis=1, keepdims=True)
        d1 = jnp.sum(dd, axis=1, keepdims=True) - d0
        ad_ref[...] = _operand_lanes(-d0, -d1, None)

    blk = pl.BlockSpec((TM, 128), lambda i, p: (i, Q_BLK + p))
    return pl.pallas_call(
        body, name=name, grid=(s // TM, HEAD_PAIRS), in_specs=[blk, blk],
        out_specs=pl.BlockSpec((TM, 128), lambda i, p: (i, p)),
        out_shape=jax.ShapeDtypeStruct((s, HEAD_PAIRS * 128), BF16),
        compiler_params=_cp("parallel", "parallel"),
    )(dycat, ycat)


def _causal_pairs(nq, key_major):
    if key_major:
        pairs = [(q, k) for k in range(nq) for q in range(k, nq)]
    else:
        pairs = [(q, k) for q in range(nq) for k in range(q + 1)]
    return (jnp.asarray([p[0] for p in pairs], jnp.int32), jnp.asarray([p[1] for p in pairs], jnp.int32))


def _diag_mask(sc):
    row = lax.broadcasted_iota(jnp.int32, sc.shape, 0)
    col = lax.broadcasted_iota(jnp.int32, sc.shape, 1)
    return jnp.where(row >= col, sc, NEG)


def _fox_fwd2(proj, aq, ak, name):
    s = proj.shape[0]
    nq = s // TQ
    qi_arr, ki_arr = _causal_pairs(nq, key_major=False)

    def body(qi_ref, ki_ref, q_ref, k_ref, v_ref, aq_ref, ak_ref, o_ref, aqb_ref, m_ref, acc_ref, aux_ref):
        t = pl.program_id(1)
        qi, ki = qi_ref[t], ki_ref[t]
        lane = lax.broadcasted_iota(jnp.int32, (1, 128), 1)
        masks = [lane < 64, lane >= 64]
        ones_v = jnp.where((lane & 63) == 8, 1.0, 0.0).astype(BF16)

        @pl.when(ki == 0)
        def _():
            m_ref[...] = jnp.full_like(m_ref, NEG)
            acc_ref[...] = jnp.zeros_like(acc_ref)
            aux_ref[...] = jnp.zeros_like(aux_ref)

        def step(diag):
            q2s = q_ref[...] * 0.125
            k2, v2, aq2, ak2 = k_ref[...], v_ref[...], aq_ref[...], ak_ref[...]
            pv, alpha = [], []
            for hh in range(2):
                qh = jnp.where(masks[hh], q2s, aq2)
                kh = jnp.where(masks[hh], k2, ak2)
                vh = jnp.where(masks[hh], v2, ones_v)
                sc = _dot(qh, kh, NT)
                if diag:
                    sc = _diag_mask(sc)
                m_prev = m_ref[hh]
                m_new = jnp.maximum(m_prev, jnp.max(sc, axis=1, keepdims=True))
                m_ref[hh] = m_new
                alpha.append(jnp.exp(m_prev - m_new))
                pv.append(_dot(jnp.exp(sc - m_new).astype(BF16), vh, NN))
            acc_ref[...] = acc_ref[...] * jnp.where(masks[0], alpha[0], alpha[1]) + jnp.where(masks[0], pv[0], pv[1])
            aux_ref[...] = aux_ref[...] * jnp.where(masks[0], alpha[1], alpha[0]) + jnp.where(masks[0], pv[1], pv[0])

        @pl.when(ki < qi)
        def _():
            step(False)

        @pl.when(ki == qi)
        def _():
            step(True)
            aux = aux_ref[...]
            l0, l1 = aux[:, 72:73], aux[:, 8:9]
            o_ref[...] = (acc_ref[...] * jnp.where(masks[0], 1.0 / l0, 1.0 / l1)).astype(BF16)
            aqf = aq_ref[...].astype(F32)
            cum0 = aqf[:, 64:65] + aqf[:, 65:66] + aqf[:, 66:67]
            cum1 = aqf[:, 0:1] + aqf[:, 1:2] + aqf[:, 2:3]
            aqb_ref[...] = _operand_lanes(cum0 - (m_ref[0] + jnp.log(l0)), cum1 - (m_ref[1] + jnp.log(l1)), 3)

    grid_spec = pltpu.PrefetchScalarGridSpec(
        num_scalar_prefetch=2, grid=(HEAD_PAIRS, int(qi_arr.shape[0])),
        in_specs=[pl.BlockSpec((TQ, 128), lambda p, t, qi, ki: (qi[t], Q_BLK + p)),
                  pl.BlockSpec((TQ, 128), lambda p, t, qi, ki: (ki[t], K_BLK + p)),
                  pl.BlockSpec((TQ, 128), lambda p, t, qi, ki: (ki[t], V_BLK + p)),
                  pl.BlockSpec((TQ, 128), lambda p, t, qi, ki: (qi[t], p)),
                  pl.BlockSpec((TQ, 128), lambda p, t, qi, ki: (ki[t], p))],
        out_specs=[pl.BlockSpec((TQ, 128), lambda p, t, qi, ki: (qi[t], Q_BLK + p)),
                   pl.BlockSpec((TQ, 128), lambda p, t, qi, ki: (qi[t], p))],
        scratch_shapes=[pltpu.VMEM((2, TQ, 1), F32), pltpu.VMEM((TQ, 128), F32), pltpu.VMEM((TQ, 128), F32)])
    return pl.pallas_call(
        body, name=name, grid_spec=grid_spec,
        out_shape=[jax.ShapeDtypeStruct((s, D_MODEL), BF16), jax.ShapeDtypeStruct((s, HEAD_PAIRS * 128), BF16)],
        compiler_params=_cp("parallel", "arbitrary"),
    )(qi_arr, ki_arr, proj, proj, proj, aq, ak)


def _fox_bwd2(proj, dycat, aqb, ak, ad, name):
    s = proj.shape[0]
    nq = s // TQ
    qi_arr, ki_arr = _causal_pairs(nq, key_major=True)

    def body(qi_ref, ki_ref, q_ref, k_ref, v_ref, do_ref, aq_ref, ak_ref, ad_ref,
             dq_ref, dk_ref, dv_ref, qaux_ref, kaux_ref, dq_acc, qaux_acc, dk_acc, dv_acc, kaux_acc):
        t = pl.program_id(1)
        qi, ki = qi_ref[t], ki_ref[t]
        lane = lax.broadcasted_iota(jnp.int32, (1, 128), 1)
        masks = [lane < 64, lane >= 64]
        ones_v = jnp.where((lane & 63) < 3, 1.0, 0.0).astype(BF16)

        @pl.when(qi == ki)
        def _():
            dk_acc[...] = jnp.zeros_like(dk_acc)
            dv_acc[...] = jnp.zeros_like(dv_acc)
            kaux_acc[...] = jnp.zeros_like(kaux_acc)

        def step(diag):
            q2s = q_ref[...] * 0.125
            k2, v2, do2 = k_ref[...], v_ref[...], do_ref[...]
            aq2, ak2, ad2 = aq_ref[...], ak_ref[...], ad_ref[...]
            dq, dk, dv = [], [], []
            for hh in range(2):
                qh = jnp.where(masks[hh], q2s, aq2)
                kh = jnp.where(masks[hh], k2, ak2)
                doh = jnp.where(masks[hh], do2, ad2)
                vh = jnp.where(masks[hh], v2, ones_v)
                sc = _dot(qh, kh, NT)
                if diag:
                    sc = _diag_mask(sc)
                p = jnp.exp(sc)
                dsb = (p * _dot(doh, vh, NT)).astype(BF16)
                dv.append(_dot(p.astype(BF16), doh, TN))
                dk.append(_dot(dsb, qh, TN))
                dq.append(_dot(dsb, kh, NN))
            dk_acc[...] += jnp.where(masks[0], dk[0], dk[1])
            kaux_acc[...] += jnp.where(masks[0], dk[1], dk[0])
            dv_acc[...] += jnp.where(masks[0], dv[0], dv[1])
            dq_new = jnp.where(masks[0], dq[0], dq[1])
            qaux_new = jnp.where(masks[0], dq[1], dq[0])

            @pl.when(ki == 0)
            def _():
                dq_acc[qi] = dq_new
                qaux_acc[qi] = qaux_new

            @pl.when(ki > 0)
            def _():
                dq_acc[qi] += dq_new
                qaux_acc[qi] += qaux_new

        @pl.when(qi > ki)
        def _():
            step(False)

        @pl.when(qi == ki)
        def _():
            step(True)
            rows = pl.ds(pl.multiple_of(qi * TQ, TQ), TQ)
            dq_ref[rows, :] = (dq_acc[qi] * 0.125).astype(BF16)
            qaux_ref[rows, :] = qaux_acc[qi]

        @pl.when(qi == nq - 1)
        def _():
            dk_ref[...] = dk_acc[...].astype(BF16)
            dv_ref[...] = dv_acc[...].astype(BF16)
            kaux_ref[...] = kaux_acc[...]

    grid_spec = pltpu.PrefetchScalarGridSpec(
        num_scalar_prefetch=2, grid=(HEAD_PAIRS, int(qi_arr.shape[0])),
        in_specs=[pl.BlockSpec((TQ, 128), lambda p, t, qi, ki: (qi[t], Q_BLK + p)),
                  pl.BlockSpec((TQ, 128), lambda p, t, qi, ki: (ki[t], K_BLK + p)),
                  pl.BlockSpec((TQ, 128), lambda p, t, qi, ki: (ki[t], V_BLK + p)),
                  pl.BlockSpec((TQ, 128), lambda p, t, qi, ki: (qi[t], Q_BLK + p)),
                  pl.BlockSpec((TQ, 128), lambda p, t, qi, ki: (qi[t], p)),
                  pl.BlockSpec((TQ, 128), lambda p, t, qi, ki: (ki[t], p)),
                  pl.BlockSpec((TQ, 128), lambda p, t, qi, ki: (qi[t], p))],
        out_specs=[pl.BlockSpec((s, 128), lambda p, t, qi, ki: (0, p)),
                   pl.BlockSpec((TQ, 128), lambda p, t, qi, ki: (ki[t], p)),
                   pl.BlockSpec((TQ, 128), lambda p, t, qi, ki: (ki[t], p)),
                   pl.BlockSpec((None, s, 128), lambda p, t, qi, ki: (p, 0, 0)),
                   pl.BlockSpec((None, TQ, 128), lambda p, t, qi, ki: (p, ki[t], 0))],
        scratch_shapes=[pltpu.VMEM((nq, TQ, 128), F32), pltpu.VMEM((nq, TQ, 128), F32),
                        pltpu.VMEM((TQ, 128), F32), pltpu.VMEM((TQ, 128), F32), pltpu.VMEM((TQ, 128), F32)])
    return pl.pallas_call(
        body, name=name, grid_spec=grid_spec,
        out_shape=[jax.ShapeDtypeStruct((s, D_FOX), BF16)] * 3 + [jax.ShapeDtypeStruct((HEAD_PAIRS, s, 128), F32)] * 2,
        compiler_params=_cp("arbitrary", "arbitrary"),
    )(qi_arr, ki_arr, proj, proj, proj, dycat, aqb, ak, ad)


XA_SCALE = XA_DIM ** -0.5


def _xattn_fwd(q2, kv, name):
    s = q2.shape[0]
    m = kv.shape[0]

    def body(q_ref, kv_ref, o_ref):
        for h in range(XA_HEADS):
            c0 = h * XA_DIM
            sc = _dot(q_ref[:, c0:c0 + XA_DIM], kv_ref[:, c0:c0 + XA_DIM], NT) * XA_SCALE
            e = jnp.exp(sc - jnp.max(sc, axis=1, keepdims=True))
            p = e / jnp.sum(e, axis=1, keepdims=True)
            o_ref[:, c0:c0 + XA_DIM] = _dot(p.astype(BF16), kv_ref[:, D_MODEL + c0:D_MODEL + c0 + XA_DIM], NN).astype(BF16)

    return pl.pallas_call(
        body, name=name, grid=(s // TM,),
        in_specs=[_row_spec(TM, D_MODEL), pl.BlockSpec((m, 2 * D_MODEL), lambda i: (0, 0))],
        out_specs=_row_spec(TM, D_MODEL), out_shape=jax.ShapeDtypeStruct((s, D_MODEL), BF16),
        compiler_params=_cp("parallel"),
    )(q2, kv)


def _xattn_bwd(q2, kv, do2, name):
    s = q2.shape[0]
    m = kv.shape[0]

    def body(q_ref, kv_ref, do_ref, dq_ref, dkv_ref):
        i = pl.program_id(0)

        @pl.when(i == 0)
        def _():
            dkv_ref[...] = jnp.zeros_like(dkv_ref)

        for h in range(XA_HEADS):
            c0 = h * XA_DIM
            v0 = D_MODEL + c0
            qh = q_ref[:, c0:c0 + XA_DIM]
            kh = kv_ref[:, c0:c0 + XA_DIM]
            doh = do_ref[:, c0:c0 + XA_DIM]
            sc = _dot(qh, kh, NT) * XA_SCALE
            e = jnp.exp(sc - jnp.max(sc, axis=1, keepdims=True))
            p = e / jnp.sum(e, axis=1, keepdims=True)
            dp = _dot(doh, kv_ref[:, v0:v0 + XA_DIM], NT)
            ds = p * (dp - jnp.sum(p * dp, axis=1, keepdims=True))
            dsb = (ds * XA_SCALE).astype(BF16)
            dq_ref[:, c0:c0 + XA_DIM] = _dot(dsb, kh, NN).astype(BF16)
            dkv_ref[:, c0:c0 + XA_DIM] += _dot(dsb, qh, TN)
            dkv_ref[:, v0:v0 + XA_DIM] += _dot(p.astype(BF16), doh, TN)

    return pl.pallas_call(
        body, name=name, grid=(s // TM,),
        in_specs=[_row_spec(TM, D_MODEL), pl.BlockSpec((m, 2 * D_MODEL), lambda i: (0, 0)), _row_spec(TM, D_MODEL)],
        out_specs=[_row_spec(TM, D_MODEL), pl.BlockSpec((m, 2 * D_MODEL), lambda i: (0, 0))],
        out_shape=[jax.ShapeDtypeStruct((s, D_MODEL), BF16), jax.ShapeDtypeStruct((m, 2 * D_MODEL), F32)],
        compiler_params=_cp("arbitrary"),
    )(q2, kv, do2)


GELU_C = math.sqrt(2.0 / math.pi)
GELU_A = 0.044715


def _gelu(x):
    return 0.5 * x * (1.0 + jnp.tanh(GELU_C * (x + GELU_A * x * x * x)))


def _gelu_and_grad(x):
    t = jnp.tanh(GELU_C * (x + GELU_A * x * x * x))
    g = 0.5 * x * (1.0 + t)
    dg = 0.5 * (1.0 + t) + 0.5 * x * (1.0 - t * t) * GELU_C * (1.0 + 3.0 * GELU_A * x * x)
    return g, dg


def _shift_down(main, prev8):
    row = lax.broadcasted_iota(jnp.int32, main.shape, 0)
    s1 = jnp.where(row == 0, prev8[7:8, :], pltpu.roll(main, 1, 0))
    s2 = jnp.where(row == 0, prev8[6:7, :], jnp.where(row == 1, prev8[7:8, :], pltpu.roll(main, 2, 0)))
    return s1, s2


def _shift_up(main, next8):
    n = main.shape[0]
    row = lax.broadcasted_iota(jnp.int32, main.shape, 0)
    u1 = jnp.where(row == n - 1, next8[0:1, :], pltpu.roll(main, n - 1, 0))
    u2 = jnp.where(row == n - 2, next8[0:1, :], jnp.where(row == n - 1, next8[1:2, :], pltpu.roll(main, n - 2, 0)))
    return u1, u2


def _conv(h, s1, s2, w_ref, b_ref):
    return w_ref[0:1, :] * s2 + w_ref[1:2, :] * s1 + w_ref[2:3, :] * h + b_ref[...]


def _ffn_fwd(h3, w_up, cw, cb, w_down, name):
    s = h3.shape[0]
    tn = TN_FF
    nj = D_FF // tn
    per = D_MODEL // tn
    hb = TM // 8

    def body(h_ref, halo_ref, wg_ref, wu_ref, cwg_ref, cwu_ref, cbg_ref, cbu_ref, wd_ref,
             hg_ref, hu_ref, a_ref, y_ref):
        i, j = pl.program_id(0), pl.program_id(1)
        h = h_ref[...]
        halo = halo_ref[...]
        halo = jnp.where(i > 0, halo, jnp.zeros_like(halo))
        conv = []
        for w_ref, cw_ref, cb_ref, hid_ref in ((wg_ref, cwg_ref, cbg_ref, hg_ref), (wu_ref, cwu_ref, cbu_ref, hu_ref)):
            hm_b = _dot(h, w_ref[...], NN).astype(BF16)
            hid_ref[...] = hm_b
            hm = hm_b.astype(F32)
            hl = _dot(halo, w_ref[...], NN).astype(BF16).astype(F32)
            s1, s2 = _shift_down(hm, hl)
            conv.append(_conv(hm, s1, s2, cw_ref, cb_ref))
        a = (_gelu(conv[0]) * conv[1]).astype(BF16)
        a_ref[...] = a
        contrib = _dot(a, wd_ref[...], NN)

        @pl.when(j == 0)
        def _():
            y_ref[...] = contrib

        @pl.when(j > 0)
        def _():
            y_ref[...] += contrib

    return pl.pallas_call(
        body, name=name, grid=(s // TM, nj),
        in_specs=[pl.BlockSpec((TM, D_MODEL), lambda i, j: (i, 0)),
                  pl.BlockSpec((8, D_MODEL), lambda i, j: (jnp.maximum(i * hb - 1, 0), 0)),
                  pl.BlockSpec((None, D_MODEL, tn), lambda i, j: (j // per, 0, j % per)),
                  pl.BlockSpec((None, D_MODEL, tn), lambda i, j: (NDEV // 2 + j // per, 0, j % per)),
                  pl.BlockSpec((8, tn), lambda i, j: (0, j)),
                  pl.BlockSpec((8, tn), lambda i, j: (0, nj + j)),
                  pl.BlockSpec((1, tn), lambda i, j: (0, j)),
                  pl.BlockSpec((1, tn), lambda i, j: (0, nj + j)),
                  pl.BlockSpec((tn, D_MODEL), lambda i, j: (j, 0))],
        out_specs=[pl.BlockSpec((TM, tn), lambda i, j: (i, j)), pl.BlockSpec((TM, tn), lambda i, j: (i, j)),
                   pl.BlockSpec((TM, tn), lambda i, j: (i, j)), pl.BlockSpec((TM, D_MODEL), lambda i, j: (i, 0))],
        out_shape=[jax.ShapeDtypeStruct((s, D_FF), BF16), jax.ShapeDtypeStruct((s, D_FF), BF16),
                   jax.ShapeDtypeStruct((s, D_FF), BF16), jax.ShapeDtypeStruct((s, D_MODEL), F32)],
        compiler_params=_cp("parallel", "arbitrary"),
    )(h3, h3, w_up, w_up, cw, cw, cb, cb, w_down)


def _ffn_bwd(dy3, w_down, hid_g, hid_u, cw, cb, name):
    s = dy3.shape[0]
    n = s // TM
    tn = TN_FF
    nj = D_FF // tn
    hb = TM // 8
    last8 = s // 8 - 1

    def body(dy_ref, dyp_ref, wd_ref, hg_ref, hgl_ref, hgn_ref, hu_ref, hul_ref, hun_ref,
             cwg_ref, cwu_ref, cbg_ref, cbu_ref,
             dhg_ref, dhu_ref, dcwg_ref, dcwu_ref, dcbg_ref, dcbu_ref):
        i = pl.program_id(1)
        first, last = i == 0, i == n - 1
        da = _dot(dy_ref[...], wd_ref[...], NT)
        dyp = dyp_ref[...]
        dyp = jnp.where(last, jnp.zeros_like(dyp), dyp)
        da_n = _dot(dyp, wd_ref[...], NT)
        parts = []
        for h_ref, hl_ref, hn_ref, cw_ref, cb_ref in ((hg_ref, hgl_ref, hgn_ref, cwg_ref, cbg_ref),
                                                     (hu_ref, hul_ref, hun_ref, cwu_ref, cbu_ref)):
            hm = h_ref[...].astype(F32)
            hl = jnp.where(first, 0.0, hl_ref[...].astype(F32))
            hn = hn_ref[...].astype(F32)
            s1, s2 = _shift_down(hm, hl)
            c = _conv(hm, s1, s2, cw_ref, cb_ref)
            n1, n2 = _shift_down(hn, hm[TM - 8:, :])
            cn = _conv(hn, n1, n2, cw_ref, cb_ref)
            parts.append((hm, s1, s2, c, cn))
        g, dg = _gelu_and_grad(parts[0][3])
        gn, dgn = _gelu_and_grad(parts[0][4])
        dc_g = da * parts[1][3] * dg
        dc_u = da * g
        dcn_g = da_n * parts[1][4] * dgn
        dcn_u = da_n * gn
        outs = ((dc_g, dcn_g, parts[0], cwg_ref, dhg_ref, dcwg_ref, dcbg_ref),
                (dc_u, dcn_u, parts[1], cwu_ref, dhu_ref, dcwu_ref, dcbu_ref))
        for dc, dcn, (hm, s1, s2, _, _), cw_ref, dh_ref, dcw_ref, dcb_ref in outs:
            u1, u2 = _shift_up(dc, dcn)
            dh_ref[...] = (cw_ref[2:3, :] * dc + cw_ref[1:2, :] * u1 + cw_ref[0:1, :] * u2).astype(BF16)
            dcb = jnp.sum(dc, axis=0, keepdims=True)
            row8 = lax.broadcasted_iota(jnp.int32, (8, tn), 0)
            dcw = jnp.where(row8 == 0, jnp.sum(dc * s2, axis=0, keepdims=True),
                            jnp.where(row8 == 1, jnp.sum(dc * s1, axis=0, keepdims=True),
                                      jnp.where(row8 == 2, jnp.sum(dc * hm, axis=0, keepdims=True), 0.0)))

            @pl.when(first)
            def _():
                dcw_ref[...] = dcw
                dcb_ref[...] = dcb

            @pl.when(i > 0)
            def _():
                dcw_ref[...] += dcw
                dcb_ref[...] += dcb

    prev8 = lambda j, i: (jnp.maximum(i * hb - 1, 0), j)
    next8 = lambda j, i: (jnp.minimum((i + 1) * hb, last8), j)
    blk = lambda j, i: (i, j)
    col = lambda j, i: (0, j)
    colu = lambda j, i: (0, nj + j)
    return pl.pallas_call(
        body, name=name, grid=(nj, n),
        in_specs=[pl.BlockSpec((TM, D_MODEL), lambda j, i: (i, 0)),
                  pl.BlockSpec((8, D_MODEL), lambda j, i: (jnp.minimum((i + 1) * hb, last8), 0)),
                  pl.BlockSpec((tn, D_MODEL), lambda j, i: (j, 0)),
                  pl.BlockSpec((TM, tn), blk), pl.BlockSpec((8, tn), prev8), pl.BlockSpec((8, tn), next8),
                  pl.BlockSpec((TM, tn), blk), pl.BlockSpec((8, tn), prev8), pl.BlockSpec((8, tn), next8),
                  pl.BlockSpec((8, tn), col), pl.BlockSpec((8, tn), colu),
                  pl.BlockSpec((1, tn), col), pl.BlockSpec((1, tn), colu)],
        out_specs=[pl.BlockSpec((TM, tn), blk), pl.BlockSpec((TM, tn), blk),
                   pl.BlockSpec((8, tn), col), pl.BlockSpec((8, tn), col),
                   pl.BlockSpec((1, tn), col), pl.BlockSpec((1, tn), col)],
        out_shape=[jax.ShapeDtypeStruct((s, D_FF), BF16), jax.ShapeDtypeStruct((s, D_FF), BF16),
                   jax.ShapeDtypeStruct((8, D_FF), F32), jax.ShapeDtypeStruct((8, D_FF), F32),
                   jax.ShapeDtypeStruct((1, D_FF), F32), jax.ShapeDtypeStruct((1, D_FF), F32)],
        compiler_params=_cp("parallel", "arbitrary"),
    )(dy3, dy3, w_down, hid_g, hid_g, hid_g, hid_u, hid_u, hid_u, cw, cw, cb, cb)


def _slot(p):
    return 4 * p[0] + 2 * p[1] + p[2]


def _all_gather(shards, name):
    n = len(shards)

    def body(*refs):
        ins, outs = refs[:n], refs[n:2 * n]
        send_sems, recv_sems, local_sems = refs[2 * n:]
        x, y, c = lax.axis_index("x"), lax.axis_index("y"), lax.axis_index("c")
        me, sibling = (x, y, c), (x, y, 1 - c)
        chips = [(1 - x, y), (x, 1 - y), (1 - x, 1 - y)]

        def copy(a, k, block, to, from_input=False):
            dst = outs[a].at[_slot(block)]
            return pltpu.make_async_remote_copy(
                src_ref=ins[a] if from_input else dst, dst_ref=dst,
                send_sem=send_sems.at[a, k], recv_sem=recv_sems.at[a, k],
                device_id=to, device_id_type=MESH)

        mine = [pltpu.make_async_copy(ins[a], outs[a].at[_slot(me)], local_sems.at[a]) for a in range(n)]
        for cp in mine:
            cp.start()
        first = []
        for a in range(n):
            first.append(copy(a, 0, me, sibling, True))
            first += [copy(a, 1 + j, me, (*chip, c), True) for j, chip in enumerate(chips)]
        for cp in first:
            cp.start()
        passed = []
        for j, chip in enumerate(chips):
            for a in range(n):
                copy(a, 1 + j, (*chip, c), me).wait_recv()
                fwd = copy(a, 4 + j, (*chip, c), sibling)
                fwd.start()
                passed.append(fwd)
        for a in range(n):
            copy(a, 0, sibling, me).wait_recv()
            for j, chip in enumerate(chips):
                copy(a, 4 + j, (*chip, 1 - c), me).wait_recv()
        for cp in first + passed:
            cp.wait_send()
        for cp in mine:
            cp.wait()

    any_spec = pl.BlockSpec(memory_space=pl.ANY)
    return pl.pallas_call(
        body, name=name,
        in_specs=[any_spec] * n, out_specs=[any_spec] * n,
        out_shape=[jax.ShapeDtypeStruct((NDEV,) + s.shape, s.dtype) for s in shards],
        scratch_shapes=[pltpu.SemaphoreType.DMA((n, 7)), pltpu.SemaphoreType.DMA((n, 7)),
                        pltpu.SemaphoreType.DMA((n,))],
    )(*shards)


def _scatter_blocks(full, name):
    n = len(full)

    def body(*refs):
        ins, outs = refs[:n], refs[n:2 * n]
        send_sems, recv_sems, local_sems = refs[2 * n:]
        x, y, c = lax.axis_index("x"), lax.axis_index("y"), lax.axis_index("c")
        me = (x, y, c)
        copies = []
        for a in range(n):
            cp = pltpu.make_async_copy(ins[a].at[_slot(me)], outs[a].at[_slot(me)], local_sems.at[a])
            cp.start()
            copies.append(cp)
        remote = []
        for mask in range(1, NDEV):
            peer = (1 - x if mask & 4 else x, 1 - y if mask & 2 else y, 1 - c if mask & 1 else c)
            for a in range(n):
                cp = pltpu.make_async_remote_copy(
                    src_ref=ins[a].at[_slot(peer)], dst_ref=outs[a].at[_slot(me)],
                    send_sem=send_sems.at[a, mask - 1], recv_sem=recv_sems.at[a, mask - 1],
                    device_id=peer, device_id_type=MESH)
                cp.start()
                remote.append(cp)
        for cp in remote:
            cp.wait()
        for cp in copies:
            cp.wait()

    any_spec = pl.BlockSpec(memory_space=pl.ANY)
    return pl.pallas_call(
        body, name=name,
        in_specs=[any_spec] * n, out_specs=[any_spec] * n,
        out_shape=[jax.ShapeDtypeStruct(f.shape, f.dtype) for f in full],
        scratch_shapes=[pltpu.SemaphoreType.DMA((n, 7)), pltpu.SemaphoreType.DMA((n, 7)),
                        pltpu.SemaphoreType.DMA((n,))],
    )(*full)


def _peer_list(x, y, c):
    return [(1 - x if m & 4 else x, 1 - y if m & 2 else y, 1 - c if m & 1 else c) for m in range(1, NDEV)]


def _exchange_copies(src_refs, land_refs, send_sems, recv_sems, gather):
    x, y, c = lax.axis_index("x"), lax.axis_index("y"), lax.axis_index("c")
    me = (x, y, c)
    copies = []
    for m, peer in enumerate(_peer_list(x, y, c)):
        for a in range(len(src_refs)):
            copies.append(pltpu.make_async_remote_copy(
                src_ref=src_refs[a] if gather else src_refs[a].at[_slot(peer)], dst_ref=land_refs[a].at[_slot(me)],
                send_sem=send_sems.at[a * (NDEV - 1) + m], recv_sem=recv_sems.at[a * (NDEV - 1) + m],
                device_id=peer, device_id_type=MESH))
    return copies


def _exchange_start(srcs, lands, after, gather, name):
    n = len(srcs)
    hbm = pl.BlockSpec(memory_space=pltpu.HBM)

    def body(*refs):
        for cp in _exchange_copies(refs[:n], refs[n:2 * n], refs[2 * n + 1], refs[2 * n + 2], gather):
            cp.start()
        token = refs[-1]
        token[...] = jnp.zeros_like(token)

    outs = pl.pallas_call(
        body, name=name,
        out_shape=(pltpu.SemaphoreType.DMA((n * (NDEV - 1),)), pltpu.SemaphoreType.DMA((n * (NDEV - 1),)),
                   *[pltpu.HBM(a.shape, a.dtype) for a in list(srcs) + list(lands)],
                   jax.ShapeDtypeStruct((8, 128), F32)),
        in_specs=[hbm] * (2 * n) + [pl.BlockSpec(memory_space=pl.ANY)],
        out_specs=(pl.BlockSpec(memory_space=pltpu.SEMAPHORE), pl.BlockSpec(memory_space=pltpu.SEMAPHORE),
                   *[hbm] * (2 * n), pl.BlockSpec(memory_space=pltpu.VMEM)),
        input_output_aliases={i: 2 + i for i in range(2 * n)},
        compiler_params=pltpu.CompilerParams(has_side_effects=pltpu.SideEffectType.DATAFLOW_SIDE_EFFECTING),
    )(*[pltpu.with_memory_space_constraint(a, pltpu.HBM) for a in list(srcs) + list(lands)], after)
    return outs[0], outs[1], outs[2:2 + n], outs[2 + n:2 + 2 * n], outs[-1]


def _exchange_wait(send_sems, recv_sems, srcs, lands, after, gather, name):
    n = len(srcs)
    hbm = pl.BlockSpec(memory_space=pltpu.HBM)

    def body(*refs):
        for cp in _exchange_copies(refs[:n], refs[n:2 * n], refs[2 * n], refs[2 * n + 1], gather):
            cp.wait_send()
            cp.wait_recv()

    outs = pl.pallas_call(
        body, name=name,
        out_shape=tuple(pltpu.HBM(a.shape, a.dtype) for a in list(srcs) + list(lands)),
        in_specs=[hbm] * (2 * n) + [pl.BlockSpec(memory_space=pltpu.SEMAPHORE)] * 2 + [pl.BlockSpec(memory_space=pl.ANY)],
        out_specs=tuple([hbm] * (2 * n)),
        input_output_aliases={i: i for i in range(2 * n)},
        compiler_params=pltpu.CompilerParams(has_side_effects=pltpu.SideEffectType.DATAFLOW_SIDE_EFFECTING),
    )(*srcs, *lands, send_sems, recv_sems, after)
    return outs[n:]


def _own_slot(block):
    me = 4 * lax.axis_index("x") + 2 * lax.axis_index("y") + lax.axis_index("c")
    return lax.dynamic_update_slice(jnp.zeros((NDEV,) + block.shape, block.dtype), block[None], (me, 0, 0))


def _adamw(parts, w, m, v, name):
    r, c = w.shape
    tr = r if r * c <= 160 * 1024 else max(8, (160 * 1024 // c) // 8 * 8)
    while r % tr:
        tr -= 8
    bc1 = 1.0 - ADAM_B1 ** ADAM_STEP
    bc2 = 1.0 - ADAM_B2 ** ADAM_STEP

    def body(p_ref, w_ref, m_ref, v_ref, g_ref, d_ref, mo_ref, vo_ref):
        g = p_ref[0].astype(F32)
        for d in range(1, NDEV):
            g = g + p_ref[d].astype(F32)
        g_ref[...] = g
        mn = ADAM_B1 * m_ref[...] + (1.0 - ADAM_B1) * g
        vn = ADAM_B2 * v_ref[...] + (1.0 - ADAM_B2) * (g * g)
        mo_ref[...] = mn
        vo_ref[...] = vn
        d_ref[...] = -ADAM_LR * ((mn / bc1) / (jnp.sqrt(vn / bc2) + ADAM_EPS) + ADAM_WD * w_ref[...])

    spec = pl.BlockSpec((tr, c), lambda i: (i, 0))
    return pl.pallas_call(
        body, name=name, grid=(r // tr,),
        in_specs=[pl.BlockSpec((NDEV, tr, c), lambda i: (0, i, 0)), spec, spec, spec],
        out_specs=[spec] * 4, out_shape=[jax.ShapeDtypeStruct((r, c), F32)] * 4,
        compiler_params=_cp("parallel"),
    )(parts, w, m, v)


def _pair_cols(a):
    s = a.shape[0]
    t = a[:, :FOX_HEADS].reshape(s, HEAD_PAIRS, 2).transpose(1, 0, 2)
    return jnp.pad(t, ((0, 0), (0, 0), (0, 126)))


def _pair_rows(at):
    s = at.shape[1]
    return jnp.pad(at[:FOX_HEADS].reshape(HEAD_PAIRS, 2, s), ((0, 0), (0, 6), (0, 0)))


def _local_step(x, mem, tgt, gains, b_forget, w_pool, pool_scale, conv_b, w_in,
                mix_weights, ffn_weights, send_mix_grads, send_ffn_grads):
    w_f = w_in[:, F_COL:]
    b_pad = jnp.pad(b_forget, ((0, 0), (0, 128 - FOX_HEADS)))
    wbd = jnp.zeros((D_POOL, D_POOL), F32)
    for g in range(4):
        wbd = wbd.at[64 * g:64 * g + 64, 64 * g:64 * g + 64].set(w_pool[g])
    wbd = wbd.astype(BF16)
    scale = pool_scale.reshape(1, D_POOL)

    h1 = _norm_fwd(x, gains["mix_pre"], "norm_mix_pre")
    proj = _mm(h1, w_in, "nn", BF16, 1024, 384, 1024, "proj_in")
    fraw = _mm(h1, w_f, "nn", F32, 1024, 128, 1024, "proj_gate")
    flog, cum = _gate_cumsum(fraw, b_pad, "gate_cumsum")
    aq, ak = _fox_operands(cum, "fox_operands")
    ycat, aqb = _fox_fwd2(proj, aq, ak, "fox_fwd")
    ycat = _pool_fwd(proj, wbd, scale, ycat, "pool_fwd")
    w_mix, w_xq, w_xo, w_xkv = mix_weights(ycat)
    y1 = _mm(ycat, w_mix, "nn", F32, 1024, 1024, 1024, "mix_out")
    x1, h2 = _resid_norm_fwd(x, y1, gains["mix_post"], gains["xa_pre"], "resid_mix")
    q2 = _mm(h2, w_xq, "nn", BF16, 1024, 1024, 1024, "xa_q")
    mem_n = _norm_fwd(mem, gains["mem"], "norm_mem")
    kv = _mm(mem_n, w_xkv, "nn", BF16, mem.shape[0], 256, 1024, "xa_kv", b_cols=256)
    o2 = _xattn_fwd(q2, kv, "xattn_fwd")
    y2 = _mm(o2, w_xo, "nn", F32, 1024, 1024, 1024, "xa_out")
    x2, h3 = _resid_norm_fwd(x1, y2, gains["xa_post"], gains["ffn_pre"], "resid_xa")
    w_up, w_down, cw = ffn_weights(h3)
    hid_g, hid_u, act, y3 = _ffn_fwd(h3, w_up, cw, conv_b, w_down, "ffn_fwd")

    loss, dx3, dy3, dg_ffn_post = _loss_bwd(x2, y3, gains["ffn_post"], tgt, "loss_bwd")
    dhid_g, dhid_u, dcw_g, dcw_u, dcb_g, dcb_u = _ffn_bwd(dy3, w_down, hid_g, hid_u, cw, conv_b, "ffn_bwd")
    d_w_down = _mm(act, dy3, "tn", BF16, 1024, 1024, 1024, "dw_down")
    d_w_up = _mm(h3, [dhid_g, dhid_u], "tn", BF16, 1024, 1024, 1024, "dw_up", out_cols=1024)
    sent = send_ffn_grads(d_w_up, d_w_down, jnp.concatenate([dcw_g, dcw_u], axis=1))
    dh3 = _mm([dhid_g, dhid_u], w_up, "nt", F32, 1024, 1024, 1024, "dh_ffn", b_cols=1024)
    dx2, dg_ffn_pre, dy2, dg_xa_post = _norm_bwd(dh3, x2, dx3, gains["ffn_pre"] + sent, "norm_bwd_ffn",
                                                 prev=(y2, gains["xa_post"]))
    do2 = _mm(dy2, w_xo, "nt", BF16, 1024, 1024, 1024, "d_xa_out")
    d_w_xo = _mm(o2, dy2, "tn", BF16, 1024, 1024, 1024, "dw_xo")
    dq2, dkv = _xattn_bwd(q2, kv, do2, "xattn_bwd")
    dkv = dkv.astype(BF16)
    dh2 = _mm(dq2, w_xq, "nt", F32, 1024, 1024, 1024, "dh_xa")
    d_w_xq = _mm(h2, dq2, "tn", BF16, 1024, 1024, 1024, "dw_xq")
    dmem_n = _mm(dkv, w_xkv, "nt", F32, mem.shape[0], 1024, 256, "d_mem", b_cols=256)
    d_w_xkv = _mm(mem_n, dkv, "tn", BF16, 1024, 256, mem.shape[0], "dw_xkv", out_cols=256)
    _, dg_mem = _norm_bwd(dmem_n, mem, jnp.zeros_like(mem), gains["mem"], "norm_bwd_mem")
    dx1, dg_xa_pre, dy1, dg_mix_post = _norm_bwd(dh2, x1, dx2, gains["xa_pre"], "norm_bwd_xa",
                                                 prev=(y1, gains["mix_post"]))
    dycat = _mm(dy1, w_mix, "nt", BF16, 1024, 1024, 1024, "d_mix_out")
    d_w_mix = _mm(ycat, dy1, "tn", BF16, 1024, 1024, 1024, "dw_mix")
    sent_mix = send_mix_grads(d_w_mix, d_w_xq, d_w_xo, d_w_xkv)
    ad = _fox_do_operand(dycat, ycat, "fox_do_operand")
    dq, dk, dv, qaux, kaux = _fox_bwd2(proj, dycat, aqb, ak, ad, "fox_bwd")
    du, d_wbd, d_scale = _pool_bwd(proj, dycat, wbd, scale, "pool_bwd")
    df, db_f = _gate_bwd(qaux, kaux, flog, "gate_bwd")
    dproj = jnp.concatenate([du, dq, dk, dv, df], axis=1)
    dh1 = _mm(dproj, w_in, "nt", F32, 1024, 1024, 896, "dh_mix")
    d_w_in = _mm(h1, dproj, "tn", BF16, 1024, 384, 1024, "dw_in")
    grad_x, dg_mix_pre = _norm_bwd(dh1, x, dx1, gains["mix_pre"] + sent_mix, "norm_bwd_mix")

    small = dict(
        mix_pre=dg_mix_pre, mix_post=dg_mix_post, mem=dg_mem, xa_pre=dg_xa_pre, xa_post=dg_xa_post,
        ffn_pre=dg_ffn_pre, ffn_post=dg_ffn_post,
        conv_b=jnp.concatenate([dcb_g, dcb_u], axis=1),
        w_pool=jnp.stack([d_wbd[64 * g:64 * g + 64, 64 * g:64 * g + 64] for g in range(4)]),
        pool_scale=d_scale.reshape(4, 64),
        b_forget=db_f[:, :FOX_HEADS],
    )
    return loss, grad_x, d_w_in, small


SMALL_ORDER = ("mix_pre", "mix_post", "mem", "xa_pre", "xa_post", "ffn_pre", "ffn_post", "conv_b",
               "w_pool", "pool_scale", "b_forget")
SMALL_ROWS = 256


def _pack_small(d):
    flat = jnp.concatenate([d[k].reshape(-1).astype(F32) for k in SMALL_ORDER])
    return jnp.pad(flat, (0, SMALL_ROWS * 128 - flat.shape[0])).reshape(SMALL_ROWS, 128)


def _unpack_small(a, like):
    flat = a.reshape(-1)
    out, off = {}, 0
    for k in SMALL_ORDER:
        n = like[k].size
        out[k] = flat[off:off + n].reshape(like[k].shape)
        off += n
    return out


def kernel(x, mem, norm_mix_pre, norm_mix_post, w_in, b_forget, w_pool, pool_scale, w_mix_out, norm_mem, norm_xa_pre, norm_xa_post, w_xq, w_xkv, w_xo, norm_ffn_pre, norm_ffn_post, w_up, conv_w, conv_b, w_down, loss_target, m_norm_mix_pre, m_norm_mix_post, m_w_in, m_b_forget, m_w_pool, m_pool_scale, m_w_mix_out, m_norm_mem, m_norm_xa_pre, m_norm_xa_post, m_w_xq, m_w_xkv, m_w_xo, m_norm_ffn_pre, m_norm_ffn_post, m_w_up, m_conv_w, m_conv_b, m_w_down, v_norm_mix_pre, v_norm_mix_post, v_w_in, v_b_forget, v_w_pool, v_pool_scale, v_w_mix_out, v_norm_mem, v_norm_xa_pre, v_norm_xa_post, v_w_xq, v_w_xkv, v_w_xo, v_norm_ffn_pre, v_norm_ffn_post, v_w_up, v_conv_w, v_conv_b, v_w_down):
    names = ("norm_mix_pre", "norm_mix_post", "w_in", "b_forget", "w_pool", "pool_scale", "w_mix_out", "norm_mem",
             "norm_xa_pre", "norm_xa_post", "w_xq", "w_xkv", "w_xo", "norm_ffn_pre", "norm_ffn_post", "w_up",
             "conv_w", "conv_b", "w_down")
    w = dict(zip(names, (norm_mix_pre, norm_mix_post, w_in, b_forget, w_pool, pool_scale, w_mix_out, norm_mem,
                         norm_xa_pre, norm_xa_post, w_xq, w_xkv, w_xo, norm_ffn_pre, norm_ffn_post, w_up,
                         conv_w, conv_b, w_down)))
    mo = dict(zip(names, (m_norm_mix_pre, m_norm_mix_post, m_w_in, m_b_forget, m_w_pool, m_pool_scale, m_w_mix_out,
                          m_norm_mem, m_norm_xa_pre, m_norm_xa_post, m_w_xq, m_w_xkv, m_w_xo, m_norm_ffn_pre,
                          m_norm_ffn_post, m_w_up, m_conv_w, m_conv_b, m_w_down)))
    vo = dict(zip(names, (v_norm_mix_pre, v_norm_mix_post, v_w_in, v_b_forget, v_w_pool, v_pool_scale, v_w_mix_out,
                          v_norm_mem, v_norm_xa_pre, v_norm_xa_post, v_w_xq, v_w_xkv, v_w_xo, v_norm_ffn_pre,
                          v_norm_ffn_post, v_w_up, v_conv_w, v_conv_b, v_w_down)))

    big_names = ("w_in", "w_mix_out", "w_xq", "w_xo", "w_xkv", "w_up", "w_down")
    shards = {k: w[k][0].astype(BF16) for k in big_names}
    shards["w_in"] = jnp.pad(shards["w_in"], ((0, 0), (0, D_IN_PAD - shards["w_in"].shape[1])))
    conv_w_sh = jnp.pad(conv_w[0, :, 0, :], ((0, 5), (0, 0)))
    (g_in,) = _all_gather([shards["w_in"]], "gather_w_in")
    mix_srcs = [shards[k] for k in ("w_mix_out", "w_xq", "w_xo", "w_xkv")]
    mix_flight = _exchange_start(mix_srcs, [_own_slot(a) for a in mix_srcs], g_in, True, "gather_mix_start")
    ffn_srcs = [shards["w_up"], shards["w_down"], conv_w_sh]
    ffn_flight = _exchange_start(ffn_srcs, [_own_slot(a) for a in ffn_srcs], mix_flight[4], True, "gather_ffn_start")
    my_slot = 4 * lax.axis_index("x") + 2 * lax.axis_index("y") + lax.axis_index("c")
    own_block = lambda a: _own_slot(lax.dynamic_index_in_dim(a, my_slot, 0, keepdims=False))
    by_rows = lambda a: a.reshape(NDEV, a.shape[0] // NDEV, a.shape[1])
    by_cols = lambda a: a.reshape(a.shape[0], NDEV, a.shape[1] // NDEV).transpose(1, 0, 2)
    grad_flight = {}

    def mix_weights(after):
        g_mix, g_xq, g_xo, g_xkv = _exchange_wait(*mix_flight[:4], after, True, "gather_mix_wait")
        return (g_mix.reshape(D_MODEL, D_MODEL), g_xq.reshape(D_MODEL, D_MODEL), g_xo.reshape(D_MODEL, D_MODEL), g_xkv)

    def ffn_weights(after):
        g_up, g_down, g_cw = _exchange_wait(*ffn_flight[:4], after, True, "gather_ffn_wait")
        return g_up, g_down.reshape(D_FF, D_MODEL), g_cw.transpose(1, 0, 2).reshape(8, 2 * D_FF)

    def send_ffn_grads(d_w_up, d_w_down, d_cw):
        srcs = [d_w_up, by_rows(d_w_down), by_cols(d_cw)]
        grad_flight["ffn"] = _exchange_start(srcs, [own_block(a) for a in srcs], d_w_up, False, "scatter_ffn_start")
        return grad_flight["ffn"][4][0, 0]

    def send_mix_grads(d_w_mix, d_w_xq, d_w_xo, d_w_xkv):
        srcs = [by_rows(d_w_mix), by_rows(d_w_xq), by_rows(d_w_xo), d_w_xkv]
        grad_flight["mix"] = _exchange_start(srcs, [own_block(a) for a in srcs], d_w_mix, False, "scatter_mix_start")
        return grad_flight["mix"][4][0, 0]

    gains = dict(mix_pre=norm_mix_pre + ffn_flight[4][0, 0], mix_post=norm_mix_post, mem=norm_mem, xa_pre=norm_xa_pre,
                 xa_post=norm_xa_post, ffn_pre=norm_ffn_pre, ffn_post=norm_ffn_post)
    loss, grad_x, d_w_in, small = _local_step(
        x[0], mem[0], loss_target[0], gains, b_forget, w_pool[0], pool_scale[0], conv_b,
        g_in.reshape(D_MODEL, D_IN_PAD), mix_weights, ffn_weights, send_mix_grads, send_ffn_grads)

    (p_in,) = _scatter_blocks([by_rows(d_w_in)], "scatter_w_in")
    p_up, p_down, p_cw = _exchange_wait(*grad_flight["ffn"][:4], p_in, False, "scatter_ffn_wait")
    p_mix, p_xq, p_xo, p_xkv = _exchange_wait(*grad_flight["mix"][:4], p_in, False, "scatter_mix_wait")
    parts = [p_in, p_mix, p_xq, p_xo, p_xkv, p_up, p_down, p_cw]
    small_like = dict(mix_pre=norm_mix_pre, mix_post=norm_mix_post, mem=norm_mem, xa_pre=norm_xa_pre,
                      xa_post=norm_xa_post, ffn_pre=norm_ffn_pre, ffn_post=norm_ffn_post, conv_b=conv_b,
                      w_pool=w_pool, pool_scale=pool_scale, b_forget=b_forget)
    small = {k: small[k].reshape(small_like[k].shape) for k in SMALL_ORDER}
    (small_parts,) = _all_gather([_pack_small(small)], "gather_small_grads")

    res = {}
    for k, p in zip(big_names, parts[:7]):
        if k == "w_in":
            p = p[:, :, :w_in.shape[2]]
        res[k] = [a[None] for a in _adamw(p, w[k][0], mo[k][0], vo[k][0], "adamw_" + k)]
    pad_cw = lambda a: jnp.pad(a[0, :, 0, :], ((0, 5), (0, 0)))
    res["conv_w"] = [a[:3][None, :, None, :] for a in
                     _adamw(parts[7], pad_cw(conv_w), pad_cw(m_conv_w), pad_cw(v_conv_w), "adamw_conv_w")]
    key_of = dict(mix_pre="norm_mix_pre", mix_post="norm_mix_post", mem="norm_mem", xa_pre="norm_xa_pre",
                  xa_post="norm_xa_post", ffn_pre="norm_ffn_pre", ffn_post="norm_ffn_post", conv_b="conv_b",
                  w_pool="w_pool", pool_scale="pool_scale", b_forget="b_forget")
    pack_of = lambda src: _pack_small({k: src[key_of[k]] for k in SMALL_ORDER})
    small_out = _adamw(small_parts, pack_of(w), pack_of(mo), pack_of(vo), "adamw_small")
    small_out = [_unpack_small(a, small_like) for a in small_out]
    for k in SMALL_ORDER:
        res[key_of[k]] = [so[k] for so in small_out]

    total = lax.psum(loss[0, 0], ("x", "y", "c"))
    outs = [total, grad_x[None]]
    for idx in range(4):
        outs += [res[k][idx] for k in names]
    return tuple(outs)
```

```python
import functools
import math

import jax
import jax.numpy as jnp
from jax import lax
from jax.experimental import pallas as pl
from jax.experimental.pallas import tpu as pltpu

F32 = jnp.float32
BF16 = jnp.bfloat16

NDEV = 8
D_MODEL = 1024
D_POOL = 256
D_FOX = 768
FOX_HEADS = 12
HEAD_PAIRS = FOX_HEADS // 2
XA_HEADS = 4
XA_DIM = 256
D_FF = 4096
D_IN_PAD = 2688
F_COL = 2560
POOL_HALO = 16
NORM_EPS = 1e-6
NEG = -1e30

ADAM_LR = 0.001
ADAM_B1 = 0.9
ADAM_B2 = 0.999
ADAM_EPS = 1e-08
ADAM_WD = 0.01
ADAM_STEP = 10

TM = 512
TQ = 512
TN_FF = 512
VMEM_LIMIT = 48 * 1024 * 1024
MESH = pl.DeviceIdType.MESH


def _cp(*sem):
    return pltpu.CompilerParams(dimension_semantics=sem, vmem_limit_bytes=VMEM_LIMIT)


def _dot(a, b, dims):
    return lax.dot_general(a, b, (dims, ((), ())), preferred_element_type=F32)


NN = ((1,), (0,))
NT = ((1,), (1,))
TN = ((0,), (0,))


def _mm(a, b, mode, out_dtype, tm, tn, tk, name, b_cols=None, out_cols=None, after=None):
    a_list = list(a) if isinstance(a, (list, tuple)) else [a]
    b_list = list(b) if isinstance(b, (list, tuple)) else [b]
    assert len(a_list) == 1 or len(b_list) == 1
    if mode == "tn":
        K, M = a_list[0].shape
        assert len(a_list) == 1
        Ns = [x.shape[1] for x in b_list]
        N = sum(Ns)
        assert b_cols is None
    else:
        assert len(b_list) == 1
        M = a_list[0].shape[0]
        Ks = [x.shape[1] for x in a_list]
        K = sum(Ks)
        if b_cols is None:
            N = b_list[0].shape[0] if mode == "nt" else b_list[0].shape[1]
        else:
            N = b_list[0].shape[1] if mode == "nt" else NDEV * b_cols
    assert M % tm == 0 and N % tn == 0 and K % tk == 0, (name, M, N, K)
    grid = (M // tm, N // tn, K // tk)
    nk = grid[2]
    dims = {"nn": NN, "nt": NT, "tn": TN}[mode]

    in_specs = []
    if mode == "tn":
        in_specs.append(pl.BlockSpec((tk, tm), lambda i, j, k: (k, i)))
        if len(b_list) == 1:
            in_specs.append(pl.BlockSpec((tk, tn), lambda i, j, k: (k, j)))
        else:
            nj1 = Ns[0] // tn
            in_specs.append(pl.BlockSpec((tk, tn), lambda i, j, k: (k, jnp.minimum(j, nj1 - 1))))
            in_specs.append(pl.BlockSpec((tk, tn), lambda i, j, k: (k, jnp.maximum(j - nj1, 0))))
    else:
        if len(a_list) == 1:
            in_specs.append(pl.BlockSpec((tm, tk), lambda i, j, k: (i, k)))
        else:
            nk1 = Ks[0] // tk
            in_specs.append(pl.BlockSpec((tm, tk), lambda i, j, k: (i, jnp.minimum(k, nk1 - 1))))
            in_specs.append(pl.BlockSpec((tm, tk), lambda i, j, k: (i, jnp.maximum(k - nk1, 0))))
        if b_cols is None:
            if mode == "nn":
                in_specs.append(pl.BlockSpec((tk, tn), lambda i, j, k: (k, j)))
            else:
                in_specs.append(pl.BlockSpec((tn, tk), lambda i, j, k: (j, k)))
        else:
            if mode == "nn":
                per = b_cols // tn
                in_specs.append(pl.BlockSpec((None, tk, tn), lambda i, j, k: (j // per, k, j % per)))
            else:
                per = b_cols // tk
                in_specs.append(pl.BlockSpec((None, tn, tk), lambda i, j, k: (k // per, j, k % per)))
    if out_cols is None:
        out_spec = pl.BlockSpec((tm, tn), lambda i, j, k: (i, j))
        out_shape = jax.ShapeDtypeStruct((M, N), out_dtype)
    else:
        pero = out_cols // tn
        out_spec = pl.BlockSpec((None, tm, tn), lambda i, j, k: (j // pero, i, j % pero))
        out_shape = jax.ShapeDtypeStruct((NDEV, M, out_cols), out_dtype)

    two_a = len(a_list) == 2
    two_b = len(b_list) == 2
    extra = []
    if after is not None:
        in_specs.append(pl.BlockSpec(memory_space=pl.ANY))
        extra.append(after)

    def body(*refs):
        o_ref, acc_ref = refs[-2], refs[-1]
        j = pl.program_id(1)
        k = pl.program_id(2)

        @pl.when(k == 0)
        def _():
            acc_ref[...] = jnp.zeros_like(acc_ref)

        if two_a:
            a1, a2, b1 = refs[0], refs[1], refs[2]
            nk1_ = Ks[0] // tk

            @pl.when(k < nk1_)
            def _():
                acc_ref[...] += _dot(a1[...], b1[...], dims)

            @pl.when(k >= nk1_)
            def _():
                acc_ref[...] += _dot(a2[...], b1[...], dims)
        elif two_b:
            a1, b1, b2 = refs[0], refs[1], refs[2]
            nj1_ = Ns[0] // tn

            @pl.when(j < nj1_)
            def _():
                acc_ref[...] += _dot(a1[...], b1[...], dims)

            @pl.when(j >= nj1_)
            def _():
                acc_ref[...] += _dot(a1[...], b2[...], dims)
        else:
            acc_ref[...] += _dot(refs[0][...], refs[1][...], dims)

        @pl.when(k == nk - 1)
        def _():
            o_ref[...] = acc_ref[...].astype(o_ref.dtype)

    return pl.pallas_call(
        body, name=name, grid=grid, in_specs=in_specs, out_specs=out_spec, out_shape=out_shape,
        scratch_shapes=[pltpu.VMEM((tm, tn), F32)],
        compiler_params=_cp("parallel", "parallel", "arbitrary"),
    )(*a_list, *b_list, *extra)


def _rstd(x):
    return lax.rsqrt(jnp.mean(x * x, axis=-1, keepdims=True) + NORM_EPS)


def _norm_bwd_rows(dxn, xn, r):
    return r * (dxn - xn * jnp.mean(dxn * xn, axis=-1, keepdims=True))


def _row_spec(tm, d):
    return pl.BlockSpec((tm, d), lambda i: (i, 0))


def _vec_spec(d):
    return pl.BlockSpec((1, d), lambda i: (0, 0))


def _norm_fwd(x, g, name):
    s, d = x.shape
    tm = min(TM, s)

    def body(x_ref, g_ref, h_ref):
        xv = x_ref[...]
        h_ref[...] = (xv * _rstd(xv) * g_ref[...]).astype(BF16)

    return pl.pallas_call(
        body, name=name, grid=(s // tm,), in_specs=[_row_spec(tm, d), _vec_spec(d)],
        out_specs=_row_spec(tm, d), out_shape=jax.ShapeDtypeStruct((s, d), BF16),
        compiler_params=_cp("parallel"),
    )(x, g)


def _resid_norm_fwd(x_in, y, g_post, g_next, name):
    s, d = x_in.shape

    def body(x_ref, y_ref, gp_ref, gn_ref, xo_ref, h_ref):
        yv = y_ref[...]
        xo = x_ref[...] + yv * _rstd(yv) * gp_ref[...]
        xo_ref[...] = xo
        h_ref[...] = (xo * _rstd(xo) * gn_ref[...]).astype(BF16)

    return pl.pallas_call(
        body, name=name, grid=(s // TM,),
        in_specs=[_row_spec(TM, d), _row_spec(TM, d), _vec_spec(d), _vec_spec(d)],
        out_specs=[_row_spec(TM, d), _row_spec(TM, d)],
        out_shape=[jax.ShapeDtypeStruct((s, d), F32), jax.ShapeDtypeStruct((s, d), BF16)],
        compiler_params=_cp("parallel"),
    )(x_in, y, g_post, g_next)


def _norm_bwd(dh, x, dx_res, g_pre, name, prev=None):
    s, d = x.shape
    tm = min(TM, s)
    has_prev = prev is not None

    def body(*refs):
        if has_prev:
            dh_ref, x_ref, dr_ref, g_ref, y_ref, gp_ref, dx_ref, dg_ref, dy_ref, dgp_ref = refs
        else:
            dh_ref, x_ref, dr_ref, g_ref, dx_ref, dg_ref = refs
        i = pl.program_id(0)
        xv = x_ref[...]
        r = _rstd(xv)
        xn = xv * r
        dhv = dh_ref[...].astype(F32)
        dx = dr_ref[...] + _norm_bwd_rows(dhv * g_ref[...], xn, r)
        dx_ref[...] = dx
        dg = jnp.sum(dhv * xn, axis=0, keepdims=True)

        @pl.when(i == 0)
        def _():
            dg_ref[...] = dg

        @pl.when(i > 0)
        def _():
            dg_ref[...] += dg

        if has_prev:
            yv = y_ref[...]
            r2 = _rstd(yv)
            yn = yv * r2
            dy_ref[...] = _norm_bwd_rows(dx * gp_ref[...], yn, r2).astype(BF16)
            dgp = jnp.sum(dx * yn, axis=0, keepdims=True)

            @pl.when(i == 0)
            def _():
                dgp_ref[...] = dgp

            @pl.when(i > 0)
            def _():
                dgp_ref[...] += dgp

    in_specs = [_row_spec(tm, d), _row_spec(tm, d), _row_spec(tm, d), _vec_spec(d)]
    out_specs = [_row_spec(tm, d), _vec_spec(d)]
    out_shape = [jax.ShapeDtypeStruct((s, d), F32), jax.ShapeDtypeStruct((1, d), F32)]
    args = [dh, x, dx_res, g_pre]
    if has_prev:
        in_specs += [_row_spec(tm, d), _vec_spec(d)]
        out_specs += [_row_spec(tm, d), _vec_spec(d)]
        out_shape += [jax.ShapeDtypeStruct((s, d), BF16), jax.ShapeDtypeStruct((1, d), F32)]
        args += list(prev)
    return pl.pallas_call(
        body, name=name, grid=(s // tm,), in_specs=in_specs, out_specs=out_specs, out_shape=out_shape,
        compiler_params=_cp("arbitrary"),
    )(*args)


def _loss_bwd(x2, y3, g_post, tgt, name):
    s, d = x2.shape

    def body(x_ref, y_ref, g_ref, t_ref, loss_ref, dx_ref, dy_ref, dg_ref):
        i = pl.program_id(0)
        yv = y_ref[...]
        r = _rstd(yv)
        yn = yv * r
        e = x_ref[...] + yn * g_ref[...] - t_ref[...]
        part = 0.5 * jnp.sum(jnp.mean(e * e, axis=-1, keepdims=True), axis=0, keepdims=True)
        dx = e * (1.0 / d)
        dx_ref[...] = dx
        dy_ref[...] = _norm_bwd_rows(dx * g_ref[...], yn, r).astype(BF16)
        dg = jnp.sum(dx * yn, axis=0, keepdims=True)
        part = jnp.broadcast_to(part, (1, 128))

        @pl.when(i == 0)
        def _():
            dg_ref[...] = dg
            loss_ref[...] = part

        @pl.when(i > 0)
        def _():
            dg_ref[...] += dg
            loss_ref[...] += part

    return pl.pallas_call(
        body, name=name, grid=(s // TM,),
        in_specs=[_row_spec(TM, d), _row_spec(TM, d), _vec_spec(d), _row_spec(TM, d)],
        out_specs=[_vec_spec(128), _row_spec(TM, d), _row_spec(TM, d), _vec_spec(d)],
        out_shape=[jax.ShapeDtypeStruct((1, 128), F32), jax.ShapeDtypeStruct((s, d), F32),
                   jax.ShapeDtypeStruct((s, d), BF16), jax.ShapeDtypeStruct((1, d), F32)],
        compiler_params=_cp("arbitrary"),
    )(x2, y3, g_post, tgt)


def _split3(v):
    hi = v.astype(BF16)
    r1 = v - hi.astype(F32)
    mid = r1.astype(BF16)
    lo = (r1 - mid.astype(F32)).astype(BF16)
    return hi, mid, lo


def _tri_dot(tri, v):
    hi, mid, lo = _split3(v)
    return _dot(tri, hi, NN) + _dot(tri, mid, NN) + _dot(tri, lo, NN)


def _gate_cumsum(fraw, b_pad, name):
    s = fraw.shape[0]

    def body(f_ref, b_ref, flog_ref, cum_ref, carry_ref):
        i = pl.program_id(0)

        @pl.when(i == 0)
        def _():
            carry_ref[...] = jnp.zeros_like(carry_ref)

        flog = f_ref[...] + b_ref[...]
        flog_ref[...] = flog
        lf = jnp.minimum(flog, 0.0) - jnp.log(1.0 + jnp.exp(-jnp.abs(flog)))
        lane = lax.broadcasted_iota(jnp.int32, (1, 128), 1)
        lf = jnp.where(lane < FOX_HEADS, lf, 0.0)
        row = lax.broadcasted_iota(jnp.int32, (TM, TM), 0)
        col = lax.broadcasted_iota(jnp.int32, (TM, TM), 1)
        tri = (row >= col).astype(BF16)
        cum = _tri_dot(tri, lf) + carry_ref[...]
        cum_ref[...] = cum
        carry_ref[...] = cum[TM - 1:TM, :]

    return pl.pallas_call(
        body, name=name, grid=(s // TM,),
        in_specs=[_row_spec(TM, 128), _vec_spec(128)],
        out_specs=[_row_spec(TM, 128), _row_spec(TM, 128)],
        out_shape=[jax.ShapeDtypeStruct((s, 128), F32), jax.ShapeDtypeStruct((s, 128), F32)],
        scratch_shapes=[pltpu.VMEM((1, 128), F32)],
        compiler_params=_cp("arbitrary"),
    )(fraw, b_pad)


def _gate_bwd(qaux, kaux, flog, name):
    s = flog.shape[0]
    n = s // TM

    def body(qa_ref, ka_ref, fl_ref, dp_ref, db_ref, carry_ref):
        i = pl.program_id(0)

        @pl.when(i == 0)
        def _():
            carry_ref[...] = jnp.zeros_like(carry_ref)

        src = lax.broadcasted_iota(jnp.int32, (128, 128), 0)
        dst = lax.broadcasted_iota(jnp.int32, (128, 128), 1)
        dcum = jnp.zeros((TM, 128), F32)
        for p in range(HEAD_PAIRS):
            for ref, l0, l1, sign in ((qa_ref, 64, 0, 1.0), (ka_ref, 67, 3, -1.0)):
                hit = jnp.logical_or(jnp.logical_and(src == l0, dst == 2 * p),
                                     jnp.logical_and(src == l1, dst == 2 * p + 1))
                sel = jnp.where(hit, sign, 0.0).astype(BF16)
                for piece in _split3(ref[p]):
                    dcum = dcum + _dot(piece, sel, NN)
        row = lax.broadcasted_iota(jnp.int32, (TM, TM), 0)
        col = lax.broadcasted_iota(jnp.int32, (TM, TM), 1)
        tri = (row <= col).astype(BF16)
        dlf = _tri_dot(tri, dcum) + carry_ref[...]
        carry_ref[...] = dlf[0:1, :]
        lane = lax.broadcasted_iota(jnp.int32, (1, 128), 1)
        df = jnp.where(lane < FOX_HEADS, dlf / (1.0 + jnp.exp(fl_ref[...])), 0.0)
        dp_ref[...] = df.astype(BF16)
        db = jnp.sum(df, axis=0, keepdims=True)

        @pl.when(i == 0)
        def _():
            db_ref[...] = db

        @pl.when(i > 0)
        def _():
            db_ref[...] += db

    rev = lambda i: (n - 1 - i, 0)
    return pl.pallas_call(
        body, name=name, grid=(n,),
        in_specs=[pl.BlockSpec((HEAD_PAIRS, TM, 128), lambda i: (0, n - 1 - i, 0)),
                  pl.BlockSpec((HEAD_PAIRS, TM, 128), lambda i: (0, n - 1 - i, 0)), pl.BlockSpec((TM, 128), rev)],
        out_specs=[pl.BlockSpec((TM, 128), rev), _vec_spec(128)],
        out_shape=[jax.ShapeDtypeStruct((s, 128), BF16), jax.ShapeDtypeStruct((1, 128), F32)],
        scratch_shapes=[pltpu.VMEM((1, 128), F32)],
        compiler_params=_cp("arbitrary"),
    )(qaux, kaux, flog)


def _pool_consts(i, rows):
    lane = lax.broadcasted_iota(jnp.int32, (rows, D_POOL), 1)
    t1 = lax.broadcasted_iota(jnp.int32, (rows, D_POOL), 0) + i * TM + 1
    win = jnp.where(lane < 64, 2, jnp.where(lane < 128, 4, jnp.where(lane < 192, 8, 16)))
    inv = 1.0 / jnp.minimum(t1, win).astype(F32)
    return lane, inv


def _by_group(lane, s2, s4, s8, s16):
    return jnp.where(lane < 64, s2, jnp.where(lane < 128, s4, jnp.where(lane < 192, s8, s16)))


def _pool_diff(i, u_ref, halo_ref):
    u = u_ref[...].astype(F32)
    halo = jnp.where(i > 0, halo_ref[...].astype(F32), 0.0)
    ext = jnp.concatenate([halo, u], axis=0)
    s2 = ext + pltpu.roll(ext, 1, 0)
    s4 = s2 + pltpu.roll(s2, 2, 0)
    s8 = s4 + pltpu.roll(s4, 4, 0)
    s16 = s8 + pltpu.roll(s8, 8, 0)
    lane, inv = _pool_consts(i, TM)
    sel = _by_group(lane, s2[POOL_HALO:], s4[POOL_HALO:], s8[POOL_HALO:], s16[POOL_HALO:])
    return sel * inv - u


def _pool_fwd(proj, wbd, scale, ycat, name):
    s = proj.shape[0]
    hb = TM // POOL_HALO

    def body(u_ref, halo_ref, w_ref, sc_ref, y_any, y_ref):
        del y_any
        i = pl.program_id(0)
        diff = _pool_diff(i, u_ref, halo_ref)
        mixed = _dot(diff.astype(BF16), w_ref[...], NN)
        y_ref[...] = (mixed * sc_ref[...]).astype(BF16)

    return pl.pallas_call(
        body, name=name, grid=(s // TM,),
        in_specs=[pl.BlockSpec((TM, D_POOL), lambda i: (i, 0)),
                  pl.BlockSpec((POOL_HALO, D_POOL), lambda i: (jnp.maximum(i * hb - 1, 0), 0)),
                  pl.BlockSpec((D_POOL, D_POOL), lambda i: (0, 0)), _vec_spec(D_POOL),
                  pl.BlockSpec(memory_space=pl.ANY)],
        out_specs=pl.BlockSpec((TM, D_POOL), lambda i: (i, 0)),
        out_shape=jax.ShapeDtypeStruct(ycat.shape, ycat.dtype),
        input_output_aliases={4: 0},
        compiler_params=_cp("parallel"),
    )(proj, proj, wbd, scale, ycat)


def _pool_bwd(proj, dycat, wbd, scale, name):
    s = proj.shape[0]
    n = s // TM
    hb = TM // POOL_HALO
    last_halo = s // POOL_HALO - 1

    def body(u_ref, halo_ref, dy_ref, dyp_ref, w_ref, sc_ref, dp_ref, dw_ref, dsc_ref):
        i = pl.program_id(0)
        diff = _pool_diff(i, u_ref, halo_ref)
        diff_b = diff.astype(BF16)
        mixed = _dot(diff_b, w_ref[...], NN)
        dy = dy_ref[...].astype(F32)
        dmix = (dy * sc_ref[...]).astype(BF16)
        dyp = jnp.where(i < n - 1, dyp_ref[...].astype(F32), 0.0)
        dmix_p = (dyp * sc_ref[...]).astype(BF16)
        dd = _dot(dmix, w_ref[...], NT)
        dd_p = _dot(dmix_p, w_ref[...], NT)
        lane, inv = _pool_consts(i, TM)
        _, inv_p = _pool_consts(i + 1, POOL_HALO)
        ext = jnp.concatenate([dd * inv, dd_p * inv_p], axis=0)
        rows = TM + POOL_HALO
        l2 = ext + pltpu.roll(ext, rows - 1, 0)
        l4 = l2 + pltpu.roll(l2, rows - 2, 0)
        l8 = l4 + pltpu.roll(l4, rows - 4, 0)
        l16 = l8 + pltpu.roll(l8, rows - 8, 0)
        du = _by_group(lane, l2[:TM], l4[:TM], l8[:TM], l16[:TM]) - dd
        dp_ref[...] = du.astype(BF16)
        dw = _dot(diff_b, dmix, TN)
        dsc = jnp.sum(dy * mixed, axis=0, keepdims=True)

        @pl.when(i == 0)
        def _():
            dw_ref[...] = dw
            dsc_ref[...] = dsc

        @pl.when(i > 0)
        def _():
            dw_ref[...] += dw
            dsc_ref[...] += dsc

    return pl.pallas_call(
        body, name=name, grid=(n,),
        in_specs=[pl.BlockSpec((TM, D_POOL), lambda i: (i, 0)),
                  pl.BlockSpec((POOL_HALO, D_POOL), lambda i: (jnp.maximum(i * hb - 1, 0), 0)),
                  pl.BlockSpec((TM, D_POOL), lambda i: (i, 0)),
                  pl.BlockSpec((POOL_HALO, D_POOL), lambda i: (jnp.minimum((i + 1) * hb, last_halo), 0)),
                  pl.BlockSpec((D_POOL, D_POOL), lambda i: (0, 0)), _vec_spec(D_POOL)],
        out_specs=[pl.BlockSpec((TM, D_POOL), lambda i: (i, 0)),
                   pl.BlockSpec((D_POOL, D_POOL), lambda i: (0, 0)), _vec_spec(D_POOL)],
        out_shape=[jax.ShapeDtypeStruct((s, D_POOL), BF16),
                   jax.ShapeDtypeStruct((D_POOL, D_POOL), F32), jax.ShapeDtypeStruct((1, D_POOL), F32)],
        compiler_params=_cp("arbitrary"),
    )(proj, proj, dycat, dycat, wbd, scale)


Q_BLK = D_POOL // 128
K_BLK = Q_BLK + D_FOX // 128
V_BLK = K_BLK + D_FOX // 128


def _head_masks():
    lane = lax.broadcasted_iota(jnp.int32, (1, 128), 1)
    return [lane < 64, lane >= 64]


def _fox_scores(qh, k2, cq_col, ck_row, row_off):
    sc = _dot(qh, k2, NT) * 0.125 + cq_col - ck_row
    row = lax.broadcasted_iota(jnp.int32, sc.shape, 0) + row_off
    col = lax.broadcasted_iota(jnp.int32, sc.shape, 1)
    return jnp.where(row >= col, sc, NEG)


def _fox_fwd(proj, cum_c, cum_r, name):
    s = proj.shape[0]
    nq = s // TQ

    def body(q_ref, k_ref, v_ref, cq_ref, ck_ref, o_ref, lse_ref, m_ref, l_ref, acc_ref):
        qi, ki = pl.program_id(1), pl.program_id(2)
        masks = _head_masks()

        @pl.when(ki == 0)
        def _():
            m_ref[...] = jnp.full_like(m_ref, NEG)
            l_ref[...] = jnp.zeros_like(l_ref)
            acc_ref[...] = jnp.zeros_like(acc_ref)

        @pl.when(ki <= qi)
        def _():
            q2, k2, v2 = q_ref[...], k_ref[...], v_ref[...]
            cq, ck = cq_ref[...], ck_ref[...]
            pv = []
            alpha = []
            for hh in range(2):
                qh = jnp.where(masks[hh], q2, jnp.zeros_like(q2))
                vh = jnp.where(masks[hh], v2, jnp.zeros_like(v2))
                sc = _fox_scores(qh, k2, cq[:, hh:hh + 1], ck[hh:hh + 1, :], (qi - ki) * TQ)
                m_prev = m_ref[hh]
                m_new = jnp.maximum(m_prev, jnp.max(sc, axis=1, keepdims=True))
                a = jnp.exp(m_prev - m_new)
                p = jnp.exp(sc - m_new)
                l_ref[hh] = a * l_ref[hh] + jnp.sum(p, axis=1, keepdims=True)
                m_ref[hh] = m_new
                pv.append(_dot(p.astype(BF16), vh, NN))
                alpha.append(a)
            acc_ref[...] = acc_ref[...] * jnp.where(masks[0], alpha[0], alpha[1]) + pv[0] + pv[1]

        @pl.when(ki == qi)
        def _():
            inv = jnp.where(masks[0], 1.0 / l_ref[0], 1.0 / l_ref[1])
            o_ref[...] = (acc_ref[...] * inv).astype(BF16)
            lane = lax.broadcasted_iota(jnp.int32, (1, 128), 1)
            lse_ref[...] = jnp.where(lane == 0, m_ref[0] + jnp.log(l_ref[0]), m_ref[1] + jnp.log(l_ref[1]))

    kv_row = lambda p, qi, ki: jnp.minimum(ki, qi)
    return pl.pallas_call(
        body, name=name, grid=(HEAD_PAIRS, nq, nq),
        in_specs=[pl.BlockSpec((TQ, 128), lambda p, qi, ki: (qi, Q_BLK + p)),
                  pl.BlockSpec((TQ, 128), lambda p, qi, ki: (kv_row(p, qi, ki), K_BLK + p)),
                  pl.BlockSpec((TQ, 128), lambda p, qi, ki: (kv_row(p, qi, ki), V_BLK + p)),
                  pl.BlockSpec((None, TQ, 128), lambda p, qi, ki: (p, qi, 0)),
                  pl.BlockSpec((None, 8, TQ), lambda p, qi, ki: (p, 0, kv_row(p, qi, ki)))],
        out_specs=[pl.BlockSpec((TQ, 128), lambda p, qi, ki: (qi, Q_BLK + p)),
                   pl.BlockSpec((None, TQ, 128), lambda p, qi, ki: (p, qi, 0))],
        out_shape=[jax.ShapeDtypeStruct((s, D_MODEL), BF16), jax.ShapeDtypeStruct((HEAD_PAIRS, s, 128), F32)],
        scratch_shapes=[pltpu.VMEM((2, TQ, 1), F32), pltpu.VMEM((2, TQ, 1), F32), pltpu.VMEM((TQ, 128), F32)],
        compiler_params=_cp("parallel", "parallel", "arbitrary"),
    )(proj, proj, proj, cum_c, cum_r)


def _fox_bwd(proj, ycat, dycat, cum_c, cum_r, lse, name):
    s = proj.shape[0]
    nq = s // TQ

    def body(q_ref, k_ref, v_ref, do_ref, o_ref, cq_ref, ck_ref, lse_ref,
             dq_ref, dk_ref, dv_ref, dc_ref, dcq_ref, dq_acc, dk_acc, dv_acc, dc_acc, dcq_acc):
        ki, qi = pl.program_id(1), pl.program_id(2)
        masks = _head_masks()
        lane = lax.broadcasted_iota(jnp.int32, (1, 128), 1)

        @pl.when(qi == ki)
        def _():
            dk_acc[...] = jnp.zeros_like(dk_acc)
            dv_acc[...] = jnp.zeros_like(dv_acc)
            dc_acc[...] = jnp.zeros_like(dc_acc)

        @pl.when(qi >= ki)
        def _():
            q2, k2, v2, do2 = q_ref[...], k_ref[...], v_ref[...], do_ref[...]
            cq, ck, lse2 = cq_ref[...], ck_ref[...], lse_ref[...]
            dd = do2.astype(F32) * o_ref[...].astype(F32)
            d0 = jnp.sum(jnp.where(masks[0], dd, 0.0), axis=1, keepdims=True)
            drow = [d0, jnp.sum(dd, axis=1, keepdims=True) - d0]
            dq = jnp.zeros((TQ, 128), F32)
            dk = jnp.zeros((TQ, 128), F32)
            dv = jnp.zeros((TQ, 128), F32)
            dcs = []
            rs = []
            for hh in range(2):
                zero = jnp.zeros_like(q2)
                qh = jnp.where(masks[hh], q2, zero)
                kh = jnp.where(masks[hh], k2, zero)
                doh = jnp.where(masks[hh], do2, zero)
                sc = _fox_scores(qh, k2, cq[:, hh:hh + 1], ck[hh:hh + 1, :], (qi - ki) * TQ)
                p = jnp.exp(sc - lse2[:, hh:hh + 1])
                dp = _dot(doh, v2, NT)
                ds = p * (dp - drow[hh])
                dsb = (ds * 0.125).astype(BF16)
                dv = dv + _dot(p.astype(BF16), doh, TN)
                dk = dk + _dot(dsb, qh, TN)
                dq = dq + _dot(dsb, kh, NN)
                dcs.append(-jnp.sum(ds, axis=0, keepdims=True))
                rs.append(jnp.sum(ds, axis=1, keepdims=True))
            dk_acc[...] += dk
            dv_acc[...] += dv
            dc_acc[0:1, :] += dcs[0]
            dc_acc[1:2, :] += dcs[1]
            dcq = jnp.where(lane == 0, rs[0], rs[1])

            @pl.when(ki == 0)
            def _():
                dq_acc[qi] = dq
                dcq_acc[qi] = dcq

            @pl.when(ki > 0)
            def _():
                dq_acc[qi] += dq
                dcq_acc[qi] += dcq

            @pl.when(qi == ki)
            def _():
                rows = pl.ds(pl.multiple_of(qi * TQ, TQ), TQ)
                dq_ref[rows, :] = dq_acc[qi].astype(BF16)
                dcq_ref[rows, :] = dcq_acc[qi]

        @pl.when(qi == nq - 1)
        def _():
            dk_ref[...] = dk_acc[...].astype(BF16)
            dv_ref[...] = dv_acc[...].astype(BF16)
            dc_ref[...] = dc_acc[...]

    q_row = lambda p, ki, qi: jnp.maximum(qi, ki)
    return pl.pallas_call(
        body, name=name, grid=(HEAD_PAIRS, nq, nq),
        in_specs=[pl.BlockSpec((TQ, 128), lambda p, ki, qi: (q_row(p, ki, qi), Q_BLK + p)),
                  pl.BlockSpec((TQ, 128), lambda p, ki, qi: (ki, K_BLK + p)),
                  pl.BlockSpec((TQ, 128), lambda p, ki, qi: (ki, V_BLK + p)),
                  pl.BlockSpec((TQ, 128), lambda p, ki, qi: (q_row(p, ki, qi), Q_BLK + p)),
                  pl.BlockSpec((TQ, 128), lambda p, ki, qi: (q_row(p, ki, qi), Q_BLK + p)),
                  pl.BlockSpec((None, TQ, 128), lambda p, ki, qi: (p, q_row(p, ki, qi), 0)),
                  pl.BlockSpec((None, 8, TQ), lambda p, ki, qi: (p, 0, ki)),
                  pl.BlockSpec((None, TQ, 128), lambda p, ki, qi: (p, q_row(p, ki, qi), 0))],
        out_specs=[pl.BlockSpec((s, 128), lambda p, ki, qi: (0, p)),
                   pl.BlockSpec((TQ, 128), lambda p, ki, qi: (ki, p)),
                   pl.BlockSpec((TQ, 128), lambda p, ki, qi: (ki, p)),
                   pl.BlockSpec((None, 8, TQ), lambda p, ki, qi: (p, 0, ki)),
                   pl.BlockSpec((None, s, 128), lambda p, ki, qi: (p, 0, 0))],
        out_shape=[jax.ShapeDtypeStruct((s, D_FOX), BF16)] * 3 + [jax.ShapeDtypeStruct((HEAD_PAIRS, 8, s), F32),
                                                                 jax.ShapeDtypeStruct((HEAD_PAIRS, s, 128), F32)],
        scratch_shapes=[pltpu.VMEM((nq, TQ, 128), F32), pltpu.VMEM((TQ, 128), F32), pltpu.VMEM((TQ, 128), F32),
                        pltpu.VMEM((8, TQ), F32), pltpu.VMEM((nq, TQ, 128), F32)],
        compiler_params=_cp("arbitrary", "arbitrary", "arbitrary"),
    )(proj, proj, proj, dycat, ycat, cum_c, cum_r, lse)


def _operand_lanes(v0, v1, ones_off):
    lane = lax.broadcasted_iota(jnp.int32, (1, 128), 1)
    out = jnp.zeros((v0.shape[0], 128), F32)
    if ones_off is not None:
        half = lane & 63
        out = out + jnp.where(jnp.logical_and(half >= ones_off, half < ones_off + 3), 1.0, 0.0)
    for base, v in ((64, v0), (0, v1)):
        for j, piece in enumerate(_split3(v)):
            out = jnp.where(lane == base + j, piece.astype(F32), out)
    return out.astype(BF16)


def _fox_operands(cum, name):
    s = cum.shape[0]
    width = HEAD_PAIRS * 128

    def body(c_ref, aq_ref, ak_ref):
        pieces = _split3(c_ref[...])
        row = lax.broadcasted_iota(jnp.int32, (128, width), 0)
        col = lax.broadcasted_iota(jnp.int32, (128, width), 1)
        base = (row >> 1) * 128 + (1 - (row & 1)) * 64
        half = lax.broadcasted_iota(jnp.int32, (1, width), 1) & 63
        for o_ref, off, sign, ones_off in ((aq_ref, 0, 1.0, 3), (ak_ref, 3, -1.0, 0)):
            out = jnp.where(jnp.logical_and(half >= ones_off, half < ones_off + 3), 1.0, 0.0)
            for j, piece in enumerate(pieces):
                sel = jnp.where(jnp.logical_and(col == base + off + j, row < FOX_HEADS), sign, 0.0).astype(BF16)
                out = out + _dot(piece, sel, NN)
            o_ref[...] = out.astype(BF16)

    return pl.pallas_call(
        body, name=name, grid=(s // TM,), in_specs=[_row_spec(TM, 128)],
        out_specs=[_row_spec(TM, width), _row_spec(TM, width)],
        out_shape=[jax.ShapeDtypeStruct((s, width), BF16)] * 2,
        compiler_params=_cp("parallel"),
    )(cum)


def _fox_do_operand(dycat, ycat, after, name):
    s = dycat.shape[0]

    def body(do_ref, o_ref, after_ref, ad_ref):
        del after_ref
        lane = lax.broadcasted_iota(jnp.int32, (1, 128), 1)
        dd = do_ref[...].astype(F32) * o_ref[...].astype(F32)
        d0 = jnp.sum(jnp.where(lane < 64, dd, 0.0), axis=1, keepdims=True)
        d1 = jnp.sum(dd, axis=1, keepdims=True) - d0
        ad_ref[...] = _operand_lanes(-d0, -d1, None)

    blk = pl.BlockSpec((TM, 128), lambda i, p: (i, Q_BLK + p))
    return pl.pallas_call(
        body, name=name, grid=(s // TM, HEAD_PAIRS), in_specs=[blk, blk, pl.BlockSpec(memory_space=pl.ANY)],
        out_specs=pl.BlockSpec((TM, 128), lambda i, p: (i, p)),
        out_shape=jax.ShapeDtypeStruct((s, HEAD_PAIRS * 128), BF16),
        compiler_params=_cp("parallel", "parallel"),
    )(dycat, ycat, after)


def _causal_pairs(nq, key_major):
    if key_major:
        pairs = [(q, k) for k in range(nq) for q in range(k, nq)]
    else:
        pairs = [(q, k) for q in range(nq) for k in range(q + 1)]
    return (jnp.asarray([p[0] for p in pairs], jnp.int32), jnp.asarray([p[1] for p in pairs], jnp.int32))


def _diag_mask(sc):
    row = lax.broadcasted_iota(jnp.int32, sc.shape, 0)
    col = lax.broadcasted_iota(jnp.int32, sc.shape, 1)
    return jnp.where(row >= col, sc, NEG)


def _fox_fwd2(proj, aq, ak, name):
    s = proj.shape[0]
    nq = s // TQ
    qi_arr, ki_arr = _causal_pairs(nq, key_major=False)

    def body(qi_ref, ki_ref, q_ref, k_ref, v_ref, aq_ref, ak_ref, o_ref, aqb_ref, m_ref, acc_ref, aux_ref):
        t = pl.program_id(1)
        qi, ki = qi_ref[t], ki_ref[t]
        lane = lax.broadcasted_iota(jnp.int32, (1, 128), 1)
        masks = [lane < 64, lane >= 64]
        ones_v = jnp.where((lane & 63) == 8, 1.0, 0.0).astype(BF16)

        @pl.when(ki == 0)
        def _():
            m_ref[...] = jnp.full_like(m_ref, NEG)
            acc_ref[...] = jnp.zeros_like(acc_ref)
            aux_ref[...] = jnp.zeros_like(aux_ref)

        def step(diag):
            q2s = q_ref[...] * 0.125
            k2, v2, aq2, ak2 = k_ref[...], v_ref[...], aq_ref[...], ak_ref[...]
            pv, alpha = [], []
            for hh in range(2):
                qh = jnp.where(masks[hh], q2s, aq2)
                kh = jnp.where(masks[hh], k2, ak2)
                vh = jnp.where(masks[hh], v2, ones_v)
                sc = _dot(qh, kh, NT)
                if diag:
                    sc = _diag_mask(sc)
                m_prev = m_ref[hh]
                m_new = jnp.maximum(m_prev, jnp.max(sc, axis=1, keepdims=True))
                m_ref[hh] = m_new
                alpha.append(jnp.exp(m_prev - m_new))
                pv.append(_dot(jnp.exp(sc - m_new).astype(BF16), vh, NN))
            acc_ref[...] = acc_ref[...] * jnp.where(masks[0], alpha[0], alpha[1]) + jnp.where(masks[0], pv[0], pv[1])
            aux_ref[...] = aux_ref[...] * jnp.where(masks[0], alpha[1], alpha[0]) + jnp.where(masks[0], pv[1], pv[0])

        @pl.when(ki < qi)
        def _():
            step(False)

        @pl.when(ki == qi)
        def _():
            step(True)
            aux = aux_ref[...]
            l0, l1 = aux[:, 72:73], aux[:, 8:9]
            o_ref[...] = (acc_ref[...] * jnp.where(masks[0], 1.0 / l0, 1.0 / l1)).astype(BF16)
            aqf = aq_ref[...].astype(F32)
            cum0 = aqf[:, 64:65] + aqf[:, 65:66] + aqf[:, 66:67]
            cum1 = aqf[:, 0:1] + aqf[:, 1:2] + aqf[:, 2:3]
            aqb_ref[...] = _operand_lanes(cum0 - (m_ref[0] + jnp.log(l0)), cum1 - (m_ref[1] + jnp.log(l1)), 3)

    grid_spec = pltpu.PrefetchScalarGridSpec(
        num_scalar_prefetch=2, grid=(HEAD_PAIRS, int(qi_arr.shape[0])),
        in_specs=[pl.BlockSpec((TQ, 128), lambda p, t, qi, ki: (qi[t], Q_BLK + p)),
                  pl.BlockSpec((TQ, 128), lambda p, t, qi, ki: (ki[t], K_BLK + p)),
                  pl.BlockSpec((TQ, 128), lambda p, t, qi, ki: (ki[t], V_BLK + p)),
                  pl.BlockSpec((TQ, 128), lambda p, t, qi, ki: (qi[t], p)),
                  pl.BlockSpec((TQ, 128), lambda p, t, qi, ki: (ki[t], p))],
        out_specs=[pl.BlockSpec((TQ, 128), lambda p, t, qi, ki: (qi[t], Q_BLK + p)),
                   pl.BlockSpec((TQ, 128), lambda p, t, qi, ki: (qi[t], p))],
        scratch_shapes=[pltpu.VMEM((2, TQ, 1), F32), pltpu.VMEM((TQ, 128), F32), pltpu.VMEM((TQ, 128), F32)])
    return pl.pallas_call(
        body, name=name, grid_spec=grid_spec,
        out_shape=[jax.ShapeDtypeStruct((s, D_MODEL), BF16), jax.ShapeDtypeStruct((s, HEAD_PAIRS * 128), BF16)],
        compiler_params=_cp("parallel", "arbitrary"),
    )(qi_arr, ki_arr, proj, proj, proj, aq, ak)


def _fox_bwd2(proj, dycat, aqb, ak, ad, name):
    s = proj.shape[0]
    nq = s // TQ
    qi_arr, ki_arr = _causal_pairs(nq, key_major=True)

    def body(qi_ref, ki_ref, q_ref, k_ref, v_ref, do_ref, aq_ref, ak_ref, ad_ref,
             dq_ref, dk_ref, dv_ref, qaux_ref, kaux_ref, dq_acc, qaux_acc, dk_acc, dv_acc, kaux_acc):
        t = pl.program_id(1)
        qi, ki = qi_ref[t], ki_ref[t]
        lane = lax.broadcasted_iota(jnp.int32, (1, 128), 1)
        masks = [lane < 64, lane >= 64]
        ones_v = jnp.where((lane & 63) < 3, 1.0, 0.0).astype(BF16)

        @pl.when(qi == ki)
        def _():
            dk_acc[...] = jnp.zeros_like(dk_acc)
            dv_acc[...] = jnp.zeros_like(dv_acc)
            kaux_acc[...] = jnp.zeros_like(kaux_acc)

        def step(diag):
            q2s = q_ref[...] * 0.125
            k2, v2, do2 = k_ref[...], v_ref[...], do_ref[...]
            aq2, ak2, ad2 = aq_ref[...], ak_ref[...], ad_ref[...]
            dq, dk, dv = [], [], []
            for hh in range(2):
                qh = jnp.where(masks[hh], q2s, aq2)
                kh = jnp.where(masks[hh], k2, ak2)
                doh = jnp.where(masks[hh], do2, ad2)
                vh = jnp.where(masks[hh], v2, ones_v)
                sc = _dot(qh, kh, NT)
                if diag:
                    sc = _diag_mask(sc)
                p = jnp.exp(sc)
                dsb = (p * _dot(doh, vh, NT)).astype(BF16)
                dv.append(_dot(p.astype(BF16), doh, TN))
                dk.append(_dot(dsb, qh, TN))
                dq.append(_dot(dsb, kh, NN))
            dk_acc[...] += jnp.where(masks[0], dk[0], dk[1])
            kaux_acc[...] += jnp.where(masks[0], dk[1], dk[0])
            dv_acc[...] += jnp.where(masks[0], dv[0], dv[1])
            dq_new = jnp.where(masks[0], dq[0], dq[1])
            qaux_new = jnp.where(masks[0], dq[1], dq[0])

            @pl.when(ki == 0)
            def _():
                dq_acc[qi] = dq_new
                qaux_acc[qi] = qaux_new

            @pl.when(ki > 0)
            def _():
                dq_acc[qi] += dq_new
                qaux_acc[qi] += qaux_new

        @pl.when(qi > ki)
        def _():
            step(False)

        @pl.when(qi == ki)
        def _():
            step(True)
            rows = pl.ds(pl.multiple_of(qi * TQ, TQ), TQ)
            dq_ref[rows, :] = (dq_acc[qi] * 0.125).astype(BF16)
            qaux_ref[rows, :] = qaux_acc[qi]

        @pl.when(qi == nq - 1)
        def _():
            dk_ref[...] = dk_acc[...].astype(BF16)
            dv_ref[...] = dv_acc[...].astype(BF16)
            kaux_ref[...] = kaux_acc[...]

    grid_spec = pltpu.PrefetchScalarGridSpec(
        num_scalar_prefetch=2, grid=(HEAD_PAIRS, int(qi_arr.shape[0])),
        in_specs=[pl.BlockSpec((TQ, 128), lambda p, t, qi, ki: (qi[t], Q_BLK + p)),
                  pl.BlockSpec((TQ, 128), lambda p, t, qi, ki: (ki[t], K_BLK + p)),
                  pl.BlockSpec((TQ, 128), lambda p, t, qi, ki: (ki[t], V_BLK + p)),
                  pl.BlockSpec((TQ, 128), lambda p, t, qi, ki: (qi[t], Q_BLK + p)),
                  pl.BlockSpec((TQ, 128), lambda p, t, qi, ki: (qi[t], p)),
                  pl.BlockSpec((TQ, 128), lambda p, t, qi, ki: (ki[t], p)),
                  pl.BlockSpec((TQ, 128), lambda p, t, qi, ki: (qi[t], p))],
        out_specs=[pl.BlockSpec((s, 128), lambda p, t, qi, ki: (0, p)),
                   pl.BlockSpec((TQ, 128), lambda p, t, qi, ki: (ki[t], p)),
                   pl.BlockSpec((TQ, 128), lambda p, t, qi, ki: (ki[t], p)),
                   pl.BlockSpec((None, s, 128), lambda p, t, qi, ki: (p, 0, 0)),
                   pl.BlockSpec((None, TQ, 128), lambda p, t, qi, ki: (p, ki[t], 0))],
        scratch_shapes=[pltpu.VMEM((nq, TQ, 128), F32), pltpu.VMEM((nq, TQ, 128), F32),
                        pltpu.VMEM((TQ, 128), F32), pltpu.VMEM((TQ, 128), F32), pltpu.VMEM((TQ, 128), F32)])
    return pl.pallas_call(
        body, name=name, grid_spec=grid_spec,
        out_shape=[jax.ShapeDtypeStruct((s, D_FOX), BF16)] * 3 + [jax.ShapeDtypeStruct((HEAD_PAIRS, s, 128), F32)] * 2,
        compiler_params=_cp("arbitrary", "arbitrary"),
    )(qi_arr, ki_arr, proj, proj, proj, dycat, aqb, ak, ad)


XA_SCALE = XA_DIM ** -0.5


def _xattn_fwd(q2, kv, name):
    s = q2.shape[0]
    m = kv.shape[0]

    def body(q_ref, kv_ref, o_ref):
        for h in range(XA_HEADS):
            c0 = h * XA_DIM
            sc = _dot(q_ref[:, c0:c0 + XA_DIM], kv_ref[:, c0:c0 + XA_DIM], NT) * XA_SCALE
            e = jnp.exp(sc - jnp.max(sc, axis=1, keepdims=True))
            p = e / jnp.sum(e, axis=1, keepdims=True)
            o_ref[:, c0:c0 + XA_DIM] = _dot(p.astype(BF16), kv_ref[:, D_MODEL + c0:D_MODEL + c0 + XA_DIM], NN).astype(BF16)

    return pl.pallas_call(
        body, name=name, grid=(s // TM,),
        in_specs=[_row_spec(TM, D_MODEL), pl.BlockSpec((m, 2 * D_MODEL), lambda i: (0, 0))],
        out_specs=_row_spec(TM, D_MODEL), out_shape=jax.ShapeDtypeStruct((s, D_MODEL), BF16),
        compiler_params=_cp("parallel"),
    )(q2, kv)


def _xattn_bwd(q2, kv, do2, name):
    s = q2.shape[0]
    m = kv.shape[0]

    def body(q_ref, kv_ref, do_ref, dq_ref, dkv_ref):
        i = pl.program_id(0)

        @pl.when(i == 0)
        def _():
            dkv_ref[...] = jnp.zeros_like(dkv_ref)

        for h in range(XA_HEADS):
            c0 = h * XA_DIM
            v0 = D_MODEL + c0
            qh = q_ref[:, c0:c0 + XA_DIM]
            kh = kv_ref[:, c0:c0 + XA_DIM]
            doh = do_ref[:, c0:c0 + XA_DIM]
            sc = _dot(qh, kh, NT) * XA_SCALE
            e = jnp.exp(sc - jnp.max(sc, axis=1, keepdims=True))
            p = e / jnp.sum(e, axis=1, keepdims=True)
            dp = _dot(doh, kv_ref[:, v0:v0 + XA_DIM], NT)
            ds = p * (dp - jnp.sum(p * dp, axis=1, keepdims=True))
            dsb = (ds * XA_SCALE).astype(BF16)
            dq_ref[:, c0:c0 + XA_DIM] = _dot(dsb, kh, NN).astype(BF16)
            dkv_ref[:, c0:c0 + XA_DIM] += _dot(dsb, qh, TN)
            dkv_ref[:, v0:v0 + XA_DIM] += _dot(p.astype(BF16), doh, TN)

    return pl.pallas_call(
        body, name=name, grid=(s // TM,),
        in_specs=[_row_spec(TM, D_MODEL), pl.BlockSpec((m, 2 * D_MODEL), lambda i: (0, 0)), _row_spec(TM, D_MODEL)],
        out_specs=[_row_spec(TM, D_MODEL), pl.BlockSpec((m, 2 * D_MODEL), lambda i: (0, 0))],
        out_shape=[jax.ShapeDtypeStruct((s, D_MODEL), BF16), jax.ShapeDtypeStruct((m, 2 * D_MODEL), F32)],
        compiler_params=_cp("arbitrary"),
    )(q2, kv, do2)


GELU_C = math.sqrt(2.0 / math.pi)
GELU_A = 0.044715


def _gelu(x):
    return 0.5 * x * (1.0 + jnp.tanh(GELU_C * (x + GELU_A * x * x * x)))


def _gelu_and_grad(x):
    t = jnp.tanh(GELU_C * (x + GELU_A * x * x * x))
    g = 0.5 * x * (1.0 + t)
    dg = 0.5 * (1.0 + t) + 0.5 * x * (1.0 - t * t) * GELU_C * (1.0 + 3.0 * GELU_A * x * x)
    return g, dg


def _shift_down(main, prev8):
    row = lax.broadcasted_iota(jnp.int32, main.shape, 0)
    s1 = jnp.where(row == 0, prev8[7:8, :], pltpu.roll(main, 1, 0))
    s2 = jnp.where(row == 0, prev8[6:7, :], jnp.where(row == 1, prev8[7:8, :], pltpu.roll(main, 2, 0)))
    return s1, s2


def _shift_up(main, next8):
    n = main.shape[0]
    row = lax.broadcasted_iota(jnp.int32, main.shape, 0)
    u1 = jnp.where(row == n - 1, next8[0:1, :], pltpu.roll(main, n - 1, 0))
    u2 = jnp.where(row == n - 2, next8[0:1, :], jnp.where(row == n - 1, next8[1:2, :], pltpu.roll(main, n - 2, 0)))
    return u1, u2


def _conv(h, s1, s2, w_ref, b_ref):
    return w_ref[0:1, :] * s2 + w_ref[1:2, :] * s1 + w_ref[2:3, :] * h + b_ref[...]


def _ffn_fwd(h3, w_up, cw, cb, w_down, name):
    s = h3.shape[0]
    tn = TN_FF
    nj = D_FF // tn
    per = D_MODEL // tn
    hb = TM // 8

    def body(h_ref, halo_ref, wg_ref, wu_ref, cwg_ref, cwu_ref, cbg_ref, cbu_ref, wd_ref,
             hg_ref, hu_ref, a_ref, y_ref):
        i, j = pl.program_id(0), pl.program_id(1)
        h = h_ref[...]
        halo = halo_ref[...]
        halo = jnp.where(i > 0, halo, jnp.zeros_like(halo))
        conv = []
        for w_ref, cw_ref, cb_ref, hid_ref in ((wg_ref, cwg_ref, cbg_ref, hg_ref), (wu_ref, cwu_ref, cbu_ref, hu_ref)):
            hm_b = _dot(h, w_ref[...], NN).astype(BF16)
            hid_ref[...] = hm_b
            hm = hm_b.astype(F32)
            hl = _dot(halo, w_ref[...], NN).astype(BF16).astype(F32)
            s1, s2 = _shift_down(hm, hl)
            conv.append(_conv(hm, s1, s2, cw_ref, cb_ref))
        a = (_gelu(conv[0]) * conv[1]).astype(BF16)
        a_ref[...] = a
        contrib = _dot(a, wd_ref[...], NN)

        @pl.when(j == 0)
        def _():
            y_ref[...] = contrib

        @pl.when(j > 0)
        def _():
            y_ref[...] += contrib

    return pl.pallas_call(
        body, name=name, grid=(s // TM, nj),
        in_specs=[pl.BlockSpec((TM, D_MODEL), lambda i, j: (i, 0)),
                  pl.BlockSpec((8, D_MODEL), lambda i, j: (jnp.maximum(i * hb - 1, 0), 0)),
                  pl.BlockSpec((None, D_MODEL, tn), lambda i, j: (j // per, 0, j % per)),
                  pl.BlockSpec((None, D_MODEL, tn), lambda i, j: (NDEV // 2 + j // per, 0, j % per)),
                  pl.BlockSpec((8, tn), lambda i, j: (0, j)),
                  pl.BlockSpec((8, tn), lambda i, j: (0, nj + j)),
                  pl.BlockSpec((1, tn), lambda i, j: (0, j)),
                  pl.BlockSpec((1, tn), lambda i, j: (0, nj + j)),
                  pl.BlockSpec((tn, D_MODEL), lambda i, j: (j, 0))],
        out_specs=[pl.BlockSpec((TM, tn), lambda i, j: (i, j)), pl.BlockSpec((TM, tn), lambda i, j: (i, j)),
                   pl.BlockSpec((TM, tn), lambda i, j: (i, j)), pl.BlockSpec((TM, D_MODEL), lambda i, j: (i, 0))],
        out_shape=[jax.ShapeDtypeStruct((s, D_FF), BF16), jax.ShapeDtypeStruct((s, D_FF), BF16),
                   jax.ShapeDtypeStruct((s, D_FF), BF16), jax.ShapeDtypeStruct((s, D_MODEL), F32)],
        compiler_params=_cp("parallel", "arbitrary"),
    )(h3, h3, w_up, w_up, cw, cw, cb, cb, w_down)


def _ffn_bwd(dy3, w_down, hid_g, hid_u, cw, cb, name):
    s = dy3.shape[0]
    n = s // TM
    tn = TN_FF
    nj = D_FF // tn
    hb = TM // 8
    last8 = s // 8 - 1

    def body(dy_ref, dyp_ref, wd_ref, hg_ref, hgl_ref, hgn_ref, hu_ref, hul_ref, hun_ref,
             cwg_ref, cwu_ref, cbg_ref, cbu_ref,
             dhg_ref, dhu_ref, dcwg_ref, dcwu_ref, dcbg_ref, dcbu_ref):
        i = pl.program_id(1)
        first, last = i == 0, i == n - 1
        da = _dot(dy_ref[...], wd_ref[...], NT)
        dyp = dyp_ref[...]
        dyp = jnp.where(last, jnp.zeros_like(dyp), dyp)
        da_n = _dot(dyp, wd_ref[...], NT)
        parts = []
        for h_ref, hl_ref, hn_ref, cw_ref, cb_ref in ((hg_ref, hgl_ref, hgn_ref, cwg_ref, cbg_ref),
                                                     (hu_ref, hul_ref, hun_ref, cwu_ref, cbu_ref)):
            hm = h_ref[...].astype(F32)
            hl = jnp.where(first, 0.0, hl_ref[...].astype(F32))
            hn = hn_ref[...].astype(F32)
            s1, s2 = _shift_down(hm, hl)
            c = _conv(hm, s1, s2, cw_ref, cb_ref)
            n1, n2 = _shift_down(hn, hm[TM - 8:, :])
            cn = _conv(hn, n1, n2, cw_ref, cb_ref)
            parts.append((hm, s1, s2, c, cn))
        g, dg = _gelu_and_grad(parts[0][3])
        gn, dgn = _gelu_and_grad(parts[0][4])
        dc_g = da * parts[1][3] * dg
        dc_u = da * g
        dcn_g = da_n * parts[1][4] * dgn
        dcn_u = da_n * gn
        outs = ((dc_g, dcn_g, parts[0], cwg_ref, dhg_ref, dcwg_ref, dcbg_ref),
                (dc_u, dcn_u, parts[1], cwu_ref, dhu_ref, dcwu_ref, dcbu_ref))
        for dc, dcn, (hm, s1, s2, _, _), cw_ref, dh_ref, dcw_ref, dcb_ref in outs:
            u1, u2 = _shift_up(dc, dcn)
            dh_ref[...] = (cw_ref[2:3, :] * dc + cw_ref[1:2, :] * u1 + cw_ref[0:1, :] * u2).astype(BF16)
            dcb = jnp.sum(dc, axis=0, keepdims=True)
            row8 = lax.broadcasted_iota(jnp.int32, (8, tn), 0)
            dcw = jnp.where(row8 == 0, jnp.sum(dc * s2, axis=0, keepdims=True),
                            jnp.where(row8 == 1, jnp.sum(dc * s1, axis=0, keepdims=True),
                                      jnp.where(row8 == 2, jnp.sum(dc * hm, axis=0, keepdims=True), 0.0)))

            @pl.when(first)
            def _():
                dcw_ref[...] = dcw
                dcb_ref[...] = dcb

            @pl.when(i > 0)
            def _():
                dcw_ref[...] += dcw
                dcb_ref[...] += dcb

    prev8 = lambda j, i: (jnp.maximum(i * hb - 1, 0), j)
    next8 = lambda j, i: (jnp.minimum((i + 1) * hb, last8), j)
    blk = lambda j, i: (i, j)
    col = lambda j, i: (0, j)
    colu = lambda j, i: (0, nj + j)
    return pl.pallas_call(
        body, name=name, grid=(nj, n),
        in_specs=[pl.BlockSpec((TM, D_MODEL), lambda j, i: (i, 0)),
                  pl.BlockSpec((8, D_MODEL), lambda j, i: (jnp.minimum((i + 1) * hb, last8), 0)),
                  pl.BlockSpec((tn, D_MODEL), lambda j, i: (j, 0)),
                  pl.BlockSpec((TM, tn), blk), pl.BlockSpec((8, tn), prev8), pl.BlockSpec((8, tn), next8),
                  pl.BlockSpec((TM, tn), blk), pl.BlockSpec((8, tn), prev8), pl.BlockSpec((8, tn), next8),
                  pl.BlockSpec((8, tn), col), pl.BlockSpec((8, tn), colu),
                  pl.BlockSpec((1, tn), col), pl.BlockSpec((1, tn), colu)],
        out_specs=[pl.BlockSpec((TM, tn), blk), pl.BlockSpec((TM, tn), blk),
                   pl.BlockSpec((8, tn), col), pl.BlockSpec((8, tn), col),
                   pl.BlockSpec((1, tn), col), pl.BlockSpec((1, tn), col)],
        out_shape=[jax.ShapeDtypeStruct((s, D_FF), BF16), jax.ShapeDtypeStruct((s, D_FF), BF16),
                   jax.ShapeDtypeStruct((8, D_FF), F32), jax.ShapeDtypeStruct((8, D_FF), F32),
                   jax.ShapeDtypeStruct((1, D_FF), F32), jax.ShapeDtypeStruct((1, D_FF), F32)],
        compiler_params=_cp("parallel", "arbitrary"),
    )(dy3, dy3, w_down, hid_g, hid_g, hid_g, hid_u, hid_u, hid_u, cw, cw, cb, cb)


def _slot(p):
    return 4 * p[0] + 2 * p[1] + p[2]


def _all_gather(shards, name):
    n = len(shards)

    def body(*refs):
        ins, outs = refs[:n], refs[n:2 * n]
        send_sems, recv_sems, local_sems = refs[2 * n:]
        x, y, c = lax.axis_index("x"), lax.axis_index("y"), lax.axis_index("c")
        me, sibling = (x, y, c), (x, y, 1 - c)
        chips = [(1 - x, y), (x, 1 - y), (1 - x, 1 - y)]

        def copy(a, k, block, to, from_input=False):
            dst = outs[a].at[_slot(block)]
            return pltpu.make_async_remote_copy(
                src_ref=ins[a] if from_input else dst, dst_ref=dst,
                send_sem=send_sems.at[a, k], recv_sem=recv_sems.at[a, k],
                device_id=to, device_id_type=MESH)

        mine = [pltpu.make_async_copy(ins[a], outs[a].at[_slot(me)], local_sems.at[a]) for a in range(n)]
        for cp in mine:
            cp.start()
        first = []
        for a in range(n):
            first.append(copy(a, 0, me, sibling, True))
            first += [copy(a, 1 + j, me, (*chip, c), True) for j, chip in enumerate(chips)]
        for cp in first:
            cp.start()
        passed = []
        for j, chip in enumerate(chips):
            for a in range(n):
                copy(a, 1 + j, (*chip, c), me).wait_recv()
                fwd = copy(a, 4 + j, (*chip, c), sibling)
                fwd.start()
                passed.append(fwd)
        for a in range(n):
            copy(a, 0, sibling, me).wait_recv()
            for j, chip in enumerate(chips):
                copy(a, 4 + j, (*chip, 1 - c), me).wait_recv()
        for cp in first + passed:
            cp.wait_send()
        for cp in mine:
            cp.wait()

    any_spec = pl.BlockSpec(memory_space=pl.ANY)
    return pl.pallas_call(
        body, name=name,
        in_specs=[any_spec] * n, out_specs=[any_spec] * n,
        out_shape=[jax.ShapeDtypeStruct((NDEV,) + s.shape, s.dtype) for s in shards],
        scratch_shapes=[pltpu.SemaphoreType.DMA((n, 7)), pltpu.SemaphoreType.DMA((n, 7)),
                        pltpu.SemaphoreType.DMA((n,))],
    )(*shards)


def _scatter_blocks(full, name):
    n = len(full)

    def body(*refs):
        ins, outs = refs[:n], refs[n:2 * n]
        send_sems, recv_sems, local_sems = refs[2 * n:]
        x, y, c = lax.axis_index("x"), lax.axis_index("y"), lax.axis_index("c")
        me = (x, y, c)
        copies = []
        for a in range(n):
            cp = pltpu.make_async_copy(ins[a].at[_slot(me)], outs[a].at[_slot(me)], local_sems.at[a])
            cp.start()
            copies.append(cp)
        remote = []
        for mask in range(1, NDEV):
            peer = (1 - x if mask & 4 else x, 1 - y if mask & 2 else y, 1 - c if mask & 1 else c)
            for a in range(n):
                cp = pltpu.make_async_remote_copy(
                    src_ref=ins[a].at[_slot(peer)], dst_ref=outs[a].at[_slot(me)],
                    send_sem=send_sems.at[a, mask - 1], recv_sem=recv_sems.at[a, mask - 1],
                    device_id=peer, device_id_type=MESH)
                cp.start()
                remote.append(cp)
        for cp in remote:
            cp.wait()
        for cp in copies:
            cp.wait()

    any_spec = pl.BlockSpec(memory_space=pl.ANY)
    return pl.pallas_call(
        body, name=name,
        in_specs=[any_spec] * n, out_specs=[any_spec] * n,
        out_shape=[jax.ShapeDtypeStruct(f.shape, f.dtype) for f in full],
        scratch_shapes=[pltpu.SemaphoreType.DMA((n, 7)), pltpu.SemaphoreType.DMA((n, 7)),
                        pltpu.SemaphoreType.DMA((n,))],
    )(*full)


def _peer_list(x, y, c):
    return [(1 - x if m & 4 else x, 1 - y if m & 2 else y, 1 - c if m & 1 else c) for m in range(1, NDEV)]


def _exchange_copies(src_refs, land_refs, send_sems, recv_sems, gather):
    x, y, c = lax.axis_index("x"), lax.axis_index("y"), lax.axis_index("c")
    me = (x, y, c)
    copies = []
    for m, peer in enumerate(_peer_list(x, y, c)):
        for a in range(len(src_refs)):
            copies.append(pltpu.make_async_remote_copy(
                src_ref=src_refs[a] if gather else src_refs[a].at[_slot(peer)], dst_ref=land_refs[a].at[_slot(me)],
                send_sem=send_sems.at[a * (NDEV - 1) + m], recv_sem=recv_sems.at[a * (NDEV - 1) + m],
                device_id=peer, device_id_type=MESH))
    return copies


def _exchange_start(srcs, lands, after, gather, name):
    n = len(srcs)
    hbm = pl.BlockSpec(memory_space=pltpu.HBM)

    def body(*refs):
        for cp in _exchange_copies(refs[:n], refs[n:2 * n], refs[2 * n + 1], refs[2 * n + 2], gather):
            cp.start()
        token = refs[-1]
        token[...] = jnp.zeros_like(token)

    outs = pl.pallas_call(
        body, name=name,
        out_shape=(pltpu.SemaphoreType.DMA((n * (NDEV - 1),)), pltpu.SemaphoreType.DMA((n * (NDEV - 1),)),
                   *[pltpu.HBM(a.shape, a.dtype) for a in list(srcs) + list(lands)],
                   jax.ShapeDtypeStruct((8, 128), F32)),
        in_specs=[hbm] * (2 * n) + [pl.BlockSpec(memory_space=pl.ANY)],
        out_specs=(pl.BlockSpec(memory_space=pltpu.SEMAPHORE), pl.BlockSpec(memory_space=pltpu.SEMAPHORE),
                   *[hbm] * (2 * n), pl.BlockSpec(memory_space=pltpu.VMEM)),
        input_output_aliases={i: 2 + i for i in range(2 * n)},
        compiler_params=pltpu.CompilerParams(has_side_effects=pltpu.SideEffectType.DATAFLOW_SIDE_EFFECTING),
    )(*[pltpu.with_memory_space_constraint(a, pltpu.HBM) for a in list(srcs) + list(lands)], after)
    return outs[0], outs[1], outs[2:2 + n], outs[2 + n:2 + 2 * n], outs[-1]


def _exchange_wait(send_sems, recv_sems, srcs, lands, after, gather, name):
    n = len(srcs)
    hbm = pl.BlockSpec(memory_space=pltpu.HBM)

    def body(*refs):
        for cp in _exchange_copies(refs[:n], refs[n:2 * n], refs[2 * n], refs[2 * n + 1], gather):
            cp.wait_send()
            cp.wait_recv()

    outs = pl.pallas_call(
        body, name=name,
        out_shape=tuple(pltpu.HBM(a.shape, a.dtype) for a in list(srcs) + list(lands)),
        in_specs=[hbm] * (2 * n) + [pl.BlockSpec(memory_space=pltpu.SEMAPHORE)] * 2 + [pl.BlockSpec(memory_space=pl.ANY)],
        out_specs=tuple([hbm] * (2 * n)),
        input_output_aliases={i: i for i in range(2 * n)},
        compiler_params=pltpu.CompilerParams(has_side_effects=pltpu.SideEffectType.DATAFLOW_SIDE_EFFECTING),
    )(*srcs, *lands, send_sems, recv_sems, after)
    return outs[n:]


def _own_slot(block):
    me = 4 * lax.axis_index("x") + 2 * lax.axis_index("y") + lax.axis_index("c")
    return lax.dynamic_update_slice(jnp.zeros((NDEV,) + block.shape, block.dtype), block[None], (me, 0, 0))


def _adamw(parts, w, m, v, name):
    r, c = w.shape
    tr = r if r * c <= 160 * 1024 else max(8, (160 * 1024 // c) // 8 * 8)
    while r % tr:
        tr -= 8
    bc1 = 1.0 - ADAM_B1 ** ADAM_STEP
    bc2 = 1.0 - ADAM_B2 ** ADAM_STEP

    def body(p_ref, w_ref, m_ref, v_ref, g_ref, d_ref, mo_ref, vo_ref):
        g = p_ref[0].astype(F32)
        for d in range(1, NDEV):
            g = g + p_ref[d].astype(F32)
        g_ref[...] = g
        mn = ADAM_B1 * m_ref[...] + (1.0 - ADAM_B1) * g
        vn = ADAM_B2 * v_ref[...] + (1.0 - ADAM_B2) * (g * g)
        mo_ref[...] = mn
        vo_ref[...] = vn
        d_ref[...] = -ADAM_LR * ((mn / bc1) / (jnp.sqrt(vn / bc2) + ADAM_EPS) + ADAM_WD * w_ref[...])

    spec = pl.BlockSpec((tr, c), lambda i: (i, 0))
    return pl.pallas_call(
        body, name=name, grid=(r // tr,),
        in_specs=[pl.BlockSpec((NDEV, tr, c), lambda i: (0, i, 0)), spec, spec, spec],
        out_specs=[spec] * 4, out_shape=[jax.ShapeDtypeStruct((r, c), F32)] * 4,
        compiler_params=_cp("parallel"),
    )(parts, w, m, v)


def _pair_cols(a):
    s = a.shape[0]
    t = a[:, :FOX_HEADS].reshape(s, HEAD_PAIRS, 2).transpose(1, 0, 2)
    return jnp.pad(t, ((0, 0), (0, 0), (0, 126)))


def _pair_rows(at):
    s = at.shape[1]
    return jnp.pad(at[:FOX_HEADS].reshape(HEAD_PAIRS, 2, s), ((0, 0), (0, 6), (0, 0)))


def _local_step(x, mem, tgt, gains, b_forget, w_pool, pool_scale, conv_b, w_in,
                mix_weights, ffn_weights, send_in_grad, send_mix_grads, send_ffn_grads):
    w_f = w_in[:, F_COL:]
    b_pad = jnp.pad(b_forget, ((0, 0), (0, 128 - FOX_HEADS)))
    wbd = jnp.zeros((D_POOL, D_POOL), F32)
    for g in range(4):
        wbd = wbd.at[64 * g:64 * g + 64, 64 * g:64 * g + 64].set(w_pool[g])
    wbd = wbd.astype(BF16)
    scale = pool_scale.reshape(1, D_POOL)

    h1 = _norm_fwd(x, gains["mix_pre"], "norm_mix_pre")
    proj = _mm(h1, w_in, "nn", BF16, 1024, 384, 1024, "proj_in")
    fraw = _mm(h1, w_f, "nn", F32, 1024, 128, 1024, "proj_gate")
    flog, cum = _gate_cumsum(fraw, b_pad, "gate_cumsum")
    aq, ak = _fox_operands(cum, "fox_operands")
    ycat, aqb = _fox_fwd2(proj, aq, ak, "fox_fwd")
    ycat = _pool_fwd(proj, wbd, scale, ycat, "pool_fwd")
    w_mix, w_xq, w_xo, w_xkv = mix_weights(ycat)
    y1 = _mm(ycat, w_mix, "nn", F32, 1024, 1024, 1024, "mix_out")
    x1, h2 = _resid_norm_fwd(x, y1, gains["mix_post"], gains["xa_pre"], "resid_mix")
    q2 = _mm(h2, w_xq, "nn", BF16, 1024, 1024, 1024, "xa_q")
    mem_n = _norm_fwd(mem, gains["mem"], "norm_mem")
    kv = _mm(mem_n, w_xkv, "nn", BF16, mem.shape[0], 256, 1024, "xa_kv", b_cols=256)
    o2 = _xattn_fwd(q2, kv, "xattn_fwd")
    y2 = _mm(o2, w_xo, "nn", F32, 1024, 1024, 1024, "xa_out")
    x2, h3 = _resid_norm_fwd(x1, y2, gains["xa_post"], gains["ffn_pre"], "resid_xa")
    w_up, w_down, cw = ffn_weights(h3)
    hid_g, hid_u, act, y3 = _ffn_fwd(h3, w_up, cw, conv_b, w_down, "ffn_fwd")

    loss, dx3, dy3, dg_ffn_post = _loss_bwd(x2, y3, gains["ffn_post"], tgt, "loss_bwd")
    dhid_g, dhid_u, dcw_g, dcw_u, dcb_g, dcb_u = _ffn_bwd(dy3, w_down, hid_g, hid_u, cw, conv_b, "ffn_bwd")
    d_w_down = _mm(act, dy3, "tn", BF16, 1024, 1024, 1024, "dw_down")
    d_w_up = _mm(h3, [dhid_g, dhid_u], "tn", BF16, 1024, 1024, 1024, "dw_up", out_cols=1024)
    sent = send_ffn_grads(d_w_up, d_w_down, jnp.concatenate([dcw_g, dcw_u], axis=1))
    dh3 = _mm([dhid_g, dhid_u], w_up, "nt", F32, 1024, 1024, 1024, "dh_ffn", b_cols=1024)
    dx2, dg_ffn_pre, dy2, dg_xa_post = _norm_bwd(dh3, x2, dx3, gains["ffn_pre"] + sent[0, 0], "norm_bwd_ffn",
                                                 prev=(y2, gains["xa_post"]))
    do2 = _mm(dy2, w_xo, "nt", BF16, 1024, 1024, 1024, "d_xa_out")
    d_w_xo = _mm(o2, dy2, "tn", BF16, 1024, 1024, 1024, "dw_xo")
    dq2, dkv = _xattn_bwd(q2, kv, do2, "xattn_bwd")
    dkv = dkv.astype(BF16)
    dh2 = _mm(dq2, w_xq, "nt", F32, 1024, 1024, 1024, "dh_xa")
    d_w_xq = _mm(h2, dq2, "tn", BF16, 1024, 1024, 1024, "dw_xq")
    dmem_n = _mm(dkv, w_xkv, "nt", F32, mem.shape[0], 1024, 256, "d_mem", b_cols=256)
    d_w_xkv = _mm(mem_n, dkv, "tn", BF16, 1024, 256, mem.shape[0], "dw_xkv", out_cols=256)
    _, dg_mem = _norm_bwd(dmem_n, mem, jnp.zeros_like(mem), gains["mem"], "norm_bwd_mem")
    dx1, dg_xa_pre, dy1, dg_mix_post = _norm_bwd(dh2, x1, dx2, gains["xa_pre"], "norm_bwd_xa",
                                                 prev=(y1, gains["mix_post"]))
    dycat = _mm(dy1, w_mix, "nt", BF16, 1024, 1024, 1024, "d_mix_out")
    d_w_mix = _mm(ycat, dy1, "tn", BF16, 1024, 1024, 1024, "dw_mix")
    sent_mix = send_mix_grads(d_w_mix, d_w_xq, d_w_xo, d_w_xkv)
    ad = _fox_do_operand(dycat, ycat, sent_mix, "fox_do_operand")
    dq, dk, dv, qaux, kaux = _fox_bwd2(proj, dycat, aqb, ak, ad, "fox_bwd")
    du, d_wbd, d_scale = _pool_bwd(proj, dycat, wbd, scale, "pool_bwd")
    df, db_f = _gate_bwd(qaux, kaux, flog, "gate_bwd")
    dproj = jnp.concatenate([du, dq, dk, dv, df], axis=1)
    sent_in = send_in_grad(_mm(h1, dproj, "tn", BF16, 1024, 384, 1024, "dw_in"))
    dh1 = _mm(dproj, w_in, "nt", F32, 1024, 1024, 896, "dh_mix", after=sent_in)
    grad_x, dg_mix_pre = _norm_bwd(dh1, x, dx1, gains["mix_pre"], "norm_bwd_mix")

    small = dict(
        mix_pre=dg_mix_pre, mix_post=dg_mix_post, mem=dg_mem, xa_pre=dg_xa_pre, xa_post=dg_xa_post,
        ffn_pre=dg_ffn_pre, ffn_post=dg_ffn_post,
        conv_b=jnp.concatenate([dcb_g, dcb_u], axis=1),
        w_pool=jnp.stack([d_wbd[64 * g:64 * g + 64, 64 * g:64 * g + 64] for g in range(4)]),
        pool_scale=d_scale.reshape(4, 64),
        b_forget=db_f[:, :FOX_HEADS],
    )
    return loss, grad_x, small


SMALL_ORDER = ("mix_pre", "mix_post", "mem", "xa_pre", "xa_post", "ffn_pre", "ffn_post", "conv_b",
               "w_pool", "pool_scale", "b_forget")
SMALL_ROWS = 256


def _pack_small(d):
    flat = jnp.concatenate([d[k].reshape(-1).astype(F32) for k in SMALL_ORDER])
    return jnp.pad(flat, (0, SMALL_ROWS * 128 - flat.shape[0])).reshape(SMALL_ROWS, 128)


def _unpack_small(a, like):
    flat = a.reshape(-1)
    out, off = {}, 0
    for k in SMALL_ORDER:
        n = like[k].size
        out[k] = flat[off:off + n].reshape(like[k].shape)
        off += n
    return out


def kernel(x, mem, norm_mix_pre, norm_mix_post, w_in, b_forget, w_pool, pool_scale, w_mix_out, norm_mem, norm_xa_pre, norm_xa_post, w_xq, w_xkv, w_xo, norm_ffn_pre, norm_ffn_post, w_up, conv_w, conv_b, w_down, loss_target, m_norm_mix_pre, m_norm_mix_post, m_w_in, m_b_forget, m_w_pool, m_pool_scale, m_w_mix_out, m_norm_mem, m_norm_xa_pre, m_norm_xa_post, m_w_xq, m_w_xkv, m_w_xo, m_norm_ffn_pre, m_norm_ffn_post, m_w_up, m_conv_w, m_conv_b, m_w_down, v_norm_mix_pre, v_norm_mix_post, v_w_in, v_b_forget, v_w_pool, v_pool_scale, v_w_mix_out, v_norm_mem, v_norm_xa_pre, v_norm_xa_post, v_w_xq, v_w_xkv, v_w_xo, v_norm_ffn_pre, v_norm_ffn_post, v_w_up, v_conv_w, v_conv_b, v_w_down):
    names = ("norm_mix_pre", "norm_mix_post", "w_in", "b_forget", "w_pool", "pool_scale", "w_mix_out", "norm_mem",
             "norm_xa_pre", "norm_xa_post", "w_xq", "w_xkv", "w_xo", "norm_ffn_pre", "norm_ffn_post", "w_up",
             "conv_w", "conv_b", "w_down")
    w = dict(zip(names, (norm_mix_pre, norm_mix_post, w_in, b_forget, w_pool, pool_scale, w_mix_out, norm_mem,
                         norm_xa_pre, norm_xa_post, w_xq, w_xkv, w_xo, norm_ffn_pre, norm_ffn_post, w_up,
                         conv_w, conv_b, w_down)))
    mo = dict(zip(names, (m_norm_mix_pre, m_norm_mix_post, m_w_in, m_b_forget, m_w_pool, m_pool_scale, m_w_mix_out,
                          m_norm_mem, m_norm_xa_pre, m_norm_xa_post, m_w_xq, m_w_xkv, m_w_xo, m_norm_ffn_pre,
                          m_norm_ffn_post, m_w_up, m_conv_w, m_conv_b, m_w_down)))
    vo = dict(zip(names, (v_norm_mix_pre, v_norm_mix_post, v_w_in, v_b_forget, v_w_pool, v_pool_scale, v_w_mix_out,
                          v_norm_mem, v_norm_xa_pre, v_norm_xa_post, v_w_xq, v_w_xkv, v_w_xo, v_norm_ffn_pre,
                          v_norm_ffn_post, v_w_up, v_conv_w, v_conv_b, v_w_down)))

    big_names = ("w_in", "w_mix_out", "w_xq", "w_xo", "w_xkv", "w_up", "w_down")
    shards = {k: w[k][0].astype(BF16) for k in big_names}
    shards["w_in"] = jnp.pad(shards["w_in"], ((0, 0), (0, D_IN_PAD - shards["w_in"].shape[1])))
    conv_w_sh = jnp.pad(conv_w[0, :, 0, :], ((0, 5), (0, 0)))
    (g_in,) = _all_gather([shards["w_in"]], "gather_w_in")
    mix_srcs = [shards[k] for k in ("w_mix_out", "w_xq", "w_xo", "w_xkv")]
    mix_flight = _exchange_start(mix_srcs, [_own_slot(a) for a in mix_srcs], g_in, True, "gather_mix_start")
    ffn_srcs = [shards["w_up"], shards["w_down"], conv_w_sh]
    ffn_flight = _exchange_start(ffn_srcs, [_own_slot(a) for a in ffn_srcs], mix_flight[4], True, "gather_ffn_start")
    my_slot = 4 * lax.axis_index("x") + 2 * lax.axis_index("y") + lax.axis_index("c")
    own_block = lambda a: _own_slot(lax.dynamic_index_in_dim(a, my_slot, 0, keepdims=False))
    by_rows = lambda a: a.reshape(NDEV, a.shape[0] // NDEV, a.shape[1])
    by_cols = lambda a: a.reshape(a.shape[0], NDEV, a.shape[1] // NDEV).transpose(1, 0, 2)
    grad_flight = {}

    def mix_weights(after):
        g_mix, g_xq, g_xo, g_xkv = _exchange_wait(*mix_flight[:4], after, True, "gather_mix_wait")
        return (g_mix.reshape(D_MODEL, D_MODEL), g_xq.reshape(D_MODEL, D_MODEL), g_xo.reshape(D_MODEL, D_MODEL), g_xkv)

    def ffn_weights(after):
        g_up, g_down, g_cw = _exchange_wait(*ffn_flight[:4], after, True, "gather_ffn_wait")
        return g_up, g_down.reshape(D_FF, D_MODEL), g_cw.transpose(1, 0, 2).reshape(8, 2 * D_FF)

    def send_ffn_grads(d_w_up, d_w_down, d_cw):
        srcs = [d_w_up, by_rows(d_w_down), by_cols(d_cw)]
        grad_flight["ffn"] = _exchange_start(srcs, [own_block(a) for a in srcs], d_w_up, False, "scatter_ffn_start")
        return grad_flight["ffn"][4]

    def send_mix_grads(d_w_mix, d_w_xq, d_w_xo, d_w_xkv):
        srcs = [by_rows(d_w_mix), by_rows(d_w_xq), by_rows(d_w_xo), d_w_xkv]
        grad_flight["mix"] = _exchange_start(srcs, [own_block(a) for a in srcs], d_w_mix, False, "scatter_mix_start")
        return grad_flight["mix"][4]

    def send_in_grad(d_w_in):
        srcs = [by_rows(d_w_in)]
        grad_flight["in"] = _exchange_start(srcs, [own_block(a) for a in srcs], d_w_in, False, "scatter_in_start")
        return grad_flight["in"][4]

    gains = dict(mix_pre=norm_mix_pre + ffn_flight[4][0, 0], mix_post=norm_mix_post, mem=norm_mem, xa_pre=norm_xa_pre,
                 xa_post=norm_xa_post, ffn_pre=norm_ffn_pre, ffn_post=norm_ffn_post)
    loss, grad_x, small = _local_step(
        x[0], mem[0], loss_target[0], gains, b_forget, w_pool[0], pool_scale[0], conv_b,
        g_in.reshape(D_MODEL, D_IN_PAD), mix_weights, ffn_weights, send_in_grad, send_mix_grads, send_ffn_grads)

    p_up, p_down, p_cw = _exchange_wait(*grad_flight["ffn"][:4], grad_x, False, "scatter_ffn_wait")
    p_mix, p_xq, p_xo, p_xkv = _exchange_wait(*grad_flight["mix"][:4], grad_x, False, "scatter_mix_wait")
    (p_in,) = _exchange_wait(*grad_flight["in"][:4], grad_x, False, "scatter_in_wait")
    parts = [p_in, p_mix, p_xq, p_xo, p_xkv, p_up, p_down, p_cw]
    small_like = dict(mix_pre=norm_mix_pre, mix_post=norm_mix_post, mem=norm_mem, xa_pre=norm_xa_pre,
                      xa_post=norm_xa_post, ffn_pre=norm_ffn_pre, ffn_post=norm_ffn_post, conv_b=conv_b,
                      w_pool=w_pool, pool_scale=pool_scale, b_forget=b_forget)
    small = {k: small[k].reshape(small_like[k].shape) for k in SMALL_ORDER}
    (small_parts,) = _all_gather([_pack_small(small)], "gather_small_grads")

    res = {}
    for k, p in zip(big_names, parts[:7]):
        if k == "w_in":
            p = p[:, :, :w_in.shape[2]]
        res[k] = [a[None] for a in _adamw(p, w[k][0], mo[k][0], vo[k][0], "adamw_" + k)]
    pad_cw = lambda a: jnp.pad(a[0, :, 0, :], ((0, 5), (0, 0)))
    res["conv_w"] = [a[:3][None, :, None, :] for a in
                     _adamw(parts[7], pad_cw(conv_w), pad_cw(m_conv_w), pad_cw(v_conv_w), "adamw_conv_w")]
    key_of = dict(mix_pre="norm_mix_pre", mix_post="norm_mix_post", mem="norm_mem", xa_pre="norm_xa_pre",
                  xa_post="norm_xa_post", ffn_pre="norm_ffn_pre", ffn_post="norm_ffn_post", conv_b="conv_b",
                  w_pool="w_pool", pool_scale="pool_scale", b_forget="b_forget")
    pack_of = lambda src: _pack_small({k: src[key_of[k]] for k in SMALL_ORDER})
    small_out = _adamw(small_parts, pack_of(w), pack_of(mo), pack_of(vo), "adamw_small")
    small_out = [_unpack_small(a, small_like) for a in small_out]
    for k in SMALL_ORDER:
        res[key_of[k]] = [so[k] for so in small_out]

    total = lax.psum(loss[0, 0], ("x", "y", "c"))
    outs = [total, grad_x[None]]
    for idx in range(4):
        outs += [res[k][idx] for k in names]
    return tuple(outs)
```

```python
import functools
import math

import jax
import jax.numpy as jnp
from jax import lax
from jax.experimental import pallas as pl
from jax.experimental.pallas import tpu as pltpu

F32 = jnp.float32
BF16 = jnp.bfloat16

NDEV = 8
D_MODEL = 1024
D_POOL = 256
D_FOX = 768
FOX_HEADS = 12
HEAD_PAIRS = FOX_HEADS // 2
XA_HEADS = 4
XA_DIM = 256
D_FF = 4096
D_IN_PAD = 2688
F_COL = 2560
POOL_HALO = 16
NORM_EPS = 1e-6
NEG = -1e30

ADAM_LR = 0.001
ADAM_B1 = 0.9
ADAM_B2 = 0.999
ADAM_EPS = 1e-08
ADAM_WD = 0.01
ADAM_STEP = 10

TM = 512
TQ = 512
TN_FF = 512
VMEM_LIMIT = 56 * 1024 * 1024
MESH = pl.DeviceIdType.MESH


def _cp(*sem):
    return pltpu.CompilerParams(dimension_semantics=sem, vmem_limit_bytes=VMEM_LIMIT)


def _dot(a, b, dims):
    return lax.dot_general(a, b, (dims, ((), ())), preferred_element_type=F32)


NN = ((1,), (0,))
NT = ((1,), (1,))
TN = ((0,), (0,))


def _mm(a, b, mode, out_dtype, tm, tn, tk, name, b_cols=None, out_cols=None, after=None):
    a_list = list(a) if isinstance(a, (list, tuple)) else [a]
    b_list = list(b) if isinstance(b, (list, tuple)) else [b]
    assert len(a_list) == 1 or len(b_list) == 1
    if mode == "tn":
        K, M = a_list[0].shape
        assert len(a_list) == 1
        Ns = [x.shape[1] for x in b_list]
        N = sum(Ns)
        assert b_cols is None
    else:
        assert len(b_list) == 1
        M = a_list[0].shape[0]
        Ks = [x.shape[1] for x in a_list]
        K = sum(Ks)
        if b_cols is None:
            N = b_list[0].shape[0] if mode == "nt" else b_list[0].shape[1]
        else:
            N = b_list[0].shape[1] if mode == "nt" else NDEV * b_cols
    assert M % tm == 0 and N % tn == 0 and K % tk == 0, (name, M, N, K)
    grid = (M // tm, N // tn, K // tk)
    nk = grid[2]
    dims = {"nn": NN, "nt": NT, "tn": TN}[mode]

    in_specs = []
    if mode == "tn":
        in_specs.append(pl.BlockSpec((tk, tm), lambda i, j, k: (k, i)))
        if len(b_list) == 1:
            in_specs.append(pl.BlockSpec((tk, tn), lambda i, j, k: (k, j)))
        else:
            nj1 = Ns[0] // tn
            in_specs.append(pl.BlockSpec((tk, tn), lambda i, j, k: (k, jnp.minimum(j, nj1 - 1))))
            in_specs.append(pl.BlockSpec((tk, tn), lambda i, j, k: (k, jnp.maximum(j - nj1, 0))))
    else:
        if len(a_list) == 1:
            in_specs.append(pl.BlockSpec((tm, tk), lambda i, j, k: (i, k)))
        else:
            nk1 = Ks[0] // tk
            in_specs.append(pl.BlockSpec((tm, tk), lambda i, j, k: (i, jnp.minimum(k, nk1 - 1))))
            in_specs.append(pl.BlockSpec((tm, tk), lambda i, j, k: (i, jnp.maximum(k - nk1, 0))))
        if b_cols is None:
            if mode == "nn":
                in_specs.append(pl.BlockSpec((tk, tn), lambda i, j, k: (k, j)))
            else:
                in_specs.append(pl.BlockSpec((tn, tk), lambda i, j, k: (j, k)))
        else:
            if mode == "nn":
                per = b_cols // tn
                in_specs.append(pl.BlockSpec((None, tk, tn), lambda i, j, k: (j // per, k, j % per)))
            else:
                per = b_cols // tk
                in_specs.append(pl.BlockSpec((None, tn, tk), lambda i, j, k: (k // per, j, k % per)))
    if out_cols is None:
        out_spec = pl.BlockSpec((tm, tn), lambda i, j, k: (i, j))
        out_shape = jax.ShapeDtypeStruct((M, N), out_dtype)
    else:
        pero = out_cols // tn
        out_spec = pl.BlockSpec((None, tm, tn), lambda i, j, k: (j // pero, i, j % pero))
        out_shape = jax.ShapeDtypeStruct((NDEV, M, out_cols), out_dtype)

    two_a = len(a_list) == 2
    two_b = len(b_list) == 2
    extra = []
    if after is not None:
        in_specs.append(pl.BlockSpec(memory_space=pl.ANY))
        extra.append(after)

    def body(*refs):
        o_ref, acc_ref = refs[-2], refs[-1]
        j = pl.program_id(1)
        k = pl.program_id(2)

        @pl.when(k == 0)
        def _():
            acc_ref[...] = jnp.zeros_like(acc_ref)

        if two_a:
            a1, a2, b1 = refs[0], refs[1], refs[2]
            nk1_ = Ks[0] // tk

            @pl.when(k < nk1_)
            def _():
                acc_ref[...] += _dot(a1[...], b1[...], dims)

            @pl.when(k >= nk1_)
            def _():
                acc_ref[...] += _dot(a2[...], b1[...], dims)
        elif two_b:
            a1, b1, b2 = refs[0], refs[1], refs[2]
            nj1_ = Ns[0] // tn

            @pl.when(j < nj1_)
            def _():
                acc_ref[...] += _dot(a1[...], b1[...], dims)

            @pl.when(j >= nj1_)
            def _():
                acc_ref[...] += _dot(a1[...], b2[...], dims)
        else:
            acc_ref[...] += _dot(refs[0][...], refs[1][...], dims)

        @pl.when(k == nk - 1)
        def _():
            o_ref[...] = acc_ref[...].astype(o_ref.dtype)

    return pl.pallas_call(
        body, name=name, grid=grid, in_specs=in_specs, out_specs=out_spec, out_shape=out_shape,
        scratch_shapes=[pltpu.VMEM((tm, tn), F32)],
        compiler_params=_cp("parallel", "parallel", "arbitrary"),
    )(*a_list, *b_list, *extra)


def _rstd(x):
    return lax.rsqrt(jnp.mean(x * x, axis=-1, keepdims=True) + NORM_EPS)


def _norm_bwd_rows(dxn, xn, r):
    return r * (dxn - xn * jnp.mean(dxn * xn, axis=-1, keepdims=True))


def _row_spec(tm, d):
    return pl.BlockSpec((tm, d), lambda i: (i, 0))


def _vec_spec(d):
    return pl.BlockSpec((1, d), lambda i: (0, 0))


def _mm_rows(a, b, mode, tk, name, rows, vecs, outs, epilogue, b_cols=None, after=None):
    a_list = list(a) if isinstance(a, (list, tuple)) else [a]
    m = a_list[0].shape[0]
    ks = [x.shape[1] for x in a_list]
    n = D_MODEL
    nk = sum(ks) // tk
    dims = NN if mode == "nn" else NT
    if len(a_list) == 1:
        in_specs = [pl.BlockSpec((TM, tk), lambda i, k: (i, k))]
    else:
        nk1 = ks[0] // tk
        in_specs = [pl.BlockSpec((TM, tk), lambda i, k: (i, jnp.minimum(k, nk1 - 1))),
                    pl.BlockSpec((TM, tk), lambda i, k: (i, jnp.maximum(k - nk1, 0)))]
    if mode == "nn":
        in_specs.append(pl.BlockSpec((tk, n), lambda i, k: (k, 0)))
    elif b_cols is None:
        in_specs.append(pl.BlockSpec((n, tk), lambda i, k: (0, k)))
    else:
        per = b_cols // tk
        in_specs.append(pl.BlockSpec((None, n, tk), lambda i, k: (k // per, 0, k % per)))
    in_specs += [pl.BlockSpec((TM, n), lambda i, k: (i, 0))] * len(rows)
    in_specs += [pl.BlockSpec((1, n), lambda i, k: (0, 0))] * len(vecs)
    extra = []
    if after is not None:
        in_specs.append(pl.BlockSpec(memory_space=pl.ANY))
        extra.append(after)
    out_specs, out_shape = [], []
    for o in outs:
        if o == "sum":
            out_specs.append(pl.BlockSpec((1, n), lambda i, k: (0, 0)))
            out_shape.append(jax.ShapeDtypeStruct((1, n), F32))
        else:
            out_specs.append(pl.BlockSpec((TM, n), lambda i, k: (i, 0)))
            out_shape.append(jax.ShapeDtypeStruct((m, n), o))
    na, nr, nv = len(a_list), len(rows), len(vecs)

    def body(*refs):
        a_refs, b_ref = refs[:na], refs[na]
        row_refs = refs[na + 1:na + 1 + nr]
        vec_refs = refs[na + 1 + nr:na + 1 + nr + nv]
        out_refs = refs[len(refs) - 1 - len(outs):len(refs) - 1]
        acc_ref = refs[-1]
        i, k = pl.program_id(0), pl.program_id(1)

        @pl.when(k == 0)
        def _():
            acc_ref[...] = jnp.zeros_like(acc_ref)

        if na == 1:
            acc_ref[...] += _dot(a_refs[0][...], b_ref[...], dims)
        else:
            nk1_ = ks[0] // tk

            @pl.when(k < nk1_)
            def _():
                acc_ref[...] += _dot(a_refs[0][...], b_ref[...], dims)

            @pl.when(k >= nk1_)
            def _():
                acc_ref[...] += _dot(a_refs[1][...], b_ref[...], dims)

        @pl.when(k == nk - 1)
        def _():
            vals = epilogue(acc_ref[...], [r[...] for r in row_refs], [v[...] for v in vec_refs])
            for o, ref, val in zip(outs, out_refs, vals):
                if o == "sum":
                    @pl.when(i == 0)
                    def _():
                        ref[...] = val

                    @pl.when(i > 0)
                    def _():
                        ref[...] += val
                else:
                    ref[...] = val.astype(o)

    return pl.pallas_call(
        body, name=name, grid=(m // TM, nk), in_specs=in_specs, out_specs=out_specs, out_shape=out_shape,
        scratch_shapes=[pltpu.VMEM((TM, n), F32)],
        compiler_params=_cp("arbitrary", "arbitrary"),
    )(*a_list, b, *rows, *vecs, *extra)


def _epi_resid(y, rows, vecs):
    (x_in,), (g_post, g_next) = rows, vecs
    xo = x_in + y * _rstd(y) * g_post
    return y, xo, xo * _rstd(xo) * g_next


def _epi_norm_bwd(dh, rows, vecs):
    x, dx_res = rows[0], rows[1]
    r = _rstd(x)
    xn = x * r
    dx = dx_res + _norm_bwd_rows(dh * vecs[0], xn, r)
    res = [dx, jnp.sum(dh * xn, axis=0, keepdims=True)]
    if len(rows) == 3:
        y = rows[2]
        r2 = _rstd(y)
        yn = y * r2
        res += [_norm_bwd_rows(dx * vecs[1], yn, r2), jnp.sum(dx * yn, axis=0, keepdims=True)]
    return res


def _norm_fwd(x, g, name):
    s, d = x.shape
    tm = min(TM, s)

    def body(x_ref, g_ref, h_ref):
        xv = x_ref[...]
        h_ref[...] = (xv * _rstd(xv) * g_ref[...]).astype(BF16)

    return pl.pallas_call(
        body, name=name, grid=(s // tm,), in_specs=[_row_spec(tm, d), _vec_spec(d)],
        out_specs=_row_spec(tm, d), out_shape=jax.ShapeDtypeStruct((s, d), BF16),
        compiler_params=_cp("parallel"),
    )(x, g)


def _resid_norm_fwd(x_in, y, g_post, g_next, name):
    s, d = x_in.shape

    def body(x_ref, y_ref, gp_ref, gn_ref, xo_ref, h_ref):
        yv = y_ref[...]
        xo = x_ref[...] + yv * _rstd(yv) * gp_ref[...]
        xo_ref[...] = xo
        h_ref[...] = (xo * _rstd(xo) * gn_ref[...]).astype(BF16)

    return pl.pallas_call(
        body, name=name, grid=(s // TM,),
        in_specs=[_row_spec(TM, d), _row_spec(TM, d), _vec_spec(d), _vec_spec(d)],
        out_specs=[_row_spec(TM, d), _row_spec(TM, d)],
        out_shape=[jax.ShapeDtypeStruct((s, d), F32), jax.ShapeDtypeStruct((s, d), BF16)],
        compiler_params=_cp("parallel"),
    )(x_in, y, g_post, g_next)


def _norm_bwd(dh, x, dx_res, g_pre, name, prev=None):
    s, d = x.shape
    tm = min(TM, s)
    has_prev = prev is not None

    def body(*refs):
        if has_prev:
            dh_ref, x_ref, dr_ref, g_ref, y_ref, gp_ref, dx_ref, dg_ref, dy_ref, dgp_ref = refs
        else:
            dh_ref, x_ref, dr_ref, g_ref, dx_ref, dg_ref = refs
        i = pl.program_id(0)
        xv = x_ref[...]
        r = _rstd(xv)
        xn = xv * r
        dhv = dh_ref[...].astype(F32)
        dx = dr_ref[...] + _norm_bwd_rows(dhv * g_ref[...], xn, r)
        dx_ref[...] = dx
        dg = jnp.sum(dhv * xn, axis=0, keepdims=True)

        @pl.when(i == 0)
        def _():
            dg_ref[...] = dg

        @pl.when(i > 0)
        def _():
            dg_ref[...] += dg

        if has_prev:
            yv = y_ref[...]
            r2 = _rstd(yv)
            yn = yv * r2
            dy_ref[...] = _norm_bwd_rows(dx * gp_ref[...], yn, r2).astype(BF16)
            dgp = jnp.sum(dx * yn, axis=0, keepdims=True)

            @pl.when(i == 0)
            def _():
                dgp_ref[...] = dgp

            @pl.when(i > 0)
            def _():
                dgp_ref[...] += dgp

    in_specs = [_row_spec(tm, d), _row_spec(tm, d), _row_spec(tm, d), _vec_spec(d)]
    out_specs = [_row_spec(tm, d), _vec_spec(d)]
    out_shape = [jax.ShapeDtypeStruct((s, d), F32), jax.ShapeDtypeStruct((1, d), F32)]
    args = [dh, x, dx_res, g_pre]
    if has_prev:
        in_specs += [_row_spec(tm, d), _vec_spec(d)]
        out_specs += [_row_spec(tm, d), _vec_spec(d)]
        out_shape += [jax.ShapeDtypeStruct((s, d), BF16), jax.ShapeDtypeStruct((1, d), F32)]
        args += list(prev)
    return pl.pallas_call(
        body, name=name, grid=(s // tm,), in_specs=in_specs, out_specs=out_specs, out_shape=out_shape,
        compiler_params=_cp("arbitrary"),
    )(*args)


def _loss_bwd(x2, y3, g_post, tgt, name):
    s, d = x2.shape

    def body(x_ref, y_ref, g_ref, t_ref, loss_ref, dx_ref, dy_ref, dg_ref):
        i = pl.program_id(0)
        yv = y_ref[...]
        r = _rstd(yv)
        yn = yv * r
        e = x_ref[...] + yn * g_ref[...] - t_ref[...]
        part = 0.5 * jnp.sum(jnp.mean(e * e, axis=-1, keepdims=True), axis=0, keepdims=True)
        dx = e * (1.0 / d)
        dx_ref[...] = dx
        dy_ref[...] = _norm_bwd_rows(dx * g_ref[...], yn, r).astype(BF16)
        dg = jnp.sum(dx * yn, axis=0, keepdims=True)
        part = jnp.broadcast_to(part, (1, 128))

        @pl.when(i == 0)
        def _():
            dg_ref[...] = dg
            loss_ref[...] = part

        @pl.when(i > 0)
        def _():
            dg_ref[...] += dg
            loss_ref[...] += part

    return pl.pallas_call(
        body, name=name, grid=(s // TM,),
        in_specs=[_row_spec(TM, d), _row_spec(TM, d), _vec_spec(d), _row_spec(TM, d)],
        out_specs=[_vec_spec(128), _row_spec(TM, d), _row_spec(TM, d), _vec_spec(d)],
        out_shape=[jax.ShapeDtypeStruct((1, 128), F32), jax.ShapeDtypeStruct((s, d), F32),
                   jax.ShapeDtypeStruct((s, d), BF16), jax.ShapeDtypeStruct((1, d), F32)],
        compiler_params=_cp("arbitrary"),
    )(x2, y3, g_post, tgt)


def _split3(v):
    hi = v.astype(BF16)
    r1 = v - hi.astype(F32)
    mid = r1.astype(BF16)
    lo = (r1 - mid.astype(F32)).astype(BF16)
    return hi, mid, lo


def _tri_dot(tri, v):
    hi, mid, lo = _split3(v)
    return _dot(tri, hi, NN) + _dot(tri, mid, NN) + _dot(tri, lo, NN)


def _gate_cumsum(fraw, b_pad, name):
    s = fraw.shape[0]

    def body(f_ref, b_ref, flog_ref, cum_ref, carry_ref):
        i = pl.program_id(0)

        @pl.when(i == 0)
        def _():
            carry_ref[...] = jnp.zeros_like(carry_ref)

        flog = f_ref[...] + b_ref[...]
        flog_ref[...] = flog
        lf = jnp.minimum(flog, 0.0) - jnp.log(1.0 + jnp.exp(-jnp.abs(flog)))
        lane = lax.broadcasted_iota(jnp.int32, (1, 128), 1)
        lf = jnp.where(lane < FOX_HEADS, lf, 0.0)
        row = lax.broadcasted_iota(jnp.int32, (TM, TM), 0)
        col = lax.broadcasted_iota(jnp.int32, (TM, TM), 1)
        tri = (row >= col).astype(BF16)
        cum = _tri_dot(tri, lf) + carry_ref[...]
        cum_ref[...] = cum
        carry_ref[...] = cum[TM - 1:TM, :]

    return pl.pallas_call(
        body, name=name, grid=(s // TM,),
        in_specs=[_row_spec(TM, 128), _vec_spec(128)],
        out_specs=[_row_spec(TM, 128), _row_spec(TM, 128)],
        out_shape=[jax.ShapeDtypeStruct((s, 128), F32), jax.ShapeDtypeStruct((s, 128), F32)],
        scratch_shapes=[pltpu.VMEM((1, 128), F32)],
        compiler_params=_cp("arbitrary"),
    )(fraw, b_pad)


def _gate_bwd(qaux, kaux, flog, name):
    s = flog.shape[0]
    n = s // TM

    def body(qa_ref, ka_ref, fl_ref, dp_ref, db_ref, carry_ref):
        i = pl.program_id(0)

        @pl.when(i == 0)
        def _():
            carry_ref[...] = jnp.zeros_like(carry_ref)

        src = lax.broadcasted_iota(jnp.int32, (128, 128), 0)
        dst = lax.broadcasted_iota(jnp.int32, (128, 128), 1)
        dcum = jnp.zeros((TM, 128), F32)
        for p in range(HEAD_PAIRS):
            for ref, l0, l1, sign in ((qa_ref, 64, 0, 1.0), (ka_ref, 67, 3, -1.0)):
                hit = jnp.logical_or(jnp.logical_and(src == l0, dst == 2 * p),
                                     jnp.logical_and(src == l1, dst == 2 * p + 1))
                sel = jnp.where(hit, sign, 0.0).astype(BF16)
                for piece in _split3(ref[p]):
                    dcum = dcum + _dot(piece, sel, NN)
        row = lax.broadcasted_iota(jnp.int32, (TM, TM), 0)
        col = lax.broadcasted_iota(jnp.int32, (TM, TM), 1)
        tri = (row <= col).astype(BF16)
        dlf = _tri_dot(tri, dcum) + carry_ref[...]
        carry_ref[...] = dlf[0:1, :]
        lane = lax.broadcasted_iota(jnp.int32, (1, 128), 1)
        df = jnp.where(lane < FOX_HEADS, dlf / (1.0 + jnp.exp(fl_ref[...])), 0.0)
        dp_ref[...] = df.astype(BF16)
        db = jnp.sum(df, axis=0, keepdims=True)

        @pl.when(i == 0)
        def _():
            db_ref[...] = db

        @pl.when(i > 0)
        def _():
            db_ref[...] += db

    rev = lambda i: (n - 1 - i, 0)
    return pl.pallas_call(
        body, name=name, grid=(n,),
        in_specs=[pl.BlockSpec((HEAD_PAIRS, TM, 128), lambda i: (0, n - 1 - i, 0)),
                  pl.BlockSpec((HEAD_PAIRS, TM, 128), lambda i: (0, n - 1 - i, 0)), pl.BlockSpec((TM, 128), rev)],
        out_specs=[pl.BlockSpec((TM, 128), rev), _vec_spec(128)],
        out_shape=[jax.ShapeDtypeStruct((s, 128), BF16), jax.ShapeDtypeStruct((1, 128), F32)],
        scratch_shapes=[pltpu.VMEM((1, 128), F32)],
        compiler_params=_cp("arbitrary"),
    )(qaux, kaux, flog)


def _pool_consts(i, rows):
    lane = lax.broadcasted_iota(jnp.int32, (rows, D_POOL), 1)
    t1 = lax.broadcasted_iota(jnp.int32, (rows, D_POOL), 0) + i * TM + 1
    win = jnp.where(lane < 64, 2, jnp.where(lane < 128, 4, jnp.where(lane < 192, 8, 16)))
    inv = 1.0 / jnp.minimum(t1, win).astype(F32)
    return lane, inv


def _by_group(lane, s2, s4, s8, s16):
    return jnp.where(lane < 64, s2, jnp.where(lane < 128, s4, jnp.where(lane < 192, s8, s16)))


def _pool_diff(i, u_ref, halo_ref):
    u = u_ref[...].astype(F32)
    halo = jnp.where(i > 0, halo_ref[...].astype(F32), 0.0)
    ext = jnp.concatenate([halo, u], axis=0)
    s2 = ext + pltpu.roll(ext, 1, 0)
    s4 = s2 + pltpu.roll(s2, 2, 0)
    s8 = s4 + pltpu.roll(s4, 4, 0)
    s16 = s8 + pltpu.roll(s8, 8, 0)
    lane, inv = _pool_consts(i, TM)
    sel = _by_group(lane, s2[POOL_HALO:], s4[POOL_HALO:], s8[POOL_HALO:], s16[POOL_HALO:])
    return sel * inv - u


def _pool_fwd(proj, wbd, scale, ycat, name):
    s = proj.shape[0]
    hb = TM // POOL_HALO

    def body(u_ref, halo_ref, w_ref, sc_ref, y_any, y_ref):
        del y_any
        i = pl.program_id(0)
        diff = _pool_diff(i, u_ref, halo_ref)
        mixed = _dot(diff.astype(BF16), w_ref[...], NN)
        y_ref[...] = (mixed * sc_ref[...]).astype(BF16)

    return pl.pallas_call(
        body, name=name, grid=(s // TM,),
        in_specs=[pl.BlockSpec((TM, D_POOL), lambda i: (i, 0)),
                  pl.BlockSpec((POOL_HALO, D_POOL), lambda i: (jnp.maximum(i * hb - 1, 0), 0)),
                  pl.BlockSpec((D_POOL, D_POOL), lambda i: (0, 0)), _vec_spec(D_POOL),
                  pl.BlockSpec(memory_space=pl.ANY)],
        out_specs=pl.BlockSpec((TM, D_POOL), lambda i: (i, 0)),
        out_shape=jax.ShapeDtypeStruct(ycat.shape, ycat.dtype),
        input_output_aliases={4: 0},
        compiler_params=_cp("parallel"),
    )(proj, proj, wbd, scale, ycat)


def _pool_bwd(proj, dycat, wbd, scale, name):
    s = proj.shape[0]
    n = s // TM
    hb = TM // POOL_HALO
    last_halo = s // POOL_HALO - 1

    def body(u_ref, halo_ref, dy_ref, dyp_ref, w_ref, sc_ref, dp_ref, dw_ref, dsc_ref):
        i = pl.program_id(0)
        diff = _pool_diff(i, u_ref, halo_ref)
        diff_b = diff.astype(BF16)
        mixed = _dot(diff_b, w_ref[...], NN)
        dy = dy_ref[...].astype(F32)
        dmix = (dy * sc_ref[...]).astype(BF16)
        dyp = jnp.where(i < n - 1, dyp_ref[...].astype(F32), 0.0)
        dmix_p = (dyp * sc_ref[...]).astype(BF16)
        dd = _dot(dmix, w_ref[...], NT)
        dd_p = _dot(dmix_p, w_ref[...], NT)
        lane, inv = _pool_consts(i, TM)
        _, inv_p = _pool_consts(i + 1, POOL_HALO)
        ext = jnp.concatenate([dd * inv, dd_p * inv_p], axis=0)
        rows = TM + POOL_HALO
        l2 = ext + pltpu.roll(ext, rows - 1, 0)
        l4 = l2 + pltpu.roll(l2, rows - 2, 0)
        l8 = l4 + pltpu.roll(l4, rows - 4, 0)
        l16 = l8 + pltpu.roll(l8, rows - 8, 0)
        du = _by_group(lane, l2[:TM], l4[:TM], l8[:TM], l16[:TM]) - dd
        dp_ref[...] = du.astype(BF16)
        dw = _dot(diff_b, dmix, TN)
        dsc = jnp.sum(dy * mixed, axis=0, keepdims=True)

        @pl.when(i == 0)
        def _():
            dw_ref[...] = dw
            dsc_ref[...] = dsc

        @pl.when(i > 0)
        def _():
            dw_ref[...] += dw
            dsc_ref[...] += dsc

    return pl.pallas_call(
        body, name=name, grid=(n,),
        in_specs=[pl.BlockSpec((TM, D_POOL), lambda i: (i, 0)),
                  pl.BlockSpec((POOL_HALO, D_POOL), lambda i: (jnp.maximum(i * hb - 1, 0), 0)),
                  pl.BlockSpec((TM, D_POOL), lambda i: (i, 0)),
                  pl.BlockSpec((POOL_HALO, D_POOL), lambda i: (jnp.minimum((i + 1) * hb, last_halo), 0)),
                  pl.BlockSpec((D_POOL, D_POOL), lambda i: (0, 0)), _vec_spec(D_POOL)],
        out_specs=[pl.BlockSpec((TM, D_POOL), lambda i: (i, 0)),
                   pl.BlockSpec((D_POOL, D_POOL), lambda i: (0, 0)), _vec_spec(D_POOL)],
        out_shape=[jax.ShapeDtypeStruct((s, D_POOL), BF16),
                   jax.ShapeDtypeStruct((D_POOL, D_POOL), F32), jax.ShapeDtypeStruct((1, D_POOL), F32)],
        compiler_params=_cp("arbitrary"),
    )(proj, proj, dycat, dycat, wbd, scale)


Q_BLK = D_POOL // 128
K_BLK = Q_BLK + D_FOX // 128
V_BLK = K_BLK + D_FOX // 128


def _operand_lanes(v0, v1, ones_off):
    lane = lax.broadcasted_iota(jnp.int32, (1, 128), 1)
    out = jnp.zeros((v0.shape[0], 128), F32)
    if ones_off is not None:
        half = lane & 63
        out = out + jnp.where(jnp.logical_and(half >= ones_off, half < ones_off + 3), 1.0, 0.0)
    for base, v in ((64, v0), (0, v1)):
        for j, piece in enumerate(_split3(v)):
            out = jnp.where(lane == base + j, piece.astype(F32), out)
    return out.astype(BF16)


def _fox_operands(cum, name):
    s = cum.shape[0]
    width = HEAD_PAIRS * 128

    def body(c_ref, aq_ref, ak_ref):
        pieces = _split3(c_ref[...])
        row = lax.broadcasted_iota(jnp.int32, (128, width), 0)
        col = lax.broadcasted_iota(jnp.int32, (128, width), 1)
        base = (row >> 1) * 128 + (1 - (row & 1)) * 64
        half = lax.broadcasted_iota(jnp.int32, (1, width), 1) & 63
        for o_ref, off, sign, ones_off in ((aq_ref, 0, 1.0, 3), (ak_ref, 3, -1.0, 0)):
            out = jnp.where(jnp.logical_and(half >= ones_off, half < ones_off + 3), 1.0, 0.0)
            for j, piece in enumerate(pieces):
                sel = jnp.where(jnp.logical_and(col == base + off + j, row < FOX_HEADS), sign, 0.0).astype(BF16)
                out = out + _dot(piece, sel, NN)
            o_ref[...] = out.astype(BF16)

    return pl.pallas_call(
        body, name=name, grid=(s // TM,), in_specs=[_row_spec(TM, 128)],
        out_specs=[_row_spec(TM, width), _row_spec(TM, width)],
        out_shape=[jax.ShapeDtypeStruct((s, width), BF16)] * 2,
        compiler_params=_cp("parallel"),
    )(cum)


def _fox_do_operand(dycat, ycat, after, name):
    s = dycat.shape[0]

    def body(do_ref, o_ref, after_ref, ad_ref):
        del after_ref
        lane = lax.broadcasted_iota(jnp.int32, (1, 128), 1)
        dd = do_ref[...].astype(F32) * o_ref[...].astype(F32)
        d0 = jnp.sum(jnp.where(lane < 64, dd, 0.0), axis=1, keepdims=True)
        d1 = jnp.sum(dd, axis=1, keepdims=True) - d0
        ad_ref[...] = _operand_lanes(-d0, -d1, None)

    blk = pl.BlockSpec((TM, 128), lambda i, p: (i, Q_BLK + p))
    return pl.pallas_call(
        body, name=name, grid=(s // TM, HEAD_PAIRS), in_specs=[blk, blk, pl.BlockSpec(memory_space=pl.ANY)],
        out_specs=pl.BlockSpec((TM, 128), lambda i, p: (i, p)),
        out_shape=jax.ShapeDtypeStruct((s, HEAD_PAIRS * 128), BF16),
        compiler_params=_cp("parallel", "parallel"),
    )(dycat, ycat, after)


def _causal_pairs(nq, key_major):
    if key_major:
        pairs = [(q, k) for k in range(nq) for q in range(k, nq)]
    else:
        pairs = [(q, k) for q in range(nq) for k in range(q + 1)]
    return (jnp.asarray([p[0] for p in pairs], jnp.int32), jnp.asarray([p[1] for p in pairs], jnp.int32))


def _diag_mask(sc):
    row = lax.broadcasted_iota(jnp.int32, sc.shape, 0)
    col = lax.broadcasted_iota(jnp.int32, sc.shape, 1)
    return jnp.where(row >= col, sc, NEG)


def _fox_fwd(proj, aq, ak, name):
    s = proj.shape[0]
    nq = s // TQ
    qi_arr, ki_arr = _causal_pairs(nq, key_major=False)

    def body(qi_ref, ki_ref, q_ref, k_ref, v_ref, aq_ref, ak_ref, o_ref, aqb_ref, m0_ref, m1_ref, acc_ref, aux_ref):
        t = pl.program_id(1)
        qi, ki = qi_ref[t], ki_ref[t]
        lane = lax.broadcasted_iota(jnp.int32, (1, 128), 1)
        masks = [lane < 64, lane >= 64]
        ones_v = jnp.where((lane & 63) == 8, 1.0, 0.0).astype(BF16)
        m_ref = [m0_ref, m1_ref]

        @pl.when(ki == 0)
        def _():
            m0_ref[...] = jnp.full_like(m0_ref, NEG)
            m1_ref[...] = jnp.full_like(m1_ref, NEG)
            acc_ref[...] = jnp.zeros_like(acc_ref)
            aux_ref[...] = jnp.zeros_like(aux_ref)

        def step(diag):
            q2s = q_ref[...] * 0.125
            k2, v2, aq2, ak2 = k_ref[...], v_ref[...], aq_ref[...], ak_ref[...]
            pv, alpha = [], []
            for hh in range(2):
                qh = jnp.where(masks[hh], q2s, aq2)
                kh = jnp.where(masks[hh], k2, ak2)
                vh = jnp.where(masks[hh], v2, ones_v)
                sc = _dot(qh, kh, NT)
                if diag:
                    sc = _diag_mask(sc)
                m_prev = m_ref[hh][...]
                m_new = jnp.maximum(m_prev, jnp.max(sc, axis=1, keepdims=True))
                m_ref[hh][...] = m_new
                alpha.append(jnp.exp(m_prev - m_new))
                pv.append(_dot(jnp.exp(sc - m_new).astype(BF16), vh, NN))
            acc_ref[...] = acc_ref[...] * jnp.where(masks[0], alpha[0], alpha[1]) + jnp.where(masks[0], pv[0], pv[1])
            aux_ref[...] = aux_ref[...] * jnp.where(masks[0], alpha[1], alpha[0]) + jnp.where(masks[0], pv[1], pv[0])

        @pl.when(ki < qi)
        def _():
            step(False)

        @pl.when(ki == qi)
        def _():
            step(True)
            aux = aux_ref[...]
            l0, l1 = aux[:, 72:73], aux[:, 8:9]
            o_ref[...] = (acc_ref[...] * jnp.where(masks[0], 1.0 / l0, 1.0 / l1)).astype(BF16)
            aqf = aq_ref[...].astype(F32)
            cum0 = aqf[:, 64:65] + aqf[:, 65:66] + aqf[:, 66:67]
            cum1 = aqf[:, 0:1] + aqf[:, 1:2] + aqf[:, 2:3]
            aqb_ref[...] = _operand_lanes(cum0 - (m0_ref[...] + jnp.log(l0)), cum1 - (m1_ref[...] + jnp.log(l1)), 3)

    grid_spec = pltpu.PrefetchScalarGridSpec(
        num_scalar_prefetch=2, grid=(HEAD_PAIRS, int(qi_arr.shape[0])),
        in_specs=[pl.BlockSpec((TQ, 128), lambda p, t, qi, ki: (qi[t], Q_BLK + p)),
                  pl.BlockSpec((TQ, 128), lambda p, t, qi, ki: (ki[t], K_BLK + p)),
                  pl.BlockSpec((TQ, 128), lambda p, t, qi, ki: (ki[t], V_BLK + p)),
                  pl.BlockSpec((TQ, 128), lambda p, t, qi, ki: (qi[t], p)),
                  pl.BlockSpec((TQ, 128), lambda p, t, qi, ki: (ki[t], p))],
        out_specs=[pl.BlockSpec((TQ, 128), lambda p, t, qi, ki: (qi[t], Q_BLK + p)),
                   pl.BlockSpec((TQ, 128), lambda p, t, qi, ki: (qi[t], p))],
        scratch_shapes=[pltpu.VMEM((TQ, 1), F32), pltpu.VMEM((TQ, 1), F32),
                        pltpu.VMEM((TQ, 128), F32), pltpu.VMEM((TQ, 128), F32)])
    return pl.pallas_call(
        body, name=name, grid_spec=grid_spec,
        out_shape=[jax.ShapeDtypeStruct((s, D_MODEL), BF16), jax.ShapeDtypeStruct((s, HEAD_PAIRS * 128), BF16)],
        compiler_params=_cp("parallel", "arbitrary"),
    )(qi_arr, ki_arr, proj, proj, proj, aq, ak)


def _fox_bwd(proj, dycat, aqb, ak, ad, name):
    s = proj.shape[0]
    nq = s // TQ
    qi_arr, ki_arr = _causal_pairs(nq, key_major=True)

    def body(qi_ref, ki_ref, q_ref, k_ref, v_ref, do_ref, aq_ref, ak_ref, ad_ref,
             dq_ref, dk_ref, dv_ref, qaux_ref, kaux_ref, dq_acc, qaux_acc, dk_acc, dv_acc, kaux_acc):
        t = pl.program_id(1)
        qi, ki = qi_ref[t], ki_ref[t]
        lane = lax.broadcasted_iota(jnp.int32, (1, 128), 1)
        masks = [lane < 64, lane >= 64]
        ones_v = jnp.where((lane & 63) < 3, 1.0, 0.0).astype(BF16)

        @pl.when(qi == ki)
        def _():
            dk_acc[...] = jnp.zeros_like(dk_acc)
            dv_acc[...] = jnp.zeros_like(dv_acc)
            kaux_acc[...] = jnp.zeros_like(kaux_acc)

        def step(diag):
            q2s = q_ref[...] * 0.125
            k2, v2, do2 = k_ref[...], v_ref[...], do_ref[...]
            aq2, ak2, ad2 = aq_ref[...], ak_ref[...], ad_ref[...]
            dq, dk, dv = [], [], []
            for hh in range(2):
                qh = jnp.where(masks[hh], q2s, aq2)
                kh = jnp.where(masks[hh], k2, ak2)
                doh = jnp.where(masks[hh], do2, ad2)
                vh = jnp.where(masks[hh], v2, ones_v)
                sc = _dot(qh, kh, NT)
                if diag:
                    sc = _diag_mask(sc)
                p = jnp.exp(sc)
                dsb = (p * _dot(doh, vh, NT)).astype(BF16)
                dv.append(_dot(p.astype(BF16), doh, TN))
                dk.append(_dot(dsb, qh, TN))
                dq.append(_dot(dsb, kh, NN))
            dk_acc[...] += jnp.where(masks[0], dk[0], dk[1])
            kaux_acc[...] += jnp.where(masks[0], dk[1], dk[0])
            dv_acc[...] += jnp.where(masks[0], dv[0], dv[1])
            dq_new = jnp.where(masks[0], dq[0], dq[1])
            qaux_new = jnp.where(masks[0], dq[1], dq[0])

            @pl.when(ki == 0)
            def _():
                dq_acc[qi] = dq_new
                qaux_acc[qi] = qaux_new

            @pl.when(ki > 0)
            def _():
                dq_acc[qi] += dq_new
                qaux_acc[qi] += qaux_new

        @pl.when(qi > ki)
        def _():
            step(False)

        @pl.when(qi == ki)
        def _():
            step(True)
            rows = pl.ds(pl.multiple_of(qi * TQ, TQ), TQ)
            dq_ref[rows, :] = (dq_acc[qi] * 0.125).astype(BF16)
            qaux_ref[rows, :] = qaux_acc[qi]

        @pl.when(qi == nq - 1)
        def _():
            dk_ref[...] = dk_acc[...].astype(BF16)
            dv_ref[...] = dv_acc[...].astype(BF16)
            kaux_ref[...] = kaux_acc[...]

    grid_spec = pltpu.PrefetchScalarGridSpec(
        num_scalar_prefetch=2, grid=(HEAD_PAIRS, int(qi_arr.shape[0])),
        in_specs=[pl.BlockSpec((TQ, 128), lambda p, t, qi, ki: (qi[t], Q_BLK + p)),
                  pl.BlockSpec((TQ, 128), lambda p, t, qi, ki: (ki[t], K_BLK + p)),
                  pl.BlockSpec((TQ, 128), lambda p, t, qi, ki: (ki[t], V_BLK + p)),
                  pl.BlockSpec((TQ, 128), lambda p, t, qi, ki: (qi[t], Q_BLK + p)),
                  pl.BlockSpec((TQ, 128), lambda p, t, qi, ki: (qi[t], p)),
                  pl.BlockSpec((TQ, 128), lambda p, t, qi, ki: (ki[t], p)),
                  pl.BlockSpec((TQ, 128), lambda p, t, qi, ki: (qi[t], p))],
        out_specs=[pl.BlockSpec((s, 128), lambda p, t, qi, ki: (0, p)),
                   pl.BlockSpec((TQ, 128), lambda p, t, qi, ki: (ki[t], p)),
                   pl.BlockSpec((TQ, 128), lambda p, t, qi, ki: (ki[t], p)),
                   pl.BlockSpec((None, s, 128), lambda p, t, qi, ki: (p, 0, 0)),
                   pl.BlockSpec((None, TQ, 128), lambda p, t, qi, ki: (p, ki[t], 0))],
        scratch_shapes=[pltpu.VMEM((nq, TQ, 128), F32), pltpu.VMEM((nq, TQ, 128), F32),
                        pltpu.VMEM((TQ, 128), F32), pltpu.VMEM((TQ, 128), F32), pltpu.VMEM((TQ, 128), F32)])
    return pl.pallas_call(
        body, name=name, grid_spec=grid_spec,
        out_shape=[jax.ShapeDtypeStruct((s, D_FOX), BF16)] * 3 + [jax.ShapeDtypeStruct((HEAD_PAIRS, s, 128), F32)] * 2,
        compiler_params=_cp("arbitrary", "arbitrary"),
    )(qi_arr, ki_arr, proj, proj, proj, dycat, aqb, ak, ad)


XA_SCALE = XA_DIM ** -0.5


def _xattn_fwd(q2, kv, name):
    s = q2.shape[0]
    m = kv.shape[0]

    def body(q_ref, kv_ref, o_ref):
        for h in range(XA_HEADS):
            c0 = h * XA_DIM
            sc = _dot(q_ref[:, c0:c0 + XA_DIM], kv_ref[:, c0:c0 + XA_DIM], NT) * XA_SCALE
            e = jnp.exp(sc - jnp.max(sc, axis=1, keepdims=True))
            p = e / jnp.sum(e, axis=1, keepdims=True)
            o_ref[:, c0:c0 + XA_DIM] = _dot(p.astype(BF16), kv_ref[:, D_MODEL + c0:D_MODEL + c0 + XA_DIM], NN).astype(BF16)

    return pl.pallas_call(
        body, name=name, grid=(s // TM,),
        in_specs=[_row_spec(TM, D_MODEL), pl.BlockSpec((m, 2 * D_MODEL), lambda i: (0, 0))],
        out_specs=_row_spec(TM, D_MODEL), out_shape=jax.ShapeDtypeStruct((s, D_MODEL), BF16),
        compiler_params=_cp("parallel"),
    )(q2, kv)


def _xattn_bwd(q2, kv, do2, name):
    s = q2.shape[0]
    m = kv.shape[0]

    def body(q_ref, kv_ref, do_ref, dq_ref, dkv_ref):
        i = pl.program_id(0)

        @pl.when(i == 0)
        def _():
            dkv_ref[...] = jnp.zeros_like(dkv_ref)

        for h in range(XA_HEADS):
            c0 = h * XA_DIM
            v0 = D_MODEL + c0
            qh = q_ref[:, c0:c0 + XA_DIM]
            kh = kv_ref[:, c0:c0 + XA_DIM]
            doh = do_ref[:, c0:c0 + XA_DIM]
            sc = _dot(qh, kh, NT) * XA_SCALE
            e = jnp.exp(sc - jnp.max(sc, axis=1, keepdims=True))
            p = e / jnp.sum(e, axis=1, keepdims=True)
            dp = _dot(doh, kv_ref[:, v0:v0 + XA_DIM], NT)
            ds = p * (dp - jnp.sum(p * dp, axis=1, keepdims=True))
            dsb = (ds * XA_SCALE).astype(BF16)
            dq_ref[:, c0:c0 + XA_DIM] = _dot(dsb, kh, NN).astype(BF16)
            dkv_ref[:, c0:c0 + XA_DIM] += _dot(dsb, qh, TN)
            dkv_ref[:, v0:v0 + XA_DIM] += _dot(p.astype(BF16), doh, TN)

    return pl.pallas_call(
        body, name=name, grid=(s // TM,),
        in_specs=[_row_spec(TM, D_MODEL), pl.BlockSpec((m, 2 * D_MODEL), lambda i: (0, 0)), _row_spec(TM, D_MODEL)],
        out_specs=[_row_spec(TM, D_MODEL), pl.BlockSpec((m, 2 * D_MODEL), lambda i: (0, 0))],
        out_shape=[jax.ShapeDtypeStruct((s, D_MODEL), BF16), jax.ShapeDtypeStruct((m, 2 * D_MODEL), F32)],
        compiler_params=_cp("arbitrary"),
    )(q2, kv, do2)


GELU_C = math.sqrt(2.0 / math.pi)
GELU_A = 0.044715


def _gelu(x):
    return 0.5 * x * (1.0 + jnp.tanh(GELU_C * (x + GELU_A * x * x * x)))


def _gelu_and_grad(x):
    t = jnp.tanh(GELU_C * (x + GELU_A * x * x * x))
    g = 0.5 * x * (1.0 + t)
    dg = 0.5 * (1.0 + t) + 0.5 * x * (1.0 - t * t) * GELU_C * (1.0 + 3.0 * GELU_A * x * x)
    return g, dg


def _shift_down(main, prev8):
    n = main.shape[0]
    if n <= 8:
        row = lax.broadcasted_iota(jnp.int32, main.shape, 0)
        s1 = jnp.where(row == 0, prev8[7:8, :], pltpu.roll(main, 1, 0))
        s2 = jnp.where(row == 0, prev8[6:7, :], jnp.where(row == 1, prev8[7:8, :], pltpu.roll(main, 2, 0)))
        return s1, s2
    row = lax.broadcasted_iota(jnp.int32, (8, main.shape[1]), 0)
    r1, r2 = pltpu.roll(main, 1, 0), pltpu.roll(main, 2, 0)
    h1 = jnp.where(row == 0, prev8[7:8, :], r1[:8])
    h2 = jnp.where(row == 0, prev8[6:7, :], jnp.where(row == 1, prev8[7:8, :], r2[:8]))
    return jnp.concatenate([h1, r1[8:]], axis=0), jnp.concatenate([h2, r2[8:]], axis=0)


def _shift_up(main, next8):
    n = main.shape[0]
    row = lax.broadcasted_iota(jnp.int32, (8, main.shape[1]), 0)
    r1, r2 = pltpu.roll(main, n - 1, 0), pltpu.roll(main, n - 2, 0)
    t1 = jnp.where(row == 7, next8[0:1, :], r1[n - 8:])
    t2 = jnp.where(row == 6, next8[0:1, :], jnp.where(row == 7, next8[1:2, :], r2[n - 8:]))
    return jnp.concatenate([r1[:n - 8], t1], axis=0), jnp.concatenate([r2[:n - 8], t2], axis=0)


def _conv(h, s1, s2, w_ref, b_ref):
    return w_ref[0:1, :] * s2 + w_ref[1:2, :] * s1 + w_ref[2:3, :] * h + b_ref[...]


def _ffn_fwd(h3, w_up, cw, cb, w_down, x2, tgt, g_post, name):
    s = h3.shape[0]
    tn = TN_FF
    nj = D_FF // tn
    per = D_MODEL // tn
    hb = TM // 8

    def body(h_ref, halo_ref, wg_ref, wu_ref, cwg_ref, cwu_ref, cbg_ref, cbu_ref, wd_ref, x_ref, t_ref, g_ref,
             hg_ref, hu_ref, a_ref, loss_ref, dx_ref, dy_ref, dg_ref, y_acc):
        i, j = pl.program_id(0), pl.program_id(1)
        h = h_ref[...]
        halo = halo_ref[...]
        halo = jnp.where(i > 0, halo, jnp.zeros_like(halo))
        conv = []
        for w_ref, cw_ref, cb_ref, hid_ref in ((wg_ref, cwg_ref, cbg_ref, hg_ref), (wu_ref, cwu_ref, cbu_ref, hu_ref)):
            hm = _dot(h, w_ref[...], NN)
            hid_ref[...] = hm.astype(BF16)
            hl = _dot(halo, w_ref[...], NN)
            s1, s2 = _shift_down(hm, hl)
            conv.append(_conv(hm, s1, s2, cw_ref, cb_ref))
        a = (_gelu(conv[0]) * conv[1]).astype(BF16)
        a_ref[...] = a
        contrib = _dot(a, wd_ref[...], NN)

        @pl.when(j == 0)
        def _():
            y_acc[...] = contrib

        @pl.when(j > 0)
        def _():
            y_acc[...] += contrib

        @pl.when(j == nj - 1)
        def _():
            yv = y_acc[...]
            r = _rstd(yv)
            yn = yv * r
            e = x_ref[...] + yn * g_ref[...] - t_ref[...]
            part = 0.5 * jnp.sum(jnp.mean(e * e, axis=-1, keepdims=True), axis=0, keepdims=True)
            part = jnp.broadcast_to(part, (1, 128))
            dx = e * (1.0 / D_MODEL)
            dx_ref[...] = dx
            dy_ref[...] = _norm_bwd_rows(dx * g_ref[...], yn, r).astype(BF16)
            dg = jnp.sum(dx * yn, axis=0, keepdims=True)

            @pl.when(i == 0)
            def _():
                dg_ref[...] = dg
                loss_ref[...] = part

            @pl.when(i > 0)
            def _():
                dg_ref[...] += dg
                loss_ref[...] += part

    rows = pl.BlockSpec((TM, D_MODEL), lambda i, j: (i, 0))
    return pl.pallas_call(
        body, name=name, grid=(s // TM, nj),
        in_specs=[rows,
                  pl.BlockSpec((8, D_MODEL), lambda i, j: (jnp.maximum(i * hb - 1, 0), 0)),
                  pl.BlockSpec((None, D_MODEL, tn), lambda i, j: (j // per, 0, j % per)),
                  pl.BlockSpec((None, D_MODEL, tn), lambda i, j: (NDEV // 2 + j // per, 0, j % per)),
                  pl.BlockSpec((8, tn), lambda i, j: (0, j)),
                  pl.BlockSpec((8, tn), lambda i, j: (0, nj + j)),
                  pl.BlockSpec((1, tn), lambda i, j: (0, j)),
                  pl.BlockSpec((1, tn), lambda i, j: (0, nj + j)),
                  pl.BlockSpec((tn, D_MODEL), lambda i, j: (j, 0)),
                  rows, rows, pl.BlockSpec((1, D_MODEL), lambda i, j: (0, 0))],
        out_specs=[pl.BlockSpec((TM, tn), lambda i, j: (i, j)), pl.BlockSpec((TM, tn), lambda i, j: (i, j)),
                   pl.BlockSpec((TM, tn), lambda i, j: (i, j)),
                   pl.BlockSpec((1, 128), lambda i, j: (0, 0)), rows, rows,
                   pl.BlockSpec((1, D_MODEL), lambda i, j: (0, 0))],
        out_shape=[jax.ShapeDtypeStruct((s, D_FF), BF16), jax.ShapeDtypeStruct((s, D_FF), BF16),
                   jax.ShapeDtypeStruct((s, D_FF), BF16),
                   jax.ShapeDtypeStruct((1, 128), F32), jax.ShapeDtypeStruct((s, D_MODEL), F32),
                   jax.ShapeDtypeStruct((s, D_MODEL), BF16), jax.ShapeDtypeStruct((1, D_MODEL), F32)],
        scratch_shapes=[pltpu.VMEM((TM, D_MODEL), F32)],
        compiler_params=_cp("arbitrary", "arbitrary"),
    )(h3, h3, w_up, w_up, cw, cw, cb, cb, w_down, x2, tgt, g_post)


def _ffn_bwd(dy3, w_down, hid_g, hid_u, cw, cb, name):
    s = dy3.shape[0]
    n = s // TM
    tn = TN_FF
    nj = D_FF // tn
    hb = TM // 8
    last8 = s // 8 - 1

    def body(dy_ref, dyp_ref, wd_ref, hg_ref, hgl_ref, hgn_ref, hu_ref, hul_ref, hun_ref,
             cwg_ref, cwu_ref, cbg_ref, cbu_ref,
             dhg_ref, dhu_ref, dcwg_ref, dcwu_ref, dcbg_ref, dcbu_ref):
        i = pl.program_id(1)
        first, last = i == 0, i == n - 1
        da = _dot(dy_ref[...], wd_ref[...], NT)
        dyp = dyp_ref[...]
        dyp = jnp.where(last, jnp.zeros_like(dyp), dyp)
        da_n = _dot(dyp, wd_ref[...], NT)
        parts = []
        for h_ref, hl_ref, hn_ref, cw_ref, cb_ref in ((hg_ref, hgl_ref, hgn_ref, cwg_ref, cbg_ref),
                                                     (hu_ref, hul_ref, hun_ref, cwu_ref, cbu_ref)):
            hm = h_ref[...].astype(F32)
            hl = jnp.where(first, 0.0, hl_ref[...].astype(F32))
            hn = hn_ref[...].astype(F32)
            s1, s2 = _shift_down(hm, hl)
            c = _conv(hm, s1, s2, cw_ref, cb_ref)
            n1, n2 = _shift_down(hn, hm[TM - 8:, :])
            cn = _conv(hn, n1, n2, cw_ref, cb_ref)
            parts.append((hm, s1, s2, c, cn))
        g, dg = _gelu_and_grad(parts[0][3])
        gn, dgn = _gelu_and_grad(parts[0][4])
        dc_g = da * parts[1][3] * dg
        dc_u = da * g
        dcn_g = da_n * parts[1][4] * dgn
        dcn_u = da_n * gn
        outs = ((dc_g, dcn_g, parts[0], cwg_ref, dhg_ref, dcwg_ref, dcbg_ref),
                (dc_u, dcn_u, parts[1], cwu_ref, dhu_ref, dcwu_ref, dcbu_ref))
        for dc, dcn, (hm, s1, s2, _, _), cw_ref, dh_ref, dcw_ref, dcb_ref in outs:
            u1, u2 = _shift_up(dc, dcn)
            dh_ref[...] = (cw_ref[2:3, :] * dc + cw_ref[1:2, :] * u1 + cw_ref[0:1, :] * u2).astype(BF16)
            dcb = jnp.sum(dc, axis=0, keepdims=True)
            row8 = lax.broadcasted_iota(jnp.int32, (8, tn), 0)
            dcw = jnp.where(row8 == 0, jnp.sum(dc * s2, axis=0, keepdims=True),
                            jnp.where(row8 == 1, jnp.sum(dc * s1, axis=0, keepdims=True),
                                      jnp.where(row8 == 2, jnp.sum(dc * hm, axis=0, keepdims=True), 0.0)))

            @pl.when(first)
            def _():
                dcw_ref[...] = dcw
                dcb_ref[...] = dcb

            @pl.when(i > 0)
            def _():
                dcw_ref[...] += dcw
                dcb_ref[...] += dcb

    prev8 = lambda j, i: (jnp.maximum(i * hb - 1, 0), j)
    next8 = lambda j, i: (jnp.minimum((i + 1) * hb, last8), j)
    blk = lambda j, i: (i, j)
    col = lambda j, i: (0, j)
    colu = lambda j, i: (0, nj + j)
    return pl.pallas_call(
        body, name=name, grid=(nj, n),
        in_specs=[pl.BlockSpec((TM, D_MODEL), lambda j, i: (i, 0)),
                  pl.BlockSpec((8, D_MODEL), lambda j, i: (jnp.minimum((i + 1) * hb, last8), 0)),
                  pl.BlockSpec((tn, D_MODEL), lambda j, i: (j, 0)),
                  pl.BlockSpec((TM, tn), blk), pl.BlockSpec((8, tn), prev8), pl.BlockSpec((8, tn), next8),
                  pl.BlockSpec((TM, tn), blk), pl.BlockSpec((8, tn), prev8), pl.BlockSpec((8, tn), next8),
                  pl.BlockSpec((8, tn), col), pl.BlockSpec((8, tn), colu),
                  pl.BlockSpec((1, tn), col), pl.BlockSpec((1, tn), colu)],
        out_specs=[pl.BlockSpec((TM, tn), blk), pl.BlockSpec((TM, tn), blk),
                   pl.BlockSpec((8, tn), col), pl.BlockSpec((8, tn), col),
                   pl.BlockSpec((1, tn), col), pl.BlockSpec((1, tn), col)],
        out_shape=[jax.ShapeDtypeStruct((s, D_FF), BF16), jax.ShapeDtypeStruct((s, D_FF), BF16),
                   jax.ShapeDtypeStruct((8, D_FF), F32), jax.ShapeDtypeStruct((8, D_FF), F32),
                   jax.ShapeDtypeStruct((1, D_FF), F32), jax.ShapeDtypeStruct((1, D_FF), F32)],
        compiler_params=_cp("parallel", "arbitrary"),
    )(dy3, dy3, w_down, hid_g, hid_g, hid_g, hid_u, hid_u, hid_u, cw, cw, cb, cb)


def _slot(p):
    return 4 * p[0] + 2 * p[1] + p[2]


def _all_gather(shards, name):
    n = len(shards)

    def body(*refs):
        ins, outs = refs[:n], refs[n:2 * n]
        send_sems, recv_sems, local_sems = refs[2 * n:]
        x, y, c = lax.axis_index("x"), lax.axis_index("y"), lax.axis_index("c")
        me, sibling = (x, y, c), (x, y, 1 - c)
        chips = [(1 - x, y), (x, 1 - y), (1 - x, 1 - y)]

        def copy(a, k, block, to, from_input=False):
            dst = outs[a].at[_slot(block)]
            return pltpu.make_async_remote_copy(
                src_ref=ins[a] if from_input else dst, dst_ref=dst,
                send_sem=send_sems.at[a, k], recv_sem=recv_sems.at[a, k],
                device_id=to, device_id_type=MESH)

        mine = [pltpu.make_async_copy(ins[a], outs[a].at[_slot(me)], local_sems.at[a]) for a in range(n)]
        for cp in mine:
            cp.start()
        first = []
        for a in range(n):
            first.append(copy(a, 0, me, sibling, True))
            first += [copy(a, 1 + j, me, (*chip, c), True) for j, chip in enumerate(chips)]
        for cp in first:
            cp.start()
        passed = []
        for j, chip in enumerate(chips):
            for a in range(n):
                copy(a, 1 + j, (*chip, c), me).wait_recv()
                fwd = copy(a, 4 + j, (*chip, c), sibling)
                fwd.start()
                passed.append(fwd)
        for a in range(n):
            copy(a, 0, sibling, me).wait_recv()
            for j, chip in enumerate(chips):
                copy(a, 4 + j, (*chip, 1 - c), me).wait_recv()
        for cp in first + passed:
            cp.wait_send()
        for cp in mine:
            cp.wait()

    any_spec = pl.BlockSpec(memory_space=pl.ANY)
    return pl.pallas_call(
        body, name=name,
        in_specs=[any_spec] * n, out_specs=[any_spec] * n,
        out_shape=[jax.ShapeDtypeStruct((NDEV,) + s.shape, s.dtype) for s in shards],
        scratch_shapes=[pltpu.SemaphoreType.DMA((n, 7)), pltpu.SemaphoreType.DMA((n, 7)),
                        pltpu.SemaphoreType.DMA((n,))],
    )(*shards)


def _peer_list(x, y, c):
    return [(1 - x if m & 4 else x, 1 - y if m & 2 else y, 1 - c if m & 1 else c) for m in range(1, NDEV)]


def _exchange_copies(src_refs, land_refs, send_sems, recv_sems, gather):
    x, y, c = lax.axis_index("x"), lax.axis_index("y"), lax.axis_index("c")
    me = (x, y, c)
    copies = []
    for m, peer in enumerate(_peer_list(x, y, c)):
        for a in range(len(src_refs)):
            copies.append(pltpu.make_async_remote_copy(
                src_ref=src_refs[a] if gather else src_refs[a].at[_slot(peer)], dst_ref=land_refs[a].at[_slot(me)],
                send_sem=send_sems.at[a * (NDEV - 1) + m], recv_sem=recv_sems.at[a * (NDEV - 1) + m],
                device_id=peer, device_id_type=MESH))
    return copies


def _exchange_start(srcs, lands, after, gather, name):
    n = len(srcs)
    hbm = pl.BlockSpec(memory_space=pltpu.HBM)

    def body(*refs):
        for cp in _exchange_copies(refs[:n], refs[n:2 * n], refs[2 * n + 1], refs[2 * n + 2], gather):
            cp.start()
        token = refs[-1]
        token[...] = jnp.zeros_like(token)

    outs = pl.pallas_call(
        body, name=name,
        out_shape=(pltpu.SemaphoreType.DMA((n * (NDEV - 1),)), pltpu.SemaphoreType.DMA((n * (NDEV - 1),)),
                   *[pltpu.HBM(a.shape, a.dtype) for a in list(srcs) + list(lands)],
                   jax.ShapeDtypeStruct((8, 128), F32)),
        in_specs=[hbm] * (2 * n) + [pl.BlockSpec(memory_space=pl.ANY)],
        out_specs=(pl.BlockSpec(memory_space=pltpu.SEMAPHORE), pl.BlockSpec(memory_space=pltpu.SEMAPHORE),
                   *[hbm] * (2 * n), pl.BlockSpec(memory_space=pltpu.VMEM)),
        input_output_aliases={i: 2 + i for i in range(2 * n)},
        compiler_params=pltpu.CompilerParams(has_side_effects=pltpu.SideEffectType.DATAFLOW_SIDE_EFFECTING),
    )(*[pltpu.with_memory_space_constraint(a, pltpu.HBM) for a in list(srcs) + list(lands)], after)
    return outs[0], outs[1], outs[2:2 + n], outs[2 + n:2 + 2 * n], outs[-1]


def _exchange_wait(send_sems, recv_sems, srcs, lands, after, gather, name):
    n = len(srcs)
    hbm = pl.BlockSpec(memory_space=pltpu.HBM)

    def body(*refs):
        for cp in _exchange_copies(refs[:n], refs[n:2 * n], refs[2 * n], refs[2 * n + 1], gather):
            cp.wait_send()
            cp.wait_recv()

    outs = pl.pallas_call(
        body, name=name,
        out_shape=tuple(pltpu.HBM(a.shape, a.dtype) for a in list(srcs) + list(lands)),
        in_specs=[hbm] * (2 * n) + [pl.BlockSpec(memory_space=pltpu.SEMAPHORE)] * 2 + [pl.BlockSpec(memory_space=pl.ANY)],
        out_specs=tuple([hbm] * (2 * n)),
        input_output_aliases={i: i for i in range(2 * n)},
        compiler_params=pltpu.CompilerParams(has_side_effects=pltpu.SideEffectType.DATAFLOW_SIDE_EFFECTING),
    )(*srcs, *lands, send_sems, recv_sems, after)
    return outs[n:]


def _own_slot(block):
    me = 4 * lax.axis_index("x") + 2 * lax.axis_index("y") + lax.axis_index("c")
    return lax.dynamic_update_slice(jnp.zeros((NDEV,) + block.shape, block.dtype), block[None], (me, 0, 0))


def _adamw(parts, w, m, v, name):
    r, c = w.shape
    tr = r if r * c <= 160 * 1024 else max(8, (160 * 1024 // c) // 8 * 8)
    while r % tr:
        tr -= 8
    bc1 = 1.0 - ADAM_B1 ** ADAM_STEP
    bc2 = 1.0 - ADAM_B2 ** ADAM_STEP

    def body(p_ref, w_ref, m_ref, v_ref, g_ref, d_ref, mo_ref, vo_ref):
        g = p_ref[0].astype(F32)
        for d in range(1, NDEV):
            g = g + p_ref[d].astype(F32)
        g_ref[...] = g
        mn = ADAM_B1 * m_ref[...] + (1.0 - ADAM_B1) * g
        vn = ADAM_B2 * v_ref[...] + (1.0 - ADAM_B2) * (g * g)
        mo_ref[...] = mn
        vo_ref[...] = vn
        d_ref[...] = -ADAM_LR * ((mn / bc1) / (jnp.sqrt(vn / bc2) + ADAM_EPS) + ADAM_WD * w_ref[...])

    spec = pl.BlockSpec((tr, c), lambda i: (i, 0))
    return pl.pallas_call(
        body, name=name, grid=(r // tr,),
        in_specs=[pl.BlockSpec((NDEV, tr, c), lambda i: (0, i, 0)), spec, spec, spec],
        out_specs=[spec] * 4, out_shape=[jax.ShapeDtypeStruct((r, c), F32)] * 4,
        compiler_params=_cp("parallel"),
    )(parts, w, m, v)


def _local_step(x, mem, tgt, gains, b_forget, w_pool, pool_scale, conv_b, w_in,
                mix_weights, ffn_weights, send_in_grad, send_mix_grads, send_ffn_grads):
    w_f = w_in[:, F_COL:]
    b_pad = jnp.pad(b_forget, ((0, 0), (0, 128 - FOX_HEADS)))
    wbd = jnp.zeros((D_POOL, D_POOL), F32)
    for g in range(4):
        wbd = wbd.at[64 * g:64 * g + 64, 64 * g:64 * g + 64].set(w_pool[g])
    wbd = wbd.astype(BF16)
    scale = pool_scale.reshape(1, D_POOL)

    h1 = _norm_fwd(x, gains["mix_pre"], "norm_mix_pre")
    proj = _mm(h1, w_in, "nn", BF16, 1024, 384, 1024, "proj_in")
    fraw = _mm(h1, w_f, "nn", F32, 1024, 128, 1024, "proj_gate")
    flog, cum = _gate_cumsum(fraw, b_pad, "gate_cumsum")
    aq, ak = _fox_operands(cum, "fox_operands")
    ycat, aqb = _fox_fwd(proj, aq, ak, "fox_fwd")
    ycat = _pool_fwd(proj, wbd, scale, ycat, "pool_fwd")
    w_mix, w_xq, w_xo, w_xkv = mix_weights(ycat)
    y1, x1, h2 = _mm_rows(ycat, w_mix, "nn", 1024, "mix_out", [x], [gains["mix_post"], gains["xa_pre"]],
                          [F32, F32, BF16], _epi_resid)
    q2 = _mm(h2, w_xq, "nn", BF16, 1024, 1024, 1024, "xa_q")
    mem_n = _norm_fwd(mem, gains["mem"], "norm_mem")
    kv = _mm(mem_n, w_xkv, "nn", BF16, mem.shape[0], 256, 1024, "xa_kv", b_cols=256)
    o2 = _xattn_fwd(q2, kv, "xattn_fwd")
    y2, x2, h3 = _mm_rows(o2, w_xo, "nn", 1024, "xa_out", [x1], [gains["xa_post"], gains["ffn_pre"]],
                          [F32, F32, BF16], _epi_resid)
    w_up, w_down, cw = ffn_weights(h3)
    hid_g, hid_u, act, loss, dx3, dy3, dg_ffn_post = _ffn_fwd(h3, w_up, cw, conv_b, w_down, x2, tgt,
                                                              gains["ffn_post"], "ffn_fwd")

    dhid_g, dhid_u, dcw_g, dcw_u, dcb_g, dcb_u = _ffn_bwd(dy3, w_down, hid_g, hid_u, cw, conv_b, "ffn_bwd")
    d_w_down = _mm(act, dy3, "tn", BF16, 1024, 1024, 1024, "dw_down")
    d_w_up = _mm(h3, [dhid_g, dhid_u], "tn", BF16, 1024, 1024, 1024, "dw_up", out_cols=1024)
    sent = send_ffn_grads(d_w_up, d_w_down, jnp.concatenate([dcw_g, dcw_u], axis=1))
    dx2, dg_ffn_pre, dy2, dg_xa_post = _mm_rows(
        [dhid_g, dhid_u], w_up, "nt", 1024, "dh_ffn", [x2, dx3, y2], [gains["ffn_pre"], gains["xa_post"]],
        [F32, "sum", BF16, "sum"], _epi_norm_bwd, b_cols=1024, after=sent)
    do2 = _mm(dy2, w_xo, "nt", BF16, 1024, 1024, 1024, "d_xa_out")
    d_w_xo = _mm(o2, dy2, "tn", BF16, 1024, 1024, 1024, "dw_xo")
    dq2, dkv = _xattn_bwd(q2, kv, do2, "xattn_bwd")
    dkv = dkv.astype(BF16)
    dx1, dg_xa_pre, dy1, dg_mix_post = _mm_rows(
        dq2, w_xq, "nt", 1024, "dh_xa", [x1, dx2, y1], [gains["xa_pre"], gains["mix_post"]],
        [F32, "sum", BF16, "sum"], _epi_norm_bwd)
    d_w_xq = _mm(h2, dq2, "tn", BF16, 1024, 1024, 1024, "dw_xq")
    dmem_n = _mm(dkv, w_xkv, "nt", F32, mem.shape[0], 1024, 256, "d_mem", b_cols=256)
    d_w_xkv = _mm(mem_n, dkv, "tn", BF16, 1024, 256, mem.shape[0], "dw_xkv", out_cols=256)
    _, dg_mem = _norm_bwd(dmem_n, mem, jnp.zeros_like(mem), gains["mem"], "norm_bwd_mem")
    dycat = _mm(dy1, w_mix, "nt", BF16, 1024, 1024, 1024, "d_mix_out")
    d_w_mix = _mm(ycat, dy1, "tn", BF16, 1024, 1024, 1024, "dw_mix")
    sent_mix = send_mix_grads(d_w_mix, d_w_xq, d_w_xo, d_w_xkv)
    ad = _fox_do_operand(dycat, ycat, sent_mix, "fox_do_operand")
    dq, dk, dv, qaux, kaux = _fox_bwd(proj, dycat, aqb, ak, ad, "fox_bwd")
    du, d_wbd, d_scale = _pool_bwd(proj, dycat, wbd, scale, "pool_bwd")
    df, db_f = _gate_bwd(qaux, kaux, flog, "gate_bwd")
    dproj = jnp.concatenate([du, dq, dk, dv, df], axis=1)
    sent_in = send_in_grad(_mm(h1, dproj, "tn", BF16, 1024, 384, 1024, "dw_in"))
    grad_x, dg_mix_pre = _mm_rows(dproj, w_in, "nt", 896, "dh_mix", [x, dx1], [gains["mix_pre"]],
                                  [F32, "sum"], _epi_norm_bwd, after=sent_in)

    small = dict(
        mix_pre=dg_mix_pre, mix_post=dg_mix_post, mem=dg_mem, xa_pre=dg_xa_pre, xa_post=dg_xa_post,
        ffn_pre=dg_ffn_pre, ffn_post=dg_ffn_post,
        conv_b=jnp.concatenate([dcb_g, dcb_u], axis=1),
        w_pool=jnp.stack([d_wbd[64 * g:64 * g + 64, 64 * g:64 * g + 64] for g in range(4)]),
        pool_scale=d_scale.reshape(4, 64),
        b_forget=db_f[:, :FOX_HEADS],
    )
    return loss, grad_x, small


SMALL_ORDER = ("mix_pre", "mix_post", "mem", "xa_pre", "xa_post", "ffn_pre", "ffn_post", "conv_b",
               "w_pool", "pool_scale", "b_forget")
SMALL_ROWS = 256


def _pack_small(d):
    flat = jnp.concatenate([d[k].reshape(-1).astype(F32) for k in SMALL_ORDER])
    return jnp.pad(flat, (0, SMALL_ROWS * 128 - flat.shape[0])).reshape(SMALL_ROWS, 128)


def _unpack_small(a, like):
    flat = a.reshape(-1)
    out, off = {}, 0
    for k in SMALL_ORDER:
        n = like[k].size
        out[k] = flat[off:off + n].reshape(like[k].shape)
        off += n
    return out


def kernel(x, mem, norm_mix_pre, norm_mix_post, w_in, b_forget, w_pool, pool_scale, w_mix_out, norm_mem, norm_xa_pre, norm_xa_post, w_xq, w_xkv, w_xo, norm_ffn_pre, norm_ffn_post, w_up, conv_w, conv_b, w_down, loss_target, m_norm_mix_pre, m_norm_mix_post, m_w_in, m_b_forget, m_w_pool, m_pool_scale, m_w_mix_out, m_norm_mem, m_norm_xa_pre, m_norm_xa_post, m_w_xq, m_w_xkv, m_w_xo, m_norm_ffn_pre, m_norm_ffn_post, m_w_up, m_conv_w, m_conv_b, m_w_down, v_norm_mix_pre, v_norm_mix_post, v_w_in, v_b_forget, v_w_pool, v_pool_scale, v_w_mix_out, v_norm_mem, v_norm_xa_pre, v_norm_xa_post, v_w_xq, v_w_xkv, v_w_xo, v_norm_ffn_pre, v_norm_ffn_post, v_w_up, v_conv_w, v_conv_b, v_w_down):
    names = ("norm_mix_pre", "norm_mix_post", "w_in", "b_forget", "w_pool", "pool_scale", "w_mix_out", "norm_mem",
             "norm_xa_pre", "norm_xa_post", "w_xq", "w_xkv", "w_xo", "norm_ffn_pre", "norm_ffn_post", "w_up",
             "conv_w", "conv_b", "w_down")
    w = dict(zip(names, (norm_mix_pre, norm_mix_post, w_in, b_forget, w_pool, pool_scale, w_mix_out, norm_mem,
                         norm_xa_pre, norm_xa_post, w_xq, w_xkv, w_xo, norm_ffn_pre, norm_ffn_post, w_up,
                         conv_w, conv_b, w_down)))
    mo = dict(zip(names, (m_norm_mix_pre, m_norm_mix_post, m_w_in, m_b_forget, m_w_pool, m_pool_scale, m_w_mix_out,
                          m_norm_mem, m_norm_xa_pre, m_norm_xa_post, m_w_xq, m_w_xkv, m_w_xo, m_norm_ffn_pre,
                          m_norm_ffn_post, m_w_up, m_conv_w, m_conv_b, m_w_down)))
    vo = dict(zip(names, (v_norm_mix_pre, v_norm_mix_post, v_w_in, v_b_forget, v_w_pool, v_pool_scale, v_w_mix_out,
                          v_norm_mem, v_norm_xa_pre, v_norm_xa_post, v_w_xq, v_w_xkv, v_w_xo, v_norm_ffn_pre,
                          v_norm_ffn_post, v_w_up, v_conv_w, v_conv_b, v_w_down)))

    big_names = ("w_in", "w_mix_out", "w_xq", "w_xo", "w_xkv", "w_up", "w_down")
    shards = {k: w[k][0].astype(BF16) for k in big_names}
    shards["w_in"] = jnp.pad(shards["w_in"], ((0, 0), (0, D_IN_PAD - shards["w_in"].shape[1])))
    conv_w_sh = jnp.pad(conv_w[0, :, 0, :], ((0, 5), (0, 0)))
    (g_in,) = _all_gather([shards["w_in"]], "gather_w_in")
    mix_srcs = [shards[k] for k in ("w_mix_out", "w_xq", "w_xo", "w_xkv")]
    mix_flight = _exchange_start(mix_srcs, [_own_slot(a) for a in mix_srcs], g_in, True, "gather_mix_start")
    ffn_srcs = [shards["w_up"], shards["w_down"], conv_w_sh]
    ffn_flight = _exchange_start(ffn_srcs, [_own_slot(a) for a in ffn_srcs], mix_flight[4], True, "gather_ffn_start")
    my_slot = 4 * lax.axis_index("x") + 2 * lax.axis_index("y") + lax.axis_index("c")
    own_block = lambda a: _own_slot(lax.dynamic_index_in_dim(a, my_slot, 0, keepdims=False))
    by_rows = lambda a: a.reshape(NDEV, a.shape[0] // NDEV, a.shape[1])
    by_cols = lambda a: a.reshape(a.shape[0], NDEV, a.shape[1] // NDEV).transpose(1, 0, 2)
    grad_flight = {}

    def mix_weights(after):
        g_mix, g_xq, g_xo, g_xkv = _exchange_wait(*mix_flight[:4], after, True, "gather_mix_wait")
        return (g_mix.reshape(D_MODEL, D_MODEL), g_xq.reshape(D_MODEL, D_MODEL), g_xo.reshape(D_MODEL, D_MODEL), g_xkv)

    def ffn_weights(after):
        g_up, g_down, g_cw = _exchange_wait(*ffn_flight[:4], after, True, "gather_ffn_wait")
        return g_up, g_down.reshape(D_FF, D_MODEL), g_cw.transpose(1, 0, 2).reshape(8, 2 * D_FF)

    def send_ffn_grads(d_w_up, d_w_down, d_cw):
        srcs = [d_w_up, by_rows(d_w_down), by_cols(d_cw)]
        grad_flight["ffn"] = _exchange_start(srcs, [own_block(a) for a in srcs], d_w_up, False, "scatter_ffn_start")
        return grad_flight["ffn"][4]

    def send_mix_grads(d_w_mix, d_w_xq, d_w_xo, d_w_xkv):
        srcs = [by_rows(d_w_mix), by_rows(d_w_xq), by_rows(d_w_xo), d_w_xkv]
        grad_flight["mix"] = _exchange_start(srcs, [own_block(a) for a in srcs], d_w_mix, False, "scatter_mix_start")
        return grad_flight["mix"][4]

    def send_in_grad(d_w_in):
        srcs = [by_rows(d_w_in)]
        grad_flight["in"] = _exchange_start(srcs, [own_block(a) for a in srcs], d_w_in, False, "scatter_in_start")
        return grad_flight["in"][4]

    gains = dict(mix_pre=norm_mix_pre + ffn_flight[4][0, 0], mix_post=norm_mix_post, mem=norm_mem, xa_pre=norm_xa_pre,
                 xa_post=norm_xa_post, ffn_pre=norm_ffn_pre, ffn_post=norm_ffn_post)
    loss, grad_x, small = _local_step(
        x[0], mem[0], loss_target[0], gains, b_forget, w_pool[0], pool_scale[0], conv_b,
        g_in.reshape(D_MODEL, D_IN_PAD), mix_weights, ffn_weights, send_in_grad, send_mix_grads, send_ffn_grads)

    p_up, p_down, p_cw = _exchange_wait(*grad_flight["ffn"][:4], grad_x, False, "scatter_ffn_wait")
    p_mix, p_xq, p_xo, p_xkv = _exchange_wait(*grad_flight["mix"][:4], grad_x, False, "scatter_mix_wait")
    (p_in,) = _exchange_wait(*grad_flight["in"][:4], grad_x, False, "scatter_in_wait")
    parts = [p_in, p_mix, p_xq, p_xo, p_xkv, p_up, p_down, p_cw]
    small_like = dict(mix_pre=norm_mix_pre, mix_post=norm_mix_post, mem=norm_mem, xa_pre=norm_xa_pre,
                      xa_post=norm_xa_post, ffn_pre=norm_ffn_pre, ffn_post=norm_ffn_post, conv_b=conv_b,
                      w_pool=w_pool, pool_scale=pool_scale, b_forget=b_forget)
    small = {k: small[k].reshape(small_like[k].shape) for k in SMALL_ORDER}
    (small_parts,) = _all_gather([_pack_small(small)], "gather_small_grads")

    res = {}
    for k, p in zip(big_names, parts[:7]):
        if k == "w_in":
            p = p[:, :, :w_in.shape[2]]
        res[k] = [a[None] for a in _adamw(p, w[k][0], mo[k][0], vo[k][0], "adamw_" + k)]
    pad_cw = lambda a: jnp.pad(a[0, :, 0, :], ((0, 5), (0, 0)))
    res["conv_w"] = [a[:3][None, :, None, :] for a in
                     _adamw(parts[7], pad_cw(conv_w), pad_cw(m_conv_w), pad_cw(v_conv_w), "adamw_conv_w")]
    key_of = dict(mix_pre="norm_mix_pre", mix_post="norm_mix_post", mem="norm_mem", xa_pre="norm_xa_pre",
                  xa_post="norm_xa_post", ffn_pre="norm_ffn_pre", ffn_post="norm_ffn_post", conv_b="conv_b",
                  w_pool="w_pool", pool_scale="pool_scale", b_forget="b_forget")
    pack_of = lambda src: _pack_small({k: src[key_of[k]] for k in SMALL_ORDER})
    small_out = _adamw(small_parts, pack_of(w), pack_of(mo), pack_of(vo), "adamw_small")
    small_out = [_unpack_small(a, small_like) for a in small_out]
    for k in SMALL_ORDER:
        res[key_of[k]] = [so[k] for so in small_out]

    total = lax.psum(loss[0, 0], ("x", "y", "c"))
    outs = [total, grad_x[None]]
    for idx in range(4):
        outs += [res[k][idx] for k in names]
    return tuple(outs)
```

```python
import functools
import math

import jax
import jax.numpy as jnp
from jax import lax
from jax.experimental import pallas as pl
from jax.experimental.pallas import tpu as pltpu

F32 = jnp.float32
BF16 = jnp.bfloat16

NDEV = 8
D_MODEL = 1024
D_POOL = 256
D_FOX = 768
FOX_HEADS = 12
HEAD_PAIRS = FOX_HEADS // 2
XA_HEADS = 4
XA_DIM = 256
D_FF = 4096
D_IN_PAD = 2688
F_COL = 2560
POOL_HALO = 16
NORM_EPS = 1e-6
NEG = -1e30

ADAM_LR = 0.001
ADAM_B1 = 0.9
ADAM_B2 = 0.999
ADAM_EPS = 1e-08
ADAM_WD = 0.01
ADAM_STEP = 10

TM = 512
TQ = 512
TN_FF = 512
VMEM_LIMIT = 56 * 1024 * 1024
MESH = pl.DeviceIdType.MESH


def _cp(*sem):
    return pltpu.CompilerParams(dimension_semantics=sem, vmem_limit_bytes=VMEM_LIMIT)


def _dot(a, b, dims):
    return lax.dot_general(a, b, (dims, ((), ())), preferred_element_type=F32)


NN = ((1,), (0,))
NT = ((1,), (1,))
TN = ((0,), (0,))


def _mm(a, b, mode, out_dtype, tm, tn, tk, name, b_cols=None, out_cols=None, after=None):
    a_list = list(a) if isinstance(a, (list, tuple)) else [a]
    b_list = list(b) if isinstance(b, (list, tuple)) else [b]
    assert len(a_list) == 1 or len(b_list) == 1
    if mode == "tn":
        K, M = a_list[0].shape
        assert len(a_list) == 1
        Ns = [x.shape[1] for x in b_list]
        N = sum(Ns)
        assert b_cols is None
    else:
        assert len(b_list) == 1
        M = a_list[0].shape[0]
        Ks = [x.shape[1] for x in a_list]
        K = sum(Ks)
        if b_cols is None:
            N = b_list[0].shape[0] if mode == "nt" else b_list[0].shape[1]
        else:
            N = b_list[0].shape[1] if mode == "nt" else NDEV * b_cols
    assert M % tm == 0 and N % tn == 0 and K % tk == 0, (name, M, N, K)
    grid = (M // tm, N // tn, K // tk)
    nk = grid[2]
    dims = {"nn": NN, "nt": NT, "tn": TN}[mode]

    in_specs = []
    if mode == "tn":
        in_specs.append(pl.BlockSpec((tk, tm), lambda i, j, k: (k, i)))
        if len(b_list) == 1:
            in_specs.append(pl.BlockSpec((tk, tn), lambda i, j, k: (k, j)))
        else:
            nj1 = Ns[0] // tn
            in_specs.append(pl.BlockSpec((tk, tn), lambda i, j, k: (k, jnp.minimum(j, nj1 - 1))))
            in_specs.append(pl.BlockSpec((tk, tn), lambda i, j, k: (k, jnp.maximum(j - nj1, 0))))
    else:
        if len(a_list) == 1:
            in_specs.append(pl.BlockSpec((tm, tk), lambda i, j, k: (i, k)))
        else:
            nk1 = Ks[0] // tk
            in_specs.append(pl.BlockSpec((tm, tk), lambda i, j, k: (i, jnp.minimum(k, nk1 - 1))))
            in_specs.append(pl.BlockSpec((tm, tk), lambda i, j, k: (i, jnp.maximum(k - nk1, 0))))
        if b_cols is None:
            if mode == "nn":
                in_specs.append(pl.BlockSpec((tk, tn), lambda i, j, k: (k, j)))
            else:
                in_specs.append(pl.BlockSpec((tn, tk), lambda i, j, k: (j, k)))
        else:
            if mode == "nn":
                per = b_cols // tn
                in_specs.append(pl.BlockSpec((None, tk, tn), lambda i, j, k: (j // per, k, j % per)))
            else:
                per = b_cols // tk
                in_specs.append(pl.BlockSpec((None, tn, tk), lambda i, j, k: (k // per, j, k % per)))
    if out_cols is None:
        out_spec = pl.BlockSpec((tm, tn), lambda i, j, k: (i, j))
        out_shape = jax.ShapeDtypeStruct((M, N), out_dtype)
    else:
        pero = out_cols // tn
        out_spec = pl.BlockSpec((None, tm, tn), lambda i, j, k: (j // pero, i, j % pero))
        out_shape = jax.ShapeDtypeStruct((NDEV, M, out_cols), out_dtype)

    two_a = len(a_list) == 2
    two_b = len(b_list) == 2
    extra = []
    if after is not None:
        in_specs.append(pl.BlockSpec(memory_space=pl.ANY))
        extra.append(after)

    def body(*refs):
        o_ref, acc_ref = refs[-2], refs[-1]
        j = pl.program_id(1)
        k = pl.program_id(2)

        @pl.when(k == 0)
        def _():
            acc_ref[...] = jnp.zeros_like(acc_ref)

        if two_a:
            a1, a2, b1 = refs[0], refs[1], refs[2]
            nk1_ = Ks[0] // tk

            @pl.when(k < nk1_)
            def _():
                acc_ref[...] += _dot(a1[...], b1[...], dims)

            @pl.when(k >= nk1_)
            def _():
                acc_ref[...] += _dot(a2[...], b1[...], dims)
        elif two_b:
            a1, b1, b2 = refs[0], refs[1], refs[2]
            nj1_ = Ns[0] // tn

            @pl.when(j < nj1_)
            def _():
                acc_ref[...] += _dot(a1[...], b1[...], dims)

            @pl.when(j >= nj1_)
            def _():
                acc_ref[...] += _dot(a1[...], b2[...], dims)
        else:
            acc_ref[...] += _dot(refs[0][...], refs[1][...], dims)

        @pl.when(k == nk - 1)
        def _():
            o_ref[...] = acc_ref[...].astype(o_ref.dtype)

    return pl.pallas_call(
        body, name=name, grid=grid, in_specs=in_specs, out_specs=out_spec, out_shape=out_shape,
        scratch_shapes=[pltpu.VMEM((tm, tn), F32)],
        compiler_params=_cp("parallel", "parallel", "arbitrary"),
    )(*a_list, *b_list, *extra)


def _rstd(x):
    return lax.rsqrt(jnp.mean(x * x, axis=-1, keepdims=True) + NORM_EPS)


def _norm_bwd_rows(dxn, xn, r):
    return r * (dxn - xn * jnp.mean(dxn * xn, axis=-1, keepdims=True))


def _row_spec(tm, d):
    return pl.BlockSpec((tm, d), lambda i: (i, 0))


def _vec_spec(d):
    return pl.BlockSpec((1, d), lambda i: (0, 0))


def _mm_rows(a, b, mode, tk, name, rows, vecs, outs, epilogue, b_cols=None, after=None):
    a_list = list(a) if isinstance(a, (list, tuple)) else [a]
    m = a_list[0].shape[0]
    ks = [x.shape[1] for x in a_list]
    n = D_MODEL
    nk = sum(ks) // tk
    dims = NN if mode == "nn" else NT
    if len(a_list) == 1:
        in_specs = [pl.BlockSpec((TM, tk), lambda i, k: (i, k))]
    else:
        nk1 = ks[0] // tk
        in_specs = [pl.BlockSpec((TM, tk), lambda i, k: (i, jnp.minimum(k, nk1 - 1))),
                    pl.BlockSpec((TM, tk), lambda i, k: (i, jnp.maximum(k - nk1, 0)))]
    if mode == "nn":
        in_specs.append(pl.BlockSpec((tk, n), lambda i, k: (k, 0)))
    elif b_cols is None:
        in_specs.append(pl.BlockSpec((n, tk), lambda i, k: (0, k)))
    else:
        per = b_cols // tk
        in_specs.append(pl.BlockSpec((None, n, tk), lambda i, k: (k // per, 0, k % per)))
    in_specs += [pl.BlockSpec((TM, n), lambda i, k: (i, 0))] * len(rows)
    in_specs += [pl.BlockSpec((1, n), lambda i, k: (0, 0))] * len(vecs)
    extra = []
    if after is not None:
        in_specs.append(pl.BlockSpec(memory_space=pl.ANY))
        extra.append(after)
    out_specs, out_shape = [], []
    for o in outs:
        if o == "sum":
            out_specs.append(pl.BlockSpec((1, n), lambda i, k: (0, 0)))
            out_shape.append(jax.ShapeDtypeStruct((1, n), F32))
        else:
            out_specs.append(pl.BlockSpec((TM, n), lambda i, k: (i, 0)))
            out_shape.append(jax.ShapeDtypeStruct((m, n), o))
    na, nr, nv = len(a_list), len(rows), len(vecs)

    def body(*refs):
        a_refs, b_ref = refs[:na], refs[na]
        row_refs = refs[na + 1:na + 1 + nr]
        vec_refs = refs[na + 1 + nr:na + 1 + nr + nv]
        out_refs = refs[len(refs) - 1 - len(outs):len(refs) - 1]
        acc_ref = refs[-1]
        i, k = pl.program_id(0), pl.program_id(1)

        @pl.when(k == 0)
        def _():
            acc_ref[...] = jnp.zeros_like(acc_ref)

        if na == 1:
            acc_ref[...] += _dot(a_refs[0][...], b_ref[...], dims)
        else:
            nk1_ = ks[0] // tk

            @pl.when(k < nk1_)
            def _():
                acc_ref[...] += _dot(a_refs[0][...], b_ref[...], dims)

            @pl.when(k >= nk1_)
            def _():
                acc_ref[...] += _dot(a_refs[1][...], b_ref[...], dims)

        @pl.when(k == nk - 1)
        def _():
            vals = epilogue(acc_ref[...], [r[...] for r in row_refs], [v[...] for v in vec_refs])
            for o, ref, val in zip(outs, out_refs, vals):
                if o == "sum":
                    @pl.when(i == 0)
                    def _():
                        ref[...] = val

                    @pl.when(i > 0)
                    def _():
                        ref[...] += val
                else:
                    ref[...] = val.astype(o)

    return pl.pallas_call(
        body, name=name, grid=(m // TM, nk), in_specs=in_specs, out_specs=out_specs, out_shape=out_shape,
        scratch_shapes=[pltpu.VMEM((TM, n), F32)],
        compiler_params=_cp("arbitrary", "arbitrary"),
    )(*a_list, b, *rows, *vecs, *extra)


def _epi_resid(y, rows, vecs):
    (x_in,), (g_post, g_next) = rows, vecs
    xo = x_in + y * _rstd(y) * g_post
    return y, xo, xo * _rstd(xo) * g_next


def _epi_norm_bwd(dh, rows, vecs):
    x, dx_res = rows[0], rows[1]
    r = _rstd(x)
    xn = x * r
    dx = dx_res + _norm_bwd_rows(dh * vecs[0], xn, r)
    res = [dx, jnp.sum(dh * xn, axis=0, keepdims=True)]
    if len(rows) == 3:
        y = rows[2]
        r2 = _rstd(y)
        yn = y * r2
        res += [_norm_bwd_rows(dx * vecs[1], yn, r2), jnp.sum(dx * yn, axis=0, keepdims=True)]
    return res


def _norm_fwd(x, g, name):
    s, d = x.shape
    tm = min(TM, s)

    def body(x_ref, g_ref, h_ref):
        xv = x_ref[...]
        h_ref[...] = (xv * _rstd(xv) * g_ref[...]).astype(BF16)

    return pl.pallas_call(
        body, name=name, grid=(s // tm,), in_specs=[_row_spec(tm, d), _vec_spec(d)],
        out_specs=_row_spec(tm, d), out_shape=jax.ShapeDtypeStruct((s, d), BF16),
        compiler_params=_cp("parallel"),
    )(x, g)


def _resid_norm_fwd(x_in, y, g_post, g_next, name):
    s, d = x_in.shape

    def body(x_ref, y_ref, gp_ref, gn_ref, xo_ref, h_ref):
        yv = y_ref[...]
        xo = x_ref[...] + yv * _rstd(yv) * gp_ref[...]
        xo_ref[...] = xo
        h_ref[...] = (xo * _rstd(xo) * gn_ref[...]).astype(BF16)

    return pl.pallas_call(
        body, name=name, grid=(s // TM,),
        in_specs=[_row_spec(TM, d), _row_spec(TM, d), _vec_spec(d), _vec_spec(d)],
        out_specs=[_row_spec(TM, d), _row_spec(TM, d)],
        out_shape=[jax.ShapeDtypeStruct((s, d), F32), jax.ShapeDtypeStruct((s, d), BF16)],
        compiler_params=_cp("parallel"),
    )(x_in, y, g_post, g_next)


def _norm_bwd(dh, x, dx_res, g_pre, name, prev=None):
    s, d = x.shape
    tm = min(TM, s)
    has_prev = prev is not None

    def body(*refs):
        if has_prev:
            dh_ref, x_ref, dr_ref, g_ref, y_ref, gp_ref, dx_ref, dg_ref, dy_ref, dgp_ref = refs
        else:
            dh_ref, x_ref, dr_ref, g_ref, dx_ref, dg_ref = refs
        i = pl.program_id(0)
        xv = x_ref[...]
        r = _rstd(xv)
        xn = xv * r
        dhv = dh_ref[...].astype(F32)
        dx = dr_ref[...] + _norm_bwd_rows(dhv * g_ref[...], xn, r)
        dx_ref[...] = dx
        dg = jnp.sum(dhv * xn, axis=0, keepdims=True)

        @pl.when(i == 0)
        def _():
            dg_ref[...] = dg

        @pl.when(i > 0)
        def _():
            dg_ref[...] += dg

        if has_prev:
            yv = y_ref[...]
            r2 = _rstd(yv)
            yn = yv * r2
            dy_ref[...] = _norm_bwd_rows(dx * gp_ref[...], yn, r2).astype(BF16)
            dgp = jnp.sum(dx * yn, axis=0, keepdims=True)

            @pl.when(i == 0)
            def _():
                dgp_ref[...] = dgp

            @pl.when(i > 0)
            def _():
                dgp_ref[...] += dgp

    in_specs = [_row_spec(tm, d), _row_spec(tm, d), _row_spec(tm, d), _vec_spec(d)]
    out_specs = [_row_spec(tm, d), _vec_spec(d)]
    out_shape = [jax.ShapeDtypeStruct((s, d), F32), jax.ShapeDtypeStruct((1, d), F32)]
    args = [dh, x, dx_res, g_pre]
    if has_prev:
        in_specs += [_row_spec(tm, d), _vec_spec(d)]
        out_specs += [_row_spec(tm, d), _vec_spec(d)]
        out_shape += [jax.ShapeDtypeStruct((s, d), BF16), jax.ShapeDtypeStruct((1, d), F32)]
        args += list(prev)
    return pl.pallas_call(
        body, name=name, grid=(s // tm,), in_specs=in_specs, out_specs=out_specs, out_shape=out_shape,
        compiler_params=_cp("arbitrary"),
    )(*args)


def _loss_bwd(x2, y3, g_post, tgt, name):
    s, d = x2.shape

    def body(x_ref, y_ref, g_ref, t_ref, loss_ref, dx_ref, dy_ref, dg_ref):
        i = pl.program_id(0)
        yv = y_ref[...]
        r = _rstd(yv)
        yn = yv * r
        e = x_ref[...] + yn * g_ref[...] - t_ref[...]
        part = 0.5 * jnp.sum(jnp.mean(e * e, axis=-1, keepdims=True), axis=0, keepdims=True)
        dx = e * (1.0 / d)
        dx_ref[...] = dx
        dy_ref[...] = _norm_bwd_rows(dx * g_ref[...], yn, r).astype(BF16)
        dg = jnp.sum(dx * yn, axis=0, keepdims=True)
        part = jnp.broadcast_to(part, (1, 128))

        @pl.when(i == 0)
        def _():
            dg_ref[...] = dg
            loss_ref[...] = part

        @pl.when(i > 0)
        def _():
            dg_ref[...] += dg
            loss_ref[...] += part

    return pl.pallas_call(
        body, name=name, grid=(s // TM,),
        in_specs=[_row_spec(TM, d), _row_spec(TM, d), _vec_spec(d), _row_spec(TM, d)],
        out_specs=[_vec_spec(128), _row_spec(TM, d), _row_spec(TM, d), _vec_spec(d)],
        out_shape=[jax.ShapeDtypeStruct((1, 128), F32), jax.ShapeDtypeStruct((s, d), F32),
                   jax.ShapeDtypeStruct((s, d), BF16), jax.ShapeDtypeStruct((1, d), F32)],
        compiler_params=_cp("arbitrary"),
    )(x2, y3, g_post, tgt)


def _split3(v):
    hi = v.astype(BF16)
    r1 = v - hi.astype(F32)
    mid = r1.astype(BF16)
    lo = (r1 - mid.astype(F32)).astype(BF16)
    return hi, mid, lo


def _tri_dot(tri, v):
    hi, mid, lo = _split3(v)
    return _dot(tri, hi, NN) + _dot(tri, mid, NN) + _dot(tri, lo, NN)


def _gate_cumsum(fraw, b_pad, name):
    s = fraw.shape[0]

    def body(f_ref, b_ref, flog_ref, cum_ref, carry_ref):
        i = pl.program_id(0)

        @pl.when(i == 0)
        def _():
            carry_ref[...] = jnp.zeros_like(carry_ref)

        flog = f_ref[...] + b_ref[...]
        flog_ref[...] = flog
        lf = jnp.minimum(flog, 0.0) - jnp.log(1.0 + jnp.exp(-jnp.abs(flog)))
        lane = lax.broadcasted_iota(jnp.int32, (1, 128), 1)
        lf = jnp.where(lane < FOX_HEADS, lf, 0.0)
        row = lax.broadcasted_iota(jnp.int32, (TM, TM), 0)
        col = lax.broadcasted_iota(jnp.int32, (TM, TM), 1)
        tri = (row >= col).astype(BF16)
        cum = _tri_dot(tri, lf) + carry_ref[...]
        cum_ref[...] = cum
        carry_ref[...] = cum[TM - 1:TM, :]

    return pl.pallas_call(
        body, name=name, grid=(s // TM,),
        in_specs=[_row_spec(TM, 128), _vec_spec(128)],
        out_specs=[_row_spec(TM, 128), _row_spec(TM, 128)],
        out_shape=[jax.ShapeDtypeStruct((s, 128), F32), jax.ShapeDtypeStruct((s, 128), F32)],
        scratch_shapes=[pltpu.VMEM((1, 128), F32)],
        compiler_params=_cp("arbitrary"),
    )(fraw, b_pad)


def _gate_bwd(qaux, kaux, flog, name):
    s = flog.shape[0]
    n = s // TM

    def body(qa_ref, ka_ref, fl_ref, dp_ref, db_ref, carry_ref):
        i = pl.program_id(0)

        @pl.when(i == 0)
        def _():
            carry_ref[...] = jnp.zeros_like(carry_ref)

        src = lax.broadcasted_iota(jnp.int32, (128, 128), 0)
        dst = lax.broadcasted_iota(jnp.int32, (128, 128), 1)
        dcum = jnp.zeros((TM, 128), F32)
        for p in range(HEAD_PAIRS):
            for ref, l0, l1, sign in ((qa_ref, 64, 0, 1.0), (ka_ref, 67, 3, -1.0)):
                hit = jnp.logical_or(jnp.logical_and(src == l0, dst == 2 * p),
                                     jnp.logical_and(src == l1, dst == 2 * p + 1))
                sel = jnp.where(hit, sign, 0.0).astype(BF16)
                for piece in _split3(ref[p]):
                    dcum = dcum + _dot(piece, sel, NN)
        row = lax.broadcasted_iota(jnp.int32, (TM, TM), 0)
        col = lax.broadcasted_iota(jnp.int32, (TM, TM), 1)
        tri = (row <= col).astype(BF16)
        dlf = _tri_dot(tri, dcum) + carry_ref[...]
        carry_ref[...] = dlf[0:1, :]
        lane = lax.broadcasted_iota(jnp.int32, (1, 128), 1)
        df = jnp.where(lane < FOX_HEADS, dlf / (1.0 + jnp.exp(fl_ref[...])), 0.0)
        dp_ref[...] = df.astype(BF16)
        db = jnp.sum(df, axis=0, keepdims=True)

        @pl.when(i == 0)
        def _():
            db_ref[...] = db

        @pl.when(i > 0)
        def _():
            db_ref[...] += db

    rev = lambda i: (n - 1 - i, 0)
    return pl.pallas_call(
        body, name=name, grid=(n,),
        in_specs=[pl.BlockSpec((HEAD_PAIRS, TM, 128), lambda i: (0, n - 1 - i, 0)),
                  pl.BlockSpec((HEAD_PAIRS, TM, 128), lambda i: (0, n - 1 - i, 0)), pl.BlockSpec((TM, 128), rev)],
        out_specs=[pl.BlockSpec((TM, 128), rev), _vec_spec(128)],
        out_shape=[jax.ShapeDtypeStruct((s, 128), BF16), jax.ShapeDtypeStruct((1, 128), F32)],
        scratch_shapes=[pltpu.VMEM((1, 128), F32)],
        compiler_params=_cp("arbitrary"),
    )(qaux, kaux, flog)


def _pool_consts(i, rows):
    lane = lax.broadcasted_iota(jnp.int32, (rows, D_POOL), 1)
    t1 = lax.broadcasted_iota(jnp.int32, (rows, D_POOL), 0) + i * TM + 1
    win = jnp.where(lane < 64, 2, jnp.where(lane < 128, 4, jnp.where(lane < 192, 8, 16)))
    inv = 1.0 / jnp.minimum(t1, win).astype(F32)
    return lane, inv


def _by_group(lane, s2, s4, s8, s16):
    return jnp.where(lane < 64, s2, jnp.where(lane < 128, s4, jnp.where(lane < 192, s8, s16)))


def _pool_diff(i, u_ref, halo_ref):
    u = u_ref[...].astype(F32)
    halo = jnp.where(i > 0, halo_ref[...].astype(F32), 0.0)
    ext = jnp.concatenate([halo, u], axis=0)
    s2 = ext + pltpu.roll(ext, 1, 0)
    s4 = s2 + pltpu.roll(s2, 2, 0)
    s8 = s4 + pltpu.roll(s4, 4, 0)
    s16 = s8 + pltpu.roll(s8, 8, 0)
    lane, inv = _pool_consts(i, TM)
    sel = _by_group(lane, s2[POOL_HALO:], s4[POOL_HALO:], s8[POOL_HALO:], s16[POOL_HALO:])
    return sel * inv - u


def _pool_fwd(proj, wbd, scale, ycat, name):
    s = proj.shape[0]
    hb = TM // POOL_HALO

    def body(u_ref, halo_ref, w_ref, sc_ref, y_any, y_ref):
        del y_any
        i = pl.program_id(0)
        diff = _pool_diff(i, u_ref, halo_ref)
        mixed = _dot(diff.astype(BF16), w_ref[...], NN)
        y_ref[...] = (mixed * sc_ref[...]).astype(BF16)

    return pl.pallas_call(
        body, name=name, grid=(s // TM,),
        in_specs=[pl.BlockSpec((TM, D_POOL), lambda i: (i, 0)),
                  pl.BlockSpec((POOL_HALO, D_POOL), lambda i: (jnp.maximum(i * hb - 1, 0), 0)),
                  pl.BlockSpec((D_POOL, D_POOL), lambda i: (0, 0)), _vec_spec(D_POOL),
                  pl.BlockSpec(memory_space=pl.ANY)],
        out_specs=pl.BlockSpec((TM, D_POOL), lambda i: (i, 0)),
        out_shape=jax.ShapeDtypeStruct(ycat.shape, ycat.dtype),
        input_output_aliases={4: 0},
        compiler_params=_cp("parallel"),
    )(proj, proj, wbd, scale, ycat)


def _pool_bwd(proj, dycat, wbd, scale, name):
    s = proj.shape[0]
    n = s // TM
    hb = TM // POOL_HALO
    last_halo = s // POOL_HALO - 1

    def body(u_ref, halo_ref, dy_ref, dyp_ref, w_ref, sc_ref, dp_ref, dw_ref, dsc_ref):
        i = pl.program_id(0)
        diff = _pool_diff(i, u_ref, halo_ref)
        diff_b = diff.astype(BF16)
        mixed = _dot(diff_b, w_ref[...], NN)
        dy = dy_ref[...].astype(F32)
        dmix = (dy * sc_ref[...]).astype(BF16)
        dyp = jnp.where(i < n - 1, dyp_ref[...].astype(F32), 0.0)
        dmix_p = (dyp * sc_ref[...]).astype(BF16)
        dd = _dot(dmix, w_ref[...], NT)
        dd_p = _dot(dmix_p, w_ref[...], NT)
        lane, inv = _pool_consts(i, TM)
        _, inv_p = _pool_consts(i + 1, POOL_HALO)
        ext = jnp.concatenate([dd * inv, dd_p * inv_p], axis=0)
        rows = TM + POOL_HALO
        l2 = ext + pltpu.roll(ext, rows - 1, 0)
        l4 = l2 + pltpu.roll(l2, rows - 2, 0)
        l8 = l4 + pltpu.roll(l4, rows - 4, 0)
        l16 = l8 + pltpu.roll(l8, rows - 8, 0)
        du = _by_group(lane, l2[:TM], l4[:TM], l8[:TM], l16[:TM]) - dd
        dp_ref[...] = du.astype(BF16)
        dw = _dot(diff_b, dmix, TN)
        dsc = jnp.sum(dy * mixed, axis=0, keepdims=True)

        @pl.when(i == 0)
        def _():
            dw_ref[...] = dw
            dsc_ref[...] = dsc

        @pl.when(i > 0)
        def _():
            dw_ref[...] += dw
            dsc_ref[...] += dsc

    return pl.pallas_call(
        body, name=name, grid=(n,),
        in_specs=[pl.BlockSpec((TM, D_POOL), lambda i: (i, 0)),
                  pl.BlockSpec((POOL_HALO, D_POOL), lambda i: (jnp.maximum(i * hb - 1, 0), 0)),
                  pl.BlockSpec((TM, D_POOL), lambda i: (i, 0)),
                  pl.BlockSpec((POOL_HALO, D_POOL), lambda i: (jnp.minimum((i + 1) * hb, last_halo), 0)),
                  pl.BlockSpec((D_POOL, D_POOL), lambda i: (0, 0)), _vec_spec(D_POOL)],
        out_specs=[pl.BlockSpec((TM, D_POOL), lambda i: (i, 0)),
                   pl.BlockSpec((D_POOL, D_POOL), lambda i: (0, 0)), _vec_spec(D_POOL)],
        out_shape=[jax.ShapeDtypeStruct((s, D_POOL), BF16),
                   jax.ShapeDtypeStruct((D_POOL, D_POOL), F32), jax.ShapeDtypeStruct((1, D_POOL), F32)],
        compiler_params=_cp("arbitrary"),
    )(proj, proj, dycat, dycat, wbd, scale)


Q_BLK = D_POOL // 128
K_BLK = Q_BLK + D_FOX // 128
V_BLK = K_BLK + D_FOX // 128


def _operand_lanes(v0, v1, ones_off):
    lane = lax.broadcasted_iota(jnp.int32, (1, 128), 1)
    out = jnp.zeros((v0.shape[0], 128), F32)
    if ones_off is not None:
        half = lane & 63
        out = out + jnp.where(jnp.logical_and(half >= ones_off, half < ones_off + 3), 1.0, 0.0)
    for base, v in ((64, v0), (0, v1)):
        for j, piece in enumerate(_split3(v)):
            out = jnp.where(lane == base + j, piece.astype(F32), out)
    return out.astype(BF16)


def _operand_rows(v0, v1, ones_off):
    row = lax.broadcasted_iota(jnp.int32, (128, 1), 0)
    half = row & 63
    out = jnp.where(jnp.logical_and(half >= ones_off, half < ones_off + 3), 1.0, 0.0) + jnp.zeros_like(v0)
    for base, v in ((64, v0), (0, v1)):
        for j, piece in enumerate(_split3(v)):
            out = jnp.where(row == base + j, piece.astype(F32), out)
    return out


def _fox_operands(cum, name):
    s = cum.shape[0]
    width = HEAD_PAIRS * 128

    def body(c_ref, aq_ref, ak_ref):
        pieces = _split3(c_ref[...])
        row = lax.broadcasted_iota(jnp.int32, (128, width), 0)
        col = lax.broadcasted_iota(jnp.int32, (128, width), 1)
        base = (row >> 1) * 128 + (1 - (row & 1)) * 64
        half = lax.broadcasted_iota(jnp.int32, (1, width), 1) & 63
        for o_ref, off, sign, ones_off in ((aq_ref, 0, 1.0, 3), (ak_ref, 3, -1.0, 0)):
            out = jnp.where(jnp.logical_and(half >= ones_off, half < ones_off + 3), 1.0, 0.0)
            for j, piece in enumerate(pieces):
                sel = jnp.where(jnp.logical_and(col == base + off + j, row < FOX_HEADS), sign, 0.0).astype(BF16)
                out = out + _dot(piece, sel, NN)
            o_ref[...] = out.astype(BF16)

    return pl.pallas_call(
        body, name=name, grid=(s // TM,), in_specs=[_row_spec(TM, 128)],
        out_specs=[_row_spec(TM, width), _row_spec(TM, width)],
        out_shape=[jax.ShapeDtypeStruct((s, width), BF16)] * 2,
        compiler_params=_cp("parallel"),
    )(cum)


def _fox_do_operand(dycat, ycat, after, name):
    s = dycat.shape[0]

    def body(do_ref, o_ref, after_ref, ad_ref):
        del after_ref
        lane = lax.broadcasted_iota(jnp.int32, (1, 128), 1)
        dd = do_ref[...].astype(F32) * o_ref[...].astype(F32)
        d0 = jnp.sum(jnp.where(lane < 64, dd, 0.0), axis=1, keepdims=True)
        d1 = jnp.sum(dd, axis=1, keepdims=True) - d0
        ad_ref[...] = _operand_lanes(-d0, -d1, None)

    blk = pl.BlockSpec((TM, 128), lambda i, p: (i, Q_BLK + p))
    return pl.pallas_call(
        body, name=name, grid=(s // TM, HEAD_PAIRS), in_specs=[blk, blk, pl.BlockSpec(memory_space=pl.ANY)],
        out_specs=pl.BlockSpec((TM, 128), lambda i, p: (i, p)),
        out_shape=jax.ShapeDtypeStruct((s, HEAD_PAIRS * 128), BF16),
        compiler_params=_cp("parallel", "parallel"),
    )(dycat, ycat, after)


def _causal_pairs(nq, key_major):
    if key_major:
        pairs = [(q, k) for k in range(nq) for q in range(k, nq)]
    else:
        pairs = [(q, k) for q in range(nq) for k in range(q + 1)]
    return (jnp.asarray([p[0] for p in pairs], jnp.int32), jnp.asarray([p[1] for p in pairs], jnp.int32))


def _diag_mask(sc):
    row = lax.broadcasted_iota(jnp.int32, sc.shape, 0)
    col = lax.broadcasted_iota(jnp.int32, sc.shape, 1)
    return jnp.where(row >= col, sc, NEG)


def _fox_fwd(proj, aq, ak, name):
    s = proj.shape[0]
    nq = s // TQ
    qi_arr, ki_arr = _causal_pairs(nq, key_major=False)

    def body(qi_ref, ki_ref, q_ref, k_ref, v_ref, aq_ref, ak_ref, o_ref, aqb_ref, m0_ref, m1_ref, acc_ref, aux_ref):
        t = pl.program_id(1)
        qi, ki = qi_ref[t], ki_ref[t]
        lane = lax.broadcasted_iota(jnp.int32, (1, 128), 1)
        masks = [lane < 64, lane >= 64]
        ones_v = jnp.where((lane & 63) == 8, 1.0, 0.0).astype(BF16)
        top = lax.broadcasted_iota(jnp.int32, (128, 1), 0) < 64
        m_ref = [m0_ref, m1_ref]

        @pl.when(ki == 0)
        def _():
            m0_ref[...] = jnp.full_like(m0_ref, NEG)
            m1_ref[...] = jnp.full_like(m1_ref, NEG)
            acc_ref[...] = jnp.zeros_like(acc_ref)
            aux_ref[...] = jnp.zeros_like(aux_ref)

        def step(diag):
            q2s = q_ref[...] * 0.125
            k2, v2, aq2, ak2 = k_ref[...], v_ref[...], aq_ref[...], ak_ref[...]
            pv, alpha = [], []
            for hh in range(2):
                qh = jnp.where(masks[hh], q2s, aq2)
                kh = jnp.where(masks[hh], k2, ak2)
                vh = jnp.where(masks[hh], v2, ones_v)
                sc = _dot(kh, qh, NT)
                if diag:
                    key = lax.broadcasted_iota(jnp.int32, sc.shape, 0)
                    qry = lax.broadcasted_iota(jnp.int32, sc.shape, 1)
                    sc = jnp.where(qry >= key, sc, NEG)
                m_prev = m_ref[hh][...]
                m_new = jnp.maximum(m_prev, jnp.max(sc, axis=0, keepdims=True))
                m_ref[hh][...] = m_new
                alpha.append(jnp.exp(m_prev - m_new))
                pv.append(_dot(vh, jnp.exp(sc - m_new).astype(BF16), TN))
            acc_ref[...] = acc_ref[...] * jnp.where(top, alpha[0], alpha[1]) + jnp.where(top, pv[0], pv[1])
            aux_ref[...] = aux_ref[...] * jnp.where(top, alpha[1], alpha[0]) + jnp.where(top, pv[1], pv[0])

        @pl.when(ki < qi)
        def _():
            step(False)

        @pl.when(ki == qi)
        def _():
            step(True)
            aux = aux_ref[...]
            l0, l1 = aux[72:73, :], aux[8:9, :]
            o_ref[...] = (acc_ref[...] * jnp.where(top, 1.0 / l0, 1.0 / l1)).T.astype(BF16)
            aqt = aq_ref[...].astype(F32).T
            cum0 = aqt[64:65, :] + aqt[65:66, :] + aqt[66:67, :]
            cum1 = aqt[0:1, :] + aqt[1:2, :] + aqt[2:3, :]
            aqb = _operand_rows(cum0 - (m0_ref[...] + jnp.log(l0)), cum1 - (m1_ref[...] + jnp.log(l1)), 3)
            aqb_ref[...] = aqb.T.astype(BF16)

    grid_spec = pltpu.PrefetchScalarGridSpec(
        num_scalar_prefetch=2, grid=(HEAD_PAIRS, int(qi_arr.shape[0])),
        in_specs=[pl.BlockSpec((TQ, 128), lambda p, t, qi, ki: (qi[t], Q_BLK + p)),
                  pl.BlockSpec((TQ, 128), lambda p, t, qi, ki: (ki[t], K_BLK + p)),
                  pl.BlockSpec((TQ, 128), lambda p, t, qi, ki: (ki[t], V_BLK + p)),
                  pl.BlockSpec((TQ, 128), lambda p, t, qi, ki: (qi[t], p)),
                  pl.BlockSpec((TQ, 128), lambda p, t, qi, ki: (ki[t], p))],
        out_specs=[pl.BlockSpec((TQ, 128), lambda p, t, qi, ki: (qi[t], Q_BLK + p)),
                   pl.BlockSpec((TQ, 128), lambda p, t, qi, ki: (qi[t], p))],
        scratch_shapes=[pltpu.VMEM((1, TQ), F32), pltpu.VMEM((1, TQ), F32),
                        pltpu.VMEM((128, TQ), F32), pltpu.VMEM((128, TQ), F32)])
    return pl.pallas_call(
        body, name=name, grid_spec=grid_spec,
        out_shape=[jax.ShapeDtypeStruct((s, D_MODEL), BF16), jax.ShapeDtypeStruct((s, HEAD_PAIRS * 128), BF16)],
        compiler_params=_cp("parallel", "arbitrary"),
    )(qi_arr, ki_arr, proj, proj, proj, aq, ak)


def _fox_bwd(proj, dycat, aqb, ak, ad, name):
    s = proj.shape[0]
    nq = s // TQ
    qi_arr, ki_arr = _causal_pairs(nq, key_major=True)

    def body(qi_ref, ki_ref, q_ref, k_ref, v_ref, do_ref, aq_ref, ak_ref, ad_ref,
             dq_ref, dk_ref, dv_ref, qaux_ref, kaux_ref, dq_acc, qaux_acc, dk_acc, dv_acc, kaux_acc):
        t = pl.program_id(1)
        qi, ki = qi_ref[t], ki_ref[t]
        lane = lax.broadcasted_iota(jnp.int32, (1, 128), 1)
        masks = [lane < 64, lane >= 64]
        ones_v = jnp.where((lane & 63) < 3, 1.0, 0.0).astype(BF16)
        top = lax.broadcasted_iota(jnp.int32, (128, 1), 0) < 64

        @pl.when(qi == ki)
        def _():
            dk_acc[...] = jnp.zeros_like(dk_acc)
            dv_acc[...] = jnp.zeros_like(dv_acc)
            kaux_acc[...] = jnp.zeros_like(kaux_acc)

        def step(diag):
            q2s = q_ref[...] * 0.125
            k2, v2, do2 = k_ref[...], v_ref[...], do_ref[...]
            aq2, ak2, ad2 = aq_ref[...], ak_ref[...], ad_ref[...]
            dq, dk, dv = [], [], []
            for hh in range(2):
                qh = jnp.where(masks[hh], q2s, aq2)
                kh = jnp.where(masks[hh], k2, ak2)
                doh = jnp.where(masks[hh], do2, ad2)
                vh = jnp.where(masks[hh], v2, ones_v)
                sc = _dot(kh, qh, NT)
                if diag:
                    key = lax.broadcasted_iota(jnp.int32, sc.shape, 0)
                    qry = lax.broadcasted_iota(jnp.int32, sc.shape, 1)
                    sc = jnp.where(qry >= key, sc, NEG)
                p = jnp.exp(sc)
                dsb = (p * _dot(vh, doh, NT)).astype(BF16)
                dv.append(_dot(p.astype(BF16), doh, NN))
                dk.append(_dot(dsb, qh, NN))
                dq.append(_dot(kh, dsb, TN))
            dk_acc[...] += jnp.where(masks[0], dk[0], dk[1])
            kaux_acc[...] += jnp.where(masks[0], dk[1], dk[0])
            dv_acc[...] += jnp.where(masks[0], dv[0], dv[1])
            dq_new = jnp.where(top, dq[0], dq[1])
            qaux_new = jnp.where(top, dq[1], dq[0])

            @pl.when(ki == 0)
            def _():
                dq_acc[qi] = dq_new
                qaux_acc[qi] = qaux_new

            @pl.when(ki > 0)
            def _():
                dq_acc[qi] += dq_new
                qaux_acc[qi] += qaux_new

        @pl.when(qi > ki)
        def _():
            step(False)

        @pl.when(qi == ki)
        def _():
            step(True)
            rows = pl.ds(pl.multiple_of(qi * TQ, TQ), TQ)
            dq_ref[rows, :] = (dq_acc[qi] * 0.125).T.astype(BF16)
            qaux_ref[rows, :] = qaux_acc[qi].T

        @pl.when(qi == nq - 1)
        def _():
            dk_ref[...] = dk_acc[...].astype(BF16)
            dv_ref[...] = dv_acc[...].astype(BF16)
            kaux_ref[...] = kaux_acc[...]

    grid_spec = pltpu.PrefetchScalarGridSpec(
        num_scalar_prefetch=2, grid=(HEAD_PAIRS, int(qi_arr.shape[0])),
        in_specs=[pl.BlockSpec((TQ, 128), lambda p, t, qi, ki: (qi[t], Q_BLK + p)),
                  pl.BlockSpec((TQ, 128), lambda p, t, qi, ki: (ki[t], K_BLK + p)),
                  pl.BlockSpec((TQ, 128), lambda p, t, qi, ki: (ki[t], V_BLK + p)),
                  pl.BlockSpec((TQ, 128), lambda p, t, qi, ki: (qi[t], Q_BLK + p)),
                  pl.BlockSpec((TQ, 128), lambda p, t, qi, ki: (qi[t], p)),
                  pl.BlockSpec((TQ, 128), lambda p, t, qi, ki: (ki[t], p)),
                  pl.BlockSpec((TQ, 128), lambda p, t, qi, ki: (qi[t], p))],
        out_specs=[pl.BlockSpec((s, 128), lambda p, t, qi, ki: (0, p)),
                   pl.BlockSpec((TQ, 128), lambda p, t, qi, ki: (ki[t], p)),
                   pl.BlockSpec((TQ, 128), lambda p, t, qi, ki: (ki[t], p)),
                   pl.BlockSpec((None, s, 128), lambda p, t, qi, ki: (p, 0, 0)),
                   pl.BlockSpec((None, TQ, 128), lambda p, t, qi, ki: (p, ki[t], 0))],
        scratch_shapes=[pltpu.VMEM((nq, 128, TQ), F32), pltpu.VMEM((nq, 128, TQ), F32),
                        pltpu.VMEM((TQ, 128), F32), pltpu.VMEM((TQ, 128), F32), pltpu.VMEM((TQ, 128), F32)])
    return pl.pallas_call(
        body, name=name, grid_spec=grid_spec,
        out_shape=[jax.ShapeDtypeStruct((s, D_FOX), BF16)] * 3 + [jax.ShapeDtypeStruct((HEAD_PAIRS, s, 128), F32)] * 2,
        compiler_params=_cp("arbitrary", "arbitrary"),
    )(qi_arr, ki_arr, proj, proj, proj, dycat, aqb, ak, ad)


XA_SCALE = XA_DIM ** -0.5


def _xattn_fwd(q2, kv, name):
    s = q2.shape[0]
    m = kv.shape[0]

    def body(q_ref, kv_ref, o_ref):
        for h in range(XA_HEADS):
            c0 = h * XA_DIM
            sc = _dot(q_ref[:, c0:c0 + XA_DIM], kv_ref[:, c0:c0 + XA_DIM], NT) * XA_SCALE
            e = jnp.exp(sc - jnp.max(sc, axis=1, keepdims=True))
            p = e / jnp.sum(e, axis=1, keepdims=True)
            o_ref[:, c0:c0 + XA_DIM] = _dot(p.astype(BF16), kv_ref[:, D_MODEL + c0:D_MODEL + c0 + XA_DIM], NN).astype(BF16)

    return pl.pallas_call(
        body, name=name, grid=(s // TM,),
        in_specs=[_row_spec(TM, D_MODEL), pl.BlockSpec((m, 2 * D_MODEL), lambda i: (0, 0))],
        out_specs=_row_spec(TM, D_MODEL), out_shape=jax.ShapeDtypeStruct((s, D_MODEL), BF16),
        compiler_params=_cp("parallel"),
    )(q2, kv)


def _xattn_bwd(q2, kv, do2, name):
    s = q2.shape[0]
    m = kv.shape[0]

    def body(q_ref, kv_ref, do_ref, dq_ref, dkv_ref):
        i = pl.program_id(0)

        @pl.when(i == 0)
        def _():
            dkv_ref[...] = jnp.zeros_like(dkv_ref)

        for h in range(XA_HEADS):
            c0 = h * XA_DIM
            v0 = D_MODEL + c0
            qh = q_ref[:, c0:c0 + XA_DIM]
            kh = kv_ref[:, c0:c0 + XA_DIM]
            doh = do_ref[:, c0:c0 + XA_DIM]
            sc = _dot(qh, kh, NT) * XA_SCALE
            e = jnp.exp(sc - jnp.max(sc, axis=1, keepdims=True))
            p = e / jnp.sum(e, axis=1, keepdims=True)
            dp = _dot(doh, kv_ref[:, v0:v0 + XA_DIM], NT)
            ds = p * (dp - jnp.sum(p * dp, axis=1, keepdims=True))
            dsb = (ds * XA_SCALE).astype(BF16)
            dq_ref[:, c0:c0 + XA_DIM] = _dot(dsb, kh, NN).astype(BF16)
            dkv_ref[:, c0:c0 + XA_DIM] += _dot(dsb, qh, TN)
            dkv_ref[:, v0:v0 + XA_DIM] += _dot(p.astype(BF16), doh, TN)

    return pl.pallas_call(
        body, name=name, grid=(s // TM,),
        in_specs=[_row_spec(TM, D_MODEL), pl.BlockSpec((m, 2 * D_MODEL), lambda i: (0, 0)), _row_spec(TM, D_MODEL)],
        out_specs=[_row_spec(TM, D_MODEL), pl.BlockSpec((m, 2 * D_MODEL), lambda i: (0, 0))],
        out_shape=[jax.ShapeDtypeStruct((s, D_MODEL), BF16), jax.ShapeDtypeStruct((m, 2 * D_MODEL), F32)],
        compiler_params=_cp("arbitrary"),
    )(q2, kv, do2)


GELU_C = math.sqrt(2.0 / math.pi)
GELU_A = 0.044715


def _gelu(x):
    return 0.5 * x * (1.0 + jnp.tanh(GELU_C * (x + GELU_A * x * x * x)))


def _gelu_and_grad(x):
    t = jnp.tanh(GELU_C * (x + GELU_A * x * x * x))
    g = 0.5 * x * (1.0 + t)
    dg = 0.5 * (1.0 + t) + 0.5 * x * (1.0 - t * t) * GELU_C * (1.0 + 3.0 * GELU_A * x * x)
    return g, dg


def _shift_down(main, prev8):
    n = main.shape[0]
    if n <= 8:
        row = lax.broadcasted_iota(jnp.int32, main.shape, 0)
        s1 = jnp.where(row == 0, prev8[7:8, :], pltpu.roll(main, 1, 0))
        s2 = jnp.where(row == 0, prev8[6:7, :], jnp.where(row == 1, prev8[7:8, :], pltpu.roll(main, 2, 0)))
        return s1, s2
    row = lax.broadcasted_iota(jnp.int32, (8, main.shape[1]), 0)
    r1, r2 = pltpu.roll(main, 1, 0), pltpu.roll(main, 2, 0)
    h1 = jnp.where(row == 0, prev8[7:8, :], r1[:8])
    h2 = jnp.where(row == 0, prev8[6:7, :], jnp.where(row == 1, prev8[7:8, :], r2[:8]))
    return jnp.concatenate([h1, r1[8:]], axis=0), jnp.concatenate([h2, r2[8:]], axis=0)


def _shift_up(main, next8):
    n = main.shape[0]
    row = lax.broadcasted_iota(jnp.int32, (8, main.shape[1]), 0)
    r1, r2 = pltpu.roll(main, n - 1, 0), pltpu.roll(main, n - 2, 0)
    t1 = jnp.where(row == 7, next8[0:1, :], r1[n - 8:])
    t2 = jnp.where(row == 6, next8[0:1, :], jnp.where(row == 7, next8[1:2, :], r2[n - 8:]))
    return jnp.concatenate([r1[:n - 8], t1], axis=0), jnp.concatenate([r2[:n - 8], t2], axis=0)


def _conv(h, s1, s2, w_ref, b_ref):
    return w_ref[0:1, :] * s2 + w_ref[1:2, :] * s1 + w_ref[2:3, :] * h + b_ref[...]


def _ffn_fwd(h3, w_up, cw, cb, w_down, x2, tgt, g_post, name):
    s = h3.shape[0]
    tn = TN_FF
    nj = D_FF // tn
    per = D_MODEL // tn
    hb = TM // 8

    def body(h_ref, halo_ref, wg_ref, wu_ref, cwg_ref, cwu_ref, cbg_ref, cbu_ref, wd_ref, x_ref, t_ref, g_ref,
             hg_ref, hu_ref, a_ref, loss_ref, dx_ref, dy_ref, dg_ref, y_acc):
        i, j = pl.program_id(0), pl.program_id(1)
        h = h_ref[...]
        halo = halo_ref[...]
        halo = jnp.where(i > 0, halo, jnp.zeros_like(halo))
        conv = []
        for w_ref, cw_ref, cb_ref, hid_ref in ((wg_ref, cwg_ref, cbg_ref, hg_ref), (wu_ref, cwu_ref, cbu_ref, hu_ref)):
            hm = _dot(h, w_ref[...], NN)
            hid_ref[...] = hm.astype(BF16)
            hl = _dot(halo, w_ref[...], NN)
            s1, s2 = _shift_down(hm, hl)
            conv.append(_conv(hm, s1, s2, cw_ref, cb_ref))
        a = (_gelu(conv[0]) * conv[1]).astype(BF16)
        a_ref[...] = a
        contrib = _dot(a, wd_ref[...], NN)

        @pl.when(j == 0)
        def _():
            y_acc[...] = contrib

        @pl.when(j > 0)
        def _():
            y_acc[...] += contrib

        @pl.when(j == nj - 1)
        def _():
            yv = y_acc[...]
            r = _rstd(yv)
            yn = yv * r
            e = x_ref[...] + yn * g_ref[...] - t_ref[...]
            part = 0.5 * jnp.sum(jnp.mean(e * e, axis=-1, keepdims=True), axis=0, keepdims=True)
            part = jnp.broadcast_to(part, (1, 128))
            dx = e * (1.0 / D_MODEL)
            dx_ref[...] = dx
            dy_ref[...] = _norm_bwd_rows(dx * g_ref[...], yn, r).astype(BF16)
            dg = jnp.sum(dx * yn, axis=0, keepdims=True)

            @pl.when(i == 0)
            def _():
                dg_ref[...] = dg
                loss_ref[...] = part

            @pl.when(i > 0)
            def _():
                dg_ref[...] += dg
                loss_ref[...] += part

    rows = pl.BlockSpec((TM, D_MODEL), lambda i, j: (i, 0))
    return pl.pallas_call(
        body, name=name, grid=(s // TM, nj),
        in_specs=[rows,
                  pl.BlockSpec((8, D_MODEL), lambda i, j: (jnp.maximum(i * hb - 1, 0), 0)),
                  pl.BlockSpec((None, D_MODEL, tn), lambda i, j: (j // per, 0, j % per)),
                  pl.BlockSpec((None, D_MODEL, tn), lambda i, j: (NDEV // 2 + j // per, 0, j % per)),
                  pl.BlockSpec((8, tn), lambda i, j: (0, j)),
                  pl.BlockSpec((8, tn), lambda i, j: (0, nj + j)),
                  pl.BlockSpec((1, tn), lambda i, j: (0, j)),
                  pl.BlockSpec((1, tn), lambda i, j: (0, nj + j)),
                  pl.BlockSpec((tn, D_MODEL), lambda i, j: (j, 0)),
                  rows, rows, pl.BlockSpec((1, D_MODEL), lambda i, j: (0, 0))],
        out_specs=[pl.BlockSpec((TM, tn), lambda i, j: (i, j)), pl.BlockSpec((TM, tn), lambda i, j: (i, j)),
                   pl.BlockSpec((TM, tn), lambda i, j: (i, j)),
                   pl.BlockSpec((1, 128), lambda i, j: (0, 0)), rows, rows,
                   pl.BlockSpec((1, D_MODEL), lambda i, j: (0, 0))],
        out_shape=[jax.ShapeDtypeStruct((s, D_FF), BF16), jax.ShapeDtypeStruct((s, D_FF), BF16),
                   jax.ShapeDtypeStruct((s, D_FF), BF16),
                   jax.ShapeDtypeStruct((1, 128), F32), jax.ShapeDtypeStruct((s, D_MODEL), F32),
                   jax.ShapeDtypeStruct((s, D_MODEL), BF16), jax.ShapeDtypeStruct((1, D_MODEL), F32)],
        scratch_shapes=[pltpu.VMEM((TM, D_MODEL), F32)],
        compiler_params=_cp("arbitrary", "arbitrary"),
    )(h3, h3, w_up, w_up, cw, cw, cb, cb, w_down, x2, tgt, g_post)


def _ffn_bwd(dy3, w_down, hid_g, hid_u, cw, cb, name):
    s = dy3.shape[0]
    n = s // TM
    tn = TN_FF
    nj = D_FF // tn
    hb = TM // 8
    last8 = s // 8 - 1

    def body(dy_ref, dyp_ref, wd_ref, hg_ref, hgl_ref, hgn_ref, hu_ref, hul_ref, hun_ref,
             cwg_ref, cwu_ref, cbg_ref, cbu_ref,
             dhg_ref, dhu_ref, dcwg_ref, dcwu_ref, dcbg_ref, dcbu_ref):
        i = pl.program_id(1)
        first, last = i == 0, i == n - 1
        da = _dot(dy_ref[...], wd_ref[...], NT)
        dyp = dyp_ref[...]
        dyp = jnp.where(last, jnp.zeros_like(dyp), dyp)
        da_n = _dot(dyp, wd_ref[...], NT)
        parts = []
        for h_ref, hl_ref, hn_ref, cw_ref, cb_ref in ((hg_ref, hgl_ref, hgn_ref, cwg_ref, cbg_ref),
                                                     (hu_ref, hul_ref, hun_ref, cwu_ref, cbu_ref)):
            hm = h_ref[...].astype(F32)
            hl = jnp.where(first, 0.0, hl_ref[...].astype(F32))
            hn = hn_ref[...].astype(F32)
            s1, s2 = _shift_down(hm, hl)
            c = _conv(hm, s1, s2, cw_ref, cb_ref)
            n1, n2 = _shift_down(hn, hm[TM - 8:, :])
            cn = _conv(hn, n1, n2, cw_ref, cb_ref)
            parts.append((hm, s1, s2, c, cn))
        g, dg = _gelu_and_grad(parts[0][3])
        gn, dgn = _gelu_and_grad(parts[0][4])
        dc_g = da * parts[1][3] * dg
        dc_u = da * g
        dcn_g = da_n * parts[1][4] * dgn
        dcn_u = da_n * gn
        outs = ((dc_g, dcn_g, parts[0], cwg_ref, dhg_ref, dcwg_ref, dcbg_ref),
                (dc_u, dcn_u, parts[1], cwu_ref, dhu_ref, dcwu_ref, dcbu_ref))
        for dc, dcn, (hm, s1, s2, _, _), cw_ref, dh_ref, dcw_ref, dcb_ref in outs:
            u1, u2 = _shift_up(dc, dcn)
            dh_ref[...] = (cw_ref[2:3, :] * dc + cw_ref[1:2, :] * u1 + cw_ref[0:1, :] * u2).astype(BF16)
            dcb = jnp.sum(dc, axis=0, keepdims=True)
            row8 = lax.broadcasted_iota(jnp.int32, (8, tn), 0)
            dcw = jnp.where(row8 == 0, jnp.sum(dc * s2, axis=0, keepdims=True),
                            jnp.where(row8 == 1, jnp.sum(dc * s1, axis=0, keepdims=True),
                                      jnp.where(row8 == 2, jnp.sum(dc * hm, axis=0, keepdims=True), 0.0)))

            @pl.when(first)
            def _():
                dcw_ref[...] = dcw
                dcb_ref[...] = dcb

            @pl.when(i > 0)
            def _():
                dcw_ref[...] += dcw
                dcb_ref[...] += dcb

    prev8 = lambda j, i: (jnp.maximum(i * hb - 1, 0), j)
    next8 = lambda j, i: (jnp.minimum((i + 1) * hb, last8), j)
    blk = lambda j, i: (i, j)
    col = lambda j, i: (0, j)
    colu = lambda j, i: (0, nj + j)
    return pl.pallas_call(
        body, name=name, grid=(nj, n),
        in_specs=[pl.BlockSpec((TM, D_MODEL), lambda j, i: (i, 0)),
                  pl.BlockSpec((8, D_MODEL), lambda j, i: (jnp.minimum((i + 1) * hb, last8), 0)),
                  pl.BlockSpec((tn, D_MODEL), lambda j, i: (j, 0)),
                  pl.BlockSpec((TM, tn), blk), pl.BlockSpec((8, tn), prev8), pl.BlockSpec((8, tn), next8),
                  pl.BlockSpec((TM, tn), blk), pl.BlockSpec((8, tn), prev8), pl.BlockSpec((8, tn), next8),
                  pl.BlockSpec((8, tn), col), pl.BlockSpec((8, tn), colu),
                  pl.BlockSpec((1, tn), col), pl.BlockSpec((1, tn), colu)],
        out_specs=[pl.BlockSpec((TM, tn), blk), pl.BlockSpec((TM, tn), blk),
                   pl.BlockSpec((8, tn), col), pl.BlockSpec((8, tn), col),
                   pl.BlockSpec((1, tn), col), pl.BlockSpec((1, tn), col)],
        out_shape=[jax.ShapeDtypeStruct((s, D_FF), BF16), jax.ShapeDtypeStruct((s, D_FF), BF16),
                   jax.ShapeDtypeStruct((8, D_FF), F32), jax.ShapeDtypeStruct((8, D_FF), F32),
                   jax.ShapeDtypeStruct((1, D_FF), F32), jax.ShapeDtypeStruct((1, D_FF), F32)],
        compiler_params=_cp("parallel", "arbitrary"),
    )(dy3, dy3, w_down, hid_g, hid_g, hid_g, hid_u, hid_u, hid_u, cw, cw, cb, cb)


def _slot(p):
    return 4 * p[0] + 2 * p[1] + p[2]


def _all_gather(shards, name):
    n = len(shards)

    def body(*refs):
        ins, outs = refs[:n], refs[n:2 * n]
        send_sems, recv_sems, local_sems = refs[2 * n:]
        x, y, c = lax.axis_index("x"), lax.axis_index("y"), lax.axis_index("c")
        me, sibling = (x, y, c), (x, y, 1 - c)
        chips = [(1 - x, y), (x, 1 - y), (1 - x, 1 - y)]

        def copy(a, k, block, to, from_input=False):
            dst = outs[a].at[_slot(block)]
            return pltpu.make_async_remote_copy(
                src_ref=ins[a] if from_input else dst, dst_ref=dst,
                send_sem=send_sems.at[a, k], recv_sem=recv_sems.at[a, k],
                device_id=to, device_id_type=MESH)

        mine = [pltpu.make_async_copy(ins[a], outs[a].at[_slot(me)], local_sems.at[a]) for a in range(n)]
        for cp in mine:
            cp.start()
        first = []
        for a in range(n):
            first.append(copy(a, 0, me, sibling, True))
            first += [copy(a, 1 + j, me, (*chip, c), True) for j, chip in enumerate(chips)]
        for cp in first:
            cp.start()
        passed = []
        for j, chip in enumerate(chips):
            for a in range(n):
                copy(a, 1 + j, (*chip, c), me).wait_recv()
                fwd = copy(a, 4 + j, (*chip, c), sibling)
                fwd.start()
                passed.append(fwd)
        for a in range(n):
            copy(a, 0, sibling, me).wait_recv()
            for j, chip in enumerate(chips):
                copy(a, 4 + j, (*chip, 1 - c), me).wait_recv()
        for cp in first + passed:
            cp.wait_send()
        for cp in mine:
            cp.wait()

    any_spec = pl.BlockSpec(memory_space=pl.ANY)
    return pl.pallas_call(
        body, name=name,
        in_specs=[any_spec] * n, out_specs=[any_spec] * n,
        out_shape=[jax.ShapeDtypeStruct((NDEV,) + s.shape, s.dtype) for s in shards],
        scratch_shapes=[pltpu.SemaphoreType.DMA((n, 7)), pltpu.SemaphoreType.DMA((n, 7)),
                        pltpu.SemaphoreType.DMA((n,))],
    )(*shards)


def _peer_list(x, y, c):
    return [(1 - x if m & 4 else x, 1 - y if m & 2 else y, 1 - c if m & 1 else c) for m in range(1, NDEV)]


def _exchange_copies(src_refs, land_refs, send_sems, recv_sems, gather):
    x, y, c = lax.axis_index("x"), lax.axis_index("y"), lax.axis_index("c")
    me = (x, y, c)
    copies = []
    for m, peer in enumerate(_peer_list(x, y, c)):
        for a in range(len(src_refs)):
            copies.append(pltpu.make_async_remote_copy(
                src_ref=src_refs[a] if gather else src_refs[a].at[_slot(peer)], dst_ref=land_refs[a].at[_slot(me)],
                send_sem=send_sems.at[a * (NDEV - 1) + m], recv_sem=recv_sems.at[a * (NDEV - 1) + m],
                device_id=peer, device_id_type=MESH))
    return copies


def _exchange_start(srcs, lands, after, gather, name):
    n = len(srcs)
    hbm = pl.BlockSpec(memory_space=pltpu.HBM)

    def body(*refs):
        for cp in _exchange_copies(refs[:n], refs[n:2 * n], refs[2 * n + 1], refs[2 * n + 2], gather):
            cp.start()
        token = refs[-1]
        token[...] = jnp.zeros_like(token)

    outs = pl.pallas_call(
        body, name=name,
        out_shape=(pltpu.SemaphoreType.DMA((n * (NDEV - 1),)), pltpu.SemaphoreType.DMA((n * (NDEV - 1),)),
                   *[pltpu.HBM(a.shape, a.dtype) for a in list(srcs) + list(lands)],
                   jax.ShapeDtypeStruct((8, 128), F32)),
        in_specs=[hbm] * (2 * n) + [pl.BlockSpec(memory_space=pl.ANY)],
        out_specs=(pl.BlockSpec(memory_space=pltpu.SEMAPHORE), pl.BlockSpec(memory_space=pltpu.SEMAPHORE),
                   *[hbm] * (2 * n), pl.BlockSpec(memory_space=pltpu.VMEM)),
        input_output_aliases={i: 2 + i for i in range(2 * n)},
        compiler_params=pltpu.CompilerParams(has_side_effects=pltpu.SideEffectType.DATAFLOW_SIDE_EFFECTING),
    )(*[pltpu.with_memory_space_constraint(a, pltpu.HBM) for a in list(srcs) + list(lands)], after)
    return outs[0], outs[1], outs[2:2 + n], outs[2 + n:2 + 2 * n], outs[-1]


def _exchange_wait(send_sems, recv_sems, srcs, lands, after, gather, name):
    n = len(srcs)
    hbm = pl.BlockSpec(memory_space=pltpu.HBM)

    def body(*refs):
        for cp in _exchange_copies(refs[:n], refs[n:2 * n], refs[2 * n], refs[2 * n + 1], gather):
            cp.wait_send()
            cp.wait_recv()

    outs = pl.pallas_call(
        body, name=name,
        out_shape=tuple(pltpu.HBM(a.shape, a.dtype) for a in list(srcs) + list(lands)),
        in_specs=[hbm] * (2 * n) + [pl.BlockSpec(memory_space=pltpu.SEMAPHORE)] * 2 + [pl.BlockSpec(memory_space=pl.ANY)],
        out_specs=tuple([hbm] * (2 * n)),
        input_output_aliases={i: i for i in range(2 * n)},
        compiler_params=pltpu.CompilerParams(has_side_effects=pltpu.SideEffectType.DATAFLOW_SIDE_EFFECTING),
    )(*srcs, *lands, send_sems, recv_sems, after)
    return outs[n:]


def _own_slot(block):
    me = 4 * lax.axis_index("x") + 2 * lax.axis_index("y") + lax.axis_index("c")
    return lax.dynamic_update_slice(lax.empty((NDEV,) + block.shape, block.dtype), block[None], (me, 0, 0))


def _adamw(parts, w, m, v, name):
    r, c = w.shape
    tr = r if r * c <= 160 * 1024 else max(8, (160 * 1024 // c) // 8 * 8)
    while r % tr:
        tr -= 8
    bc1 = 1.0 - ADAM_B1 ** ADAM_STEP
    bc2 = 1.0 - ADAM_B2 ** ADAM_STEP

    def body(p_ref, w_ref, m_ref, v_ref, g_ref, d_ref, mo_ref, vo_ref):
        g = p_ref[0].astype(F32)
        for d in range(1, NDEV):
            g = g + p_ref[d].astype(F32)
        g_ref[...] = g
        mn = ADAM_B1 * m_ref[...] + (1.0 - ADAM_B1) * g
        vn = ADAM_B2 * v_ref[...] + (1.0 - ADAM_B2) * (g * g)
        mo_ref[...] = mn
        vo_ref[...] = vn
        d_ref[...] = -ADAM_LR * ((mn / bc1) / (jnp.sqrt(vn / bc2) + ADAM_EPS) + ADAM_WD * w_ref[...])

    spec = pl.BlockSpec((tr, c), lambda i: (i, 0))
    return pl.pallas_call(
        body, name=name, grid=(r // tr,),
        in_specs=[pl.BlockSpec((NDEV, tr, c), lambda i: (0, i, 0)), spec, spec, spec],
        out_specs=[spec] * 4, out_shape=[jax.ShapeDtypeStruct((r, c), F32)] * 4,
        compiler_params=_cp("parallel"),
    )(parts, w, m, v)


def _local_step(x, mem, tgt, gains, b_forget, w_pool, pool_scale, conv_b, w_in,
                mix_weights, ffn_weights, send_in_grad, send_mix_grads, send_ffn_grads):
    w_f = w_in[:, F_COL:]
    b_pad = jnp.pad(b_forget, ((0, 0), (0, 128 - FOX_HEADS)))
    wbd = jnp.zeros((D_POOL, D_POOL), F32)
    for g in range(4):
        wbd = wbd.at[64 * g:64 * g + 64, 64 * g:64 * g + 64].set(w_pool[g])
    wbd = wbd.astype(BF16)
    scale = pool_scale.reshape(1, D_POOL)

    h1 = _norm_fwd(x, gains["mix_pre"], "norm_mix_pre")
    proj = _mm(h1, w_in, "nn", BF16, 1024, 384, 1024, "proj_in")
    fraw = _mm(h1, w_f, "nn", F32, 1024, 128, 1024, "proj_gate")
    flog, cum = _gate_cumsum(fraw, b_pad, "gate_cumsum")
    aq, ak = _fox_operands(cum, "fox_operands")
    ycat, aqb = _fox_fwd(proj, aq, ak, "fox_fwd")
    ycat = _pool_fwd(proj, wbd, scale, ycat, "pool_fwd")
    w_mix, w_xq, w_xo, w_xkv = mix_weights(ycat)
    y1, x1, h2 = _mm_rows(ycat, w_mix, "nn", 1024, "mix_out", [x], [gains["mix_post"], gains["xa_pre"]],
                          [F32, F32, BF16], _epi_resid)
    q2 = _mm(h2, w_xq, "nn", BF16, 1024, 1024, 1024, "xa_q")
    mem_n = _norm_fwd(mem, gains["mem"], "norm_mem")
    kv = _mm(mem_n, w_xkv, "nn", BF16, mem.shape[0], 256, 1024, "xa_kv", b_cols=256)
    o2 = _xattn_fwd(q2, kv, "xattn_fwd")
    y2, x2, h3 = _mm_rows(o2, w_xo, "nn", 1024, "xa_out", [x1], [gains["xa_post"], gains["ffn_pre"]],
                          [F32, F32, BF16], _epi_resid)
    w_up, w_down, cw = ffn_weights(h3)
    hid_g, hid_u, act, loss, dx3, dy3, dg_ffn_post = _ffn_fwd(h3, w_up, cw, conv_b, w_down, x2, tgt,
                                                              gains["ffn_post"], "ffn_fwd")

    dhid_g, dhid_u, dcw_g, dcw_u, dcb_g, dcb_u = _ffn_bwd(dy3, w_down, hid_g, hid_u, cw, conv_b, "ffn_bwd")
    d_w_down = _mm(act, dy3, "tn", BF16, 1024, 1024, 1024, "dw_down")
    d_w_up = _mm(h3, [dhid_g, dhid_u], "tn", BF16, 1024, 1024, 1024, "dw_up", out_cols=1024)
    sent = send_ffn_grads(d_w_up, d_w_down, jnp.concatenate([dcw_g, dcw_u], axis=1))
    dh3 = _mm([dhid_g, dhid_u], w_up, "nt", F32, 1024, 1024, 1024, "dh_ffn", b_cols=1024, after=sent)
    dx2, dg_ffn_pre, dy2, dg_xa_post = _norm_bwd(dh3, x2, dx3, gains["ffn_pre"], "norm_bwd_ffn",
                                                 prev=(y2, gains["xa_post"]))
    do2 = _mm(dy2, w_xo, "nt", BF16, 1024, 1024, 1024, "d_xa_out")
    d_w_xo = _mm(o2, dy2, "tn", BF16, 1024, 1024, 1024, "dw_xo")
    dq2, dkv = _xattn_bwd(q2, kv, do2, "xattn_bwd")
    dkv = dkv.astype(BF16)
    dx1, dg_xa_pre, dy1, dg_mix_post = _mm_rows(
        dq2, w_xq, "nt", 1024, "dh_xa", [x1, dx2, y1], [gains["xa_pre"], gains["mix_post"]],
        [F32, "sum", BF16, "sum"], _epi_norm_bwd)
    d_w_xq = _mm(h2, dq2, "tn", BF16, 1024, 1024, 1024, "dw_xq")
    dmem_n = _mm(dkv, w_xkv, "nt", F32, mem.shape[0], 1024, 256, "d_mem", b_cols=256)
    d_w_xkv = _mm(mem_n, dkv, "tn", BF16, 1024, 256, mem.shape[0], "dw_xkv", out_cols=256)
    _, dg_mem = _norm_bwd(dmem_n, mem, jnp.zeros_like(mem), gains["mem"], "norm_bwd_mem")
    dycat = _mm(dy1, w_mix, "nt", BF16, 1024, 1024, 1024, "d_mix_out")
    d_w_mix = _mm(ycat, dy1, "tn", BF16, 1024, 1024, 1024, "dw_mix")
    sent_mix = send_mix_grads(d_w_mix, d_w_xq, d_w_xo, d_w_xkv)
    ad = _fox_do_operand(dycat, ycat, sent_mix, "fox_do_operand")
    dq, dk, dv, qaux, kaux = _fox_bwd(proj, dycat, aqb, ak, ad, "fox_bwd")
    du, d_wbd, d_scale = _pool_bwd(proj, dycat, wbd, scale, "pool_bwd")
    df, db_f = _gate_bwd(qaux, kaux, flog, "gate_bwd")
    dproj = jnp.concatenate([du, dq, dk, dv, df], axis=1)
    sent_in = send_in_grad(_mm(h1, dproj, "tn", BF16, 1024, 384, 1024, "dw_in"))
    grad_x, dg_mix_pre = _mm_rows(dproj, w_in, "nt", 896, "dh_mix", [x, dx1], [gains["mix_pre"]],
                                  [F32, "sum"], _epi_norm_bwd, after=sent_in)

    small = dict(
        mix_pre=dg_mix_pre, mix_post=dg_mix_post, mem=dg_mem, xa_pre=dg_xa_pre, xa_post=dg_xa_post,
        ffn_pre=dg_ffn_pre, ffn_post=dg_ffn_post,
        conv_b=jnp.concatenate([dcb_g, dcb_u], axis=1),
        w_pool=jnp.stack([d_wbd[64 * g:64 * g + 64, 64 * g:64 * g + 64] for g in range(4)]),
        pool_scale=d_scale.reshape(4, 64),
        b_forget=db_f[:, :FOX_HEADS],
    )
    return loss, grad_x, small


SMALL_ORDER = ("mix_pre", "mix_post", "mem", "xa_pre", "xa_post", "ffn_pre", "ffn_post", "conv_b",
               "w_pool", "pool_scale", "b_forget")
SMALL_ROWS = 256


def _pack_small(d):
    flat = jnp.concatenate([d[k].reshape(-1).astype(F32) for k in SMALL_ORDER])
    return jnp.pad(flat, (0, SMALL_ROWS * 128 - flat.shape[0])).reshape(SMALL_ROWS, 128)


def _unpack_small(a, like):
    flat = a.reshape(-1)
    out, off = {}, 0
    for k in SMALL_ORDER:
        n = like[k].size
        out[k] = flat[off:off + n].reshape(like[k].shape)
        off += n
    return out


def kernel(x, mem, norm_mix_pre, norm_mix_post, w_in, b_forget, w_pool, pool_scale, w_mix_out, norm_mem, norm_xa_pre, norm_xa_post, w_xq, w_xkv, w_xo, norm_ffn_pre, norm_ffn_post, w_up, conv_w, conv_b, w_down, loss_target, m_norm_mix_pre, m_norm_mix_post, m_w_in, m_b_forget, m_w_pool, m_pool_scale, m_w_mix_out, m_norm_mem, m_norm_xa_pre, m_norm_xa_post, m_w_xq, m_w_xkv, m_w_xo, m_norm_ffn_pre, m_norm_ffn_post, m_w_up, m_conv_w, m_conv_b, m_w_down, v_norm_mix_pre, v_norm_mix_post, v_w_in, v_b_forget, v_w_pool, v_pool_scale, v_w_mix_out, v_norm_mem, v_norm_xa_pre, v_norm_xa_post, v_w_xq, v_w_xkv, v_w_xo, v_norm_ffn_pre, v_norm_ffn_post, v_w_up, v_conv_w, v_conv_b, v_w_down):
    names = ("norm_mix_pre", "norm_mix_post", "w_in", "b_forget", "w_pool", "pool_scale", "w_mix_out", "norm_mem",
             "norm_xa_pre", "norm_xa_post", "w_xq", "w_xkv", "w_xo", "norm_ffn_pre", "norm_ffn_post", "w_up",
             "conv_w", "conv_b", "w_down")
    w = dict(zip(names, (norm_mix_pre, norm_mix_post, w_in, b_forget, w_pool, pool_scale, w_mix_out, norm_mem,
                         norm_xa_pre, norm_xa_post, w_xq, w_xkv, w_xo, norm_ffn_pre, norm_ffn_post, w_up,
                         conv_w, conv_b, w_down)))
    mo = dict(zip(names, (m_norm_mix_pre, m_norm_mix_post, m_w_in, m_b_forget, m_w_pool, m_pool_scale, m_w_mix_out,
                          m_norm_mem, m_norm_xa_pre, m_norm_xa_post, m_w_xq, m_w_xkv, m_w_xo, m_norm_ffn_pre,
                          m_norm_ffn_post, m_w_up, m_conv_w, m_conv_b, m_w_down)))
    vo = dict(zip(names, (v_norm_mix_pre, v_norm_mix_post, v_w_in, v_b_forget, v_w_pool, v_pool_scale, v_w_mix_out,
                          v_norm_mem, v_norm_xa_pre, v_norm_xa_post, v_w_xq, v_w_xkv, v_w_xo, v_norm_ffn_pre,
                          v_norm_ffn_post, v_w_up, v_conv_w, v_conv_b, v_w_down)))

    big_names = ("w_in", "w_mix_out", "w_xq", "w_xo", "w_xkv", "w_up", "w_down")
    shards = {k: w[k][0].astype(BF16) for k in big_names}
    shards["w_in"] = jnp.pad(shards["w_in"], ((0, 0), (0, D_IN_PAD - shards["w_in"].shape[1])))
    conv_w_sh = jnp.pad(conv_w[0, :, 0, :], ((0, 5), (0, 0)))
    (g_in,) = _all_gather([shards["w_in"]], "gather_w_in")
    mix_srcs = [shards[k] for k in ("w_mix_out", "w_xq", "w_xo", "w_xkv")]
    mix_flight = _exchange_start(mix_srcs, [_own_slot(a) for a in mix_srcs], g_in, True, "gather_mix_start")
    ffn_srcs = [shards["w_up"], shards["w_down"], conv_w_sh]
    ffn_flight = _exchange_start(ffn_srcs, [_own_slot(a) for a in ffn_srcs], mix_flight[4], True, "gather_ffn_start")
    my_slot = 4 * lax.axis_index("x") + 2 * lax.axis_index("y") + lax.axis_index("c")
    own_block = lambda a: _own_slot(lax.dynamic_index_in_dim(a, my_slot, 0, keepdims=False))
    by_rows = lambda a: a.reshape(NDEV, a.shape[0] // NDEV, a.shape[1])
    by_cols = lambda a: a.reshape(a.shape[0], NDEV, a.shape[1] // NDEV).transpose(1, 0, 2)
    grad_flight = {}

    def mix_weights(after):
        g_mix, g_xq, g_xo, g_xkv = _exchange_wait(*mix_flight[:4], after, True, "gather_mix_wait")
        return (g_mix.reshape(D_MODEL, D_MODEL), g_xq.reshape(D_MODEL, D_MODEL), g_xo.reshape(D_MODEL, D_MODEL), g_xkv)

    def ffn_weights(after):
        g_up, g_down, g_cw = _exchange_wait(*ffn_flight[:4], after, True, "gather_ffn_wait")
        return g_up, g_down.reshape(D_FF, D_MODEL), g_cw.transpose(1, 0, 2).reshape(8, 2 * D_FF)

    def send_ffn_grads(d_w_up, d_w_down, d_cw):
        srcs = [d_w_up, by_rows(d_w_down), by_cols(d_cw)]
        grad_flight["ffn"] = _exchange_start(srcs, [own_block(a) for a in srcs], d_w_up, False, "scatter_ffn_start")
        return grad_flight["ffn"][4]

    def send_mix_grads(d_w_mix, d_w_xq, d_w_xo, d_w_xkv):
        srcs = [by_rows(d_w_mix), by_rows(d_w_xq), by_rows(d_w_xo), d_w_xkv]
        grad_flight["mix"] = _exchange_start(srcs, [own_block(a) for a in srcs], d_w_mix, False, "scatter_mix_start")
        return grad_flight["mix"][4]

    def send_in_grad(d_w_in):
        srcs = [by_rows(d_w_in)]
        grad_flight["in"] = _exchange_start(srcs, [own_block(a) for a in srcs], d_w_in, False, "scatter_in_start")
        return grad_flight["in"][4]

    gains = dict(mix_pre=norm_mix_pre + ffn_flight[4][0, 0], mix_post=norm_mix_post, mem=norm_mem, xa_pre=norm_xa_pre,
                 xa_post=norm_xa_post, ffn_pre=norm_ffn_pre, ffn_post=norm_ffn_post)
    loss, grad_x, small = _local_step(
        x[0], mem[0], loss_target[0], gains, b_forget, w_pool[0], pool_scale[0], conv_b,
        g_in.reshape(D_MODEL, D_IN_PAD), mix_weights, ffn_weights, send_in_grad, send_mix_grads, send_ffn_grads)

    p_up, p_down, p_cw = _exchange_wait(*grad_flight["ffn"][:4], grad_x, False, "scatter_ffn_wait")
    p_mix, p_xq, p_xo, p_xkv = _exchange_wait(*grad_flight["mix"][:4], grad_x, False, "scatter_mix_wait")
    (p_in,) = _exchange_wait(*grad_flight["in"][:4], grad_x, False, "scatter_in_wait")
    parts = [p_in, p_mix, p_xq, p_xo, p_xkv, p_up, p_down, p_cw]
    small_like = dict(mix_pre=norm_mix_pre, mix_post=norm_mix_post, mem=norm_mem, xa_pre=norm_xa_pre,
                      xa_post=norm_xa_post, ffn_pre=norm_ffn_pre, ffn_post=norm_ffn_post, conv_b=conv_b,
                      w_pool=w_pool, pool_scale=pool_scale, b_forget=b_forget)
    small = {k: small[k].reshape(small_like[k].shape) for k in SMALL_ORDER}
    (small_parts,) = _all_gather([_pack_small(small)], "gather_small_grads")

    res = {}
    for k, p in zip(big_names, parts[:7]):
        if k == "w_in":
            p = p[:, :, :w_in.shape[2]]
        res[k] = [a[None] for a in _adamw(p, w[k][0], mo[k][0], vo[k][0], "adamw_" + k)]
    pad_cw = lambda a: jnp.pad(a[0, :, 0, :], ((0, 5), (0, 0)))
    res["conv_w"] = [a[:3][None, :, None, :] for a in
                     _adamw(parts[7], pad_cw(conv_w), pad_cw(m_conv_w), pad_cw(v_conv_w), "adamw_conv_w")]
    key_of = dict(mix_pre="norm_mix_pre", mix_post="norm_mix_post", mem="norm_mem", xa_pre="norm_xa_pre",
                  xa_post="norm_xa_post", ffn_pre="norm_ffn_pre", ffn_post="norm_ffn_post", conv_b="conv_b",
                  w_pool="w_pool", pool_scale="pool_scale", b_forget="b_forget")
    pack_of = lambda src: _pack_small({k: src[key_of[k]] for k in SMALL_ORDER})
    small_out = _adamw(small_parts, pack_of(w), pack_of(mo), pack_of(vo), "adamw_small")
    small_out = [_unpack_small(a, small_like) for a in small_out]
    for k in SMALL_ORDER:
        res[key_of[k]] = [so[k] for so in small_out]

    total = lax.psum(loss[0, 0], ("x", "y", "c"))
    outs = [total, grad_x[None]]
    for idx in range(4):
        outs += [res[k][idx] for k in names]
    return tuple(outs)
```

```python
import functools
import math

import jax
import jax.numpy as jnp
from jax import lax
from jax.experimental import pallas as pl
from jax.experimental.pallas import tpu as pltpu

F32 = jnp.float32
BF16 = jnp.bfloat16

NDEV = 8
D_MODEL = 1024
D_POOL = 256
D_FOX = 768
FOX_HEADS = 12
HEAD_PAIRS = FOX_HEADS // 2
XA_HEADS = 4
XA_DIM = 256
D_FF = 4096
D_IN_PAD = 2688
F_COL = 2560
POOL_HALO = 16
NORM_EPS = 1e-6
NEG = -1e30

ADAM_LR = 0.001
ADAM_B1 = 0.9
ADAM_B2 = 0.999
ADAM_EPS = 1e-08
ADAM_WD = 0.01
ADAM_STEP = 10

TM = 512
TQ = 512
TN_FF = 512
VMEM_LIMIT = 56 * 1024 * 1024
MESH = pl.DeviceIdType.MESH


def _cp(*sem):
    return pltpu.CompilerParams(dimension_semantics=sem, vmem_limit_bytes=VMEM_LIMIT)


def _dot(a, b, dims):
    return lax.dot_general(a, b, (dims, ((), ())), preferred_element_type=F32)


NN = ((1,), (0,))
NT = ((1,), (1,))
TN = ((0,), (0,))


def _mm(a, b, mode, out_dtype, tm, tn, tk, name, b_cols=None, out_cols=None, after=None):
    a_list = list(a) if isinstance(a, (list, tuple)) else [a]
    b_list = list(b) if isinstance(b, (list, tuple)) else [b]
    assert len(a_list) == 1 or len(b_list) == 1
    if mode == "tn":
        K, M = a_list[0].shape
        assert len(a_list) == 1
        Ns = [x.shape[1] for x in b_list]
        N = sum(Ns)
        assert b_cols is None
    else:
        assert len(b_list) == 1
        M = a_list[0].shape[0]
        Ks = [x.shape[1] for x in a_list]
        K = sum(Ks)
        if b_cols is None:
            N = b_list[0].shape[0] if mode == "nt" else b_list[0].shape[1]
        else:
            N = b_list[0].shape[1] if mode == "nt" else NDEV * b_cols
    assert M % tm == 0 and N % tn == 0 and K % tk == 0, (name, M, N, K)
    grid = (M // tm, N // tn, K // tk)
    nk = grid[2]
    dims = {"nn": NN, "nt": NT, "tn": TN}[mode]

    in_specs = []
    if mode == "tn":
        in_specs.append(pl.BlockSpec((tk, tm), lambda i, j, k: (k, i)))
        if len(b_list) == 1:
            in_specs.append(pl.BlockSpec((tk, tn), lambda i, j, k: (k, j)))
        else:
            nj1 = Ns[0] // tn
            in_specs.append(pl.BlockSpec((tk, tn), lambda i, j, k: (k, jnp.minimum(j, nj1 - 1))))
            in_specs.append(pl.BlockSpec((tk, tn), lambda i, j, k: (k, jnp.maximum(j - nj1, 0))))
    else:
        if len(a_list) == 1:
            in_specs.append(pl.BlockSpec((tm, tk), lambda i, j, k: (i, k)))
        else:
            nk1 = Ks[0] // tk
            in_specs.append(pl.BlockSpec((tm, tk), lambda i, j, k: (i, jnp.minimum(k, nk1 - 1))))
            in_specs.append(pl.BlockSpec((tm, tk), lambda i, j, k: (i, jnp.maximum(k - nk1, 0))))
        if b_cols is None:
            if mode == "nn":
                in_specs.append(pl.BlockSpec((tk, tn), lambda i, j, k: (k, j)))
            else:
                in_specs.append(pl.BlockSpec((tn, tk), lambda i, j, k: (j, k)))
        else:
            if mode == "nn":
                per = b_cols // tn
                in_specs.append(pl.BlockSpec((None, tk, tn), lambda i, j, k: (j // per, k, j % per)))
            else:
                per = b_cols // tk
                in_specs.append(pl.BlockSpec((None, tn, tk), lambda i, j, k: (k // per, j, k % per)))
    if out_cols is None:
        out_spec = pl.BlockSpec((tm, tn), lambda i, j, k: (i, j))
        out_shape = jax.ShapeDtypeStruct((M, N), out_dtype)
    else:
        pero = out_cols // tn
        out_spec = pl.BlockSpec((None, tm, tn), lambda i, j, k: (j // pero, i, j % pero))
        out_shape = jax.ShapeDtypeStruct((NDEV, M, out_cols), out_dtype)

    two_a = len(a_list) == 2
    two_b = len(b_list) == 2
    extra = []
    if after is not None:
        in_specs.append(pl.BlockSpec(memory_space=pl.ANY))
        extra.append(after)

    def body(*refs):
        o_ref, acc_ref = refs[-2], refs[-1]
        j = pl.program_id(1)
        k = pl.program_id(2)

        @pl.when(k == 0)
        def _():
            acc_ref[...] = jnp.zeros_like(acc_ref)

        if two_a:
            a1, a2, b1 = refs[0], refs[1], refs[2]
            nk1_ = Ks[0] // tk

            @pl.when(k < nk1_)
            def _():
                acc_ref[...] += _dot(a1[...], b1[...], dims)

            @pl.when(k >= nk1_)
            def _():
                acc_ref[...] += _dot(a2[...], b1[...], dims)
        elif two_b:
            a1, b1, b2 = refs[0], refs[1], refs[2]
            nj1_ = Ns[0] // tn

            @pl.when(j < nj1_)
            def _():
                acc_ref[...] += _dot(a1[...], b1[...], dims)

            @pl.when(j >= nj1_)
            def _():
                acc_ref[...] += _dot(a1[...], b2[...], dims)
        else:
            acc_ref[...] += _dot(refs[0][...], refs[1][...], dims)

        @pl.when(k == nk - 1)
        def _():
            o_ref[...] = acc_ref[...].astype(o_ref.dtype)

    return pl.pallas_call(
        body, name=name, grid=grid, in_specs=in_specs, out_specs=out_spec, out_shape=out_shape,
        scratch_shapes=[pltpu.VMEM((tm, tn), F32)],
        compiler_params=_cp("parallel", "parallel", "arbitrary"),
    )(*a_list, *b_list, *extra)


def _rstd(x):
    return lax.rsqrt(jnp.mean(x * x, axis=-1, keepdims=True) + NORM_EPS)


def _norm_bwd_rows(dxn, xn, r):
    return r * (dxn - xn * jnp.mean(dxn * xn, axis=-1, keepdims=True))


def _row_spec(tm, d):
    return pl.BlockSpec((tm, d), lambda i: (i, 0))


def _vec_spec(d):
    return pl.BlockSpec((1, d), lambda i: (0, 0))


def _mm_rows(a, b, mode, tk, name, rows, vecs, outs, epilogue, b_cols=None, after=None):
    a_list = list(a) if isinstance(a, (list, tuple)) else [a]
    m = a_list[0].shape[0]
    ks = [x.shape[1] for x in a_list]
    n = D_MODEL
    nk = sum(ks) // tk
    dims = NN if mode == "nn" else NT
    if len(a_list) == 1:
        in_specs = [pl.BlockSpec((TM, tk), lambda i, k: (i, k))]
    else:
        nk1 = ks[0] // tk
        in_specs = [pl.BlockSpec((TM, tk), lambda i, k: (i, jnp.minimum(k, nk1 - 1))),
                    pl.BlockSpec((TM, tk), lambda i, k: (i, jnp.maximum(k - nk1, 0)))]
    if mode == "nn":
        in_specs.append(pl.BlockSpec((tk, n), lambda i, k: (k, 0)))
    elif b_cols is None:
        in_specs.append(pl.BlockSpec((n, tk), lambda i, k: (0, k)))
    else:
        per = b_cols // tk
        in_specs.append(pl.BlockSpec((None, n, tk), lambda i, k: (k // per, 0, k % per)))
    in_specs += [pl.BlockSpec((TM, n), lambda i, k: (i, 0))] * len(rows)
    in_specs += [pl.BlockSpec((1, n), lambda i, k: (0, 0))] * len(vecs)
    extra = []
    if after is not None:
        in_specs.append(pl.BlockSpec(memory_space=pl.ANY))
        extra.append(after)
    out_specs, out_shape = [], []
    for o in outs:
        if o == "sum":
            out_specs.append(pl.BlockSpec((1, n), lambda i, k: (0, 0)))
            out_shape.append(jax.ShapeDtypeStruct((1, n), F32))
        else:
            out_specs.append(pl.BlockSpec((TM, n), lambda i, k: (i, 0)))
            out_shape.append(jax.ShapeDtypeStruct((m, n), o))
    na, nr, nv = len(a_list), len(rows), len(vecs)

    def body(*refs):
        a_refs, b_ref = refs[:na], refs[na]
        row_refs = refs[na + 1:na + 1 + nr]
        vec_refs = refs[na + 1 + nr:na + 1 + nr + nv]
        out_refs = refs[len(refs) - 1 - len(outs):len(refs) - 1]
        acc_ref = refs[-1]
        i, k = pl.program_id(0), pl.program_id(1)

        @pl.when(k == 0)
        def _():
            acc_ref[...] = jnp.zeros_like(acc_ref)

        if na == 1:
            acc_ref[...] += _dot(a_refs[0][...], b_ref[...], dims)
        else:
            nk1_ = ks[0] // tk

            @pl.when(k < nk1_)
            def _():
                acc_ref[...] += _dot(a_refs[0][...], b_ref[...], dims)

            @pl.when(k >= nk1_)
            def _():
                acc_ref[...] += _dot(a_refs[1][...], b_ref[...], dims)

        @pl.when(k == nk - 1)
        def _():
            vals = epilogue(acc_ref[...], [r[...] for r in row_refs], [v[...] for v in vec_refs])
            for o, ref, val in zip(outs, out_refs, vals):
                if o == "sum":
                    @pl.when(i == 0)
                    def _():
                        ref[...] = val

                    @pl.when(i > 0)
                    def _():
                        ref[...] += val
                else:
                    ref[...] = val.astype(o)

    return pl.pallas_call(
        body, name=name, grid=(m // TM, nk), in_specs=in_specs, out_specs=out_specs, out_shape=out_shape,
        scratch_shapes=[pltpu.VMEM((TM, n), F32)],
        compiler_params=_cp("arbitrary", "arbitrary"),
    )(*a_list, b, *rows, *vecs, *extra)


def _epi_resid(y, rows, vecs):
    (x_in,), (g_post, g_next) = rows, vecs
    xo = x_in + y * _rstd(y) * g_post
    return y, xo, xo * _rstd(xo) * g_next


def _epi_norm_bwd(dh, rows, vecs):
    x, dx_res = rows[0], rows[1]
    r = _rstd(x)
    xn = x * r
    dx = dx_res + _norm_bwd_rows(dh * vecs[0], xn, r)
    res = [dx, jnp.sum(dh * xn, axis=0, keepdims=True)]
    if len(rows) == 3:
        y = rows[2]
        r2 = _rstd(y)
        yn = y * r2
        res += [_norm_bwd_rows(dx * vecs[1], yn, r2), jnp.sum(dx * yn, axis=0, keepdims=True)]
    return res


def _norm_fwd(x, g, name):
    s, d = x.shape
    tm = min(TM, s)

    def body(x_ref, g_ref, h_ref):
        xv = x_ref[...]
        h_ref[...] = (xv * _rstd(xv) * g_ref[...]).astype(BF16)

    return pl.pallas_call(
        body, name=name, grid=(s // tm,), in_specs=[_row_spec(tm, d), _vec_spec(d)],
        out_specs=_row_spec(tm, d), out_shape=jax.ShapeDtypeStruct((s, d), BF16),
        compiler_params=_cp("parallel"),
    )(x, g)


def _norm_bwd(dh, x, dx_res, g_pre, name, prev=None):
    s, d = x.shape
    tm = min(TM, s)
    has_prev = prev is not None

    def body(*refs):
        if has_prev:
            dh_ref, x_ref, dr_ref, g_ref, y_ref, gp_ref, dx_ref, dg_ref, dy_ref, dgp_ref = refs
        else:
            dh_ref, x_ref, dr_ref, g_ref, dx_ref, dg_ref = refs
        i = pl.program_id(0)
        xv = x_ref[...]
        r = _rstd(xv)
        xn = xv * r
        dhv = dh_ref[...].astype(F32)
        dx = dr_ref[...] + _norm_bwd_rows(dhv * g_ref[...], xn, r)
        dx_ref[...] = dx
        dg = jnp.sum(dhv * xn, axis=0, keepdims=True)

        @pl.when(i == 0)
        def _():
            dg_ref[...] = dg

        @pl.when(i > 0)
        def _():
            dg_ref[...] += dg

        if has_prev:
            yv = y_ref[...]
            r2 = _rstd(yv)
            yn = yv * r2
            dy_ref[...] = _norm_bwd_rows(dx * gp_ref[...], yn, r2).astype(BF16)
            dgp = jnp.sum(dx * yn, axis=0, keepdims=True)

            @pl.when(i == 0)
            def _():
                dgp_ref[...] = dgp

            @pl.when(i > 0)
            def _():
                dgp_ref[...] += dgp

    in_specs = [_row_spec(tm, d), _row_spec(tm, d), _row_spec(tm, d), _vec_spec(d)]
    out_specs = [_row_spec(tm, d), _vec_spec(d)]
    out_shape = [jax.ShapeDtypeStruct((s, d), F32), jax.ShapeDtypeStruct((1, d), F32)]
    args = [dh, x, dx_res, g_pre]
    if has_prev:
        in_specs += [_row_spec(tm, d), _vec_spec(d)]
        out_specs += [_row_spec(tm, d), _vec_spec(d)]
        out_shape += [jax.ShapeDtypeStruct((s, d), BF16), jax.ShapeDtypeStruct((1, d), F32)]
        args += list(prev)
    return pl.pallas_call(
        body, name=name, grid=(s // tm,), in_specs=in_specs, out_specs=out_specs, out_shape=out_shape,
        compiler_params=_cp("arbitrary"),
    )(*args)


def _split3(v):
    hi = v.astype(BF16)
    r1 = v - hi.astype(F32)
    mid = r1.astype(BF16)
    lo = (r1 - mid.astype(F32)).astype(BF16)
    return hi, mid, lo


def _tri_dot(tri, v):
    hi, mid, lo = _split3(v)
    return _dot(tri, hi, NN) + _dot(tri, mid, NN) + _dot(tri, lo, NN)


def _gate_cumsum(fraw, b_pad, name):
    s = fraw.shape[0]

    def body(f_ref, b_ref, flog_ref, cum_ref, carry_ref):
        i = pl.program_id(0)

        @pl.when(i == 0)
        def _():
            carry_ref[...] = jnp.zeros_like(carry_ref)

        flog = f_ref[...] + b_ref[...]
        flog_ref[...] = flog
        lf = jnp.minimum(flog, 0.0) - jnp.log(1.0 + jnp.exp(-jnp.abs(flog)))
        lane = lax.broadcasted_iota(jnp.int32, (1, 128), 1)
        lf = jnp.where(lane < FOX_HEADS, lf, 0.0)
        row = lax.broadcasted_iota(jnp.int32, (TM, TM), 0)
        col = lax.broadcasted_iota(jnp.int32, (TM, TM), 1)
        tri = (row >= col).astype(BF16)
        cum = _tri_dot(tri, lf) + carry_ref[...]
        cum_ref[...] = cum
        carry_ref[...] = cum[TM - 1:TM, :]

    return pl.pallas_call(
        body, name=name, grid=(s // TM,),
        in_specs=[_row_spec(TM, 128), _vec_spec(128)],
        out_specs=[_row_spec(TM, 128), _row_spec(TM, 128)],
        out_shape=[jax.ShapeDtypeStruct((s, 128), F32), jax.ShapeDtypeStruct((s, 128), F32)],
        scratch_shapes=[pltpu.VMEM((1, 128), F32)],
        compiler_params=_cp("arbitrary"),
    )(fraw, b_pad)


def _gate_bwd(qaux, kaux, flog, name):
    s = flog.shape[0]
    n = s // TM

    def body(qa_ref, ka_ref, fl_ref, dp_ref, db_ref, carry_ref):
        i = pl.program_id(0)

        @pl.when(i == 0)
        def _():
            carry_ref[...] = jnp.zeros_like(carry_ref)

        src = lax.broadcasted_iota(jnp.int32, (128, 128), 0)
        dst = lax.broadcasted_iota(jnp.int32, (128, 128), 1)
        dcum = jnp.zeros((TM, 128), F32)
        for p in range(HEAD_PAIRS):
            for ref, l0, l1, sign in ((qa_ref, 64, 0, 1.0), (ka_ref, 67, 3, -1.0)):
                hit = jnp.logical_or(jnp.logical_and(src == l0, dst == 2 * p),
                                     jnp.logical_and(src == l1, dst == 2 * p + 1))
                sel = jnp.where(hit, sign, 0.0).astype(BF16)
                for piece in _split3(ref[p]):
                    dcum = dcum + _dot(piece, sel, NN)
        row = lax.broadcasted_iota(jnp.int32, (TM, TM), 0)
        col = lax.broadcasted_iota(jnp.int32, (TM, TM), 1)
        tri = (row <= col).astype(BF16)
        dlf = _tri_dot(tri, dcum) + carry_ref[...]
        carry_ref[...] = dlf[0:1, :]
        lane = lax.broadcasted_iota(jnp.int32, (1, 128), 1)
        df = jnp.where(lane < FOX_HEADS, dlf / (1.0 + jnp.exp(fl_ref[...])), 0.0)
        dp_ref[...] = df.astype(BF16)
        db = jnp.sum(df, axis=0, keepdims=True)

        @pl.when(i == 0)
        def _():
            db_ref[...] = db

        @pl.when(i > 0)
        def _():
            db_ref[...] += db

    rev = lambda i: (n - 1 - i, 0)
    return pl.pallas_call(
        body, name=name, grid=(n,),
        in_specs=[pl.BlockSpec((HEAD_PAIRS, TM, 128), lambda i: (0, n - 1 - i, 0)),
                  pl.BlockSpec((HEAD_PAIRS, TM, 128), lambda i: (0, n - 1 - i, 0)), pl.BlockSpec((TM, 128), rev)],
        out_specs=[pl.BlockSpec((TM, 128), rev), _vec_spec(128)],
        out_shape=[jax.ShapeDtypeStruct((s, 128), BF16), jax.ShapeDtypeStruct((1, 128), F32)],
        scratch_shapes=[pltpu.VMEM((1, 128), F32)],
        compiler_params=_cp("arbitrary"),
    )(qaux, kaux, flog)


def _pool_consts(i, rows):
    lane = lax.broadcasted_iota(jnp.int32, (rows, D_POOL), 1)
    t1 = lax.broadcasted_iota(jnp.int32, (rows, D_POOL), 0) + i * TM + 1
    win = jnp.where(lane < 64, 2, jnp.where(lane < 128, 4, jnp.where(lane < 192, 8, 16)))
    inv = 1.0 / jnp.minimum(t1, win).astype(F32)
    return lane, inv


def _by_group(lane, s2, s4, s8, s16):
    return jnp.where(lane < 64, s2, jnp.where(lane < 128, s4, jnp.where(lane < 192, s8, s16)))


def _pool_diff(i, u_ref, halo_ref):
    u = u_ref[...].astype(F32)
    halo = jnp.where(i > 0, halo_ref[...].astype(F32), 0.0)
    ext = jnp.concatenate([halo, u], axis=0)
    s2 = ext + pltpu.roll(ext, 1, 0)
    s4 = s2 + pltpu.roll(s2, 2, 0)
    s8 = s4 + pltpu.roll(s4, 4, 0)
    s16 = s8 + pltpu.roll(s8, 8, 0)
    lane, inv = _pool_consts(i, TM)
    sel = _by_group(lane, s2[POOL_HALO:], s4[POOL_HALO:], s8[POOL_HALO:], s16[POOL_HALO:])
    return sel * inv - u


def _pool_fwd(proj, wbd, scale, ycat, name):
    s = proj.shape[0]
    hb = TM // POOL_HALO

    def body(u_ref, halo_ref, w_ref, sc_ref, y_any, y_ref):
        del y_any
        i = pl.program_id(0)
        diff = _pool_diff(i, u_ref, halo_ref)
        mixed = _dot(diff.astype(BF16), w_ref[...], NN)
        y_ref[...] = (mixed * sc_ref[...]).astype(BF16)

    return pl.pallas_call(
        body, name=name, grid=(s // TM,),
        in_specs=[pl.BlockSpec((TM, D_POOL), lambda i: (i, 0)),
                  pl.BlockSpec((POOL_HALO, D_POOL), lambda i: (jnp.maximum(i * hb - 1, 0), 0)),
                  pl.BlockSpec((D_POOL, D_POOL), lambda i: (0, 0)), _vec_spec(D_POOL),
                  pl.BlockSpec(memory_space=pl.ANY)],
        out_specs=pl.BlockSpec((TM, D_POOL), lambda i: (i, 0)),
        out_shape=jax.ShapeDtypeStruct(ycat.shape, ycat.dtype),
        input_output_aliases={4: 0},
        compiler_params=_cp("parallel"),
    )(proj, proj, wbd, scale, ycat)


def _pool_bwd(proj, dycat, wbd, scale, name):
    s = proj.shape[0]
    n = s // TM
    hb = TM // POOL_HALO
    last_halo = s // POOL_HALO - 1

    def body(u_ref, halo_ref, dy_ref, dyp_ref, w_ref, sc_ref, dp_ref, dw_ref, dsc_ref):
        i = pl.program_id(0)
        diff = _pool_diff(i, u_ref, halo_ref)
        diff_b = diff.astype(BF16)
        mixed = _dot(diff_b, w_ref[...], NN)
        dy = dy_ref[...].astype(F32)
        dmix = (dy * sc_ref[...]).astype(BF16)
        dyp = jnp.where(i < n - 1, dyp_ref[...].astype(F32), 0.0)
        dmix_p = (dyp * sc_ref[...]).astype(BF16)
        dd = _dot(dmix, w_ref[...], NT)
        dd_p = _dot(dmix_p, w_ref[...], NT)
        lane, inv = _pool_consts(i, TM)
        _, inv_p = _pool_consts(i + 1, POOL_HALO)
        ext = jnp.concatenate([dd * inv, dd_p * inv_p], axis=0)
        rows = TM + POOL_HALO
        l2 = ext + pltpu.roll(ext, rows - 1, 0)
        l4 = l2 + pltpu.roll(l2, rows - 2, 0)
        l8 = l4 + pltpu.roll(l4, rows - 4, 0)
        l16 = l8 + pltpu.roll(l8, rows - 8, 0)
        du = _by_group(lane, l2[:TM], l4[:TM], l8[:TM], l16[:TM]) - dd
        dp_ref[...] = du.astype(BF16)
        dw = _dot(diff_b, dmix, TN)
        dsc = jnp.sum(dy * mixed, axis=0, keepdims=True)

        @pl.when(i == 0)
        def _():
            dw_ref[...] = dw
            dsc_ref[...] = dsc

        @pl.when(i > 0)
        def _():
            dw_ref[...] += dw
            dsc_ref[...] += dsc

    return pl.pallas_call(
        body, name=name, grid=(n,),
        in_specs=[pl.BlockSpec((TM, D_POOL), lambda i: (i, 0)),
                  pl.BlockSpec((POOL_HALO, D_POOL), lambda i: (jnp.maximum(i * hb - 1, 0), 0)),
                  pl.BlockSpec((TM, D_POOL), lambda i: (i, 0)),
                  pl.BlockSpec((POOL_HALO, D_POOL), lambda i: (jnp.minimum((i + 1) * hb, last_halo), 0)),
                  pl.BlockSpec((D_POOL, D_POOL), lambda i: (0, 0)), _vec_spec(D_POOL)],
        out_specs=[pl.BlockSpec((TM, D_POOL), lambda i: (i, 0)),
                   pl.BlockSpec((D_POOL, D_POOL), lambda i: (0, 0)), _vec_spec(D_POOL)],
        out_shape=[jax.ShapeDtypeStruct((s, D_POOL), BF16),
                   jax.ShapeDtypeStruct((D_POOL, D_POOL), F32), jax.ShapeDtypeStruct((1, D_POOL), F32)],
        compiler_params=_cp("arbitrary"),
    )(proj, proj, dycat, dycat, wbd, scale)


Q_BLK = D_POOL // 128
K_BLK = Q_BLK + D_FOX // 128
V_BLK = K_BLK + D_FOX // 128


def _operand_rows(v0, v1, ones_off):
    row = lax.broadcasted_iota(jnp.int32, (128, 1), 0)
    half = row & 63
    out = jnp.where(jnp.logical_and(half >= ones_off, half < ones_off + 3), 1.0, 0.0) + jnp.zeros_like(v0)
    for base, v in ((64, v0), (0, v1)):
        for j, piece in enumerate(_split3(v)):
            out = jnp.where(row == base + j, piece.astype(F32), out)
    return out


def _fox_operands(cum, name):
    s = cum.shape[0]
    width = HEAD_PAIRS * 128

    def body(c_ref, aq_ref, ak_ref):
        pieces = _split3(c_ref[...])
        row = lax.broadcasted_iota(jnp.int32, (128, width), 0)
        col = lax.broadcasted_iota(jnp.int32, (128, width), 1)
        base = (row >> 1) * 128 + (1 - (row & 1)) * 64
        half = lax.broadcasted_iota(jnp.int32, (1, width), 1) & 63
        for o_ref, off, sign, ones_off in ((aq_ref, 0, 1.0, 3), (ak_ref, 3, -1.0, 0)):
            out = jnp.where(jnp.logical_and(half >= ones_off, half < ones_off + 3), 1.0, 0.0)
            for j, piece in enumerate(pieces):
                sel = jnp.where(jnp.logical_and(col == base + off + j, row < FOX_HEADS), sign, 0.0).astype(BF16)
                out = out + _dot(piece, sel, NN)
            o_ref[...] = out.astype(BF16)

    return pl.pallas_call(
        body, name=name, grid=(s // TM,), in_specs=[_row_spec(TM, 128)],
        out_specs=[_row_spec(TM, width), _row_spec(TM, width)],
        out_shape=[jax.ShapeDtypeStruct((s, width), BF16)] * 2,
        compiler_params=_cp("parallel"),
    )(cum)


def _fox_do_operand(dycat, ycat, after, name):
    s = dycat.shape[0]

    rb = 1024

    def body(do_ref, o_ref, after_ref, ad_ref):
        del after_ref
        src = lax.broadcasted_iota(jnp.int32, (128, 128), 0)
        dst = lax.broadcasted_iota(jnp.int32, (128, 128), 1)
        hit = jnp.logical_or(jnp.logical_and(src < 64, jnp.logical_and(dst >= 64, dst < 67)),
                             jnp.logical_and(src >= 64, dst < 3))
        sel = jnp.where(hit, 1.0, 0.0).astype(BF16)
        dd = do_ref[...].astype(F32) * o_ref[...].astype(F32)
        dsum = jnp.zeros(dd.shape, F32)
        for piece in _split3(dd):
            dsum = dsum + _dot(piece, sel, NN)
        hi, mid, lo = _split3(-dsum)
        j = lax.broadcasted_iota(jnp.int32, (1, 128), 1) & 63
        ad_ref[...] = jnp.where(j == 0, hi, jnp.where(j == 1, mid, lo))

    blk = pl.BlockSpec((rb, 128), lambda i, p: (i, Q_BLK + p))
    return pl.pallas_call(
        body, name=name, grid=(s // rb, HEAD_PAIRS), in_specs=[blk, blk, pl.BlockSpec(memory_space=pl.ANY)],
        out_specs=pl.BlockSpec((rb, 128), lambda i, p: (i, p)),
        out_shape=jax.ShapeDtypeStruct((s, HEAD_PAIRS * 128), BF16),
        compiler_params=_cp("parallel", "parallel"),
    )(dycat, ycat, after)


def _causal_pairs(nq, key_major):
    if key_major:
        pairs = [(q, k) for k in range(nq) for q in range(k, nq)]
    else:
        pairs = [(q, k) for q in range(nq) for k in range(q + 1)]
    return (jnp.asarray([p[0] for p in pairs], jnp.int32), jnp.asarray([p[1] for p in pairs], jnp.int32))


def _fox_fwd(proj, aq, ak, name):
    s = proj.shape[0]
    nq = s // TQ
    qi_arr, ki_arr = _causal_pairs(nq, key_major=False)

    def body(qi_ref, ki_ref, q_ref, k_ref, v_ref, aq_ref, ak_ref, o_ref, aqb_ref, m0_ref, m1_ref, acc_ref, aux_ref):
        t = pl.program_id(1)
        qi, ki = qi_ref[t], ki_ref[t]
        lane = lax.broadcasted_iota(jnp.int32, (1, 128), 1)
        masks = [lane < 64, lane >= 64]
        ones_v = jnp.where((lane & 63) == 8, 1.0, 0.0).astype(BF16)
        top = lax.broadcasted_iota(jnp.int32, (128, 1), 0) < 64
        m_ref = [m0_ref, m1_ref]

        @pl.when(ki == 0)
        def _():
            m0_ref[...] = jnp.full_like(m0_ref, NEG)
            m1_ref[...] = jnp.full_like(m1_ref, NEG)
            acc_ref[...] = jnp.zeros_like(acc_ref)
            aux_ref[...] = jnp.zeros_like(aux_ref)

        def step(diag):
            q2s = q_ref[...] * 0.125
            k2, v2, aq2, ak2 = k_ref[...], v_ref[...], aq_ref[...], ak_ref[...]
            pv, alpha = [], []
            for hh in range(2):
                qh = jnp.where(masks[hh], q2s, aq2)
                kh = jnp.where(masks[hh], k2, ak2)
                vh = jnp.where(masks[hh], v2, ones_v)
                sc = _dot(kh, qh, NT)
                if diag:
                    key = lax.broadcasted_iota(jnp.int32, sc.shape, 0)
                    qry = lax.broadcasted_iota(jnp.int32, sc.shape, 1)
                    sc = jnp.where(qry >= key, sc, NEG)
                m_prev = m_ref[hh][...]
                m_new = jnp.maximum(m_prev, jnp.max(sc, axis=0, keepdims=True))
                m_ref[hh][...] = m_new
                alpha.append(jnp.exp(m_prev - m_new))
                pv.append(_dot(vh, jnp.exp(sc - m_new).astype(BF16), TN))
            acc_ref[...] = acc_ref[...] * jnp.where(top, alpha[0], alpha[1]) + jnp.where(top, pv[0], pv[1])
            aux_ref[...] = aux_ref[...] * jnp.where(top, alpha[1], alpha[0]) + jnp.where(top, pv[1], pv[0])

        @pl.when(ki < qi)
        def _():
            step(False)

        @pl.when(ki == qi)
        def _():
            step(True)
            aux = aux_ref[...]
            l0, l1 = aux[72:73, :], aux[8:9, :]
            o_ref[...] = (acc_ref[...] * jnp.where(top, 1.0 / l0, 1.0 / l1)).T.astype(BF16)
            aqt = aq_ref[...].astype(F32).T
            cum0 = aqt[64:65, :] + aqt[65:66, :] + aqt[66:67, :]
            cum1 = aqt[0:1, :] + aqt[1:2, :] + aqt[2:3, :]
            aqb = _operand_rows(cum0 - (m0_ref[...] + jnp.log(l0)), cum1 - (m1_ref[...] + jnp.log(l1)), 3)
            aqb_ref[...] = aqb.T.astype(BF16)

    grid_spec = pltpu.PrefetchScalarGridSpec(
        num_scalar_prefetch=2, grid=(HEAD_PAIRS, int(qi_arr.shape[0])),
        in_specs=[pl.BlockSpec((TQ, 128), lambda p, t, qi, ki: (qi[t], Q_BLK + p)),
                  pl.BlockSpec((TQ, 128), lambda p, t, qi, ki: (ki[t], K_BLK + p)),
                  pl.BlockSpec((TQ, 128), lambda p, t, qi, ki: (ki[t], V_BLK + p)),
                  pl.BlockSpec((TQ, 128), lambda p, t, qi, ki: (qi[t], p)),
                  pl.BlockSpec((TQ, 128), lambda p, t, qi, ki: (ki[t], p))],
        out_specs=[pl.BlockSpec((TQ, 128), lambda p, t, qi, ki: (qi[t], Q_BLK + p)),
                   pl.BlockSpec((TQ, 128), lambda p, t, qi, ki: (qi[t], p))],
        scratch_shapes=[pltpu.VMEM((1, TQ), F32), pltpu.VMEM((1, TQ), F32),
                        pltpu.VMEM((128, TQ), F32), pltpu.VMEM((128, TQ), F32)])
    return pl.pallas_call(
        body, name=name, grid_spec=grid_spec,
        out_shape=[jax.ShapeDtypeStruct((s, D_MODEL), BF16), jax.ShapeDtypeStruct((s, HEAD_PAIRS * 128), BF16)],
        compiler_params=_cp("parallel", "arbitrary"),
    )(qi_arr, ki_arr, proj, proj, proj, aq, ak)


def _fox_bwd(proj, dycat, aqb, ak, ad, name):
    s = proj.shape[0]
    nq = s // TQ
    qi_arr, ki_arr = _causal_pairs(nq, key_major=True)

    def body(qi_ref, ki_ref, q_ref, k_ref, v_ref, do_ref, aq_ref, ak_ref, ad_ref,
             dq_ref, dk_ref, dv_ref, qaux_ref, kaux_ref, dq_acc, qaux_acc, dk_acc, dv_acc, kaux_acc):
        t = pl.program_id(1)
        qi, ki = qi_ref[t], ki_ref[t]
        lane = lax.broadcasted_iota(jnp.int32, (1, 128), 1)
        masks = [lane < 64, lane >= 64]
        ones_v = jnp.where((lane & 63) < 3, 1.0, 0.0).astype(BF16)
        top = lax.broadcasted_iota(jnp.int32, (128, 1), 0) < 64

        @pl.when(qi == ki)
        def _():
            dk_acc[...] = jnp.zeros_like(dk_acc)
            dv_acc[...] = jnp.zeros_like(dv_acc)
            kaux_acc[...] = jnp.zeros_like(kaux_acc)

        def step(diag):
            q2s = q_ref[...] * 0.125
            k2, v2, do2 = k_ref[...], v_ref[...], do_ref[...]
            aq2, ak2, ad2 = aq_ref[...], ak_ref[...], ad_ref[...]
            dq, dk, dv = [], [], []
            for hh in range(2):
                qh = jnp.where(masks[hh], q2s, aq2)
                kh = jnp.where(masks[hh], k2, ak2)
                doh = jnp.where(masks[hh], do2, ad2)
                vh = jnp.where(masks[hh], v2, ones_v)
                sc = _dot(kh, qh, NT)
                if diag:
                    key = lax.broadcasted_iota(jnp.int32, sc.shape, 0)
                    qry = lax.broadcasted_iota(jnp.int32, sc.shape, 1)
                    sc = jnp.where(qry >= key, sc, NEG)
                p = jnp.exp(sc)
                dsb = (p * _dot(vh, doh, NT)).astype(BF16)
                dv.append(_dot(p.astype(BF16), doh, NN))
                dk.append(_dot(dsb, qh, NN))
                dq.append(_dot(kh, dsb, TN))
            dk_acc[...] += jnp.where(masks[0], dk[0], dk[1])
            kaux_acc[...] += jnp.where(masks[0], dk[1], dk[0])
            dv_acc[...] += jnp.where(masks[0], dv[0], dv[1])
            dq_new = jnp.where(top, dq[0], dq[1])
            qaux_new = jnp.where(top, dq[1], dq[0])

            @pl.when(ki == 0)
            def _():
                dq_acc[qi] = dq_new
                qaux_acc[qi] = qaux_new

            @pl.when(ki > 0)
            def _():
                dq_acc[qi] += dq_new
                qaux_acc[qi] += qaux_new

        @pl.when(qi > ki)
        def _():
            step(False)

        @pl.when(qi == ki)
        def _():
            step(True)
            rows = pl.ds(pl.multiple_of(qi * TQ, TQ), TQ)
            dq_ref[rows, :] = (dq_acc[qi] * 0.125).T.astype(BF16)
            qaux_ref[rows, :] = qaux_acc[qi].T

        @pl.when(qi == nq - 1)
        def _():
            dk_ref[...] = dk_acc[...].astype(BF16)
            dv_ref[...] = dv_acc[...].astype(BF16)
            kaux_ref[...] = kaux_acc[...]

    grid_spec = pltpu.PrefetchScalarGridSpec(
        num_scalar_prefetch=2, grid=(HEAD_PAIRS, int(qi_arr.shape[0])),
        in_specs=[pl.BlockSpec((TQ, 128), lambda p, t, qi, ki: (qi[t], Q_BLK + p)),
                  pl.BlockSpec((TQ, 128), lambda p, t, qi, ki: (ki[t], K_BLK + p)),
                  pl.BlockSpec((TQ, 128), lambda p, t, qi, ki: (ki[t], V_BLK + p)),
                  pl.BlockSpec((TQ, 128), lambda p, t, qi, ki: (qi[t], Q_BLK + p)),
                  pl.BlockSpec((TQ, 128), lambda p, t, qi, ki: (qi[t], p)),
                  pl.BlockSpec((TQ, 128), lambda p, t, qi, ki: (ki[t], p)),
                  pl.BlockSpec((TQ, 128), lambda p, t, qi, ki: (qi[t], p))],
        out_specs=[pl.BlockSpec((s, 128), lambda p, t, qi, ki: (0, p)),
                   pl.BlockSpec((TQ, 128), lambda p, t, qi, ki: (ki[t], p)),
                   pl.BlockSpec((TQ, 128), lambda p, t, qi, ki: (ki[t], p)),
                   pl.BlockSpec((None, s, 128), lambda p, t, qi, ki: (p, 0, 0)),
                   pl.BlockSpec((None, TQ, 128), lambda p, t, qi, ki: (p, ki[t], 0))],
        scratch_shapes=[pltpu.VMEM((nq, 128, TQ), F32), pltpu.VMEM((nq, 128, TQ), F32),
                        pltpu.VMEM((TQ, 128), F32), pltpu.VMEM((TQ, 128), F32), pltpu.VMEM((TQ, 128), F32)])
    return pl.pallas_call(
        body, name=name, grid_spec=grid_spec,
        out_shape=[jax.ShapeDtypeStruct((s, D_FOX), BF16)] * 3 + [jax.ShapeDtypeStruct((HEAD_PAIRS, s, 128), F32)] * 2,
        compiler_params=_cp("arbitrary", "arbitrary"),
    )(qi_arr, ki_arr, proj, proj, proj, dycat, aqb, ak, ad)


XA_SCALE = XA_DIM ** -0.5


def _xattn_fwd(q2, kv, name):
    s = q2.shape[0]
    m = kv.shape[0]

    def body(q_ref, kv_ref, o_ref):
        for h in range(XA_HEADS):
            c0 = h * XA_DIM
            sc = _dot(q_ref[:, c0:c0 + XA_DIM], kv_ref[:, c0:c0 + XA_DIM], NT) * XA_SCALE
            e = jnp.exp(sc - jnp.max(sc, axis=1, keepdims=True))
            p = e / jnp.sum(e, axis=1, keepdims=True)
            o_ref[:, c0:c0 + XA_DIM] = _dot(p.astype(BF16), kv_ref[:, D_MODEL + c0:D_MODEL + c0 + XA_DIM], NN).astype(BF16)

    return pl.pallas_call(
        body, name=name, grid=(s // TM,),
        in_specs=[_row_spec(TM, D_MODEL), pl.BlockSpec((m, 2 * D_MODEL), lambda i: (0, 0))],
        out_specs=_row_spec(TM, D_MODEL), out_shape=jax.ShapeDtypeStruct((s, D_MODEL), BF16),
        compiler_params=_cp("parallel"),
    )(q2, kv)


def _xattn_bwd(q2, kv, do2, name):
    s = q2.shape[0]
    m = kv.shape[0]

    def body(q_ref, kv_ref, do_ref, dq_ref, dkv_ref):
        i = pl.program_id(0)

        @pl.when(i == 0)
        def _():
            dkv_ref[...] = jnp.zeros_like(dkv_ref)

        for h in range(XA_HEADS):
            c0 = h * XA_DIM
            v0 = D_MODEL + c0
            qh = q_ref[:, c0:c0 + XA_DIM]
            kh = kv_ref[:, c0:c0 + XA_DIM]
            doh = do_ref[:, c0:c0 + XA_DIM]
            sc = _dot(qh, kh, NT) * XA_SCALE
            e = jnp.exp(sc - jnp.max(sc, axis=1, keepdims=True))
            p = e / jnp.sum(e, axis=1, keepdims=True)
            dp = _dot(doh, kv_ref[:, v0:v0 + XA_DIM], NT)
            ds = p * (dp - jnp.sum(p * dp, axis=1, keepdims=True))
            dsb = (ds * XA_SCALE).astype(BF16)
            dq_ref[:, c0:c0 + XA_DIM] = _dot(dsb, kh, NN).astype(BF16)
            dkv_ref[:, c0:c0 + XA_DIM] += _dot(dsb, qh, TN)
            dkv_ref[:, v0:v0 + XA_DIM] += _dot(p.astype(BF16), doh, TN)

    return pl.pallas_call(
        body, name=name, grid=(s // TM,),
        in_specs=[_row_spec(TM, D_MODEL), pl.BlockSpec((m, 2 * D_MODEL), lambda i: (0, 0)), _row_spec(TM, D_MODEL)],
        out_specs=[_row_spec(TM, D_MODEL), pl.BlockSpec((m, 2 * D_MODEL), lambda i: (0, 0))],
        out_shape=[jax.ShapeDtypeStruct((s, D_MODEL), BF16), jax.ShapeDtypeStruct((m, 2 * D_MODEL), F32)],
        compiler_params=_cp("arbitrary"),
    )(q2, kv, do2)


GELU_C = math.sqrt(2.0 / math.pi)
GELU_A = 0.044715


def _gelu(x):
    return 0.5 * x * (1.0 + jnp.tanh(GELU_C * (x + GELU_A * x * x * x)))


def _gelu_and_grad(x):
    t = jnp.tanh(GELU_C * (x + GELU_A * x * x * x))
    g = 0.5 * x * (1.0 + t)
    dg = 0.5 * (1.0 + t) + 0.5 * x * (1.0 - t * t) * GELU_C * (1.0 + 3.0 * GELU_A * x * x)
    return g, dg


def _conv(h, s1, s2, w_ref, b_ref):
    return w_ref[0:1, :] * s2 + w_ref[1:2, :] * s1 + w_ref[2:3, :] * h + b_ref[...]


def _shift_down(main, prev8):
    row = lax.broadcasted_iota(jnp.int32, main.shape, 0)
    s1 = jnp.where(row == 0, prev8[7:8, :], pltpu.roll(main, 1, 0))
    s2 = jnp.where(row == 0, prev8[6:7, :], jnp.where(row == 1, prev8[7:8, :], pltpu.roll(main, 2, 0)))
    return s1, s2


def _shift_up(main, next8):
    n = main.shape[0]
    row = lax.broadcasted_iota(jnp.int32, main.shape, 0)
    u1 = jnp.where(row == n - 1, next8[0:1, :], pltpu.roll(main, n - 1, 0))
    u2 = jnp.where(row == n - 2, next8[0:1, :], jnp.where(row == n - 1, next8[1:2, :], pltpu.roll(main, n - 2, 0)))
    return u1, u2


def _ffn_fwd(h3, w_up, cw, cb, w_down, x2, tgt, g_post, name):
    s = h3.shape[0]
    tn = TN_FF
    nj = D_FF // tn
    per = D_MODEL // tn
    hb = TM // 8

    def body(h_ref, halo_ref, wg_ref, wu_ref, cwg_ref, cwu_ref, cbg_ref, cbu_ref, wd_ref, x_ref, t_ref, g_ref,
             hg_ref, hu_ref, cg_ref, cu_ref, a_ref, loss_ref, dx_ref, dy_ref, dg_ref, y_acc):
        i, j = pl.program_id(0), pl.program_id(1)
        h = h_ref[...]
        halo = halo_ref[...]
        halo = jnp.where(i > 0, halo, jnp.zeros_like(halo))
        conv = []
        for w_ref, cw_ref, cb_ref, hid_ref, c_ref in ((wg_ref, cwg_ref, cbg_ref, hg_ref, cg_ref),
                                                      (wu_ref, cwu_ref, cbu_ref, hu_ref, cu_ref)):
            hm = _dot(h, w_ref[...], NN)
            hid_ref[...] = hm.astype(BF16)
            s1, s2 = _shift_down(hm, _dot(halo, w_ref[...], NN))
            c = _conv(hm, s1, s2, cw_ref, cb_ref)
            c_ref[...] = c.astype(BF16)
            conv.append(c)
        a = (_gelu(conv[0]) * conv[1]).astype(BF16)
        a_ref[...] = a
        contrib = _dot(a, wd_ref[...], NN)

        @pl.when(j == 0)
        def _():
            y_acc[...] = contrib

        @pl.when(j > 0)
        def _():
            y_acc[...] += contrib

        @pl.when(j == nj - 1)
        def _():
            yv = y_acc[...]
            r = _rstd(yv)
            yn = yv * r
            e = x_ref[...] + yn * g_ref[...] - t_ref[...]
            part = 0.5 * jnp.sum(jnp.mean(e * e, axis=-1, keepdims=True), axis=0, keepdims=True)
            part = jnp.broadcast_to(part, (1, 128))
            dx = e * (1.0 / D_MODEL)
            dx_ref[...] = dx
            dy_ref[...] = _norm_bwd_rows(dx * g_ref[...], yn, r).astype(BF16)
            dg = jnp.sum(dx * yn, axis=0, keepdims=True)

            @pl.when(i == 0)
            def _():
                dg_ref[...] = dg
                loss_ref[...] = part

            @pl.when(i > 0)
            def _():
                dg_ref[...] += dg
                loss_ref[...] += part

    rows = pl.BlockSpec((TM, D_MODEL), lambda i, j: (i, 0))
    tile = pl.BlockSpec((TM, tn), lambda i, j: (i, j))
    wide = jax.ShapeDtypeStruct((s, D_FF), BF16)
    return pl.pallas_call(
        body, name=name, grid=(s // TM, nj),
        in_specs=[rows,
                  pl.BlockSpec((8, D_MODEL), lambda i, j: (jnp.maximum(i * hb - 1, 0), 0)),
                  pl.BlockSpec((None, D_MODEL, tn), lambda i, j: (j // per, 0, j % per)),
                  pl.BlockSpec((None, D_MODEL, tn), lambda i, j: (NDEV // 2 + j // per, 0, j % per)),
                  pl.BlockSpec((8, tn), lambda i, j: (0, j)),
                  pl.BlockSpec((8, tn), lambda i, j: (0, nj + j)),
                  pl.BlockSpec((1, tn), lambda i, j: (0, j)),
                  pl.BlockSpec((1, tn), lambda i, j: (0, nj + j)),
                  pl.BlockSpec((tn, D_MODEL), lambda i, j: (j, 0)),
                  rows, rows, pl.BlockSpec((1, D_MODEL), lambda i, j: (0, 0))],
        out_specs=[tile, tile, tile, tile, tile,
                   pl.BlockSpec((1, 128), lambda i, j: (0, 0)), rows, rows,
                   pl.BlockSpec((1, D_MODEL), lambda i, j: (0, 0))],
        out_shape=[wide, wide, wide, wide, wide,
                   jax.ShapeDtypeStruct((1, 128), F32), jax.ShapeDtypeStruct((s, D_MODEL), F32),
                   jax.ShapeDtypeStruct((s, D_MODEL), BF16), jax.ShapeDtypeStruct((1, D_MODEL), F32)],
        scratch_shapes=[pltpu.VMEM((TM, D_MODEL), F32)],
        compiler_params=_cp("arbitrary", "arbitrary"),
    )(h3, h3, w_up, w_up, cw, cw, cb, cb, w_down, x2, tgt, g_post)


def _ffn_bwd(dy3, w_down, hid_g, hid_u, conv_g, conv_u, cw, name):
    s = dy3.shape[0]
    n = s // TM
    tn = TN_FF
    nj = D_FF // tn
    hb = TM // 8
    last8 = s // 8 - 1

    def body(dy_ref, dyn_ref, wd_ref, hg_ref, hu_ref, cg_ref, cgn_ref, cu_ref, cun_ref, cwg_ref, cwu_ref,
             dhg_ref, dhu_ref, dcwg_ref, dcwu_ref, dcbg_ref, dcbu_ref):
        i = pl.program_id(1)
        first, last = i == 0, i == n - 1
        da = _dot(dy_ref[...], wd_ref[...], NT)
        dyn = dyn_ref[...]
        dyn = jnp.where(last, jnp.zeros_like(dyn), dyn)
        da_n = _dot(dyn, wd_ref[...], NT)
        c_g, c_u = cg_ref[...].astype(F32), cu_ref[...].astype(F32)
        g, dg = _gelu_and_grad(c_g)
        gn, dgn = _gelu_and_grad(cgn_ref[...].astype(F32))
        outs = ((da * c_u * dg, da_n * cun_ref[...].astype(F32) * dgn, hg_ref, cwg_ref, dhg_ref, dcwg_ref, dcbg_ref),
                (da * g, da_n * gn, hu_ref, cwu_ref, dhu_ref, dcwu_ref, dcbu_ref))
        row8 = lax.broadcasted_iota(jnp.int32, (8, tn), 0)
        for dc, dcn, h_ref, cw_ref, dh_ref, dcw_ref, dcb_ref in outs:
            u1, u2 = _shift_up(dc, dcn)
            dh_ref[...] = (cw_ref[2:3, :] * dc + cw_ref[1:2, :] * u1 + cw_ref[0:1, :] * u2).astype(BF16)
            hm = h_ref[...].astype(F32)
            dcb = jnp.sum(dc, axis=0, keepdims=True)
            dcw = jnp.where(row8 == 0, jnp.sum(hm * u2, axis=0, keepdims=True),
                            jnp.where(row8 == 1, jnp.sum(hm * u1, axis=0, keepdims=True),
                                      jnp.where(row8 == 2, jnp.sum(hm * dc, axis=0, keepdims=True), 0.0)))

            @pl.when(first)
            def _():
                dcw_ref[...] = dcw
                dcb_ref[...] = dcb

            @pl.when(i > 0)
            def _():
                dcw_ref[...] += dcw
                dcb_ref[...] += dcb

    next8 = lambda j, i: (jnp.minimum((i + 1) * hb, last8), j)
    blk = lambda j, i: (i, j)
    col = lambda j, i: (0, j)
    colu = lambda j, i: (0, nj + j)
    tile = pl.BlockSpec((TM, tn), blk)
    return pl.pallas_call(
        body, name=name, grid=(nj, n),
        in_specs=[pl.BlockSpec((TM, D_MODEL), lambda j, i: (i, 0)),
                  pl.BlockSpec((8, D_MODEL), lambda j, i: (jnp.minimum((i + 1) * hb, last8), 0)),
                  pl.BlockSpec((tn, D_MODEL), lambda j, i: (j, 0)),
                  tile, tile, tile, pl.BlockSpec((8, tn), next8), tile, pl.BlockSpec((8, tn), next8),
                  pl.BlockSpec((8, tn), col), pl.BlockSpec((8, tn), colu)],
        out_specs=[tile, tile, pl.BlockSpec((8, tn), col), pl.BlockSpec((8, tn), col),
                   pl.BlockSpec((1, tn), col), pl.BlockSpec((1, tn), col)],
        out_shape=[jax.ShapeDtypeStruct((s, D_FF), BF16), jax.ShapeDtypeStruct((s, D_FF), BF16),
                   jax.ShapeDtypeStruct((8, D_FF), F32), jax.ShapeDtypeStruct((8, D_FF), F32),
                   jax.ShapeDtypeStruct((1, D_FF), F32), jax.ShapeDtypeStruct((1, D_FF), F32)],
        compiler_params=_cp("parallel", "arbitrary"),
    )(dy3, dy3, w_down, hid_g, hid_u, conv_g, conv_g, conv_u, conv_u, cw, cw)


def _slot(p):
    return 4 * p[0] + 2 * p[1] + p[2]


def _all_gather(shards, name):
    n = len(shards)

    def body(*refs):
        ins, outs = refs[:n], refs[n:2 * n]
        send_sems, recv_sems, local_sems = refs[2 * n:]
        x, y, c = lax.axis_index("x"), lax.axis_index("y"), lax.axis_index("c")
        me, sibling = (x, y, c), (x, y, 1 - c)
        chips = [(1 - x, y), (x, 1 - y), (1 - x, 1 - y)]

        def copy(a, k, block, to, from_input=False):
            dst = outs[a].at[_slot(block)]
            return pltpu.make_async_remote_copy(
                src_ref=ins[a] if from_input else dst, dst_ref=dst,
                send_sem=send_sems.at[a, k], recv_sem=recv_sems.at[a, k],
                device_id=to, device_id_type=MESH)

        mine = [pltpu.make_async_copy(ins[a], outs[a].at[_slot(me)], local_sems.at[a]) for a in range(n)]
        for cp in mine:
            cp.start()
        first = []
        for a in range(n):
            first.append(copy(a, 0, me, sibling, True))
            first += [copy(a, 1 + j, me, (*chip, c), True) for j, chip in enumerate(chips)]
        for cp in first:
            cp.start()
        passed = []
        for j, chip in enumerate(chips):
            for a in range(n):
                copy(a, 1 + j, (*chip, c), me).wait_recv()
                fwd = copy(a, 4 + j, (*chip, c), sibling)
                fwd.start()
                passed.append(fwd)
        for a in range(n):
            copy(a, 0, sibling, me).wait_recv()
            for j, chip in enumerate(chips):
                copy(a, 4 + j, (*chip, 1 - c), me).wait_recv()
        for cp in first + passed:
            cp.wait_send()
        for cp in mine:
            cp.wait()

    any_spec = pl.BlockSpec(memory_space=pl.ANY)
    return pl.pallas_call(
        body, name=name,
        in_specs=[any_spec] * n, out_specs=[any_spec] * n,
        out_shape=[jax.ShapeDtypeStruct((NDEV,) + s.shape, s.dtype) for s in shards],
        scratch_shapes=[pltpu.SemaphoreType.DMA((n, 7)), pltpu.SemaphoreType.DMA((n, 7)),
                        pltpu.SemaphoreType.DMA((n,))],
    )(*shards)


def _peer_list(x, y, c):
    return [(1 - x if m & 4 else x, 1 - y if m & 2 else y, 1 - c if m & 1 else c) for m in range(1, NDEV)]


def _exchange_copies(src_refs, land_refs, send_sems, recv_sems, gather):
    x, y, c = lax.axis_index("x"), lax.axis_index("y"), lax.axis_index("c")
    me = (x, y, c)
    copies = []
    for m, peer in enumerate(_peer_list(x, y, c)):
        for a in range(len(src_refs)):
            copies.append(pltpu.make_async_remote_copy(
                src_ref=src_refs[a] if gather else src_refs[a].at[_slot(peer)], dst_ref=land_refs[a].at[_slot(me)],
                send_sem=send_sems.at[a * (NDEV - 1) + m], recv_sem=recv_sems.at[a * (NDEV - 1) + m],
                device_id=peer, device_id_type=MESH))
    return copies


def _exchange_start(srcs, lands, after, gather, name):
    n = len(srcs)
    hbm = pl.BlockSpec(memory_space=pltpu.HBM)

    def body(*refs):
        for cp in _exchange_copies(refs[:n], refs[n:2 * n], refs[2 * n + 1], refs[2 * n + 2], gather):
            cp.start()
        token = refs[-1]
        token[...] = jnp.zeros_like(token)

    outs = pl.pallas_call(
        body, name=name,
        out_shape=(pltpu.SemaphoreType.DMA((n * (NDEV - 1),)), pltpu.SemaphoreType.DMA((n * (NDEV - 1),)),
                   *[pltpu.HBM(a.shape, a.dtype) for a in list(srcs) + list(lands)],
                   jax.ShapeDtypeStruct((8, 128), F32)),
        in_specs=[hbm] * (2 * n) + [pl.BlockSpec(memory_space=pl.ANY)],
        out_specs=(pl.BlockSpec(memory_space=pltpu.SEMAPHORE), pl.BlockSpec(memory_space=pltpu.SEMAPHORE),
                   *[hbm] * (2 * n), pl.BlockSpec(memory_space=pltpu.VMEM)),
        input_output_aliases={i: 2 + i for i in range(2 * n)},
        compiler_params=pltpu.CompilerParams(has_side_effects=pltpu.SideEffectType.DATAFLOW_SIDE_EFFECTING),
    )(*[pltpu.with_memory_space_constraint(a, pltpu.HBM) for a in list(srcs) + list(lands)], after)
    return outs[0], outs[1], outs[2:2 + n], outs[2 + n:2 + 2 * n], outs[-1]


def _exchange_wait(send_sems, recv_sems, srcs, lands, after, gather, name):
    n = len(srcs)
    hbm = pl.BlockSpec(memory_space=pltpu.HBM)

    def body(*refs):
        for cp in _exchange_copies(refs[:n], refs[n:2 * n], refs[2 * n], refs[2 * n + 1], gather):
            cp.wait_send()
            cp.wait_recv()

    outs = pl.pallas_call(
        body, name=name,
        out_shape=tuple(pltpu.HBM(a.shape, a.dtype) for a in list(srcs) + list(lands)),
        in_specs=[hbm] * (2 * n) + [pl.BlockSpec(memory_space=pltpu.SEMAPHORE)] * 2 + [pl.BlockSpec(memory_space=pl.ANY)],
        out_specs=tuple([hbm] * (2 * n)),
        input_output_aliases={i: i for i in range(2 * n)},
        compiler_params=pltpu.CompilerParams(has_side_effects=pltpu.SideEffectType.DATAFLOW_SIDE_EFFECTING),
    )(*srcs, *lands, send_sems, recv_sems, after)
    return outs[n:]


def _own_slot(block):
    me = 4 * lax.axis_index("x") + 2 * lax.axis_index("y") + lax.axis_index("c")
    return lax.dynamic_update_slice(lax.empty((NDEV,) + block.shape, block.dtype), block[None], (me, 0, 0))


def _adamw(parts, w, m, v, name):
    r, c = w.shape
    tr = r if r * c <= 160 * 1024 else max(8, (160 * 1024 // c) // 8 * 8)
    while r % tr:
        tr -= 8
    bc1 = 1.0 - ADAM_B1 ** ADAM_STEP
    bc2 = 1.0 - ADAM_B2 ** ADAM_STEP

    def body(p_ref, w_ref, m_ref, v_ref, g_ref, d_ref, mo_ref, vo_ref):
        g = p_ref[0].astype(F32)
        for d in range(1, NDEV):
            g = g + p_ref[d].astype(F32)
        g_ref[...] = g
        mn = ADAM_B1 * m_ref[...] + (1.0 - ADAM_B1) * g
        vn = ADAM_B2 * v_ref[...] + (1.0 - ADAM_B2) * (g * g)
        mo_ref[...] = mn
        vo_ref[...] = vn
        d_ref[...] = -ADAM_LR * ((mn / bc1) / (jnp.sqrt(vn / bc2) + ADAM_EPS) + ADAM_WD * w_ref[...])

    spec = pl.BlockSpec((tr, c), lambda i: (i, 0))
    return pl.pallas_call(
        body, name=name, grid=(r // tr,),
        in_specs=[pl.BlockSpec((NDEV, tr, c), lambda i: (0, i, 0)), spec, spec, spec],
        out_specs=[spec] * 4, out_shape=[jax.ShapeDtypeStruct((r, c), F32)] * 4,
        compiler_params=_cp("parallel"),
    )(parts, w, m, v)


def _local_step(x, mem, tgt, gains, b_forget, w_pool, pool_scale, conv_b, w_in,
                mix_weights, ffn_weights, send_in_grad, send_mix_grads, send_ffn_grads):
    w_f = w_in[:, F_COL:]
    b_pad = jnp.pad(b_forget, ((0, 0), (0, 128 - FOX_HEADS)))
    wbd = jnp.zeros((D_POOL, D_POOL), F32)
    for g in range(4):
        wbd = wbd.at[64 * g:64 * g + 64, 64 * g:64 * g + 64].set(w_pool[g])
    wbd = wbd.astype(BF16)
    scale = pool_scale.reshape(1, D_POOL)

    h1 = _norm_fwd(x, gains["mix_pre"], "norm_mix_pre")
    proj = _mm(h1, w_in, "nn", BF16, 1024, 384, 1024, "proj_in")
    fraw = _mm(h1, w_f, "nn", F32, 1024, 128, 1024, "proj_gate")
    flog, cum = _gate_cumsum(fraw, b_pad, "gate_cumsum")
    aq, ak = _fox_operands(cum, "fox_operands")
    ycat, aqb = _fox_fwd(proj, aq, ak, "fox_fwd")
    ycat = _pool_fwd(proj, wbd, scale, ycat, "pool_fwd")
    w_mix, w_xq, w_xo, w_xkv = mix_weights(ycat)
    y1, x1, h2 = _mm_rows(ycat, w_mix, "nn", 1024, "mix_out", [x], [gains["mix_post"], gains["xa_pre"]],
                          [F32, F32, BF16], _epi_resid)
    q2 = _mm(h2, w_xq, "nn", BF16, 1024, 1024, 1024, "xa_q")
    mem_n = _norm_fwd(mem, gains["mem"], "norm_mem")
    kv = _mm(mem_n, w_xkv, "nn", BF16, mem.shape[0], 256, 1024, "xa_kv", b_cols=256)
    o2 = _xattn_fwd(q2, kv, "xattn_fwd")
    y2, x2, h3 = _mm_rows(o2, w_xo, "nn", 1024, "xa_out", [x1], [gains["xa_post"], gains["ffn_pre"]],
                          [F32, F32, BF16], _epi_resid)
    w_up, w_down, cw = ffn_weights(h3)
    hid_g, hid_u, conv_g, conv_u, act, loss, dx3, dy3, dg_ffn_post = _ffn_fwd(
        h3, w_up, cw, conv_b, w_down, x2, tgt, gains["ffn_post"], "ffn_fwd")

    dhid_g, dhid_u, dcw_g, dcw_u, dcb_g, dcb_u = _ffn_bwd(dy3, w_down, hid_g, hid_u, conv_g, conv_u, cw, "ffn_bwd")
    d_w_down = _mm(act, dy3, "tn", BF16, 1024, 1024, 1024, "dw_down")
    d_w_up = _mm(h3, [dhid_g, dhid_u], "tn", BF16, 1024, 1024, 1024, "dw_up", out_cols=1024)
    sent = send_ffn_grads(d_w_up, d_w_down, jnp.concatenate([dcw_g, dcw_u], axis=1))
    dh3 = _mm([dhid_g, dhid_u], w_up, "nt", F32, 1024, 1024, 1024, "dh_ffn", b_cols=1024, after=sent)
    dx2, dg_ffn_pre, dy2, dg_xa_post = _norm_bwd(dh3, x2, dx3, gains["ffn_pre"], "norm_bwd_ffn",
                                                 prev=(y2, gains["xa_post"]))
    do2 = _mm(dy2, w_xo, "nt", BF16, 1024, 1024, 1024, "d_xa_out")
    d_w_xo = _mm(o2, dy2, "tn", BF16, 1024, 1024, 1024, "dw_xo")
    dq2, dkv = _xattn_bwd(q2, kv, do2, "xattn_bwd")
    dkv = dkv.astype(BF16)
    dx1, dg_xa_pre, dy1, dg_mix_post = _mm_rows(
        dq2, w_xq, "nt", 1024, "dh_xa", [x1, dx2, y1], [gains["xa_pre"], gains["mix_post"]],
        [F32, "sum", BF16, "sum"], _epi_norm_bwd)
    d_w_xq = _mm(h2, dq2, "tn", BF16, 1024, 1024, 1024, "dw_xq")
    dmem_n = _mm(dkv, w_xkv, "nt", F32, mem.shape[0], 1024, 256, "d_mem", b_cols=256)
    d_w_xkv = _mm(mem_n, dkv, "tn", BF16, 1024, 256, mem.shape[0], "dw_xkv", out_cols=256)
    _, dg_mem = _norm_bwd(dmem_n, mem, jnp.zeros_like(mem), gains["mem"], "norm_bwd_mem")
    dycat = _mm(dy1, w_mix, "nt", BF16, 1024, 1024, 1024, "d_mix_out")
    d_w_mix = _mm(ycat, dy1, "tn", BF16, 1024, 1024, 1024, "dw_mix")
    sent_mix = send_mix_grads(d_w_mix, d_w_xq, d_w_xo, d_w_xkv)
    ad = _fox_do_operand(dycat, ycat, sent_mix, "fox_do_operand")
    dq, dk, dv, qaux, kaux = _fox_bwd(proj, dycat, aqb, ak, ad, "fox_bwd")
    du, d_wbd, d_scale = _pool_bwd(proj, dycat, wbd, scale, "pool_bwd")
    df, db_f = _gate_bwd(qaux, kaux, flog, "gate_bwd")
    dproj = jnp.concatenate([du, dq, dk, dv, df], axis=1)
    sent_in = send_in_grad(_mm(h1, dproj, "tn", BF16, 1024, 384, 1024, "dw_in"))
    grad_x, dg_mix_pre = _mm_rows(dproj, w_in, "nt", 896, "dh_mix", [x, dx1], [gains["mix_pre"]],
                                  [F32, "sum"], _epi_norm_bwd, after=sent_in)

    small = dict(
        mix_pre=dg_mix_pre, mix_post=dg_mix_post, mem=dg_mem, xa_pre=dg_xa_pre, xa_post=dg_xa_post,
        ffn_pre=dg_ffn_pre, ffn_post=dg_ffn_post,
        conv_b=jnp.concatenate([dcb_g, dcb_u], axis=1),
        w_pool=jnp.stack([d_wbd[64 * g:64 * g + 64, 64 * g:64 * g + 64] for g in range(4)]),
        pool_scale=d_scale.reshape(4, 64),
        b_forget=db_f[:, :FOX_HEADS],
    )
    return loss, grad_x, small


SMALL_ORDER = ("mix_pre", "mix_post", "mem", "xa_pre", "xa_post", "ffn_pre", "ffn_post", "conv_b",
               "w_pool", "pool_scale", "b_forget")
SMALL_ROWS = 256


def _pack_small(d):
    flat = jnp.concatenate([d[k].reshape(-1).astype(F32) for k in SMALL_ORDER])
    return jnp.pad(flat, (0, SMALL_ROWS * 128 - flat.shape[0])).reshape(SMALL_ROWS, 128)


def _unpack_small(a, like):
    flat = a.reshape(-1)
    out, off = {}, 0
    for k in SMALL_ORDER:
        n = like[k].size
        out[k] = flat[off:off + n].reshape(like[k].shape)
        off += n
    return out


def kernel(x, mem, norm_mix_pre, norm_mix_post, w_in, b_forget, w_pool, pool_scale, w_mix_out, norm_mem, norm_xa_pre, norm_xa_post, w_xq, w_xkv, w_xo, norm_ffn_pre, norm_ffn_post, w_up, conv_w, conv_b, w_down, loss_target, m_norm_mix_pre, m_norm_mix_post, m_w_in, m_b_forget, m_w_pool, m_pool_scale, m_w_mix_out, m_norm_mem, m_norm_xa_pre, m_norm_xa_post, m_w_xq, m_w_xkv, m_w_xo, m_norm_ffn_pre, m_norm_ffn_post, m_w_up, m_conv_w, m_conv_b, m_w_down, v_norm_mix_pre, v_norm_mix_post, v_w_in, v_b_forget, v_w_pool, v_pool_scale, v_w_mix_out, v_norm_mem, v_norm_xa_pre, v_norm_xa_post, v_w_xq, v_w_xkv, v_w_xo, v_norm_ffn_pre, v_norm_ffn_post, v_w_up, v_conv_w, v_conv_b, v_w_down):
    names = ("norm_mix_pre", "norm_mix_post", "w_in", "b_forget", "w_pool", "pool_scale", "w_mix_out", "norm_mem",
             "norm_xa_pre", "norm_xa_post", "w_xq", "w_xkv", "w_xo", "norm_ffn_pre", "norm_ffn_post", "w_up",
             "conv_w", "conv_b", "w_down")
    w = dict(zip(names, (norm_mix_pre, norm_mix_post, w_in, b_forget, w_pool, pool_scale, w_mix_out, norm_mem,
                         norm_xa_pre, norm_xa_post, w_xq, w_xkv, w_xo, norm_ffn_pre, norm_ffn_post, w_up,
                         conv_w, conv_b, w_down)))
    mo = dict(zip(names, (m_norm_mix_pre, m_norm_mix_post, m_w_in, m_b_forget, m_w_pool, m_pool_scale, m_w_mix_out,
                          m_norm_mem, m_norm_xa_pre, m_norm_xa_post, m_w_xq, m_w_xkv, m_w_xo, m_norm_ffn_pre,
                          m_norm_ffn_post, m_w_up, m_conv_w, m_conv_b, m_w_down)))
    vo = dict(zip(names, (v_norm_mix_pre, v_norm_mix_post, v_w_in, v_b_forget, v_w_pool, v_pool_scale, v_w_mix_out,
                          v_norm_mem, v_norm_xa_pre, v_norm_xa_post, v_w_xq, v_w_xkv, v_w_xo, v_norm_ffn_pre,
                          v_norm_ffn_post, v_w_up, v_conv_w, v_conv_b, v_w_down)))

    big_names = ("w_in", "w_mix_out", "w_xq", "w_xo", "w_xkv", "w_up", "w_down")
    shards = {k: w[k][0].astype(BF16) for k in big_names}
    shards["w_in"] = jnp.pad(shards["w_in"], ((0, 0), (0, D_IN_PAD - shards["w_in"].shape[1])))
    conv_w_sh = jnp.pad(conv_w[0, :, 0, :], ((0, 5), (0, 0)))
    (g_in,) = _all_gather([shards["w_in"]], "gather_w_in")
    mix_srcs = [shards[k] for k in ("w_mix_out", "w_xq", "w_xo", "w_xkv")]
    mix_flight = _exchange_start(mix_srcs, [_own_slot(a) for a in mix_srcs], g_in, True, "gather_mix_start")
    ffn_srcs = [shards["w_up"], shards["w_down"], conv_w_sh]
    ffn_flight = _exchange_start(ffn_srcs, [_own_slot(a) for a in ffn_srcs], mix_flight[4], True, "gather_ffn_start")
    my_slot = 4 * lax.axis_index("x") + 2 * lax.axis_index("y") + lax.axis_index("c")
    own_block = lambda a: _own_slot(lax.dynamic_index_in_dim(a, my_slot, 0, keepdims=False))
    by_rows = lambda a: a.reshape(NDEV, a.shape[0] // NDEV, a.shape[1])
    by_cols = lambda a: a.reshape(a.shape[0], NDEV, a.shape[1] // NDEV).transpose(1, 0, 2)
    grad_flight = {}

    def mix_weights(after):
        g_mix, g_xq, g_xo, g_xkv = _exchange_wait(*mix_flight[:4], after, True, "gather_mix_wait")
        return (g_mix.reshape(D_MODEL, D_MODEL), g_xq.reshape(D_MODEL, D_MODEL), g_xo.reshape(D_MODEL, D_MODEL), g_xkv)

    def ffn_weights(after):
        g_up, g_down, g_cw = _exchange_wait(*ffn_flight[:4], after, True, "gather_ffn_wait")
        return g_up, g_down.reshape(D_FF, D_MODEL), g_cw.transpose(1, 0, 2).reshape(8, 2 * D_FF)

    def send_ffn_grads(d_w_up, d_w_down, d_cw):
        srcs = [d_w_up, by_rows(d_w_down), by_cols(d_cw)]
        grad_flight["ffn"] = _exchange_start(srcs, [own_block(a) for a in srcs], d_w_up, False, "scatter_ffn_start")
        return grad_flight["ffn"][4]

    def send_mix_grads(d_w_mix, d_w_xq, d_w_xo, d_w_xkv):
        srcs = [by_rows(d_w_mix), by_rows(d_w_xq), by_rows(d_w_xo), d_w_xkv]
        grad_flight["mix"] = _exchange_start(srcs, [own_block(a) for a in srcs], d_w_mix, False, "scatter_mix_start")
        return grad_flight["mix"][4]

    def send_in_grad(d_w_in):
        srcs = [by_rows(d_w_in)]
        grad_flight["in"] = _exchange_start(srcs, [own_block(a) for a in srcs], d_w_in, False, "scatter_in_start")
        return grad_flight["in"][4]

    gains = dict(mix_pre=norm_mix_pre + ffn_flight[4][0, 0], mix_post=norm_mix_post, mem=norm_mem, xa_pre=norm_xa_pre,
                 xa_post=norm_xa_post, ffn_pre=norm_ffn_pre, ffn_post=norm_ffn_post)
    loss, grad_x, small = _local_step(
        x[0], mem[0], loss_target[0], gains, b_forget, w_pool[0], pool_scale[0], conv_b,
        g_in.reshape(D_MODEL, D_IN_PAD), mix_weights, ffn_weights, send_in_grad, send_mix_grads, send_ffn_grads)

    p_up, p_down, p_cw = _exchange_wait(*grad_flight["ffn"][:4], grad_x, False, "scatter_ffn_wait")
    p_mix, p_xq, p_xo, p_xkv = _exchange_wait(*grad_flight["mix"][:4], grad_x, False, "scatter_mix_wait")
    (p_in,) = _exchange_wait(*grad_flight["in"][:4], grad_x, False, "scatter_in_wait")
    parts = [p_in, p_mix, p_xq, p_xo, p_xkv, p_up, p_down, p_cw]
    small_like = dict(mix_pre=norm_mix_pre, mix_post=norm_mix_post, mem=norm_mem, xa_pre=norm_xa_pre,
                      xa_post=norm_xa_post, ffn_pre=norm_ffn_pre, ffn_post=norm_ffn_post, conv_b=conv_b,
                      w_pool=w_pool, pool_scale=pool_scale, b_forget=b_forget)
    small = {k: small[k].reshape(small_like[k].shape) for k in SMALL_ORDER}
    (small_parts,) = _all_gather([_pack_small(small)], "gather_small_grads")

    res = {}
    for k, p in zip(big_names, parts[:7]):
        if k == "w_in":
            p = p[:, :, :w_in.shape[2]]
        res[k] = [a[None] for a in _adamw(p, w[k][0], mo[k][0], vo[k][0], "adamw_" + k)]
    pad_cw = lambda a: jnp.pad(a[0, :, 0, :], ((0, 5), (0, 0)))
    res["conv_w"] = [a[:3][None, :, None, :] for a in
                     _adamw(parts[7], pad_cw(conv_w), pad_cw(m_conv_w), pad_cw(v_conv_w), "adamw_conv_w")]
    key_of = dict(mix_pre="norm_mix_pre", mix_post="norm_mix_post", mem="norm_mem", xa_pre="norm_xa_pre",
                  xa_post="norm_xa_post", ffn_pre="norm_ffn_pre", ffn_post="norm_ffn_post", conv_b="conv_b",
                  w_pool="w_pool", pool_scale="pool_scale", b_forget="b_forget")
    pack_of = lambda src: _pack_small({k: src[key_of[k]] for k in SMALL_ORDER})
    small_out = _adamw(small_parts, pack_of(w), pack_of(mo), pack_of(vo), "adamw_small")
    small_out = [_unpack_small(a, small_like) for a in small_out]
    for k in SMALL_ORDER:
        res[key_of[k]] = [so[k] for so in small_out]

    total = lax.psum(loss[0, 0], ("x", "y", "c"))
    outs = [total, grad_x[None]]
    for idx in range(4):
        outs += [res[k][idx] for k in names]
    return tuple(outs)
```

```python
import functools
import math

import jax
import jax.numpy as jnp
from jax import lax
from jax.experimental import pallas as pl
from jax.experimental.pallas import tpu as pltpu

F32 = jnp.float32
BF16 = jnp.bfloat16

NDEV = 8
D_MODEL = 1024
D_POOL = 256
D_FOX = 768
FOX_HEADS = 12
HEAD_PAIRS = FOX_HEADS // 2
XA_HEADS = 4
XA_DIM = 256
D_FF = 4096
D_IN_PAD = 2688
F_COL = 2560
POOL_HALO = 16
NORM_EPS = 1e-6
NEG = -1e30

ADAM_LR = 0.001
ADAM_B1 = 0.9
ADAM_B2 = 0.999
ADAM_EPS = 1e-08
ADAM_WD = 0.01
ADAM_STEP = 10

TM = 512
TQ = 512
TN_FF = 512
VMEM_LIMIT = 56 * 1024 * 1024
MESH = pl.DeviceIdType.MESH


def _cp(*sem):
    return pltpu.CompilerParams(dimension_semantics=sem, vmem_limit_bytes=VMEM_LIMIT)


def _dot(a, b, dims):
    return lax.dot_general(a, b, (dims, ((), ())), preferred_element_type=F32)


NN = ((1,), (0,))
NT = ((1,), (1,))
TN = ((0,), (0,))


def _mm(a, b, mode, out_dtype, tm, tn, tk, name, b_cols=None, out_cols=None, after=None):
    a_list = list(a) if isinstance(a, (list, tuple)) else [a]
    b_list = list(b) if isinstance(b, (list, tuple)) else [b]
    assert len(a_list) == 1 or len(b_list) == 1
    if mode == "tn":
        K, M = a_list[0].shape
        assert len(a_list) == 1
        Ns = [x.shape[1] for x in b_list]
        N = sum(Ns)
        assert b_cols is None
    else:
        assert len(b_list) == 1
        M = a_list[0].shape[0]
        Ks = [x.shape[1] for x in a_list]
        K = sum(Ks)
        if b_cols is None:
            N = b_list[0].shape[0] if mode == "nt" else b_list[0].shape[1]
        else:
            N = b_list[0].shape[1] if mode == "nt" else NDEV * b_cols
    assert M % tm == 0 and N % tn == 0 and K % tk == 0, (name, M, N, K)
    grid = (M // tm, N // tn, K // tk)
    nk = grid[2]
    dims = {"nn": NN, "nt": NT, "tn": TN}[mode]

    in_specs = []
    if mode == "tn":
        in_specs.append(pl.BlockSpec((tk, tm), lambda i, j, k: (k, i)))
        if len(b_list) == 1:
            in_specs.append(pl.BlockSpec((tk, tn), lambda i, j, k: (k, j)))
        else:
            nj1 = Ns[0] // tn
            in_specs.append(pl.BlockSpec((tk, tn), lambda i, j, k: (k, jnp.minimum(j, nj1 - 1))))
            in_specs.append(pl.BlockSpec((tk, tn), lambda i, j, k: (k, jnp.maximum(j - nj1, 0))))
    else:
        if len(a_list) == 1:
            in_specs.append(pl.BlockSpec((tm, tk), lambda i, j, k: (i, k)))
        else:
            nk1 = Ks[0] // tk
            in_specs.append(pl.BlockSpec((tm, tk), lambda i, j, k: (i, jnp.minimum(k, nk1 - 1))))
            in_specs.append(pl.BlockSpec((tm, tk), lambda i, j, k: (i, jnp.maximum(k - nk1, 0))))
        if b_cols is None:
            if mode == "nn":
                in_specs.append(pl.BlockSpec((tk, tn), lambda i, j, k: (k, j)))
            else:
                in_specs.append(pl.BlockSpec((tn, tk), lambda i, j, k: (j, k)))
        else:
            if mode == "nn":
                per = b_cols // tn
                in_specs.append(pl.BlockSpec((None, tk, tn), lambda i, j, k: (j // per, k, j % per)))
            else:
                per = b_cols // tk
                in_specs.append(pl.BlockSpec((None, tn, tk), lambda i, j, k: (k // per, j, k % per)))
    if out_cols is None:
        out_spec = pl.BlockSpec((tm, tn), lambda i, j, k: (i, j))
        out_shape = jax.ShapeDtypeStruct((M, N), out_dtype)
    else:
        pero = out_cols // tn
        out_spec = pl.BlockSpec((None, tm, tn), lambda i, j, k: (j // pero, i, j % pero))
        out_shape = jax.ShapeDtypeStruct((NDEV, M, out_cols), out_dtype)

    two_a = len(a_list) == 2
    two_b = len(b_list) == 2
    extra = []
    if after is not None:
        in_specs.append(pl.BlockSpec(memory_space=pl.ANY))
        extra.append(after)

    def body(*refs):
        o_ref, acc_ref = refs[-2], refs[-1]
        j = pl.program_id(1)
        k = pl.program_id(2)

        @pl.when(k == 0)
        def _():
            acc_ref[...] = jnp.zeros_like(acc_ref)

        if two_a:
            a1, a2, b1 = refs[0], refs[1], refs[2]
            nk1_ = Ks[0] // tk

            @pl.when(k < nk1_)
            def _():
                acc_ref[...] += _dot(a1[...], b1[...], dims)

            @pl.when(k >= nk1_)
            def _():
                acc_ref[...] += _dot(a2[...], b1[...], dims)
        elif two_b:
            a1, b1, b2 = refs[0], refs[1], refs[2]
            nj1_ = Ns[0] // tn

            @pl.when(j < nj1_)
            def _():
                acc_ref[...] += _dot(a1[...], b1[...], dims)

            @pl.when(j >= nj1_)
            def _():
                acc_ref[...] += _dot(a1[...], b2[...], dims)
        else:
            acc_ref[...] += _dot(refs[0][...], refs[1][...], dims)

        @pl.when(k == nk - 1)
        def _():
            o_ref[...] = acc_ref[...].astype(o_ref.dtype)

    return pl.pallas_call(
        body, name=name, grid=grid, in_specs=in_specs, out_specs=out_spec, out_shape=out_shape,
        scratch_shapes=[pltpu.VMEM((tm, tn), F32)],
        compiler_params=_cp("parallel", "parallel", "arbitrary"),
    )(*a_list, *b_list, *extra)


def _rstd(x):
    return lax.rsqrt(jnp.mean(x * x, axis=-1, keepdims=True) + NORM_EPS)


def _norm_bwd_rows(dxn, xn, r):
    return r * (dxn - xn * jnp.mean(dxn * xn, axis=-1, keepdims=True))


def _row_spec(tm, d):
    return pl.BlockSpec((tm, d), lambda i: (i, 0))


def _vec_spec(d):
    return pl.BlockSpec((1, d), lambda i: (0, 0))


def _mm_rows(a, b, mode, tk, name, rows, vecs, outs, epilogue, b_cols=None, after=None):
    a_list = list(a) if isinstance(a, (list, tuple)) else [a]
    m = a_list[0].shape[0]
    ks = [x.shape[1] for x in a_list]
    n = D_MODEL
    nk = sum(ks) // tk
    dims = NN if mode == "nn" else NT
    if len(a_list) == 1:
        in_specs = [pl.BlockSpec((TM, tk), lambda i, k: (i, k))]
    else:
        nk1 = ks[0] // tk
        in_specs = [pl.BlockSpec((TM, tk), lambda i, k: (i, jnp.minimum(k, nk1 - 1))),
                    pl.BlockSpec((TM, tk), lambda i, k: (i, jnp.maximum(k - nk1, 0)))]
    if mode == "nn":
        in_specs.append(pl.BlockSpec((tk, n), lambda i, k: (k, 0)))
    elif b_cols is None:
        in_specs.append(pl.BlockSpec((n, tk), lambda i, k: (0, k)))
    else:
        per = b_cols // tk
        in_specs.append(pl.BlockSpec((None, n, tk), lambda i, k: (k // per, 0, k % per)))
    in_specs += [pl.BlockSpec((TM, n), lambda i, k: (i, 0))] * len(rows)
    in_specs += [pl.BlockSpec((1, n), lambda i, k: (0, 0))] * len(vecs)
    extra = []
    if after is not None:
        in_specs.append(pl.BlockSpec(memory_space=pl.ANY))
        extra.append(after)
    out_specs, out_shape = [], []
    for o in outs:
        if o == "sum":
            out_specs.append(pl.BlockSpec((1, n), lambda i, k: (0, 0)))
            out_shape.append(jax.ShapeDtypeStruct((1, n), F32))
        else:
            out_specs.append(pl.BlockSpec((TM, n), lambda i, k: (i, 0)))
            out_shape.append(jax.ShapeDtypeStruct((m, n), o))
    na, nr, nv = len(a_list), len(rows), len(vecs)

    def body(*refs):
        a_refs, b_ref = refs[:na], refs[na]
        row_refs = refs[na + 1:na + 1 + nr]
        vec_refs = refs[na + 1 + nr:na + 1 + nr + nv]
        out_refs = refs[len(refs) - 1 - len(outs):len(refs) - 1]
        acc_ref = refs[-1]
        i, k = pl.program_id(0), pl.program_id(1)

        @pl.when(k == 0)
        def _():
            acc_ref[...] = jnp.zeros_like(acc_ref)

        if na == 1:
            acc_ref[...] += _dot(a_refs[0][...], b_ref[...], dims)
        else:
            nk1_ = ks[0] // tk

            @pl.when(k < nk1_)
            def _():
                acc_ref[...] += _dot(a_refs[0][...], b_ref[...], dims)

            @pl.when(k >= nk1_)
            def _():
                acc_ref[...] += _dot(a_refs[1][...], b_ref[...], dims)

        @pl.when(k == nk - 1)
        def _():
            vals = epilogue(acc_ref[...], [r[...] for r in row_refs], [v[...] for v in vec_refs])
            for o, ref, val in zip(outs, out_refs, vals):
                if o == "sum":
                    @pl.when(i == 0)
                    def _():
                        ref[...] = val

                    @pl.when(i > 0)
                    def _():
                        ref[...] += val
                else:
                    ref[...] = val.astype(o)

    return pl.pallas_call(
        body, name=name, grid=(m // TM, nk), in_specs=in_specs, out_specs=out_specs, out_shape=out_shape,
        scratch_shapes=[pltpu.VMEM((TM, n), F32)],
        compiler_params=_cp("arbitrary", "arbitrary"),
    )(*a_list, b, *rows, *vecs, *extra)


def _epi_resid(y, rows, vecs):
    (x_in,), (g_post, g_next) = rows, vecs
    xo = x_in + y * _rstd(y) * g_post
    return y, xo, xo * _rstd(xo) * g_next


def _epi_norm_bwd(dh, rows, vecs):
    x, dx_res = rows[0], rows[1]
    r = _rstd(x)
    xn = x * r
    dx = dx_res + _norm_bwd_rows(dh * vecs[0], xn, r)
    res = [dx, jnp.sum(dh * xn, axis=0, keepdims=True)]
    if len(rows) == 3:
        y = rows[2]
        r2 = _rstd(y)
        yn = y * r2
        res += [_norm_bwd_rows(dx * vecs[1], yn, r2), jnp.sum(dx * yn, axis=0, keepdims=True)]
    return res


def _norm_fwd(x, g, name):
    s, d = x.shape
    tm = min(TM, s)

    def body(x_ref, g_ref, h_ref):
        xv = x_ref[...]
        h_ref[...] = (xv * _rstd(xv) * g_ref[...]).astype(BF16)

    return pl.pallas_call(
        body, name=name, grid=(s // tm,), in_specs=[_row_spec(tm, d), _vec_spec(d)],
        out_specs=_row_spec(tm, d), out_shape=jax.ShapeDtypeStruct((s, d), BF16),
        compiler_params=_cp("parallel"),
    )(x, g)


def _norm_bwd(dh, x, dx_res, g_pre, name, prev=None):
    s, d = x.shape
    tm = min(TM, s)
    has_prev = prev is not None

    def body(*refs):
        if has_prev:
            dh_ref, x_ref, dr_ref, g_ref, y_ref, gp_ref, dx_ref, dg_ref, dy_ref, dgp_ref = refs
        else:
            dh_ref, x_ref, dr_ref, g_ref, dx_ref, dg_ref = refs
        i = pl.program_id(0)
        xv = x_ref[...]
        r = _rstd(xv)
        xn = xv * r
        dhv = dh_ref[...].astype(F32)
        dx = dr_ref[...] + _norm_bwd_rows(dhv * g_ref[...], xn, r)
        dx_ref[...] = dx
        dg = jnp.sum(dhv * xn, axis=0, keepdims=True)

        @pl.when(i == 0)
        def _():
            dg_ref[...] = dg

        @pl.when(i > 0)
        def _():
            dg_ref[...] += dg

        if has_prev:
            yv = y_ref[...]
            r2 = _rstd(yv)
            yn = yv * r2
            dy_ref[...] = _norm_bwd_rows(dx * gp_ref[...], yn, r2).astype(BF16)
            dgp = jnp.sum(dx * yn, axis=0, keepdims=True)

            @pl.when(i == 0)
            def _():
                dgp_ref[...] = dgp

            @pl.when(i > 0)
            def _():
                dgp_ref[...] += dgp

    in_specs = [_row_spec(tm, d), _row_spec(tm, d), _row_spec(tm, d), _vec_spec(d)]
    out_specs = [_row_spec(tm, d), _vec_spec(d)]
    out_shape = [jax.ShapeDtypeStruct((s, d), F32), jax.ShapeDtypeStruct((1, d), F32)]
    args = [dh, x, dx_res, g_pre]
    if has_prev:
        in_specs += [_row_spec(tm, d), _vec_spec(d)]
        out_specs += [_row_spec(tm, d), _vec_spec(d)]
        out_shape += [jax.ShapeDtypeStruct((s, d), BF16), jax.ShapeDtypeStruct((1, d), F32)]
        args += list(prev)
    return pl.pallas_call(
        body, name=name, grid=(s // tm,), in_specs=in_specs, out_specs=out_specs, out_shape=out_shape,
        compiler_params=_cp("arbitrary"),
    )(*args)


def _split3(v):
    hi = v.astype(BF16)
    r1 = v - hi.astype(F32)
    mid = r1.astype(BF16)
    lo = (r1 - mid.astype(F32)).astype(BF16)
    return hi, mid, lo


def _tri_dot(tri, v):
    hi, mid, lo = _split3(v)
    return _dot(tri, hi, NN) + _dot(tri, mid, NN) + _dot(tri, lo, NN)


def _gate_cumsum(fraw, b_pad, name):
    s = fraw.shape[0]

    def body(f_ref, b_ref, flog_ref, cum_ref, carry_ref):
        i = pl.program_id(0)

        @pl.when(i == 0)
        def _():
            carry_ref[...] = jnp.zeros_like(carry_ref)

        flog = f_ref[...] + b_ref[...]
        flog_ref[...] = flog
        lf = jnp.minimum(flog, 0.0) - jnp.log(1.0 + jnp.exp(-jnp.abs(flog)))
        lane = lax.broadcasted_iota(jnp.int32, (1, 128), 1)
        lf = jnp.where(lane < FOX_HEADS, lf, 0.0)
        row = lax.broadcasted_iota(jnp.int32, (TM, TM), 0)
        col = lax.broadcasted_iota(jnp.int32, (TM, TM), 1)
        tri = (row >= col).astype(BF16)
        cum = _tri_dot(tri, lf) + carry_ref[...]
        cum_ref[...] = cum
        carry_ref[...] = cum[TM - 1:TM, :]

    return pl.pallas_call(
        body, name=name, grid=(s // TM,),
        in_specs=[_row_spec(TM, 128), _vec_spec(128)],
        out_specs=[_row_spec(TM, 128), _row_spec(TM, 128)],
        out_shape=[jax.ShapeDtypeStruct((s, 128), F32), jax.ShapeDtypeStruct((s, 128), F32)],
        scratch_shapes=[pltpu.VMEM((1, 128), F32)],
        compiler_params=_cp("arbitrary"),
    )(fraw, b_pad)


def _gate_bwd(qaux, kaux, flog, name):
    s = flog.shape[0]
    n = s // TM

    def body(qa_ref, ka_ref, fl_ref, dp_ref, db_ref, carry_ref):
        i = pl.program_id(0)

        @pl.when(i == 0)
        def _():
            carry_ref[...] = jnp.zeros_like(carry_ref)

        src = lax.broadcasted_iota(jnp.int32, (128, 128), 0)
        dst = lax.broadcasted_iota(jnp.int32, (128, 128), 1)
        dcum = jnp.zeros((TM, 128), F32)
        for p in range(HEAD_PAIRS):
            for ref, l0, l1, sign in ((qa_ref, 64, 0, 1.0), (ka_ref, 67, 3, -1.0)):
                hit = jnp.logical_or(jnp.logical_and(src == l0, dst == 2 * p),
                                     jnp.logical_and(src == l1, dst == 2 * p + 1))
                sel = jnp.where(hit, sign, 0.0).astype(BF16)
                for piece in _split3(ref[p]):
                    dcum = dcum + _dot(piece, sel, NN)
        row = lax.broadcasted_iota(jnp.int32, (TM, TM), 0)
        col = lax.broadcasted_iota(jnp.int32, (TM, TM), 1)
        tri = (row <= col).astype(BF16)
        dlf = _tri_dot(tri, dcum) + carry_ref[...]
        carry_ref[...] = dlf[0:1, :]
        lane = lax.broadcasted_iota(jnp.int32, (1, 128), 1)
        df = jnp.where(lane < FOX_HEADS, dlf / (1.0 + jnp.exp(fl_ref[...])), 0.0)
        dp_ref[...] = df.astype(BF16)
        db = jnp.sum(df, axis=0, keepdims=True)

        @pl.when(i == 0)
        def _():
            db_ref[...] = db

        @pl.when(i > 0)
        def _():
            db_ref[...] += db

    rev = lambda i: (n - 1 - i, 0)
    return pl.pallas_call(
        body, name=name, grid=(n,),
        in_specs=[pl.BlockSpec((HEAD_PAIRS, TM, 128), lambda i: (0, n - 1 - i, 0)),
                  pl.BlockSpec((HEAD_PAIRS, TM, 128), lambda i: (0, n - 1 - i, 0)), pl.BlockSpec((TM, 128), rev)],
        out_specs=[pl.BlockSpec((TM, 128), rev), _vec_spec(128)],
        out_shape=[jax.ShapeDtypeStruct((s, 128), BF16), jax.ShapeDtypeStruct((1, 128), F32)],
        scratch_shapes=[pltpu.VMEM((1, 128), F32)],
        compiler_params=_cp("arbitrary"),
    )(qaux, kaux, flog)


def _pool_consts(i, rows):
    lane = lax.broadcasted_iota(jnp.int32, (rows, D_POOL), 1)
    t1 = lax.broadcasted_iota(jnp.int32, (rows, D_POOL), 0) + i * TM + 1
    win = jnp.where(lane < 64, 2, jnp.where(lane < 128, 4, jnp.where(lane < 192, 8, 16)))
    inv = 1.0 / jnp.minimum(t1, win).astype(F32)
    return lane, inv


def _by_group(lane, s2, s4, s8, s16):
    return jnp.where(lane < 64, s2, jnp.where(lane < 128, s4, jnp.where(lane < 192, s8, s16)))


def _pool_diff(i, u_ref, halo_ref):
    u = u_ref[...].astype(F32)
    halo = jnp.where(i > 0, halo_ref[...].astype(F32), 0.0)
    ext = jnp.concatenate([halo, u], axis=0)
    s2 = ext + pltpu.roll(ext, 1, 0)
    s4 = s2 + pltpu.roll(s2, 2, 0)
    s8 = s4 + pltpu.roll(s4, 4, 0)
    s16 = s8 + pltpu.roll(s8, 8, 0)
    lane, inv = _pool_consts(i, TM)
    sel = _by_group(lane, s2[POOL_HALO:], s4[POOL_HALO:], s8[POOL_HALO:], s16[POOL_HALO:])
    return sel * inv - u


def _pool_fwd(proj, wbd, scale, ycat, name):
    s = proj.shape[0]
    hb = TM // POOL_HALO

    def body(u_ref, halo_ref, w_ref, sc_ref, y_any, y_ref):
        del y_any
        i = pl.program_id(0)
        diff = _pool_diff(i, u_ref, halo_ref)
        mixed = _dot(diff.astype(BF16), w_ref[...], NN)
        y_ref[...] = (mixed * sc_ref[...]).astype(BF16)

    return pl.pallas_call(
        body, name=name, grid=(s // TM,),
        in_specs=[pl.BlockSpec((TM, D_POOL), lambda i: (i, 0)),
                  pl.BlockSpec((POOL_HALO, D_POOL), lambda i: (jnp.maximum(i * hb - 1, 0), 0)),
                  pl.BlockSpec((D_POOL, D_POOL), lambda i: (0, 0)), _vec_spec(D_POOL),
                  pl.BlockSpec(memory_space=pl.ANY)],
        out_specs=pl.BlockSpec((TM, D_POOL), lambda i: (i, 0)),
        out_shape=jax.ShapeDtypeStruct(ycat.shape, ycat.dtype),
        input_output_aliases={4: 0},
        compiler_params=_cp("parallel"),
    )(proj, proj, wbd, scale, ycat)


def _pool_bwd(proj, dycat, wbd, scale, name):
    s = proj.shape[0]
    n = s // TM
    hb = TM // POOL_HALO
    last_halo = s // POOL_HALO - 1

    def body(u_ref, halo_ref, dy_ref, dyp_ref, w_ref, sc_ref, dp_ref, dw_ref, dsc_ref):
        i = pl.program_id(0)
        diff = _pool_diff(i, u_ref, halo_ref)
        diff_b = diff.astype(BF16)
        mixed = _dot(diff_b, w_ref[...], NN)
        dy = dy_ref[...].astype(F32)
        dmix = (dy * sc_ref[...]).astype(BF16)
        dyp = jnp.where(i < n - 1, dyp_ref[...].astype(F32), 0.0)
        dmix_p = (dyp * sc_ref[...]).astype(BF16)
        dd = _dot(dmix, w_ref[...], NT)
        dd_p = _dot(dmix_p, w_ref[...], NT)
        lane, inv = _pool_consts(i, TM)
        _, inv_p = _pool_consts(i + 1, POOL_HALO)
        ext = jnp.concatenate([dd * inv, dd_p * inv_p], axis=0)
        rows = TM + POOL_HALO
        l2 = ext + pltpu.roll(ext, rows - 1, 0)
        l4 = l2 + pltpu.roll(l2, rows - 2, 0)
        l8 = l4 + pltpu.roll(l4, rows - 4, 0)
        l16 = l8 + pltpu.roll(l8, rows - 8, 0)
        du = _by_group(lane, l2[:TM], l4[:TM], l8[:TM], l16[:TM]) - dd
        dp_ref[...] = du.astype(BF16)
        dw = _dot(diff_b, dmix, TN)
        dsc = jnp.sum(dy * mixed, axis=0, keepdims=True)

        @pl.when(i == 0)
        def _():
            dw_ref[...] = dw
            dsc_ref[...] = dsc

        @pl.when(i > 0)
        def _():
            dw_ref[...] += dw
            dsc_ref[...] += dsc

    return pl.pallas_call(
        body, name=name, grid=(n,),
        in_specs=[pl.BlockSpec((TM, D_POOL), lambda i: (i, 0)),
                  pl.BlockSpec((POOL_HALO, D_POOL), lambda i: (jnp.maximum(i * hb - 1, 0), 0)),
                  pl.BlockSpec((TM, D_POOL), lambda i: (i, 0)),
                  pl.BlockSpec((POOL_HALO, D_POOL), lambda i: (jnp.minimum((i + 1) * hb, last_halo), 0)),
                  pl.BlockSpec((D_POOL, D_POOL), lambda i: (0, 0)), _vec_spec(D_POOL)],
        out_specs=[pl.BlockSpec((TM, D_POOL), lambda i: (i, 0)),
                   pl.BlockSpec((D_POOL, D_POOL), lambda i: (0, 0)), _vec_spec(D_POOL)],
        out_shape=[jax.ShapeDtypeStruct((s, D_POOL), BF16),
                   jax.ShapeDtypeStruct((D_POOL, D_POOL), F32), jax.ShapeDtypeStruct((1, D_POOL), F32)],
        compiler_params=_cp("arbitrary"),
    )(proj, proj, dycat, dycat, wbd, scale)


Q_BLK = D_POOL // 128
K_BLK = Q_BLK + D_FOX // 128
V_BLK = K_BLK + D_FOX // 128


def _operand_rows(v0, v1, ones_off):
    row = lax.broadcasted_iota(jnp.int32, (128, 1), 0)
    half = row & 63
    out = jnp.where(jnp.logical_and(half >= ones_off, half < ones_off + 3), 1.0, 0.0) + jnp.zeros_like(v0)
    for base, v in ((64, v0), (0, v1)):
        for j, piece in enumerate(_split3(v)):
            out = jnp.where(row == base + j, piece.astype(F32), out)
    return out


def _fox_operands(cum, name):
    s = cum.shape[0]
    width = HEAD_PAIRS * 128

    def body(c_ref, aq_ref, ak_ref):
        pieces = _split3(c_ref[...])
        row = lax.broadcasted_iota(jnp.int32, (128, width), 0)
        col = lax.broadcasted_iota(jnp.int32, (128, width), 1)
        base = (row >> 1) * 128 + (1 - (row & 1)) * 64
        half = lax.broadcasted_iota(jnp.int32, (1, width), 1) & 63
        for o_ref, off, sign, ones_off in ((aq_ref, 0, 1.0, 3), (ak_ref, 3, -1.0, 0)):
            out = jnp.where(jnp.logical_and(half >= ones_off, half < ones_off + 3), 1.0, 0.0)
            for j, piece in enumerate(pieces):
                sel = jnp.where(jnp.logical_and(col == base + off + j, row < FOX_HEADS), sign, 0.0).astype(BF16)
                out = out + _dot(piece, sel, NN)
            o_ref[...] = out.astype(BF16)

    return pl.pallas_call(
        body, name=name, grid=(s // TM,), in_specs=[_row_spec(TM, 128)],
        out_specs=[_row_spec(TM, width), _row_spec(TM, width)],
        out_shape=[jax.ShapeDtypeStruct((s, width), BF16)] * 2,
        compiler_params=_cp("parallel"),
    )(cum)


def _fox_do_operand(dycat, ycat, after, name):
    s = dycat.shape[0]

    rb = 1024

    def body(do_ref, o_ref, after_ref, ad_ref):
        del after_ref
        src = lax.broadcasted_iota(jnp.int32, (128, 128), 0)
        dst = lax.broadcasted_iota(jnp.int32, (128, 128), 1)
        hit = jnp.logical_or(jnp.logical_and(src < 64, jnp.logical_and(dst >= 64, dst < 67)),
                             jnp.logical_and(src >= 64, dst < 3))
        sel = jnp.where(hit, 1.0, 0.0).astype(BF16)
        dd = do_ref[...].astype(F32) * o_ref[...].astype(F32)
        dsum = jnp.zeros(dd.shape, F32)
        for piece in _split3(dd):
            dsum = dsum + _dot(piece, sel, NN)
        hi, mid, lo = _split3(-dsum)
        j = lax.broadcasted_iota(jnp.int32, (1, 128), 1) & 63
        ad_ref[...] = jnp.where(j == 0, hi, jnp.where(j == 1, mid, lo))

    blk = pl.BlockSpec((rb, 128), lambda i, p: (i, Q_BLK + p))
    return pl.pallas_call(
        body, name=name, grid=(s // rb, HEAD_PAIRS), in_specs=[blk, blk, pl.BlockSpec(memory_space=pl.ANY)],
        out_specs=pl.BlockSpec((rb, 128), lambda i, p: (i, p)),
        out_shape=jax.ShapeDtypeStruct((s, HEAD_PAIRS * 128), BF16),
        compiler_params=_cp("parallel", "parallel"),
    )(dycat, ycat, after)


def _causal_pairs(nq, key_major):
    if key_major:
        pairs = [(q, k) for k in range(nq) for q in range(k, nq)]
    else:
        pairs = [(q, k) for q in range(nq) for k in range(q + 1)]
    return (jnp.asarray([p[0] for p in pairs], jnp.int32), jnp.asarray([p[1] for p in pairs], jnp.int32))


def _fox_fwd(proj, aq, ak, name):
    s = proj.shape[0]
    nq = s // TQ
    qi_arr, ki_arr = _causal_pairs(nq, key_major=False)

    def body(qi_ref, ki_ref, q_ref, k_ref, v_ref, aq_ref, ak_ref, o_ref, aqb_ref, m0_ref, m1_ref, acc_ref, aux_ref):
        t = pl.program_id(1)
        qi, ki = qi_ref[t], ki_ref[t]
        lane = lax.broadcasted_iota(jnp.int32, (1, 128), 1)
        masks = [lane < 64, lane >= 64]
        ones_v = jnp.where((lane & 63) == 8, 1.0, 0.0).astype(BF16)
        top = lax.broadcasted_iota(jnp.int32, (128, 1), 0) < 64
        m_ref = [m0_ref, m1_ref]

        @pl.when(ki == 0)
        def _():
            m0_ref[...] = jnp.full_like(m0_ref, NEG)
            m1_ref[...] = jnp.full_like(m1_ref, NEG)
            acc_ref[...] = jnp.zeros_like(acc_ref)
            aux_ref[...] = jnp.zeros_like(aux_ref)

        def step(diag):
            q2s = q_ref[...] * 0.125
            k2, v2, aq2, ak2 = k_ref[...], v_ref[...], aq_ref[...], ak_ref[...]
            pv, alpha = [], []
            for hh in range(2):
                qh = jnp.where(masks[hh], q2s, aq2)
                kh = jnp.where(masks[hh], k2, ak2)
                vh = jnp.where(masks[hh], v2, ones_v)
                sc = _dot(kh, qh, NT)
                if diag:
                    key = lax.broadcasted_iota(jnp.int32, sc.shape, 0)
                    qry = lax.broadcasted_iota(jnp.int32, sc.shape, 1)
                    sc = jnp.where(qry >= key, sc, NEG)
                m_prev = m_ref[hh][...]
                m_new = jnp.maximum(m_prev, jnp.max(sc, axis=0, keepdims=True))
                m_ref[hh][...] = m_new
                alpha.append(jnp.exp(m_prev - m_new))
                pv.append(_dot(vh, jnp.exp(sc - m_new).astype(BF16), TN))
            acc_ref[...] = acc_ref[...] * jnp.where(top, alpha[0], alpha[1]) + jnp.where(top, pv[0], pv[1])
            aux_ref[...] = aux_ref[...] * jnp.where(top, alpha[1], alpha[0]) + jnp.where(top, pv[1], pv[0])

        @pl.when(ki < qi)
        def _():
            step(False)

        @pl.when(ki == qi)
        def _():
            step(True)
            aux = aux_ref[...]
            l0, l1 = aux[72:73, :], aux[8:9, :]
            o_ref[...] = (acc_ref[...] * jnp.where(top, 1.0 / l0, 1.0 / l1)).T.astype(BF16)
            aqt = aq_ref[...].astype(F32).T
            cum0 = aqt[64:65, :] + aqt[65:66, :] + aqt[66:67, :]
            cum1 = aqt[0:1, :] + aqt[1:2, :] + aqt[2:3, :]
            aqb = _operand_rows(cum0 - (m0_ref[...] + jnp.log(l0)), cum1 - (m1_ref[...] + jnp.log(l1)), 3)
            aqb_ref[...] = aqb.T.astype(BF16)

    grid_spec = pltpu.PrefetchScalarGridSpec(
        num_scalar_prefetch=2, grid=(HEAD_PAIRS, int(qi_arr.shape[0])),
        in_specs=[pl.BlockSpec((TQ, 128), lambda p, t, qi, ki: (qi[t], Q_BLK + p)),
                  pl.BlockSpec((TQ, 128), lambda p, t, qi, ki: (ki[t], K_BLK + p)),
                  pl.BlockSpec((TQ, 128), lambda p, t, qi, ki: (ki[t], V_BLK + p)),
                  pl.BlockSpec((TQ, 128), lambda p, t, qi, ki: (qi[t], p)),
                  pl.BlockSpec((TQ, 128), lambda p, t, qi, ki: (ki[t], p))],
        out_specs=[pl.BlockSpec((TQ, 128), lambda p, t, qi, ki: (qi[t], Q_BLK + p)),
                   pl.BlockSpec((TQ, 128), lambda p, t, qi, ki: (qi[t], p))],
        scratch_shapes=[pltpu.VMEM((1, TQ), F32), pltpu.VMEM((1, TQ), F32),
                        pltpu.VMEM((128, TQ), F32), pltpu.VMEM((128, TQ), F32)])
    return pl.pallas_call(
        body, name=name, grid_spec=grid_spec,
        out_shape=[jax.ShapeDtypeStruct((s, D_MODEL), BF16), jax.ShapeDtypeStruct((s, HEAD_PAIRS * 128), BF16)],
        compiler_params=_cp("parallel", "arbitrary"),
    )(qi_arr, ki_arr, proj, proj, proj, aq, ak)


def _fox_bwd(proj, dycat, aqb, ak, ad, name):
    s = proj.shape[0]
    nq = s // TQ
    qi_arr, ki_arr = _causal_pairs(nq, key_major=True)

    def body(qi_ref, ki_ref, q_ref, k_ref, v_ref, do_ref, aq_ref, ak_ref, ad_ref,
             dq_ref, dk_ref, dv_ref, qaux_ref, kaux_ref, dq_acc, qaux_acc, dk_acc, dv_acc, kaux_acc):
        t = pl.program_id(1)
        qi, ki = qi_ref[t], ki_ref[t]
        lane = lax.broadcasted_iota(jnp.int32, (1, 128), 1)
        masks = [lane < 64, lane >= 64]
        ones_v = jnp.where((lane & 63) < 3, 1.0, 0.0).astype(BF16)
        top = lax.broadcasted_iota(jnp.int32, (128, 1), 0) < 64

        @pl.when(qi == ki)
        def _():
            dk_acc[...] = jnp.zeros_like(dk_acc)
            dv_acc[...] = jnp.zeros_like(dv_acc)
            kaux_acc[...] = jnp.zeros_like(kaux_acc)

        def step(diag):
            q2s = q_ref[...] * 0.125
            k2, v2, do2 = k_ref[...], v_ref[...], do_ref[...]
            aq2, ak2, ad2 = aq_ref[...], ak_ref[...], ad_ref[...]
            dq, dk, dv = [], [], []
            for hh in range(2):
                qh = jnp.where(masks[hh], q2s, aq2)
                kh = jnp.where(masks[hh], k2, ak2)
                doh = jnp.where(masks[hh], do2, ad2)
                vh = jnp.where(masks[hh], v2, ones_v)
                sc = _dot(kh, qh, NT)
                if diag:
                    key = lax.broadcasted_iota(jnp.int32, sc.shape, 0)
                    qry = lax.broadcasted_iota(jnp.int32, sc.shape, 1)
                    sc = jnp.where(qry >= key, sc, NEG)
                p = jnp.exp(sc)
                dsb = (p * _dot(vh, doh, NT)).astype(BF16)
                dv.append(_dot(p.astype(BF16), doh, NN))
                dk.append(_dot(dsb, qh, NN))
                dq.append(_dot(kh, dsb, TN))
            dk_acc[...] += jnp.where(masks[0], dk[0], dk[1])
            kaux_acc[...] += jnp.where(masks[0], dk[1], dk[0])
            dv_acc[...] += jnp.where(masks[0], dv[0], dv[1])
            dq_new = jnp.where(top, dq[0], dq[1])
            qaux_new = jnp.where(top, dq[1], dq[0])

            @pl.when(ki == 0)
            def _():
                dq_acc[qi] = dq_new
                qaux_acc[qi] = qaux_new

            @pl.when(ki > 0)
            def _():
                dq_acc[qi] += dq_new
                qaux_acc[qi] += qaux_new

        @pl.when(qi > ki)
        def _():
            step(False)

        @pl.when(qi == ki)
        def _():
            step(True)
            rows = pl.ds(pl.multiple_of(qi * TQ, TQ), TQ)
            dq_ref[rows, :] = (dq_acc[qi] * 0.125).T.astype(BF16)
            qaux_ref[rows, :] = qaux_acc[qi].T

        @pl.when(qi == nq - 1)
        def _():
            dk_ref[...] = dk_acc[...].astype(BF16)
            dv_ref[...] = dv_acc[...].astype(BF16)
            kaux_ref[...] = kaux_acc[...]

    grid_spec = pltpu.PrefetchScalarGridSpec(
        num_scalar_prefetch=2, grid=(HEAD_PAIRS, int(qi_arr.shape[0])),
        in_specs=[pl.BlockSpec((TQ, 128), lambda p, t, qi, ki: (qi[t], Q_BLK + p)),
                  pl.BlockSpec((TQ, 128), lambda p, t, qi, ki: (ki[t], K_BLK + p)),
                  pl.BlockSpec((TQ, 128), lambda p, t, qi, ki: (ki[t], V_BLK + p)),
                  pl.BlockSpec((TQ, 128), lambda p, t, qi, ki: (qi[t], Q_BLK + p)),
                  pl.BlockSpec((TQ, 128), lambda p, t, qi, ki: (qi[t], p)),
                  pl.BlockSpec((TQ, 128), lambda p, t, qi, ki: (ki[t], p)),
                  pl.BlockSpec((TQ, 128), lambda p, t, qi, ki: (qi[t], p))],
        out_specs=[pl.BlockSpec((s, 128), lambda p, t, qi, ki: (0, p)),
                   pl.BlockSpec((TQ, 128), lambda p, t, qi, ki: (ki[t], p)),
                   pl.BlockSpec((TQ, 128), lambda p, t, qi, ki: (ki[t], p)),
                   pl.BlockSpec((None, s, 128), lambda p, t, qi, ki: (p, 0, 0)),
                   pl.BlockSpec((None, TQ, 128), lambda p, t, qi, ki: (p, ki[t], 0))],
        scratch_shapes=[pltpu.VMEM((nq, 128, TQ), F32), pltpu.VMEM((nq, 128, TQ), F32),
                        pltpu.VMEM((TQ, 128), F32), pltpu.VMEM((TQ, 128), F32), pltpu.VMEM((TQ, 128), F32)])
    return pl.pallas_call(
        body, name=name, grid_spec=grid_spec,
        out_shape=[jax.ShapeDtypeStruct((s, D_FOX), BF16)] * 3 + [jax.ShapeDtypeStruct((HEAD_PAIRS, s, 128), F32)] * 2,
        compiler_params=_cp("arbitrary", "arbitrary"),
    )(qi_arr, ki_arr, proj, proj, proj, dycat, aqb, ak, ad)


XA_SCALE = XA_DIM ** -0.5


def _xattn_fwd(q2, kv, name):
    s = q2.shape[0]
    m = kv.shape[0]

    def body(q_ref, kv_ref, o_ref):
        for h in range(XA_HEADS):
            c0 = h * XA_DIM
            sc = _dot(kv_ref[:, c0:c0 + XA_DIM], q_ref[:, c0:c0 + XA_DIM], NT) * XA_SCALE
            e = jnp.exp(sc - jnp.max(sc, axis=0, keepdims=True))
            p = e / jnp.sum(e, axis=0, keepdims=True)
            ot = _dot(kv_ref[:, D_MODEL + c0:D_MODEL + c0 + XA_DIM], p.astype(BF16), TN)
            o_ref[:, c0:c0 + XA_DIM] = ot.T.astype(BF16)

    return pl.pallas_call(
        body, name=name, grid=(s // TM,),
        in_specs=[_row_spec(TM, D_MODEL), pl.BlockSpec((m, 2 * D_MODEL), lambda i: (0, 0))],
        out_specs=_row_spec(TM, D_MODEL), out_shape=jax.ShapeDtypeStruct((s, D_MODEL), BF16),
        compiler_params=_cp("parallel"),
    )(q2, kv)


def _xattn_bwd(q2, kv, do2, name):
    s = q2.shape[0]
    m = kv.shape[0]

    def body(q_ref, kv_ref, do_ref, dq_ref, dkv_ref):
        i = pl.program_id(0)

        @pl.when(i == 0)
        def _():
            dkv_ref[...] = jnp.zeros_like(dkv_ref)

        for h in range(XA_HEADS):
            c0 = h * XA_DIM
            v0 = D_MODEL + c0
            qh = q_ref[:, c0:c0 + XA_DIM]
            kh = kv_ref[:, c0:c0 + XA_DIM]
            doh = do_ref[:, c0:c0 + XA_DIM]
            sc = _dot(kh, qh, NT) * XA_SCALE
            e = jnp.exp(sc - jnp.max(sc, axis=0, keepdims=True))
            p = e / jnp.sum(e, axis=0, keepdims=True)
            dp = _dot(kv_ref[:, v0:v0 + XA_DIM], doh, NT)
            ds = p * (dp - jnp.sum(p * dp, axis=0, keepdims=True))
            dsb = (ds * XA_SCALE).astype(BF16)
            dq_ref[:, c0:c0 + XA_DIM] = _dot(kh, dsb, TN).T.astype(BF16)
            dkv_ref[:, c0:c0 + XA_DIM] += _dot(dsb, qh, NN)
            dkv_ref[:, v0:v0 + XA_DIM] += _dot(p.astype(BF16), doh, NN)

    return pl.pallas_call(
        body, name=name, grid=(s // TM,),
        in_specs=[_row_spec(TM, D_MODEL), pl.BlockSpec((m, 2 * D_MODEL), lambda i: (0, 0)), _row_spec(TM, D_MODEL)],
        out_specs=[_row_spec(TM, D_MODEL), pl.BlockSpec((m, 2 * D_MODEL), lambda i: (0, 0))],
        out_shape=[jax.ShapeDtypeStruct((s, D_MODEL), BF16), jax.ShapeDtypeStruct((m, 2 * D_MODEL), F32)],
        compiler_params=_cp("arbitrary"),
    )(q2, kv, do2)


GELU_C = math.sqrt(2.0 / math.pi)
GELU_A = 0.044715


def _gelu(x):
    return 0.5 * x * (1.0 + jnp.tanh(GELU_C * (x + GELU_A * x * x * x)))


def _gelu_and_grad(x):
    t = jnp.tanh(GELU_C * (x + GELU_A * x * x * x))
    g = 0.5 * x * (1.0 + t)
    dg = 0.5 * (1.0 + t) + 0.5 * x * (1.0 - t * t) * GELU_C * (1.0 + 3.0 * GELU_A * x * x)
    return g, dg


def _conv(h, s1, s2, w_ref, b_ref):
    return w_ref[0:1, :] * s2 + w_ref[1:2, :] * s1 + w_ref[2:3, :] * h + b_ref[...]


def _shift_down(main, prev8):
    row = lax.broadcasted_iota(jnp.int32, main.shape, 0)
    s1 = jnp.where(row == 0, prev8[7:8, :], pltpu.roll(main, 1, 0))
    s2 = jnp.where(row == 0, prev8[6:7, :], jnp.where(row == 1, prev8[7:8, :], pltpu.roll(main, 2, 0)))
    return s1, s2


def _shift_up(main, next8):
    n = main.shape[0]
    row = lax.broadcasted_iota(jnp.int32, main.shape, 0)
    u1 = jnp.where(row == n - 1, next8[0:1, :], pltpu.roll(main, n - 1, 0))
    u2 = jnp.where(row == n - 2, next8[0:1, :], jnp.where(row == n - 1, next8[1:2, :], pltpu.roll(main, n - 2, 0)))
    return u1, u2


def _ffn_fwd(h3, w_up, cw, cb, w_down, x2, tgt, g_post, name):
    s = h3.shape[0]
    tn = TN_FF
    nj = D_FF // tn
    per = D_MODEL // tn
    hb = TM // 8

    def body(h_ref, halo_ref, wg_ref, wu_ref, cwg_ref, cwu_ref, cbg_ref, cbu_ref, wd_ref, x_ref, t_ref, g_ref,
             hg_ref, hu_ref, cg_ref, cu_ref, a_ref, loss_ref, dx_ref, dy_ref, dg_ref, y_acc):
        i, j = pl.program_id(0), pl.program_id(1)
        h = h_ref[...]
        halo = halo_ref[...]
        halo = jnp.where(i > 0, halo, jnp.zeros_like(halo))
        conv = []
        for w_ref, cw_ref, cb_ref, hid_ref, c_ref in ((wg_ref, cwg_ref, cbg_ref, hg_ref, cg_ref),
                                                      (wu_ref, cwu_ref, cbu_ref, hu_ref, cu_ref)):
            hm = _dot(h, w_ref[...], NN)
            hid_ref[...] = hm.astype(BF16)
            s1, s2 = _shift_down(hm, _dot(halo, w_ref[...], NN))
            c = _conv(hm, s1, s2, cw_ref, cb_ref)
            c_ref[...] = c.astype(BF16)
            conv.append(c)
        a = (_gelu(conv[0]) * conv[1]).astype(BF16)
        a_ref[...] = a
        contrib = _dot(a, wd_ref[...], NN)

        @pl.when(j == 0)
        def _():
            y_acc[...] = contrib

        @pl.when(j > 0)
        def _():
            y_acc[...] += contrib

        @pl.when(j == nj - 1)
        def _():
            yv = y_acc[...]
            r = _rstd(yv)
            yn = yv * r
            e = x_ref[...] + yn * g_ref[...] - t_ref[...]
            part = 0.5 * jnp.sum(jnp.mean(e * e, axis=-1, keepdims=True), axis=0, keepdims=True)
            part = jnp.broadcast_to(part, (1, 128))
            dx = e * (1.0 / D_MODEL)
            dx_ref[...] = dx
            dy_ref[...] = _norm_bwd_rows(dx * g_ref[...], yn, r).astype(BF16)
            dg = jnp.sum(dx * yn, axis=0, keepdims=True)

            @pl.when(i == 0)
            def _():
                dg_ref[...] = dg
                loss_ref[...] = part

            @pl.when(i > 0)
            def _():
                dg_ref[...] += dg
                loss_ref[...] += part

    rows = pl.BlockSpec((TM, D_MODEL), lambda i, j: (i, 0))
    tile = pl.BlockSpec((TM, tn), lambda i, j: (i, j))
    wide = jax.ShapeDtypeStruct((s, D_FF), BF16)
    return pl.pallas_call(
        body, name=name, grid=(s // TM, nj),
        in_specs=[rows,
                  pl.BlockSpec((8, D_MODEL), lambda i, j: (jnp.maximum(i * hb - 1, 0), 0)),
                  pl.BlockSpec((None, D_MODEL, tn), lambda i, j: (j // per, 0, j % per)),
                  pl.BlockSpec((None, D_MODEL, tn), lambda i, j: (NDEV // 2 + j // per, 0, j % per)),
                  pl.BlockSpec((8, tn), lambda i, j: (0, j)),
                  pl.BlockSpec((8, tn), lambda i, j: (0, nj + j)),
                  pl.BlockSpec((1, tn), lambda i, j: (0, j)),
                  pl.BlockSpec((1, tn), lambda i, j: (0, nj + j)),
                  pl.BlockSpec((tn, D_MODEL), lambda i, j: (j, 0)),
                  rows, rows, pl.BlockSpec((1, D_MODEL), lambda i, j: (0, 0))],
        out_specs=[tile, tile, tile, tile, tile,
                   pl.BlockSpec((1, 128), lambda i, j: (0, 0)), rows, rows,
                   pl.BlockSpec((1, D_MODEL), lambda i, j: (0, 0))],
        out_shape=[wide, wide, wide, wide, wide,
                   jax.ShapeDtypeStruct((1, 128), F32), jax.ShapeDtypeStruct((s, D_MODEL), F32),
                   jax.ShapeDtypeStruct((s, D_MODEL), BF16), jax.ShapeDtypeStruct((1, D_MODEL), F32)],
        scratch_shapes=[pltpu.VMEM((TM, D_MODEL), F32)],
        compiler_params=_cp("arbitrary", "arbitrary"),
    )(h3, h3, w_up, w_up, cw, cw, cb, cb, w_down, x2, tgt, g_post)


def _ffn_bwd(dy3, w_down, hid_g, hid_u, conv_g, conv_u, cw, name):
    s = dy3.shape[0]
    n = s // TM
    tn = TN_FF
    nj = D_FF // tn
    hb = TM // 8
    last8 = s // 8 - 1

    def body(dy_ref, dyn_ref, wd_ref, hg_ref, hu_ref, cg_ref, cgn_ref, cu_ref, cun_ref, cwg_ref, cwu_ref,
             dhg_ref, dhu_ref, dcwg_ref, dcwu_ref, dcbg_ref, dcbu_ref):
        i = pl.program_id(1)
        first, last = i == 0, i == n - 1
        da = _dot(dy_ref[...], wd_ref[...], NT)
        dyn = dyn_ref[...]
        dyn = jnp.where(last, jnp.zeros_like(dyn), dyn)
        da_n = _dot(dyn, wd_ref[...], NT)
        c_g, c_u = cg_ref[...].astype(F32), cu_ref[...].astype(F32)
        g, dg = _gelu_and_grad(c_g)
        gn, dgn = _gelu_and_grad(cgn_ref[...].astype(F32))
        outs = ((da * c_u * dg, da_n * cun_ref[...].astype(F32) * dgn, hg_ref, cwg_ref, dhg_ref, dcwg_ref, dcbg_ref),
                (da * g, da_n * gn, hu_ref, cwu_ref, dhu_ref, dcwu_ref, dcbu_ref))
        row8 = lax.broadcasted_iota(jnp.int32, (8, tn), 0)
        for dc, dcn, h_ref, cw_ref, dh_ref, dcw_ref, dcb_ref in outs:
            u1, u2 = _shift_up(dc, dcn)
            dh_ref[...] = (cw_ref[2:3, :] * dc + cw_ref[1:2, :] * u1 + cw_ref[0:1, :] * u2).astype(BF16)
            hm = h_ref[...].astype(F32)
            dcb = jnp.sum(dc, axis=0, keepdims=True)
            dcw = jnp.where(row8 == 0, jnp.sum(hm * u2, axis=0, keepdims=True),
                            jnp.where(row8 == 1, jnp.sum(hm * u1, axis=0, keepdims=True),
                                      jnp.where(row8 == 2, jnp.sum(hm * dc, axis=0, keepdims=True), 0.0)))

            @pl.when(first)
            def _():
                dcw_ref[...] = dcw
                dcb_ref[...] = dcb

            @pl.when(i > 0)
            def _():
                dcw_ref[...] += dcw
                dcb_ref[...] += dcb

    next8 = lambda j, i: (jnp.minimum((i + 1) * hb, last8), j)
    blk = lambda j, i: (i, j)
    col = lambda j, i: (0, j)
    colu = lambda j, i: (0, nj + j)
    tile = pl.BlockSpec((TM, tn), blk)
    return pl.pallas_call(
        body, name=name, grid=(nj, n),
        in_specs=[pl.BlockSpec((TM, D_MODEL), lambda j, i: (i, 0)),
                  pl.BlockSpec((8, D_MODEL), lambda j, i: (jnp.minimum((i + 1) * hb, last8), 0)),
                  pl.BlockSpec((tn, D_MODEL), lambda j, i: (j, 0)),
                  tile, tile, tile, pl.BlockSpec((8, tn), next8), tile, pl.BlockSpec((8, tn), next8),
                  pl.BlockSpec((8, tn), col), pl.BlockSpec((8, tn), colu)],
        out_specs=[tile, tile, pl.BlockSpec((8, tn), col), pl.BlockSpec((8, tn), col),
                   pl.BlockSpec((1, tn), col), pl.BlockSpec((1, tn), col)],
        out_shape=[jax.ShapeDtypeStruct((s, D_FF), BF16), jax.ShapeDtypeStruct((s, D_FF), BF16),
                   jax.ShapeDtypeStruct((8, D_FF), F32), jax.ShapeDtypeStruct((8, D_FF), F32),
                   jax.ShapeDtypeStruct((1, D_FF), F32), jax.ShapeDtypeStruct((1, D_FF), F32)],
        compiler_params=_cp("parallel", "arbitrary"),
    )(dy3, dy3, w_down, hid_g, hid_u, conv_g, conv_g, conv_u, conv_u, cw, cw)


def _slot(p):
    return 4 * p[0] + 2 * p[1] + p[2]


def _all_gather(shards, name):
    n = len(shards)

    def body(*refs):
        ins, outs = refs[:n], refs[n:2 * n]
        send_sems, recv_sems, local_sems = refs[2 * n:]
        x, y, c = lax.axis_index("x"), lax.axis_index("y"), lax.axis_index("c")
        me, sibling = (x, y, c), (x, y, 1 - c)
        chips = [(1 - x, y), (x, 1 - y), (1 - x, 1 - y)]

        def copy(a, k, block, to, from_input=False):
            dst = outs[a].at[_slot(block)]
            return pltpu.make_async_remote_copy(
                src_ref=ins[a] if from_input else dst, dst_ref=dst,
                send_sem=send_sems.at[a, k], recv_sem=recv_sems.at[a, k],
                device_id=to, device_id_type=MESH)

        mine = [pltpu.make_async_copy(ins[a], outs[a].at[_slot(me)], local_sems.at[a]) for a in range(n)]
        for cp in mine:
            cp.start()
        first = []
        for a in range(n):
            first.append(copy(a, 0, me, sibling, True))
            first += [copy(a, 1 + j, me, (*chip, c), True) for j, chip in enumerate(chips)]
        for cp in first:
            cp.start()
        passed = []
        for j, chip in enumerate(chips):
            for a in range(n):
                copy(a, 1 + j, (*chip, c), me).wait_recv()
                fwd = copy(a, 4 + j, (*chip, c), sibling)
                fwd.start()
                passed.append(fwd)
        for a in range(n):
            copy(a, 0, sibling, me).wait_recv()
            for j, chip in enumerate(chips):
                copy(a, 4 + j, (*chip, 1 - c), me).wait_recv()
        for cp in first + passed:
            cp.wait_send()
        for cp in mine:
            cp.wait()

    any_spec = pl.BlockSpec(memory_space=pl.ANY)
    return pl.pallas_call(
        body, name=name,
        in_specs=[any_spec] * n, out_specs=[any_spec] * n,
        out_shape=[jax.ShapeDtypeStruct((NDEV,) + s.shape, s.dtype) for s in shards],
        scratch_shapes=[pltpu.SemaphoreType.DMA((n, 7)), pltpu.SemaphoreType.DMA((n, 7)),
                        pltpu.SemaphoreType.DMA((n,))],
    )(*shards)


def _peer_list(x, y, c):
    return [(1 - x if m & 4 else x, 1 - y if m & 2 else y, 1 - c if m & 1 else c) for m in range(1, NDEV)]


def _exchange_copies(src_refs, land_refs, send_sems, recv_sems, gather):
    x, y, c = lax.axis_index("x"), lax.axis_index("y"), lax.axis_index("c")
    me = (x, y, c)
    copies = []
    for m, peer in enumerate(_peer_list(x, y, c)):
        for a in range(len(src_refs)):
            copies.append(pltpu.make_async_remote_copy(
                src_ref=src_refs[a] if gather else src_refs[a].at[_slot(peer)], dst_ref=land_refs[a].at[_slot(me)],
                send_sem=send_sems.at[a * (NDEV - 1) + m], recv_sem=recv_sems.at[a * (NDEV - 1) + m],
                device_id=peer, device_id_type=MESH))
    return copies


def _exchange_start(srcs, lands, after, gather, name):
    n = len(srcs)
    hbm = pl.BlockSpec(memory_space=pltpu.HBM)

    def body(*refs):
        for cp in _exchange_copies(refs[:n], refs[n:2 * n], refs[2 * n + 1], refs[2 * n + 2], gather):
            cp.start()
        token = refs[-1]
        token[...] = jnp.zeros_like(token)

    outs = pl.pallas_call(
        body, name=name,
        out_shape=(pltpu.SemaphoreType.DMA((n * (NDEV - 1),)), pltpu.SemaphoreType.DMA((n * (NDEV - 1),)),
                   *[pltpu.HBM(a.shape, a.dtype) for a in list(srcs) + list(lands)],
                   jax.ShapeDtypeStruct((8, 128), F32)),
        in_specs=[hbm] * (2 * n) + [pl.BlockSpec(memory_space=pl.ANY)],
        out_specs=(pl.BlockSpec(memory_space=pltpu.SEMAPHORE), pl.BlockSpec(memory_space=pltpu.SEMAPHORE),
                   *[hbm] * (2 * n), pl.BlockSpec(memory_space=pltpu.VMEM)),
        input_output_aliases={i: 2 + i for i in range(2 * n)},
        compiler_params=pltpu.CompilerParams(has_side_effects=pltpu.SideEffectType.DATAFLOW_SIDE_EFFECTING),
    )(*[pltpu.with_memory_space_constraint(a, pltpu.HBM) for a in list(srcs) + list(lands)], after)
    return outs[0], outs[1], outs[2:2 + n], outs[2 + n:2 + 2 * n], outs[-1]


def _exchange_wait(send_sems, recv_sems, srcs, lands, after, gather, name):
    n = len(srcs)
    hbm = pl.BlockSpec(memory_space=pltpu.HBM)

    def body(*refs):
        for cp in _exchange_copies(refs[:n], refs[n:2 * n], refs[2 * n], refs[2 * n + 1], gather):
            cp.wait_send()
            cp.wait_recv()

    outs = pl.pallas_call(
        body, name=name,
        out_shape=tuple(pltpu.HBM(a.shape, a.dtype) for a in list(srcs) + list(lands)),
        in_specs=[hbm] * (2 * n) + [pl.BlockSpec(memory_space=pltpu.SEMAPHORE)] * 2 + [pl.BlockSpec(memory_space=pl.ANY)],
        out_specs=tuple([hbm] * (2 * n)),
        input_output_aliases={i: i for i in range(2 * n)},
        compiler_params=pltpu.CompilerParams(has_side_effects=pltpu.SideEffectType.DATAFLOW_SIDE_EFFECTING),
    )(*srcs, *lands, send_sems, recv_sems, after)
    return outs[n:]


def _own_slot(block):
    me = 4 * lax.axis_index("x") + 2 * lax.axis_index("y") + lax.axis_index("c")
    return lax.dynamic_update_slice(lax.empty((NDEV,) + block.shape, block.dtype), block[None], (me, 0, 0))


def _adam_update(p_ref, w_ref, m_ref, v_ref, g_ref, d_ref, mo_ref, vo_ref):
    bc1 = 1.0 - ADAM_B1 ** ADAM_STEP
    bc2 = 1.0 - ADAM_B2 ** ADAM_STEP
    g = p_ref[0].astype(F32)
    for d in range(1, NDEV):
        g = g + p_ref[d].astype(F32)
    g_ref[...] = g
    mn = ADAM_B1 * m_ref[...] + (1.0 - ADAM_B1) * g
    vn = ADAM_B2 * v_ref[...] + (1.0 - ADAM_B2) * (g * g)
    mo_ref[...] = mn
    vo_ref[...] = vn
    d_ref[...] = -ADAM_LR * ((mn / bc1) / (jnp.sqrt(vn / bc2) + ADAM_EPS) + ADAM_WD * w_ref[...])


def _adamw_small(parts, ws, ms, vs, name):
    n = len(ws)

    def body(*refs):
        ins, outs = refs[:4 * n], refs[4 * n:]
        for k in range(n):
            _adam_update(ins[k], ins[n + k], ins[2 * n + k], ins[3 * n + k], *outs[4 * k:4 * k + 4])

    whole = pl.BlockSpec(memory_space=pltpu.VMEM)
    res = pl.pallas_call(
        body, name=name, in_specs=[whole] * (4 * n), out_specs=[whole] * (4 * n),
        out_shape=[jax.ShapeDtypeStruct(a.shape, F32) for a in ws for _ in range(4)],
    )(*parts, *ws, *ms, *vs)
    return [res[4 * k:4 * k + 4] for k in range(n)]


def _adamw(parts, w, m, v, name):
    r, c = w.shape
    tr = r if r * c <= 160 * 1024 else max(8, (160 * 1024 // c) // 8 * 8)
    while r % tr:
        tr -= 8
    body = functools.partial(_adam_update)
    spec = pl.BlockSpec((tr, c), lambda i: (i, 0))
    return pl.pallas_call(
        body, name=name, grid=(r // tr,),
        in_specs=[pl.BlockSpec((NDEV, tr, c), lambda i: (0, i, 0)), spec, spec, spec],
        out_specs=[spec] * 4, out_shape=[jax.ShapeDtypeStruct((r, c), F32)] * 4,
        compiler_params=_cp("parallel"),
    )(parts, w, m, v)


def _local_step(x, mem, tgt, gains, b_forget, w_pool, pool_scale, conv_b, w_in,
                mix_weights, ffn_weights, send_in_grad, send_mix_grads, send_ffn_grads):
    w_f = w_in[:, F_COL:]
    b_pad = jnp.pad(b_forget, ((0, 0), (0, 128 - FOX_HEADS)))
    wbd = jnp.zeros((D_POOL, D_POOL), F32)
    for g in range(4):
        wbd = wbd.at[64 * g:64 * g + 64, 64 * g:64 * g + 64].set(w_pool[g])
    wbd = wbd.astype(BF16)
    scale = pool_scale.reshape(1, D_POOL)

    h1 = _norm_fwd(x, gains["mix_pre"], "norm_mix_pre")
    proj = _mm(h1, w_in, "nn", BF16, 512, D_IN_PAD, 1024, "proj_in")
    fraw = _mm(h1, w_f, "nn", F32, 1024, 128, 1024, "proj_gate")
    flog, cum = _gate_cumsum(fraw, b_pad, "gate_cumsum")
    aq, ak = _fox_operands(cum, "fox_operands")
    ycat, aqb = _fox_fwd(proj, aq, ak, "fox_fwd")
    ycat = _pool_fwd(proj, wbd, scale, ycat, "pool_fwd")
    w_mix, w_xq, w_xo, w_xkv = mix_weights(ycat)
    y1, x1, h2 = _mm_rows(ycat, w_mix, "nn", 1024, "mix_out", [x], [gains["mix_post"], gains["xa_pre"]],
                          [F32, F32, BF16], _epi_resid)
    q2 = _mm(h2, w_xq, "nn", BF16, 1024, 1024, 1024, "xa_q")
    mem_n = _norm_fwd(mem, gains["mem"], "norm_mem")
    kv = _mm(mem_n, w_xkv, "nn", BF16, mem.shape[0], 256, 1024, "xa_kv", b_cols=256)
    o2 = _xattn_fwd(q2, kv, "xattn_fwd")
    y2, x2, h3 = _mm_rows(o2, w_xo, "nn", 1024, "xa_out", [x1], [gains["xa_post"], gains["ffn_pre"]],
                          [F32, F32, BF16], _epi_resid)
    w_up, w_down, cw = ffn_weights(h3)
    hid_g, hid_u, conv_g, conv_u, act, loss, dx3, dy3, dg_ffn_post = _ffn_fwd(
        h3, w_up, cw, conv_b, w_down, x2, tgt, gains["ffn_post"], "ffn_fwd")

    dhid_g, dhid_u, dcw_g, dcw_u, dcb_g, dcb_u = _ffn_bwd(dy3, w_down, hid_g, hid_u, conv_g, conv_u, cw, "ffn_bwd")
    d_w_down = _mm(act, dy3, "tn", BF16, 1024, 1024, 1024, "dw_down")
    d_w_up = _mm(h3, [dhid_g, dhid_u], "tn", BF16, 1024, 1024, 1024, "dw_up", out_cols=1024)
    sent = send_ffn_grads(d_w_up, d_w_down, jnp.concatenate([dcw_g, dcw_u], axis=1))
    dh3 = _mm([dhid_g, dhid_u], w_up, "nt", F32, 1024, 1024, 1024, "dh_ffn", b_cols=1024, after=sent)
    dx2, dg_ffn_pre, dy2, dg_xa_post = _norm_bwd(dh3, x2, dx3, gains["ffn_pre"], "norm_bwd_ffn",
                                                 prev=(y2, gains["xa_post"]))
    do2 = _mm(dy2, w_xo, "nt", BF16, 1024, 1024, 1024, "d_xa_out")
    d_w_xo = _mm(o2, dy2, "tn", BF16, 1024, 1024, 1024, "dw_xo")
    dq2, dkv = _xattn_bwd(q2, kv, do2, "xattn_bwd")
    dkv = dkv.astype(BF16)
    dx1, dg_xa_pre, dy1, dg_mix_post = _mm_rows(
        dq2, w_xq, "nt", 1024, "dh_xa", [x1, dx2, y1], [gains["xa_pre"], gains["mix_post"]],
        [F32, "sum", BF16, "sum"], _epi_norm_bwd)
    d_w_xq = _mm(h2, dq2, "tn", BF16, 1024, 1024, 1024, "dw_xq")
    dmem_n = _mm(dkv, w_xkv, "nt", F32, mem.shape[0], 1024, 256, "d_mem", b_cols=256)
    d_w_xkv = _mm(mem_n, dkv, "tn", BF16, 1024, 256, mem.shape[0], "dw_xkv", out_cols=256)
    _, dg_mem = _norm_bwd(dmem_n, mem, jnp.zeros_like(mem), gains["mem"], "norm_bwd_mem")
    dycat = _mm(dy1, w_mix, "nt", BF16, 1024, 1024, 1024, "d_mix_out")
    d_w_mix = _mm(ycat, dy1, "tn", BF16, 1024, 1024, 1024, "dw_mix")
    sent_mix = send_mix_grads(d_w_mix, d_w_xq, d_w_xo, d_w_xkv)
    ad = _fox_do_operand(dycat, ycat, sent_mix, "fox_do_operand")
    dq, dk, dv, qaux, kaux = _fox_bwd(proj, dycat, aqb, ak, ad, "fox_bwd")
    du, d_wbd, d_scale = _pool_bwd(proj, dycat, wbd, scale, "pool_bwd")
    df, db_f = _gate_bwd(qaux, kaux, flog, "gate_bwd")
    dproj = jnp.concatenate([du, dq, dk, dv, df], axis=1)
    sent_in = send_in_grad(_mm(h1, dproj, "tn", BF16, 512, D_IN_PAD, 1024, "dw_in"))
    grad_x, dg_mix_pre = _mm_rows(dproj, w_in, "nt", D_IN_PAD, "dh_mix", [x, dx1], [gains["mix_pre"]],
                                  [F32, "sum"], _epi_norm_bwd, after=sent_in)

    small = dict(
        mix_pre=dg_mix_pre, mix_post=dg_mix_post, mem=dg_mem, xa_pre=dg_xa_pre, xa_post=dg_xa_post,
        ffn_pre=dg_ffn_pre, ffn_post=dg_ffn_post,
        conv_b=jnp.concatenate([dcb_g, dcb_u], axis=1),
        w_pool=jnp.concatenate([d_wbd[64 * g:64 * g + 64, 64 * g:64 * g + 64] for g in range(4)], axis=0),
        pool_scale=d_scale.reshape(4, 64),
        b_forget=db_f[:, :FOX_HEADS],
    )
    return loss, grad_x, small


SMALL_ORDER = ("mix_pre", "mix_post", "mem", "xa_pre", "xa_post", "ffn_pre", "ffn_post", "conv_b",
               "w_pool", "pool_scale", "b_forget")


def kernel(x, mem, norm_mix_pre, norm_mix_post, w_in, b_forget, w_pool, pool_scale, w_mix_out, norm_mem, norm_xa_pre, norm_xa_post, w_xq, w_xkv, w_xo, norm_ffn_pre, norm_ffn_post, w_up, conv_w, conv_b, w_down, loss_target, m_norm_mix_pre, m_norm_mix_post, m_w_in, m_b_forget, m_w_pool, m_pool_scale, m_w_mix_out, m_norm_mem, m_norm_xa_pre, m_norm_xa_post, m_w_xq, m_w_xkv, m_w_xo, m_norm_ffn_pre, m_norm_ffn_post, m_w_up, m_conv_w, m_conv_b, m_w_down, v_norm_mix_pre, v_norm_mix_post, v_w_in, v_b_forget, v_w_pool, v_pool_scale, v_w_mix_out, v_norm_mem, v_norm_xa_pre, v_norm_xa_post, v_w_xq, v_w_xkv, v_w_xo, v_norm_ffn_pre, v_norm_ffn_post, v_w_up, v_conv_w, v_conv_b, v_w_down):
    names = ("norm_mix_pre", "norm_mix_post", "w_in", "b_forget", "w_pool", "pool_scale", "w_mix_out", "norm_mem",
             "norm_xa_pre", "norm_xa_post", "w_xq", "w_xkv", "w_xo", "norm_ffn_pre", "norm_ffn_post", "w_up",
             "conv_w", "conv_b", "w_down")
    w = dict(zip(names, (norm_mix_pre, norm_mix_post, w_in, b_forget, w_pool, pool_scale, w_mix_out, norm_mem,
                         norm_xa_pre, norm_xa_post, w_xq, w_xkv, w_xo, norm_ffn_pre, norm_ffn_post, w_up,
                         conv_w, conv_b, w_down)))
    mo = dict(zip(names, (m_norm_mix_pre, m_norm_mix_post, m_w_in, m_b_forget, m_w_pool, m_pool_scale, m_w_mix_out,
                          m_norm_mem, m_norm_xa_pre, m_norm_xa_post, m_w_xq, m_w_xkv, m_w_xo, m_norm_ffn_pre,
                          m_norm_ffn_post, m_w_up, m_conv_w, m_conv_b, m_w_down)))
    vo = dict(zip(names, (v_norm_mix_pre, v_norm_mix_post, v_w_in, v_b_forget, v_w_pool, v_pool_scale, v_w_mix_out,
                          v_norm_mem, v_norm_xa_pre, v_norm_xa_post, v_w_xq, v_w_xkv, v_w_xo, v_norm_ffn_pre,
                          v_norm_ffn_post, v_w_up, v_conv_w, v_conv_b, v_w_down)))

    big_names = ("w_in", "w_mix_out", "w_xq", "w_xo", "w_xkv", "w_up", "w_down")
    shards = {k: w[k][0].astype(BF16) for k in big_names}
    shards["w_in"] = jnp.pad(shards["w_in"], ((0, 0), (0, D_IN_PAD - shards["w_in"].shape[1])))
    conv_w_sh = jnp.pad(conv_w[0, :, 0, :], ((0, 5), (0, 0)))
    (g_in,) = _all_gather([shards["w_in"]], "gather_w_in")
    mix_srcs = [shards[k] for k in ("w_mix_out", "w_xq", "w_xo", "w_xkv")]
    mix_flight = _exchange_start(mix_srcs, [_own_slot(a) for a in mix_srcs], g_in, True, "gather_mix_start")
    ffn_srcs = [shards["w_up"], shards["w_down"], conv_w_sh]
    ffn_flight = _exchange_start(ffn_srcs, [_own_slot(a) for a in ffn_srcs], mix_flight[4], True, "gather_ffn_start")
    my_slot = 4 * lax.axis_index("x") + 2 * lax.axis_index("y") + lax.axis_index("c")
    own_block = lambda a: _own_slot(lax.dynamic_index_in_dim(a, my_slot, 0, keepdims=False))
    by_rows = lambda a: a.reshape(NDEV, a.shape[0] // NDEV, a.shape[1])
    by_cols = lambda a: a.reshape(a.shape[0], NDEV, a.shape[1] // NDEV).transpose(1, 0, 2)
    grad_flight = {}

    def mix_weights(after):
        g_mix, g_xq, g_xo, g_xkv = _exchange_wait(*mix_flight[:4], after, True, "gather_mix_wait")
        return (g_mix.reshape(D_MODEL, D_MODEL), g_xq.reshape(D_MODEL, D_MODEL), g_xo.reshape(D_MODEL, D_MODEL), g_xkv)

    def ffn_weights(after):
        g_up, g_down, g_cw = _exchange_wait(*ffn_flight[:4], after, True, "gather_ffn_wait")
        return g_up, g_down.reshape(D_FF, D_MODEL), g_cw.transpose(1, 0, 2).reshape(8, 2 * D_FF)

    def send_ffn_grads(d_w_up, d_w_down, d_cw):
        srcs = [d_w_up, by_rows(d_w_down), by_cols(d_cw)]
        grad_flight["ffn"] = _exchange_start(srcs, [own_block(a) for a in srcs], d_w_up, False, "scatter_ffn_start")
        return grad_flight["ffn"][4]

    def send_mix_grads(d_w_mix, d_w_xq, d_w_xo, d_w_xkv):
        srcs = [by_rows(d_w_mix), by_rows(d_w_xq), by_rows(d_w_xo), d_w_xkv]
        grad_flight["mix"] = _exchange_start(srcs, [own_block(a) for a in srcs], d_w_mix, False, "scatter_mix_start")
        return grad_flight["mix"][4]

    def send_in_grad(d_w_in):
        srcs = [by_rows(d_w_in)]
        grad_flight["in"] = _exchange_start(srcs, [own_block(a) for a in srcs], d_w_in, False, "scatter_in_start")
        return grad_flight["in"][4]

    gains = dict(mix_pre=norm_mix_pre + ffn_flight[4][0, 0], mix_post=norm_mix_post, mem=norm_mem, xa_pre=norm_xa_pre,
                 xa_post=norm_xa_post, ffn_pre=norm_ffn_pre, ffn_post=norm_ffn_post)
    loss, grad_x, small = _local_step(
        x[0], mem[0], loss_target[0], gains, b_forget, w_pool[0], pool_scale[0], conv_b,
        g_in.reshape(D_MODEL, D_IN_PAD), mix_weights, ffn_weights, send_in_grad, send_mix_grads, send_ffn_grads)

    p_up, p_down, p_cw = _exchange_wait(*grad_flight["ffn"][:4], grad_x, False, "scatter_ffn_wait")
    p_mix, p_xq, p_xo, p_xkv = _exchange_wait(*grad_flight["mix"][:4], grad_x, False, "scatter_mix_wait")
    (p_in,) = _exchange_wait(*grad_flight["in"][:4], grad_x, False, "scatter_in_wait")
    parts = [p_in, p_mix, p_xq, p_xo, p_xkv, p_up, p_down, p_cw]
    small_parts = _all_gather([small[k] for k in SMALL_ORDER], "gather_small_grads")

    res = {}
    for k, p in zip(big_names, parts[:7]):
        if k == "w_in":
            p = p[:, :, :w_in.shape[2]]
        res[k] = [a[None] for a in _adamw(p, w[k][0], mo[k][0], vo[k][0], "adamw_" + k)]
    pad_cw = lambda a: jnp.pad(a[0, :, 0, :], ((0, 5), (0, 0)))
    res["conv_w"] = [a[:3][None, :, None, :] for a in
                     _adamw(parts[7], pad_cw(conv_w), pad_cw(m_conv_w), pad_cw(v_conv_w), "adamw_conv_w")]
    key_of = dict(mix_pre="norm_mix_pre", mix_post="norm_mix_post", mem="norm_mem", xa_pre="norm_xa_pre",
                  xa_post="norm_xa_post", ffn_pre="norm_ffn_pre", ffn_post="norm_ffn_post", conv_b="conv_b",
                  w_pool="w_pool", pool_scale="pool_scale", b_forget="b_forget")
    flat2d = lambda src: [src[key_of[k]].reshape(small[k].shape) for k in SMALL_ORDER]
    small_out = _adamw_small(small_parts, flat2d(w), flat2d(mo), flat2d(vo), "adamw_small")
    for k, four in zip(SMALL_ORDER, small_out):
        res[key_of[k]] = [a.reshape(w[key_of[k]].shape) for a in four]

    total = lax.psum(loss[0, 0], ("x", "y", "c"))
    outs = [total, grad_x[None]]
    for idx in range(4):
        outs += [res[k][idx] for k in names]
    return tuple(outs)
```

```python
import functools
import math

import jax
import jax.numpy as jnp
from jax import lax
from jax.experimental import pallas as pl
from jax.experimental.pallas import tpu as pltpu

F32 = jnp.float32
BF16 = jnp.bfloat16

NDEV = 8
D_MODEL = 1024
D_POOL = 256
D_FOX = 768
FOX_HEADS = 12
HEAD_PAIRS = FOX_HEADS // 2
XA_HEADS = 4
XA_DIM = 256
D_FF = 4096
D_IN_PAD = 2688
F_COL = 2560
POOL_HALO = 16
NORM_EPS = 1e-6
NEG = -1e30

ADAM_LR = 0.001
ADAM_B1 = 0.9
ADAM_B2 = 0.999
ADAM_EPS = 1e-08
ADAM_WD = 0.01
ADAM_STEP = 10

TM = 512
TQ = 512
TN_FF = 1024
VMEM_LIMIT = 56 * 1024 * 1024
MESH = pl.DeviceIdType.MESH


def _cp(*sem):
    return pltpu.CompilerParams(dimension_semantics=sem, vmem_limit_bytes=VMEM_LIMIT)


def _dot(a, b, dims):
    return lax.dot_general(a, b, (dims, ((), ())), preferred_element_type=F32)


NN = ((1,), (0,))
NT = ((1,), (1,))
TN = ((0,), (0,))


def _mm(a, b, mode, out_dtype, tm, tn, tk, name, b_cols=None, out_cols=None, after=None):
    a_list = list(a) if isinstance(a, (list, tuple)) else [a]
    b_list = list(b) if isinstance(b, (list, tuple)) else [b]
    assert len(a_list) == 1 or len(b_list) == 1
    if mode == "tn":
        K, M = a_list[0].shape
        assert len(a_list) == 1
        Ns = [x.shape[1] for x in b_list]
        N = sum(Ns)
        assert b_cols is None
    else:
        assert len(b_list) == 1
        M = a_list[0].shape[0]
        Ks = [x.shape[1] for x in a_list]
        K = sum(Ks)
        if b_cols is None:
            N = b_list[0].shape[0] if mode == "nt" else b_list[0].shape[1]
        else:
            N = b_list[0].shape[1] if mode == "nt" else NDEV * b_cols
    assert M % tm == 0 and N % tn == 0 and K % tk == 0, (name, M, N, K)
    grid = (M // tm, N // tn, K // tk)
    nk = grid[2]
    dims = {"nn": NN, "nt": NT, "tn": TN}[mode]

    in_specs = []
    if mode == "tn":
        in_specs.append(pl.BlockSpec((tk, tm), lambda i, j, k: (k, i)))
        if len(b_list) == 1:
            in_specs.append(pl.BlockSpec((tk, tn), lambda i, j, k: (k, j)))
        else:
            nj1 = Ns[0] // tn
            in_specs.append(pl.BlockSpec((tk, tn), lambda i, j, k: (k, jnp.minimum(j, nj1 - 1))))
            in_specs.append(pl.BlockSpec((tk, tn), lambda i, j, k: (k, jnp.maximum(j - nj1, 0))))
    else:
        if len(a_list) == 1:
            in_specs.append(pl.BlockSpec((tm, tk), lambda i, j, k: (i, k)))
        else:
            nk1 = Ks[0] // tk
            in_specs.append(pl.BlockSpec((tm, tk), lambda i, j, k: (i, jnp.minimum(k, nk1 - 1))))
            in_specs.append(pl.BlockSpec((tm, tk), lambda i, j, k: (i, jnp.maximum(k - nk1, 0))))
        if b_cols is None:
            if mode == "nn":
                in_specs.append(pl.BlockSpec((tk, tn), lambda i, j, k: (k, j)))
            else:
                in_specs.append(pl.BlockSpec((tn, tk), lambda i, j, k: (j, k)))
        else:
            if mode == "nn":
                per = b_cols // tn
                in_specs.append(pl.BlockSpec((None, tk, tn), lambda i, j, k: (j // per, k, j % per)))
            else:
                per = b_cols // tk
                in_specs.append(pl.BlockSpec((None, tn, tk), lambda i, j, k: (k // per, j, k % per)))
    if out_cols is None:
        out_spec = pl.BlockSpec((tm, tn), lambda i, j, k: (i, j))
        out_shape = jax.ShapeDtypeStruct((M, N), out_dtype)
    else:
        pero = out_cols // tn
        out_spec = pl.BlockSpec((None, tm, tn), lambda i, j, k: (j // pero, i, j % pero))
        out_shape = jax.ShapeDtypeStruct((NDEV, M, out_cols), out_dtype)

    two_a = len(a_list) == 2
    two_b = len(b_list) == 2
    extra = []
    if after is not None:
        in_specs.append(pl.BlockSpec(memory_space=pl.ANY))
        extra.append(after)

    def body(*refs):
        o_ref, acc_ref = refs[-2], refs[-1]
        j = pl.program_id(1)
        k = pl.program_id(2)

        @pl.when(k == 0)
        def _():
            acc_ref[...] = jnp.zeros_like(acc_ref)

        if two_a:
            a1, a2, b1 = refs[0], refs[1], refs[2]
            nk1_ = Ks[0] // tk

            @pl.when(k < nk1_)
            def _():
                acc_ref[...] += _dot(a1[...], b1[...], dims)

            @pl.when(k >= nk1_)
            def _():
                acc_ref[...] += _dot(a2[...], b1[...], dims)
        elif two_b:
            a1, b1, b2 = refs[0], refs[1], refs[2]
            nj1_ = Ns[0] // tn

            @pl.when(j < nj1_)
            def _():
                acc_ref[...] += _dot(a1[...], b1[...], dims)

            @pl.when(j >= nj1_)
            def _():
                acc_ref[...] += _dot(a1[...], b2[...], dims)
        else:
            acc_ref[...] += _dot(refs[0][...], refs[1][...], dims)

        @pl.when(k == nk - 1)
        def _():
            o_ref[...] = acc_ref[...].astype(o_ref.dtype)

    return pl.pallas_call(
        body, name=name, grid=grid, in_specs=in_specs, out_specs=out_spec, out_shape=out_shape,
        scratch_shapes=[pltpu.VMEM((tm, tn), F32)],
        compiler_params=_cp("parallel", "parallel", "arbitrary"),
    )(*a_list, *b_list, *extra)


def _rstd(x):
    return lax.rsqrt(jnp.mean(x * x, axis=-1, keepdims=True) + NORM_EPS)


def _norm_bwd_rows(dxn, xn, r):
    return r * (dxn - xn * jnp.mean(dxn * xn, axis=-1, keepdims=True))


def _row_spec(tm, d):
    return pl.BlockSpec((tm, d), lambda i: (i, 0))


def _vec_spec(d):
    return pl.BlockSpec((1, d), lambda i: (0, 0))


def _mm_rows(a, b, mode, tk, name, rows, vecs, outs, epilogue, b_cols=None, after=None):
    a_list = list(a) if isinstance(a, (list, tuple)) else [a]
    m = a_list[0].shape[0]
    ks = [x.shape[1] for x in a_list]
    n = D_MODEL
    nk = sum(ks) // tk
    dims = NN if mode == "nn" else NT
    if len(a_list) == 1:
        in_specs = [pl.BlockSpec((TM, tk), lambda i, k: (i, k))]
    else:
        nk1 = ks[0] // tk
        in_specs = [pl.BlockSpec((TM, tk), lambda i, k: (i, jnp.minimum(k, nk1 - 1))),
                    pl.BlockSpec((TM, tk), lambda i, k: (i, jnp.maximum(k - nk1, 0)))]
    if mode == "nn":
        in_specs.append(pl.BlockSpec((tk, n), lambda i, k: (k, 0)))
    elif b_cols is None:
        in_specs.append(pl.BlockSpec((n, tk), lambda i, k: (0, k)))
    else:
        per = b_cols // tk
        in_specs.append(pl.BlockSpec((None, n, tk), lambda i, k: (k // per, 0, k % per)))
    in_specs += [pl.BlockSpec((TM, n), lambda i, k: (i, 0))] * len(rows)
    in_specs += [pl.BlockSpec((1, n), lambda i, k: (0, 0))] * len(vecs)
    extra = []
    if after is not None:
        in_specs.append(pl.BlockSpec(memory_space=pl.ANY))
        extra.append(after)
    out_specs, out_shape = [], []
    for o in outs:
        if o == "sum":
            out_specs.append(pl.BlockSpec((1, n), lambda i, k: (0, 0)))
            out_shape.append(jax.ShapeDtypeStruct((1, n), F32))
        else:
            out_specs.append(pl.BlockSpec((TM, n), lambda i, k: (i, 0)))
            out_shape.append(jax.ShapeDtypeStruct((m, n), o))
    na, nr, nv = len(a_list), len(rows), len(vecs)

    def body(*refs):
        a_refs, b_ref = refs[:na], refs[na]
        row_refs = refs[na + 1:na + 1 + nr]
        vec_refs = refs[na + 1 + nr:na + 1 + nr + nv]
        out_refs = refs[len(refs) - 1 - len(outs):len(refs) - 1]
        acc_ref = refs[-1]
        i, k = pl.program_id(0), pl.program_id(1)

        @pl.when(k == 0)
        def _():
            acc_ref[...] = jnp.zeros_like(acc_ref)

        if na == 1:
            acc_ref[...] += _dot(a_refs[0][...], b_ref[...], dims)
        else:
            nk1_ = ks[0] // tk

            @pl.when(k < nk1_)
            def _():
                acc_ref[...] += _dot(a_refs[0][...], b_ref[...], dims)

            @pl.when(k >= nk1_)
            def _():
                acc_ref[...] += _dot(a_refs[1][...], b_ref[...], dims)

        @pl.when(k == nk - 1)
        def _():
            vals = epilogue(acc_ref[...], [r[...] for r in row_refs], [v[...] for v in vec_refs])
            for o, ref, val in zip(outs, out_refs, vals):
                if o == "sum":
                    @pl.when(i == 0)
                    def _():
                        ref[...] = val

                    @pl.when(i > 0)
                    def _():
                        ref[...] += val
                else:
                    ref[...] = val.astype(o)

    return pl.pallas_call(
        body, name=name, grid=(m // TM, nk), in_specs=in_specs, out_specs=out_specs, out_shape=out_shape,
        scratch_shapes=[pltpu.VMEM((TM, n), F32)],
        compiler_params=_cp("arbitrary", "arbitrary"),
    )(*a_list, b, *rows, *vecs, *extra)


def _epi_resid(y, rows, vecs):
    (x_in,), (g_post, g_next) = rows, vecs
    xo = x_in + y * _rstd(y) * g_post
    return y, xo, xo * _rstd(xo) * g_next


def _epi_norm_bwd(dh, rows, vecs):
    x, dx_res = rows[0], rows[1]
    r = _rstd(x)
    xn = x * r
    dx = dx_res + _norm_bwd_rows(dh * vecs[0], xn, r)
    res = [dx, jnp.sum(dh * xn, axis=0, keepdims=True)]
    if len(rows) == 3:
        y = rows[2]
        r2 = _rstd(y)
        yn = y * r2
        res += [_norm_bwd_rows(dx * vecs[1], yn, r2), jnp.sum(dx * yn, axis=0, keepdims=True)]
    return res


def _norm_fwd(x, g, name):
    s, d = x.shape
    tm = min(TM, s)

    def body(x_ref, g_ref, h_ref):
        xv = x_ref[...]
        h_ref[...] = (xv * _rstd(xv) * g_ref[...]).astype(BF16)

    return pl.pallas_call(
        body, name=name, grid=(s // tm,), in_specs=[_row_spec(tm, d), _vec_spec(d)],
        out_specs=_row_spec(tm, d), out_shape=jax.ShapeDtypeStruct((s, d), BF16),
        compiler_params=_cp("parallel"),
    )(x, g)


def _norm_bwd(dh, x, dx_res, g_pre, name, prev=None):
    s, d = x.shape
    tm = min(TM, s)
    has_prev = prev is not None

    def body(*refs):
        if has_prev:
            dh_ref, x_ref, dr_ref, g_ref, y_ref, gp_ref, dx_ref, dg_ref, dy_ref, dgp_ref = refs
        else:
            dh_ref, x_ref, dr_ref, g_ref, dx_ref, dg_ref = refs
        i = pl.program_id(0)
        xv = x_ref[...]
        r = _rstd(xv)
        xn = xv * r
        dhv = dh_ref[...].astype(F32)
        dx = dr_ref[...] + _norm_bwd_rows(dhv * g_ref[...], xn, r)
        dx_ref[...] = dx
        dg = jnp.sum(dhv * xn, axis=0, keepdims=True)

        @pl.when(i == 0)
        def _():
            dg_ref[...] = dg

        @pl.when(i > 0)
        def _():
            dg_ref[...] += dg

        if has_prev:
            yv = y_ref[...]
            r2 = _rstd(yv)
            yn = yv * r2
            dy_ref[...] = _norm_bwd_rows(dx * gp_ref[...], yn, r2).astype(BF16)
            dgp = jnp.sum(dx * yn, axis=0, keepdims=True)

            @pl.when(i == 0)
            def _():
                dgp_ref[...] = dgp

            @pl.when(i > 0)
            def _():
                dgp_ref[...] += dgp

    in_specs = [_row_spec(tm, d), _row_spec(tm, d), _row_spec(tm, d), _vec_spec(d)]
    out_specs = [_row_spec(tm, d), _vec_spec(d)]
    out_shape = [jax.ShapeDtypeStruct((s, d), F32), jax.ShapeDtypeStruct((1, d), F32)]
    args = [dh, x, dx_res, g_pre]
    if has_prev:
        in_specs += [_row_spec(tm, d), _vec_spec(d)]
        out_specs += [_row_spec(tm, d), _vec_spec(d)]
        out_shape += [jax.ShapeDtypeStruct((s, d), BF16), jax.ShapeDtypeStruct((1, d), F32)]
        args += list(prev)
    return pl.pallas_call(
        body, name=name, grid=(s // tm,), in_specs=in_specs, out_specs=out_specs, out_shape=out_shape,
        compiler_params=_cp("arbitrary"),
    )(*args)


def _split3(v):
    hi = v.astype(BF16)
    r1 = v - hi.astype(F32)
    mid = r1.astype(BF16)
    lo = (r1 - mid.astype(F32)).astype(BF16)
    return hi, mid, lo


def _tri_dot(tri, v):
    hi, mid, lo = _split3(v)
    return _dot(tri, hi, NN) + _dot(tri, mid, NN) + _dot(tri, lo, NN)


def _gate_cumsum(fraw, b_pad, name):
    s = fraw.shape[0]

    def body(f_ref, b_ref, flog_ref, cum_ref, carry_ref):
        i = pl.program_id(0)

        @pl.when(i == 0)
        def _():
            carry_ref[...] = jnp.zeros_like(carry_ref)

        flog = f_ref[...] + b_ref[...]
        flog_ref[...] = flog
        lf = jnp.minimum(flog, 0.0) - jnp.log(1.0 + jnp.exp(-jnp.abs(flog)))
        lane = lax.broadcasted_iota(jnp.int32, (1, 128), 1)
        lf = jnp.where(lane < FOX_HEADS, lf, 0.0)
        row = lax.broadcasted_iota(jnp.int32, (TM, TM), 0)
        col = lax.broadcasted_iota(jnp.int32, (TM, TM), 1)
        tri = (row >= col).astype(BF16)
        cum = _tri_dot(tri, lf) + carry_ref[...]
        cum_ref[...] = cum
        carry_ref[...] = cum[TM - 1:TM, :]

    return pl.pallas_call(
        body, name=name, grid=(s // TM,),
        in_specs=[_row_spec(TM, 128), _vec_spec(128)],
        out_specs=[_row_spec(TM, 128), _row_spec(TM, 128)],
        out_shape=[jax.ShapeDtypeStruct((s, 128), F32), jax.ShapeDtypeStruct((s, 128), F32)],
        scratch_shapes=[pltpu.VMEM((1, 128), F32)],
        compiler_params=_cp("arbitrary"),
    )(fraw, b_pad)


def _gate_bwd(qaux, kaux, flog, name):
    s = flog.shape[0]
    n = s // TM

    def body(qa_ref, ka_ref, fl_ref, dp_ref, db_ref, carry_ref):
        i = pl.program_id(0)

        @pl.when(i == 0)
        def _():
            carry_ref[...] = jnp.zeros_like(carry_ref)

        src = lax.broadcasted_iota(jnp.int32, (128, 128), 0)
        dst = lax.broadcasted_iota(jnp.int32, (128, 128), 1)
        dcum = jnp.zeros((TM, 128), F32)
        for p in range(HEAD_PAIRS):
            for ref, l0, l1, sign in ((qa_ref, 64, 0, 1.0), (ka_ref, 67, 3, -1.0)):
                hit = jnp.logical_or(jnp.logical_and(src == l0, dst == 2 * p),
                                     jnp.logical_and(src == l1, dst == 2 * p + 1))
                sel = jnp.where(hit, sign, 0.0).astype(BF16)
                for piece in _split3(ref[p]):
                    dcum = dcum + _dot(piece, sel, NN)
        row = lax.broadcasted_iota(jnp.int32, (TM, TM), 0)
        col = lax.broadcasted_iota(jnp.int32, (TM, TM), 1)
        tri = (row <= col).astype(BF16)
        dlf = _tri_dot(tri, dcum) + carry_ref[...]
        carry_ref[...] = dlf[0:1, :]
        lane = lax.broadcasted_iota(jnp.int32, (1, 128), 1)
        df = jnp.where(lane < FOX_HEADS, dlf / (1.0 + jnp.exp(fl_ref[...])), 0.0)
        dp_ref[...] = df.astype(BF16)
        db = jnp.sum(df, axis=0, keepdims=True)

        @pl.when(i == 0)
        def _():
            db_ref[...] = db

        @pl.when(i > 0)
        def _():
            db_ref[...] += db

    rev = lambda i: (n - 1 - i, 0)
    return pl.pallas_call(
        body, name=name, grid=(n,),
        in_specs=[pl.BlockSpec((HEAD_PAIRS, TM, 128), lambda i: (0, n - 1 - i, 0)),
                  pl.BlockSpec((HEAD_PAIRS, TM, 128), lambda i: (0, n - 1 - i, 0)), pl.BlockSpec((TM, 128), rev)],
        out_specs=[pl.BlockSpec((TM, 128), rev), _vec_spec(128)],
        out_shape=[jax.ShapeDtypeStruct((s, 128), BF16), jax.ShapeDtypeStruct((1, 128), F32)],
        scratch_shapes=[pltpu.VMEM((1, 128), F32)],
        compiler_params=_cp("arbitrary"),
    )(qaux, kaux, flog)


def _pool_consts(i, rows):
    lane = lax.broadcasted_iota(jnp.int32, (rows, D_POOL), 1)
    t1 = lax.broadcasted_iota(jnp.int32, (rows, D_POOL), 0) + i * TM + 1
    win = jnp.where(lane < 64, 2, jnp.where(lane < 128, 4, jnp.where(lane < 192, 8, 16)))
    inv = 1.0 / jnp.minimum(t1, win).astype(F32)
    return lane, inv


def _by_group(lane, s2, s4, s8, s16):
    return jnp.where(lane < 64, s2, jnp.where(lane < 128, s4, jnp.where(lane < 192, s8, s16)))


def _pool_diff(i, u_ref, halo_ref):
    u = u_ref[...].astype(F32)
    halo = jnp.where(i > 0, halo_ref[...].astype(F32), 0.0)
    ext = jnp.concatenate([halo, u], axis=0)
    s2 = ext + pltpu.roll(ext, 1, 0)
    s4 = s2 + pltpu.roll(s2, 2, 0)
    s8 = s4 + pltpu.roll(s4, 4, 0)
    s16 = s8 + pltpu.roll(s8, 8, 0)
    lane, inv = _pool_consts(i, TM)
    sel = _by_group(lane, s2[POOL_HALO:], s4[POOL_HALO:], s8[POOL_HALO:], s16[POOL_HALO:])
    return sel * inv - u


def _pool_fwd(proj, wbd, scale, ycat, name):
    s = proj.shape[0]
    hb = TM // POOL_HALO

    def body(u_ref, halo_ref, w_ref, sc_ref, y_any, y_ref):
        del y_any
        i = pl.program_id(0)
        diff = _pool_diff(i, u_ref, halo_ref)
        mixed = _dot(diff.astype(BF16), w_ref[...], NN)
        y_ref[...] = (mixed * sc_ref[...]).astype(BF16)

    return pl.pallas_call(
        body, name=name, grid=(s // TM,),
        in_specs=[pl.BlockSpec((TM, D_POOL), lambda i: (i, 0)),
                  pl.BlockSpec((POOL_HALO, D_POOL), lambda i: (jnp.maximum(i * hb - 1, 0), 0)),
                  pl.BlockSpec((D_POOL, D_POOL), lambda i: (0, 0)), _vec_spec(D_POOL),
                  pl.BlockSpec(memory_space=pl.ANY)],
        out_specs=pl.BlockSpec((TM, D_POOL), lambda i: (i, 0)),
        out_shape=jax.ShapeDtypeStruct(ycat.shape, ycat.dtype),
        input_output_aliases={4: 0},
        compiler_params=_cp("parallel"),
    )(proj, proj, wbd, scale, ycat)


def _pool_bwd(proj, dycat, wbd, scale, name):
    s = proj.shape[0]
    n = s // TM
    hb = TM // POOL_HALO
    last_halo = s // POOL_HALO - 1

    def body(u_ref, halo_ref, dy_ref, dyp_ref, w_ref, sc_ref, dp_ref, dw_ref, dsc_ref):
        i = pl.program_id(0)
        diff = _pool_diff(i, u_ref, halo_ref)
        diff_b = diff.astype(BF16)
        mixed = _dot(diff_b, w_ref[...], NN)
        dy = dy_ref[...].astype(F32)
        dmix = (dy * sc_ref[...]).astype(BF16)
        dyp = jnp.where(i < n - 1, dyp_ref[...].astype(F32), 0.0)
        dmix_p = (dyp * sc_ref[...]).astype(BF16)
        dd = _dot(dmix, w_ref[...], NT)
        dd_p = _dot(dmix_p, w_ref[...], NT)
        lane, inv = _pool_consts(i, TM)
        _, inv_p = _pool_consts(i + 1, POOL_HALO)
        ext = jnp.concatenate([dd * inv, dd_p * inv_p], axis=0)
        rows = TM + POOL_HALO
        l2 = ext + pltpu.roll(ext, rows - 1, 0)
        l4 = l2 + pltpu.roll(l2, rows - 2, 0)
        l8 = l4 + pltpu.roll(l4, rows - 4, 0)
        l16 = l8 + pltpu.roll(l8, rows - 8, 0)
        du = _by_group(lane, l2[:TM], l4[:TM], l8[:TM], l16[:TM]) - dd
        dp_ref[...] = du.astype(BF16)
        dw = _dot(diff_b, dmix, TN)
        dsc = jnp.sum(dy * mixed, axis=0, keepdims=True)

        @pl.when(i == 0)
        def _():
            dw_ref[...] = dw
            dsc_ref[...] = dsc

        @pl.when(i > 0)
        def _():
            dw_ref[...] += dw
            dsc_ref[...] += dsc

    return pl.pallas_call(
        body, name=name, grid=(n,),
        in_specs=[pl.BlockSpec((TM, D_POOL), lambda i: (i, 0)),
                  pl.BlockSpec((POOL_HALO, D_POOL), lambda i: (jnp.maximum(i * hb - 1, 0), 0)),
                  pl.BlockSpec((TM, D_POOL), lambda i: (i, 0)),
                  pl.BlockSpec((POOL_HALO, D_POOL), lambda i: (jnp.minimum((i + 1) * hb, last_halo), 0)),
                  pl.BlockSpec((D_POOL, D_POOL), lambda i: (0, 0)), _vec_spec(D_POOL)],
        out_specs=[pl.BlockSpec((TM, D_POOL), lambda i: (i, 0)),
                   pl.BlockSpec((D_POOL, D_POOL), lambda i: (0, 0)), _vec_spec(D_POOL)],
        out_shape=[jax.ShapeDtypeStruct((s, D_POOL), BF16),
                   jax.ShapeDtypeStruct((D_POOL, D_POOL), F32), jax.ShapeDtypeStruct((1, D_POOL), F32)],
        compiler_params=_cp("arbitrary"),
    )(proj, proj, dycat, dycat, wbd, scale)


Q_BLK = D_POOL // 128
K_BLK = Q_BLK + D_FOX // 128
V_BLK = K_BLK + D_FOX // 128


def _operand_rows(v0, v1, ones_off):
    row = lax.broadcasted_iota(jnp.int32, (128, 1), 0)
    half = row & 63
    out = jnp.where(jnp.logical_and(half >= ones_off, half < ones_off + 3), 1.0, 0.0) + jnp.zeros_like(v0)
    for base, v in ((64, v0), (0, v1)):
        for j, piece in enumerate(_split3(v)):
            out = jnp.where(row == base + j, piece.astype(F32), out)
    return out


def _fox_operands(cum, name):
    s = cum.shape[0]
    width = HEAD_PAIRS * 128

    def body(c_ref, aq_ref, ak_ref):
        pieces = _split3(c_ref[...])
        row = lax.broadcasted_iota(jnp.int32, (128, width), 0)
        col = lax.broadcasted_iota(jnp.int32, (128, width), 1)
        base = (row >> 1) * 128 + (1 - (row & 1)) * 64
        half = lax.broadcasted_iota(jnp.int32, (1, width), 1) & 63
        for o_ref, off, sign, ones_off in ((aq_ref, 0, 1.0, 3), (ak_ref, 3, -1.0, 0)):
            out = jnp.where(jnp.logical_and(half >= ones_off, half < ones_off + 3), 1.0, 0.0)
            for j, piece in enumerate(pieces):
                sel = jnp.where(jnp.logical_and(col == base + off + j, row < FOX_HEADS), sign, 0.0).astype(BF16)
                out = out + _dot(piece, sel, NN)
            o_ref[...] = out.astype(BF16)

    return pl.pallas_call(
        body, name=name, grid=(s // TM,), in_specs=[_row_spec(TM, 128)],
        out_specs=[_row_spec(TM, width), _row_spec(TM, width)],
        out_shape=[jax.ShapeDtypeStruct((s, width), BF16)] * 2,
        compiler_params=_cp("parallel"),
    )(cum)


def _fox_do_operand(dycat, ycat, after, name):
    s = dycat.shape[0]

    rb = 1024
    nblk = D_FOX // D_POOL

    def body(*refs):
        do_refs, o_refs, ad_ref = refs[:nblk], refs[nblk:2 * nblk], refs[-1]
        src = lax.broadcasted_iota(jnp.int32, (D_POOL, D_POOL), 0)
        dst = lax.broadcasted_iota(jnp.int32, (D_POOL, D_POOL), 1)
        same_pair = (src >> 7) == (dst >> 7)
        s_in, d_in = src & 127, dst & 127
        hit = jnp.logical_and(same_pair, jnp.logical_or(
            jnp.logical_and(s_in < 64, jnp.logical_and(d_in >= 64, d_in < 67)), jnp.logical_and(s_in >= 64, d_in < 3)))
        sel = jnp.where(hit, 1.0, 0.0).astype(BF16)
        j = lax.broadcasted_iota(jnp.int32, (1, D_POOL), 1) & 63
        for b in range(nblk):
            dd = do_refs[b][...].astype(F32) * o_refs[b][...].astype(F32)
            dsum = jnp.zeros(dd.shape, F32)
            for piece in _split3(dd):
                dsum = dsum + _dot(piece, sel, NN)
            hi, mid, lo = _split3(-dsum)
            ad_ref[:, D_POOL * b:D_POOL * (b + 1)] = jnp.where(j == 0, hi, jnp.where(j == 1, mid, lo))

    blks = [pl.BlockSpec((rb, D_POOL), functools.partial(lambda i, b: (i, 1 + b), b=b)) for b in range(nblk)]
    return pl.pallas_call(
        body, name=name, grid=(s // rb,), in_specs=blks + blks + [pl.BlockSpec(memory_space=pl.ANY)],
        out_specs=pl.BlockSpec((rb, D_FOX), lambda i: (i, 0)),
        out_shape=jax.ShapeDtypeStruct((s, D_FOX), BF16),
        compiler_params=_cp("parallel"),
    )(*[dycat] * nblk, *[ycat] * nblk, after)


def _causal_pairs(nq, key_major):
    if key_major:
        pairs = [(q, k) for k in range(nq) for q in range(k, nq)]
    else:
        pairs = [(q, k) for q in range(nq) for k in range(q + 1)]
    return (jnp.asarray([p[0] for p in pairs], jnp.int32), jnp.asarray([p[1] for p in pairs], jnp.int32))


def _fox_fwd(proj, aq, ak, name):
    s = proj.shape[0]
    nq = s // TQ
    qi_arr, ki_arr = _causal_pairs(nq, key_major=False)

    def body(qi_ref, ki_ref, q_ref, k_ref, v_ref, aq_ref, ak_ref, o_ref, aqb_ref, m0_ref, m1_ref, acc_ref, aux_ref):
        t = pl.program_id(1)
        qi, ki = qi_ref[t], ki_ref[t]
        lane = lax.broadcasted_iota(jnp.int32, (1, 128), 1)
        masks = [lane < 64, lane >= 64]
        ones_v = jnp.where((lane & 63) == 8, 1.0, 0.0).astype(BF16)
        top = lax.broadcasted_iota(jnp.int32, (128, 1), 0) < 64
        m_ref = [m0_ref, m1_ref]

        @pl.when(ki == 0)
        def _():
            m0_ref[...] = jnp.full_like(m0_ref, NEG)
            m1_ref[...] = jnp.full_like(m1_ref, NEG)
            acc_ref[...] = jnp.zeros_like(acc_ref)
            aux_ref[...] = jnp.zeros_like(aux_ref)

        def step(diag):
            q2s = q_ref[...] * 0.125
            k2, v2, aq2, ak2 = k_ref[...], v_ref[...], aq_ref[...], ak_ref[...]
            pv, alpha = [], []
            for hh in range(2):
                qh = jnp.where(masks[hh], q2s, aq2)
                kh = jnp.where(masks[hh], k2, ak2)
                vh = jnp.where(masks[hh], v2, ones_v)
                sc = _dot(kh, qh, NT)
                if diag:
                    key = lax.broadcasted_iota(jnp.int32, sc.shape, 0)
                    qry = lax.broadcasted_iota(jnp.int32, sc.shape, 1)
                    sc = jnp.where(qry >= key, sc, NEG)
                m_prev = m_ref[hh][...]
                m_new = jnp.maximum(m_prev, jnp.max(sc, axis=0, keepdims=True))
                m_ref[hh][...] = m_new
                alpha.append(jnp.exp(m_prev - m_new))
                pv.append(_dot(vh, jnp.exp(sc - m_new).astype(BF16), TN))
            acc_ref[...] = acc_ref[...] * jnp.where(top, alpha[0], alpha[1]) + jnp.where(top, pv[0], pv[1])
            aux_ref[...] = aux_ref[...] * jnp.where(top, alpha[1], alpha[0]) + jnp.where(top, pv[1], pv[0])

        @pl.when(ki < qi)
        def _():
            step(False)

        @pl.when(ki == qi)
        def _():
            step(True)
            aux = aux_ref[...]
            l0, l1 = aux[72:73, :], aux[8:9, :]
            o_ref[...] = (acc_ref[...] * jnp.where(top, 1.0 / l0, 1.0 / l1)).T.astype(BF16)
            aqt = aq_ref[...].astype(F32).T
            cum0 = aqt[64:65, :] + aqt[65:66, :] + aqt[66:67, :]
            cum1 = aqt[0:1, :] + aqt[1:2, :] + aqt[2:3, :]
            aqb = _operand_rows(cum0 - (m0_ref[...] + jnp.log(l0)), cum1 - (m1_ref[...] + jnp.log(l1)), 3)
            aqb_ref[...] = aqb.T.astype(BF16)

    grid_spec = pltpu.PrefetchScalarGridSpec(
        num_scalar_prefetch=2, grid=(HEAD_PAIRS, int(qi_arr.shape[0])),
        in_specs=[pl.BlockSpec((TQ, 128), lambda p, t, qi, ki: (qi[t], Q_BLK + p)),
                  pl.BlockSpec((TQ, 128), lambda p, t, qi, ki: (ki[t], K_BLK + p)),
                  pl.BlockSpec((TQ, 128), lambda p, t, qi, ki: (ki[t], V_BLK + p)),
                  pl.BlockSpec((TQ, 128), lambda p, t, qi, ki: (qi[t], p)),
                  pl.BlockSpec((TQ, 128), lambda p, t, qi, ki: (ki[t], p))],
        out_specs=[pl.BlockSpec((TQ, 128), lambda p, t, qi, ki: (qi[t], Q_BLK + p)),
                   pl.BlockSpec((TQ, 128), lambda p, t, qi, ki: (qi[t], p))],
        scratch_shapes=[pltpu.VMEM((1, TQ), F32), pltpu.VMEM((1, TQ), F32),
                        pltpu.VMEM((128, TQ), F32), pltpu.VMEM((128, TQ), F32)])
    return pl.pallas_call(
        body, name=name, grid_spec=grid_spec,
        out_shape=[jax.ShapeDtypeStruct((s, D_MODEL), BF16), jax.ShapeDtypeStruct((s, HEAD_PAIRS * 128), BF16)],
        compiler_params=_cp("parallel", "arbitrary"),
    )(qi_arr, ki_arr, proj, proj, proj, aq, ak)


def _fox_bwd(proj, dycat, aqb, ak, ad, name):
    s = proj.shape[0]
    nq = s // TQ
    qi_arr, ki_arr = _causal_pairs(nq, key_major=True)

    def body(qi_ref, ki_ref, q_ref, k_ref, v_ref, do_ref, aq_ref, ak_ref, ad_ref,
             dq_ref, dk_ref, dv_ref, qaux_ref, kaux_ref, dq_acc, qaux_acc, dk_acc, dv_acc, kaux_acc):
        t = pl.program_id(1)
        qi, ki = qi_ref[t], ki_ref[t]
        lane = lax.broadcasted_iota(jnp.int32, (1, 128), 1)
        masks = [lane < 64, lane >= 64]
        ones_v = jnp.where((lane & 63) < 3, 1.0, 0.0).astype(BF16)
        top = lax.broadcasted_iota(jnp.int32, (128, 1), 0) < 64

        @pl.when(qi == ki)
        def _():
            dk_acc[...] = jnp.zeros_like(dk_acc)
            dv_acc[...] = jnp.zeros_like(dv_acc)
            kaux_acc[...] = jnp.zeros_like(kaux_acc)

        def step(diag):
            q2s = q_ref[...] * 0.125
            k2, v2, do2 = k_ref[...], v_ref[...], do_ref[...]
            aq2, ak2, ad2 = aq_ref[...], ak_ref[...], ad_ref[...]
            dq, dk, dv = [], [], []
            for hh in range(2):
                qh = jnp.where(masks[hh], q2s, aq2)
                kh = jnp.where(masks[hh], k2, ak2)
                doh = jnp.where(masks[hh], do2, ad2)
                vh = jnp.where(masks[hh], v2, ones_v)
                sc = _dot(kh, qh, NT)
                if diag:
                    key = lax.broadcasted_iota(jnp.int32, sc.shape, 0)
                    qry = lax.broadcasted_iota(jnp.int32, sc.shape, 1)
                    sc = jnp.where(qry >= key, sc, NEG)
                p = jnp.exp(sc)
                dsb = (p * _dot(vh, doh, NT)).astype(BF16)
                dv.append(_dot(p.astype(BF16), doh, NN))
                dk.append(_dot(dsb, qh, NN))
                dq.append(_dot(kh, dsb, TN))
            dk_acc[...] += jnp.where(masks[0], dk[0], dk[1])
            kaux_acc[...] += jnp.where(masks[0], dk[1], dk[0])
            dv_acc[...] += jnp.where(masks[0], dv[0], dv[1])
            dq_new = jnp.where(top, dq[0], dq[1])
            qaux_new = jnp.where(top, dq[1], dq[0])

            @pl.when(ki == 0)
            def _():
                dq_acc[qi] = dq_new
                qaux_acc[qi] = qaux_new

            @pl.when(ki > 0)
            def _():
                dq_acc[qi] += dq_new
                qaux_acc[qi] += qaux_new

        @pl.when(qi > ki)
        def _():
            step(False)

        @pl.when(qi == ki)
        def _():
            step(True)
            rows = pl.ds(pl.multiple_of(qi * TQ, TQ), TQ)
            dq_ref[rows, :] = (dq_acc[qi] * 0.125).T.astype(BF16)
            qaux_ref[rows, :] = qaux_acc[qi].T

        @pl.when(qi == nq - 1)
        def _():
            dk_ref[...] = dk_acc[...].astype(BF16)
            dv_ref[...] = dv_acc[...].astype(BF16)
            kaux_ref[...] = kaux_acc[...]

    grid_spec = pltpu.PrefetchScalarGridSpec(
        num_scalar_prefetch=2, grid=(HEAD_PAIRS, int(qi_arr.shape[0])),
        in_specs=[pl.BlockSpec((TQ, 128), lambda p, t, qi, ki: (qi[t], Q_BLK + p)),
                  pl.BlockSpec((TQ, 128), lambda p, t, qi, ki: (ki[t], K_BLK + p)),
                  pl.BlockSpec((TQ, 128), lambda p, t, qi, ki: (ki[t], V_BLK + p)),
                  pl.BlockSpec((TQ, 128), lambda p, t, qi, ki: (qi[t], Q_BLK + p)),
                  pl.BlockSpec((TQ, 128), lambda p, t, qi, ki: (qi[t], p)),
                  pl.BlockSpec((TQ, 128), lambda p, t, qi, ki: (ki[t], p)),
                  pl.BlockSpec((TQ, 128), lambda p, t, qi, ki: (qi[t], p))],
        out_specs=[pl.BlockSpec((s, 128), lambda p, t, qi, ki: (0, p)),
                   pl.BlockSpec((TQ, 128), lambda p, t, qi, ki: (ki[t], p)),
                   pl.BlockSpec((TQ, 128), lambda p, t, qi, ki: (ki[t], p)),
                   pl.BlockSpec((None, s, 128), lambda p, t, qi, ki: (p, 0, 0)),
                   pl.BlockSpec((None, TQ, 128), lambda p, t, qi, ki: (p, ki[t], 0))],
        scratch_shapes=[pltpu.VMEM((nq, 128, TQ), F32), pltpu.VMEM((nq, 128, TQ), F32),
                        pltpu.VMEM((TQ, 128), F32), pltpu.VMEM((TQ, 128), F32), pltpu.VMEM((TQ, 128), F32)])
    return pl.pallas_call(
        body, name=name, grid_spec=grid_spec,
        out_shape=[jax.ShapeDtypeStruct((s, D_FOX), BF16)] * 3 + [jax.ShapeDtypeStruct((HEAD_PAIRS, s, 128), F32)] * 2,
        compiler_params=_cp("arbitrary", "arbitrary"),
    )(qi_arr, ki_arr, proj, proj, proj, dycat, aqb, ak, ad)


XA_SCALE = XA_DIM ** -0.5


def _xattn_fwd(q2, kv, name):
    s = q2.shape[0]
    m = kv.shape[0]

    def body(q_ref, kv_ref, o_ref):
        for h in range(XA_HEADS):
            c0 = h * XA_DIM
            sc = _dot(kv_ref[:, c0:c0 + XA_DIM], q_ref[:, c0:c0 + XA_DIM], NT) * XA_SCALE
            e = jnp.exp(sc - jnp.max(sc, axis=0, keepdims=True))
            p = e / jnp.sum(e, axis=0, keepdims=True)
            ot = _dot(kv_ref[:, D_MODEL + c0:D_MODEL + c0 + XA_DIM], p.astype(BF16), TN)
            o_ref[:, c0:c0 + XA_DIM] = ot.T.astype(BF16)

    return pl.pallas_call(
        body, name=name, grid=(s // TM,),
        in_specs=[_row_spec(TM, D_MODEL), pl.BlockSpec((m, 2 * D_MODEL), lambda i: (0, 0))],
        out_specs=_row_spec(TM, D_MODEL), out_shape=jax.ShapeDtypeStruct((s, D_MODEL), BF16),
        compiler_params=_cp("parallel"),
    )(q2, kv)


def _xattn_bwd(q2, kv, do2, name):
    s = q2.shape[0]
    m = kv.shape[0]

    def body(q_ref, kv_ref, do_ref, dq_ref, dkv_ref):
        i = pl.program_id(0)

        @pl.when(i == 0)
        def _():
            dkv_ref[...] = jnp.zeros_like(dkv_ref)

        for h in range(XA_HEADS):
            c0 = h * XA_DIM
            v0 = D_MODEL + c0
            qh = q_ref[:, c0:c0 + XA_DIM]
            kh = kv_ref[:, c0:c0 + XA_DIM]
            doh = do_ref[:, c0:c0 + XA_DIM]
            sc = _dot(kh, qh, NT) * XA_SCALE
            e = jnp.exp(sc - jnp.max(sc, axis=0, keepdims=True))
            p = e / jnp.sum(e, axis=0, keepdims=True)
            dp = _dot(kv_ref[:, v0:v0 + XA_DIM], doh, NT)
            ds = p * (dp - jnp.sum(p * dp, axis=0, keepdims=True))
            dsb = (ds * XA_SCALE).astype(BF16)
            dq_ref[:, c0:c0 + XA_DIM] = _dot(kh, dsb, TN).T.astype(BF16)
            dkv_ref[:, c0:c0 + XA_DIM] += _dot(dsb, qh, NN)
            dkv_ref[:, v0:v0 + XA_DIM] += _dot(p.astype(BF16), doh, NN)

    return pl.pallas_call(
        body, name=name, grid=(s // TM,),
        in_specs=[_row_spec(TM, D_MODEL), pl.BlockSpec((m, 2 * D_MODEL), lambda i: (0, 0)), _row_spec(TM, D_MODEL)],
        out_specs=[_row_spec(TM, D_MODEL), pl.BlockSpec((m, 2 * D_MODEL), lambda i: (0, 0))],
        out_shape=[jax.ShapeDtypeStruct((s, D_MODEL), BF16), jax.ShapeDtypeStruct((m, 2 * D_MODEL), F32)],
        compiler_params=_cp("arbitrary"),
    )(q2, kv, do2)


GELU_C = math.sqrt(2.0 / math.pi)
GELU_A = 0.044715


def _gelu(x):
    return 0.5 * x * (1.0 + jnp.tanh(GELU_C * (x + GELU_A * x * x * x)))


def _gelu_and_grad(x):
    t = jnp.tanh(GELU_C * (x + GELU_A * x * x * x))
    g = 0.5 * x * (1.0 + t)
    dg = 0.5 * (1.0 + t) + 0.5 * x * (1.0 - t * t) * GELU_C * (1.0 + 3.0 * GELU_A * x * x)
    return g, dg


def _conv(h, s1, s2, w_ref, b_ref):
    return w_ref[0:1, :] * s2 + w_ref[1:2, :] * s1 + w_ref[2:3, :] * h + b_ref[...]


def _shift_down(main, prev8):
    row = lax.broadcasted_iota(jnp.int32, main.shape, 0)
    s1 = jnp.where(row == 0, prev8[7:8, :], pltpu.roll(main, 1, 0))
    s2 = jnp.where(row == 0, prev8[6:7, :], jnp.where(row == 1, prev8[7:8, :], pltpu.roll(main, 2, 0)))
    return s1, s2


def _shift_up(main, next8):
    n = main.shape[0]
    row = lax.broadcasted_iota(jnp.int32, main.shape, 0)
    u1 = jnp.where(row == n - 1, next8[0:1, :], pltpu.roll(main, n - 1, 0))
    u2 = jnp.where(row == n - 2, next8[0:1, :], jnp.where(row == n - 1, next8[1:2, :], pltpu.roll(main, n - 2, 0)))
    return u1, u2


def _ffn_fwd(h3, w_up, cw, cb, w_down, x2, tgt, g_post, name):
    s = h3.shape[0]
    tn = TN_FF
    nj = D_FF // tn
    per = D_MODEL // tn
    hb = TM // 8

    def body(h_ref, halo_ref, wg_ref, wu_ref, cwg_ref, cwu_ref, cbg_ref, cbu_ref, wd_ref, x_ref, t_ref, g_ref,
             hg_ref, hu_ref, cg_ref, cu_ref, a_ref, loss_ref, dx_ref, dy_ref, dg_ref, y_acc):
        i, j = pl.program_id(0), pl.program_id(1)
        h = h_ref[...]
        halo = halo_ref[...]
        halo = jnp.where(i > 0, halo, jnp.zeros_like(halo))
        conv = []
        for w_ref, cw_ref, cb_ref, hid_ref, c_ref in ((wg_ref, cwg_ref, cbg_ref, hg_ref, cg_ref),
                                                      (wu_ref, cwu_ref, cbu_ref, hu_ref, cu_ref)):
            hm = _dot(h, w_ref[...], NN)
            hid_ref[...] = hm.astype(BF16)
            s1, s2 = _shift_down(hm, _dot(halo, w_ref[...], NN))
            c = _conv(hm, s1, s2, cw_ref, cb_ref)
            c_ref[...] = c.astype(BF16)
            conv.append(c)
        a = (_gelu(conv[0]) * conv[1]).astype(BF16)
        a_ref[...] = a
        contrib = _dot(a, wd_ref[...], NN)

        @pl.when(j == 0)
        def _():
            y_acc[...] = contrib

        @pl.when(j > 0)
        def _():
            y_acc[...] += contrib

        @pl.when(j == nj - 1)
        def _():
            yv = y_acc[...]
            r = _rstd(yv)
            yn = yv * r
            e = x_ref[...] + yn * g_ref[...] - t_ref[...]
            part = 0.5 * jnp.sum(jnp.mean(e * e, axis=-1, keepdims=True), axis=0, keepdims=True)
            part = jnp.broadcast_to(part, (1, 128))
            dx = e * (1.0 / D_MODEL)
            dx_ref[...] = dx
            dy_ref[...] = _norm_bwd_rows(dx * g_ref[...], yn, r).astype(BF16)
            dg = jnp.sum(dx * yn, axis=0, keepdims=True)

            @pl.when(i == 0)
            def _():
                dg_ref[...] = dg
                loss_ref[...] = part

            @pl.when(i > 0)
            def _():
                dg_ref[...] += dg
                loss_ref[...] += part

    rows = pl.BlockSpec((TM, D_MODEL), lambda i, j: (i, 0))
    tile = pl.BlockSpec((TM, tn), lambda i, j: (i, j))
    wide = jax.ShapeDtypeStruct((s, D_FF), BF16)
    return pl.pallas_call(
        body, name=name, grid=(s // TM, nj),
        in_specs=[rows,
                  pl.BlockSpec((8, D_MODEL), lambda i, j: (jnp.maximum(i * hb - 1, 0), 0)),
                  pl.BlockSpec((None, D_MODEL, tn), lambda i, j: (j // per, 0, j % per)),
                  pl.BlockSpec((None, D_MODEL, tn), lambda i, j: (NDEV // 2 + j // per, 0, j % per)),
                  pl.BlockSpec((8, tn), lambda i, j: (0, j)),
                  pl.BlockSpec((8, tn), lambda i, j: (0, nj + j)),
                  pl.BlockSpec((1, tn), lambda i, j: (0, j)),
                  pl.BlockSpec((1, tn), lambda i, j: (0, nj + j)),
                  pl.BlockSpec((tn, D_MODEL), lambda i, j: (j, 0)),
                  rows, rows, pl.BlockSpec((1, D_MODEL), lambda i, j: (0, 0))],
        out_specs=[tile, tile, tile, tile, tile,
                   pl.BlockSpec((1, 128), lambda i, j: (0, 0)), rows, rows,
                   pl.BlockSpec((1, D_MODEL), lambda i, j: (0, 0))],
        out_shape=[wide, wide, wide, wide, wide,
                   jax.ShapeDtypeStruct((1, 128), F32), jax.ShapeDtypeStruct((s, D_MODEL), F32),
                   jax.ShapeDtypeStruct((s, D_MODEL), BF16), jax.ShapeDtypeStruct((1, D_MODEL), F32)],
        scratch_shapes=[pltpu.VMEM((TM, D_MODEL), F32)],
        compiler_params=_cp("arbitrary", "arbitrary"),
    )(h3, h3, w_up, w_up, cw, cw, cb, cb, w_down, x2, tgt, g_post)


def _ffn_bwd(dy3, w_down, hid_g, hid_u, conv_g, conv_u, cw, name):
    s = dy3.shape[0]
    n = s // TM
    tn = TN_FF
    nj = D_FF // tn
    hb = TM // 8
    last8 = s // 8 - 1

    def body(dy_ref, dyn_ref, wd_ref, hg_ref, hu_ref, cg_ref, cgn_ref, cu_ref, cun_ref, cwg_ref, cwu_ref,
             dhg_ref, dhu_ref, dcwg_ref, dcwu_ref, dcbg_ref, dcbu_ref):
        i = pl.program_id(1)
        first, last = i == 0, i == n - 1
        da = _dot(dy_ref[...], wd_ref[...], NT)
        dyn = dyn_ref[...]
        dyn = jnp.where(last, jnp.zeros_like(dyn), dyn)
        da_n = _dot(dyn, wd_ref[...], NT)
        c_g, c_u = cg_ref[...].astype(F32), cu_ref[...].astype(F32)
        g, dg = _gelu_and_grad(c_g)
        gn, dgn = _gelu_and_grad(cgn_ref[...].astype(F32))
        outs = ((da * c_u * dg, da_n * cun_ref[...].astype(F32) * dgn, hg_ref, cwg_ref, dhg_ref, dcwg_ref, dcbg_ref),
                (da * g, da_n * gn, hu_ref, cwu_ref, dhu_ref, dcwu_ref, dcbu_ref))
        row8 = lax.broadcasted_iota(jnp.int32, (8, tn), 0)
        for dc, dcn, h_ref, cw_ref, dh_ref, dcw_ref, dcb_ref in outs:
            u1, u2 = _shift_up(dc, dcn)
            dh_ref[...] = (cw_ref[2:3, :] * dc + cw_ref[1:2, :] * u1 + cw_ref[0:1, :] * u2).astype(BF16)
            hm = h_ref[...].astype(F32)
            dcb = jnp.sum(dc, axis=0, keepdims=True)
            dcw = jnp.where(row8 == 0, jnp.sum(hm * u2, axis=0, keepdims=True),
                            jnp.where(row8 == 1, jnp.sum(hm * u1, axis=0, keepdims=True),
                                      jnp.where(row8 == 2, jnp.sum(hm * dc, axis=0, keepdims=True), 0.0)))

            @pl.when(first)
            def _():
                dcw_ref[...] = dcw
                dcb_ref[...] = dcb

            @pl.when(i > 0)
            def _():
                dcw_ref[...] += dcw
                dcb_ref[...] += dcb

    next8 = lambda j, i: (jnp.minimum((i + 1) * hb, last8), j)
    blk = lambda j, i: (i, j)
    col = lambda j, i: (0, j)
    colu = lambda j, i: (0, nj + j)
    tile = pl.BlockSpec((TM, tn), blk)
    return pl.pallas_call(
        body, name=name, grid=(nj, n),
        in_specs=[pl.BlockSpec((TM, D_MODEL), lambda j, i: (i, 0)),
                  pl.BlockSpec((8, D_MODEL), lambda j, i: (jnp.minimum((i + 1) * hb, last8), 0)),
                  pl.BlockSpec((tn, D_MODEL), lambda j, i: (j, 0)),
                  tile, tile, tile, pl.BlockSpec((8, tn), next8), tile, pl.BlockSpec((8, tn), next8),
                  pl.BlockSpec((8, tn), col), pl.BlockSpec((8, tn), colu)],
        out_specs=[tile, tile, pl.BlockSpec((8, tn), col), pl.BlockSpec((8, tn), col),
                   pl.BlockSpec((1, tn), col), pl.BlockSpec((1, tn), col)],
        out_shape=[jax.ShapeDtypeStruct((s, D_FF), BF16), jax.ShapeDtypeStruct((s, D_FF), BF16),
                   jax.ShapeDtypeStruct((8, D_FF), F32), jax.ShapeDtypeStruct((8, D_FF), F32),
                   jax.ShapeDtypeStruct((1, D_FF), F32), jax.ShapeDtypeStruct((1, D_FF), F32)],
        compiler_params=_cp("parallel", "arbitrary"),
    )(dy3, dy3, w_down, hid_g, hid_u, conv_g, conv_g, conv_u, conv_u, cw, cw)


def _slot(p):
    return 4 * p[0] + 2 * p[1] + p[2]


def _all_gather(shards, name):
    n = len(shards)

    def body(*refs):
        ins, outs = refs[:n], refs[n:2 * n]
        send_sems, recv_sems, local_sems = refs[2 * n:]
        x, y, c = lax.axis_index("x"), lax.axis_index("y"), lax.axis_index("c")
        me, sibling = (x, y, c), (x, y, 1 - c)
        chips = [(1 - x, y), (x, 1 - y), (1 - x, 1 - y)]

        def copy(a, k, block, to, from_input=False):
            dst = outs[a].at[_slot(block)]
            return pltpu.make_async_remote_copy(
                src_ref=ins[a] if from_input else dst, dst_ref=dst,
                send_sem=send_sems.at[a, k], recv_sem=recv_sems.at[a, k],
                device_id=to, device_id_type=MESH)

        mine = [pltpu.make_async_copy(ins[a], outs[a].at[_slot(me)], local_sems.at[a]) for a in range(n)]
        for cp in mine:
            cp.start()
        first = []
        for a in range(n):
            first.append(copy(a, 0, me, sibling, True))
            first += [copy(a, 1 + j, me, (*chip, c), True) for j, chip in enumerate(chips)]
        for cp in first:
            cp.start()
        passed = []
        for j, chip in enumerate(chips):
            for a in range(n):
                copy(a, 1 + j, (*chip, c), me).wait_recv()
                fwd = copy(a, 4 + j, (*chip, c), sibling)
                fwd.start()
                passed.append(fwd)
        for a in range(n):
            copy(a, 0, sibling, me).wait_recv()
            for j, chip in enumerate(chips):
                copy(a, 4 + j, (*chip, 1 - c), me).wait_recv()
        for cp in first + passed:
            cp.wait_send()
        for cp in mine:
            cp.wait()

    any_spec = pl.BlockSpec(memory_space=pl.ANY)
    return pl.pallas_call(
        body, name=name,
        in_specs=[any_spec] * n, out_specs=[any_spec] * n,
        out_shape=[jax.ShapeDtypeStruct((NDEV,) + s.shape, s.dtype) for s in shards],
        scratch_shapes=[pltpu.SemaphoreType.DMA((n, 7)), pltpu.SemaphoreType.DMA((n, 7)),
                        pltpu.SemaphoreType.DMA((n,))],
    )(*shards)


def _peer_list(x, y, c):
    return [(1 - x if m & 4 else x, 1 - y if m & 2 else y, 1 - c if m & 1 else c) for m in range(1, NDEV)]


def _exchange_copies(src_refs, land_refs, send_sems, recv_sems, gather):
    x, y, c = lax.axis_index("x"), lax.axis_index("y"), lax.axis_index("c")
    me = (x, y, c)
    copies = []
    for m, peer in enumerate(_peer_list(x, y, c)):
        for a in range(len(src_refs)):
            copies.append(pltpu.make_async_remote_copy(
                src_ref=src_refs[a] if gather else src_refs[a].at[_slot(peer)], dst_ref=land_refs[a].at[_slot(me)],
                send_sem=send_sems.at[a * (NDEV - 1) + m], recv_sem=recv_sems.at[a * (NDEV - 1) + m],
                device_id=peer, device_id_type=MESH))
    return copies


def _exchange_start(srcs, lands, after, gather, name):
    n = len(srcs)
    hbm = pl.BlockSpec(memory_space=pltpu.HBM)

    def body(*refs):
        for cp in _exchange_copies(refs[:n], refs[n:2 * n], refs[2 * n + 1], refs[2 * n + 2], gather):
            cp.start()
        token = refs[-1]
        token[...] = jnp.zeros_like(token)

    outs = pl.pallas_call(
        body, name=name,
        out_shape=(pltpu.SemaphoreType.DMA((n * (NDEV - 1),)), pltpu.SemaphoreType.DMA((n * (NDEV - 1),)),
                   *[pltpu.HBM(a.shape, a.dtype) for a in list(srcs) + list(lands)],
                   jax.ShapeDtypeStruct((8, 128), F32)),
        in_specs=[hbm] * (2 * n) + [pl.BlockSpec(memory_space=pl.ANY)],
        out_specs=(pl.BlockSpec(memory_space=pltpu.SEMAPHORE), pl.BlockSpec(memory_space=pltpu.SEMAPHORE),
                   *[hbm] * (2 * n), pl.BlockSpec(memory_space=pltpu.VMEM)),
        input_output_aliases={i: 2 + i for i in range(2 * n)},
        compiler_params=pltpu.CompilerParams(has_side_effects=pltpu.SideEffectType.DATAFLOW_SIDE_EFFECTING),
    )(*[pltpu.with_memory_space_constraint(a, pltpu.HBM) for a in list(srcs) + list(lands)], after)
    return outs[0], outs[1], outs[2:2 + n], outs[2 + n:2 + 2 * n], outs[-1]


def _exchange_wait(send_sems, recv_sems, srcs, lands, after, gather, name):
    n = len(srcs)
    hbm = pl.BlockSpec(memory_space=pltpu.HBM)

    def body(*refs):
        for cp in _exchange_copies(refs[:n], refs[n:2 * n], refs[2 * n], refs[2 * n + 1], gather):
            cp.wait_send()
            cp.wait_recv()

    outs = pl.pallas_call(
        body, name=name,
        out_shape=tuple(pltpu.HBM(a.shape, a.dtype) for a in list(srcs) + list(lands)),
        in_specs=[hbm] * (2 * n) + [pl.BlockSpec(memory_space=pltpu.SEMAPHORE)] * 2 + [pl.BlockSpec(memory_space=pl.ANY)],
        out_specs=tuple([hbm] * (2 * n)),
        input_output_aliases={i: i for i in range(2 * n)},
        compiler_params=pltpu.CompilerParams(has_side_effects=pltpu.SideEffectType.DATAFLOW_SIDE_EFFECTING),
    )(*srcs, *lands, send_sems, recv_sems, after)
    return outs[n:]


def _own_slot(block):
    me = 4 * lax.axis_index("x") + 2 * lax.axis_index("y") + lax.axis_index("c")
    return lax.dynamic_update_slice(lax.empty((NDEV,) + block.shape, block.dtype), block[None], (me, 0, 0))


def _adam_update(p_ref, w_ref, m_ref, v_ref, g_ref, d_ref, mo_ref, vo_ref):
    bc1 = 1.0 - ADAM_B1 ** ADAM_STEP
    bc2 = 1.0 - ADAM_B2 ** ADAM_STEP
    g = p_ref[0].astype(F32)
    for d in range(1, NDEV):
        g = g + p_ref[d].astype(F32)
    g_ref[...] = g
    mn = ADAM_B1 * m_ref[...] + (1.0 - ADAM_B1) * g
    vn = ADAM_B2 * v_ref[...] + (1.0 - ADAM_B2) * (g * g)
    mo_ref[...] = mn
    vo_ref[...] = vn
    d_ref[...] = -ADAM_LR * ((mn / bc1) / (jnp.sqrt(vn / bc2) + ADAM_EPS) + ADAM_WD * w_ref[...])


def _adamw_small(parts, ws, ms, vs, name):
    n = len(ws)

    def body(*refs):
        ins, outs = refs[:4 * n], refs[4 * n:]
        for k in range(n):
            _adam_update(ins[k], ins[n + k], ins[2 * n + k], ins[3 * n + k], *outs[4 * k:4 * k + 4])

    whole = pl.BlockSpec(memory_space=pltpu.VMEM)
    res = pl.pallas_call(
        body, name=name, in_specs=[whole] * (4 * n), out_specs=[whole] * (4 * n),
        out_shape=[jax.ShapeDtypeStruct(a.shape, F32) for a in ws for _ in range(4)],
    )(*parts, *ws, *ms, *vs)
    return [res[4 * k:4 * k + 4] for k in range(n)]


def _adamw(parts, w, m, v, name):
    r, c = w.shape
    tr = r if r * c <= 160 * 1024 else max(8, (160 * 1024 // c) // 8 * 8)
    while r % tr:
        tr -= 8
    body = functools.partial(_adam_update)
    spec = pl.BlockSpec((tr, c), lambda i: (i, 0))
    return pl.pallas_call(
        body, name=name, grid=(r // tr,),
        in_specs=[pl.BlockSpec((NDEV, tr, c), lambda i: (0, i, 0)), spec, spec, spec],
        out_specs=[spec] * 4, out_shape=[jax.ShapeDtypeStruct((r, c), F32)] * 4,
        compiler_params=_cp("parallel"),
    )(parts, w, m, v)


def _local_step(x, mem, tgt, gains, b_forget, w_pool, pool_scale, conv_b, w_in,
                mix_weights, ffn_weights, send_in_grad, send_mix_grads, send_ffn_grads):
    w_f = w_in[:, F_COL:]
    b_pad = jnp.pad(b_forget, ((0, 0), (0, 128 - FOX_HEADS)))
    wbd = jnp.zeros((D_POOL, D_POOL), F32)
    for g in range(4):
        wbd = wbd.at[64 * g:64 * g + 64, 64 * g:64 * g + 64].set(w_pool[g])
    wbd = wbd.astype(BF16)
    scale = pool_scale.reshape(1, D_POOL)

    h1 = _norm_fwd(x, gains["mix_pre"], "norm_mix_pre")
    proj = _mm(h1, w_in, "nn", BF16, 512, D_IN_PAD, 1024, "proj_in")
    fraw = _mm(h1, w_f, "nn", F32, 1024, 128, 1024, "proj_gate")
    flog, cum = _gate_cumsum(fraw, b_pad, "gate_cumsum")
    aq, ak = _fox_operands(cum, "fox_operands")
    ycat, aqb = _fox_fwd(proj, aq, ak, "fox_fwd")
    ycat = _pool_fwd(proj, wbd, scale, ycat, "pool_fwd")
    w_mix, w_xq, w_xo, w_xkv = mix_weights(ycat)
    y1, x1, h2 = _mm_rows(ycat, w_mix, "nn", 1024, "mix_out", [x], [gains["mix_post"], gains["xa_pre"]],
                          [F32, F32, BF16], _epi_resid)
    q2 = _mm(h2, w_xq, "nn", BF16, 1024, 1024, 1024, "xa_q")
    mem_n = _norm_fwd(mem, gains["mem"], "norm_mem")
    kv = _mm(mem_n, w_xkv, "nn", BF16, mem.shape[0], 256, 1024, "xa_kv", b_cols=256)
    o2 = _xattn_fwd(q2, kv, "xattn_fwd")
    y2, x2, h3 = _mm_rows(o2, w_xo, "nn", 1024, "xa_out", [x1], [gains["xa_post"], gains["ffn_pre"]],
                          [F32, F32, BF16], _epi_resid)
    w_up, w_down, cw = ffn_weights(h3)
    hid_g, hid_u, conv_g, conv_u, act, loss, dx3, dy3, dg_ffn_post = _ffn_fwd(
        h3, w_up, cw, conv_b, w_down, x2, tgt, gains["ffn_post"], "ffn_fwd")

    dhid_g, dhid_u, dcw_g, dcw_u, dcb_g, dcb_u = _ffn_bwd(dy3, w_down, hid_g, hid_u, conv_g, conv_u, cw, "ffn_bwd")
    d_w_down = _mm(act, dy3, "tn", BF16, 1024, 1024, 1024, "dw_down")
    d_w_up = _mm(h3, [dhid_g, dhid_u], "tn", BF16, 1024, 1024, 1024, "dw_up", out_cols=1024)
    sent = send_ffn_grads(d_w_up, d_w_down, jnp.concatenate([dcw_g, dcw_u], axis=1))
    dh3 = _mm([dhid_g, dhid_u], w_up, "nt", F32, 1024, 1024, 1024, "dh_ffn", b_cols=1024, after=sent)
    dx2, dg_ffn_pre, dy2, dg_xa_post = _norm_bwd(dh3, x2, dx3, gains["ffn_pre"], "norm_bwd_ffn",
                                                 prev=(y2, gains["xa_post"]))
    do2 = _mm(dy2, w_xo, "nt", BF16, 1024, 1024, 1024, "d_xa_out")
    d_w_xo = _mm(o2, dy2, "tn", BF16, 1024, 1024, 1024, "dw_xo")
    dq2, dkv = _xattn_bwd(q2, kv, do2, "xattn_bwd")
    dkv = dkv.astype(BF16)
    dx1, dg_xa_pre, dy1, dg_mix_post = _mm_rows(
        dq2, w_xq, "nt", 1024, "dh_xa", [x1, dx2, y1], [gains["xa_pre"], gains["mix_post"]],
        [F32, "sum", BF16, "sum"], _epi_norm_bwd)
    d_w_xq = _mm(h2, dq2, "tn", BF16, 1024, 1024, 1024, "dw_xq")
    dmem_n = _mm(dkv, w_xkv, "nt", F32, mem.shape[0], 1024, 256, "d_mem", b_cols=256)
    d_w_xkv = _mm(mem_n, dkv, "tn", BF16, 1024, 256, mem.shape[0], "dw_xkv", out_cols=256)
    _, dg_mem = _norm_bwd(dmem_n, mem, jnp.zeros_like(mem), gains["mem"], "norm_bwd_mem")
    dycat = _mm(dy1, w_mix, "nt", BF16, 1024, 1024, 1024, "d_mix_out")
    d_w_mix = _mm(ycat, dy1, "tn", BF16, 1024, 1024, 1024, "dw_mix")
    sent_mix = send_mix_grads(d_w_mix, d_w_xq, d_w_xo, d_w_xkv)
    ad = _fox_do_operand(dycat, ycat, sent_mix, "fox_do_operand")
    dq, dk, dv, qaux, kaux = _fox_bwd(proj, dycat, aqb, ak, ad, "fox_bwd")
    du, d_wbd, d_scale = _pool_bwd(proj, dycat, wbd, scale, "pool_bwd")
    df, db_f = _gate_bwd(qaux, kaux, flog, "gate_bwd")
    dproj = jnp.concatenate([du, dq, dk, dv, df], axis=1)
    sent_in = send_in_grad(_mm(h1, dproj, "tn", BF16, 512, D_IN_PAD, 1024, "dw_in"))
    grad_x, dg_mix_pre = _mm_rows(dproj, w_in, "nt", D_IN_PAD, "dh_mix", [x, dx1], [gains["mix_pre"]],
                                  [F32, "sum"], _epi_norm_bwd, after=sent_in)

    small = dict(
        mix_pre=dg_mix_pre, mix_post=dg_mix_post, mem=dg_mem, xa_pre=dg_xa_pre, xa_post=dg_xa_post,
        ffn_pre=dg_ffn_pre, ffn_post=dg_ffn_post,
        conv_b=jnp.concatenate([dcb_g, dcb_u], axis=1),
        w_pool=jnp.concatenate([d_wbd[64 * g:64 * g + 64, 64 * g:64 * g + 64] for g in range(4)], axis=0),
        pool_scale=d_scale.reshape(4, 64),
        b_forget=db_f[:, :FOX_HEADS],
    )
    return loss, grad_x, small


SMALL_ORDER = ("mix_pre", "mix_post", "mem", "xa_pre", "xa_post", "ffn_pre", "ffn_post", "conv_b",
               "w_pool", "pool_scale", "b_forget")


def kernel(x, mem, norm_mix_pre, norm_mix_post, w_in, b_forget, w_pool, pool_scale, w_mix_out, norm_mem, norm_xa_pre, norm_xa_post, w_xq, w_xkv, w_xo, norm_ffn_pre, norm_ffn_post, w_up, conv_w, conv_b, w_down, loss_target, m_norm_mix_pre, m_norm_mix_post, m_w_in, m_b_forget, m_w_pool, m_pool_scale, m_w_mix_out, m_norm_mem, m_norm_xa_pre, m_norm_xa_post, m_w_xq, m_w_xkv, m_w_xo, m_norm_ffn_pre, m_norm_ffn_post, m_w_up, m_conv_w, m_conv_b, m_w_down, v_norm_mix_pre, v_norm_mix_post, v_w_in, v_b_forget, v_w_pool, v_pool_scale, v_w_mix_out, v_norm_mem, v_norm_xa_pre, v_norm_xa_post, v_w_xq, v_w_xkv, v_w_xo, v_norm_ffn_pre, v_norm_ffn_post, v_w_up, v_conv_w, v_conv_b, v_w_down):
    names = ("norm_mix_pre", "norm_mix_post", "w_in", "b_forget", "w_pool", "pool_scale", "w_mix_out", "norm_mem",
             "norm_xa_pre", "norm_xa_post", "w_xq", "w_xkv", "w_xo", "norm_ffn_pre", "norm_ffn_post", "w_up",
             "conv_w", "conv_b", "w_down")
    w = dict(zip(names, (norm_mix_pre, norm_mix_post, w_in, b_forget, w_pool, pool_scale, w_mix_out, norm_mem,
                         norm_xa_pre, norm_xa_post, w_xq, w_xkv, w_xo, norm_ffn_pre, norm_ffn_post, w_up,
                         conv_w, conv_b, w_down)))
    mo = dict(zip(names, (m_norm_mix_pre, m_norm_mix_post, m_w_in, m_b_forget, m_w_pool, m_pool_scale, m_w_mix_out,
                          m_norm_mem, m_norm_xa_pre, m_norm_xa_post, m_w_xq, m_w_xkv, m_w_xo, m_norm_ffn_pre,
                          m_norm_ffn_post, m_w_up, m_conv_w, m_conv_b, m_w_down)))
    vo = dict(zip(names, (v_norm_mix_pre, v_norm_mix_post, v_w_in, v_b_forget, v_w_pool, v_pool_scale, v_w_mix_out,
                          v_norm_mem, v_norm_xa_pre, v_norm_xa_post, v_w_xq, v_w_xkv, v_w_xo, v_norm_ffn_pre,
                          v_norm_ffn_post, v_w_up, v_conv_w, v_conv_b, v_w_down)))

    big_names = ("w_in", "w_mix_out", "w_xq", "w_xo", "w_xkv", "w_up", "w_down")
    shards = {k: w[k][0].astype(BF16) for k in big_names}
    shards["w_in"] = jnp.pad(shards["w_in"], ((0, 0), (0, D_IN_PAD - shards["w_in"].shape[1])))
    conv_w_sh = jnp.pad(conv_w[0, :, 0, :], ((0, 5), (0, 0)))
    (g_in,) = _all_gather([shards["w_in"]], "gather_w_in")
    mix_srcs = [shards[k] for k in ("w_mix_out", "w_xq", "w_xo", "w_xkv")]
    mix_flight = _exchange_start(mix_srcs, [_own_slot(a) for a in mix_srcs], g_in, True, "gather_mix_start")
    ffn_srcs = [shards["w_up"], shards["w_down"], conv_w_sh]
    ffn_flight = _exchange_start(ffn_srcs, [_own_slot(a) for a in ffn_srcs], mix_flight[4], True, "gather_ffn_start")
    my_slot = 4 * lax.axis_index("x") + 2 * lax.axis_index("y") + lax.axis_index("c")
    own_block = lambda a: _own_slot(lax.dynamic_index_in_dim(a, my_slot, 0, keepdims=False))
    by_rows = lambda a: a.reshape(NDEV, a.shape[0] // NDEV, a.shape[1])
    by_cols = lambda a: a.reshape(a.shape[0], NDEV, a.shape[1] // NDEV).transpose(1, 0, 2)
    grad_flight = {}

    def mix_weights(after):
        g_mix, g_xq, g_xo, g_xkv = _exchange_wait(*mix_flight[:4], after, True, "gather_mix_wait")
        return (g_mix.reshape(D_MODEL, D_MODEL), g_xq.reshape(D_MODEL, D_MODEL), g_xo.reshape(D_MODEL, D_MODEL), g_xkv)

    def ffn_weights(after):
        g_up, g_down, g_cw = _exchange_wait(*ffn_flight[:4], after, True, "gather_ffn_wait")
        return g_up, g_down.reshape(D_FF, D_MODEL), g_cw.transpose(1, 0, 2).reshape(8, 2 * D_FF)

    def send_ffn_grads(d_w_up, d_w_down, d_cw):
        srcs = [d_w_up, by_rows(d_w_down), by_cols(d_cw)]
        grad_flight["ffn"] = _exchange_start(srcs, [own_block(a) for a in srcs], d_w_up, False, "scatter_ffn_start")
        return grad_flight["ffn"][4]

    def send_mix_grads(d_w_mix, d_w_xq, d_w_xo, d_w_xkv):
        srcs = [by_rows(d_w_mix), by_rows(d_w_xq), by_rows(d_w_xo), d_w_xkv]
        grad_flight["mix"] = _exchange_start(srcs, [own_block(a) for a in srcs], d_w_mix, False, "scatter_mix_start")
        return grad_flight["mix"][4]

    def send_in_grad(d_w_in):
        srcs = [by_rows(d_w_in)]
        grad_flight["in"] = _exchange_start(srcs, [own_block(a) for a in srcs], d_w_in, False, "scatter_in_start")
        return grad_flight["in"][4]

    gains = dict(mix_pre=norm_mix_pre + ffn_flight[4][0, 0], mix_post=norm_mix_post, mem=norm_mem, xa_pre=norm_xa_pre,
                 xa_post=norm_xa_post, ffn_pre=norm_ffn_pre, ffn_post=norm_ffn_post)
    loss, grad_x, small = _local_step(
        x[0], mem[0], loss_target[0], gains, b_forget, w_pool[0], pool_scale[0], conv_b,
        g_in.reshape(D_MODEL, D_IN_PAD), mix_weights, ffn_weights, send_in_grad, send_mix_grads, send_ffn_grads)

    p_up, p_down, p_cw = _exchange_wait(*grad_flight["ffn"][:4], grad_x, False, "scatter_ffn_wait")
    p_mix, p_xq, p_xo, p_xkv = _exchange_wait(*grad_flight["mix"][:4], grad_x, False, "scatter_mix_wait")
    parts = dict(w_mix_out=p_mix, w_xq=p_xq, w_xo=p_xo, w_xkv=p_xkv, w_up=p_up, w_down=p_down)
    small_parts = _all_gather([small[k] for k in SMALL_ORDER], "gather_small_grads")

    res = {k: [a[None] for a in _adamw(p, w[k][0], mo[k][0], vo[k][0], "adamw_" + k)] for k, p in parts.items()}
    pad_cw = lambda a: jnp.pad(a[0, :, 0, :], ((0, 5), (0, 0)))
    res["conv_w"] = [a[:3][None, :, None, :] for a in
                     _adamw(p_cw, pad_cw(conv_w), pad_cw(m_conv_w), pad_cw(v_conv_w), "adamw_conv_w")]
    key_of = dict(mix_pre="norm_mix_pre", mix_post="norm_mix_post", mem="norm_mem", xa_pre="norm_xa_pre",
                  xa_post="norm_xa_post", ffn_pre="norm_ffn_pre", ffn_post="norm_ffn_post", conv_b="conv_b",
                  w_pool="w_pool", pool_scale="pool_scale", b_forget="b_forget")
    flat2d = lambda src: [src[key_of[k]].reshape(small[k].shape) for k in SMALL_ORDER]
    small_out = _adamw_small(small_parts, flat2d(w), flat2d(mo), flat2d(vo), "adamw_small")
    for k, four in zip(SMALL_ORDER, small_out):
        res[key_of[k]] = [a.reshape(w[key_of[k]].shape) for a in four]
    (p_in,) = _exchange_wait(*grad_flight["in"][:4], res["w_up"][1], False, "scatter_in_wait")
    res["w_in"] = [a[None] for a in _adamw(p_in[:, :, :w_in.shape[2]], w["w_in"][0], mo["w_in"][0], vo["w_in"][0],
                                           "adamw_w_in")]

    total = lax.psum(loss[0, 0], ("x", "y", "c"))
    outs = [total, grad_x[None]]
    for idx in range(4):
        outs += [res[k][idx] for k in names]
    return tuple(outs)
```

```python
import functools
import math

import jax
import jax.numpy as jnp
from jax import lax
from jax.experimental import pallas as pl
from jax.experimental.pallas import tpu as pltpu

F32 = jnp.float32
BF16 = jnp.bfloat16

NDEV = 8
D_MODEL = 1024
D_POOL = 256
D_FOX = 768
FOX_HEADS = 12
HEAD_PAIRS = FOX_HEADS // 2
XA_HEADS = 4
XA_DIM = 256
D_FF = 4096
D_IN_PAD = 2688
F_COL = 2560
POOL_HALO = 16
NORM_EPS = 1e-6
NEG = -1e30

ADAM_LR = 0.001
ADAM_B1 = 0.9
ADAM_B2 = 0.999
ADAM_EPS = 1e-08
ADAM_WD = 0.01
ADAM_STEP = 10

TM = 512
TQ = 512
TN_FF = 1024
VMEM_LIMIT = 56 * 1024 * 1024
MESH = pl.DeviceIdType.MESH


def _cp(*sem):
    return pltpu.CompilerParams(dimension_semantics=sem, vmem_limit_bytes=VMEM_LIMIT)


def _dot(a, b, dims):
    return lax.dot_general(a, b, (dims, ((), ())), preferred_element_type=F32)


NN = ((1,), (0,))
NT = ((1,), (1,))
TN = ((0,), (0,))


def _mm(a, b, mode, out_dtype, tm, tn, tk, name, b_cols=None, out_cols=None, after=None):
    a_list = list(a) if isinstance(a, (list, tuple)) else [a]
    b_list = list(b) if isinstance(b, (list, tuple)) else [b]
    assert len(a_list) == 1 or len(b_list) == 1
    if mode == "tn":
        K, M = a_list[0].shape
        assert len(a_list) == 1
        Ns = [x.shape[1] for x in b_list]
        N = sum(Ns)
        assert b_cols is None
    else:
        assert len(b_list) == 1
        M = a_list[0].shape[0]
        Ks = [x.shape[1] for x in a_list]
        K = sum(Ks)
        if b_cols is None:
            N = b_list[0].shape[0] if mode == "nt" else b_list[0].shape[1]
        else:
            N = b_list[0].shape[1] if mode == "nt" else NDEV * b_cols
    assert M % tm == 0 and N % tn == 0 and K % tk == 0, (name, M, N, K)
    grid = (M // tm, N // tn, K // tk)
    nk = grid[2]
    dims = {"nn": NN, "nt": NT, "tn": TN}[mode]

    in_specs = []
    if mode == "tn":
        in_specs.append(pl.BlockSpec((tk, tm), lambda i, j, k: (k, i)))
        if len(b_list) == 1:
            in_specs.append(pl.BlockSpec((tk, tn), lambda i, j, k: (k, j)))
        else:
            nj1 = Ns[0] // tn
            in_specs.append(pl.BlockSpec((tk, tn), lambda i, j, k: (k, jnp.minimum(j, nj1 - 1))))
            in_specs.append(pl.BlockSpec((tk, tn), lambda i, j, k: (k, jnp.maximum(j - nj1, 0))))
    else:
        if len(a_list) == 1:
            in_specs.append(pl.BlockSpec((tm, tk), lambda i, j, k: (i, k)))
        else:
            nk1 = Ks[0] // tk
            in_specs.append(pl.BlockSpec((tm, tk), lambda i, j, k: (i, jnp.minimum(k, nk1 - 1))))
            in_specs.append(pl.BlockSpec((tm, tk), lambda i, j, k: (i, jnp.maximum(k - nk1, 0))))
        if b_cols is None:
            if mode == "nn":
                in_specs.append(pl.BlockSpec((tk, tn), lambda i, j, k: (k, j)))
            else:
                in_specs.append(pl.BlockSpec((tn, tk), lambda i, j, k: (j, k)))
        else:
            if mode == "nn":
                per = b_cols // tn
                in_specs.append(pl.BlockSpec((None, tk, tn), lambda i, j, k: (j // per, k, j % per)))
            else:
                per = b_cols // tk
                in_specs.append(pl.BlockSpec((None, tn, tk), lambda i, j, k: (k // per, j, k % per)))
    if out_cols is None:
        out_spec = pl.BlockSpec((tm, tn), lambda i, j, k: (i, j))
        out_shape = jax.ShapeDtypeStruct((M, N), out_dtype)
    else:
        pero = out_cols // tn
        out_spec = pl.BlockSpec((None, tm, tn), lambda i, j, k: (j // pero, i, j % pero))
        out_shape = jax.ShapeDtypeStruct((NDEV, M, out_cols), out_dtype)

    two_a = len(a_list) == 2
    two_b = len(b_list) == 2
    extra = []
    if after is not None:
        in_specs.append(pl.BlockSpec(memory_space=pl.ANY))
        extra.append(after)

    def body(*refs):
        o_ref, acc_ref = refs[-2], refs[-1]
        j = pl.program_id(1)
        k = pl.program_id(2)

        @pl.when(k == 0)
        def _():
            acc_ref[...] = jnp.zeros_like(acc_ref)

        if two_a:
            a1, a2, b1 = refs[0], refs[1], refs[2]
            nk1_ = Ks[0] // tk

            @pl.when(k < nk1_)
            def _():
                acc_ref[...] += _dot(a1[...], b1[...], dims)

            @pl.when(k >= nk1_)
            def _():
                acc_ref[...] += _dot(a2[...], b1[...], dims)
        elif two_b:
            a1, b1, b2 = refs[0], refs[1], refs[2]
            nj1_ = Ns[0] // tn

            @pl.when(j < nj1_)
            def _():
                acc_ref[...] += _dot(a1[...], b1[...], dims)

            @pl.when(j >= nj1_)
            def _():
                acc_ref[...] += _dot(a1[...], b2[...], dims)
        else:
            acc_ref[...] += _dot(refs[0][...], refs[1][...], dims)

        @pl.when(k == nk - 1)
        def _():
            o_ref[...] = acc_ref[...].astype(o_ref.dtype)

    return pl.pallas_call(
        body, name=name, grid=grid, in_specs=in_specs, out_specs=out_spec, out_shape=out_shape,
        scratch_shapes=[pltpu.VMEM((tm, tn), F32)],
        compiler_params=_cp("parallel", "parallel", "arbitrary"),
    )(*a_list, *b_list, *extra)


def _rstd(x):
    return lax.rsqrt(jnp.mean(x * x, axis=-1, keepdims=True) + NORM_EPS)


def _norm_bwd_rows(dxn, xn, r):
    return r * (dxn - xn * jnp.mean(dxn * xn, axis=-1, keepdims=True))


def _row_spec(tm, d):
    return pl.BlockSpec((tm, d), lambda i: (i, 0))


def _vec_spec(d):
    return pl.BlockSpec((1, d), lambda i: (0, 0))


def _mm_rows(a, b, mode, tk, name, rows, vecs, outs, epilogue, b_cols=None, after=None):
    a_list = list(a) if isinstance(a, (list, tuple)) else [a]
    m = a_list[0].shape[0]
    ks = [x.shape[1] for x in a_list]
    n = D_MODEL
    nk = sum(ks) // tk
    dims = NN if mode == "nn" else NT
    if len(a_list) == 1:
        in_specs = [pl.BlockSpec((TM, tk), lambda i, k: (i, k))]
    else:
        nk1 = ks[0] // tk
        in_specs = [pl.BlockSpec((TM, tk), lambda i, k: (i, jnp.minimum(k, nk1 - 1))),
                    pl.BlockSpec((TM, tk), lambda i, k: (i, jnp.maximum(k - nk1, 0)))]
    if mode == "nn":
        in_specs.append(pl.BlockSpec((tk, n), lambda i, k: (k, 0)))
    elif b_cols is None:
        in_specs.append(pl.BlockSpec((n, tk), lambda i, k: (0, k)))
    else:
        per = b_cols // tk
        in_specs.append(pl.BlockSpec((None, n, tk), lambda i, k: (k // per, 0, k % per)))
    in_specs += [pl.BlockSpec((TM, n), lambda i, k: (i, 0))] * len(rows)
    in_specs += [pl.BlockSpec((1, n), lambda i, k: (0, 0))] * len(vecs)
    extra = []
    if after is not None:
        in_specs.append(pl.BlockSpec(memory_space=pl.ANY))
        extra.append(after)
    out_specs, out_shape = [], []
    for o in outs:
        if o == "sum":
            out_specs.append(pl.BlockSpec((1, n), lambda i, k: (0, 0)))
            out_shape.append(jax.ShapeDtypeStruct((1, n), F32))
        else:
            out_specs.append(pl.BlockSpec((TM, n), lambda i, k: (i, 0)))
            out_shape.append(jax.ShapeDtypeStruct((m, n), o))
    na, nr, nv = len(a_list), len(rows), len(vecs)

    def body(*refs):
        a_refs, b_ref = refs[:na], refs[na]
        row_refs = refs[na + 1:na + 1 + nr]
        vec_refs = refs[na + 1 + nr:na + 1 + nr + nv]
        out_refs = refs[len(refs) - 1 - len(outs):len(refs) - 1]
        acc_ref = refs[-1]
        i, k = pl.program_id(0), pl.program_id(1)

        @pl.when(k == 0)
        def _():
            acc_ref[...] = jnp.zeros_like(acc_ref)

        if na == 1:
            acc_ref[...] += _dot(a_refs[0][...], b_ref[...], dims)
        else:
            nk1_ = ks[0] // tk

            @pl.when(k < nk1_)
            def _():
                acc_ref[...] += _dot(a_refs[0][...], b_ref[...], dims)

            @pl.when(k >= nk1_)
            def _():
                acc_ref[...] += _dot(a_refs[1][...], b_ref[...], dims)

        @pl.when(k == nk - 1)
        def _():
            vals = epilogue(acc_ref[...], [r[...] for r in row_refs], [v[...] for v in vec_refs])
            for o, ref, val in zip(outs, out_refs, vals):
                if o == "sum":
                    @pl.when(i == 0)
                    def _():
                        ref[...] = val

                    @pl.when(i > 0)
                    def _():
                        ref[...] += val
                else:
                    ref[...] = val.astype(o)

    return pl.pallas_call(
        body, name=name, grid=(m // TM, nk), in_specs=in_specs, out_specs=out_specs, out_shape=out_shape,
        scratch_shapes=[pltpu.VMEM((TM, n), F32)],
        compiler_params=_cp("arbitrary", "arbitrary"),
    )(*a_list, b, *rows, *vecs, *extra)


def _epi_resid(y, rows, vecs):
    (x_in,), (g_post, g_next) = rows, vecs
    xo = x_in + y * _rstd(y) * g_post
    return y, xo, xo * _rstd(xo) * g_next


def _epi_norm_bwd(dh, rows, vecs):
    x, dx_res = rows[0], rows[1]
    r = _rstd(x)
    xn = x * r
    dx = dx_res + _norm_bwd_rows(dh * vecs[0], xn, r)
    res = [dx, jnp.sum(dh * xn, axis=0, keepdims=True)]
    if len(rows) == 3:
        y = rows[2]
        r2 = _rstd(y)
        yn = y * r2
        res += [_norm_bwd_rows(dx * vecs[1], yn, r2), jnp.sum(dx * yn, axis=0, keepdims=True)]
    return res


def _norm_fwd(x, g, name):
    s, d = x.shape
    tm = min(TM, s)

    def body(x_ref, g_ref, h_ref):
        xv = x_ref[...]
        h_ref[...] = (xv * _rstd(xv) * g_ref[...]).astype(BF16)

    return pl.pallas_call(
        body, name=name, grid=(s // tm,), in_specs=[_row_spec(tm, d), _vec_spec(d)],
        out_specs=_row_spec(tm, d), out_shape=jax.ShapeDtypeStruct((s, d), BF16),
        compiler_params=_cp("parallel"),
    )(x, g)


def _norm_bwd(dh, x, dx_res, g_pre, name, prev=None):
    s, d = x.shape
    tm = min(TM, s)
    has_prev = prev is not None

    def body(*refs):
        if has_prev:
            dh_ref, x_ref, dr_ref, g_ref, y_ref, gp_ref, dx_ref, dg_ref, dy_ref, dgp_ref = refs
        else:
            dh_ref, x_ref, dr_ref, g_ref, dx_ref, dg_ref = refs
        i = pl.program_id(0)
        xv = x_ref[...]
        r = _rstd(xv)
        xn = xv * r
        dhv = dh_ref[...].astype(F32)
        dx = dr_ref[...] + _norm_bwd_rows(dhv * g_ref[...], xn, r)
        dx_ref[...] = dx
        dg = jnp.sum(dhv * xn, axis=0, keepdims=True)

        @pl.when(i == 0)
        def _():
            dg_ref[...] = dg

        @pl.when(i > 0)
        def _():
            dg_ref[...] += dg

        if has_prev:
            yv = y_ref[...]
            r2 = _rstd(yv)
            yn = yv * r2
            dy_ref[...] = _norm_bwd_rows(dx * gp_ref[...], yn, r2).astype(BF16)
            dgp = jnp.sum(dx * yn, axis=0, keepdims=True)

            @pl.when(i == 0)
            def _():
                dgp_ref[...] = dgp

            @pl.when(i > 0)
            def _():
                dgp_ref[...] += dgp

    in_specs = [_row_spec(tm, d), _row_spec(tm, d), _row_spec(tm, d), _vec_spec(d)]
    out_specs = [_row_spec(tm, d), _vec_spec(d)]
    out_shape = [jax.ShapeDtypeStruct((s, d), F32), jax.ShapeDtypeStruct((1, d), F32)]
    args = [dh, x, dx_res, g_pre]
    if has_prev:
        in_specs += [_row_spec(tm, d), _vec_spec(d)]
        out_specs += [_row_spec(tm, d), _vec_spec(d)]
        out_shape += [jax.ShapeDtypeStruct((s, d), BF16), jax.ShapeDtypeStruct((1, d), F32)]
        args += list(prev)
    return pl.pallas_call(
        body, name=name, grid=(s // tm,), in_specs=in_specs, out_specs=out_specs, out_shape=out_shape,
        compiler_params=_cp("arbitrary"),
    )(*args)


def _split3(v):
    hi = v.astype(BF16)
    r1 = v - hi.astype(F32)
    mid = r1.astype(BF16)
    lo = (r1 - mid.astype(F32)).astype(BF16)
    return hi, mid, lo


def _tri_dot(tri, v):
    hi, mid, lo = _split3(v)
    return _dot(tri, hi, NN) + _dot(tri, mid, NN) + _dot(tri, lo, NN)


def _gate_cumsum(fraw, b_pad, name):
    s = fraw.shape[0]

    def body(f_ref, b_ref, flog_ref, cum_ref, carry_ref):
        i = pl.program_id(0)

        @pl.when(i == 0)
        def _():
            carry_ref[...] = jnp.zeros_like(carry_ref)

        flog = f_ref[...] + b_ref[...]
        flog_ref[...] = flog
        lf = jnp.minimum(flog, 0.0) - jnp.log(1.0 + jnp.exp(-jnp.abs(flog)))
        lane = lax.broadcasted_iota(jnp.int32, (1, 128), 1)
        lf = jnp.where(lane < FOX_HEADS, lf, 0.0)
        row = lax.broadcasted_iota(jnp.int32, (TM, TM), 0)
        col = lax.broadcasted_iota(jnp.int32, (TM, TM), 1)
        tri = (row >= col).astype(BF16)
        cum = _tri_dot(tri, lf) + carry_ref[...]
        cum_ref[...] = cum
        carry_ref[...] = cum[TM - 1:TM, :]

    return pl.pallas_call(
        body, name=name, grid=(s // TM,),
        in_specs=[_row_spec(TM, 128), _vec_spec(128)],
        out_specs=[_row_spec(TM, 128), _row_spec(TM, 128)],
        out_shape=[jax.ShapeDtypeStruct((s, 128), F32), jax.ShapeDtypeStruct((s, 128), F32)],
        scratch_shapes=[pltpu.VMEM((1, 128), F32)],
        compiler_params=_cp("arbitrary"),
    )(fraw, b_pad)


def _gate_bwd(qaux, kaux, flog, name):
    s = flog.shape[0]
    n = s // TM

    def body(qa_ref, ka_ref, fl_ref, dp_ref, db_ref, carry_ref):
        i = pl.program_id(0)

        @pl.when(i == 0)
        def _():
            carry_ref[...] = jnp.zeros_like(carry_ref)

        src = lax.broadcasted_iota(jnp.int32, (128, 128), 0)
        dst = lax.broadcasted_iota(jnp.int32, (128, 128), 1)
        dcum = jnp.zeros((TM, 128), F32)
        for p in range(HEAD_PAIRS):
            for ref, l0, l1, sign in ((qa_ref, 64, 0, 1.0), (ka_ref, 67, 3, -1.0)):
                hit = jnp.logical_or(jnp.logical_and(src == l0, dst == 2 * p),
                                     jnp.logical_and(src == l1, dst == 2 * p + 1))
                sel = jnp.where(hit, sign, 0.0).astype(BF16)
                for piece in _split3(ref[p]):
                    dcum = dcum + _dot(piece, sel, NN)
        row = lax.broadcasted_iota(jnp.int32, (TM, TM), 0)
        col = lax.broadcasted_iota(jnp.int32, (TM, TM), 1)
        tri = (row <= col).astype(BF16)
        dlf = _tri_dot(tri, dcum) + carry_ref[...]
        carry_ref[...] = dlf[0:1, :]
        lane = lax.broadcasted_iota(jnp.int32, (1, 128), 1)
        df = jnp.where(lane < FOX_HEADS, dlf / (1.0 + jnp.exp(fl_ref[...])), 0.0)
        dp_ref[...] = df.astype(BF16)
        db = jnp.sum(df, axis=0, keepdims=True)

        @pl.when(i == 0)
        def _():
            db_ref[...] = db

        @pl.when(i > 0)
        def _():
            db_ref[...] += db

    rev = lambda i: (n - 1 - i, 0)
    return pl.pallas_call(
        body, name=name, grid=(n,),
        in_specs=[pl.BlockSpec((HEAD_PAIRS, TM, 128), lambda i: (0, n - 1 - i, 0)),
                  pl.BlockSpec((HEAD_PAIRS, TM, 128), lambda i: (0, n - 1 - i, 0)), pl.BlockSpec((TM, 128), rev)],
        out_specs=[pl.BlockSpec((TM, 128), rev), _vec_spec(128)],
        out_shape=[jax.ShapeDtypeStruct((s, 128), BF16), jax.ShapeDtypeStruct((1, 128), F32)],
        scratch_shapes=[pltpu.VMEM((1, 128), F32)],
        compiler_params=_cp("arbitrary"),
    )(qaux, kaux, flog)


def _pool_consts(i, rows):
    lane = lax.broadcasted_iota(jnp.int32, (rows, D_POOL), 1)
    t1 = lax.broadcasted_iota(jnp.int32, (rows, D_POOL), 0) + i * TM + 1
    win = jnp.where(lane < 64, 2, jnp.where(lane < 128, 4, jnp.where(lane < 192, 8, 16)))
    inv = 1.0 / jnp.minimum(t1, win).astype(F32)
    return lane, inv


def _by_group(lane, s2, s4, s8, s16):
    return jnp.where(lane < 64, s2, jnp.where(lane < 128, s4, jnp.where(lane < 192, s8, s16)))


def _pool_diff(i, u_ref, halo_ref):
    u = u_ref[...].astype(F32)
    halo = jnp.where(i > 0, halo_ref[...].astype(F32), 0.0)
    ext = jnp.concatenate([halo, u], axis=0)
    s2 = ext + pltpu.roll(ext, 1, 0)
    s4 = s2 + pltpu.roll(s2, 2, 0)
    s8 = s4 + pltpu.roll(s4, 4, 0)
    s16 = s8 + pltpu.roll(s8, 8, 0)
    lane, inv = _pool_consts(i, TM)
    sel = _by_group(lane, s2[POOL_HALO:], s4[POOL_HALO:], s8[POOL_HALO:], s16[POOL_HALO:])
    return sel * inv - u


def _pool_fwd(proj, wbd, scale, ycat, name):
    s = proj.shape[0]
    hb = TM // POOL_HALO

    def body(u_ref, halo_ref, w_ref, sc_ref, y_any, y_ref):
        del y_any
        i = pl.program_id(0)
        diff = _pool_diff(i, u_ref, halo_ref)
        mixed = _dot(diff.astype(BF16), w_ref[...], NN)
        y_ref[...] = (mixed * sc_ref[...]).astype(BF16)

    return pl.pallas_call(
        body, name=name, grid=(s // TM,),
        in_specs=[pl.BlockSpec((TM, D_POOL), lambda i: (i, 0)),
                  pl.BlockSpec((POOL_HALO, D_POOL), lambda i: (jnp.maximum(i * hb - 1, 0), 0)),
                  pl.BlockSpec((D_POOL, D_POOL), lambda i: (0, 0)), _vec_spec(D_POOL),
                  pl.BlockSpec(memory_space=pl.ANY)],
        out_specs=pl.BlockSpec((TM, D_POOL), lambda i: (i, 0)),
        out_shape=jax.ShapeDtypeStruct(ycat.shape, ycat.dtype),
        input_output_aliases={4: 0},
        compiler_params=_cp("parallel"),
    )(proj, proj, wbd, scale, ycat)


def _pool_bwd(proj, dycat, wbd, scale, name):
    s = proj.shape[0]
    n = s // TM
    hb = TM // POOL_HALO
    last_halo = s // POOL_HALO - 1

    def body(u_ref, halo_ref, dy_ref, dyp_ref, w_ref, sc_ref, dp_ref, dw_ref, dsc_ref):
        i = pl.program_id(0)
        diff = _pool_diff(i, u_ref, halo_ref)
        diff_b = diff.astype(BF16)
        mixed = _dot(diff_b, w_ref[...], NN)
        dy = dy_ref[...].astype(F32)
        dmix = (dy * sc_ref[...]).astype(BF16)
        dyp = jnp.where(i < n - 1, dyp_ref[...].astype(F32), 0.0)
        dmix_p = (dyp * sc_ref[...]).astype(BF16)
        dd = _dot(dmix, w_ref[...], NT)
        dd_p = _dot(dmix_p, w_ref[...], NT)
        lane, inv = _pool_consts(i, TM)
        _, inv_p = _pool_consts(i + 1, POOL_HALO)
        ext = jnp.concatenate([dd * inv, dd_p * inv_p], axis=0)
        rows = TM + POOL_HALO
        l2 = ext + pltpu.roll(ext, rows - 1, 0)
        l4 = l2 + pltpu.roll(l2, rows - 2, 0)
        l8 = l4 + pltpu.roll(l4, rows - 4, 0)
        l16 = l8 + pltpu.roll(l8, rows - 8, 0)
        du = _by_group(lane, l2[:TM], l4[:TM], l8[:TM], l16[:TM]) - dd
        dp_ref[...] = du.astype(BF16)
        dw = _dot(diff_b, dmix, TN)
        dsc = jnp.sum(dy * mixed, axis=0, keepdims=True)

        @pl.when(i == 0)
        def _():
            dw_ref[...] = dw
            dsc_ref[...] = dsc

        @pl.when(i > 0)
        def _():
            dw_ref[...] += dw
            dsc_ref[...] += dsc

    return pl.pallas_call(
        body, name=name, grid=(n,),
        in_specs=[pl.BlockSpec((TM, D_POOL), lambda i: (i, 0)),
                  pl.BlockSpec((POOL_HALO, D_POOL), lambda i: (jnp.maximum(i * hb - 1, 0), 0)),
                  pl.BlockSpec((TM, D_POOL), lambda i: (i, 0)),
                  pl.BlockSpec((POOL_HALO, D_POOL), lambda i: (jnp.minimum((i + 1) * hb, last_halo), 0)),
                  pl.BlockSpec((D_POOL, D_POOL), lambda i: (0, 0)), _vec_spec(D_POOL)],
        out_specs=[pl.BlockSpec((TM, D_POOL), lambda i: (i, 0)),
                   pl.BlockSpec((D_POOL, D_POOL), lambda i: (0, 0)), _vec_spec(D_POOL)],
        out_shape=[jax.ShapeDtypeStruct((s, D_POOL), BF16),
                   jax.ShapeDtypeStruct((D_POOL, D_POOL), F32), jax.ShapeDtypeStruct((1, D_POOL), F32)],
        compiler_params=_cp("arbitrary"),
    )(proj, proj, dycat, dycat, wbd, scale)


Q_BLK = D_POOL // 128
K_BLK = Q_BLK + D_FOX // 128
V_BLK = K_BLK + D_FOX // 128


def _operand_rows(v0, v1, ones_off):
    row = lax.broadcasted_iota(jnp.int32, (128, 1), 0)
    half = row & 63
    out = jnp.where(jnp.logical_and(half >= ones_off, half < ones_off + 3), 1.0, 0.0) + jnp.zeros_like(v0)
    for base, v in ((64, v0), (0, v1)):
        for j, piece in enumerate(_split3(v)):
            out = jnp.where(row == base + j, piece.astype(F32), out)
    return out


def _fox_operands(cum, name):
    s = cum.shape[0]
    width = HEAD_PAIRS * 128

    def body(c_ref, aq_ref, ak_ref):
        pieces = _split3(c_ref[...])
        row = lax.broadcasted_iota(jnp.int32, (128, width), 0)
        col = lax.broadcasted_iota(jnp.int32, (128, width), 1)
        base = (row >> 1) * 128 + (1 - (row & 1)) * 64
        half = lax.broadcasted_iota(jnp.int32, (1, width), 1) & 63
        for o_ref, off, sign, ones_off in ((aq_ref, 0, 1.0, 3), (ak_ref, 3, -1.0, 0)):
            out = jnp.where(jnp.logical_and(half >= ones_off, half < ones_off + 3), 1.0, 0.0)
            for j, piece in enumerate(pieces):
                sel = jnp.where(jnp.logical_and(col == base + off + j, row < FOX_HEADS), sign, 0.0).astype(BF16)
                out = out + _dot(piece, sel, NN)
            o_ref[...] = out.astype(BF16)

    return pl.pallas_call(
        body, name=name, grid=(s // TM,), in_specs=[_row_spec(TM, 128)],
        out_specs=[_row_spec(TM, width), _row_spec(TM, width)],
        out_shape=[jax.ShapeDtypeStruct((s, width), BF16)] * 2,
        compiler_params=_cp("parallel"),
    )(cum)


def _fox_do_operand(dycat, ycat, after, name):
    s = dycat.shape[0]

    rb = 1024
    nblk = D_FOX // D_POOL

    def body(*refs):
        do_refs, o_refs, ad_ref = refs[:nblk], refs[nblk:2 * nblk], refs[-1]
        src = lax.broadcasted_iota(jnp.int32, (D_POOL, D_POOL), 0)
        dst = lax.broadcasted_iota(jnp.int32, (D_POOL, D_POOL), 1)
        same_pair = (src >> 7) == (dst >> 7)
        s_in, d_in = src & 127, dst & 127
        hit = jnp.logical_and(same_pair, jnp.logical_or(
            jnp.logical_and(s_in < 64, jnp.logical_and(d_in >= 64, d_in < 67)), jnp.logical_and(s_in >= 64, d_in < 3)))
        sel = jnp.where(hit, 1.0, 0.0).astype(BF16)
        j = lax.broadcasted_iota(jnp.int32, (1, D_POOL), 1) & 63
        for b in range(nblk):
            dd = do_refs[b][...].astype(F32) * o_refs[b][...].astype(F32)
            dsum = jnp.zeros(dd.shape, F32)
            for piece in _split3(dd):
                dsum = dsum + _dot(piece, sel, NN)
            hi, mid, lo = _split3(-dsum)
            ad_ref[:, D_POOL * b:D_POOL * (b + 1)] = jnp.where(j == 0, hi, jnp.where(j == 1, mid, lo))

    blks = [pl.BlockSpec((rb, D_POOL), functools.partial(lambda i, b: (i, 1 + b), b=b)) for b in range(nblk)]
    return pl.pallas_call(
        body, name=name, grid=(s // rb,), in_specs=blks + blks + [pl.BlockSpec(memory_space=pl.ANY)],
        out_specs=pl.BlockSpec((rb, D_FOX), lambda i: (i, 0)),
        out_shape=jax.ShapeDtypeStruct((s, D_FOX), BF16),
        compiler_params=_cp("parallel"),
    )(*[dycat] * nblk, *[ycat] * nblk, after)


def _causal_pairs(nq, key_major):
    if key_major:
        pairs = [(q, k) for k in range(nq) for q in range(k, nq)]
    else:
        pairs = [(q, k) for q in range(nq) for k in range(q + 1)]
    return (jnp.asarray([p[0] for p in pairs], jnp.int32), jnp.asarray([p[1] for p in pairs], jnp.int32))


def _fox_fwd(proj, aq, ak, name):
    s = proj.shape[0]
    nq = s // TQ
    qi_arr, ki_arr = _causal_pairs(nq, key_major=False)

    def body(qi_ref, ki_ref, q_ref, k_ref, v_ref, aq_ref, ak_ref, o_ref, aqb_ref, m0_ref, m1_ref, acc_ref, aux_ref):
        t = pl.program_id(1)
        qi, ki = qi_ref[t], ki_ref[t]
        lane = lax.broadcasted_iota(jnp.int32, (1, 128), 1)
        masks = [lane < 64, lane >= 64]
        ones_v = jnp.where((lane & 63) == 8, 1.0, 0.0).astype(BF16)
        top = lax.broadcasted_iota(jnp.int32, (128, 1), 0) < 64
        m_ref = [m0_ref, m1_ref]

        @pl.when(ki == 0)
        def _():
            m0_ref[...] = jnp.full_like(m0_ref, NEG)
            m1_ref[...] = jnp.full_like(m1_ref, NEG)
            acc_ref[...] = jnp.zeros_like(acc_ref)
            aux_ref[...] = jnp.zeros_like(aux_ref)

        def step(diag):
            q2s = q_ref[...] * 0.125
            k2, v2, aq2, ak2 = k_ref[...], v_ref[...], aq_ref[...], ak_ref[...]
            pv, alpha = [], []
            for hh in range(2):
                qh = jnp.where(masks[hh], q2s, aq2)
                kh = jnp.where(masks[hh], k2, ak2)
                vh = jnp.where(masks[hh], v2, ones_v)
                sc = _dot(kh, qh, NT)
                if diag:
                    key = lax.broadcasted_iota(jnp.int32, sc.shape, 0)
                    qry = lax.broadcasted_iota(jnp.int32, sc.shape, 1)
                    sc = jnp.where(qry >= key, sc, NEG)
                m_prev = m_ref[hh][...]
                m_new = jnp.maximum(m_prev, jnp.max(sc, axis=0, keepdims=True))
                m_ref[hh][...] = m_new
                alpha.append(jnp.exp(m_prev - m_new))
                pv.append(_dot(vh, jnp.exp(sc - m_new).astype(BF16), TN))
            acc_ref[...] = acc_ref[...] * jnp.where(top, alpha[0], alpha[1]) + jnp.where(top, pv[0], pv[1])
            aux_ref[...] = aux_ref[...] * jnp.where(top, alpha[1], alpha[0]) + jnp.where(top, pv[1], pv[0])

        @pl.when(ki < qi)
        def _():
            step(False)

        @pl.when(ki == qi)
        def _():
            step(True)
            aux = aux_ref[...]
            l0, l1 = aux[72:73, :], aux[8:9, :]
            o_ref[...] = (acc_ref[...] * jnp.where(top, 1.0 / l0, 1.0 / l1)).T.astype(BF16)
            aqt = aq_ref[...].astype(F32).T
            cum0 = aqt[64:65, :] + aqt[65:66, :] + aqt[66:67, :]
            cum1 = aqt[0:1, :] + aqt[1:2, :] + aqt[2:3, :]
            aqb = _operand_rows(cum0 - (m0_ref[...] + jnp.log(l0)), cum1 - (m1_ref[...] + jnp.log(l1)), 3)
            aqb_ref[...] = aqb.T.astype(BF16)

    grid_spec = pltpu.PrefetchScalarGridSpec(
        num_scalar_prefetch=2, grid=(HEAD_PAIRS, int(qi_arr.shape[0])),
        in_specs=[pl.BlockSpec((TQ, 128), lambda p, t, qi, ki: (qi[t], Q_BLK + p)),
                  pl.BlockSpec((TQ, 128), lambda p, t, qi, ki: (ki[t], K_BLK + p)),
                  pl.BlockSpec((TQ, 128), lambda p, t, qi, ki: (ki[t], V_BLK + p)),
                  pl.BlockSpec((TQ, 128), lambda p, t, qi, ki: (qi[t], p)),
                  pl.BlockSpec((TQ, 128), lambda p, t, qi, ki: (ki[t], p))],
        out_specs=[pl.BlockSpec((TQ, 128), lambda p, t, qi, ki: (qi[t], Q_BLK + p)),
                   pl.BlockSpec((TQ, 128), lambda p, t, qi, ki: (qi[t], p))],
        scratch_shapes=[pltpu.VMEM((1, TQ), F32), pltpu.VMEM((1, TQ), F32),
                        pltpu.VMEM((128, TQ), F32), pltpu.VMEM((128, TQ), F32)])
    return pl.pallas_call(
        body, name=name, grid_spec=grid_spec,
        out_shape=[jax.ShapeDtypeStruct((s, D_MODEL), BF16), jax.ShapeDtypeStruct((s, HEAD_PAIRS * 128), BF16)],
        compiler_params=_cp("parallel", "arbitrary"),
    )(qi_arr, ki_arr, proj, proj, proj, aq, ak)


def _fox_bwd(proj, dycat, aqb, ak, ad, name):
    s = proj.shape[0]
    nq = s // TQ
    qi_arr, ki_arr = _causal_pairs(nq, key_major=True)

    def body(qi_ref, ki_ref, q_ref, k_ref, v_ref, do_ref, aq_ref, ak_ref, ad_ref,
             dq_ref, dk_ref, dv_ref, qaux_ref, kaux_ref, dq_acc, qaux_acc, dk_acc, dv_acc, kaux_acc):
        t = pl.program_id(1)
        qi, ki = qi_ref[t], ki_ref[t]
        lane = lax.broadcasted_iota(jnp.int32, (1, 128), 1)
        masks = [lane < 64, lane >= 64]
        ones_v = jnp.where((lane & 63) < 3, 1.0, 0.0).astype(BF16)
        top = lax.broadcasted_iota(jnp.int32, (128, 1), 0) < 64

        @pl.when(qi == ki)
        def _():
            dk_acc[...] = jnp.zeros_like(dk_acc)
            dv_acc[...] = jnp.zeros_like(dv_acc)
            kaux_acc[...] = jnp.zeros_like(kaux_acc)

        def step(diag):
            q2s = q_ref[...] * 0.125
            k2, v2, do2 = k_ref[...], v_ref[...], do_ref[...]
            aq2, ak2, ad2 = aq_ref[...], ak_ref[...], ad_ref[...]
            dq, dk, dv = [], [], []
            for hh in range(2):
                qh = jnp.where(masks[hh], q2s, aq2)
                kh = jnp.where(masks[hh], k2, ak2)
                doh = jnp.where(masks[hh], do2, ad2)
                vh = jnp.where(masks[hh], v2, ones_v)
                sc = _dot(kh, qh, NT)
                if diag:
                    key = lax.broadcasted_iota(jnp.int32, sc.shape, 0)
                    qry = lax.broadcasted_iota(jnp.int32, sc.shape, 1)
                    sc = jnp.where(qry >= key, sc, NEG)
                p = jnp.exp(sc)
                dsb = (p * _dot(vh, doh, NT)).astype(BF16)
                dv.append(_dot(p.astype(BF16), doh, NN))
                dk.append(_dot(dsb, qh, NN))
                dq.append(_dot(kh, dsb, TN))
            dk_acc[...] += jnp.where(masks[0], dk[0], dk[1])
            kaux_acc[...] += jnp.where(masks[0], dk[1], dk[0])
            dv_acc[...] += jnp.where(masks[0], dv[0], dv[1])
            dq_new = jnp.where(top, dq[0], dq[1])
            qaux_new = jnp.where(top, dq[1], dq[0])

            @pl.when(ki == 0)
            def _():
                dq_acc[qi] = dq_new
                qaux_acc[qi] = qaux_new

            @pl.when(ki > 0)
            def _():
                dq_acc[qi] += dq_new
                qaux_acc[qi] += qaux_new

        @pl.when(qi > ki)
        def _():
            step(False)

        @pl.when(qi == ki)
        def _():
            step(True)
            rows = pl.ds(pl.multiple_of(qi * TQ, TQ), TQ)
            dq_ref[rows, :] = (dq_acc[qi] * 0.125).T.astype(BF16)
            qaux_ref[rows, :] = qaux_acc[qi].T

        @pl.when(qi == nq - 1)
        def _():
            dk_ref[...] = dk_acc[...].astype(BF16)
            dv_ref[...] = dv_acc[...].astype(BF16)
            kaux_ref[...] = kaux_acc[...]

    grid_spec = pltpu.PrefetchScalarGridSpec(
        num_scalar_prefetch=2, grid=(HEAD_PAIRS, int(qi_arr.shape[0])),
        in_specs=[pl.BlockSpec((TQ, 128), lambda p, t, qi, ki: (qi[t], Q_BLK + p)),
                  pl.BlockSpec((TQ, 128), lambda p, t, qi, ki: (ki[t], K_BLK + p)),
                  pl.BlockSpec((TQ, 128), lambda p, t, qi, ki: (ki[t], V_BLK + p)),
                  pl.BlockSpec((TQ, 128), lambda p, t, qi, ki: (qi[t], Q_BLK + p)),
                  pl.BlockSpec((TQ, 128), lambda p, t, qi, ki: (qi[t], p)),
                  pl.BlockSpec((TQ, 128), lambda p, t, qi, ki: (ki[t], p)),
                  pl.BlockSpec((TQ, 128), lambda p, t, qi, ki: (qi[t], p))],
        out_specs=[pl.BlockSpec((s, 128), lambda p, t, qi, ki: (0, p)),
                   pl.BlockSpec((TQ, 128), lambda p, t, qi, ki: (ki[t], p)),
                   pl.BlockSpec((TQ, 128), lambda p, t, qi, ki: (ki[t], p)),
                   pl.BlockSpec((None, s, 128), lambda p, t, qi, ki: (p, 0, 0)),
                   pl.BlockSpec((None, TQ, 128), lambda p, t, qi, ki: (p, ki[t], 0))],
        scratch_shapes=[pltpu.VMEM((nq, 128, TQ), F32), pltpu.VMEM((nq, 128, TQ), F32),
                        pltpu.VMEM((TQ, 128), F32), pltpu.VMEM((TQ, 128), F32), pltpu.VMEM((TQ, 128), F32)])
    return pl.pallas_call(
        body, name=name, grid_spec=grid_spec,
        out_shape=[jax.ShapeDtypeStruct((s, D_FOX), BF16)] * 3 + [jax.ShapeDtypeStruct((HEAD_PAIRS, s, 128), F32)] * 2,
        compiler_params=_cp("arbitrary", "arbitrary"),
    )(qi_arr, ki_arr, proj, proj, proj, dycat, aqb, ak, ad)


XA_SCALE = XA_DIM ** -0.5


def _xattn_fwd(q2, kv, name):
    s = q2.shape[0]
    m = kv.shape[0]

    def body(q_ref, kv_ref, o_ref):
        for h in range(XA_HEADS):
            c0 = h * XA_DIM
            sc = _dot(q_ref[:, c0:c0 + XA_DIM], kv_ref[:, c0:c0 + XA_DIM], NT) * XA_SCALE
            e = jnp.exp(sc - jnp.max(sc, axis=1, keepdims=True))
            p = e / jnp.sum(e, axis=1, keepdims=True)
            o_ref[:, c0:c0 + XA_DIM] = _dot(p.astype(BF16), kv_ref[:, D_MODEL + c0:D_MODEL + c0 + XA_DIM], NN).astype(BF16)

    return pl.pallas_call(
        body, name=name, grid=(s // TM,),
        in_specs=[_row_spec(TM, D_MODEL), pl.BlockSpec((m, 2 * D_MODEL), lambda i: (0, 0))],
        out_specs=_row_spec(TM, D_MODEL), out_shape=jax.ShapeDtypeStruct((s, D_MODEL), BF16),
        compiler_params=_cp("parallel"),
    )(q2, kv)


def _xattn_bwd(q2, kv, do2, name):
    s = q2.shape[0]
    m = kv.shape[0]

    def body(q_ref, kv_ref, do_ref, dq_ref, dkv_ref):
        i = pl.program_id(0)

        @pl.when(i == 0)
        def _():
            dkv_ref[...] = jnp.zeros_like(dkv_ref)

        for h in range(XA_HEADS):
            c0 = h * XA_DIM
            v0 = D_MODEL + c0
            qh = q_ref[:, c0:c0 + XA_DIM]
            kh = kv_ref[:, c0:c0 + XA_DIM]
            doh = do_ref[:, c0:c0 + XA_DIM]
            sc = _dot(kh, qh, NT) * XA_SCALE
            e = jnp.exp(sc - jnp.max(sc, axis=0, keepdims=True))
            p = e / jnp.sum(e, axis=0, keepdims=True)
            dp = _dot(kv_ref[:, v0:v0 + XA_DIM], doh, NT)
            ds = p * (dp - jnp.sum(p * dp, axis=0, keepdims=True))
            dsb = (ds * XA_SCALE).astype(BF16)
            dq_ref[:, c0:c0 + XA_DIM] = _dot(kh, dsb, TN).T.astype(BF16)
            dkv_ref[:, c0:c0 + XA_DIM] += _dot(dsb, qh, NN)
            dkv_ref[:, v0:v0 + XA_DIM] += _dot(p.astype(BF16), doh, NN)

    return pl.pallas_call(
        body, name=name, grid=(s // TM,),
        in_specs=[_row_spec(TM, D_MODEL), pl.BlockSpec((m, 2 * D_MODEL), lambda i: (0, 0)), _row_spec(TM, D_MODEL)],
        out_specs=[_row_spec(TM, D_MODEL), pl.BlockSpec((m, 2 * D_MODEL), lambda i: (0, 0))],
        out_shape=[jax.ShapeDtypeStruct((s, D_MODEL), BF16), jax.ShapeDtypeStruct((m, 2 * D_MODEL), F32)],
        compiler_params=_cp("arbitrary"),
    )(q2, kv, do2)


GELU_C = math.sqrt(2.0 / math.pi)
GELU_A = 0.044715


def _gelu(x):
    return 0.5 * x * (1.0 + jnp.tanh(GELU_C * (x + GELU_A * x * x * x)))


def _gelu_and_grad(x):
    t = jnp.tanh(GELU_C * (x + GELU_A * x * x * x))
    g = 0.5 * x * (1.0 + t)
    dg = 0.5 * (1.0 + t) + 0.5 * x * (1.0 - t * t) * GELU_C * (1.0 + 3.0 * GELU_A * x * x)
    return g, dg


def _conv(h, s1, s2, w_ref, b_ref):
    return w_ref[0:1, :] * s2 + w_ref[1:2, :] * s1 + w_ref[2:3, :] * h + b_ref[...]


def _shift_down(main, prev8):
    row = lax.broadcasted_iota(jnp.int32, main.shape, 0)
    s1 = jnp.where(row == 0, prev8[7:8, :], pltpu.roll(main, 1, 0))
    s2 = jnp.where(row == 0, prev8[6:7, :], jnp.where(row == 1, prev8[7:8, :], pltpu.roll(main, 2, 0)))
    return s1, s2


def _shift_up(main, next8):
    n = main.shape[0]
    row = lax.broadcasted_iota(jnp.int32, main.shape, 0)
    u1 = jnp.where(row == n - 1, next8[0:1, :], pltpu.roll(main, n - 1, 0))
    u2 = jnp.where(row == n - 2, next8[0:1, :], jnp.where(row == n - 1, next8[1:2, :], pltpu.roll(main, n - 2, 0)))
    return u1, u2


def _ffn_fwd(h3, w_up, cw, cb, w_down, x2, tgt, g_post, name):
    s = h3.shape[0]
    tn = TN_FF
    nj = D_FF // tn
    per = D_MODEL // tn
    hb = TM // 8

    def body(h_ref, halo_ref, wg_ref, wu_ref, cwg_ref, cwu_ref, cbg_ref, cbu_ref, wd_ref, x_ref, t_ref, g_ref,
             hg_ref, hu_ref, cg_ref, cu_ref, a_ref, loss_ref, dx_ref, dy_ref, dg_ref, y_acc):
        i, j = pl.program_id(0), pl.program_id(1)
        h = h_ref[...]
        halo = halo_ref[...]
        halo = jnp.where(i > 0, halo, jnp.zeros_like(halo))
        conv = []
        for w_ref, cw_ref, cb_ref, hid_ref, c_ref in ((wg_ref, cwg_ref, cbg_ref, hg_ref, cg_ref),
                                                      (wu_ref, cwu_ref, cbu_ref, hu_ref, cu_ref)):
            hm = _dot(h, w_ref[...], NN)
            hid_ref[...] = hm.astype(BF16)
            s1, s2 = _shift_down(hm, _dot(halo, w_ref[...], NN))
            c = _conv(hm, s1, s2, cw_ref, cb_ref)
            c_ref[...] = c.astype(BF16)
            conv.append(c)
        a = (_gelu(conv[0]) * conv[1]).astype(BF16)
        a_ref[...] = a
        contrib = _dot(a, wd_ref[...], NN)

        @pl.when(j == 0)
        def _():
            y_acc[...] = contrib

        @pl.when(j > 0)
        def _():
            y_acc[...] += contrib

        @pl.when(j == nj - 1)
        def _():
            yv = y_acc[...]
            r = _rstd(yv)
            yn = yv * r
            e = x_ref[...] + yn * g_ref[...] - t_ref[...]
            part = 0.5 * jnp.sum(jnp.mean(e * e, axis=-1, keepdims=True), axis=0, keepdims=True)
            part = jnp.broadcast_to(part, (1, 128))
            dx = e * (1.0 / D_MODEL)
            dx_ref[...] = dx
            dy_ref[...] = _norm_bwd_rows(dx * g_ref[...], yn, r).astype(BF16)
            dg = jnp.sum(dx * yn, axis=0, keepdims=True)

            @pl.when(i == 0)
            def _():
                dg_ref[...] = dg
                loss_ref[...] = part

            @pl.when(i > 0)
            def _():
                dg_ref[...] += dg
                loss_ref[...] += part

    rows = pl.BlockSpec((TM, D_MODEL), lambda i, j: (i, 0))
    tile = pl.BlockSpec((TM, tn), lambda i, j: (i, j))
    wide = jax.ShapeDtypeStruct((s, D_FF), BF16)
    return pl.pallas_call(
        body, name=name, grid=(s // TM, nj),
        in_specs=[rows,
                  pl.BlockSpec((8, D_MODEL), lambda i, j: (jnp.maximum(i * hb - 1, 0), 0)),
                  pl.BlockSpec((None, D_MODEL, tn), lambda i, j: (j // per, 0, j % per)),
                  pl.BlockSpec((None, D_MODEL, tn), lambda i, j: (NDEV // 2 + j // per, 0, j % per)),
                  pl.BlockSpec((8, tn), lambda i, j: (0, j)),
                  pl.BlockSpec((8, tn), lambda i, j: (0, nj + j)),
                  pl.BlockSpec((1, tn), lambda i, j: (0, j)),
                  pl.BlockSpec((1, tn), lambda i, j: (0, nj + j)),
                  pl.BlockSpec((tn, D_MODEL), lambda i, j: (j, 0)),
                  rows, rows, pl.BlockSpec((1, D_MODEL), lambda i, j: (0, 0))],
        out_specs=[tile, tile, tile, tile, tile,
                   pl.BlockSpec((1, 128), lambda i, j: (0, 0)), rows, rows,
                   pl.BlockSpec((1, D_MODEL), lambda i, j: (0, 0))],
        out_shape=[wide, wide, wide, wide, wide,
                   jax.ShapeDtypeStruct((1, 128), F32), jax.ShapeDtypeStruct((s, D_MODEL), F32),
                   jax.ShapeDtypeStruct((s, D_MODEL), BF16), jax.ShapeDtypeStruct((1, D_MODEL), F32)],
        scratch_shapes=[pltpu.VMEM((TM, D_MODEL), F32)],
        compiler_params=_cp("arbitrary", "arbitrary"),
    )(h3, h3, w_up, w_up, cw, cw, cb, cb, w_down, x2, tgt, g_post)


def _ffn_bwd(dy3, w_down, hid_g, hid_u, conv_g, conv_u, cw, name):
    s = dy3.shape[0]
    n = s // TM
    tn = TN_FF
    nj = D_FF // tn
    hb = TM // 8
    last8 = s // 8 - 1

    def body(dy_ref, dyn_ref, wd_ref, hg_ref, hu_ref, cg_ref, cgn_ref, cu_ref, cun_ref, cwg_ref, cwu_ref,
             dhg_ref, dhu_ref, dcwg_ref, dcwu_ref, dcbg_ref, dcbu_ref):
        i = pl.program_id(1)
        first, last = i == 0, i == n - 1
        da = _dot(dy_ref[...], wd_ref[...], NT)
        dyn = dyn_ref[...]
        dyn = jnp.where(last, jnp.zeros_like(dyn), dyn)
        da_n = _dot(dyn, wd_ref[...], NT)
        c_g, c_u = cg_ref[...].astype(F32), cu_ref[...].astype(F32)
        g, dg = _gelu_and_grad(c_g)
        gn, dgn = _gelu_and_grad(cgn_ref[...].astype(F32))
        outs = ((da * c_u * dg, da_n * cun_ref[...].astype(F32) * dgn, hg_ref, cwg_ref, dhg_ref, dcwg_ref, dcbg_ref),
                (da * g, da_n * gn, hu_ref, cwu_ref, dhu_ref, dcwu_ref, dcbu_ref))
        row8 = lax.broadcasted_iota(jnp.int32, (8, tn), 0)
        for dc, dcn, h_ref, cw_ref, dh_ref, dcw_ref, dcb_ref in outs:
            u1, u2 = _shift_up(dc, dcn)
            dh_ref[...] = (cw_ref[2:3, :] * dc + cw_ref[1:2, :] * u1 + cw_ref[0:1, :] * u2).astype(BF16)
            hm = h_ref[...].astype(F32)
            dcb = jnp.sum(dc, axis=0, keepdims=True)
            dcw = jnp.where(row8 == 0, jnp.sum(hm * u2, axis=0, keepdims=True),
                            jnp.where(row8 == 1, jnp.sum(hm * u1, axis=0, keepdims=True),
                                      jnp.where(row8 == 2, jnp.sum(hm * dc, axis=0, keepdims=True), 0.0)))

            @pl.when(first)
            def _():
                dcw_ref[...] = dcw
                dcb_ref[...] = dcb

            @pl.when(i > 0)
            def _():
                dcw_ref[...] += dcw
                dcb_ref[...] += dcb

    next8 = lambda j, i: (jnp.minimum((i + 1) * hb, last8), j)
    blk = lambda j, i: (i, j)
    col = lambda j, i: (0, j)
    colu = lambda j, i: (0, nj + j)
    tile = pl.BlockSpec((TM, tn), blk)
    return pl.pallas_call(
        body, name=name, grid=(nj, n),
        in_specs=[pl.BlockSpec((TM, D_MODEL), lambda j, i: (i, 0)),
                  pl.BlockSpec((8, D_MODEL), lambda j, i: (jnp.minimum((i + 1) * hb, last8), 0)),
                  pl.BlockSpec((tn, D_MODEL), lambda j, i: (j, 0)),
                  tile, tile, tile, pl.BlockSpec((8, tn), next8), tile, pl.BlockSpec((8, tn), next8),
                  pl.BlockSpec((8, tn), col), pl.BlockSpec((8, tn), colu)],
        out_specs=[tile, tile, pl.BlockSpec((8, tn), col), pl.BlockSpec((8, tn), col),
                   pl.BlockSpec((1, tn), col), pl.BlockSpec((1, tn), col)],
        out_shape=[jax.ShapeDtypeStruct((s, D_FF), BF16), jax.ShapeDtypeStruct((s, D_FF), BF16),
                   jax.ShapeDtypeStruct((8, D_FF), F32), jax.ShapeDtypeStruct((8, D_FF), F32),
                   jax.ShapeDtypeStruct((1, D_FF), F32), jax.ShapeDtypeStruct((1, D_FF), F32)],
        compiler_params=_cp("parallel", "arbitrary"),
    )(dy3, dy3, w_down, hid_g, hid_u, conv_g, conv_g, conv_u, conv_u, cw, cw)


def _slot(p):
    return 4 * p[0] + 2 * p[1] + p[2]


def _all_gather(shards, name):
    n = len(shards)

    def body(*refs):
        ins, outs = refs[:n], refs[n:2 * n]
        send_sems, recv_sems, local_sems = refs[2 * n:]
        x, y, c = lax.axis_index("x"), lax.axis_index("y"), lax.axis_index("c")
        me, sibling = (x, y, c), (x, y, 1 - c)
        chips = [(1 - x, y), (x, 1 - y), (1 - x, 1 - y)]

        def copy(a, k, block, to, from_input=False):
            dst = outs[a].at[_slot(block)]
            return pltpu.make_async_remote_copy(
                src_ref=ins[a] if from_input else dst, dst_ref=dst,
                send_sem=send_sems.at[a, k], recv_sem=recv_sems.at[a, k],
                device_id=to, device_id_type=MESH)

        mine = [pltpu.make_async_copy(ins[a], outs[a].at[_slot(me)], local_sems.at[a]) for a in range(n)]
        for cp in mine:
            cp.start()
        first = []
        for a in range(n):
            first.append(copy(a, 0, me, sibling, True))
            first += [copy(a, 1 + j, me, (*chip, c), True) for j, chip in enumerate(chips)]
        for cp in first:
            cp.start()
        passed = []
        for j, chip in enumerate(chips):
            for a in range(n):
                copy(a, 1 + j, (*chip, c), me).wait_recv()
                fwd = copy(a, 4 + j, (*chip, c), sibling)
                fwd.start()
                passed.append(fwd)
        for a in range(n):
            copy(a, 0, sibling, me).wait_recv()
            for j, chip in enumerate(chips):
                copy(a, 4 + j, (*chip, 1 - c), me).wait_recv()
        for cp in first + passed:
            cp.wait_send()
        for cp in mine:
            cp.wait()

    any_spec = pl.BlockSpec(memory_space=pl.ANY)
    return pl.pallas_call(
        body, name=name,
        in_specs=[any_spec] * n, out_specs=[any_spec] * n,
        out_shape=[jax.ShapeDtypeStruct((NDEV,) + s.shape, s.dtype) for s in shards],
        scratch_shapes=[pltpu.SemaphoreType.DMA((n, 7)), pltpu.SemaphoreType.DMA((n, 7)),
                        pltpu.SemaphoreType.DMA((n,))],
    )(*shards)


def _peer_list(x, y, c):
    return [(1 - x if m & 4 else x, 1 - y if m & 2 else y, 1 - c if m & 1 else c) for m in range(1, NDEV)]


def _exchange_copies(src_refs, land_refs, send_sems, recv_sems, gather):
    x, y, c = lax.axis_index("x"), lax.axis_index("y"), lax.axis_index("c")
    me = (x, y, c)
    copies = []
    for m, peer in enumerate(_peer_list(x, y, c)):
        for a in range(len(src_refs)):
            copies.append(pltpu.make_async_remote_copy(
                src_ref=src_refs[a] if gather else src_refs[a].at[_slot(peer)], dst_ref=land_refs[a].at[_slot(me)],
                send_sem=send_sems.at[a * (NDEV - 1) + m], recv_sem=recv_sems.at[a * (NDEV - 1) + m],
                device_id=peer, device_id_type=MESH))
    return copies


def _all_gather_small(shards, name):
    n = len(shards)

    def body(*refs):
        ins, outs = refs[:n], refs[n:2 * n]
        send_sems, recv_sems, local_sems = refs[2 * n:]
        me = (lax.axis_index("x"), lax.axis_index("y"), lax.axis_index("c"))
        mine = [pltpu.make_async_copy(ins[a], outs[a].at[_slot(me)], local_sems.at[a]) for a in range(n)]
        copies = _exchange_copies(ins, outs, send_sems, recv_sems, True)
        for cp in mine + copies:
            cp.start()
        for cp in copies + mine:
            cp.wait()

    any_spec = pl.BlockSpec(memory_space=pl.ANY)
    return pl.pallas_call(
        body, name=name,
        in_specs=[any_spec] * n, out_specs=[any_spec] * n,
        out_shape=[jax.ShapeDtypeStruct((NDEV,) + s.shape, s.dtype) for s in shards],
        scratch_shapes=[pltpu.SemaphoreType.DMA((n * (NDEV - 1),)), pltpu.SemaphoreType.DMA((n * (NDEV - 1),)),
                        pltpu.SemaphoreType.DMA((n,))],
    )(*shards)


def _exchange_start(srcs, lands, after, gather, name):
    n = len(srcs)
    hbm = pl.BlockSpec(memory_space=pltpu.HBM)

    def body(*refs):
        for cp in _exchange_copies(refs[:n], refs[n:2 * n], refs[2 * n + 1], refs[2 * n + 2], gather):
            cp.start()
        token = refs[-1]
        token[...] = jnp.zeros_like(token)

    outs = pl.pallas_call(
        body, name=name,
        out_shape=(pltpu.SemaphoreType.DMA((n * (NDEV - 1),)), pltpu.SemaphoreType.DMA((n * (NDEV - 1),)),
                   *[pltpu.HBM(a.shape, a.dtype) for a in list(srcs) + list(lands)],
                   jax.ShapeDtypeStruct((8, 128), F32)),
        in_specs=[hbm] * (2 * n) + [pl.BlockSpec(memory_space=pl.ANY)],
        out_specs=(pl.BlockSpec(memory_space=pltpu.SEMAPHORE), pl.BlockSpec(memory_space=pltpu.SEMAPHORE),
                   *[hbm] * (2 * n), pl.BlockSpec(memory_space=pltpu.VMEM)),
        input_output_aliases={i: 2 + i for i in range(2 * n)},
        compiler_params=pltpu.CompilerParams(has_side_effects=pltpu.SideEffectType.DATAFLOW_SIDE_EFFECTING),
    )(*[pltpu.with_memory_space_constraint(a, pltpu.HBM) for a in list(srcs) + list(lands)], after)
    return outs[0], outs[1], outs[2:2 + n], outs[2 + n:2 + 2 * n], outs[-1]


def _exchange_wait(send_sems, recv_sems, srcs, lands, after, gather, name):
    n = len(srcs)
    hbm = pl.BlockSpec(memory_space=pltpu.HBM)

    def body(*refs):
        for cp in _exchange_copies(refs[:n], refs[n:2 * n], refs[2 * n], refs[2 * n + 1], gather):
            cp.wait_send()
            cp.wait_recv()

    outs = pl.pallas_call(
        body, name=name,
        out_shape=tuple(pltpu.HBM(a.shape, a.dtype) for a in list(srcs) + list(lands)),
        in_specs=[hbm] * (2 * n) + [pl.BlockSpec(memory_space=pltpu.SEMAPHORE)] * 2 + [pl.BlockSpec(memory_space=pl.ANY)],
        out_specs=tuple([hbm] * (2 * n)),
        input_output_aliases={i: i for i in range(2 * n)},
        compiler_params=pltpu.CompilerParams(has_side_effects=pltpu.SideEffectType.DATAFLOW_SIDE_EFFECTING),
    )(*srcs, *lands, send_sems, recv_sems, after)
    return outs[n:]


def _own_slot(block):
    me = 4 * lax.axis_index("x") + 2 * lax.axis_index("y") + lax.axis_index("c")
    return lax.dynamic_update_slice(lax.empty((NDEV,) + block.shape, block.dtype), block[None], (me, 0, 0))


def _adam_update(p_ref, w_ref, m_ref, v_ref, g_ref, d_ref, mo_ref, vo_ref):
    bc1 = 1.0 - ADAM_B1 ** ADAM_STEP
    bc2 = 1.0 - ADAM_B2 ** ADAM_STEP
    g = p_ref[0].astype(F32)
    for d in range(1, NDEV):
        g = g + p_ref[d].astype(F32)
    g_ref[...] = g
    mn = ADAM_B1 * m_ref[...] + (1.0 - ADAM_B1) * g
    vn = ADAM_B2 * v_ref[...] + (1.0 - ADAM_B2) * (g * g)
    mo_ref[...] = mn
    vo_ref[...] = vn
    d_ref[...] = -ADAM_LR * ((mn / bc1) / (jnp.sqrt(vn / bc2) + ADAM_EPS) + ADAM_WD * w_ref[...])


def _adamw_small(parts, ws, ms, vs, name):
    n = len(ws)

    def body(*refs):
        ins, outs = refs[:4 * n], refs[4 * n:]
        for k in range(n):
            _adam_update(ins[k], ins[n + k], ins[2 * n + k], ins[3 * n + k], *outs[4 * k:4 * k + 4])

    whole = pl.BlockSpec(memory_space=pltpu.VMEM)
    res = pl.pallas_call(
        body, name=name, in_specs=[whole] * (4 * n), out_specs=[whole] * (4 * n),
        out_shape=[jax.ShapeDtypeStruct(a.shape, F32) for a in ws for _ in range(4)],
    )(*parts, *ws, *ms, *vs)
    return [res[4 * k:4 * k + 4] for k in range(n)]


def _adamw(parts, w, m, v, name):
    r, c = w.shape
    tr = r if r * c <= 160 * 1024 else max(8, (160 * 1024 // c) // 8 * 8)
    while r % tr:
        tr -= 8
    body = functools.partial(_adam_update)
    spec = pl.BlockSpec((tr, c), lambda i: (i, 0))
    return pl.pallas_call(
        body, name=name, grid=(r // tr,),
        in_specs=[pl.BlockSpec((NDEV, tr, c), lambda i: (0, i, 0)), spec, spec, spec],
        out_specs=[spec] * 4, out_shape=[jax.ShapeDtypeStruct((r, c), F32)] * 4,
        compiler_params=_cp("parallel"),
    )(parts, w, m, v)


def _local_step(x, mem, tgt, gains, b_forget, w_pool, pool_scale, conv_b, w_in,
                mix_weights, ffn_weights, send_in_grad, send_mix_grads, send_ffn_grads):
    w_f = w_in[:, F_COL:]
    b_pad = jnp.pad(b_forget, ((0, 0), (0, 128 - FOX_HEADS)))
    wbd = jnp.zeros((D_POOL, D_POOL), F32)
    for g in range(4):
        wbd = wbd.at[64 * g:64 * g + 64, 64 * g:64 * g + 64].set(w_pool[g])
    wbd = wbd.astype(BF16)
    scale = pool_scale.reshape(1, D_POOL)

    h1 = _norm_fwd(x, gains["mix_pre"], "norm_mix_pre")
    proj = _mm(h1, w_in, "nn", BF16, 512, D_IN_PAD, 1024, "proj_in")
    fraw = _mm(h1, w_f, "nn", F32, 1024, 128, 1024, "proj_gate")
    flog, cum = _gate_cumsum(fraw, b_pad, "gate_cumsum")
    aq, ak = _fox_operands(cum, "fox_operands")
    ycat, aqb = _fox_fwd(proj, aq, ak, "fox_fwd")
    ycat = _pool_fwd(proj, wbd, scale, ycat, "pool_fwd")
    w_mix, w_xq, w_xo, w_xkv = mix_weights(ycat)
    y1, x1, h2 = _mm_rows(ycat, w_mix, "nn", 1024, "mix_out", [x], [gains["mix_post"], gains["xa_pre"]],
                          [F32, F32, BF16], _epi_resid)
    q2 = _mm(h2, w_xq, "nn", BF16, 1024, 1024, 1024, "xa_q")
    mem_n = _norm_fwd(mem, gains["mem"], "norm_mem")
    kv = _mm(mem_n, w_xkv, "nn", BF16, mem.shape[0], 256, 1024, "xa_kv", b_cols=256)
    o2 = _xattn_fwd(q2, kv, "xattn_fwd")
    y2, x2, h3 = _mm_rows(o2, w_xo, "nn", 1024, "xa_out", [x1], [gains["xa_post"], gains["ffn_pre"]],
                          [F32, F32, BF16], _epi_resid)
    w_up, w_down, cw = ffn_weights(h3)
    hid_g, hid_u, conv_g, conv_u, act, loss, dx3, dy3, dg_ffn_post = _ffn_fwd(
        h3, w_up, cw, conv_b, w_down, x2, tgt, gains["ffn_post"], "ffn_fwd")

    dhid_g, dhid_u, dcw_g, dcw_u, dcb_g, dcb_u = _ffn_bwd(dy3, w_down, hid_g, hid_u, conv_g, conv_u, cw, "ffn_bwd")
    d_w_down = _mm(act, dy3, "tn", BF16, 1024, 1024, 2048, "dw_down")
    d_w_up = _mm(h3, [dhid_g, dhid_u], "tn", BF16, 1024, 1024, 2048, "dw_up", out_cols=1024)
    sent = send_ffn_grads(d_w_up, d_w_down, jnp.concatenate([dcw_g, dcw_u], axis=1))
    dh3 = _mm([dhid_g, dhid_u], w_up, "nt", F32, 1024, 1024, 1024, "dh_ffn", b_cols=1024, after=sent)
    dx2, dg_ffn_pre, dy2, dg_xa_post = _norm_bwd(dh3, x2, dx3, gains["ffn_pre"], "norm_bwd_ffn",
                                                 prev=(y2, gains["xa_post"]))
    do2 = _mm(dy2, w_xo, "nt", BF16, 1024, 1024, 1024, "d_xa_out")
    d_w_xo = _mm(o2, dy2, "tn", BF16, 1024, 1024, 1024, "dw_xo")
    dq2, dkv = _xattn_bwd(q2, kv, do2, "xattn_bwd")
    dkv = dkv.astype(BF16)
    dx1, dg_xa_pre, dy1, dg_mix_post = _mm_rows(
        dq2, w_xq, "nt", 1024, "dh_xa", [x1, dx2, y1], [gains["xa_pre"], gains["mix_post"]],
        [F32, "sum", BF16, "sum"], _epi_norm_bwd)
    d_w_xq = _mm(h2, dq2, "tn", BF16, 1024, 1024, 1024, "dw_xq")
    dmem_n = _mm(dkv, w_xkv, "nt", F32, mem.shape[0], 1024, 256, "d_mem", b_cols=256)
    d_w_xkv = _mm(mem_n, dkv, "tn", BF16, 1024, 256, mem.shape[0], "dw_xkv", out_cols=256)
    _, dg_mem = _norm_bwd(dmem_n, mem, jnp.zeros_like(mem), gains["mem"], "norm_bwd_mem")
    dycat = _mm(dy1, w_mix, "nt", BF16, 1024, 1024, 1024, "d_mix_out")
    d_w_mix = _mm(ycat, dy1, "tn", BF16, 1024, 1024, 1024, "dw_mix")
    sent_mix = send_mix_grads(d_w_mix, d_w_xq, d_w_xo, d_w_xkv)
    ad = _fox_do_operand(dycat, ycat, sent_mix, "fox_do_operand")
    dq, dk, dv, qaux, kaux = _fox_bwd(proj, dycat, aqb, ak, ad, "fox_bwd")
    du, d_wbd, d_scale = _pool_bwd(proj, dycat, wbd, scale, "pool_bwd")
    df, db_f = _gate_bwd(qaux, kaux, flog, "gate_bwd")
    dproj = jnp.concatenate([du, dq, dk, dv, df], axis=1)
    sent_in = send_in_grad(_mm(h1, dproj, "tn", BF16, 512, D_IN_PAD, 1024, "dw_in"))
    grad_x, dg_mix_pre = _mm_rows(dproj, w_in, "nt", D_IN_PAD, "dh_mix", [x, dx1], [gains["mix_pre"]],
                                  [F32, "sum"], _epi_norm_bwd, after=sent_in)

    small = dict(
        mix_pre=dg_mix_pre, mix_post=dg_mix_post, mem=dg_mem, xa_pre=dg_xa_pre, xa_post=dg_xa_post,
        ffn_pre=dg_ffn_pre, ffn_post=dg_ffn_post,
        conv_b=jnp.concatenate([dcb_g, dcb_u], axis=1),
        w_pool=jnp.concatenate([d_wbd[64 * g:64 * g + 64, 64 * g:64 * g + 64] for g in range(4)], axis=0),
        pool_scale=d_scale.reshape(4, 64),
        b_forget=db_f[:, :FOX_HEADS],
    )
    return loss, grad_x, small


SMALL_ORDER = ("mix_pre", "mix_post", "mem", "xa_pre", "xa_post", "ffn_pre", "ffn_post", "conv_b",
               "w_pool", "pool_scale", "b_forget")


def kernel(x, mem, norm_mix_pre, norm_mix_post, w_in, b_forget, w_pool, pool_scale, w_mix_out, norm_mem, norm_xa_pre, norm_xa_post, w_xq, w_xkv, w_xo, norm_ffn_pre, norm_ffn_post, w_up, conv_w, conv_b, w_down, loss_target, m_norm_mix_pre, m_norm_mix_post, m_w_in, m_b_forget, m_w_pool, m_pool_scale, m_w_mix_out, m_norm_mem, m_norm_xa_pre, m_norm_xa_post, m_w_xq, m_w_xkv, m_w_xo, m_norm_ffn_pre, m_norm_ffn_post, m_w_up, m_conv_w, m_conv_b, m_w_down, v_norm_mix_pre, v_norm_mix_post, v_w_in, v_b_forget, v_w_pool, v_pool_scale, v_w_mix_out, v_norm_mem, v_norm_xa_pre, v_norm_xa_post, v_w_xq, v_w_xkv, v_w_xo, v_norm_ffn_pre, v_norm_ffn_post, v_w_up, v_conv_w, v_conv_b, v_w_down):
    names = ("norm_mix_pre", "norm_mix_post", "w_in", "b_forget", "w_pool", "pool_scale", "w_mix_out", "norm_mem",
             "norm_xa_pre", "norm_xa_post", "w_xq", "w_xkv", "w_xo", "norm_ffn_pre", "norm_ffn_post", "w_up",
             "conv_w", "conv_b", "w_down")
    w = dict(zip(names, (norm_mix_pre, norm_mix_post, w_in, b_forget, w_pool, pool_scale, w_mix_out, norm_mem,
                         norm_xa_pre, norm_xa_post, w_xq, w_xkv, w_xo, norm_ffn_pre, norm_ffn_post, w_up,
                         conv_w, conv_b, w_down)))
    mo = dict(zip(names, (m_norm_mix_pre, m_norm_mix_post, m_w_in, m_b_forget, m_w_pool, m_pool_scale, m_w_mix_out,
                          m_norm_mem, m_norm_xa_pre, m_norm_xa_post, m_w_xq, m_w_xkv, m_w_xo, m_norm_ffn_pre,
                          m_norm_ffn_post, m_w_up, m_conv_w, m_conv_b, m_w_down)))
    vo = dict(zip(names, (v_norm_mix_pre, v_norm_mix_post, v_w_in, v_b_forget, v_w_pool, v_pool_scale, v_w_mix_out,
                          v_norm_mem, v_norm_xa_pre, v_norm_xa_post, v_w_xq, v_w_xkv, v_w_xo, v_norm_ffn_pre,
                          v_norm_ffn_post, v_w_up, v_conv_w, v_conv_b, v_w_down)))

    big_names = ("w_in", "w_mix_out", "w_xq", "w_xo", "w_xkv", "w_up", "w_down")
    shards = {k: w[k][0].astype(BF16) for k in big_names}
    shards["w_in"] = jnp.pad(shards["w_in"], ((0, 0), (0, D_IN_PAD - shards["w_in"].shape[1])))
    conv_w_sh = jnp.pad(conv_w[0, :, 0, :], ((0, 5), (0, 0)))
    (g_in,) = _all_gather([shards["w_in"]], "gather_w_in")
    mix_srcs = [shards[k] for k in ("w_mix_out", "w_xq", "w_xo", "w_xkv")]
    mix_flight = _exchange_start(mix_srcs, [_own_slot(a) for a in mix_srcs], g_in, True, "gather_mix_start")
    ffn_srcs = [shards["w_up"], shards["w_down"], conv_w_sh]
    ffn_flight = _exchange_start(ffn_srcs, [_own_slot(a) for a in ffn_srcs], mix_flight[4], True, "gather_ffn_start")
    my_slot = 4 * lax.axis_index("x") + 2 * lax.axis_index("y") + lax.axis_index("c")
    own_block = lambda a: _own_slot(lax.dynamic_index_in_dim(a, my_slot, 0, keepdims=False))
    by_rows = lambda a: a.reshape(NDEV, a.shape[0] // NDEV, a.shape[1])
    by_cols = lambda a: a.reshape(a.shape[0], NDEV, a.shape[1] // NDEV).transpose(1, 0, 2)
    grad_flight = {}

    def mix_weights(after):
        g_mix, g_xq, g_xo, g_xkv = _exchange_wait(*mix_flight[:4], after, True, "gather_mix_wait")
        return (g_mix.reshape(D_MODEL, D_MODEL), g_xq.reshape(D_MODEL, D_MODEL), g_xo.reshape(D_MODEL, D_MODEL), g_xkv)

    def ffn_weights(after):
        g_up, g_down, g_cw = _exchange_wait(*ffn_flight[:4], after, True, "gather_ffn_wait")
        return g_up, g_down.reshape(D_FF, D_MODEL), g_cw.transpose(1, 0, 2).reshape(8, 2 * D_FF)

    def send_ffn_grads(d_w_up, d_w_down, d_cw):
        srcs = [d_w_up, by_rows(d_w_down), by_cols(d_cw)]
        grad_flight["ffn"] = _exchange_start(srcs, [own_block(a) for a in srcs], ffn_flight[4], False, "scatter_ffn_start")
        return grad_flight["ffn"][4]

    def send_mix_grads(d_w_mix, d_w_xq, d_w_xo, d_w_xkv):
        srcs = [by_rows(d_w_mix), by_rows(d_w_xq), by_rows(d_w_xo), d_w_xkv]
        grad_flight["mix"] = _exchange_start(srcs, [own_block(a) for a in srcs], ffn_flight[4], False, "scatter_mix_start")
        return grad_flight["mix"][4]

    def send_in_grad(d_w_in):
        srcs = [by_rows(d_w_in)]
        grad_flight["in"] = _exchange_start(srcs, [own_block(a) for a in srcs], ffn_flight[4], False, "scatter_in_start")
        return grad_flight["in"][4]

    gains = dict(mix_pre=norm_mix_pre + ffn_flight[4][0, 0], mix_post=norm_mix_post, mem=norm_mem, xa_pre=norm_xa_pre,
                 xa_post=norm_xa_post, ffn_pre=norm_ffn_pre, ffn_post=norm_ffn_post)
    loss, grad_x, small = _local_step(
        x[0], mem[0], loss_target[0], gains, b_forget, w_pool[0], pool_scale[0], conv_b,
        g_in.reshape(D_MODEL, D_IN_PAD), mix_weights, ffn_weights, send_in_grad, send_mix_grads, send_ffn_grads)

    p_up, p_down, p_cw = _exchange_wait(*grad_flight["ffn"][:4], grad_x, False, "scatter_ffn_wait")
    p_mix, p_xq, p_xo, p_xkv = _exchange_wait(*grad_flight["mix"][:4], grad_x, False, "scatter_mix_wait")
    parts = dict(w_mix_out=p_mix, w_xq=p_xq, w_xo=p_xo, w_xkv=p_xkv, w_up=p_up, w_down=p_down)
    *small_parts, loss_parts = _all_gather_small([small[k] for k in SMALL_ORDER] + [loss], "gather_small_grads")

    res = {k: [a[None] for a in _adamw(p, w[k][0], mo[k][0], vo[k][0], "adamw_" + k)] for k, p in parts.items()}
    pad_cw = lambda a: jnp.pad(a[0, :, 0, :], ((0, 5), (0, 0)))
    res["conv_w"] = [a[:3][None, :, None, :] for a in
                     _adamw(p_cw, pad_cw(conv_w), pad_cw(m_conv_w), pad_cw(v_conv_w), "adamw_conv_w")]
    key_of = dict(mix_pre="norm_mix_pre", mix_post="norm_mix_post", mem="norm_mem", xa_pre="norm_xa_pre",
                  xa_post="norm_xa_post", ffn_pre="norm_ffn_pre", ffn_post="norm_ffn_post", conv_b="conv_b",
                  w_pool="w_pool", pool_scale="pool_scale", b_forget="b_forget")
    flat2d = lambda src: [src[key_of[k]].reshape(small[k].shape) for k in SMALL_ORDER]
    small_out = _adamw_small(small_parts, flat2d(w), flat2d(mo), flat2d(vo), "adamw_small")
    for k, four in zip(SMALL_ORDER, small_out):
        res[key_of[k]] = [a.reshape(w[key_of[k]].shape) for a in four]
    (p_in,) = _exchange_wait(*grad_flight["in"][:4], res["w_up"][1], False, "scatter_in_wait")
    res["w_in"] = [a[None] for a in _adamw(p_in[:, :, :w_in.shape[2]], w["w_in"][0], mo["w_in"][0], vo["w_in"][0],
                                           "adamw_w_in")]

    outs = [jnp.sum(loss_parts[:, 0, 0]), grad_x[None]]
    for idx in range(4):
        outs += [res[k][idx] for k in names]
    return tuple(outs)
```

```python
import functools
import math

import jax
import jax.numpy as jnp
from jax import lax
from jax.experimental import pallas as pl
from jax.experimental.pallas import tpu as pltpu

F32 = jnp.float32
BF16 = jnp.bfloat16

NDEV = 8
D_MODEL = 1024
D_POOL = 256
D_FOX = 768
FOX_HEADS = 12
HEAD_PAIRS = FOX_HEADS // 2
XA_HEADS = 4
XA_DIM = 256
D_FF = 4096
D_IN_PAD = 2688
F_COL = 2560
POOL_HALO = 16
NORM_EPS = 1e-6
NEG = -1e30

ADAM_LR = 0.001
ADAM_B1 = 0.9
ADAM_B2 = 0.999
ADAM_EPS = 1e-08
ADAM_WD = 0.01
ADAM_STEP = 10

TM = 512
TQ = 512
TN_FF = 1024
VMEM_LIMIT = 56 * 1024 * 1024
MESH = pl.DeviceIdType.MESH


def _cp(*sem):
    return pltpu.CompilerParams(dimension_semantics=sem, vmem_limit_bytes=VMEM_LIMIT)


def _dot(a, b, dims):
    return lax.dot_general(a, b, (dims, ((), ())), preferred_element_type=F32)


NN = ((1,), (0,))
NT = ((1,), (1,))
TN = ((0,), (0,))


def _mm(a, b, mode, out_dtype, tm, tn, tk, name, b_cols=None, out_cols=None, after=None):
    a_list = list(a) if isinstance(a, (list, tuple)) else [a]
    b_list = list(b) if isinstance(b, (list, tuple)) else [b]
    assert len(a_list) == 1 or len(b_list) == 1
    if mode == "tn":
        K, M = a_list[0].shape
        assert len(a_list) == 1
        Ns = [x.shape[1] for x in b_list]
        N = sum(Ns)
        assert b_cols is None
    else:
        assert len(b_list) == 1
        M = a_list[0].shape[0]
        Ks = [x.shape[1] for x in a_list]
        K = sum(Ks)
        if b_cols is None:
            N = b_list[0].shape[0] if mode == "nt" else b_list[0].shape[1]
        else:
            N = b_list[0].shape[1] if mode == "nt" else NDEV * b_cols
    assert M % tm == 0 and N % tn == 0 and K % tk == 0, (name, M, N, K)
    grid = (M // tm, N // tn, K // tk)
    nk = grid[2]
    dims = {"nn": NN, "nt": NT, "tn": TN}[mode]

    in_specs = []
    if mode == "tn":
        in_specs.append(pl.BlockSpec((tk, tm), lambda i, j, k: (k, i)))
        if len(b_list) == 1:
            in_specs.append(pl.BlockSpec((tk, tn), lambda i, j, k: (k, j)))
        else:
            nj1 = Ns[0] // tn
            in_specs.append(pl.BlockSpec((tk, tn), lambda i, j, k: (k, jnp.minimum(j, nj1 - 1))))
            in_specs.append(pl.BlockSpec((tk, tn), lambda i, j, k: (k, jnp.maximum(j - nj1, 0))))
    else:
        if len(a_list) == 1:
            in_specs.append(pl.BlockSpec((tm, tk), lambda i, j, k: (i, k)))
        else:
            nk1 = Ks[0] // tk
            in_specs.append(pl.BlockSpec((tm, tk), lambda i, j, k: (i, jnp.minimum(k, nk1 - 1))))
            in_specs.append(pl.BlockSpec((tm, tk), lambda i, j, k: (i, jnp.maximum(k - nk1, 0))))
        if b_cols is None:
            if mode == "nn":
                in_specs.append(pl.BlockSpec((tk, tn), lambda i, j, k: (k, j)))
            else:
                in_specs.append(pl.BlockSpec((tn, tk), lambda i, j, k: (j, k)))
        else:
            if mode == "nn":
                per = b_cols // tn
                in_specs.append(pl.BlockSpec((None, tk, tn), lambda i, j, k: (j // per, k, j % per)))
            else:
                per = b_cols // tk
                in_specs.append(pl.BlockSpec((None, tn, tk), lambda i, j, k: (k // per, j, k % per)))
    if out_cols is None:
        out_spec = pl.BlockSpec((tm, tn), lambda i, j, k: (i, j))
        out_shape = jax.ShapeDtypeStruct((M, N), out_dtype)
    else:
        pero = out_cols // tn
        out_spec = pl.BlockSpec((None, tm, tn), lambda i, j, k: (j // pero, i, j % pero))
        out_shape = jax.ShapeDtypeStruct((NDEV, M, out_cols), out_dtype)

    two_a = len(a_list) == 2
    two_b = len(b_list) == 2
    extra = []
    if after is not None:
        in_specs.append(pl.BlockSpec(memory_space=pl.ANY))
        extra.append(after)

    def body(*refs):
        o_ref, acc_ref = refs[-2], refs[-1]
        j = pl.program_id(1)
        k = pl.program_id(2)

        @pl.when(k == 0)
        def _():
            acc_ref[...] = jnp.zeros_like(acc_ref)

        if two_a:
            a1, a2, b1 = refs[0], refs[1], refs[2]
            nk1_ = Ks[0] // tk

            @pl.when(k < nk1_)
            def _():
                acc_ref[...] += _dot(a1[...], b1[...], dims)

            @pl.when(k >= nk1_)
            def _():
                acc_ref[...] += _dot(a2[...], b1[...], dims)
        elif two_b:
            a1, b1, b2 = refs[0], refs[1], refs[2]
            nj1_ = Ns[0] // tn

            @pl.when(j < nj1_)
            def _():
                acc_ref[...] += _dot(a1[...], b1[...], dims)

            @pl.when(j >= nj1_)
            def _():
                acc_ref[...] += _dot(a1[...], b2[...], dims)
        else:
            acc_ref[...] += _dot(refs[0][...], refs[1][...], dims)

        @pl.when(k == nk - 1)
        def _():
            o_ref[...] = acc_ref[...].astype(o_ref.dtype)

    return pl.pallas_call(
        body, name=name, grid=grid, in_specs=in_specs, out_specs=out_spec, out_shape=out_shape,
        scratch_shapes=[pltpu.VMEM((tm, tn), F32)],
        compiler_params=_cp("parallel", "parallel", "arbitrary"),
    )(*a_list, *b_list, *extra)


def _rstd(x):
    return lax.rsqrt(jnp.mean(x * x, axis=-1, keepdims=True) + NORM_EPS)


def _norm_bwd_rows(dxn, xn, r):
    return r * (dxn - xn * jnp.mean(dxn * xn, axis=-1, keepdims=True))


def _row_spec(tm, d):
    return pl.BlockSpec((tm, d), lambda i: (i, 0))


def _vec_spec(d):
    return pl.BlockSpec((1, d), lambda i: (0, 0))


def _mm_rows(a, b, mode, tk, name, rows, vecs, outs, epilogue, b_cols=None, after=None):
    a_list = list(a) if isinstance(a, (list, tuple)) else [a]
    m = a_list[0].shape[0]
    ks = [x.shape[1] for x in a_list]
    n = D_MODEL
    pieces = tk is None
    nk = 1 if pieces else sum(ks) // tk
    dims = NN if mode == "nn" else NT
    if pieces:
        assert mode == "nt" and b_cols is None
        in_specs = [pl.BlockSpec((TM, kp), lambda i, k: (i, 0)) for kp in ks]
        tk = sum(ks)
    elif len(a_list) == 1:
        in_specs = [pl.BlockSpec((TM, tk), lambda i, k: (i, k))]
    else:
        nk1 = ks[0] // tk
        in_specs = [pl.BlockSpec((TM, tk), lambda i, k: (i, jnp.minimum(k, nk1 - 1))),
                    pl.BlockSpec((TM, tk), lambda i, k: (i, jnp.maximum(k - nk1, 0)))]
    if mode == "nn":
        in_specs.append(pl.BlockSpec((tk, n), lambda i, k: (k, 0)))
    elif b_cols is None:
        in_specs.append(pl.BlockSpec((n, tk), lambda i, k: (0, k)))
    else:
        per = b_cols // tk
        in_specs.append(pl.BlockSpec((None, n, tk), lambda i, k: (k // per, 0, k % per)))
    in_specs += [pl.BlockSpec((TM, n), lambda i, k: (i, 0))] * len(rows)
    in_specs += [pl.BlockSpec((1, n), lambda i, k: (0, 0))] * len(vecs)
    extra = []
    if after is not None:
        in_specs.append(pl.BlockSpec(memory_space=pl.ANY))
        extra.append(after)
    out_specs, out_shape = [], []
    for o in outs:
        if o == "sum":
            out_specs.append(pl.BlockSpec((1, n), lambda i, k: (0, 0)))
            out_shape.append(jax.ShapeDtypeStruct((1, n), F32))
        else:
            out_specs.append(pl.BlockSpec((TM, n), lambda i, k: (i, 0)))
            out_shape.append(jax.ShapeDtypeStruct((m, n), o))
    na, nr, nv = len(a_list), len(rows), len(vecs)

    def body(*refs):
        a_refs, b_ref = refs[:na], refs[na]
        row_refs = refs[na + 1:na + 1 + nr]
        vec_refs = refs[na + 1 + nr:na + 1 + nr + nv]
        out_refs = refs[len(refs) - 1 - len(outs):len(refs) - 1]
        acc_ref = refs[-1]
        i, k = pl.program_id(0), pl.program_id(1)

        @pl.when(k == 0)
        def _():
            acc_ref[...] = jnp.zeros_like(acc_ref)

        if pieces:
            off = 0
            for a_ref in a_refs:
                kp = a_ref.shape[1]
                acc_ref[...] += _dot(a_ref[...], b_ref[:, off:off + kp], dims)
                off += kp
        elif na == 1:
            acc_ref[...] += _dot(a_refs[0][...], b_ref[...], dims)
        else:
            nk1_ = ks[0] // tk

            @pl.when(k < nk1_)
            def _():
                acc_ref[...] += _dot(a_refs[0][...], b_ref[...], dims)

            @pl.when(k >= nk1_)
            def _():
                acc_ref[...] += _dot(a_refs[1][...], b_ref[...], dims)

        @pl.when(k == nk - 1)
        def _():
            vals = epilogue(acc_ref[...], [r[...] for r in row_refs], [v[...] for v in vec_refs])
            for o, ref, val in zip(outs, out_refs, vals):
                if o == "sum":
                    @pl.when(i == 0)
                    def _():
                        ref[...] = val

                    @pl.when(i > 0)
                    def _():
                        ref[...] += val
                else:
                    ref[...] = val.astype(o)

    return pl.pallas_call(
        body, name=name, grid=(m // TM, nk), in_specs=in_specs, out_specs=out_specs, out_shape=out_shape,
        scratch_shapes=[pltpu.VMEM((TM, n), F32)],
        compiler_params=_cp("arbitrary", "arbitrary"),
    )(*a_list, b, *rows, *vecs, *extra)


def _proj_in(x, g, w_in, name):
    s, d = x.shape
    n = w_in.shape[1]

    def body(x_ref, g_ref, w_ref, h_ref, p_ref, f_ref):
        xv = x_ref[...]
        h = (xv * _rstd(xv) * g_ref[...]).astype(BF16)
        h_ref[...] = h
        acc = _dot(h, w_ref[...], NN)
        p_ref[...] = acc.astype(BF16)
        f_ref[...] = acc[:, F_COL:]

    return pl.pallas_call(
        body, name=name, grid=(s // TM,),
        in_specs=[_row_spec(TM, d), _vec_spec(d), pl.BlockSpec((d, n), lambda i: (0, 0))],
        out_specs=[_row_spec(TM, d), _row_spec(TM, n), _row_spec(TM, n - F_COL)],
        out_shape=[jax.ShapeDtypeStruct((s, d), BF16), jax.ShapeDtypeStruct((s, n), BF16),
                   jax.ShapeDtypeStruct((s, n - F_COL), F32)],
        compiler_params=_cp("parallel"),
    )(x, g, w_in)


def _dw_in(h, pieces, name):
    s, d = h.shape
    n = sum(p.shape[1] for p in pieces)
    tk = 1024
    nk = s // tk

    def body(*refs):
        h_ref, piece_refs, o_ref, acc_ref = refs[0], refs[1:-2], refs[-2], refs[-1]
        k = pl.program_id(1)

        @pl.when(k == 0)
        def _():
            acc_ref[...] = jnp.zeros_like(acc_ref)

        off = 0
        for p_ref in piece_refs:
            w = p_ref.shape[1]
            acc_ref[:, off:off + w] += _dot(h_ref[...], p_ref[...], TN)
            off += w

        @pl.when(k == nk - 1)
        def _():
            o_ref[...] = acc_ref[...].astype(BF16)

    return pl.pallas_call(
        body, name=name, grid=(d // TM, nk),
        in_specs=[pl.BlockSpec((tk, TM), lambda i, k: (k, i))] +
                 [pl.BlockSpec((tk, p.shape[1]), lambda i, k: (k, 0)) for p in pieces],
        out_specs=pl.BlockSpec((TM, n), lambda i, k: (i, 0)),
        out_shape=jax.ShapeDtypeStruct((d, n), BF16),
        scratch_shapes=[pltpu.VMEM((TM, n), F32)],
        compiler_params=_cp("parallel", "arbitrary"),
    )(h, *pieces)


def _epi_resid(y, rows, vecs):
    (x_in,), (g_post, g_next) = rows, vecs
    xo = x_in + y * _rstd(y) * g_post
    return y, xo, xo * _rstd(xo) * g_next


def _epi_norm_bwd(dh, rows, vecs):
    x, dx_res = rows[0], rows[1]
    r = _rstd(x)
    xn = x * r
    dx = dx_res + _norm_bwd_rows(dh * vecs[0], xn, r)
    res = [dx, jnp.sum(dh * xn, axis=0, keepdims=True)]
    if len(rows) == 3:
        y = rows[2]
        r2 = _rstd(y)
        yn = y * r2
        res += [_norm_bwd_rows(dx * vecs[1], yn, r2), jnp.sum(dx * yn, axis=0, keepdims=True)]
    return res


def _norm_fwd(x, g, name):
    s, d = x.shape
    tm = min(TM, s)

    def body(x_ref, g_ref, h_ref):
        xv = x_ref[...]
        h_ref[...] = (xv * _rstd(xv) * g_ref[...]).astype(BF16)

    return pl.pallas_call(
        body, name=name, grid=(s // tm,), in_specs=[_row_spec(tm, d), _vec_spec(d)],
        out_specs=_row_spec(tm, d), out_shape=jax.ShapeDtypeStruct((s, d), BF16),
        compiler_params=_cp("parallel"),
    )(x, g)


def _norm_bwd(dh, x, dx_res, g_pre, name, prev=None):
    s, d = x.shape
    tm = min(TM, s)
    has_prev = prev is not None

    def body(*refs):
        if has_prev:
            dh_ref, x_ref, dr_ref, g_ref, y_ref, gp_ref, dx_ref, dg_ref, dy_ref, dgp_ref = refs
        else:
            dh_ref, x_ref, dr_ref, g_ref, dx_ref, dg_ref = refs
        i = pl.program_id(0)
        xv = x_ref[...]
        r = _rstd(xv)
        xn = xv * r
        dhv = dh_ref[...].astype(F32)
        dx = dr_ref[...] + _norm_bwd_rows(dhv * g_ref[...], xn, r)
        dx_ref[...] = dx
        dg = jnp.sum(dhv * xn, axis=0, keepdims=True)

        @pl.when(i == 0)
        def _():
            dg_ref[...] = dg

        @pl.when(i > 0)
        def _():
            dg_ref[...] += dg

        if has_prev:
            yv = y_ref[...]
            r2 = _rstd(yv)
            yn = yv * r2
            dy_ref[...] = _norm_bwd_rows(dx * gp_ref[...], yn, r2).astype(BF16)
            dgp = jnp.sum(dx * yn, axis=0, keepdims=True)

            @pl.when(i == 0)
            def _():
                dgp_ref[...] = dgp

            @pl.when(i > 0)
            def _():
                dgp_ref[...] += dgp

    in_specs = [_row_spec(tm, d), _row_spec(tm, d), _row_spec(tm, d), _vec_spec(d)]
    out_specs = [_row_spec(tm, d), _vec_spec(d)]
    out_shape = [jax.ShapeDtypeStruct((s, d), F32), jax.ShapeDtypeStruct((1, d), F32)]
    args = [dh, x, dx_res, g_pre]
    if has_prev:
        in_specs += [_row_spec(tm, d), _vec_spec(d)]
        out_specs += [_row_spec(tm, d), _vec_spec(d)]
        out_shape += [jax.ShapeDtypeStruct((s, d), BF16), jax.ShapeDtypeStruct((1, d), F32)]
        args += list(prev)
    return pl.pallas_call(
        body, name=name, grid=(s // tm,), in_specs=in_specs, out_specs=out_specs, out_shape=out_shape,
        compiler_params=_cp("arbitrary"),
    )(*args)


def _split3(v):
    hi = v.astype(BF16)
    r1 = v - hi.astype(F32)
    mid = r1.astype(BF16)
    lo = (r1 - mid.astype(F32)).astype(BF16)
    return hi, mid, lo


def _tri_dot(tri, v):
    hi, mid, lo = _split3(v)
    return _dot(tri, hi, NN) + _dot(tri, mid, NN) + _dot(tri, lo, NN)


def _gate_cumsum(fraw, b_pad, name):
    s = fraw.shape[0]

    def body(f_ref, b_ref, flog_ref, cum_ref, carry_ref):
        i = pl.program_id(0)

        @pl.when(i == 0)
        def _():
            carry_ref[...] = jnp.zeros_like(carry_ref)

        flog = f_ref[...] + b_ref[...]
        flog_ref[...] = flog
        lf = jnp.minimum(flog, 0.0) - jnp.log(1.0 + jnp.exp(-jnp.abs(flog)))
        lane = lax.broadcasted_iota(jnp.int32, (1, 128), 1)
        lf = jnp.where(lane < FOX_HEADS, lf, 0.0)
        row = lax.broadcasted_iota(jnp.int32, (TM, TM), 0)
        col = lax.broadcasted_iota(jnp.int32, (TM, TM), 1)
        tri = (row >= col).astype(BF16)
        cum = _tri_dot(tri, lf) + carry_ref[...]
        cum_ref[...] = cum
        carry_ref[...] = cum[TM - 1:TM, :]

    return pl.pallas_call(
        body, name=name, grid=(s // TM,),
        in_specs=[_row_spec(TM, 128), _vec_spec(128)],
        out_specs=[_row_spec(TM, 128), _row_spec(TM, 128)],
        out_shape=[jax.ShapeDtypeStruct((s, 128), F32), jax.ShapeDtypeStruct((s, 128), F32)],
        scratch_shapes=[pltpu.VMEM((1, 128), F32)],
        compiler_params=_cp("arbitrary"),
    )(fraw, b_pad)


def _gate_bwd(qaux, kaux, flog, name):
    s = flog.shape[0]
    n = s // TM

    def body(qa_ref, ka_ref, fl_ref, dp_ref, db_ref, carry_ref):
        i = pl.program_id(0)

        @pl.when(i == 0)
        def _():
            carry_ref[...] = jnp.zeros_like(carry_ref)

        src = lax.broadcasted_iota(jnp.int32, (128, 128), 0)
        dst = lax.broadcasted_iota(jnp.int32, (128, 128), 1)
        dcum = jnp.zeros((TM, 128), F32)
        for p in range(HEAD_PAIRS):
            for ref, l0, l1, sign in ((qa_ref, 64, 0, 1.0), (ka_ref, 67, 3, -1.0)):
                hit = jnp.logical_or(jnp.logical_and(src == l0, dst == 2 * p),
                                     jnp.logical_and(src == l1, dst == 2 * p + 1))
                sel = jnp.where(hit, sign, 0.0).astype(BF16)
                for piece in _split3(ref[p]):
                    dcum = dcum + _dot(piece, sel, NN)
        row = lax.broadcasted_iota(jnp.int32, (TM, TM), 0)
        col = lax.broadcasted_iota(jnp.int32, (TM, TM), 1)
        tri = (row <= col).astype(BF16)
        dlf = _tri_dot(tri, dcum) + carry_ref[...]
        carry_ref[...] = dlf[0:1, :]
        lane = lax.broadcasted_iota(jnp.int32, (1, 128), 1)
        df = jnp.where(lane < FOX_HEADS, dlf / (1.0 + jnp.exp(fl_ref[...])), 0.0)
        dp_ref[...] = df.astype(BF16)
        db = jnp.sum(df, axis=0, keepdims=True)

        @pl.when(i == 0)
        def _():
            db_ref[...] = db

        @pl.when(i > 0)
        def _():
            db_ref[...] += db

    rev = lambda i: (n - 1 - i, 0)
    return pl.pallas_call(
        body, name=name, grid=(n,),
        in_specs=[pl.BlockSpec((HEAD_PAIRS, TM, 128), lambda i: (0, n - 1 - i, 0)),
                  pl.BlockSpec((HEAD_PAIRS, TM, 128), lambda i: (0, n - 1 - i, 0)), pl.BlockSpec((TM, 128), rev)],
        out_specs=[pl.BlockSpec((TM, 128), rev), _vec_spec(128)],
        out_shape=[jax.ShapeDtypeStruct((s, 128), BF16), jax.ShapeDtypeStruct((1, 128), F32)],
        scratch_shapes=[pltpu.VMEM((1, 128), F32)],
        compiler_params=_cp("arbitrary"),
    )(qaux, kaux, flog)


def _pool_consts(i, rows):
    lane = lax.broadcasted_iota(jnp.int32, (rows, D_POOL), 1)
    t1 = lax.broadcasted_iota(jnp.int32, (rows, D_POOL), 0) + i * TM + 1
    win = jnp.where(lane < 64, 2, jnp.where(lane < 128, 4, jnp.where(lane < 192, 8, 16)))
    inv = 1.0 / jnp.minimum(t1, win).astype(F32)
    return lane, inv


def _by_group(lane, s2, s4, s8, s16):
    return jnp.where(lane < 64, s2, jnp.where(lane < 128, s4, jnp.where(lane < 192, s8, s16)))


def _pool_diff(i, u_ref, halo_ref):
    u = u_ref[...].astype(F32)
    halo = jnp.where(i > 0, halo_ref[...].astype(F32), 0.0)
    ext = jnp.concatenate([halo, u], axis=0)
    s2 = ext + pltpu.roll(ext, 1, 0)
    s4 = s2 + pltpu.roll(s2, 2, 0)
    s8 = s4 + pltpu.roll(s4, 4, 0)
    s16 = s8 + pltpu.roll(s8, 8, 0)
    lane, inv = _pool_consts(i, TM)
    sel = _by_group(lane, s2[POOL_HALO:], s4[POOL_HALO:], s8[POOL_HALO:], s16[POOL_HALO:])
    return sel * inv - u


def _pool_fwd(proj, wbd, scale, ycat, name):
    s = proj.shape[0]
    hb = TM // POOL_HALO

    def body(u_ref, halo_ref, w_ref, sc_ref, y_any, y_ref):
        del y_any
        i = pl.program_id(0)
        diff = _pool_diff(i, u_ref, halo_ref)
        mixed = _dot(diff.astype(BF16), w_ref[...], NN)
        y_ref[...] = (mixed * sc_ref[...]).astype(BF16)

    return pl.pallas_call(
        body, name=name, grid=(s // TM,),
        in_specs=[pl.BlockSpec((TM, D_POOL), lambda i: (i, 0)),
                  pl.BlockSpec((POOL_HALO, D_POOL), lambda i: (jnp.maximum(i * hb - 1, 0), 0)),
                  pl.BlockSpec((D_POOL, D_POOL), lambda i: (0, 0)), _vec_spec(D_POOL),
                  pl.BlockSpec(memory_space=pl.ANY)],
        out_specs=pl.BlockSpec((TM, D_POOL), lambda i: (i, 0)),
        out_shape=jax.ShapeDtypeStruct(ycat.shape, ycat.dtype),
        input_output_aliases={4: 0},
        compiler_params=_cp("parallel"),
    )(proj, proj, wbd, scale, ycat)


def _pool_bwd(proj, dycat, wbd, scale, name):
    s = proj.shape[0]
    n = s // TM
    hb = TM // POOL_HALO
    last_halo = s // POOL_HALO - 1

    def body(u_ref, halo_ref, dy_ref, dyp_ref, w_ref, sc_ref, dp_ref, dw_ref, dsc_ref):
        i = pl.program_id(0)
        diff = _pool_diff(i, u_ref, halo_ref)
        diff_b = diff.astype(BF16)
        mixed = _dot(diff_b, w_ref[...], NN)
        dy = dy_ref[...].astype(F32)
        dmix = (dy * sc_ref[...]).astype(BF16)
        dyp = jnp.where(i < n - 1, dyp_ref[...].astype(F32), 0.0)
        dmix_p = (dyp * sc_ref[...]).astype(BF16)
        dd = _dot(dmix, w_ref[...], NT)
        dd_p = _dot(dmix_p, w_ref[...], NT)
        lane, inv = _pool_consts(i, TM)
        _, inv_p = _pool_consts(i + 1, POOL_HALO)
        ext = jnp.concatenate([dd * inv, dd_p * inv_p], axis=0)
        rows = TM + POOL_HALO
        l2 = ext + pltpu.roll(ext, rows - 1, 0)
        l4 = l2 + pltpu.roll(l2, rows - 2, 0)
        l8 = l4 + pltpu.roll(l4, rows - 4, 0)
        l16 = l8 + pltpu.roll(l8, rows - 8, 0)
        du = _by_group(lane, l2[:TM], l4[:TM], l8[:TM], l16[:TM]) - dd
        dp_ref[...] = du.astype(BF16)
        dw = _dot(diff_b, dmix, TN)
        dsc = jnp.sum(dy * mixed, axis=0, keepdims=True)

        @pl.when(i == 0)
        def _():
            dw_ref[...] = dw
            dsc_ref[...] = dsc

        @pl.when(i > 0)
        def _():
            dw_ref[...] += dw
            dsc_ref[...] += dsc

    return pl.pallas_call(
        body, name=name, grid=(n,),
        in_specs=[pl.BlockSpec((TM, D_POOL), lambda i: (i, 0)),
                  pl.BlockSpec((POOL_HALO, D_POOL), lambda i: (jnp.maximum(i * hb - 1, 0), 0)),
                  pl.BlockSpec((TM, D_POOL), lambda i: (i, 0)),
                  pl.BlockSpec((POOL_HALO, D_POOL), lambda i: (jnp.minimum((i + 1) * hb, last_halo), 0)),
                  pl.BlockSpec((D_POOL, D_POOL), lambda i: (0, 0)), _vec_spec(D_POOL)],
        out_specs=[pl.BlockSpec((TM, D_POOL), lambda i: (i, 0)),
                   pl.BlockSpec((D_POOL, D_POOL), lambda i: (0, 0)), _vec_spec(D_POOL)],
        out_shape=[jax.ShapeDtypeStruct((s, D_POOL), BF16),
                   jax.ShapeDtypeStruct((D_POOL, D_POOL), F32), jax.ShapeDtypeStruct((1, D_POOL), F32)],
        compiler_params=_cp("arbitrary"),
    )(proj, proj, dycat, dycat, wbd, scale)


Q_BLK = D_POOL // 128
K_BLK = Q_BLK + D_FOX // 128
V_BLK = K_BLK + D_FOX // 128


def _operand_rows(v0, v1, ones_off):
    row = lax.broadcasted_iota(jnp.int32, (128, 1), 0)
    half = row & 63
    out = jnp.where(jnp.logical_and(half >= ones_off, half < ones_off + 3), 1.0, 0.0) + jnp.zeros_like(v0)
    for base, v in ((64, v0), (0, v1)):
        for j, piece in enumerate(_split3(v)):
            out = jnp.where(row == base + j, piece.astype(F32), out)
    return out


def _fox_operands(cum, name):
    s = cum.shape[0]
    width = HEAD_PAIRS * 128

    def body(c_ref, aq_ref, ak_ref):
        pieces = _split3(c_ref[...])
        row = lax.broadcasted_iota(jnp.int32, (128, width), 0)
        col = lax.broadcasted_iota(jnp.int32, (128, width), 1)
        base = (row >> 1) * 128 + (1 - (row & 1)) * 64
        half = lax.broadcasted_iota(jnp.int32, (1, width), 1) & 63
        for o_ref, off, sign, ones_off in ((aq_ref, 0, 1.0, 3), (ak_ref, 3, -1.0, 0)):
            out = jnp.where(jnp.logical_and(half >= ones_off, half < ones_off + 3), 1.0, 0.0)
            for j, piece in enumerate(pieces):
                sel = jnp.where(jnp.logical_and(col == base + off + j, row < FOX_HEADS), sign, 0.0).astype(BF16)
                out = out + _dot(piece, sel, NN)
            o_ref[...] = out.astype(BF16)

    return pl.pallas_call(
        body, name=name, grid=(s // TM,), in_specs=[_row_spec(TM, 128)],
        out_specs=[_row_spec(TM, width), _row_spec(TM, width)],
        out_shape=[jax.ShapeDtypeStruct((s, width), BF16)] * 2,
        compiler_params=_cp("parallel"),
    )(cum)


def _fox_do_operand(dycat, ycat, after, name):
    s = dycat.shape[0]

    rb = 1024
    nblk = D_FOX // D_POOL

    def body(*refs):
        do_refs, o_refs, ad_ref = refs[:nblk], refs[nblk:2 * nblk], refs[-1]
        src = lax.broadcasted_iota(jnp.int32, (D_POOL, D_POOL), 0)
        dst = lax.broadcasted_iota(jnp.int32, (D_POOL, D_POOL), 1)
        same_pair = (src >> 7) == (dst >> 7)
        s_in, d_in = src & 127, dst & 127
        hit = jnp.logical_and(same_pair, jnp.logical_or(
            jnp.logical_and(s_in < 64, jnp.logical_and(d_in >= 64, d_in < 67)), jnp.logical_and(s_in >= 64, d_in < 3)))
        sel = jnp.where(hit, 1.0, 0.0).astype(BF16)
        j = lax.broadcasted_iota(jnp.int32, (1, D_POOL), 1) & 63
        for b in range(nblk):
            dd = do_refs[b][...].astype(F32) * o_refs[b][...].astype(F32)
            dsum = jnp.zeros(dd.shape, F32)
            for piece in _split3(dd):
                dsum = dsum + _dot(piece, sel, NN)
            hi, mid, lo = _split3(-dsum)
            ad_ref[:, D_POOL * b:D_POOL * (b + 1)] = jnp.where(j == 0, hi, jnp.where(j == 1, mid, lo))

    blks = [pl.BlockSpec((rb, D_POOL), functools.partial(lambda i, b: (i, 1 + b), b=b)) for b in range(nblk)]
    return pl.pallas_call(
        body, name=name, grid=(s // rb,), in_specs=blks + blks + [pl.BlockSpec(memory_space=pl.ANY)],
        out_specs=pl.BlockSpec((rb, D_FOX), lambda i: (i, 0)),
        out_shape=jax.ShapeDtypeStruct((s, D_FOX), BF16),
        compiler_params=_cp("parallel"),
    )(*[dycat] * nblk, *[ycat] * nblk, after)


def _causal_pairs(nq, key_major):
    if key_major:
        pairs = [(q, k) for k in range(nq) for q in range(k, nq)]
    else:
        pairs = [(q, k) for q in range(nq) for k in range(q + 1)]
    return (jnp.asarray([p[0] for p in pairs], jnp.int32), jnp.asarray([p[1] for p in pairs], jnp.int32))


def _fox_fwd(proj, aq, ak, name):
    s = proj.shape[0]
    nq = s // TQ
    qi_arr, ki_arr = _causal_pairs(nq, key_major=False)

    def body(qi_ref, ki_ref, q_ref, k_ref, v_ref, aq_ref, ak_ref, o_ref, aqb_ref, m0_ref, m1_ref, acc_ref, aux_ref):
        t = pl.program_id(1)
        qi, ki = qi_ref[t], ki_ref[t]
        lane = lax.broadcasted_iota(jnp.int32, (1, 128), 1)
        masks = [lane < 64, lane >= 64]
        ones_v = jnp.where((lane & 63) == 8, 1.0, 0.0).astype(BF16)
        top = lax.broadcasted_iota(jnp.int32, (128, 1), 0) < 64
        m_ref = [m0_ref, m1_ref]

        @pl.when(ki == 0)
        def _():
            m0_ref[...] = jnp.full_like(m0_ref, NEG)
            m1_ref[...] = jnp.full_like(m1_ref, NEG)
            acc_ref[...] = jnp.zeros_like(acc_ref)
            aux_ref[...] = jnp.zeros_like(aux_ref)

        def step(diag):
            q2s = q_ref[...] * 0.125
            k2, v2, aq2, ak2 = k_ref[...], v_ref[...], aq_ref[...], ak_ref[...]
            pv, alpha = [], []
            for hh in range(2):
                qh = jnp.where(masks[hh], q2s, aq2)
                kh = jnp.where(masks[hh], k2, ak2)
                vh = jnp.where(masks[hh], v2, ones_v)
                sc = _dot(kh, qh, NT)
                if diag:
                    key = lax.broadcasted_iota(jnp.int32, sc.shape, 0)
                    qry = lax.broadcasted_iota(jnp.int32, sc.shape, 1)
                    sc = jnp.where(qry >= key, sc, NEG)
                m_prev = m_ref[hh][...]
                m_new = jnp.maximum(m_prev, jnp.max(sc, axis=0, keepdims=True))
                m_ref[hh][...] = m_new
                alpha.append(jnp.exp(m_prev - m_new))
                pv.append(_dot(vh, jnp.exp(sc - m_new).astype(BF16), TN))
            acc_ref[...] = acc_ref[...] * jnp.where(top, alpha[0], alpha[1]) + jnp.where(top, pv[0], pv[1])
            aux_ref[...] = aux_ref[...] * jnp.where(top, alpha[1], alpha[0]) + jnp.where(top, pv[1], pv[0])

        @pl.when(ki < qi)
        def _():
            step(False)

        @pl.when(ki == qi)
        def _():
            step(True)
            aux = aux_ref[...]
            l0, l1 = aux[72:73, :], aux[8:9, :]
            o_ref[...] = (acc_ref[...] * jnp.where(top, 1.0 / l0, 1.0 / l1)).T.astype(BF16)
            aqt = aq_ref[...].astype(F32).T
            cum0 = aqt[64:65, :] + aqt[65:66, :] + aqt[66:67, :]
            cum1 = aqt[0:1, :] + aqt[1:2, :] + aqt[2:3, :]
            aqb = _operand_rows(cum0 - (m0_ref[...] + jnp.log(l0)), cum1 - (m1_ref[...] + jnp.log(l1)), 3)
            aqb_ref[...] = aqb.T.astype(BF16)

    grid_spec = pltpu.PrefetchScalarGridSpec(
        num_scalar_prefetch=2, grid=(HEAD_PAIRS, int(qi_arr.shape[0])),
        in_specs=[pl.BlockSpec((TQ, 128), lambda p, t, qi, ki: (qi[t], Q_BLK + p)),
                  pl.BlockSpec((TQ, 128), lambda p, t, qi, ki: (ki[t], K_BLK + p)),
                  pl.BlockSpec((TQ, 128), lambda p, t, qi, ki: (ki[t], V_BLK + p)),
                  pl.BlockSpec((TQ, 128), lambda p, t, qi, ki: (qi[t], p)),
                  pl.BlockSpec((TQ, 128), lambda p, t, qi, ki: (ki[t], p))],
        out_specs=[pl.BlockSpec((TQ, 128), lambda p, t, qi, ki: (qi[t], Q_BLK + p)),
                   pl.BlockSpec((TQ, 128), lambda p, t, qi, ki: (qi[t], p))],
        scratch_shapes=[pltpu.VMEM((1, TQ), F32), pltpu.VMEM((1, TQ), F32),
                        pltpu.VMEM((128, TQ), F32), pltpu.VMEM((128, TQ), F32)])
    return pl.pallas_call(
        body, name=name, grid_spec=grid_spec,
        out_shape=[jax.ShapeDtypeStruct((s, D_MODEL), BF16), jax.ShapeDtypeStruct((s, HEAD_PAIRS * 128), BF16)],
        compiler_params=_cp("parallel", "arbitrary"),
    )(qi_arr, ki_arr, proj, proj, proj, aq, ak)


def _fox_bwd(proj, dycat, aqb, ak, ad, name):
    s = proj.shape[0]
    nq = s // TQ
    qi_arr, ki_arr = _causal_pairs(nq, key_major=True)

    def body(qi_ref, ki_ref, q_ref, k_ref, v_ref, do_ref, aq_ref, ak_ref, ad_ref,
             dq_ref, dk_ref, dv_ref, qaux_ref, kaux_ref, dq_acc, qaux_acc, dk_acc, dv_acc, kaux_acc):
        t = pl.program_id(1)
        qi, ki = qi_ref[t], ki_ref[t]
        lane = lax.broadcasted_iota(jnp.int32, (1, 128), 1)
        masks = [lane < 64, lane >= 64]
        ones_v = jnp.where((lane & 63) < 3, 1.0, 0.0).astype(BF16)
        top = lax.broadcasted_iota(jnp.int32, (128, 1), 0) < 64

        @pl.when(qi == ki)
        def _():
            dk_acc[...] = jnp.zeros_like(dk_acc)
            dv_acc[...] = jnp.zeros_like(dv_acc)
            kaux_acc[...] = jnp.zeros_like(kaux_acc)

        def step(diag):
            q2s = q_ref[...] * 0.125
            k2, v2, do2 = k_ref[...], v_ref[...], do_ref[...]
            aq2, ak2, ad2 = aq_ref[...], ak_ref[...], ad_ref[...]
            dq, dk, dv = [], [], []
            for hh in range(2):
                qh = jnp.where(masks[hh], q2s, aq2)
                kh = jnp.where(masks[hh], k2, ak2)
                doh = jnp.where(masks[hh], do2, ad2)
                vh = jnp.where(masks[hh], v2, ones_v)
                sc = _dot(kh, qh, NT)
                if diag:
                    key = lax.broadcasted_iota(jnp.int32, sc.shape, 0)
                    qry = lax.broadcasted_iota(jnp.int32, sc.shape, 1)
                    sc = jnp.where(qry >= key, sc, NEG)
                p = jnp.exp(sc)
                dsb = (p * _dot(vh, doh, NT)).astype(BF16)
                dv.append(_dot(p.astype(BF16), doh, NN))
                dk.append(_dot(dsb, qh, NN))
                dq.append(_dot(kh, dsb, TN))
            dk_acc[...] += jnp.where(masks[0], dk[0], dk[1])
            kaux_acc[...] += jnp.where(masks[0], dk[1], dk[0])
            dv_acc[...] += jnp.where(masks[0], dv[0], dv[1])
            dq_new = jnp.where(top, dq[0], dq[1])
            qaux_new = jnp.where(top, dq[1], dq[0])

            @pl.when(ki == 0)
            def _():
                dq_acc[qi] = dq_new
                qaux_acc[qi] = qaux_new

            @pl.when(ki > 0)
            def _():
                dq_acc[qi] += dq_new
                qaux_acc[qi] += qaux_new

        @pl.when(qi > ki)
        def _():
            step(False)

        @pl.when(qi == ki)
        def _():
            step(True)
            rows = pl.ds(pl.multiple_of(qi * TQ, TQ), TQ)
            dq_ref[rows, :] = (dq_acc[qi] * 0.125).T.astype(BF16)
            qaux_ref[rows, :] = qaux_acc[qi].T

        @pl.when(qi == nq - 1)
        def _():
            dk_ref[...] = dk_acc[...].astype(BF16)
            dv_ref[...] = dv_acc[...].astype(BF16)
            kaux_ref[...] = kaux_acc[...]

    grid_spec = pltpu.PrefetchScalarGridSpec(
        num_scalar_prefetch=2, grid=(HEAD_PAIRS, int(qi_arr.shape[0])),
        in_specs=[pl.BlockSpec((TQ, 128), lambda p, t, qi, ki: (qi[t], Q_BLK + p)),
                  pl.BlockSpec((TQ, 128), lambda p, t, qi, ki: (ki[t], K_BLK + p)),
                  pl.BlockSpec((TQ, 128), lambda p, t, qi, ki: (ki[t], V_BLK + p)),
                  pl.BlockSpec((TQ, 128), lambda p, t, qi, ki: (qi[t], Q_BLK + p)),
                  pl.BlockSpec((TQ, 128), lambda p, t, qi, ki: (qi[t], p)),
                  pl.BlockSpec((TQ, 128), lambda p, t, qi, ki: (ki[t], p)),
                  pl.BlockSpec((TQ, 128), lambda p, t, qi, ki: (qi[t], p))],
        out_specs=[pl.BlockSpec((s, 128), lambda p, t, qi, ki: (0, p)),
                   pl.BlockSpec((TQ, 128), lambda p, t, qi, ki: (ki[t], p)),
                   pl.BlockSpec((TQ, 128), lambda p, t, qi, ki: (ki[t], p)),
                   pl.BlockSpec((None, s, 128), lambda p, t, qi, ki: (p, 0, 0)),
                   pl.BlockSpec((None, TQ, 128), lambda p, t, qi, ki: (p, ki[t], 0))],
        scratch_shapes=[pltpu.VMEM((nq, 128, TQ), F32), pltpu.VMEM((nq, 128, TQ), F32),
                        pltpu.VMEM((TQ, 128), F32), pltpu.VMEM((TQ, 128), F32), pltpu.VMEM((TQ, 128), F32)])
    return pl.pallas_call(
        body, name=name, grid_spec=grid_spec,
        out_shape=[jax.ShapeDtypeStruct((s, D_FOX), BF16)] * 3 + [jax.ShapeDtypeStruct((HEAD_PAIRS, s, 128), F32)] * 2,
        compiler_params=_cp("arbitrary", "arbitrary"),
    )(qi_arr, ki_arr, proj, proj, proj, dycat, aqb, ak, ad)


XA_SCALE = XA_DIM ** -0.5


def _xattn_fwd(q2, kv, name):
    s = q2.shape[0]
    m = kv.shape[0]

    def body(q_ref, kv_ref, o_ref):
        for h in range(XA_HEADS):
            c0 = h * XA_DIM
            sc = _dot(q_ref[:, c0:c0 + XA_DIM], kv_ref[:, c0:c0 + XA_DIM], NT) * XA_SCALE
            e = jnp.exp(sc - jnp.max(sc, axis=1, keepdims=True))
            p = e / jnp.sum(e, axis=1, keepdims=True)
            o_ref[:, c0:c0 + XA_DIM] = _dot(p.astype(BF16), kv_ref[:, D_MODEL + c0:D_MODEL + c0 + XA_DIM], NN).astype(BF16)

    return pl.pallas_call(
        body, name=name, grid=(s // TM,),
        in_specs=[_row_spec(TM, D_MODEL), pl.BlockSpec((m, 2 * D_MODEL), lambda i: (0, 0))],
        out_specs=_row_spec(TM, D_MODEL), out_shape=jax.ShapeDtypeStruct((s, D_MODEL), BF16),
        compiler_params=_cp("parallel"),
    )(q2, kv)


def _xattn_bwd(q2, kv, do2, name):
    s = q2.shape[0]
    m = kv.shape[0]

    def body(q_ref, kv_ref, do_ref, dq_ref, dkv_ref):
        i = pl.program_id(0)

        @pl.when(i == 0)
        def _():
            dkv_ref[...] = jnp.zeros_like(dkv_ref)

        for h in range(XA_HEADS):
            c0 = h * XA_DIM
            v0 = D_MODEL + c0
            qh = q_ref[:, c0:c0 + XA_DIM]
            kh = kv_ref[:, c0:c0 + XA_DIM]
            doh = do_ref[:, c0:c0 + XA_DIM]
            sc = _dot(kh, qh, NT) * XA_SCALE
            e = jnp.exp(sc - jnp.max(sc, axis=0, keepdims=True))
            p = e / jnp.sum(e, axis=0, keepdims=True)
            dp = _dot(kv_ref[:, v0:v0 + XA_DIM], doh, NT)
            ds = p * (dp - jnp.sum(p * dp, axis=0, keepdims=True))
            dsb = (ds * XA_SCALE).astype(BF16)
            dq_ref[:, c0:c0 + XA_DIM] = _dot(kh, dsb, TN).T.astype(BF16)
            dkv_ref[:, c0:c0 + XA_DIM] += _dot(dsb, qh, NN)
            dkv_ref[:, v0:v0 + XA_DIM] += _dot(p.astype(BF16), doh, NN)

    return pl.pallas_call(
        body, name=name, grid=(s // TM,),
        in_specs=[_row_spec(TM, D_MODEL), pl.BlockSpec((m, 2 * D_MODEL), lambda i: (0, 0)), _row_spec(TM, D_MODEL)],
        out_specs=[_row_spec(TM, D_MODEL), pl.BlockSpec((m, 2 * D_MODEL), lambda i: (0, 0))],
        out_shape=[jax.ShapeDtypeStruct((s, D_MODEL), BF16), jax.ShapeDtypeStruct((m, 2 * D_MODEL), F32)],
        compiler_params=_cp("arbitrary"),
    )(q2, kv, do2)


GELU_C = math.sqrt(2.0 / math.pi)
GELU_A = 0.044715


def _gelu(x):
    return 0.5 * x * (1.0 + jnp.tanh(GELU_C * (x + GELU_A * x * x * x)))


def _gelu_and_grad(x):
    t = jnp.tanh(GELU_C * (x + GELU_A * x * x * x))
    g = 0.5 * x * (1.0 + t)
    dg = 0.5 * (1.0 + t) + 0.5 * x * (1.0 - t * t) * GELU_C * (1.0 + 3.0 * GELU_A * x * x)
    return g, dg


def _conv(h, s1, s2, w_ref, b_ref):
    return w_ref[0:1, :] * s2 + w_ref[1:2, :] * s1 + w_ref[2:3, :] * h + b_ref[...]


def _shift_down(main, prev8):
    row = lax.broadcasted_iota(jnp.int32, main.shape, 0)
    s1 = jnp.where(row == 0, prev8[7:8, :], pltpu.roll(main, 1, 0))
    s2 = jnp.where(row == 0, prev8[6:7, :], jnp.where(row == 1, prev8[7:8, :], pltpu.roll(main, 2, 0)))
    return s1, s2


def _shift_up(main, next8):
    n = main.shape[0]
    row = lax.broadcasted_iota(jnp.int32, main.shape, 0)
    u1 = jnp.where(row == n - 1, next8[0:1, :], pltpu.roll(main, n - 1, 0))
    u2 = jnp.where(row == n - 2, next8[0:1, :], jnp.where(row == n - 1, next8[1:2, :], pltpu.roll(main, n - 2, 0)))
    return u1, u2


def _ffn_fwd(h3, w_up, cw, cb, w_down, x2, tgt, g_post, name):
    s = h3.shape[0]
    tn = TN_FF
    nj = D_FF // tn
    per = D_MODEL // tn
    hb = TM // 8

    def body(h_ref, halo_ref, wg_ref, wu_ref, cwg_ref, cwu_ref, cbg_ref, cbu_ref, wd_ref, x_ref, t_ref, g_ref,
             hg_ref, hu_ref, cg_ref, cu_ref, a_ref, loss_ref, dx_ref, dy_ref, dg_ref, y_acc):
        i, j = pl.program_id(0), pl.program_id(1)
        h = h_ref[...]
        halo = halo_ref[...]
        halo = jnp.where(i > 0, halo, jnp.zeros_like(halo))
        conv = []
        for w_ref, cw_ref, cb_ref, hid_ref, c_ref in ((wg_ref, cwg_ref, cbg_ref, hg_ref, cg_ref),
                                                      (wu_ref, cwu_ref, cbu_ref, hu_ref, cu_ref)):
            hm = _dot(h, w_ref[...], NN)
            hid_ref[...] = hm.astype(BF16)
            s1, s2 = _shift_down(hm, _dot(halo, w_ref[...], NN))
            c = _conv(hm, s1, s2, cw_ref, cb_ref)
            c_ref[...] = c.astype(BF16)
            conv.append(c)
        a = (_gelu(conv[0]) * conv[1]).astype(BF16)
        a_ref[...] = a
        contrib = _dot(a, wd_ref[...], NN)

        @pl.when(j == 0)
        def _():
            y_acc[...] = contrib

        @pl.when(j > 0)
        def _():
            y_acc[...] += contrib

        @pl.when(j == nj - 1)
        def _():
            yv = y_acc[...]
            r = _rstd(yv)
            yn = yv * r
            e = x_ref[...] + yn * g_ref[...] - t_ref[...]
            part = 0.5 * jnp.sum(jnp.mean(e * e, axis=-1, keepdims=True), axis=0, keepdims=True)
            part = jnp.broadcast_to(part, (1, 128))
            dx = e * (1.0 / D_MODEL)
            dx_ref[...] = dx
            dy_ref[...] = _norm_bwd_rows(dx * g_ref[...], yn, r).astype(BF16)
            dg = jnp.sum(dx * yn, axis=0, keepdims=True)

            @pl.when(i == 0)
            def _():
                dg_ref[...] = dg
                loss_ref[...] = part

            @pl.when(i > 0)
            def _():
                dg_ref[...] += dg
                loss_ref[...] += part

    rows = pl.BlockSpec((TM, D_MODEL), lambda i, j: (i, 0))
    tile = pl.BlockSpec((TM, tn), lambda i, j: (i, j))
    wide = jax.ShapeDtypeStruct((s, D_FF), BF16)
    return pl.pallas_call(
        body, name=name, grid=(s // TM, nj),
        in_specs=[rows,
                  pl.BlockSpec((8, D_MODEL), lambda i, j: (jnp.maximum(i * hb - 1, 0), 0)),
                  pl.BlockSpec((None, D_MODEL, tn), lambda i, j: (j // per, 0, j % per)),
                  pl.BlockSpec((None, D_MODEL, tn), lambda i, j: (NDEV // 2 + j // per, 0, j % per)),
                  pl.BlockSpec((8, tn), lambda i, j: (0, j)),
                  pl.BlockSpec((8, tn), lambda i, j: (0, nj + j)),
                  pl.BlockSpec((1, tn), lambda i, j: (0, j)),
                  pl.BlockSpec((1, tn), lambda i, j: (0, nj + j)),
                  pl.BlockSpec((tn, D_MODEL), lambda i, j: (j, 0)),
                  rows, rows, pl.BlockSpec((1, D_MODEL), lambda i, j: (0, 0))],
        out_specs=[tile, tile, tile, tile, tile,
                   pl.BlockSpec((1, 128), lambda i, j: (0, 0)), rows, rows,
                   pl.BlockSpec((1, D_MODEL), lambda i, j: (0, 0))],
        out_shape=[wide, wide, wide, wide, wide,
                   jax.ShapeDtypeStruct((1, 128), F32), jax.ShapeDtypeStruct((s, D_MODEL), F32),
                   jax.ShapeDtypeStruct((s, D_MODEL), BF16), jax.ShapeDtypeStruct((1, D_MODEL), F32)],
        scratch_shapes=[pltpu.VMEM((TM, D_MODEL), F32)],
        compiler_params=_cp("arbitrary", "arbitrary"),
    )(h3, h3, w_up, w_up, cw, cw, cb, cb, w_down, x2, tgt, g_post)


def _ffn_bwd(dy3, w_down, hid_g, hid_u, conv_g, conv_u, cw, name):
    s = dy3.shape[0]
    n = s // TM
    tn = TN_FF
    nj = D_FF // tn
    hb = TM // 8
    last8 = s // 8 - 1

    def body(dy_ref, dyn_ref, wd_ref, hg_ref, hu_ref, cg_ref, cgn_ref, cu_ref, cun_ref, cwg_ref, cwu_ref,
             dhg_ref, dhu_ref, dcwg_ref, dcwu_ref, dcbg_ref, dcbu_ref):
        i = pl.program_id(1)
        first, last = i == 0, i == n - 1
        da = _dot(dy_ref[...], wd_ref[...], NT)
        dyn = dyn_ref[...]
        dyn = jnp.where(last, jnp.zeros_like(dyn), dyn)
        da_n = _dot(dyn, wd_ref[...], NT)
        c_g, c_u = cg_ref[...].astype(F32), cu_ref[...].astype(F32)
        g, dg = _gelu_and_grad(c_g)
        gn, dgn = _gelu_and_grad(cgn_ref[...].astype(F32))
        outs = ((da * c_u * dg, da_n * cun_ref[...].astype(F32) * dgn, hg_ref, cwg_ref, dhg_ref, dcwg_ref, dcbg_ref),
                (da * g, da_n * gn, hu_ref, cwu_ref, dhu_ref, dcwu_ref, dcbu_ref))
        row8 = lax.broadcasted_iota(jnp.int32, (8, tn), 0)
        for dc, dcn, h_ref, cw_ref, dh_ref, dcw_ref, dcb_ref in outs:
            u1, u2 = _shift_up(dc, dcn)
            dh_ref[...] = (cw_ref[2:3, :] * dc + cw_ref[1:2, :] * u1 + cw_ref[0:1, :] * u2).astype(BF16)
            hm = h_ref[...].astype(F32)
            dcb = jnp.sum(dc, axis=0, keepdims=True)
            dcw = jnp.where(row8 == 0, jnp.sum(hm * u2, axis=0, keepdims=True),
                            jnp.where(row8 == 1, jnp.sum(hm * u1, axis=0, keepdims=True),
                                      jnp.where(row8 == 2, jnp.sum(hm * dc, axis=0, keepdims=True), 0.0)))

            @pl.when(first)
            def _():
                dcw_ref[...] = dcw
                dcb_ref[...] = dcb

            @pl.when(i > 0)
            def _():
                dcw_ref[...] += dcw
                dcb_ref[...] += dcb

    next8 = lambda j, i: (jnp.minimum((i + 1) * hb, last8), j)
    blk = lambda j, i: (i, j)
    col = lambda j, i: (0, j)
    colu = lambda j, i: (0, nj + j)
    tile = pl.BlockSpec((TM, tn), blk)
    return pl.pallas_call(
        body, name=name, grid=(nj, n),
        in_specs=[pl.BlockSpec((TM, D_MODEL), lambda j, i: (i, 0)),
                  pl.BlockSpec((8, D_MODEL), lambda j, i: (jnp.minimum((i + 1) * hb, last8), 0)),
                  pl.BlockSpec((tn, D_MODEL), lambda j, i: (j, 0)),
                  tile, tile, tile, pl.BlockSpec((8, tn), next8), tile, pl.BlockSpec((8, tn), next8),
                  pl.BlockSpec((8, tn), col), pl.BlockSpec((8, tn), colu)],
        out_specs=[tile, tile, pl.BlockSpec((8, tn), col), pl.BlockSpec((8, tn), col),
                   pl.BlockSpec((1, tn), col), pl.BlockSpec((1, tn), col)],
        out_shape=[jax.ShapeDtypeStruct((s, D_FF), BF16), jax.ShapeDtypeStruct((s, D_FF), BF16),
                   jax.ShapeDtypeStruct((8, D_FF), F32), jax.ShapeDtypeStruct((8, D_FF), F32),
                   jax.ShapeDtypeStruct((1, D_FF), F32), jax.ShapeDtypeStruct((1, D_FF), F32)],
        compiler_params=_cp("parallel", "arbitrary"),
    )(dy3, dy3, w_down, hid_g, hid_u, conv_g, conv_g, conv_u, conv_u, cw, cw)


def _slot(p):
    return 4 * p[0] + 2 * p[1] + p[2]


def _all_gather(shards, name):
    n = len(shards)

    def body(*refs):
        ins, outs = refs[:n], refs[n:2 * n]
        send_sems, recv_sems, local_sems = refs[2 * n:]
        x, y, c = lax.axis_index("x"), lax.axis_index("y"), lax.axis_index("c")
        me, sibling = (x, y, c), (x, y, 1 - c)
        chips = [(1 - x, y), (x, 1 - y), (1 - x, 1 - y)]

        def copy(a, k, block, to, from_input=False):
            dst = outs[a].at[_slot(block)]
            return pltpu.make_async_remote_copy(
                src_ref=ins[a] if from_input else dst, dst_ref=dst,
                send_sem=send_sems.at[a, k], recv_sem=recv_sems.at[a, k],
                device_id=to, device_id_type=MESH)

        mine = [pltpu.make_async_copy(ins[a], outs[a].at[_slot(me)], local_sems.at[a]) for a in range(n)]
        for cp in mine:
            cp.start()
        first = []
        for a in range(n):
            first.append(copy(a, 0, me, sibling, True))
            first += [copy(a, 1 + j, me, (*chip, c), True) for j, chip in enumerate(chips)]
        for cp in first:
            cp.start()
        passed = []
        for j, chip in enumerate(chips):
            for a in range(n):
                copy(a, 1 + j, (*chip, c), me).wait_recv()
                fwd = copy(a, 4 + j, (*chip, c), sibling)
                fwd.start()
                passed.append(fwd)
        for a in range(n):
            copy(a, 0, sibling, me).wait_recv()
            for j, chip in enumerate(chips):
                copy(a, 4 + j, (*chip, 1 - c), me).wait_recv()
        for cp in first + passed:
            cp.wait_send()
        for cp in mine:
            cp.wait()

    any_spec = pl.BlockSpec(memory_space=pl.ANY)
    return pl.pallas_call(
        body, name=name,
        in_specs=[any_spec] * n, out_specs=[any_spec] * n,
        out_shape=[jax.ShapeDtypeStruct((NDEV,) + s.shape, s.dtype) for s in shards],
        scratch_shapes=[pltpu.SemaphoreType.DMA((n, 7)), pltpu.SemaphoreType.DMA((n, 7)),
                        pltpu.SemaphoreType.DMA((n,))],
    )(*shards)


def _peer_list(x, y, c):
    return [(1 - x if m & 4 else x, 1 - y if m & 2 else y, 1 - c if m & 1 else c) for m in range(1, NDEV)]


def _exchange_copies(src_refs, land_refs, send_sems, recv_sems, gather):
    x, y, c = lax.axis_index("x"), lax.axis_index("y"), lax.axis_index("c")
    me = (x, y, c)
    copies = []
    for m, peer in enumerate(_peer_list(x, y, c)):
        for a in range(len(src_refs)):
            copies.append(pltpu.make_async_remote_copy(
                src_ref=src_refs[a] if gather else src_refs[a].at[_slot(peer)], dst_ref=land_refs[a].at[_slot(me)],
                send_sem=send_sems.at[a * (NDEV - 1) + m], recv_sem=recv_sems.at[a * (NDEV - 1) + m],
                device_id=peer, device_id_type=MESH))
    return copies


def _all_gather_small(shards, name):
    n = len(shards)

    def body(*refs):
        ins, outs = refs[:n], refs[n:2 * n]
        send_sems, recv_sems, local_sems = refs[2 * n:]
        me = (lax.axis_index("x"), lax.axis_index("y"), lax.axis_index("c"))
        mine = [pltpu.make_async_copy(ins[a], outs[a].at[_slot(me)], local_sems.at[a]) for a in range(n)]
        copies = _exchange_copies(ins, outs, send_sems, recv_sems, True)
        for cp in mine + copies:
            cp.start()
        for cp in copies + mine:
            cp.wait()

    any_spec = pl.BlockSpec(memory_space=pl.ANY)
    return pl.pallas_call(
        body, name=name,
        in_specs=[any_spec] * n, out_specs=[any_spec] * n,
        out_shape=[jax.ShapeDtypeStruct((NDEV,) + s.shape, s.dtype) for s in shards],
        scratch_shapes=[pltpu.SemaphoreType.DMA((n * (NDEV - 1),)), pltpu.SemaphoreType.DMA((n * (NDEV - 1),)),
                        pltpu.SemaphoreType.DMA((n,))],
    )(*shards)


def _exchange_start(srcs, lands, after, gather, name):
    n = len(srcs)
    hbm = pl.BlockSpec(memory_space=pltpu.HBM)

    def body(*refs):
        for cp in _exchange_copies(refs[:n], refs[n:2 * n], refs[2 * n + 1], refs[2 * n + 2], gather):
            cp.start()
        token = refs[-1]
        token[...] = jnp.zeros_like(token)

    outs = pl.pallas_call(
        body, name=name,
        out_shape=(pltpu.SemaphoreType.DMA((n * (NDEV - 1),)), pltpu.SemaphoreType.DMA((n * (NDEV - 1),)),
                   *[pltpu.HBM(a.shape, a.dtype) for a in list(srcs) + list(lands)],
                   jax.ShapeDtypeStruct((8, 128), F32)),
        in_specs=[hbm] * (2 * n) + [pl.BlockSpec(memory_space=pl.ANY)],
        out_specs=(pl.BlockSpec(memory_space=pltpu.SEMAPHORE), pl.BlockSpec(memory_space=pltpu.SEMAPHORE),
                   *[hbm] * (2 * n), pl.BlockSpec(memory_space=pltpu.VMEM)),
        input_output_aliases={i: 2 + i for i in range(2 * n)},
        compiler_params=pltpu.CompilerParams(has_side_effects=pltpu.SideEffectType.DATAFLOW_SIDE_EFFECTING),
    )(*[pltpu.with_memory_space_constraint(a, pltpu.HBM) for a in list(srcs) + list(lands)], after)
    return outs[0], outs[1], outs[2:2 + n], outs[2 + n:2 + 2 * n], outs[-1]


def _exchange_wait(send_sems, recv_sems, srcs, lands, after, gather, name):
    n = len(srcs)
    hbm = pl.BlockSpec(memory_space=pltpu.HBM)

    def body(*refs):
        for cp in _exchange_copies(refs[:n], refs[n:2 * n], refs[2 * n], refs[2 * n + 1], gather):
            cp.wait_send()
            cp.wait_recv()

    outs = pl.pallas_call(
        body, name=name,
        out_shape=tuple(pltpu.HBM(a.shape, a.dtype) for a in list(srcs) + list(lands)),
        in_specs=[hbm] * (2 * n) + [pl.BlockSpec(memory_space=pltpu.SEMAPHORE)] * 2 + [pl.BlockSpec(memory_space=pl.ANY)],
        out_specs=tuple([hbm] * (2 * n)),
        input_output_aliases={i: i for i in range(2 * n)},
        compiler_params=pltpu.CompilerParams(has_side_effects=pltpu.SideEffectType.DATAFLOW_SIDE_EFFECTING),
    )(*srcs, *lands, send_sems, recv_sems, after)
    return outs[n:]


def _own_slot(block):
    me = 4 * lax.axis_index("x") + 2 * lax.axis_index("y") + lax.axis_index("c")
    return lax.dynamic_update_slice(lax.empty((NDEV,) + block.shape, block.dtype), block[None], (me, 0, 0))


def _adam_update(p_ref, w_ref, m_ref, v_ref, g_ref, d_ref, mo_ref, vo_ref):
    bc1 = 1.0 - ADAM_B1 ** ADAM_STEP
    bc2 = 1.0 - ADAM_B2 ** ADAM_STEP
    g = p_ref[0].astype(F32)
    for d in range(1, NDEV):
        g = g + p_ref[d].astype(F32)
    g_ref[...] = g
    mn = ADAM_B1 * m_ref[...] + (1.0 - ADAM_B1) * g
    vn = ADAM_B2 * v_ref[...] + (1.0 - ADAM_B2) * (g * g)
    mo_ref[...] = mn
    vo_ref[...] = vn
    d_ref[...] = -ADAM_LR * ((mn / bc1) / (jnp.sqrt(vn / bc2) + ADAM_EPS) + ADAM_WD * w_ref[...])


def _adamw_small(parts, ws, ms, vs, name):
    n = len(ws)

    def body(*refs):
        ins, outs = refs[:4 * n], refs[4 * n:]
        for k in range(n):
            _adam_update(ins[k], ins[n + k], ins[2 * n + k], ins[3 * n + k], *outs[4 * k:4 * k + 4])

    whole = pl.BlockSpec(memory_space=pltpu.VMEM)
    res = pl.pallas_call(
        body, name=name, in_specs=[whole] * (4 * n), out_specs=[whole] * (4 * n),
        out_shape=[jax.ShapeDtypeStruct(a.shape, F32) for a in ws for _ in range(4)],
    )(*parts, *ws, *ms, *vs)
    return [res[4 * k:4 * k + 4] for k in range(n)]


def _adamw(parts, w, m, v, name):
    r, c = w.shape
    tr = r if r * c <= 160 * 1024 else max(8, (160 * 1024 // c) // 8 * 8)
    while r % tr:
        tr -= 8
    body = functools.partial(_adam_update)
    spec = pl.BlockSpec((tr, c), lambda i: (i, 0))
    return pl.pallas_call(
        body, name=name, grid=(r // tr,),
        in_specs=[pl.BlockSpec((NDEV, tr, c), lambda i: (0, i, 0)), spec, spec, spec],
        out_specs=[spec] * 4, out_shape=[jax.ShapeDtypeStruct((r, c), F32)] * 4,
        compiler_params=_cp("parallel"),
    )(parts, w, m, v)


def _local_step(x, mem, tgt, gains, b_forget, w_pool, pool_scale, conv_b, w_in,
                mix_weights, ffn_weights, send_in_grad, send_mix_grads, send_ffn_grads):
    b_pad = jnp.pad(b_forget, ((0, 0), (0, 128 - FOX_HEADS)))
    wbd = jnp.zeros((D_POOL, D_POOL), F32)
    for g in range(4):
        wbd = wbd.at[64 * g:64 * g + 64, 64 * g:64 * g + 64].set(w_pool[g])
    wbd = wbd.astype(BF16)
    scale = pool_scale.reshape(1, D_POOL)

    h1, proj, fraw = _proj_in(x, gains["mix_pre"], w_in, "proj_in")
    flog, cum = _gate_cumsum(fraw, b_pad, "gate_cumsum")
    aq, ak = _fox_operands(cum, "fox_operands")
    ycat, aqb = _fox_fwd(proj, aq, ak, "fox_fwd")
    ycat = _pool_fwd(proj, wbd, scale, ycat, "pool_fwd")
    w_mix, w_xq, w_xo, w_xkv = mix_weights(ycat)
    y1, x1, h2 = _mm_rows(ycat, w_mix, "nn", 1024, "mix_out", [x], [gains["mix_post"], gains["xa_pre"]],
                          [F32, F32, BF16], _epi_resid)
    q2 = _mm(h2, w_xq, "nn", BF16, 1024, 1024, 1024, "xa_q")
    mem_n = _norm_fwd(mem, gains["mem"], "norm_mem")
    kv = _mm(mem_n, w_xkv, "nn", BF16, mem.shape[0], 256, 1024, "xa_kv", b_cols=256)
    o2 = _xattn_fwd(q2, kv, "xattn_fwd")
    y2, x2, h3 = _mm_rows(o2, w_xo, "nn", 1024, "xa_out", [x1], [gains["xa_post"], gains["ffn_pre"]],
                          [F32, F32, BF16], _epi_resid)
    w_up, w_down, cw = ffn_weights(h3)
    hid_g, hid_u, conv_g, conv_u, act, loss, dx3, dy3, dg_ffn_post = _ffn_fwd(
        h3, w_up, cw, conv_b, w_down, x2, tgt, gains["ffn_post"], "ffn_fwd")

    dhid_g, dhid_u, dcw_g, dcw_u, dcb_g, dcb_u = _ffn_bwd(dy3, w_down, hid_g, hid_u, conv_g, conv_u, cw, "ffn_bwd")
    d_w_down = _mm(act, dy3, "tn", BF16, 1024, 1024, 2048, "dw_down")
    d_w_up = _mm(h3, [dhid_g, dhid_u], "tn", BF16, 1024, 1024, 2048, "dw_up", out_cols=1024)
    sent = send_ffn_grads(d_w_up, d_w_down, jnp.concatenate([dcw_g, dcw_u], axis=1))
    dh3 = _mm([dhid_g, dhid_u], w_up, "nt", F32, 1024, 1024, 1024, "dh_ffn", b_cols=1024, after=sent)
    dx2, dg_ffn_pre, dy2, dg_xa_post = _norm_bwd(dh3, x2, dx3, gains["ffn_pre"], "norm_bwd_ffn",
                                                 prev=(y2, gains["xa_post"]))
    do2 = _mm(dy2, w_xo, "nt", BF16, 1024, 1024, 1024, "d_xa_out")
    d_w_xo = _mm(o2, dy2, "tn", BF16, 1024, 1024, 1024, "dw_xo")
    dq2, dkv = _xattn_bwd(q2, kv, do2, "xattn_bwd")
    dkv = dkv.astype(BF16)
    dx1, dg_xa_pre, dy1, dg_mix_post = _mm_rows(
        dq2, w_xq, "nt", 1024, "dh_xa", [x1, dx2, y1], [gains["xa_pre"], gains["mix_post"]],
        [F32, "sum", BF16, "sum"], _epi_norm_bwd)
    d_w_xq = _mm(h2, dq2, "tn", BF16, 1024, 1024, 1024, "dw_xq")
    dmem_n = _mm(dkv, w_xkv, "nt", F32, mem.shape[0], 1024, 256, "d_mem", b_cols=256)
    d_w_xkv = _mm(mem_n, dkv, "tn", BF16, 1024, 256, mem.shape[0], "dw_xkv", out_cols=256)
    _, dg_mem = _norm_bwd(dmem_n, mem, jnp.zeros_like(mem), gains["mem"], "norm_bwd_mem")
    dycat = _mm(dy1, w_mix, "nt", BF16, 1024, 1024, 1024, "d_mix_out")
    d_w_mix = _mm(ycat, dy1, "tn", BF16, 1024, 1024, 1024, "dw_mix")
    sent_mix = send_mix_grads(d_w_mix, d_w_xq, d_w_xo, d_w_xkv)
    ad = _fox_do_operand(dycat, ycat, sent_mix, "fox_do_operand")
    dq, dk, dv, qaux, kaux = _fox_bwd(proj, dycat, aqb, ak, ad, "fox_bwd")
    du, d_wbd, d_scale = _pool_bwd(proj, dycat, wbd, scale, "pool_bwd")
    df, db_f = _gate_bwd(qaux, kaux, flog, "gate_bwd")
    dproj = [du, dq, dk, dv, df]
    sent_in = send_in_grad(_dw_in(h1, dproj, "dw_in"))
    grad_x, dg_mix_pre = _mm_rows(dproj, w_in, "nt", None, "dh_mix", [x, dx1], [gains["mix_pre"]],
                                  [F32, "sum"], _epi_norm_bwd, after=sent_in)

    small = dict(
        mix_pre=dg_mix_pre, mix_post=dg_mix_post, mem=dg_mem, xa_pre=dg_xa_pre, xa_post=dg_xa_post,
        ffn_pre=dg_ffn_pre, ffn_post=dg_ffn_post,
        conv_b=jnp.concatenate([dcb_g, dcb_u], axis=1),
        w_pool=jnp.concatenate([d_wbd[64 * g:64 * g + 64, 64 * g:64 * g + 64] for g in range(4)], axis=0),
        pool_scale=d_scale.reshape(4, 64),
        b_forget=db_f[:, :FOX_HEADS],
    )
    return loss, grad_x, small


SMALL_ORDER = ("mix_pre", "mix_post", "mem", "xa_pre", "xa_post", "ffn_pre", "ffn_post", "conv_b",
               "w_pool", "pool_scale", "b_forget")


def kernel(x, mem, norm_mix_pre, norm_mix_post, w_in, b_forget, w_pool, pool_scale, w_mix_out, norm_mem, norm_xa_pre, norm_xa_post, w_xq, w_xkv, w_xo, norm_ffn_pre, norm_ffn_post, w_up, conv_w, conv_b, w_down, loss_target, m_norm_mix_pre, m_norm_mix_post, m_w_in, m_b_forget, m_w_pool, m_pool_scale, m_w_mix_out, m_norm_mem, m_norm_xa_pre, m_norm_xa_post, m_w_xq, m_w_xkv, m_w_xo, m_norm_ffn_pre, m_norm_ffn_post, m_w_up, m_conv_w, m_conv_b, m_w_down, v_norm_mix_pre, v_norm_mix_post, v_w_in, v_b_forget, v_w_pool, v_pool_scale, v_w_mix_out, v_norm_mem, v_norm_xa_pre, v_norm_xa_post, v_w_xq, v_w_xkv, v_w_xo, v_norm_ffn_pre, v_norm_ffn_post, v_w_up, v_conv_w, v_conv_b, v_w_down):
    names = ("norm_mix_pre", "norm_mix_post", "w_in", "b_forget", "w_pool", "pool_scale", "w_mix_out", "norm_mem",
             "norm_xa_pre", "norm_xa_post", "w_xq", "w_xkv", "w_xo", "norm_ffn_pre", "norm_ffn_post", "w_up",
             "conv_w", "conv_b", "w_down")
    w = dict(zip(names, (norm_mix_pre, norm_mix_post, w_in, b_forget, w_pool, pool_scale, w_mix_out, norm_mem,
                         norm_xa_pre, norm_xa_post, w_xq, w_xkv, w_xo, norm_ffn_pre, norm_ffn_post, w_up,
                         conv_w, conv_b, w_down)))
    mo = dict(zip(names, (m_norm_mix_pre, m_norm_mix_post, m_w_in, m_b_forget, m_w_pool, m_pool_scale, m_w_mix_out,
                          m_norm_mem, m_norm_xa_pre, m_norm_xa_post, m_w_xq, m_w_xkv, m_w_xo, m_norm_ffn_pre,
                          m_norm_ffn_post, m_w_up, m_conv_w, m_conv_b, m_w_down)))
    vo = dict(zip(names, (v_norm_mix_pre, v_norm_mix_post, v_w_in, v_b_forget, v_w_pool, v_pool_scale, v_w_mix_out,
                          v_norm_mem, v_norm_xa_pre, v_norm_xa_post, v_w_xq, v_w_xkv, v_w_xo, v_norm_ffn_pre,
                          v_norm_ffn_post, v_w_up, v_conv_w, v_conv_b, v_w_down)))

    big_names = ("w_in", "w_mix_out", "w_xq", "w_xo", "w_xkv", "w_up", "w_down")
    shards = {k: w[k][0].astype(BF16) for k in big_names}
    shards["w_in"] = jnp.pad(shards["w_in"], ((0, 0), (0, D_IN_PAD - shards["w_in"].shape[1])))
    conv_w_sh = jnp.pad(conv_w[0, :, 0, :], ((0, 5), (0, 0)))
    (g_in,) = _all_gather([shards["w_in"]], "gather_w_in")
    mix_srcs = [shards[k] for k in ("w_mix_out", "w_xq", "w_xo", "w_xkv")]
    mix_flight = _exchange_start(mix_srcs, [_own_slot(a) for a in mix_srcs], g_in, True, "gather_mix_start")
    ffn_srcs = [shards["w_up"], shards["w_down"], conv_w_sh]
    ffn_flight = _exchange_start(ffn_srcs, [_own_slot(a) for a in ffn_srcs], mix_flight[4], True, "gather_ffn_start")
    my_slot = 4 * lax.axis_index("x") + 2 * lax.axis_index("y") + lax.axis_index("c")
    own_block = lambda a: _own_slot(lax.dynamic_index_in_dim(a, my_slot, 0, keepdims=False))
    by_rows = lambda a: a.reshape(NDEV, a.shape[0] // NDEV, a.shape[1])
    by_cols = lambda a: a.reshape(a.shape[0], NDEV, a.shape[1] // NDEV).transpose(1, 0, 2)
    grad_flight = {}

    def mix_weights(after):
        g_mix, g_xq, g_xo, g_xkv = _exchange_wait(*mix_flight[:4], after, True, "gather_mix_wait")
        return (g_mix.reshape(D_MODEL, D_MODEL), g_xq.reshape(D_MODEL, D_MODEL), g_xo.reshape(D_MODEL, D_MODEL), g_xkv)

    def ffn_weights(after):
        g_up, g_down, g_cw = _exchange_wait(*ffn_flight[:4], after, True, "gather_ffn_wait")
        return g_up, g_down.reshape(D_FF, D_MODEL), g_cw.transpose(1, 0, 2).reshape(8, 2 * D_FF)

    def send_ffn_grads(d_w_up, d_w_down, d_cw):
        srcs = [d_w_up, by_rows(d_w_down), by_cols(d_cw)]
        grad_flight["ffn"] = _exchange_start(srcs, [own_block(a) for a in srcs], ffn_flight[4], False, "scatter_ffn_start")
        return grad_flight["ffn"][4]

    def send_mix_grads(d_w_mix, d_w_xq, d_w_xo, d_w_xkv):
        srcs = [by_rows(d_w_mix), by_rows(d_w_xq), by_rows(d_w_xo), d_w_xkv]
        grad_flight["mix"] = _exchange_start(srcs, [own_block(a) for a in srcs], ffn_flight[4], False, "scatter_mix_start")
        return grad_flight["mix"][4]

    def send_in_grad(d_w_in):
        srcs = [by_rows(d_w_in)]
        grad_flight["in"] = _exchange_start(srcs, [own_block(a) for a in srcs], ffn_flight[4], False, "scatter_in_start")
        return grad_flight["in"][4]

    gains = dict(mix_pre=norm_mix_pre + ffn_flight[4][0, 0], mix_post=norm_mix_post, mem=norm_mem, xa_pre=norm_xa_pre,
                 xa_post=norm_xa_post, ffn_pre=norm_ffn_pre, ffn_post=norm_ffn_post)
    loss, grad_x, small = _local_step(
        x[0], mem[0], loss_target[0], gains, b_forget, w_pool[0], pool_scale[0], conv_b,
        g_in.reshape(D_MODEL, D_IN_PAD), mix_weights, ffn_weights, send_in_grad, send_mix_grads, send_ffn_grads)

    p_up, p_down, p_cw = _exchange_wait(*grad_flight["ffn"][:4], grad_x, False, "scatter_ffn_wait")
    p_mix, p_xq, p_xo, p_xkv = _exchange_wait(*grad_flight["mix"][:4], grad_x, False, "scatter_mix_wait")
    parts = dict(w_mix_out=p_mix, w_xq=p_xq, w_xo=p_xo, w_xkv=p_xkv, w_up=p_up, w_down=p_down)
    *small_parts, loss_parts = _all_gather_small([small[k] for k in SMALL_ORDER] + [loss], "gather_small_grads")

    res = {k: [a[None] for a in _adamw(p, w[k][0], mo[k][0], vo[k][0], "adamw_" + k)] for k, p in parts.items()}
    pad_cw = lambda a: jnp.pad(a[0, :, 0, :], ((0, 5), (0, 0)))
    res["conv_w"] = [a[:3][None, :, None, :] for a in
                     _adamw(p_cw, pad_cw(conv_w), pad_cw(m_conv_w), pad_cw(v_conv_w), "adamw_conv_w")]
    key_of = dict(mix_pre="norm_mix_pre", mix_post="norm_mix_post", mem="norm_mem", xa_pre="norm_xa_pre",
                  xa_post="norm_xa_post", ffn_pre="norm_ffn_pre", ffn_post="norm_ffn_post", conv_b="conv_b",
                  w_pool="w_pool", pool_scale="pool_scale", b_forget="b_forget")
    flat2d = lambda src: [src[key_of[k]].reshape(small[k].shape) for k in SMALL_ORDER]
    small_out = _adamw_small(small_parts, flat2d(w), flat2d(mo), flat2d(vo), "adamw_small")
    for k, four in zip(SMALL_ORDER, small_out):
        res[key_of[k]] = [a.reshape(w[key_of[k]].shape) for a in four]
    (p_in,) = _exchange_wait(*grad_flight["in"][:4], res["w_up"][1], False, "scatter_in_wait")
    res["w_in"] = [a[None] for a in _adamw(p_in[:, :, :w_in.shape[2]], w["w_in"][0], mo["w_in"][0], vo["w_in"][0],
                                           "adamw_w_in")]

    outs = [jnp.sum(loss_parts[:, 0, 0]), grad_x[None]]
    for idx in range(4):
        outs += [res[k][idx] for k in names]
    return tuple(outs)
```

```python
import functools
import math

import jax
import jax.numpy as jnp
from jax import lax
from jax.experimental import pallas as pl
from jax.experimental.pallas import tpu as pltpu

F32 = jnp.float32
BF16 = jnp.bfloat16

NDEV = 8
D_MODEL = 1024
D_POOL = 256
D_FOX = 768
FOX_HEADS = 12
HEAD_PAIRS = FOX_HEADS // 2
XA_HEADS = 4
XA_DIM = 256
D_FF = 4096
D_IN_PAD = 2688
F_COL = 2560
POOL_HALO = 16
NORM_EPS = 1e-6
NEG = -1e30

ADAM_LR = 0.001
ADAM_B1 = 0.9
ADAM_B2 = 0.999
ADAM_EPS = 1e-08
ADAM_WD = 0.01
ADAM_STEP = 10

TM = 512
TQ = 512
TN_FF = 1024
VMEM_LIMIT = 56 * 1024 * 1024
MESH = pl.DeviceIdType.MESH


def _cp(*sem):
    return pltpu.CompilerParams(dimension_semantics=sem, vmem_limit_bytes=VMEM_LIMIT)


def _dot(a, b, dims):
    return lax.dot_general(a, b, (dims, ((), ())), preferred_element_type=F32)


NN = ((1,), (0,))
NT = ((1,), (1,))
TN = ((0,), (0,))


def _mm(a, b, mode, out_dtype, tm, tn, tk, name, b_cols=None, out_cols=None, after=None):
    a_list = list(a) if isinstance(a, (list, tuple)) else [a]
    b_list = list(b) if isinstance(b, (list, tuple)) else [b]
    assert len(a_list) == 1 or len(b_list) == 1
    if mode == "tn":
        K, M = a_list[0].shape
        assert len(a_list) == 1
        Ns = [x.shape[1] for x in b_list]
        N = sum(Ns)
        assert b_cols is None
    else:
        assert len(b_list) == 1
        M = a_list[0].shape[0]
        Ks = [x.shape[1] for x in a_list]
        K = sum(Ks)
        if b_cols is None:
            N = b_list[0].shape[0] if mode == "nt" else b_list[0].shape[1]
        else:
            N = b_list[0].shape[1] if mode == "nt" else NDEV * b_cols
    assert M % tm == 0 and N % tn == 0 and K % tk == 0, (name, M, N, K)
    grid = (M // tm, N // tn, K // tk)
    nk = grid[2]
    dims = {"nn": NN, "nt": NT, "tn": TN}[mode]

    in_specs = []
    if mode == "tn":
        in_specs.append(pl.BlockSpec((tk, tm), lambda i, j, k: (k, i)))
        if len(b_list) == 1:
            in_specs.append(pl.BlockSpec((tk, tn), lambda i, j, k: (k, j)))
        else:
            nj1 = Ns[0] // tn
            in_specs.append(pl.BlockSpec((tk, tn), lambda i, j, k: (k, jnp.minimum(j, nj1 - 1))))
            in_specs.append(pl.BlockSpec((tk, tn), lambda i, j, k: (k, jnp.maximum(j - nj1, 0))))
    else:
        if len(a_list) == 1:
            in_specs.append(pl.BlockSpec((tm, tk), lambda i, j, k: (i, k)))
        else:
            nk1 = Ks[0] // tk
            in_specs.append(pl.BlockSpec((tm, tk), lambda i, j, k: (i, jnp.minimum(k, nk1 - 1))))
            in_specs.append(pl.BlockSpec((tm, tk), lambda i, j, k: (i, jnp.maximum(k - nk1, 0))))
        if b_cols is None:
            if mode == "nn":
                in_specs.append(pl.BlockSpec((tk, tn), lambda i, j, k: (k, j)))
            else:
                in_specs.append(pl.BlockSpec((tn, tk), lambda i, j, k: (j, k)))
        else:
            if mode == "nn":
                per = b_cols // tn
                in_specs.append(pl.BlockSpec((None, tk, tn), lambda i, j, k: (j // per, k, j % per)))
            else:
                per = b_cols // tk
                in_specs.append(pl.BlockSpec((None, tn, tk), lambda i, j, k: (k // per, j, k % per)))
    if out_cols is None:
        out_spec = pl.BlockSpec((tm, tn), lambda i, j, k: (i, j))
        out_shape = jax.ShapeDtypeStruct((M, N), out_dtype)
    else:
        pero = out_cols // tn
        out_spec = pl.BlockSpec((None, tm, tn), lambda i, j, k: (j // pero, i, j % pero))
        out_shape = jax.ShapeDtypeStruct((NDEV, M, out_cols), out_dtype)

    two_a = len(a_list) == 2
    two_b = len(b_list) == 2
    extra = []
    if after is not None:
        in_specs.append(pl.BlockSpec(memory_space=pl.ANY))
        extra.append(after)

    def body(*refs):
        o_ref, acc_ref = refs[-2], refs[-1]
        j = pl.program_id(1)
        k = pl.program_id(2)

        @pl.when(k == 0)
        def _():
            acc_ref[...] = jnp.zeros_like(acc_ref)

        if two_a:
            a1, a2, b1 = refs[0], refs[1], refs[2]
            nk1_ = Ks[0] // tk

            @pl.when(k < nk1_)
            def _():
                acc_ref[...] += _dot(a1[...], b1[...], dims)

            @pl.when(k >= nk1_)
            def _():
                acc_ref[...] += _dot(a2[...], b1[...], dims)
        elif two_b:
            a1, b1, b2 = refs[0], refs[1], refs[2]
            nj1_ = Ns[0] // tn

            @pl.when(j < nj1_)
            def _():
                acc_ref[...] += _dot(a1[...], b1[...], dims)

            @pl.when(j >= nj1_)
            def _():
                acc_ref[...] += _dot(a1[...], b2[...], dims)
        else:
            acc_ref[...] += _dot(refs[0][...], refs[1][...], dims)

        @pl.when(k == nk - 1)
        def _():
            o_ref[...] = acc_ref[...].astype(o_ref.dtype)

    return pl.pallas_call(
        body, name=name, grid=grid, in_specs=in_specs, out_specs=out_spec, out_shape=out_shape,
        scratch_shapes=[pltpu.VMEM((tm, tn), F32)],
        compiler_params=_cp("parallel", "parallel", "arbitrary"),
    )(*a_list, *b_list, *extra)


def _rstd(x):
    return lax.rsqrt(jnp.mean(x * x, axis=-1, keepdims=True) + NORM_EPS)


def _norm_bwd_rows(dxn, xn, r):
    return r * (dxn - xn * jnp.mean(dxn * xn, axis=-1, keepdims=True))


def _row_spec(tm, d):
    return pl.BlockSpec((tm, d), lambda i: (i, 0))


def _vec_spec(d):
    return pl.BlockSpec((1, d), lambda i: (0, 0))


def _mm_rows(a, b, mode, tk, name, rows, vecs, outs, epilogue, b_cols=None, after=None):
    a_list = list(a) if isinstance(a, (list, tuple)) else [a]
    m = a_list[0].shape[0]
    ks = [x.shape[1] for x in a_list]
    n = D_MODEL
    pieces = tk is None
    nk = 1 if pieces else sum(ks) // tk
    dims = NN if mode == "nn" else NT
    if pieces:
        assert mode == "nt" and b_cols is None
        in_specs = [pl.BlockSpec((TM, kp), lambda i, k: (i, 0)) for kp in ks]
        tk = sum(ks)
    elif len(a_list) == 1:
        in_specs = [pl.BlockSpec((TM, tk), lambda i, k: (i, k))]
    else:
        nk1 = ks[0] // tk
        in_specs = [pl.BlockSpec((TM, tk), lambda i, k: (i, jnp.minimum(k, nk1 - 1))),
                    pl.BlockSpec((TM, tk), lambda i, k: (i, jnp.maximum(k - nk1, 0)))]
    if mode == "nn":
        in_specs.append(pl.BlockSpec((tk, n), lambda i, k: (k, 0)))
    elif b_cols is None:
        in_specs.append(pl.BlockSpec((n, tk), lambda i, k: (0, k)))
    else:
        per = b_cols // tk
        in_specs.append(pl.BlockSpec((None, n, tk), lambda i, k: (k // per, 0, k % per)))
    in_specs += [pl.BlockSpec((TM, n), lambda i, k: (i, 0))] * len(rows)
    in_specs += [pl.BlockSpec((1, n), lambda i, k: (0, 0))] * len(vecs)
    extra = []
    if after is not None:
        in_specs.append(pl.BlockSpec(memory_space=pl.ANY))
        extra.append(after)
    out_specs, out_shape = [], []
    for o in outs:
        if o == "sum":
            out_specs.append(pl.BlockSpec((1, n), lambda i, k: (0, 0)))
            out_shape.append(jax.ShapeDtypeStruct((1, n), F32))
        else:
            out_specs.append(pl.BlockSpec((TM, n), lambda i, k: (i, 0)))
            out_shape.append(jax.ShapeDtypeStruct((m, n), o))
    na, nr, nv = len(a_list), len(rows), len(vecs)

    def body(*refs):
        a_refs, b_ref = refs[:na], refs[na]
        row_refs = refs[na + 1:na + 1 + nr]
        vec_refs = refs[na + 1 + nr:na + 1 + nr + nv]
        out_refs = refs[len(refs) - 1 - len(outs):len(refs) - 1]
        acc_ref = refs[-1]
        i, k = pl.program_id(0), pl.program_id(1)

        @pl.when(k == 0)
        def _():
            acc_ref[...] = jnp.zeros_like(acc_ref)

        if pieces:
            off = 0
            for a_ref in a_refs:
                kp = a_ref.shape[1]
                acc_ref[...] += _dot(a_ref[...], b_ref[:, off:off + kp], dims)
                off += kp
        elif na == 1:
            acc_ref[...] += _dot(a_refs[0][...], b_ref[...], dims)
        else:
            nk1_ = ks[0] // tk

            @pl.when(k < nk1_)
            def _():
                acc_ref[...] += _dot(a_refs[0][...], b_ref[...], dims)

            @pl.when(k >= nk1_)
            def _():
                acc_ref[...] += _dot(a_refs[1][...], b_ref[...], dims)

        @pl.when(k == nk - 1)
        def _():
            vals = epilogue(acc_ref[...], [r[...] for r in row_refs], [v[...] for v in vec_refs])
            for o, ref, val in zip(outs, out_refs, vals):
                if o == "sum":
                    @pl.when(i == 0)
                    def _():
                        ref[...] = val

                    @pl.when(i > 0)
                    def _():
                        ref[...] += val
                else:
                    ref[...] = val.astype(o)

    return pl.pallas_call(
        body, name=name, grid=(m // TM, nk), in_specs=in_specs, out_specs=out_specs, out_shape=out_shape,
        scratch_shapes=[pltpu.VMEM((TM, n), F32)],
        compiler_params=_cp("arbitrary", "arbitrary"),
    )(*a_list, b, *rows, *vecs, *extra)


def _proj_in(x, g, w_in, name):
    s, d = x.shape
    n = w_in.shape[1]

    def body(x_ref, g_ref, w_ref, h_ref, p_ref, f_ref):
        xv = x_ref[...]
        h = (xv * _rstd(xv) * g_ref[...]).astype(BF16)
        h_ref[...] = h
        acc = _dot(h, w_ref[...], NN)
        p_ref[...] = acc.astype(BF16)
        f_ref[...] = acc[:, F_COL:]

    return pl.pallas_call(
        body, name=name, grid=(s // TM,),
        in_specs=[_row_spec(TM, d), _vec_spec(d), pl.BlockSpec((d, n), lambda i: (0, 0))],
        out_specs=[_row_spec(TM, d), _row_spec(TM, n), _row_spec(TM, n - F_COL)],
        out_shape=[jax.ShapeDtypeStruct((s, d), BF16), jax.ShapeDtypeStruct((s, n), BF16),
                   jax.ShapeDtypeStruct((s, n - F_COL), F32)],
        compiler_params=_cp("parallel"),
    )(x, g, w_in)


def _dw_in(h, pieces, name):
    s, d = h.shape
    n = sum(p.shape[1] for p in pieces)
    tk = 1024
    nk = s // tk

    def body(*refs):
        h_ref, piece_refs, o_ref, acc_ref = refs[0], refs[1:-2], refs[-2], refs[-1]
        k = pl.program_id(1)

        @pl.when(k == 0)
        def _():
            acc_ref[...] = jnp.zeros_like(acc_ref)

        off = 0
        for p_ref in piece_refs:
            w = p_ref.shape[1]
            acc_ref[:, off:off + w] += _dot(h_ref[...], p_ref[...], TN)
            off += w

        @pl.when(k == nk - 1)
        def _():
            o_ref[...] = acc_ref[...].astype(BF16)

    return pl.pallas_call(
        body, name=name, grid=(d // TM, nk),
        in_specs=[pl.BlockSpec((tk, TM), lambda i, k: (k, i))] +
                 [pl.BlockSpec((tk, p.shape[1]), lambda i, k: (k, 0)) for p in pieces],
        out_specs=pl.BlockSpec((TM, n), lambda i, k: (i, 0)),
        out_shape=jax.ShapeDtypeStruct((d, n), BF16),
        scratch_shapes=[pltpu.VMEM((TM, n), F32)],
        compiler_params=_cp("parallel", "arbitrary"),
    )(h, *pieces)


def _epi_resid(y, rows, vecs):
    (x_in,), (g_post, g_next) = rows, vecs
    xo = x_in + y * _rstd(y) * g_post
    return y, xo, xo * _rstd(xo) * g_next


def _epi_norm_bwd(dh, rows, vecs):
    x, dx_res = rows[0], rows[1]
    r = _rstd(x)
    xn = x * r
    dx = dx_res + _norm_bwd_rows(dh * vecs[0], xn, r)
    res = [dx, jnp.sum(dh * xn, axis=0, keepdims=True)]
    if len(rows) == 3:
        y = rows[2]
        r2 = _rstd(y)
        yn = y * r2
        res += [_norm_bwd_rows(dx * vecs[1], yn, r2), jnp.sum(dx * yn, axis=0, keepdims=True)]
    return res


def _norm_fwd(x, g, name):
    s, d = x.shape
    tm = min(TM, s)

    def body(x_ref, g_ref, h_ref):
        xv = x_ref[...]
        h_ref[...] = (xv * _rstd(xv) * g_ref[...]).astype(BF16)

    return pl.pallas_call(
        body, name=name, grid=(s // tm,), in_specs=[_row_spec(tm, d), _vec_spec(d)],
        out_specs=_row_spec(tm, d), out_shape=jax.ShapeDtypeStruct((s, d), BF16),
        compiler_params=_cp("parallel"),
    )(x, g)


def _norm_bwd(dh, x, dx_res, g_pre, name, prev=None):
    s, d = x.shape
    tm = min(TM, s)
    has_prev = prev is not None

    def body(*refs):
        if has_prev:
            dh_ref, x_ref, dr_ref, g_ref, y_ref, gp_ref, dx_ref, dg_ref, dy_ref, dgp_ref = refs
        else:
            dh_ref, x_ref, dr_ref, g_ref, dx_ref, dg_ref = refs
        i = pl.program_id(0)
        xv = x_ref[...]
        r = _rstd(xv)
        xn = xv * r
        dhv = dh_ref[...].astype(F32)
        dx = dr_ref[...] + _norm_bwd_rows(dhv * g_ref[...], xn, r)
        dx_ref[...] = dx
        dg = jnp.sum(dhv * xn, axis=0, keepdims=True)

        @pl.when(i == 0)
        def _():
            dg_ref[...] = dg

        @pl.when(i > 0)
        def _():
            dg_ref[...] += dg

        if has_prev:
            yv = y_ref[...]
            r2 = _rstd(yv)
            yn = yv * r2
            dy_ref[...] = _norm_bwd_rows(dx * gp_ref[...], yn, r2).astype(BF16)
            dgp = jnp.sum(dx * yn, axis=0, keepdims=True)

            @pl.when(i == 0)
            def _():
                dgp_ref[...] = dgp

            @pl.when(i > 0)
            def _():
                dgp_ref[...] += dgp

    in_specs = [_row_spec(tm, d), _row_spec(tm, d), _row_spec(tm, d), _vec_spec(d)]
    out_specs = [_row_spec(tm, d), _vec_spec(d)]
    out_shape = [jax.ShapeDtypeStruct((s, d), F32), jax.ShapeDtypeStruct((1, d), F32)]
    args = [dh, x, dx_res, g_pre]
    if has_prev:
        in_specs += [_row_spec(tm, d), _vec_spec(d)]
        out_specs += [_row_spec(tm, d), _vec_spec(d)]
        out_shape += [jax.ShapeDtypeStruct((s, d), BF16), jax.ShapeDtypeStruct((1, d), F32)]
        args += list(prev)
    return pl.pallas_call(
        body, name=name, grid=(s // tm,), in_specs=in_specs, out_specs=out_specs, out_shape=out_shape,
        compiler_params=_cp("arbitrary"),
    )(*args)


def _split3(v):
    hi = v.astype(BF16)
    r1 = v - hi.astype(F32)
    mid = r1.astype(BF16)
    lo = (r1 - mid.astype(F32)).astype(BF16)
    return hi, mid, lo


def _tri_dot(tri, v):
    hi, mid, lo = _split3(v)
    return _dot(tri, hi, NN) + _dot(tri, mid, NN) + _dot(tri, lo, NN)


def _gate_cumsum(fraw, b_pad, name):
    s = fraw.shape[0]

    def body(f_ref, b_ref, flog_ref, cum_ref, carry_ref):
        i = pl.program_id(0)

        @pl.when(i == 0)
        def _():
            carry_ref[...] = jnp.zeros_like(carry_ref)

        flog = f_ref[...] + b_ref[...]
        flog_ref[...] = flog
        lf = jnp.minimum(flog, 0.0) - jnp.log(1.0 + jnp.exp(-jnp.abs(flog)))
        lane = lax.broadcasted_iota(jnp.int32, (1, 128), 1)
        lf = jnp.where(lane < FOX_HEADS, lf, 0.0)
        row = lax.broadcasted_iota(jnp.int32, (TM, TM), 0)
        col = lax.broadcasted_iota(jnp.int32, (TM, TM), 1)
        tri = (row >= col).astype(BF16)
        cum = _tri_dot(tri, lf) + carry_ref[...]
        cum_ref[...] = cum
        carry_ref[...] = cum[TM - 1:TM, :]

    return pl.pallas_call(
        body, name=name, grid=(s // TM,),
        in_specs=[_row_spec(TM, 128), _vec_spec(128)],
        out_specs=[_row_spec(TM, 128), _row_spec(TM, 128)],
        out_shape=[jax.ShapeDtypeStruct((s, 128), F32), jax.ShapeDtypeStruct((s, 128), F32)],
        scratch_shapes=[pltpu.VMEM((1, 128), F32)],
        compiler_params=_cp("arbitrary"),
    )(fraw, b_pad)


def _gate_bwd(qaux, kaux, flog, name):
    s = flog.shape[0]
    n = s // TM

    def body(qa_ref, ka_ref, fl_ref, dp_ref, db_ref, carry_ref):
        i = pl.program_id(0)

        @pl.when(i == 0)
        def _():
            carry_ref[...] = jnp.zeros_like(carry_ref)

        src = lax.broadcasted_iota(jnp.int32, (128, 128), 0)
        dst = lax.broadcasted_iota(jnp.int32, (128, 128), 1)
        dcum = jnp.zeros((TM, 128), F32)
        for p in range(HEAD_PAIRS):
            for ref, l0, l1, sign in ((qa_ref, 64, 0, 1.0), (ka_ref, 67, 3, -1.0)):
                hit = jnp.logical_or(jnp.logical_and(src == l0, dst == 2 * p),
                                     jnp.logical_and(src == l1, dst == 2 * p + 1))
                sel = jnp.where(hit, sign, 0.0).astype(BF16)
                for piece in _split3(ref[p]):
                    dcum = dcum + _dot(piece, sel, NN)
        row = lax.broadcasted_iota(jnp.int32, (TM, TM), 0)
        col = lax.broadcasted_iota(jnp.int32, (TM, TM), 1)
        tri = (row <= col).astype(BF16)
        dlf = _tri_dot(tri, dcum) + carry_ref[...]
        carry_ref[...] = dlf[0:1, :]
        lane = lax.broadcasted_iota(jnp.int32, (1, 128), 1)
        df = jnp.where(lane < FOX_HEADS, dlf / (1.0 + jnp.exp(fl_ref[...])), 0.0)
        dp_ref[...] = df.astype(BF16)
        db = jnp.sum(df, axis=0, keepdims=True)

        @pl.when(i == 0)
        def _():
            db_ref[...] = db

        @pl.when(i > 0)
        def _():
            db_ref[...] += db

    rev = lambda i: (n - 1 - i, 0)
    return pl.pallas_call(
        body, name=name, grid=(n,),
        in_specs=[pl.BlockSpec((HEAD_PAIRS, TM, 128), lambda i: (0, n - 1 - i, 0)),
                  pl.BlockSpec((HEAD_PAIRS, TM, 128), lambda i: (0, n - 1 - i, 0)), pl.BlockSpec((TM, 128), rev)],
        out_specs=[pl.BlockSpec((TM, 128), rev), _vec_spec(128)],
        out_shape=[jax.ShapeDtypeStruct((s, 128), BF16), jax.ShapeDtypeStruct((1, 128), F32)],
        scratch_shapes=[pltpu.VMEM((1, 128), F32)],
        compiler_params=_cp("arbitrary"),
    )(qaux, kaux, flog)


def _pool_consts(i, rows):
    lane = lax.broadcasted_iota(jnp.int32, (rows, D_POOL), 1)
    t1 = lax.broadcasted_iota(jnp.int32, (rows, D_POOL), 0) + i * TM + 1
    win = jnp.where(lane < 64, 2, jnp.where(lane < 128, 4, jnp.where(lane < 192, 8, 16)))
    inv = 1.0 / jnp.minimum(t1, win).astype(F32)
    return lane, inv


def _by_group(lane, s2, s4, s8, s16):
    return jnp.where(lane < 64, s2, jnp.where(lane < 128, s4, jnp.where(lane < 192, s8, s16)))


def _pool_diff(i, u_ref, halo_ref):
    u = u_ref[...].astype(F32)
    halo = jnp.where(i > 0, halo_ref[...].astype(F32), 0.0)
    ext = jnp.concatenate([halo, u], axis=0)
    s2 = ext + pltpu.roll(ext, 1, 0)
    s4 = s2 + pltpu.roll(s2, 2, 0)
    s8 = s4 + pltpu.roll(s4, 4, 0)
    s16 = s8 + pltpu.roll(s8, 8, 0)
    lane, inv = _pool_consts(i, TM)
    sel = _by_group(lane, s2[POOL_HALO:], s4[POOL_HALO:], s8[POOL_HALO:], s16[POOL_HALO:])
    return sel * inv - u


def _pool_fwd(proj, wbd, scale, ycat, name):
    s = proj.shape[0]
    hb = TM // POOL_HALO

    def body(u_ref, halo_ref, w_ref, sc_ref, y_any, y_ref):
        del y_any
        i = pl.program_id(0)
        diff = _pool_diff(i, u_ref, halo_ref)
        mixed = _dot(diff.astype(BF16), w_ref[...], NN)
        y_ref[...] = (mixed * sc_ref[...]).astype(BF16)

    return pl.pallas_call(
        body, name=name, grid=(s // TM,),
        in_specs=[pl.BlockSpec((TM, D_POOL), lambda i: (i, 0)),
                  pl.BlockSpec((POOL_HALO, D_POOL), lambda i: (jnp.maximum(i * hb - 1, 0), 0)),
                  pl.BlockSpec((D_POOL, D_POOL), lambda i: (0, 0)), _vec_spec(D_POOL),
                  pl.BlockSpec(memory_space=pl.ANY)],
        out_specs=pl.BlockSpec((TM, D_POOL), lambda i: (i, 0)),
        out_shape=jax.ShapeDtypeStruct(ycat.shape, ycat.dtype),
        input_output_aliases={4: 0},
        compiler_params=_cp("parallel"),
    )(proj, proj, wbd, scale, ycat)


def _pool_bwd(proj, dycat, wbd, scale, name):
    s = proj.shape[0]
    n = s // TM
    hb = TM // POOL_HALO
    last_halo = s // POOL_HALO - 1

    def body(u_ref, halo_ref, dy_ref, dyp_ref, w_ref, sc_ref, dp_ref, dw_ref, dsc_ref):
        i = pl.program_id(0)
        diff = _pool_diff(i, u_ref, halo_ref)
        diff_b = diff.astype(BF16)
        mixed = _dot(diff_b, w_ref[...], NN)
        dy = dy_ref[...].astype(F32)
        dmix = (dy * sc_ref[...]).astype(BF16)
        dyp = jnp.where(i < n - 1, dyp_ref[...].astype(F32), 0.0)
        dmix_p = (dyp * sc_ref[...]).astype(BF16)
        dd = _dot(dmix, w_ref[...], NT)
        dd_p = _dot(dmix_p, w_ref[...], NT)
        lane, inv = _pool_consts(i, TM)
        _, inv_p = _pool_consts(i + 1, POOL_HALO)
        ext = jnp.concatenate([dd * inv, dd_p * inv_p], axis=0)
        rows = TM + POOL_HALO
        l2 = ext + pltpu.roll(ext, rows - 1, 0)
        l4 = l2 + pltpu.roll(l2, rows - 2, 0)
        l8 = l4 + pltpu.roll(l4, rows - 4, 0)
        l16 = l8 + pltpu.roll(l8, rows - 8, 0)
        du = _by_group(lane, l2[:TM], l4[:TM], l8[:TM], l16[:TM]) - dd
        dp_ref[...] = du.astype(BF16)
        dw = _dot(diff_b, dmix, TN)
        dsc = jnp.sum(dy * mixed, axis=0, keepdims=True)

        @pl.when(i == 0)
        def _():
            dw_ref[...] = dw
            dsc_ref[...] = dsc

        @pl.when(i > 0)
        def _():
            dw_ref[...] += dw
            dsc_ref[...] += dsc

    return pl.pallas_call(
        body, name=name, grid=(n,),
        in_specs=[pl.BlockSpec((TM, D_POOL), lambda i: (i, 0)),
                  pl.BlockSpec((POOL_HALO, D_POOL), lambda i: (jnp.maximum(i * hb - 1, 0), 0)),
                  pl.BlockSpec((TM, D_POOL), lambda i: (i, 0)),
                  pl.BlockSpec((POOL_HALO, D_POOL), lambda i: (jnp.minimum((i + 1) * hb, last_halo), 0)),
                  pl.BlockSpec((D_POOL, D_POOL), lambda i: (0, 0)), _vec_spec(D_POOL)],
        out_specs=[pl.BlockSpec((TM, D_POOL), lambda i: (i, 0)),
                   pl.BlockSpec((D_POOL, D_POOL), lambda i: (0, 0)), _vec_spec(D_POOL)],
        out_shape=[jax.ShapeDtypeStruct((s, D_POOL), BF16),
                   jax.ShapeDtypeStruct((D_POOL, D_POOL), F32), jax.ShapeDtypeStruct((1, D_POOL), F32)],
        compiler_params=_cp("arbitrary"),
    )(proj, proj, dycat, dycat, wbd, scale)


Q_BLK = D_POOL // 128
K_BLK = Q_BLK + D_FOX // 128
V_BLK = K_BLK + D_FOX // 128


def _operand_rows(v0, v1, ones_off):
    row = lax.broadcasted_iota(jnp.int32, (128, 1), 0)
    half = row & 63
    out = jnp.where(jnp.logical_and(half >= ones_off, half < ones_off + 3), 1.0, 0.0) + jnp.zeros_like(v0)
    for base, v in ((64, v0), (0, v1)):
        for j, piece in enumerate(_split3(v)):
            out = jnp.where(row == base + j, piece.astype(F32), out)
    return out


def _fox_operands(cum, name):
    s = cum.shape[0]
    width = HEAD_PAIRS * 128

    def body(c_ref, aq_ref, ak_ref):
        pieces = _split3(c_ref[...])
        row = lax.broadcasted_iota(jnp.int32, (128, width), 0)
        col = lax.broadcasted_iota(jnp.int32, (128, width), 1)
        base = (row >> 1) * 128 + (1 - (row & 1)) * 64
        half = lax.broadcasted_iota(jnp.int32, (1, width), 1) & 63
        for o_ref, off, sign, ones_off in ((aq_ref, 0, 1.0, 3), (ak_ref, 3, -1.0, 0)):
            out = jnp.where(jnp.logical_and(half >= ones_off, half < ones_off + 3), 1.0, 0.0)
            for j, piece in enumerate(pieces):
                sel = jnp.where(jnp.logical_and(col == base + off + j, row < FOX_HEADS), sign, 0.0).astype(BF16)
                out = out + _dot(piece, sel, NN)
            o_ref[...] = out.astype(BF16)

    return pl.pallas_call(
        body, name=name, grid=(s // TM,), in_specs=[_row_spec(TM, 128)],
        out_specs=[_row_spec(TM, width), _row_spec(TM, width)],
        out_shape=[jax.ShapeDtypeStruct((s, width), BF16)] * 2,
        compiler_params=_cp("parallel"),
    )(cum)


def _fox_do_operand(dycat, ycat, after, name):
    s = dycat.shape[0]

    rb = 1024
    nblk = D_FOX // D_POOL

    def body(*refs):
        do_refs, o_refs, ad_ref = refs[:nblk], refs[nblk:2 * nblk], refs[-1]
        src = lax.broadcasted_iota(jnp.int32, (D_POOL, D_POOL), 0)
        dst = lax.broadcasted_iota(jnp.int32, (D_POOL, D_POOL), 1)
        same_pair = (src >> 7) == (dst >> 7)
        s_in, d_in = src & 127, dst & 127
        hit = jnp.logical_and(same_pair, jnp.logical_or(
            jnp.logical_and(s_in < 64, jnp.logical_and(d_in >= 64, d_in < 67)), jnp.logical_and(s_in >= 64, d_in < 3)))
        sel = jnp.where(hit, 1.0, 0.0).astype(BF16)
        j = lax.broadcasted_iota(jnp.int32, (1, D_POOL), 1) & 63
        for b in range(nblk):
            dd = do_refs[b][...].astype(F32) * o_refs[b][...].astype(F32)
            dsum = jnp.zeros(dd.shape, F32)
            for piece in _split3(dd):
                dsum = dsum + _dot(piece, sel, NN)
            hi, mid, lo = _split3(-dsum)
            ad_ref[:, D_POOL * b:D_POOL * (b + 1)] = jnp.where(j == 0, hi, jnp.where(j == 1, mid, lo))

    blks = [pl.BlockSpec((rb, D_POOL), functools.partial(lambda i, b: (i, 1 + b), b=b)) for b in range(nblk)]
    return pl.pallas_call(
        body, name=name, grid=(s // rb,), in_specs=blks + blks + [pl.BlockSpec(memory_space=pl.ANY)],
        out_specs=pl.BlockSpec((rb, D_FOX), lambda i: (i, 0)),
        out_shape=jax.ShapeDtypeStruct((s, D_FOX), BF16),
        compiler_params=_cp("parallel"),
    )(*[dycat] * nblk, *[ycat] * nblk, after)


def _causal_pairs(nq, key_major):
    if key_major:
        pairs = [(q, k) for k in range(nq) for q in range(k, nq)]
    else:
        pairs = [(q, k) for q in range(nq) for k in range(q + 1)]
    return (jnp.asarray([p[0] for p in pairs], jnp.int32), jnp.asarray([p[1] for p in pairs], jnp.int32))


def _fox_fwd(proj, aq, ak, name):
    s = proj.shape[0]
    nq = s // TQ
    qi_arr, ki_arr = _causal_pairs(nq, key_major=False)

    def body(qi_ref, ki_ref, q_ref, k_ref, v_ref, aq_ref, ak_ref, o_ref, aqb_ref, m0_ref, m1_ref, acc_ref, aux_ref):
        t = pl.program_id(1)
        qi, ki = qi_ref[t], ki_ref[t]
        lane = lax.broadcasted_iota(jnp.int32, (1, 128), 1)
        masks = [lane < 64, lane >= 64]
        ones_v = jnp.where((lane & 63) == 8, 1.0, 0.0).astype(BF16)
        top = lax.broadcasted_iota(jnp.int32, (128, 1), 0) < 64
        m_ref = [m0_ref, m1_ref]

        @pl.when(ki == 0)
        def _():
            m0_ref[...] = jnp.full_like(m0_ref, NEG)
            m1_ref[...] = jnp.full_like(m1_ref, NEG)
            acc_ref[...] = jnp.zeros_like(acc_ref)
            aux_ref[...] = jnp.zeros_like(aux_ref)

        def step(diag):
            q2s = q_ref[...] * 0.125
            k2, v2, aq2, ak2 = k_ref[...], v_ref[...], aq_ref[...], ak_ref[...]
            pv, alpha = [], []
            for hh in range(2):
                qh = jnp.where(masks[hh], q2s, aq2)
                kh = jnp.where(masks[hh], k2, ak2)
                vh = jnp.where(masks[hh], v2, ones_v)
                sc = _dot(kh, qh, NT)
                if diag:
                    key = lax.broadcasted_iota(jnp.int32, sc.shape, 0)
                    qry = lax.broadcasted_iota(jnp.int32, sc.shape, 1)
                    sc = jnp.where(qry >= key, sc, NEG)
                m_prev = m_ref[hh][...]
                m_new = jnp.maximum(m_prev, jnp.max(sc, axis=0, keepdims=True))
                m_ref[hh][...] = m_new
                alpha.append(jnp.exp(m_prev - m_new))
                pv.append(_dot(vh, jnp.exp(sc - m_new).astype(BF16), TN))
            acc_ref[...] = acc_ref[...] * jnp.where(top, alpha[0], alpha[1]) + jnp.where(top, pv[0], pv[1])
            aux_ref[...] = aux_ref[...] * jnp.where(top, alpha[1], alpha[0]) + jnp.where(top, pv[1], pv[0])

        @pl.when(ki < qi)
        def _():
            step(False)

        @pl.when(ki == qi)
        def _():
            step(True)
            aux = aux_ref[...]
            l0, l1 = aux[72:73, :], aux[8:9, :]
            o_ref[...] = (acc_ref[...] * jnp.where(top, 1.0 / l0, 1.0 / l1)).T.astype(BF16)
            aqt = aq_ref[...].astype(F32).T
            cum0 = aqt[64:65, :] + aqt[65:66, :] + aqt[66:67, :]
            cum1 = aqt[0:1, :] + aqt[1:2, :] + aqt[2:3, :]
            aqb = _operand_rows(cum0 - (m0_ref[...] + jnp.log(l0)), cum1 - (m1_ref[...] + jnp.log(l1)), 3)
            aqb_ref[...] = aqb.T.astype(BF16)

    grid_spec = pltpu.PrefetchScalarGridSpec(
        num_scalar_prefetch=2, grid=(HEAD_PAIRS, int(qi_arr.shape[0])),
        in_specs=[pl.BlockSpec((TQ, 128), lambda p, t, qi, ki: (qi[t], Q_BLK + p)),
                  pl.BlockSpec((TQ, 128), lambda p, t, qi, ki: (ki[t], K_BLK + p)),
                  pl.BlockSpec((TQ, 128), lambda p, t, qi, ki: (ki[t], V_BLK + p)),
                  pl.BlockSpec((TQ, 128), lambda p, t, qi, ki: (qi[t], p)),
                  pl.BlockSpec((TQ, 128), lambda p, t, qi, ki: (ki[t], p))],
        out_specs=[pl.BlockSpec((TQ, 128), lambda p, t, qi, ki: (qi[t], Q_BLK + p)),
                   pl.BlockSpec((TQ, 128), lambda p, t, qi, ki: (qi[t], p))],
        scratch_shapes=[pltpu.VMEM((1, TQ), F32), pltpu.VMEM((1, TQ), F32),
                        pltpu.VMEM((128, TQ), F32), pltpu.VMEM((128, TQ), F32)])
    return pl.pallas_call(
        body, name=name, grid_spec=grid_spec,
        out_shape=[jax.ShapeDtypeStruct((s, D_MODEL), BF16), jax.ShapeDtypeStruct((s, HEAD_PAIRS * 128), BF16)],
        compiler_params=_cp("parallel", "arbitrary"),
    )(qi_arr, ki_arr, proj, proj, proj, aq, ak)


def _fox_bwd(proj, dycat, aqb, ak, ad, name):
    s = proj.shape[0]
    nq = s // TQ
    qi_arr, ki_arr = _causal_pairs(nq, key_major=True)

    def body(qi_ref, ki_ref, q_ref, k_ref, v_ref, do_ref, aq_ref, ak_ref, ad_ref,
             dq_ref, dk_ref, dv_ref, qaux_ref, kaux_ref, dq_acc, qaux_acc, dk_acc, dv_acc, kaux_acc):
        t = pl.program_id(1)
        qi, ki = qi_ref[t], ki_ref[t]
        lane = lax.broadcasted_iota(jnp.int32, (1, 128), 1)
        masks = [lane < 64, lane >= 64]
        ones_v = jnp.where((lane & 63) < 3, 1.0, 0.0).astype(BF16)
        top = lax.broadcasted_iota(jnp.int32, (128, 1), 0) < 64

        @pl.when(qi == ki)
        def _():
            dk_acc[...] = jnp.zeros_like(dk_acc)
            dv_acc[...] = jnp.zeros_like(dv_acc)
            kaux_acc[...] = jnp.zeros_like(kaux_acc)

        def step(diag):
            q2s = q_ref[...] * 0.125
            k2, v2, do2 = k_ref[...], v_ref[...], do_ref[...]
            aq2, ak2, ad2 = aq_ref[...], ak_ref[...], ad_ref[...]
            dq, dk, dv = [], [], []
            for hh in range(2):
                qh = jnp.where(masks[hh], q2s, aq2)
                kh = jnp.where(masks[hh], k2, ak2)
                doh = jnp.where(masks[hh], do2, ad2)
                vh = jnp.where(masks[hh], v2, ones_v)
                sc = _dot(kh, qh, NT)
                if diag:
                    key = lax.broadcasted_iota(jnp.int32, sc.shape, 0)
                    qry = lax.broadcasted_iota(jnp.int32, sc.shape, 1)
                    sc = jnp.where(qry >= key, sc, NEG)
                p = jnp.exp(sc)
                dsb = (p * _dot(vh, doh, NT)).astype(BF16)
                dv.append(_dot(p.astype(BF16), doh, NN))
                dk.append(_dot(dsb, qh, NN))
                dq.append(_dot(kh, dsb, TN))
            dk_acc[...] += jnp.where(masks[0], dk[0], dk[1])
            kaux_acc[...] += jnp.where(masks[0], dk[1], dk[0])
            dv_acc[...] += jnp.where(masks[0], dv[0], dv[1])
            dq_new = jnp.where(top, dq[0], dq[1])
            qaux_new = jnp.where(top, dq[1], dq[0])

            @pl.when(ki == 0)
            def _():
                dq_acc[qi] = dq_new
                qaux_acc[qi] = qaux_new

            @pl.when(ki > 0)
            def _():
                dq_acc[qi] += dq_new
                qaux_acc[qi] += qaux_new

        @pl.when(qi > ki)
        def _():
            step(False)

        @pl.when(qi == ki)
        def _():
            step(True)
            rows = pl.ds(pl.multiple_of(qi * TQ, TQ), TQ)
            dq_ref[rows, :] = (dq_acc[qi] * 0.125).T.astype(BF16)
            qaux_ref[rows, :] = qaux_acc[qi].T

        @pl.when(qi == nq - 1)
        def _():
            dk_ref[...] = dk_acc[...].astype(BF16)
            dv_ref[...] = dv_acc[...].astype(BF16)
            kaux_ref[...] = kaux_acc[...]

    grid_spec = pltpu.PrefetchScalarGridSpec(
        num_scalar_prefetch=2, grid=(HEAD_PAIRS, int(qi_arr.shape[0])),
        in_specs=[pl.BlockSpec((TQ, 128), lambda p, t, qi, ki: (qi[t], Q_BLK + p)),
                  pl.BlockSpec((TQ, 128), lambda p, t, qi, ki: (ki[t], K_BLK + p)),
                  pl.BlockSpec((TQ, 128), lambda p, t, qi, ki: (ki[t], V_BLK + p)),
                  pl.BlockSpec((TQ, 128), lambda p, t, qi, ki: (qi[t], Q_BLK + p)),
                  pl.BlockSpec((TQ, 128), lambda p, t, qi, ki: (qi[t], p)),
                  pl.BlockSpec((TQ, 128), lambda p, t, qi, ki: (ki[t], p)),
                  pl.BlockSpec((TQ, 128), lambda p, t, qi, ki: (qi[t], p))],
        out_specs=[pl.BlockSpec((s, 128), lambda p, t, qi, ki: (0, p)),
                   pl.BlockSpec((TQ, 128), lambda p, t, qi, ki: (ki[t], p)),
                   pl.BlockSpec((TQ, 128), lambda p, t, qi, ki: (ki[t], p)),
                   pl.BlockSpec((None, s, 128), lambda p, t, qi, ki: (p, 0, 0)),
                   pl.BlockSpec((None, TQ, 128), lambda p, t, qi, ki: (p, ki[t], 0))],
        scratch_shapes=[pltpu.VMEM((nq, 128, TQ), F32), pltpu.VMEM((nq, 128, TQ), F32),
                        pltpu.VMEM((TQ, 128), F32), pltpu.VMEM((TQ, 128), F32), pltpu.VMEM((TQ, 128), F32)])
    return pl.pallas_call(
        body, name=name, grid_spec=grid_spec,
        out_shape=[jax.ShapeDtypeStruct((s, D_FOX), BF16)] * 3 + [jax.ShapeDtypeStruct((HEAD_PAIRS, s, 128), F32)] * 2,
        compiler_params=_cp("arbitrary", "arbitrary"),
    )(qi_arr, ki_arr, proj, proj, proj, dycat, aqb, ak, ad)


XA_SCALE = XA_DIM ** -0.5


def _xattn_fwd(q2, kv, name):
    s = q2.shape[0]
    m = kv.shape[0]

    def body(q_ref, kv_ref, o_ref):
        for h in range(XA_HEADS):
            c0 = h * XA_DIM
            sc = _dot(q_ref[:, c0:c0 + XA_DIM], kv_ref[:, c0:c0 + XA_DIM], NT) * XA_SCALE
            e = jnp.exp(sc - jnp.max(sc, axis=1, keepdims=True))
            p = e / jnp.sum(e, axis=1, keepdims=True)
            o_ref[:, c0:c0 + XA_DIM] = _dot(p.astype(BF16), kv_ref[:, D_MODEL + c0:D_MODEL + c0 + XA_DIM], NN).astype(BF16)

    return pl.pallas_call(
        body, name=name, grid=(s // TM,),
        in_specs=[_row_spec(TM, D_MODEL), pl.BlockSpec((m, 2 * D_MODEL), lambda i: (0, 0))],
        out_specs=_row_spec(TM, D_MODEL), out_shape=jax.ShapeDtypeStruct((s, D_MODEL), BF16),
        compiler_params=_cp("parallel"),
    )(q2, kv)


def _xattn_bwd(q2, kv, do2, name):
    s = q2.shape[0]
    m = kv.shape[0]

    def body(q_ref, kv_ref, do_ref, dq_ref, dkv_ref):
        i = pl.program_id(0)

        @pl.when(i == 0)
        def _():
            dkv_ref[...] = jnp.zeros_like(dkv_ref)

        for h in range(XA_HEADS):
            c0 = h * XA_DIM
            v0 = D_MODEL + c0
            qh = q_ref[:, c0:c0 + XA_DIM]
            kh = kv_ref[:, c0:c0 + XA_DIM]
            doh = do_ref[:, c0:c0 + XA_DIM]
            sc = _dot(kh, qh, NT) * XA_SCALE
            e = jnp.exp(sc - jnp.max(sc, axis=0, keepdims=True))
            p = e / jnp.sum(e, axis=0, keepdims=True)
            dp = _dot(kv_ref[:, v0:v0 + XA_DIM], doh, NT)
            ds = p * (dp - jnp.sum(p * dp, axis=0, keepdims=True))
            dsb = (ds * XA_SCALE).astype(BF16)
            dq_ref[:, c0:c0 + XA_DIM] = _dot(kh, dsb, TN).T.astype(BF16)
            dkv_ref[:, c0:c0 + XA_DIM] += _dot(dsb, qh, NN)
            dkv_ref[:, v0:v0 + XA_DIM] += _dot(p.astype(BF16), doh, NN)

    return pl.pallas_call(
        body, name=name, grid=(s // TM,),
        in_specs=[_row_spec(TM, D_MODEL), pl.BlockSpec((m, 2 * D_MODEL), lambda i: (0, 0)), _row_spec(TM, D_MODEL)],
        out_specs=[_row_spec(TM, D_MODEL), pl.BlockSpec((m, 2 * D_MODEL), lambda i: (0, 0))],
        out_shape=[jax.ShapeDtypeStruct((s, D_MODEL), BF16), jax.ShapeDtypeStruct((m, 2 * D_MODEL), F32)],
        compiler_params=_cp("arbitrary"),
    )(q2, kv, do2)


GELU_C = math.sqrt(2.0 / math.pi)
GELU_A = 0.044715


def _gelu(x):
    return 0.5 * x * (1.0 + jnp.tanh(GELU_C * (x + GELU_A * x * x * x)))


def _gelu_and_grad(x):
    t = jnp.tanh(GELU_C * (x + GELU_A * x * x * x))
    g = 0.5 * x * (1.0 + t)
    dg = 0.5 * (1.0 + t) + 0.5 * x * (1.0 - t * t) * GELU_C * (1.0 + 3.0 * GELU_A * x * x)
    return g, dg


def _conv(h, s1, s2, w_ref, b_ref):
    return w_ref[0:1, :] * s2 + w_ref[1:2, :] * s1 + w_ref[2:3, :] * h + b_ref[...]


def _shift_down(main, prev8):
    row = lax.broadcasted_iota(jnp.int32, main.shape, 0)
    s1 = jnp.where(row == 0, prev8[7:8, :], pltpu.roll(main, 1, 0))
    s2 = jnp.where(row == 0, prev8[6:7, :], jnp.where(row == 1, prev8[7:8, :], pltpu.roll(main, 2, 0)))
    return s1, s2


def _shift_up(main, next8):
    n = main.shape[0]
    row = lax.broadcasted_iota(jnp.int32, main.shape, 0)
    u1 = jnp.where(row == n - 1, next8[0:1, :], pltpu.roll(main, n - 1, 0))
    u2 = jnp.where(row == n - 2, next8[0:1, :], jnp.where(row == n - 1, next8[1:2, :], pltpu.roll(main, n - 2, 0)))
    return u1, u2


def _ffn_fwd(h3, w_up, cw, cb, w_down, x2, tgt, g_post, name):
    s = h3.shape[0]
    tn = TN_FF
    nj = D_FF // tn
    per = D_MODEL // tn
    hb = TM // 8

    def body(h_ref, halo_ref, wg_ref, wu_ref, cwg_ref, cwu_ref, cbg_ref, cbu_ref, wd_ref, x_ref, t_ref, g_ref,
             hg_ref, hu_ref, cg_ref, cu_ref, a_ref, loss_ref, dx_ref, dy_ref, dg_ref, y_acc):
        i, j = pl.program_id(0), pl.program_id(1)
        h = h_ref[...]
        halo = halo_ref[...]
        halo = jnp.where(i > 0, halo, jnp.zeros_like(halo))
        conv = []
        for w_ref, cw_ref, cb_ref, hid_ref, c_ref in ((wg_ref, cwg_ref, cbg_ref, hg_ref, cg_ref),
                                                      (wu_ref, cwu_ref, cbu_ref, hu_ref, cu_ref)):
            hm = _dot(h, w_ref[...], NN)
            hid_ref[...] = hm.astype(BF16)
            s1, s2 = _shift_down(hm, _dot(halo, w_ref[...], NN))
            c = _conv(hm, s1, s2, cw_ref, cb_ref)
            c_ref[...] = c.astype(BF16)
            conv.append(c)
        a = (_gelu(conv[0]) * conv[1]).astype(BF16)
        a_ref[...] = a
        contrib = _dot(a, wd_ref[...], NN)

        @pl.when(j == 0)
        def _():
            y_acc[...] = contrib

        @pl.when(j > 0)
        def _():
            y_acc[...] += contrib

        @pl.when(j == nj - 1)
        def _():
            yv = y_acc[...]
            r = _rstd(yv)
            yn = yv * r
            e = x_ref[...] + yn * g_ref[...] - t_ref[...]
            part = 0.5 * jnp.sum(jnp.mean(e * e, axis=-1, keepdims=True), axis=0, keepdims=True)
            part = jnp.broadcast_to(part, (1, 128))
            dx = e * (1.0 / D_MODEL)
            dx_ref[...] = dx
            dy_ref[...] = _norm_bwd_rows(dx * g_ref[...], yn, r).astype(BF16)
            dg = jnp.sum(dx * yn, axis=0, keepdims=True)

            @pl.when(i == 0)
            def _():
                dg_ref[...] = dg
                loss_ref[...] = part

            @pl.when(i > 0)
            def _():
                dg_ref[...] += dg
                loss_ref[...] += part

    rows = pl.BlockSpec((TM, D_MODEL), lambda i, j: (i, 0))
    tile = pl.BlockSpec((TM, tn), lambda i, j: (i, j))
    wide = jax.ShapeDtypeStruct((s, D_FF), BF16)
    return pl.pallas_call(
        body, name=name, grid=(s // TM, nj),
        in_specs=[rows,
                  pl.BlockSpec((8, D_MODEL), lambda i, j: (jnp.maximum(i * hb - 1, 0), 0)),
                  pl.BlockSpec((None, D_MODEL, tn), lambda i, j: (j // per, 0, j % per)),
                  pl.BlockSpec((None, D_MODEL, tn), lambda i, j: (NDEV // 2 + j // per, 0, j % per)),
                  pl.BlockSpec((8, tn), lambda i, j: (0, j)),
                  pl.BlockSpec((8, tn), lambda i, j: (0, nj + j)),
                  pl.BlockSpec((1, tn), lambda i, j: (0, j)),
                  pl.BlockSpec((1, tn), lambda i, j: (0, nj + j)),
                  pl.BlockSpec((tn, D_MODEL), lambda i, j: (j, 0)),
                  rows, rows, pl.BlockSpec((1, D_MODEL), lambda i, j: (0, 0))],
        out_specs=[tile, tile, tile, tile, tile,
                   pl.BlockSpec((1, 128), lambda i, j: (0, 0)), rows, rows,
                   pl.BlockSpec((1, D_MODEL), lambda i, j: (0, 0))],
        out_shape=[wide, wide, wide, wide, wide,
                   jax.ShapeDtypeStruct((1, 128), F32), jax.ShapeDtypeStruct((s, D_MODEL), F32),
                   jax.ShapeDtypeStruct((s, D_MODEL), BF16), jax.ShapeDtypeStruct((1, D_MODEL), F32)],
        scratch_shapes=[pltpu.VMEM((TM, D_MODEL), F32)],
        compiler_params=_cp("arbitrary", "arbitrary"),
    )(h3, h3, w_up, w_up, cw, cw, cb, cb, w_down, x2, tgt, g_post)


def _ffn_bwd(dy3, w_down, hid_g, hid_u, conv_g, conv_u, cw, name):
    s = dy3.shape[0]
    n = s // TM
    tn = TN_FF
    nj = D_FF // tn
    hb = TM // 8
    last8 = s // 8 - 1

    def body(dy_ref, dyn_ref, wd_ref, hg_ref, hu_ref, cg_ref, cgn_ref, cu_ref, cun_ref, cwg_ref, cwu_ref,
             dhg_ref, dhu_ref, dcwg_ref, dcwu_ref, dcbg_ref, dcbu_ref):
        i = pl.program_id(1)
        first, last = i == 0, i == n - 1
        da = _dot(dy_ref[...], wd_ref[...], NT)
        dyn = dyn_ref[...]
        dyn = jnp.where(last, jnp.zeros_like(dyn), dyn)
        da_n = _dot(dyn, wd_ref[...], NT)
        c_g, c_u = cg_ref[...].astype(F32), cu_ref[...].astype(F32)
        g, dg = _gelu_and_grad(c_g)
        gn, dgn = _gelu_and_grad(cgn_ref[...].astype(F32))
        outs = ((da * c_u * dg, da_n * cun_ref[...].astype(F32) * dgn, hg_ref, cwg_ref, dhg_ref, dcwg_ref, dcbg_ref),
                (da * g, da_n * gn, hu_ref, cwu_ref, dhu_ref, dcwu_ref, dcbu_ref))
        row8 = lax.broadcasted_iota(jnp.int32, (8, tn), 0)
        for dc, dcn, h_ref, cw_ref, dh_ref, dcw_ref, dcb_ref in outs:
            u1, u2 = _shift_up(dc, dcn)
            dh_ref[...] = (cw_ref[2:3, :] * dc + cw_ref[1:2, :] * u1 + cw_ref[0:1, :] * u2).astype(BF16)
            hm = h_ref[...].astype(F32)
            dcb = jnp.sum(dc, axis=0, keepdims=True)
            dcw = jnp.where(row8 == 0, jnp.sum(hm * u2, axis=0, keepdims=True),
                            jnp.where(row8 == 1, jnp.sum(hm * u1, axis=0, keepdims=True),
                                      jnp.where(row8 == 2, jnp.sum(hm * dc, axis=0, keepdims=True), 0.0)))

            @pl.when(first)
            def _():
                dcw_ref[...] = dcw
                dcb_ref[...] = dcb

            @pl.when(i > 0)
            def _():
                dcw_ref[...] += dcw
                dcb_ref[...] += dcb

    next8 = lambda j, i: (jnp.minimum((i + 1) * hb, last8), j)
    blk = lambda j, i: (i, j)
    col = lambda j, i: (0, j)
    colu = lambda j, i: (0, nj + j)
    tile = pl.BlockSpec((TM, tn), blk)
    return pl.pallas_call(
        body, name=name, grid=(nj, n),
        in_specs=[pl.BlockSpec((TM, D_MODEL), lambda j, i: (i, 0)),
                  pl.BlockSpec((8, D_MODEL), lambda j, i: (jnp.minimum((i + 1) * hb, last8), 0)),
                  pl.BlockSpec((tn, D_MODEL), lambda j, i: (j, 0)),
                  tile, tile, tile, pl.BlockSpec((8, tn), next8), tile, pl.BlockSpec((8, tn), next8),
                  pl.BlockSpec((8, tn), col), pl.BlockSpec((8, tn), colu)],
        out_specs=[tile, tile, pl.BlockSpec((8, tn), col), pl.BlockSpec((8, tn), col),
                   pl.BlockSpec((1, tn), col), pl.BlockSpec((1, tn), col)],
        out_shape=[jax.ShapeDtypeStruct((s, D_FF), BF16), jax.ShapeDtypeStruct((s, D_FF), BF16),
                   jax.ShapeDtypeStruct((8, D_FF), F32), jax.ShapeDtypeStruct((8, D_FF), F32),
                   jax.ShapeDtypeStruct((1, D_FF), F32), jax.ShapeDtypeStruct((1, D_FF), F32)],
        compiler_params=_cp("parallel", "arbitrary"),
    )(dy3, dy3, w_down, hid_g, hid_u, conv_g, conv_g, conv_u, conv_u, cw, cw)


def _slot(p):
    return 4 * p[0] + 2 * p[1] + p[2]


def _all_gather(shards, name):
    n = len(shards)

    def body(*refs):
        ins, outs = refs[:n], refs[n:2 * n]
        send_sems, recv_sems, local_sems = refs[2 * n:]
        x, y, c = lax.axis_index("x"), lax.axis_index("y"), lax.axis_index("c")
        me, sibling = (x, y, c), (x, y, 1 - c)
        chips = [(1 - x, y), (x, 1 - y), (1 - x, 1 - y)]

        def copy(a, k, block, to, from_input=False):
            dst = outs[a].at[_slot(block)]
            return pltpu.make_async_remote_copy(
                src_ref=ins[a] if from_input else dst, dst_ref=dst,
                send_sem=send_sems.at[a, k], recv_sem=recv_sems.at[a, k],
                device_id=to, device_id_type=MESH)

        mine = [pltpu.make_async_copy(ins[a], outs[a].at[_slot(me)], local_sems.at[a]) for a in range(n)]
        for cp in mine:
            cp.start()
        first = []
        for a in range(n):
            first.append(copy(a, 0, me, sibling, True))
            first += [copy(a, 1 + j, me, (*chip, c), True) for j, chip in enumerate(chips)]
        for cp in first:
            cp.start()
        passed = []
        for j, chip in enumerate(chips):
            for a in range(n):
                copy(a, 1 + j, (*chip, c), me).wait_recv()
                fwd = copy(a, 4 + j, (*chip, c), sibling)
                fwd.start()
                passed.append(fwd)
        for a in range(n):
            copy(a, 0, sibling, me).wait_recv()
            for j, chip in enumerate(chips):
                copy(a, 4 + j, (*chip, 1 - c), me).wait_recv()
        for cp in first + passed:
            cp.wait_send()
        for cp in mine:
            cp.wait()

    any_spec = pl.BlockSpec(memory_space=pl.ANY)
    return pl.pallas_call(
        body, name=name,
        in_specs=[any_spec] * n, out_specs=[any_spec] * n,
        out_shape=[jax.ShapeDtypeStruct((NDEV,) + s.shape, s.dtype) for s in shards],
        scratch_shapes=[pltpu.SemaphoreType.DMA((n, 7)), pltpu.SemaphoreType.DMA((n, 7)),
                        pltpu.SemaphoreType.DMA((n,))],
    )(*shards)


def _peer_list(x, y, c):
    return [(1 - x if m & 4 else x, 1 - y if m & 2 else y, 1 - c if m & 1 else c) for m in range(1, NDEV)]


def _exchange_copies(src_refs, land_refs, send_sems, recv_sems, gather):
    x, y, c = lax.axis_index("x"), lax.axis_index("y"), lax.axis_index("c")
    me = (x, y, c)
    copies = []
    for m, peer in enumerate(_peer_list(x, y, c)):
        for a in range(len(src_refs)):
            copies.append(pltpu.make_async_remote_copy(
                src_ref=src_refs[a] if gather else src_refs[a].at[_slot(peer)], dst_ref=land_refs[a].at[_slot(me)],
                send_sem=send_sems.at[a * (NDEV - 1) + m], recv_sem=recv_sems.at[a * (NDEV - 1) + m],
                device_id=peer, device_id_type=MESH))
    return copies


def _all_gather_small(shards, name):
    n = len(shards)

    def body(*refs):
        ins, outs = refs[:n], refs[n:2 * n]
        send_sems, recv_sems, local_sems = refs[2 * n:]
        me = (lax.axis_index("x"), lax.axis_index("y"), lax.axis_index("c"))
        mine = [pltpu.make_async_copy(ins[a], outs[a].at[_slot(me)], local_sems.at[a]) for a in range(n)]
        copies = _exchange_copies(ins, outs, send_sems, recv_sems, True)
        for cp in mine + copies:
            cp.start()
        for cp in copies + mine:
            cp.wait()

    any_spec = pl.BlockSpec(memory_space=pl.ANY)
    return pl.pallas_call(
        body, name=name,
        in_specs=[any_spec] * n, out_specs=[any_spec] * n,
        out_shape=[jax.ShapeDtypeStruct((NDEV,) + s.shape, s.dtype) for s in shards],
        scratch_shapes=[pltpu.SemaphoreType.DMA((n * (NDEV - 1),)), pltpu.SemaphoreType.DMA((n * (NDEV - 1),)),
                        pltpu.SemaphoreType.DMA((n,))],
    )(*shards)


def _exchange_start(srcs, lands, after, gather, name):
    n = len(srcs)
    hbm = pl.BlockSpec(memory_space=pltpu.HBM)

    def body(*refs):
        for cp in _exchange_copies(refs[:n], refs[n:2 * n], refs[2 * n + 1], refs[2 * n + 2], gather):
            cp.start()
        token = refs[-1]
        token[...] = jnp.zeros_like(token)

    outs = pl.pallas_call(
        body, name=name,
        out_shape=(pltpu.SemaphoreType.DMA((n * (NDEV - 1),)), pltpu.SemaphoreType.DMA((n * (NDEV - 1),)),
                   *[pltpu.HBM(a.shape, a.dtype) for a in list(srcs) + list(lands)],
                   jax.ShapeDtypeStruct((8, 128), F32)),
        in_specs=[hbm] * (2 * n) + [pl.BlockSpec(memory_space=pl.ANY)],
        out_specs=(pl.BlockSpec(memory_space=pltpu.SEMAPHORE), pl.BlockSpec(memory_space=pltpu.SEMAPHORE),
                   *[hbm] * (2 * n), pl.BlockSpec(memory_space=pltpu.VMEM)),
        input_output_aliases={i: 2 + i for i in range(2 * n)},
        compiler_params=pltpu.CompilerParams(has_side_effects=pltpu.SideEffectType.DATAFLOW_SIDE_EFFECTING),
    )(*[pltpu.with_memory_space_constraint(a, pltpu.HBM) for a in list(srcs) + list(lands)], after)
    return outs[0], outs[1], outs[2:2 + n], outs[2 + n:2 + 2 * n], outs[-1]


def _exchange_wait(send_sems, recv_sems, srcs, lands, after, gather, name):
    n = len(srcs)
    hbm = pl.BlockSpec(memory_space=pltpu.HBM)

    def body(*refs):
        for cp in _exchange_copies(refs[:n], refs[n:2 * n], refs[2 * n], refs[2 * n + 1], gather):
            cp.wait_send()
            cp.wait_recv()

    outs = pl.pallas_call(
        body, name=name,
        out_shape=tuple(pltpu.HBM(a.shape, a.dtype) for a in list(srcs) + list(lands)),
        in_specs=[hbm] * (2 * n) + [pl.BlockSpec(memory_space=pltpu.SEMAPHORE)] * 2 + [pl.BlockSpec(memory_space=pl.ANY)],
        out_specs=tuple([hbm] * (2 * n)),
        input_output_aliases={i: i for i in range(2 * n)},
        compiler_params=pltpu.CompilerParams(has_side_effects=pltpu.SideEffectType.DATAFLOW_SIDE_EFFECTING),
    )(*srcs, *lands, send_sems, recv_sems, after)
    return outs[n:]


def _own_slot(block):
    me = 4 * lax.axis_index("x") + 2 * lax.axis_index("y") + lax.axis_index("c")
    return lax.dynamic_update_slice(lax.empty((NDEV,) + block.shape, block.dtype), block[None], (me, 0, 0))


def _adam_update(p_ref, w_ref, m_ref, v_ref, g_ref, d_ref, mo_ref, vo_ref):
    bc1 = 1.0 - ADAM_B1 ** ADAM_STEP
    bc2 = 1.0 - ADAM_B2 ** ADAM_STEP
    g = p_ref[0].astype(F32)
    for d in range(1, NDEV):
        g = g + p_ref[d].astype(F32)
    g_ref[...] = g
    mn = ADAM_B1 * m_ref[...] + (1.0 - ADAM_B1) * g
    vn = ADAM_B2 * v_ref[...] + (1.0 - ADAM_B2) * (g * g)
    mo_ref[...] = mn
    vo_ref[...] = vn
    d_ref[...] = -ADAM_LR * ((mn / bc1) / (jnp.sqrt(vn / bc2) + ADAM_EPS) + ADAM_WD * w_ref[...])


def _adamw_small(parts, ws, ms, vs, name):
    n = len(ws)

    def body(*refs):
        ins, outs = refs[:4 * n], refs[4 * n:]
        for k in range(n):
            _adam_update(ins[k], ins[n + k], ins[2 * n + k], ins[3 * n + k], *outs[4 * k:4 * k + 4])

    whole = pl.BlockSpec(memory_space=pltpu.VMEM)
    res = pl.pallas_call(
        body, name=name, in_specs=[whole] * (4 * n), out_specs=[whole] * (4 * n),
        out_shape=[jax.ShapeDtypeStruct(a.shape, F32) for a in ws for _ in range(4)],
    )(*parts, *ws, *ms, *vs)
    return [res[4 * k:4 * k + 4] for k in range(n)]


def _adamw(parts, w, m, v, name):
    r, c = w.shape
    tr = r if r * c <= 160 * 1024 else max(8, (160 * 1024 // c) // 8 * 8)
    while r % tr:
        tr -= 8
    body = functools.partial(_adam_update)
    spec = pl.BlockSpec((tr, c), lambda i: (i, 0))
    return pl.pallas_call(
        body, name=name, grid=(r // tr,),
        in_specs=[pl.BlockSpec((NDEV, tr, c), lambda i: (0, i, 0)), spec, spec, spec],
        out_specs=[spec] * 4, out_shape=[jax.ShapeDtypeStruct((r, c), F32)] * 4,
        compiler_params=_cp("parallel"),
    )(parts, w, m, v)


def _local_step(x, mem, tgt, gains, b_forget, w_pool, pool_scale, conv_b, w_in,
                mix_weights, ffn_weights, send_in_grad, send_mix_grads, send_ffn_grads):
    b_pad = jnp.pad(b_forget, ((0, 0), (0, 128 - FOX_HEADS)))
    wbd = jnp.zeros((D_POOL, D_POOL), F32)
    for g in range(4):
        wbd = wbd.at[64 * g:64 * g + 64, 64 * g:64 * g + 64].set(w_pool[g])
    wbd = wbd.astype(BF16)
    scale = pool_scale.reshape(1, D_POOL)

    h1, proj, fraw = _proj_in(x, gains["mix_pre"], w_in, "proj_in")
    flog, cum = _gate_cumsum(fraw, b_pad, "gate_cumsum")
    aq, ak = _fox_operands(cum, "fox_operands")
    ycat, aqb = _fox_fwd(proj, aq, ak, "fox_fwd")
    ycat = _pool_fwd(proj, wbd, scale, ycat, "pool_fwd")
    w_mix, w_xq, w_xo, w_xkv = mix_weights(ycat)
    y1, x1, h2 = _mm_rows(ycat, w_mix, "nn", 1024, "mix_out", [x], [gains["mix_post"], gains["xa_pre"]],
                          [F32, F32, BF16], _epi_resid)
    q2 = _mm(h2, w_xq, "nn", BF16, 2048, 1024, 1024, "xa_q")
    mem_n = _norm_fwd(mem, gains["mem"], "norm_mem")
    kv = _mm(mem_n, w_xkv, "nn", BF16, mem.shape[0], 256, 1024, "xa_kv", b_cols=256)
    o2 = _xattn_fwd(q2, kv, "xattn_fwd")
    y2, x2, h3 = _mm_rows(o2, w_xo, "nn", 1024, "xa_out", [x1], [gains["xa_post"], gains["ffn_pre"]],
                          [F32, F32, BF16], _epi_resid)
    w_up, w_down, cw = ffn_weights(h3)
    hid_g, hid_u, conv_g, conv_u, act, loss, dx3, dy3, dg_ffn_post = _ffn_fwd(
        h3, w_up, cw, conv_b, w_down, x2, tgt, gains["ffn_post"], "ffn_fwd")

    dhid_g, dhid_u, dcw_g, dcw_u, dcb_g, dcb_u = _ffn_bwd(dy3, w_down, hid_g, hid_u, conv_g, conv_u, cw, "ffn_bwd")
    d_w_down = _mm(act, dy3, "tn", BF16, 2048, 1024, 1024, "dw_down")
    d_w_up = _mm(h3, [dhid_g, dhid_u], "tn", BF16, 1024, 1024, 2048, "dw_up", out_cols=1024)
    sent = send_ffn_grads(d_w_up, d_w_down, jnp.concatenate([dcw_g, dcw_u], axis=1))
    dh3 = _mm([dhid_g, dhid_u], w_up, "nt", F32, 2048, 1024, 1024, "dh_ffn", b_cols=1024, after=sent)
    dx2, dg_ffn_pre, dy2, dg_xa_post = _norm_bwd(dh3, x2, dx3, gains["ffn_pre"], "norm_bwd_ffn",
                                                 prev=(y2, gains["xa_post"]))
    do2 = _mm(dy2, w_xo, "nt", BF16, 2048, 1024, 1024, "d_xa_out")
    d_w_xo = _mm(o2, dy2, "tn", BF16, 1024, 1024, 1024, "dw_xo")
    dq2, dkv = _xattn_bwd(q2, kv, do2, "xattn_bwd")
    dkv = dkv.astype(BF16)
    dx1, dg_xa_pre, dy1, dg_mix_post = _mm_rows(
        dq2, w_xq, "nt", 1024, "dh_xa", [x1, dx2, y1], [gains["xa_pre"], gains["mix_post"]],
        [F32, "sum", BF16, "sum"], _epi_norm_bwd)
    d_w_xq = _mm(h2, dq2, "tn", BF16, 1024, 1024, 1024, "dw_xq")
    dmem_n = _mm(dkv, w_xkv, "nt", F32, mem.shape[0], 1024, 256, "d_mem", b_cols=256)
    d_w_xkv = _mm(mem_n, dkv, "tn", BF16, 1024, 256, mem.shape[0], "dw_xkv", out_cols=256)
    _, dg_mem = _norm_bwd(dmem_n, mem, jnp.zeros_like(mem), gains["mem"], "norm_bwd_mem")
    dycat = _mm(dy1, w_mix, "nt", BF16, 2048, 1024, 1024, "d_mix_out")
    d_w_mix = _mm(ycat, dy1, "tn", BF16, 1024, 1024, 1024, "dw_mix")
    sent_mix = send_mix_grads(d_w_mix, d_w_xq, d_w_xo, d_w_xkv)
    ad = _fox_do_operand(dycat, ycat, sent_mix, "fox_do_operand")
    dq, dk, dv, qaux, kaux = _fox_bwd(proj, dycat, aqb, ak, ad, "fox_bwd")
    du, d_wbd, d_scale = _pool_bwd(proj, dycat, wbd, scale, "pool_bwd")
    df, db_f = _gate_bwd(qaux, kaux, flog, "gate_bwd")
    dproj = [du, dq, dk, dv, df]
    sent_in = send_in_grad(_dw_in(h1, dproj, "dw_in"))
    grad_x, dg_mix_pre = _mm_rows(dproj, w_in, "nt", None, "dh_mix", [x, dx1], [gains["mix_pre"]],
                                  [F32, "sum"], _epi_norm_bwd, after=sent_in)

    small = dict(
        mix_pre=dg_mix_pre, mix_post=dg_mix_post, mem=dg_mem, xa_pre=dg_xa_pre, xa_post=dg_xa_post,
        ffn_pre=dg_ffn_pre, ffn_post=dg_ffn_post,
        conv_b=jnp.concatenate([dcb_g, dcb_u], axis=1),
        w_pool=jnp.concatenate([d_wbd[64 * g:64 * g + 64, 64 * g:64 * g + 64] for g in range(4)], axis=0),
        pool_scale=d_scale.reshape(4, 64),
        b_forget=db_f[:, :FOX_HEADS],
    )
    return loss, grad_x, small


SMALL_ORDER = ("mix_pre", "mix_post", "mem", "xa_pre", "xa_post", "ffn_pre", "ffn_post", "conv_b",
               "w_pool", "pool_scale", "b_forget")


def kernel(x, mem, norm_mix_pre, norm_mix_post, w_in, b_forget, w_pool, pool_scale, w_mix_out, norm_mem, norm_xa_pre, norm_xa_post, w_xq, w_xkv, w_xo, norm_ffn_pre, norm_ffn_post, w_up, conv_w, conv_b, w_down, loss_target, m_norm_mix_pre, m_norm_mix_post, m_w_in, m_b_forget, m_w_pool, m_pool_scale, m_w_mix_out, m_norm_mem, m_norm_xa_pre, m_norm_xa_post, m_w_xq, m_w_xkv, m_w_xo, m_norm_ffn_pre, m_norm_ffn_post, m_w_up, m_conv_w, m_conv_b, m_w_down, v_norm_mix_pre, v_norm_mix_post, v_w_in, v_b_forget, v_w_pool, v_pool_scale, v_w_mix_out, v_norm_mem, v_norm_xa_pre, v_norm_xa_post, v_w_xq, v_w_xkv, v_w_xo, v_norm_ffn_pre, v_norm_ffn_post, v_w_up, v_conv_w, v_conv_b, v_w_down):
    names = ("norm_mix_pre", "norm_mix_post", "w_in", "b_forget", "w_pool", "pool_scale", "w_mix_out", "norm_mem",
             "norm_xa_pre", "norm_xa_post", "w_xq", "w_xkv", "w_xo", "norm_ffn_pre", "norm_ffn_post", "w_up",
             "conv_w", "conv_b", "w_down")
    w = dict(zip(names, (norm_mix_pre, norm_mix_post, w_in, b_forget, w_pool, pool_scale, w_mix_out, norm_mem,
                         norm_xa_pre, norm_xa_post, w_xq, w_xkv, w_xo, norm_ffn_pre, norm_ffn_post, w_up,
                         conv_w, conv_b, w_down)))
    mo = dict(zip(names, (m_norm_mix_pre, m_norm_mix_post, m_w_in, m_b_forget, m_w_pool, m_pool_scale, m_w_mix_out,
                          m_norm_mem, m_norm_xa_pre, m_norm_xa_post, m_w_xq, m_w_xkv, m_w_xo, m_norm_ffn_pre,
                          m_norm_ffn_post, m_w_up, m_conv_w, m_conv_b, m_w_down)))
    vo = dict(zip(names, (v_norm_mix_pre, v_norm_mix_post, v_w_in, v_b_forget, v_w_pool, v_pool_scale, v_w_mix_out,
                          v_norm_mem, v_norm_xa_pre, v_norm_xa_post, v_w_xq, v_w_xkv, v_w_xo, v_norm_ffn_pre,
                          v_norm_ffn_post, v_w_up, v_conv_w, v_conv_b, v_w_down)))

    big_names = ("w_in", "w_mix_out", "w_xq", "w_xo", "w_xkv", "w_up", "w_down")
    shards = {k: w[k][0].astype(BF16) for k in big_names}
    shards["w_in"] = jnp.pad(shards["w_in"], ((0, 0), (0, D_IN_PAD - shards["w_in"].shape[1])))
    conv_w_sh = jnp.pad(conv_w[0, :, 0, :], ((0, 5), (0, 0)))
    (g_in,) = _all_gather([shards["w_in"]], "gather_w_in")
    mix_srcs = [shards[k] for k in ("w_mix_out", "w_xq", "w_xo", "w_xkv")]
    mix_flight = _exchange_start(mix_srcs, [_own_slot(a) for a in mix_srcs], g_in, True, "gather_mix_start")
    ffn_srcs = [shards["w_up"], shards["w_down"], conv_w_sh]
    ffn_flight = _exchange_start(ffn_srcs, [_own_slot(a) for a in ffn_srcs], mix_flight[4], True, "gather_ffn_start")
    my_slot = 4 * lax.axis_index("x") + 2 * lax.axis_index("y") + lax.axis_index("c")
    own_block = lambda a: _own_slot(lax.dynamic_index_in_dim(a, my_slot, 0, keepdims=False))
    by_rows = lambda a: a.reshape(NDEV, a.shape[0] // NDEV, a.shape[1])
    by_cols = lambda a: a.reshape(a.shape[0], NDEV, a.shape[1] // NDEV).transpose(1, 0, 2)
    grad_flight = {}

    def mix_weights(after):
        g_mix, g_xq, g_xo, g_xkv = _exchange_wait(*mix_flight[:4], after, True, "gather_mix_wait")
        return (g_mix.reshape(D_MODEL, D_MODEL), g_xq.reshape(D_MODEL, D_MODEL), g_xo.reshape(D_MODEL, D_MODEL), g_xkv)

    def ffn_weights(after):
        g_up, g_down, g_cw = _exchange_wait(*ffn_flight[:4], after, True, "gather_ffn_wait")
        return g_up, g_down.reshape(D_FF, D_MODEL), g_cw.transpose(1, 0, 2).reshape(8, 2 * D_FF)

    def send_ffn_grads(d_w_up, d_w_down, d_cw):
        srcs = [d_w_up, by_rows(d_w_down), by_cols(d_cw)]
        grad_flight["ffn"] = _exchange_start(srcs, [own_block(a) for a in srcs], ffn_flight[4], False, "scatter_ffn_start")
        return grad_flight["ffn"][4]

    def send_mix_grads(d_w_mix, d_w_xq, d_w_xo, d_w_xkv):
        srcs = [by_rows(d_w_mix), by_rows(d_w_xq), by_rows(d_w_xo), d_w_xkv]
        grad_flight["mix"] = _exchange_start(srcs, [own_block(a) for a in srcs], ffn_flight[4], False, "scatter_mix_start")
        return grad_flight["mix"][4]

    def send_in_grad(d_w_in):
        srcs = [by_rows(d_w_in)]
        grad_flight["in"] = _exchange_start(srcs, [own_block(a) for a in srcs], ffn_flight[4], False, "scatter_in_start")
        return grad_flight["in"][4]

    gains = dict(mix_pre=norm_mix_pre + ffn_flight[4][0, 0], mix_post=norm_mix_post, mem=norm_mem, xa_pre=norm_xa_pre,
                 xa_post=norm_xa_post, ffn_pre=norm_ffn_pre, ffn_post=norm_ffn_post)
    loss, grad_x, small = _local_step(
        x[0], mem[0], loss_target[0], gains, b_forget, w_pool[0], pool_scale[0], conv_b,
        g_in.reshape(D_MODEL, D_IN_PAD), mix_weights, ffn_weights, send_in_grad, send_mix_grads, send_ffn_grads)

    p_up, p_down, p_cw = _exchange_wait(*grad_flight["ffn"][:4], grad_x, False, "scatter_ffn_wait")
    p_mix, p_xq, p_xo, p_xkv = _exchange_wait(*grad_flight["mix"][:4], grad_x, False, "scatter_mix_wait")
    parts = dict(w_mix_out=p_mix, w_xq=p_xq, w_xo=p_xo, w_xkv=p_xkv, w_up=p_up, w_down=p_down)
    *small_parts, loss_parts = _all_gather_small([small[k] for k in SMALL_ORDER] + [loss], "gather_small_grads")

    res = {k: [a[None] for a in _adamw(p, w[k][0], mo[k][0], vo[k][0], "adamw_" + k)] for k, p in parts.items()}
    pad_cw = lambda a: jnp.pad(a[0, :, 0, :], ((0, 5), (0, 0)))
    res["conv_w"] = [a[:3][None, :, None, :] for a in
                     _adamw(p_cw, pad_cw(conv_w), pad_cw(m_conv_w), pad_cw(v_conv_w), "adamw_conv_w")]
    key_of = dict(mix_pre="norm_mix_pre", mix_post="norm_mix_post", mem="norm_mem", xa_pre="norm_xa_pre",
                  xa_post="norm_xa_post", ffn_pre="norm_ffn_pre", ffn_post="norm_ffn_post", conv_b="conv_b",
                  w_pool="w_pool", pool_scale="pool_scale", b_forget="b_forget")
    flat2d = lambda src: [src[key_of[k]].reshape(small[k].shape) for k in SMALL_ORDER]
    small_out = _adamw_small(small_parts, flat2d(w), flat2d(mo), flat2d(vo), "adamw_small")
    for k, four in zip(SMALL_ORDER, small_out):
        res[key_of[k]] = [a.reshape(w[key_of[k]].shape) for a in four]
    (p_in,) = _exchange_wait(*grad_flight["in"][:4], res["w_up"][1], False, "scatter_in_wait")
    res["w_in"] = [a[None] for a in _adamw(p_in[:, :, :w_in.shape[2]], w["w_in"][0], mo["w_in"][0], vo["w_in"][0],
                                           "adamw_w_in")]

    outs = [jnp.sum(loss_parts[:, 0, 0]), grad_x[None]]
    for idx in range(4):
        outs += [res[k][idx] for k in names]
    return tuple(outs)
```

```python
import functools
import math

import jax
import jax.numpy as jnp
from jax import lax
from jax.experimental import pallas as pl
from jax.experimental.pallas import tpu as pltpu

F32 = jnp.float32
BF16 = jnp.bfloat16

NDEV = 8
D_MODEL = 1024
D_POOL = 256
D_FOX = 768
FOX_HEADS = 12
HEAD_PAIRS = FOX_HEADS // 2
XA_HEADS = 4
XA_DIM = 256
D_FF = 4096
D_IN_PAD = 2688
F_COL = 2560
POOL_HALO = 16
NORM_EPS = 1e-6
NEG = -1e30

ADAM_LR = 0.001
ADAM_B1 = 0.9
ADAM_B2 = 0.999
ADAM_EPS = 1e-08
ADAM_WD = 0.01
ADAM_STEP = 10

TM = 512
TQ = 512
TN_FF = 1024
VMEM_LIMIT = 56 * 1024 * 1024
MESH = pl.DeviceIdType.MESH


def _cp(*sem):
    return pltpu.CompilerParams(dimension_semantics=sem, vmem_limit_bytes=VMEM_LIMIT)


def _dot(a, b, dims):
    return lax.dot_general(a, b, (dims, ((), ())), preferred_element_type=F32)


NN = ((1,), (0,))
NT = ((1,), (1,))
TN = ((0,), (0,))


def _mm(a, b, mode, out_dtype, tm, tn, tk, name, b_cols=None, out_cols=None, after=None):
    a_list = list(a) if isinstance(a, (list, tuple)) else [a]
    b_list = list(b) if isinstance(b, (list, tuple)) else [b]
    assert len(a_list) == 1 or len(b_list) == 1
    if mode == "tn":
        K, M = a_list[0].shape
        assert len(a_list) == 1
        Ns = [x.shape[1] for x in b_list]
        N = sum(Ns)
        assert b_cols is None
    else:
        assert len(b_list) == 1
        M = a_list[0].shape[0]
        Ks = [x.shape[1] for x in a_list]
        K = sum(Ks)
        if b_cols is None:
            N = b_list[0].shape[0] if mode == "nt" else b_list[0].shape[1]
        else:
            N = b_list[0].shape[1] if mode == "nt" else NDEV * b_cols
    assert M % tm == 0 and N % tn == 0 and K % tk == 0, (name, M, N, K)
    grid = (M // tm, N // tn, K // tk)
    nk = grid[2]
    dims = {"nn": NN, "nt": NT, "tn": TN}[mode]

    in_specs = []
    if mode == "tn":
        in_specs.append(pl.BlockSpec((tk, tm), lambda i, j, k: (k, i)))
        if len(b_list) == 1:
            in_specs.append(pl.BlockSpec((tk, tn), lambda i, j, k: (k, j)))
        else:
            nj1 = Ns[0] // tn
            in_specs.append(pl.BlockSpec((tk, tn), lambda i, j, k: (k, jnp.minimum(j, nj1 - 1))))
            in_specs.append(pl.BlockSpec((tk, tn), lambda i, j, k: (k, jnp.maximum(j - nj1, 0))))
    else:
        if len(a_list) == 1:
            in_specs.append(pl.BlockSpec((tm, tk), lambda i, j, k: (i, k)))
        else:
            nk1 = Ks[0] // tk
            in_specs.append(pl.BlockSpec((tm, tk), lambda i, j, k: (i, jnp.minimum(k, nk1 - 1))))
            in_specs.append(pl.BlockSpec((tm, tk), lambda i, j, k: (i, jnp.maximum(k - nk1, 0))))
        if b_cols is None:
            if mode == "nn":
                in_specs.append(pl.BlockSpec((tk, tn), lambda i, j, k: (k, j)))
            else:
                in_specs.append(pl.BlockSpec((tn, tk), lambda i, j, k: (j, k)))
        else:
            if mode == "nn":
                per = b_cols // tn
                in_specs.append(pl.BlockSpec((None, tk, tn), lambda i, j, k: (j // per, k, j % per)))
            else:
                per = b_cols // tk
                in_specs.append(pl.BlockSpec((None, tn, tk), lambda i, j, k: (k // per, j, k % per)))
    if out_cols is None:
        out_spec = pl.BlockSpec((tm, tn), lambda i, j, k: (i, j))
        out_shape = jax.ShapeDtypeStruct((M, N), out_dtype)
    else:
        pero = out_cols // tn
        out_spec = pl.BlockSpec((None, tm, tn), lambda i, j, k: (j // pero, i, j % pero))
        out_shape = jax.ShapeDtypeStruct((NDEV, M, out_cols), out_dtype)

    two_a = len(a_list) == 2
    two_b = len(b_list) == 2
    extra = []
    if after is not None:
        in_specs.append(pl.BlockSpec(memory_space=pl.ANY))
        extra.append(after)

    def body(*refs):
        o_ref, acc_ref = refs[-2], refs[-1]
        j = pl.program_id(1)
        k = pl.program_id(2)

        @pl.when(k == 0)
        def _():
            acc_ref[...] = jnp.zeros_like(acc_ref)

        if two_a:
            a1, a2, b1 = refs[0], refs[1], refs[2]
            nk1_ = Ks[0] // tk

            @pl.when(k < nk1_)
            def _():
                acc_ref[...] += _dot(a1[...], b1[...], dims)

            @pl.when(k >= nk1_)
            def _():
                acc_ref[...] += _dot(a2[...], b1[...], dims)
        elif two_b:
            a1, b1, b2 = refs[0], refs[1], refs[2]
            nj1_ = Ns[0] // tn

            @pl.when(j < nj1_)
            def _():
                acc_ref[...] += _dot(a1[...], b1[...], dims)

            @pl.when(j >= nj1_)
            def _():
                acc_ref[...] += _dot(a1[...], b2[...], dims)
        else:
            acc_ref[...] += _dot(refs[0][...], refs[1][...], dims)

        @pl.when(k == nk - 1)
        def _():
            o_ref[...] = acc_ref[...].astype(o_ref.dtype)

    return pl.pallas_call(
        body, name=name, grid=grid, in_specs=in_specs, out_specs=out_spec, out_shape=out_shape,
        scratch_shapes=[pltpu.VMEM((tm, tn), F32)],
        compiler_params=_cp("parallel", "parallel", "arbitrary"),
    )(*a_list, *b_list, *extra)


def _rstd(x):
    return lax.rsqrt(jnp.mean(x * x, axis=-1, keepdims=True) + NORM_EPS)


def _norm_bwd_rows(dxn, xn, r):
    return r * (dxn - xn * jnp.mean(dxn * xn, axis=-1, keepdims=True))


def _row_spec(tm, d):
    return pl.BlockSpec((tm, d), lambda i: (i, 0))


def _vec_spec(d):
    return pl.BlockSpec((1, d), lambda i: (0, 0))


def _mm_rows(a, b, mode, tk, name, rows, vecs, outs, epilogue, b_cols=None, after=None):
    a_list = list(a) if isinstance(a, (list, tuple)) else [a]
    m = a_list[0].shape[0]
    ks = [x.shape[1] for x in a_list]
    n = D_MODEL
    pieces = tk is None
    nk = 1 if pieces else sum(ks) // tk
    dims = NN if mode == "nn" else NT
    if pieces:
        assert mode == "nt" and b_cols is None
        in_specs = [pl.BlockSpec((TM, kp), lambda i, k: (i, 0)) for kp in ks]
        tk = sum(ks)
    elif len(a_list) == 1:
        in_specs = [pl.BlockSpec((TM, tk), lambda i, k: (i, k))]
    else:
        nk1 = ks[0] // tk
        in_specs = [pl.BlockSpec((TM, tk), lambda i, k: (i, jnp.minimum(k, nk1 - 1))),
                    pl.BlockSpec((TM, tk), lambda i, k: (i, jnp.maximum(k - nk1, 0)))]
    if mode == "nn":
        in_specs.append(pl.BlockSpec((tk, n), lambda i, k: (k, 0)))
    elif b_cols is None:
        in_specs.append(pl.BlockSpec((n, tk), lambda i, k: (0, k)))
    else:
        per = b_cols // tk
        in_specs.append(pl.BlockSpec((None, n, tk), lambda i, k: (k // per, 0, k % per)))
    in_specs += [pl.BlockSpec((TM, n), lambda i, k: (i, 0))] * len(rows)
    in_specs += [pl.BlockSpec((1, n), lambda i, k: (0, 0))] * len(vecs)
    extra = []
    if after is not None:
        in_specs.append(pl.BlockSpec(memory_space=pl.ANY))
        extra.append(after)
    out_specs, out_shape = [], []
    for o in outs:
        if o == "sum":
            out_specs.append(pl.BlockSpec((1, n), lambda i, k: (0, 0)))
            out_shape.append(jax.ShapeDtypeStruct((1, n), F32))
        else:
            out_specs.append(pl.BlockSpec((TM, n), lambda i, k: (i, 0)))
            out_shape.append(jax.ShapeDtypeStruct((m, n), o))
    na, nr, nv = len(a_list), len(rows), len(vecs)

    def body(*refs):
        a_refs, b_ref = refs[:na], refs[na]
        row_refs = refs[na + 1:na + 1 + nr]
        vec_refs = refs[na + 1 + nr:na + 1 + nr + nv]
        out_refs = refs[len(refs) - 1 - len(outs):len(refs) - 1]
        acc_ref = refs[-1]
        i, k = pl.program_id(0), pl.program_id(1)

        @pl.when(k == 0)
        def _():
            acc_ref[...] = jnp.zeros_like(acc_ref)

        if pieces:
            off = 0
            for a_ref in a_refs:
                kp = a_ref.shape[1]
                acc_ref[...] += _dot(a_ref[...], b_ref[:, off:off + kp], dims)
                off += kp
        elif na == 1:
            acc_ref[...] += _dot(a_refs[0][...], b_ref[...], dims)
        else:
            nk1_ = ks[0] // tk

            @pl.when(k < nk1_)
            def _():
                acc_ref[...] += _dot(a_refs[0][...], b_ref[...], dims)

            @pl.when(k >= nk1_)
            def _():
                acc_ref[...] += _dot(a_refs[1][...], b_ref[...], dims)

        @pl.when(k == nk - 1)
        def _():
            vals = epilogue(acc_ref[...], [r[...] for r in row_refs], [v[...] for v in vec_refs])
            for o, ref, val in zip(outs, out_refs, vals):
                if o == "sum":
                    @pl.when(i == 0)
                    def _():
                        ref[...] = val

                    @pl.when(i > 0)
                    def _():
                        ref[...] += val
                else:
                    ref[...] = val.astype(o)

    return pl.pallas_call(
        body, name=name, grid=(m // TM, nk), in_specs=in_specs, out_specs=out_specs, out_shape=out_shape,
        scratch_shapes=[pltpu.VMEM((TM, n), F32)],
        compiler_params=_cp("arbitrary", "arbitrary"),
    )(*a_list, b, *rows, *vecs, *extra)


def _proj_in(x, g, w_in, name):
    s, d = x.shape
    n = w_in.shape[1]

    def body(x_ref, g_ref, w_ref, h_ref, p_ref, f_ref):
        xv = x_ref[...]
        h = (xv * _rstd(xv) * g_ref[...]).astype(BF16)
        h_ref[...] = h
        acc = _dot(h, w_ref[...], NN)
        p_ref[...] = acc.astype(BF16)
        f_ref[...] = acc[:, F_COL:]

    return pl.pallas_call(
        body, name=name, grid=(s // TM,),
        in_specs=[_row_spec(TM, d), _vec_spec(d), pl.BlockSpec((d, n), lambda i: (0, 0))],
        out_specs=[_row_spec(TM, d), _row_spec(TM, n), _row_spec(TM, n - F_COL)],
        out_shape=[jax.ShapeDtypeStruct((s, d), BF16), jax.ShapeDtypeStruct((s, n), BF16),
                   jax.ShapeDtypeStruct((s, n - F_COL), F32)],
        compiler_params=_cp("parallel"),
    )(x, g, w_in)


def _dw_in(h, pieces, name):
    s, d = h.shape
    n = sum(p.shape[1] for p in pieces)
    tk = 1024
    nk = s // tk

    def body(*refs):
        h_ref, piece_refs, o_ref, acc_ref = refs[0], refs[1:-2], refs[-2], refs[-1]
        k = pl.program_id(1)

        @pl.when(k == 0)
        def _():
            acc_ref[...] = jnp.zeros_like(acc_ref)

        off = 0
        for p_ref in piece_refs:
            w = p_ref.shape[1]
            acc_ref[:, off:off + w] += _dot(h_ref[...], p_ref[...], TN)
            off += w

        @pl.when(k == nk - 1)
        def _():
            o_ref[...] = acc_ref[...].astype(BF16)

    return pl.pallas_call(
        body, name=name, grid=(d // TM, nk),
        in_specs=[pl.BlockSpec((tk, TM), lambda i, k: (k, i))] +
                 [pl.BlockSpec((tk, p.shape[1]), lambda i, k: (k, 0)) for p in pieces],
        out_specs=pl.BlockSpec((TM, n), lambda i, k: (i, 0)),
        out_shape=jax.ShapeDtypeStruct((d, n), BF16),
        scratch_shapes=[pltpu.VMEM((TM, n), F32)],
        compiler_params=_cp("parallel", "arbitrary"),
    )(h, *pieces)


def _epi_resid(y, rows, vecs):
    (x_in,), (g_post, g_next) = rows, vecs
    xo = x_in + y * _rstd(y) * g_post
    return y, xo, xo * _rstd(xo) * g_next


def _epi_norm_bwd(dh, rows, vecs):
    x, dx_res = rows[0], rows[1]
    r = _rstd(x)
    xn = x * r
    dx = dx_res + _norm_bwd_rows(dh * vecs[0], xn, r)
    res = [dx, jnp.sum(dh * xn, axis=0, keepdims=True)]
    if len(rows) == 3:
        y = rows[2]
        r2 = _rstd(y)
        yn = y * r2
        res += [_norm_bwd_rows(dx * vecs[1], yn, r2), jnp.sum(dx * yn, axis=0, keepdims=True)]
    return res


def _norm_fwd(x, g, name):
    s, d = x.shape
    tm = min(TM, s)

    def body(x_ref, g_ref, h_ref):
        xv = x_ref[...]
        h_ref[...] = (xv * _rstd(xv) * g_ref[...]).astype(BF16)

    return pl.pallas_call(
        body, name=name, grid=(s // tm,), in_specs=[_row_spec(tm, d), _vec_spec(d)],
        out_specs=_row_spec(tm, d), out_shape=jax.ShapeDtypeStruct((s, d), BF16),
        compiler_params=_cp("parallel"),
    )(x, g)


def _norm_bwd(dh, x, dx_res, g_pre, name, prev=None):
    s, d = x.shape
    tm = min(TM, s)
    has_prev = prev is not None

    def body(*refs):
        if has_prev:
            dh_ref, x_ref, dr_ref, g_ref, y_ref, gp_ref, dx_ref, dg_ref, dy_ref, dgp_ref = refs
        else:
            dh_ref, x_ref, dr_ref, g_ref, dx_ref, dg_ref = refs
        i = pl.program_id(0)
        xv = x_ref[...]
        r = _rstd(xv)
        xn = xv * r
        dhv = dh_ref[...].astype(F32)
        dx = dr_ref[...] + _norm_bwd_rows(dhv * g_ref[...], xn, r)
        dx_ref[...] = dx
        dg = jnp.sum(dhv * xn, axis=0, keepdims=True)

        @pl.when(i == 0)
        def _():
            dg_ref[...] = dg

        @pl.when(i > 0)
        def _():
            dg_ref[...] += dg

        if has_prev:
            yv = y_ref[...]
            r2 = _rstd(yv)
            yn = yv * r2
            dy_ref[...] = _norm_bwd_rows(dx * gp_ref[...], yn, r2).astype(BF16)
            dgp = jnp.sum(dx * yn, axis=0, keepdims=True)

            @pl.when(i == 0)
            def _():
                dgp_ref[...] = dgp

            @pl.when(i > 0)
            def _():
                dgp_ref[...] += dgp

    in_specs = [_row_spec(tm, d), _row_spec(tm, d), _row_spec(tm, d), _vec_spec(d)]
    out_specs = [_row_spec(tm, d), _vec_spec(d)]
    out_shape = [jax.ShapeDtypeStruct((s, d), F32), jax.ShapeDtypeStruct((1, d), F32)]
    args = [dh, x, dx_res, g_pre]
    if has_prev:
        in_specs += [_row_spec(tm, d), _vec_spec(d)]
        out_specs += [_row_spec(tm, d), _vec_spec(d)]
        out_shape += [jax.ShapeDtypeStruct((s, d), BF16), jax.ShapeDtypeStruct((1, d), F32)]
        args += list(prev)
    return pl.pallas_call(
        body, name=name, grid=(s // tm,), in_specs=in_specs, out_specs=out_specs, out_shape=out_shape,
        compiler_params=_cp("arbitrary"),
    )(*args)


def _split3(v):
    hi = v.astype(BF16)
    r1 = v - hi.astype(F32)
    mid = r1.astype(BF16)
    lo = (r1 - mid.astype(F32)).astype(BF16)
    return hi, mid, lo


def _tri_dot(tri, v):
    hi, mid, lo = _split3(v)
    return _dot(tri, hi, NN) + _dot(tri, mid, NN) + _dot(tri, lo, NN)


def _gate_cumsum(fraw, b_pad, name):
    s = fraw.shape[0]
    width = HEAD_PAIRS * 128

    def body(f_ref, b_ref, flog_ref, aq_ref, ak_ref, carry_ref):
        i = pl.program_id(0)

        @pl.when(i == 0)
        def _():
            carry_ref[...] = jnp.zeros_like(carry_ref)

        flog = f_ref[...] + b_ref[...]
        flog_ref[...] = flog
        lf = jnp.minimum(flog, 0.0) - jnp.log(1.0 + jnp.exp(-jnp.abs(flog)))
        lane = lax.broadcasted_iota(jnp.int32, (1, 128), 1)
        lf = jnp.where(lane < FOX_HEADS, lf, 0.0)
        row = lax.broadcasted_iota(jnp.int32, (TM, TM), 0)
        col = lax.broadcasted_iota(jnp.int32, (TM, TM), 1)
        tri = (row >= col).astype(BF16)
        cum = _tri_dot(tri, lf) + carry_ref[...]
        carry_ref[...] = cum[TM - 1:TM, :]
        aq_ref[...], ak_ref[...] = _fox_operands(cum)

    return pl.pallas_call(
        body, name=name, grid=(s // TM,),
        in_specs=[_row_spec(TM, 128), _vec_spec(128)],
        out_specs=[_row_spec(TM, 128), _row_spec(TM, width), _row_spec(TM, width)],
        out_shape=[jax.ShapeDtypeStruct((s, 128), F32), jax.ShapeDtypeStruct((s, width), BF16),
                   jax.ShapeDtypeStruct((s, width), BF16)],
        scratch_shapes=[pltpu.VMEM((1, 128), F32)],
        compiler_params=_cp("arbitrary"),
    )(fraw, b_pad)


def _gate_bwd(qaux, kaux, flog, name):
    s = flog.shape[0]
    n = s // TM

    def body(qa_ref, ka_ref, fl_ref, dp_ref, db_ref, carry_ref):
        i = pl.program_id(0)

        @pl.when(i == 0)
        def _():
            carry_ref[...] = jnp.zeros_like(carry_ref)

        src = lax.broadcasted_iota(jnp.int32, (128, 128), 0)
        dst = lax.broadcasted_iota(jnp.int32, (128, 128), 1)
        dcum = jnp.zeros((TM, 128), F32)
        for p in range(HEAD_PAIRS):
            for ref, l0, l1, sign in ((qa_ref, 64, 0, 1.0), (ka_ref, 67, 3, -1.0)):
                hit = jnp.logical_or(jnp.logical_and(src == l0, dst == 2 * p),
                                     jnp.logical_and(src == l1, dst == 2 * p + 1))
                sel = jnp.where(hit, sign, 0.0).astype(BF16)
                for piece in _split3(ref[p]):
                    dcum = dcum + _dot(piece, sel, NN)
        row = lax.broadcasted_iota(jnp.int32, (TM, TM), 0)
        col = lax.broadcasted_iota(jnp.int32, (TM, TM), 1)
        tri = (row <= col).astype(BF16)
        dlf = _tri_dot(tri, dcum) + carry_ref[...]
        carry_ref[...] = dlf[0:1, :]
        lane = lax.broadcasted_iota(jnp.int32, (1, 128), 1)
        df = jnp.where(lane < FOX_HEADS, dlf / (1.0 + jnp.exp(fl_ref[...])), 0.0)
        dp_ref[...] = df.astype(BF16)
        db = jnp.sum(df, axis=0, keepdims=True)

        @pl.when(i == 0)
        def _():
            db_ref[...] = db

        @pl.when(i > 0)
        def _():
            db_ref[...] += db

    rev = lambda i: (n - 1 - i, 0)
    return pl.pallas_call(
        body, name=name, grid=(n,),
        in_specs=[pl.BlockSpec((HEAD_PAIRS, TM, 128), lambda i: (0, n - 1 - i, 0)),
                  pl.BlockSpec((HEAD_PAIRS, TM, 128), lambda i: (0, n - 1 - i, 0)), pl.BlockSpec((TM, 128), rev)],
        out_specs=[pl.BlockSpec((TM, 128), rev), _vec_spec(128)],
        out_shape=[jax.ShapeDtypeStruct((s, 128), BF16), jax.ShapeDtypeStruct((1, 128), F32)],
        scratch_shapes=[pltpu.VMEM((1, 128), F32)],
        compiler_params=_cp("arbitrary"),
    )(qaux, kaux, flog)


def _pool_consts(i, rows):
    lane = lax.broadcasted_iota(jnp.int32, (rows, D_POOL), 1)
    t1 = lax.broadcasted_iota(jnp.int32, (rows, D_POOL), 0) + i * TM + 1
    win = jnp.where(lane < 64, 2, jnp.where(lane < 128, 4, jnp.where(lane < 192, 8, 16)))
    inv = 1.0 / jnp.minimum(t1, win).astype(F32)
    return lane, inv


def _by_group(lane, s2, s4, s8, s16):
    return jnp.where(lane < 64, s2, jnp.where(lane < 128, s4, jnp.where(lane < 192, s8, s16)))


def _pool_diff(i, u_ref, halo_ref):
    u = u_ref[...].astype(F32)
    halo = jnp.where(i > 0, halo_ref[...].astype(F32), 0.0)
    ext = jnp.concatenate([halo, u], axis=0)
    s2 = ext + pltpu.roll(ext, 1, 0)
    s4 = s2 + pltpu.roll(s2, 2, 0)
    s8 = s4 + pltpu.roll(s4, 4, 0)
    s16 = s8 + pltpu.roll(s8, 8, 0)
    lane, inv = _pool_consts(i, TM)
    sel = _by_group(lane, s2[POOL_HALO:], s4[POOL_HALO:], s8[POOL_HALO:], s16[POOL_HALO:])
    return sel * inv - u


def _pool_fwd(proj, wbd, scale, ycat, name):
    s = proj.shape[0]
    hb = TM // POOL_HALO

    def body(u_ref, halo_ref, w_ref, sc_ref, y_any, y_ref):
        del y_any
        i = pl.program_id(0)
        diff = _pool_diff(i, u_ref, halo_ref)
        mixed = _dot(diff.astype(BF16), w_ref[...], NN)
        y_ref[...] = (mixed * sc_ref[...]).astype(BF16)

    return pl.pallas_call(
        body, name=name, grid=(s // TM,),
        in_specs=[pl.BlockSpec((TM, D_POOL), lambda i: (i, 0)),
                  pl.BlockSpec((POOL_HALO, D_POOL), lambda i: (jnp.maximum(i * hb - 1, 0), 0)),
                  pl.BlockSpec((D_POOL, D_POOL), lambda i: (0, 0)), _vec_spec(D_POOL),
                  pl.BlockSpec(memory_space=pl.ANY)],
        out_specs=pl.BlockSpec((TM, D_POOL), lambda i: (i, 0)),
        out_shape=jax.ShapeDtypeStruct(ycat.shape, ycat.dtype),
        input_output_aliases={4: 0},
        compiler_params=_cp("parallel"),
    )(proj, proj, wbd, scale, ycat)


def _pool_bwd(proj, dycat, wbd, scale, name):
    s = proj.shape[0]
    n = s // TM
    hb = TM // POOL_HALO
    last_halo = s // POOL_HALO - 1

    def body(u_ref, halo_ref, dy_ref, dyp_ref, w_ref, sc_ref, dp_ref, dw_ref, dsc_ref):
        i = pl.program_id(0)
        diff = _pool_diff(i, u_ref, halo_ref)
        diff_b = diff.astype(BF16)
        mixed = _dot(diff_b, w_ref[...], NN)
        dy = dy_ref[...].astype(F32)
        dmix = (dy * sc_ref[...]).astype(BF16)
        dyp = jnp.where(i < n - 1, dyp_ref[...].astype(F32), 0.0)
        dmix_p = (dyp * sc_ref[...]).astype(BF16)
        dd = _dot(dmix, w_ref[...], NT)
        dd_p = _dot(dmix_p, w_ref[...], NT)
        lane, inv = _pool_consts(i, TM)
        _, inv_p = _pool_consts(i + 1, POOL_HALO)
        ext = jnp.concatenate([dd * inv, dd_p * inv_p], axis=0)
        rows = TM + POOL_HALO
        l2 = ext + pltpu.roll(ext, rows - 1, 0)
        l4 = l2 + pltpu.roll(l2, rows - 2, 0)
        l8 = l4 + pltpu.roll(l4, rows - 4, 0)
        l16 = l8 + pltpu.roll(l8, rows - 8, 0)
        du = _by_group(lane, l2[:TM], l4[:TM], l8[:TM], l16[:TM]) - dd
        dp_ref[...] = du.astype(BF16)
        dw = _dot(diff_b, dmix, TN)
        dsc = jnp.sum(dy * mixed, axis=0, keepdims=True)

        @pl.when(i == 0)
        def _():
            dw_ref[...] = dw
            dsc_ref[...] = dsc

        @pl.when(i > 0)
        def _():
            dw_ref[...] += dw
            dsc_ref[...] += dsc

    return pl.pallas_call(
        body, name=name, grid=(n,),
        in_specs=[pl.BlockSpec((TM, D_POOL), lambda i: (i, 0)),
                  pl.BlockSpec((POOL_HALO, D_POOL), lambda i: (jnp.maximum(i * hb - 1, 0), 0)),
                  pl.BlockSpec((TM, D_POOL), lambda i: (i, 0)),
                  pl.BlockSpec((POOL_HALO, D_POOL), lambda i: (jnp.minimum((i + 1) * hb, last_halo), 0)),
                  pl.BlockSpec((D_POOL, D_POOL), lambda i: (0, 0)), _vec_spec(D_POOL)],
        out_specs=[pl.BlockSpec((TM, D_POOL), lambda i: (i, 0)),
                   pl.BlockSpec((D_POOL, D_POOL), lambda i: (0, 0)), _vec_spec(D_POOL)],
        out_shape=[jax.ShapeDtypeStruct((s, D_POOL), BF16),
                   jax.ShapeDtypeStruct((D_POOL, D_POOL), F32), jax.ShapeDtypeStruct((1, D_POOL), F32)],
        compiler_params=_cp("arbitrary"),
    )(proj, proj, dycat, dycat, wbd, scale)


Q_BLK = D_POOL // 128
K_BLK = Q_BLK + D_FOX // 128
V_BLK = K_BLK + D_FOX // 128


def _operand_rows(v0, v1, ones_off):
    row = lax.broadcasted_iota(jnp.int32, (128, 1), 0)
    half = row & 63
    out = jnp.where(jnp.logical_and(half >= ones_off, half < ones_off + 3), 1.0, 0.0) + jnp.zeros_like(v0)
    for base, v in ((64, v0), (0, v1)):
        for j, piece in enumerate(_split3(v)):
            out = jnp.where(row == base + j, piece.astype(F32), out)
    return out


def _fox_operands(cum):
    width = HEAD_PAIRS * 128
    pieces = _split3(cum)
    row = lax.broadcasted_iota(jnp.int32, (128, width), 0)
    col = lax.broadcasted_iota(jnp.int32, (128, width), 1)
    base = (row >> 1) * 128 + (1 - (row & 1)) * 64
    half = lax.broadcasted_iota(jnp.int32, (1, width), 1) & 63
    res = []
    for off, sign, ones_off in ((0, 1.0, 3), (3, -1.0, 0)):
        out = jnp.where(jnp.logical_and(half >= ones_off, half < ones_off + 3), 1.0, 0.0)
        for j, piece in enumerate(pieces):
            sel = jnp.where(jnp.logical_and(col == base + off + j, row < FOX_HEADS), sign, 0.0).astype(BF16)
            out = out + _dot(piece, sel, NN)
        res.append(out.astype(BF16))
    return res


def _fox_do_operand(dycat, ycat, after, name):
    s = dycat.shape[0]

    rb = 1024
    nblk = D_FOX // D_POOL

    def body(*refs):
        do_refs, o_refs, ad_ref = refs[:nblk], refs[nblk:2 * nblk], refs[-1]
        src = lax.broadcasted_iota(jnp.int32, (D_POOL, D_POOL), 0)
        dst = lax.broadcasted_iota(jnp.int32, (D_POOL, D_POOL), 1)
        same_pair = (src >> 7) == (dst >> 7)
        s_in, d_in = src & 127, dst & 127
        hit = jnp.logical_and(same_pair, jnp.logical_or(
            jnp.logical_and(s_in < 64, jnp.logical_and(d_in >= 64, d_in < 67)), jnp.logical_and(s_in >= 64, d_in < 3)))
        sel = jnp.where(hit, 1.0, 0.0).astype(BF16)
        j = lax.broadcasted_iota(jnp.int32, (1, D_POOL), 1) & 63
        for b in range(nblk):
            dd = do_refs[b][...].astype(F32) * o_refs[b][...].astype(F32)
            dsum = jnp.zeros(dd.shape, F32)
            for piece in _split3(dd):
                dsum = dsum + _dot(piece, sel, NN)
            hi, mid, lo = _split3(-dsum)
            ad_ref[:, D_POOL * b:D_POOL * (b + 1)] = jnp.where(j == 0, hi, jnp.where(j == 1, mid, lo))

    blks = [pl.BlockSpec((rb, D_POOL), functools.partial(lambda i, b: (i, 1 + b), b=b)) for b in range(nblk)]
    return pl.pallas_call(
        body, name=name, grid=(s // rb,), in_specs=blks + blks + [pl.BlockSpec(memory_space=pl.ANY)],
        out_specs=pl.BlockSpec((rb, D_FOX), lambda i: (i, 0)),
        out_shape=jax.ShapeDtypeStruct((s, D_FOX), BF16),
        compiler_params=_cp("parallel"),
    )(*[dycat] * nblk, *[ycat] * nblk, after)


def _causal_pairs(nq, key_major):
    if key_major:
        pairs = [(q, k) for k in range(nq) for q in range(k, nq)]
    else:
        pairs = [(q, k) for q in range(nq) for k in range(q + 1)]
    return (jnp.asarray([p[0] for p in pairs], jnp.int32), jnp.asarray([p[1] for p in pairs], jnp.int32))


def _fox_fwd(proj, aq, ak, name):
    s = proj.shape[0]
    nq = s // TQ
    qi_arr, ki_arr = _causal_pairs(nq, key_major=False)

    def body(qi_ref, ki_ref, q_ref, k_ref, v_ref, aq_ref, ak_ref, o_ref, aqb_ref, m0_ref, m1_ref, acc_ref, aux_ref):
        t = pl.program_id(1)
        qi, ki = qi_ref[t], ki_ref[t]
        lane = lax.broadcasted_iota(jnp.int32, (1, 128), 1)
        masks = [lane < 64, lane >= 64]
        ones_v = jnp.where((lane & 63) == 8, 1.0, 0.0).astype(BF16)
        top = lax.broadcasted_iota(jnp.int32, (128, 1), 0) < 64
        m_ref = [m0_ref, m1_ref]

        @pl.when(ki == 0)
        def _():
            m0_ref[...] = jnp.full_like(m0_ref, NEG)
            m1_ref[...] = jnp.full_like(m1_ref, NEG)
            acc_ref[...] = jnp.zeros_like(acc_ref)
            aux_ref[...] = jnp.zeros_like(aux_ref)

        def step(diag):
            q2s = q_ref[...] * 0.125
            k2, v2, aq2, ak2 = k_ref[...], v_ref[...], aq_ref[...], ak_ref[...]
            pv, alpha = [], []
            for hh in range(2):
                qh = jnp.where(masks[hh], q2s, aq2)
                kh = jnp.where(masks[hh], k2, ak2)
                vh = jnp.where(masks[hh], v2, ones_v)
                sc = _dot(kh, qh, NT)
                if diag:
                    key = lax.broadcasted_iota(jnp.int32, sc.shape, 0)
                    qry = lax.broadcasted_iota(jnp.int32, sc.shape, 1)
                    sc = jnp.where(qry >= key, sc, NEG)
                m_prev = m_ref[hh][...]
                m_new = jnp.maximum(m_prev, jnp.max(sc, axis=0, keepdims=True))
                m_ref[hh][...] = m_new
                alpha.append(jnp.exp(m_prev - m_new))
                pv.append(_dot(vh, jnp.exp(sc - m_new).astype(BF16), TN))
            acc_ref[...] = acc_ref[...] * jnp.where(top, alpha[0], alpha[1]) + jnp.where(top, pv[0], pv[1])
            aux_ref[...] = aux_ref[...] * jnp.where(top, alpha[1], alpha[0]) + jnp.where(top, pv[1], pv[0])

        @pl.when(ki < qi)
        def _():
            step(False)

        @pl.when(ki == qi)
        def _():
            step(True)
            aux = aux_ref[...]
            l0, l1 = aux[72:73, :], aux[8:9, :]
            o_ref[...] = (acc_ref[...] * jnp.where(top, 1.0 / l0, 1.0 / l1)).T.astype(BF16)
            aqt = aq_ref[...].astype(F32).T
            cum0 = aqt[64:65, :] + aqt[65:66, :] + aqt[66:67, :]
            cum1 = aqt[0:1, :] + aqt[1:2, :] + aqt[2:3, :]
            aqb = _operand_rows(cum0 - (m0_ref[...] + jnp.log(l0)), cum1 - (m1_ref[...] + jnp.log(l1)), 3)
            aqb_ref[...] = aqb.T.astype(BF16)

    grid_spec = pltpu.PrefetchScalarGridSpec(
        num_scalar_prefetch=2, grid=(HEAD_PAIRS, int(qi_arr.shape[0])),
        in_specs=[pl.BlockSpec((TQ, 128), lambda p, t, qi, ki: (qi[t], Q_BLK + p)),
                  pl.BlockSpec((TQ, 128), lambda p, t, qi, ki: (ki[t], K_BLK + p)),
                  pl.BlockSpec((TQ, 128), lambda p, t, qi, ki: (ki[t], V_BLK + p)),
                  pl.BlockSpec((TQ, 128), lambda p, t, qi, ki: (qi[t], p)),
                  pl.BlockSpec((TQ, 128), lambda p, t, qi, ki: (ki[t], p))],
        out_specs=[pl.BlockSpec((TQ, 128), lambda p, t, qi, ki: (qi[t], Q_BLK + p)),
                   pl.BlockSpec((TQ, 128), lambda p, t, qi, ki: (qi[t], p))],
        scratch_shapes=[pltpu.VMEM((1, TQ), F32), pltpu.VMEM((1, TQ), F32),
                        pltpu.VMEM((128, TQ), F32), pltpu.VMEM((128, TQ), F32)])
    return pl.pallas_call(
        body, name=name, grid_spec=grid_spec,
        out_shape=[jax.ShapeDtypeStruct((s, D_MODEL), BF16), jax.ShapeDtypeStruct((s, HEAD_PAIRS * 128), BF16)],
        compiler_params=_cp("parallel", "arbitrary"),
    )(qi_arr, ki_arr, proj, proj, proj, aq, ak)


def _fox_bwd(proj, dycat, aqb, ak, ad, name):
    s = proj.shape[0]
    nq = s // TQ
    qi_arr, ki_arr = _causal_pairs(nq, key_major=True)

    def body(qi_ref, ki_ref, q_ref, k_ref, v_ref, do_ref, aq_ref, ak_ref, ad_ref,
             dq_ref, dk_ref, dv_ref, qaux_ref, kaux_ref, dq_acc, qaux_acc, dk_acc, dv_acc, kaux_acc):
        t = pl.program_id(1)
        qi, ki = qi_ref[t], ki_ref[t]
        lane = lax.broadcasted_iota(jnp.int32, (1, 128), 1)
        masks = [lane < 64, lane >= 64]
        ones_v = jnp.where((lane & 63) < 3, 1.0, 0.0).astype(BF16)
        top = lax.broadcasted_iota(jnp.int32, (128, 1), 0) < 64

        @pl.when(qi == ki)
        def _():
            dk_acc[...] = jnp.zeros_like(dk_acc)
            dv_acc[...] = jnp.zeros_like(dv_acc)
            kaux_acc[...] = jnp.zeros_like(kaux_acc)

        def step(diag):
            q2s = q_ref[...] * 0.125
            k2, v2, do2 = k_ref[...], v_ref[...], do_ref[...]
            aq2, ak2, ad2 = aq_ref[...], ak_ref[...], ad_ref[...]
            dq, dk, dv = [], [], []
            for hh in range(2):
                qh = jnp.where(masks[hh], q2s, aq2)
                kh = jnp.where(masks[hh], k2, ak2)
                doh = jnp.where(masks[hh], do2, ad2)
                vh = jnp.where(masks[hh], v2, ones_v)
                sc = _dot(kh, qh, NT)
                if diag:
                    key = lax.broadcasted_iota(jnp.int32, sc.shape, 0)
                    qry = lax.broadcasted_iota(jnp.int32, sc.shape, 1)
                    sc = jnp.where(qry >= key, sc, NEG)
                p = jnp.exp(sc)
                dsb = (p * _dot(vh, doh, NT)).astype(BF16)
                dv.append(_dot(p.astype(BF16), doh, NN))
                dk.append(_dot(dsb, qh, NN))
                dq.append(_dot(kh, dsb, TN))
            dk_acc[...] += jnp.where(masks[0], dk[0], dk[1])
            kaux_acc[...] += jnp.where(masks[0], dk[1], dk[0])
            dv_acc[...] += jnp.where(masks[0], dv[0], dv[1])
            dq_new = jnp.where(top, dq[0], dq[1])
            qaux_new = jnp.where(top, dq[1], dq[0])

            @pl.when(ki == 0)
            def _():
                dq_acc[qi] = dq_new
                qaux_acc[qi] = qaux_new

            @pl.when(ki > 0)
            def _():
                dq_acc[qi] += dq_new
                qaux_acc[qi] += qaux_new

        @pl.when(qi > ki)
        def _():
            step(False)

        @pl.when(qi == ki)
        def _():
            step(True)
            rows = pl.ds(pl.multiple_of(qi * TQ, TQ), TQ)
            dq_ref[rows, :] = (dq_acc[qi] * 0.125).T.astype(BF16)
            qaux_ref[rows, :] = qaux_acc[qi].T

        @pl.when(qi == nq - 1)
        def _():
            dk_ref[...] = dk_acc[...].astype(BF16)
            dv_ref[...] = dv_acc[...].astype(BF16)
            kaux_ref[...] = kaux_acc[...]

    grid_spec = pltpu.PrefetchScalarGridSpec(
        num_scalar_prefetch=2, grid=(HEAD_PAIRS, int(qi_arr.shape[0])),
        in_specs=[pl.BlockSpec((TQ, 128), lambda p, t, qi, ki: (qi[t], Q_BLK + p)),
                  pl.BlockSpec((TQ, 128), lambda p, t, qi, ki: (ki[t], K_BLK + p)),
                  pl.BlockSpec((TQ, 128), lambda p, t, qi, ki: (ki[t], V_BLK + p)),
                  pl.BlockSpec((TQ, 128), lambda p, t, qi, ki: (qi[t], Q_BLK + p)),
                  pl.BlockSpec((TQ, 128), lambda p, t, qi, ki: (qi[t], p)),
                  pl.BlockSpec((TQ, 128), lambda p, t, qi, ki: (ki[t], p)),
                  pl.BlockSpec((TQ, 128), lambda p, t, qi, ki: (qi[t], p))],
        out_specs=[pl.BlockSpec((s, 128), lambda p, t, qi, ki: (0, p)),
                   pl.BlockSpec((TQ, 128), lambda p, t, qi, ki: (ki[t], p)),
                   pl.BlockSpec((TQ, 128), lambda p, t, qi, ki: (ki[t], p)),
                   pl.BlockSpec((None, s, 128), lambda p, t, qi, ki: (p, 0, 0)),
                   pl.BlockSpec((None, TQ, 128), lambda p, t, qi, ki: (p, ki[t], 0))],
        scratch_shapes=[pltpu.VMEM((nq, 128, TQ), F32), pltpu.VMEM((nq, 128, TQ), F32),
                        pltpu.VMEM((TQ, 128), F32), pltpu.VMEM((TQ, 128), F32), pltpu.VMEM((TQ, 128), F32)])
    return pl.pallas_call(
        body, name=name, grid_spec=grid_spec,
        out_shape=[jax.ShapeDtypeStruct((s, D_FOX), BF16)] * 3 + [jax.ShapeDtypeStruct((HEAD_PAIRS, s, 128), F32)] * 2,
        compiler_params=_cp("arbitrary", "arbitrary"),
    )(qi_arr, ki_arr, proj, proj, proj, dycat, aqb, ak, ad)


XA_SCALE = XA_DIM ** -0.5


def _xattn_fwd(q2, kv, name):
    s = q2.shape[0]
    m = kv.shape[0]

    def body(q_ref, kv_ref, o_ref):
        for h in range(XA_HEADS):
            c0 = h * XA_DIM
            sc = _dot(q_ref[:, c0:c0 + XA_DIM], kv_ref[:, c0:c0 + XA_DIM], NT) * XA_SCALE
            e = jnp.exp(sc - jnp.max(sc, axis=1, keepdims=True))
            p = e / jnp.sum(e, axis=1, keepdims=True)
            o_ref[:, c0:c0 + XA_DIM] = _dot(p.astype(BF16), kv_ref[:, D_MODEL + c0:D_MODEL + c0 + XA_DIM], NN).astype(BF16)

    return pl.pallas_call(
        body, name=name, grid=(s // TM,),
        in_specs=[_row_spec(TM, D_MODEL), pl.BlockSpec((m, 2 * D_MODEL), lambda i: (0, 0))],
        out_specs=_row_spec(TM, D_MODEL), out_shape=jax.ShapeDtypeStruct((s, D_MODEL), BF16),
        compiler_params=_cp("parallel"),
    )(q2, kv)


def _xattn_bwd(q2, kv, do2, name):
    s = q2.shape[0]
    m = kv.shape[0]

    def body(q_ref, kv_ref, do_ref, dq_ref, dkv_ref):
        i = pl.program_id(0)

        @pl.when(i == 0)
        def _():
            dkv_ref[...] = jnp.zeros_like(dkv_ref)

        for h in range(XA_HEADS):
            c0 = h * XA_DIM
            v0 = D_MODEL + c0
            qh = q_ref[:, c0:c0 + XA_DIM]
            kh = kv_ref[:, c0:c0 + XA_DIM]
            doh = do_ref[:, c0:c0 + XA_DIM]
            sc = _dot(kh, qh, NT) * XA_SCALE
            e = jnp.exp(sc - jnp.max(sc, axis=0, keepdims=True))
            p = e / jnp.sum(e, axis=0, keepdims=True)
            dp = _dot(kv_ref[:, v0:v0 + XA_DIM], doh, NT)
            ds = p * (dp - jnp.sum(p * dp, axis=0, keepdims=True))
            dsb = (ds * XA_SCALE).astype(BF16)
            dq_ref[:, c0:c0 + XA_DIM] = _dot(kh, dsb, TN).T.astype(BF16)
            dkv_ref[:, c0:c0 + XA_DIM] += _dot(dsb, qh, NN)
            dkv_ref[:, v0:v0 + XA_DIM] += _dot(p.astype(BF16), doh, NN)

    return pl.pallas_call(
        body, name=name, grid=(s // TM,),
        in_specs=[_row_spec(TM, D_MODEL), pl.BlockSpec((m, 2 * D_MODEL), lambda i: (0, 0)), _row_spec(TM, D_MODEL)],
        out_specs=[_row_spec(TM, D_MODEL), pl.BlockSpec((m, 2 * D_MODEL), lambda i: (0, 0))],
        out_shape=[jax.ShapeDtypeStruct((s, D_MODEL), BF16), jax.ShapeDtypeStruct((m, 2 * D_MODEL), F32)],
        compiler_params=_cp("arbitrary"),
    )(q2, kv, do2)


GELU_C = math.sqrt(2.0 / math.pi)
GELU_A = 0.044715


def _gelu(x):
    return 0.5 * x * (1.0 + jnp.tanh(GELU_C * (x + GELU_A * x * x * x)))


def _gelu_and_grad(x):
    t = jnp.tanh(GELU_C * (x + GELU_A * x * x * x))
    g = 0.5 * x * (1.0 + t)
    dg = 0.5 * (1.0 + t) + 0.5 * x * (1.0 - t * t) * GELU_C * (1.0 + 3.0 * GELU_A * x * x)
    return g, dg


def _conv(h, s1, s2, w_ref, b_ref):
    return w_ref[0:1, :] * s2 + w_ref[1:2, :] * s1 + w_ref[2:3, :] * h + b_ref[...]


def _shift_down(main, prev8):
    row = lax.broadcasted_iota(jnp.int32, main.shape, 0)
    s1 = jnp.where(row == 0, prev8[7:8, :], pltpu.roll(main, 1, 0))
    s2 = jnp.where(row == 0, prev8[6:7, :], jnp.where(row == 1, prev8[7:8, :], pltpu.roll(main, 2, 0)))
    return s1, s2


def _shift_up(main, next8):
    n = main.shape[0]
    row = lax.broadcasted_iota(jnp.int32, main.shape, 0)
    u1 = jnp.where(row == n - 1, next8[0:1, :], pltpu.roll(main, n - 1, 0))
    u2 = jnp.where(row == n - 2, next8[0:1, :], jnp.where(row == n - 1, next8[1:2, :], pltpu.roll(main, n - 2, 0)))
    return u1, u2


def _ffn_fwd(h3, w_up, cw, cb, w_down, x2, tgt, g_post, name):
    s = h3.shape[0]
    tn = TN_FF
    nj = D_FF // tn
    per = D_MODEL // tn
    hb = TM // 8

    def body(h_ref, halo_ref, wg_ref, wu_ref, cwg_ref, cwu_ref, cbg_ref, cbu_ref, wd_ref, x_ref, t_ref, g_ref,
             hg_ref, hu_ref, cg_ref, cu_ref, a_ref, loss_ref, dx_ref, dy_ref, dg_ref, y_acc):
        i, j = pl.program_id(0), pl.program_id(1)
        h = h_ref[...]
        halo = halo_ref[...]
        halo = jnp.where(i > 0, halo, jnp.zeros_like(halo))
        conv = []
        for w_ref, cw_ref, cb_ref, hid_ref, c_ref in ((wg_ref, cwg_ref, cbg_ref, hg_ref, cg_ref),
                                                      (wu_ref, cwu_ref, cbu_ref, hu_ref, cu_ref)):
            hm = _dot(h, w_ref[...], NN)
            hid_ref[...] = hm.astype(BF16)
            s1, s2 = _shift_down(hm, _dot(halo, w_ref[...], NN))
            c = _conv(hm, s1, s2, cw_ref, cb_ref)
            c_ref[...] = c.astype(BF16)
            conv.append(c)
        a = (_gelu(conv[0]) * conv[1]).astype(BF16)
        a_ref[...] = a
        contrib = _dot(a, wd_ref[...], NN)

        @pl.when(j == 0)
        def _():
            y_acc[...] = contrib

        @pl.when(j > 0)
        def _():
            y_acc[...] += contrib

        @pl.when(j == nj - 1)
        def _():
            yv = y_acc[...]
            r = _rstd(yv)
            yn = yv * r
            e = x_ref[...] + yn * g_ref[...] - t_ref[...]
            part = 0.5 * jnp.sum(jnp.mean(e * e, axis=-1, keepdims=True), axis=0, keepdims=True)
            part = jnp.broadcast_to(part, (1, 128))
            dx = e * (1.0 / D_MODEL)
            dx_ref[...] = dx
            dy_ref[...] = _norm_bwd_rows(dx * g_ref[...], yn, r).astype(BF16)
            dg = jnp.sum(dx * yn, axis=0, keepdims=True)

            @pl.when(i == 0)
            def _():
                dg_ref[...] = dg
                loss_ref[...] = part

            @pl.when(i > 0)
            def _():
                dg_ref[...] += dg
                loss_ref[...] += part

    rows = pl.BlockSpec((TM, D_MODEL), lambda i, j: (i, 0))
    tile = pl.BlockSpec((TM, tn), lambda i, j: (i, j))
    wide = jax.ShapeDtypeStruct((s, D_FF), BF16)
    return pl.pallas_call(
        body, name=name, grid=(s // TM, nj),
        in_specs=[rows,
                  pl.BlockSpec((8, D_MODEL), lambda i, j: (jnp.maximum(i * hb - 1, 0), 0)),
                  pl.BlockSpec((None, D_MODEL, tn), lambda i, j: (j // per, 0, j % per)),
                  pl.BlockSpec((None, D_MODEL, tn), lambda i, j: (NDEV // 2 + j // per, 0, j % per)),
                  pl.BlockSpec((8, tn), lambda i, j: (0, j)),
                  pl.BlockSpec((8, tn), lambda i, j: (0, nj + j)),
                  pl.BlockSpec((1, tn), lambda i, j: (0, j)),
                  pl.BlockSpec((1, tn), lambda i, j: (0, nj + j)),
                  pl.BlockSpec((tn, D_MODEL), lambda i, j: (j, 0)),
                  rows, rows, pl.BlockSpec((1, D_MODEL), lambda i, j: (0, 0))],
        out_specs=[tile, tile, tile, tile, tile,
                   pl.BlockSpec((1, 128), lambda i, j: (0, 0)), rows, rows,
                   pl.BlockSpec((1, D_MODEL), lambda i, j: (0, 0))],
        out_shape=[wide, wide, wide, wide, wide,
                   jax.ShapeDtypeStruct((1, 128), F32), jax.ShapeDtypeStruct((s, D_MODEL), F32),
                   jax.ShapeDtypeStruct((s, D_MODEL), BF16), jax.ShapeDtypeStruct((1, D_MODEL), F32)],
        scratch_shapes=[pltpu.VMEM((TM, D_MODEL), F32)],
        compiler_params=_cp("arbitrary", "arbitrary"),
    )(h3, h3, w_up, w_up, cw, cw, cb, cb, w_down, x2, tgt, g_post)


def _ffn_bwd(dy3, w_down, hid_g, hid_u, conv_g, conv_u, cw, name):
    s = dy3.shape[0]
    n = s // TM
    tn = TN_FF
    nj = D_FF // tn
    hb = TM // 8
    last8 = s // 8 - 1

    def body(dy_ref, dyn_ref, wd_ref, hg_ref, hu_ref, cg_ref, cgn_ref, cu_ref, cun_ref, cwg_ref, cwu_ref,
             dhg_ref, dhu_ref, dcwg_ref, dcwu_ref, dcbg_ref, dcbu_ref):
        i = pl.program_id(1)
        first, last = i == 0, i == n - 1
        da = _dot(dy_ref[...], wd_ref[...], NT)
        dyn = dyn_ref[...]
        dyn = jnp.where(last, jnp.zeros_like(dyn), dyn)
        da_n = _dot(dyn, wd_ref[...], NT)
        c_g, c_u = cg_ref[...].astype(F32), cu_ref[...].astype(F32)
        g, dg = _gelu_and_grad(c_g)
        gn, dgn = _gelu_and_grad(cgn_ref[...].astype(F32))
        outs = ((da * c_u * dg, da_n * cun_ref[...].astype(F32) * dgn, hg_ref, cwg_ref, dhg_ref, dcwg_ref, dcbg_ref),
                (da * g, da_n * gn, hu_ref, cwu_ref, dhu_ref, dcwu_ref, dcbu_ref))
        row8 = lax.broadcasted_iota(jnp.int32, (8, tn), 0)
        for dc, dcn, h_ref, cw_ref, dh_ref, dcw_ref, dcb_ref in outs:
            u1, u2 = _shift_up(dc, dcn)
            dh_ref[...] = (cw_ref[2:3, :] * dc + cw_ref[1:2, :] * u1 + cw_ref[0:1, :] * u2).astype(BF16)
            hm = h_ref[...].astype(F32)
            dcb = jnp.sum(dc, axis=0, keepdims=True)
            dcw = jnp.where(row8 == 0, jnp.sum(hm * u2, axis=0, keepdims=True),
                            jnp.where(row8 == 1, jnp.sum(hm * u1, axis=0, keepdims=True),
                                      jnp.where(row8 == 2, jnp.sum(hm * dc, axis=0, keepdims=True), 0.0)))

            @pl.when(first)
            def _():
                dcw_ref[...] = dcw
                dcb_ref[...] = dcb

            @pl.when(i > 0)
            def _():
                dcw_ref[...] += dcw
                dcb_ref[...] += dcb

    next8 = lambda j, i: (jnp.minimum((i + 1) * hb, last8), j)
    blk = lambda j, i: (i, j)
    col = lambda j, i: (0, j)
    colu = lambda j, i: (0, nj + j)
    tile = pl.BlockSpec((TM, tn), blk)
    return pl.pallas_call(
        body, name=name, grid=(nj, n),
        in_specs=[pl.BlockSpec((TM, D_MODEL), lambda j, i: (i, 0)),
                  pl.BlockSpec((8, D_MODEL), lambda j, i: (jnp.minimum((i + 1) * hb, last8), 0)),
                  pl.BlockSpec((tn, D_MODEL), lambda j, i: (j, 0)),
                  tile, tile, tile, pl.BlockSpec((8, tn), next8), tile, pl.BlockSpec((8, tn), next8),
                  pl.BlockSpec((8, tn), col), pl.BlockSpec((8, tn), colu)],
        out_specs=[tile, tile, pl.BlockSpec((8, tn), col), pl.BlockSpec((8, tn), col),
                   pl.BlockSpec((1, tn), col), pl.BlockSpec((1, tn), col)],
        out_shape=[jax.ShapeDtypeStruct((s, D_FF), BF16), jax.ShapeDtypeStruct((s, D_FF), BF16),
                   jax.ShapeDtypeStruct((8, D_FF), F32), jax.ShapeDtypeStruct((8, D_FF), F32),
                   jax.ShapeDtypeStruct((1, D_FF), F32), jax.ShapeDtypeStruct((1, D_FF), F32)],
        compiler_params=_cp("parallel", "arbitrary"),
    )(dy3, dy3, w_down, hid_g, hid_u, conv_g, conv_g, conv_u, conv_u, cw, cw)


def _slot(p):
    return 4 * p[0] + 2 * p[1] + p[2]


def _all_gather(shards, name):
    n = len(shards)

    def body(*refs):
        ins, outs = refs[:n], refs[n:2 * n]
        send_sems, recv_sems, local_sems = refs[2 * n:]
        x, y, c = lax.axis_index("x"), lax.axis_index("y"), lax.axis_index("c")
        me, sibling = (x, y, c), (x, y, 1 - c)
        chips = [(1 - x, y), (x, 1 - y), (1 - x, 1 - y)]

        def copy(a, k, block, to, from_input=False):
            dst = outs[a].at[_slot(block)]
            return pltpu.make_async_remote_copy(
                src_ref=ins[a] if from_input else dst, dst_ref=dst,
                send_sem=send_sems.at[a, k], recv_sem=recv_sems.at[a, k],
                device_id=to, device_id_type=MESH)

        mine = [pltpu.make_async_copy(ins[a], outs[a].at[_slot(me)], local_sems.at[a]) for a in range(n)]
        for cp in mine:
            cp.start()
        first = []
        for a in range(n):
            first.append(copy(a, 0, me, sibling, True))
            first += [copy(a, 1 + j, me, (*chip, c), True) for j, chip in enumerate(chips)]
        for cp in first:
            cp.start()
        passed = []
        for j, chip in enumerate(chips):
            for a in range(n):
                copy(a, 1 + j, (*chip, c), me).wait_recv()
                fwd = copy(a, 4 + j, (*chip, c), sibling)
                fwd.start()
                passed.append(fwd)
        for a in range(n):
            copy(a, 0, sibling, me).wait_recv()
            for j, chip in enumerate(chips):
                copy(a, 4 + j, (*chip, 1 - c), me).wait_recv()
        for cp in first + passed:
            cp.wait_send()
        for cp in mine:
            cp.wait()

    any_spec = pl.BlockSpec(memory_space=pl.ANY)
    return pl.pallas_call(
        body, name=name,
        in_specs=[any_spec] * n, out_specs=[any_spec] * n,
        out_shape=[jax.ShapeDtypeStruct((NDEV,) + s.shape, s.dtype) for s in shards],
        scratch_shapes=[pltpu.SemaphoreType.DMA((n, 7)), pltpu.SemaphoreType.DMA((n, 7)),
                        pltpu.SemaphoreType.DMA((n,))],
    )(*shards)


def _peer_list(x, y, c):
    return [(1 - x if m & 4 else x, 1 - y if m & 2 else y, 1 - c if m & 1 else c) for m in range(1, NDEV)]


def _exchange_copies(src_refs, land_refs, send_sems, recv_sems, gather):
    x, y, c = lax.axis_index("x"), lax.axis_index("y"), lax.axis_index("c")
    me = (x, y, c)
    copies = []
    for m, peer in enumerate(_peer_list(x, y, c)):
        for a in range(len(src_refs)):
            copies.append(pltpu.make_async_remote_copy(
                src_ref=src_refs[a] if gather else src_refs[a].at[_slot(peer)], dst_ref=land_refs[a].at[_slot(me)],
                send_sem=send_sems.at[a * (NDEV - 1) + m], recv_sem=recv_sems.at[a * (NDEV - 1) + m],
                device_id=peer, device_id_type=MESH))
    return copies


def _all_gather_small(shards, name):
    n = len(shards)

    def body(*refs):
        ins, outs = refs[:n], refs[n:2 * n]
        send_sems, recv_sems, local_sems = refs[2 * n:]
        me = (lax.axis_index("x"), lax.axis_index("y"), lax.axis_index("c"))
        mine = [pltpu.make_async_copy(ins[a], outs[a].at[_slot(me)], local_sems.at[a]) for a in range(n)]
        copies = _exchange_copies(ins, outs, send_sems, recv_sems, True)
        for cp in mine + copies:
            cp.start()
        for cp in copies + mine:
            cp.wait()

    any_spec = pl.BlockSpec(memory_space=pl.ANY)
    return pl.pallas_call(
        body, name=name,
        in_specs=[any_spec] * n, out_specs=[any_spec] * n,
        out_shape=[jax.ShapeDtypeStruct((NDEV,) + s.shape, s.dtype) for s in shards],
        scratch_shapes=[pltpu.SemaphoreType.DMA((n * (NDEV - 1),)), pltpu.SemaphoreType.DMA((n * (NDEV - 1),)),
                        pltpu.SemaphoreType.DMA((n,))],
    )(*shards)


def _exchange_start(srcs, lands, after, gather, name):
    n = len(srcs)
    hbm = pl.BlockSpec(memory_space=pltpu.HBM)

    def body(*refs):
        for cp in _exchange_copies(refs[:n], refs[n:2 * n], refs[2 * n + 1], refs[2 * n + 2], gather):
            cp.start()
        token = refs[-1]
        token[...] = jnp.zeros_like(token)

    outs = pl.pallas_call(
        body, name=name,
        out_shape=(pltpu.SemaphoreType.DMA((n * (NDEV - 1),)), pltpu.SemaphoreType.DMA((n * (NDEV - 1),)),
                   *[pltpu.HBM(a.shape, a.dtype) for a in list(srcs) + list(lands)],
                   jax.ShapeDtypeStruct((8, 128), F32)),
        in_specs=[hbm] * (2 * n) + [pl.BlockSpec(memory_space=pl.ANY)],
        out_specs=(pl.BlockSpec(memory_space=pltpu.SEMAPHORE), pl.BlockSpec(memory_space=pltpu.SEMAPHORE),
                   *[hbm] * (2 * n), pl.BlockSpec(memory_space=pltpu.VMEM)),
        input_output_aliases={i: 2 + i for i in range(2 * n)},
        compiler_params=pltpu.CompilerParams(has_side_effects=pltpu.SideEffectType.DATAFLOW_SIDE_EFFECTING),
    )(*[pltpu.with_memory_space_constraint(a, pltpu.HBM) for a in list(srcs) + list(lands)], after)
    return outs[0], outs[1], outs[2:2 + n], outs[2 + n:2 + 2 * n], outs[-1]


def _exchange_wait(send_sems, recv_sems, srcs, lands, after, gather, name):
    n = len(srcs)
    hbm = pl.BlockSpec(memory_space=pltpu.HBM)

    def body(*refs):
        for cp in _exchange_copies(refs[:n], refs[n:2 * n], refs[2 * n], refs[2 * n + 1], gather):
            cp.wait_send()
            cp.wait_recv()

    outs = pl.pallas_call(
        body, name=name,
        out_shape=tuple(pltpu.HBM(a.shape, a.dtype) for a in list(srcs) + list(lands)),
        in_specs=[hbm] * (2 * n) + [pl.BlockSpec(memory_space=pltpu.SEMAPHORE)] * 2 + [pl.BlockSpec(memory_space=pl.ANY)],
        out_specs=tuple([hbm] * (2 * n)),
        input_output_aliases={i: i for i in range(2 * n)},
        compiler_params=pltpu.CompilerParams(has_side_effects=pltpu.SideEffectType.DATAFLOW_SIDE_EFFECTING),
    )(*srcs, *lands, send_sems, recv_sems, after)
    return outs[n:]


def _own_slot(block):
    me = 4 * lax.axis_index("x") + 2 * lax.axis_index("y") + lax.axis_index("c")
    return lax.dynamic_update_slice(lax.empty((NDEV,) + block.shape, block.dtype), block[None], (me, 0, 0))


def _adam_update(p_ref, w_ref, m_ref, v_ref, g_ref, d_ref, mo_ref, vo_ref):
    bc1 = 1.0 - ADAM_B1 ** ADAM_STEP
    bc2 = 1.0 - ADAM_B2 ** ADAM_STEP
    g = p_ref[0].astype(F32)
    for d in range(1, NDEV):
        g = g + p_ref[d].astype(F32)
    g_ref[...] = g
    mn = ADAM_B1 * m_ref[...] + (1.0 - ADAM_B1) * g
    vn = ADAM_B2 * v_ref[...] + (1.0 - ADAM_B2) * (g * g)
    mo_ref[...] = mn
    vo_ref[...] = vn
    d_ref[...] = -ADAM_LR * ((mn / bc1) / (jnp.sqrt(vn / bc2) + ADAM_EPS) + ADAM_WD * w_ref[...])


def _adamw_small(parts, ws, ms, vs, name):
    n = len(ws)

    def body(*refs):
        ins, outs = refs[:4 * n], refs[4 * n:]
        for k in range(n):
            _adam_update(ins[k], ins[n + k], ins[2 * n + k], ins[3 * n + k], *outs[4 * k:4 * k + 4])

    whole = pl.BlockSpec(memory_space=pltpu.VMEM)
    res = pl.pallas_call(
        body, name=name, in_specs=[whole] * (4 * n), out_specs=[whole] * (4 * n),
        out_shape=[jax.ShapeDtypeStruct(a.shape, F32) for a in ws for _ in range(4)],
    )(*parts, *ws, *ms, *vs)
    return [res[4 * k:4 * k + 4] for k in range(n)]


def _adamw(parts, w, m, v, name):
    r, c = w.shape
    tr = r if r * c <= 160 * 1024 else max(8, (160 * 1024 // c) // 8 * 8)
    while r % tr:
        tr -= 8
    body = functools.partial(_adam_update)
    spec = pl.BlockSpec((tr, c), lambda i: (i, 0))
    return pl.pallas_call(
        body, name=name, grid=(r // tr,),
        in_specs=[pl.BlockSpec((NDEV, tr, c), lambda i: (0, i, 0)), spec, spec, spec],
        out_specs=[spec] * 4, out_shape=[jax.ShapeDtypeStruct((r, c), F32)] * 4,
        compiler_params=_cp("parallel"),
    )(parts, w, m, v)


def _local_step(x, mem, tgt, gains, b_forget, w_pool, pool_scale, conv_b, w_in,
                mix_weights, ffn_weights, send_in_grad, send_mix_grads, send_ffn_grads):
    b_pad = jnp.pad(b_forget, ((0, 0), (0, 128 - FOX_HEADS)))
    wbd = jnp.zeros((D_POOL, D_POOL), F32)
    for g in range(4):
        wbd = wbd.at[64 * g:64 * g + 64, 64 * g:64 * g + 64].set(w_pool[g])
    wbd = wbd.astype(BF16)
    scale = pool_scale.reshape(1, D_POOL)

    h1, proj, fraw = _proj_in(x, gains["mix_pre"], w_in, "proj_in")
    flog, aq, ak = _gate_cumsum(fraw, b_pad, "gate_cumsum")
    ycat, aqb = _fox_fwd(proj, aq, ak, "fox_fwd")
    ycat = _pool_fwd(proj, wbd, scale, ycat, "pool_fwd")
    w_mix, w_xq, w_xo, w_xkv = mix_weights(ycat)
    y1, x1, h2 = _mm_rows(ycat, w_mix, "nn", 1024, "mix_out", [x], [gains["mix_post"], gains["xa_pre"]],
                          [F32, F32, BF16], _epi_resid)
    q2 = _mm(h2, w_xq, "nn", BF16, 2048, 1024, 1024, "xa_q")
    mem_n = _norm_fwd(mem, gains["mem"], "norm_mem")
    kv = _mm(mem_n, w_xkv, "nn", BF16, mem.shape[0], 256, 1024, "xa_kv", b_cols=256)
    o2 = _xattn_fwd(q2, kv, "xattn_fwd")
    y2, x2, h3 = _mm_rows(o2, w_xo, "nn", 1024, "xa_out", [x1], [gains["xa_post"], gains["ffn_pre"]],
                          [F32, F32, BF16], _epi_resid)
    w_up, w_down, cw = ffn_weights(h3)
    hid_g, hid_u, conv_g, conv_u, act, loss, dx3, dy3, dg_ffn_post = _ffn_fwd(
        h3, w_up, cw, conv_b, w_down, x2, tgt, gains["ffn_post"], "ffn_fwd")

    dhid_g, dhid_u, dcw_g, dcw_u, dcb_g, dcb_u = _ffn_bwd(dy3, w_down, hid_g, hid_u, conv_g, conv_u, cw, "ffn_bwd")
    d_w_down = _mm(act, dy3, "tn", BF16, 2048, 1024, 1024, "dw_down")
    d_w_up = _mm(h3, [dhid_g, dhid_u], "tn", BF16, 1024, 1024, 2048, "dw_up", out_cols=1024)
    sent = send_ffn_grads(d_w_up, d_w_down, jnp.concatenate([dcw_g, dcw_u], axis=1))
    dh3 = _mm([dhid_g, dhid_u], w_up, "nt", F32, 2048, 1024, 1024, "dh_ffn", b_cols=1024, after=sent)
    dx2, dg_ffn_pre, dy2, dg_xa_post = _norm_bwd(dh3, x2, dx3, gains["ffn_pre"], "norm_bwd_ffn",
                                                 prev=(y2, gains["xa_post"]))
    do2 = _mm(dy2, w_xo, "nt", BF16, 2048, 1024, 1024, "d_xa_out")
    d_w_xo = _mm(o2, dy2, "tn", BF16, 1024, 1024, 1024, "dw_xo")
    dq2, dkv = _xattn_bwd(q2, kv, do2, "xattn_bwd")
    dkv = dkv.astype(BF16)
    dx1, dg_xa_pre, dy1, dg_mix_post = _mm_rows(
        dq2, w_xq, "nt", 1024, "dh_xa", [x1, dx2, y1], [gains["xa_pre"], gains["mix_post"]],
        [F32, "sum", BF16, "sum"], _epi_norm_bwd)
    d_w_xq = _mm(h2, dq2, "tn", BF16, 1024, 1024, 1024, "dw_xq")
    dmem_n = _mm(dkv, w_xkv, "nt", F32, mem.shape[0], 1024, 256, "d_mem", b_cols=256)
    d_w_xkv = _mm(mem_n, dkv, "tn", BF16, 1024, 256, mem.shape[0], "dw_xkv", out_cols=256)
    _, dg_mem = _norm_bwd(dmem_n, mem, jnp.zeros_like(mem), gains["mem"], "norm_bwd_mem")
    dycat = _mm(dy1, w_mix, "nt", BF16, 2048, 1024, 1024, "d_mix_out")
    d_w_mix = _mm(ycat, dy1, "tn", BF16, 1024, 1024, 1024, "dw_mix")
    sent_mix = send_mix_grads(d_w_mix, d_w_xq, d_w_xo, d_w_xkv)
    ad = _fox_do_operand(dycat, ycat, sent_mix, "fox_do_operand")
    dq, dk, dv, qaux, kaux = _fox_bwd(proj, dycat, aqb, ak, ad, "fox_bwd")
    du, d_wbd, d_scale = _pool_bwd(proj, dycat, wbd, scale, "pool_bwd")
    df, db_f = _gate_bwd(qaux, kaux, flog, "gate_bwd")
    dproj = [du, dq, dk, dv, df]
    sent_in = send_in_grad(_dw_in(h1, dproj, "dw_in"))
    grad_x, dg_mix_pre = _mm_rows(dproj, w_in, "nt", None, "dh_mix", [x, dx1], [gains["mix_pre"]],
                                  [F32, "sum"], _epi_norm_bwd, after=sent_in)

    small = dict(
        mix_pre=dg_mix_pre, mix_post=dg_mix_post, mem=dg_mem, xa_pre=dg_xa_pre, xa_post=dg_xa_post,
        ffn_pre=dg_ffn_pre, ffn_post=dg_ffn_post,
        conv_b=jnp.concatenate([dcb_g, dcb_u], axis=1),
        w_pool=jnp.concatenate([d_wbd[64 * g:64 * g + 64, 64 * g:64 * g + 64] for g in range(4)], axis=0),
        pool_scale=d_scale.reshape(4, 64),
        b_forget=db_f[:, :FOX_HEADS],
    )
    return loss, grad_x, small


SMALL_ORDER = ("mix_pre", "mix_post", "mem", "xa_pre", "xa_post", "ffn_pre", "ffn_post", "conv_b",
               "w_pool", "pool_scale", "b_forget")


def kernel(x, mem, norm_mix_pre, norm_mix_post, w_in, b_forget, w_pool, pool_scale, w_mix_out, norm_mem, norm_xa_pre, norm_xa_post, w_xq, w_xkv, w_xo, norm_ffn_pre, norm_ffn_post, w_up, conv_w, conv_b, w_down, loss_target, m_norm_mix_pre, m_norm_mix_post, m_w_in, m_b_forget, m_w_pool, m_pool_scale, m_w_mix_out, m_norm_mem, m_norm_xa_pre, m_norm_xa_post, m_w_xq, m_w_xkv, m_w_xo, m_norm_ffn_pre, m_norm_ffn_post, m_w_up, m_conv_w, m_conv_b, m_w_down, v_norm_mix_pre, v_norm_mix_post, v_w_in, v_b_forget, v_w_pool, v_pool_scale, v_w_mix_out, v_norm_mem, v_norm_xa_pre, v_norm_xa_post, v_w_xq, v_w_xkv, v_w_xo, v_norm_ffn_pre, v_norm_ffn_post, v_w_up, v_conv_w, v_conv_b, v_w_down):
    names = ("norm_mix_pre", "norm_mix_post", "w_in", "b_forget", "w_pool", "pool_scale", "w_mix_out", "norm_mem",
             "norm_xa_pre", "norm_xa_post", "w_xq", "w_xkv", "w_xo", "norm_ffn_pre", "norm_ffn_post", "w_up",
             "conv_w", "conv_b", "w_down")
    w = dict(zip(names, (norm_mix_pre, norm_mix_post, w_in, b_forget, w_pool, pool_scale, w_mix_out, norm_mem,
                         norm_xa_pre, norm_xa_post, w_xq, w_xkv, w_xo, norm_ffn_pre, norm_ffn_post, w_up,
                         conv_w, conv_b, w_down)))
    mo = dict(zip(names, (m_norm_mix_pre, m_norm_mix_post, m_w_in, m_b_forget, m_w_pool, m_pool_scale, m_w_mix_out,
                          m_norm_mem, m_norm_xa_pre, m_norm_xa_post, m_w_xq, m_w_xkv, m_w_xo, m_norm_ffn_pre,
                          m_norm_ffn_post, m_w_up, m_conv_w, m_conv_b, m_w_down)))
    vo = dict(zip(names, (v_norm_mix_pre, v_norm_mix_post, v_w_in, v_b_forget, v_w_pool, v_pool_scale, v_w_mix_out,
                          v_norm_mem, v_norm_xa_pre, v_norm_xa_post, v_w_xq, v_w_xkv, v_w_xo, v_norm_ffn_pre,
                          v_norm_ffn_post, v_w_up, v_conv_w, v_conv_b, v_w_down)))

    big_names = ("w_in", "w_mix_out", "w_xq", "w_xo", "w_xkv", "w_up", "w_down")
    shards = {k: w[k][0].astype(BF16) for k in big_names}
    shards["w_in"] = jnp.pad(shards["w_in"], ((0, 0), (0, D_IN_PAD - shards["w_in"].shape[1])))
    conv_w_sh = jnp.pad(conv_w[0, :, 0, :], ((0, 5), (0, 0)))
    (g_in,) = _all_gather([shards["w_in"]], "gather_w_in")
    mix_srcs = [shards[k] for k in ("w_mix_out", "w_xq", "w_xo", "w_xkv")]
    mix_flight = _exchange_start(mix_srcs, [_own_slot(a) for a in mix_srcs], g_in, True, "gather_mix_start")
    ffn_srcs = [shards["w_up"], shards["w_down"], conv_w_sh]
    ffn_flight = _exchange_start(ffn_srcs, [_own_slot(a) for a in ffn_srcs], mix_flight[4], True, "gather_ffn_start")
    my_slot = 4 * lax.axis_index("x") + 2 * lax.axis_index("y") + lax.axis_index("c")
    own_block = lambda a: _own_slot(lax.dynamic_index_in_dim(a, my_slot, 0, keepdims=False))
    by_rows = lambda a: a.reshape(NDEV, a.shape[0] // NDEV, a.shape[1])
    by_cols = lambda a: a.reshape(a.shape[0], NDEV, a.shape[1] // NDEV).transpose(1, 0, 2)
    grad_flight = {}

    def mix_weights(after):
        g_mix, g_xq, g_xo, g_xkv = _exchange_wait(*mix_flight[:4], after, True, "gather_mix_wait")
        return (g_mix.reshape(D_MODEL, D_MODEL), g_xq.reshape(D_MODEL, D_MODEL), g_xo.reshape(D_MODEL, D_MODEL), g_xkv)

    def ffn_weights(after):
        g_up, g_down, g_cw = _exchange_wait(*ffn_flight[:4], after, True, "gather_ffn_wait")
        return g_up, g_down.reshape(D_FF, D_MODEL), g_cw.transpose(1, 0, 2).reshape(8, 2 * D_FF)

    def send_ffn_grads(d_w_up, d_w_down, d_cw):
        srcs = [d_w_up, by_rows(d_w_down), by_cols(d_cw)]
        grad_flight["ffn"] = _exchange_start(srcs, [own_block(a) for a in srcs], ffn_flight[4], False, "scatter_ffn_start")
        return grad_flight["ffn"][4]

    def send_mix_grads(d_w_mix, d_w_xq, d_w_xo, d_w_xkv):
        srcs = [by_rows(d_w_mix), by_rows(d_w_xq), by_rows(d_w_xo), d_w_xkv]
        grad_flight["mix"] = _exchange_start(srcs, [own_block(a) for a in srcs], ffn_flight[4], False, "scatter_mix_start")
        return grad_flight["mix"][4]

    def send_in_grad(d_w_in):
        srcs = [by_rows(d_w_in)]
        grad_flight["in"] = _exchange_start(srcs, [own_block(a) for a in srcs], ffn_flight[4], False, "scatter_in_start")
        return grad_flight["in"][4]

    gains = dict(mix_pre=norm_mix_pre + ffn_flight[4][0, 0], mix_post=norm_mix_post, mem=norm_mem, xa_pre=norm_xa_pre,
                 xa_post=norm_xa_post, ffn_pre=norm_ffn_pre, ffn_post=norm_ffn_post)
    loss, grad_x, small = _local_step(
        x[0], mem[0], loss_target[0], gains, b_forget, w_pool[0], pool_scale[0], conv_b,
        g_in.reshape(D_MODEL, D_IN_PAD), mix_weights, ffn_weights, send_in_grad, send_mix_grads, send_ffn_grads)

    p_up, p_down, p_cw = _exchange_wait(*grad_flight["ffn"][:4], grad_x, False, "scatter_ffn_wait")
    p_mix, p_xq, p_xo, p_xkv = _exchange_wait(*grad_flight["mix"][:4], grad_x, False, "scatter_mix_wait")
    parts = dict(w_mix_out=p_mix, w_xq=p_xq, w_xo=p_xo, w_xkv=p_xkv, w_up=p_up, w_down=p_down)
    *small_parts, loss_parts = _all_gather_small([small[k] for k in SMALL_ORDER] + [loss], "gather_small_grads")

    res = {k: [a[None] for a in _adamw(p, w[k][0], mo[k][0], vo[k][0], "adamw_" + k)] for k, p in parts.items()}
    pad_cw = lambda a: jnp.pad(a[0, :, 0, :], ((0, 5), (0, 0)))
    res["conv_w"] = [a[:3][None, :, None, :] for a in
                     _adamw(p_cw, pad_cw(conv_w), pad_cw(m_conv_w), pad_cw(v_conv_w), "adamw_conv_w")]
    key_of = dict(mix_pre="norm_mix_pre", mix_post="norm_mix_post", mem="norm_mem", xa_pre="norm_xa_pre",
                  xa_post="norm_xa_post", ffn_pre="norm_ffn_pre", ffn_post="norm_ffn_post", conv_b="conv_b",
                  w_pool="w_pool", pool_scale="pool_scale", b_forget="b_forget")
    flat2d = lambda src: [src[key_of[k]].reshape(small[k].shape) for k in SMALL_ORDER]
    small_out = _adamw_small(small_parts, flat2d(w), flat2d(mo), flat2d(vo), "adamw_small")
    for k, four in zip(SMALL_ORDER, small_out):
        res[key_of[k]] = [a.reshape(w[key_of[k]].shape) for a in four]
    (p_in,) = _exchange_wait(*grad_flight["in"][:4], res["w_up"][1], False, "scatter_in_wait")
    res["w_in"] = [a[None] for a in _adamw(p_in[:, :, :w_in.shape[2]], w["w_in"][0], mo["w_in"][0], vo["w_in"][0],
                                           "adamw_w_in")]

    outs = [jnp.sum(loss_parts[:, 0, 0]), grad_x[None]]
    for idx in range(4):
        outs += [res[k][idx] for k in names]
    return tuple(outs)
```

```python
import functools
import math

import jax
import jax.numpy as jnp
from jax import lax
from jax.experimental import pallas as pl
from jax.experimental.pallas import tpu as pltpu

F32 = jnp.float32
BF16 = jnp.bfloat16

NDEV = 8
D_MODEL = 1024
D_POOL = 256
D_FOX = 768
FOX_HEADS = 12
HEAD_PAIRS = FOX_HEADS // 2
XA_HEADS = 4
XA_DIM = 256
D_FF = 4096
D_IN_PAD = 2688
F_COL = 2560
POOL_HALO = 16
NORM_EPS = 1e-6
NEG = -1e30

ADAM_LR = 0.001
ADAM_B1 = 0.9
ADAM_B2 = 0.999
ADAM_EPS = 1e-08
ADAM_WD = 0.01
ADAM_STEP = 10

TM = 512
TQ = 512
TN_FF = 1024
VMEM_LIMIT = 56 * 1024 * 1024
MESH = pl.DeviceIdType.MESH


def _cp(*sem):
    return pltpu.CompilerParams(dimension_semantics=sem, vmem_limit_bytes=VMEM_LIMIT)


def _dot(a, b, dims):
    return lax.dot_general(a, b, (dims, ((), ())), preferred_element_type=F32)


NN = ((1,), (0,))
NT = ((1,), (1,))
TN = ((0,), (0,))


def _mm(a, b, mode, out_dtype, tm, tn, tk, name, b_cols=None, out_cols=None, after=None):
    a_list = list(a) if isinstance(a, (list, tuple)) else [a]
    b_list = list(b) if isinstance(b, (list, tuple)) else [b]
    assert len(a_list) == 1 or len(b_list) == 1
    if mode == "tn":
        K, M = a_list[0].shape
        assert len(a_list) == 1
        Ns = [x.shape[1] for x in b_list]
        N = sum(Ns)
        assert b_cols is None
    else:
        assert len(b_list) == 1
        M = a_list[0].shape[0]
        Ks = [x.shape[1] for x in a_list]
        K = sum(Ks)
        if b_cols is None:
            N = b_list[0].shape[0] if mode == "nt" else b_list[0].shape[1]
        else:
            N = b_list[0].shape[1] if mode == "nt" else NDEV * b_cols
    assert M % tm == 0 and N % tn == 0 and K % tk == 0, (name, M, N, K)
    grid = (M // tm, N // tn, K // tk)
    nk = grid[2]
    dims = {"nn": NN, "nt": NT, "tn": TN}[mode]

    in_specs = []
    if mode == "tn":
        in_specs.append(pl.BlockSpec((tk, tm), lambda i, j, k: (k, i)))
        if len(b_list) == 1:
            in_specs.append(pl.BlockSpec((tk, tn), lambda i, j, k: (k, j)))
        else:
            nj1 = Ns[0] // tn
            in_specs.append(pl.BlockSpec((tk, tn), lambda i, j, k: (k, jnp.minimum(j, nj1 - 1))))
            in_specs.append(pl.BlockSpec((tk, tn), lambda i, j, k: (k, jnp.maximum(j - nj1, 0))))
    else:
        if len(a_list) == 1:
            in_specs.append(pl.BlockSpec((tm, tk), lambda i, j, k: (i, k)))
        else:
            nk1 = Ks[0] // tk
            in_specs.append(pl.BlockSpec((tm, tk), lambda i, j, k: (i, jnp.minimum(k, nk1 - 1))))
            in_specs.append(pl.BlockSpec((tm, tk), lambda i, j, k: (i, jnp.maximum(k - nk1, 0))))
        if b_cols is None:
            if mode == "nn":
                in_specs.append(pl.BlockSpec((tk, tn), lambda i, j, k: (k, j)))
            else:
                in_specs.append(pl.BlockSpec((tn, tk), lambda i, j, k: (j, k)))
        else:
            if mode == "nn":
                per = b_cols // tn
                in_specs.append(pl.BlockSpec((None, tk, tn), lambda i, j, k: (j // per, k, j % per)))
            else:
                per = b_cols // tk
                in_specs.append(pl.BlockSpec((None, tn, tk), lambda i, j, k: (k // per, j, k % per)))
    if out_cols is None:
        out_spec = pl.BlockSpec((tm, tn), lambda i, j, k: (i, j))
        out_shape = jax.ShapeDtypeStruct((M, N), out_dtype)
    else:
        pero = out_cols // tn
        out_spec = pl.BlockSpec((None, tm, tn), lambda i, j, k: (j // pero, i, j % pero))
        out_shape = jax.ShapeDtypeStruct((NDEV, M, out_cols), out_dtype)

    two_a = len(a_list) == 2
    two_b = len(b_list) == 2
    extra = []
    if after is not None:
        in_specs.append(pl.BlockSpec(memory_space=pl.ANY))
        extra.append(after)

    def body(*refs):
        o_ref, acc_ref = refs[-2], refs[-1]
        j = pl.program_id(1)
        k = pl.program_id(2)

        @pl.when(k == 0)
        def _():
            acc_ref[...] = jnp.zeros_like(acc_ref)

        if two_a:
            a1, a2, b1 = refs[0], refs[1], refs[2]
            nk1_ = Ks[0] // tk

            @pl.when(k < nk1_)
            def _():
                acc_ref[...] += _dot(a1[...], b1[...], dims)

            @pl.when(k >= nk1_)
            def _():
                acc_ref[...] += _dot(a2[...], b1[...], dims)
        elif two_b:
            a1, b1, b2 = refs[0], refs[1], refs[2]
            nj1_ = Ns[0] // tn

            @pl.when(j < nj1_)
            def _():
                acc_ref[...] += _dot(a1[...], b1[...], dims)

            @pl.when(j >= nj1_)
            def _():
                acc_ref[...] += _dot(a1[...], b2[...], dims)
        else:
            acc_ref[...] += _dot(refs[0][...], refs[1][...], dims)

        @pl.when(k == nk - 1)
        def _():
            o_ref[...] = acc_ref[...].astype(o_ref.dtype)

    return pl.pallas_call(
        body, name=name, grid=grid, in_specs=in_specs, out_specs=out_spec, out_shape=out_shape,
        scratch_shapes=[pltpu.VMEM((tm, tn), F32)],
        compiler_params=_cp("parallel", "parallel", "arbitrary"),
    )(*a_list, *b_list, *extra)


def _rstd(x):
    return lax.rsqrt(jnp.mean(x * x, axis=-1, keepdims=True) + NORM_EPS)


def _norm_bwd_rows(dxn, xn, r):
    return r * (dxn - xn * jnp.mean(dxn * xn, axis=-1, keepdims=True))


def _row_spec(tm, d):
    return pl.BlockSpec((tm, d), lambda i: (i, 0))


def _vec_spec(d):
    return pl.BlockSpec((1, d), lambda i: (0, 0))


def _mm_rows(a, b, mode, tk, name, rows, vecs, outs, epilogue, b_cols=None, after=None):
    a_list = list(a) if isinstance(a, (list, tuple)) else [a]
    m = a_list[0].shape[0]
    ks = [x.shape[1] for x in a_list]
    n = D_MODEL
    pieces = tk is None
    nk = 1 if pieces else sum(ks) // tk
    dims = NN if mode == "nn" else NT
    if pieces:
        assert mode == "nt" and b_cols is None
        in_specs = [pl.BlockSpec((TM, kp), lambda i, k: (i, 0)) for kp in ks]
        tk = sum(ks)
    elif len(a_list) == 1:
        in_specs = [pl.BlockSpec((TM, tk), lambda i, k: (i, k))]
    else:
        nk1 = ks[0] // tk
        in_specs = [pl.BlockSpec((TM, tk), lambda i, k: (i, jnp.minimum(k, nk1 - 1))),
                    pl.BlockSpec((TM, tk), lambda i, k: (i, jnp.maximum(k - nk1, 0)))]
    if mode == "nn":
        in_specs.append(pl.BlockSpec((tk, n), lambda i, k: (k, 0)))
    elif b_cols is None:
        in_specs.append(pl.BlockSpec((n, tk), lambda i, k: (0, k)))
    else:
        per = b_cols // tk
        in_specs.append(pl.BlockSpec((None, n, tk), lambda i, k: (k // per, 0, k % per)))
    in_specs += [pl.BlockSpec((TM, n), lambda i, k: (i, 0))] * len(rows)
    in_specs += [pl.BlockSpec((1, n), lambda i, k: (0, 0))] * len(vecs)
    extra = []
    if after is not None:
        in_specs.append(pl.BlockSpec(memory_space=pl.ANY))
        extra.append(after)
    out_specs, out_shape = [], []
    for o in outs:
        if o == "sum":
            out_specs.append(pl.BlockSpec((1, n), lambda i, k: (0, 0)))
            out_shape.append(jax.ShapeDtypeStruct((1, n), F32))
        else:
            out_specs.append(pl.BlockSpec((TM, n), lambda i, k: (i, 0)))
            out_shape.append(jax.ShapeDtypeStruct((m, n), o))
    na, nr, nv = len(a_list), len(rows), len(vecs)

    def body(*refs):
        a_refs, b_ref = refs[:na], refs[na]
        row_refs = refs[na + 1:na + 1 + nr]
        vec_refs = refs[na + 1 + nr:na + 1 + nr + nv]
        out_refs = refs[len(refs) - 1 - len(outs):len(refs) - 1]
        acc_ref = refs[-1]
        i, k = pl.program_id(0), pl.program_id(1)

        @pl.when(k == 0)
        def _():
            acc_ref[...] = jnp.zeros_like(acc_ref)

        if pieces:
            off = 0
            for a_ref in a_refs:
                kp = a_ref.shape[1]
                acc_ref[...] += _dot(a_ref[...], b_ref[:, off:off + kp], dims)
                off += kp
        elif na == 1:
            acc_ref[...] += _dot(a_refs[0][...], b_ref[...], dims)
        else:
            nk1_ = ks[0] // tk

            @pl.when(k < nk1_)
            def _():
                acc_ref[...] += _dot(a_refs[0][...], b_ref[...], dims)

            @pl.when(k >= nk1_)
            def _():
                acc_ref[...] += _dot(a_refs[1][...], b_ref[...], dims)

        @pl.when(k == nk - 1)
        def _():
            vals = epilogue(acc_ref[...], [r[...] for r in row_refs], [v[...] for v in vec_refs])
            for o, ref, val in zip(outs, out_refs, vals):
                if o == "sum":
                    @pl.when(i == 0)
                    def _():
                        ref[...] = val

                    @pl.when(i > 0)
                    def _():
                        ref[...] += val
                else:
                    ref[...] = val.astype(o)

    return pl.pallas_call(
        body, name=name, grid=(m // TM, nk), in_specs=in_specs, out_specs=out_specs, out_shape=out_shape,
        scratch_shapes=[pltpu.VMEM((TM, n), F32)],
        compiler_params=_cp("arbitrary", "arbitrary"),
    )(*a_list, b, *rows, *vecs, *extra)


def _proj_in(x, g, w_in, name):
    s, d = x.shape
    n = w_in.shape[1]

    def body(x_ref, g_ref, w_ref, h_ref, p_ref, f_ref):
        xv = x_ref[...]
        h = (xv * _rstd(xv) * g_ref[...]).astype(BF16)
        h_ref[...] = h
        acc = _dot(h, w_ref[...], NN)
        p_ref[...] = acc.astype(BF16)
        f_ref[...] = acc[:, F_COL:]

    return pl.pallas_call(
        body, name=name, grid=(s // TM,),
        in_specs=[_row_spec(TM, d), _vec_spec(d), pl.BlockSpec((d, n), lambda i: (0, 0))],
        out_specs=[_row_spec(TM, d), _row_spec(TM, n), _row_spec(TM, n - F_COL)],
        out_shape=[jax.ShapeDtypeStruct((s, d), BF16), jax.ShapeDtypeStruct((s, n), BF16),
                   jax.ShapeDtypeStruct((s, n - F_COL), F32)],
        compiler_params=_cp("parallel"),
    )(x, g, w_in)


def _dw_in(h, pieces, name):
    s, d = h.shape
    n = sum(p.shape[1] for p in pieces)
    tk = 1024
    nk = s // tk

    def body(*refs):
        h_ref, piece_refs, o_ref, acc_ref = refs[0], refs[1:-2], refs[-2], refs[-1]
        k = pl.program_id(1)

        @pl.when(k == 0)
        def _():
            acc_ref[...] = jnp.zeros_like(acc_ref)

        off = 0
        for p_ref in piece_refs:
            w = p_ref.shape[1]
            acc_ref[:, off:off + w] += _dot(h_ref[...], p_ref[...], TN)
            off += w

        @pl.when(k == nk - 1)
        def _():
            o_ref[...] = acc_ref[...].astype(BF16)

    return pl.pallas_call(
        body, name=name, grid=(d // TM, nk),
        in_specs=[pl.BlockSpec((tk, TM), lambda i, k: (k, i))] +
                 [pl.BlockSpec((tk, p.shape[1]), lambda i, k: (k, 0)) for p in pieces],
        out_specs=pl.BlockSpec((TM, n), lambda i, k: (i, 0)),
        out_shape=jax.ShapeDtypeStruct((d, n), BF16),
        scratch_shapes=[pltpu.VMEM((TM, n), F32)],
        compiler_params=_cp("parallel", "arbitrary"),
    )(h, *pieces)


def _epi_resid(y, rows, vecs):
    (x_in,), (g_post, g_next) = rows, vecs
    xo = x_in + y * _rstd(y) * g_post
    return y, xo, xo * _rstd(xo) * g_next


def _epi_norm_bwd(dh, rows, vecs):
    x, dx_res = rows[0], rows[1]
    r = _rstd(x)
    xn = x * r
    dx = dx_res + _norm_bwd_rows(dh * vecs[0], xn, r)
    res = [dx, jnp.sum(dh * xn, axis=0, keepdims=True)]
    if len(rows) == 3:
        y = rows[2]
        r2 = _rstd(y)
        yn = y * r2
        res += [_norm_bwd_rows(dx * vecs[1], yn, r2), jnp.sum(dx * yn, axis=0, keepdims=True)]
    return res


def _norm_fwd(x, g, name):
    s, d = x.shape
    tm = min(TM, s)

    def body(x_ref, g_ref, h_ref):
        xv = x_ref[...]
        h_ref[...] = (xv * _rstd(xv) * g_ref[...]).astype(BF16)

    return pl.pallas_call(
        body, name=name, grid=(s // tm,), in_specs=[_row_spec(tm, d), _vec_spec(d)],
        out_specs=_row_spec(tm, d), out_shape=jax.ShapeDtypeStruct((s, d), BF16),
        compiler_params=_cp("parallel"),
    )(x, g)


def _norm_bwd(dh, x, dx_res, g_pre, name, prev=None):
    s, d = x.shape
    tm = min(TM, s)
    has_prev = prev is not None

    def body(*refs):
        if has_prev:
            dh_ref, x_ref, dr_ref, g_ref, y_ref, gp_ref, dx_ref, dg_ref, dy_ref, dgp_ref = refs
        else:
            dh_ref, x_ref, dr_ref, g_ref, dx_ref, dg_ref = refs
        i = pl.program_id(0)
        xv = x_ref[...]
        r = _rstd(xv)
        xn = xv * r
        dhv = dh_ref[...].astype(F32)
        dx = dr_ref[...] + _norm_bwd_rows(dhv * g_ref[...], xn, r)
        dx_ref[...] = dx
        dg = jnp.sum(dhv * xn, axis=0, keepdims=True)

        @pl.when(i == 0)
        def _():
            dg_ref[...] = dg

        @pl.when(i > 0)
        def _():
            dg_ref[...] += dg

        if has_prev:
            yv = y_ref[...]
            r2 = _rstd(yv)
            yn = yv * r2
            dy_ref[...] = _norm_bwd_rows(dx * gp_ref[...], yn, r2).astype(BF16)
            dgp = jnp.sum(dx * yn, axis=0, keepdims=True)

            @pl.when(i == 0)
            def _():
                dgp_ref[...] = dgp

            @pl.when(i > 0)
            def _():
                dgp_ref[...] += dgp

    in_specs = [_row_spec(tm, d), _row_spec(tm, d), _row_spec(tm, d), _vec_spec(d)]
    out_specs = [_row_spec(tm, d), _vec_spec(d)]
    out_shape = [jax.ShapeDtypeStruct((s, d), F32), jax.ShapeDtypeStruct((1, d), F32)]
    args = [dh, x, dx_res, g_pre]
    if has_prev:
        in_specs += [_row_spec(tm, d), _vec_spec(d)]
        out_specs += [_row_spec(tm, d), _vec_spec(d)]
        out_shape += [jax.ShapeDtypeStruct((s, d), BF16), jax.ShapeDtypeStruct((1, d), F32)]
        args += list(prev)
    return pl.pallas_call(
        body, name=name, grid=(s // tm,), in_specs=in_specs, out_specs=out_specs, out_shape=out_shape,
        compiler_params=_cp("arbitrary"),
    )(*args)


def _split3(v):
    hi = v.astype(BF16)
    r1 = v - hi.astype(F32)
    mid = r1.astype(BF16)
    lo = (r1 - mid.astype(F32)).astype(BF16)
    return hi, mid, lo


def _tri_dot(tri, v):
    hi, mid, lo = _split3(v)
    return _dot(tri, hi, NN) + _dot(tri, mid, NN) + _dot(tri, lo, NN)


def _gate_cumsum(fraw, b_pad, name):
    s = fraw.shape[0]
    width = HEAD_PAIRS * 128

    def body(f_ref, b_ref, flog_ref, aq_ref, ak_ref, carry_ref):
        i = pl.program_id(0)

        @pl.when(i == 0)
        def _():
            carry_ref[...] = jnp.zeros_like(carry_ref)

        flog = f_ref[...] + b_ref[...]
        flog_ref[...] = flog
        lf = jnp.minimum(flog, 0.0) - jnp.log(1.0 + jnp.exp(-jnp.abs(flog)))
        lane = lax.broadcasted_iota(jnp.int32, (1, 128), 1)
        lf = jnp.where(lane < FOX_HEADS, lf, 0.0)
        row = lax.broadcasted_iota(jnp.int32, (TM, TM), 0)
        col = lax.broadcasted_iota(jnp.int32, (TM, TM), 1)
        tri = (row >= col).astype(BF16)
        cum = _tri_dot(tri, lf) + carry_ref[...]
        carry_ref[...] = cum[TM - 1:TM, :]
        aq_ref[...], ak_ref[...] = _fox_operands(cum)

    return pl.pallas_call(
        body, name=name, grid=(s // TM,),
        in_specs=[_row_spec(TM, 128), _vec_spec(128)],
        out_specs=[_row_spec(TM, 128), _row_spec(TM, width), _row_spec(TM, width)],
        out_shape=[jax.ShapeDtypeStruct((s, 128), F32), jax.ShapeDtypeStruct((s, width), BF16),
                   jax.ShapeDtypeStruct((s, width), BF16)],
        scratch_shapes=[pltpu.VMEM((1, 128), F32)],
        compiler_params=_cp("arbitrary"),
    )(fraw, b_pad)


def _gate_bwd(qaux, kaux, flog, name):
    s = flog.shape[0]
    n = s // TM

    def body(qa_ref, ka_ref, fl_ref, dp_ref, db_ref, carry_ref):
        i = pl.program_id(0)

        @pl.when(i == 0)
        def _():
            carry_ref[...] = jnp.zeros_like(carry_ref)

        src = lax.broadcasted_iota(jnp.int32, (128, 128), 0)
        dst = lax.broadcasted_iota(jnp.int32, (128, 128), 1)
        dcum = jnp.zeros((TM, 128), F32)
        for p in range(HEAD_PAIRS):
            for ref, l0, l1, sign in ((qa_ref, 64, 0, 1.0), (ka_ref, 67, 3, -1.0)):
                hit = jnp.logical_or(jnp.logical_and(src == l0, dst == 2 * p),
                                     jnp.logical_and(src == l1, dst == 2 * p + 1))
                sel = jnp.where(hit, sign, 0.0).astype(BF16)
                for piece in _split3(ref[p]):
                    dcum = dcum + _dot(piece, sel, NN)
        row = lax.broadcasted_iota(jnp.int32, (TM, TM), 0)
        col = lax.broadcasted_iota(jnp.int32, (TM, TM), 1)
        tri = (row <= col).astype(BF16)
        dlf = _tri_dot(tri, dcum) + carry_ref[...]
        carry_ref[...] = dlf[0:1, :]
        lane = lax.broadcasted_iota(jnp.int32, (1, 128), 1)
        df = jnp.where(lane < FOX_HEADS, dlf / (1.0 + jnp.exp(fl_ref[...])), 0.0)
        dp_ref[...] = df.astype(BF16)
        db = jnp.sum(df, axis=0, keepdims=True)

        @pl.when(i == 0)
        def _():
            db_ref[...] = db

        @pl.when(i > 0)
        def _():
            db_ref[...] += db

    rev = lambda i: (n - 1 - i, 0)
    return pl.pallas_call(
        body, name=name, grid=(n,),
        in_specs=[pl.BlockSpec((HEAD_PAIRS, TM, 128), lambda i: (0, n - 1 - i, 0)),
                  pl.BlockSpec((HEAD_PAIRS, TM, 128), lambda i: (0, n - 1 - i, 0)), pl.BlockSpec((TM, 128), rev)],
        out_specs=[pl.BlockSpec((TM, 128), rev), _vec_spec(128)],
        out_shape=[jax.ShapeDtypeStruct((s, 128), BF16), jax.ShapeDtypeStruct((1, 128), F32)],
        scratch_shapes=[pltpu.VMEM((1, 128), F32)],
        compiler_params=_cp("arbitrary"),
    )(qaux, kaux, flog)


def _pool_consts(i, rows):
    lane = lax.broadcasted_iota(jnp.int32, (rows, D_POOL), 1)
    t1 = lax.broadcasted_iota(jnp.int32, (rows, D_POOL), 0) + i * TM + 1
    win = jnp.where(lane < 64, 2, jnp.where(lane < 128, 4, jnp.where(lane < 192, 8, 16)))
    inv = 1.0 / jnp.minimum(t1, win).astype(F32)
    return lane, inv


def _by_group(lane, s2, s4, s8, s16):
    return jnp.where(lane < 64, s2, jnp.where(lane < 128, s4, jnp.where(lane < 192, s8, s16)))


def _pool_diff(i, u_ref, halo_ref):
    u = u_ref[...].astype(F32)
    halo = jnp.where(i > 0, halo_ref[...].astype(F32), 0.0)
    ext = jnp.concatenate([halo, u], axis=0)
    s2 = ext + pltpu.roll(ext, 1, 0)
    s4 = s2 + pltpu.roll(s2, 2, 0)
    s8 = s4 + pltpu.roll(s4, 4, 0)
    s16 = s8 + pltpu.roll(s8, 8, 0)
    lane, inv = _pool_consts(i, TM)
    sel = _by_group(lane, s2[POOL_HALO:], s4[POOL_HALO:], s8[POOL_HALO:], s16[POOL_HALO:])
    return sel * inv - u


def _pool_fwd(proj, wbd, scale, ycat, name):
    s = proj.shape[0]
    hb = TM // POOL_HALO

    def body(u_ref, halo_ref, w_ref, sc_ref, y_any, y_ref):
        del y_any
        i = pl.program_id(0)
        diff = _pool_diff(i, u_ref, halo_ref)
        mixed = _dot(diff.astype(BF16), w_ref[...], NN)
        y_ref[...] = (mixed * sc_ref[...]).astype(BF16)

    return pl.pallas_call(
        body, name=name, grid=(s // TM,),
        in_specs=[pl.BlockSpec((TM, D_POOL), lambda i: (i, 0)),
                  pl.BlockSpec((POOL_HALO, D_POOL), lambda i: (jnp.maximum(i * hb - 1, 0), 0)),
                  pl.BlockSpec((D_POOL, D_POOL), lambda i: (0, 0)), _vec_spec(D_POOL),
                  pl.BlockSpec(memory_space=pl.ANY)],
        out_specs=pl.BlockSpec((TM, D_POOL), lambda i: (i, 0)),
        out_shape=jax.ShapeDtypeStruct(ycat.shape, ycat.dtype),
        input_output_aliases={4: 0},
        compiler_params=_cp("parallel"),
    )(proj, proj, wbd, scale, ycat)


def _pool_bwd(proj, dycat, wbd, scale, name):
    s = proj.shape[0]
    n = s // TM
    hb = TM // POOL_HALO
    last_halo = s // POOL_HALO - 1

    def body(u_ref, halo_ref, dy_ref, dyp_ref, w_ref, sc_ref, dp_ref, dw_ref, dsc_ref):
        i = pl.program_id(0)
        diff = _pool_diff(i, u_ref, halo_ref)
        diff_b = diff.astype(BF16)
        mixed = _dot(diff_b, w_ref[...], NN)
        dy = dy_ref[...].astype(F32)
        dmix = (dy * sc_ref[...]).astype(BF16)
        dyp = jnp.where(i < n - 1, dyp_ref[...].astype(F32), 0.0)
        dmix_p = (dyp * sc_ref[...]).astype(BF16)
        dd = _dot(dmix, w_ref[...], NT)
        dd_p = _dot(dmix_p, w_ref[...], NT)
        lane, inv = _pool_consts(i, TM)
        _, inv_p = _pool_consts(i + 1, POOL_HALO)
        ext = jnp.concatenate([dd * inv, dd_p * inv_p], axis=0)
        rows = TM + POOL_HALO
        l2 = ext + pltpu.roll(ext, rows - 1, 0)
        l4 = l2 + pltpu.roll(l2, rows - 2, 0)
        l8 = l4 + pltpu.roll(l4, rows - 4, 0)
        l16 = l8 + pltpu.roll(l8, rows - 8, 0)
        du = _by_group(lane, l2[:TM], l4[:TM], l8[:TM], l16[:TM]) - dd
        dp_ref[...] = du.astype(BF16)
        dw = _dot(diff_b, dmix, TN)
        dsc = jnp.sum(dy * mixed, axis=0, keepdims=True)

        @pl.when(i == 0)
        def _():
            dw_ref[...] = dw
            dsc_ref[...] = dsc

        @pl.when(i > 0)
        def _():
            dw_ref[...] += dw
            dsc_ref[...] += dsc

    return pl.pallas_call(
        body, name=name, grid=(n,),
        in_specs=[pl.BlockSpec((TM, D_POOL), lambda i: (i, 0)),
                  pl.BlockSpec((POOL_HALO, D_POOL), lambda i: (jnp.maximum(i * hb - 1, 0), 0)),
                  pl.BlockSpec((TM, D_POOL), lambda i: (i, 0)),
                  pl.BlockSpec((POOL_HALO, D_POOL), lambda i: (jnp.minimum((i + 1) * hb, last_halo), 0)),
                  pl.BlockSpec((D_POOL, D_POOL), lambda i: (0, 0)), _vec_spec(D_POOL)],
        out_specs=[pl.BlockSpec((TM, D_POOL), lambda i: (i, 0)),
                   pl.BlockSpec((D_POOL, D_POOL), lambda i: (0, 0)), _vec_spec(D_POOL)],
        out_shape=[jax.ShapeDtypeStruct((s, D_POOL), BF16),
                   jax.ShapeDtypeStruct((D_POOL, D_POOL), F32), jax.ShapeDtypeStruct((1, D_POOL), F32)],
        compiler_params=_cp("arbitrary"),
    )(proj, proj, dycat, dycat, wbd, scale)


Q_BLK = D_POOL // 128
K_BLK = Q_BLK + D_FOX // 128
V_BLK = K_BLK + D_FOX // 128


def _operand_rows(v0, v1, ones_off):
    row = lax.broadcasted_iota(jnp.int32, (128, 1), 0)
    half = row & 63
    out = jnp.where(jnp.logical_and(half >= ones_off, half < ones_off + 3), 1.0, 0.0) + jnp.zeros_like(v0)
    for base, v in ((64, v0), (0, v1)):
        for j, piece in enumerate(_split3(v)):
            out = jnp.where(row == base + j, piece.astype(F32), out)
    return out


def _fox_operands(cum):
    width = HEAD_PAIRS * 128
    pieces = _split3(cum)
    row = lax.broadcasted_iota(jnp.int32, (128, width), 0)
    col = lax.broadcasted_iota(jnp.int32, (128, width), 1)
    base = (row >> 1) * 128 + (1 - (row & 1)) * 64
    half = lax.broadcasted_iota(jnp.int32, (1, width), 1) & 63
    res = []
    for off, sign, ones_off in ((0, 1.0, 3), (3, -1.0, 0)):
        out = jnp.where(jnp.logical_and(half >= ones_off, half < ones_off + 3), 1.0, 0.0)
        for j, piece in enumerate(pieces):
            sel = jnp.where(jnp.logical_and(col == base + off + j, row < FOX_HEADS), sign, 0.0).astype(BF16)
            out = out + _dot(piece, sel, NN)
        res.append(out.astype(BF16))
    return res


def _fox_do_operand(dycat, ycat, after, name):
    s = dycat.shape[0]

    rb = 1024
    nblk = D_FOX // D_POOL

    def body(*refs):
        do_refs, o_refs, ad_ref = refs[:nblk], refs[nblk:2 * nblk], refs[-1]
        src = lax.broadcasted_iota(jnp.int32, (D_POOL, D_POOL), 0)
        dst = lax.broadcasted_iota(jnp.int32, (D_POOL, D_POOL), 1)
        same_pair = (src >> 7) == (dst >> 7)
        s_in, d_in = src & 127, dst & 127
        hit = jnp.logical_and(same_pair, jnp.logical_or(
            jnp.logical_and(s_in < 64, jnp.logical_and(d_in >= 64, d_in < 67)), jnp.logical_and(s_in >= 64, d_in < 3)))
        sel = jnp.where(hit, 1.0, 0.0).astype(BF16)
        j = lax.broadcasted_iota(jnp.int32, (1, D_POOL), 1) & 63
        for b in range(nblk):
            dd = do_refs[b][...].astype(F32) * o_refs[b][...].astype(F32)
            dsum = jnp.zeros(dd.shape, F32)
            for piece in _split3(dd):
                dsum = dsum + _dot(piece, sel, NN)
            hi, mid, lo = _split3(-dsum)
            ad_ref[:, D_POOL * b:D_POOL * (b + 1)] = jnp.where(j == 0, hi, jnp.where(j == 1, mid, lo))

    blks = [pl.BlockSpec((rb, D_POOL), functools.partial(lambda i, b: (i, 1 + b), b=b)) for b in range(nblk)]
    return pl.pallas_call(
        body, name=name, grid=(s // rb,), in_specs=blks + blks + [pl.BlockSpec(memory_space=pl.ANY)],
        out_specs=pl.BlockSpec((rb, D_FOX), lambda i: (i, 0)),
        out_shape=jax.ShapeDtypeStruct((s, D_FOX), BF16),
        compiler_params=_cp("parallel"),
    )(*[dycat] * nblk, *[ycat] * nblk, after)


def _causal_pairs(nq, key_major):
    if key_major:
        pairs = [(q, k) for k in range(nq) for q in range(k, nq)]
    else:
        pairs = [(q, k) for q in range(nq) for k in range(q + 1)]
    return (jnp.asarray([p[0] for p in pairs], jnp.int32), jnp.asarray([p[1] for p in pairs], jnp.int32))


def _fox_fwd(proj, aq, ak, name):
    s = proj.shape[0]
    nq = s // TQ
    qi_arr, ki_arr = _causal_pairs(nq, key_major=False)

    def body(qi_ref, ki_ref, q_ref, k_ref, v_ref, aq_ref, ak_ref, o_ref, aqb_ref, m0_ref, m1_ref, acc_ref, aux_ref):
        t = pl.program_id(1)
        qi, ki = qi_ref[t], ki_ref[t]
        lane = lax.broadcasted_iota(jnp.int32, (1, 128), 1)
        masks = [lane < 64, lane >= 64]
        ones_v = jnp.where((lane & 63) == 8, 1.0, 0.0).astype(BF16)
        top = lax.broadcasted_iota(jnp.int32, (128, 1), 0) < 64
        m_ref = [m0_ref, m1_ref]

        @pl.when(ki == 0)
        def _():
            m0_ref[...] = jnp.full_like(m0_ref, NEG)
            m1_ref[...] = jnp.full_like(m1_ref, NEG)
            acc_ref[...] = jnp.zeros_like(acc_ref)
            aux_ref[...] = jnp.zeros_like(aux_ref)

        def step(diag):
            q2s = q_ref[...] * 0.125
            k2, v2, aq2, ak2 = k_ref[...], v_ref[...], aq_ref[...], ak_ref[...]
            pv, alpha = [], []
            for hh in range(2):
                qh = jnp.where(masks[hh], q2s, aq2)
                kh = jnp.where(masks[hh], k2, ak2)
                vh = jnp.where(masks[hh], v2, ones_v)
                sc = _dot(kh, qh, NT)
                if diag:
                    key = lax.broadcasted_iota(jnp.int32, sc.shape, 0)
                    qry = lax.broadcasted_iota(jnp.int32, sc.shape, 1)
                    sc = jnp.where(qry >= key, sc, NEG)
                m_prev = m_ref[hh][...]
                m_new = jnp.maximum(m_prev, jnp.max(sc, axis=0, keepdims=True))
                m_ref[hh][...] = m_new
                alpha.append(jnp.exp(m_prev - m_new))
                pv.append(_dot(vh, jnp.exp(sc - m_new).astype(BF16), TN))
            acc_ref[...] = acc_ref[...] * jnp.where(top, alpha[0], alpha[1]) + jnp.where(top, pv[0], pv[1])
            aux_ref[...] = aux_ref[...] * jnp.where(top, alpha[1], alpha[0]) + jnp.where(top, pv[1], pv[0])

        @pl.when(ki < qi)
        def _():
            step(False)

        @pl.when(ki == qi)
        def _():
            step(True)
            aux = aux_ref[...]
            l0, l1 = aux[72:73, :], aux[8:9, :]
            o_ref[...] = (acc_ref[...] * jnp.where(top, 1.0 / l0, 1.0 / l1)).T.astype(BF16)
            aqt = aq_ref[...].astype(F32).T
            cum0 = aqt[64:65, :] + aqt[65:66, :] + aqt[66:67, :]
            cum1 = aqt[0:1, :] + aqt[1:2, :] + aqt[2:3, :]
            aqb = _operand_rows(cum0 - (m0_ref[...] + jnp.log(l0)), cum1 - (m1_ref[...] + jnp.log(l1)), 3)
            aqb_ref[...] = aqb.T.astype(BF16)

    grid_spec = pltpu.PrefetchScalarGridSpec(
        num_scalar_prefetch=2, grid=(HEAD_PAIRS, int(qi_arr.shape[0])),
        in_specs=[pl.BlockSpec((TQ, 128), lambda p, t, qi, ki: (qi[t], Q_BLK + p)),
                  pl.BlockSpec((TQ, 128), lambda p, t, qi, ki: (ki[t], K_BLK + p)),
                  pl.BlockSpec((TQ, 128), lambda p, t, qi, ki: (ki[t], V_BLK + p)),
                  pl.BlockSpec((TQ, 128), lambda p, t, qi, ki: (qi[t], p)),
                  pl.BlockSpec((TQ, 128), lambda p, t, qi, ki: (ki[t], p))],
        out_specs=[pl.BlockSpec((TQ, 128), lambda p, t, qi, ki: (qi[t], Q_BLK + p)),
                   pl.BlockSpec((TQ, 128), lambda p, t, qi, ki: (qi[t], p))],
        scratch_shapes=[pltpu.VMEM((1, TQ), F32), pltpu.VMEM((1, TQ), F32),
                        pltpu.VMEM((128, TQ), F32), pltpu.VMEM((128, TQ), F32)])
    return pl.pallas_call(
        body, name=name, grid_spec=grid_spec,
        out_shape=[jax.ShapeDtypeStruct((s, D_MODEL), BF16), jax.ShapeDtypeStruct((s, HEAD_PAIRS * 128), BF16)],
        compiler_params=_cp("parallel", "arbitrary"),
    )(qi_arr, ki_arr, proj, proj, proj, aq, ak)


def _fox_bwd(proj, dycat, aqb, ak, ad, name):
    s = proj.shape[0]
    nq = s // TQ
    qi_arr, ki_arr = _causal_pairs(nq, key_major=True)

    def body(qi_ref, ki_ref, q_ref, k_ref, v_ref, do_ref, aq_ref, ak_ref, ad_ref,
             dq_ref, dk_ref, dv_ref, qaux_ref, kaux_ref, dq_acc, qaux_acc, dk_acc, dv_acc, kaux_acc):
        t = pl.program_id(1)
        qi, ki = qi_ref[t], ki_ref[t]
        lane = lax.broadcasted_iota(jnp.int32, (1, 128), 1)
        masks = [lane < 64, lane >= 64]
        ones_v = jnp.where((lane & 63) < 3, 1.0, 0.0).astype(BF16)
        top = lax.broadcasted_iota(jnp.int32, (128, 1), 0) < 64

        @pl.when(qi == ki)
        def _():
            dk_acc[...] = jnp.zeros_like(dk_acc)
            dv_acc[...] = jnp.zeros_like(dv_acc)
            kaux_acc[...] = jnp.zeros_like(kaux_acc)

        def step(diag):
            q2s = q_ref[...] * 0.125
            k2, v2, do2 = k_ref[...], v_ref[...], do_ref[...]
            aq2, ak2, ad2 = aq_ref[...], ak_ref[...], ad_ref[...]
            dq, dk, dv = [], [], []
            for hh in range(2):
                qh = jnp.where(masks[hh], q2s, aq2)
                kh = jnp.where(masks[hh], k2, ak2)
                doh = jnp.where(masks[hh], do2, ad2)
                vh = jnp.where(masks[hh], v2, ones_v)
                sc = _dot(kh, qh, NT)
                if diag:
                    key = lax.broadcasted_iota(jnp.int32, sc.shape, 0)
                    qry = lax.broadcasted_iota(jnp.int32, sc.shape, 1)
                    sc = jnp.where(qry >= key, sc, NEG)
                p = jnp.exp(sc)
                dsb = (p * _dot(vh, doh, NT)).astype(BF16)
                dv.append(_dot(p.astype(BF16), doh, NN))
                dk.append(_dot(dsb, qh, NN))
                dq.append(_dot(kh, dsb, TN))
            dk_acc[...] += jnp.where(masks[0], dk[0], dk[1])
            kaux_acc[...] += jnp.where(masks[0], dk[1], dk[0])
            dv_acc[...] += jnp.where(masks[0], dv[0], dv[1])
            dq_new = jnp.where(top, dq[0], dq[1])
            qaux_new = jnp.where(top, dq[1], dq[0])

            @pl.when(ki == 0)
            def _():
                dq_acc[qi] = dq_new
                qaux_acc[qi] = qaux_new

            @pl.when(ki > 0)
            def _():
                dq_acc[qi] += dq_new
                qaux_acc[qi] += qaux_new

        @pl.when(qi > ki)
        def _():
            step(False)

        @pl.when(qi == ki)
        def _():
            step(True)
            rows = pl.ds(pl.multiple_of(qi * TQ, TQ), TQ)
            dq_ref[rows, :] = (dq_acc[qi] * 0.125).T.astype(BF16)
            qaux_ref[rows, :] = qaux_acc[qi].T

        @pl.when(qi == nq - 1)
        def _():
            dk_ref[...] = dk_acc[...].astype(BF16)
            dv_ref[...] = dv_acc[...].astype(BF16)
            kaux_ref[...] = kaux_acc[...]

    grid_spec = pltpu.PrefetchScalarGridSpec(
        num_scalar_prefetch=2, grid=(HEAD_PAIRS, int(qi_arr.shape[0])),
        in_specs=[pl.BlockSpec((TQ, 128), lambda p, t, qi, ki: (qi[t], Q_BLK + p)),
                  pl.BlockSpec((TQ, 128), lambda p, t, qi, ki: (ki[t], K_BLK + p)),
                  pl.BlockSpec((TQ, 128), lambda p, t, qi, ki: (ki[t], V_BLK + p)),
                  pl.BlockSpec((TQ, 128), lambda p, t, qi, ki: (qi[t], Q_BLK + p)),
                  pl.BlockSpec((TQ, 128), lambda p, t, qi, ki: (qi[t], p)),
                  pl.BlockSpec((TQ, 128), lambda p, t, qi, ki: (ki[t], p)),
                  pl.BlockSpec((TQ, 128), lambda p, t, qi, ki: (qi[t], p))],
        out_specs=[pl.BlockSpec((s, 128), lambda p, t, qi, ki: (0, p)),
                   pl.BlockSpec((TQ, 128), lambda p, t, qi, ki: (ki[t], p)),
                   pl.BlockSpec((TQ, 128), lambda p, t, qi, ki: (ki[t], p)),
                   pl.BlockSpec((None, s, 128), lambda p, t, qi, ki: (p, 0, 0)),
                   pl.BlockSpec((None, TQ, 128), lambda p, t, qi, ki: (p, ki[t], 0))],
        scratch_shapes=[pltpu.VMEM((nq, 128, TQ), F32), pltpu.VMEM((nq, 128, TQ), F32),
                        pltpu.VMEM((TQ, 128), F32), pltpu.VMEM((TQ, 128), F32), pltpu.VMEM((TQ, 128), F32)])
    return pl.pallas_call(
        body, name=name, grid_spec=grid_spec,
        out_shape=[jax.ShapeDtypeStruct((s, D_FOX), BF16)] * 3 + [jax.ShapeDtypeStruct((HEAD_PAIRS, s, 128), F32)] * 2,
        compiler_params=_cp("arbitrary", "arbitrary"),
    )(qi_arr, ki_arr, proj, proj, proj, dycat, aqb, ak, ad)


XA_SCALE = XA_DIM ** -0.5


def _xattn_fwd(q2, kv, name):
    s = q2.shape[0]
    m = kv.shape[0]

    def body(q_ref, kv_ref, o_ref):
        for h in range(XA_HEADS):
            c0 = h * XA_DIM
            sc = _dot(q_ref[:, c0:c0 + XA_DIM], kv_ref[:, c0:c0 + XA_DIM], NT) * XA_SCALE
            e = jnp.exp(sc - jnp.max(sc, axis=1, keepdims=True))
            p = e / jnp.sum(e, axis=1, keepdims=True)
            o_ref[:, c0:c0 + XA_DIM] = _dot(p.astype(BF16), kv_ref[:, D_MODEL + c0:D_MODEL + c0 + XA_DIM], NN).astype(BF16)

    return pl.pallas_call(
        body, name=name, grid=(s // TM,),
        in_specs=[_row_spec(TM, D_MODEL), pl.BlockSpec((m, 2 * D_MODEL), lambda i: (0, 0))],
        out_specs=_row_spec(TM, D_MODEL), out_shape=jax.ShapeDtypeStruct((s, D_MODEL), BF16),
        compiler_params=_cp("parallel"),
    )(q2, kv)


def _xattn_bwd(q2, kv, do2, name):
    s = q2.shape[0]
    m = kv.shape[0]

    def body(q_ref, kv_ref, do_ref, dq_ref, dkv_ref):
        i = pl.program_id(0)

        @pl.when(i == 0)
        def _():
            dkv_ref[...] = jnp.zeros_like(dkv_ref)

        for h in range(XA_HEADS):
            c0 = h * XA_DIM
            v0 = D_MODEL + c0
            qh = q_ref[:, c0:c0 + XA_DIM]
            kh = kv_ref[:, c0:c0 + XA_DIM]
            doh = do_ref[:, c0:c0 + XA_DIM]
            sc = _dot(kh, qh, NT) * XA_SCALE
            e = jnp.exp(sc - jnp.max(sc, axis=0, keepdims=True))
            p = e / jnp.sum(e, axis=0, keepdims=True)
            dp = _dot(kv_ref[:, v0:v0 + XA_DIM], doh, NT)
            ds = p * (dp - jnp.sum(p * dp, axis=0, keepdims=True))
            dsb = (ds * XA_SCALE).astype(BF16)
            dq_ref[:, c0:c0 + XA_DIM] = _dot(kh, dsb, TN).T.astype(BF16)
            dkv_ref[:, c0:c0 + XA_DIM] += _dot(dsb, qh, NN)
            dkv_ref[:, v0:v0 + XA_DIM] += _dot(p.astype(BF16), doh, NN)

    return pl.pallas_call(
        body, name=name, grid=(s // TM,),
        in_specs=[_row_spec(TM, D_MODEL), pl.BlockSpec((m, 2 * D_MODEL), lambda i: (0, 0)), _row_spec(TM, D_MODEL)],
        out_specs=[_row_spec(TM, D_MODEL), pl.BlockSpec((m, 2 * D_MODEL), lambda i: (0, 0))],
        out_shape=[jax.ShapeDtypeStruct((s, D_MODEL), BF16), jax.ShapeDtypeStruct((m, 2 * D_MODEL), F32)],
        compiler_params=_cp("arbitrary"),
    )(q2, kv, do2)


GELU_C = math.sqrt(2.0 / math.pi)
GELU_A = 0.044715


def _gelu(x):
    return (0.5 * x) * (1.0 + jnp.tanh(x * (GELU_C * GELU_A * (x * x) + GELU_C)))


def _gelu_and_grad(x):
    x2 = x * x
    s = 1.0 + jnp.tanh(x * (GELU_C * GELU_A * x2 + GELU_C))
    hx = 0.5 * x
    return hx * s, s * (0.5 + hx * (2.0 - s) * (3.0 * GELU_C * GELU_A * x2 + GELU_C))


def _conv(h, s1, s2, w_ref, b_ref):
    return w_ref[0:1, :] * s2 + w_ref[1:2, :] * s1 + w_ref[2:3, :] * h + b_ref[...]


def _shift_down(main, prev8):
    row = lax.broadcasted_iota(jnp.int32, main.shape, 0)
    s1 = jnp.where(row == 0, prev8[7:8, :], pltpu.roll(main, 1, 0))
    s2 = jnp.where(row == 0, prev8[6:7, :], jnp.where(row == 1, prev8[7:8, :], pltpu.roll(main, 2, 0)))
    return s1, s2


def _shift_up(main, next8):
    n = main.shape[0]
    row = lax.broadcasted_iota(jnp.int32, main.shape, 0)
    u1 = jnp.where(row == n - 1, next8[0:1, :], pltpu.roll(main, n - 1, 0))
    u2 = jnp.where(row == n - 2, next8[0:1, :], jnp.where(row == n - 1, next8[1:2, :], pltpu.roll(main, n - 2, 0)))
    return u1, u2


def _ffn_fwd(h3, w_up, cw, cb, w_down, x2, tgt, g_post, name):
    s = h3.shape[0]
    tn = TN_FF
    nj = D_FF // tn
    per = D_MODEL // tn
    hb = TM // 8

    def body(h_ref, halo_ref, wg_ref, wu_ref, cwg_ref, cwu_ref, cbg_ref, cbu_ref, wd_ref, x_ref, t_ref, g_ref,
             hg_ref, hu_ref, cg_ref, cu_ref, a_ref, loss_ref, dx_ref, dy_ref, dg_ref, y_acc):
        i, j = pl.program_id(0), pl.program_id(1)
        h = h_ref[...]
        halo = halo_ref[...]
        halo = jnp.where(i > 0, halo, jnp.zeros_like(halo))
        conv = []
        for w_ref, cw_ref, cb_ref, hid_ref, c_ref in ((wg_ref, cwg_ref, cbg_ref, hg_ref, cg_ref),
                                                      (wu_ref, cwu_ref, cbu_ref, hu_ref, cu_ref)):
            hm = _dot(h, w_ref[...], NN)
            hid_ref[...] = hm.astype(BF16)
            s1, s2 = _shift_down(hm, _dot(halo, w_ref[...], NN))
            c = _conv(hm, s1, s2, cw_ref, cb_ref)
            c_ref[...] = c.astype(BF16)
            conv.append(c)
        a = (_gelu(conv[0]) * conv[1]).astype(BF16)
        a_ref[...] = a
        contrib = _dot(a, wd_ref[...], NN)

        @pl.when(j == 0)
        def _():
            y_acc[...] = contrib

        @pl.when(j > 0)
        def _():
            y_acc[...] += contrib

        @pl.when(j == nj - 1)
        def _():
            yv = y_acc[...]
            r = _rstd(yv)
            yn = yv * r
            e = x_ref[...] + yn * g_ref[...] - t_ref[...]
            part = 0.5 * jnp.sum(jnp.mean(e * e, axis=-1, keepdims=True), axis=0, keepdims=True)
            part = jnp.broadcast_to(part, (1, 128))
            dx = e * (1.0 / D_MODEL)
            dx_ref[...] = dx
            dy_ref[...] = _norm_bwd_rows(dx * g_ref[...], yn, r).astype(BF16)
            dg = jnp.sum(dx * yn, axis=0, keepdims=True)

            @pl.when(i == 0)
            def _():
                dg_ref[...] = dg
                loss_ref[...] = part

            @pl.when(i > 0)
            def _():
                dg_ref[...] += dg
                loss_ref[...] += part

    rows = pl.BlockSpec((TM, D_MODEL), lambda i, j: (i, 0))
    tile = pl.BlockSpec((TM, tn), lambda i, j: (i, j))
    wide = jax.ShapeDtypeStruct((s, D_FF), BF16)
    return pl.pallas_call(
        body, name=name, grid=(s // TM, nj),
        in_specs=[rows,
                  pl.BlockSpec((8, D_MODEL), lambda i, j: (jnp.maximum(i * hb - 1, 0), 0)),
                  pl.BlockSpec((None, D_MODEL, tn), lambda i, j: (j // per, 0, j % per)),
                  pl.BlockSpec((None, D_MODEL, tn), lambda i, j: (NDEV // 2 + j // per, 0, j % per)),
                  pl.BlockSpec((8, tn), lambda i, j: (0, j)),
                  pl.BlockSpec((8, tn), lambda i, j: (0, nj + j)),
                  pl.BlockSpec((1, tn), lambda i, j: (0, j)),
                  pl.BlockSpec((1, tn), lambda i, j: (0, nj + j)),
                  pl.BlockSpec((tn, D_MODEL), lambda i, j: (j, 0)),
                  rows, rows, pl.BlockSpec((1, D_MODEL), lambda i, j: (0, 0))],
        out_specs=[tile, tile, tile, tile, tile,
                   pl.BlockSpec((1, 128), lambda i, j: (0, 0)), rows, rows,
                   pl.BlockSpec((1, D_MODEL), lambda i, j: (0, 0))],
        out_shape=[wide, wide, wide, wide, wide,
                   jax.ShapeDtypeStruct((1, 128), F32), jax.ShapeDtypeStruct((s, D_MODEL), F32),
                   jax.ShapeDtypeStruct((s, D_MODEL), BF16), jax.ShapeDtypeStruct((1, D_MODEL), F32)],
        scratch_shapes=[pltpu.VMEM((TM, D_MODEL), F32)],
        compiler_params=_cp("arbitrary", "arbitrary"),
    )(h3, h3, w_up, w_up, cw, cw, cb, cb, w_down, x2, tgt, g_post)


def _ffn_bwd(dy3, w_down, hid_g, hid_u, conv_g, conv_u, cw, name):
    s = dy3.shape[0]
    n = s // TM
    tn = TN_FF
    nj = D_FF // tn
    hb = TM // 8
    last8 = s // 8 - 1

    def body(dy_ref, dyn_ref, wd_ref, hg_ref, hu_ref, cg_ref, cgn_ref, cu_ref, cun_ref, cwg_ref, cwu_ref,
             dhg_ref, dhu_ref, dcwg_ref, dcwu_ref, dcbg_ref, dcbu_ref):
        i = pl.program_id(1)
        first, last = i == 0, i == n - 1
        da = _dot(dy_ref[...], wd_ref[...], NT)
        dyn = dyn_ref[...]
        dyn = jnp.where(last, jnp.zeros_like(dyn), dyn)
        da_n = _dot(dyn, wd_ref[...], NT)
        c_g, c_u = cg_ref[...].astype(F32), cu_ref[...].astype(F32)
        g, dg = _gelu_and_grad(c_g)
        gn, dgn = _gelu_and_grad(cgn_ref[...].astype(F32))
        outs = ((da * c_u * dg, da_n * cun_ref[...].astype(F32) * dgn, hg_ref, cwg_ref, dhg_ref, dcwg_ref, dcbg_ref),
                (da * g, da_n * gn, hu_ref, cwu_ref, dhu_ref, dcwu_ref, dcbu_ref))
        row8 = lax.broadcasted_iota(jnp.int32, (8, tn), 0)
        for dc, dcn, h_ref, cw_ref, dh_ref, dcw_ref, dcb_ref in outs:
            u1, u2 = _shift_up(dc, dcn)
            dh_ref[...] = (cw_ref[2:3, :] * dc + cw_ref[1:2, :] * u1 + cw_ref[0:1, :] * u2).astype(BF16)
            hm = h_ref[...].astype(F32)
            dcb = jnp.sum(dc, axis=0, keepdims=True)
            dcw = jnp.where(row8 == 0, jnp.sum(hm * u2, axis=0, keepdims=True),
                            jnp.where(row8 == 1, jnp.sum(hm * u1, axis=0, keepdims=True),
                                      jnp.where(row8 == 2, jnp.sum(hm * dc, axis=0, keepdims=True), 0.0)))

            @pl.when(first)
            def _():
                dcw_ref[...] = dcw
                dcb_ref[...] = dcb

            @pl.when(i > 0)
            def _():
                dcw_ref[...] += dcw
                dcb_ref[...] += dcb

    next8 = lambda j, i: (jnp.minimum((i + 1) * hb, last8), j)
    blk = lambda j, i: (i, j)
    col = lambda j, i: (0, j)
    colu = lambda j, i: (0, nj + j)
    tile = pl.BlockSpec((TM, tn), blk)
    return pl.pallas_call(
        body, name=name, grid=(nj, n),
        in_specs=[pl.BlockSpec((TM, D_MODEL), lambda j, i: (i, 0)),
                  pl.BlockSpec((8, D_MODEL), lambda j, i: (jnp.minimum((i + 1) * hb, last8), 0)),
                  pl.BlockSpec((tn, D_MODEL), lambda j, i: (j, 0)),
                  tile, tile, tile, pl.BlockSpec((8, tn), next8), tile, pl.BlockSpec((8, tn), next8),
                  pl.BlockSpec((8, tn), col), pl.BlockSpec((8, tn), colu)],
        out_specs=[tile, tile, pl.BlockSpec((8, tn), col), pl.BlockSpec((8, tn), col),
                   pl.BlockSpec((1, tn), col), pl.BlockSpec((1, tn), col)],
        out_shape=[jax.ShapeDtypeStruct((s, D_FF), BF16), jax.ShapeDtypeStruct((s, D_FF), BF16),
                   jax.ShapeDtypeStruct((8, D_FF), F32), jax.ShapeDtypeStruct((8, D_FF), F32),
                   jax.ShapeDtypeStruct((1, D_FF), F32), jax.ShapeDtypeStruct((1, D_FF), F32)],
        compiler_params=_cp("parallel", "arbitrary"),
    )(dy3, dy3, w_down, hid_g, hid_u, conv_g, conv_g, conv_u, conv_u, cw, cw)


def _slot(p):
    return 4 * p[0] + 2 * p[1] + p[2]


def _all_gather(shards, name):
    n = len(shards)

    def body(*refs):
        ins, outs = refs[:n], refs[n:2 * n]
        send_sems, recv_sems, local_sems = refs[2 * n:]
        x, y, c = lax.axis_index("x"), lax.axis_index("y"), lax.axis_index("c")
        me, sibling = (x, y, c), (x, y, 1 - c)
        chips = [(1 - x, y), (x, 1 - y), (1 - x, 1 - y)]

        def copy(a, k, block, to, from_input=False):
            dst = outs[a].at[_slot(block)]
            return pltpu.make_async_remote_copy(
                src_ref=ins[a] if from_input else dst, dst_ref=dst,
                send_sem=send_sems.at[a, k], recv_sem=recv_sems.at[a, k],
                device_id=to, device_id_type=MESH)

        mine = [pltpu.make_async_copy(ins[a], outs[a].at[_slot(me)], local_sems.at[a]) for a in range(n)]
        for cp in mine:
            cp.start()
        first = []
        for a in range(n):
            first.append(copy(a, 0, me, sibling, True))
            first += [copy(a, 1 + j, me, (*chip, c), True) for j, chip in enumerate(chips)]
        for cp in first:
            cp.start()
        passed = []
        for j, chip in enumerate(chips):
            for a in range(n):
                copy(a, 1 + j, (*chip, c), me).wait_recv()
                fwd = copy(a, 4 + j, (*chip, c), sibling)
                fwd.start()
                passed.append(fwd)
        for a in range(n):
            copy(a, 0, sibling, me).wait_recv()
            for j, chip in enumerate(chips):
                copy(a, 4 + j, (*chip, 1 - c), me).wait_recv()
        for cp in first + passed:
            cp.wait_send()
        for cp in mine:
            cp.wait()

    any_spec = pl.BlockSpec(memory_space=pl.ANY)
    return pl.pallas_call(
        body, name=name,
        in_specs=[any_spec] * n, out_specs=[any_spec] * n,
        out_shape=[jax.ShapeDtypeStruct((NDEV,) + s.shape, s.dtype) for s in shards],
        scratch_shapes=[pltpu.SemaphoreType.DMA((n, 7)), pltpu.SemaphoreType.DMA((n, 7)),
                        pltpu.SemaphoreType.DMA((n,))],
    )(*shards)


def _peer_list(x, y, c):
    return [(1 - x if m & 4 else x, 1 - y if m & 2 else y, 1 - c if m & 1 else c) for m in range(1, NDEV)]


def _exchange_copies(src_refs, land_refs, send_sems, recv_sems, gather):
    x, y, c = lax.axis_index("x"), lax.axis_index("y"), lax.axis_index("c")
    me = (x, y, c)
    copies = []
    for m, peer in enumerate(_peer_list(x, y, c)):
        for a in range(len(src_refs)):
            copies.append(pltpu.make_async_remote_copy(
                src_ref=src_refs[a] if gather else src_refs[a].at[_slot(peer)], dst_ref=land_refs[a].at[_slot(me)],
                send_sem=send_sems.at[a * (NDEV - 1) + m], recv_sem=recv_sems.at[a * (NDEV - 1) + m],
                device_id=peer, device_id_type=MESH))
    return copies


def _all_gather_small(shards, name):
    n = len(shards)

    def body(*refs):
        ins, outs = refs[:n], refs[n:2 * n]
        send_sems, recv_sems, local_sems = refs[2 * n:]
        me = (lax.axis_index("x"), lax.axis_index("y"), lax.axis_index("c"))
        mine = [pltpu.make_async_copy(ins[a], outs[a].at[_slot(me)], local_sems.at[a]) for a in range(n)]
        copies = _exchange_copies(ins, outs, send_sems, recv_sems, True)
        for cp in mine + copies:
            cp.start()
        for cp in copies + mine:
            cp.wait()

    any_spec = pl.BlockSpec(memory_space=pl.ANY)
    return pl.pallas_call(
        body, name=name,
        in_specs=[any_spec] * n, out_specs=[any_spec] * n,
        out_shape=[jax.ShapeDtypeStruct((NDEV,) + s.shape, s.dtype) for s in shards],
        scratch_shapes=[pltpu.SemaphoreType.DMA((n * (NDEV - 1),)), pltpu.SemaphoreType.DMA((n * (NDEV - 1),)),
                        pltpu.SemaphoreType.DMA((n,))],
    )(*shards)


def _exchange_start(srcs, lands, after, gather, name):
    n = len(srcs)
    hbm = pl.BlockSpec(memory_space=pltpu.HBM)

    def body(*refs):
        for cp in _exchange_copies(refs[:n], refs[n:2 * n], refs[2 * n + 1], refs[2 * n + 2], gather):
            cp.start()
        token = refs[-1]
        token[...] = jnp.zeros_like(token)

    outs = pl.pallas_call(
        body, name=name,
        out_shape=(pltpu.SemaphoreType.DMA((n * (NDEV - 1),)), pltpu.SemaphoreType.DMA((n * (NDEV - 1),)),
                   *[pltpu.HBM(a.shape, a.dtype) for a in list(srcs) + list(lands)],
                   jax.ShapeDtypeStruct((8, 128), F32)),
        in_specs=[hbm] * (2 * n) + [pl.BlockSpec(memory_space=pl.ANY)],
        out_specs=(pl.BlockSpec(memory_space=pltpu.SEMAPHORE), pl.BlockSpec(memory_space=pltpu.SEMAPHORE),
                   *[hbm] * (2 * n), pl.BlockSpec(memory_space=pltpu.VMEM)),
        input_output_aliases={i: 2 + i for i in range(2 * n)},
        compiler_params=pltpu.CompilerParams(has_side_effects=pltpu.SideEffectType.DATAFLOW_SIDE_EFFECTING),
    )(*[pltpu.with_memory_space_constraint(a, pltpu.HBM) for a in list(srcs) + list(lands)], after)
    return outs[0], outs[1], outs[2:2 + n], outs[2 + n:2 + 2 * n], outs[-1]


def _exchange_wait(send_sems, recv_sems, srcs, lands, after, gather, name):
    n = len(srcs)
    hbm = pl.BlockSpec(memory_space=pltpu.HBM)

    def body(*refs):
        for cp in _exchange_copies(refs[:n], refs[n:2 * n], refs[2 * n], refs[2 * n + 1], gather):
            cp.wait_send()
            cp.wait_recv()

    outs = pl.pallas_call(
        body, name=name,
        out_shape=tuple(pltpu.HBM(a.shape, a.dtype) for a in list(srcs) + list(lands)),
        in_specs=[hbm] * (2 * n) + [pl.BlockSpec(memory_space=pltpu.SEMAPHORE)] * 2 + [pl.BlockSpec(memory_space=pl.ANY)],
        out_specs=tuple([hbm] * (2 * n)),
        input_output_aliases={i: i for i in range(2 * n)},
        compiler_params=pltpu.CompilerParams(has_side_effects=pltpu.SideEffectType.DATAFLOW_SIDE_EFFECTING),
    )(*srcs, *lands, send_sems, recv_sems, after)
    return outs[n:]


def _own_slot(block):
    me = 4 * lax.axis_index("x") + 2 * lax.axis_index("y") + lax.axis_index("c")
    return lax.dynamic_update_slice(lax.empty((NDEV,) + block.shape, block.dtype), block[None], (me, 0, 0))


def _adam_update(p_ref, w_ref, m_ref, v_ref, g_ref, d_ref, mo_ref, vo_ref):
    bc1 = 1.0 - ADAM_B1 ** ADAM_STEP
    bc2 = 1.0 - ADAM_B2 ** ADAM_STEP
    g = p_ref[0].astype(F32)
    for d in range(1, NDEV):
        g = g + p_ref[d].astype(F32)
    g_ref[...] = g
    mn = ADAM_B1 * m_ref[...] + (1.0 - ADAM_B1) * g
    vn = ADAM_B2 * v_ref[...] + (1.0 - ADAM_B2) * (g * g)
    mo_ref[...] = mn
    vo_ref[...] = vn
    d_ref[...] = -ADAM_LR * ((mn / bc1) / (jnp.sqrt(vn / bc2) + ADAM_EPS) + ADAM_WD * w_ref[...])


def _adamw_small(parts, ws, ms, vs, name):
    n = len(ws)

    def body(*refs):
        ins, outs = refs[:4 * n], refs[4 * n:]
        for k in range(n):
            _adam_update(ins[k], ins[n + k], ins[2 * n + k], ins[3 * n + k], *outs[4 * k:4 * k + 4])

    whole = pl.BlockSpec(memory_space=pltpu.VMEM)
    res = pl.pallas_call(
        body, name=name, in_specs=[whole] * (4 * n), out_specs=[whole] * (4 * n),
        out_shape=[jax.ShapeDtypeStruct(a.shape, F32) for a in ws for _ in range(4)],
    )(*parts, *ws, *ms, *vs)
    return [res[4 * k:4 * k + 4] for k in range(n)]


def _adamw(parts, w, m, v, name):
    r, c = w.shape
    tr = r if r * c <= 160 * 1024 else max(8, (160 * 1024 // c) // 8 * 8)
    while r % tr:
        tr -= 8
    body = functools.partial(_adam_update)
    spec = pl.BlockSpec((tr, c), lambda i: (i, 0))
    return pl.pallas_call(
        body, name=name, grid=(r // tr,),
        in_specs=[pl.BlockSpec((NDEV, tr, c), lambda i: (0, i, 0)), spec, spec, spec],
        out_specs=[spec] * 4, out_shape=[jax.ShapeDtypeStruct((r, c), F32)] * 4,
        compiler_params=_cp("parallel"),
    )(parts, w, m, v)


def _local_step(x, mem, tgt, gains, b_forget, w_pool, pool_scale, conv_b, w_in,
                mix_weights, ffn_weights, send_in_grad, send_mix_grads, send_ffn_grads):
    b_pad = jnp.pad(b_forget, ((0, 0), (0, 128 - FOX_HEADS)))
    wbd = jnp.zeros((D_POOL, D_POOL), F32)
    for g in range(4):
        wbd = wbd.at[64 * g:64 * g + 64, 64 * g:64 * g + 64].set(w_pool[g])
    wbd = wbd.astype(BF16)
    scale = pool_scale.reshape(1, D_POOL)

    h1, proj, fraw = _proj_in(x, gains["mix_pre"], w_in, "proj_in")
    flog, aq, ak = _gate_cumsum(fraw, b_pad, "gate_cumsum")
    ycat, aqb = _fox_fwd(proj, aq, ak, "fox_fwd")
    ycat = _pool_fwd(proj, wbd, scale, ycat, "pool_fwd")
    w_mix, w_xq, w_xo, w_xkv = mix_weights(ycat)
    y1, x1, h2 = _mm_rows(ycat, w_mix, "nn", 1024, "mix_out", [x], [gains["mix_post"], gains["xa_pre"]],
                          [F32, F32, BF16], _epi_resid)
    q2 = _mm(h2, w_xq, "nn", BF16, 2048, 1024, 1024, "xa_q")
    mem_n = _norm_fwd(mem, gains["mem"], "norm_mem")
    kv = _mm(mem_n, w_xkv, "nn", BF16, mem.shape[0], 256, 1024, "xa_kv", b_cols=256)
    o2 = _xattn_fwd(q2, kv, "xattn_fwd")
    y2, x2, h3 = _mm_rows(o2, w_xo, "nn", 1024, "xa_out", [x1], [gains["xa_post"], gains["ffn_pre"]],
                          [F32, F32, BF16], _epi_resid)
    w_up, w_down, cw = ffn_weights(h3)
    hid_g, hid_u, conv_g, conv_u, act, loss, dx3, dy3, dg_ffn_post = _ffn_fwd(
        h3, w_up, cw, conv_b, w_down, x2, tgt, gains["ffn_post"], "ffn_fwd")

    dhid_g, dhid_u, dcw_g, dcw_u, dcb_g, dcb_u = _ffn_bwd(dy3, w_down, hid_g, hid_u, conv_g, conv_u, cw, "ffn_bwd")
    d_w_down = _mm(act, dy3, "tn", BF16, 2048, 1024, 1024, "dw_down")
    d_w_up = _mm(h3, [dhid_g, dhid_u], "tn", BF16, 1024, 1024, 2048, "dw_up", out_cols=1024)
    sent = send_ffn_grads(d_w_up, d_w_down, jnp.concatenate([dcw_g, dcw_u], axis=1))
    dh3 = _mm([dhid_g, dhid_u], w_up, "nt", F32, 2048, 1024, 1024, "dh_ffn", b_cols=1024, after=sent)
    dx2, dg_ffn_pre, dy2, dg_xa_post = _norm_bwd(dh3, x2, dx3, gains["ffn_pre"], "norm_bwd_ffn",
                                                 prev=(y2, gains["xa_post"]))
    do2 = _mm(dy2, w_xo, "nt", BF16, 2048, 1024, 1024, "d_xa_out")
    d_w_xo = _mm(o2, dy2, "tn", BF16, 1024, 1024, 1024, "dw_xo")
    dq2, dkv = _xattn_bwd(q2, kv, do2, "xattn_bwd")
    dkv = dkv.astype(BF16)
    dx1, dg_xa_pre, dy1, dg_mix_post = _mm_rows(
        dq2, w_xq, "nt", 1024, "dh_xa", [x1, dx2, y1], [gains["xa_pre"], gains["mix_post"]],
        [F32, "sum", BF16, "sum"], _epi_norm_bwd)
    d_w_xq = _mm(h2, dq2, "tn", BF16, 1024, 1024, 1024, "dw_xq")
    dmem_n = _mm(dkv, w_xkv, "nt", F32, mem.shape[0], 1024, 256, "d_mem", b_cols=256)
    d_w_xkv = _mm(mem_n, dkv, "tn", BF16, 1024, 256, mem.shape[0], "dw_xkv", out_cols=256)
    _, dg_mem = _norm_bwd(dmem_n, mem, jnp.zeros_like(mem), gains["mem"], "norm_bwd_mem")
    dycat = _mm(dy1, w_mix, "nt", BF16, 2048, 1024, 1024, "d_mix_out")
    d_w_mix = _mm(ycat, dy1, "tn", BF16, 1024, 1024, 1024, "dw_mix")
    sent_mix = send_mix_grads(d_w_mix, d_w_xq, d_w_xo, d_w_xkv)
    ad = _fox_do_operand(dycat, ycat, sent_mix, "fox_do_operand")
    dq, dk, dv, qaux, kaux = _fox_bwd(proj, dycat, aqb, ak, ad, "fox_bwd")
    du, d_wbd, d_scale = _pool_bwd(proj, dycat, wbd, scale, "pool_bwd")
    df, db_f = _gate_bwd(qaux, kaux, flog, "gate_bwd")
    dproj = [du, dq, dk, dv, df]
    sent_in = send_in_grad(_dw_in(h1, dproj, "dw_in"))
    grad_x, dg_mix_pre = _mm_rows(dproj, w_in, "nt", None, "dh_mix", [x, dx1], [gains["mix_pre"]],
                                  [F32, "sum"], _epi_norm_bwd, after=sent_in)

    small = dict(
        mix_pre=dg_mix_pre, mix_post=dg_mix_post, mem=dg_mem, xa_pre=dg_xa_pre, xa_post=dg_xa_post,
        ffn_pre=dg_ffn_pre, ffn_post=dg_ffn_post,
        conv_b=jnp.concatenate([dcb_g, dcb_u], axis=1),
        w_pool=jnp.concatenate([d_wbd[64 * g:64 * g + 64, 64 * g:64 * g + 64] for g in range(4)], axis=0),
        pool_scale=d_scale.reshape(4, 64),
        b_forget=db_f[:, :FOX_HEADS],
    )
    return loss, grad_x, small


SMALL_ORDER = ("mix_pre", "mix_post", "mem", "xa_pre", "xa_post", "ffn_pre", "ffn_post", "conv_b",
               "w_pool", "pool_scale", "b_forget")


def kernel(x, mem, norm_mix_pre, norm_mix_post, w_in, b_forget, w_pool, pool_scale, w_mix_out, norm_mem, norm_xa_pre, norm_xa_post, w_xq, w_xkv, w_xo, norm_ffn_pre, norm_ffn_post, w_up, conv_w, conv_b, w_down, loss_target, m_norm_mix_pre, m_norm_mix_post, m_w_in, m_b_forget, m_w_pool, m_pool_scale, m_w_mix_out, m_norm_mem, m_norm_xa_pre, m_norm_xa_post, m_w_xq, m_w_xkv, m_w_xo, m_norm_ffn_pre, m_norm_ffn_post, m_w_up, m_conv_w, m_conv_b, m_w_down, v_norm_mix_pre, v_norm_mix_post, v_w_in, v_b_forget, v_w_pool, v_pool_scale, v_w_mix_out, v_norm_mem, v_norm_xa_pre, v_norm_xa_post, v_w_xq, v_w_xkv, v_w_xo, v_norm_ffn_pre, v_norm_ffn_post, v_w_up, v_conv_w, v_conv_b, v_w_down):
    names = ("norm_mix_pre", "norm_mix_post", "w_in", "b_forget", "w_pool", "pool_scale", "w_mix_out", "norm_mem",
             "norm_xa_pre", "norm_xa_post", "w_xq", "w_xkv", "w_xo", "norm_ffn_pre", "norm_ffn_post", "w_up",
             "conv_w", "conv_b", "w_down")
    w = dict(zip(names, (norm_mix_pre, norm_mix_post, w_in, b_forget, w_pool, pool_scale, w_mix_out, norm_mem,
                         norm_xa_pre, norm_xa_post, w_xq, w_xkv, w_xo, norm_ffn_pre, norm_ffn_post, w_up,
                         conv_w, conv_b, w_down)))
    mo = dict(zip(names, (m_norm_mix_pre, m_norm_mix_post, m_w_in, m_b_forget, m_w_pool, m_pool_scale, m_w_mix_out,
                          m_norm_mem, m_norm_xa_pre, m_norm_xa_post, m_w_xq, m_w_xkv, m_w_xo, m_norm_ffn_pre,
                          m_norm_ffn_post, m_w_up, m_conv_w, m_conv_b, m_w_down)))
    vo = dict(zip(names, (v_norm_mix_pre, v_norm_mix_post, v_w_in, v_b_forget, v_w_pool, v_pool_scale, v_w_mix_out,
                          v_norm_mem, v_norm_xa_pre, v_norm_xa_post, v_w_xq, v_w_xkv, v_w_xo, v_norm_ffn_pre,
                          v_norm_ffn_post, v_w_up, v_conv_w, v_conv_b, v_w_down)))

    big_names = ("w_in", "w_mix_out", "w_xq", "w_xo", "w_xkv", "w_up", "w_down")
    shards = {k: w[k][0].astype(BF16) for k in big_names}
    shards["w_in"] = jnp.pad(shards["w_in"], ((0, 0), (0, D_IN_PAD - shards["w_in"].shape[1])))
    conv_w_sh = jnp.pad(conv_w[0, :, 0, :], ((0, 5), (0, 0)))
    (g_in,) = _all_gather([shards["w_in"]], "gather_w_in")
    mix_srcs = [shards[k] for k in ("w_mix_out", "w_xq", "w_xo", "w_xkv")]
    mix_flight = _exchange_start(mix_srcs, [_own_slot(a) for a in mix_srcs], g_in, True, "gather_mix_start")
    ffn_srcs = [shards["w_up"], shards["w_down"], conv_w_sh]
    ffn_flight = _exchange_start(ffn_srcs, [_own_slot(a) for a in ffn_srcs], mix_flight[4], True, "gather_ffn_start")
    my_slot = 4 * lax.axis_index("x") + 2 * lax.axis_index("y") + lax.axis_index("c")
    own_block = lambda a: _own_slot(lax.dynamic_index_in_dim(a, my_slot, 0, keepdims=False))
    by_rows = lambda a: a.reshape(NDEV, a.shape[0] // NDEV, a.shape[1])
    by_cols = lambda a: a.reshape(a.shape[0], NDEV, a.shape[1] // NDEV).transpose(1, 0, 2)
    grad_flight = {}

    def mix_weights(after):
        g_mix, g_xq, g_xo, g_xkv = _exchange_wait(*mix_flight[:4], after, True, "gather_mix_wait")
        return (g_mix.reshape(D_MODEL, D_MODEL), g_xq.reshape(D_MODEL, D_MODEL), g_xo.reshape(D_MODEL, D_MODEL), g_xkv)

    def ffn_weights(after):
        g_up, g_down, g_cw = _exchange_wait(*ffn_flight[:4], after, True, "gather_ffn_wait")
        return g_up, g_down.reshape(D_FF, D_MODEL), g_cw.transpose(1, 0, 2).reshape(8, 2 * D_FF)

    def send_ffn_grads(d_w_up, d_w_down, d_cw):
        srcs = [d_w_up, by_rows(d_w_down), by_cols(d_cw)]
        grad_flight["ffn"] = _exchange_start(srcs, [own_block(a) for a in srcs], ffn_flight[4], False, "scatter_ffn_start")
        return grad_flight["ffn"][4]

    def send_mix_grads(d_w_mix, d_w_xq, d_w_xo, d_w_xkv):
        srcs = [by_rows(d_w_mix), by_rows(d_w_xq), by_rows(d_w_xo), d_w_xkv]
        grad_flight["mix"] = _exchange_start(srcs, [own_block(a) for a in srcs], ffn_flight[4], False, "scatter_mix_start")
        return grad_flight["mix"][4]

    def send_in_grad(d_w_in):
        srcs = [by_rows(d_w_in)]
        grad_flight["in"] = _exchange_start(srcs, [own_block(a) for a in srcs], ffn_flight[4], False, "scatter_in_start")
        return grad_flight["in"][4]

    gains = dict(mix_pre=norm_mix_pre + ffn_flight[4][0, 0], mix_post=norm_mix_post, mem=norm_mem, xa_pre=norm_xa_pre,
                 xa_post=norm_xa_post, ffn_pre=norm_ffn_pre, ffn_post=norm_ffn_post)
    loss, grad_x, small = _local_step(
        x[0], mem[0], loss_target[0], gains, b_forget, w_pool[0], pool_scale[0], conv_b,
        g_in.reshape(D_MODEL, D_IN_PAD), mix_weights, ffn_weights, send_in_grad, send_mix_grads, send_ffn_grads)

    p_up, p_down, p_cw = _exchange_wait(*grad_flight["ffn"][:4], grad_x, False, "scatter_ffn_wait")
    p_mix, p_xq, p_xo, p_xkv = _exchange_wait(*grad_flight["mix"][:4], grad_x, False, "scatter_mix_wait")
    parts = dict(w_mix_out=p_mix, w_xq=p_xq, w_xo=p_xo, w_xkv=p_xkv, w_up=p_up, w_down=p_down)
    *small_parts, loss_parts = _all_gather_small([small[k] for k in SMALL_ORDER] + [loss], "gather_small_grads")

    res = {k: [a[None] for a in _adamw(p, w[k][0], mo[k][0], vo[k][0], "adamw_" + k)] for k, p in parts.items()}
    pad_cw = lambda a: jnp.pad(a[0, :, 0, :], ((0, 5), (0, 0)))
    res["conv_w"] = [a[:3][None, :, None, :] for a in
                     _adamw(p_cw, pad_cw(conv_w), pad_cw(m_conv_w), pad_cw(v_conv_w), "adamw_conv_w")]
    key_of = dict(mix_pre="norm_mix_pre", mix_post="norm_mix_post", mem="norm_mem", xa_pre="norm_xa_pre",
                  xa_post="norm_xa_post", ffn_pre="norm_ffn_pre", ffn_post="norm_ffn_post", conv_b="conv_b",
                  w_pool="w_pool", pool_scale="pool_scale", b_forget="b_forget")
    flat2d = lambda src: [src[key_of[k]].reshape(small[k].shape) for k in SMALL_ORDER]
    small_out = _adamw_small(small_parts, flat2d(w), flat2d(mo), flat2d(vo), "adamw_small")
    for k, four in zip(SMALL_ORDER, small_out):
        res[key_of[k]] = [a.reshape(w[key_of[k]].shape) for a in four]
    (p_in,) = _exchange_wait(*grad_flight["in"][:4], res["w_up"][1], False, "scatter_in_wait")
    res["w_in"] = [a[None] for a in _adamw(p_in[:, :, :w_in.shape[2]], w["w_in"][0], mo["w_in"][0], vo["w_in"][0],
                                           "adamw_w_in")]

    outs = [jnp.sum(loss_parts[:, 0, 0]), grad_x[None]]
    for idx in range(4):
        outs += [res[k][idx] for k in names]
    return tuple(outs)
```

```python
import functools
import math

import jax
import jax.numpy as jnp
from jax import lax
from jax.experimental import pallas as pl
from jax.experimental.pallas import tpu as pltpu

F32 = jnp.float32
BF16 = jnp.bfloat16

NDEV = 8
D_MODEL = 1024
D_POOL = 256
D_FOX = 768
FOX_HEADS = 12
HEAD_PAIRS = FOX_HEADS // 2
XA_HEADS = 4
XA_DIM = 256
D_FF = 4096
D_IN_PAD = 2688
F_COL = 2560
POOL_HALO = 16
NORM_EPS = 1e-6
NEG = -1e30

ADAM_LR = 0.001
ADAM_B1 = 0.9
ADAM_B2 = 0.999
ADAM_EPS = 1e-08
ADAM_WD = 0.01
ADAM_STEP = 10

TM = 512
TQ = 512
TN_FF = 1024
VMEM_LIMIT = 56 * 1024 * 1024
MESH = pl.DeviceIdType.MESH


def _cp(*sem):
    return pltpu.CompilerParams(dimension_semantics=sem, vmem_limit_bytes=VMEM_LIMIT)


def _dot(a, b, dims):
    return lax.dot_general(a, b, (dims, ((), ())), preferred_element_type=F32)


NN = ((1,), (0,))
NT = ((1,), (1,))
TN = ((0,), (0,))


def _mm(a, b, mode, out_dtype, tm, tn, tk, name, b_cols=None, out_cols=None, after=None):
    a_list = list(a) if isinstance(a, (list, tuple)) else [a]
    b_list = list(b) if isinstance(b, (list, tuple)) else [b]
    assert len(a_list) == 1 or len(b_list) == 1
    if mode == "tn":
        K, M = a_list[0].shape
        assert len(a_list) == 1
        Ns = [x.shape[1] for x in b_list]
        N = sum(Ns)
        assert b_cols is None
    else:
        assert len(b_list) == 1
        M = a_list[0].shape[0]
        Ks = [x.shape[1] for x in a_list]
        K = sum(Ks)
        if b_cols is None:
            N = b_list[0].shape[0] if mode == "nt" else b_list[0].shape[1]
        else:
            N = b_list[0].shape[1] if mode == "nt" else NDEV * b_cols
    assert M % tm == 0 and N % tn == 0 and K % tk == 0, (name, M, N, K)
    grid = (M // tm, N // tn, K // tk)
    nk = grid[2]
    dims = {"nn": NN, "nt": NT, "tn": TN}[mode]

    in_specs = []
    if mode == "tn":
        in_specs.append(pl.BlockSpec((tk, tm), lambda i, j, k: (k, i)))
        if len(b_list) == 1:
            in_specs.append(pl.BlockSpec((tk, tn), lambda i, j, k: (k, j)))
        else:
            nj1 = Ns[0] // tn
            in_specs.append(pl.BlockSpec((tk, tn), lambda i, j, k: (k, jnp.minimum(j, nj1 - 1))))
            in_specs.append(pl.BlockSpec((tk, tn), lambda i, j, k: (k, jnp.maximum(j - nj1, 0))))
    else:
        if len(a_list) == 1:
            in_specs.append(pl.BlockSpec((tm, tk), lambda i, j, k: (i, k)))
        else:
            nk1 = Ks[0] // tk
            in_specs.append(pl.BlockSpec((tm, tk), lambda i, j, k: (i, jnp.minimum(k, nk1 - 1))))
            in_specs.append(pl.BlockSpec((tm, tk), lambda i, j, k: (i, jnp.maximum(k - nk1, 0))))
        if b_cols is None:
            if mode == "nn":
                in_specs.append(pl.BlockSpec((tk, tn), lambda i, j, k: (k, j)))
            else:
                in_specs.append(pl.BlockSpec((tn, tk), lambda i, j, k: (j, k)))
        else:
            if mode == "nn":
                per = b_cols // tn
                in_specs.append(pl.BlockSpec((None, tk, tn), lambda i, j, k: (j // per, k, j % per)))
            else:
                per = b_cols // tk
                in_specs.append(pl.BlockSpec((None, tn, tk), lambda i, j, k: (k // per, j, k % per)))
    if out_cols is None:
        out_spec = pl.BlockSpec((tm, tn), lambda i, j, k: (i, j))
        out_shape = jax.ShapeDtypeStruct((M, N), out_dtype)
    else:
        pero = out_cols // tn
        out_spec = pl.BlockSpec((None, tm, tn), lambda i, j, k: (j // pero, i, j % pero))
        out_shape = jax.ShapeDtypeStruct((NDEV, M, out_cols), out_dtype)

    two_a = len(a_list) == 2
    two_b = len(b_list) == 2
    extra = []
    if after is not None:
        in_specs.append(pl.BlockSpec(memory_space=pl.ANY))
        extra.append(after)

    def body(*refs):
        o_ref, acc_ref = refs[-2], refs[-1]
        j = pl.program_id(1)
        k = pl.program_id(2)

        @pl.when(k == 0)
        def _():
            acc_ref[...] = jnp.zeros_like(acc_ref)

        if two_a:
            a1, a2, b1 = refs[0], refs[1], refs[2]
            nk1_ = Ks[0] // tk

            @pl.when(k < nk1_)
            def _():
                acc_ref[...] += _dot(a1[...], b1[...], dims)

            @pl.when(k >= nk1_)
            def _():
                acc_ref[...] += _dot(a2[...], b1[...], dims)
        elif two_b:
            a1, b1, b2 = refs[0], refs[1], refs[2]
            nj1_ = Ns[0] // tn

            @pl.when(j < nj1_)
            def _():
                acc_ref[...] += _dot(a1[...], b1[...], dims)

            @pl.when(j >= nj1_)
            def _():
                acc_ref[...] += _dot(a1[...], b2[...], dims)
        else:
            acc_ref[...] += _dot(refs[0][...], refs[1][...], dims)

        @pl.when(k == nk - 1)
        def _():
            o_ref[...] = acc_ref[...].astype(o_ref.dtype)

    return pl.pallas_call(
        body, name=name, grid=grid, in_specs=in_specs, out_specs=out_spec, out_shape=out_shape,
        scratch_shapes=[pltpu.VMEM((tm, tn), F32)],
        compiler_params=_cp("parallel", "parallel", "arbitrary"),
    )(*a_list, *b_list, *extra)


def _rstd(x):
    return lax.rsqrt(jnp.mean(x * x, axis=-1, keepdims=True) + NORM_EPS)


def _norm_bwd_rows(dxn, xn, r):
    return r * (dxn - xn * jnp.mean(dxn * xn, axis=-1, keepdims=True))


def _row_spec(tm, d):
    return pl.BlockSpec((tm, d), lambda i: (i, 0))


def _vec_spec(d):
    return pl.BlockSpec((1, d), lambda i: (0, 0))


def _mm_rows(a, b, mode, tk, name, rows, vecs, outs, epilogue, b_cols=None, after=None):
    a_list = list(a) if isinstance(a, (list, tuple)) else [a]
    m = a_list[0].shape[0]
    ks = [x.shape[1] for x in a_list]
    n = D_MODEL
    pieces = tk is None
    nk = 1 if pieces else sum(ks) // tk
    dims = NN if mode == "nn" else NT
    if pieces:
        assert mode == "nt" and b_cols is None
        in_specs = [pl.BlockSpec((TM, kp), lambda i, k: (i, 0)) for kp in ks]
        tk = sum(ks)
    elif len(a_list) == 1:
        in_specs = [pl.BlockSpec((TM, tk), lambda i, k: (i, k))]
    else:
        nk1 = ks[0] // tk
        in_specs = [pl.BlockSpec((TM, tk), lambda i, k: (i, jnp.minimum(k, nk1 - 1))),
                    pl.BlockSpec((TM, tk), lambda i, k: (i, jnp.maximum(k - nk1, 0)))]
    if mode == "nn":
        in_specs.append(pl.BlockSpec((tk, n), lambda i, k: (k, 0)))
    elif b_cols is None:
        in_specs.append(pl.BlockSpec((n, tk), lambda i, k: (0, k)))
    else:
        per = b_cols // tk
        in_specs.append(pl.BlockSpec((None, n, tk), lambda i, k: (k // per, 0, k % per)))
    in_specs += [pl.BlockSpec((TM, n), lambda i, k: (i, 0))] * len(rows)
    in_specs += [pl.BlockSpec((1, n), lambda i, k: (0, 0))] * len(vecs)
    extra = []
    if after is not None:
        in_specs.append(pl.BlockSpec(memory_space=pl.ANY))
        extra.append(after)
    out_specs, out_shape = [], []
    for o in outs:
        if o == "sum":
            out_specs.append(pl.BlockSpec((1, n), lambda i, k: (0, 0)))
            out_shape.append(jax.ShapeDtypeStruct((1, n), F32))
        else:
            out_specs.append(pl.BlockSpec((TM, n), lambda i, k: (i, 0)))
            out_shape.append(jax.ShapeDtypeStruct((m, n), o))
    na, nr, nv = len(a_list), len(rows), len(vecs)

    def body(*refs):
        a_refs, b_ref = refs[:na], refs[na]
        row_refs = refs[na + 1:na + 1 + nr]
        vec_refs = refs[na + 1 + nr:na + 1 + nr + nv]
        out_refs = refs[len(refs) - 1 - len(outs):len(refs) - 1]
        acc_ref = refs[-1]
        i, k = pl.program_id(0), pl.program_id(1)

        @pl.when(k == 0)
        def _():
            acc_ref[...] = jnp.zeros_like(acc_ref)

        if pieces:
            off = 0
            for a_ref in a_refs:
                kp = a_ref.shape[1]
                acc_ref[...] += _dot(a_ref[...], b_ref[:, off:off + kp], dims)
                off += kp
        elif na == 1:
            acc_ref[...] += _dot(a_refs[0][...], b_ref[...], dims)
        else:
            nk1_ = ks[0] // tk

            @pl.when(k < nk1_)
            def _():
                acc_ref[...] += _dot(a_refs[0][...], b_ref[...], dims)

            @pl.when(k >= nk1_)
            def _():
                acc_ref[...] += _dot(a_refs[1][...], b_ref[...], dims)

        @pl.when(k == nk - 1)
        def _():
            vals = epilogue(acc_ref[...], [r[...] for r in row_refs], [v[...] for v in vec_refs])
            for o, ref, val in zip(outs, out_refs, vals):
                if o == "sum":
                    @pl.when(i == 0)
                    def _():
                        ref[...] = val

                    @pl.when(i > 0)
                    def _():
                        ref[...] += val
                else:
                    ref[...] = val.astype(o)

    return pl.pallas_call(
        body, name=name, grid=(m // TM, nk), in_specs=in_specs, out_specs=out_specs, out_shape=out_shape,
        scratch_shapes=[pltpu.VMEM((TM, n), F32)],
        compiler_params=_cp("arbitrary", "arbitrary"),
    )(*a_list, b, *rows, *vecs, *extra)


def _proj_in(x, g, w_in, name):
    s, d = x.shape
    n = w_in.shape[1]

    def body(x_ref, g_ref, w_ref, h_ref, p_ref, f_ref):
        xv = x_ref[...]
        h = (xv * _rstd(xv) * g_ref[...]).astype(BF16)
        h_ref[...] = h
        acc = _dot(h, w_ref[...], NN)
        p_ref[...] = acc.astype(BF16)
        f_ref[...] = acc[:, F_COL:]

    return pl.pallas_call(
        body, name=name, grid=(s // TM,),
        in_specs=[_row_spec(TM, d), _vec_spec(d), pl.BlockSpec((d, n), lambda i: (0, 0))],
        out_specs=[_row_spec(TM, d), _row_spec(TM, n), _row_spec(TM, n - F_COL)],
        out_shape=[jax.ShapeDtypeStruct((s, d), BF16), jax.ShapeDtypeStruct((s, n), BF16),
                   jax.ShapeDtypeStruct((s, n - F_COL), F32)],
        compiler_params=_cp("parallel"),
    )(x, g, w_in)


def _dw_in(h, pieces, name):
    s, d = h.shape
    n = sum(p.shape[1] for p in pieces)
    tk = 1024
    nk = s // tk

    def body(*refs):
        h_ref, piece_refs, o_ref, acc_ref = refs[0], refs[1:-2], refs[-2], refs[-1]
        k = pl.program_id(1)

        @pl.when(k == 0)
        def _():
            acc_ref[...] = jnp.zeros_like(acc_ref)

        off = 0
        for p_ref in piece_refs:
            w = p_ref.shape[1]
            acc_ref[:, off:off + w] += _dot(h_ref[...], p_ref[...], TN)
            off += w

        @pl.when(k == nk - 1)
        def _():
            o_ref[...] = acc_ref[...].astype(BF16)

    return pl.pallas_call(
        body, name=name, grid=(d // TM, nk),
        in_specs=[pl.BlockSpec((tk, TM), lambda i, k: (k, i))] +
                 [pl.BlockSpec((tk, p.shape[1]), lambda i, k: (k, 0)) for p in pieces],
        out_specs=pl.BlockSpec((TM, n), lambda i, k: (i, 0)),
        out_shape=jax.ShapeDtypeStruct((d, n), BF16),
        scratch_shapes=[pltpu.VMEM((TM, n), F32)],
        compiler_params=_cp("parallel", "arbitrary"),
    )(h, *pieces)


def _epi_resid(y, rows, vecs):
    (x_in,), (g_post, g_next) = rows, vecs
    xo = x_in + y * _rstd(y) * g_post
    return y, xo, xo * _rstd(xo) * g_next


def _epi_norm_bwd(dh, rows, vecs):
    x, dx_res = rows[0], rows[1]
    r = _rstd(x)
    xn = x * r
    dx = dx_res + _norm_bwd_rows(dh * vecs[0], xn, r)
    res = [dx, jnp.sum(dh * xn, axis=0, keepdims=True)]
    if len(rows) == 3:
        y = rows[2]
        r2 = _rstd(y)
        yn = y * r2
        res += [_norm_bwd_rows(dx * vecs[1], yn, r2), jnp.sum(dx * yn, axis=0, keepdims=True)]
    return res


def _norm_fwd(x, g, name):
    s, d = x.shape
    tm = min(TM, s)

    def body(x_ref, g_ref, h_ref):
        xv = x_ref[...]
        h_ref[...] = (xv * _rstd(xv) * g_ref[...]).astype(BF16)

    return pl.pallas_call(
        body, name=name, grid=(s // tm,), in_specs=[_row_spec(tm, d), _vec_spec(d)],
        out_specs=_row_spec(tm, d), out_shape=jax.ShapeDtypeStruct((s, d), BF16),
        compiler_params=_cp("parallel"),
    )(x, g)


def _norm_bwd(dh, x, dx_res, g_pre, name, prev=None):
    s, d = x.shape
    tm = min(TM, s)
    has_prev = prev is not None

    def body(*refs):
        if has_prev:
            dh_ref, x_ref, dr_ref, g_ref, y_ref, gp_ref, dx_ref, dg_ref, dy_ref, dgp_ref = refs
        else:
            dh_ref, x_ref, dr_ref, g_ref, dx_ref, dg_ref = refs
        i = pl.program_id(0)
        xv = x_ref[...]
        r = _rstd(xv)
        xn = xv * r
        dhv = dh_ref[...].astype(F32)
        dx = dr_ref[...] + _norm_bwd_rows(dhv * g_ref[...], xn, r)
        dx_ref[...] = dx
        dg = jnp.sum(dhv * xn, axis=0, keepdims=True)

        @pl.when(i == 0)
        def _():
            dg_ref[...] = dg

        @pl.when(i > 0)
        def _():
            dg_ref[...] += dg

        if has_prev:
            yv = y_ref[...]
            r2 = _rstd(yv)
            yn = yv * r2
            dy_ref[...] = _norm_bwd_rows(dx * gp_ref[...], yn, r2).astype(BF16)
            dgp = jnp.sum(dx * yn, axis=0, keepdims=True)

            @pl.when(i == 0)
            def _():
                dgp_ref[...] = dgp

            @pl.when(i > 0)
            def _():
                dgp_ref[...] += dgp

    in_specs = [_row_spec(tm, d), _row_spec(tm, d), _row_spec(tm, d), _vec_spec(d)]
    out_specs = [_row_spec(tm, d), _vec_spec(d)]
    out_shape = [jax.ShapeDtypeStruct((s, d), F32), jax.ShapeDtypeStruct((1, d), F32)]
    args = [dh, x, dx_res, g_pre]
    if has_prev:
        in_specs += [_row_spec(tm, d), _vec_spec(d)]
        out_specs += [_row_spec(tm, d), _vec_spec(d)]
        out_shape += [jax.ShapeDtypeStruct((s, d), BF16), jax.ShapeDtypeStruct((1, d), F32)]
        args += list(prev)
    return pl.pallas_call(
        body, name=name, grid=(s // tm,), in_specs=in_specs, out_specs=out_specs, out_shape=out_shape,
        compiler_params=_cp("arbitrary"),
    )(*args)


def _split3(v):
    hi = v.astype(BF16)
    r1 = v - hi.astype(F32)
    mid = r1.astype(BF16)
    lo = (r1 - mid.astype(F32)).astype(BF16)
    return hi, mid, lo


def _tri_dot(tri, v):
    hi, mid, lo = _split3(v)
    return _dot(tri, hi, NN) + _dot(tri, mid, NN) + _dot(tri, lo, NN)


def _gate_cumsum(fraw, b_pad, name):
    s = fraw.shape[0]
    width = HEAD_PAIRS * 128

    def body(f_ref, b_ref, flog_ref, aq_ref, ak_ref, carry_ref):
        i = pl.program_id(0)

        @pl.when(i == 0)
        def _():
            carry_ref[...] = jnp.zeros_like(carry_ref)

        flog = f_ref[...] + b_ref[...]
        flog_ref[...] = flog
        lf = jnp.minimum(flog, 0.0) - jnp.log(1.0 + jnp.exp(-jnp.abs(flog)))
        lane = lax.broadcasted_iota(jnp.int32, (1, 128), 1)
        lf = jnp.where(lane < FOX_HEADS, lf, 0.0)
        row = lax.broadcasted_iota(jnp.int32, (TM, TM), 0)
        col = lax.broadcasted_iota(jnp.int32, (TM, TM), 1)
        tri = (row >= col).astype(BF16)
        cum = _tri_dot(tri, lf) + carry_ref[...]
        carry_ref[...] = cum[TM - 1:TM, :]
        aq_ref[...], ak_ref[...] = _fox_operands(cum)

    return pl.pallas_call(
        body, name=name, grid=(s // TM,),
        in_specs=[_row_spec(TM, 128), _vec_spec(128)],
        out_specs=[_row_spec(TM, 128), _row_spec(TM, width), _row_spec(TM, width)],
        out_shape=[jax.ShapeDtypeStruct((s, 128), F32), jax.ShapeDtypeStruct((s, width), BF16),
                   jax.ShapeDtypeStruct((s, width), BF16)],
        scratch_shapes=[pltpu.VMEM((1, 128), F32)],
        compiler_params=_cp("arbitrary"),
    )(fraw, b_pad)


def _gate_bwd(qaux, kaux, flog, name):
    s = flog.shape[0]
    n = s // TM

    def body(qa_ref, ka_ref, fl_ref, dp_ref, db_ref, carry_ref):
        i = pl.program_id(0)

        @pl.when(i == 0)
        def _():
            carry_ref[...] = jnp.zeros_like(carry_ref)

        lane = lax.broadcasted_iota(jnp.int32, (1, 128), 1)
        dcum = jnp.zeros((TM, 128), F32)
        for p in range(HEAD_PAIRS):
            d = qa_ref[p] - pltpu.roll(ka_ref[p], 128 - 3, 1)
            dcum = jnp.where(lane == 2 * p, pltpu.roll(d, 64 + 2 * p, 1),
                             jnp.where(lane == 2 * p + 1, pltpu.roll(d, 2 * p + 1, 1), dcum))
        row = lax.broadcasted_iota(jnp.int32, (TM, TM), 0)
        col = lax.broadcasted_iota(jnp.int32, (TM, TM), 1)
        tri = (row <= col).astype(BF16)
        dlf = _tri_dot(tri, dcum) + carry_ref[...]
        carry_ref[...] = dlf[0:1, :]
        df = jnp.where(lane < FOX_HEADS, dlf / (1.0 + jnp.exp(fl_ref[...])), 0.0)
        dp_ref[...] = df.astype(BF16)
        db = jnp.sum(df, axis=0, keepdims=True)

        @pl.when(i == 0)
        def _():
            db_ref[...] = db

        @pl.when(i > 0)
        def _():
            db_ref[...] += db

    rev = lambda i: (n - 1 - i, 0)
    return pl.pallas_call(
        body, name=name, grid=(n,),
        in_specs=[pl.BlockSpec((HEAD_PAIRS, TM, 128), lambda i: (0, n - 1 - i, 0)),
                  pl.BlockSpec((HEAD_PAIRS, TM, 128), lambda i: (0, n - 1 - i, 0)), pl.BlockSpec((TM, 128), rev)],
        out_specs=[pl.BlockSpec((TM, 128), rev), _vec_spec(128)],
        out_shape=[jax.ShapeDtypeStruct((s, 128), BF16), jax.ShapeDtypeStruct((1, 128), F32)],
        scratch_shapes=[pltpu.VMEM((1, 128), F32)],
        compiler_params=_cp("arbitrary"),
    )(qaux, kaux, flog)


def _pool_consts(i, rows):
    lane = lax.broadcasted_iota(jnp.int32, (rows, D_POOL), 1)
    t1 = lax.broadcasted_iota(jnp.int32, (rows, D_POOL), 0) + i * TM + 1
    win = jnp.where(lane < 64, 2, jnp.where(lane < 128, 4, jnp.where(lane < 192, 8, 16)))
    inv = 1.0 / jnp.minimum(t1, win).astype(F32)
    return lane, inv


def _by_group(lane, s2, s4, s8, s16):
    return jnp.where(lane < 64, s2, jnp.where(lane < 128, s4, jnp.where(lane < 192, s8, s16)))


def _pool_diff(i, u_ref, halo_ref):
    u = u_ref[...].astype(F32)
    halo = jnp.where(i > 0, halo_ref[...].astype(F32), 0.0)
    ext = jnp.concatenate([halo, u], axis=0)
    s2 = ext + pltpu.roll(ext, 1, 0)
    s4 = s2 + pltpu.roll(s2, 2, 0)
    s8 = s4 + pltpu.roll(s4, 4, 0)
    s16 = s8 + pltpu.roll(s8, 8, 0)
    lane, inv = _pool_consts(i, TM)
    sel = _by_group(lane, s2[POOL_HALO:], s4[POOL_HALO:], s8[POOL_HALO:], s16[POOL_HALO:])
    return sel * inv - u


def _pool_fwd(proj, wbd, scale, ycat, name):
    s = proj.shape[0]
    hb = TM // POOL_HALO

    def body(u_ref, halo_ref, w_ref, sc_ref, y_any, y_ref):
        del y_any
        i = pl.program_id(0)
        diff = _pool_diff(i, u_ref, halo_ref)
        mixed = _dot(diff.astype(BF16), w_ref[...], NN)
        y_ref[...] = (mixed * sc_ref[...]).astype(BF16)

    return pl.pallas_call(
        body, name=name, grid=(s // TM,),
        in_specs=[pl.BlockSpec((TM, D_POOL), lambda i: (i, 0)),
                  pl.BlockSpec((POOL_HALO, D_POOL), lambda i: (jnp.maximum(i * hb - 1, 0), 0)),
                  pl.BlockSpec((D_POOL, D_POOL), lambda i: (0, 0)), _vec_spec(D_POOL),
                  pl.BlockSpec(memory_space=pl.ANY)],
        out_specs=pl.BlockSpec((TM, D_POOL), lambda i: (i, 0)),
        out_shape=jax.ShapeDtypeStruct(ycat.shape, ycat.dtype),
        input_output_aliases={4: 0},
        compiler_params=_cp("parallel"),
    )(proj, proj, wbd, scale, ycat)


def _pool_bwd(proj, dycat, wbd, scale, name):
    s = proj.shape[0]
    n = s // TM
    hb = TM // POOL_HALO
    last_halo = s // POOL_HALO - 1

    def body(u_ref, halo_ref, dy_ref, dyp_ref, w_ref, sc_ref, dp_ref, dw_ref, dsc_ref):
        i = pl.program_id(0)
        diff = _pool_diff(i, u_ref, halo_ref)
        diff_b = diff.astype(BF16)
        mixed = _dot(diff_b, w_ref[...], NN)
        dy = dy_ref[...].astype(F32)
        dmix = (dy * sc_ref[...]).astype(BF16)
        dyp = jnp.where(i < n - 1, dyp_ref[...].astype(F32), 0.0)
        dmix_p = (dyp * sc_ref[...]).astype(BF16)
        dd = _dot(dmix, w_ref[...], NT)
        dd_p = _dot(dmix_p, w_ref[...], NT)
        lane, inv = _pool_consts(i, TM)
        _, inv_p = _pool_consts(i + 1, POOL_HALO)
        ext = jnp.concatenate([dd * inv, dd_p * inv_p], axis=0)
        rows = TM + POOL_HALO
        l2 = ext + pltpu.roll(ext, rows - 1, 0)
        l4 = l2 + pltpu.roll(l2, rows - 2, 0)
        l8 = l4 + pltpu.roll(l4, rows - 4, 0)
        l16 = l8 + pltpu.roll(l8, rows - 8, 0)
        du = _by_group(lane, l2[:TM], l4[:TM], l8[:TM], l16[:TM]) - dd
        dp_ref[...] = du.astype(BF16)
        dw = _dot(diff_b, dmix, TN)
        dsc = jnp.sum(dy * mixed, axis=0, keepdims=True)

        @pl.when(i == 0)
        def _():
            dw_ref[...] = dw
            dsc_ref[...] = dsc

        @pl.when(i > 0)
        def _():
            dw_ref[...] += dw
            dsc_ref[...] += dsc

    return pl.pallas_call(
        body, name=name, grid=(n,),
        in_specs=[pl.BlockSpec((TM, D_POOL), lambda i: (i, 0)),
                  pl.BlockSpec((POOL_HALO, D_POOL), lambda i: (jnp.maximum(i * hb - 1, 0), 0)),
                  pl.BlockSpec((TM, D_POOL), lambda i: (i, 0)),
                  pl.BlockSpec((POOL_HALO, D_POOL), lambda i: (jnp.minimum((i + 1) * hb, last_halo), 0)),
                  pl.BlockSpec((D_POOL, D_POOL), lambda i: (0, 0)), _vec_spec(D_POOL)],
        out_specs=[pl.BlockSpec((TM, D_POOL), lambda i: (i, 0)),
                   pl.BlockSpec((D_POOL, D_POOL), lambda i: (0, 0)), _vec_spec(D_POOL)],
        out_shape=[jax.ShapeDtypeStruct((s, D_POOL), BF16),
                   jax.ShapeDtypeStruct((D_POOL, D_POOL), F32), jax.ShapeDtypeStruct((1, D_POOL), F32)],
        compiler_params=_cp("arbitrary"),
    )(proj, proj, dycat, dycat, wbd, scale)


Q_BLK = D_POOL // 128
K_BLK = Q_BLK + D_FOX // 128
V_BLK = K_BLK + D_FOX // 128


def _operand_rows(v0, v1, ones_off):
    row = lax.broadcasted_iota(jnp.int32, (128, 1), 0)
    half = row & 63
    out = jnp.where(jnp.logical_and(half >= ones_off, half < ones_off + 3), 1.0, 0.0) + jnp.zeros_like(v0)
    for base, v in ((64, v0), (0, v1)):
        for j, piece in enumerate(_split3(v)):
            out = jnp.where(row == base + j, piece.astype(F32), out)
    return out


def _fox_operands(cum):
    width = HEAD_PAIRS * 128
    hi, mid, lo = _split3(cum)
    packed = (hi.astype(F32) + pltpu.roll(mid.astype(F32), 16, 1) + pltpu.roll(lo.astype(F32), 32, 1)).astype(BF16)
    row = lax.broadcasted_iota(jnp.int32, (128, width), 0)
    col = lax.broadcasted_iota(jnp.int32, (128, width), 1)
    head, j = row & 15, row >> 4
    base = (head >> 1) * 128 + (1 - (head & 1)) * 64
    used = jnp.logical_and(head < FOX_HEADS, j < 3)
    half = lax.broadcasted_iota(jnp.int32, (1, width), 1) & 63
    res = []
    for off, sign, ones_off in ((0, 1.0, 3), (3, -1.0, 0)):
        ones = jnp.where(jnp.logical_and(half >= ones_off, half < ones_off + 3), 1.0, 0.0)
        sel = jnp.where(jnp.logical_and(col == base + off + j, used), sign, 0.0).astype(BF16)
        res.append((ones + _dot(packed, sel, NN)).astype(BF16))
    return res


def _fox_do_operand(dycat, ycat, after, name):
    s = dycat.shape[0]

    rb = 1024
    nblk = D_FOX // D_POOL

    def body(*refs):
        do_refs, o_refs, ad_ref = refs[:nblk], refs[nblk:2 * nblk], refs[-1]
        src = lax.broadcasted_iota(jnp.int32, (D_POOL, D_POOL), 0)
        dst = lax.broadcasted_iota(jnp.int32, (D_POOL, D_POOL), 1)
        same_pair = (src >> 7) == (dst >> 7)
        s_in, d_in = src & 127, dst & 127
        hit = jnp.logical_and(same_pair, jnp.logical_or(
            jnp.logical_and(s_in < 64, jnp.logical_and(d_in >= 64, d_in < 67)), jnp.logical_and(s_in >= 64, d_in < 3)))
        sel = jnp.where(hit, 1.0, 0.0).astype(BF16)
        j = lax.broadcasted_iota(jnp.int32, (1, D_POOL), 1) & 63
        for b in range(nblk):
            dd = do_refs[b][...].astype(F32) * o_refs[b][...].astype(F32)
            dsum = jnp.zeros(dd.shape, F32)
            for piece in _split3(dd):
                dsum = dsum + _dot(piece, sel, NN)
            hi, mid, lo = _split3(-dsum)
            ad_ref[:, D_POOL * b:D_POOL * (b + 1)] = jnp.where(j == 0, hi, jnp.where(j == 1, mid, lo))

    blks = [pl.BlockSpec((rb, D_POOL), functools.partial(lambda i, b: (i, 1 + b), b=b)) for b in range(nblk)]
    return pl.pallas_call(
        body, name=name, grid=(s // rb,), in_specs=blks + blks + [pl.BlockSpec(memory_space=pl.ANY)],
        out_specs=pl.BlockSpec((rb, D_FOX), lambda i: (i, 0)),
        out_shape=jax.ShapeDtypeStruct((s, D_FOX), BF16),
        compiler_params=_cp("parallel"),
    )(*[dycat] * nblk, *[ycat] * nblk, after)


def _causal_pairs(nq, key_major):
    if key_major:
        pairs = [(q, k) for k in range(nq) for q in range(k, nq)]
    else:
        pairs = [(q, k) for q in range(nq) for k in range(q + 1)]
    return (jnp.asarray([p[0] for p in pairs], jnp.int32), jnp.asarray([p[1] for p in pairs], jnp.int32))


def _fox_fwd(proj, aq, ak, name):
    s = proj.shape[0]
    nq = s // TQ
    qi_arr, ki_arr = _causal_pairs(nq, key_major=False)

    def body(qi_ref, ki_ref, q_ref, k_ref, v_ref, aq_ref, ak_ref, o_ref, aqb_ref, m0_ref, m1_ref, acc_ref, aux_ref):
        t = pl.program_id(1)
        qi, ki = qi_ref[t], ki_ref[t]
        lane = lax.broadcasted_iota(jnp.int32, (1, 128), 1)
        masks = [lane < 64, lane >= 64]
        ones_v = jnp.where((lane & 63) == 8, 1.0, 0.0).astype(BF16)
        top = lax.broadcasted_iota(jnp.int32, (128, 1), 0) < 64
        m_ref = [m0_ref, m1_ref]

        @pl.when(ki == 0)
        def _():
            m0_ref[...] = jnp.full_like(m0_ref, NEG)
            m1_ref[...] = jnp.full_like(m1_ref, NEG)
            acc_ref[...] = jnp.zeros_like(acc_ref)
            aux_ref[...] = jnp.zeros_like(aux_ref)

        def step(diag):
            q2s = q_ref[...] * 0.125
            k2, v2, aq2, ak2 = k_ref[...], v_ref[...], aq_ref[...], ak_ref[...]
            pv, alpha = [], []
            for hh in range(2):
                qh = jnp.where(masks[hh], q2s, aq2)
                kh = jnp.where(masks[hh], k2, ak2)
                vh = jnp.where(masks[hh], v2, ones_v)
                sc = _dot(kh, qh, NT)
                if diag:
                    key = lax.broadcasted_iota(jnp.int32, sc.shape, 0)
                    qry = lax.broadcasted_iota(jnp.int32, sc.shape, 1)
                    sc = jnp.where(qry >= key, sc, NEG)
                m_prev = m_ref[hh][...]
                m_new = jnp.maximum(m_prev, jnp.max(sc, axis=0, keepdims=True))
                m_ref[hh][...] = m_new
                alpha.append(jnp.exp(m_prev - m_new))
                pv.append(_dot(vh, jnp.exp(sc - m_new).astype(BF16), TN))
            acc_ref[...] = acc_ref[...] * jnp.where(top, alpha[0], alpha[1]) + jnp.where(top, pv[0], pv[1])
            aux_ref[...] = aux_ref[...] * jnp.where(top, alpha[1], alpha[0]) + jnp.where(top, pv[1], pv[0])

        @pl.when(ki < qi)
        def _():
            step(False)

        @pl.when(ki == qi)
        def _():
            step(True)
            aux = aux_ref[...]
            l0, l1 = aux[72:73, :], aux[8:9, :]
            o_ref[...] = (acc_ref[...] * jnp.where(top, 1.0 / l0, 1.0 / l1)).T.astype(BF16)
            aqt = aq_ref[...].astype(F32).T
            cum0 = aqt[64:65, :] + aqt[65:66, :] + aqt[66:67, :]
            cum1 = aqt[0:1, :] + aqt[1:2, :] + aqt[2:3, :]
            aqb = _operand_rows(cum0 - (m0_ref[...] + jnp.log(l0)), cum1 - (m1_ref[...] + jnp.log(l1)), 3)
            aqb_ref[...] = aqb.T.astype(BF16)

    grid_spec = pltpu.PrefetchScalarGridSpec(
        num_scalar_prefetch=2, grid=(HEAD_PAIRS, int(qi_arr.shape[0])),
        in_specs=[pl.BlockSpec((TQ, 128), lambda p, t, qi, ki: (qi[t], Q_BLK + p)),
                  pl.BlockSpec((TQ, 128), lambda p, t, qi, ki: (ki[t], K_BLK + p)),
                  pl.BlockSpec((TQ, 128), lambda p, t, qi, ki: (ki[t], V_BLK + p)),
                  pl.BlockSpec((TQ, 128), lambda p, t, qi, ki: (qi[t], p)),
                  pl.BlockSpec((TQ, 128), lambda p, t, qi, ki: (ki[t], p))],
        out_specs=[pl.BlockSpec((TQ, 128), lambda p, t, qi, ki: (qi[t], Q_BLK + p)),
                   pl.BlockSpec((TQ, 128), lambda p, t, qi, ki: (qi[t], p))],
        scratch_shapes=[pltpu.VMEM((1, TQ), F32), pltpu.VMEM((1, TQ), F32),
                        pltpu.VMEM((128, TQ), F32), pltpu.VMEM((128, TQ), F32)])
    return pl.pallas_call(
        body, name=name, grid_spec=grid_spec,
        out_shape=[jax.ShapeDtypeStruct((s, D_MODEL), BF16), jax.ShapeDtypeStruct((s, HEAD_PAIRS * 128), BF16)],
        compiler_params=_cp("parallel", "arbitrary"),
    )(qi_arr, ki_arr, proj, proj, proj, aq, ak)


def _fox_bwd(proj, dycat, aqb, ak, ad, name):
    s = proj.shape[0]
    nq = s // TQ
    qi_arr, ki_arr = _causal_pairs(nq, key_major=True)

    def body(qi_ref, ki_ref, q_ref, k_ref, v_ref, do_ref, aq_ref, ak_ref, ad_ref,
             dq_ref, dk_ref, dv_ref, qaux_ref, kaux_ref, dq_acc, qaux_acc, dk_acc, dv_acc, kaux_acc):
        t = pl.program_id(1)
        qi, ki = qi_ref[t], ki_ref[t]
        lane = lax.broadcasted_iota(jnp.int32, (1, 128), 1)
        masks = [lane < 64, lane >= 64]
        ones_v = jnp.where((lane & 63) < 3, 1.0, 0.0).astype(BF16)
        top = lax.broadcasted_iota(jnp.int32, (128, 1), 0) < 64

        @pl.when(qi == ki)
        def _():
            dk_acc[...] = jnp.zeros_like(dk_acc)
            dv_acc[...] = jnp.zeros_like(dv_acc)
            kaux_acc[...] = jnp.zeros_like(kaux_acc)

        def step(diag):
            q2s = q_ref[...] * 0.125
            k2, v2, do2 = k_ref[...], v_ref[...], do_ref[...]
            aq2, ak2, ad2 = aq_ref[...], ak_ref[...], ad_ref[...]
            dq, dk, dv = [], [], []
            for hh in range(2):
                qh = jnp.where(masks[hh], q2s, aq2)
                kh = jnp.where(masks[hh], k2, ak2)
                doh = jnp.where(masks[hh], do2, ad2)
                vh = jnp.where(masks[hh], v2, ones_v)
                sc = _dot(kh, qh, NT)
                if diag:
                    key = lax.broadcasted_iota(jnp.int32, sc.shape, 0)
                    qry = lax.broadcasted_iota(jnp.int32, sc.shape, 1)
                    sc = jnp.where(qry >= key, sc, NEG)
                p = jnp.exp(sc)
                dsb = (p * _dot(vh, doh, NT)).astype(BF16)
                dv.append(_dot(p.astype(BF16), doh, NN))
                dk.append(_dot(dsb, qh, NN))
                dq.append(_dot(kh, dsb, TN))
            dk_acc[...] += jnp.where(masks[0], dk[0], dk[1])
            kaux_acc[...] += jnp.where(masks[0], dk[1], dk[0])
            dv_acc[...] += jnp.where(masks[0], dv[0], dv[1])
            dq_new = jnp.where(top, dq[0], dq[1])
            qaux_new = jnp.where(top, dq[1], dq[0])

            @pl.when(ki == 0)
            def _():
                dq_acc[qi] = dq_new
                qaux_acc[qi] = qaux_new

            @pl.when(ki > 0)
            def _():
                dq_acc[qi] += dq_new
                qaux_acc[qi] += qaux_new

        @pl.when(qi > ki)
        def _():
            step(False)

        @pl.when(qi == ki)
        def _():
            step(True)
            rows = pl.ds(pl.multiple_of(qi * TQ, TQ), TQ)
            dq_ref[rows, :] = (dq_acc[qi] * 0.125).T.astype(BF16)
            qaux_ref[rows, :] = qaux_acc[qi].T

        @pl.when(qi == nq - 1)
        def _():
            dk_ref[...] = dk_acc[...].astype(BF16)
            dv_ref[...] = dv_acc[...].astype(BF16)
            kaux_ref[...] = kaux_acc[...]

    grid_spec = pltpu.PrefetchScalarGridSpec(
        num_scalar_prefetch=2, grid=(HEAD_PAIRS, int(qi_arr.shape[0])),
        in_specs=[pl.BlockSpec((TQ, 128), lambda p, t, qi, ki: (qi[t], Q_BLK + p)),
                  pl.BlockSpec((TQ, 128), lambda p, t, qi, ki: (ki[t], K_BLK + p)),
                  pl.BlockSpec((TQ, 128), lambda p, t, qi, ki: (ki[t], V_BLK + p)),
                  pl.BlockSpec((TQ, 128), lambda p, t, qi, ki: (qi[t], Q_BLK + p)),
                  pl.BlockSpec((TQ, 128), lambda p, t, qi, ki: (qi[t], p)),
                  pl.BlockSpec((TQ, 128), lambda p, t, qi, ki: (ki[t], p)),
                  pl.BlockSpec((TQ, 128), lambda p, t, qi, ki: (qi[t], p))],
        out_specs=[pl.BlockSpec((s, 128), lambda p, t, qi, ki: (0, p)),
                   pl.BlockSpec((TQ, 128), lambda p, t, qi, ki: (ki[t], p)),
                   pl.BlockSpec((TQ, 128), lambda p, t, qi, ki: (ki[t], p)),
                   pl.BlockSpec((None, s, 128), lambda p, t, qi, ki: (p, 0, 0)),
                   pl.BlockSpec((None, TQ, 128), lambda p, t, qi, ki: (p, ki[t], 0))],
        scratch_shapes=[pltpu.VMEM((nq, 128, TQ), F32), pltpu.VMEM((nq, 128, TQ), F32),
                        pltpu.VMEM((TQ, 128), F32), pltpu.VMEM((TQ, 128), F32), pltpu.VMEM((TQ, 128), F32)])
    return pl.pallas_call(
        body, name=name, grid_spec=grid_spec,
        out_shape=[jax.ShapeDtypeStruct((s, D_FOX), BF16)] * 3 + [jax.ShapeDtypeStruct((HEAD_PAIRS, s, 128), F32)] * 2,
        compiler_params=_cp("arbitrary", "arbitrary"),
    )(qi_arr, ki_arr, proj, proj, proj, dycat, aqb, ak, ad)


XA_SCALE = XA_DIM ** -0.5


def _xattn_fwd(q2, kv, name):
    s = q2.shape[0]
    m = kv.shape[0]

    def body(q_ref, kv_ref, o_ref):
        for h in range(XA_HEADS):
            c0 = h * XA_DIM
            sc = _dot(q_ref[:, c0:c0 + XA_DIM], kv_ref[:, c0:c0 + XA_DIM], NT) * XA_SCALE
            e = jnp.exp(sc - jnp.max(sc, axis=1, keepdims=True))
            p = e / jnp.sum(e, axis=1, keepdims=True)
            o_ref[:, c0:c0 + XA_DIM] = _dot(p.astype(BF16), kv_ref[:, D_MODEL + c0:D_MODEL + c0 + XA_DIM], NN).astype(BF16)

    return pl.pallas_call(
        body, name=name, grid=(s // TM,),
        in_specs=[_row_spec(TM, D_MODEL), pl.BlockSpec((m, 2 * D_MODEL), lambda i: (0, 0))],
        out_specs=_row_spec(TM, D_MODEL), out_shape=jax.ShapeDtypeStruct((s, D_MODEL), BF16),
        compiler_params=_cp("parallel"),
    )(q2, kv)


def _xattn_bwd(q2, kv, do2, name):
    s = q2.shape[0]
    m = kv.shape[0]

    def body(q_ref, kv_ref, do_ref, dq_ref, dkv_ref):
        i = pl.program_id(0)

        @pl.when(i == 0)
        def _():
            dkv_ref[...] = jnp.zeros_like(dkv_ref)

        for h in range(XA_HEADS):
            c0 = h * XA_DIM
            v0 = D_MODEL + c0
            qh = q_ref[:, c0:c0 + XA_DIM]
            kh = kv_ref[:, c0:c0 + XA_DIM]
            doh = do_ref[:, c0:c0 + XA_DIM]
            sc = _dot(kh, qh, NT) * XA_SCALE
            e = jnp.exp(sc - jnp.max(sc, axis=0, keepdims=True))
            p = e / jnp.sum(e, axis=0, keepdims=True)
            dp = _dot(kv_ref[:, v0:v0 + XA_DIM], doh, NT)
            ds = p * (dp - jnp.sum(p * dp, axis=0, keepdims=True))
            dsb = (ds * XA_SCALE).astype(BF16)
            dq_ref[:, c0:c0 + XA_DIM] = _dot(kh, dsb, TN).T.astype(BF16)
            dkv_ref[:, c0:c0 + XA_DIM] += _dot(dsb, qh, NN)
            dkv_ref[:, v0:v0 + XA_DIM] += _dot(p.astype(BF16), doh, NN)

    return pl.pallas_call(
        body, name=name, grid=(s // TM,),
        in_specs=[_row_spec(TM, D_MODEL), pl.BlockSpec((m, 2 * D_MODEL), lambda i: (0, 0)), _row_spec(TM, D_MODEL)],
        out_specs=[_row_spec(TM, D_MODEL), pl.BlockSpec((m, 2 * D_MODEL), lambda i: (0, 0))],
        out_shape=[jax.ShapeDtypeStruct((s, D_MODEL), BF16), jax.ShapeDtypeStruct((m, 2 * D_MODEL), F32)],
        compiler_params=_cp("arbitrary"),
    )(q2, kv, do2)


GELU_C = math.sqrt(2.0 / math.pi)
GELU_A = 0.044715


def _gelu(x):
    return (0.5 * x) * (1.0 + jnp.tanh(x * (GELU_C * GELU_A * (x * x) + GELU_C)))


def _gelu_and_grad(x):
    x2 = x * x
    s = 1.0 + jnp.tanh(x * (GELU_C * GELU_A * x2 + GELU_C))
    hx = 0.5 * x
    return hx * s, s * (0.5 + hx * (2.0 - s) * (3.0 * GELU_C * GELU_A * x2 + GELU_C))


def _conv(h, s1, s2, w_ref, b_ref):
    return w_ref[0:1, :] * s2 + w_ref[1:2, :] * s1 + w_ref[2:3, :] * h + b_ref[...]


def _shift_down(main, prev8):
    row = lax.broadcasted_iota(jnp.int32, main.shape, 0)
    s1 = jnp.where(row == 0, prev8[7:8, :], pltpu.roll(main, 1, 0))
    s2 = jnp.where(row == 0, prev8[6:7, :], jnp.where(row == 1, prev8[7:8, :], pltpu.roll(main, 2, 0)))
    return s1, s2


def _shift_up(main, next8):
    n = main.shape[0]
    row = lax.broadcasted_iota(jnp.int32, main.shape, 0)
    u1 = jnp.where(row == n - 1, next8[0:1, :], pltpu.roll(main, n - 1, 0))
    u2 = jnp.where(row == n - 2, next8[0:1, :], jnp.where(row == n - 1, next8[1:2, :], pltpu.roll(main, n - 2, 0)))
    return u1, u2


def _ffn_fwd(h3, w_up, cw, cb, w_down, x2, tgt, g_post, name):
    s = h3.shape[0]
    tn = TN_FF
    nj = D_FF // tn
    per = D_MODEL // tn
    hb = TM // 8

    def body(h_ref, halo_ref, wg_ref, wu_ref, cwg_ref, cwu_ref, cbg_ref, cbu_ref, wd_ref, x_ref, t_ref, g_ref,
             hg_ref, hu_ref, cg_ref, cu_ref, a_ref, loss_ref, dx_ref, dy_ref, dg_ref, y_acc):
        i, j = pl.program_id(0), pl.program_id(1)
        h = h_ref[...]
        halo = halo_ref[...]
        halo = jnp.where(i > 0, halo, jnp.zeros_like(halo))
        conv = []
        for w_ref, cw_ref, cb_ref, hid_ref, c_ref in ((wg_ref, cwg_ref, cbg_ref, hg_ref, cg_ref),
                                                      (wu_ref, cwu_ref, cbu_ref, hu_ref, cu_ref)):
            hm = _dot(h, w_ref[...], NN)
            hid_ref[...] = hm.astype(BF16)
            s1, s2 = _shift_down(hm, _dot(halo, w_ref[...], NN))
            c = _conv(hm, s1, s2, cw_ref, cb_ref)
            c_ref[...] = c.astype(BF16)
            conv.append(c)
        a = (_gelu(conv[0]) * conv[1]).astype(BF16)
        a_ref[...] = a
        contrib = _dot(a, wd_ref[...], NN)

        @pl.when(j == 0)
        def _():
            y_acc[...] = contrib

        @pl.when(j > 0)
        def _():
            y_acc[...] += contrib

        @pl.when(j == nj - 1)
        def _():
            yv = y_acc[...]
            r = _rstd(yv)
            yn = yv * r
            e = x_ref[...] + yn * g_ref[...] - t_ref[...]
            part = 0.5 * jnp.sum(jnp.mean(e * e, axis=-1, keepdims=True), axis=0, keepdims=True)
            part = jnp.broadcast_to(part, (1, 128))
            dx = e * (1.0 / D_MODEL)
            dx_ref[...] = dx
            dy_ref[...] = _norm_bwd_rows(dx * g_ref[...], yn, r).astype(BF16)
            dg = jnp.sum(dx * yn, axis=0, keepdims=True)

            @pl.when(i == 0)
            def _():
                dg_ref[...] = dg
                loss_ref[...] = part

            @pl.when(i > 0)
            def _():
                dg_ref[...] += dg
                loss_ref[...] += part

    rows = pl.BlockSpec((TM, D_MODEL), lambda i, j: (i, 0))
    tile = pl.BlockSpec((TM, tn), lambda i, j: (i, j))
    wide = jax.ShapeDtypeStruct((s, D_FF), BF16)
    return pl.pallas_call(
        body, name=name, grid=(s // TM, nj),
        in_specs=[rows,
                  pl.BlockSpec((8, D_MODEL), lambda i, j: (jnp.maximum(i * hb - 1, 0), 0)),
                  pl.BlockSpec((None, D_MODEL, tn), lambda i, j: (j // per, 0, j % per)),
                  pl.BlockSpec((None, D_MODEL, tn), lambda i, j: (NDEV // 2 + j // per, 0, j % per)),
                  pl.BlockSpec((8, tn), lambda i, j: (0, j)),
                  pl.BlockSpec((8, tn), lambda i, j: (0, nj + j)),
                  pl.BlockSpec((1, tn), lambda i, j: (0, j)),
                  pl.BlockSpec((1, tn), lambda i, j: (0, nj + j)),
                  pl.BlockSpec((tn, D_MODEL), lambda i, j: (j, 0)),
                  rows, rows, pl.BlockSpec((1, D_MODEL), lambda i, j: (0, 0))],
        out_specs=[tile, tile, tile, tile, tile,
                   pl.BlockSpec((1, 128), lambda i, j: (0, 0)), rows, rows,
                   pl.BlockSpec((1, D_MODEL), lambda i, j: (0, 0))],
        out_shape=[wide, wide, wide, wide, wide,
                   jax.ShapeDtypeStruct((1, 128), F32), jax.ShapeDtypeStruct((s, D_MODEL), F32),
                   jax.ShapeDtypeStruct((s, D_MODEL), BF16), jax.ShapeDtypeStruct((1, D_MODEL), F32)],
        scratch_shapes=[pltpu.VMEM((TM, D_MODEL), F32)],
        compiler_params=_cp("arbitrary", "arbitrary"),
    )(h3, h3, w_up, w_up, cw, cw, cb, cb, w_down, x2, tgt, g_post)


def _ffn_bwd(dy3, w_down, hid_g, hid_u, conv_g, conv_u, cw, name):
    s = dy3.shape[0]
    n = s // TM
    tn = TN_FF
    nj = D_FF // tn
    hb = TM // 8
    last8 = s // 8 - 1

    def body(dy_ref, dyn_ref, wd_ref, hg_ref, hu_ref, cg_ref, cgn_ref, cu_ref, cun_ref, cwg_ref, cwu_ref,
             dhg_ref, dhu_ref, dcwg_ref, dcwu_ref, dcbg_ref, dcbu_ref):
        i = pl.program_id(1)
        first, last = i == 0, i == n - 1
        da = _dot(dy_ref[...], wd_ref[...], NT)
        dyn = dyn_ref[...]
        dyn = jnp.where(last, jnp.zeros_like(dyn), dyn)
        da_n = _dot(dyn, wd_ref[...], NT)
        c_g, c_u = cg_ref[...].astype(F32), cu_ref[...].astype(F32)
        g, dg = _gelu_and_grad(c_g)
        gn, dgn = _gelu_and_grad(cgn_ref[...].astype(F32))
        outs = ((da * c_u * dg, da_n * cun_ref[...].astype(F32) * dgn, hg_ref, cwg_ref, dhg_ref, dcwg_ref, dcbg_ref),
                (da * g, da_n * gn, hu_ref, cwu_ref, dhu_ref, dcwu_ref, dcbu_ref))
        row8 = lax.broadcasted_iota(jnp.int32, (8, tn), 0)
        for dc, dcn, h_ref, cw_ref, dh_ref, dcw_ref, dcb_ref in outs:
            u1, u2 = _shift_up(dc, dcn)
            dh_ref[...] = (cw_ref[2:3, :] * dc + cw_ref[1:2, :] * u1 + cw_ref[0:1, :] * u2).astype(BF16)
            hm = h_ref[...].astype(F32)
            dcb = jnp.sum(dc, axis=0, keepdims=True)
            dcw = jnp.where(row8 == 0, jnp.sum(hm * u2, axis=0, keepdims=True),
                            jnp.where(row8 == 1, jnp.sum(hm * u1, axis=0, keepdims=True),
                                      jnp.where(row8 == 2, jnp.sum(hm * dc, axis=0, keepdims=True), 0.0)))

            @pl.when(first)
            def _():
                dcw_ref[...] = dcw
                dcb_ref[...] = dcb

            @pl.when(i > 0)
            def _():
                dcw_ref[...] += dcw
                dcb_ref[...] += dcb

    next8 = lambda j, i: (jnp.minimum((i + 1) * hb, last8), j)
    blk = lambda j, i: (i, j)
    col = lambda j, i: (0, j)
    colu = lambda j, i: (0, nj + j)
    tile = pl.BlockSpec((TM, tn), blk)
    return pl.pallas_call(
        body, name=name, grid=(nj, n),
        in_specs=[pl.BlockSpec((TM, D_MODEL), lambda j, i: (i, 0)),
                  pl.BlockSpec((8, D_MODEL), lambda j, i: (jnp.minimum((i + 1) * hb, last8), 0)),
                  pl.BlockSpec((tn, D_MODEL), lambda j, i: (j, 0)),
                  tile, tile, tile, pl.BlockSpec((8, tn), next8), tile, pl.BlockSpec((8, tn), next8),
                  pl.BlockSpec((8, tn), col), pl.BlockSpec((8, tn), colu)],
        out_specs=[tile, tile, pl.BlockSpec((8, tn), col), pl.BlockSpec((8, tn), col),
                   pl.BlockSpec((1, tn), col), pl.BlockSpec((1, tn), col)],
        out_shape=[jax.ShapeDtypeStruct((s, D_FF), BF16), jax.ShapeDtypeStruct((s, D_FF), BF16),
                   jax.ShapeDtypeStruct((8, D_FF), F32), jax.ShapeDtypeStruct((8, D_FF), F32),
                   jax.ShapeDtypeStruct((1, D_FF), F32), jax.ShapeDtypeStruct((1, D_FF), F32)],
        compiler_params=_cp("parallel", "arbitrary"),
    )(dy3, dy3, w_down, hid_g, hid_u, conv_g, conv_g, conv_u, conv_u, cw, cw)


def _slot(p):
    return 4 * p[0] + 2 * p[1] + p[2]


def _all_gather(shards, name):
    n = len(shards)

    def body(*refs):
        ins, outs = refs[:n], refs[n:2 * n]
        send_sems, recv_sems, local_sems = refs[2 * n:]
        x, y, c = lax.axis_index("x"), lax.axis_index("y"), lax.axis_index("c")
        me, sibling = (x, y, c), (x, y, 1 - c)
        chips = [(1 - x, y), (x, 1 - y), (1 - x, 1 - y)]

        def copy(a, k, block, to, from_input=False):
            dst = outs[a].at[_slot(block)]
            return pltpu.make_async_remote_copy(
                src_ref=ins[a] if from_input else dst, dst_ref=dst,
                send_sem=send_sems.at[a, k], recv_sem=recv_sems.at[a, k],
                device_id=to, device_id_type=MESH)

        mine = [pltpu.make_async_copy(ins[a], outs[a].at[_slot(me)], local_sems.at[a]) for a in range(n)]
        for cp in mine:
            cp.start()
        first = []
        for a in range(n):
            first.append(copy(a, 0, me, sibling, True))
            first += [copy(a, 1 + j, me, (*chip, c), True) for j, chip in enumerate(chips)]
        for cp in first:
            cp.start()
        passed = []
        for j, chip in enumerate(chips):
            for a in range(n):
                copy(a, 1 + j, (*chip, c), me).wait_recv()
                fwd = copy(a, 4 + j, (*chip, c), sibling)
                fwd.start()
                passed.append(fwd)
        for a in range(n):
            copy(a, 0, sibling, me).wait_recv()
            for j, chip in enumerate(chips):
                copy(a, 4 + j, (*chip, 1 - c), me).wait_recv()
        for cp in first + passed:
            cp.wait_send()
        for cp in mine:
            cp.wait()

    any_spec = pl.BlockSpec(memory_space=pl.ANY)
    return pl.pallas_call(
        body, name=name,
        in_specs=[any_spec] * n, out_specs=[any_spec] * n,
        out_shape=[jax.ShapeDtypeStruct((NDEV,) + s.shape, s.dtype) for s in shards],
        scratch_shapes=[pltpu.SemaphoreType.DMA((n, 7)), pltpu.SemaphoreType.DMA((n, 7)),
                        pltpu.SemaphoreType.DMA((n,))],
    )(*shards)


def _peer_list(x, y, c):
    return [(1 - x if m & 4 else x, 1 - y if m & 2 else y, 1 - c if m & 1 else c) for m in range(1, NDEV)]


def _exchange_copies(src_refs, land_refs, send_sems, recv_sems, gather):
    x, y, c = lax.axis_index("x"), lax.axis_index("y"), lax.axis_index("c")
    me = (x, y, c)
    copies = []
    for m, peer in enumerate(_peer_list(x, y, c)):
        for a in range(len(src_refs)):
            copies.append(pltpu.make_async_remote_copy(
                src_ref=src_refs[a] if gather else src_refs[a].at[_slot(peer)], dst_ref=land_refs[a].at[_slot(me)],
                send_sem=send_sems.at[a * (NDEV - 1) + m], recv_sem=recv_sems.at[a * (NDEV - 1) + m],
                device_id=peer, device_id_type=MESH))
    return copies


def _all_gather_small(shards, name):
    n = len(shards)

    def body(*refs):
        ins, outs = refs[:n], refs[n:2 * n]
        send_sems, recv_sems, local_sems = refs[2 * n:]
        me = (lax.axis_index("x"), lax.axis_index("y"), lax.axis_index("c"))
        mine = [pltpu.make_async_copy(ins[a], outs[a].at[_slot(me)], local_sems.at[a]) for a in range(n)]
        copies = _exchange_copies(ins, outs, send_sems, recv_sems, True)
        for cp in mine + copies:
            cp.start()
        for cp in copies + mine:
            cp.wait()

    any_spec = pl.BlockSpec(memory_space=pl.ANY)
    return pl.pallas_call(
        body, name=name,
        in_specs=[any_spec] * n, out_specs=[any_spec] * n,
        out_shape=[jax.ShapeDtypeStruct((NDEV,) + s.shape, s.dtype) for s in shards],
        scratch_shapes=[pltpu.SemaphoreType.DMA((n * (NDEV - 1),)), pltpu.SemaphoreType.DMA((n * (NDEV - 1),)),
                        pltpu.SemaphoreType.DMA((n,))],
    )(*shards)


def _exchange_start(srcs, lands, after, gather, name):
    n = len(srcs)
    hbm = pl.BlockSpec(memory_space=pltpu.HBM)

    def body(*refs):
        for cp in _exchange_copies(refs[:n], refs[n:2 * n], refs[2 * n + 1], refs[2 * n + 2], gather):
            cp.start()
        token = refs[-1]
        token[...] = jnp.zeros_like(token)

    outs = pl.pallas_call(
        body, name=name,
        out_shape=(pltpu.SemaphoreType.DMA((n * (NDEV - 1),)), pltpu.SemaphoreType.DMA((n * (NDEV - 1),)),
                   *[pltpu.HBM(a.shape, a.dtype) for a in list(srcs) + list(lands)],
                   jax.ShapeDtypeStruct((8, 128), F32)),
        in_specs=[hbm] * (2 * n) + [pl.BlockSpec(memory_space=pl.ANY)],
        out_specs=(pl.BlockSpec(memory_space=pltpu.SEMAPHORE), pl.BlockSpec(memory_space=pltpu.SEMAPHORE),
                   *[hbm] * (2 * n), pl.BlockSpec(memory_space=pltpu.VMEM)),
        input_output_aliases={i: 2 + i for i in range(2 * n)},
        compiler_params=pltpu.CompilerParams(has_side_effects=pltpu.SideEffectType.DATAFLOW_SIDE_EFFECTING),
    )(*[pltpu.with_memory_space_constraint(a, pltpu.HBM) for a in list(srcs) + list(lands)], after)
    return outs[0], outs[1], outs[2:2 + n], outs[2 + n:2 + 2 * n], outs[-1]


def _exchange_wait(send_sems, recv_sems, srcs, lands, after, gather, name):
    n = len(srcs)
    hbm = pl.BlockSpec(memory_space=pltpu.HBM)

    def body(*refs):
        for cp in _exchange_copies(refs[:n], refs[n:2 * n], refs[2 * n], refs[2 * n + 1], gather):
            cp.wait_send()
            cp.wait_recv()

    outs = pl.pallas_call(
        body, name=name,
        out_shape=tuple(pltpu.HBM(a.shape, a.dtype) for a in list(srcs) + list(lands)),
        in_specs=[hbm] * (2 * n) + [pl.BlockSpec(memory_space=pltpu.SEMAPHORE)] * 2 + [pl.BlockSpec(memory_space=pl.ANY)],
        out_specs=tuple([hbm] * (2 * n)),
        input_output_aliases={i: i for i in range(2 * n)},
        compiler_params=pltpu.CompilerParams(has_side_effects=pltpu.SideEffectType.DATAFLOW_SIDE_EFFECTING),
    )(*srcs, *lands, send_sems, recv_sems, after)
    return outs[n:]


def _own_slot(block):
    me = 4 * lax.axis_index("x") + 2 * lax.axis_index("y") + lax.axis_index("c")
    return lax.dynamic_update_slice(lax.empty((NDEV,) + block.shape, block.dtype), block[None], (me, 0, 0))


def _adam_update(p_ref, w_ref, m_ref, v_ref, g_ref, d_ref, mo_ref, vo_ref):
    bc1 = 1.0 - ADAM_B1 ** ADAM_STEP
    bc2 = 1.0 - ADAM_B2 ** ADAM_STEP
    g = p_ref[0].astype(F32)
    for d in range(1, NDEV):
        g = g + p_ref[d].astype(F32)
    g_ref[...] = g
    mn = ADAM_B1 * m_ref[...] + (1.0 - ADAM_B1) * g
    vn = ADAM_B2 * v_ref[...] + (1.0 - ADAM_B2) * (g * g)
    mo_ref[...] = mn
    vo_ref[...] = vn
    d_ref[...] = -ADAM_LR * ((mn / bc1) / (jnp.sqrt(vn / bc2) + ADAM_EPS) + ADAM_WD * w_ref[...])


def _adamw_small(parts, ws, ms, vs, name):
    n = len(ws)

    def body(*refs):
        ins, outs = refs[:4 * n], refs[4 * n:]
        for k in range(n):
            _adam_update(ins[k], ins[n + k], ins[2 * n + k], ins[3 * n + k], *outs[4 * k:4 * k + 4])

    whole = pl.BlockSpec(memory_space=pltpu.VMEM)
    res = pl.pallas_call(
        body, name=name, in_specs=[whole] * (4 * n), out_specs=[whole] * (4 * n),
        out_shape=[jax.ShapeDtypeStruct(a.shape, F32) for a in ws for _ in range(4)],
    )(*parts, *ws, *ms, *vs)
    return [res[4 * k:4 * k + 4] for k in range(n)]


def _adamw(parts, w, m, v, name):
    r, c = w.shape
    tr = r if r * c <= 160 * 1024 else max(8, (160 * 1024 // c) // 8 * 8)
    while r % tr:
        tr -= 8
    body = functools.partial(_adam_update)
    spec = pl.BlockSpec((tr, c), lambda i: (i, 0))
    return pl.pallas_call(
        body, name=name, grid=(r // tr,),
        in_specs=[pl.BlockSpec((NDEV, tr, c), lambda i: (0, i, 0)), spec, spec, spec],
        out_specs=[spec] * 4, out_shape=[jax.ShapeDtypeStruct((r, c), F32)] * 4,
        compiler_params=_cp("parallel"),
    )(parts, w, m, v)


def _local_step(x, mem, tgt, gains, b_forget, w_pool, pool_scale, conv_b, w_in,
                mix_weights, ffn_weights, send_in_grad, send_mix_grads, send_ffn_grads):
    b_pad = jnp.pad(b_forget, ((0, 0), (0, 128 - FOX_HEADS)))
    wbd = jnp.zeros((D_POOL, D_POOL), F32)
    for g in range(4):
        wbd = wbd.at[64 * g:64 * g + 64, 64 * g:64 * g + 64].set(w_pool[g])
    wbd = wbd.astype(BF16)
    scale = pool_scale.reshape(1, D_POOL)

    h1, proj, fraw = _proj_in(x, gains["mix_pre"], w_in, "proj_in")
    flog, aq, ak = _gate_cumsum(fraw, b_pad, "gate_cumsum")
    ycat, aqb = _fox_fwd(proj, aq, ak, "fox_fwd")
    ycat = _pool_fwd(proj, wbd, scale, ycat, "pool_fwd")
    w_mix, w_xq, w_xo, w_xkv = mix_weights(ycat)
    y1, x1, h2 = _mm_rows(ycat, w_mix, "nn", 1024, "mix_out", [x], [gains["mix_post"], gains["xa_pre"]],
                          [F32, F32, BF16], _epi_resid)
    q2 = _mm(h2, w_xq, "nn", BF16, 2048, 1024, 1024, "xa_q")
    mem_n = _norm_fwd(mem, gains["mem"], "norm_mem")
    kv = _mm(mem_n, w_xkv, "nn", BF16, mem.shape[0], 256, 1024, "xa_kv", b_cols=256)
    o2 = _xattn_fwd(q2, kv, "xattn_fwd")
    y2, x2, h3 = _mm_rows(o2, w_xo, "nn", 1024, "xa_out", [x1], [gains["xa_post"], gains["ffn_pre"]],
                          [F32, F32, BF16], _epi_resid)
    w_up, w_down, cw = ffn_weights(h3)
    hid_g, hid_u, conv_g, conv_u, act, loss, dx3, dy3, dg_ffn_post = _ffn_fwd(
        h3, w_up, cw, conv_b, w_down, x2, tgt, gains["ffn_post"], "ffn_fwd")

    dhid_g, dhid_u, dcw_g, dcw_u, dcb_g, dcb_u = _ffn_bwd(dy3, w_down, hid_g, hid_u, conv_g, conv_u, cw, "ffn_bwd")
    d_w_down = _mm(act, dy3, "tn", BF16, 2048, 1024, 1024, "dw_down")
    d_w_up = _mm(h3, [dhid_g, dhid_u], "tn", BF16, 1024, 1024, 2048, "dw_up", out_cols=1024)
    sent = send_ffn_grads(d_w_up, d_w_down, jnp.concatenate([dcw_g, dcw_u], axis=1))
    dh3 = _mm([dhid_g, dhid_u], w_up, "nt", F32, 2048, 1024, 1024, "dh_ffn", b_cols=1024, after=sent)
    dx2, dg_ffn_pre, dy2, dg_xa_post = _norm_bwd(dh3, x2, dx3, gains["ffn_pre"], "norm_bwd_ffn",
                                                 prev=(y2, gains["xa_post"]))
    do2 = _mm(dy2, w_xo, "nt", BF16, 2048, 1024, 1024, "d_xa_out")
    d_w_xo = _mm(o2, dy2, "tn", BF16, 1024, 1024, 1024, "dw_xo")
    dq2, dkv = _xattn_bwd(q2, kv, do2, "xattn_bwd")
    dkv = dkv.astype(BF16)
    dx1, dg_xa_pre, dy1, dg_mix_post = _mm_rows(
        dq2, w_xq, "nt", 1024, "dh_xa", [x1, dx2, y1], [gains["xa_pre"], gains["mix_post"]],
        [F32, "sum", BF16, "sum"], _epi_norm_bwd)
    d_w_xq = _mm(h2, dq2, "tn", BF16, 1024, 1024, 1024, "dw_xq")
    dmem_n = _mm(dkv, w_xkv, "nt", F32, mem.shape[0], 1024, 256, "d_mem", b_cols=256)
    d_w_xkv = _mm(mem_n, dkv, "tn", BF16, 1024, 256, mem.shape[0], "dw_xkv", out_cols=256)
    _, dg_mem = _norm_bwd(dmem_n, mem, jnp.zeros_like(mem), gains["mem"], "norm_bwd_mem")
    dycat = _mm(dy1, w_mix, "nt", BF16, 2048, 1024, 1024, "d_mix_out")
    d_w_mix = _mm(ycat, dy1, "tn", BF16, 1024, 1024, 1024, "dw_mix")
    sent_mix = send_mix_grads(d_w_mix, d_w_xq, d_w_xo, d_w_xkv)
    ad = _fox_do_operand(dycat, ycat, sent_mix, "fox_do_operand")
    dq, dk, dv, qaux, kaux = _fox_bwd(proj, dycat, aqb, ak, ad, "fox_bwd")
    du, d_wbd, d_scale = _pool_bwd(proj, dycat, wbd, scale, "pool_bwd")
    df, db_f = _gate_bwd(qaux, kaux, flog, "gate_bwd")
    dproj = [du, dq, dk, dv, df]
    sent_in = send_in_grad(_dw_in(h1, dproj, "dw_in"))
    grad_x, dg_mix_pre = _mm_rows(dproj, w_in, "nt", None, "dh_mix", [x, dx1], [gains["mix_pre"]],
                                  [F32, "sum"], _epi_norm_bwd, after=sent_in)

    small = dict(
        mix_pre=dg_mix_pre, mix_post=dg_mix_post, mem=dg_mem, xa_pre=dg_xa_pre, xa_post=dg_xa_post,
        ffn_pre=dg_ffn_pre, ffn_post=dg_ffn_post,
        conv_b=jnp.concatenate([dcb_g, dcb_u], axis=1),
        w_pool=jnp.concatenate([d_wbd[64 * g:64 * g + 64, 64 * g:64 * g + 64] for g in range(4)], axis=0),
        pool_scale=d_scale.reshape(4, 64),
        b_forget=db_f[:, :FOX_HEADS],
    )
    return loss, grad_x, small


SMALL_ORDER = ("mix_pre", "mix_post", "mem", "xa_pre", "xa_post", "ffn_pre", "ffn_post", "conv_b",
               "w_pool", "pool_scale", "b_forget")


def kernel(x, mem, norm_mix_pre, norm_mix_post, w_in, b_forget, w_pool, pool_scale, w_mix_out, norm_mem, norm_xa_pre, norm_xa_post, w_xq, w_xkv, w_xo, norm_ffn_pre, norm_ffn_post, w_up, conv_w, conv_b, w_down, loss_target, m_norm_mix_pre, m_norm_mix_post, m_w_in, m_b_forget, m_w_pool, m_pool_scale, m_w_mix_out, m_norm_mem, m_norm_xa_pre, m_norm_xa_post, m_w_xq, m_w_xkv, m_w_xo, m_norm_ffn_pre, m_norm_ffn_post, m_w_up, m_conv_w, m_conv_b, m_w_down, v_norm_mix_pre, v_norm_mix_post, v_w_in, v_b_forget, v_w_pool, v_pool_scale, v_w_mix_out, v_norm_mem, v_norm_xa_pre, v_norm_xa_post, v_w_xq, v_w_xkv, v_w_xo, v_norm_ffn_pre, v_norm_ffn_post, v_w_up, v_conv_w, v_conv_b, v_w_down):
    names = ("norm_mix_pre", "norm_mix_post", "w_in", "b_forget", "w_pool", "pool_scale", "w_mix_out", "norm_mem",
             "norm_xa_pre", "norm_xa_post", "w_xq", "w_xkv", "w_xo", "norm_ffn_pre", "norm_ffn_post", "w_up",
             "conv_w", "conv_b", "w_down")
    w = dict(zip(names, (norm_mix_pre, norm_mix_post, w_in, b_forget, w_pool, pool_scale, w_mix_out, norm_mem,
                         norm_xa_pre, norm_xa_post, w_xq, w_xkv, w_xo, norm_ffn_pre, norm_ffn_post, w_up,
                         conv_w, conv_b, w_down)))
    mo = dict(zip(names, (m_norm_mix_pre, m_norm_mix_post, m_w_in, m_b_forget, m_w_pool, m_pool_scale, m_w_mix_out,
                          m_norm_mem, m_norm_xa_pre, m_norm_xa_post, m_w_xq, m_w_xkv, m_w_xo, m_norm_ffn_pre,
                          m_norm_ffn_post, m_w_up, m_conv_w, m_conv_b, m_w_down)))
    vo = dict(zip(names, (v_norm_mix_pre, v_norm_mix_post, v_w_in, v_b_forget, v_w_pool, v_pool_scale, v_w_mix_out,
                          v_norm_mem, v_norm_xa_pre, v_norm_xa_post, v_w_xq, v_w_xkv, v_w_xo, v_norm_ffn_pre,
                          v_norm_ffn_post, v_w_up, v_conv_w, v_conv_b, v_w_down)))

    big_names = ("w_in", "w_mix_out", "w_xq", "w_xo", "w_xkv", "w_up", "w_down")
    shards = {k: w[k][0].astype(BF16) for k in big_names}
    shards["w_in"] = jnp.pad(shards["w_in"], ((0, 0), (0, D_IN_PAD - shards["w_in"].shape[1])))
    conv_w_sh = jnp.pad(conv_w[0, :, 0, :], ((0, 5), (0, 0)))
    (g_in,) = _all_gather([shards["w_in"]], "gather_w_in")
    mix_srcs = [shards[k] for k in ("w_mix_out", "w_xq", "w_xo", "w_xkv")]
    mix_flight = _exchange_start(mix_srcs, [_own_slot(a) for a in mix_srcs], g_in, True, "gather_mix_start")
    ffn_srcs = [shards["w_up"], shards["w_down"], conv_w_sh]
    ffn_flight = _exchange_start(ffn_srcs, [_own_slot(a) for a in ffn_srcs], mix_flight[4], True, "gather_ffn_start")
    my_slot = 4 * lax.axis_index("x") + 2 * lax.axis_index("y") + lax.axis_index("c")
    own_block = lambda a: _own_slot(lax.dynamic_index_in_dim(a, my_slot, 0, keepdims=False))
    by_rows = lambda a: a.reshape(NDEV, a.shape[0] // NDEV, a.shape[1])
    by_cols = lambda a: a.reshape(a.shape[0], NDEV, a.shape[1] // NDEV).transpose(1, 0, 2)
    grad_flight = {}

    def mix_weights(after):
        g_mix, g_xq, g_xo, g_xkv = _exchange_wait(*mix_flight[:4], after, True, "gather_mix_wait")
        return (g_mix.reshape(D_MODEL, D_MODEL), g_xq.reshape(D_MODEL, D_MODEL), g_xo.reshape(D_MODEL, D_MODEL), g_xkv)

    def ffn_weights(after):
        g_up, g_down, g_cw = _exchange_wait(*ffn_flight[:4], after, True, "gather_ffn_wait")
        return g_up, g_down.reshape(D_FF, D_MODEL), g_cw.transpose(1, 0, 2).reshape(8, 2 * D_FF)

    def send_ffn_grads(d_w_up, d_w_down, d_cw):
        srcs = [d_w_up, by_rows(d_w_down), by_cols(d_cw)]
        grad_flight["ffn"] = _exchange_start(srcs, [own_block(a) for a in srcs], ffn_flight[4], False, "scatter_ffn_start")
        return grad_flight["ffn"][4]

    def send_mix_grads(d_w_mix, d_w_xq, d_w_xo, d_w_xkv):
        srcs = [by_rows(d_w_mix), by_rows(d_w_xq), by_rows(d_w_xo), d_w_xkv]
        grad_flight["mix"] = _exchange_start(srcs, [own_block(a) for a in srcs], ffn_flight[4], False, "scatter_mix_start")
        return grad_flight["mix"][4]

    def send_in_grad(d_w_in):
        srcs = [by_rows(d_w_in)]
        grad_flight["in"] = _exchange_start(srcs, [own_block(a) for a in srcs], ffn_flight[4], False, "scatter_in_start")
        return grad_flight["in"][4]

    gains = dict(mix_pre=norm_mix_pre + ffn_flight[4][0, 0], mix_post=norm_mix_post, mem=norm_mem, xa_pre=norm_xa_pre,
                 xa_post=norm_xa_post, ffn_pre=norm_ffn_pre, ffn_post=norm_ffn_post)
    loss, grad_x, small = _local_step(
        x[0], mem[0], loss_target[0], gains, b_forget, w_pool[0], pool_scale[0], conv_b,
        g_in.reshape(D_MODEL, D_IN_PAD), mix_weights, ffn_weights, send_in_grad, send_mix_grads, send_ffn_grads)

    p_up, p_down, p_cw = _exchange_wait(*grad_flight["ffn"][:4], grad_x, False, "scatter_ffn_wait")
    p_mix, p_xq, p_xo, p_xkv = _exchange_wait(*grad_flight["mix"][:4], grad_x, False, "scatter_mix_wait")
    parts = dict(w_mix_out=p_mix, w_xq=p_xq, w_xo=p_xo, w_xkv=p_xkv, w_up=p_up, w_down=p_down)
    *small_parts, loss_parts = _all_gather_small([small[k] for k in SMALL_ORDER] + [loss], "gather_small_grads")

    res = {k: [a[None] for a in _adamw(p, w[k][0], mo[k][0], vo[k][0], "adamw_" + k)] for k, p in parts.items()}
    pad_cw = lambda a: jnp.pad(a[0, :, 0, :], ((0, 5), (0, 0)))
    res["conv_w"] = [a[:3][None, :, None, :] for a in
                     _adamw(p_cw, pad_cw(conv_w), pad_cw(m_conv_w), pad_cw(v_conv_w), "adamw_conv_w")]
    key_of = dict(mix_pre="norm_mix_pre", mix_post="norm_mix_post", mem="norm_mem", xa_pre="norm_xa_pre",
                  xa_post="norm_xa_post", ffn_pre="norm_ffn_pre", ffn_post="norm_ffn_post", conv_b="conv_b",
                  w_pool="w_pool", pool_scale="pool_scale", b_forget="b_forget")
    flat2d = lambda src: [src[key_of[k]].reshape(small[k].shape) for k in SMALL_ORDER]
    small_out = _adamw_small(small_parts, flat2d(w), flat2d(mo), flat2d(vo), "adamw_small")
    for k, four in zip(SMALL_ORDER, small_out):
        res[key_of[k]] = [a.reshape(w[key_of[k]].shape) for a in four]
    (p_in,) = _exchange_wait(*grad_flight["in"][:4], res["w_up"][1], False, "scatter_in_wait")
    res["w_in"] = [a[None] for a in _adamw(p_in[:, :, :w_in.shape[2]], w["w_in"][0], mo["w_in"][0], vo["w_in"][0],
                                           "adamw_w_in")]

    outs = [jnp.sum(loss_parts[:, 0, 0]), grad_x[None]]
    for idx in range(4):
        outs += [res[k][idx] for k in names]
    return tuple(outs)
```

```python
import functools
import math

import jax
import jax.numpy as jnp
from jax import lax
from jax.experimental import pallas as pl
from jax.experimental.pallas import tpu as pltpu

F32 = jnp.float32
BF16 = jnp.bfloat16

NDEV = 8
D_MODEL = 1024
D_POOL = 256
D_FOX = 768
FOX_HEADS = 12
HEAD_PAIRS = FOX_HEADS // 2
XA_HEADS = 4
XA_DIM = 256
D_FF = 4096
D_IN_PAD = 2688
F_COL = 2560
POOL_HALO = 16
NORM_EPS = 1e-6
NEG = -1e30

ADAM_LR = 0.001
ADAM_B1 = 0.9
ADAM_B2 = 0.999
ADAM_EPS = 1e-08
ADAM_WD = 0.01
ADAM_STEP = 10

TM = 512
TQ = 512
TN_FF = 1024
VMEM_LIMIT = 56 * 1024 * 1024
MESH = pl.DeviceIdType.MESH


def _cp(*sem):
    return pltpu.CompilerParams(dimension_semantics=sem, vmem_limit_bytes=VMEM_LIMIT)


def _dot(a, b, dims):
    return lax.dot_general(a, b, (dims, ((), ())), preferred_element_type=F32)


NN = ((1,), (0,))
NT = ((1,), (1,))
TN = ((0,), (0,))


def _mm(a, b, mode, out_dtype, tm, tn, tk, name, b_cols=None, out_cols=None, after=None):
    a_list = list(a) if isinstance(a, (list, tuple)) else [a]
    b_list = list(b) if isinstance(b, (list, tuple)) else [b]
    assert len(a_list) == 1 or len(b_list) == 1
    if mode == "tn":
        K, M = a_list[0].shape
        assert len(a_list) == 1
        Ns = [x.shape[1] for x in b_list]
        N = sum(Ns)
        assert b_cols is None
    else:
        assert len(b_list) == 1
        M = a_list[0].shape[0]
        Ks = [x.shape[1] for x in a_list]
        K = sum(Ks)
        if b_cols is None:
            N = b_list[0].shape[0] if mode == "nt" else b_list[0].shape[1]
        else:
            N = b_list[0].shape[1] if mode == "nt" else NDEV * b_cols
    assert M % tm == 0 and N % tn == 0 and K % tk == 0, (name, M, N, K)
    grid = (M // tm, N // tn, K // tk)
    nk = grid[2]
    dims = {"nn": NN, "nt": NT, "tn": TN}[mode]

    in_specs = []
    if mode == "tn":
        in_specs.append(pl.BlockSpec((tk, tm), lambda i, j, k: (k, i)))
        if len(b_list) == 1:
            in_specs.append(pl.BlockSpec((tk, tn), lambda i, j, k: (k, j)))
        else:
            nj1 = Ns[0] // tn
            in_specs.append(pl.BlockSpec((tk, tn), lambda i, j, k: (k, jnp.minimum(j, nj1 - 1))))
            in_specs.append(pl.BlockSpec((tk, tn), lambda i, j, k: (k, jnp.maximum(j - nj1, 0))))
    else:
        if len(a_list) == 1:
            in_specs.append(pl.BlockSpec((tm, tk), lambda i, j, k: (i, k)))
        else:
            nk1 = Ks[0] // tk
            in_specs.append(pl.BlockSpec((tm, tk), lambda i, j, k: (i, jnp.minimum(k, nk1 - 1))))
            in_specs.append(pl.BlockSpec((tm, tk), lambda i, j, k: (i, jnp.maximum(k - nk1, 0))))
        if b_cols is None:
            if mode == "nn":
                in_specs.append(pl.BlockSpec((tk, tn), lambda i, j, k: (k, j)))
            else:
                in_specs.append(pl.BlockSpec((tn, tk), lambda i, j, k: (j, k)))
        else:
            if mode == "nn":
                per = b_cols // tn
                in_specs.append(pl.BlockSpec((None, tk, tn), lambda i, j, k: (j // per, k, j % per)))
            else:
                per = b_cols // tk
                in_specs.append(pl.BlockSpec((None, tn, tk), lambda i, j, k: (k // per, j, k % per)))
    if out_cols is None:
        out_spec = pl.BlockSpec((tm, tn), lambda i, j, k: (i, j))
        out_shape = jax.ShapeDtypeStruct((M, N), out_dtype)
    else:
        pero = out_cols // tn
        out_spec = pl.BlockSpec((None, tm, tn), lambda i, j, k: (j // pero, i, j % pero))
        out_shape = jax.ShapeDtypeStruct((NDEV, M, out_cols), out_dtype)

    two_a = len(a_list) == 2
    two_b = len(b_list) == 2
    extra = []
    if after is not None:
        in_specs.append(pl.BlockSpec(memory_space=pl.ANY))
        extra.append(after)

    def body(*refs):
        o_ref, acc_ref = refs[-2], refs[-1]
        j = pl.program_id(1)
        k = pl.program_id(2)

        @pl.when(k == 0)
        def _():
            acc_ref[...] = jnp.zeros_like(acc_ref)

        if two_a:
            a1, a2, b1 = refs[0], refs[1], refs[2]
            nk1_ = Ks[0] // tk

            @pl.when(k < nk1_)
            def _():
                acc_ref[...] += _dot(a1[...], b1[...], dims)

            @pl.when(k >= nk1_)
            def _():
                acc_ref[...] += _dot(a2[...], b1[...], dims)
        elif two_b:
            a1, b1, b2 = refs[0], refs[1], refs[2]
            nj1_ = Ns[0] // tn

            @pl.when(j < nj1_)
            def _():
                acc_ref[...] += _dot(a1[...], b1[...], dims)

            @pl.when(j >= nj1_)
            def _():
                acc_ref[...] += _dot(a1[...], b2[...], dims)
        else:
            acc_ref[...] += _dot(refs[0][...], refs[1][...], dims)

        @pl.when(k == nk - 1)
        def _():
            o_ref[...] = acc_ref[...].astype(o_ref.dtype)

    return pl.pallas_call(
        body, name=name, grid=grid, in_specs=in_specs, out_specs=out_spec, out_shape=out_shape,
        scratch_shapes=[pltpu.VMEM((tm, tn), F32)],
        compiler_params=_cp("parallel", "parallel", "arbitrary"),
    )(*a_list, *b_list, *extra)


def _rstd(x):
    return lax.rsqrt(jnp.mean(x * x, axis=-1, keepdims=True) + NORM_EPS)


def _norm_bwd_rows(dxn, xn, r):
    return r * (dxn - xn * jnp.mean(dxn * xn, axis=-1, keepdims=True))


def _row_spec(tm, d):
    return pl.BlockSpec((tm, d), lambda i: (i, 0))


def _vec_spec(d):
    return pl.BlockSpec((1, d), lambda i: (0, 0))


def _mm_rows(a, b, mode, tk, name, rows, vecs, outs, epilogue, b_cols=None, after=None):
    a_list = list(a) if isinstance(a, (list, tuple)) else [a]
    m = a_list[0].shape[0]
    ks = [x.shape[1] for x in a_list]
    n = D_MODEL
    pieces = tk is None
    nk = 1 if pieces else sum(ks) // tk
    dims = NN if mode == "nn" else NT
    if pieces:
        assert mode == "nt" and b_cols is None
        in_specs = [pl.BlockSpec((TM, kp), lambda i, k: (i, 0)) for kp in ks]
        tk = sum(ks)
    elif len(a_list) == 1:
        in_specs = [pl.BlockSpec((TM, tk), lambda i, k: (i, k))]
    else:
        nk1 = ks[0] // tk
        in_specs = [pl.BlockSpec((TM, tk), lambda i, k: (i, jnp.minimum(k, nk1 - 1))),
                    pl.BlockSpec((TM, tk), lambda i, k: (i, jnp.maximum(k - nk1, 0)))]
    if mode == "nn":
        in_specs.append(pl.BlockSpec((tk, n), lambda i, k: (k, 0)))
    elif b_cols is None:
        in_specs.append(pl.BlockSpec((n, tk), lambda i, k: (0, k)))
    else:
        per = b_cols // tk
        in_specs.append(pl.BlockSpec((None, n, tk), lambda i, k: (k // per, 0, k % per)))
    in_specs += [pl.BlockSpec((TM, n), lambda i, k: (i, 0))] * len(rows)
    in_specs += [pl.BlockSpec((1, n), lambda i, k: (0, 0))] * len(vecs)
    extra = []
    if after is not None:
        in_specs.append(pl.BlockSpec(memory_space=pl.ANY))
        extra.append(after)
    out_specs, out_shape = [], []
    for o in outs:
        if o == "sum":
            out_specs.append(pl.BlockSpec((1, n), lambda i, k: (0, 0)))
            out_shape.append(jax.ShapeDtypeStruct((1, n), F32))
        else:
            out_specs.append(pl.BlockSpec((TM, n), lambda i, k: (i, 0)))
            out_shape.append(jax.ShapeDtypeStruct((m, n), o))
    na, nr, nv = len(a_list), len(rows), len(vecs)

    def body(*refs):
        a_refs, b_ref = refs[:na], refs[na]
        row_refs = refs[na + 1:na + 1 + nr]
        vec_refs = refs[na + 1 + nr:na + 1 + nr + nv]
        out_refs = refs[len(refs) - 1 - len(outs):len(refs) - 1]
        acc_ref = refs[-1]
        i, k = pl.program_id(0), pl.program_id(1)

        @pl.when(k == 0)
        def _():
            acc_ref[...] = jnp.zeros_like(acc_ref)

        if pieces:
            off = 0
            for a_ref in a_refs:
                kp = a_ref.shape[1]
                acc_ref[...] += _dot(a_ref[...], b_ref[:, off:off + kp], dims)
                off += kp
        elif na == 1:
            acc_ref[...] += _dot(a_refs[0][...], b_ref[...], dims)
        else:
            nk1_ = ks[0] // tk

            @pl.when(k < nk1_)
            def _():
                acc_ref[...] += _dot(a_refs[0][...], b_ref[...], dims)

            @pl.when(k >= nk1_)
            def _():
                acc_ref[...] += _dot(a_refs[1][...], b_ref[...], dims)

        @pl.when(k == nk - 1)
        def _():
            vals = epilogue(acc_ref[...], [r[...] for r in row_refs], [v[...] for v in vec_refs])
            for o, ref, val in zip(outs, out_refs, vals):
                if o == "sum":
                    @pl.when(i == 0)
                    def _():
                        ref[...] = val

                    @pl.when(i > 0)
                    def _():
                        ref[...] += val
                else:
                    ref[...] = val.astype(o)

    return pl.pallas_call(
        body, name=name, grid=(m // TM, nk), in_specs=in_specs, out_specs=out_specs, out_shape=out_shape,
        scratch_shapes=[pltpu.VMEM((TM, n), F32)],
        compiler_params=_cp("arbitrary", "arbitrary"),
    )(*a_list, b, *rows, *vecs, *extra)


def _proj_in(x, g, w_in, name):
    s, d = x.shape
    n = w_in.shape[1]

    def body(x_ref, g_ref, w_ref, h_ref, p_ref, f_ref):
        xv = x_ref[...]
        h = (xv * _rstd(xv) * g_ref[...]).astype(BF16)
        h_ref[...] = h
        acc = _dot(h, w_ref[...], NN)
        p_ref[...] = acc.astype(BF16)
        f_ref[...] = acc[:, F_COL:]

    return pl.pallas_call(
        body, name=name, grid=(s // TM,),
        in_specs=[_row_spec(TM, d), _vec_spec(d), pl.BlockSpec((d, n), lambda i: (0, 0))],
        out_specs=[_row_spec(TM, d), _row_spec(TM, n), _row_spec(TM, n - F_COL)],
        out_shape=[jax.ShapeDtypeStruct((s, d), BF16), jax.ShapeDtypeStruct((s, n), BF16),
                   jax.ShapeDtypeStruct((s, n - F_COL), F32)],
        compiler_params=_cp("parallel"),
    )(x, g, w_in)


def _dw_in(h, pieces, name):
    s, d = h.shape
    n = sum(p.shape[1] for p in pieces)
    tk = 1024
    nk = s // tk

    def body(*refs):
        h_ref, piece_refs, o_ref, acc_ref = refs[0], refs[1:-2], refs[-2], refs[-1]
        k = pl.program_id(1)

        @pl.when(k == 0)
        def _():
            acc_ref[...] = jnp.zeros_like(acc_ref)

        off = 0
        for p_ref in piece_refs:
            w = p_ref.shape[1]
            acc_ref[:, off:off + w] += _dot(h_ref[...], p_ref[...], TN)
            off += w

        @pl.when(k == nk - 1)
        def _():
            o_ref[...] = acc_ref[...].astype(BF16)

    return pl.pallas_call(
        body, name=name, grid=(d // TM, nk),
        in_specs=[pl.BlockSpec((tk, TM), lambda i, k: (k, i))] +
                 [pl.BlockSpec((tk, p.shape[1]), lambda i, k: (k, 0)) for p in pieces],
        out_specs=pl.BlockSpec((TM, n), lambda i, k: (i, 0)),
        out_shape=jax.ShapeDtypeStruct((d, n), BF16),
        scratch_shapes=[pltpu.VMEM((TM, n), F32)],
        compiler_params=_cp("parallel", "arbitrary"),
    )(h, *pieces)


def _epi_resid(y, rows, vecs):
    (x_in,), (g_post, g_next) = rows, vecs
    xo = x_in + y * _rstd(y) * g_post
    return y, xo, xo * _rstd(xo) * g_next


def _epi_norm_bwd(dh, rows, vecs):
    x, dx_res = rows[0], rows[1]
    r = _rstd(x)
    xn = x * r
    dx = dx_res + _norm_bwd_rows(dh * vecs[0], xn, r)
    res = [dx, jnp.sum(dh * xn, axis=0, keepdims=True)]
    if len(rows) == 3:
        y = rows[2]
        r2 = _rstd(y)
        yn = y * r2
        res += [_norm_bwd_rows(dx * vecs[1], yn, r2), jnp.sum(dx * yn, axis=0, keepdims=True)]
    return res


def _norm_fwd(x, g, name):
    s, d = x.shape
    tm = min(TM, s)

    def body(x_ref, g_ref, h_ref):
        xv = x_ref[...]
        h_ref[...] = (xv * _rstd(xv) * g_ref[...]).astype(BF16)

    return pl.pallas_call(
        body, name=name, grid=(s // tm,), in_specs=[_row_spec(tm, d), _vec_spec(d)],
        out_specs=_row_spec(tm, d), out_shape=jax.ShapeDtypeStruct((s, d), BF16),
        compiler_params=_cp("parallel"),
    )(x, g)


def _norm_bwd(dh, x, dx_res, g_pre, name, prev=None):
    s, d = x.shape
    tm = min(TM, s)
    has_prev = prev is not None

    def body(*refs):
        if has_prev:
            dh_ref, x_ref, dr_ref, g_ref, y_ref, gp_ref, dx_ref, dg_ref, dy_ref, dgp_ref = refs
        else:
            dh_ref, x_ref, dr_ref, g_ref, dx_ref, dg_ref = refs
        i = pl.program_id(0)
        xv = x_ref[...]
        r = _rstd(xv)
        xn = xv * r
        dhv = dh_ref[...].astype(F32)
        dx = dr_ref[...] + _norm_bwd_rows(dhv * g_ref[...], xn, r)
        dx_ref[...] = dx
        dg = jnp.sum(dhv * xn, axis=0, keepdims=True)

        @pl.when(i == 0)
        def _():
            dg_ref[...] = dg

        @pl.when(i > 0)
        def _():
            dg_ref[...] += dg

        if has_prev:
            yv = y_ref[...]
            r2 = _rstd(yv)
            yn = yv * r2
            dy_ref[...] = _norm_bwd_rows(dx * gp_ref[...], yn, r2).astype(BF16)
            dgp = jnp.sum(dx * yn, axis=0, keepdims=True)

            @pl.when(i == 0)
            def _():
                dgp_ref[...] = dgp

            @pl.when(i > 0)
            def _():
                dgp_ref[...] += dgp

    in_specs = [_row_spec(tm, d), _row_spec(tm, d), _row_spec(tm, d), _vec_spec(d)]
    out_specs = [_row_spec(tm, d), _vec_spec(d)]
    out_shape = [jax.ShapeDtypeStruct((s, d), F32), jax.ShapeDtypeStruct((1, d), F32)]
    args = [dh, x, dx_res, g_pre]
    if has_prev:
        in_specs += [_row_spec(tm, d), _vec_spec(d)]
        out_specs += [_row_spec(tm, d), _vec_spec(d)]
        out_shape += [jax.ShapeDtypeStruct((s, d), BF16), jax.ShapeDtypeStruct((1, d), F32)]
        args += list(prev)
    return pl.pallas_call(
        body, name=name, grid=(s // tm,), in_specs=in_specs, out_specs=out_specs, out_shape=out_shape,
        compiler_params=_cp("arbitrary"),
    )(*args)


def _split3(v):
    hi = v.astype(BF16)
    r1 = v - hi.astype(F32)
    mid = r1.astype(BF16)
    lo = (r1 - mid.astype(F32)).astype(BF16)
    return hi, mid, lo


def _tri_dot(tri, v):
    hi, mid, lo = _split3(v)
    return _dot(tri, hi, NN) + _dot(tri, mid, NN) + _dot(tri, lo, NN)


def _gate_cumsum(fraw, b_pad, name):
    s = fraw.shape[0]
    width = HEAD_PAIRS * 128

    def body(f_ref, b_ref, flog_ref, aq_ref, ak_ref, carry_ref):
        i = pl.program_id(0)

        @pl.when(i == 0)
        def _():
            carry_ref[...] = jnp.zeros_like(carry_ref)

        flog = f_ref[...] + b_ref[...]
        flog_ref[...] = flog
        lf = jnp.minimum(flog, 0.0) - jnp.log(1.0 + jnp.exp(-jnp.abs(flog)))
        lane = lax.broadcasted_iota(jnp.int32, (1, 128), 1)
        lf = jnp.where(lane < FOX_HEADS, lf, 0.0)
        row = lax.broadcasted_iota(jnp.int32, (TM, TM), 0)
        col = lax.broadcasted_iota(jnp.int32, (TM, TM), 1)
        tri = (row >= col).astype(BF16)
        cum = _tri_dot(tri, lf) + carry_ref[...]
        carry_ref[...] = cum[TM - 1:TM, :]
        aq_ref[...], ak_ref[...] = _fox_operands(cum)

    return pl.pallas_call(
        body, name=name, grid=(s // TM,),
        in_specs=[_row_spec(TM, 128), _vec_spec(128)],
        out_specs=[_row_spec(TM, 128), _row_spec(TM, width), _row_spec(TM, width)],
        out_shape=[jax.ShapeDtypeStruct((s, 128), F32), jax.ShapeDtypeStruct((s, width), BF16),
                   jax.ShapeDtypeStruct((s, width), BF16)],
        scratch_shapes=[pltpu.VMEM((1, 128), F32)],
        compiler_params=_cp("arbitrary"),
    )(fraw, b_pad)


def _gate_bwd(qaux, kaux, flog, name):
    s = flog.shape[0]
    n = s // TM

    def body(qa_ref, ka_ref, fl_ref, dp_ref, db_ref, carry_ref):
        i = pl.program_id(0)

        @pl.when(i == 0)
        def _():
            carry_ref[...] = jnp.zeros_like(carry_ref)

        lane = lax.broadcasted_iota(jnp.int32, (1, 128), 1)
        dcum = jnp.zeros((TM, 128), F32)
        for p in range(HEAD_PAIRS):
            d = qa_ref[p] - pltpu.roll(ka_ref[p], 128 - 3, 1)
            dcum = jnp.where(lane == 2 * p, pltpu.roll(d, 64 + 2 * p, 1),
                             jnp.where(lane == 2 * p + 1, pltpu.roll(d, 2 * p + 1, 1), dcum))
        row = lax.broadcasted_iota(jnp.int32, (TM, TM), 0)
        col = lax.broadcasted_iota(jnp.int32, (TM, TM), 1)
        tri = (row <= col).astype(BF16)
        dlf = _tri_dot(tri, dcum) + carry_ref[...]
        carry_ref[...] = dlf[0:1, :]
        df = jnp.where(lane < FOX_HEADS, dlf / (1.0 + jnp.exp(fl_ref[...])), 0.0)
        dp_ref[...] = df.astype(BF16)
        db = jnp.sum(df, axis=0, keepdims=True)

        @pl.when(i == 0)
        def _():
            db_ref[...] = db

        @pl.when(i > 0)
        def _():
            db_ref[...] += db

    rev = lambda i: (n - 1 - i, 0)
    return pl.pallas_call(
        body, name=name, grid=(n,),
        in_specs=[pl.BlockSpec((HEAD_PAIRS, TM, 128), lambda i: (0, n - 1 - i, 0)),
                  pl.BlockSpec((HEAD_PAIRS, TM, 128), lambda i: (0, n - 1 - i, 0)), pl.BlockSpec((TM, 128), rev)],
        out_specs=[pl.BlockSpec((TM, 128), rev), _vec_spec(128)],
        out_shape=[jax.ShapeDtypeStruct((s, 128), BF16), jax.ShapeDtypeStruct((1, 128), F32)],
        scratch_shapes=[pltpu.VMEM((1, 128), F32)],
        compiler_params=_cp("arbitrary"),
    )(qaux, kaux, flog)


def _pool_consts(i, rows):
    lane = lax.broadcasted_iota(jnp.int32, (rows, D_POOL), 1)
    t1 = lax.broadcasted_iota(jnp.int32, (rows, D_POOL), 0) + i * TM + 1
    win = jnp.where(lane < 64, 2, jnp.where(lane < 128, 4, jnp.where(lane < 192, 8, 16)))
    inv = 1.0 / jnp.minimum(t1, win).astype(F32)
    return lane, inv


def _by_group(lane, s2, s4, s8, s16):
    return jnp.where(lane < 64, s2, jnp.where(lane < 128, s4, jnp.where(lane < 192, s8, s16)))


def _pool_diff(i, u_ref, halo_ref):
    u = u_ref[...].astype(F32)
    halo = jnp.where(i > 0, halo_ref[...].astype(F32), 0.0)
    ext = jnp.concatenate([halo, u], axis=0)
    s2 = ext + pltpu.roll(ext, 1, 0)
    s4 = s2 + pltpu.roll(s2, 2, 0)
    s8 = s4 + pltpu.roll(s4, 4, 0)
    s16 = s8 + pltpu.roll(s8, 8, 0)
    lane, inv = _pool_consts(i, TM)
    sel = _by_group(lane, s2[POOL_HALO:], s4[POOL_HALO:], s8[POOL_HALO:], s16[POOL_HALO:])
    return sel * inv - u


def _pool_fwd(proj, wbd, scale, ycat, name):
    s = proj.shape[0]
    hb = TM // POOL_HALO

    def body(u_ref, halo_ref, w_ref, sc_ref, y_any, y_ref):
        del y_any
        i = pl.program_id(0)
        diff = _pool_diff(i, u_ref, halo_ref)
        mixed = _dot(diff.astype(BF16), w_ref[...], NN)
        y_ref[...] = (mixed * sc_ref[...]).astype(BF16)

    return pl.pallas_call(
        body, name=name, grid=(s // TM,),
        in_specs=[pl.BlockSpec((TM, D_POOL), lambda i: (i, 0)),
                  pl.BlockSpec((POOL_HALO, D_POOL), lambda i: (jnp.maximum(i * hb - 1, 0), 0)),
                  pl.BlockSpec((D_POOL, D_POOL), lambda i: (0, 0)), _vec_spec(D_POOL),
                  pl.BlockSpec(memory_space=pl.ANY)],
        out_specs=pl.BlockSpec((TM, D_POOL), lambda i: (i, 0)),
        out_shape=jax.ShapeDtypeStruct(ycat.shape, ycat.dtype),
        input_output_aliases={4: 0},
        compiler_params=_cp("parallel"),
    )(proj, proj, wbd, scale, ycat)


def _pool_bwd(proj, dycat, wbd, scale, name):
    s = proj.shape[0]
    n = s // TM
    hb = TM // POOL_HALO
    last_halo = s // POOL_HALO - 1

    def body(u_ref, halo_ref, dy_ref, dyp_ref, w_ref, sc_ref, dp_ref, dw_ref, dsc_ref):
        i = pl.program_id(0)
        diff = _pool_diff(i, u_ref, halo_ref)
        diff_b = diff.astype(BF16)
        mixed = _dot(diff_b, w_ref[...], NN)
        dy = dy_ref[...].astype(F32)
        dmix = (dy * sc_ref[...]).astype(BF16)
        dyp = jnp.where(i < n - 1, dyp_ref[...].astype(F32), 0.0)
        dmix_p = (dyp * sc_ref[...]).astype(BF16)
        dd = _dot(dmix, w_ref[...], NT)
        dd_p = _dot(dmix_p, w_ref[...], NT)
        lane, inv = _pool_consts(i, TM)
        _, inv_p = _pool_consts(i + 1, POOL_HALO)
        ext = jnp.concatenate([dd * inv, dd_p * inv_p], axis=0)
        rows = TM + POOL_HALO
        l2 = ext + pltpu.roll(ext, rows - 1, 0)
        l4 = l2 + pltpu.roll(l2, rows - 2, 0)
        l8 = l4 + pltpu.roll(l4, rows - 4, 0)
        l16 = l8 + pltpu.roll(l8, rows - 8, 0)
        du = _by_group(lane, l2[:TM], l4[:TM], l8[:TM], l16[:TM]) - dd
        dp_ref[...] = du.astype(BF16)
        dw = _dot(diff_b, dmix, TN)
        dsc = jnp.sum(dy * mixed, axis=0, keepdims=True)

        @pl.when(i == 0)
        def _():
            dw_ref[...] = dw
            dsc_ref[...] = dsc

        @pl.when(i > 0)
        def _():
            dw_ref[...] += dw
            dsc_ref[...] += dsc

    return pl.pallas_call(
        body, name=name, grid=(n,),
        in_specs=[pl.BlockSpec((TM, D_POOL), lambda i: (i, 0)),
                  pl.BlockSpec((POOL_HALO, D_POOL), lambda i: (jnp.maximum(i * hb - 1, 0), 0)),
                  pl.BlockSpec((TM, D_POOL), lambda i: (i, 0)),
                  pl.BlockSpec((POOL_HALO, D_POOL), lambda i: (jnp.minimum((i + 1) * hb, last_halo), 0)),
                  pl.BlockSpec((D_POOL, D_POOL), lambda i: (0, 0)), _vec_spec(D_POOL)],
        out_specs=[pl.BlockSpec((TM, D_POOL), lambda i: (i, 0)),
                   pl.BlockSpec((D_POOL, D_POOL), lambda i: (0, 0)), _vec_spec(D_POOL)],
        out_shape=[jax.ShapeDtypeStruct((s, D_POOL), BF16),
                   jax.ShapeDtypeStruct((D_POOL, D_POOL), F32), jax.ShapeDtypeStruct((1, D_POOL), F32)],
        compiler_params=_cp("arbitrary"),
    )(proj, proj, dycat, dycat, wbd, scale)


Q_BLK = D_POOL // 128
K_BLK = Q_BLK + D_FOX // 128
V_BLK = K_BLK + D_FOX // 128


def _operand_rows(v0, v1, ones_off):
    row = lax.broadcasted_iota(jnp.int32, (128, 1), 0)
    half = row & 63
    out = jnp.where(jnp.logical_and(half >= ones_off, half < ones_off + 3), 1.0, 0.0) + jnp.zeros_like(v0)
    for base, v in ((64, v0), (0, v1)):
        for j, piece in enumerate(_split3(v)):
            out = jnp.where(row == base + j, piece.astype(F32), out)
    return out


def _fox_operands(cum):
    width = HEAD_PAIRS * 128
    hi, mid, lo = _split3(cum)
    packed = (hi.astype(F32) + pltpu.roll(mid.astype(F32), 16, 1) + pltpu.roll(lo.astype(F32), 32, 1)).astype(BF16)
    row = lax.broadcasted_iota(jnp.int32, (128, width), 0)
    col = lax.broadcasted_iota(jnp.int32, (128, width), 1)
    head, j = row & 15, row >> 4
    base = (head >> 1) * 128 + (1 - (head & 1)) * 64
    used = jnp.logical_and(head < FOX_HEADS, j < 3)
    half = lax.broadcasted_iota(jnp.int32, (1, width), 1) & 63
    res = []
    for off, sign, ones_off in ((0, 1.0, 3), (3, -1.0, 0)):
        ones = jnp.where(jnp.logical_and(half >= ones_off, half < ones_off + 3), 1.0, 0.0)
        sel = jnp.where(jnp.logical_and(col == base + off + j, used), sign, 0.0).astype(BF16)
        res.append((ones + _dot(packed, sel, NN)).astype(BF16))
    return res


def _fox_do_operand(dycat, ycat, after, name):
    s = dycat.shape[0]

    rb = 1024
    nblk = D_FOX // D_POOL

    def body(*refs):
        do_refs, o_refs, ad_ref = refs[:nblk], refs[nblk:2 * nblk], refs[-1]
        src = lax.broadcasted_iota(jnp.int32, (D_POOL, D_POOL), 0)
        dst = lax.broadcasted_iota(jnp.int32, (D_POOL, D_POOL), 1)
        same_pair = (src >> 7) == (dst >> 7)
        s_in, d_in = src & 127, dst & 127
        hit = jnp.logical_and(same_pair, jnp.logical_or(
            jnp.logical_and(s_in < 64, jnp.logical_and(d_in >= 64, d_in < 67)), jnp.logical_and(s_in >= 64, d_in < 3)))
        sel = jnp.where(hit, 1.0, 0.0).astype(BF16)
        j = lax.broadcasted_iota(jnp.int32, (1, D_POOL), 1) & 63
        for b in range(nblk):
            dd = do_refs[b][...].astype(F32) * o_refs[b][...].astype(F32)
            dsum = jnp.zeros(dd.shape, F32)
            for piece in _split3(dd):
                dsum = dsum + _dot(piece, sel, NN)
            hi, mid, lo = _split3(-dsum)
            ad_ref[:, D_POOL * b:D_POOL * (b + 1)] = jnp.where(j == 0, hi, jnp.where(j == 1, mid, lo))

    blks = [pl.BlockSpec((rb, D_POOL), functools.partial(lambda i, b: (i, 1 + b), b=b)) for b in range(nblk)]
    return pl.pallas_call(
        body, name=name, grid=(s // rb,), in_specs=blks + blks + [pl.BlockSpec(memory_space=pl.ANY)],
        out_specs=pl.BlockSpec((rb, D_FOX), lambda i: (i, 0)),
        out_shape=jax.ShapeDtypeStruct((s, D_FOX), BF16),
        compiler_params=_cp("parallel"),
    )(*[dycat] * nblk, *[ycat] * nblk, after)


def _causal_pairs(nq, key_major):
    if key_major:
        pairs = [(q, k) for k in range(nq) for q in range(k, nq)]
    else:
        pairs = [(q, k) for q in range(nq) for k in range(q + 1)]
    return (jnp.asarray([p[0] for p in pairs], jnp.int32), jnp.asarray([p[1] for p in pairs], jnp.int32))


def _fox_fwd(proj, aq, ak, name):
    s = proj.shape[0]
    nq = s // TQ
    qi_arr, ki_arr = _causal_pairs(nq, key_major=False)

    def body(qi_ref, ki_ref, q_ref, k_ref, v_ref, aq_ref, ak_ref, o_ref, aqb_ref, m0_ref, m1_ref, acc_ref, aux_ref):
        t = pl.program_id(1)
        qi, ki = qi_ref[t], ki_ref[t]
        lane = lax.broadcasted_iota(jnp.int32, (1, 128), 1)
        masks = [lane < 64, lane >= 64]
        ones_v = jnp.where((lane & 63) == 8, 1.0, 0.0).astype(BF16)
        top = lax.broadcasted_iota(jnp.int32, (128, 1), 0) < 64
        m_ref = [m0_ref, m1_ref]

        @pl.when(ki == 0)
        def _():
            m0_ref[...] = jnp.full_like(m0_ref, NEG)
            m1_ref[...] = jnp.full_like(m1_ref, NEG)
            acc_ref[...] = jnp.zeros_like(acc_ref)
            aux_ref[...] = jnp.zeros_like(aux_ref)

        def step(diag):
            q2s = q_ref[...] * 0.125
            k2, v2, aq2, ak2 = k_ref[...], v_ref[...], aq_ref[...], ak_ref[...]
            scs = [_dot(jnp.where(masks[hh], k2, ak2), jnp.where(masks[hh], q2s, aq2), NT) for hh in range(2)]
            ps, alpha = [], []
            for hh in range(2):
                sc = scs[hh]
                if diag:
                    key = lax.broadcasted_iota(jnp.int32, sc.shape, 0)
                    qry = lax.broadcasted_iota(jnp.int32, sc.shape, 1)
                    sc = jnp.where(qry >= key, sc, NEG)
                m_prev = m_ref[hh][...]
                m_new = jnp.maximum(m_prev, jnp.max(sc, axis=0, keepdims=True))
                m_ref[hh][...] = m_new
                alpha.append(jnp.exp(m_prev - m_new))
                ps.append(jnp.exp(sc - m_new).astype(BF16))
            pv = [_dot(jnp.where(masks[hh], v2, ones_v), ps[hh], TN) for hh in range(2)]
            acc_ref[...] = acc_ref[...] * jnp.where(top, alpha[0], alpha[1]) + jnp.where(top, pv[0], pv[1])
            aux_ref[...] = aux_ref[...] * jnp.where(top, alpha[1], alpha[0]) + jnp.where(top, pv[1], pv[0])

        @pl.when(ki < qi)
        def _():
            step(False)

        @pl.when(ki == qi)
        def _():
            step(True)
            aux = aux_ref[...]
            l0, l1 = aux[72:73, :], aux[8:9, :]
            o_ref[...] = (acc_ref[...] * jnp.where(top, 1.0 / l0, 1.0 / l1)).T.astype(BF16)
            aqt = aq_ref[...].astype(F32).T
            cum0 = aqt[64:65, :] + aqt[65:66, :] + aqt[66:67, :]
            cum1 = aqt[0:1, :] + aqt[1:2, :] + aqt[2:3, :]
            aqb = _operand_rows(cum0 - (m0_ref[...] + jnp.log(l0)), cum1 - (m1_ref[...] + jnp.log(l1)), 3)
            aqb_ref[...] = aqb.T.astype(BF16)

    grid_spec = pltpu.PrefetchScalarGridSpec(
        num_scalar_prefetch=2, grid=(HEAD_PAIRS, int(qi_arr.shape[0])),
        in_specs=[pl.BlockSpec((TQ, 128), lambda p, t, qi, ki: (qi[t], Q_BLK + p)),
                  pl.BlockSpec((TQ, 128), lambda p, t, qi, ki: (ki[t], K_BLK + p)),
                  pl.BlockSpec((TQ, 128), lambda p, t, qi, ki: (ki[t], V_BLK + p)),
                  pl.BlockSpec((TQ, 128), lambda p, t, qi, ki: (qi[t], p)),
                  pl.BlockSpec((TQ, 128), lambda p, t, qi, ki: (ki[t], p))],
        out_specs=[pl.BlockSpec((TQ, 128), lambda p, t, qi, ki: (qi[t], Q_BLK + p)),
                   pl.BlockSpec((TQ, 128), lambda p, t, qi, ki: (qi[t], p))],
        scratch_shapes=[pltpu.VMEM((1, TQ), F32), pltpu.VMEM((1, TQ), F32),
                        pltpu.VMEM((128, TQ), F32), pltpu.VMEM((128, TQ), F32)])
    return pl.pallas_call(
        body, name=name, grid_spec=grid_spec,
        out_shape=[jax.ShapeDtypeStruct((s, D_MODEL), BF16), jax.ShapeDtypeStruct((s, HEAD_PAIRS * 128), BF16)],
        compiler_params=_cp("parallel", "arbitrary"),
    )(qi_arr, ki_arr, proj, proj, proj, aq, ak)


def _fox_bwd(proj, dycat, aqb, ak, ad, name):
    s = proj.shape[0]
    nq = s // TQ
    qi_arr, ki_arr = _causal_pairs(nq, key_major=True)

    def body(qi_ref, ki_ref, q_ref, k_ref, v_ref, do_ref, aq_ref, ak_ref, ad_ref,
             dq_ref, dk_ref, dv_ref, qaux_ref, kaux_ref, dq_acc, qaux_acc, dk_acc, dv_acc, kaux_acc):
        t = pl.program_id(1)
        qi, ki = qi_ref[t], ki_ref[t]
        lane = lax.broadcasted_iota(jnp.int32, (1, 128), 1)
        masks = [lane < 64, lane >= 64]
        ones_v = jnp.where((lane & 63) < 3, 1.0, 0.0).astype(BF16)
        top = lax.broadcasted_iota(jnp.int32, (128, 1), 0) < 64

        @pl.when(qi == ki)
        def _():
            dk_acc[...] = jnp.zeros_like(dk_acc)
            dv_acc[...] = jnp.zeros_like(dv_acc)
            kaux_acc[...] = jnp.zeros_like(kaux_acc)

        def step(diag):
            q2s = q_ref[...] * 0.125
            k2, v2, do2 = k_ref[...], v_ref[...], do_ref[...]
            aq2, ak2, ad2 = aq_ref[...], ak_ref[...], ad_ref[...]
            dq, dk, dv = [], [], []
            for hh in range(2):
                qh = jnp.where(masks[hh], q2s, aq2)
                kh = jnp.where(masks[hh], k2, ak2)
                doh = jnp.where(masks[hh], do2, ad2)
                vh = jnp.where(masks[hh], v2, ones_v)
                sc = _dot(kh, qh, NT)
                if diag:
                    key = lax.broadcasted_iota(jnp.int32, sc.shape, 0)
                    qry = lax.broadcasted_iota(jnp.int32, sc.shape, 1)
                    sc = jnp.where(qry >= key, sc, NEG)
                p = jnp.exp(sc)
                dsb = (p * _dot(vh, doh, NT)).astype(BF16)
                dv.append(_dot(p.astype(BF16), doh, NN))
                dk.append(_dot(dsb, qh, NN))
                dq.append(_dot(kh, dsb, TN))
            dk_acc[...] += jnp.where(masks[0], dk[0], dk[1])
            kaux_acc[...] += jnp.where(masks[0], dk[1], dk[0])
            dv_acc[...] += jnp.where(masks[0], dv[0], dv[1])
            dq_new = jnp.where(top, dq[0], dq[1])
            qaux_new = jnp.where(top, dq[1], dq[0])

            @pl.when(ki == 0)
            def _():
                dq_acc[qi] = dq_new
                qaux_acc[qi] = qaux_new

            @pl.when(ki > 0)
            def _():
                dq_acc[qi] += dq_new
                qaux_acc[qi] += qaux_new

        @pl.when(qi > ki)
        def _():
            step(False)

        @pl.when(qi == ki)
        def _():
            step(True)
            rows = pl.ds(pl.multiple_of(qi * TQ, TQ), TQ)
            dq_ref[rows, :] = (dq_acc[qi] * 0.125).T.astype(BF16)
            qaux_ref[rows, :] = qaux_acc[qi].T

        @pl.when(qi == nq - 1)
        def _():
            dk_ref[...] = dk_acc[...].astype(BF16)
            dv_ref[...] = dv_acc[...].astype(BF16)
            kaux_ref[...] = kaux_acc[...]

    grid_spec = pltpu.PrefetchScalarGridSpec(
        num_scalar_prefetch=2, grid=(HEAD_PAIRS, int(qi_arr.shape[0])),
        in_specs=[pl.BlockSpec((TQ, 128), lambda p, t, qi, ki: (qi[t], Q_BLK + p)),
                  pl.BlockSpec((TQ, 128), lambda p, t, qi, ki: (ki[t], K_BLK + p)),
                  pl.BlockSpec((TQ, 128), lambda p, t, qi, ki: (ki[t], V_BLK + p)),
                  pl.BlockSpec((TQ, 128), lambda p, t, qi, ki: (qi[t], Q_BLK + p)),
                  pl.BlockSpec((TQ, 128), lambda p, t, qi, ki: (qi[t], p)),
                  pl.BlockSpec((TQ, 128), lambda p, t, qi, ki: (ki[t], p)),
                  pl.BlockSpec((TQ, 128), lambda p, t, qi, ki: (qi[t], p))],
        out_specs=[pl.BlockSpec((s, 128), lambda p, t, qi, ki: (0, p)),
                   pl.BlockSpec((TQ, 128), lambda p, t, qi, ki: (ki[t], p)),
                   pl.BlockSpec((TQ, 128), lambda p, t, qi, ki: (ki[t], p)),
                   pl.BlockSpec((None, s, 128), lambda p, t, qi, ki: (p, 0, 0)),
                   pl.BlockSpec((None, TQ, 128), lambda p, t, qi, ki: (p, ki[t], 0))],
        scratch_shapes=[pltpu.VMEM((nq, 128, TQ), F32), pltpu.VMEM((nq, 128, TQ), F32),
                        pltpu.VMEM((TQ, 128), F32), pltpu.VMEM((TQ, 128), F32), pltpu.VMEM((TQ, 128), F32)])
    return pl.pallas_call(
        body, name=name, grid_spec=grid_spec,
        out_shape=[jax.ShapeDtypeStruct((s, D_FOX), BF16)] * 3 + [jax.ShapeDtypeStruct((HEAD_PAIRS, s, 128), F32)] * 2,
        compiler_params=_cp("arbitrary", "arbitrary"),
    )(qi_arr, ki_arr, proj, proj, proj, dycat, aqb, ak, ad)


XA_SCALE = XA_DIM ** -0.5


def _xattn_fwd(q2, kv, name):
    s = q2.shape[0]
    m = kv.shape[0]

    def body(q_ref, kv_ref, o_ref):
        heads = [slice(h * XA_DIM, (h + 1) * XA_DIM) for h in range(XA_HEADS)]
        scs = [_dot(q_ref[:, cols], kv_ref[:, cols], NT) for cols in heads]
        for h in range(XA_HEADS):
            c0 = h * XA_DIM
            sc = scs[h] * XA_SCALE
            e = jnp.exp(sc - jnp.max(sc, axis=1, keepdims=True))
            p = e / jnp.sum(e, axis=1, keepdims=True)
            o_ref[:, c0:c0 + XA_DIM] = _dot(p.astype(BF16), kv_ref[:, D_MODEL + c0:D_MODEL + c0 + XA_DIM], NN).astype(BF16)

    return pl.pallas_call(
        body, name=name, grid=(s // TM,),
        in_specs=[_row_spec(TM, D_MODEL), pl.BlockSpec((m, 2 * D_MODEL), lambda i: (0, 0))],
        out_specs=_row_spec(TM, D_MODEL), out_shape=jax.ShapeDtypeStruct((s, D_MODEL), BF16),
        compiler_params=_cp("parallel"),
    )(q2, kv)


def _xattn_bwd(q2, kv, do2, name):
    s = q2.shape[0]
    m = kv.shape[0]

    def body(q_ref, kv_ref, do_ref, dq_ref, dkv_ref):
        i = pl.program_id(0)

        @pl.when(i == 0)
        def _():
            dkv_ref[...] = jnp.zeros_like(dkv_ref)

        heads = [slice(h * XA_DIM, (h + 1) * XA_DIM) for h in range(XA_HEADS)]
        scs = [_dot(kv_ref[:, cols], q_ref[:, cols], NT) for cols in heads]
        dps = [_dot(kv_ref[:, D_MODEL + cols.start:D_MODEL + cols.stop], do_ref[:, cols], NT) for cols in heads]
        for h in range(XA_HEADS):
            c0 = h * XA_DIM
            v0 = D_MODEL + c0
            qh = q_ref[:, c0:c0 + XA_DIM]
            kh = kv_ref[:, c0:c0 + XA_DIM]
            doh = do_ref[:, c0:c0 + XA_DIM]
            sc = scs[h] * XA_SCALE
            e = jnp.exp(sc - jnp.max(sc, axis=0, keepdims=True))
            p = e / jnp.sum(e, axis=0, keepdims=True)
            dp = dps[h]
            ds = p * (dp - jnp.sum(p * dp, axis=0, keepdims=True))
            dsb = (ds * XA_SCALE).astype(BF16)
            dq_ref[:, c0:c0 + XA_DIM] = _dot(kh, dsb, TN).T.astype(BF16)
            dkv_ref[:, c0:c0 + XA_DIM] += _dot(dsb, qh, NN)
            dkv_ref[:, v0:v0 + XA_DIM] += _dot(p.astype(BF16), doh, NN)

    return pl.pallas_call(
        body, name=name, grid=(s // TM,),
        in_specs=[_row_spec(TM, D_MODEL), pl.BlockSpec((m, 2 * D_MODEL), lambda i: (0, 0)), _row_spec(TM, D_MODEL)],
        out_specs=[_row_spec(TM, D_MODEL), pl.BlockSpec((m, 2 * D_MODEL), lambda i: (0, 0))],
        out_shape=[jax.ShapeDtypeStruct((s, D_MODEL), BF16), jax.ShapeDtypeStruct((m, 2 * D_MODEL), F32)],
        compiler_params=_cp("arbitrary"),
    )(q2, kv, do2)


GELU_C = math.sqrt(2.0 / math.pi)
GELU_A = 0.044715


def _gelu(x):
    return (0.5 * x) * (1.0 + jnp.tanh(x * (GELU_C * GELU_A * (x * x) + GELU_C)))


def _gelu_and_grad(x):
    x2 = x * x
    s = 1.0 + jnp.tanh(x * (GELU_C * GELU_A * x2 + GELU_C))
    hx = 0.5 * x
    return hx * s, s * (0.5 + hx * (2.0 - s) * (3.0 * GELU_C * GELU_A * x2 + GELU_C))


def _conv(h, s1, s2, w_ref, b_ref):
    return w_ref[0:1, :] * s2 + w_ref[1:2, :] * s1 + w_ref[2:3, :] * h + b_ref[...]


def _shift_down(main, prev8):
    row = lax.broadcasted_iota(jnp.int32, main.shape, 0)
    s1 = jnp.where(row == 0, prev8[7:8, :], pltpu.roll(main, 1, 0))
    s2 = jnp.where(row == 0, prev8[6:7, :], jnp.where(row == 1, prev8[7:8, :], pltpu.roll(main, 2, 0)))
    return s1, s2


def _shift_up(main, next8):
    n = main.shape[0]
    row = lax.broadcasted_iota(jnp.int32, main.shape, 0)
    u1 = jnp.where(row == n - 1, next8[0:1, :], pltpu.roll(main, n - 1, 0))
    u2 = jnp.where(row == n - 2, next8[0:1, :], jnp.where(row == n - 1, next8[1:2, :], pltpu.roll(main, n - 2, 0)))
    return u1, u2


def _ffn_fwd(h3, w_up, cw, cb, w_down, x2, tgt, g_post, name):
    s = h3.shape[0]
    tn = TN_FF
    nj = D_FF // tn
    per = D_MODEL // tn
    hb = TM // 8

    def body(h_ref, halo_ref, wg_ref, wu_ref, cwg_ref, cwu_ref, cbg_ref, cbu_ref, wd_ref, x_ref, t_ref, g_ref,
             hg_ref, hu_ref, cg_ref, cu_ref, a_ref, loss_ref, dx_ref, dy_ref, dg_ref, y_acc):
        i, j = pl.program_id(0), pl.program_id(1)
        h = h_ref[...]
        halo = halo_ref[...]
        halo = jnp.where(i > 0, halo, jnp.zeros_like(halo))
        conv = []
        for w_ref, cw_ref, cb_ref, hid_ref, c_ref in ((wg_ref, cwg_ref, cbg_ref, hg_ref, cg_ref),
                                                      (wu_ref, cwu_ref, cbu_ref, hu_ref, cu_ref)):
            hm = _dot(h, w_ref[...], NN)
            hid_ref[...] = hm.astype(BF16)
            s1, s2 = _shift_down(hm, _dot(halo, w_ref[...], NN))
            c = _conv(hm, s1, s2, cw_ref, cb_ref)
            c_ref[...] = c.astype(BF16)
            conv.append(c)
        a = (_gelu(conv[0]) * conv[1]).astype(BF16)
        a_ref[...] = a
        contrib = _dot(a, wd_ref[...], NN)

        @pl.when(j == 0)
        def _():
            y_acc[...] = contrib

        @pl.when(j > 0)
        def _():
            y_acc[...] += contrib

        @pl.when(j == nj - 1)
        def _():
            yv = y_acc[...]
            r = _rstd(yv)
            yn = yv * r
            e = x_ref[...] + yn * g_ref[...] - t_ref[...]
            part = 0.5 * jnp.sum(jnp.mean(e * e, axis=-1, keepdims=True), axis=0, keepdims=True)
            part = jnp.broadcast_to(part, (1, 128))
            dx = e * (1.0 / D_MODEL)
            dx_ref[...] = dx
            dy_ref[...] = _norm_bwd_rows(dx * g_ref[...], yn, r).astype(BF16)
            dg = jnp.sum(dx * yn, axis=0, keepdims=True)

            @pl.when(i == 0)
            def _():
                dg_ref[...] = dg
                loss_ref[...] = part

            @pl.when(i > 0)
            def _():
                dg_ref[...] += dg
                loss_ref[...] += part

    rows = pl.BlockSpec((TM, D_MODEL), lambda i, j: (i, 0))
    tile = pl.BlockSpec((TM, tn), lambda i, j: (i, j))
    wide = jax.ShapeDtypeStruct((s, D_FF), BF16)
    return pl.pallas_call(
        body, name=name, grid=(s // TM, nj),
        in_specs=[rows,
                  pl.BlockSpec((8, D_MODEL), lambda i, j: (jnp.maximum(i * hb - 1, 0), 0)),
                  pl.BlockSpec((None, D_MODEL, tn), lambda i, j: (j // per, 0, j % per)),
                  pl.BlockSpec((None, D_MODEL, tn), lambda i, j: (NDEV // 2 + j // per, 0, j % per)),
                  pl.BlockSpec((8, tn), lambda i, j: (0, j)),
                  pl.BlockSpec((8, tn), lambda i, j: (0, nj + j)),
                  pl.BlockSpec((1, tn), lambda i, j: (0, j)),
                  pl.BlockSpec((1, tn), lambda i, j: (0, nj + j)),
                  pl.BlockSpec((tn, D_MODEL), lambda i, j: (j, 0)),
                  rows, rows, pl.BlockSpec((1, D_MODEL), lambda i, j: (0, 0))],
        out_specs=[tile, tile, tile, tile, tile,
                   pl.BlockSpec((1, 128), lambda i, j: (0, 0)), rows, rows,
                   pl.BlockSpec((1, D_MODEL), lambda i, j: (0, 0))],
        out_shape=[wide, wide, wide, wide, wide,
                   jax.ShapeDtypeStruct((1, 128), F32), jax.ShapeDtypeStruct((s, D_MODEL), F32),
                   jax.ShapeDtypeStruct((s, D_MODEL), BF16), jax.ShapeDtypeStruct((1, D_MODEL), F32)],
        scratch_shapes=[pltpu.VMEM((TM, D_MODEL), F32)],
        compiler_params=_cp("arbitrary", "arbitrary"),
    )(h3, h3, w_up, w_up, cw, cw, cb, cb, w_down, x2, tgt, g_post)


def _ffn_bwd(dy3, w_down, hid_g, hid_u, conv_g, conv_u, cw, name):
    s = dy3.shape[0]
    n = s // TM
    tn = TN_FF
    nj = D_FF // tn
    hb = TM // 8
    last8 = s // 8 - 1

    def body(dy_ref, dyn_ref, wd_ref, hg_ref, hu_ref, cg_ref, cgn_ref, cu_ref, cun_ref, cwg_ref, cwu_ref,
             dhg_ref, dhu_ref, dcwg_ref, dcwu_ref, dcbg_ref, dcbu_ref):
        i = pl.program_id(1)
        first, last = i == 0, i == n - 1
        da = _dot(dy_ref[...], wd_ref[...], NT)
        dyn = dyn_ref[...]
        dyn = jnp.where(last, jnp.zeros_like(dyn), dyn)
        da_n = _dot(dyn, wd_ref[...], NT)
        c_g, c_u = cg_ref[...].astype(F32), cu_ref[...].astype(F32)
        g, dg = _gelu_and_grad(c_g)
        gn, dgn = _gelu_and_grad(cgn_ref[...].astype(F32))
        outs = ((da * c_u * dg, da_n * cun_ref[...].astype(F32) * dgn, hg_ref, cwg_ref, dhg_ref, dcwg_ref, dcbg_ref),
                (da * g, da_n * gn, hu_ref, cwu_ref, dhu_ref, dcwu_ref, dcbu_ref))
        row8 = lax.broadcasted_iota(jnp.int32, (8, tn), 0)
        for dc, dcn, h_ref, cw_ref, dh_ref, dcw_ref, dcb_ref in outs:
            u1, u2 = _shift_up(dc, dcn)
            dh_ref[...] = (cw_ref[2:3, :] * dc + cw_ref[1:2, :] * u1 + cw_ref[0:1, :] * u2).astype(BF16)
            hm = h_ref[...].astype(F32)
            dcb = jnp.sum(dc, axis=0, keepdims=True)
            dcw = jnp.where(row8 == 0, jnp.sum(hm * u2, axis=0, keepdims=True),
                            jnp.where(row8 == 1, jnp.sum(hm * u1, axis=0, keepdims=True),
                                      jnp.where(row8 == 2, jnp.sum(hm * dc, axis=0, keepdims=True), 0.0)))

            @pl.when(first)
            def _():
                dcw_ref[...] = dcw
                dcb_ref[...] = dcb

            @pl.when(i > 0)
            def _():
                dcw_ref[...] += dcw
                dcb_ref[...] += dcb

    next8 = lambda j, i: (jnp.minimum((i + 1) * hb, last8), j)
    blk = lambda j, i: (i, j)
    col = lambda j, i: (0, j)
    colu = lambda j, i: (0, nj + j)
    tile = pl.BlockSpec((TM, tn), blk)
    return pl.pallas_call(
        body, name=name, grid=(nj, n),
        in_specs=[pl.BlockSpec((TM, D_MODEL), lambda j, i: (i, 0)),
                  pl.BlockSpec((8, D_MODEL), lambda j, i: (jnp.minimum((i + 1) * hb, last8), 0)),
                  pl.BlockSpec((tn, D_MODEL), lambda j, i: (j, 0)),
                  tile, tile, tile, pl.BlockSpec((8, tn), next8), tile, pl.BlockSpec((8, tn), next8),
                  pl.BlockSpec((8, tn), col), pl.BlockSpec((8, tn), colu)],
        out_specs=[tile, tile, pl.BlockSpec((8, tn), col), pl.BlockSpec((8, tn), col),
                   pl.BlockSpec((1, tn), col), pl.BlockSpec((1, tn), col)],
        out_shape=[jax.ShapeDtypeStruct((s, D_FF), BF16), jax.ShapeDtypeStruct((s, D_FF), BF16),
                   jax.ShapeDtypeStruct((8, D_FF), F32), jax.ShapeDtypeStruct((8, D_FF), F32),
                   jax.ShapeDtypeStruct((1, D_FF), F32), jax.ShapeDtypeStruct((1, D_FF), F32)],
        compiler_params=_cp("parallel", "arbitrary"),
    )(dy3, dy3, w_down, hid_g, hid_u, conv_g, conv_g, conv_u, conv_u, cw, cw)


def _slot(p):
    return 4 * p[0] + 2 * p[1] + p[2]


def _all_gather(shards, name):
    n = len(shards)

    def body(*refs):
        ins, outs = refs[:n], refs[n:2 * n]
        send_sems, recv_sems, local_sems = refs[2 * n:]
        x, y, c = lax.axis_index("x"), lax.axis_index("y"), lax.axis_index("c")
        me, sibling = (x, y, c), (x, y, 1 - c)
        chips = [(1 - x, y), (x, 1 - y), (1 - x, 1 - y)]

        def copy(a, k, block, to, from_input=False):
            dst = outs[a].at[_slot(block)]
            return pltpu.make_async_remote_copy(
                src_ref=ins[a] if from_input else dst, dst_ref=dst,
                send_sem=send_sems.at[a, k], recv_sem=recv_sems.at[a, k],
                device_id=to, device_id_type=MESH)

        mine = [pltpu.make_async_copy(ins[a], outs[a].at[_slot(me)], local_sems.at[a]) for a in range(n)]
        for cp in mine:
            cp.start()
        first = []
        for a in range(n):
            first.append(copy(a, 0, me, sibling, True))
            first += [copy(a, 1 + j, me, (*chip, c), True) for j, chip in enumerate(chips)]
        for cp in first:
            cp.start()
        passed = []
        for j, chip in enumerate(chips):
            for a in range(n):
                copy(a, 1 + j, (*chip, c), me).wait_recv()
                fwd = copy(a, 4 + j, (*chip, c), sibling)
                fwd.start()
                passed.append(fwd)
        for a in range(n):
            copy(a, 0, sibling, me).wait_recv()
            for j, chip in enumerate(chips):
                copy(a, 4 + j, (*chip, 1 - c), me).wait_recv()
        for cp in first + passed:
            cp.wait_send()
        for cp in mine:
            cp.wait()

    any_spec = pl.BlockSpec(memory_space=pl.ANY)
    return pl.pallas_call(
        body, name=name,
        in_specs=[any_spec] * n, out_specs=[any_spec] * n,
        out_shape=[jax.ShapeDtypeStruct((NDEV,) + s.shape, s.dtype) for s in shards],
        scratch_shapes=[pltpu.SemaphoreType.DMA((n, 7)), pltpu.SemaphoreType.DMA((n, 7)),
                        pltpu.SemaphoreType.DMA((n,))],
    )(*shards)


def _peer_list(x, y, c):
    return [(1 - x if m & 4 else x, 1 - y if m & 2 else y, 1 - c if m & 1 else c) for m in range(1, NDEV)]


def _exchange_copies(src_refs, land_refs, send_sems, recv_sems, gather):
    x, y, c = lax.axis_index("x"), lax.axis_index("y"), lax.axis_index("c")
    me = (x, y, c)
    copies = []
    for m, peer in enumerate(_peer_list(x, y, c)):
        for a in range(len(src_refs)):
            copies.append(pltpu.make_async_remote_copy(
                src_ref=src_refs[a] if gather else src_refs[a].at[_slot(peer)], dst_ref=land_refs[a].at[_slot(me)],
                send_sem=send_sems.at[a * (NDEV - 1) + m], recv_sem=recv_sems.at[a * (NDEV - 1) + m],
                device_id=peer, device_id_type=MESH))
    return copies


def _all_gather_small(shards, name):
    n = len(shards)

    def body(*refs):
        ins, outs = refs[:n], refs[n:2 * n]
        send_sems, recv_sems, local_sems = refs[2 * n:]
        me = (lax.axis_index("x"), lax.axis_index("y"), lax.axis_index("c"))
        mine = [pltpu.make_async_copy(ins[a], outs[a].at[_slot(me)], local_sems.at[a]) for a in range(n)]
        copies = _exchange_copies(ins, outs, send_sems, recv_sems, True)
        for cp in mine + copies:
            cp.start()
        for cp in copies + mine:
            cp.wait()

    any_spec = pl.BlockSpec(memory_space=pl.ANY)
    return pl.pallas_call(
        body, name=name,
        in_specs=[any_spec] * n, out_specs=[any_spec] * n,
        out_shape=[jax.ShapeDtypeStruct((NDEV,) + s.shape, s.dtype) for s in shards],
        scratch_shapes=[pltpu.SemaphoreType.DMA((n * (NDEV - 1),)), pltpu.SemaphoreType.DMA((n * (NDEV - 1),)),
                        pltpu.SemaphoreType.DMA((n,))],
    )(*shards)


def _exchange_start(srcs, lands, after, gather, name):
    n = len(srcs)
    hbm = pl.BlockSpec(memory_space=pltpu.HBM)

    def body(*refs):
        for cp in _exchange_copies(refs[:n], refs[n:2 * n], refs[2 * n + 1], refs[2 * n + 2], gather):
            cp.start()
        token = refs[-1]
        token[...] = jnp.zeros_like(token)

    outs = pl.pallas_call(
        body, name=name,
        out_shape=(pltpu.SemaphoreType.DMA((n * (NDEV - 1),)), pltpu.SemaphoreType.DMA((n * (NDEV - 1),)),
                   *[pltpu.HBM(a.shape, a.dtype) for a in list(srcs) + list(lands)],
                   jax.ShapeDtypeStruct((8, 128), F32)),
        in_specs=[hbm] * (2 * n) + [pl.BlockSpec(memory_space=pl.ANY)],
        out_specs=(pl.BlockSpec(memory_space=pltpu.SEMAPHORE), pl.BlockSpec(memory_space=pltpu.SEMAPHORE),
                   *[hbm] * (2 * n), pl.BlockSpec(memory_space=pltpu.VMEM)),
        input_output_aliases={i: 2 + i for i in range(2 * n)},
        compiler_params=pltpu.CompilerParams(has_side_effects=pltpu.SideEffectType.DATAFLOW_SIDE_EFFECTING),
    )(*[pltpu.with_memory_space_constraint(a, pltpu.HBM) for a in list(srcs) + list(lands)], after)
    return outs[0], outs[1], outs[2:2 + n], outs[2 + n:2 + 2 * n], outs[-1]


def _exchange_wait(send_sems, recv_sems, srcs, lands, after, gather, name):
    n = len(srcs)
    hbm = pl.BlockSpec(memory_space=pltpu.HBM)

    def body(*refs):
        for cp in _exchange_copies(refs[:n], refs[n:2 * n], refs[2 * n], refs[2 * n + 1], gather):
            cp.wait_send()
            cp.wait_recv()

    outs = pl.pallas_call(
        body, name=name,
        out_shape=tuple(pltpu.HBM(a.shape, a.dtype) for a in list(srcs) + list(lands)),
        in_specs=[hbm] * (2 * n) + [pl.BlockSpec(memory_space=pltpu.SEMAPHORE)] * 2 + [pl.BlockSpec(memory_space=pl.ANY)],
        out_specs=tuple([hbm] * (2 * n)),
        input_output_aliases={i: i for i in range(2 * n)},
        compiler_params=pltpu.CompilerParams(has_side_effects=pltpu.SideEffectType.DATAFLOW_SIDE_EFFECTING),
    )(*srcs, *lands, send_sems, recv_sems, after)
    return outs[n:]


def _own_slot(block):
    me = 4 * lax.axis_index("x") + 2 * lax.axis_index("y") + lax.axis_index("c")
    return lax.dynamic_update_slice(lax.empty((NDEV,) + block.shape, block.dtype), block[None], (me, 0, 0))


def _adam_update(p_ref, w_ref, m_ref, v_ref, g_ref, d_ref, mo_ref, vo_ref):
    bc1 = 1.0 - ADAM_B1 ** ADAM_STEP
    bc2 = 1.0 - ADAM_B2 ** ADAM_STEP
    g = p_ref[0].astype(F32)
    for d in range(1, NDEV):
        g = g + p_ref[d].astype(F32)
    g_ref[...] = g
    mn = ADAM_B1 * m_ref[...] + (1.0 - ADAM_B1) * g
    vn = ADAM_B2 * v_ref[...] + (1.0 - ADAM_B2) * (g * g)
    mo_ref[...] = mn
    vo_ref[...] = vn
    d_ref[...] = -ADAM_LR * ((mn / bc1) / (jnp.sqrt(vn / bc2) + ADAM_EPS) + ADAM_WD * w_ref[...])


def _adamw_small(parts, ws, ms, vs, name):
    n = len(ws)

    def body(*refs):
        ins, outs = refs[:4 * n], refs[4 * n:]
        for k in range(n):
            _adam_update(ins[k], ins[n + k], ins[2 * n + k], ins[3 * n + k], *outs[4 * k:4 * k + 4])

    whole = pl.BlockSpec(memory_space=pltpu.VMEM)
    res = pl.pallas_call(
        body, name=name, in_specs=[whole] * (4 * n), out_specs=[whole] * (4 * n),
        out_shape=[jax.ShapeDtypeStruct(a.shape, F32) for a in ws for _ in range(4)],
    )(*parts, *ws, *ms, *vs)
    return [res[4 * k:4 * k + 4] for k in range(n)]


def _adamw(parts, w, m, v, name):
    r, c = w.shape
    tr = r if r * c <= 160 * 1024 else max(8, (160 * 1024 // c) // 8 * 8)
    while r % tr:
        tr -= 8
    body = functools.partial(_adam_update)
    spec = pl.BlockSpec((tr, c), lambda i: (i, 0))
    return pl.pallas_call(
        body, name=name, grid=(r // tr,),
        in_specs=[pl.BlockSpec((NDEV, tr, c), lambda i: (0, i, 0)), spec, spec, spec],
        out_specs=[spec] * 4, out_shape=[jax.ShapeDtypeStruct((r, c), F32)] * 4,
        compiler_params=_cp("parallel"),
    )(parts, w, m, v)


def _local_step(x, mem, tgt, gains, b_forget, w_pool, pool_scale, conv_b, w_in,
                mix_weights, ffn_weights, send_in_grad, send_mix_grads, send_ffn_grads):
    b_pad = jnp.pad(b_forget, ((0, 0), (0, 128 - FOX_HEADS)))
    wbd = jnp.zeros((D_POOL, D_POOL), F32)
    for g in range(4):
        wbd = wbd.at[64 * g:64 * g + 64, 64 * g:64 * g + 64].set(w_pool[g])
    wbd = wbd.astype(BF16)
    scale = pool_scale.reshape(1, D_POOL)

    h1, proj, fraw = _proj_in(x, gains["mix_pre"], w_in, "proj_in")
    flog, aq, ak = _gate_cumsum(fraw, b_pad, "gate_cumsum")
    ycat, aqb = _fox_fwd(proj, aq, ak, "fox_fwd")
    ycat = _pool_fwd(proj, wbd, scale, ycat, "pool_fwd")
    w_mix, w_xq, w_xo, w_xkv = mix_weights(ycat)
    y1, x1, h2 = _mm_rows(ycat, w_mix, "nn", 1024, "mix_out", [x], [gains["mix_post"], gains["xa_pre"]],
                          [F32, F32, BF16], _epi_resid)
    q2 = _mm(h2, w_xq, "nn", BF16, 2048, 1024, 1024, "xa_q")
    mem_n = _norm_fwd(mem, gains["mem"], "norm_mem")
    kv = _mm(mem_n, w_xkv, "nn", BF16, mem.shape[0], 256, 1024, "xa_kv", b_cols=256)
    o2 = _xattn_fwd(q2, kv, "xattn_fwd")
    y2, x2, h3 = _mm_rows(o2, w_xo, "nn", 1024, "xa_out", [x1], [gains["xa_post"], gains["ffn_pre"]],
                          [F32, F32, BF16], _epi_resid)
    w_up, w_down, cw = ffn_weights(h3)
    hid_g, hid_u, conv_g, conv_u, act, loss, dx3, dy3, dg_ffn_post = _ffn_fwd(
        h3, w_up, cw, conv_b, w_down, x2, tgt, gains["ffn_post"], "ffn_fwd")

    dhid_g, dhid_u, dcw_g, dcw_u, dcb_g, dcb_u = _ffn_bwd(dy3, w_down, hid_g, hid_u, conv_g, conv_u, cw, "ffn_bwd")
    d_w_down = _mm(act, dy3, "tn", BF16, 2048, 1024, 1024, "dw_down")
    d_w_up = _mm(h3, [dhid_g, dhid_u], "tn", BF16, 1024, 1024, 2048, "dw_up", out_cols=1024)
    sent = send_ffn_grads(d_w_up, d_w_down, jnp.concatenate([dcw_g, dcw_u], axis=1))
    dh3 = _mm([dhid_g, dhid_u], w_up, "nt", F32, 2048, 1024, 1024, "dh_ffn", b_cols=1024, after=sent)
    dx2, dg_ffn_pre, dy2, dg_xa_post = _norm_bwd(dh3, x2, dx3, gains["ffn_pre"], "norm_bwd_ffn",
                                                 prev=(y2, gains["xa_post"]))
    do2 = _mm(dy2, w_xo, "nt", BF16, 2048, 1024, 1024, "d_xa_out")
    d_w_xo = _mm(o2, dy2, "tn", BF16, 1024, 1024, 1024, "dw_xo")
    dq2, dkv = _xattn_bwd(q2, kv, do2, "xattn_bwd")
    dkv = dkv.astype(BF16)
    dx1, dg_xa_pre, dy1, dg_mix_post = _mm_rows(
        dq2, w_xq, "nt", 1024, "dh_xa", [x1, dx2, y1], [gains["xa_pre"], gains["mix_post"]],
        [F32, "sum", BF16, "sum"], _epi_norm_bwd)
    d_w_xq = _mm(h2, dq2, "tn", BF16, 1024, 1024, 1024, "dw_xq")
    dmem_n = _mm(dkv, w_xkv, "nt", F32, mem.shape[0], 1024, 256, "d_mem", b_cols=256)
    d_w_xkv = _mm(mem_n, dkv, "tn", BF16, 1024, 256, mem.shape[0], "dw_xkv", out_cols=256)
    _, dg_mem = _norm_bwd(dmem_n, mem, jnp.zeros_like(mem), gains["mem"], "norm_bwd_mem")
    dycat = _mm(dy1, w_mix, "nt", BF16, 2048, 1024, 1024, "d_mix_out")
    d_w_mix = _mm(ycat, dy1, "tn", BF16, 1024, 1024, 1024, "dw_mix")
    sent_mix = send_mix_grads(d_w_mix, d_w_xq, d_w_xo, d_w_xkv)
    ad = _fox_do_operand(dycat, ycat, sent_mix, "fox_do_operand")
    dq, dk, dv, qaux, kaux = _fox_bwd(proj, dycat, aqb, ak, ad, "fox_bwd")
    du, d_wbd, d_scale = _pool_bwd(proj, dycat, wbd, scale, "pool_bwd")
    df, db_f = _gate_bwd(qaux, kaux, flog, "gate_bwd")
    dproj = [du, dq, dk, dv, df]
    sent_in = send_in_grad(_dw_in(h1, dproj, "dw_in"))
    grad_x, dg_mix_pre = _mm_rows(dproj, w_in, "nt", None, "dh_mix", [x, dx1], [gains["mix_pre"]],
                                  [F32, "sum"], _epi_norm_bwd, after=sent_in)

    small = dict(
        mix_pre=dg_mix_pre, mix_post=dg_mix_post, mem=dg_mem, xa_pre=dg_xa_pre, xa_post=dg_xa_post,
        ffn_pre=dg_ffn_pre, ffn_post=dg_ffn_post,
        conv_b=jnp.concatenate([dcb_g, dcb_u], axis=1),
        w_pool=jnp.concatenate([d_wbd[64 * g:64 * g + 64, 64 * g:64 * g + 64] for g in range(4)], axis=0),
        pool_scale=d_scale.reshape(4, 64),
        b_forget=db_f[:, :FOX_HEADS],
    )
    return loss, grad_x, small


SMALL_ORDER = ("mix_pre", "mix_post", "mem", "xa_pre", "xa_post", "ffn_pre", "ffn_post", "conv_b",
               "w_pool", "pool_scale", "b_forget")


def kernel(x, mem, norm_mix_pre, norm_mix_post, w_in, b_forget, w_pool, pool_scale, w_mix_out, norm_mem, norm_xa_pre, norm_xa_post, w_xq, w_xkv, w_xo, norm_ffn_pre, norm_ffn_post, w_up, conv_w, conv_b, w_down, loss_target, m_norm_mix_pre, m_norm_mix_post, m_w_in, m_b_forget, m_w_pool, m_pool_scale, m_w_mix_out, m_norm_mem, m_norm_xa_pre, m_norm_xa_post, m_w_xq, m_w_xkv, m_w_xo, m_norm_ffn_pre, m_norm_ffn_post, m_w_up, m_conv_w, m_conv_b, m_w_down, v_norm_mix_pre, v_norm_mix_post, v_w_in, v_b_forget, v_w_pool, v_pool_scale, v_w_mix_out, v_norm_mem, v_norm_xa_pre, v_norm_xa_post, v_w_xq, v_w_xkv, v_w_xo, v_norm_ffn_pre, v_norm_ffn_post, v_w_up, v_conv_w, v_conv_b, v_w_down):
    names = ("norm_mix_pre", "norm_mix_post", "w_in", "b_forget", "w_pool", "pool_scale", "w_mix_out", "norm_mem",
             "norm_xa_pre", "norm_xa_post", "w_xq", "w_xkv", "w_xo", "norm_ffn_pre", "norm_ffn_post", "w_up",
             "conv_w", "conv_b", "w_down")
    w = dict(zip(names, (norm_mix_pre, norm_mix_post, w_in, b_forget, w_pool, pool_scale, w_mix_out, norm_mem,
                         norm_xa_pre, norm_xa_post, w_xq, w_xkv, w_xo, norm_ffn_pre, norm_ffn_post, w_up,
                         conv_w, conv_b, w_down)))
    mo = dict(zip(names, (m_norm_mix_pre, m_norm_mix_post, m_w_in, m_b_forget, m_w_pool, m_pool_scale, m_w_mix_out,
                          m_norm_mem, m_norm_xa_pre, m_norm_xa_post, m_w_xq, m_w_xkv, m_w_xo, m_norm_ffn_pre,
                          m_norm_ffn_post, m_w_up, m_conv_w, m_conv_b, m_w_down)))
    vo = dict(zip(names, (v_norm_mix_pre, v_norm_mix_post, v_w_in, v_b_forget, v_w_pool, v_pool_scale, v_w_mix_out,
                          v_norm_mem, v_norm_xa_pre, v_norm_xa_post, v_w_xq, v_w_xkv, v_w_xo, v_norm_ffn_pre,
                          v_norm_ffn_post, v_w_up, v_conv_w, v_conv_b, v_w_down)))

    big_names = ("w_in", "w_mix_out", "w_xq", "w_xo", "w_xkv", "w_up", "w_down")
    shards = {k: w[k][0].astype(BF16) for k in big_names}
    shards["w_in"] = jnp.pad(shards["w_in"], ((0, 0), (0, D_IN_PAD - shards["w_in"].shape[1])))
    conv_w_sh = jnp.pad(conv_w[0, :, 0, :], ((0, 5), (0, 0)))
    (g_in,) = _all_gather([shards["w_in"]], "gather_w_in")
    mix_srcs = [shards[k] for k in ("w_mix_out", "w_xq", "w_xo", "w_xkv")]
    mix_flight = _exchange_start(mix_srcs, [_own_slot(a) for a in mix_srcs], g_in, True, "gather_mix_start")
    ffn_srcs = [shards["w_up"], shards["w_down"], conv_w_sh]
    ffn_flight = _exchange_start(ffn_srcs, [_own_slot(a) for a in ffn_srcs], mix_flight[4], True, "gather_ffn_start")
    my_slot = 4 * lax.axis_index("x") + 2 * lax.axis_index("y") + lax.axis_index("c")
    own_block = lambda a: _own_slot(lax.dynamic_index_in_dim(a, my_slot, 0, keepdims=False))
    by_rows = lambda a: a.reshape(NDEV, a.shape[0] // NDEV, a.shape[1])
    by_cols = lambda a: a.reshape(a.shape[0], NDEV, a.shape[1] // NDEV).transpose(1, 0, 2)
    grad_flight = {}

    def mix_weights(after):
        g_mix, g_xq, g_xo, g_xkv = _exchange_wait(*mix_flight[:4], after, True, "gather_mix_wait")
        return (g_mix.reshape(D_MODEL, D_MODEL), g_xq.reshape(D_MODEL, D_MODEL), g_xo.reshape(D_MODEL, D_MODEL), g_xkv)

    def ffn_weights(after):
        g_up, g_down, g_cw = _exchange_wait(*ffn_flight[:4], after, True, "gather_ffn_wait")
        return g_up, g_down.reshape(D_FF, D_MODEL), g_cw.transpose(1, 0, 2).reshape(8, 2 * D_FF)

    def send_ffn_grads(d_w_up, d_w_down, d_cw):
        srcs = [d_w_up, by_rows(d_w_down), by_cols(d_cw)]
        grad_flight["ffn"] = _exchange_start(srcs, [own_block(a) for a in srcs], ffn_flight[4], False, "scatter_ffn_start")
        return grad_flight["ffn"][4]

    def send_mix_grads(d_w_mix, d_w_xq, d_w_xo, d_w_xkv):
        srcs = [by_rows(d_w_mix), by_rows(d_w_xq), by_rows(d_w_xo), d_w_xkv]
        grad_flight["mix"] = _exchange_start(srcs, [own_block(a) for a in srcs], ffn_flight[4], False, "scatter_mix_start")
        return grad_flight["mix"][4]

    def send_in_grad(d_w_in):
        srcs = [by_rows(d_w_in)]
        grad_flight["in"] = _exchange_start(srcs, [own_block(a) for a in srcs], ffn_flight[4], False, "scatter_in_start")
        return grad_flight["in"][4]

    gains = dict(mix_pre=norm_mix_pre + ffn_flight[4][0, 0], mix_post=norm_mix_post, mem=norm_mem, xa_pre=norm_xa_pre,
                 xa_post=norm_xa_post, ffn_pre=norm_ffn_pre, ffn_post=norm_ffn_post)
    loss, grad_x, small = _local_step(
        x[0], mem[0], loss_target[0], gains, b_forget, w_pool[0], pool_scale[0], conv_b,
        g_in.reshape(D_MODEL, D_IN_PAD), mix_weights, ffn_weights, send_in_grad, send_mix_grads, send_ffn_grads)

    p_up, p_down, p_cw = _exchange_wait(*grad_flight["ffn"][:4], grad_x, False, "scatter_ffn_wait")
    p_mix, p_xq, p_xo, p_xkv = _exchange_wait(*grad_flight["mix"][:4], grad_x, False, "scatter_mix_wait")
    parts = dict(w_mix_out=p_mix, w_xq=p_xq, w_xo=p_xo, w_xkv=p_xkv, w_up=p_up, w_down=p_down)
    *small_parts, loss_parts = _all_gather_small([small[k] for k in SMALL_ORDER] + [loss], "gather_small_grads")

    res = {k: [a[None] for a in _adamw(p, w[k][0], mo[k][0], vo[k][0], "adamw_" + k)] for k, p in parts.items()}
    pad_cw = lambda a: jnp.pad(a[0, :, 0, :], ((0, 5), (0, 0)))
    res["conv_w"] = [a[:3][None, :, None, :] for a in
                     _adamw(p_cw, pad_cw(conv_w), pad_cw(m_conv_w), pad_cw(v_conv_w), "adamw_conv_w")]
    key_of = dict(mix_pre="norm_mix_pre", mix_post="norm_mix_post", mem="norm_mem", xa_pre="norm_xa_pre",
                  xa_post="norm_xa_post", ffn_pre="norm_ffn_pre", ffn_post="norm_ffn_post", conv_b="conv_b",
                  w_pool="w_pool", pool_scale="pool_scale", b_forget="b_forget")
    flat2d = lambda src: [src[key_of[k]].reshape(small[k].shape) for k in SMALL_ORDER]
    small_out = _adamw_small(small_parts, flat2d(w), flat2d(mo), flat2d(vo), "adamw_small")
    for k, four in zip(SMALL_ORDER, small_out):
        res[key_of[k]] = [a.reshape(w[key_of[k]].shape) for a in four]
    (p_in,) = _exchange_wait(*grad_flight["in"][:4], res["w_up"][1], False, "scatter_in_wait")
    res["w_in"] = [a[None] for a in _adamw(p_in[:, :, :w_in.shape[2]], w["w_in"][0], mo["w_in"][0], vo["w_in"][0],
                                           "adamw_w_in")]

    outs = [jnp.sum(loss_parts[:, 0, 0]), grad_x[None]]
    for idx in range(4):
        outs += [res[k][idx] for k in names]
    return tuple(outs)
```

```python
import functools
import math

import jax
import jax.numpy as jnp
from jax import lax
from jax.experimental import pallas as pl
from jax.experimental.pallas import tpu as pltpu

F32 = jnp.float32
BF16 = jnp.bfloat16

NDEV = 8
D_MODEL = 1024
D_POOL = 256
D_FOX = 768
FOX_HEADS = 12
HEAD_PAIRS = FOX_HEADS // 2
XA_HEADS = 4
XA_DIM = 256
D_FF = 4096
D_IN_PAD = 2688
F_COL = 2560
POOL_HALO = 16
NORM_EPS = 1e-6
NEG = -1e30

ADAM_LR = 0.001
ADAM_B1 = 0.9
ADAM_B2 = 0.999
ADAM_EPS = 1e-08
ADAM_WD = 0.01
ADAM_STEP = 10

TM = 512
TQ = 512
TN_FF = 1024
VMEM_LIMIT = 56 * 1024 * 1024
MESH = pl.DeviceIdType.MESH


def _cp(*sem):
    return pltpu.CompilerParams(dimension_semantics=sem, vmem_limit_bytes=VMEM_LIMIT)


def _dot(a, b, dims):
    return lax.dot_general(a, b, (dims, ((), ())), preferred_element_type=F32)


NN = ((1,), (0,))
NT = ((1,), (1,))
TN = ((0,), (0,))


def _mm(a, b, mode, out_dtype, tm, tn, tk, name, b_cols=None, out_cols=None, after=None):
    a_list = list(a) if isinstance(a, (list, tuple)) else [a]
    b_list = list(b) if isinstance(b, (list, tuple)) else [b]
    assert len(a_list) == 1 or len(b_list) == 1
    if mode == "tn":
        K, M = a_list[0].shape
        assert len(a_list) == 1
        Ns = [x.shape[1] for x in b_list]
        N = sum(Ns)
        assert b_cols is None
    else:
        assert len(b_list) == 1
        M = a_list[0].shape[0]
        Ks = [x.shape[1] for x in a_list]
        K = sum(Ks)
        if b_cols is None:
            N = b_list[0].shape[0] if mode == "nt" else b_list[0].shape[1]
        else:
            N = b_list[0].shape[1] if mode == "nt" else NDEV * b_cols
    assert M % tm == 0 and N % tn == 0 and K % tk == 0, (name, M, N, K)
    grid = (M // tm, N // tn, K // tk)
    nk = grid[2]
    dims = {"nn": NN, "nt": NT, "tn": TN}[mode]

    in_specs = []
    if mode == "tn":
        in_specs.append(pl.BlockSpec((tk, tm), lambda i, j, k: (k, i)))
        if len(b_list) == 1:
            in_specs.append(pl.BlockSpec((tk, tn), lambda i, j, k: (k, j)))
        else:
            nj1 = Ns[0] // tn
            in_specs.append(pl.BlockSpec((tk, tn), lambda i, j, k: (k, jnp.minimum(j, nj1 - 1))))
            in_specs.append(pl.BlockSpec((tk, tn), lambda i, j, k: (k, jnp.maximum(j - nj1, 0))))
    else:
        if len(a_list) == 1:
            in_specs.append(pl.BlockSpec((tm, tk), lambda i, j, k: (i, k)))
        else:
            nk1 = Ks[0] // tk
            in_specs.append(pl.BlockSpec((tm, tk), lambda i, j, k: (i, jnp.minimum(k, nk1 - 1))))
            in_specs.append(pl.BlockSpec((tm, tk), lambda i, j, k: (i, jnp.maximum(k - nk1, 0))))
        if b_cols is None:
            if mode == "nn":
                in_specs.append(pl.BlockSpec((tk, tn), lambda i, j, k: (k, j)))
            else:
                in_specs.append(pl.BlockSpec((tn, tk), lambda i, j, k: (j, k)))
        else:
            if mode == "nn":
                per = b_cols // tn
                in_specs.append(pl.BlockSpec((None, tk, tn), lambda i, j, k: (j // per, k, j % per)))
            else:
                per = b_cols // tk
                in_specs.append(pl.BlockSpec((None, tn, tk), lambda i, j, k: (k // per, j, k % per)))
    if out_cols is None:
        out_spec = pl.BlockSpec((tm, tn), lambda i, j, k: (i, j))
        out_shape = jax.ShapeDtypeStruct((M, N), out_dtype)
    else:
        pero = out_cols // tn
        out_spec = pl.BlockSpec((None, tm, tn), lambda i, j, k: (j // pero, i, j % pero))
        out_shape = jax.ShapeDtypeStruct((NDEV, M, out_cols), out_dtype)

    two_a = len(a_list) == 2
    two_b = len(b_list) == 2
    extra = []
    if after is not None:
        in_specs.append(pl.BlockSpec(memory_space=pl.ANY))
        extra.append(after)

    def body(*refs):
        o_ref, acc_ref = refs[-2], refs[-1]
        j = pl.program_id(1)
        k = pl.program_id(2)

        @pl.when(k == 0)
        def _():
            acc_ref[...] = jnp.zeros_like(acc_ref)

        if two_a:
            a1, a2, b1 = refs[0], refs[1], refs[2]
            nk1_ = Ks[0] // tk

            @pl.when(k < nk1_)
            def _():
                acc_ref[...] += _dot(a1[...], b1[...], dims)

            @pl.when(k >= nk1_)
            def _():
                acc_ref[...] += _dot(a2[...], b1[...], dims)
        elif two_b:
            a1, b1, b2 = refs[0], refs[1], refs[2]
            nj1_ = Ns[0] // tn

            @pl.when(j < nj1_)
            def _():
                acc_ref[...] += _dot(a1[...], b1[...], dims)

            @pl.when(j >= nj1_)
            def _():
                acc_ref[...] += _dot(a1[...], b2[...], dims)
        else:
            acc_ref[...] += _dot(refs[0][...], refs[1][...], dims)

        @pl.when(k == nk - 1)
        def _():
            o_ref[...] = acc_ref[...].astype(o_ref.dtype)

    return pl.pallas_call(
        body, name=name, grid=grid, in_specs=in_specs, out_specs=out_spec, out_shape=out_shape,
        scratch_shapes=[pltpu.VMEM((tm, tn), F32)],
        compiler_params=_cp("parallel", "parallel", "arbitrary"),
    )(*a_list, *b_list, *extra)


def _rstd(x):
    return lax.rsqrt(jnp.mean(x * x, axis=-1, keepdims=True) + NORM_EPS)


def _norm_bwd_rows(dxn, xn, r):
    return r * (dxn - xn * jnp.mean(dxn * xn, axis=-1, keepdims=True))


def _row_spec(tm, d):
    return pl.BlockSpec((tm, d), lambda i: (i, 0))


def _vec_spec(d):
    return pl.BlockSpec((1, d), lambda i: (0, 0))


def _mm_rows(a, b, mode, tk, name, rows, vecs, outs, epilogue, b_cols=None, after=None):
    a_list = list(a) if isinstance(a, (list, tuple)) else [a]
    m = a_list[0].shape[0]
    ks = [x.shape[1] for x in a_list]
    n = D_MODEL
    pieces = tk is None
    nk = 1 if pieces else sum(ks) // tk
    dims = NN if mode == "nn" else NT
    if pieces:
        assert mode == "nt" and b_cols is None
        in_specs = [pl.BlockSpec((TM, kp), lambda i, k: (i, 0)) for kp in ks]
        tk = sum(ks)
    elif len(a_list) == 1:
        in_specs = [pl.BlockSpec((TM, tk), lambda i, k: (i, k))]
    else:
        nk1 = ks[0] // tk
        in_specs = [pl.BlockSpec((TM, tk), lambda i, k: (i, jnp.minimum(k, nk1 - 1))),
                    pl.BlockSpec((TM, tk), lambda i, k: (i, jnp.maximum(k - nk1, 0)))]
    if mode == "nn":
        in_specs.append(pl.BlockSpec((tk, n), lambda i, k: (k, 0)))
    elif b_cols is None:
        in_specs.append(pl.BlockSpec((n, tk), lambda i, k: (0, k)))
    else:
        per = b_cols // tk
        in_specs.append(pl.BlockSpec((None, n, tk), lambda i, k: (k // per, 0, k % per)))
    in_specs += [pl.BlockSpec((TM, n), lambda i, k: (i, 0))] * len(rows)
    in_specs += [pl.BlockSpec((1, n), lambda i, k: (0, 0))] * len(vecs)
    extra = []
    if after is not None:
        in_specs.append(pl.BlockSpec(memory_space=pl.ANY))
        extra.append(after)
    out_specs, out_shape = [], []
    for o in outs:
        if o == "sum":
            out_specs.append(pl.BlockSpec((1, n), lambda i, k: (0, 0)))
            out_shape.append(jax.ShapeDtypeStruct((1, n), F32))
        else:
            out_specs.append(pl.BlockSpec((TM, n), lambda i, k: (i, 0)))
            out_shape.append(jax.ShapeDtypeStruct((m, n), o))
    na, nr, nv = len(a_list), len(rows), len(vecs)

    def body(*refs):
        a_refs, b_ref = refs[:na], refs[na]
        row_refs = refs[na + 1:na + 1 + nr]
        vec_refs = refs[na + 1 + nr:na + 1 + nr + nv]
        out_refs = refs[len(refs) - 1 - len(outs):len(refs) - 1]
        acc_ref = refs[-1]
        i, k = pl.program_id(0), pl.program_id(1)

        @pl.when(k == 0)
        def _():
            acc_ref[...] = jnp.zeros_like(acc_ref)

        if pieces:
            off = 0
            for a_ref in a_refs:
                kp = a_ref.shape[1]
                acc_ref[...] += _dot(a_ref[...], b_ref[:, off:off + kp], dims)
                off += kp
        elif na == 1:
            acc_ref[...] += _dot(a_refs[0][...], b_ref[...], dims)
        else:
            nk1_ = ks[0] // tk

            @pl.when(k < nk1_)
            def _():
                acc_ref[...] += _dot(a_refs[0][...], b_ref[...], dims)

            @pl.when(k >= nk1_)
            def _():
                acc_ref[...] += _dot(a_refs[1][...], b_ref[...], dims)

        @pl.when(k == nk - 1)
        def _():
            vals = epilogue(acc_ref[...], [r[...] for r in row_refs], [v[...] for v in vec_refs])
            for o, ref, val in zip(outs, out_refs, vals):
                if o == "sum":
                    @pl.when(i == 0)
                    def _():
                        ref[...] = val

                    @pl.when(i > 0)
                    def _():
                        ref[...] += val
                else:
                    ref[...] = val.astype(o)

    return pl.pallas_call(
        body, name=name, grid=(m // TM, nk), in_specs=in_specs, out_specs=out_specs, out_shape=out_shape,
        scratch_shapes=[pltpu.VMEM((TM, n), F32)],
        compiler_params=_cp("arbitrary", "arbitrary"),
    )(*a_list, b, *rows, *vecs, *extra)


def _proj_in(x, g, w_in, name):
    s, d = x.shape
    n = w_in.shape[1]

    def body(x_ref, g_ref, w_ref, h_ref, p_ref, f_ref):
        xv = x_ref[...]
        h = (xv * _rstd(xv) * g_ref[...]).astype(BF16)
        h_ref[...] = h
        acc = _dot(h, w_ref[...], NN)
        p_ref[...] = acc.astype(BF16)
        f_ref[...] = acc[:, F_COL:]

    return pl.pallas_call(
        body, name=name, grid=(s // TM,),
        in_specs=[_row_spec(TM, d), _vec_spec(d), pl.BlockSpec((d, n), lambda i: (0, 0))],
        out_specs=[_row_spec(TM, d), _row_spec(TM, n), _row_spec(TM, n - F_COL)],
        out_shape=[jax.ShapeDtypeStruct((s, d), BF16), jax.ShapeDtypeStruct((s, n), BF16),
                   jax.ShapeDtypeStruct((s, n - F_COL), F32)],
        compiler_params=_cp("parallel"),
    )(x, g, w_in)


def _dw_in(h, pieces, name):
    s, d = h.shape
    n = sum(p.shape[1] for p in pieces)
    tk = 1024
    nk = s // tk

    def body(*refs):
        h_ref, piece_refs, o_ref, acc_ref = refs[0], refs[1:-2], refs[-2], refs[-1]
        k = pl.program_id(1)

        @pl.when(k == 0)
        def _():
            acc_ref[...] = jnp.zeros_like(acc_ref)

        off = 0
        for p_ref in piece_refs:
            w = p_ref.shape[1]
            acc_ref[:, off:off + w] += _dot(h_ref[...], p_ref[...], TN)
            off += w

        @pl.when(k == nk - 1)
        def _():
            o_ref[...] = acc_ref[...].astype(BF16)

    return pl.pallas_call(
        body, name=name, grid=(d // TM, nk),
        in_specs=[pl.BlockSpec((tk, TM), lambda i, k: (k, i))] +
                 [pl.BlockSpec((tk, p.shape[1]), lambda i, k: (k, 0)) for p in pieces],
        out_specs=pl.BlockSpec((TM, n), lambda i, k: (i, 0)),
        out_shape=jax.ShapeDtypeStruct((d, n), BF16),
        scratch_shapes=[pltpu.VMEM((TM, n), F32)],
        compiler_params=_cp("parallel", "arbitrary"),
    )(h, *pieces)


def _epi_resid(y, rows, vecs):
    (x_in,), (g_post, g_next) = rows, vecs
    xo = x_in + y * _rstd(y) * g_post
    return y, xo, xo * _rstd(xo) * g_next


def _epi_norm_bwd(dh, rows, vecs):
    x, dx_res = rows[0], rows[1]
    r = _rstd(x)
    xn = x * r
    dx = dx_res + _norm_bwd_rows(dh * vecs[0], xn, r)
    res = [dx, jnp.sum(dh * xn, axis=0, keepdims=True)]
    if len(rows) == 3:
        y = rows[2]
        r2 = _rstd(y)
        yn = y * r2
        res += [_norm_bwd_rows(dx * vecs[1], yn, r2), jnp.sum(dx * yn, axis=0, keepdims=True)]
    return res


def _norm_fwd(x, g, name):
    s, d = x.shape
    tm = min(TM, s)

    def body(x_ref, g_ref, h_ref):
        xv = x_ref[...]
        h_ref[...] = (xv * _rstd(xv) * g_ref[...]).astype(BF16)

    return pl.pallas_call(
        body, name=name, grid=(s // tm,), in_specs=[_row_spec(tm, d), _vec_spec(d)],
        out_specs=_row_spec(tm, d), out_shape=jax.ShapeDtypeStruct((s, d), BF16),
        compiler_params=_cp("parallel"),
    )(x, g)


def _norm_bwd(dh, x, dx_res, g_pre, name, prev=None):
    s, d = x.shape
    tm = min(TM, s)
    has_prev = prev is not None

    def body(*refs):
        if has_prev:
            dh_ref, x_ref, dr_ref, g_ref, y_ref, gp_ref, dx_ref, dg_ref, dy_ref, dgp_ref = refs
        else:
            dh_ref, x_ref, dr_ref, g_ref, dx_ref, dg_ref = refs
        i = pl.program_id(0)
        xv = x_ref[...]
        r = _rstd(xv)
        xn = xv * r
        dhv = dh_ref[...].astype(F32)
        dx = dr_ref[...] + _norm_bwd_rows(dhv * g_ref[...], xn, r)
        dx_ref[...] = dx
        dg = jnp.sum(dhv * xn, axis=0, keepdims=True)

        @pl.when(i == 0)
        def _():
            dg_ref[...] = dg

        @pl.when(i > 0)
        def _():
            dg_ref[...] += dg

        if has_prev:
            yv = y_ref[...]
            r2 = _rstd(yv)
            yn = yv * r2
            dy_ref[...] = _norm_bwd_rows(dx * gp_ref[...], yn, r2).astype(BF16)
            dgp = jnp.sum(dx * yn, axis=0, keepdims=True)

            @pl.when(i == 0)
            def _():
                dgp_ref[...] = dgp

            @pl.when(i > 0)
            def _():
                dgp_ref[...] += dgp

    in_specs = [_row_spec(tm, d), _row_spec(tm, d), _row_spec(tm, d), _vec_spec(d)]
    out_specs = [_row_spec(tm, d), _vec_spec(d)]
    out_shape = [jax.ShapeDtypeStruct((s, d), F32), jax.ShapeDtypeStruct((1, d), F32)]
    args = [dh, x, dx_res, g_pre]
    if has_prev:
        in_specs += [_row_spec(tm, d), _vec_spec(d)]
        out_specs += [_row_spec(tm, d), _vec_spec(d)]
        out_shape += [jax.ShapeDtypeStruct((s, d), BF16), jax.ShapeDtypeStruct((1, d), F32)]
        args += list(prev)
    return pl.pallas_call(
        body, name=name, grid=(s // tm,), in_specs=in_specs, out_specs=out_specs, out_shape=out_shape,
        compiler_params=_cp("arbitrary"),
    )(*args)


def _split3(v):
    hi = v.astype(BF16)
    r1 = v - hi.astype(F32)
    mid = r1.astype(BF16)
    lo = (r1 - mid.astype(F32)).astype(BF16)
    return hi, mid, lo


def _tri_dot(tri, v):
    hi, mid, lo = _split3(v)
    return _dot(tri, hi, NN) + _dot(tri, mid, NN) + _dot(tri, lo, NN)


def _gate_cumsum(fraw, b_pad, name):
    s = fraw.shape[0]
    width = HEAD_PAIRS * 128

    def body(f_ref, b_ref, flog_ref, aq_ref, ak_ref, carry_ref):
        i = pl.program_id(0)

        @pl.when(i == 0)
        def _():
            carry_ref[...] = jnp.zeros_like(carry_ref)

        flog = f_ref[...] + b_ref[...]
        flog_ref[...] = flog
        lf = jnp.minimum(flog, 0.0) - jnp.log(1.0 + jnp.exp(-jnp.abs(flog)))
        lane = lax.broadcasted_iota(jnp.int32, (1, 128), 1)
        lf = jnp.where(lane < FOX_HEADS, lf, 0.0)
        row = lax.broadcasted_iota(jnp.int32, (TM, TM), 0)
        col = lax.broadcasted_iota(jnp.int32, (TM, TM), 1)
        tri = (row >= col).astype(BF16)
        cum = _tri_dot(tri, lf) + carry_ref[...]
        carry_ref[...] = cum[TM - 1:TM, :]
        aq_ref[...], ak_ref[...] = _fox_operands(cum)

    return pl.pallas_call(
        body, name=name, grid=(s // TM,),
        in_specs=[_row_spec(TM, 128), _vec_spec(128)],
        out_specs=[_row_spec(TM, 128), _row_spec(TM, width), _row_spec(TM, width)],
        out_shape=[jax.ShapeDtypeStruct((s, 128), F32), jax.ShapeDtypeStruct((s, width), BF16),
                   jax.ShapeDtypeStruct((s, width), BF16)],
        scratch_shapes=[pltpu.VMEM((1, 128), F32)],
        compiler_params=_cp("arbitrary"),
    )(fraw, b_pad)


def _gate_bwd(qaux, kaux, flog, name):
    s = flog.shape[0]
    n = s // TM

    def body(qa_ref, ka_ref, fl_ref, dp_ref, db_ref, carry_ref):
        i = pl.program_id(0)

        @pl.when(i == 0)
        def _():
            carry_ref[...] = jnp.zeros_like(carry_ref)

        lane = lax.broadcasted_iota(jnp.int32, (1, 128), 1)
        dcum = jnp.zeros((TM, 128), F32)
        for p in range(HEAD_PAIRS):
            d = qa_ref[p] - pltpu.roll(ka_ref[p], 128 - 3, 1)
            dcum = jnp.where(lane == 2 * p, pltpu.roll(d, 64 + 2 * p, 1),
                             jnp.where(lane == 2 * p + 1, pltpu.roll(d, 2 * p + 1, 1), dcum))
        row = lax.broadcasted_iota(jnp.int32, (TM, TM), 0)
        col = lax.broadcasted_iota(jnp.int32, (TM, TM), 1)
        tri = (row <= col).astype(BF16)
        dlf = _tri_dot(tri, dcum) + carry_ref[...]
        carry_ref[...] = dlf[0:1, :]
        df = jnp.where(lane < FOX_HEADS, dlf / (1.0 + jnp.exp(fl_ref[...])), 0.0)
        dp_ref[...] = df.astype(BF16)
        db = jnp.sum(df, axis=0, keepdims=True)

        @pl.when(i == 0)
        def _():
            db_ref[...] = db

        @pl.when(i > 0)
        def _():
            db_ref[...] += db

    rev = lambda i: (n - 1 - i, 0)
    return pl.pallas_call(
        body, name=name, grid=(n,),
        in_specs=[pl.BlockSpec((HEAD_PAIRS, TM, 128), lambda i: (0, n - 1 - i, 0)),
                  pl.BlockSpec((HEAD_PAIRS, TM, 128), lambda i: (0, n - 1 - i, 0)), pl.BlockSpec((TM, 128), rev)],
        out_specs=[pl.BlockSpec((TM, 128), rev), _vec_spec(128)],
        out_shape=[jax.ShapeDtypeStruct((s, 128), BF16), jax.ShapeDtypeStruct((1, 128), F32)],
        scratch_shapes=[pltpu.VMEM((1, 128), F32)],
        compiler_params=_cp("arbitrary"),
    )(qaux, kaux, flog)


def _pool_consts(i, rows):
    lane = lax.broadcasted_iota(jnp.int32, (rows, D_POOL), 1)
    t1 = lax.broadcasted_iota(jnp.int32, (rows, D_POOL), 0) + i * TM + 1
    win = jnp.where(lane < 64, 2, jnp.where(lane < 128, 4, jnp.where(lane < 192, 8, 16)))
    inv = 1.0 / jnp.minimum(t1, win).astype(F32)
    return lane, inv


def _by_group(lane, s2, s4, s8, s16):
    return jnp.where(lane < 64, s2, jnp.where(lane < 128, s4, jnp.where(lane < 192, s8, s16)))


def _pool_diff(i, u_ref, halo_ref):
    u = u_ref[...].astype(F32)
    halo = jnp.where(i > 0, halo_ref[...].astype(F32), 0.0)
    ext = jnp.concatenate([halo, u], axis=0)
    s2 = ext + pltpu.roll(ext, 1, 0)
    s4 = s2 + pltpu.roll(s2, 2, 0)
    s8 = s4 + pltpu.roll(s4, 4, 0)
    s16 = s8 + pltpu.roll(s8, 8, 0)
    lane, inv = _pool_consts(i, TM)
    sel = _by_group(lane, s2[POOL_HALO:], s4[POOL_HALO:], s8[POOL_HALO:], s16[POOL_HALO:])
    return sel * inv - u


def _pool_fwd(proj, wbd, scale, ycat, name):
    s = proj.shape[0]
    hb = TM // POOL_HALO

    def body(u_ref, halo_ref, w_ref, sc_ref, y_any, y_ref):
        del y_any
        i = pl.program_id(0)
        diff = _pool_diff(i, u_ref, halo_ref)
        mixed = _dot(diff.astype(BF16), w_ref[...], NN)
        y_ref[...] = (mixed * sc_ref[...]).astype(BF16)

    return pl.pallas_call(
        body, name=name, grid=(s // TM,),
        in_specs=[pl.BlockSpec((TM, D_POOL), lambda i: (i, 0)),
                  pl.BlockSpec((POOL_HALO, D_POOL), lambda i: (jnp.maximum(i * hb - 1, 0), 0)),
                  pl.BlockSpec((D_POOL, D_POOL), lambda i: (0, 0)), _vec_spec(D_POOL),
                  pl.BlockSpec(memory_space=pl.ANY)],
        out_specs=pl.BlockSpec((TM, D_POOL), lambda i: (i, 0)),
        out_shape=jax.ShapeDtypeStruct(ycat.shape, ycat.dtype),
        input_output_aliases={4: 0},
        compiler_params=_cp("parallel"),
    )(proj, proj, wbd, scale, ycat)


def _pool_bwd(proj, dycat, wbd, scale, name):
    s = proj.shape[0]
    n = s // TM
    hb = TM // POOL_HALO
    last_halo = s // POOL_HALO - 1

    def body(u_ref, halo_ref, dy_ref, dyp_ref, w_ref, sc_ref, dp_ref, dw_ref, dsc_ref):
        i = pl.program_id(0)
        diff = _pool_diff(i, u_ref, halo_ref)
        diff_b = diff.astype(BF16)
        mixed = _dot(diff_b, w_ref[...], NN)
        dy = dy_ref[...].astype(F32)
        dmix = (dy * sc_ref[...]).astype(BF16)
        dyp = jnp.where(i < n - 1, dyp_ref[...].astype(F32), 0.0)
        dmix_p = (dyp * sc_ref[...]).astype(BF16)
        dd = _dot(dmix, w_ref[...], NT)
        dd_p = _dot(dmix_p, w_ref[...], NT)
        lane, inv = _pool_consts(i, TM)
        _, inv_p = _pool_consts(i + 1, POOL_HALO)
        ext = jnp.concatenate([dd * inv, dd_p * inv_p], axis=0)
        rows = TM + POOL_HALO
        l2 = ext + pltpu.roll(ext, rows - 1, 0)
        l4 = l2 + pltpu.roll(l2, rows - 2, 0)
        l8 = l4 + pltpu.roll(l4, rows - 4, 0)
        l16 = l8 + pltpu.roll(l8, rows - 8, 0)
        du = _by_group(lane, l2[:TM], l4[:TM], l8[:TM], l16[:TM]) - dd
        dp_ref[...] = du.astype(BF16)
        dw = _dot(diff_b, dmix, TN)
        dsc = jnp.sum(dy * mixed, axis=0, keepdims=True)

        @pl.when(i == 0)
        def _():
            dw_ref[...] = dw
            dsc_ref[...] = dsc

        @pl.when(i > 0)
        def _():
            dw_ref[...] += dw
            dsc_ref[...] += dsc

    return pl.pallas_call(
        body, name=name, grid=(n,),
        in_specs=[pl.BlockSpec((TM, D_POOL), lambda i: (i, 0)),
                  pl.BlockSpec((POOL_HALO, D_POOL), lambda i: (jnp.maximum(i * hb - 1, 0), 0)),
                  pl.BlockSpec((TM, D_POOL), lambda i: (i, 0)),
                  pl.BlockSpec((POOL_HALO, D_POOL), lambda i: (jnp.minimum((i + 1) * hb, last_halo), 0)),
                  pl.BlockSpec((D_POOL, D_POOL), lambda i: (0, 0)), _vec_spec(D_POOL)],
        out_specs=[pl.BlockSpec((TM, D_POOL), lambda i: (i, 0)),
                   pl.BlockSpec((D_POOL, D_POOL), lambda i: (0, 0)), _vec_spec(D_POOL)],
        out_shape=[jax.ShapeDtypeStruct((s, D_POOL), BF16),
                   jax.ShapeDtypeStruct((D_POOL, D_POOL), F32), jax.ShapeDtypeStruct((1, D_POOL), F32)],
        compiler_params=_cp("arbitrary"),
    )(proj, proj, dycat, dycat, wbd, scale)


Q_BLK = D_POOL // 128
K_BLK = Q_BLK + D_FOX // 128
V_BLK = K_BLK + D_FOX // 128
FWD_PAIRS = 2
assert Q_BLK % FWD_PAIRS == 0 and K_BLK % FWD_PAIRS == 0 and V_BLK % FWD_PAIRS == 0 and HEAD_PAIRS % FWD_PAIRS == 0


def _operand_rows(v0, v1, ones_off):
    row = lax.broadcasted_iota(jnp.int32, (128, 1), 0)
    half = row & 63
    out = jnp.where(jnp.logical_and(half >= ones_off, half < ones_off + 3), 1.0, 0.0) + jnp.zeros_like(v0)
    for base, v in ((64, v0), (0, v1)):
        for j, piece in enumerate(_split3(v)):
            out = jnp.where(row == base + j, piece.astype(F32), out)
    return out


def _fox_operands(cum):
    width = HEAD_PAIRS * 128
    hi, mid, lo = _split3(cum)
    packed = (hi.astype(F32) + pltpu.roll(mid.astype(F32), 16, 1) + pltpu.roll(lo.astype(F32), 32, 1)).astype(BF16)
    row = lax.broadcasted_iota(jnp.int32, (128, width), 0)
    col = lax.broadcasted_iota(jnp.int32, (128, width), 1)
    head, j = row & 15, row >> 4
    base = (head >> 1) * 128 + (1 - (head & 1)) * 64
    used = jnp.logical_and(head < FOX_HEADS, j < 3)
    half = lax.broadcasted_iota(jnp.int32, (1, width), 1) & 63
    res = []
    for off, sign, ones_off in ((0, 1.0, 3), (3, -1.0, 0)):
        ones = jnp.where(jnp.logical_and(half >= ones_off, half < ones_off + 3), 1.0, 0.0)
        sel = jnp.where(jnp.logical_and(col == base + off + j, used), sign, 0.0).astype(BF16)
        res.append((ones + _dot(packed, sel, NN)).astype(BF16))
    return res


def _fox_do_operand(dycat, ycat, after, name):
    s = dycat.shape[0]

    rb = 1024
    nblk = D_FOX // D_POOL

    def body(*refs):
        do_refs, o_refs, ad_ref = refs[:nblk], refs[nblk:2 * nblk], refs[-1]
        src = lax.broadcasted_iota(jnp.int32, (D_POOL, D_POOL), 0)
        dst = lax.broadcasted_iota(jnp.int32, (D_POOL, D_POOL), 1)
        same_pair = (src >> 7) == (dst >> 7)
        s_in, d_in = src & 127, dst & 127
        hit = jnp.logical_and(same_pair, jnp.logical_or(
            jnp.logical_and(s_in < 64, jnp.logical_and(d_in >= 64, d_in < 67)), jnp.logical_and(s_in >= 64, d_in < 3)))
        sel = jnp.where(hit, 1.0, 0.0).astype(BF16)
        j = lax.broadcasted_iota(jnp.int32, (1, D_POOL), 1) & 63
        for b in range(nblk):
            dd = do_refs[b][...].astype(F32) * o_refs[b][...].astype(F32)
            dsum = jnp.zeros(dd.shape, F32)
            for piece in _split3(dd):
                dsum = dsum + _dot(piece, sel, NN)
            hi, mid, lo = _split3(-dsum)
            ad_ref[:, D_POOL * b:D_POOL * (b + 1)] = jnp.where(j == 0, hi, jnp.where(j == 1, mid, lo))

    blks = [pl.BlockSpec((rb, D_POOL), functools.partial(lambda i, b: (i, 1 + b), b=b)) for b in range(nblk)]
    return pl.pallas_call(
        body, name=name, grid=(s // rb,), in_specs=blks + blks + [pl.BlockSpec(memory_space=pl.ANY)],
        out_specs=pl.BlockSpec((rb, D_FOX), lambda i: (i, 0)),
        out_shape=jax.ShapeDtypeStruct((s, D_FOX), BF16),
        compiler_params=_cp("parallel"),
    )(*[dycat] * nblk, *[ycat] * nblk, after)


def _causal_pairs(nq, key_major):
    if key_major:
        pairs = [(q, k) for k in range(nq) for q in range(k, nq)]
    else:
        pairs = [(q, k) for q in range(nq) for k in range(q + 1)]
    return (jnp.asarray([p[0] for p in pairs], jnp.int32), jnp.asarray([p[1] for p in pairs], jnp.int32))


def _fox_fwd(proj, aq, ak, name):
    s = proj.shape[0]
    nq = s // TQ
    qi_arr, ki_arr = _causal_pairs(nq, key_major=False)

    wide = FWD_PAIRS * 128
    heads = [(pp, hh) for pp in range(FWD_PAIRS) for hh in range(2)]

    def body(qi_ref, ki_ref, q_ref, k_ref, v_ref, aq_ref, ak_ref, o_ref, aqb_ref, m_ref, acc_ref, aux_ref):
        t = pl.program_id(1)
        qi, ki = qi_ref[t], ki_ref[t]
        lane = lax.broadcasted_iota(jnp.int32, (1, 128), 1)
        masks = [lane < 64, lane >= 64]
        ones_v = jnp.where((lane & 63) == 8, 1.0, 0.0).astype(BF16)
        top = lax.broadcasted_iota(jnp.int32, (128, 1), 0) < 64

        @pl.when(ki == 0)
        def _():
            m_ref[...] = jnp.full_like(m_ref, NEG)
            acc_ref[...] = jnp.zeros_like(acc_ref)
            aux_ref[...] = jnp.zeros_like(aux_ref)

        def step(diag):
            q2s = q_ref[...] * 0.125
            k2, v2, aq2, ak2 = k_ref[...], v_ref[...], aq_ref[...], ak_ref[...]

            def operand(main, lanes, pp, hh):
                cols = slice(128 * pp, 128 * (pp + 1))
                return jnp.where(masks[hh], main[:, cols], lanes[:, cols])

            scs = [_dot(operand(k2, ak2, pp, hh), operand(q2s, aq2, pp, hh), NT) for pp, hh in heads]
            ps, alpha = [], []
            for n, sc in enumerate(scs):
                if diag:
                    key = lax.broadcasted_iota(jnp.int32, sc.shape, 0)
                    qry = lax.broadcasted_iota(jnp.int32, sc.shape, 1)
                    sc = jnp.where(qry >= key, sc, NEG)
                m_prev = m_ref[n]
                m_new = jnp.maximum(m_prev, jnp.max(sc, axis=0, keepdims=True))
                m_ref[n] = m_new
                alpha.append(jnp.exp(m_prev - m_new))
                ps.append(jnp.exp(sc - m_new).astype(BF16))
            ones2 = jnp.concatenate([ones_v] * FWD_PAIRS, axis=1)
            pv = [_dot(operand(v2, ones2, pp, hh), ps[n], TN) for n, (pp, hh) in enumerate(heads)]
            for pp in range(FWD_PAIRS):
                a0, a1, pv0, pv1 = alpha[2 * pp], alpha[2 * pp + 1], pv[2 * pp], pv[2 * pp + 1]
                acc_ref[pp] = acc_ref[pp] * jnp.where(top, a0, a1) + jnp.where(top, pv0, pv1)
                aux_ref[pp] = aux_ref[pp] * jnp.where(top, a1, a0) + jnp.where(top, pv1, pv0)

        @pl.when(ki < qi)
        def _():
            step(False)

        @pl.when(ki == qi)
        def _():
            step(True)
            for pp in range(FWD_PAIRS):
                cols = slice(128 * pp, 128 * (pp + 1))
                aux = aux_ref[pp]
                l0, l1 = aux[72:73, :], aux[8:9, :]
                o_ref[:, cols] = (acc_ref[pp] * jnp.where(top, 1.0 / l0, 1.0 / l1)).T.astype(BF16)
                aqt = aq_ref[:, cols].astype(F32).T
                cum0 = aqt[64:65, :] + aqt[65:66, :] + aqt[66:67, :]
                cum1 = aqt[0:1, :] + aqt[1:2, :] + aqt[2:3, :]
                aqb = _operand_rows(cum0 - (m_ref[2 * pp] + jnp.log(l0)), cum1 - (m_ref[2 * pp + 1] + jnp.log(l1)), 3)
                aqb_ref[:, cols] = aqb.T.astype(BF16)

    grid_spec = pltpu.PrefetchScalarGridSpec(
        num_scalar_prefetch=2, grid=(HEAD_PAIRS // FWD_PAIRS, int(qi_arr.shape[0])),
        in_specs=[pl.BlockSpec((TQ, wide), lambda p, t, qi, ki: (qi[t], Q_BLK // FWD_PAIRS + p)),
                  pl.BlockSpec((TQ, wide), lambda p, t, qi, ki: (ki[t], K_BLK // FWD_PAIRS + p)),
                  pl.BlockSpec((TQ, wide), lambda p, t, qi, ki: (ki[t], V_BLK // FWD_PAIRS + p)),
                  pl.BlockSpec((TQ, wide), lambda p, t, qi, ki: (qi[t], p)),
                  pl.BlockSpec((TQ, wide), lambda p, t, qi, ki: (ki[t], p))],
        out_specs=[pl.BlockSpec((TQ, wide), lambda p, t, qi, ki: (qi[t], Q_BLK // FWD_PAIRS + p)),
                   pl.BlockSpec((TQ, wide), lambda p, t, qi, ki: (qi[t], p))],
        scratch_shapes=[pltpu.VMEM((2 * FWD_PAIRS, 1, TQ), F32),
                        pltpu.VMEM((FWD_PAIRS, 128, TQ), F32), pltpu.VMEM((FWD_PAIRS, 128, TQ), F32)])
    return pl.pallas_call(
        body, name=name, grid_spec=grid_spec,
        out_shape=[jax.ShapeDtypeStruct((s, D_MODEL), BF16), jax.ShapeDtypeStruct((s, HEAD_PAIRS * 128), BF16)],
        compiler_params=_cp("parallel", "arbitrary"),
    )(qi_arr, ki_arr, proj, proj, proj, aq, ak)


def _fox_bwd(proj, dycat, aqb, ak, ad, name):
    s = proj.shape[0]
    nq = s // TQ
    qi_arr, ki_arr = _causal_pairs(nq, key_major=True)

    def body(qi_ref, ki_ref, q_ref, k_ref, v_ref, do_ref, aq_ref, ak_ref, ad_ref,
             dq_ref, dk_ref, dv_ref, qaux_ref, kaux_ref, dq_acc, qaux_acc, dk_acc, dv_acc, kaux_acc):
        t = pl.program_id(1)
        qi, ki = qi_ref[t], ki_ref[t]
        lane = lax.broadcasted_iota(jnp.int32, (1, 128), 1)
        masks = [lane < 64, lane >= 64]
        ones_v = jnp.where((lane & 63) < 3, 1.0, 0.0).astype(BF16)
        top = lax.broadcasted_iota(jnp.int32, (128, 1), 0) < 64

        @pl.when(qi == ki)
        def _():
            dk_acc[...] = jnp.zeros_like(dk_acc)
            dv_acc[...] = jnp.zeros_like(dv_acc)
            kaux_acc[...] = jnp.zeros_like(kaux_acc)

        def step(diag):
            q2s = q_ref[...] * 0.125
            k2, v2, do2 = k_ref[...], v_ref[...], do_ref[...]
            aq2, ak2, ad2 = aq_ref[...], ak_ref[...], ad_ref[...]
            dq, dk, dv = [], [], []
            for hh in range(2):
                qh = jnp.where(masks[hh], q2s, aq2)
                kh = jnp.where(masks[hh], k2, ak2)
                doh = jnp.where(masks[hh], do2, ad2)
                vh = jnp.where(masks[hh], v2, ones_v)
                sc = _dot(kh, qh, NT)
                if diag:
                    key = lax.broadcasted_iota(jnp.int32, sc.shape, 0)
                    qry = lax.broadcasted_iota(jnp.int32, sc.shape, 1)
                    sc = jnp.where(qry >= key, sc, NEG)
                p = jnp.exp(sc)
                dsb = (p * _dot(vh, doh, NT)).astype(BF16)
                dv.append(_dot(p.astype(BF16), doh, NN))
                dk.append(_dot(dsb, qh, NN))
                dq.append(_dot(kh, dsb, TN))
            dk_acc[...] += jnp.where(masks[0], dk[0], dk[1])
            kaux_acc[...] += jnp.where(masks[0], dk[1], dk[0])
            dv_acc[...] += jnp.where(masks[0], dv[0], dv[1])
            dq_new = jnp.where(top, dq[0], dq[1])
            qaux_new = jnp.where(top, dq[1], dq[0])

            @pl.when(ki == 0)
            def _():
                dq_acc[qi] = dq_new
                qaux_acc[qi] = qaux_new

            @pl.when(ki > 0)
            def _():
                dq_acc[qi] += dq_new
                qaux_acc[qi] += qaux_new

        @pl.when(qi > ki)
        def _():
            step(False)

        @pl.when(qi == ki)
        def _():
            step(True)
            rows = pl.ds(pl.multiple_of(qi * TQ, TQ), TQ)
            dq_ref[rows, :] = (dq_acc[qi] * 0.125).T.astype(BF16)
            qaux_ref[rows, :] = qaux_acc[qi].T

        @pl.when(qi == nq - 1)
        def _():
            dk_ref[...] = dk_acc[...].astype(BF16)
            dv_ref[...] = dv_acc[...].astype(BF16)
            kaux_ref[...] = kaux_acc[...]

    grid_spec = pltpu.PrefetchScalarGridSpec(
        num_scalar_prefetch=2, grid=(HEAD_PAIRS, int(qi_arr.shape[0])),
        in_specs=[pl.BlockSpec((TQ, 128), lambda p, t, qi, ki: (qi[t], Q_BLK + p)),
                  pl.BlockSpec((TQ, 128), lambda p, t, qi, ki: (ki[t], K_BLK + p)),
                  pl.BlockSpec((TQ, 128), lambda p, t, qi, ki: (ki[t], V_BLK + p)),
                  pl.BlockSpec((TQ, 128), lambda p, t, qi, ki: (qi[t], Q_BLK + p)),
                  pl.BlockSpec((TQ, 128), lambda p, t, qi, ki: (qi[t], p)),
                  pl.BlockSpec((TQ, 128), lambda p, t, qi, ki: (ki[t], p)),
                  pl.BlockSpec((TQ, 128), lambda p, t, qi, ki: (qi[t], p))],
        out_specs=[pl.BlockSpec((s, 128), lambda p, t, qi, ki: (0, p)),
                   pl.BlockSpec((TQ, 128), lambda p, t, qi, ki: (ki[t], p)),
                   pl.BlockSpec((TQ, 128), lambda p, t, qi, ki: (ki[t], p)),
                   pl.BlockSpec((None, s, 128), lambda p, t, qi, ki: (p, 0, 0)),
                   pl.BlockSpec((None, TQ, 128), lambda p, t, qi, ki: (p, ki[t], 0))],
        scratch_shapes=[pltpu.VMEM((nq, 128, TQ), F32), pltpu.VMEM((nq, 128, TQ), F32),
                        pltpu.VMEM((TQ, 128), F32), pltpu.VMEM((TQ, 128), F32), pltpu.VMEM((TQ, 128), F32)])
    return pl.pallas_call(
        body, name=name, grid_spec=grid_spec,
        out_shape=[jax.ShapeDtypeStruct((s, D_FOX), BF16)] * 3 + [jax.ShapeDtypeStruct((HEAD_PAIRS, s, 128), F32)] * 2,
        compiler_params=_cp("arbitrary", "arbitrary"),
    )(qi_arr, ki_arr, proj, proj, proj, dycat, aqb, ak, ad)


XA_SCALE = XA_DIM ** -0.5


def _xattn_fwd(q2, kv, name):
    s = q2.shape[0]
    m = kv.shape[0]

    def body(q_ref, kv_ref, o_ref):
        heads = [slice(h * XA_DIM, (h + 1) * XA_DIM) for h in range(XA_HEADS)]
        scs = [_dot(q_ref[:, cols], kv_ref[:, cols], NT) for cols in heads]
        for h in range(XA_HEADS):
            c0 = h * XA_DIM
            sc = scs[h] * XA_SCALE
            e = jnp.exp(sc - jnp.max(sc, axis=1, keepdims=True))
            p = e / jnp.sum(e, axis=1, keepdims=True)
            o_ref[:, c0:c0 + XA_DIM] = _dot(p.astype(BF16), kv_ref[:, D_MODEL + c0:D_MODEL + c0 + XA_DIM], NN).astype(BF16)

    return pl.pallas_call(
        body, name=name, grid=(s // TM,),
        in_specs=[_row_spec(TM, D_MODEL), pl.BlockSpec((m, 2 * D_MODEL), lambda i: (0, 0))],
        out_specs=_row_spec(TM, D_MODEL), out_shape=jax.ShapeDtypeStruct((s, D_MODEL), BF16),
        compiler_params=_cp("parallel"),
    )(q2, kv)


def _xattn_bwd(q2, kv, do2, name):
    s = q2.shape[0]
    m = kv.shape[0]

    def body(q_ref, kv_ref, do_ref, dq_ref, dkv_ref):
        i = pl.program_id(0)

        @pl.when(i == 0)
        def _():
            dkv_ref[...] = jnp.zeros_like(dkv_ref)

        heads = [slice(h * XA_DIM, (h + 1) * XA_DIM) for h in range(XA_HEADS)]
        scs = [_dot(kv_ref[:, cols], q_ref[:, cols], NT) for cols in heads]
        dps = [_dot(kv_ref[:, D_MODEL + cols.start:D_MODEL + cols.stop], do_ref[:, cols], NT) for cols in heads]
        for h in range(XA_HEADS):
            c0 = h * XA_DIM
            v0 = D_MODEL + c0
            qh = q_ref[:, c0:c0 + XA_DIM]
            kh = kv_ref[:, c0:c0 + XA_DIM]
            doh = do_ref[:, c0:c0 + XA_DIM]
            sc = scs[h] * XA_SCALE
            e = jnp.exp(sc - jnp.max(sc, axis=0, keepdims=True))
            p = e / jnp.sum(e, axis=0, keepdims=True)
            dp = dps[h]
            ds = p * (dp - jnp.sum(p * dp, axis=0, keepdims=True))
            dsb = (ds * XA_SCALE).astype(BF16)
            dq_ref[:, c0:c0 + XA_DIM] = _dot(kh, dsb, TN).T.astype(BF16)
            dkv_ref[:, c0:c0 + XA_DIM] += _dot(dsb, qh, NN)
            dkv_ref[:, v0:v0 + XA_DIM] += _dot(p.astype(BF16), doh, NN)

    return pl.pallas_call(
        body, name=name, grid=(s // TM,),
        in_specs=[_row_spec(TM, D_MODEL), pl.BlockSpec((m, 2 * D_MODEL), lambda i: (0, 0)), _row_spec(TM, D_MODEL)],
        out_specs=[_row_spec(TM, D_MODEL), pl.BlockSpec((m, 2 * D_MODEL), lambda i: (0, 0))],
        out_shape=[jax.ShapeDtypeStruct((s, D_MODEL), BF16), jax.ShapeDtypeStruct((m, 2 * D_MODEL), F32)],
        compiler_params=_cp("arbitrary"),
    )(q2, kv, do2)


GELU_C = math.sqrt(2.0 / math.pi)
GELU_A = 0.044715


def _gelu(x):
    return (0.5 * x) * (1.0 + jnp.tanh(x * (GELU_C * GELU_A * (x * x) + GELU_C)))


def _gelu_and_grad(x):
    x2 = x * x
    s = 1.0 + jnp.tanh(x * (GELU_C * GELU_A * x2 + GELU_C))
    hx = 0.5 * x
    return hx * s, s * (0.5 + hx * (2.0 - s) * (3.0 * GELU_C * GELU_A * x2 + GELU_C))


def _conv(h, s1, s2, w_ref, b_ref):
    return w_ref[0:1, :] * s2 + w_ref[1:2, :] * s1 + w_ref[2:3, :] * h + b_ref[...]


def _shift_down(main, prev8):
    row = lax.broadcasted_iota(jnp.int32, main.shape, 0)
    s1 = jnp.where(row == 0, prev8[7:8, :], pltpu.roll(main, 1, 0))
    s2 = jnp.where(row == 0, prev8[6:7, :], jnp.where(row == 1, prev8[7:8, :], pltpu.roll(main, 2, 0)))
    return s1, s2


def _shift_up(main, next8):
    n = main.shape[0]
    row = lax.broadcasted_iota(jnp.int32, main.shape, 0)
    u1 = jnp.where(row == n - 1, next8[0:1, :], pltpu.roll(main, n - 1, 0))
    u2 = jnp.where(row == n - 2, next8[0:1, :], jnp.where(row == n - 1, next8[1:2, :], pltpu.roll(main, n - 2, 0)))
    return u1, u2


def _ffn_fwd(h3, w_up, cw, cb, w_down, x2, tgt, g_post, name):
    s = h3.shape[0]
    tn = TN_FF
    nj = D_FF // tn
    per = D_MODEL // tn
    hb = TM // 8

    def body(h_ref, halo_ref, wg_ref, wu_ref, cwg_ref, cwu_ref, cbg_ref, cbu_ref, wd_ref, x_ref, t_ref, g_ref,
             hg_ref, hu_ref, cg_ref, cu_ref, a_ref, loss_ref, dx_ref, dy_ref, dg_ref, y_acc):
        i, j = pl.program_id(0), pl.program_id(1)
        h = h_ref[...]
        halo = halo_ref[...]
        halo = jnp.where(i > 0, halo, jnp.zeros_like(halo))
        hid = [(_dot(h, w_ref[...], NN), _dot(halo, w_ref[...], NN)) for w_ref in (wg_ref, wu_ref)]
        conv = []
        for (hm, hm_halo), cw_ref, cb_ref, hid_ref, c_ref in zip(hid, (cwg_ref, cwu_ref), (cbg_ref, cbu_ref),
                                                                  (hg_ref, hu_ref), (cg_ref, cu_ref)):
            hid_ref[...] = hm.astype(BF16)
            s1, s2 = _shift_down(hm, hm_halo)
            c = _conv(hm, s1, s2, cw_ref, cb_ref)
            c_ref[...] = c.astype(BF16)
            conv.append(c)
        a = (_gelu(conv[0]) * conv[1]).astype(BF16)
        a_ref[...] = a
        contrib = _dot(a, wd_ref[...], NN)

        @pl.when(j == 0)
        def _():
            y_acc[...] = contrib

        @pl.when(j > 0)
        def _():
            y_acc[...] += contrib

        @pl.when(j == nj - 1)
        def _():
            yv = y_acc[...]
            r = _rstd(yv)
            yn = yv * r
            e = x_ref[...] + yn * g_ref[...] - t_ref[...]
            part = 0.5 * jnp.sum(jnp.mean(e * e, axis=-1, keepdims=True), axis=0, keepdims=True)
            part = jnp.broadcast_to(part, (1, 128))
            dx = e * (1.0 / D_MODEL)
            dx_ref[...] = dx
            dy_ref[...] = _norm_bwd_rows(dx * g_ref[...], yn, r).astype(BF16)
            dg = jnp.sum(dx * yn, axis=0, keepdims=True)

            @pl.when(i == 0)
            def _():
                dg_ref[...] = dg
                loss_ref[...] = part

            @pl.when(i > 0)
            def _():
                dg_ref[...] += dg
                loss_ref[...] += part

    rows = pl.BlockSpec((TM, D_MODEL), lambda i, j: (i, 0))
    tile = pl.BlockSpec((TM, tn), lambda i, j: (i, j))
    wide = jax.ShapeDtypeStruct((s, D_FF), BF16)
    return pl.pallas_call(
        body, name=name, grid=(s // TM, nj),
        in_specs=[rows,
                  pl.BlockSpec((8, D_MODEL), lambda i, j: (jnp.maximum(i * hb - 1, 0), 0)),
                  pl.BlockSpec((None, D_MODEL, tn), lambda i, j: (j // per, 0, j % per)),
                  pl.BlockSpec((None, D_MODEL, tn), lambda i, j: (NDEV // 2 + j // per, 0, j % per)),
                  pl.BlockSpec((8, tn), lambda i, j: (0, j)),
                  pl.BlockSpec((8, tn), lambda i, j: (0, nj + j)),
                  pl.BlockSpec((1, tn), lambda i, j: (0, j)),
                  pl.BlockSpec((1, tn), lambda i, j: (0, nj + j)),
                  pl.BlockSpec((tn, D_MODEL), lambda i, j: (j, 0)),
                  rows, rows, pl.BlockSpec((1, D_MODEL), lambda i, j: (0, 0))],
        out_specs=[tile, tile, tile, tile, tile,
                   pl.BlockSpec((1, 128), lambda i, j: (0, 0)), rows, rows,
                   pl.BlockSpec((1, D_MODEL), lambda i, j: (0, 0))],
        out_shape=[wide, wide, wide, wide, wide,
                   jax.ShapeDtypeStruct((1, 128), F32), jax.ShapeDtypeStruct((s, D_MODEL), F32),
                   jax.ShapeDtypeStruct((s, D_MODEL), BF16), jax.ShapeDtypeStruct((1, D_MODEL), F32)],
        scratch_shapes=[pltpu.VMEM((TM, D_MODEL), F32)],
        compiler_params=_cp("arbitrary", "arbitrary"),
    )(h3, h3, w_up, w_up, cw, cw, cb, cb, w_down, x2, tgt, g_post)


def _ffn_bwd(dy3, w_down, hid_g, hid_u, conv_g, conv_u, cw, name):
    s = dy3.shape[0]
    n = s // TM
    tn = TN_FF
    nj = D_FF // tn
    hb = TM // 8
    last8 = s // 8 - 1

    def body(dy_ref, dyn_ref, wd_ref, hg_ref, hu_ref, cg_ref, cgn_ref, cu_ref, cun_ref, cwg_ref, cwu_ref,
             dhg_ref, dhu_ref, dcwg_ref, dcwu_ref, dcbg_ref, dcbu_ref):
        i = pl.program_id(1)
        first, last = i == 0, i == n - 1
        da = _dot(dy_ref[...], wd_ref[...], NT)
        dyn = dyn_ref[...]
        dyn = jnp.where(last, jnp.zeros_like(dyn), dyn)
        da_n = _dot(dyn, wd_ref[...], NT)
        c_g, c_u = cg_ref[...].astype(F32), cu_ref[...].astype(F32)
        g, dg = _gelu_and_grad(c_g)
        gn, dgn = _gelu_and_grad(cgn_ref[...].astype(F32))
        outs = ((da * c_u * dg, da_n * cun_ref[...].astype(F32) * dgn, hg_ref, cwg_ref, dhg_ref, dcwg_ref, dcbg_ref),
                (da * g, da_n * gn, hu_ref, cwu_ref, dhu_ref, dcwu_ref, dcbu_ref))
        row8 = lax.broadcasted_iota(jnp.int32, (8, tn), 0)
        for dc, dcn, h_ref, cw_ref, dh_ref, dcw_ref, dcb_ref in outs:
            u1, u2 = _shift_up(dc, dcn)
            dh_ref[...] = (cw_ref[2:3, :] * dc + cw_ref[1:2, :] * u1 + cw_ref[0:1, :] * u2).astype(BF16)
            hm = h_ref[...].astype(F32)
            dcb = jnp.sum(dc, axis=0, keepdims=True)
            dcw = jnp.where(row8 == 0, jnp.sum(hm * u2, axis=0, keepdims=True),
                            jnp.where(row8 == 1, jnp.sum(hm * u1, axis=0, keepdims=True),
                                      jnp.where(row8 == 2, jnp.sum(hm * dc, axis=0, keepdims=True), 0.0)))

            @pl.when(first)
            def _():
                dcw_ref[...] = dcw
                dcb_ref[...] = dcb

            @pl.when(i > 0)
            def _():
                dcw_ref[...] += dcw
                dcb_ref[...] += dcb

    next8 = lambda j, i: (jnp.minimum((i + 1) * hb, last8), j)
    blk = lambda j, i: (i, j)
    col = lambda j, i: (0, j)
    colu = lambda j, i: (0, nj + j)
    tile = pl.BlockSpec((TM, tn), blk)
    return pl.pallas_call(
        body, name=name, grid=(nj, n),
        in_specs=[pl.BlockSpec((TM, D_MODEL), lambda j, i: (i, 0)),
                  pl.BlockSpec((8, D_MODEL), lambda j, i: (jnp.minimum((i + 1) * hb, last8), 0)),
                  pl.BlockSpec((tn, D_MODEL), lambda j, i: (j, 0)),
                  tile, tile, tile, pl.BlockSpec((8, tn), next8), tile, pl.BlockSpec((8, tn), next8),
                  pl.BlockSpec((8, tn), col), pl.BlockSpec((8, tn), colu)],
        out_specs=[tile, tile, pl.BlockSpec((8, tn), col), pl.BlockSpec((8, tn), col),
                   pl.BlockSpec((1, tn), col), pl.BlockSpec((1, tn), col)],
        out_shape=[jax.ShapeDtypeStruct((s, D_FF), BF16), jax.ShapeDtypeStruct((s, D_FF), BF16),
                   jax.ShapeDtypeStruct((8, D_FF), F32), jax.ShapeDtypeStruct((8, D_FF), F32),
                   jax.ShapeDtypeStruct((1, D_FF), F32), jax.ShapeDtypeStruct((1, D_FF), F32)],
        compiler_params=_cp("parallel", "arbitrary"),
    )(dy3, dy3, w_down, hid_g, hid_u, conv_g, conv_g, conv_u, conv_u, cw, cw)


def _slot(p):
    return 4 * p[0] + 2 * p[1] + p[2]


def _all_gather(shards, name):
    n = len(shards)

    def body(*refs):
        ins, outs = refs[:n], refs[n:2 * n]
        send_sems, recv_sems, local_sems = refs[2 * n:]
        x, y, c = lax.axis_index("x"), lax.axis_index("y"), lax.axis_index("c")
        me, sibling = (x, y, c), (x, y, 1 - c)
        chips = [(1 - x, y), (x, 1 - y), (1 - x, 1 - y)]

        def copy(a, k, block, to, from_input=False):
            dst = outs[a].at[_slot(block)]
            return pltpu.make_async_remote_copy(
                src_ref=ins[a] if from_input else dst, dst_ref=dst,
                send_sem=send_sems.at[a, k], recv_sem=recv_sems.at[a, k],
                device_id=to, device_id_type=MESH)

        mine = [pltpu.make_async_copy(ins[a], outs[a].at[_slot(me)], local_sems.at[a]) for a in range(n)]
        for cp in mine:
            cp.start()
        first = []
        for a in range(n):
            first.append(copy(a, 0, me, sibling, True))
            first += [copy(a, 1 + j, me, (*chip, c), True) for j, chip in enumerate(chips)]
        for cp in first:
            cp.start()
        passed = []
        for j, chip in enumerate(chips):
            for a in range(n):
                copy(a, 1 + j, (*chip, c), me).wait_recv()
                fwd = copy(a, 4 + j, (*chip, c), sibling)
                fwd.start()
                passed.append(fwd)
        for a in range(n):
            copy(a, 0, sibling, me).wait_recv()
            for j, chip in enumerate(chips):
                copy(a, 4 + j, (*chip, 1 - c), me).wait_recv()
        for cp in first + passed:
            cp.wait_send()
        for cp in mine:
            cp.wait()

    any_spec = pl.BlockSpec(memory_space=pl.ANY)
    return pl.pallas_call(
        body, name=name,
        in_specs=[any_spec] * n, out_specs=[any_spec] * n,
        out_shape=[jax.ShapeDtypeStruct((NDEV,) + s.shape, s.dtype) for s in shards],
        scratch_shapes=[pltpu.SemaphoreType.DMA((n, 7)), pltpu.SemaphoreType.DMA((n, 7)),
                        pltpu.SemaphoreType.DMA((n,))],
    )(*shards)


def _peer_list(x, y, c):
    return [(1 - x if m & 4 else x, 1 - y if m & 2 else y, 1 - c if m & 1 else c) for m in range(1, NDEV)]


def _exchange_copies(src_refs, land_refs, send_sems, recv_sems, gather):
    x, y, c = lax.axis_index("x"), lax.axis_index("y"), lax.axis_index("c")
    me = (x, y, c)
    copies = []
    for m, peer in enumerate(_peer_list(x, y, c)):
        for a in range(len(src_refs)):
            copies.append(pltpu.make_async_remote_copy(
                src_ref=src_refs[a] if gather else src_refs[a].at[_slot(peer)], dst_ref=land_refs[a].at[_slot(me)],
                send_sem=send_sems.at[a * (NDEV - 1) + m], recv_sem=recv_sems.at[a * (NDEV - 1) + m],
                device_id=peer, device_id_type=MESH))
    return copies


def _all_gather_small(shards, name):
    n = len(shards)

    def body(*refs):
        ins, outs = refs[:n], refs[n:2 * n]
        send_sems, recv_sems, local_sems = refs[2 * n:]
        me = (lax.axis_index("x"), lax.axis_index("y"), lax.axis_index("c"))
        mine = [pltpu.make_async_copy(ins[a], outs[a].at[_slot(me)], local_sems.at[a]) for a in range(n)]
        copies = _exchange_copies(ins, outs, send_sems, recv_sems, True)
        for cp in mine + copies:
            cp.start()
        for cp in copies + mine:
            cp.wait()

    any_spec = pl.BlockSpec(memory_space=pl.ANY)
    return pl.pallas_call(
        body, name=name,
        in_specs=[any_spec] * n, out_specs=[any_spec] * n,
        out_shape=[jax.ShapeDtypeStruct((NDEV,) + s.shape, s.dtype) for s in shards],
        scratch_shapes=[pltpu.SemaphoreType.DMA((n * (NDEV - 1),)), pltpu.SemaphoreType.DMA((n * (NDEV - 1),)),
                        pltpu.SemaphoreType.DMA((n,))],
    )(*shards)


def _exchange_start(srcs, lands, after, gather, name):
    n = len(srcs)
    hbm = pl.BlockSpec(memory_space=pltpu.HBM)

    def body(*refs):
        for cp in _exchange_copies(refs[:n], refs[n:2 * n], refs[2 * n + 1], refs[2 * n + 2], gather):
            cp.start()
        token = refs[-1]
        token[...] = jnp.zeros_like(token)

    outs = pl.pallas_call(
        body, name=name,
        out_shape=(pltpu.SemaphoreType.DMA((n * (NDEV - 1),)), pltpu.SemaphoreType.DMA((n * (NDEV - 1),)),
                   *[pltpu.HBM(a.shape, a.dtype) for a in list(srcs) + list(lands)],
                   jax.ShapeDtypeStruct((8, 128), F32)),
        in_specs=[hbm] * (2 * n) + [pl.BlockSpec(memory_space=pl.ANY)],
        out_specs=(pl.BlockSpec(memory_space=pltpu.SEMAPHORE), pl.BlockSpec(memory_space=pltpu.SEMAPHORE),
                   *[hbm] * (2 * n), pl.BlockSpec(memory_space=pltpu.VMEM)),
        input_output_aliases={i: 2 + i for i in range(2 * n)},
        compiler_params=pltpu.CompilerParams(has_side_effects=pltpu.SideEffectType.DATAFLOW_SIDE_EFFECTING),
    )(*[pltpu.with_memory_space_constraint(a, pltpu.HBM) for a in list(srcs) + list(lands)], after)
    return outs[0], outs[1], outs[2:2 + n], outs[2 + n:2 + 2 * n], outs[-1]


def _exchange_wait(send_sems, recv_sems, srcs, lands, after, gather, name):
    n = len(srcs)
    hbm = pl.BlockSpec(memory_space=pltpu.HBM)

    def body(*refs):
        for cp in _exchange_copies(refs[:n], refs[n:2 * n], refs[2 * n], refs[2 * n + 1], gather):
            cp.wait_send()
            cp.wait_recv()

    outs = pl.pallas_call(
        body, name=name,
        out_shape=tuple(pltpu.HBM(a.shape, a.dtype) for a in list(srcs) + list(lands)),
        in_specs=[hbm] * (2 * n) + [pl.BlockSpec(memory_space=pltpu.SEMAPHORE)] * 2 + [pl.BlockSpec(memory_space=pl.ANY)],
        out_specs=tuple([hbm] * (2 * n)),
        input_output_aliases={i: i for i in range(2 * n)},
        compiler_params=pltpu.CompilerParams(has_side_effects=pltpu.SideEffectType.DATAFLOW_SIDE_EFFECTING),
    )(*srcs, *lands, send_sems, recv_sems, after)
    return outs[n:]


def _own_slot(block):
    me = 4 * lax.axis_index("x") + 2 * lax.axis_index("y") + lax.axis_index("c")
    return lax.dynamic_update_slice(lax.empty((NDEV,) + block.shape, block.dtype), block[None], (me, 0, 0))


def _adam_update(p_ref, w_ref, m_ref, v_ref, g_ref, d_ref, mo_ref, vo_ref):
    bc1 = 1.0 - ADAM_B1 ** ADAM_STEP
    bc2 = 1.0 - ADAM_B2 ** ADAM_STEP
    g = p_ref[0].astype(F32)
    for d in range(1, NDEV):
        g = g + p_ref[d].astype(F32)
    g_ref[...] = g
    mn = ADAM_B1 * m_ref[...] + (1.0 - ADAM_B1) * g
    vn = ADAM_B2 * v_ref[...] + (1.0 - ADAM_B2) * (g * g)
    mo_ref[...] = mn
    vo_ref[...] = vn
    d_ref[...] = -ADAM_LR * ((mn / bc1) / (jnp.sqrt(vn / bc2) + ADAM_EPS) + ADAM_WD * w_ref[...])


def _adamw_small(parts, ws, ms, vs, name):
    n = len(ws)

    def body(*refs):
        ins, outs = refs[:4 * n], refs[4 * n:]
        for k in range(n):
            _adam_update(ins[k], ins[n + k], ins[2 * n + k], ins[3 * n + k], *outs[4 * k:4 * k + 4])

    whole = pl.BlockSpec(memory_space=pltpu.VMEM)
    res = pl.pallas_call(
        body, name=name, in_specs=[whole] * (4 * n), out_specs=[whole] * (4 * n),
        out_shape=[jax.ShapeDtypeStruct(a.shape, F32) for a in ws for _ in range(4)],
    )(*parts, *ws, *ms, *vs)
    return [res[4 * k:4 * k + 4] for k in range(n)]


def _adamw(parts, w, m, v, name):
    r, c = w.shape
    tr = r if r * c <= 160 * 1024 else max(8, (160 * 1024 // c) // 8 * 8)
    while r % tr:
        tr -= 8
    body = functools.partial(_adam_update)
    spec = pl.BlockSpec((tr, c), lambda i: (i, 0))
    return pl.pallas_call(
        body, name=name, grid=(r // tr,),
        in_specs=[pl.BlockSpec((NDEV, tr, c), lambda i: (0, i, 0)), spec, spec, spec],
        out_specs=[spec] * 4, out_shape=[jax.ShapeDtypeStruct((r, c), F32)] * 4,
        compiler_params=_cp("parallel"),
    )(parts, w, m, v)


def _local_step(x, mem, tgt, gains, b_forget, w_pool, pool_scale, conv_b, w_in,
                mix_weights, ffn_weights, send_in_grad, send_mix_grads, send_ffn_grads):
    b_pad = jnp.pad(b_forget, ((0, 0), (0, 128 - FOX_HEADS)))
    wbd = jnp.zeros((D_POOL, D_POOL), F32)
    for g in range(4):
        wbd = wbd.at[64 * g:64 * g + 64, 64 * g:64 * g + 64].set(w_pool[g])
    wbd = wbd.astype(BF16)
    scale = pool_scale.reshape(1, D_POOL)

    h1, proj, fraw = _proj_in(x, gains["mix_pre"], w_in, "proj_in")
    flog, aq, ak = _gate_cumsum(fraw, b_pad, "gate_cumsum")
    ycat, aqb = _fox_fwd(proj, aq, ak, "fox_fwd")
    ycat = _pool_fwd(proj, wbd, scale, ycat, "pool_fwd")
    w_mix, w_xq, w_xo, w_xkv = mix_weights(ycat)
    y1, x1, h2 = _mm_rows(ycat, w_mix, "nn", 1024, "mix_out", [x], [gains["mix_post"], gains["xa_pre"]],
                          [F32, F32, BF16], _epi_resid)
    q2 = _mm(h2, w_xq, "nn", BF16, 2048, 1024, 1024, "xa_q")
    mem_n = _norm_fwd(mem, gains["mem"], "norm_mem")
    kv = _mm(mem_n, w_xkv, "nn", BF16, mem.shape[0], 256, 1024, "xa_kv", b_cols=256)
    o2 = _xattn_fwd(q2, kv, "xattn_fwd")
    y2, x2, h3 = _mm_rows(o2, w_xo, "nn", 1024, "xa_out", [x1], [gains["xa_post"], gains["ffn_pre"]],
                          [F32, F32, BF16], _epi_resid)
    w_up, w_down, cw = ffn_weights(h3)
    hid_g, hid_u, conv_g, conv_u, act, loss, dx3, dy3, dg_ffn_post = _ffn_fwd(
        h3, w_up, cw, conv_b, w_down, x2, tgt, gains["ffn_post"], "ffn_fwd")

    dhid_g, dhid_u, dcw_g, dcw_u, dcb_g, dcb_u = _ffn_bwd(dy3, w_down, hid_g, hid_u, conv_g, conv_u, cw, "ffn_bwd")
    d_w_down = _mm(act, dy3, "tn", BF16, 2048, 1024, 1024, "dw_down")
    d_w_up = _mm(h3, [dhid_g, dhid_u], "tn", BF16, 1024, 1024, 2048, "dw_up", out_cols=1024)
    sent = send_ffn_grads(d_w_up, d_w_down, jnp.concatenate([dcw_g, dcw_u], axis=1))
    dh3 = _mm([dhid_g, dhid_u], w_up, "nt", F32, 2048, 1024, 1024, "dh_ffn", b_cols=1024, after=sent)
    dx2, dg_ffn_pre, dy2, dg_xa_post = _norm_bwd(dh3, x2, dx3, gains["ffn_pre"], "norm_bwd_ffn",
                                                 prev=(y2, gains["xa_post"]))
    do2 = _mm(dy2, w_xo, "nt", BF16, 2048, 1024, 1024, "d_xa_out")
    d_w_xo = _mm(o2, dy2, "tn", BF16, 1024, 1024, 1024, "dw_xo")
    dq2, dkv = _xattn_bwd(q2, kv, do2, "xattn_bwd")
    dkv = dkv.astype(BF16)
    dx1, dg_xa_pre, dy1, dg_mix_post = _mm_rows(
        dq2, w_xq, "nt", 1024, "dh_xa", [x1, dx2, y1], [gains["xa_pre"], gains["mix_post"]],
        [F32, "sum", BF16, "sum"], _epi_norm_bwd)
    d_w_xq = _mm(h2, dq2, "tn", BF16, 1024, 1024, 1024, "dw_xq")
    dmem_n = _mm(dkv, w_xkv, "nt", F32, mem.shape[0], 1024, 256, "d_mem", b_cols=256)
    d_w_xkv = _mm(mem_n, dkv, "tn", BF16, 1024, 256, mem.shape[0], "dw_xkv", out_cols=256)
    _, dg_mem = _norm_bwd(dmem_n, mem, jnp.zeros_like(mem), gains["mem"], "norm_bwd_mem")
    dycat = _mm(dy1, w_mix, "nt", BF16, 2048, 1024, 1024, "d_mix_out")
    d_w_mix = _mm(ycat, dy1, "tn", BF16, 1024, 1024, 1024, "dw_mix")
    sent_mix = send_mix_grads(d_w_mix, d_w_xq, d_w_xo, d_w_xkv)
    ad = _fox_do_operand(dycat, ycat, sent_mix, "fox_do_operand")
    dq, dk, dv, qaux, kaux = _fox_bwd(proj, dycat, aqb, ak, ad, "fox_bwd")
    du, d_wbd, d_scale = _pool_bwd(proj, dycat, wbd, scale, "pool_bwd")
    df, db_f = _gate_bwd(qaux, kaux, flog, "gate_bwd")
    dproj = [du, dq, dk, dv, df]
    sent_in = send_in_grad(_dw_in(h1, dproj, "dw_in"))
    grad_x, dg_mix_pre = _mm_rows(dproj, w_in, "nt", None, "dh_mix", [x, dx1], [gains["mix_pre"]],
                                  [F32, "sum"], _epi_norm_bwd, after=sent_in)

    small = dict(
        mix_pre=dg_mix_pre, mix_post=dg_mix_post, mem=dg_mem, xa_pre=dg_xa_pre, xa_post=dg_xa_post,
        ffn_pre=dg_ffn_pre, ffn_post=dg_ffn_post,
        conv_b=jnp.concatenate([dcb_g, dcb_u], axis=1),
        w_pool=jnp.concatenate([d_wbd[64 * g:64 * g + 64, 64 * g:64 * g + 64] for g in range(4)], axis=0),
        pool_scale=d_scale.reshape(4, 64),
        b_forget=db_f[:, :FOX_HEADS],
    )
    return loss, grad_x, small


SMALL_ORDER = ("mix_pre", "mix_post", "mem", "xa_pre", "xa_post", "ffn_pre", "ffn_post", "conv_b",
               "w_pool", "pool_scale", "b_forget")


def kernel(x, mem, norm_mix_pre, norm_mix_post, w_in, b_forget, w_pool, pool_scale, w_mix_out, norm_mem, norm_xa_pre, norm_xa_post, w_xq, w_xkv, w_xo, norm_ffn_pre, norm_ffn_post, w_up, conv_w, conv_b, w_down, loss_target, m_norm_mix_pre, m_norm_mix_post, m_w_in, m_b_forget, m_w_pool, m_pool_scale, m_w_mix_out, m_norm_mem, m_norm_xa_pre, m_norm_xa_post, m_w_xq, m_w_xkv, m_w_xo, m_norm_ffn_pre, m_norm_ffn_post, m_w_up, m_conv_w, m_conv_b, m_w_down, v_norm_mix_pre, v_norm_mix_post, v_w_in, v_b_forget, v_w_pool, v_pool_scale, v_w_mix_out, v_norm_mem, v_norm_xa_pre, v_norm_xa_post, v_w_xq, v_w_xkv, v_w_xo, v_norm_ffn_pre, v_norm_ffn_post, v_w_up, v_conv_w, v_conv_b, v_w_down):
    names = ("norm_mix_pre", "norm_mix_post", "w_in", "b_forget", "w_pool", "pool_scale", "w_mix_out", "norm_mem",
             "norm_xa_pre", "norm_xa_post", "w_xq", "w_xkv", "w_xo", "norm_ffn_pre", "norm_ffn_post", "w_up",
             "conv_w", "conv_b", "w_down")
    w = dict(zip(names, (norm_mix_pre, norm_mix_post, w_in, b_forget, w_pool, pool_scale, w_mix_out, norm_mem,
                         norm_xa_pre, norm_xa_post, w_xq, w_xkv, w_xo, norm_ffn_pre, norm_ffn_post, w_up,
                         conv_w, conv_b, w_down)))
    mo = dict(zip(names, (m_norm_mix_pre, m_norm_mix_post, m_w_in, m_b_forget, m_w_pool, m_pool_scale, m_w_mix_out,
                          m_norm_mem, m_norm_xa_pre, m_norm_xa_post, m_w_xq, m_w_xkv, m_w_xo, m_norm_ffn_pre,
                          m_norm_ffn_post, m_w_up, m_conv_w, m_conv_b, m_w_down)))
    vo = dict(zip(names, (v_norm_mix_pre, v_norm_mix_post, v_w_in, v_b_forget, v_w_pool, v_pool_scale, v_w_mix_out,
                          v_norm_mem, v_norm_xa_pre, v_norm_xa_post, v_w_xq, v_w_xkv, v_w_xo, v_norm_ffn_pre,
                          v_norm_ffn_post, v_w_up, v_conv_w, v_conv_b, v_w_down)))

    big_names = ("w_in", "w_mix_out", "w_xq", "w_xo", "w_xkv", "w_up", "w_down")
    shards = {k: w[k][0].astype(BF16) for k in big_names}
    shards["w_in"] = jnp.pad(shards["w_in"], ((0, 0), (0, D_IN_PAD - shards["w_in"].shape[1])))
    conv_w_sh = jnp.pad(conv_w[0, :, 0, :], ((0, 5), (0, 0)))
    (g_in,) = _all_gather([shards["w_in"]], "gather_w_in")
    mix_srcs = [shards[k] for k in ("w_mix_out", "w_xq", "w_xo", "w_xkv")]
    mix_flight = _exchange_start(mix_srcs, [_own_slot(a) for a in mix_srcs], g_in, True, "gather_mix_start")
    ffn_srcs = [shards["w_up"], shards["w_down"], conv_w_sh]
    ffn_flight = _exchange_start(ffn_srcs, [_own_slot(a) for a in ffn_srcs], mix_flight[4], True, "gather_ffn_start")
    my_slot = 4 * lax.axis_index("x") + 2 * lax.axis_index("y") + lax.axis_index("c")
    own_block = lambda a: _own_slot(lax.dynamic_index_in_dim(a, my_slot, 0, keepdims=False))
    by_rows = lambda a: a.reshape(NDEV, a.shape[0] // NDEV, a.shape[1])
    by_cols = lambda a: a.reshape(a.shape[0], NDEV, a.shape[1] // NDEV).transpose(1, 0, 2)
    grad_flight = {}

    def mix_weights(after):
        g_mix, g_xq, g_xo, g_xkv = _exchange_wait(*mix_flight[:4], after, True, "gather_mix_wait")
        return (g_mix.reshape(D_MODEL, D_MODEL), g_xq.reshape(D_MODEL, D_MODEL), g_xo.reshape(D_MODEL, D_MODEL), g_xkv)

    def ffn_weights(after):
        g_up, g_down, g_cw = _exchange_wait(*ffn_flight[:4], after, True, "gather_ffn_wait")
        return g_up, g_down.reshape(D_FF, D_MODEL), g_cw.transpose(1, 0, 2).reshape(8, 2 * D_FF)

    def send_ffn_grads(d_w_up, d_w_down, d_cw):
        srcs = [d_w_up, by_rows(d_w_down), by_cols(d_cw)]
        grad_flight["ffn"] = _exchange_start(srcs, [own_block(a) for a in srcs], ffn_flight[4], False, "scatter_ffn_start")
        return grad_flight["ffn"][4]

    def send_mix_grads(d_w_mix, d_w_xq, d_w_xo, d_w_xkv):
        srcs = [by_rows(d_w_mix), by_rows(d_w_xq), by_rows(d_w_xo), d_w_xkv]
        grad_flight["mix"] = _exchange_start(srcs, [own_block(a) for a in srcs], ffn_flight[4], False, "scatter_mix_start")
        return grad_flight["mix"][4]

    def send_in_grad(d_w_in):
        srcs = [by_rows(d_w_in)]
        grad_flight["in"] = _exchange_start(srcs, [own_block(a) for a in srcs], ffn_flight[4], False, "scatter_in_start")
        return grad_flight["in"][4]

    gains = dict(mix_pre=norm_mix_pre + ffn_flight[4][0, 0], mix_post=norm_mix_post, mem=norm_mem, xa_pre=norm_xa_pre,
                 xa_post=norm_xa_post, ffn_pre=norm_ffn_pre, ffn_post=norm_ffn_post)
    loss, grad_x, small = _local_step(
        x[0], mem[0], loss_target[0], gains, b_forget, w_pool[0], pool_scale[0], conv_b,
        g_in.reshape(D_MODEL, D_IN_PAD), mix_weights, ffn_weights, send_in_grad, send_mix_grads, send_ffn_grads)

    p_up, p_down, p_cw = _exchange_wait(*grad_flight["ffn"][:4], grad_x, False, "scatter_ffn_wait")
    p_mix, p_xq, p_xo, p_xkv = _exchange_wait(*grad_flight["mix"][:4], grad_x, False, "scatter_mix_wait")
    parts = dict(w_mix_out=p_mix, w_xq=p_xq, w_xo=p_xo, w_xkv=p_xkv, w_up=p_up, w_down=p_down)
    *small_parts, loss_parts = _all_gather_small([small[k] for k in SMALL_ORDER] + [loss], "gather_small_grads")

    res = {k: [a[None] for a in _adamw(p, w[k][0], mo[k][0], vo[k][0], "adamw_" + k)] for k, p in parts.items()}
    pad_cw = lambda a: jnp.pad(a[0, :, 0, :], ((0, 5), (0, 0)))
    res["conv_w"] = [a[:3][None, :, None, :] for a in
                     _adamw(p_cw, pad_cw(conv_w), pad_cw(m_conv_w), pad_cw(v_conv_w), "adamw_conv_w")]
    key_of = dict(mix_pre="norm_mix_pre", mix_post="norm_mix_post", mem="norm_mem", xa_pre="norm_xa_pre",
                  xa_post="norm_xa_post", ffn_pre="norm_ffn_pre", ffn_post="norm_ffn_post", conv_b="conv_b",
                  w_pool="w_pool", pool_scale="pool_scale", b_forget="b_forget")
    flat2d = lambda src: [src[key_of[k]].reshape(small[k].shape) for k in SMALL_ORDER]
    small_out = _adamw_small(small_parts, flat2d(w), flat2d(mo), flat2d(vo), "adamw_small")
    for k, four in zip(SMALL_ORDER, small_out):
        res[key_of[k]] = [a.reshape(w[key_of[k]].shape) for a in four]
    (p_in,) = _exchange_wait(*grad_flight["in"][:4], res["w_up"][1], False, "scatter_in_wait")
    res["w_in"] = [a[None] for a in _adamw(p_in[:, :, :w_in.shape[2]], w["w_in"][0], mo["w_in"][0], vo["w_in"][0],
                                           "adamw_w_in")]

    outs = [jnp.sum(loss_parts[:, 0, 0]), grad_x[None]]
    for idx in range(4):
        outs += [res[k][idx] for k in names]
    return tuple(outs)
```

```python
import functools
import math

import jax
import jax.numpy as jnp
from jax import lax
from jax.experimental import pallas as pl
from jax.experimental.pallas import tpu as pltpu

F32 = jnp.float32
BF16 = jnp.bfloat16

NDEV = 8
D_MODEL = 1024
D_POOL = 256
D_FOX = 768
FOX_HEADS = 12
HEAD_PAIRS = FOX_HEADS // 2
XA_HEADS = 4
XA_DIM = 256
D_FF = 4096
D_IN_PAD = 2688
F_COL = 2560
POOL_HALO = 16
NORM_EPS = 1e-6
NEG = -1e30

ADAM_LR = 0.001
ADAM_B1 = 0.9
ADAM_B2 = 0.999
ADAM_EPS = 1e-08
ADAM_WD = 0.01
ADAM_STEP = 10

TM = 512
TQ = 512
TN_FF = 1024
VMEM_LIMIT = 56 * 1024 * 1024
MESH = pl.DeviceIdType.MESH


def _cp(*sem):
    return pltpu.CompilerParams(dimension_semantics=sem, vmem_limit_bytes=VMEM_LIMIT)


def _dot(a, b, dims):
    return lax.dot_general(a, b, (dims, ((), ())), preferred_element_type=F32)


NN = ((1,), (0,))
NT = ((1,), (1,))
TN = ((0,), (0,))


def _mm(a, b, mode, out_dtype, tm, tn, tk, name, b_cols=None, out_cols=None, after=None):
    a_list = list(a) if isinstance(a, (list, tuple)) else [a]
    b_list = list(b) if isinstance(b, (list, tuple)) else [b]
    assert len(a_list) == 1 or len(b_list) == 1
    if mode == "tn":
        K, M = a_list[0].shape
        assert len(a_list) == 1
        Ns = [x.shape[1] for x in b_list]
        N = sum(Ns)
        assert b_cols is None
    else:
        assert len(b_list) == 1
        M = a_list[0].shape[0]
        Ks = [x.shape[1] for x in a_list]
        K = sum(Ks)
        if b_cols is None:
            N = b_list[0].shape[0] if mode == "nt" else b_list[0].shape[1]
        else:
            N = b_list[0].shape[1] if mode == "nt" else NDEV * b_cols
    assert M % tm == 0 and N % tn == 0 and K % tk == 0, (name, M, N, K)
    grid = (M // tm, N // tn, K // tk)
    nk = grid[2]
    dims = {"nn": NN, "nt": NT, "tn": TN}[mode]

    in_specs = []
    if mode == "tn":
        in_specs.append(pl.BlockSpec((tk, tm), lambda i, j, k: (k, i)))
        if len(b_list) == 1:
            in_specs.append(pl.BlockSpec((tk, tn), lambda i, j, k: (k, j)))
        else:
            nj1 = Ns[0] // tn
            in_specs.append(pl.BlockSpec((tk, tn), lambda i, j, k: (k, jnp.minimum(j, nj1 - 1))))
            in_specs.append(pl.BlockSpec((tk, tn), lambda i, j, k: (k, jnp.maximum(j - nj1, 0))))
    else:
        if len(a_list) == 1:
            in_specs.append(pl.BlockSpec((tm, tk), lambda i, j, k: (i, k)))
        else:
            nk1 = Ks[0] // tk
            in_specs.append(pl.BlockSpec((tm, tk), lambda i, j, k: (i, jnp.minimum(k, nk1 - 1))))
            in_specs.append(pl.BlockSpec((tm, tk), lambda i, j, k: (i, jnp.maximum(k - nk1, 0))))
        if b_cols is None:
            if mode == "nn":
                in_specs.append(pl.BlockSpec((tk, tn), lambda i, j, k: (k, j)))
            else:
                in_specs.append(pl.BlockSpec((tn, tk), lambda i, j, k: (j, k)))
        else:
            if mode == "nn":
                per = b_cols // tn
                in_specs.append(pl.BlockSpec((None, tk, tn), lambda i, j, k: (j // per, k, j % per)))
            else:
                per = b_cols // tk
                in_specs.append(pl.BlockSpec((None, tn, tk), lambda i, j, k: (k // per, j, k % per)))
    if out_cols is None:
        out_spec = pl.BlockSpec((tm, tn), lambda i, j, k: (i, j))
        out_shape = jax.ShapeDtypeStruct((M, N), out_dtype)
    else:
        pero = out_cols // tn
        out_spec = pl.BlockSpec((None, tm, tn), lambda i, j, k: (j // pero, i, j % pero))
        out_shape = jax.ShapeDtypeStruct((NDEV, M, out_cols), out_dtype)

    two_a = len(a_list) == 2
    two_b = len(b_list) == 2
    extra = []
    if after is not None:
        in_specs.append(pl.BlockSpec(memory_space=pl.ANY))
        extra.append(after)

    def body(*refs):
        o_ref, acc_ref = refs[-2], refs[-1]
        j = pl.program_id(1)
        k = pl.program_id(2)

        @pl.when(k == 0)
        def _():
            acc_ref[...] = jnp.zeros_like(acc_ref)

        if two_a:
            a1, a2, b1 = refs[0], refs[1], refs[2]
            nk1_ = Ks[0] // tk

            @pl.when(k < nk1_)
            def _():
                acc_ref[...] += _dot(a1[...], b1[...], dims)

            @pl.when(k >= nk1_)
            def _():
                acc_ref[...] += _dot(a2[...], b1[...], dims)
        elif two_b:
            a1, b1, b2 = refs[0], refs[1], refs[2]
            nj1_ = Ns[0] // tn

            @pl.when(j < nj1_)
            def _():
                acc_ref[...] += _dot(a1[...], b1[...], dims)

            @pl.when(j >= nj1_)
            def _():
                acc_ref[...] += _dot(a1[...], b2[...], dims)
        else:
            acc_ref[...] += _dot(refs[0][...], refs[1][...], dims)

        @pl.when(k == nk - 1)
        def _():
            o_ref[...] = acc_ref[...].astype(o_ref.dtype)

    return pl.pallas_call(
        body, name=name, grid=grid, in_specs=in_specs, out_specs=out_spec, out_shape=out_shape,
        scratch_shapes=[pltpu.VMEM((tm, tn), F32)],
        compiler_params=_cp("parallel", "parallel", "arbitrary"),
    )(*a_list, *b_list, *extra)


def _rstd(x):
    return lax.rsqrt(jnp.mean(x * x, axis=-1, keepdims=True) + NORM_EPS)


def _norm_bwd_rows(dxn, xn, r):
    return r * (dxn - xn * jnp.mean(dxn * xn, axis=-1, keepdims=True))


def _row_spec(tm, d):
    return pl.BlockSpec((tm, d), lambda i: (i, 0))


def _vec_spec(d):
    return pl.BlockSpec((1, d), lambda i: (0, 0))


def _mm_rows(a, b, mode, tk, name, rows, vecs, outs, epilogue, b_cols=None, after=None):
    a_list = list(a) if isinstance(a, (list, tuple)) else [a]
    m = a_list[0].shape[0]
    ks = [x.shape[1] for x in a_list]
    n = D_MODEL
    pieces = tk is None
    nk = 1 if pieces else sum(ks) // tk
    dims = NN if mode == "nn" else NT
    if pieces:
        assert mode == "nt" and b_cols is None
        in_specs = [pl.BlockSpec((TM, kp), lambda i, k: (i, 0)) for kp in ks]
        tk = sum(ks)
    elif len(a_list) == 1:
        in_specs = [pl.BlockSpec((TM, tk), lambda i, k: (i, k))]
    else:
        nk1 = ks[0] // tk
        in_specs = [pl.BlockSpec((TM, tk), lambda i, k: (i, jnp.minimum(k, nk1 - 1))),
                    pl.BlockSpec((TM, tk), lambda i, k: (i, jnp.maximum(k - nk1, 0)))]
    if mode == "nn":
        in_specs.append(pl.BlockSpec((tk, n), lambda i, k: (k, 0)))
    elif b_cols is None:
        in_specs.append(pl.BlockSpec((n, tk), lambda i, k: (0, k)))
    else:
        per = b_cols // tk
        in_specs.append(pl.BlockSpec((None, n, tk), lambda i, k: (k // per, 0, k % per)))
    in_specs += [pl.BlockSpec((TM, n), lambda i, k: (i, 0))] * len(rows)
    in_specs += [pl.BlockSpec((1, n), lambda i, k: (0, 0))] * len(vecs)
    extra = []
    if after is not None:
        in_specs.append(pl.BlockSpec(memory_space=pl.ANY))
        extra.append(after)
    out_specs, out_shape = [], []
    for o in outs:
        if o == "sum":
            out_specs.append(pl.BlockSpec((1, n), lambda i, k: (0, 0)))
            out_shape.append(jax.ShapeDtypeStruct((1, n), F32))
        else:
            out_specs.append(pl.BlockSpec((TM, n), lambda i, k: (i, 0)))
            out_shape.append(jax.ShapeDtypeStruct((m, n), o))
    na, nr, nv = len(a_list), len(rows), len(vecs)

    def body(*refs):
        a_refs, b_ref = refs[:na], refs[na]
        row_refs = refs[na + 1:na + 1 + nr]
        vec_refs = refs[na + 1 + nr:na + 1 + nr + nv]
        out_refs = refs[len(refs) - 1 - len(outs):len(refs) - 1]
        acc_ref = refs[-1]
        i, k = pl.program_id(0), pl.program_id(1)

        @pl.when(k == 0)
        def _():
            acc_ref[...] = jnp.zeros_like(acc_ref)

        if pieces:
            off = 0
            for a_ref in a_refs:
                kp = a_ref.shape[1]
                acc_ref[...] += _dot(a_ref[...], b_ref[:, off:off + kp], dims)
                off += kp
        elif na == 1:
            acc_ref[...] += _dot(a_refs[0][...], b_ref[...], dims)
        else:
            nk1_ = ks[0] // tk

            @pl.when(k < nk1_)
            def _():
                acc_ref[...] += _dot(a_refs[0][...], b_ref[...], dims)

            @pl.when(k >= nk1_)
            def _():
                acc_ref[...] += _dot(a_refs[1][...], b_ref[...], dims)

        @pl.when(k == nk - 1)
        def _():
            vals = epilogue(acc_ref[...], [r[...] for r in row_refs], [v[...] for v in vec_refs])
            for o, ref, val in zip(outs, out_refs, vals):
                if o == "sum":
                    @pl.when(i == 0)
                    def _():
                        ref[...] = val

                    @pl.when(i > 0)
                    def _():
                        ref[...] += val
                else:
                    ref[...] = val.astype(o)

    return pl.pallas_call(
        body, name=name, grid=(m // TM, nk), in_specs=in_specs, out_specs=out_specs, out_shape=out_shape,
        scratch_shapes=[pltpu.VMEM((TM, n), F32)],
        compiler_params=_cp("arbitrary", "arbitrary"),
    )(*a_list, b, *rows, *vecs, *extra)


def _proj_in(x, g, w_in, name):
    s, d = x.shape
    n = w_in.shape[1]

    def body(x_ref, g_ref, w_ref, h_ref, p_ref, f_ref):
        xv = x_ref[...]
        h = (xv * _rstd(xv) * g_ref[...]).astype(BF16)
        h_ref[...] = h
        acc = _dot(h, w_ref[...], NN)
        p_ref[...] = acc.astype(BF16)
        f_ref[...] = acc[:, F_COL:]

    return pl.pallas_call(
        body, name=name, grid=(s // TM,),
        in_specs=[_row_spec(TM, d), _vec_spec(d), pl.BlockSpec((d, n), lambda i: (0, 0))],
        out_specs=[_row_spec(TM, d), _row_spec(TM, n), _row_spec(TM, n - F_COL)],
        out_shape=[jax.ShapeDtypeStruct((s, d), BF16), jax.ShapeDtypeStruct((s, n), BF16),
                   jax.ShapeDtypeStruct((s, n - F_COL), F32)],
        compiler_params=_cp("parallel"),
    )(x, g, w_in)


def _dw_in(h, pieces, name):
    s, d = h.shape
    n = sum(p.shape[1] for p in pieces)
    tk = 1024
    nk = s // tk

    def body(*refs):
        h_ref, piece_refs, o_ref, acc_ref = refs[0], refs[1:-2], refs[-2], refs[-1]
        k = pl.program_id(1)

        @pl.when(k == 0)
        def _():
            acc_ref[...] = jnp.zeros_like(acc_ref)

        off = 0
        for p_ref in piece_refs:
            w = p_ref.shape[1]
            acc_ref[:, off:off + w] += _dot(h_ref[...], p_ref[...], TN)
            off += w

        @pl.when(k == nk - 1)
        def _():
            o_ref[...] = acc_ref[...].astype(BF16)

    return pl.pallas_call(
        body, name=name, grid=(d // TM, nk),
        in_specs=[pl.BlockSpec((tk, TM), lambda i, k: (k, i))] +
                 [pl.BlockSpec((tk, p.shape[1]), lambda i, k: (k, 0)) for p in pieces],
        out_specs=pl.BlockSpec((TM, n), lambda i, k: (i, 0)),
        out_shape=jax.ShapeDtypeStruct((d, n), BF16),
        scratch_shapes=[pltpu.VMEM((TM, n), F32)],
        compiler_params=_cp("parallel", "arbitrary"),
    )(h, *pieces)


def _epi_resid(y, rows, vecs):
    (x_in,), (g_post, g_next) = rows, vecs
    xo = x_in + y * _rstd(y) * g_post
    return y, xo, xo * _rstd(xo) * g_next


def _epi_norm_bwd(dh, rows, vecs):
    x, dx_res = rows[0], rows[1]
    r = _rstd(x)
    xn = x * r
    dx = dx_res + _norm_bwd_rows(dh * vecs[0], xn, r)
    res = [dx, jnp.sum(dh * xn, axis=0, keepdims=True)]
    if len(rows) == 3:
        y = rows[2]
        r2 = _rstd(y)
        yn = y * r2
        res += [_norm_bwd_rows(dx * vecs[1], yn, r2), jnp.sum(dx * yn, axis=0, keepdims=True)]
    return res


def _norm_fwd(x, g, name):
    s, d = x.shape
    tm = min(TM, s)

    def body(x_ref, g_ref, h_ref):
        xv = x_ref[...]
        h_ref[...] = (xv * _rstd(xv) * g_ref[...]).astype(BF16)

    return pl.pallas_call(
        body, name=name, grid=(s // tm,), in_specs=[_row_spec(tm, d), _vec_spec(d)],
        out_specs=_row_spec(tm, d), out_shape=jax.ShapeDtypeStruct((s, d), BF16),
        compiler_params=_cp("parallel"),
    )(x, g)


def _norm_bwd(dh, x, dx_res, g_pre, name, prev=None):
    s, d = x.shape
    tm = min(TM, s)
    has_prev = prev is not None

    def body(*refs):
        if has_prev:
            dh_ref, x_ref, dr_ref, g_ref, y_ref, gp_ref, dx_ref, dg_ref, dy_ref, dgp_ref = refs
        else:
            dh_ref, x_ref, dr_ref, g_ref, dx_ref, dg_ref = refs
        i = pl.program_id(0)
        xv = x_ref[...]
        r = _rstd(xv)
        xn = xv * r
        dhv = dh_ref[...].astype(F32)
        dx = dr_ref[...] + _norm_bwd_rows(dhv * g_ref[...], xn, r)
        dx_ref[...] = dx
        dg = jnp.sum(dhv * xn, axis=0, keepdims=True)

        @pl.when(i == 0)
        def _():
            dg_ref[...] = dg

        @pl.when(i > 0)
        def _():
            dg_ref[...] += dg

        if has_prev:
            yv = y_ref[...]
            r2 = _rstd(yv)
            yn = yv * r2
            dy_ref[...] = _norm_bwd_rows(dx * gp_ref[...], yn, r2).astype(BF16)
            dgp = jnp.sum(dx * yn, axis=0, keepdims=True)

            @pl.when(i == 0)
            def _():
                dgp_ref[...] = dgp

            @pl.when(i > 0)
            def _():
                dgp_ref[...] += dgp

    in_specs = [_row_spec(tm, d), _row_spec(tm, d), _row_spec(tm, d), _vec_spec(d)]
    out_specs = [_row_spec(tm, d), _vec_spec(d)]
    out_shape = [jax.ShapeDtypeStruct((s, d), F32), jax.ShapeDtypeStruct((1, d), F32)]
    args = [dh, x, dx_res, g_pre]
    if has_prev:
        in_specs += [_row_spec(tm, d), _vec_spec(d)]
        out_specs += [_row_spec(tm, d), _vec_spec(d)]
        out_shape += [jax.ShapeDtypeStruct((s, d), BF16), jax.ShapeDtypeStruct((1, d), F32)]
        args += list(prev)
    return pl.pallas_call(
        body, name=name, grid=(s // tm,), in_specs=in_specs, out_specs=out_specs, out_shape=out_shape,
        compiler_params=_cp("arbitrary"),
    )(*args)


def _split3(v):
    hi = v.astype(BF16)
    r1 = v - hi.astype(F32)
    mid = r1.astype(BF16)
    lo = (r1 - mid.astype(F32)).astype(BF16)
    return hi, mid, lo


def _tri_dot(tri, v):
    hi, mid, lo = _split3(v)
    return _dot(tri, hi, NN) + _dot(tri, mid, NN) + _dot(tri, lo, NN)


def _gate_cumsum(fraw, b_pad, name):
    s = fraw.shape[0]
    width = HEAD_PAIRS * 128

    def body(f_ref, b_ref, flog_ref, aq_ref, ak_ref, carry_ref):
        i = pl.program_id(0)

        @pl.when(i == 0)
        def _():
            carry_ref[...] = jnp.zeros_like(carry_ref)

        flog = f_ref[...] + b_ref[...]
        flog_ref[...] = flog
        lf = jnp.minimum(flog, 0.0) - jnp.log(1.0 + jnp.exp(-jnp.abs(flog)))
        lane = lax.broadcasted_iota(jnp.int32, (1, 128), 1)
        lf = jnp.where(lane < FOX_HEADS, lf, 0.0)
        row = lax.broadcasted_iota(jnp.int32, (TM, TM), 0)
        col = lax.broadcasted_iota(jnp.int32, (TM, TM), 1)
        tri = (row >= col).astype(BF16)
        cum = _tri_dot(tri, lf) + carry_ref[...]
        carry_ref[...] = cum[TM - 1:TM, :]
        aq_ref[...], ak_ref[...] = _fox_operands(cum)

    return pl.pallas_call(
        body, name=name, grid=(s // TM,),
        in_specs=[_row_spec(TM, 128), _vec_spec(128)],
        out_specs=[_row_spec(TM, 128), _row_spec(TM, width), _row_spec(TM, width)],
        out_shape=[jax.ShapeDtypeStruct((s, 128), F32), jax.ShapeDtypeStruct((s, width), BF16),
                   jax.ShapeDtypeStruct((s, width), BF16)],
        scratch_shapes=[pltpu.VMEM((1, 128), F32)],
        compiler_params=_cp("arbitrary"),
    )(fraw, b_pad)


def _gate_bwd(qaux, kaux, flog, name):
    s = flog.shape[0]
    n = s // TM

    def body(qa_ref, ka_ref, fl_ref, dp_ref, db_ref, carry_ref):
        i = pl.program_id(0)

        @pl.when(i == 0)
        def _():
            carry_ref[...] = jnp.zeros_like(carry_ref)

        lane = lax.broadcasted_iota(jnp.int32, (1, 128), 1)
        dcum = jnp.zeros((TM, 128), F32)
        for p in range(HEAD_PAIRS):
            d = qa_ref[p] - pltpu.roll(ka_ref[p], 128 - 3, 1)
            dcum = jnp.where(lane == 2 * p, pltpu.roll(d, 64 + 2 * p, 1),
                             jnp.where(lane == 2 * p + 1, pltpu.roll(d, 2 * p + 1, 1), dcum))
        row = lax.broadcasted_iota(jnp.int32, (TM, TM), 0)
        col = lax.broadcasted_iota(jnp.int32, (TM, TM), 1)
        tri = (row <= col).astype(BF16)
        dlf = _tri_dot(tri, dcum) + carry_ref[...]
        carry_ref[...] = dlf[0:1, :]
        df = jnp.where(lane < FOX_HEADS, dlf / (1.0 + jnp.exp(fl_ref[...])), 0.0)
        dp_ref[...] = df.astype(BF16)
        db = jnp.sum(df, axis=0, keepdims=True)

        @pl.when(i == 0)
        def _():
            db_ref[...] = db

        @pl.when(i > 0)
        def _():
            db_ref[...] += db

    rev = lambda i: (n - 1 - i, 0)
    return pl.pallas_call(
        body, name=name, grid=(n,),
        in_specs=[pl.BlockSpec((HEAD_PAIRS, TM, 128), lambda i: (0, n - 1 - i, 0)),
                  pl.BlockSpec((HEAD_PAIRS, TM, 128), lambda i: (0, n - 1 - i, 0)), pl.BlockSpec((TM, 128), rev)],
        out_specs=[pl.BlockSpec((TM, 128), rev), _vec_spec(128)],
        out_shape=[jax.ShapeDtypeStruct((s, 128), BF16), jax.ShapeDtypeStruct((1, 128), F32)],
        scratch_shapes=[pltpu.VMEM((1, 128), F32)],
        compiler_params=_cp("arbitrary"),
    )(qaux, kaux, flog)


def _pool_consts(i, rows):
    lane = lax.broadcasted_iota(jnp.int32, (rows, D_POOL), 1)
    t1 = lax.broadcasted_iota(jnp.int32, (rows, D_POOL), 0) + i * TM + 1
    win = jnp.where(lane < 64, 2, jnp.where(lane < 128, 4, jnp.where(lane < 192, 8, 16)))
    inv = 1.0 / jnp.minimum(t1, win).astype(F32)
    return lane, inv


def _by_group(lane, s2, s4, s8, s16):
    return jnp.where(lane < 64, s2, jnp.where(lane < 128, s4, jnp.where(lane < 192, s8, s16)))


def _pool_diff(i, u_ref, halo_ref):
    u = u_ref[...].astype(F32)
    halo = jnp.where(i > 0, halo_ref[...].astype(F32), 0.0)
    ext = jnp.concatenate([halo, u], axis=0)
    s2 = ext + pltpu.roll(ext, 1, 0)
    s4 = s2 + pltpu.roll(s2, 2, 0)
    s8 = s4 + pltpu.roll(s4, 4, 0)
    s16 = s8 + pltpu.roll(s8, 8, 0)
    lane, inv = _pool_consts(i, TM)
    sel = _by_group(lane, s2[POOL_HALO:], s4[POOL_HALO:], s8[POOL_HALO:], s16[POOL_HALO:])
    return sel * inv - u


def _pool_fwd(proj, wbd, scale, ycat, name):
    s = proj.shape[0]
    hb = TM // POOL_HALO

    def body(u_ref, halo_ref, w_ref, sc_ref, y_any, y_ref):
        del y_any
        i = pl.program_id(0)
        diff = _pool_diff(i, u_ref, halo_ref)
        mixed = _dot(diff.astype(BF16), w_ref[...], NN)
        y_ref[...] = (mixed * sc_ref[...]).astype(BF16)

    return pl.pallas_call(
        body, name=name, grid=(s // TM,),
        in_specs=[pl.BlockSpec((TM, D_POOL), lambda i: (i, 0)),
                  pl.BlockSpec((POOL_HALO, D_POOL), lambda i: (jnp.maximum(i * hb - 1, 0), 0)),
                  pl.BlockSpec((D_POOL, D_POOL), lambda i: (0, 0)), _vec_spec(D_POOL),
                  pl.BlockSpec(memory_space=pl.ANY)],
        out_specs=pl.BlockSpec((TM, D_POOL), lambda i: (i, 0)),
        out_shape=jax.ShapeDtypeStruct(ycat.shape, ycat.dtype),
        input_output_aliases={4: 0},
        compiler_params=_cp("parallel"),
    )(proj, proj, wbd, scale, ycat)


def _pool_bwd(proj, dycat, wbd, scale, name):
    s = proj.shape[0]
    n = s // TM
    hb = TM // POOL_HALO
    last_halo = s // POOL_HALO - 1

    def body(u_ref, halo_ref, dy_ref, dyp_ref, w_ref, sc_ref, dp_ref, dw_ref, dsc_ref):
        i = pl.program_id(0)
        diff = _pool_diff(i, u_ref, halo_ref)
        diff_b = diff.astype(BF16)
        mixed = _dot(diff_b, w_ref[...], NN)
        dy = dy_ref[...].astype(F32)
        dmix = (dy * sc_ref[...]).astype(BF16)
        dyp = jnp.where(i < n - 1, dyp_ref[...].astype(F32), 0.0)
        dmix_p = (dyp * sc_ref[...]).astype(BF16)
        dd = _dot(dmix, w_ref[...], NT)
        dd_p = _dot(dmix_p, w_ref[...], NT)
        lane, inv = _pool_consts(i, TM)
        _, inv_p = _pool_consts(i + 1, POOL_HALO)
        ext = jnp.concatenate([dd * inv, dd_p * inv_p], axis=0)
        rows = TM + POOL_HALO
        l2 = ext + pltpu.roll(ext, rows - 1, 0)
        l4 = l2 + pltpu.roll(l2, rows - 2, 0)
        l8 = l4 + pltpu.roll(l4, rows - 4, 0)
        l16 = l8 + pltpu.roll(l8, rows - 8, 0)
        du = _by_group(lane, l2[:TM], l4[:TM], l8[:TM], l16[:TM]) - dd
        dp_ref[...] = du.astype(BF16)
        dw = _dot(diff_b, dmix, TN)
        dsc = jnp.sum(dy * mixed, axis=0, keepdims=True)

        @pl.when(i == 0)
        def _():
            dw_ref[...] = dw
            dsc_ref[...] = dsc

        @pl.when(i > 0)
        def _():
            dw_ref[...] += dw
            dsc_ref[...] += dsc

    return pl.pallas_call(
        body, name=name, grid=(n,),
        in_specs=[pl.BlockSpec((TM, D_POOL), lambda i: (i, 0)),
                  pl.BlockSpec((POOL_HALO, D_POOL), lambda i: (jnp.maximum(i * hb - 1, 0), 0)),
                  pl.BlockSpec((TM, D_POOL), lambda i: (i, 0)),
                  pl.BlockSpec((POOL_HALO, D_POOL), lambda i: (jnp.minimum((i + 1) * hb, last_halo), 0)),
                  pl.BlockSpec((D_POOL, D_POOL), lambda i: (0, 0)), _vec_spec(D_POOL)],
        out_specs=[pl.BlockSpec((TM, D_POOL), lambda i: (i, 0)),
                   pl.BlockSpec((D_POOL, D_POOL), lambda i: (0, 0)), _vec_spec(D_POOL)],
        out_shape=[jax.ShapeDtypeStruct((s, D_POOL), BF16),
                   jax.ShapeDtypeStruct((D_POOL, D_POOL), F32), jax.ShapeDtypeStruct((1, D_POOL), F32)],
        compiler_params=_cp("arbitrary"),
    )(proj, proj, dycat, dycat, wbd, scale)


Q_BLK = D_POOL // 128
K_BLK = Q_BLK + D_FOX // 128
V_BLK = K_BLK + D_FOX // 128
STEP_PAIRS = 2
assert Q_BLK % STEP_PAIRS == 0 and K_BLK % STEP_PAIRS == 0 and V_BLK % STEP_PAIRS == 0 and HEAD_PAIRS % STEP_PAIRS == 0


def _operand_rows(v0, v1, ones_off):
    row = lax.broadcasted_iota(jnp.int32, (128, 1), 0)
    half = row & 63
    out = jnp.where(jnp.logical_and(half >= ones_off, half < ones_off + 3), 1.0, 0.0) + jnp.zeros_like(v0)
    for base, v in ((64, v0), (0, v1)):
        for j, piece in enumerate(_split3(v)):
            out = jnp.where(row == base + j, piece.astype(F32), out)
    return out


def _fox_operands(cum):
    width = HEAD_PAIRS * 128
    hi, mid, lo = _split3(cum)
    packed = (hi.astype(F32) + pltpu.roll(mid.astype(F32), 16, 1) + pltpu.roll(lo.astype(F32), 32, 1)).astype(BF16)
    row = lax.broadcasted_iota(jnp.int32, (128, width), 0)
    col = lax.broadcasted_iota(jnp.int32, (128, width), 1)
    head, j = row & 15, row >> 4
    base = (head >> 1) * 128 + (1 - (head & 1)) * 64
    used = jnp.logical_and(head < FOX_HEADS, j < 3)
    half = lax.broadcasted_iota(jnp.int32, (1, width), 1) & 63
    res = []
    for off, sign, ones_off in ((0, 1.0, 3), (3, -1.0, 0)):
        ones = jnp.where(jnp.logical_and(half >= ones_off, half < ones_off + 3), 1.0, 0.0)
        sel = jnp.where(jnp.logical_and(col == base + off + j, used), sign, 0.0).astype(BF16)
        res.append((ones + _dot(packed, sel, NN)).astype(BF16))
    return res


def _fox_do_operand(dycat, ycat, after, name):
    s = dycat.shape[0]

    rb = 1024
    nblk = D_FOX // D_POOL

    def body(*refs):
        do_refs, o_refs, ad_ref = refs[:nblk], refs[nblk:2 * nblk], refs[-1]
        src = lax.broadcasted_iota(jnp.int32, (D_POOL, D_POOL), 0)
        dst = lax.broadcasted_iota(jnp.int32, (D_POOL, D_POOL), 1)
        same_pair = (src >> 7) == (dst >> 7)
        s_in, d_in = src & 127, dst & 127
        hit = jnp.logical_and(same_pair, jnp.logical_or(
            jnp.logical_and(s_in < 64, jnp.logical_and(d_in >= 64, d_in < 67)), jnp.logical_and(s_in >= 64, d_in < 3)))
        sel = jnp.where(hit, 1.0, 0.0).astype(BF16)
        j = lax.broadcasted_iota(jnp.int32, (1, D_POOL), 1) & 63
        for b in range(nblk):
            dd = do_refs[b][...].astype(F32) * o_refs[b][...].astype(F32)
            dsum = jnp.zeros(dd.shape, F32)
            for piece in _split3(dd):
                dsum = dsum + _dot(piece, sel, NN)
            hi, mid, lo = _split3(-dsum)
            ad_ref[:, D_POOL * b:D_POOL * (b + 1)] = jnp.where(j == 0, hi, jnp.where(j == 1, mid, lo))

    blks = [pl.BlockSpec((rb, D_POOL), functools.partial(lambda i, b: (i, 1 + b), b=b)) for b in range(nblk)]
    return pl.pallas_call(
        body, name=name, grid=(s // rb,), in_specs=blks + blks + [pl.BlockSpec(memory_space=pl.ANY)],
        out_specs=pl.BlockSpec((rb, D_FOX), lambda i: (i, 0)),
        out_shape=jax.ShapeDtypeStruct((s, D_FOX), BF16),
        compiler_params=_cp("parallel"),
    )(*[dycat] * nblk, *[ycat] * nblk, after)


def _causal_pairs(nq, key_major):
    if key_major:
        pairs = [(q, k) for k in range(nq) for q in range(k, nq)]
    else:
        pairs = [(q, k) for q in range(nq) for k in range(q + 1)]
    return (jnp.asarray([p[0] for p in pairs], jnp.int32), jnp.asarray([p[1] for p in pairs], jnp.int32))


def _fox_fwd(proj, aq, ak, name):
    s = proj.shape[0]
    nq = s // TQ
    qi_arr, ki_arr = _causal_pairs(nq, key_major=False)

    wide = STEP_PAIRS * 128
    heads = [(pp, hh) for pp in range(STEP_PAIRS) for hh in range(2)]

    def body(qi_ref, ki_ref, q_ref, k_ref, v_ref, aq_ref, ak_ref, o_ref, aqb_ref, m_ref, acc_ref, aux_ref):
        t = pl.program_id(1)
        qi, ki = qi_ref[t], ki_ref[t]
        lane = lax.broadcasted_iota(jnp.int32, (1, 128), 1)
        masks = [lane < 64, lane >= 64]
        ones_v = jnp.where((lane & 63) == 8, 1.0, 0.0).astype(BF16)
        top = lax.broadcasted_iota(jnp.int32, (128, 1), 0) < 64

        @pl.when(ki == 0)
        def _():
            m_ref[...] = jnp.full_like(m_ref, NEG)
            acc_ref[...] = jnp.zeros_like(acc_ref)
            aux_ref[...] = jnp.zeros_like(aux_ref)

        def step(diag):
            q2s = q_ref[...] * 0.125
            k2, v2, aq2, ak2 = k_ref[...], v_ref[...], aq_ref[...], ak_ref[...]

            def operand(main, lanes, pp, hh):
                cols = slice(128 * pp, 128 * (pp + 1))
                return jnp.where(masks[hh], main[:, cols], lanes[:, cols])

            scs = [_dot(operand(k2, ak2, pp, hh), operand(q2s, aq2, pp, hh), NT) for pp, hh in heads]
            ps, alpha = [], []
            for n, sc in enumerate(scs):
                if diag:
                    key = lax.broadcasted_iota(jnp.int32, sc.shape, 0)
                    qry = lax.broadcasted_iota(jnp.int32, sc.shape, 1)
                    sc = jnp.where(qry >= key, sc, NEG)
                m_prev = m_ref[n]
                m_new = jnp.maximum(m_prev, jnp.max(sc, axis=0, keepdims=True))
                m_ref[n] = m_new
                alpha.append(jnp.exp(m_prev - m_new))
                ps.append(jnp.exp(sc - m_new).astype(BF16))
            ones2 = jnp.concatenate([ones_v] * STEP_PAIRS, axis=1)
            pv = [_dot(operand(v2, ones2, pp, hh), ps[n], TN) for n, (pp, hh) in enumerate(heads)]
            for pp in range(STEP_PAIRS):
                a0, a1, pv0, pv1 = alpha[2 * pp], alpha[2 * pp + 1], pv[2 * pp], pv[2 * pp + 1]
                acc_ref[pp] = acc_ref[pp] * jnp.where(top, a0, a1) + jnp.where(top, pv0, pv1)
                aux_ref[pp] = aux_ref[pp] * jnp.where(top, a1, a0) + jnp.where(top, pv1, pv0)

        @pl.when(ki < qi)
        def _():
            step(False)

        @pl.when(ki == qi)
        def _():
            step(True)
            for pp in range(STEP_PAIRS):
                cols = slice(128 * pp, 128 * (pp + 1))
                aux = aux_ref[pp]
                l0, l1 = aux[72:73, :], aux[8:9, :]
                o_ref[:, cols] = (acc_ref[pp] * jnp.where(top, 1.0 / l0, 1.0 / l1)).T.astype(BF16)
                aqt = aq_ref[:, cols].astype(F32).T
                cum0 = aqt[64:65, :] + aqt[65:66, :] + aqt[66:67, :]
                cum1 = aqt[0:1, :] + aqt[1:2, :] + aqt[2:3, :]
                aqb = _operand_rows(cum0 - (m_ref[2 * pp] + jnp.log(l0)), cum1 - (m_ref[2 * pp + 1] + jnp.log(l1)), 3)
                aqb_ref[:, cols] = aqb.T.astype(BF16)

    grid_spec = pltpu.PrefetchScalarGridSpec(
        num_scalar_prefetch=2, grid=(HEAD_PAIRS // STEP_PAIRS, int(qi_arr.shape[0])),
        in_specs=[pl.BlockSpec((TQ, wide), lambda p, t, qi, ki: (qi[t], Q_BLK // STEP_PAIRS + p)),
                  pl.BlockSpec((TQ, wide), lambda p, t, qi, ki: (ki[t], K_BLK // STEP_PAIRS + p)),
                  pl.BlockSpec((TQ, wide), lambda p, t, qi, ki: (ki[t], V_BLK // STEP_PAIRS + p)),
                  pl.BlockSpec((TQ, wide), lambda p, t, qi, ki: (qi[t], p)),
                  pl.BlockSpec((TQ, wide), lambda p, t, qi, ki: (ki[t], p))],
        out_specs=[pl.BlockSpec((TQ, wide), lambda p, t, qi, ki: (qi[t], Q_BLK // STEP_PAIRS + p)),
                   pl.BlockSpec((TQ, wide), lambda p, t, qi, ki: (qi[t], p))],
        scratch_shapes=[pltpu.VMEM((2 * STEP_PAIRS, 1, TQ), F32),
                        pltpu.VMEM((STEP_PAIRS, 128, TQ), F32), pltpu.VMEM((STEP_PAIRS, 128, TQ), F32)])
    return pl.pallas_call(
        body, name=name, grid_spec=grid_spec,
        out_shape=[jax.ShapeDtypeStruct((s, D_MODEL), BF16), jax.ShapeDtypeStruct((s, HEAD_PAIRS * 128), BF16)],
        compiler_params=_cp("parallel", "arbitrary"),
    )(qi_arr, ki_arr, proj, proj, proj, aq, ak)


def _fox_bwd(proj, dycat, aqb, ak, ad, name):
    s = proj.shape[0]
    nq = s // TQ
    qi_arr, ki_arr = _causal_pairs(nq, key_major=True)

    def body(qi_ref, ki_ref, q_ref, k_ref, v_ref, do_ref, aq_ref, ak_ref, ad_ref,
             dq_ref, dk_ref, dv_ref, qaux_ref, kaux_ref, dq_acc, qaux_acc, dk_acc, dv_acc, kaux_acc):
        t = pl.program_id(1)
        qi, ki = qi_ref[t], ki_ref[t]
        lane = lax.broadcasted_iota(jnp.int32, (1, 128), 1)
        masks = [lane < 64, lane >= 64]
        ones_v = jnp.where((lane & 63) < 3, 1.0, 0.0).astype(BF16)
        top = lax.broadcasted_iota(jnp.int32, (128, 1), 0) < 64

        @pl.when(qi == ki)
        def _():
            dk_acc[...] = jnp.zeros_like(dk_acc)
            dv_acc[...] = jnp.zeros_like(dv_acc)
            kaux_acc[...] = jnp.zeros_like(kaux_acc)

        def step(diag):
            q2s = q_ref[...] * 0.125
            k2, v2, do2 = k_ref[...], v_ref[...], do_ref[...]
            aq2, ak2, ad2 = aq_ref[...], ak_ref[...], ad_ref[...]
            ones2 = jnp.concatenate([ones_v] * STEP_PAIRS, axis=1)
            dq_new, qaux_new = [], []
            for pp in range(STEP_PAIRS):
                cols = slice(128 * pp, 128 * (pp + 1))
                dq, dk, dv = [], [], []
                for hh in range(2):
                    qh = jnp.where(masks[hh], q2s[:, cols], aq2[:, cols])
                    kh = jnp.where(masks[hh], k2[:, cols], ak2[:, cols])
                    doh = jnp.where(masks[hh], do2[:, cols], ad2[:, cols])
                    vh = jnp.where(masks[hh], v2[:, cols], ones2[:, cols])
                    sc = _dot(kh, qh, NT)
                    if diag:
                        key = lax.broadcasted_iota(jnp.int32, sc.shape, 0)
                        qry = lax.broadcasted_iota(jnp.int32, sc.shape, 1)
                        sc = jnp.where(qry >= key, sc, NEG)
                    p = jnp.exp(sc)
                    dsb = (p * _dot(vh, doh, NT)).astype(BF16)
                    dv.append(_dot(p.astype(BF16), doh, NN))
                    dk.append(_dot(dsb, qh, NN))
                    dq.append(_dot(kh, dsb, TN))
                dk_acc[pp] += jnp.where(masks[0], dk[0], dk[1])
                kaux_acc[pp] += jnp.where(masks[0], dk[1], dk[0])
                dv_acc[pp] += jnp.where(masks[0], dv[0], dv[1])
                dq_new.append(jnp.where(top, dq[0], dq[1]))
                qaux_new.append(jnp.where(top, dq[1], dq[0]))

            @pl.when(ki == 0)
            def _():
                for pp in range(STEP_PAIRS):
                    dq_acc[pp * nq + qi] = dq_new[pp]
                    qaux_acc[pp * nq + qi] = qaux_new[pp]

            @pl.when(ki > 0)
            def _():
                for pp in range(STEP_PAIRS):
                    dq_acc[pp * nq + qi] += dq_new[pp]
                    qaux_acc[pp * nq + qi] += qaux_new[pp]

        @pl.when(qi > ki)
        def _():
            step(False)

        @pl.when(qi == ki)
        def _():
            step(True)
            rows = pl.ds(pl.multiple_of(qi * TQ, TQ), TQ)
            for pp in range(STEP_PAIRS):
                dq_ref[rows, 128 * pp:128 * (pp + 1)] = (dq_acc[pp * nq + qi] * 0.125).T.astype(BF16)
                qaux_ref[pp, rows, :] = qaux_acc[pp * nq + qi].T

        @pl.when(qi == nq - 1)
        def _():
            for pp in range(STEP_PAIRS):
                dk_ref[:, 128 * pp:128 * (pp + 1)] = dk_acc[pp].astype(BF16)
                dv_ref[:, 128 * pp:128 * (pp + 1)] = dv_acc[pp].astype(BF16)
            kaux_ref[...] = kaux_acc[...]

    wide = STEP_PAIRS * 128
    grid_spec = pltpu.PrefetchScalarGridSpec(
        num_scalar_prefetch=2, grid=(HEAD_PAIRS // STEP_PAIRS, int(qi_arr.shape[0])),
        in_specs=[pl.BlockSpec((TQ, wide), lambda p, t, qi, ki: (qi[t], Q_BLK // STEP_PAIRS + p)),
                  pl.BlockSpec((TQ, wide), lambda p, t, qi, ki: (ki[t], K_BLK // STEP_PAIRS + p)),
                  pl.BlockSpec((TQ, wide), lambda p, t, qi, ki: (ki[t], V_BLK // STEP_PAIRS + p)),
                  pl.BlockSpec((TQ, wide), lambda p, t, qi, ki: (qi[t], Q_BLK // STEP_PAIRS + p)),
                  pl.BlockSpec((TQ, wide), lambda p, t, qi, ki: (qi[t], p)),
                  pl.BlockSpec((TQ, wide), lambda p, t, qi, ki: (ki[t], p)),
                  pl.BlockSpec((TQ, wide), lambda p, t, qi, ki: (qi[t], p))],
        out_specs=[pl.BlockSpec((s, wide), lambda p, t, qi, ki: (0, p)),
                   pl.BlockSpec((TQ, wide), lambda p, t, qi, ki: (ki[t], p)),
                   pl.BlockSpec((TQ, wide), lambda p, t, qi, ki: (ki[t], p)),
                   pl.BlockSpec((STEP_PAIRS, s, 128), lambda p, t, qi, ki: (p, 0, 0)),
                   pl.BlockSpec((STEP_PAIRS, TQ, 128), lambda p, t, qi, ki: (p, ki[t], 0))],
        scratch_shapes=[pltpu.VMEM((STEP_PAIRS * nq, 128, TQ), F32), pltpu.VMEM((STEP_PAIRS * nq, 128, TQ), F32),
                        pltpu.VMEM((STEP_PAIRS, TQ, 128), F32), pltpu.VMEM((STEP_PAIRS, TQ, 128), F32),
                        pltpu.VMEM((STEP_PAIRS, TQ, 128), F32)])
    return pl.pallas_call(
        body, name=name, grid_spec=grid_spec,
        out_shape=[jax.ShapeDtypeStruct((s, D_FOX), BF16)] * 3 + [jax.ShapeDtypeStruct((HEAD_PAIRS, s, 128), F32)] * 2,
        compiler_params=_cp("arbitrary", "arbitrary"),
    )(qi_arr, ki_arr, proj, proj, proj, dycat, aqb, ak, ad)


XA_SCALE = XA_DIM ** -0.5


def _xattn_fwd(q2, kv, name):
    s = q2.shape[0]
    m = kv.shape[0]

    def body(q_ref, kv_ref, o_ref):
        heads = [slice(h * XA_DIM, (h + 1) * XA_DIM) for h in range(XA_HEADS)]
        scs = [_dot(q_ref[:, cols], kv_ref[:, cols], NT) for cols in heads]
        for h in range(XA_HEADS):
            c0 = h * XA_DIM
            sc = scs[h] * XA_SCALE
            e = jnp.exp(sc - jnp.max(sc, axis=1, keepdims=True))
            p = e / jnp.sum(e, axis=1, keepdims=True)
            o_ref[:, c0:c0 + XA_DIM] = _dot(p.astype(BF16), kv_ref[:, D_MODEL + c0:D_MODEL + c0 + XA_DIM], NN).astype(BF16)

    return pl.pallas_call(
        body, name=name, grid=(s // TM,),
        in_specs=[_row_spec(TM, D_MODEL), pl.BlockSpec((m, 2 * D_MODEL), lambda i: (0, 0))],
        out_specs=_row_spec(TM, D_MODEL), out_shape=jax.ShapeDtypeStruct((s, D_MODEL), BF16),
        compiler_params=_cp("parallel"),
    )(q2, kv)


def _xattn_bwd(q2, kv, do2, name):
    s = q2.shape[0]
    m = kv.shape[0]

    def body(q_ref, kv_ref, do_ref, dq_ref, dkv_ref):
        i = pl.program_id(0)

        @pl.when(i == 0)
        def _():
            dkv_ref[...] = jnp.zeros_like(dkv_ref)

        heads = [slice(h * XA_DIM, (h + 1) * XA_DIM) for h in range(XA_HEADS)]
        scs = [_dot(kv_ref[:, cols], q_ref[:, cols], NT) for cols in heads]
        dps = [_dot(kv_ref[:, D_MODEL + cols.start:D_MODEL + cols.stop], do_ref[:, cols], NT) for cols in heads]
        for h in range(XA_HEADS):
            c0 = h * XA_DIM
            v0 = D_MODEL + c0
            qh = q_ref[:, c0:c0 + XA_DIM]
            kh = kv_ref[:, c0:c0 + XA_DIM]
            doh = do_ref[:, c0:c0 + XA_DIM]
            sc = scs[h] * XA_SCALE
            e = jnp.exp(sc - jnp.max(sc, axis=0, keepdims=True))
            p = e / jnp.sum(e, axis=0, keepdims=True)
            dp = dps[h]
            ds = p * (dp - jnp.sum(p * dp, axis=0, keepdims=True))
            dsb = (ds * XA_SCALE).astype(BF16)
            dq_ref[:, c0:c0 + XA_DIM] = _dot(kh, dsb, TN).T.astype(BF16)
            dkv_ref[:, c0:c0 + XA_DIM] += _dot(dsb, qh, NN)
            dkv_ref[:, v0:v0 + XA_DIM] += _dot(p.astype(BF16), doh, NN)

    return pl.pallas_call(
        body, name=name, grid=(s // TM,),
        in_specs=[_row_spec(TM, D_MODEL), pl.BlockSpec((m, 2 * D_MODEL), lambda i: (0, 0)), _row_spec(TM, D_MODEL)],
        out_specs=[_row_spec(TM, D_MODEL), pl.BlockSpec((m, 2 * D_MODEL), lambda i: (0, 0))],
        out_shape=[jax.ShapeDtypeStruct((s, D_MODEL), BF16), jax.ShapeDtypeStruct((m, 2 * D_MODEL), F32)],
        compiler_params=_cp("arbitrary"),
    )(q2, kv, do2)


GELU_C = math.sqrt(2.0 / math.pi)
GELU_A = 0.044715


def _gelu(x):
    return (0.5 * x) * (1.0 + jnp.tanh(x * (GELU_C * GELU_A * (x * x) + GELU_C)))


def _gelu_and_grad(x):
    x2 = x * x
    s = 1.0 + jnp.tanh(x * (GELU_C * GELU_A * x2 + GELU_C))
    hx = 0.5 * x
    return hx * s, s * (0.5 + hx * (2.0 - s) * (3.0 * GELU_C * GELU_A * x2 + GELU_C))


def _conv(h, s1, s2, w_ref, b_ref):
    return w_ref[0:1, :] * s2 + w_ref[1:2, :] * s1 + w_ref[2:3, :] * h + b_ref[...]


def _shift_down(main, prev8):
    row = lax.broadcasted_iota(jnp.int32, main.shape, 0)
    s1 = jnp.where(row == 0, prev8[7:8, :], pltpu.roll(main, 1, 0))
    s2 = jnp.where(row == 0, prev8[6:7, :], jnp.where(row == 1, prev8[7:8, :], pltpu.roll(main, 2, 0)))
    return s1, s2


def _shift_up(main, next8):
    n = main.shape[0]
    row = lax.broadcasted_iota(jnp.int32, main.shape, 0)
    u1 = jnp.where(row == n - 1, next8[0:1, :], pltpu.roll(main, n - 1, 0))
    u2 = jnp.where(row == n - 2, next8[0:1, :], jnp.where(row == n - 1, next8[1:2, :], pltpu.roll(main, n - 2, 0)))
    return u1, u2


def _ffn_fwd(h3, w_up, cw, cb, w_down, x2, tgt, g_post, name):
    s = h3.shape[0]
    tn = TN_FF
    nj = D_FF // tn
    per = D_MODEL // tn
    hb = TM // 8

    def body(h_ref, halo_ref, wg_ref, wu_ref, cwg_ref, cwu_ref, cbg_ref, cbu_ref, wd_ref, x_ref, t_ref, g_ref,
             hg_ref, hu_ref, cg_ref, cu_ref, a_ref, loss_ref, dx_ref, dy_ref, dg_ref, y_acc):
        i, j = pl.program_id(0), pl.program_id(1)
        h = h_ref[...]
        halo = halo_ref[...]
        halo = jnp.where(i > 0, halo, jnp.zeros_like(halo))
        hid = [(_dot(h, w_ref[...], NN), _dot(halo, w_ref[...], NN)) for w_ref in (wg_ref, wu_ref)]
        conv = []
        for (hm, hm_halo), cw_ref, cb_ref, hid_ref, c_ref in zip(hid, (cwg_ref, cwu_ref), (cbg_ref, cbu_ref),
                                                                  (hg_ref, hu_ref), (cg_ref, cu_ref)):
            hid_ref[...] = hm.astype(BF16)
            s1, s2 = _shift_down(hm, hm_halo)
            c = _conv(hm, s1, s2, cw_ref, cb_ref)
            c_ref[...] = c.astype(BF16)
            conv.append(c)
        a = (_gelu(conv[0]) * conv[1]).astype(BF16)
        a_ref[...] = a
        contrib = _dot(a, wd_ref[...], NN)

        @pl.when(j == 0)
        def _():
            y_acc[...] = contrib

        @pl.when(j > 0)
        def _():
            y_acc[...] += contrib

        @pl.when(j == nj - 1)
        def _():
            yv = y_acc[...]
            r = _rstd(yv)
            yn = yv * r
            e = x_ref[...] + yn * g_ref[...] - t_ref[...]
            part = 0.5 * jnp.sum(jnp.mean(e * e, axis=-1, keepdims=True), axis=0, keepdims=True)
            part = jnp.broadcast_to(part, (1, 128))
            dx = e * (1.0 / D_MODEL)
            dx_ref[...] = dx
            dy_ref[...] = _norm_bwd_rows(dx * g_ref[...], yn, r).astype(BF16)
            dg = jnp.sum(dx * yn, axis=0, keepdims=True)

            @pl.when(i == 0)
            def _():
                dg_ref[...] = dg
                loss_ref[...] = part

            @pl.when(i > 0)
            def _():
                dg_ref[...] += dg
                loss_ref[...] += part

    rows = pl.BlockSpec((TM, D_MODEL), lambda i, j: (i, 0))
    tile = pl.BlockSpec((TM, tn), lambda i, j: (i, j))
    wide = jax.ShapeDtypeStruct((s, D_FF), BF16)
    return pl.pallas_call(
        body, name=name, grid=(s // TM, nj),
        in_specs=[rows,
                  pl.BlockSpec((8, D_MODEL), lambda i, j: (jnp.maximum(i * hb - 1, 0), 0)),
                  pl.BlockSpec((None, D_MODEL, tn), lambda i, j: (j // per, 0, j % per)),
                  pl.BlockSpec((None, D_MODEL, tn), lambda i, j: (NDEV // 2 + j // per, 0, j % per)),
                  pl.BlockSpec((8, tn), lambda i, j: (0, j)),
                  pl.BlockSpec((8, tn), lambda i, j: (0, nj + j)),
                  pl.BlockSpec((1, tn), lambda i, j: (0, j)),
                  pl.BlockSpec((1, tn), lambda i, j: (0, nj + j)),
                  pl.BlockSpec((tn, D_MODEL), lambda i, j: (j, 0)),
                  rows, rows, pl.BlockSpec((1, D_MODEL), lambda i, j: (0, 0))],
        out_specs=[tile, tile, tile, tile, tile,
                   pl.BlockSpec((1, 128), lambda i, j: (0, 0)), rows, rows,
                   pl.BlockSpec((1, D_MODEL), lambda i, j: (0, 0))],
        out_shape=[wide, wide, wide, wide, wide,
                   jax.ShapeDtypeStruct((1, 128), F32), jax.ShapeDtypeStruct((s, D_MODEL), F32),
                   jax.ShapeDtypeStruct((s, D_MODEL), BF16), jax.ShapeDtypeStruct((1, D_MODEL), F32)],
        scratch_shapes=[pltpu.VMEM((TM, D_MODEL), F32)],
        compiler_params=_cp("arbitrary", "arbitrary"),
    )(h3, h3, w_up, w_up, cw, cw, cb, cb, w_down, x2, tgt, g_post)


def _ffn_bwd(dy3, w_down, hid_g, hid_u, conv_g, conv_u, cw, name):
    s = dy3.shape[0]
    n = s // TM
    tn = TN_FF
    nj = D_FF // tn
    hb = TM // 8
    last8 = s // 8 - 1

    def body(dy_ref, dyn_ref, wd_ref, hg_ref, hu_ref, cg_ref, cgn_ref, cu_ref, cun_ref, cwg_ref, cwu_ref,
             dhg_ref, dhu_ref, dcwg_ref, dcwu_ref, dcbg_ref, dcbu_ref):
        i = pl.program_id(1)
        first, last = i == 0, i == n - 1
        da = _dot(dy_ref[...], wd_ref[...], NT)
        dyn = dyn_ref[...]
        dyn = jnp.where(last, jnp.zeros_like(dyn), dyn)
        da_n = _dot(dyn, wd_ref[...], NT)
        c_g, c_u = cg_ref[...].astype(F32), cu_ref[...].astype(F32)
        g, dg = _gelu_and_grad(c_g)
        gn, dgn = _gelu_and_grad(cgn_ref[...].astype(F32))
        outs = ((da * c_u * dg, da_n * cun_ref[...].astype(F32) * dgn, hg_ref, cwg_ref, dhg_ref, dcwg_ref, dcbg_ref),
                (da * g, da_n * gn, hu_ref, cwu_ref, dhu_ref, dcwu_ref, dcbu_ref))
        row8 = lax.broadcasted_iota(jnp.int32, (8, tn), 0)
        for dc, dcn, h_ref, cw_ref, dh_ref, dcw_ref, dcb_ref in outs:
            u1, u2 = _shift_up(dc, dcn)
            dh_ref[...] = (cw_ref[2:3, :] * dc + cw_ref[1:2, :] * u1 + cw_ref[0:1, :] * u2).astype(BF16)
            hm = h_ref[...].astype(F32)
            dcb = jnp.sum(dc, axis=0, keepdims=True)
            dcw = jnp.where(row8 == 0, jnp.sum(hm * u2, axis=0, keepdims=True),
                            jnp.where(row8 == 1, jnp.sum(hm * u1, axis=0, keepdims=True),
                                      jnp.where(row8 == 2, jnp.sum(hm * dc, axis=0, keepdims=True), 0.0)))

            @pl.when(first)
            def _():
                dcw_ref[...] = dcw
                dcb_ref[...] = dcb

            @pl.when(i > 0)
            def _():
                dcw_ref[...] += dcw
                dcb_ref[...] += dcb

    next8 = lambda j, i: (jnp.minimum((i + 1) * hb, last8), j)
    blk = lambda j, i: (i, j)
    col = lambda j, i: (0, j)
    colu = lambda j, i: (0, nj + j)
    tile = pl.BlockSpec((TM, tn), blk)
    return pl.pallas_call(
        body, name=name, grid=(nj, n),
        in_specs=[pl.BlockSpec((TM, D_MODEL), lambda j, i: (i, 0)),
                  pl.BlockSpec((8, D_MODEL), lambda j, i: (jnp.minimum((i + 1) * hb, last8), 0)),
                  pl.BlockSpec((tn, D_MODEL), lambda j, i: (j, 0)),
                  tile, tile, tile, pl.BlockSpec((8, tn), next8), tile, pl.BlockSpec((8, tn), next8),
                  pl.BlockSpec((8, tn), col), pl.BlockSpec((8, tn), colu)],
        out_specs=[tile, tile, pl.BlockSpec((8, tn), col), pl.BlockSpec((8, tn), col),
                   pl.BlockSpec((1, tn), col), pl.BlockSpec((1, tn), col)],
        out_shape=[jax.ShapeDtypeStruct((s, D_FF), BF16), jax.ShapeDtypeStruct((s, D_FF), BF16),
                   jax.ShapeDtypeStruct((8, D_FF), F32), jax.ShapeDtypeStruct((8, D_FF), F32),
                   jax.ShapeDtypeStruct((1, D_FF), F32), jax.ShapeDtypeStruct((1, D_FF), F32)],
        compiler_params=_cp("parallel", "arbitrary"),
    )(dy3, dy3, w_down, hid_g, hid_u, conv_g, conv_g, conv_u, conv_u, cw, cw)


def _slot(p):
    return 4 * p[0] + 2 * p[1] + p[2]


def _all_gather(shards, name):
    n = len(shards)

    def body(*refs):
        ins, outs = refs[:n], refs[n:2 * n]
        send_sems, recv_sems, local_sems = refs[2 * n:]
        x, y, c = lax.axis_index("x"), lax.axis_index("y"), lax.axis_index("c")
        me, sibling = (x, y, c), (x, y, 1 - c)
        chips = [(1 - x, y), (x, 1 - y), (1 - x, 1 - y)]

        def copy(a, k, block, to, from_input=False):
            dst = outs[a].at[_slot(block)]
            return pltpu.make_async_remote_copy(
                src_ref=ins[a] if from_input else dst, dst_ref=dst,
                send_sem=send_sems.at[a, k], recv_sem=recv_sems.at[a, k],
                device_id=to, device_id_type=MESH)

        mine = [pltpu.make_async_copy(ins[a], outs[a].at[_slot(me)], local_sems.at[a]) for a in range(n)]
        for cp in mine:
            cp.start()
        first = []
        for a in range(n):
            first.append(copy(a, 0, me, sibling, True))
            first += [copy(a, 1 + j, me, (*chip, c), True) for j, chip in enumerate(chips)]
        for cp in first:
            cp.start()
        passed = []
        for j, chip in enumerate(chips):
            for a in range(n):
                copy(a, 1 + j, (*chip, c), me).wait_recv()
                fwd = copy(a, 4 + j, (*chip, c), sibling)
                fwd.start()
                passed.append(fwd)
        for a in range(n):
            copy(a, 0, sibling, me).wait_recv()
            for j, chip in enumerate(chips):
                copy(a, 4 + j, (*chip, 1 - c), me).wait_recv()
        for cp in first + passed:
            cp.wait_send()
        for cp in mine:
            cp.wait()

    any_spec = pl.BlockSpec(memory_space=pl.ANY)
    return pl.pallas_call(
        body, name=name,
        in_specs=[any_spec] * n, out_specs=[any_spec] * n,
        out_shape=[jax.ShapeDtypeStruct((NDEV,) + s.shape, s.dtype) for s in shards],
        scratch_shapes=[pltpu.SemaphoreType.DMA((n, 7)), pltpu.SemaphoreType.DMA((n, 7)),
                        pltpu.SemaphoreType.DMA((n,))],
    )(*shards)


def _peer_list(x, y, c):
    return [(1 - x if m & 4 else x, 1 - y if m & 2 else y, 1 - c if m & 1 else c) for m in range(1, NDEV)]


def _exchange_copies(src_refs, land_refs, send_sems, recv_sems, gather):
    x, y, c = lax.axis_index("x"), lax.axis_index("y"), lax.axis_index("c")
    me = (x, y, c)
    copies = []
    for m, peer in enumerate(_peer_list(x, y, c)):
        for a in range(len(src_refs)):
            copies.append(pltpu.make_async_remote_copy(
                src_ref=src_refs[a] if gather else src_refs[a].at[_slot(peer)], dst_ref=land_refs[a].at[_slot(me)],
                send_sem=send_sems.at[a * (NDEV - 1) + m], recv_sem=recv_sems.at[a * (NDEV - 1) + m],
                device_id=peer, device_id_type=MESH))
    return copies


def _all_gather_small(shards, name):
    n = len(shards)

    def body(*refs):
        ins, outs = refs[:n], refs[n:2 * n]
        send_sems, recv_sems, local_sems = refs[2 * n:]
        me = (lax.axis_index("x"), lax.axis_index("y"), lax.axis_index("c"))
        mine = [pltpu.make_async_copy(ins[a], outs[a].at[_slot(me)], local_sems.at[a]) for a in range(n)]
        copies = _exchange_copies(ins, outs, send_sems, recv_sems, True)
        for cp in mine + copies:
            cp.start()
        for cp in copies + mine:
            cp.wait()

    any_spec = pl.BlockSpec(memory_space=pl.ANY)
    return pl.pallas_call(
        body, name=name,
        in_specs=[any_spec] * n, out_specs=[any_spec] * n,
        out_shape=[jax.ShapeDtypeStruct((NDEV,) + s.shape, s.dtype) for s in shards],
        scratch_shapes=[pltpu.SemaphoreType.DMA((n * (NDEV - 1),)), pltpu.SemaphoreType.DMA((n * (NDEV - 1),)),
                        pltpu.SemaphoreType.DMA((n,))],
    )(*shards)


def _exchange_start(srcs, lands, after, gather, name):
    n = len(srcs)
    hbm = pl.BlockSpec(memory_space=pltpu.HBM)

    def body(*refs):
        for cp in _exchange_copies(refs[:n], refs[n:2 * n], refs[2 * n + 1], refs[2 * n + 2], gather):
            cp.start()
        token = refs[-1]
        token[...] = jnp.zeros_like(token)

    outs = pl.pallas_call(
        body, name=name,
        out_shape=(pltpu.SemaphoreType.DMA((n * (NDEV - 1),)), pltpu.SemaphoreType.DMA((n * (NDEV - 1),)),
                   *[pltpu.HBM(a.shape, a.dtype) for a in list(srcs) + list(lands)],
                   jax.ShapeDtypeStruct((8, 128), F32)),
        in_specs=[hbm] * (2 * n) + [pl.BlockSpec(memory_space=pl.ANY)],
        out_specs=(pl.BlockSpec(memory_space=pltpu.SEMAPHORE), pl.BlockSpec(memory_space=pltpu.SEMAPHORE),
                   *[hbm] * (2 * n), pl.BlockSpec(memory_space=pltpu.VMEM)),
        input_output_aliases={i: 2 + i for i in range(2 * n)},
        compiler_params=pltpu.CompilerParams(has_side_effects=pltpu.SideEffectType.DATAFLOW_SIDE_EFFECTING),
    )(*[pltpu.with_memory_space_constraint(a, pltpu.HBM) for a in list(srcs) + list(lands)], after)
    return outs[0], outs[1], outs[2:2 + n], outs[2 + n:2 + 2 * n], outs[-1]


def _exchange_wait(send_sems, recv_sems, srcs, lands, after, gather, name):
    n = len(srcs)
    hbm = pl.BlockSpec(memory_space=pltpu.HBM)

    def body(*refs):
        for cp in _exchange_copies(refs[:n], refs[n:2 * n], refs[2 * n], refs[2 * n + 1], gather):
            cp.wait_send()
            cp.wait_recv()

    outs = pl.pallas_call(
        body, name=name,
        out_shape=tuple(pltpu.HBM(a.shape, a.dtype) for a in list(srcs) + list(lands)),
        in_specs=[hbm] * (2 * n) + [pl.BlockSpec(memory_space=pltpu.SEMAPHORE)] * 2 + [pl.BlockSpec(memory_space=pl.ANY)],
        out_specs=tuple([hbm] * (2 * n)),
        input_output_aliases={i: i for i in range(2 * n)},
        compiler_params=pltpu.CompilerParams(has_side_effects=pltpu.SideEffectType.DATAFLOW_SIDE_EFFECTING),
    )(*srcs, *lands, send_sems, recv_sems, after)
    return outs[n:]


def _own_slot(block):
    me = 4 * lax.axis_index("x") + 2 * lax.axis_index("y") + lax.axis_index("c")
    return lax.dynamic_update_slice(lax.empty((NDEV,) + block.shape, block.dtype), block[None], (me, 0, 0))


def _adam_update(p_ref, w_ref, m_ref, v_ref, g_ref, d_ref, mo_ref, vo_ref):
    bc1 = 1.0 - ADAM_B1 ** ADAM_STEP
    bc2 = 1.0 - ADAM_B2 ** ADAM_STEP
    g = p_ref[0].astype(F32)
    for d in range(1, NDEV):
        g = g + p_ref[d].astype(F32)
    g_ref[...] = g
    mn = ADAM_B1 * m_ref[...] + (1.0 - ADAM_B1) * g
    vn = ADAM_B2 * v_ref[...] + (1.0 - ADAM_B2) * (g * g)
    mo_ref[...] = mn
    vo_ref[...] = vn
    d_ref[...] = -ADAM_LR * ((mn / bc1) / (jnp.sqrt(vn / bc2) + ADAM_EPS) + ADAM_WD * w_ref[...])


def _adamw_small(parts, ws, ms, vs, name):
    n = len(ws)

    def body(*refs):
        ins, outs = refs[:4 * n], refs[4 * n:]
        for k in range(n):
            _adam_update(ins[k], ins[n + k], ins[2 * n + k], ins[3 * n + k], *outs[4 * k:4 * k + 4])

    whole = pl.BlockSpec(memory_space=pltpu.VMEM)
    res = pl.pallas_call(
        body, name=name, in_specs=[whole] * (4 * n), out_specs=[whole] * (4 * n),
        out_shape=[jax.ShapeDtypeStruct(a.shape, F32) for a in ws for _ in range(4)],
    )(*parts, *ws, *ms, *vs)
    return [res[4 * k:4 * k + 4] for k in range(n)]


def _adamw(parts, w, m, v, name):
    r, c = w.shape
    tr = r if r * c <= 160 * 1024 else max(8, (160 * 1024 // c) // 8 * 8)
    while r % tr:
        tr -= 8
    body = functools.partial(_adam_update)
    spec = pl.BlockSpec((tr, c), lambda i: (i, 0))
    return pl.pallas_call(
        body, name=name, grid=(r // tr,),
        in_specs=[pl.BlockSpec((NDEV, tr, c), lambda i: (0, i, 0)), spec, spec, spec],
        out_specs=[spec] * 4, out_shape=[jax.ShapeDtypeStruct((r, c), F32)] * 4,
        compiler_params=_cp("parallel"),
    )(parts, w, m, v)


def _local_step(x, mem, tgt, gains, b_forget, w_pool, pool_scale, conv_b, w_in,
                mix_weights, ffn_weights, send_in_grad, send_mix_grads, send_ffn_grads):
    b_pad = jnp.pad(b_forget, ((0, 0), (0, 128 - FOX_HEADS)))
    wbd = jnp.zeros((D_POOL, D_POOL), F32)
    for g in range(4):
        wbd = wbd.at[64 * g:64 * g + 64, 64 * g:64 * g + 64].set(w_pool[g])
    wbd = wbd.astype(BF16)
    scale = pool_scale.reshape(1, D_POOL)

    h1, proj, fraw = _proj_in(x, gains["mix_pre"], w_in, "proj_in")
    flog, aq, ak = _gate_cumsum(fraw, b_pad, "gate_cumsum")
    ycat, aqb = _fox_fwd(proj, aq, ak, "fox_fwd")
    ycat = _pool_fwd(proj, wbd, scale, ycat, "pool_fwd")
    w_mix, w_xq, w_xo, w_xkv = mix_weights(ycat)
    y1, x1, h2 = _mm_rows(ycat, w_mix, "nn", 1024, "mix_out", [x], [gains["mix_post"], gains["xa_pre"]],
                          [F32, F32, BF16], _epi_resid)
    q2 = _mm(h2, w_xq, "nn", BF16, 2048, 1024, 1024, "xa_q")
    mem_n = _norm_fwd(mem, gains["mem"], "norm_mem")
    kv = _mm(mem_n, w_xkv, "nn", BF16, mem.shape[0], 256, 1024, "xa_kv", b_cols=256)
    o2 = _xattn_fwd(q2, kv, "xattn_fwd")
    y2, x2, h3 = _mm_rows(o2, w_xo, "nn", 1024, "xa_out", [x1], [gains["xa_post"], gains["ffn_pre"]],
                          [F32, F32, BF16], _epi_resid)
    w_up, w_down, cw = ffn_weights(h3)
    hid_g, hid_u, conv_g, conv_u, act, loss, dx3, dy3, dg_ffn_post = _ffn_fwd(
        h3, w_up, cw, conv_b, w_down, x2, tgt, gains["ffn_post"], "ffn_fwd")

    dhid_g, dhid_u, dcw_g, dcw_u, dcb_g, dcb_u = _ffn_bwd(dy3, w_down, hid_g, hid_u, conv_g, conv_u, cw, "ffn_bwd")
    d_w_down = _mm(act, dy3, "tn", BF16, 2048, 1024, 1024, "dw_down")
    d_w_up = _mm(h3, [dhid_g, dhid_u], "tn", BF16, 1024, 1024, 2048, "dw_up", out_cols=1024)
    sent = send_ffn_grads(d_w_up, d_w_down, jnp.concatenate([dcw_g, dcw_u], axis=1))
    dh3 = _mm([dhid_g, dhid_u], w_up, "nt", F32, 2048, 1024, 1024, "dh_ffn", b_cols=1024, after=sent)
    dx2, dg_ffn_pre, dy2, dg_xa_post = _norm_bwd(dh3, x2, dx3, gains["ffn_pre"], "norm_bwd_ffn",
                                                 prev=(y2, gains["xa_post"]))
    do2 = _mm(dy2, w_xo, "nt", BF16, 2048, 1024, 1024, "d_xa_out")
    d_w_xo = _mm(o2, dy2, "tn", BF16, 1024, 1024, 1024, "dw_xo")
    dq2, dkv = _xattn_bwd(q2, kv, do2, "xattn_bwd")
    dkv = dkv.astype(BF16)
    dx1, dg_xa_pre, dy1, dg_mix_post = _mm_rows(
        dq2, w_xq, "nt", 1024, "dh_xa", [x1, dx2, y1], [gains["xa_pre"], gains["mix_post"]],
        [F32, "sum", BF16, "sum"], _epi_norm_bwd)
    d_w_xq = _mm(h2, dq2, "tn", BF16, 1024, 1024, 1024, "dw_xq")
    dmem_n = _mm(dkv, w_xkv, "nt", F32, mem.shape[0], 1024, 256, "d_mem", b_cols=256)
    d_w_xkv = _mm(mem_n, dkv, "tn", BF16, 1024, 256, mem.shape[0], "dw_xkv", out_cols=256)
    _, dg_mem = _norm_bwd(dmem_n, mem, jnp.zeros_like(mem), gains["mem"], "norm_bwd_mem")
    dycat = _mm(dy1, w_mix, "nt", BF16, 2048, 1024, 1024, "d_mix_out")
    d_w_mix = _mm(ycat, dy1, "tn", BF16, 1024, 1024, 1024, "dw_mix")
    sent_mix = send_mix_grads(d_w_mix, d_w_xq, d_w_xo, d_w_xkv)
    ad = _fox_do_operand(dycat, ycat, sent_mix, "fox_do_operand")
    dq, dk, dv, qaux, kaux = _fox_bwd(proj, dycat, aqb, ak, ad, "fox_bwd")
    du, d_wbd, d_scale = _pool_bwd(proj, dycat, wbd, scale, "pool_bwd")
    df, db_f = _gate_bwd(qaux, kaux, flog, "gate_bwd")
    dproj = [du, dq, dk, dv, df]
    sent_in = send_in_grad(_dw_in(h1, dproj, "dw_in"))
    grad_x, dg_mix_pre = _mm_rows(dproj, w_in, "nt", None, "dh_mix", [x, dx1], [gains["mix_pre"]],
                                  [F32, "sum"], _epi_norm_bwd, after=sent_in)

    small = dict(
        mix_pre=dg_mix_pre, mix_post=dg_mix_post, mem=dg_mem, xa_pre=dg_xa_pre, xa_post=dg_xa_post,
        ffn_pre=dg_ffn_pre, ffn_post=dg_ffn_post,
        conv_b=jnp.concatenate([dcb_g, dcb_u], axis=1),
        w_pool=jnp.concatenate([d_wbd[64 * g:64 * g + 64, 64 * g:64 * g + 64] for g in range(4)], axis=0),
        pool_scale=d_scale.reshape(4, 64),
        b_forget=db_f[:, :FOX_HEADS],
    )
    return loss, grad_x, small


SMALL_ORDER = ("mix_pre", "mix_post", "mem", "xa_pre", "xa_post", "ffn_pre", "ffn_post", "conv_b",
               "w_pool", "pool_scale", "b_forget")


def kernel(x, mem, norm_mix_pre, norm_mix_post, w_in, b_forget, w_pool, pool_scale, w_mix_out, norm_mem, norm_xa_pre, norm_xa_post, w_xq, w_xkv, w_xo, norm_ffn_pre, norm_ffn_post, w_up, conv_w, conv_b, w_down, loss_target, m_norm_mix_pre, m_norm_mix_post, m_w_in, m_b_forget, m_w_pool, m_pool_scale, m_w_mix_out, m_norm_mem, m_norm_xa_pre, m_norm_xa_post, m_w_xq, m_w_xkv, m_w_xo, m_norm_ffn_pre, m_norm_ffn_post, m_w_up, m_conv_w, m_conv_b, m_w_down, v_norm_mix_pre, v_norm_mix_post, v_w_in, v_b_forget, v_w_pool, v_pool_scale, v_w_mix_out, v_norm_mem, v_norm_xa_pre, v_norm_xa_post, v_w_xq, v_w_xkv, v_w_xo, v_norm_ffn_pre, v_norm_ffn_post, v_w_up, v_conv_w, v_conv_b, v_w_down):
    names = ("norm_mix_pre", "norm_mix_post", "w_in", "b_forget", "w_pool", "pool_scale", "w_mix_out", "norm_mem",
             "norm_xa_pre", "norm_xa_post", "w_xq", "w_xkv", "w_xo", "norm_ffn_pre", "norm_ffn_post", "w_up",
             "conv_w", "conv_b", "w_down")
    w = dict(zip(names, (norm_mix_pre, norm_mix_post, w_in, b_forget, w_pool, pool_scale, w_mix_out, norm_mem,
                         norm_xa_pre, norm_xa_post, w_xq, w_xkv, w_xo, norm_ffn_pre, norm_ffn_post, w_up,
                         conv_w, conv_b, w_down)))
    mo = dict(zip(names, (m_norm_mix_pre, m_norm_mix_post, m_w_in, m_b_forget, m_w_pool, m_pool_scale, m_w_mix_out,
                          m_norm_mem, m_norm_xa_pre, m_norm_xa_post, m_w_xq, m_w_xkv, m_w_xo, m_norm_ffn_pre,
                          m_norm_ffn_post, m_w_up, m_conv_w, m_conv_b, m_w_down)))
    vo = dict(zip(names, (v_norm_mix_pre, v_norm_mix_post, v_w_in, v_b_forget, v_w_pool, v_pool_scale, v_w_mix_out,
                          v_norm_mem, v_norm_xa_pre, v_norm_xa_post, v_w_xq, v_w_xkv, v_w_xo, v_norm_ffn_pre,
                          v_norm_ffn_post, v_w_up, v_conv_w, v_conv_b, v_w_down)))

    big_names = ("w_in", "w_mix_out", "w_xq", "w_xo", "w_xkv", "w_up", "w_down")
    shards = {k: w[k][0].astype(BF16) for k in big_names}
    shards["w_in"] = jnp.pad(shards["w_in"], ((0, 0), (0, D_IN_PAD - shards["w_in"].shape[1])))
    conv_w_sh = jnp.pad(conv_w[0, :, 0, :], ((0, 5), (0, 0)))
    (g_in,) = _all_gather([shards["w_in"]], "gather_w_in")
    mix_srcs = [shards[k] for k in ("w_mix_out", "w_xq", "w_xo", "w_xkv")]
    mix_flight = _exchange_start(mix_srcs, [_own_slot(a) for a in mix_srcs], g_in, True, "gather_mix_start")
    ffn_srcs = [shards["w_up"], shards["w_down"], conv_w_sh]
    ffn_flight = _exchange_start(ffn_srcs, [_own_slot(a) for a in ffn_srcs], mix_flight[4], True, "gather_ffn_start")
    my_slot = 4 * lax.axis_index("x") + 2 * lax.axis_index("y") + lax.axis_index("c")
    own_block = lambda a: _own_slot(lax.dynamic_index_in_dim(a, my_slot, 0, keepdims=False))
    by_rows = lambda a: a.reshape(NDEV, a.shape[0] // NDEV, a.shape[1])
    by_cols = lambda a: a.reshape(a.shape[0], NDEV, a.shape[1] // NDEV).transpose(1, 0, 2)
    grad_flight = {}

    def mix_weights(after):
        g_mix, g_xq, g_xo, g_xkv = _exchange_wait(*mix_flight[:4], after, True, "gather_mix_wait")
        return (g_mix.reshape(D_MODEL, D_MODEL), g_xq.reshape(D_MODEL, D_MODEL), g_xo.reshape(D_MODEL, D_MODEL), g_xkv)

    def ffn_weights(after):
        g_up, g_down, g_cw = _exchange_wait(*ffn_flight[:4], after, True, "gather_ffn_wait")
        return g_up, g_down.reshape(D_FF, D_MODEL), g_cw.transpose(1, 0, 2).reshape(8, 2 * D_FF)

    def send_ffn_grads(d_w_up, d_w_down, d_cw):
        srcs = [d_w_up, by_rows(d_w_down), by_cols(d_cw)]
        grad_flight["ffn"] = _exchange_start(srcs, [own_block(a) for a in srcs], ffn_flight[4], False, "scatter_ffn_start")
        return grad_flight["ffn"][4]

    def send_mix_grads(d_w_mix, d_w_xq, d_w_xo, d_w_xkv):
        srcs = [by_rows(d_w_mix), by_rows(d_w_xq), by_rows(d_w_xo), d_w_xkv]
        grad_flight["mix"] = _exchange_start(srcs, [own_block(a) for a in srcs], ffn_flight[4], False, "scatter_mix_start")
        return grad_flight["mix"][4]

    def send_in_grad(d_w_in):
        srcs = [by_rows(d_w_in)]
        grad_flight["in"] = _exchange_start(srcs, [own_block(a) for a in srcs], ffn_flight[4], False, "scatter_in_start")
        return grad_flight["in"][4]

    gains = dict(mix_pre=norm_mix_pre + ffn_flight[4][0, 0], mix_post=norm_mix_post, mem=norm_mem, xa_pre=norm_xa_pre,
                 xa_post=norm_xa_post, ffn_pre=norm_ffn_pre, ffn_post=norm_ffn_post)
    loss, grad_x, small = _local_step(
        x[0], mem[0], loss_target[0], gains, b_forget, w_pool[0], pool_scale[0], conv_b,
        g_in.reshape(D_MODEL, D_IN_PAD), mix_weights, ffn_weights, send_in_grad, send_mix_grads, send_ffn_grads)

    p_up, p_down, p_cw = _exchange_wait(*grad_flight["ffn"][:4], grad_x, False, "scatter_ffn_wait")
    p_mix, p_xq, p_xo, p_xkv = _exchange_wait(*grad_flight["mix"][:4], grad_x, False, "scatter_mix_wait")
    parts = dict(w_mix_out=p_mix, w_xq=p_xq, w_xo=p_xo, w_xkv=p_xkv, w_up=p_up, w_down=p_down)
    *small_parts, loss_parts = _all_gather_small([small[k] for k in SMALL_ORDER] + [loss], "gather_small_grads")

    res = {k: [a[None] for a in _adamw(p, w[k][0], mo[k][0], vo[k][0], "adamw_" + k)] for k, p in parts.items()}
    pad_cw = lambda a: jnp.pad(a[0, :, 0, :], ((0, 5), (0, 0)))
    res["conv_w"] = [a[:3][None, :, None, :] for a in
                     _adamw(p_cw, pad_cw(conv_w), pad_cw(m_conv_w), pad_cw(v_conv_w), "adamw_conv_w")]
    key_of = dict(mix_pre="norm_mix_pre", mix_post="norm_mix_post", mem="norm_mem", xa_pre="norm_xa_pre",
                  xa_post="norm_xa_post", ffn_pre="norm_ffn_pre", ffn_post="norm_ffn_post", conv_b="conv_b",
                  w_pool="w_pool", pool_scale="pool_scale", b_forget="b_forget")
    flat2d = lambda src: [src[key_of[k]].reshape(small[k].shape) for k in SMALL_ORDER]
    small_out = _adamw_small(small_parts, flat2d(w), flat2d(mo), flat2d(vo), "adamw_small")
    for k, four in zip(SMALL_ORDER, small_out):
        res[key_of[k]] = [a.reshape(w[key_of[k]].shape) for a in four]
    (p_in,) = _exchange_wait(*grad_flight["in"][:4], res["w_up"][1], False, "scatter_in_wait")
    res["w_in"] = [a[None] for a in _adamw(p_in[:, :, :w_in.shape[2]], w["w_in"][0], mo["w_in"][0], vo["w_in"][0],
                                           "adamw_w_in")]

    outs = [jnp.sum(loss_parts[:, 0, 0]), grad_x[None]]
    for idx in range(4):
        outs += [res[k][idx] for k in names]
    return tuple(outs)
```

```python
import functools
import math

import jax
import jax.numpy as jnp
from jax import lax
from jax.experimental import pallas as pl
from jax.experimental.pallas import tpu as pltpu

F32 = jnp.float32
BF16 = jnp.bfloat16

NDEV = 8
D_MODEL = 1024
D_POOL = 256
D_FOX = 768
FOX_HEADS = 12
HEAD_PAIRS = FOX_HEADS // 2
XA_HEADS = 4
XA_DIM = 256
D_FF = 4096
D_IN_PAD = 2688
F_COL = 2560
POOL_HALO = 16
NORM_EPS = 1e-6
NEG = -1e30

ADAM_LR = 0.001
ADAM_B1 = 0.9
ADAM_B2 = 0.999
ADAM_EPS = 1e-08
ADAM_WD = 0.01
ADAM_STEP = 10

TM = 512
TQ = 512
TN_FF = 1024
VMEM_LIMIT = 56 * 1024 * 1024
MESH = pl.DeviceIdType.MESH


def _cp(*sem):
    return pltpu.CompilerParams(dimension_semantics=sem, vmem_limit_bytes=VMEM_LIMIT)


def _dot(a, b, dims):
    return lax.dot_general(a, b, (dims, ((), ())), preferred_element_type=F32)


NN = ((1,), (0,))
NT = ((1,), (1,))
TN = ((0,), (0,))


def _mm(a, b, mode, out_dtype, tm, tn, tk, name, b_cols=None, out_cols=None, after=None):
    a_list = list(a) if isinstance(a, (list, tuple)) else [a]
    b_list = list(b) if isinstance(b, (list, tuple)) else [b]
    assert len(a_list) == 1 or len(b_list) == 1
    if mode == "tn":
        K, M = a_list[0].shape
        assert len(a_list) == 1
        Ns = [x.shape[1] for x in b_list]
        N = sum(Ns)
        assert b_cols is None
    else:
        assert len(b_list) == 1
        M = a_list[0].shape[0]
        Ks = [x.shape[1] for x in a_list]
        K = sum(Ks)
        if b_cols is None:
            N = b_list[0].shape[0] if mode == "nt" else b_list[0].shape[1]
        else:
            N = b_list[0].shape[1] if mode == "nt" else NDEV * b_cols
    assert M % tm == 0 and N % tn == 0 and K % tk == 0, (name, M, N, K)
    grid = (M // tm, N // tn, K // tk)
    nk = grid[2]
    dims = {"nn": NN, "nt": NT, "tn": TN}[mode]

    in_specs = []
    if mode == "tn":
        in_specs.append(pl.BlockSpec((tk, tm), lambda i, j, k: (k, i)))
        if len(b_list) == 1:
            in_specs.append(pl.BlockSpec((tk, tn), lambda i, j, k: (k, j)))
        else:
            nj1 = Ns[0] // tn
            in_specs.append(pl.BlockSpec((tk, tn), lambda i, j, k: (k, jnp.minimum(j, nj1 - 1))))
            in_specs.append(pl.BlockSpec((tk, tn), lambda i, j, k: (k, jnp.maximum(j - nj1, 0))))
    else:
        if len(a_list) == 1:
            in_specs.append(pl.BlockSpec((tm, tk), lambda i, j, k: (i, k)))
        else:
            nk1 = Ks[0] // tk
            in_specs.append(pl.BlockSpec((tm, tk), lambda i, j, k: (i, jnp.minimum(k, nk1 - 1))))
            in_specs.append(pl.BlockSpec((tm, tk), lambda i, j, k: (i, jnp.maximum(k - nk1, 0))))
        if b_cols is None:
            if mode == "nn":
                in_specs.append(pl.BlockSpec((tk, tn), lambda i, j, k: (k, j)))
            else:
                in_specs.append(pl.BlockSpec((tn, tk), lambda i, j, k: (j, k)))
        else:
            if mode == "nn":
                per = b_cols // tn
                in_specs.append(pl.BlockSpec((None, tk, tn), lambda i, j, k: (j // per, k, j % per)))
            else:
                per = b_cols // tk
                in_specs.append(pl.BlockSpec((None, tn, tk), lambda i, j, k: (k // per, j, k % per)))
    if out_cols is None:
        out_spec = pl.BlockSpec((tm, tn), lambda i, j, k: (i, j))
        out_shape = jax.ShapeDtypeStruct((M, N), out_dtype)
    else:
        pero = out_cols // tn
        out_spec = pl.BlockSpec((None, tm, tn), lambda i, j, k: (j // pero, i, j % pero))
        out_shape = jax.ShapeDtypeStruct((NDEV, M, out_cols), out_dtype)

    two_a = len(a_list) == 2
    two_b = len(b_list) == 2
    extra = []
    if after is not None:
        in_specs.append(pl.BlockSpec(memory_space=pl.ANY))
        extra.append(after)

    def body(*refs):
        o_ref, acc_ref = refs[-2], refs[-1]
        j = pl.program_id(1)
        k = pl.program_id(2)

        @pl.when(k == 0)
        def _():
            acc_ref[...] = jnp.zeros_like(acc_ref)

        if two_a:
            a1, a2, b1 = refs[0], refs[1], refs[2]
            nk1_ = Ks[0] // tk

            @pl.when(k < nk1_)
            def _():
                acc_ref[...] += _dot(a1[...], b1[...], dims)

            @pl.when(k >= nk1_)
            def _():
                acc_ref[...] += _dot(a2[...], b1[...], dims)
        elif two_b:
            a1, b1, b2 = refs[0], refs[1], refs[2]
            nj1_ = Ns[0] // tn

            @pl.when(j < nj1_)
            def _():
                acc_ref[...] += _dot(a1[...], b1[...], dims)

            @pl.when(j >= nj1_)
            def _():
                acc_ref[...] += _dot(a1[...], b2[...], dims)
        else:
            acc_ref[...] += _dot(refs[0][...], refs[1][...], dims)

        @pl.when(k == nk - 1)
        def _():
            o_ref[...] = acc_ref[...].astype(o_ref.dtype)

    return pl.pallas_call(
        body, name=name, grid=grid, in_specs=in_specs, out_specs=out_spec, out_shape=out_shape,
        scratch_shapes=[pltpu.VMEM((tm, tn), F32)],
        compiler_params=_cp("parallel", "parallel", "arbitrary"),
    )(*a_list, *b_list, *extra)


def _rstd(x):
    return lax.rsqrt(jnp.mean(x * x, axis=-1, keepdims=True) + NORM_EPS)


def _norm_bwd_rows(dxn, xn, r):
    return r * (dxn - xn * jnp.mean(dxn * xn, axis=-1, keepdims=True))


def _row_spec(tm, d):
    return pl.BlockSpec((tm, d), lambda i: (i, 0))


def _vec_spec(d):
    return pl.BlockSpec((1, d), lambda i: (0, 0))


def _mm_rows(a, b, mode, tk, name, rows, vecs, outs, epilogue, b_cols=None, after=None):
    a_list = list(a) if isinstance(a, (list, tuple)) else [a]
    m = a_list[0].shape[0]
    ks = [x.shape[1] for x in a_list]
    n = D_MODEL
    pieces = tk is None
    nk = 1 if pieces else sum(ks) // tk
    dims = NN if mode == "nn" else NT
    if pieces:
        assert mode == "nt" and b_cols is None
        in_specs = [pl.BlockSpec((TM, kp), lambda i, k: (i, 0)) for kp in ks]
        tk = sum(ks)
    elif len(a_list) == 1:
        in_specs = [pl.BlockSpec((TM, tk), lambda i, k: (i, k))]
    else:
        nk1 = ks[0] // tk
        in_specs = [pl.BlockSpec((TM, tk), lambda i, k: (i, jnp.minimum(k, nk1 - 1))),
                    pl.BlockSpec((TM, tk), lambda i, k: (i, jnp.maximum(k - nk1, 0)))]
    if mode == "nn":
        in_specs.append(pl.BlockSpec((tk, n), lambda i, k: (k, 0)))
    elif b_cols is None:
        in_specs.append(pl.BlockSpec((n, tk), lambda i, k: (0, k)))
    else:
        per = b_cols // tk
        in_specs.append(pl.BlockSpec((None, n, tk), lambda i, k: (k // per, 0, k % per)))
    in_specs += [pl.BlockSpec((TM, n), lambda i, k: (i, 0))] * len(rows)
    in_specs += [pl.BlockSpec((1, n), lambda i, k: (0, 0))] * len(vecs)
    extra = []
    if after is not None:
        in_specs.append(pl.BlockSpec(memory_space=pl.ANY))
        extra.append(after)
    out_specs, out_shape = [], []
    for o in outs:
        if o == "sum":
            out_specs.append(pl.BlockSpec((1, n), lambda i, k: (0, 0)))
            out_shape.append(jax.ShapeDtypeStruct((1, n), F32))
        else:
            out_specs.append(pl.BlockSpec((TM, n), lambda i, k: (i, 0)))
            out_shape.append(jax.ShapeDtypeStruct((m, n), o))
    na, nr, nv = len(a_list), len(rows), len(vecs)

    def body(*refs):
        a_refs, b_ref = refs[:na], refs[na]
        row_refs = refs[na + 1:na + 1 + nr]
        vec_refs = refs[na + 1 + nr:na + 1 + nr + nv]
        out_refs = refs[len(refs) - 1 - len(outs):len(refs) - 1]
        acc_ref = refs[-1]
        i, k = pl.program_id(0), pl.program_id(1)

        @pl.when(k == 0)
        def _():
            acc_ref[...] = jnp.zeros_like(acc_ref)

        if pieces:
            off = 0
            for a_ref in a_refs:
                kp = a_ref.shape[1]
                acc_ref[...] += _dot(a_ref[...], b_ref[:, off:off + kp], dims)
                off += kp
        elif na == 1:
            acc_ref[...] += _dot(a_refs[0][...], b_ref[...], dims)
        else:
            nk1_ = ks[0] // tk

            @pl.when(k < nk1_)
            def _():
                acc_ref[...] += _dot(a_refs[0][...], b_ref[...], dims)

            @pl.when(k >= nk1_)
            def _():
                acc_ref[...] += _dot(a_refs[1][...], b_ref[...], dims)

        @pl.when(k == nk - 1)
        def _():
            vals = epilogue(acc_ref[...], [r[...] for r in row_refs], [v[...] for v in vec_refs])
            for o, ref, val in zip(outs, out_refs, vals):
                if o == "sum":
                    @pl.when(i == 0)
                    def _():
                        ref[...] = val

                    @pl.when(i > 0)
                    def _():
                        ref[...] += val
                else:
                    ref[...] = val.astype(o)

    return pl.pallas_call(
        body, name=name, grid=(m // TM, nk), in_specs=in_specs, out_specs=out_specs, out_shape=out_shape,
        scratch_shapes=[pltpu.VMEM((TM, n), F32)],
        compiler_params=_cp("arbitrary", "arbitrary"),
    )(*a_list, b, *rows, *vecs, *extra)


def _proj_in(x, g, w_in, name):
    s, d = x.shape
    n = w_in.shape[1]

    def body(x_ref, g_ref, w_ref, h_ref, p_ref, f_ref):
        xv = x_ref[...]
        h = (xv * _rstd(xv) * g_ref[...]).astype(BF16)
        h_ref[...] = h
        acc = _dot(h, w_ref[...], NN)
        p_ref[...] = acc.astype(BF16)
        f_ref[...] = acc[:, F_COL:]

    return pl.pallas_call(
        body, name=name, grid=(s // TM,),
        in_specs=[_row_spec(TM, d), _vec_spec(d), pl.BlockSpec((d, n), lambda i: (0, 0))],
        out_specs=[_row_spec(TM, d), _row_spec(TM, n), _row_spec(TM, n - F_COL)],
        out_shape=[jax.ShapeDtypeStruct((s, d), BF16), jax.ShapeDtypeStruct((s, n), BF16),
                   jax.ShapeDtypeStruct((s, n - F_COL), F32)],
        compiler_params=_cp("parallel"),
    )(x, g, w_in)


def _dw_in(h, pieces, name):
    s, d = h.shape
    n = sum(p.shape[1] for p in pieces)
    tk = 1024
    nk = s // tk

    def body(*refs):
        h_ref, piece_refs, o_ref, acc_ref = refs[0], refs[1:-2], refs[-2], refs[-1]
        k = pl.program_id(1)

        @pl.when(k == 0)
        def _():
            acc_ref[...] = jnp.zeros_like(acc_ref)

        off = 0
        for p_ref in piece_refs:
            w = p_ref.shape[1]
            acc_ref[:, off:off + w] += _dot(h_ref[...], p_ref[...], TN)
            off += w

        @pl.when(k == nk - 1)
        def _():
            o_ref[...] = acc_ref[...].astype(BF16)

    return pl.pallas_call(
        body, name=name, grid=(d // TM, nk),
        in_specs=[pl.BlockSpec((tk, TM), lambda i, k: (k, i))] +
                 [pl.BlockSpec((tk, p.shape[1]), lambda i, k: (k, 0)) for p in pieces],
        out_specs=pl.BlockSpec((TM, n), lambda i, k: (i, 0)),
        out_shape=jax.ShapeDtypeStruct((d, n), BF16),
        scratch_shapes=[pltpu.VMEM((TM, n), F32)],
        compiler_params=_cp("parallel", "arbitrary"),
    )(h, *pieces)


def _epi_resid(y, rows, vecs):
    (x_in,), (g_post, g_next) = rows, vecs
    xo = x_in + y * _rstd(y) * g_post
    return y, xo, xo * _rstd(xo) * g_next


def _epi_norm_bwd(dh, rows, vecs):
    x, dx_res = rows[0], rows[1]
    r = _rstd(x)
    xn = x * r
    dx = dx_res + _norm_bwd_rows(dh * vecs[0], xn, r)
    res = [dx, jnp.sum(dh * xn, axis=0, keepdims=True)]
    if len(rows) == 3:
        y = rows[2]
        r2 = _rstd(y)
        yn = y * r2
        res += [_norm_bwd_rows(dx * vecs[1], yn, r2), jnp.sum(dx * yn, axis=0, keepdims=True)]
    return res


def _norm_fwd(x, g, name):
    s, d = x.shape
    tm = min(TM, s)

    def body(x_ref, g_ref, h_ref):
        xv = x_ref[...]
        h_ref[...] = (xv * _rstd(xv) * g_ref[...]).astype(BF16)

    return pl.pallas_call(
        body, name=name, grid=(s // tm,), in_specs=[_row_spec(tm, d), _vec_spec(d)],
        out_specs=_row_spec(tm, d), out_shape=jax.ShapeDtypeStruct((s, d), BF16),
        compiler_params=_cp("parallel"),
    )(x, g)


def _norm_bwd(dh, x, dx_res, g_pre, name, prev=None):
    s, d = x.shape
    tm = min(TM, s)
    has_prev = prev is not None

    def body(*refs):
        if has_prev:
            dh_ref, x_ref, dr_ref, g_ref, y_ref, gp_ref, dx_ref, dg_ref, dy_ref, dgp_ref = refs
        else:
            dh_ref, x_ref, dr_ref, g_ref, dx_ref, dg_ref = refs
        i = pl.program_id(0)
        xv = x_ref[...]
        r = _rstd(xv)
        xn = xv * r
        dhv = dh_ref[...].astype(F32)
        dx = dr_ref[...] + _norm_bwd_rows(dhv * g_ref[...], xn, r)
        dx_ref[...] = dx
        dg = jnp.sum(dhv * xn, axis=0, keepdims=True)

        @pl.when(i == 0)
        def _():
            dg_ref[...] = dg

        @pl.when(i > 0)
        def _():
            dg_ref[...] += dg

        if has_prev:
            yv = y_ref[...]
            r2 = _rstd(yv)
            yn = yv * r2
            dy_ref[...] = _norm_bwd_rows(dx * gp_ref[...], yn, r2).astype(BF16)
            dgp = jnp.sum(dx * yn, axis=0, keepdims=True)

            @pl.when(i == 0)
            def _():
                dgp_ref[...] = dgp

            @pl.when(i > 0)
            def _():
                dgp_ref[...] += dgp

    in_specs = [_row_spec(tm, d), _row_spec(tm, d), _row_spec(tm, d), _vec_spec(d)]
    out_specs = [_row_spec(tm, d), _vec_spec(d)]
    out_shape = [jax.ShapeDtypeStruct((s, d), F32), jax.ShapeDtypeStruct((1, d), F32)]
    args = [dh, x, dx_res, g_pre]
    if has_prev:
        in_specs += [_row_spec(tm, d), _vec_spec(d)]
        out_specs += [_row_spec(tm, d), _vec_spec(d)]
        out_shape += [jax.ShapeDtypeStruct((s, d), BF16), jax.ShapeDtypeStruct((1, d), F32)]
        args += list(prev)
    return pl.pallas_call(
        body, name=name, grid=(s // tm,), in_specs=in_specs, out_specs=out_specs, out_shape=out_shape,
        compiler_params=_cp("arbitrary"),
    )(*args)


def _split3(v):
    hi = v.astype(BF16)
    r1 = v - hi.astype(F32)
    mid = r1.astype(BF16)
    lo = (r1 - mid.astype(F32)).astype(BF16)
    return hi, mid, lo


def _tri_dot(tri, v):
    hi, mid, lo = _split3(v)
    return _dot(tri, hi, NN) + _dot(tri, mid, NN) + _dot(tri, lo, NN)


def _gate_cumsum(fraw, b_pad, name):
    s = fraw.shape[0]
    width = HEAD_PAIRS * 128

    def body(f_ref, b_ref, flog_ref, aq_ref, ak_ref, carry_ref):
        i = pl.program_id(0)

        @pl.when(i == 0)
        def _():
            carry_ref[...] = jnp.zeros_like(carry_ref)

        flog = f_ref[...] + b_ref[...]
        flog_ref[...] = flog
        lf = jnp.minimum(flog, 0.0) - jnp.log(1.0 + jnp.exp(-jnp.abs(flog)))
        lane = lax.broadcasted_iota(jnp.int32, (1, 128), 1)
        lf = jnp.where(lane < FOX_HEADS, lf, 0.0)
        row = lax.broadcasted_iota(jnp.int32, (TM, TM), 0)
        col = lax.broadcasted_iota(jnp.int32, (TM, TM), 1)
        tri = (row >= col).astype(BF16)
        cum = _tri_dot(tri, lf) + carry_ref[...]
        carry_ref[...] = cum[TM - 1:TM, :]
        aq_ref[...], ak_ref[...] = _fox_operands(cum)

    return pl.pallas_call(
        body, name=name, grid=(s // TM,),
        in_specs=[_row_spec(TM, 128), _vec_spec(128)],
        out_specs=[_row_spec(TM, 128), _row_spec(TM, width), _row_spec(TM, width)],
        out_shape=[jax.ShapeDtypeStruct((s, 128), F32), jax.ShapeDtypeStruct((s, width), BF16),
                   jax.ShapeDtypeStruct((s, width), BF16)],
        scratch_shapes=[pltpu.VMEM((1, 128), F32)],
        compiler_params=_cp("arbitrary"),
    )(fraw, b_pad)


def _gate_bwd(qaux, kaux, flog, name):
    s = flog.shape[0]
    n = s // TM

    def body(qa_ref, ka_ref, fl_ref, dp_ref, db_ref, carry_ref):
        i = pl.program_id(0)

        @pl.when(i == 0)
        def _():
            carry_ref[...] = jnp.zeros_like(carry_ref)

        lane = lax.broadcasted_iota(jnp.int32, (1, 128), 1)
        dcum = jnp.zeros((TM, 128), F32)
        for p in range(HEAD_PAIRS):
            d = qa_ref[p] - pltpu.roll(ka_ref[p], 128 - 3, 1)
            dcum = jnp.where(lane == 2 * p, pltpu.roll(d, 64 + 2 * p, 1),
                             jnp.where(lane == 2 * p + 1, pltpu.roll(d, 2 * p + 1, 1), dcum))
        row = lax.broadcasted_iota(jnp.int32, (TM, TM), 0)
        col = lax.broadcasted_iota(jnp.int32, (TM, TM), 1)
        tri = (row <= col).astype(BF16)
        dlf = _tri_dot(tri, dcum) + carry_ref[...]
        carry_ref[...] = dlf[0:1, :]
        df = jnp.where(lane < FOX_HEADS, dlf / (1.0 + jnp.exp(fl_ref[...])), 0.0)
        dp_ref[...] = df.astype(BF16)
        db = jnp.sum(df, axis=0, keepdims=True)

        @pl.when(i == 0)
        def _():
            db_ref[...] = db

        @pl.when(i > 0)
        def _():
            db_ref[...] += db

    rev = lambda i: (n - 1 - i, 0)
    return pl.pallas_call(
        body, name=name, grid=(n,),
        in_specs=[pl.BlockSpec((HEAD_PAIRS, TM, 128), lambda i: (0, n - 1 - i, 0)),
                  pl.BlockSpec((HEAD_PAIRS, TM, 128), lambda i: (0, n - 1 - i, 0)), pl.BlockSpec((TM, 128), rev)],
        out_specs=[pl.BlockSpec((TM, 128), rev), _vec_spec(128)],
        out_shape=[jax.ShapeDtypeStruct((s, 128), BF16), jax.ShapeDtypeStruct((1, 128), F32)],
        scratch_shapes=[pltpu.VMEM((1, 128), F32)],
        compiler_params=_cp("arbitrary"),
    )(qaux, kaux, flog)


def _pool_consts(i, rows):
    lane = lax.broadcasted_iota(jnp.int32, (rows, D_POOL), 1)
    t1 = lax.broadcasted_iota(jnp.int32, (rows, D_POOL), 0) + i * TM + 1
    win = jnp.where(lane < 64, 2, jnp.where(lane < 128, 4, jnp.where(lane < 192, 8, 16)))
    inv = 1.0 / jnp.minimum(t1, win).astype(F32)
    return lane, inv


def _by_group(lane, s2, s4, s8, s16):
    return jnp.where(lane < 64, s2, jnp.where(lane < 128, s4, jnp.where(lane < 192, s8, s16)))


def _pool_diff(i, u_ref, halo_ref):
    u = u_ref[...].astype(F32)
    halo = jnp.where(i > 0, halo_ref[...].astype(F32), 0.0)
    ext = jnp.concatenate([halo, u], axis=0)
    s2 = ext + pltpu.roll(ext, 1, 0)
    s4 = s2 + pltpu.roll(s2, 2, 0)
    s8 = s4 + pltpu.roll(s4, 4, 0)
    s16 = s8 + pltpu.roll(s8, 8, 0)
    lane, inv = _pool_consts(i, TM)
    sel = _by_group(lane, s2[POOL_HALO:], s4[POOL_HALO:], s8[POOL_HALO:], s16[POOL_HALO:])
    return sel * inv - u


def _pool_fwd(proj, wbd, scale, ycat, name):
    s = proj.shape[0]
    hb = TM // POOL_HALO

    def body(u_ref, halo_ref, w_ref, sc_ref, y_any, y_ref):
        del y_any
        i = pl.program_id(0)
        diff = _pool_diff(i, u_ref, halo_ref)
        mixed = _dot(diff.astype(BF16), w_ref[...], NN)
        y_ref[...] = (mixed * sc_ref[...]).astype(BF16)

    return pl.pallas_call(
        body, name=name, grid=(s // TM,),
        in_specs=[pl.BlockSpec((TM, D_POOL), lambda i: (i, 0)),
                  pl.BlockSpec((POOL_HALO, D_POOL), lambda i: (jnp.maximum(i * hb - 1, 0), 0)),
                  pl.BlockSpec((D_POOL, D_POOL), lambda i: (0, 0)), _vec_spec(D_POOL),
                  pl.BlockSpec(memory_space=pl.ANY)],
        out_specs=pl.BlockSpec((TM, D_POOL), lambda i: (i, 0)),
        out_shape=jax.ShapeDtypeStruct(ycat.shape, ycat.dtype),
        input_output_aliases={4: 0},
        compiler_params=_cp("parallel"),
    )(proj, proj, wbd, scale, ycat)


def _pool_bwd(proj, dycat, wbd, scale, name):
    s = proj.shape[0]
    n = s // TM
    hb = TM // POOL_HALO
    last_halo = s // POOL_HALO - 1

    def body(u_ref, halo_ref, dy_ref, dyp_ref, w_ref, sc_ref, dp_ref, dw_ref, dsc_ref):
        i = pl.program_id(0)
        diff = _pool_diff(i, u_ref, halo_ref)
        diff_b = diff.astype(BF16)
        mixed = _dot(diff_b, w_ref[...], NN)
        dy = dy_ref[...].astype(F32)
        dmix = (dy * sc_ref[...]).astype(BF16)
        dyp = jnp.where(i < n - 1, dyp_ref[...].astype(F32), 0.0)
        dmix_p = (dyp * sc_ref[...]).astype(BF16)
        dd = _dot(dmix, w_ref[...], NT)
        dd_p = _dot(dmix_p, w_ref[...], NT)
        lane, inv = _pool_consts(i, TM)
        _, inv_p = _pool_consts(i + 1, POOL_HALO)
        ext = jnp.concatenate([dd * inv, dd_p * inv_p], axis=0)
        rows = TM + POOL_HALO
        l2 = ext + pltpu.roll(ext, rows - 1, 0)
        l4 = l2 + pltpu.roll(l2, rows - 2, 0)
        l8 = l4 + pltpu.roll(l4, rows - 4, 0)
        l16 = l8 + pltpu.roll(l8, rows - 8, 0)
        du = _by_group(lane, l2[:TM], l4[:TM], l8[:TM], l16[:TM]) - dd
        dp_ref[...] = du.astype(BF16)
        dw = _dot(diff_b, dmix, TN)
        dsc = jnp.sum(dy * mixed, axis=0, keepdims=True)

        @pl.when(i == 0)
        def _():
            dw_ref[...] = dw
            dsc_ref[...] = dsc

        @pl.when(i > 0)
        def _():
            dw_ref[...] += dw
            dsc_ref[...] += dsc

    return pl.pallas_call(
        body, name=name, grid=(n,),
        in_specs=[pl.BlockSpec((TM, D_POOL), lambda i: (i, 0)),
                  pl.BlockSpec((POOL_HALO, D_POOL), lambda i: (jnp.maximum(i * hb - 1, 0), 0)),
                  pl.BlockSpec((TM, D_POOL), lambda i: (i, 0)),
                  pl.BlockSpec((POOL_HALO, D_POOL), lambda i: (jnp.minimum((i + 1) * hb, last_halo), 0)),
                  pl.BlockSpec((D_POOL, D_POOL), lambda i: (0, 0)), _vec_spec(D_POOL)],
        out_specs=[pl.BlockSpec((TM, D_POOL), lambda i: (i, 0)),
                   pl.BlockSpec((D_POOL, D_POOL), lambda i: (0, 0)), _vec_spec(D_POOL)],
        out_shape=[jax.ShapeDtypeStruct((s, D_POOL), BF16),
                   jax.ShapeDtypeStruct((D_POOL, D_POOL), F32), jax.ShapeDtypeStruct((1, D_POOL), F32)],
        compiler_params=_cp("arbitrary"),
    )(proj, proj, dycat, dycat, wbd, scale)


Q_BLK = D_POOL // 128
K_BLK = Q_BLK + D_FOX // 128
V_BLK = K_BLK + D_FOX // 128
STEP_PAIRS = 2
assert Q_BLK % STEP_PAIRS == 0 and K_BLK % STEP_PAIRS == 0 and V_BLK % STEP_PAIRS == 0 and HEAD_PAIRS % STEP_PAIRS == 0


def _operand_rows(v0, v1, ones_off):
    row = lax.broadcasted_iota(jnp.int32, (128, 1), 0)
    half = row & 63
    out = jnp.where(jnp.logical_and(half >= ones_off, half < ones_off + 3), 1.0, 0.0) + jnp.zeros_like(v0)
    for base, v in ((64, v0), (0, v1)):
        for j, piece in enumerate(_split3(v)):
            out = jnp.where(row == base + j, piece.astype(F32), out)
    return out


def _fox_operands(cum):
    width = HEAD_PAIRS * 128
    hi, mid, lo = _split3(cum)
    packed = (hi.astype(F32) + pltpu.roll(mid.astype(F32), 16, 1) + pltpu.roll(lo.astype(F32), 32, 1)).astype(BF16)
    row = lax.broadcasted_iota(jnp.int32, (128, width), 0)
    col = lax.broadcasted_iota(jnp.int32, (128, width), 1)
    head, j = row & 15, row >> 4
    base = (head >> 1) * 128 + (1 - (head & 1)) * 64
    used = jnp.logical_and(head < FOX_HEADS, j < 3)
    half = lax.broadcasted_iota(jnp.int32, (1, width), 1) & 63
    res = []
    for off, sign, ones_off in ((0, 1.0, 3), (3, -1.0, 0)):
        ones = jnp.where(jnp.logical_and(half >= ones_off, half < ones_off + 3), 1.0, 0.0)
        sel = jnp.where(jnp.logical_and(col == base + off + j, used), sign, 0.0).astype(BF16)
        res.append((ones + _dot(packed, sel, NN)).astype(BF16))
    return res


def _fox_do_operand(dycat, ycat, after, name):
    s = dycat.shape[0]

    rb = 1024
    nblk = D_FOX // D_POOL

    def body(*refs):
        do_refs, o_refs, ad_ref = refs[:nblk], refs[nblk:2 * nblk], refs[-1]
        src = lax.broadcasted_iota(jnp.int32, (D_POOL, D_POOL), 0)
        dst = lax.broadcasted_iota(jnp.int32, (D_POOL, D_POOL), 1)
        same_pair = (src >> 7) == (dst >> 7)
        s_in, d_in = src & 127, dst & 127
        hit = jnp.logical_and(same_pair, jnp.logical_or(
            jnp.logical_and(s_in < 64, jnp.logical_and(d_in >= 64, d_in < 67)), jnp.logical_and(s_in >= 64, d_in < 3)))
        sel = jnp.where(hit, 1.0, 0.0).astype(BF16)
        j = lax.broadcasted_iota(jnp.int32, (1, D_POOL), 1) & 63
        for b in range(nblk):
            dd = do_refs[b][...].astype(F32) * o_refs[b][...].astype(F32)
            dsum = jnp.zeros(dd.shape, F32)
            for piece in _split3(dd):
                dsum = dsum + _dot(piece, sel, NN)
            hi, mid, lo = _split3(-dsum)
            ad_ref[:, D_POOL * b:D_POOL * (b + 1)] = jnp.where(j == 0, hi, jnp.where(j == 1, mid, lo))

    blks = [pl.BlockSpec((rb, D_POOL), functools.partial(lambda i, b: (i, 1 + b), b=b)) for b in range(nblk)]
    return pl.pallas_call(
        body, name=name, grid=(s // rb,), in_specs=blks + blks + [pl.BlockSpec(memory_space=pl.ANY)],
        out_specs=pl.BlockSpec((rb, D_FOX), lambda i: (i, 0)),
        out_shape=jax.ShapeDtypeStruct((s, D_FOX), BF16),
        compiler_params=_cp("parallel"),
    )(*[dycat] * nblk, *[ycat] * nblk, after)


def _causal_pairs(nq, key_major):
    if key_major:
        pairs = [(q, k) for k in range(nq) for q in range(k, nq)]
    else:
        pairs = [(q, k) for q in range(nq) for k in range(q + 1)]
    return (jnp.asarray([p[0] for p in pairs], jnp.int32), jnp.asarray([p[1] for p in pairs], jnp.int32))


def _fox_fwd(proj, aq, ak, name):
    s = proj.shape[0]
    nq = s // TQ
    qi_arr, ki_arr = _causal_pairs(nq, key_major=False)

    wide = STEP_PAIRS * 128
    heads = [(pp, hh) for pp in range(STEP_PAIRS) for hh in range(2)]

    def body(qi_ref, ki_ref, q_ref, k_ref, v_ref, aq_ref, ak_ref, o_ref, aqb_ref, m_ref, acc_ref, aux_ref):
        t = pl.program_id(1)
        qi, ki = qi_ref[t], ki_ref[t]
        lane = lax.broadcasted_iota(jnp.int32, (1, 128), 1)
        masks = [lane < 64, lane >= 64]
        ones_v = jnp.where((lane & 63) == 8, 1.0, 0.0).astype(BF16)
        top = lax.broadcasted_iota(jnp.int32, (128, 1), 0) < 64

        @pl.when(ki == 0)
        def _():
            m_ref[...] = jnp.full_like(m_ref, NEG)
            acc_ref[...] = jnp.zeros_like(acc_ref)
            aux_ref[...] = jnp.zeros_like(aux_ref)

        def step(diag):
            q2s = q_ref[...] * 0.125
            k2, v2, aq2, ak2 = k_ref[...], v_ref[...], aq_ref[...], ak_ref[...]

            def operand(main, lanes, pp, hh):
                cols = slice(128 * pp, 128 * (pp + 1))
                return jnp.where(masks[hh], main[:, cols], lanes[:, cols])

            scs = [_dot(operand(k2, ak2, pp, hh), operand(q2s, aq2, pp, hh), NT) for pp, hh in heads]
            ps, alpha = [], []
            for n, sc in enumerate(scs):
                if diag:
                    key = lax.broadcasted_iota(jnp.int32, sc.shape, 0)
                    qry = lax.broadcasted_iota(jnp.int32, sc.shape, 1)
                    sc = jnp.where(qry >= key, sc, NEG)
                m_prev = m_ref[n]
                m_new = jnp.maximum(m_prev, jnp.max(sc, axis=0, keepdims=True))
                m_ref[n] = m_new
                alpha.append(jnp.exp(m_prev - m_new))
                ps.append(jnp.exp(sc - m_new).astype(BF16))
            ones2 = jnp.concatenate([ones_v] * STEP_PAIRS, axis=1)
            pv = [_dot(operand(v2, ones2, pp, hh), ps[n], TN) for n, (pp, hh) in enumerate(heads)]
            for pp in range(STEP_PAIRS):
                a0, a1, pv0, pv1 = alpha[2 * pp], alpha[2 * pp + 1], pv[2 * pp], pv[2 * pp + 1]
                acc_ref[pp] = acc_ref[pp] * jnp.where(top, a0, a1) + jnp.where(top, pv0, pv1)
                aux_ref[pp] = aux_ref[pp] * jnp.where(top, a1, a0) + jnp.where(top, pv1, pv0)

        @pl.when(ki < qi)
        def _():
            step(False)

        @pl.when(ki == qi)
        def _():
            step(True)
            for pp in range(STEP_PAIRS):
                cols = slice(128 * pp, 128 * (pp + 1))
                aux = aux_ref[pp]
                l0, l1 = aux[72:73, :], aux[8:9, :]
                o_ref[:, cols] = (acc_ref[pp] * jnp.where(top, 1.0 / l0, 1.0 / l1)).T.astype(BF16)
                aqt = aq_ref[:, cols].astype(F32).T
                cum0 = aqt[64:65, :] + aqt[65:66, :] + aqt[66:67, :]
                cum1 = aqt[0:1, :] + aqt[1:2, :] + aqt[2:3, :]
                aqb = _operand_rows(cum0 - (m_ref[2 * pp] + jnp.log(l0)), cum1 - (m_ref[2 * pp + 1] + jnp.log(l1)), 3)
                aqb_ref[:, cols] = aqb.T.astype(BF16)

    grid_spec = pltpu.PrefetchScalarGridSpec(
        num_scalar_prefetch=2, grid=(HEAD_PAIRS // STEP_PAIRS, int(qi_arr.shape[0])),
        in_specs=[pl.BlockSpec((TQ, wide), lambda p, t, qi, ki: (qi[t], Q_BLK // STEP_PAIRS + p)),
                  pl.BlockSpec((TQ, wide), lambda p, t, qi, ki: (ki[t], K_BLK // STEP_PAIRS + p)),
                  pl.BlockSpec((TQ, wide), lambda p, t, qi, ki: (ki[t], V_BLK // STEP_PAIRS + p)),
                  pl.BlockSpec((TQ, wide), lambda p, t, qi, ki: (qi[t], p)),
                  pl.BlockSpec((TQ, wide), lambda p, t, qi, ki: (ki[t], p))],
        out_specs=[pl.BlockSpec((TQ, wide), lambda p, t, qi, ki: (qi[t], Q_BLK // STEP_PAIRS + p)),
                   pl.BlockSpec((TQ, wide), lambda p, t, qi, ki: (qi[t], p))],
        scratch_shapes=[pltpu.VMEM((2 * STEP_PAIRS, 1, TQ), F32),
                        pltpu.VMEM((STEP_PAIRS, 128, TQ), F32), pltpu.VMEM((STEP_PAIRS, 128, TQ), F32)])
    return pl.pallas_call(
        body, name=name, grid_spec=grid_spec,
        out_shape=[jax.ShapeDtypeStruct((s, D_MODEL), BF16), jax.ShapeDtypeStruct((s, HEAD_PAIRS * 128), BF16)],
        compiler_params=_cp("parallel", "arbitrary"),
    )(qi_arr, ki_arr, proj, proj, proj, aq, ak)


def _fox_bwd(proj, dycat, aqb, ak, ad, name):
    s = proj.shape[0]
    nq = s // TQ
    qi_arr, ki_arr = _causal_pairs(nq, key_major=True)

    def body(qi_ref, ki_ref, q_ref, k_ref, v_ref, do_ref, aq_ref, ak_ref, ad_ref,
             dq_ref, dk_ref, dv_ref, qaux_ref, kaux_ref, dq_acc, qaux_acc, dk_acc, dv_acc, kaux_acc):
        t = pl.program_id(1)
        qi, ki = qi_ref[t], ki_ref[t]
        lane = lax.broadcasted_iota(jnp.int32, (1, 128), 1)
        masks = [lane < 64, lane >= 64]
        ones_v = jnp.where((lane & 63) < 3, 1.0, 0.0).astype(BF16)
        top = lax.broadcasted_iota(jnp.int32, (128, 1), 0) < 64

        @pl.when(qi == ki)
        def _():
            dk_acc[...] = jnp.zeros_like(dk_acc)
            dv_acc[...] = jnp.zeros_like(dv_acc)
            kaux_acc[...] = jnp.zeros_like(kaux_acc)

        def step(diag):
            q2s = q_ref[...] * 0.125
            k2, v2, do2 = k_ref[...], v_ref[...], do_ref[...]
            aq2, ak2, ad2 = aq_ref[...], ak_ref[...], ad_ref[...]
            ones2 = jnp.concatenate([ones_v] * STEP_PAIRS, axis=1)
            dq_new, qaux_new = [], []
            for pp in range(STEP_PAIRS):
                cols = slice(128 * pp, 128 * (pp + 1))
                dq, dk, dv = [], [], []
                for hh in range(2):
                    qh = jnp.where(masks[hh], q2s[:, cols], aq2[:, cols])
                    kh = jnp.where(masks[hh], k2[:, cols], ak2[:, cols])
                    doh = jnp.where(masks[hh], do2[:, cols], ad2[:, cols])
                    vh = jnp.where(masks[hh], v2[:, cols], ones2[:, cols])
                    sc = _dot(kh, qh, NT)
                    if diag:
                        key = lax.broadcasted_iota(jnp.int32, sc.shape, 0)
                        qry = lax.broadcasted_iota(jnp.int32, sc.shape, 1)
                        sc = jnp.where(qry >= key, sc, NEG)
                    p = jnp.exp(sc)
                    dsb = (p * _dot(vh, doh, NT)).astype(BF16)
                    dv.append(_dot(p.astype(BF16), doh, NN))
                    dk.append(_dot(dsb, qh, NN))
                    dq.append(_dot(kh, dsb, TN))
                dk_acc[pp] += jnp.where(masks[0], dk[0], dk[1])
                kaux_acc[pp] += jnp.where(masks[0], dk[1], dk[0])
                dv_acc[pp] += jnp.where(masks[0], dv[0], dv[1])
                dq_new.append(jnp.where(top, dq[0], dq[1]))
                qaux_new.append(jnp.where(top, dq[1], dq[0]))

            @pl.when(ki == 0)
            def _():
                for pp in range(STEP_PAIRS):
                    dq_acc[pp * nq + qi] = dq_new[pp]
                    qaux_acc[pp * nq + qi] = qaux_new[pp]

            @pl.when(ki > 0)
            def _():
                for pp in range(STEP_PAIRS):
                    dq_acc[pp * nq + qi] += dq_new[pp]
                    qaux_acc[pp * nq + qi] += qaux_new[pp]

        @pl.when(qi > ki)
        def _():
            step(False)

        @pl.when(qi == ki)
        def _():
            step(True)
            rows = pl.ds(pl.multiple_of(qi * TQ, TQ), TQ)
            for pp in range(STEP_PAIRS):
                dq_ref[rows, 128 * pp:128 * (pp + 1)] = (dq_acc[pp * nq + qi] * 0.125).T.astype(BF16)
                qaux_ref[pp, rows, :] = qaux_acc[pp * nq + qi].T

        @pl.when(qi == nq - 1)
        def _():
            for pp in range(STEP_PAIRS):
                dk_ref[:, 128 * pp:128 * (pp + 1)] = dk_acc[pp].astype(BF16)
                dv_ref[:, 128 * pp:128 * (pp + 1)] = dv_acc[pp].astype(BF16)
            kaux_ref[...] = kaux_acc[...]

    wide = STEP_PAIRS * 128
    grid_spec = pltpu.PrefetchScalarGridSpec(
        num_scalar_prefetch=2, grid=(HEAD_PAIRS // STEP_PAIRS, int(qi_arr.shape[0])),
        in_specs=[pl.BlockSpec((TQ, wide), lambda p, t, qi, ki: (qi[t], Q_BLK // STEP_PAIRS + p)),
                  pl.BlockSpec((TQ, wide), lambda p, t, qi, ki: (ki[t], K_BLK // STEP_PAIRS + p)),
                  pl.BlockSpec((TQ, wide), lambda p, t, qi, ki: (ki[t], V_BLK // STEP_PAIRS + p)),
                  pl.BlockSpec((TQ, wide), lambda p, t, qi, ki: (qi[t], Q_BLK // STEP_PAIRS + p)),
                  pl.BlockSpec((TQ, wide), lambda p, t, qi, ki: (qi[t], p)),
                  pl.BlockSpec((TQ, wide), lambda p, t, qi, ki: (ki[t], p)),
                  pl.BlockSpec((TQ, wide), lambda p, t, qi, ki: (qi[t], p))],
        out_specs=[pl.BlockSpec((s, wide), lambda p, t, qi, ki: (0, p)),
                   pl.BlockSpec((TQ, wide), lambda p, t, qi, ki: (ki[t], p)),
                   pl.BlockSpec((TQ, wide), lambda p, t, qi, ki: (ki[t], p)),
                   pl.BlockSpec((STEP_PAIRS, s, 128), lambda p, t, qi, ki: (p, 0, 0)),
                   pl.BlockSpec((STEP_PAIRS, TQ, 128), lambda p, t, qi, ki: (p, ki[t], 0))],
        scratch_shapes=[pltpu.VMEM((STEP_PAIRS * nq, 128, TQ), F32), pltpu.VMEM((STEP_PAIRS * nq, 128, TQ), F32),
                        pltpu.VMEM((STEP_PAIRS, TQ, 128), F32), pltpu.VMEM((STEP_PAIRS, TQ, 128), F32),
                        pltpu.VMEM((STEP_PAIRS, TQ, 128), F32)])
    return pl.pallas_call(
        body, name=name, grid_spec=grid_spec,
        out_shape=[jax.ShapeDtypeStruct((s, D_FOX), BF16)] * 3 + [jax.ShapeDtypeStruct((HEAD_PAIRS, s, 128), F32)] * 2,
        compiler_params=_cp("arbitrary", "arbitrary"),
    )(qi_arr, ki_arr, proj, proj, proj, dycat, aqb, ak, ad)


XA_SCALE = XA_DIM ** -0.5


def _xattn_fwd(q2, kv, name):
    s = q2.shape[0]
    m = kv.shape[0]

    def body(q_ref, kv_ref, o_ref):
        heads = [slice(h * XA_DIM, (h + 1) * XA_DIM) for h in range(XA_HEADS)]
        scs = [_dot(q_ref[:, cols], kv_ref[:, cols], NT) for cols in heads]
        for h in range(XA_HEADS):
            c0 = h * XA_DIM
            sc = scs[h] * XA_SCALE
            e = jnp.exp(sc - jnp.max(sc, axis=1, keepdims=True))
            p = e / jnp.sum(e, axis=1, keepdims=True)
            o_ref[:, c0:c0 + XA_DIM] = _dot(p.astype(BF16), kv_ref[:, D_MODEL + c0:D_MODEL + c0 + XA_DIM], NN).astype(BF16)

    return pl.pallas_call(
        body, name=name, grid=(s // TM,),
        in_specs=[_row_spec(TM, D_MODEL), pl.BlockSpec((m, 2 * D_MODEL), lambda i: (0, 0))],
        out_specs=_row_spec(TM, D_MODEL), out_shape=jax.ShapeDtypeStruct((s, D_MODEL), BF16),
        compiler_params=_cp("parallel"),
    )(q2, kv)


def _xattn_bwd(q2, kv, do2, name):
    s = q2.shape[0]
    m = kv.shape[0]

    def body(q_ref, kv_ref, do_ref, dq_ref, dkv_ref):
        i = pl.program_id(0)

        @pl.when(i == 0)
        def _():
            dkv_ref[...] = jnp.zeros_like(dkv_ref)

        heads = [slice(h * XA_DIM, (h + 1) * XA_DIM) for h in range(XA_HEADS)]
        scs = [_dot(kv_ref[:, cols], q_ref[:, cols], NT) for cols in heads]
        dps = [_dot(kv_ref[:, D_MODEL + cols.start:D_MODEL + cols.stop], do_ref[:, cols], NT) for cols in heads]
        for h in range(XA_HEADS):
            c0 = h * XA_DIM
            v0 = D_MODEL + c0
            qh = q_ref[:, c0:c0 + XA_DIM]
            kh = kv_ref[:, c0:c0 + XA_DIM]
            doh = do_ref[:, c0:c0 + XA_DIM]
            sc = scs[h] * XA_SCALE
            e = jnp.exp(sc - jnp.max(sc, axis=0, keepdims=True))
            p = e / jnp.sum(e, axis=0, keepdims=True)
            dp = dps[h]
            ds = p * (dp - jnp.sum(p * dp, axis=0, keepdims=True))
            dsb = (ds * XA_SCALE).astype(BF16)
            dq_ref[:, c0:c0 + XA_DIM] = _dot(kh, dsb, TN).T.astype(BF16)
            dkv_ref[:, c0:c0 + XA_DIM] += _dot(dsb, qh, NN)
            dkv_ref[:, v0:v0 + XA_DIM] += _dot(p.astype(BF16), doh, NN)

    return pl.pallas_call(
        body, name=name, grid=(s // TM,),
        in_specs=[_row_spec(TM, D_MODEL), pl.BlockSpec((m, 2 * D_MODEL), lambda i: (0, 0)), _row_spec(TM, D_MODEL)],
        out_specs=[_row_spec(TM, D_MODEL), pl.BlockSpec((m, 2 * D_MODEL), lambda i: (0, 0))],
        out_shape=[jax.ShapeDtypeStruct((s, D_MODEL), BF16), jax.ShapeDtypeStruct((m, 2 * D_MODEL), F32)],
        compiler_params=_cp("arbitrary"),
    )(q2, kv, do2)


GELU_C = math.sqrt(2.0 / math.pi)
GELU_A = 0.044715


def _gelu(x):
    return (0.5 * x) * (1.0 + jnp.tanh(x * (GELU_C * GELU_A * (x * x) + GELU_C)))


def _gelu_and_grad(x):
    x2 = x * x
    s = 1.0 + jnp.tanh(x * (GELU_C * GELU_A * x2 + GELU_C))
    hx = 0.5 * x
    return hx * s, s * (0.5 + hx * (2.0 - s) * (3.0 * GELU_C * GELU_A * x2 + GELU_C))


def _conv(h, s1, s2, w_ref, b_ref):
    return w_ref[0:1, :] * s2 + w_ref[1:2, :] * s1 + w_ref[2:3, :] * h + b_ref[...]


def _shift_down(main, prev8):
    row = lax.broadcasted_iota(jnp.int32, main.shape, 0)
    s1 = jnp.where(row == 0, prev8[7:8, :], pltpu.roll(main, 1, 0))
    s2 = jnp.where(row == 0, prev8[6:7, :], jnp.where(row == 1, prev8[7:8, :], pltpu.roll(main, 2, 0)))
    return s1, s2


def _shift_up(main, next8):
    n = main.shape[0]
    row = lax.broadcasted_iota(jnp.int32, main.shape, 0)
    u1 = jnp.where(row == n - 1, next8[0:1, :], pltpu.roll(main, n - 1, 0))
    u2 = jnp.where(row == n - 2, next8[0:1, :], jnp.where(row == n - 1, next8[1:2, :], pltpu.roll(main, n - 2, 0)))
    return u1, u2


def _ffn_fwd(h3, w_up, cw, cb, w_down, x2, tgt, g_post, name):
    s = h3.shape[0]
    tn = TN_FF
    nj = D_FF // tn
    per = D_MODEL // tn
    hb = TM // 8

    def body(h_ref, halo_ref, wg_ref, wu_ref, cwg_ref, cwu_ref, cbg_ref, cbu_ref, wd_ref, x_ref, t_ref, g_ref,
             hg_ref, hu_ref, cg_ref, cu_ref, a_ref, loss_ref, dx_ref, dy_ref, dg_ref, y_acc):
        i, j = pl.program_id(0), pl.program_id(1)
        h = h_ref[...]
        halo = halo_ref[...]
        halo = jnp.where(i > 0, halo, jnp.zeros_like(halo))
        hid = [(_dot(h, w_ref[...], NN), _dot(halo, w_ref[...], NN)) for w_ref in (wg_ref, wu_ref)]
        conv = []
        for (hm, hm_halo), cw_ref, cb_ref, hid_ref, c_ref in zip(hid, (cwg_ref, cwu_ref), (cbg_ref, cbu_ref),
                                                                  (hg_ref, hu_ref), (cg_ref, cu_ref)):
            hid_ref[...] = hm.astype(BF16)
            s1, s2 = _shift_down(hm, hm_halo)
            c = _conv(hm, s1, s2, cw_ref, cb_ref)
            c_ref[...] = c.astype(BF16)
            conv.append(c)
        a = (_gelu(conv[0]) * conv[1]).astype(BF16)
        a_ref[...] = a
        contrib = _dot(a, wd_ref[...], NN)

        @pl.when(j == 0)
        def _():
            y_acc[...] = contrib

        @pl.when(j > 0)
        def _():
            y_acc[...] += contrib

        @pl.when(j == nj - 1)
        def _():
            yv = y_acc[...]
            r = _rstd(yv)
            yn = yv * r
            e = x_ref[...] + yn * g_ref[...] - t_ref[...]
            part = 0.5 * jnp.sum(jnp.mean(e * e, axis=-1, keepdims=True), axis=0, keepdims=True)
            part = jnp.broadcast_to(part, (1, 128))
            dx = e * (1.0 / D_MODEL)
            dx_ref[...] = dx
            dy_ref[...] = _norm_bwd_rows(dx * g_ref[...], yn, r).astype(BF16)
            dg = jnp.sum(dx * yn, axis=0, keepdims=True)

            @pl.when(i == 0)
            def _():
                dg_ref[...] = dg
                loss_ref[...] = part

            @pl.when(i > 0)
            def _():
                dg_ref[...] += dg
                loss_ref[...] += part

    rows = pl.BlockSpec((TM, D_MODEL), lambda i, j: (i, 0))
    tile = pl.BlockSpec((TM, tn), lambda i, j: (i, j))
    wide = jax.ShapeDtypeStruct((s, D_FF), BF16)
    return pl.pallas_call(
        body, name=name, grid=(s // TM, nj),
        in_specs=[rows,
                  pl.BlockSpec((8, D_MODEL), lambda i, j: (jnp.maximum(i * hb - 1, 0), 0)),
                  pl.BlockSpec((None, D_MODEL, tn), lambda i, j: (j // per, 0, j % per)),
                  pl.BlockSpec((None, D_MODEL, tn), lambda i, j: (NDEV // 2 + j // per, 0, j % per)),
                  pl.BlockSpec((8, tn), lambda i, j: (0, j)),
                  pl.BlockSpec((8, tn), lambda i, j: (0, nj + j)),
                  pl.BlockSpec((1, tn), lambda i, j: (0, j)),
                  pl.BlockSpec((1, tn), lambda i, j: (0, nj + j)),
                  pl.BlockSpec((tn, D_MODEL), lambda i, j: (j, 0)),
                  rows, rows, pl.BlockSpec((1, D_MODEL), lambda i, j: (0, 0))],
        out_specs=[tile, tile, tile, tile, tile,
                   pl.BlockSpec((1, 128), lambda i, j: (0, 0)), rows, rows,
                   pl.BlockSpec((1, D_MODEL), lambda i, j: (0, 0))],
        out_shape=[wide, wide, wide, wide, wide,
                   jax.ShapeDtypeStruct((1, 128), F32), jax.ShapeDtypeStruct((s, D_MODEL), F32),
                   jax.ShapeDtypeStruct((s, D_MODEL), BF16), jax.ShapeDtypeStruct((1, D_MODEL), F32)],
        scratch_shapes=[pltpu.VMEM((TM, D_MODEL), F32)],
        compiler_params=_cp("arbitrary", "arbitrary"),
    )(h3, h3, w_up, w_up, cw, cw, cb, cb, w_down, x2, tgt, g_post)


def _ffn_bwd(dy3, w_down, hid_g, hid_u, conv_g, conv_u, cw, name):
    s = dy3.shape[0]
    n = s // TM
    tn = TN_FF
    nj = D_FF // tn
    hb = TM // 8
    last8 = s // 8 - 1

    def body(dy_ref, dyn_ref, wd_ref, hg_ref, hu_ref, cg_ref, cgn_ref, cu_ref, cun_ref, cwg_ref, cwu_ref,
             dhg_ref, dhu_ref, dcwg_ref, dcwu_ref, dcbg_ref, dcbu_ref):
        i = pl.program_id(1)
        first, last = i == 0, i == n - 1
        @pl.when(first)
        def _():
            for ref in (dcwg_ref, dcwu_ref, dcbg_ref, dcbu_ref):
                ref[...] = jnp.zeros_like(ref)

        dyn = dyn_ref[...]
        dyn = jnp.where(last, jnp.zeros_like(dyn), dyn)
        wc = tn // 4
        chunks = [slice(c, c + wc) for c in range(0, tn, wc)]
        das = [(_dot(dy_ref[...], wd_ref[cols, :], NT), _dot(dyn, wd_ref[cols, :], NT)) for cols in chunks]
        row8 = lax.broadcasted_iota(jnp.int32, (8, wc), 0)
        for cols, (da, da_n) in zip(chunks, das):
            c_g, c_u = cg_ref[:, cols].astype(F32), cu_ref[:, cols].astype(F32)
            g, dg = _gelu_and_grad(c_g)
            gn, dgn = _gelu_and_grad(cgn_ref[:, cols].astype(F32))
            outs = ((da * c_u * dg, da_n * cun_ref[:, cols].astype(F32) * dgn, hg_ref, cwg_ref, dhg_ref, dcwg_ref, dcbg_ref),
                    (da * g, da_n * gn, hu_ref, cwu_ref, dhu_ref, dcwu_ref, dcbu_ref))
            for dc, dcn, h_ref, cw_ref, dh_ref, dcw_ref, dcb_ref in outs:
                u1, u2 = _shift_up(dc, dcn)
                dh_ref[:, cols] = (cw_ref[2:3, cols] * dc + cw_ref[1:2, cols] * u1 + cw_ref[0:1, cols] * u2).astype(BF16)
                hm = h_ref[:, cols].astype(F32)
                dcb = jnp.sum(dc, axis=0, keepdims=True)
                dcw = jnp.where(row8 == 0, jnp.sum(hm * u2, axis=0, keepdims=True),
                                jnp.where(row8 == 1, jnp.sum(hm * u1, axis=0, keepdims=True),
                                          jnp.where(row8 == 2, jnp.sum(hm * dc, axis=0, keepdims=True), 0.0)))
                dcw_ref[:, cols] += dcw
                dcb_ref[:, cols] += dcb

    next8 = lambda j, i: (jnp.minimum((i + 1) * hb, last8), j)
    blk = lambda j, i: (i, j)
    col = lambda j, i: (0, j)
    colu = lambda j, i: (0, nj + j)
    tile = pl.BlockSpec((TM, tn), blk)
    return pl.pallas_call(
        body, name=name, grid=(nj, n),
        in_specs=[pl.BlockSpec((TM, D_MODEL), lambda j, i: (i, 0)),
                  pl.BlockSpec((8, D_MODEL), lambda j, i: (jnp.minimum((i + 1) * hb, last8), 0)),
                  pl.BlockSpec((tn, D_MODEL), lambda j, i: (j, 0)),
                  tile, tile, tile, pl.BlockSpec((8, tn), next8), tile, pl.BlockSpec((8, tn), next8),
                  pl.BlockSpec((8, tn), col), pl.BlockSpec((8, tn), colu)],
        out_specs=[tile, tile, pl.BlockSpec((8, tn), col), pl.BlockSpec((8, tn), col),
                   pl.BlockSpec((1, tn), col), pl.BlockSpec((1, tn), col)],
        out_shape=[jax.ShapeDtypeStruct((s, D_FF), BF16), jax.ShapeDtypeStruct((s, D_FF), BF16),
                   jax.ShapeDtypeStruct((8, D_FF), F32), jax.ShapeDtypeStruct((8, D_FF), F32),
                   jax.ShapeDtypeStruct((1, D_FF), F32), jax.ShapeDtypeStruct((1, D_FF), F32)],
        compiler_params=_cp("parallel", "arbitrary"),
    )(dy3, dy3, w_down, hid_g, hid_u, conv_g, conv_g, conv_u, conv_u, cw, cw)


def _slot(p):
    return 4 * p[0] + 2 * p[1] + p[2]


def _all_gather(shards, name):
    n = len(shards)

    def body(*refs):
        ins, outs = refs[:n], refs[n:2 * n]
        send_sems, recv_sems, local_sems = refs[2 * n:]
        x, y, c = lax.axis_index("x"), lax.axis_index("y"), lax.axis_index("c")
        me, sibling = (x, y, c), (x, y, 1 - c)
        chips = [(1 - x, y), (x, 1 - y), (1 - x, 1 - y)]

        def copy(a, k, block, to, from_input=False):
            dst = outs[a].at[_slot(block)]
            return pltpu.make_async_remote_copy(
                src_ref=ins[a] if from_input else dst, dst_ref=dst,
                send_sem=send_sems.at[a, k], recv_sem=recv_sems.at[a, k],
                device_id=to, device_id_type=MESH)

        mine = [pltpu.make_async_copy(ins[a], outs[a].at[_slot(me)], local_sems.at[a]) for a in range(n)]
        for cp in mine:
            cp.start()
        first = []
        for a in range(n):
            first.append(copy(a, 0, me, sibling, True))
            first += [copy(a, 1 + j, me, (*chip, c), True) for j, chip in enumerate(chips)]
        for cp in first:
            cp.start()
        passed = []
        for j, chip in enumerate(chips):
            for a in range(n):
                copy(a, 1 + j, (*chip, c), me).wait_recv()
                fwd = copy(a, 4 + j, (*chip, c), sibling)
                fwd.start()
                passed.append(fwd)
        for a in range(n):
            copy(a, 0, sibling, me).wait_recv()
            for j, chip in enumerate(chips):
                copy(a, 4 + j, (*chip, 1 - c), me).wait_recv()
        for cp in first + passed:
            cp.wait_send()
        for cp in mine:
            cp.wait()

    any_spec = pl.BlockSpec(memory_space=pl.ANY)
    return pl.pallas_call(
        body, name=name,
        in_specs=[any_spec] * n, out_specs=[any_spec] * n,
        out_shape=[jax.ShapeDtypeStruct((NDEV,) + s.shape, s.dtype) for s in shards],
        scratch_shapes=[pltpu.SemaphoreType.DMA((n, 7)), pltpu.SemaphoreType.DMA((n, 7)),
                        pltpu.SemaphoreType.DMA((n,))],
    )(*shards)


def _peer_list(x, y, c):
    return [(1 - x if m & 4 else x, 1 - y if m & 2 else y, 1 - c if m & 1 else c) for m in range(1, NDEV)]


def _exchange_copies(src_refs, land_refs, send_sems, recv_sems, gather):
    x, y, c = lax.axis_index("x"), lax.axis_index("y"), lax.axis_index("c")
    me = (x, y, c)
    copies = []
    for m, peer in enumerate(_peer_list(x, y, c)):
        for a in range(len(src_refs)):
            copies.append(pltpu.make_async_remote_copy(
                src_ref=src_refs[a] if gather else src_refs[a].at[_slot(peer)], dst_ref=land_refs[a].at[_slot(me)],
                send_sem=send_sems.at[a * (NDEV - 1) + m], recv_sem=recv_sems.at[a * (NDEV - 1) + m],
                device_id=peer, device_id_type=MESH))
    return copies


def _all_gather_small(shards, name):
    n = len(shards)

    def body(*refs):
        ins, outs = refs[:n], refs[n:2 * n]
        send_sems, recv_sems, local_sems = refs[2 * n:]
        me = (lax.axis_index("x"), lax.axis_index("y"), lax.axis_index("c"))
        mine = [pltpu.make_async_copy(ins[a], outs[a].at[_slot(me)], local_sems.at[a]) for a in range(n)]
        copies = _exchange_copies(ins, outs, send_sems, recv_sems, True)
        for cp in mine + copies:
            cp.start()
        for cp in copies + mine:
            cp.wait()

    any_spec = pl.BlockSpec(memory_space=pl.ANY)
    return pl.pallas_call(
        body, name=name,
        in_specs=[any_spec] * n, out_specs=[any_spec] * n,
        out_shape=[jax.ShapeDtypeStruct((NDEV,) + s.shape, s.dtype) for s in shards],
        scratch_shapes=[pltpu.SemaphoreType.DMA((n * (NDEV - 1),)), pltpu.SemaphoreType.DMA((n * (NDEV - 1),)),
                        pltpu.SemaphoreType.DMA((n,))],
    )(*shards)


def _exchange_start(srcs, lands, after, gather, name):
    n = len(srcs)
    hbm = pl.BlockSpec(memory_space=pltpu.HBM)

    def body(*refs):
        for cp in _exchange_copies(refs[:n], refs[n:2 * n], refs[2 * n + 1], refs[2 * n + 2], gather):
            cp.start()
        token = refs[-1]
        token[...] = jnp.zeros_like(token)

    outs = pl.pallas_call(
        body, name=name,
        out_shape=(pltpu.SemaphoreType.DMA((n * (NDEV - 1),)), pltpu.SemaphoreType.DMA((n * (NDEV - 1),)),
                   *[pltpu.HBM(a.shape, a.dtype) for a in list(srcs) + list(lands)],
                   jax.ShapeDtypeStruct((8, 128), F32)),
        in_specs=[hbm] * (2 * n) + [pl.BlockSpec(memory_space=pl.ANY)],
        out_specs=(pl.BlockSpec(memory_space=pltpu.SEMAPHORE), pl.BlockSpec(memory_space=pltpu.SEMAPHORE),
                   *[hbm] * (2 * n), pl.BlockSpec(memory_space=pltpu.VMEM)),
        input_output_aliases={i: 2 + i for i in range(2 * n)},
        compiler_params=pltpu.CompilerParams(has_side_effects=pltpu.SideEffectType.DATAFLOW_SIDE_EFFECTING),
    )(*[pltpu.with_memory_space_constraint(a, pltpu.HBM) for a in list(srcs) + list(lands)], after)
    return outs[0], outs[1], outs[2:2 + n], outs[2 + n:2 + 2 * n], outs[-1]


def _exchange_wait(send_sems, recv_sems, srcs, lands, after, gather, name):
    n = len(srcs)
    hbm = pl.BlockSpec(memory_space=pltpu.HBM)

    def body(*refs):
        for cp in _exchange_copies(refs[:n], refs[n:2 * n], refs[2 * n], refs[2 * n + 1], gather):
            cp.wait_send()
            cp.wait_recv()

    outs = pl.pallas_call(
        body, name=name,
        out_shape=tuple(pltpu.HBM(a.shape, a.dtype) for a in list(srcs) + list(lands)),
        in_specs=[hbm] * (2 * n) + [pl.BlockSpec(memory_space=pltpu.SEMAPHORE)] * 2 + [pl.BlockSpec(memory_space=pl.ANY)],
        out_specs=tuple([hbm] * (2 * n)),
        input_output_aliases={i: i for i in range(2 * n)},
        compiler_params=pltpu.CompilerParams(has_side_effects=pltpu.SideEffectType.DATAFLOW_SIDE_EFFECTING),
    )(*srcs, *lands, send_sems, recv_sems, after)
    return outs[n:]


def _own_slot(block):
    me = 4 * lax.axis_index("x") + 2 * lax.axis_index("y") + lax.axis_index("c")
    return lax.dynamic_update_slice(lax.empty((NDEV,) + block.shape, block.dtype), block[None], (me, 0, 0))


def _adam_update(p_ref, w_ref, m_ref, v_ref, g_ref, d_ref, mo_ref, vo_ref):
    bc1 = 1.0 - ADAM_B1 ** ADAM_STEP
    bc2 = 1.0 - ADAM_B2 ** ADAM_STEP
    g = p_ref[0].astype(F32)
    for d in range(1, NDEV):
        g = g + p_ref[d].astype(F32)
    g_ref[...] = g
    mn = ADAM_B1 * m_ref[...] + (1.0 - ADAM_B1) * g
    vn = ADAM_B2 * v_ref[...] + (1.0 - ADAM_B2) * (g * g)
    mo_ref[...] = mn
    vo_ref[...] = vn
    d_ref[...] = -ADAM_LR * ((mn / bc1) / (jnp.sqrt(vn / bc2) + ADAM_EPS) + ADAM_WD * w_ref[...])


def _adamw_small(parts, ws, ms, vs, name):
    n = len(ws)

    def body(*refs):
        ins, outs = refs[:4 * n], refs[4 * n:]
        for k in range(n):
            _adam_update(ins[k], ins[n + k], ins[2 * n + k], ins[3 * n + k], *outs[4 * k:4 * k + 4])

    whole = pl.BlockSpec(memory_space=pltpu.VMEM)
    res = pl.pallas_call(
        body, name=name, in_specs=[whole] * (4 * n), out_specs=[whole] * (4 * n),
        out_shape=[jax.ShapeDtypeStruct(a.shape, F32) for a in ws for _ in range(4)],
    )(*parts, *ws, *ms, *vs)
    return [res[4 * k:4 * k + 4] for k in range(n)]


def _adamw(parts, w, m, v, name):
    r, c = w.shape
    tr = r if r * c <= 160 * 1024 else max(8, (160 * 1024 // c) // 8 * 8)
    while r % tr:
        tr -= 8
    body = functools.partial(_adam_update)
    spec = pl.BlockSpec((tr, c), lambda i: (i, 0))
    return pl.pallas_call(
        body, name=name, grid=(r // tr,),
        in_specs=[pl.BlockSpec((NDEV, tr, c), lambda i: (0, i, 0)), spec, spec, spec],
        out_specs=[spec] * 4, out_shape=[jax.ShapeDtypeStruct((r, c), F32)] * 4,
        compiler_params=_cp("parallel"),
    )(parts, w, m, v)


def _local_step(x, mem, tgt, gains, b_forget, w_pool, pool_scale, conv_b, w_in,
                mix_weights, ffn_weights, send_in_grad, send_mix_grads, send_ffn_grads):
    b_pad = jnp.pad(b_forget, ((0, 0), (0, 128 - FOX_HEADS)))
    wbd = jnp.zeros((D_POOL, D_POOL), F32)
    for g in range(4):
        wbd = wbd.at[64 * g:64 * g + 64, 64 * g:64 * g + 64].set(w_pool[g])
    wbd = wbd.astype(BF16)
    scale = pool_scale.reshape(1, D_POOL)

    h1, proj, fraw = _proj_in(x, gains["mix_pre"], w_in, "proj_in")
    flog, aq, ak = _gate_cumsum(fraw, b_pad, "gate_cumsum")
    ycat, aqb = _fox_fwd(proj, aq, ak, "fox_fwd")
    ycat = _pool_fwd(proj, wbd, scale, ycat, "pool_fwd")
    w_mix, w_xq, w_xo, w_xkv = mix_weights(ycat)
    y1, x1, h2 = _mm_rows(ycat, w_mix, "nn", 1024, "mix_out", [x], [gains["mix_post"], gains["xa_pre"]],
                          [F32, F32, BF16], _epi_resid)
    q2 = _mm(h2, w_xq, "nn", BF16, 2048, 1024, 1024, "xa_q")
    mem_n = _norm_fwd(mem, gains["mem"], "norm_mem")
    kv = _mm(mem_n, w_xkv, "nn", BF16, mem.shape[0], 256, 1024, "xa_kv", b_cols=256)
    o2 = _xattn_fwd(q2, kv, "xattn_fwd")
    y2, x2, h3 = _mm_rows(o2, w_xo, "nn", 1024, "xa_out", [x1], [gains["xa_post"], gains["ffn_pre"]],
                          [F32, F32, BF16], _epi_resid)
    w_up, w_down, cw = ffn_weights(h3)
    hid_g, hid_u, conv_g, conv_u, act, loss, dx3, dy3, dg_ffn_post = _ffn_fwd(
        h3, w_up, cw, conv_b, w_down, x2, tgt, gains["ffn_post"], "ffn_fwd")

    dhid_g, dhid_u, dcw_g, dcw_u, dcb_g, dcb_u = _ffn_bwd(dy3, w_down, hid_g, hid_u, conv_g, conv_u, cw, "ffn_bwd")
    d_w_down = _mm(act, dy3, "tn", BF16, 2048, 1024, 1024, "dw_down")
    d_w_up = _mm(h3, [dhid_g, dhid_u], "tn", BF16, 1024, 1024, 2048, "dw_up", out_cols=1024)
    sent = send_ffn_grads(d_w_up, d_w_down, jnp.concatenate([dcw_g, dcw_u], axis=1))
    dh3 = _mm([dhid_g, dhid_u], w_up, "nt", F32, 2048, 1024, 1024, "dh_ffn", b_cols=1024, after=sent)
    dx2, dg_ffn_pre, dy2, dg_xa_post = _norm_bwd(dh3, x2, dx3, gains["ffn_pre"], "norm_bwd_ffn",
                                                 prev=(y2, gains["xa_post"]))
    do2 = _mm(dy2, w_xo, "nt", BF16, 2048, 1024, 1024, "d_xa_out")
    d_w_xo = _mm(o2, dy2, "tn", BF16, 1024, 1024, 1024, "dw_xo")
    dq2, dkv = _xattn_bwd(q2, kv, do2, "xattn_bwd")
    dkv = dkv.astype(BF16)
    dx1, dg_xa_pre, dy1, dg_mix_post = _mm_rows(
        dq2, w_xq, "nt", 1024, "dh_xa", [x1, dx2, y1], [gains["xa_pre"], gains["mix_post"]],
        [F32, "sum", BF16, "sum"], _epi_norm_bwd)
    d_w_xq = _mm(h2, dq2, "tn", BF16, 1024, 1024, 1024, "dw_xq")
    dmem_n = _mm(dkv, w_xkv, "nt", F32, mem.shape[0], 1024, 256, "d_mem", b_cols=256)
    d_w_xkv = _mm(mem_n, dkv, "tn", BF16, 1024, 256, mem.shape[0], "dw_xkv", out_cols=256)
    _, dg_mem = _norm_bwd(dmem_n, mem, jnp.zeros_like(mem), gains["mem"], "norm_bwd_mem")
    dycat = _mm(dy1, w_mix, "nt", BF16, 2048, 1024, 1024, "d_mix_out")
    d_w_mix = _mm(ycat, dy1, "tn", BF16, 1024, 1024, 1024, "dw_mix")
    sent_mix = send_mix_grads(d_w_mix, d_w_xq, d_w_xo, d_w_xkv)
    ad = _fox_do_operand(dycat, ycat, sent_mix, "fox_do_operand")
    dq, dk, dv, qaux, kaux = _fox_bwd(proj, dycat, aqb, ak, ad, "fox_bwd")
    du, d_wbd, d_scale = _pool_bwd(proj, dycat, wbd, scale, "pool_bwd")
    df, db_f = _gate_bwd(qaux, kaux, flog, "gate_bwd")
    dproj = [du, dq, dk, dv, df]
    sent_in = send_in_grad(_dw_in(h1, dproj, "dw_in"))
    grad_x, dg_mix_pre = _mm_rows(dproj, w_in, "nt", None, "dh_mix", [x, dx1], [gains["mix_pre"]],
                                  [F32, "sum"], _epi_norm_bwd, after=sent_in)

    small = dict(
        mix_pre=dg_mix_pre, mix_post=dg_mix_post, mem=dg_mem, xa_pre=dg_xa_pre, xa_post=dg_xa_post,
        ffn_pre=dg_ffn_pre, ffn_post=dg_ffn_post,
        conv_b=jnp.concatenate([dcb_g, dcb_u], axis=1),
        w_pool=jnp.concatenate([d_wbd[64 * g:64 * g + 64, 64 * g:64 * g + 64] for g in range(4)], axis=0),
        pool_scale=d_scale.reshape(4, 64),
        b_forget=db_f[:, :FOX_HEADS],
    )
    return loss, grad_x, small


SMALL_ORDER = ("mix_pre", "mix_post", "mem", "xa_pre", "xa_post", "ffn_pre", "ffn_post", "conv_b",
               "w_pool", "pool_scale", "b_forget")


def kernel(x, mem, norm_mix_pre, norm_mix_post, w_in, b_forget, w_pool, pool_scale, w_mix_out, norm_mem, norm_xa_pre, norm_xa_post, w_xq, w_xkv, w_xo, norm_ffn_pre, norm_ffn_post, w_up, conv_w, conv_b, w_down, loss_target, m_norm_mix_pre, m_norm_mix_post, m_w_in, m_b_forget, m_w_pool, m_pool_scale, m_w_mix_out, m_norm_mem, m_norm_xa_pre, m_norm_xa_post, m_w_xq, m_w_xkv, m_w_xo, m_norm_ffn_pre, m_norm_ffn_post, m_w_up, m_conv_w, m_conv_b, m_w_down, v_norm_mix_pre, v_norm_mix_post, v_w_in, v_b_forget, v_w_pool, v_pool_scale, v_w_mix_out, v_norm_mem, v_norm_xa_pre, v_norm_xa_post, v_w_xq, v_w_xkv, v_w_xo, v_norm_ffn_pre, v_norm_ffn_post, v_w_up, v_conv_w, v_conv_b, v_w_down):
    names = ("norm_mix_pre", "norm_mix_post", "w_in", "b_forget", "w_pool", "pool_scale", "w_mix_out", "norm_mem",
             "norm_xa_pre", "norm_xa_post", "w_xq", "w_xkv", "w_xo", "norm_ffn_pre", "norm_ffn_post", "w_up",
             "conv_w", "conv_b", "w_down")
    w = dict(zip(names, (norm_mix_pre, norm_mix_post, w_in, b_forget, w_pool, pool_scale, w_mix_out, norm_mem,
                         norm_xa_pre, norm_xa_post, w_xq, w_xkv, w_xo, norm_ffn_pre, norm_ffn_post, w_up,
                         conv_w, conv_b, w_down)))
    mo = dict(zip(names, (m_norm_mix_pre, m_norm_mix_post, m_w_in, m_b_forget, m_w_pool, m_pool_scale, m_w_mix_out,
                          m_norm_mem, m_norm_xa_pre, m_norm_xa_post, m_w_xq, m_w_xkv, m_w_xo, m_norm_ffn_pre,
                          m_norm_ffn_post, m_w_up, m_conv_w, m_conv_b, m_w_down)))
    vo = dict(zip(names, (v_norm_mix_pre, v_norm_mix_post, v_w_in, v_b_forget, v_w_pool, v_pool_scale, v_w_mix_out,
                          v_norm_mem, v_norm_xa_pre, v_norm_xa_post, v_w_xq, v_w_xkv, v_w_xo, v_norm_ffn_pre,
                          v_norm_ffn_post, v_w_up, v_conv_w, v_conv_b, v_w_down)))

    big_names = ("w_in", "w_mix_out", "w_xq", "w_xo", "w_xkv", "w_up", "w_down")
    shards = {k: w[k][0].astype(BF16) for k in big_names}
    shards["w_in"] = jnp.pad(shards["w_in"], ((0, 0), (0, D_IN_PAD - shards["w_in"].shape[1])))
    conv_w_sh = jnp.pad(conv_w[0, :, 0, :], ((0, 5), (0, 0)))
    (g_in,) = _all_gather([shards["w_in"]], "gather_w_in")
    mix_srcs = [shards[k] for k in ("w_mix_out", "w_xq", "w_xo", "w_xkv")]
    mix_flight = _exchange_start(mix_srcs, [_own_slot(a) for a in mix_srcs], g_in, True, "gather_mix_start")
    ffn_srcs = [shards["w_up"], shards["w_down"], conv_w_sh]
    ffn_flight = _exchange_start(ffn_srcs, [_own_slot(a) for a in ffn_srcs], mix_flight[4], True, "gather_ffn_start")
    my_slot = 4 * lax.axis_index("x") + 2 * lax.axis_index("y") + lax.axis_index("c")
    own_block = lambda a: _own_slot(lax.dynamic_index_in_dim(a, my_slot, 0, keepdims=False))
    by_rows = lambda a: a.reshape(NDEV, a.shape[0] // NDEV, a.shape[1])
    by_cols = lambda a: a.reshape(a.shape[0], NDEV, a.shape[1] // NDEV).transpose(1, 0, 2)
    grad_flight = {}

    def mix_weights(after):
        g_mix, g_xq, g_xo, g_xkv = _exchange_wait(*mix_flight[:4], after, True, "gather_mix_wait")
        return (g_mix.reshape(D_MODEL, D_MODEL), g_xq.reshape(D_MODEL, D_MODEL), g_xo.reshape(D_MODEL, D_MODEL), g_xkv)

    def ffn_weights(after):
        g_up, g_down, g_cw = _exchange_wait(*ffn_flight[:4], after, True, "gather_ffn_wait")
        return g_up, g_down.reshape(D_FF, D_MODEL), g_cw.transpose(1, 0, 2).reshape(8, 2 * D_FF)

    def send_ffn_grads(d_w_up, d_w_down, d_cw):
        srcs = [d_w_up, by_rows(d_w_down), by_cols(d_cw)]
        grad_flight["ffn"] = _exchange_start(srcs, [own_block(a) for a in srcs], ffn_flight[4], False, "scatter_ffn_start")
        return grad_flight["ffn"][4]

    def send_mix_grads(d_w_mix, d_w_xq, d_w_xo, d_w_xkv):
        srcs = [by_rows(d_w_mix), by_rows(d_w_xq), by_rows(d_w_xo), d_w_xkv]
        grad_flight["mix"] = _exchange_start(srcs, [own_block(a) for a in srcs], ffn_flight[4], False, "scatter_mix_start")
        return grad_flight["mix"][4]

    def send_in_grad(d_w_in):
        srcs = [by_rows(d_w_in)]
        grad_flight["in"] = _exchange_start(srcs, [own_block(a) for a in srcs], ffn_flight[4], False, "scatter_in_start")
        return grad_flight["in"][4]

    gains = dict(mix_pre=norm_mix_pre + ffn_flight[4][0, 0], mix_post=norm_mix_post, mem=norm_mem, xa_pre=norm_xa_pre,
                 xa_post=norm_xa_post, ffn_pre=norm_ffn_pre, ffn_post=norm_ffn_post)
    loss, grad_x, small = _local_step(
        x[0], mem[0], loss_target[0], gains, b_forget, w_pool[0], pool_scale[0], conv_b,
        g_in.reshape(D_MODEL, D_IN_PAD), mix_weights, ffn_weights, send_in_grad, send_mix_grads, send_ffn_grads)

    p_up, p_down, p_cw = _exchange_wait(*grad_flight["ffn"][:4], grad_x, False, "scatter_ffn_wait")
    p_mix, p_xq, p_xo, p_xkv = _exchange_wait(*grad_flight["mix"][:4], grad_x, False, "scatter_mix_wait")
    parts = dict(w_mix_out=p_mix, w_xq=p_xq, w_xo=p_xo, w_xkv=p_xkv, w_up=p_up, w_down=p_down)
    *small_parts, loss_parts = _all_gather_small([small[k] for k in SMALL_ORDER] + [loss], "gather_small_grads")

    res = {k: [a[None] for a in _adamw(p, w[k][0], mo[k][0], vo[k][0], "adamw_" + k)] for k, p in parts.items()}
    pad_cw = lambda a: jnp.pad(a[0, :, 0, :], ((0, 5), (0, 0)))
    res["conv_w"] = [a[:3][None, :, None, :] for a in
                     _adamw(p_cw, pad_cw(conv_w), pad_cw(m_conv_w), pad_cw(v_conv_w), "adamw_conv_w")]
    key_of = dict(mix_pre="norm_mix_pre", mix_post="norm_mix_post", mem="norm_mem", xa_pre="norm_xa_pre",
                  xa_post="norm_xa_post", ffn_pre="norm_ffn_pre", ffn_post="norm_ffn_post", conv_b="conv_b",
                  w_pool="w_pool", pool_scale="pool_scale", b_forget="b_forget")
    flat2d = lambda src: [src[key_of[k]].reshape(small[k].shape) for k in SMALL_ORDER]
    small_out = _adamw_small(small_parts, flat2d(w), flat2d(mo), flat2d(vo), "adamw_small")
    for k, four in zip(SMALL_ORDER, small_out):
        res[key_of[k]] = [a.reshape(w[key_of[k]].shape) for a in four]
    (p_in,) = _exchange_wait(*grad_flight["in"][:4], res["w_up"][1], False, "scatter_in_wait")
    res["w_in"] = [a[None] for a in _adamw(p_in[:, :, :w_in.shape[2]], w["w_in"][0], mo["w_in"][0], vo["w_in"][0],
                                           "adamw_w_in")]

    outs = [jnp.sum(loss_parts[:, 0, 0]), grad_x[None]]
    for idx in range(4):
        outs += [res[k][idx] for k in names]
    return tuple(outs)
```

```python
import functools
import math

import jax
import jax.numpy as jnp
from jax import lax
from jax.experimental import pallas as pl
from jax.experimental.pallas import tpu as pltpu

F32 = jnp.float32
BF16 = jnp.bfloat16

NDEV = 8
D_MODEL = 1024
D_POOL = 256
D_FOX = 768
FOX_HEADS = 12
HEAD_PAIRS = FOX_HEADS // 2
XA_HEADS = 4
XA_DIM = 256
D_FF = 4096
D_IN_PAD = 2688
F_COL = 2560
POOL_HALO = 16
NORM_EPS = 1e-6
NEG = -1e30

ADAM_LR = 0.001
ADAM_B1 = 0.9
ADAM_B2 = 0.999
ADAM_EPS = 1e-08
ADAM_WD = 0.01
ADAM_STEP = 10

TM = 512
TQ = 512
TN_FF = 1024
VMEM_LIMIT = 56 * 1024 * 1024
MESH = pl.DeviceIdType.MESH


def _cp(*sem):
    return pltpu.CompilerParams(dimension_semantics=sem, vmem_limit_bytes=VMEM_LIMIT)


def _dot(a, b, dims):
    return lax.dot_general(a, b, (dims, ((), ())), preferred_element_type=F32)


NN = ((1,), (0,))
NT = ((1,), (1,))
TN = ((0,), (0,))


def _mm(a, b, mode, out_dtype, tm, tn, tk, name, b_cols=None, out_cols=None, after=None):
    a_list = list(a) if isinstance(a, (list, tuple)) else [a]
    b_list = list(b) if isinstance(b, (list, tuple)) else [b]
    assert len(a_list) == 1 or len(b_list) == 1
    if mode == "tn":
        K, M = a_list[0].shape
        assert len(a_list) == 1
        Ns = [x.shape[1] for x in b_list]
        N = sum(Ns)
        assert b_cols is None
    else:
        assert len(b_list) == 1
        M = a_list[0].shape[0]
        Ks = [x.shape[1] for x in a_list]
        K = sum(Ks)
        if b_cols is None:
            N = b_list[0].shape[0] if mode == "nt" else b_list[0].shape[1]
        else:
            N = b_list[0].shape[1] if mode == "nt" else NDEV * b_cols
    assert M % tm == 0 and N % tn == 0 and K % tk == 0, (name, M, N, K)
    grid = (M // tm, N // tn, K // tk)
    nk = grid[2]
    dims = {"nn": NN, "nt": NT, "tn": TN}[mode]

    in_specs = []
    if mode == "tn":
        in_specs.append(pl.BlockSpec((tk, tm), lambda i, j, k: (k, i)))
        if len(b_list) == 1:
            in_specs.append(pl.BlockSpec((tk, tn), lambda i, j, k: (k, j)))
        else:
            nj1 = Ns[0] // tn
            in_specs.append(pl.BlockSpec((tk, tn), lambda i, j, k: (k, jnp.minimum(j, nj1 - 1))))
            in_specs.append(pl.BlockSpec((tk, tn), lambda i, j, k: (k, jnp.maximum(j - nj1, 0))))
    else:
        if len(a_list) == 1:
            in_specs.append(pl.BlockSpec((tm, tk), lambda i, j, k: (i, k)))
        else:
            nk1 = Ks[0] // tk
            in_specs.append(pl.BlockSpec((tm, tk), lambda i, j, k: (i, jnp.minimum(k, nk1 - 1))))
            in_specs.append(pl.BlockSpec((tm, tk), lambda i, j, k: (i, jnp.maximum(k - nk1, 0))))
        if b_cols is None:
            if mode == "nn":
                in_specs.append(pl.BlockSpec((tk, tn), lambda i, j, k: (k, j)))
            else:
                in_specs.append(pl.BlockSpec((tn, tk), lambda i, j, k: (j, k)))
        else:
            if mode == "nn":
                per = b_cols // tn
                in_specs.append(pl.BlockSpec((None, tk, tn), lambda i, j, k: (j // per, k, j % per)))
            else:
                per = b_cols // tk
                in_specs.append(pl.BlockSpec((None, tn, tk), lambda i, j, k: (k // per, j, k % per)))
    if out_cols is None:
        out_spec = pl.BlockSpec((tm, tn), lambda i, j, k: (i, j))
        out_shape = jax.ShapeDtypeStruct((M, N), out_dtype)
    else:
        pero = out_cols // tn
        out_spec = pl.BlockSpec((None, tm, tn), lambda i, j, k: (j // pero, i, j % pero))
        out_shape = jax.ShapeDtypeStruct((NDEV, M, out_cols), out_dtype)

    two_a = len(a_list) == 2
    two_b = len(b_list) == 2
    extra = []
    if after is not None:
        in_specs.append(pl.BlockSpec(memory_space=pl.ANY))
        extra.append(after)

    def body(*refs):
        o_ref, acc_ref = refs[-2], refs[-1]
        j = pl.program_id(1)
        k = pl.program_id(2)

        @pl.when(k == 0)
        def _():
            acc_ref[...] = jnp.zeros_like(acc_ref)

        if two_a:
            a1, a2, b1 = refs[0], refs[1], refs[2]
            nk1_ = Ks[0] // tk

            @pl.when(k < nk1_)
            def _():
                acc_ref[...] += _dot(a1[...], b1[...], dims)

            @pl.when(k >= nk1_)
            def _():
                acc_ref[...] += _dot(a2[...], b1[...], dims)
        elif two_b:
            a1, b1, b2 = refs[0], refs[1], refs[2]
            nj1_ = Ns[0] // tn

            @pl.when(j < nj1_)
            def _():
                acc_ref[...] += _dot(a1[...], b1[...], dims)

            @pl.when(j >= nj1_)
            def _():
                acc_ref[...] += _dot(a1[...], b2[...], dims)
        else:
            acc_ref[...] += _dot(refs[0][...], refs[1][...], dims)

        @pl.when(k == nk - 1)
        def _():
            o_ref[...] = acc_ref[...].astype(o_ref.dtype)

    return pl.pallas_call(
        body, name=name, grid=grid, in_specs=in_specs, out_specs=out_spec, out_shape=out_shape,
        scratch_shapes=[pltpu.VMEM((tm, tn), F32)],
        compiler_params=_cp("parallel", "parallel", "arbitrary"),
    )(*a_list, *b_list, *extra)


def _rstd(x):
    return lax.rsqrt(jnp.mean(x * x, axis=-1, keepdims=True) + NORM_EPS)


def _norm_bwd_rows(dxn, xn, r):
    return r * (dxn - xn * jnp.mean(dxn * xn, axis=-1, keepdims=True))


def _row_spec(tm, d):
    return pl.BlockSpec((tm, d), lambda i: (i, 0))


def _vec_spec(d):
    return pl.BlockSpec((1, d), lambda i: (0, 0))


def _mm_rows(a, b, mode, tk, name, rows, vecs, outs, epilogue, b_cols=None, after=None):
    a_list = list(a) if isinstance(a, (list, tuple)) else [a]
    m = a_list[0].shape[0]
    ks = [x.shape[1] for x in a_list]
    n = D_MODEL
    pieces = tk is None
    nk = 1 if pieces else sum(ks) // tk
    dims = NN if mode == "nn" else NT
    if pieces:
        assert mode == "nt" and b_cols is None
        in_specs = [pl.BlockSpec((TM, kp), lambda i, k: (i, 0)) for kp in ks]
        tk = sum(ks)
    elif len(a_list) == 1:
        in_specs = [pl.BlockSpec((TM, tk), lambda i, k: (i, k))]
    else:
        nk1 = ks[0] // tk
        in_specs = [pl.BlockSpec((TM, tk), lambda i, k: (i, jnp.minimum(k, nk1 - 1))),
                    pl.BlockSpec((TM, tk), lambda i, k: (i, jnp.maximum(k - nk1, 0)))]
    if mode == "nn":
        in_specs.append(pl.BlockSpec((tk, n), lambda i, k: (k, 0)))
    elif b_cols is None:
        in_specs.append(pl.BlockSpec((n, tk), lambda i, k: (0, k)))
    else:
        per = b_cols // tk
        in_specs.append(pl.BlockSpec((None, n, tk), lambda i, k: (k // per, 0, k % per)))
    in_specs += [pl.BlockSpec((TM, n), lambda i, k: (i, 0))] * len(rows)
    in_specs += [pl.BlockSpec((1, n), lambda i, k: (0, 0))] * len(vecs)
    extra = []
    if after is not None:
        in_specs.append(pl.BlockSpec(memory_space=pl.ANY))
        extra.append(after)
    out_specs, out_shape = [], []
    for o in outs:
        if o == "sum":
            out_specs.append(pl.BlockSpec((1, n), lambda i, k: (0, 0)))
            out_shape.append(jax.ShapeDtypeStruct((1, n), F32))
        else:
            out_specs.append(pl.BlockSpec((TM, n), lambda i, k: (i, 0)))
            out_shape.append(jax.ShapeDtypeStruct((m, n), o))
    na, nr, nv = len(a_list), len(rows), len(vecs)

    def body(*refs):
        a_refs, b_ref = refs[:na], refs[na]
        row_refs = refs[na + 1:na + 1 + nr]
        vec_refs = refs[na + 1 + nr:na + 1 + nr + nv]
        out_refs = refs[len(refs) - 1 - len(outs):len(refs) - 1]
        acc_ref = refs[-1]
        i, k = pl.program_id(0), pl.program_id(1)

        @pl.when(k == 0)
        def _():
            acc_ref[...] = jnp.zeros_like(acc_ref)

        if pieces:
            off = 0
            for a_ref in a_refs:
                kp = a_ref.shape[1]
                acc_ref[...] += _dot(a_ref[...], b_ref[:, off:off + kp], dims)
                off += kp
        elif na == 1:
            acc_ref[...] += _dot(a_refs[0][...], b_ref[...], dims)
        else:
            nk1_ = ks[0] // tk

            @pl.when(k < nk1_)
            def _():
                acc_ref[...] += _dot(a_refs[0][...], b_ref[...], dims)

            @pl.when(k >= nk1_)
            def _():
                acc_ref[...] += _dot(a_refs[1][...], b_ref[...], dims)

        @pl.when(k == nk - 1)
        def _():
            vals = epilogue(acc_ref[...], [r[...] for r in row_refs], [v[...] for v in vec_refs])
            for o, ref, val in zip(outs, out_refs, vals):
                if o == "sum":
                    @pl.when(i == 0)
                    def _():
                        ref[...] = val

                    @pl.when(i > 0)
                    def _():
                        ref[...] += val
                else:
                    ref[...] = val.astype(o)

    return pl.pallas_call(
        body, name=name, grid=(m // TM, nk), in_specs=in_specs, out_specs=out_specs, out_shape=out_shape,
        scratch_shapes=[pltpu.VMEM((TM, n), F32)],
        compiler_params=_cp("arbitrary", "arbitrary"),
    )(*a_list, b, *rows, *vecs, *extra)


def _proj_in(x, g, w_in, name):
    s, d = x.shape
    n = w_in.shape[1]

    def body(x_ref, g_ref, w_ref, h_ref, p_ref, f_ref):
        xv = x_ref[...]
        h = (xv * _rstd(xv) * g_ref[...]).astype(BF16)
        h_ref[...] = h
        acc = _dot(h, w_ref[...], NN)
        p_ref[...] = acc.astype(BF16)
        f_ref[...] = acc[:, F_COL:]

    return pl.pallas_call(
        body, name=name, grid=(s // TM,),
        in_specs=[_row_spec(TM, d), _vec_spec(d), pl.BlockSpec((d, n), lambda i: (0, 0))],
        out_specs=[_row_spec(TM, d), _row_spec(TM, n), _row_spec(TM, n - F_COL)],
        out_shape=[jax.ShapeDtypeStruct((s, d), BF16), jax.ShapeDtypeStruct((s, n), BF16),
                   jax.ShapeDtypeStruct((s, n - F_COL), F32)],
        compiler_params=_cp("parallel"),
    )(x, g, w_in)


def _dw_in(h, pieces, name):
    s, d = h.shape
    n = sum(p.shape[1] for p in pieces)
    tk = 1024
    nk = s // tk

    def body(*refs):
        h_ref, piece_refs, o_ref, acc_ref = refs[0], refs[1:-2], refs[-2], refs[-1]
        k = pl.program_id(1)

        @pl.when(k == 0)
        def _():
            acc_ref[...] = jnp.zeros_like(acc_ref)

        off = 0
        for p_ref in piece_refs:
            w = p_ref.shape[1]
            acc_ref[:, off:off + w] += _dot(h_ref[...], p_ref[...], TN)
            off += w

        @pl.when(k == nk - 1)
        def _():
            o_ref[...] = acc_ref[...].astype(BF16)

    return pl.pallas_call(
        body, name=name, grid=(d // TM, nk),
        in_specs=[pl.BlockSpec((tk, TM), lambda i, k: (k, i))] +
                 [pl.BlockSpec((tk, p.shape[1]), lambda i, k: (k, 0)) for p in pieces],
        out_specs=pl.BlockSpec((TM, n), lambda i, k: (i, 0)),
        out_shape=jax.ShapeDtypeStruct((d, n), BF16),
        scratch_shapes=[pltpu.VMEM((TM, n), F32)],
        compiler_params=_cp("parallel", "arbitrary"),
    )(h, *pieces)


def _epi_resid(y, rows, vecs):
    (x_in,), (g_post, g_next) = rows, vecs
    xo = x_in + y * _rstd(y) * g_post
    return y, xo, xo * _rstd(xo) * g_next


def _epi_norm_bwd(dh, rows, vecs):
    x, dx_res = rows[0], rows[1]
    r = _rstd(x)
    xn = x * r
    dx = dx_res + _norm_bwd_rows(dh * vecs[0], xn, r)
    res = [dx, jnp.sum(dh * xn, axis=0, keepdims=True)]
    if len(rows) == 3:
        y = rows[2]
        r2 = _rstd(y)
        yn = y * r2
        res += [_norm_bwd_rows(dx * vecs[1], yn, r2), jnp.sum(dx * yn, axis=0, keepdims=True)]
    return res


def _norm_fwd(x, g, name):
    s, d = x.shape
    tm = min(TM, s)

    def body(x_ref, g_ref, h_ref):
        xv = x_ref[...]
        h_ref[...] = (xv * _rstd(xv) * g_ref[...]).astype(BF16)

    return pl.pallas_call(
        body, name=name, grid=(s // tm,), in_specs=[_row_spec(tm, d), _vec_spec(d)],
        out_specs=_row_spec(tm, d), out_shape=jax.ShapeDtypeStruct((s, d), BF16),
        compiler_params=_cp("parallel"),
    )(x, g)


def _norm_bwd(dh, x, dx_res, g_pre, name, prev=None):
    s, d = x.shape
    tm = min(TM, s)
    has_prev = prev is not None

    def body(*refs):
        if has_prev:
            dh_ref, x_ref, dr_ref, g_ref, y_ref, gp_ref, dx_ref, dg_ref, dy_ref, dgp_ref = refs
        else:
            dh_ref, x_ref, dr_ref, g_ref, dx_ref, dg_ref = refs
        i = pl.program_id(0)
        xv = x_ref[...]
        r = _rstd(xv)
        xn = xv * r
        dhv = dh_ref[...].astype(F32)
        dx = dr_ref[...] + _norm_bwd_rows(dhv * g_ref[...], xn, r)
        dx_ref[...] = dx
        dg = jnp.sum(dhv * xn, axis=0, keepdims=True)

        @pl.when(i == 0)
        def _():
            dg_ref[...] = dg

        @pl.when(i > 0)
        def _():
            dg_ref[...] += dg

        if has_prev:
            yv = y_ref[...]
            r2 = _rstd(yv)
            yn = yv * r2
            dy_ref[...] = _norm_bwd_rows(dx * gp_ref[...], yn, r2).astype(BF16)
            dgp = jnp.sum(dx * yn, axis=0, keepdims=True)

            @pl.when(i == 0)
            def _():
                dgp_ref[...] = dgp

            @pl.when(i > 0)
            def _():
                dgp_ref[...] += dgp

    in_specs = [_row_spec(tm, d), _row_spec(tm, d), _row_spec(tm, d), _vec_spec(d)]
    out_specs = [_row_spec(tm, d), _vec_spec(d)]
    out_shape = [jax.ShapeDtypeStruct((s, d), F32), jax.ShapeDtypeStruct((1, d), F32)]
    args = [dh, x, dx_res, g_pre]
    if has_prev:
        in_specs += [_row_spec(tm, d), _vec_spec(d)]
        out_specs += [_row_spec(tm, d), _vec_spec(d)]
        out_shape += [jax.ShapeDtypeStruct((s, d), BF16), jax.ShapeDtypeStruct((1, d), F32)]
        args += list(prev)
    return pl.pallas_call(
        body, name=name, grid=(s // tm,), in_specs=in_specs, out_specs=out_specs, out_shape=out_shape,
        compiler_params=_cp("arbitrary"),
    )(*args)


def _split3(v):
    hi = v.astype(BF16)
    r1 = v - hi.astype(F32)
    mid = r1.astype(BF16)
    lo = (r1 - mid.astype(F32)).astype(BF16)
    return hi, mid, lo


def _tri_dot(tri, v):
    hi, mid, lo = _split3(v)
    return _dot(tri, hi, NN) + _dot(tri, mid, NN) + _dot(tri, lo, NN)


def _gate_cumsum(fraw, b_pad, name):
    s = fraw.shape[0]
    width = HEAD_PAIRS * 128

    def body(f_ref, b_ref, flog_ref, aq_ref, ak_ref, carry_ref):
        i = pl.program_id(0)

        @pl.when(i == 0)
        def _():
            carry_ref[...] = jnp.zeros_like(carry_ref)

        flog = f_ref[...] + b_ref[...]
        flog_ref[...] = flog
        lf = jnp.minimum(flog, 0.0) - jnp.log(1.0 + jnp.exp(-jnp.abs(flog)))
        lane = lax.broadcasted_iota(jnp.int32, (1, 128), 1)
        lf = jnp.where(lane < FOX_HEADS, lf, 0.0)
        row = lax.broadcasted_iota(jnp.int32, (TM, TM), 0)
        col = lax.broadcasted_iota(jnp.int32, (TM, TM), 1)
        tri = (row >= col).astype(BF16)
        cum = _tri_dot(tri, lf) + carry_ref[...]
        carry_ref[...] = cum[TM - 1:TM, :]
        aq_ref[...], ak_ref[...] = _fox_operands(cum)

    return pl.pallas_call(
        body, name=name, grid=(s // TM,),
        in_specs=[_row_spec(TM, 128), _vec_spec(128)],
        out_specs=[_row_spec(TM, 128), _row_spec(TM, width), _row_spec(TM, width)],
        out_shape=[jax.ShapeDtypeStruct((s, 128), F32), jax.ShapeDtypeStruct((s, width), BF16),
                   jax.ShapeDtypeStruct((s, width), BF16)],
        scratch_shapes=[pltpu.VMEM((1, 128), F32)],
        compiler_params=_cp("arbitrary"),
    )(fraw, b_pad)


def _gate_bwd(qaux, kaux, flog, name):
    s = flog.shape[0]
    n = s // TM

    def body(qa_ref, ka_ref, fl_ref, dp_ref, db_ref, carry_ref):
        i = pl.program_id(0)

        @pl.when(i == 0)
        def _():
            carry_ref[...] = jnp.zeros_like(carry_ref)

        lane = lax.broadcasted_iota(jnp.int32, (1, 128), 1)
        moved = jnp.zeros((TM, 128), F32)
        for p in range(HEAD_PAIRS):
            d = qa_ref[p] - pltpu.roll(ka_ref[p], 128 - 3, 1)
            moved = moved + pltpu.roll(jnp.where((lane & 63) == 0, d, 0.0), 2 * p + 1, 1)
        odd = (lane & 1) == 1
        dcum = jnp.where(lane < FOX_HEADS, jnp.where(odd, moved, pltpu.roll(moved, 63, 1)), 0.0)
        row = lax.broadcasted_iota(jnp.int32, (TM, TM), 0)
        col = lax.broadcasted_iota(jnp.int32, (TM, TM), 1)
        tri = (row <= col).astype(BF16)
        dlf = _tri_dot(tri, dcum) + carry_ref[...]
        carry_ref[...] = dlf[0:1, :]
        df = jnp.where(lane < FOX_HEADS, dlf / (1.0 + jnp.exp(fl_ref[...])), 0.0)
        dp_ref[...] = df.astype(BF16)
        db = jnp.sum(df, axis=0, keepdims=True)

        @pl.when(i == 0)
        def _():
            db_ref[...] = db

        @pl.when(i > 0)
        def _():
            db_ref[...] += db

    rev = lambda i: (n - 1 - i, 0)
    return pl.pallas_call(
        body, name=name, grid=(n,),
        in_specs=[pl.BlockSpec((HEAD_PAIRS, TM, 128), lambda i: (0, n - 1 - i, 0)),
                  pl.BlockSpec((HEAD_PAIRS, TM, 128), lambda i: (0, n - 1 - i, 0)), pl.BlockSpec((TM, 128), rev)],
        out_specs=[pl.BlockSpec((TM, 128), rev), _vec_spec(128)],
        out_shape=[jax.ShapeDtypeStruct((s, 128), BF16), jax.ShapeDtypeStruct((1, 128), F32)],
        scratch_shapes=[pltpu.VMEM((1, 128), F32)],
        compiler_params=_cp("arbitrary"),
    )(qaux, kaux, flog)


def _pool_consts(i, rows):
    lane = lax.broadcasted_iota(jnp.int32, (rows, D_POOL), 1)
    t1 = lax.broadcasted_iota(jnp.int32, (rows, D_POOL), 0) + i * TM + 1
    win = jnp.where(lane < 64, 2, jnp.where(lane < 128, 4, jnp.where(lane < 192, 8, 16)))
    inv = 1.0 / jnp.minimum(t1, win).astype(F32)
    return lane, inv


def _by_group(lane, s2, s4, s8, s16):
    return jnp.where(lane < 64, s2, jnp.where(lane < 128, s4, jnp.where(lane < 192, s8, s16)))


def _pool_diff(i, u_ref, halo_ref):
    u = u_ref[...].astype(F32)
    halo = jnp.where(i > 0, halo_ref[...].astype(F32), 0.0)
    ext = jnp.concatenate([halo, u], axis=0)
    s2 = ext + pltpu.roll(ext, 1, 0)
    s4 = s2 + pltpu.roll(s2, 2, 0)
    s8 = s4 + pltpu.roll(s4, 4, 0)
    s16 = s8 + pltpu.roll(s8, 8, 0)
    lane, inv = _pool_consts(i, TM)
    sel = _by_group(lane, s2[POOL_HALO:], s4[POOL_HALO:], s8[POOL_HALO:], s16[POOL_HALO:])
    return sel * inv - u


def _pool_fwd(proj, wbd, scale, ycat, name):
    s = proj.shape[0]
    hb = TM // POOL_HALO

    def body(u_ref, halo_ref, w_ref, sc_ref, y_any, y_ref):
        del y_any
        i = pl.program_id(0)
        diff = _pool_diff(i, u_ref, halo_ref)
        mixed = _dot(diff.astype(BF16), w_ref[...], NN)
        y_ref[...] = (mixed * sc_ref[...]).astype(BF16)

    return pl.pallas_call(
        body, name=name, grid=(s // TM,),
        in_specs=[pl.BlockSpec((TM, D_POOL), lambda i: (i, 0)),
                  pl.BlockSpec((POOL_HALO, D_POOL), lambda i: (jnp.maximum(i * hb - 1, 0), 0)),
                  pl.BlockSpec((D_POOL, D_POOL), lambda i: (0, 0)), _vec_spec(D_POOL),
                  pl.BlockSpec(memory_space=pl.ANY)],
        out_specs=pl.BlockSpec((TM, D_POOL), lambda i: (i, 0)),
        out_shape=jax.ShapeDtypeStruct(ycat.shape, ycat.dtype),
        input_output_aliases={4: 0},
        compiler_params=_cp("parallel"),
    )(proj, proj, wbd, scale, ycat)


def _pool_bwd(proj, dycat, wbd, scale, name):
    s = proj.shape[0]
    n = s // TM
    hb = TM // POOL_HALO
    last_halo = s // POOL_HALO - 1

    def body(u_ref, halo_ref, dy_ref, dyp_ref, w_ref, sc_ref, dp_ref, dw_ref, dsc_ref):
        i = pl.program_id(0)
        diff = _pool_diff(i, u_ref, halo_ref)
        diff_b = diff.astype(BF16)
        mixed = _dot(diff_b, w_ref[...], NN)
        dy = dy_ref[...].astype(F32)
        dmix = (dy * sc_ref[...]).astype(BF16)
        dyp = jnp.where(i < n - 1, dyp_ref[...].astype(F32), 0.0)
        dmix_p = (dyp * sc_ref[...]).astype(BF16)
        dd = _dot(dmix, w_ref[...], NT)
        dd_p = _dot(dmix_p, w_ref[...], NT)
        lane, inv = _pool_consts(i, TM)
        _, inv_p = _pool_consts(i + 1, POOL_HALO)
        ext = jnp.concatenate([dd * inv, dd_p * inv_p], axis=0)
        rows = TM + POOL_HALO
        l2 = ext + pltpu.roll(ext, rows - 1, 0)
        l4 = l2 + pltpu.roll(l2, rows - 2, 0)
        l8 = l4 + pltpu.roll(l4, rows - 4, 0)
        l16 = l8 + pltpu.roll(l8, rows - 8, 0)
        du = _by_group(lane, l2[:TM], l4[:TM], l8[:TM], l16[:TM]) - dd
        dp_ref[...] = du.astype(BF16)
        dw = _dot(diff_b, dmix, TN)
        dsc = jnp.sum(dy * mixed, axis=0, keepdims=True)

        @pl.when(i == 0)
        def _():
            dw_ref[...] = dw
            dsc_ref[...] = dsc

        @pl.when(i > 0)
        def _():
            dw_ref[...] += dw
            dsc_ref[...] += dsc

    return pl.pallas_call(
        body, name=name, grid=(n,),
        in_specs=[pl.BlockSpec((TM, D_POOL), lambda i: (i, 0)),
                  pl.BlockSpec((POOL_HALO, D_POOL), lambda i: (jnp.maximum(i * hb - 1, 0), 0)),
                  pl.BlockSpec((TM, D_POOL), lambda i: (i, 0)),
                  pl.BlockSpec((POOL_HALO, D_POOL), lambda i: (jnp.minimum((i + 1) * hb, last_halo), 0)),
                  pl.BlockSpec((D_POOL, D_POOL), lambda i: (0, 0)), _vec_spec(D_POOL)],
        out_specs=[pl.BlockSpec((TM, D_POOL), lambda i: (i, 0)),
                   pl.BlockSpec((D_POOL, D_POOL), lambda i: (0, 0)), _vec_spec(D_POOL)],
        out_shape=[jax.ShapeDtypeStruct((s, D_POOL), BF16),
                   jax.ShapeDtypeStruct((D_POOL, D_POOL), F32), jax.ShapeDtypeStruct((1, D_POOL), F32)],
        compiler_params=_cp("arbitrary"),
    )(proj, proj, dycat, dycat, wbd, scale)


Q_BLK = D_POOL // 128
K_BLK = Q_BLK + D_FOX // 128
V_BLK = K_BLK + D_FOX // 128
STEP_PAIRS = 2
assert Q_BLK % STEP_PAIRS == 0 and K_BLK % STEP_PAIRS == 0 and V_BLK % STEP_PAIRS == 0 and HEAD_PAIRS % STEP_PAIRS == 0


def _operand_rows(v0, v1, ones_off):
    row = lax.broadcasted_iota(jnp.int32, (128, 1), 0)
    half = row & 63
    out = jnp.where(jnp.logical_and(half >= ones_off, half < ones_off + 3), 1.0, 0.0) + jnp.zeros_like(v0)
    for base, v in ((64, v0), (0, v1)):
        for j, piece in enumerate(_split3(v)):
            out = jnp.where(row == base + j, piece.astype(F32), out)
    return out


def _fox_operands(cum):
    width = HEAD_PAIRS * 128
    hi, mid, lo = _split3(cum)
    packed = (hi.astype(F32) + pltpu.roll(mid.astype(F32), 16, 1) + pltpu.roll(lo.astype(F32), 32, 1)).astype(BF16)
    row = lax.broadcasted_iota(jnp.int32, (128, width), 0)
    col = lax.broadcasted_iota(jnp.int32, (128, width), 1)
    head, j = row & 15, row >> 4
    base = (head >> 1) * 128 + (1 - (head & 1)) * 64
    used = jnp.logical_and(head < FOX_HEADS, j < 3)
    half = lax.broadcasted_iota(jnp.int32, (1, width), 1) & 63
    res = []
    for off, sign, ones_off in ((0, 1.0, 3), (3, -1.0, 0)):
        ones = jnp.where(jnp.logical_and(half >= ones_off, half < ones_off + 3), 1.0, 0.0)
        sel = jnp.where(jnp.logical_and(col == base + off + j, used), sign, 0.0).astype(BF16)
        res.append((ones + _dot(packed, sel, NN)).astype(BF16))
    return res


def _fox_do_operand(dycat, ycat, after, name):
    s = dycat.shape[0]

    rb = 1024
    nblk = D_FOX // D_POOL

    def body(*refs):
        do_refs, o_refs, ad_ref = refs[:nblk], refs[nblk:2 * nblk], refs[-1]
        src = lax.broadcasted_iota(jnp.int32, (D_POOL, D_POOL), 0)
        dst = lax.broadcasted_iota(jnp.int32, (D_POOL, D_POOL), 1)
        same_pair = (src >> 7) == (dst >> 7)
        s_in, d_in = src & 127, dst & 127
        hit = jnp.logical_and(same_pair, jnp.logical_or(
            jnp.logical_and(s_in < 64, jnp.logical_and(d_in >= 64, d_in < 67)), jnp.logical_and(s_in >= 64, d_in < 3)))
        sel = jnp.where(hit, 1.0, 0.0).astype(BF16)
        j = lax.broadcasted_iota(jnp.int32, (1, D_POOL), 1) & 63
        for b in range(nblk):
            dd = do_refs[b][...].astype(F32) * o_refs[b][...].astype(F32)
            dsum = jnp.zeros(dd.shape, F32)
            for piece in _split3(dd):
                dsum = dsum + _dot(piece, sel, NN)
            hi, mid, lo = _split3(-dsum)
            ad_ref[:, D_POOL * b:D_POOL * (b + 1)] = jnp.where(j == 0, hi, jnp.where(j == 1, mid, lo))

    blks = [pl.BlockSpec((rb, D_POOL), functools.partial(lambda i, b: (i, 1 + b), b=b)) for b in range(nblk)]
    return pl.pallas_call(
        body, name=name, grid=(s // rb,), in_specs=blks + blks + [pl.BlockSpec(memory_space=pl.ANY)],
        out_specs=pl.BlockSpec((rb, D_FOX), lambda i: (i, 0)),
        out_shape=jax.ShapeDtypeStruct((s, D_FOX), BF16),
        compiler_params=_cp("parallel"),
    )(*[dycat] * nblk, *[ycat] * nblk, after)


def _causal_pairs(nq, key_major):
    if key_major:
        pairs = [(q, k) for k in range(nq) for q in range(k, nq)]
    else:
        pairs = [(q, k) for q in range(nq) for k in range(q + 1)]
    return (jnp.asarray([p[0] for p in pairs], jnp.int32), jnp.asarray([p[1] for p in pairs], jnp.int32))


def _fox_fwd(proj, aq, ak, name):
    s = proj.shape[0]
    nq = s // TQ
    qi_arr, ki_arr = _causal_pairs(nq, key_major=False)

    wide = STEP_PAIRS * 128
    heads = [(pp, hh) for pp in range(STEP_PAIRS) for hh in range(2)]

    def body(qi_ref, ki_ref, q_ref, k_ref, v_ref, aq_ref, ak_ref, o_ref, aqb_ref, m_ref, acc_ref, aux_ref):
        t = pl.program_id(1)
        qi, ki = qi_ref[t], ki_ref[t]
        lane = lax.broadcasted_iota(jnp.int32, (1, 128), 1)
        masks = [lane < 64, lane >= 64]
        ones_v = jnp.where((lane & 63) == 8, 1.0, 0.0).astype(BF16)
        top = lax.broadcasted_iota(jnp.int32, (128, 1), 0) < 64

        @pl.when(ki == 0)
        def _():
            m_ref[...] = jnp.full_like(m_ref, NEG)
            acc_ref[...] = jnp.zeros_like(acc_ref)
            aux_ref[...] = jnp.zeros_like(aux_ref)

        def step(diag):
            q2s = q_ref[...] * 0.125
            k2, v2, aq2, ak2 = k_ref[...], v_ref[...], aq_ref[...], ak_ref[...]

            def operand(main, lanes, pp, hh):
                cols = slice(128 * pp, 128 * (pp + 1))
                return jnp.where(masks[hh], main[:, cols], lanes[:, cols])

            scs = [_dot(operand(k2, ak2, pp, hh), operand(q2s, aq2, pp, hh), NT) for pp, hh in heads]
            ps, alpha = [], []
            for n, sc in enumerate(scs):
                if diag:
                    key = lax.broadcasted_iota(jnp.int32, sc.shape, 0)
                    qry = lax.broadcasted_iota(jnp.int32, sc.shape, 1)
                    sc = jnp.where(qry >= key, sc, NEG)
                m_prev = m_ref[n]
                m_new = jnp.maximum(m_prev, jnp.max(sc, axis=0, keepdims=True))
                m_ref[n] = m_new
                alpha.append(jnp.exp(m_prev - m_new))
                ps.append(jnp.exp(sc - m_new).astype(BF16))
            ones2 = jnp.concatenate([ones_v] * STEP_PAIRS, axis=1)
            pv = [_dot(operand(v2, ones2, pp, hh), ps[n], TN) for n, (pp, hh) in enumerate(heads)]
            for pp in range(STEP_PAIRS):
                a0, a1, pv0, pv1 = alpha[2 * pp], alpha[2 * pp + 1], pv[2 * pp], pv[2 * pp + 1]
                acc_ref[pp] = acc_ref[pp] * jnp.where(top, a0, a1) + jnp.where(top, pv0, pv1)
                aux_ref[pp] = aux_ref[pp] * jnp.where(top, a1, a0) + jnp.where(top, pv1, pv0)

        @pl.when(ki < qi)
        def _():
            step(False)

        @pl.when(ki == qi)
        def _():
            step(True)
            for pp in range(STEP_PAIRS):
                cols = slice(128 * pp, 128 * (pp + 1))
                aux = aux_ref[pp]
                l0, l1 = aux[72:73, :], aux[8:9, :]
                o_ref[:, cols] = (acc_ref[pp] * jnp.where(top, 1.0 / l0, 1.0 / l1)).T.astype(BF16)
                aqt = aq_ref[:, cols].astype(F32).T
                cum0 = aqt[64:65, :] + aqt[65:66, :] + aqt[66:67, :]
                cum1 = aqt[0:1, :] + aqt[1:2, :] + aqt[2:3, :]
                aqb = _operand_rows(cum0 - (m_ref[2 * pp] + jnp.log(l0)), cum1 - (m_ref[2 * pp + 1] + jnp.log(l1)), 3)
                aqb_ref[:, cols] = aqb.T.astype(BF16)

    grid_spec = pltpu.PrefetchScalarGridSpec(
        num_scalar_prefetch=2, grid=(HEAD_PAIRS // STEP_PAIRS, int(qi_arr.shape[0])),
        in_specs=[pl.BlockSpec((TQ, wide), lambda p, t, qi, ki: (qi[t], Q_BLK // STEP_PAIRS + p)),
                  pl.BlockSpec((TQ, wide), lambda p, t, qi, ki: (ki[t], K_BLK // STEP_PAIRS + p)),
                  pl.BlockSpec((TQ, wide), lambda p, t, qi, ki: (ki[t], V_BLK // STEP_PAIRS + p)),
                  pl.BlockSpec((TQ, wide), lambda p, t, qi, ki: (qi[t], p)),
                  pl.BlockSpec((TQ, wide), lambda p, t, qi, ki: (ki[t], p))],
        out_specs=[pl.BlockSpec((TQ, wide), lambda p, t, qi, ki: (qi[t], Q_BLK // STEP_PAIRS + p)),
                   pl.BlockSpec((TQ, wide), lambda p, t, qi, ki: (qi[t], p))],
        scratch_shapes=[pltpu.VMEM((2 * STEP_PAIRS, 1, TQ), F32),
                        pltpu.VMEM((STEP_PAIRS, 128, TQ), F32), pltpu.VMEM((STEP_PAIRS, 128, TQ), F32)])
    return pl.pallas_call(
        body, name=name, grid_spec=grid_spec,
        out_shape=[jax.ShapeDtypeStruct((s, D_MODEL), BF16), jax.ShapeDtypeStruct((s, HEAD_PAIRS * 128), BF16)],
        compiler_params=_cp("parallel", "arbitrary"),
    )(qi_arr, ki_arr, proj, proj, proj, aq, ak)


def _fox_bwd(proj, dycat, aqb, ak, ad, name):
    s = proj.shape[0]
    nq = s // TQ
    qi_arr, ki_arr = _causal_pairs(nq, key_major=True)

    def body(qi_ref, ki_ref, q_ref, k_ref, v_ref, do_ref, aq_ref, ak_ref, ad_ref,
             dq_ref, dk_ref, dv_ref, qaux_ref, kaux_ref, dq_acc, qaux_acc, dk_acc, dv_acc, kaux_acc):
        t = pl.program_id(1)
        qi, ki = qi_ref[t], ki_ref[t]
        lane = lax.broadcasted_iota(jnp.int32, (1, 128), 1)
        masks = [lane < 64, lane >= 64]
        ones_v = jnp.where((lane & 63) < 3, 1.0, 0.0).astype(BF16)
        top = lax.broadcasted_iota(jnp.int32, (128, 1), 0) < 64

        @pl.when(qi == ki)
        def _():
            dk_acc[...] = jnp.zeros_like(dk_acc)
            dv_acc[...] = jnp.zeros_like(dv_acc)
            kaux_acc[...] = jnp.zeros_like(kaux_acc)

        def step(diag):
            q2s = q_ref[...] * 0.125
            k2, v2, do2 = k_ref[...], v_ref[...], do_ref[...]
            aq2, ak2, ad2 = aq_ref[...], ak_ref[...], ad_ref[...]
            ones2 = jnp.concatenate([ones_v] * STEP_PAIRS, axis=1)
            dq_new, qaux_new = [], []
            for pp in range(STEP_PAIRS):
                cols = slice(128 * pp, 128 * (pp + 1))
                dq, dk, dv = [], [], []
                for hh in range(2):
                    qh = jnp.where(masks[hh], q2s[:, cols], aq2[:, cols])
                    kh = jnp.where(masks[hh], k2[:, cols], ak2[:, cols])
                    doh = jnp.where(masks[hh], do2[:, cols], ad2[:, cols])
                    vh = jnp.where(masks[hh], v2[:, cols], ones2[:, cols])
                    sc = _dot(kh, qh, NT)
                    if diag:
                        key = lax.broadcasted_iota(jnp.int32, sc.shape, 0)
                        qry = lax.broadcasted_iota(jnp.int32, sc.shape, 1)
                        sc = jnp.where(qry >= key, sc, NEG)
                    p = jnp.exp(sc)
                    dsb = (p * _dot(vh, doh, NT)).astype(BF16)
                    dv.append(_dot(p.astype(BF16), doh, NN))
                    dk.append(_dot(dsb, qh, NN))
                    dq.append(_dot(kh, dsb, TN))
                dk_acc[pp] += jnp.where(masks[0], dk[0], dk[1])
                kaux_acc[pp] += jnp.where(masks[0], dk[1], dk[0])
                dv_acc[pp] += jnp.where(masks[0], dv[0], dv[1])
                dq_new.append(jnp.where(top, dq[0], dq[1]))
                qaux_new.append(jnp.where(top, dq[1], dq[0]))

            @pl.when(ki == 0)
            def _():
                for pp in range(STEP_PAIRS):
                    dq_acc[pp * nq + qi] = dq_new[pp]
                    qaux_acc[pp * nq + qi] = qaux_new[pp]

            @pl.when(ki > 0)
            def _():
                for pp in range(STEP_PAIRS):
                    dq_acc[pp * nq + qi] += dq_new[pp]
                    qaux_acc[pp * nq + qi] += qaux_new[pp]

        @pl.when(qi > ki)
        def _():
            step(False)

        @pl.when(qi == ki)
        def _():
            step(True)
            rows = pl.ds(pl.multiple_of(qi * TQ, TQ), TQ)
            for pp in range(STEP_PAIRS):
                dq_ref[rows, 128 * pp:128 * (pp + 1)] = (dq_acc[pp * nq + qi] * 0.125).T.astype(BF16)
                qaux_ref[pp, rows, :] = qaux_acc[pp * nq + qi].T

        @pl.when(qi == nq - 1)
        def _():
            for pp in range(STEP_PAIRS):
                dk_ref[:, 128 * pp:128 * (pp + 1)] = dk_acc[pp].astype(BF16)
                dv_ref[:, 128 * pp:128 * (pp + 1)] = dv_acc[pp].astype(BF16)
            kaux_ref[...] = kaux_acc[...]

    wide = STEP_PAIRS * 128
    grid_spec = pltpu.PrefetchScalarGridSpec(
        num_scalar_prefetch=2, grid=(HEAD_PAIRS // STEP_PAIRS, int(qi_arr.shape[0])),
        in_specs=[pl.BlockSpec((TQ, wide), lambda p, t, qi, ki: (qi[t], Q_BLK // STEP_PAIRS + p)),
                  pl.BlockSpec((TQ, wide), lambda p, t, qi, ki: (ki[t], K_BLK // STEP_PAIRS + p)),
                  pl.BlockSpec((TQ, wide), lambda p, t, qi, ki: (ki[t], V_BLK // STEP_PAIRS + p)),
                  pl.BlockSpec((TQ, wide), lambda p, t, qi, ki: (qi[t], Q_BLK // STEP_PAIRS + p)),
                  pl.BlockSpec((TQ, wide), lambda p, t, qi, ki: (qi[t], p)),
                  pl.BlockSpec((TQ, wide), lambda p, t, qi, ki: (ki[t], p)),
                  pl.BlockSpec((TQ, wide), lambda p, t, qi, ki: (qi[t], p))],
        out_specs=[pl.BlockSpec((s, wide), lambda p, t, qi, ki: (0, p)),
                   pl.BlockSpec((TQ, wide), lambda p, t, qi, ki: (ki[t], p)),
                   pl.BlockSpec((TQ, wide), lambda p, t, qi, ki: (ki[t], p)),
                   pl.BlockSpec((STEP_PAIRS, s, 128), lambda p, t, qi, ki: (p, 0, 0)),
                   pl.BlockSpec((STEP_PAIRS, TQ, 128), lambda p, t, qi, ki: (p, ki[t], 0))],
        scratch_shapes=[pltpu.VMEM((STEP_PAIRS * nq, 128, TQ), F32), pltpu.VMEM((STEP_PAIRS * nq, 128, TQ), F32),
                        pltpu.VMEM((STEP_PAIRS, TQ, 128), F32), pltpu.VMEM((STEP_PAIRS, TQ, 128), F32),
                        pltpu.VMEM((STEP_PAIRS, TQ, 128), F32)])
    return pl.pallas_call(
        body, name=name, grid_spec=grid_spec,
        out_shape=[jax.ShapeDtypeStruct((s, D_FOX), BF16)] * 3 + [jax.ShapeDtypeStruct((HEAD_PAIRS, s, 128), F32)] * 2,
        compiler_params=_cp("arbitrary", "arbitrary"),
    )(qi_arr, ki_arr, proj, proj, proj, dycat, aqb, ak, ad)


XA_SCALE = XA_DIM ** -0.5


def _xattn_fwd(q2, kv, name):
    s = q2.shape[0]
    m = kv.shape[0]

    def body(q_ref, kv_ref, o_ref):
        heads = [slice(h * XA_DIM, (h + 1) * XA_DIM) for h in range(XA_HEADS)]
        scs = [_dot(q_ref[:, cols], kv_ref[:, cols], NT) for cols in heads]
        for h in range(XA_HEADS):
            c0 = h * XA_DIM
            sc = scs[h] * XA_SCALE
            e = jnp.exp(sc - jnp.max(sc, axis=1, keepdims=True))
            p = e / jnp.sum(e, axis=1, keepdims=True)
            o_ref[:, c0:c0 + XA_DIM] = _dot(p.astype(BF16), kv_ref[:, D_MODEL + c0:D_MODEL + c0 + XA_DIM], NN).astype(BF16)

    return pl.pallas_call(
        body, name=name, grid=(s // TM,),
        in_specs=[_row_spec(TM, D_MODEL), pl.BlockSpec((m, 2 * D_MODEL), lambda i: (0, 0))],
        out_specs=_row_spec(TM, D_MODEL), out_shape=jax.ShapeDtypeStruct((s, D_MODEL), BF16),
        compiler_params=_cp("parallel"),
    )(q2, kv)


def _xattn_bwd(q2, kv, do2, name):
    s = q2.shape[0]
    m = kv.shape[0]

    def body(q_ref, kv_ref, do_ref, dq_ref, dkv_ref):
        i = pl.program_id(0)

        @pl.when(i == 0)
        def _():
            dkv_ref[...] = jnp.zeros_like(dkv_ref)

        heads = [slice(h * XA_DIM, (h + 1) * XA_DIM) for h in range(XA_HEADS)]
        scs = [_dot(kv_ref[:, cols], q_ref[:, cols], NT) for cols in heads]
        dps = [_dot(kv_ref[:, D_MODEL + cols.start:D_MODEL + cols.stop], do_ref[:, cols], NT) for cols in heads]
        for h in range(XA_HEADS):
            c0 = h * XA_DIM
            v0 = D_MODEL + c0
            qh = q_ref[:, c0:c0 + XA_DIM]
            kh = kv_ref[:, c0:c0 + XA_DIM]
            doh = do_ref[:, c0:c0 + XA_DIM]
            sc = scs[h] * XA_SCALE
            e = jnp.exp(sc - jnp.max(sc, axis=0, keepdims=True))
            p = e / jnp.sum(e, axis=0, keepdims=True)
            dp = dps[h]
            ds = p * (dp - jnp.sum(p * dp, axis=0, keepdims=True))
            dsb = (ds * XA_SCALE).astype(BF16)
            dq_ref[:, c0:c0 + XA_DIM] = _dot(kh, dsb, TN).T.astype(BF16)
            dkv_ref[:, c0:c0 + XA_DIM] += _dot(dsb, qh, NN)
            dkv_ref[:, v0:v0 + XA_DIM] += _dot(p.astype(BF16), doh, NN)

    return pl.pallas_call(
        body, name=name, grid=(s // TM,),
        in_specs=[_row_spec(TM, D_MODEL), pl.BlockSpec((m, 2 * D_MODEL), lambda i: (0, 0)), _row_spec(TM, D_MODEL)],
        out_specs=[_row_spec(TM, D_MODEL), pl.BlockSpec((m, 2 * D_MODEL), lambda i: (0, 0))],
        out_shape=[jax.ShapeDtypeStruct((s, D_MODEL), BF16), jax.ShapeDtypeStruct((m, 2 * D_MODEL), F32)],
        compiler_params=_cp("arbitrary"),
    )(q2, kv, do2)


GELU_C = math.sqrt(2.0 / math.pi)
GELU_A = 0.044715


def _gelu(x):
    return (0.5 * x) * (1.0 + jnp.tanh(x * (GELU_C * GELU_A * (x * x) + GELU_C)))


def _gelu_and_grad(x):
    x2 = x * x
    s = 1.0 + jnp.tanh(x * (GELU_C * GELU_A * x2 + GELU_C))
    hx = 0.5 * x
    return hx * s, s * (0.5 + hx * (2.0 - s) * (3.0 * GELU_C * GELU_A * x2 + GELU_C))


def _conv(h, s1, s2, w_ref, b_ref):
    return w_ref[0:1, :] * s2 + w_ref[1:2, :] * s1 + w_ref[2:3, :] * h + b_ref[...]


def _shift_down(main, prev8):
    row = lax.broadcasted_iota(jnp.int32, main.shape, 0)
    s1 = jnp.where(row == 0, prev8[7:8, :], pltpu.roll(main, 1, 0))
    s2 = jnp.where(row == 0, prev8[6:7, :], jnp.where(row == 1, prev8[7:8, :], pltpu.roll(main, 2, 0)))
    return s1, s2


def _shift_up(main, next8):
    n = main.shape[0]
    row = lax.broadcasted_iota(jnp.int32, main.shape, 0)
    u1 = jnp.where(row == n - 1, next8[0:1, :], pltpu.roll(main, n - 1, 0))
    u2 = jnp.where(row == n - 2, next8[0:1, :], jnp.where(row == n - 1, next8[1:2, :], pltpu.roll(main, n - 2, 0)))
    return u1, u2


def _ffn_fwd(h3, w_up, cw, cb, w_down, x2, tgt, g_post, name):
    s = h3.shape[0]
    tn = TN_FF
    nj = D_FF // tn
    per = D_MODEL // tn
    hb = TM // 8

    def body(h_ref, halo_ref, wg_ref, wu_ref, cwg_ref, cwu_ref, cbg_ref, cbu_ref, wd_ref, x_ref, t_ref, g_ref,
             hg_ref, hu_ref, cg_ref, cu_ref, a_ref, loss_ref, dx_ref, dy_ref, dg_ref, y_acc):
        i, j = pl.program_id(0), pl.program_id(1)

        @pl.when(j == 0)
        def _():
            y_acc[...] = jnp.zeros_like(y_acc)

        h = h_ref[...]
        halo = halo_ref[...]
        halo = jnp.where(i > 0, halo, jnp.zeros_like(halo))
        hid = [(_dot(h, w_ref[...], NN), _dot(halo, w_ref[...], NN)) for w_ref in (wg_ref, wu_ref)]
        conv = []
        for (hm, hm_halo), cw_ref, cb_ref, hid_ref, c_ref in zip(hid, (cwg_ref, cwu_ref), (cbg_ref, cbu_ref),
                                                                  (hg_ref, hu_ref), (cg_ref, cu_ref)):
            hid_ref[...] = hm.astype(BF16)
            s1, s2 = _shift_down(hm, hm_halo)
            c = _conv(hm, s1, s2, cw_ref, cb_ref)
            c_ref[...] = c.astype(BF16)
            conv.append(c)
        a = (_gelu(conv[0]) * conv[1]).astype(BF16)
        a_ref[...] = a
        y_acc[...] += _dot(a, wd_ref[...], NN)

        @pl.when(j == nj - 1)
        def _():
            yv = y_acc[...]
            r = _rstd(yv)
            yn = yv * r
            e = x_ref[...] + yn * g_ref[...] - t_ref[...]
            part = 0.5 * jnp.sum(jnp.mean(e * e, axis=-1, keepdims=True), axis=0, keepdims=True)
            part = jnp.broadcast_to(part, (1, 128))
            dx = e * (1.0 / D_MODEL)
            dx_ref[...] = dx
            dy_ref[...] = _norm_bwd_rows(dx * g_ref[...], yn, r).astype(BF16)
            dg = jnp.sum(dx * yn, axis=0, keepdims=True)

            @pl.when(i == 0)
            def _():
                dg_ref[...] = dg
                loss_ref[...] = part

            @pl.when(i > 0)
            def _():
                dg_ref[...] += dg
                loss_ref[...] += part

    rows = pl.BlockSpec((TM, D_MODEL), lambda i, j: (i, 0))
    tile = pl.BlockSpec((TM, tn), lambda i, j: (i, j))
    wide = jax.ShapeDtypeStruct((s, D_FF), BF16)
    return pl.pallas_call(
        body, name=name, grid=(s // TM, nj),
        in_specs=[rows,
                  pl.BlockSpec((8, D_MODEL), lambda i, j: (jnp.maximum(i * hb - 1, 0), 0)),
                  pl.BlockSpec((None, D_MODEL, tn), lambda i, j: (j // per, 0, j % per)),
                  pl.BlockSpec((None, D_MODEL, tn), lambda i, j: (NDEV // 2 + j // per, 0, j % per)),
                  pl.BlockSpec((8, tn), lambda i, j: (0, j)),
                  pl.BlockSpec((8, tn), lambda i, j: (0, nj + j)),
                  pl.BlockSpec((1, tn), lambda i, j: (0, j)),
                  pl.BlockSpec((1, tn), lambda i, j: (0, nj + j)),
                  pl.BlockSpec((tn, D_MODEL), lambda i, j: (j, 0)),
                  rows, rows, pl.BlockSpec((1, D_MODEL), lambda i, j: (0, 0))],
        out_specs=[tile, tile, tile, tile, tile,
                   pl.BlockSpec((1, 128), lambda i, j: (0, 0)), rows, rows,
                   pl.BlockSpec((1, D_MODEL), lambda i, j: (0, 0))],
        out_shape=[wide, wide, wide, wide, wide,
                   jax.ShapeDtypeStruct((1, 128), F32), jax.ShapeDtypeStruct((s, D_MODEL), F32),
                   jax.ShapeDtypeStruct((s, D_MODEL), BF16), jax.ShapeDtypeStruct((1, D_MODEL), F32)],
        scratch_shapes=[pltpu.VMEM((TM, D_MODEL), F32)],
        compiler_params=_cp("arbitrary", "arbitrary"),
    )(h3, h3, w_up, w_up, cw, cw, cb, cb, w_down, x2, tgt, g_post)


def _ffn_bwd(dy3, w_down, hid_g, hid_u, conv_g, conv_u, cw, name):
    s = dy3.shape[0]
    n = s // TM
    tn = TN_FF
    nj = D_FF // tn
    hb = TM // 8
    last8 = s // 8 - 1

    def body(dy_ref, dyn_ref, wd_ref, hg_ref, hu_ref, cg_ref, cgn_ref, cu_ref, cun_ref, cwg_ref, cwu_ref,
             dhg_ref, dhu_ref, dcwg_ref, dcwu_ref, dcbg_ref, dcbu_ref):
        i = pl.program_id(1)
        first, last = i == 0, i == n - 1
        @pl.when(first)
        def _():
            for ref in (dcwg_ref, dcwu_ref, dcbg_ref, dcbu_ref):
                ref[...] = jnp.zeros_like(ref)

        dyn = dyn_ref[...]
        dyn = jnp.where(last, jnp.zeros_like(dyn), dyn)
        wc = tn // 4
        chunks = [slice(c, c + wc) for c in range(0, tn, wc)]
        das = [(_dot(dy_ref[...], wd_ref[cols, :], NT), _dot(dyn, wd_ref[cols, :], NT)) for cols in chunks]
        row8 = lax.broadcasted_iota(jnp.int32, (8, wc), 0)
        for cols, (da, da_n) in zip(chunks, das):
            c_g, c_u = cg_ref[:, cols].astype(F32), cu_ref[:, cols].astype(F32)
            g, dg = _gelu_and_grad(c_g)
            gn, dgn = _gelu_and_grad(cgn_ref[:, cols].astype(F32))
            outs = ((da * c_u * dg, da_n * cun_ref[:, cols].astype(F32) * dgn, hg_ref, cwg_ref, dhg_ref, dcwg_ref, dcbg_ref),
                    (da * g, da_n * gn, hu_ref, cwu_ref, dhu_ref, dcwu_ref, dcbu_ref))
            for dc, dcn, h_ref, cw_ref, dh_ref, dcw_ref, dcb_ref in outs:
                u1, u2 = _shift_up(dc, dcn)
                dh_ref[:, cols] = (cw_ref[2:3, cols] * dc + cw_ref[1:2, cols] * u1 + cw_ref[0:1, cols] * u2).astype(BF16)
                hm = h_ref[:, cols].astype(F32)
                dcb = jnp.sum(dc, axis=0, keepdims=True)
                dcw = jnp.where(row8 == 0, jnp.sum(hm * u2, axis=0, keepdims=True),
                                jnp.where(row8 == 1, jnp.sum(hm * u1, axis=0, keepdims=True),
                                          jnp.where(row8 == 2, jnp.sum(hm * dc, axis=0, keepdims=True), 0.0)))
                dcw_ref[:, cols] += dcw
                dcb_ref[:, cols] += dcb

    next8 = lambda j, i: (jnp.minimum((i + 1) * hb, last8), j)
    blk = lambda j, i: (i, j)
    col = lambda j, i: (0, j)
    colu = lambda j, i: (0, nj + j)
    tile = pl.BlockSpec((TM, tn), blk)
    return pl.pallas_call(
        body, name=name, grid=(nj, n),
        in_specs=[pl.BlockSpec((TM, D_MODEL), lambda j, i: (i, 0)),
                  pl.BlockSpec((8, D_MODEL), lambda j, i: (jnp.minimum((i + 1) * hb, last8), 0)),
                  pl.BlockSpec((tn, D_MODEL), lambda j, i: (j, 0)),
                  tile, tile, tile, pl.BlockSpec((8, tn), next8), tile, pl.BlockSpec((8, tn), next8),
                  pl.BlockSpec((8, tn), col), pl.BlockSpec((8, tn), colu)],
        out_specs=[tile, tile, pl.BlockSpec((8, tn), col), pl.BlockSpec((8, tn), col),
                   pl.BlockSpec((1, tn), col), pl.BlockSpec((1, tn), col)],
        out_shape=[jax.ShapeDtypeStruct((s, D_FF), BF16), jax.ShapeDtypeStruct((s, D_FF), BF16),
                   jax.ShapeDtypeStruct((8, D_FF), F32), jax.ShapeDtypeStruct((8, D_FF), F32),
                   jax.ShapeDtypeStruct((1, D_FF), F32), jax.ShapeDtypeStruct((1, D_FF), F32)],
        compiler_params=_cp("parallel", "arbitrary"),
    )(dy3, dy3, w_down, hid_g, hid_u, conv_g, conv_g, conv_u, conv_u, cw, cw)


def _slot(p):
    return 4 * p[0] + 2 * p[1] + p[2]


def _all_gather(shards, name):
    n = len(shards)

    def body(*refs):
        ins, outs = refs[:n], refs[n:2 * n]
        send_sems, recv_sems, local_sems = refs[2 * n:]
        x, y, c = lax.axis_index("x"), lax.axis_index("y"), lax.axis_index("c")
        me, sibling = (x, y, c), (x, y, 1 - c)
        chips = [(1 - x, y), (x, 1 - y), (1 - x, 1 - y)]

        def copy(a, k, block, to, from_input=False):
            dst = outs[a].at[_slot(block)]
            return pltpu.make_async_remote_copy(
                src_ref=ins[a] if from_input else dst, dst_ref=dst,
                send_sem=send_sems.at[a, k], recv_sem=recv_sems.at[a, k],
                device_id=to, device_id_type=MESH)

        mine = [pltpu.make_async_copy(ins[a], outs[a].at[_slot(me)], local_sems.at[a]) for a in range(n)]
        for cp in mine:
            cp.start()
        first = []
        for a in range(n):
            first.append(copy(a, 0, me, sibling, True))
            first += [copy(a, 1 + j, me, (*chip, c), True) for j, chip in enumerate(chips)]
        for cp in first:
            cp.start()
        passed = []
        for j, chip in enumerate(chips):
            for a in range(n):
                copy(a, 1 + j, (*chip, c), me).wait_recv()
                fwd = copy(a, 4 + j, (*chip, c), sibling)
                fwd.start()
                passed.append(fwd)
        for a in range(n):
            copy(a, 0, sibling, me).wait_recv()
            for j, chip in enumerate(chips):
                copy(a, 4 + j, (*chip, 1 - c), me).wait_recv()
        for cp in first + passed:
            cp.wait_send()
        for cp in mine:
            cp.wait()

    any_spec = pl.BlockSpec(memory_space=pl.ANY)
    return pl.pallas_call(
        body, name=name,
        in_specs=[any_spec] * n, out_specs=[any_spec] * n,
        out_shape=[jax.ShapeDtypeStruct((NDEV,) + s.shape, s.dtype) for s in shards],
        scratch_shapes=[pltpu.SemaphoreType.DMA((n, 7)), pltpu.SemaphoreType.DMA((n, 7)),
                        pltpu.SemaphoreType.DMA((n,))],
    )(*shards)


def _peer_list(x, y, c):
    return [(1 - x if m & 4 else x, 1 - y if m & 2 else y, 1 - c if m & 1 else c) for m in range(1, NDEV)]


def _exchange_copies(src_refs, land_refs, send_sems, recv_sems, gather):
    x, y, c = lax.axis_index("x"), lax.axis_index("y"), lax.axis_index("c")
    me = (x, y, c)
    copies = []
    for m, peer in enumerate(_peer_list(x, y, c)):
        for a in range(len(src_refs)):
            copies.append(pltpu.make_async_remote_copy(
                src_ref=src_refs[a] if gather else src_refs[a].at[_slot(peer)], dst_ref=land_refs[a].at[_slot(me)],
                send_sem=send_sems.at[a * (NDEV - 1) + m], recv_sem=recv_sems.at[a * (NDEV - 1) + m],
                device_id=peer, device_id_type=MESH))
    return copies


def _all_gather_small(shards, name):
    n = len(shards)

    def body(*refs):
        ins, outs = refs[:n], refs[n:2 * n]
        send_sems, recv_sems, local_sems = refs[2 * n:]
        me = (lax.axis_index("x"), lax.axis_index("y"), lax.axis_index("c"))
        mine = [pltpu.make_async_copy(ins[a], outs[a].at[_slot(me)], local_sems.at[a]) for a in range(n)]
        copies = _exchange_copies(ins, outs, send_sems, recv_sems, True)
        for cp in mine + copies:
            cp.start()
        for cp in copies + mine:
            cp.wait()

    any_spec = pl.BlockSpec(memory_space=pl.ANY)
    return pl.pallas_call(
        body, name=name,
        in_specs=[any_spec] * n, out_specs=[any_spec] * n,
        out_shape=[jax.ShapeDtypeStruct((NDEV,) + s.shape, s.dtype) for s in shards],
        scratch_shapes=[pltpu.SemaphoreType.DMA((n * (NDEV - 1),)), pltpu.SemaphoreType.DMA((n * (NDEV - 1),)),
                        pltpu.SemaphoreType.DMA((n,))],
    )(*shards)


def _exchange_start(srcs, lands, after, gather, name):
    n = len(srcs)
    hbm = pl.BlockSpec(memory_space=pltpu.HBM)

    def body(*refs):
        for cp in _exchange_copies(refs[:n], refs[n:2 * n], refs[2 * n + 1], refs[2 * n + 2], gather):
            cp.start()
        token = refs[-1]
        token[...] = jnp.zeros_like(token)

    outs = pl.pallas_call(
        body, name=name,
        out_shape=(pltpu.SemaphoreType.DMA((n * (NDEV - 1),)), pltpu.SemaphoreType.DMA((n * (NDEV - 1),)),
                   *[pltpu.HBM(a.shape, a.dtype) for a in list(srcs) + list(lands)],
                   jax.ShapeDtypeStruct((8, 128), F32)),
        in_specs=[hbm] * (2 * n) + [pl.BlockSpec(memory_space=pl.ANY)],
        out_specs=(pl.BlockSpec(memory_space=pltpu.SEMAPHORE), pl.BlockSpec(memory_space=pltpu.SEMAPHORE),
                   *[hbm] * (2 * n), pl.BlockSpec(memory_space=pltpu.VMEM)),
        input_output_aliases={i: 2 + i for i in range(2 * n)},
        compiler_params=pltpu.CompilerParams(has_side_effects=pltpu.SideEffectType.DATAFLOW_SIDE_EFFECTING),
    )(*[pltpu.with_memory_space_constraint(a, pltpu.HBM) for a in list(srcs) + list(lands)], after)
    return outs[0], outs[1], outs[2:2 + n], outs[2 + n:2 + 2 * n], outs[-1]


def _exchange_wait(send_sems, recv_sems, srcs, lands, after, gather, name):
    n = len(srcs)
    hbm = pl.BlockSpec(memory_space=pltpu.HBM)

    def body(*refs):
        for cp in _exchange_copies(refs[:n], refs[n:2 * n], refs[2 * n], refs[2 * n + 1], gather):
            cp.wait_send()
            cp.wait_recv()

    outs = pl.pallas_call(
        body, name=name,
        out_shape=tuple(pltpu.HBM(a.shape, a.dtype) for a in list(srcs) + list(lands)),
        in_specs=[hbm] * (2 * n) + [pl.BlockSpec(memory_space=pltpu.SEMAPHORE)] * 2 + [pl.BlockSpec(memory_space=pl.ANY)],
        out_specs=tuple([hbm] * (2 * n)),
        input_output_aliases={i: i for i in range(2 * n)},
        compiler_params=pltpu.CompilerParams(has_side_effects=pltpu.SideEffectType.DATAFLOW_SIDE_EFFECTING),
    )(*srcs, *lands, send_sems, recv_sems, after)
    return outs[n:]


def _own_slot(block):
    me = 4 * lax.axis_index("x") + 2 * lax.axis_index("y") + lax.axis_index("c")
    return lax.dynamic_update_slice(lax.empty((NDEV,) + block.shape, block.dtype), block[None], (me, 0, 0))


def _adam_update(p_ref, w_ref, m_ref, v_ref, g_ref, d_ref, mo_ref, vo_ref):
    bc1 = 1.0 - ADAM_B1 ** ADAM_STEP
    bc2 = 1.0 - ADAM_B2 ** ADAM_STEP
    g = p_ref[0].astype(F32)
    for d in range(1, NDEV):
        g = g + p_ref[d].astype(F32)
    g_ref[...] = g
    mn = ADAM_B1 * m_ref[...] + (1.0 - ADAM_B1) * g
    vn = ADAM_B2 * v_ref[...] + (1.0 - ADAM_B2) * (g * g)
    mo_ref[...] = mn
    vo_ref[...] = vn
    d_ref[...] = -ADAM_LR * ((mn / bc1) / (jnp.sqrt(vn / bc2) + ADAM_EPS) + ADAM_WD * w_ref[...])


def _adamw_small(parts, ws, ms, vs, name):
    n = len(ws)

    def body(*refs):
        ins, outs = refs[:4 * n], refs[4 * n:]
        for k in range(n):
            _adam_update(ins[k], ins[n + k], ins[2 * n + k], ins[3 * n + k], *outs[4 * k:4 * k + 4])

    whole = pl.BlockSpec(memory_space=pltpu.VMEM)
    res = pl.pallas_call(
        body, name=name, in_specs=[whole] * (4 * n), out_specs=[whole] * (4 * n),
        out_shape=[jax.ShapeDtypeStruct(a.shape, F32) for a in ws for _ in range(4)],
    )(*parts, *ws, *ms, *vs)
    return [res[4 * k:4 * k + 4] for k in range(n)]


def _adamw(parts, w, m, v, name):
    r, c = w.shape
    tr = r if r * c <= 160 * 1024 else max(8, (160 * 1024 // c) // 8 * 8)
    while r % tr:
        tr -= 8
    body = functools.partial(_adam_update)
    spec = pl.BlockSpec((tr, c), lambda i: (i, 0))
    return pl.pallas_call(
        body, name=name, grid=(r // tr,),
        in_specs=[pl.BlockSpec((NDEV, tr, c), lambda i: (0, i, 0)), spec, spec, spec],
        out_specs=[spec] * 4, out_shape=[jax.ShapeDtypeStruct((r, c), F32)] * 4,
        compiler_params=_cp("parallel"),
    )(parts, w, m, v)


def _local_step(x, mem, tgt, gains, b_forget, w_pool, pool_scale, conv_b, w_in,
                mix_weights, ffn_weights, send_in_grad, send_mix_grads, send_ffn_grads):
    b_pad = jnp.pad(b_forget, ((0, 0), (0, 128 - FOX_HEADS)))
    wbd = jnp.zeros((D_POOL, D_POOL), F32)
    for g in range(4):
        wbd = wbd.at[64 * g:64 * g + 64, 64 * g:64 * g + 64].set(w_pool[g])
    wbd = wbd.astype(BF16)
    scale = pool_scale.reshape(1, D_POOL)

    h1, proj, fraw = _proj_in(x, gains["mix_pre"], w_in, "proj_in")
    flog, aq, ak = _gate_cumsum(fraw, b_pad, "gate_cumsum")
    ycat, aqb = _fox_fwd(proj, aq, ak, "fox_fwd")
    ycat = _pool_fwd(proj, wbd, scale, ycat, "pool_fwd")
    w_mix, w_xq, w_xo, w_xkv = mix_weights(ycat)
    y1, x1, h2 = _mm_rows(ycat, w_mix, "nn", 1024, "mix_out", [x], [gains["mix_post"], gains["xa_pre"]],
                          [F32, F32, BF16], _epi_resid)
    q2 = _mm(h2, w_xq, "nn", BF16, 2048, 1024, 1024, "xa_q")
    mem_n = _norm_fwd(mem, gains["mem"], "norm_mem")
    kv = _mm(mem_n, w_xkv, "nn", BF16, mem.shape[0], 256, 1024, "xa_kv", b_cols=256)
    o2 = _xattn_fwd(q2, kv, "xattn_fwd")
    y2, x2, h3 = _mm_rows(o2, w_xo, "nn", 1024, "xa_out", [x1], [gains["xa_post"], gains["ffn_pre"]],
                          [F32, F32, BF16], _epi_resid)
    w_up, w_down, cw = ffn_weights(h3)
    hid_g, hid_u, conv_g, conv_u, act, loss, dx3, dy3, dg_ffn_post = _ffn_fwd(
        h3, w_up, cw, conv_b, w_down, x2, tgt, gains["ffn_post"], "ffn_fwd")

    dhid_g, dhid_u, dcw_g, dcw_u, dcb_g, dcb_u = _ffn_bwd(dy3, w_down, hid_g, hid_u, conv_g, conv_u, cw, "ffn_bwd")
    d_w_down = _mm(act, dy3, "tn", BF16, 2048, 1024, 1024, "dw_down")
    d_w_up = _mm(h3, [dhid_g, dhid_u], "tn", BF16, 1024, 1024, 2048, "dw_up", out_cols=1024)
    sent = send_ffn_grads(d_w_up, d_w_down, jnp.concatenate([dcw_g, dcw_u], axis=1))
    dh3 = _mm([dhid_g, dhid_u], w_up, "nt", F32, 2048, 1024, 1024, "dh_ffn", b_cols=1024, after=sent)
    dx2, dg_ffn_pre, dy2, dg_xa_post = _norm_bwd(dh3, x2, dx3, gains["ffn_pre"], "norm_bwd_ffn",
                                                 prev=(y2, gains["xa_post"]))
    do2 = _mm(dy2, w_xo, "nt", BF16, 2048, 1024, 1024, "d_xa_out")
    d_w_xo = _mm(o2, dy2, "tn", BF16, 1024, 1024, 1024, "dw_xo")
    dq2, dkv = _xattn_bwd(q2, kv, do2, "xattn_bwd")
    dkv = dkv.astype(BF16)
    dx1, dg_xa_pre, dy1, dg_mix_post = _mm_rows(
        dq2, w_xq, "nt", 1024, "dh_xa", [x1, dx2, y1], [gains["xa_pre"], gains["mix_post"]],
        [F32, "sum", BF16, "sum"], _epi_norm_bwd)
    d_w_xq = _mm(h2, dq2, "tn", BF16, 1024, 1024, 1024, "dw_xq")
    dmem_n = _mm(dkv, w_xkv, "nt", F32, mem.shape[0], 1024, 256, "d_mem", b_cols=256)
    d_w_xkv = _mm(mem_n, dkv, "tn", BF16, 1024, 256, mem.shape[0], "dw_xkv", out_cols=256)
    _, dg_mem = _norm_bwd(dmem_n, mem, jnp.zeros_like(mem), gains["mem"], "norm_bwd_mem")
    dycat = _mm(dy1, w_mix, "nt", BF16, 2048, 1024, 1024, "d_mix_out")
    d_w_mix = _mm(ycat, dy1, "tn", BF16, 1024, 1024, 1024, "dw_mix")
    sent_mix = send_mix_grads(d_w_mix, d_w_xq, d_w_xo, d_w_xkv)
    ad = _fox_do_operand(dycat, ycat, sent_mix, "fox_do_operand")
    dq, dk, dv, qaux, kaux = _fox_bwd(proj, dycat, aqb, ak, ad, "fox_bwd")
    du, d_wbd, d_scale = _pool_bwd(proj, dycat, wbd, scale, "pool_bwd")
    df, db_f = _gate_bwd(qaux, kaux, flog, "gate_bwd")
    dproj = [du, dq, dk, dv, df]
    sent_in = send_in_grad(_dw_in(h1, dproj, "dw_in"))
    grad_x, dg_mix_pre = _mm_rows(dproj, w_in, "nt", None, "dh_mix", [x, dx1], [gains["mix_pre"]],
                                  [F32, "sum"], _epi_norm_bwd, after=sent_in)

    small = dict(
        mix_pre=dg_mix_pre, mix_post=dg_mix_post, mem=dg_mem, xa_pre=dg_xa_pre, xa_post=dg_xa_post,
        ffn_pre=dg_ffn_pre, ffn_post=dg_ffn_post,
        conv_b=jnp.concatenate([dcb_g, dcb_u], axis=1),
        w_pool=jnp.concatenate([d_wbd[64 * g:64 * g + 64, 64 * g:64 * g + 64] for g in range(4)], axis=0),
        pool_scale=d_scale.reshape(4, 64),
        b_forget=db_f[:, :FOX_HEADS],
    )
    return loss, grad_x, small


SMALL_ORDER = ("mix_pre", "mix_post", "mem", "xa_pre", "xa_post", "ffn_pre", "ffn_post", "conv_b",
               "w_pool", "pool_scale", "b_forget")


def kernel(x, mem, norm_mix_pre, norm_mix_post, w_in, b_forget, w_pool, pool_scale, w_mix_out, norm_mem, norm_xa_pre, norm_xa_post, w_xq, w_xkv, w_xo, norm_ffn_pre, norm_ffn_post, w_up, conv_w, conv_b, w_down, loss_target, m_norm_mix_pre, m_norm_mix_post, m_w_in, m_b_forget, m_w_pool, m_pool_scale, m_w_mix_out, m_norm_mem, m_norm_xa_pre, m_norm_xa_post, m_w_xq, m_w_xkv, m_w_xo, m_norm_ffn_pre, m_norm_ffn_post, m_w_up, m_conv_w, m_conv_b, m_w_down, v_norm_mix_pre, v_norm_mix_post, v_w_in, v_b_forget, v_w_pool, v_pool_scale, v_w_mix_out, v_norm_mem, v_norm_xa_pre, v_norm_xa_post, v_w_xq, v_w_xkv, v_w_xo, v_norm_ffn_pre, v_norm_ffn_post, v_w_up, v_conv_w, v_conv_b, v_w_down):
    names = ("norm_mix_pre", "norm_mix_post", "w_in", "b_forget", "w_pool", "pool_scale", "w_mix_out", "norm_mem",
             "norm_xa_pre", "norm_xa_post", "w_xq", "w_xkv", "w_xo", "norm_ffn_pre", "norm_ffn_post", "w_up",
             "conv_w", "conv_b", "w_down")
    w = dict(zip(names, (norm_mix_pre, norm_mix_post, w_in, b_forget, w_pool, pool_scale, w_mix_out, norm_mem,
                         norm_xa_pre, norm_xa_post, w_xq, w_xkv, w_xo, norm_ffn_pre, norm_ffn_post, w_up,
                         conv_w, conv_b, w_down)))
    mo = dict(zip(names, (m_norm_mix_pre, m_norm_mix_post, m_w_in, m_b_forget, m_w_pool, m_pool_scale, m_w_mix_out,
                          m_norm_mem, m_norm_xa_pre, m_norm_xa_post, m_w_xq, m_w_xkv, m_w_xo, m_norm_ffn_pre,
                          m_norm_ffn_post, m_w_up, m_conv_w, m_conv_b, m_w_down)))
    vo = dict(zip(names, (v_norm_mix_pre, v_norm_mix_post, v_w_in, v_b_forget, v_w_pool, v_pool_scale, v_w_mix_out,
                          v_norm_mem, v_norm_xa_pre, v_norm_xa_post, v_w_xq, v_w_xkv, v_w_xo, v_norm_ffn_pre,
                          v_norm_ffn_post, v_w_up, v_conv_w, v_conv_b, v_w_down)))

    big_names = ("w_in", "w_mix_out", "w_xq", "w_xo", "w_xkv", "w_up", "w_down")
    shards = {k: w[k][0].astype(BF16) for k in big_names}
    shards["w_in"] = jnp.pad(shards["w_in"], ((0, 0), (0, D_IN_PAD - shards["w_in"].shape[1])))
    conv_w_sh = jnp.pad(conv_w[0, :, 0, :], ((0, 5), (0, 0)))
    (g_in,) = _all_gather([shards["w_in"]], "gather_w_in")
    mix_srcs = [shards[k] for k in ("w_mix_out", "w_xq", "w_xo", "w_xkv")]
    mix_flight = _exchange_start(mix_srcs, [_own_slot(a) for a in mix_srcs], g_in, True, "gather_mix_start")
    ffn_srcs = [shards["w_up"], shards["w_down"], conv_w_sh]
    ffn_flight = _exchange_start(ffn_srcs, [_own_slot(a) for a in ffn_srcs], mix_flight[4], True, "gather_ffn_start")
    my_slot = 4 * lax.axis_index("x") + 2 * lax.axis_index("y") + lax.axis_index("c")
    own_block = lambda a: _own_slot(lax.dynamic_index_in_dim(a, my_slot, 0, keepdims=False))
    by_rows = lambda a: a.reshape(NDEV, a.shape[0] // NDEV, a.shape[1])
    by_cols = lambda a: a.reshape(a.shape[0], NDEV, a.shape[1] // NDEV).transpose(1, 0, 2)
    grad_flight = {}

    def mix_weights(after):
        g_mix, g_xq, g_xo, g_xkv = _exchange_wait(*mix_flight[:4], after, True, "gather_mix_wait")
        return (g_mix.reshape(D_MODEL, D_MODEL), g_xq.reshape(D_MODEL, D_MODEL), g_xo.reshape(D_MODEL, D_MODEL), g_xkv)

    def ffn_weights(after):
        g_up, g_down, g_cw = _exchange_wait(*ffn_flight[:4], after, True, "gather_ffn_wait")
        return g_up, g_down.reshape(D_FF, D_MODEL), g_cw.transpose(1, 0, 2).reshape(8, 2 * D_FF)

    def send_ffn_grads(d_w_up, d_w_down, d_cw):
        srcs = [d_w_up, by_rows(d_w_down), by_cols(d_cw)]
        grad_flight["ffn"] = _exchange_start(srcs, [own_block(a) for a in srcs], ffn_flight[4], False, "scatter_ffn_start")
        return grad_flight["ffn"][4]

    def send_mix_grads(d_w_mix, d_w_xq, d_w_xo, d_w_xkv):
        srcs = [by_rows(d_w_mix), by_rows(d_w_xq), by_rows(d_w_xo), d_w_xkv]
        grad_flight["mix"] = _exchange_start(srcs, [own_block(a) for a in srcs], ffn_flight[4], False, "scatter_mix_start")
        return grad_flight["mix"][4]

    def send_in_grad(d_w_in):
        srcs = [by_rows(d_w_in)]
        grad_flight["in"] = _exchange_start(srcs, [own_block(a) for a in srcs], ffn_flight[4], False, "scatter_in_start")
        return grad_flight["in"][4]

    gains = dict(mix_pre=norm_mix_pre + ffn_flight[4][0, 0], mix_post=norm_mix_post, mem=norm_mem, xa_pre=norm_xa_pre,
                 xa_post=norm_xa_post, ffn_pre=norm_ffn_pre, ffn_post=norm_ffn_post)
    loss, grad_x, small = _local_step(
        x[0], mem[0], loss_target[0], gains, b_forget, w_pool[0], pool_scale[0], conv_b,
        g_in.reshape(D_MODEL, D_IN_PAD), mix_weights, ffn_weights, send_in_grad, send_mix_grads, send_ffn_grads)

    p_up, p_down, p_cw = _exchange_wait(*grad_flight["ffn"][:4], grad_x, False, "scatter_ffn_wait")
    p_mix, p_xq, p_xo, p_xkv = _exchange_wait(*grad_flight["mix"][:4], grad_x, False, "scatter_mix_wait")
    parts = dict(w_mix_out=p_mix, w_xq=p_xq, w_xo=p_xo, w_xkv=p_xkv, w_up=p_up, w_down=p_down)
    *small_parts, loss_parts = _all_gather_small([small[k] for k in SMALL_ORDER] + [loss], "gather_small_grads")

    res = {k: [a[None] for a in _adamw(p, w[k][0], mo[k][0], vo[k][0], "adamw_" + k)] for k, p in parts.items()}
    pad_cw = lambda a: jnp.pad(a[0, :, 0, :], ((0, 5), (0, 0)))
    res["conv_w"] = [a[:3][None, :, None, :] for a in
                     _adamw(p_cw, pad_cw(conv_w), pad_cw(m_conv_w), pad_cw(v_conv_w), "adamw_conv_w")]
    key_of = dict(mix_pre="norm_mix_pre", mix_post="norm_mix_post", mem="norm_mem", xa_pre="norm_xa_pre",
                  xa_post="norm_xa_post", ffn_pre="norm_ffn_pre", ffn_post="norm_ffn_post", conv_b="conv_b",
                  w_pool="w_pool", pool_scale="pool_scale", b_forget="b_forget")
    flat2d = lambda src: [src[key_of[k]].reshape(small[k].shape) for k in SMALL_ORDER]
    small_out = _adamw_small(small_parts, flat2d(w), flat2d(mo), flat2d(vo), "adamw_small")
    for k, four in zip(SMALL_ORDER, small_out):
        res[key_of[k]] = [a.reshape(w[key_of[k]].shape) for a in four]
    (p_in,) = _exchange_wait(*grad_flight["in"][:4], res["w_up"][1], False, "scatter_in_wait")
    res["w_in"] = [a[None] for a in _adamw(p_in[:, :, :w_in.shape[2]], w["w_in"][0], mo["w_in"][0], vo["w_in"][0],
                                           "adamw_w_in")]

    outs = [jnp.sum(loss_parts[:, 0, 0]), grad_x[None]]
    for idx in range(4):
        outs += [res[k][idx] for k in names]
    return tuple(outs)
```

```python
import functools
import math

import jax
import jax.numpy as jnp
from jax import lax
from jax.experimental import pallas as pl
from jax.experimental.pallas import tpu as pltpu

F32 = jnp.float32
BF16 = jnp.bfloat16

NDEV = 8
D_MODEL = 1024
D_POOL = 256
D_FOX = 768
FOX_HEADS = 12
HEAD_PAIRS = FOX_HEADS // 2
XA_HEADS = 4
XA_DIM = 256
D_FF = 4096
D_IN_PAD = 2688
F_COL = 2560
POOL_HALO = 16
NORM_EPS = 1e-6
NEG = -1e30

ADAM_LR = 0.001
ADAM_B1 = 0.9
ADAM_B2 = 0.999
ADAM_EPS = 1e-08
ADAM_WD = 0.01
ADAM_STEP = 10

TM = 512
TQ = 512
TN_FF = 1024
VMEM_LIMIT = 56 * 1024 * 1024
MESH = pl.DeviceIdType.MESH


def _cp(*sem):
    return pltpu.CompilerParams(dimension_semantics=sem, vmem_limit_bytes=VMEM_LIMIT)


def _dot(a, b, dims):
    return lax.dot_general(a, b, (dims, ((), ())), preferred_element_type=F32)


NN = ((1,), (0,))
NT = ((1,), (1,))
TN = ((0,), (0,))


def _mm(a, b, mode, out_dtype, tm, tn, tk, name, b_cols=None, out_cols=None, after=None):
    a_list = list(a) if isinstance(a, (list, tuple)) else [a]
    b_list = list(b) if isinstance(b, (list, tuple)) else [b]
    assert len(a_list) == 1 or len(b_list) == 1
    if mode == "tn":
        K, M = a_list[0].shape
        assert len(a_list) == 1
        Ns = [x.shape[1] for x in b_list]
        N = sum(Ns)
        assert b_cols is None
    else:
        assert len(b_list) == 1
        M = a_list[0].shape[0]
        Ks = [x.shape[1] for x in a_list]
        K = sum(Ks)
        if b_cols is None:
            N = b_list[0].shape[0] if mode == "nt" else b_list[0].shape[1]
        else:
            N = b_list[0].shape[1] if mode == "nt" else NDEV * b_cols
    assert M % tm == 0 and N % tn == 0 and K % tk == 0, (name, M, N, K)
    grid = (M // tm, N // tn, K // tk)
    nk = grid[2]
    dims = {"nn": NN, "nt": NT, "tn": TN}[mode]

    in_specs = []
    if mode == "tn":
        in_specs.append(pl.BlockSpec((tk, tm), lambda i, j, k: (k, i)))
        if len(b_list) == 1:
            in_specs.append(pl.BlockSpec((tk, tn), lambda i, j, k: (k, j)))
        else:
            nj1 = Ns[0] // tn
            in_specs.append(pl.BlockSpec((tk, tn), lambda i, j, k: (k, jnp.minimum(j, nj1 - 1))))
            in_specs.append(pl.BlockSpec((tk, tn), lambda i, j, k: (k, jnp.maximum(j - nj1, 0))))
    else:
        if len(a_list) == 1:
            in_specs.append(pl.BlockSpec((tm, tk), lambda i, j, k: (i, k)))
        else:
            nk1 = Ks[0] // tk
            in_specs.append(pl.BlockSpec((tm, tk), lambda i, j, k: (i, jnp.minimum(k, nk1 - 1))))
            in_specs.append(pl.BlockSpec((tm, tk), lambda i, j, k: (i, jnp.maximum(k - nk1, 0))))
        if b_cols is None:
            if mode == "nn":
                in_specs.append(pl.BlockSpec((tk, tn), lambda i, j, k: (k, j)))
            else:
                in_specs.append(pl.BlockSpec((tn, tk), lambda i, j, k: (j, k)))
        else:
            if mode == "nn":
                per = b_cols // tn
                in_specs.append(pl.BlockSpec((None, tk, tn), lambda i, j, k: (j // per, k, j % per)))
            else:
                per = b_cols // tk
                in_specs.append(pl.BlockSpec((None, tn, tk), lambda i, j, k: (k // per, j, k % per)))
    if out_cols is None:
        out_spec = pl.BlockSpec((tm, tn), lambda i, j, k: (i, j))
        out_shape = jax.ShapeDtypeStruct((M, N), out_dtype)
    else:
        pero = out_cols // tn
        out_spec = pl.BlockSpec((None, tm, tn), lambda i, j, k: (j // pero, i, j % pero))
        out_shape = jax.ShapeDtypeStruct((NDEV, M, out_cols), out_dtype)

    two_a = len(a_list) == 2
    two_b = len(b_list) == 2
    extra = []
    if after is not None:
        in_specs.append(pl.BlockSpec(memory_space=pl.ANY))
        extra.append(after)

    def body(*refs):
        o_ref, acc_ref = refs[-2], refs[-1]
        j = pl.program_id(1)
        k = pl.program_id(2)

        @pl.when(k == 0)
        def _():
            acc_ref[...] = jnp.zeros_like(acc_ref)

        if two_a:
            a1, a2, b1 = refs[0], refs[1], refs[2]
            nk1_ = Ks[0] // tk

            @pl.when(k < nk1_)
            def _():
                acc_ref[...] += _dot(a1[...], b1[...], dims)

            @pl.when(k >= nk1_)
            def _():
                acc_ref[...] += _dot(a2[...], b1[...], dims)
        elif two_b:
            a1, b1, b2 = refs[0], refs[1], refs[2]
            nj1_ = Ns[0] // tn

            @pl.when(j < nj1_)
            def _():
                acc_ref[...] += _dot(a1[...], b1[...], dims)

            @pl.when(j >= nj1_)
            def _():
                acc_ref[...] += _dot(a1[...], b2[...], dims)
        else:
            acc_ref[...] += _dot(refs[0][...], refs[1][...], dims)

        @pl.when(k == nk - 1)
        def _():
            o_ref[...] = acc_ref[...].astype(o_ref.dtype)

    return pl.pallas_call(
        body, name=name, grid=grid, in_specs=in_specs, out_specs=out_spec, out_shape=out_shape,
        scratch_shapes=[pltpu.VMEM((tm, tn), F32)],
        compiler_params=_cp("parallel", "parallel", "arbitrary"),
    )(*a_list, *b_list, *extra)


def _rstd(x):
    return lax.rsqrt(jnp.mean(x * x, axis=-1, keepdims=True) + NORM_EPS)


def _norm_bwd_rows(dxn, xn, r):
    return r * (dxn - xn * jnp.mean(dxn * xn, axis=-1, keepdims=True))


def _row_spec(tm, d):
    return pl.BlockSpec((tm, d), lambda i: (i, 0))


def _vec_spec(d):
    return pl.BlockSpec((1, d), lambda i: (0, 0))


def _mm_rows(a, b, mode, tk, name, rows, vecs, outs, epilogue, b_cols=None, after=None):
    a_list = list(a) if isinstance(a, (list, tuple)) else [a]
    m = a_list[0].shape[0]
    ks = [x.shape[1] for x in a_list]
    n = D_MODEL
    pieces = tk is None
    nk = 1 if pieces else sum(ks) // tk
    dims = NN if mode == "nn" else NT
    if pieces:
        assert mode == "nt" and b_cols is None
        in_specs = [pl.BlockSpec((TM, kp), lambda i, k: (i, 0)) for kp in ks]
        tk = sum(ks)
    elif len(a_list) == 1:
        in_specs = [pl.BlockSpec((TM, tk), lambda i, k: (i, k))]
    else:
        nk1 = ks[0] // tk
        in_specs = [pl.BlockSpec((TM, tk), lambda i, k: (i, jnp.minimum(k, nk1 - 1))),
                    pl.BlockSpec((TM, tk), lambda i, k: (i, jnp.maximum(k - nk1, 0)))]
    if mode == "nn":
        in_specs.append(pl.BlockSpec((tk, n), lambda i, k: (k, 0)))
    elif b_cols is None:
        in_specs.append(pl.BlockSpec((n, tk), lambda i, k: (0, k)))
    else:
        per = b_cols // tk
        in_specs.append(pl.BlockSpec((None, n, tk), lambda i, k: (k // per, 0, k % per)))
    in_specs += [pl.BlockSpec((TM, n), lambda i, k: (i, 0))] * len(rows)
    in_specs += [pl.BlockSpec((1, n), lambda i, k: (0, 0))] * len(vecs)
    extra = []
    if after is not None:
        in_specs.append(pl.BlockSpec(memory_space=pl.ANY))
        extra.append(after)
    out_specs, out_shape = [], []
    for o in outs:
        if o == "sum":
            out_specs.append(pl.BlockSpec((1, n), lambda i, k: (0, 0)))
            out_shape.append(jax.ShapeDtypeStruct((1, n), F32))
        else:
            out_specs.append(pl.BlockSpec((TM, n), lambda i, k: (i, 0)))
            out_shape.append(jax.ShapeDtypeStruct((m, n), o))
    na, nr, nv = len(a_list), len(rows), len(vecs)

    def body(*refs):
        a_refs, b_ref = refs[:na], refs[na]
        row_refs = refs[na + 1:na + 1 + nr]
        vec_refs = refs[na + 1 + nr:na + 1 + nr + nv]
        out_refs = refs[len(refs) - 1 - len(outs):len(refs) - 1]
        acc_ref = refs[-1]
        i, k = pl.program_id(0), pl.program_id(1)

        @pl.when(k == 0)
        def _():
            acc_ref[...] = jnp.zeros_like(acc_ref)

        if pieces:
            off = 0
            for a_ref in a_refs:
                kp = a_ref.shape[1]
                acc_ref[...] += _dot(a_ref[...], b_ref[:, off:off + kp], dims)
                off += kp
        elif na == 1:
            acc_ref[...] += _dot(a_refs[0][...], b_ref[...], dims)
        else:
            nk1_ = ks[0] // tk

            @pl.when(k < nk1_)
            def _():
                acc_ref[...] += _dot(a_refs[0][...], b_ref[...], dims)

            @pl.when(k >= nk1_)
            def _():
                acc_ref[...] += _dot(a_refs[1][...], b_ref[...], dims)

        @pl.when(k == nk - 1)
        def _():
            vals = epilogue(acc_ref[...], [r[...] for r in row_refs], [v[...] for v in vec_refs])
            for o, ref, val in zip(outs, out_refs, vals):
                if o == "sum":
                    @pl.when(i == 0)
                    def _():
                        ref[...] = val

                    @pl.when(i > 0)
                    def _():
                        ref[...] += val
                else:
                    ref[...] = val.astype(o)

    return pl.pallas_call(
        body, name=name, grid=(m // TM, nk), in_specs=in_specs, out_specs=out_specs, out_shape=out_shape,
        scratch_shapes=[pltpu.VMEM((TM, n), F32)],
        compiler_params=_cp("arbitrary", "arbitrary"),
    )(*a_list, b, *rows, *vecs, *extra)


def _proj_in(x, g, w_in, name):
    s, d = x.shape
    n = w_in.shape[1]

    def body(x_ref, g_ref, w_ref, h_ref, p_ref, f_ref):
        xv = x_ref[...]
        h = (xv * _rstd(xv) * g_ref[...]).astype(BF16)
        h_ref[...] = h
        acc = _dot(h, w_ref[...], NN)
        p_ref[...] = acc.astype(BF16)
        f_ref[...] = acc[:, F_COL:]

    return pl.pallas_call(
        body, name=name, grid=(s // TM,),
        in_specs=[_row_spec(TM, d), _vec_spec(d), pl.BlockSpec((d, n), lambda i: (0, 0))],
        out_specs=[_row_spec(TM, d), _row_spec(TM, n), _row_spec(TM, n - F_COL)],
        out_shape=[jax.ShapeDtypeStruct((s, d), BF16), jax.ShapeDtypeStruct((s, n), BF16),
                   jax.ShapeDtypeStruct((s, n - F_COL), F32)],
        compiler_params=_cp("parallel"),
    )(x, g, w_in)


def _dw_in(h, pieces, name):
    s, d = h.shape
    n = sum(p.shape[1] for p in pieces)
    tk = 1024
    nk = s // tk

    def body(*refs):
        h_ref, piece_refs, o_ref, acc_ref = refs[0], refs[1:-2], refs[-2], refs[-1]
        k = pl.program_id(1)

        @pl.when(k == 0)
        def _():
            acc_ref[...] = jnp.zeros_like(acc_ref)

        off = 0
        for p_ref in piece_refs:
            w = p_ref.shape[1]
            acc_ref[:, off:off + w] += _dot(h_ref[...], p_ref[...], TN)
            off += w

        @pl.when(k == nk - 1)
        def _():
            o_ref[...] = acc_ref[...].astype(BF16)

    return pl.pallas_call(
        body, name=name, grid=(d // TM, nk),
        in_specs=[pl.BlockSpec((tk, TM), lambda i, k: (k, i))] +
                 [pl.BlockSpec((tk, p.shape[1]), lambda i, k: (k, 0)) for p in pieces],
        out_specs=pl.BlockSpec((TM, n), lambda i, k: (i, 0)),
        out_shape=jax.ShapeDtypeStruct((d, n), BF16),
        scratch_shapes=[pltpu.VMEM((TM, n), F32)],
        compiler_params=_cp("parallel", "arbitrary"),
    )(h, *pieces)


def _epi_resid(y, rows, vecs):
    (x_in,), (g_post, g_next) = rows, vecs
    xo = x_in + y * _rstd(y) * g_post
    return y, xo, xo * _rstd(xo) * g_next


def _epi_norm_bwd(dh, rows, vecs):
    x, dx_res = rows[0], rows[1]
    r = _rstd(x)
    xn = x * r
    dx = dx_res + _norm_bwd_rows(dh * vecs[0], xn, r)
    res = [dx, jnp.sum(dh * xn, axis=0, keepdims=True)]
    if len(rows) == 3:
        y = rows[2]
        r2 = _rstd(y)
        yn = y * r2
        res += [_norm_bwd_rows(dx * vecs[1], yn, r2), jnp.sum(dx * yn, axis=0, keepdims=True)]
    return res


def _norm_fwd(x, g, name):
    s, d = x.shape
    tm = min(TM, s)

    def body(x_ref, g_ref, h_ref):
        xv = x_ref[...]
        h_ref[...] = (xv * _rstd(xv) * g_ref[...]).astype(BF16)

    return pl.pallas_call(
        body, name=name, grid=(s // tm,), in_specs=[_row_spec(tm, d), _vec_spec(d)],
        out_specs=_row_spec(tm, d), out_shape=jax.ShapeDtypeStruct((s, d), BF16),
        compiler_params=_cp("parallel"),
    )(x, g)


def _norm_bwd(dh, x, dx_res, g_pre, name, prev=None):
    s, d = x.shape
    tm = min(TM, s)
    has_prev = prev is not None

    def body(*refs):
        if has_prev:
            dh_ref, x_ref, dr_ref, g_ref, y_ref, gp_ref, dx_ref, dg_ref, dy_ref, dgp_ref = refs
        else:
            dh_ref, x_ref, dr_ref, g_ref, dx_ref, dg_ref = refs
        i = pl.program_id(0)
        xv = x_ref[...]
        r = _rstd(xv)
        xn = xv * r
        dhv = dh_ref[...].astype(F32)
        dx = dr_ref[...] + _norm_bwd_rows(dhv * g_ref[...], xn, r)
        dx_ref[...] = dx
        dg = jnp.sum(dhv * xn, axis=0, keepdims=True)

        @pl.when(i == 0)
        def _():
            dg_ref[...] = dg

        @pl.when(i > 0)
        def _():
            dg_ref[...] += dg

        if has_prev:
            yv = y_ref[...]
            r2 = _rstd(yv)
            yn = yv * r2
            dy_ref[...] = _norm_bwd_rows(dx * gp_ref[...], yn, r2).astype(BF16)
            dgp = jnp.sum(dx * yn, axis=0, keepdims=True)

            @pl.when(i == 0)
            def _():
                dgp_ref[...] = dgp

            @pl.when(i > 0)
            def _():
                dgp_ref[...] += dgp

    in_specs = [_row_spec(tm, d), _row_spec(tm, d), _row_spec(tm, d), _vec_spec(d)]
    out_specs = [_row_spec(tm, d), _vec_spec(d)]
    out_shape = [jax.ShapeDtypeStruct((s, d), F32), jax.ShapeDtypeStruct((1, d), F32)]
    args = [dh, x, dx_res, g_pre]
    if has_prev:
        in_specs += [_row_spec(tm, d), _vec_spec(d)]
        out_specs += [_row_spec(tm, d), _vec_spec(d)]
        out_shape += [jax.ShapeDtypeStruct((s, d), BF16), jax.ShapeDtypeStruct((1, d), F32)]
        args += list(prev)
    return pl.pallas_call(
        body, name=name, grid=(s // tm,), in_specs=in_specs, out_specs=out_specs, out_shape=out_shape,
        compiler_params=_cp("arbitrary"),
    )(*args)


def _split3(v):
    hi = v.astype(BF16)
    r1 = v - hi.astype(F32)
    mid = r1.astype(BF16)
    lo = (r1 - mid.astype(F32)).astype(BF16)
    return hi, mid, lo


def _tri_dot(tri, v):
    hi, mid, lo = _split3(v)
    return _dot(tri, hi, NN) + _dot(tri, mid, NN) + _dot(tri, lo, NN)


def _gate_cumsum(fraw, b_pad, name):
    s = fraw.shape[0]
    width = HEAD_PAIRS * 128

    def body(f_ref, b_ref, flog_ref, aq_ref, ak_ref, carry_ref):
        i = pl.program_id(0)

        @pl.when(i == 0)
        def _():
            carry_ref[...] = jnp.zeros_like(carry_ref)

        flog = f_ref[...] + b_ref[...]
        flog_ref[...] = flog
        lf = jnp.minimum(flog, 0.0) - jnp.log(1.0 + jnp.exp(-jnp.abs(flog)))
        lane = lax.broadcasted_iota(jnp.int32, (1, 128), 1)
        lf = jnp.where(lane < FOX_HEADS, lf, 0.0)
        row = lax.broadcasted_iota(jnp.int32, (TM, TM), 0)
        col = lax.broadcasted_iota(jnp.int32, (TM, TM), 1)
        tri = (row >= col).astype(BF16)
        cum = _tri_dot(tri, lf) + carry_ref[...]
        carry_ref[...] = cum[TM - 1:TM, :]
        aq_ref[...], ak_ref[...] = _fox_operands(cum)

    return pl.pallas_call(
        body, name=name, grid=(s // TM,),
        in_specs=[_row_spec(TM, 128), _vec_spec(128)],
        out_specs=[_row_spec(TM, 128), _row_spec(TM, width), _row_spec(TM, width)],
        out_shape=[jax.ShapeDtypeStruct((s, 128), F32), jax.ShapeDtypeStruct((s, width), BF16),
                   jax.ShapeDtypeStruct((s, width), BF16)],
        scratch_shapes=[pltpu.VMEM((1, 128), F32)],
        compiler_params=_cp("arbitrary"),
    )(fraw, b_pad)


def _gate_bwd(qaux, kaux, flog, name):
    s = flog.shape[0]
    n = s // TM

    def body(qa_ref, ka_ref, fl_ref, dp_ref, db_ref, carry_ref):
        i = pl.program_id(0)

        @pl.when(i == 0)
        def _():
            carry_ref[...] = jnp.zeros_like(carry_ref)

        lane = lax.broadcasted_iota(jnp.int32, (1, 128), 1)
        moved = jnp.zeros((TM, 128), F32)
        for p in range(HEAD_PAIRS):
            d = qa_ref[p] - pltpu.roll(ka_ref[p], 128 - 3, 1)
            moved = moved + pltpu.roll(jnp.where((lane & 63) == 0, d, 0.0), 2 * p + 1, 1)
        odd = (lane & 1) == 1
        dcum = jnp.where(lane < FOX_HEADS, jnp.where(odd, moved, pltpu.roll(moved, 63, 1)), 0.0)
        row = lax.broadcasted_iota(jnp.int32, (TM, TM), 0)
        col = lax.broadcasted_iota(jnp.int32, (TM, TM), 1)
        tri = (row <= col).astype(BF16)
        dlf = _tri_dot(tri, dcum) + carry_ref[...]
        carry_ref[...] = dlf[0:1, :]
        df = jnp.where(lane < FOX_HEADS, dlf / (1.0 + jnp.exp(fl_ref[...])), 0.0)
        dp_ref[...] = df.astype(BF16)
        db = jnp.sum(df, axis=0, keepdims=True)

        @pl.when(i == 0)
        def _():
            db_ref[...] = db

        @pl.when(i > 0)
        def _():
            db_ref[...] += db

    rev = lambda i: (n - 1 - i, 0)
    return pl.pallas_call(
        body, name=name, grid=(n,),
        in_specs=[pl.BlockSpec((HEAD_PAIRS, TM, 128), lambda i: (0, n - 1 - i, 0)),
                  pl.BlockSpec((HEAD_PAIRS, TM, 128), lambda i: (0, n - 1 - i, 0)), pl.BlockSpec((TM, 128), rev)],
        out_specs=[pl.BlockSpec((TM, 128), rev), _vec_spec(128)],
        out_shape=[jax.ShapeDtypeStruct((s, 128), BF16), jax.ShapeDtypeStruct((1, 128), F32)],
        scratch_shapes=[pltpu.VMEM((1, 128), F32)],
        compiler_params=_cp("arbitrary"),
    )(qaux, kaux, flog)


def _pool_consts(i, rows):
    lane = lax.broadcasted_iota(jnp.int32, (rows, D_POOL), 1)
    t1 = lax.broadcasted_iota(jnp.int32, (rows, D_POOL), 0) + i * TM + 1
    win = jnp.where(lane < 64, 2, jnp.where(lane < 128, 4, jnp.where(lane < 192, 8, 16)))
    inv = 1.0 / jnp.minimum(t1, win).astype(F32)
    return lane, inv


def _by_group(lane, s2, s4, s8, s16):
    return jnp.where(lane < 64, s2, jnp.where(lane < 128, s4, jnp.where(lane < 192, s8, s16)))


def _pool_diff(i, u_ref, halo_ref):
    u = u_ref[...].astype(F32)
    halo = jnp.where(i > 0, halo_ref[...].astype(F32), 0.0)
    ext = jnp.concatenate([halo, u], axis=0)
    s2 = ext + pltpu.roll(ext, 1, 0)
    s4 = s2 + pltpu.roll(s2, 2, 0)
    s8 = s4 + pltpu.roll(s4, 4, 0)
    s16 = s8 + pltpu.roll(s8, 8, 0)
    lane, inv = _pool_consts(i, TM)
    sel = _by_group(lane, s2[POOL_HALO:], s4[POOL_HALO:], s8[POOL_HALO:], s16[POOL_HALO:])
    return sel * inv - u


def _pool_fwd(proj, wbd, scale, ycat, name):
    s = proj.shape[0]
    hb = TM // POOL_HALO

    def body(u_ref, halo_ref, w_ref, sc_ref, y_any, y_ref):
        del y_any
        i = pl.program_id(0)
        diff = _pool_diff(i, u_ref, halo_ref)
        mixed = _dot(diff.astype(BF16), w_ref[...], NN)
        y_ref[...] = (mixed * sc_ref[...]).astype(BF16)

    return pl.pallas_call(
        body, name=name, grid=(s // TM,),
        in_specs=[pl.BlockSpec((TM, D_POOL), lambda i: (i, 0)),
                  pl.BlockSpec((POOL_HALO, D_POOL), lambda i: (jnp.maximum(i * hb - 1, 0), 0)),
                  pl.BlockSpec((D_POOL, D_POOL), lambda i: (0, 0)), _vec_spec(D_POOL),
                  pl.BlockSpec(memory_space=pl.ANY)],
        out_specs=pl.BlockSpec((TM, D_POOL), lambda i: (i, 0)),
        out_shape=jax.ShapeDtypeStruct(ycat.shape, ycat.dtype),
        input_output_aliases={4: 0},
        compiler_params=_cp("parallel"),
    )(proj, proj, wbd, scale, ycat)


def _pool_bwd(proj, dycat, wbd, scale, name):
    s = proj.shape[0]
    n = s // TM
    hb = TM // POOL_HALO
    last_halo = s // POOL_HALO - 1

    def body(u_ref, halo_ref, dy_ref, dyp_ref, w_ref, sc_ref, dp_ref, dw_ref, dsc_ref):
        i = pl.program_id(0)
        diff = _pool_diff(i, u_ref, halo_ref)
        diff_b = diff.astype(BF16)
        mixed = _dot(diff_b, w_ref[...], NN)
        dy = dy_ref[...].astype(F32)
        dmix = (dy * sc_ref[...]).astype(BF16)
        dyp = jnp.where(i < n - 1, dyp_ref[...].astype(F32), 0.0)
        dmix_p = (dyp * sc_ref[...]).astype(BF16)
        dd = _dot(dmix, w_ref[...], NT)
        dd_p = _dot(dmix_p, w_ref[...], NT)
        lane, inv = _pool_consts(i, TM)
        _, inv_p = _pool_consts(i + 1, POOL_HALO)
        ext = jnp.concatenate([dd * inv, dd_p * inv_p], axis=0)
        rows = TM + POOL_HALO
        l2 = ext + pltpu.roll(ext, rows - 1, 0)
        l4 = l2 + pltpu.roll(l2, rows - 2, 0)
        l8 = l4 + pltpu.roll(l4, rows - 4, 0)
        l16 = l8 + pltpu.roll(l8, rows - 8, 0)
        du = _by_group(lane, l2[:TM], l4[:TM], l8[:TM], l16[:TM]) - dd
        dp_ref[...] = du.astype(BF16)
        dw = _dot(diff_b, dmix, TN)
        dsc = jnp.sum(dy * mixed, axis=0, keepdims=True)

        @pl.when(i == 0)
        def _():
            dw_ref[...] = dw
            dsc_ref[...] = dsc

        @pl.when(i > 0)
        def _():
            dw_ref[...] += dw
            dsc_ref[...] += dsc

    return pl.pallas_call(
        body, name=name, grid=(n,),
        in_specs=[pl.BlockSpec((TM, D_POOL), lambda i: (i, 0)),
                  pl.BlockSpec((POOL_HALO, D_POOL), lambda i: (jnp.maximum(i * hb - 1, 0), 0)),
                  pl.BlockSpec((TM, D_POOL), lambda i: (i, 0)),
                  pl.BlockSpec((POOL_HALO, D_POOL), lambda i: (jnp.minimum((i + 1) * hb, last_halo), 0)),
                  pl.BlockSpec((D_POOL, D_POOL), lambda i: (0, 0)), _vec_spec(D_POOL)],
        out_specs=[pl.BlockSpec((TM, D_POOL), lambda i: (i, 0)),
                   pl.BlockSpec((D_POOL, D_POOL), lambda i: (0, 0)), _vec_spec(D_POOL)],
        out_shape=[jax.ShapeDtypeStruct((s, D_POOL), BF16),
                   jax.ShapeDtypeStruct((D_POOL, D_POOL), F32), jax.ShapeDtypeStruct((1, D_POOL), F32)],
        compiler_params=_cp("arbitrary"),
    )(proj, proj, dycat, dycat, wbd, scale)


Q_BLK = D_POOL // 128
K_BLK = Q_BLK + D_FOX // 128
V_BLK = K_BLK + D_FOX // 128
STEP_PAIRS = 2
assert Q_BLK % STEP_PAIRS == 0 and K_BLK % STEP_PAIRS == 0 and V_BLK % STEP_PAIRS == 0 and HEAD_PAIRS % STEP_PAIRS == 0


def _operand_rows(v0, v1, ones_off):
    row = lax.broadcasted_iota(jnp.int32, (128, 1), 0)
    half = row & 63
    out = jnp.where(jnp.logical_and(half >= ones_off, half < ones_off + 3), 1.0, 0.0) + jnp.zeros_like(v0)
    for base, v in ((64, v0), (0, v1)):
        for j, piece in enumerate(_split3(v)):
            out = jnp.where(row == base + j, piece.astype(F32), out)
    return out


def _fox_operands(cum):
    width = HEAD_PAIRS * 128
    hi, mid, lo = _split3(cum)
    packed = (hi.astype(F32) + pltpu.roll(mid.astype(F32), 16, 1) + pltpu.roll(lo.astype(F32), 32, 1)).astype(BF16)
    row = lax.broadcasted_iota(jnp.int32, (128, width), 0)
    col = lax.broadcasted_iota(jnp.int32, (128, width), 1)
    head, j = row & 15, row >> 4
    base = (head >> 1) * 128 + (1 - (head & 1)) * 64
    used = jnp.logical_and(head < FOX_HEADS, j < 3)
    half = lax.broadcasted_iota(jnp.int32, (1, width), 1) & 63
    res = []
    for off, sign, ones_off in ((0, 1.0, 3), (3, -1.0, 0)):
        ones = jnp.where(jnp.logical_and(half >= ones_off, half < ones_off + 3), 1.0, 0.0)
        sel = jnp.where(jnp.logical_and(col == base + off + j, used), sign, 0.0).astype(BF16)
        res.append((ones + _dot(packed, sel, NN)).astype(BF16))
    return res


def _fox_do_operand(dycat, ycat, after, name):
    s = dycat.shape[0]

    rb = 1024
    nblk = D_FOX // D_POOL

    def body(*refs):
        do_refs, o_refs, ad_ref = refs[:nblk], refs[nblk:2 * nblk], refs[-1]
        src = lax.broadcasted_iota(jnp.int32, (D_POOL, D_POOL), 0)
        dst = lax.broadcasted_iota(jnp.int32, (D_POOL, D_POOL), 1)
        same_pair = (src >> 7) == (dst >> 7)
        s_in, d_in = src & 127, dst & 127
        hit = jnp.logical_and(same_pair, jnp.logical_or(
            jnp.logical_and(s_in < 64, jnp.logical_and(d_in >= 64, d_in < 67)), jnp.logical_and(s_in >= 64, d_in < 3)))
        sel = jnp.where(hit, 1.0, 0.0).astype(BF16)
        j = lax.broadcasted_iota(jnp.int32, (1, D_POOL), 1) & 63
        for b in range(nblk):
            dd = do_refs[b][...].astype(F32) * o_refs[b][...].astype(F32)
            dsum = jnp.zeros(dd.shape, F32)
            for piece in _split3(dd):
                dsum = dsum + _dot(piece, sel, NN)
            hi, mid, lo = _split3(-dsum)
            ad_ref[:, D_POOL * b:D_POOL * (b + 1)] = jnp.where(j == 0, hi, jnp.where(j == 1, mid, lo))

    blks = [pl.BlockSpec((rb, D_POOL), functools.partial(lambda i, b: (i, 1 + b), b=b)) for b in range(nblk)]
    return pl.pallas_call(
        body, name=name, grid=(s // rb,), in_specs=blks + blks + [pl.BlockSpec(memory_space=pl.ANY)],
        out_specs=pl.BlockSpec((rb, D_FOX), lambda i: (i, 0)),
        out_shape=jax.ShapeDtypeStruct((s, D_FOX), BF16),
        compiler_params=_cp("parallel"),
    )(*[dycat] * nblk, *[ycat] * nblk, after)


def _causal_pairs(nq, key_major):
    if key_major:
        pairs = [(q, k) for k in range(nq) for q in range(k, nq)]
    else:
        pairs = [(q, k) for q in range(nq) for k in range(q + 1)]
    return (jnp.asarray([p[0] for p in pairs], jnp.int32), jnp.asarray([p[1] for p in pairs], jnp.int32))


def _fox_fwd(proj, aq, ak, name):
    s = proj.shape[0]
    nq = s // TQ
    qi_arr, ki_arr = _causal_pairs(nq, key_major=False)

    wide = STEP_PAIRS * 128
    heads = [(pp, hh) for pp in range(STEP_PAIRS) for hh in range(2)]

    def body(qi_ref, ki_ref, q_ref, k_ref, v_ref, aq_ref, ak_ref, o_ref, aqb_ref, m_ref, acc_ref, aux_ref):
        t = pl.program_id(1)
        qi, ki = qi_ref[t], ki_ref[t]
        lane = lax.broadcasted_iota(jnp.int32, (1, 128), 1)
        masks = [lane < 64, lane >= 64]
        ones_v = jnp.where((lane & 63) == 8, 1.0, 0.0).astype(BF16)
        top = lax.broadcasted_iota(jnp.int32, (128, 1), 0) < 64

        @pl.when(ki == 0)
        def _():
            m_ref[...] = jnp.full_like(m_ref, NEG)
            acc_ref[...] = jnp.zeros_like(acc_ref)
            aux_ref[...] = jnp.zeros_like(aux_ref)

        def step(diag):
            q2s = q_ref[...] * 0.125
            k2, v2, aq2, ak2 = k_ref[...], v_ref[...], aq_ref[...], ak_ref[...]

            def operand(main, lanes, pp, hh):
                cols = slice(128 * pp, 128 * (pp + 1))
                return jnp.where(masks[hh], main[:, cols], lanes[:, cols])

            scs = [_dot(operand(k2, ak2, pp, hh), operand(q2s, aq2, pp, hh), NT) for pp, hh in heads]
            ps, alpha = [], []
            for n, sc in enumerate(scs):
                if diag:
                    key = lax.broadcasted_iota(jnp.int32, sc.shape, 0)
                    qry = lax.broadcasted_iota(jnp.int32, sc.shape, 1)
                    sc = jnp.where(qry >= key, sc, NEG)
                m_prev = m_ref[n]
                m_new = jnp.maximum(m_prev, jnp.max(sc, axis=0, keepdims=True))
                m_ref[n] = m_new
                alpha.append(jnp.exp(m_prev - m_new))
                ps.append(jnp.exp(sc - m_new).astype(BF16))
            ones2 = jnp.concatenate([ones_v] * STEP_PAIRS, axis=1)
            pv = [_dot(operand(v2, ones2, pp, hh), ps[n], TN) for n, (pp, hh) in enumerate(heads)]
            for pp in range(STEP_PAIRS):
                a0, a1, pv0, pv1 = alpha[2 * pp], alpha[2 * pp + 1], pv[2 * pp], pv[2 * pp + 1]
                acc_ref[pp] = acc_ref[pp] * jnp.where(top, a0, a1) + jnp.where(top, pv0, pv1)
                aux_ref[pp] = aux_ref[pp] * jnp.where(top, a1, a0) + jnp.where(top, pv1, pv0)

        @pl.when(ki < qi)
        def _():
            step(False)

        @pl.when(ki == qi)
        def _():
            step(True)
            for pp in range(STEP_PAIRS):
                cols = slice(128 * pp, 128 * (pp + 1))
                aux = aux_ref[pp]
                l0, l1 = aux[72:73, :], aux[8:9, :]
                o_ref[:, cols] = (acc_ref[pp] * jnp.where(top, 1.0 / l0, 1.0 / l1)).T.astype(BF16)
                aqt = aq_ref[:, cols].astype(F32).T
                cum0 = aqt[64:65, :] + aqt[65:66, :] + aqt[66:67, :]
                cum1 = aqt[0:1, :] + aqt[1:2, :] + aqt[2:3, :]
                aqb = _operand_rows(cum0 - (m_ref[2 * pp] + jnp.log(l0)), cum1 - (m_ref[2 * pp + 1] + jnp.log(l1)), 3)
                aqb_ref[:, cols] = aqb.T.astype(BF16)

    grid_spec = pltpu.PrefetchScalarGridSpec(
        num_scalar_prefetch=2, grid=(HEAD_PAIRS // STEP_PAIRS, int(qi_arr.shape[0])),
        in_specs=[pl.BlockSpec((TQ, wide), lambda p, t, qi, ki: (qi[t], Q_BLK // STEP_PAIRS + p)),
                  pl.BlockSpec((TQ, wide), lambda p, t, qi, ki: (ki[t], K_BLK // STEP_PAIRS + p)),
                  pl.BlockSpec((TQ, wide), lambda p, t, qi, ki: (ki[t], V_BLK // STEP_PAIRS + p)),
                  pl.BlockSpec((TQ, wide), lambda p, t, qi, ki: (qi[t], p)),
                  pl.BlockSpec((TQ, wide), lambda p, t, qi, ki: (ki[t], p))],
        out_specs=[pl.BlockSpec((TQ, wide), lambda p, t, qi, ki: (qi[t], Q_BLK // STEP_PAIRS + p)),
                   pl.BlockSpec((TQ, wide), lambda p, t, qi, ki: (qi[t], p))],
        scratch_shapes=[pltpu.VMEM((2 * STEP_PAIRS, 1, TQ), F32),
                        pltpu.VMEM((STEP_PAIRS, 128, TQ), F32), pltpu.VMEM((STEP_PAIRS, 128, TQ), F32)])
    return pl.pallas_call(
        body, name=name, grid_spec=grid_spec,
        out_shape=[jax.ShapeDtypeStruct((s, D_MODEL), BF16), jax.ShapeDtypeStruct((s, HEAD_PAIRS * 128), BF16)],
        compiler_params=_cp("parallel", "arbitrary"),
    )(qi_arr, ki_arr, proj, proj, proj, aq, ak)


def _fox_bwd(proj, dycat, aqb, ak, ad, name):
    s = proj.shape[0]
    nq = s // TQ
    qi_arr, ki_arr = _causal_pairs(nq, key_major=True)

    def body(qi_ref, ki_ref, q_ref, k_ref, v_ref, do_ref, aq_ref, ak_ref, ad_ref,
             dq_ref, dk_ref, dv_ref, qaux_ref, kaux_ref, dq_acc, qaux_acc, dk_acc, dv_acc, kaux_acc):
        t = pl.program_id(1)
        qi, ki = qi_ref[t], ki_ref[t]
        lane = lax.broadcasted_iota(jnp.int32, (1, 128), 1)
        masks = [lane < 64, lane >= 64]
        ones_v = jnp.where((lane & 63) < 3, 1.0, 0.0).astype(BF16)
        top = lax.broadcasted_iota(jnp.int32, (128, 1), 0) < 64

        @pl.when(qi == ki)
        def _():
            dk_acc[...] = jnp.zeros_like(dk_acc)
            dv_acc[...] = jnp.zeros_like(dv_acc)
            kaux_acc[...] = jnp.zeros_like(kaux_acc)

        @pl.when(ki == 0)
        def _():
            for pp in range(STEP_PAIRS):
                dq_acc[pp * nq + qi] = jnp.zeros((128, TQ), F32)
                qaux_acc[pp * nq + qi] = jnp.zeros((128, TQ), F32)

        def step(diag):
            q2s = q_ref[...] * 0.125
            k2, v2, do2 = k_ref[...], v_ref[...], do_ref[...]
            aq2, ak2, ad2 = aq_ref[...], ak_ref[...], ad_ref[...]
            ones2 = jnp.concatenate([ones_v] * STEP_PAIRS, axis=1)
            for pp in range(STEP_PAIRS):
                cols = slice(128 * pp, 128 * (pp + 1))
                dq, dk, dv = [], [], []
                for hh in range(2):
                    qh = jnp.where(masks[hh], q2s[:, cols], aq2[:, cols])
                    kh = jnp.where(masks[hh], k2[:, cols], ak2[:, cols])
                    doh = jnp.where(masks[hh], do2[:, cols], ad2[:, cols])
                    vh = jnp.where(masks[hh], v2[:, cols], ones2[:, cols])
                    sc = _dot(kh, qh, NT)
                    if diag:
                        key = lax.broadcasted_iota(jnp.int32, sc.shape, 0)
                        qry = lax.broadcasted_iota(jnp.int32, sc.shape, 1)
                        sc = jnp.where(qry >= key, sc, NEG)
                    p = jnp.exp(sc)
                    dsb = (p * _dot(vh, doh, NT)).astype(BF16)
                    dv.append(_dot(p.astype(BF16), doh, NN))
                    dk.append(_dot(dsb, qh, NN))
                    dq.append(_dot(kh, dsb, TN))
                dk_acc[pp] += jnp.where(masks[0], dk[0], dk[1])
                kaux_acc[pp] += jnp.where(masks[0], dk[1], dk[0])
                dv_acc[pp] += jnp.where(masks[0], dv[0], dv[1])
                dq_acc[pp * nq + qi] += jnp.where(top, dq[0], dq[1])
                qaux_acc[pp * nq + qi] += jnp.where(top, dq[1], dq[0])

        @pl.when(qi > ki)
        def _():
            step(False)

        @pl.when(qi == ki)
        def _():
            step(True)
            rows = pl.ds(pl.multiple_of(qi * TQ, TQ), TQ)
            for pp in range(STEP_PAIRS):
                dq_ref[rows, 128 * pp:128 * (pp + 1)] = (dq_acc[pp * nq + qi] * 0.125).T.astype(BF16)
                qaux_ref[pp, rows, :] = qaux_acc[pp * nq + qi].T

        @pl.when(qi == nq - 1)
        def _():
            for pp in range(STEP_PAIRS):
                dk_ref[:, 128 * pp:128 * (pp + 1)] = dk_acc[pp].astype(BF16)
                dv_ref[:, 128 * pp:128 * (pp + 1)] = dv_acc[pp].astype(BF16)
            kaux_ref[...] = kaux_acc[...]

    wide = STEP_PAIRS * 128
    grid_spec = pltpu.PrefetchScalarGridSpec(
        num_scalar_prefetch=2, grid=(HEAD_PAIRS // STEP_PAIRS, int(qi_arr.shape[0])),
        in_specs=[pl.BlockSpec((TQ, wide), lambda p, t, qi, ki: (qi[t], Q_BLK // STEP_PAIRS + p)),
                  pl.BlockSpec((TQ, wide), lambda p, t, qi, ki: (ki[t], K_BLK // STEP_PAIRS + p)),
                  pl.BlockSpec((TQ, wide), lambda p, t, qi, ki: (ki[t], V_BLK // STEP_PAIRS + p)),
                  pl.BlockSpec((TQ, wide), lambda p, t, qi, ki: (qi[t], Q_BLK // STEP_PAIRS + p)),
                  pl.BlockSpec((TQ, wide), lambda p, t, qi, ki: (qi[t], p)),
                  pl.BlockSpec((TQ, wide), lambda p, t, qi, ki: (ki[t], p)),
                  pl.BlockSpec((TQ, wide), lambda p, t, qi, ki: (qi[t], p))],
        out_specs=[pl.BlockSpec((s, wide), lambda p, t, qi, ki: (0, p)),
                   pl.BlockSpec((TQ, wide), lambda p, t, qi, ki: (ki[t], p)),
                   pl.BlockSpec((TQ, wide), lambda p, t, qi, ki: (ki[t], p)),
                   pl.BlockSpec((STEP_PAIRS, s, 128), lambda p, t, qi, ki: (p, 0, 0)),
                   pl.BlockSpec((STEP_PAIRS, TQ, 128), lambda p, t, qi, ki: (p, ki[t], 0))],
        scratch_shapes=[pltpu.VMEM((STEP_PAIRS * nq, 128, TQ), F32), pltpu.VMEM((STEP_PAIRS * nq, 128, TQ), F32),
                        pltpu.VMEM((STEP_PAIRS, TQ, 128), F32), pltpu.VMEM((STEP_PAIRS, TQ, 128), F32),
                        pltpu.VMEM((STEP_PAIRS, TQ, 128), F32)])
    return pl.pallas_call(
        body, name=name, grid_spec=grid_spec,
        out_shape=[jax.ShapeDtypeStruct((s, D_FOX), BF16)] * 3 + [jax.ShapeDtypeStruct((HEAD_PAIRS, s, 128), F32)] * 2,
        compiler_params=_cp("arbitrary", "arbitrary"),
    )(qi_arr, ki_arr, proj, proj, proj, dycat, aqb, ak, ad)


XA_SCALE = XA_DIM ** -0.5


def _xattn_fwd(q2, kv, name):
    s = q2.shape[0]
    m = kv.shape[0]

    def body(q_ref, kv_ref, o_ref):
        heads = [slice(h * XA_DIM, (h + 1) * XA_DIM) for h in range(XA_HEADS)]
        scs = [_dot(q_ref[:, cols], kv_ref[:, cols], NT) for cols in heads]
        for h in range(XA_HEADS):
            c0 = h * XA_DIM
            sc = scs[h] * XA_SCALE
            e = jnp.exp(sc - jnp.max(sc, axis=1, keepdims=True))
            p = e / jnp.sum(e, axis=1, keepdims=True)
            o_ref[:, c0:c0 + XA_DIM] = _dot(p.astype(BF16), kv_ref[:, D_MODEL + c0:D_MODEL + c0 + XA_DIM], NN).astype(BF16)

    return pl.pallas_call(
        body, name=name, grid=(s // TM,),
        in_specs=[_row_spec(TM, D_MODEL), pl.BlockSpec((m, 2 * D_MODEL), lambda i: (0, 0))],
        out_specs=_row_spec(TM, D_MODEL), out_shape=jax.ShapeDtypeStruct((s, D_MODEL), BF16),
        compiler_params=_cp("parallel"),
    )(q2, kv)


def _xattn_bwd(q2, kv, do2, name):
    s = q2.shape[0]
    m = kv.shape[0]

    def body(q_ref, kv_ref, do_ref, dq_ref, dkv_ref):
        i = pl.program_id(0)

        @pl.when(i == 0)
        def _():
            dkv_ref[...] = jnp.zeros_like(dkv_ref)

        heads = [slice(h * XA_DIM, (h + 1) * XA_DIM) for h in range(XA_HEADS)]
        scs = [_dot(kv_ref[:, cols], q_ref[:, cols], NT) for cols in heads]
        dps = [_dot(kv_ref[:, D_MODEL + cols.start:D_MODEL + cols.stop], do_ref[:, cols], NT) for cols in heads]
        for h in range(XA_HEADS):
            c0 = h * XA_DIM
            v0 = D_MODEL + c0
            qh = q_ref[:, c0:c0 + XA_DIM]
            kh = kv_ref[:, c0:c0 + XA_DIM]
            doh = do_ref[:, c0:c0 + XA_DIM]
            sc = scs[h] * XA_SCALE
            e = jnp.exp(sc - jnp.max(sc, axis=0, keepdims=True))
            p = e / jnp.sum(e, axis=0, keepdims=True)
            dp = dps[h]
            ds = p * (dp - jnp.sum(p * dp, axis=0, keepdims=True))
            dsb = (ds * XA_SCALE).astype(BF16)
            dq_ref[:, c0:c0 + XA_DIM] = _dot(kh, dsb, TN).T.astype(BF16)
            dkv_ref[:, c0:c0 + XA_DIM] += _dot(dsb, qh, NN)
            dkv_ref[:, v0:v0 + XA_DIM] += _dot(p.astype(BF16), doh, NN)

    return pl.pallas_call(
        body, name=name, grid=(s // TM,),
        in_specs=[_row_spec(TM, D_MODEL), pl.BlockSpec((m, 2 * D_MODEL), lambda i: (0, 0)), _row_spec(TM, D_MODEL)],
        out_specs=[_row_spec(TM, D_MODEL), pl.BlockSpec((m, 2 * D_MODEL), lambda i: (0, 0))],
        out_shape=[jax.ShapeDtypeStruct((s, D_MODEL), BF16), jax.ShapeDtypeStruct((m, 2 * D_MODEL), F32)],
        compiler_params=_cp("arbitrary"),
    )(q2, kv, do2)


GELU_C = math.sqrt(2.0 / math.pi)
GELU_A = 0.044715


def _gelu(x):
    return (0.5 * x) * (1.0 + jnp.tanh(x * (GELU_C * GELU_A * (x * x) + GELU_C)))


def _gelu_and_grad(x):
    x2 = x * x
    s = 1.0 + jnp.tanh(x * (GELU_C * GELU_A * x2 + GELU_C))
    hx = 0.5 * x
    return hx * s, s * (0.5 + hx * (2.0 - s) * (3.0 * GELU_C * GELU_A * x2 + GELU_C))


def _conv(h, s1, s2, w_ref, b_ref):
    return w_ref[0:1, :] * s2 + w_ref[1:2, :] * s1 + w_ref[2:3, :] * h + b_ref[...]


def _shift_down(main, prev8):
    row = lax.broadcasted_iota(jnp.int32, main.shape, 0)
    s1 = jnp.where(row == 0, prev8[7:8, :], pltpu.roll(main, 1, 0))
    s2 = jnp.where(row == 0, prev8[6:7, :], jnp.where(row == 1, prev8[7:8, :], pltpu.roll(main, 2, 0)))
    return s1, s2


def _shift_up(main, next8):
    n = main.shape[0]
    row = lax.broadcasted_iota(jnp.int32, main.shape, 0)
    u1 = jnp.where(row == n - 1, next8[0:1, :], pltpu.roll(main, n - 1, 0))
    u2 = jnp.where(row == n - 2, next8[0:1, :], jnp.where(row == n - 1, next8[1:2, :], pltpu.roll(main, n - 2, 0)))
    return u1, u2


def _ffn_fwd(h3, w_up, cw, cb, w_down, x2, tgt, g_post, name):
    s = h3.shape[0]
    tn = TN_FF
    nj = D_FF // tn
    per = D_MODEL // tn
    hb = TM // 8

    def body(h_ref, halo_ref, wg_ref, wu_ref, cwg_ref, cwu_ref, cbg_ref, cbu_ref, wd_ref, x_ref, t_ref, g_ref,
             hg_ref, hu_ref, cg_ref, cu_ref, a_ref, loss_ref, dx_ref, dy_ref, dg_ref, y_acc):
        i, j = pl.program_id(0), pl.program_id(1)

        @pl.when(j == 0)
        def _():
            y_acc[...] = jnp.zeros_like(y_acc)

        h = h_ref[...]
        halo = halo_ref[...]
        halo = jnp.where(i > 0, halo, jnp.zeros_like(halo))
        hid = [(_dot(h, w_ref[...], NN), _dot(halo, w_ref[...], NN)) for w_ref in (wg_ref, wu_ref)]
        conv = []
        for (hm, hm_halo), cw_ref, cb_ref, hid_ref, c_ref in zip(hid, (cwg_ref, cwu_ref), (cbg_ref, cbu_ref),
                                                                  (hg_ref, hu_ref), (cg_ref, cu_ref)):
            hid_ref[...] = hm.astype(BF16)
            s1, s2 = _shift_down(hm, hm_halo)
            c = _conv(hm, s1, s2, cw_ref, cb_ref)
            c_ref[...] = c.astype(BF16)
            conv.append(c)
        a = (_gelu(conv[0]) * conv[1]).astype(BF16)
        a_ref[...] = a
        y_acc[...] += _dot(a, wd_ref[...], NN)

        @pl.when(j == nj - 1)
        def _():
            yv = y_acc[...]
            r = _rstd(yv)
            yn = yv * r
            e = x_ref[...] + yn * g_ref[...] - t_ref[...]
            part = 0.5 * jnp.sum(jnp.mean(e * e, axis=-1, keepdims=True), axis=0, keepdims=True)
            part = jnp.broadcast_to(part, (1, 128))
            dx = e * (1.0 / D_MODEL)
            dx_ref[...] = dx
            dy_ref[...] = _norm_bwd_rows(dx * g_ref[...], yn, r).astype(BF16)
            dg = jnp.sum(dx * yn, axis=0, keepdims=True)

            @pl.when(i == 0)
            def _():
                dg_ref[...] = dg
                loss_ref[...] = part

            @pl.when(i > 0)
            def _():
                dg_ref[...] += dg
                loss_ref[...] += part

    rows = pl.BlockSpec((TM, D_MODEL), lambda i, j: (i, 0))
    tile = pl.BlockSpec((TM, tn), lambda i, j: (i, j))
    wide = jax.ShapeDtypeStruct((s, D_FF), BF16)
    return pl.pallas_call(
        body, name=name, grid=(s // TM, nj),
        in_specs=[rows,
                  pl.BlockSpec((8, D_MODEL), lambda i, j: (jnp.maximum(i * hb - 1, 0), 0)),
                  pl.BlockSpec((None, D_MODEL, tn), lambda i, j: (j // per, 0, j % per)),
                  pl.BlockSpec((None, D_MODEL, tn), lambda i, j: (NDEV // 2 + j // per, 0, j % per)),
                  pl.BlockSpec((8, tn), lambda i, j: (0, j)),
                  pl.BlockSpec((8, tn), lambda i, j: (0, nj + j)),
                  pl.BlockSpec((1, tn), lambda i, j: (0, j)),
                  pl.BlockSpec((1, tn), lambda i, j: (0, nj + j)),
                  pl.BlockSpec((tn, D_MODEL), lambda i, j: (j, 0)),
                  rows, rows, pl.BlockSpec((1, D_MODEL), lambda i, j: (0, 0))],
        out_specs=[tile, tile, tile, tile, tile,
                   pl.BlockSpec((1, 128), lambda i, j: (0, 0)), rows, rows,
                   pl.BlockSpec((1, D_MODEL), lambda i, j: (0, 0))],
        out_shape=[wide, wide, wide, wide, wide,
                   jax.ShapeDtypeStruct((1, 128), F32), jax.ShapeDtypeStruct((s, D_MODEL), F32),
                   jax.ShapeDtypeStruct((s, D_MODEL), BF16), jax.ShapeDtypeStruct((1, D_MODEL), F32)],
        scratch_shapes=[pltpu.VMEM((TM, D_MODEL), F32)],
        compiler_params=_cp("arbitrary", "arbitrary"),
    )(h3, h3, w_up, w_up, cw, cw, cb, cb, w_down, x2, tgt, g_post)


def _ffn_bwd(dy3, w_down, hid_g, hid_u, conv_g, conv_u, cw, name):
    s = dy3.shape[0]
    n = s // TM
    tn = TN_FF
    nj = D_FF // tn
    hb = TM // 8
    last8 = s // 8 - 1

    def body(dy_ref, dyn_ref, wd_ref, hg_ref, hu_ref, cg_ref, cgn_ref, cu_ref, cun_ref, cwg_ref, cwu_ref,
             dhg_ref, dhu_ref, dcwg_ref, dcwu_ref, dcbg_ref, dcbu_ref):
        i = pl.program_id(1)
        first, last = i == 0, i == n - 1
        @pl.when(first)
        def _():
            for ref in (dcwg_ref, dcwu_ref, dcbg_ref, dcbu_ref):
                ref[...] = jnp.zeros_like(ref)

        dyn = dyn_ref[...]
        dyn = jnp.where(last, jnp.zeros_like(dyn), dyn)
        wc = tn // 4
        chunks = [slice(c, c + wc) for c in range(0, tn, wc)]
        das = [(_dot(dy_ref[...], wd_ref[cols, :], NT), _dot(dyn, wd_ref[cols, :], NT)) for cols in chunks]
        row8 = lax.broadcasted_iota(jnp.int32, (8, wc), 0)
        for cols, (da, da_n) in zip(chunks, das):
            c_g, c_u = cg_ref[:, cols].astype(F32), cu_ref[:, cols].astype(F32)
            g, dg = _gelu_and_grad(c_g)
            gn, dgn = _gelu_and_grad(cgn_ref[:, cols].astype(F32))
            outs = ((da * c_u * dg, da_n * cun_ref[:, cols].astype(F32) * dgn, hg_ref, cwg_ref, dhg_ref, dcwg_ref, dcbg_ref),
                    (da * g, da_n * gn, hu_ref, cwu_ref, dhu_ref, dcwu_ref, dcbu_ref))
            for dc, dcn, h_ref, cw_ref, dh_ref, dcw_ref, dcb_ref in outs:
                u1, u2 = _shift_up(dc, dcn)
                dh_ref[:, cols] = (cw_ref[2:3, cols] * dc + cw_ref[1:2, cols] * u1 + cw_ref[0:1, cols] * u2).astype(BF16)
                hm = h_ref[:, cols].astype(F32)
                dcb = jnp.sum(dc, axis=0, keepdims=True)
                dcw = jnp.where(row8 == 0, jnp.sum(hm * u2, axis=0, keepdims=True),
                                jnp.where(row8 == 1, jnp.sum(hm * u1, axis=0, keepdims=True),
                                          jnp.where(row8 == 2, jnp.sum(hm * dc, axis=0, keepdims=True), 0.0)))
                dcw_ref[:, cols] += dcw
                dcb_ref[:, cols] += dcb

    next8 = lambda j, i: (jnp.minimum((i + 1) * hb, last8), j)
    blk = lambda j, i: (i, j)
    col = lambda j, i: (0, j)
    colu = lambda j, i: (0, nj + j)
    tile = pl.BlockSpec((TM, tn), blk)
    return pl.pallas_call(
        body, name=name, grid=(nj, n),
        in_specs=[pl.BlockSpec((TM, D_MODEL), lambda j, i: (i, 0)),
                  pl.BlockSpec((8, D_MODEL), lambda j, i: (jnp.minimum((i + 1) * hb, last8), 0)),
                  pl.BlockSpec((tn, D_MODEL), lambda j, i: (j, 0)),
                  tile, tile, tile, pl.BlockSpec((8, tn), next8), tile, pl.BlockSpec((8, tn), next8),
                  pl.BlockSpec((8, tn), col), pl.BlockSpec((8, tn), colu)],
        out_specs=[tile, tile, pl.BlockSpec((8, tn), col), pl.BlockSpec((8, tn), col),
                   pl.BlockSpec((1, tn), col), pl.BlockSpec((1, tn), col)],
        out_shape=[jax.ShapeDtypeStruct((s, D_FF), BF16), jax.ShapeDtypeStruct((s, D_FF), BF16),
                   jax.ShapeDtypeStruct((8, D_FF), F32), jax.ShapeDtypeStruct((8, D_FF), F32),
                   jax.ShapeDtypeStruct((1, D_FF), F32), jax.ShapeDtypeStruct((1, D_FF), F32)],
        compiler_params=_cp("parallel", "arbitrary"),
    )(dy3, dy3, w_down, hid_g, hid_u, conv_g, conv_g, conv_u, conv_u, cw, cw)


def _slot(p):
    return 4 * p[0] + 2 * p[1] + p[2]


def _all_gather(shards, name):
    n = len(shards)

    def body(*refs):
        ins, outs = refs[:n], refs[n:2 * n]
        send_sems, recv_sems, local_sems = refs[2 * n:]
        x, y, c = lax.axis_index("x"), lax.axis_index("y"), lax.axis_index("c")
        me, sibling = (x, y, c), (x, y, 1 - c)
        chips = [(1 - x, y), (x, 1 - y), (1 - x, 1 - y)]

        def copy(a, k, block, to, from_input=False):
            dst = outs[a].at[_slot(block)]
            return pltpu.make_async_remote_copy(
                src_ref=ins[a] if from_input else dst, dst_ref=dst,
                send_sem=send_sems.at[a, k], recv_sem=recv_sems.at[a, k],
                device_id=to, device_id_type=MESH)

        mine = [pltpu.make_async_copy(ins[a], outs[a].at[_slot(me)], local_sems.at[a]) for a in range(n)]
        for cp in mine:
            cp.start()
        first = []
        for a in range(n):
            first.append(copy(a, 0, me, sibling, True))
            first += [copy(a, 1 + j, me, (*chip, c), True) for j, chip in enumerate(chips)]
        for cp in first:
            cp.start()
        passed = []
        for j, chip in enumerate(chips):
            for a in range(n):
                copy(a, 1 + j, (*chip, c), me).wait_recv()
                fwd = copy(a, 4 + j, (*chip, c), sibling)
                fwd.start()
                passed.append(fwd)
        for a in range(n):
            copy(a, 0, sibling, me).wait_recv()
            for j, chip in enumerate(chips):
                copy(a, 4 + j, (*chip, 1 - c), me).wait_recv()
        for cp in first + passed:
            cp.wait_send()
        for cp in mine:
            cp.wait()

    any_spec = pl.BlockSpec(memory_space=pl.ANY)
    return pl.pallas_call(
        body, name=name,
        in_specs=[any_spec] * n, out_specs=[any_spec] * n,
        out_shape=[jax.ShapeDtypeStruct((NDEV,) + s.shape, s.dtype) for s in shards],
        scratch_shapes=[pltpu.SemaphoreType.DMA((n, 7)), pltpu.SemaphoreType.DMA((n, 7)),
                        pltpu.SemaphoreType.DMA((n,))],
    )(*shards)


def _peer_list(x, y, c):
    return [(1 - x if m & 4 else x, 1 - y if m & 2 else y, 1 - c if m & 1 else c) for m in range(1, NDEV)]


def _exchange_copies(src_refs, land_refs, send_sems, recv_sems, gather):
    x, y, c = lax.axis_index("x"), lax.axis_index("y"), lax.axis_index("c")
    me = (x, y, c)
    copies = []
    for m, peer in enumerate(_peer_list(x, y, c)):
        for a in range(len(src_refs)):
            copies.append(pltpu.make_async_remote_copy(
                src_ref=src_refs[a] if gather else src_refs[a].at[_slot(peer)], dst_ref=land_refs[a].at[_slot(me)],
                send_sem=send_sems.at[a * (NDEV - 1) + m], recv_sem=recv_sems.at[a * (NDEV - 1) + m],
                device_id=peer, device_id_type=MESH))
    return copies


def _all_gather_small(shards, name):
    n = len(shards)

    def body(*refs):
        ins, outs = refs[:n], refs[n:2 * n]
        send_sems, recv_sems, local_sems = refs[2 * n:]
        me = (lax.axis_index("x"), lax.axis_index("y"), lax.axis_index("c"))
        mine = [pltpu.make_async_copy(ins[a], outs[a].at[_slot(me)], local_sems.at[a]) for a in range(n)]
        copies = _exchange_copies(ins, outs, send_sems, recv_sems, True)
        for cp in mine + copies:
            cp.start()
        for cp in copies + mine:
            cp.wait()

    any_spec = pl.BlockSpec(memory_space=pl.ANY)
    return pl.pallas_call(
        body, name=name,
        in_specs=[any_spec] * n, out_specs=[any_spec] * n,
        out_shape=[jax.ShapeDtypeStruct((NDEV,) + s.shape, s.dtype) for s in shards],
        scratch_shapes=[pltpu.SemaphoreType.DMA((n * (NDEV - 1),)), pltpu.SemaphoreType.DMA((n * (NDEV - 1),)),
                        pltpu.SemaphoreType.DMA((n,))],
    )(*shards)


def _exchange_start(srcs, lands, after, gather, name):
    n = len(srcs)
    hbm = pl.BlockSpec(memory_space=pltpu.HBM)

    def body(*refs):
        for cp in _exchange_copies(refs[:n], refs[n:2 * n], refs[2 * n + 1], refs[2 * n + 2], gather):
            cp.start()
        token = refs[-1]
        token[...] = jnp.zeros_like(token)

    outs = pl.pallas_call(
        body, name=name,
        out_shape=(pltpu.SemaphoreType.DMA((n * (NDEV - 1),)), pltpu.SemaphoreType.DMA((n * (NDEV - 1),)),
                   *[pltpu.HBM(a.shape, a.dtype) for a in list(srcs) + list(lands)],
                   jax.ShapeDtypeStruct((8, 128), F32)),
        in_specs=[hbm] * (2 * n) + [pl.BlockSpec(memory_space=pl.ANY)],
        out_specs=(pl.BlockSpec(memory_space=pltpu.SEMAPHORE), pl.BlockSpec(memory_space=pltpu.SEMAPHORE),
                   *[hbm] * (2 * n), pl.BlockSpec(memory_space=pltpu.VMEM)),
        input_output_aliases={i: 2 + i for i in range(2 * n)},
        compiler_params=pltpu.CompilerParams(has_side_effects=pltpu.SideEffectType.DATAFLOW_SIDE_EFFECTING),
    )(*[pltpu.with_memory_space_constraint(a, pltpu.HBM) for a in list(srcs) + list(lands)], after)
    return outs[0], outs[1], outs[2:2 + n], outs[2 + n:2 + 2 * n], outs[-1]


def _exchange_wait(send_sems, recv_sems, srcs, lands, after, gather, name):
    n = len(srcs)
    hbm = pl.BlockSpec(memory_space=pltpu.HBM)

    def body(*refs):
        for cp in _exchange_copies(refs[:n], refs[n:2 * n], refs[2 * n], refs[2 * n + 1], gather):
            cp.wait_send()
            cp.wait_recv()

    outs = pl.pallas_call(
        body, name=name,
        out_shape=tuple(pltpu.HBM(a.shape, a.dtype) for a in list(srcs) + list(lands)),
        in_specs=[hbm] * (2 * n) + [pl.BlockSpec(memory_space=pltpu.SEMAPHORE)] * 2 + [pl.BlockSpec(memory_space=pl.ANY)],
        out_specs=tuple([hbm] * (2 * n)),
        input_output_aliases={i: i for i in range(2 * n)},
        compiler_params=pltpu.CompilerParams(has_side_effects=pltpu.SideEffectType.DATAFLOW_SIDE_EFFECTING),
    )(*srcs, *lands, send_sems, recv_sems, after)
    return outs[n:]


def _own_slot(block):
    me = 4 * lax.axis_index("x") + 2 * lax.axis_index("y") + lax.axis_index("c")
    return lax.dynamic_update_slice(lax.empty((NDEV,) + block.shape, block.dtype), block[None], (me, 0, 0))


def _adam_update(p_ref, w_ref, m_ref, v_ref, g_ref, d_ref, mo_ref, vo_ref):
    bc1 = 1.0 - ADAM_B1 ** ADAM_STEP
    bc2 = 1.0 - ADAM_B2 ** ADAM_STEP
    g = p_ref[0].astype(F32)
    for d in range(1, NDEV):
        g = g + p_ref[d].astype(F32)
    g_ref[...] = g
    mn = ADAM_B1 * m_ref[...] + (1.0 - ADAM_B1) * g
    vn = ADAM_B2 * v_ref[...] + (1.0 - ADAM_B2) * (g * g)
    mo_ref[...] = mn
    vo_ref[...] = vn
    d_ref[...] = -ADAM_LR * ((mn / bc1) / (jnp.sqrt(vn / bc2) + ADAM_EPS) + ADAM_WD * w_ref[...])


def _adamw_small(parts, ws, ms, vs, name):
    n = len(ws)

    def body(*refs):
        ins, outs = refs[:4 * n], refs[4 * n:]
        for k in range(n):
            _adam_update(ins[k], ins[n + k], ins[2 * n + k], ins[3 * n + k], *outs[4 * k:4 * k + 4])

    whole = pl.BlockSpec(memory_space=pltpu.VMEM)
    res = pl.pallas_call(
        body, name=name, in_specs=[whole] * (4 * n), out_specs=[whole] * (4 * n),
        out_shape=[jax.ShapeDtypeStruct(a.shape, F32) for a in ws for _ in range(4)],
    )(*parts, *ws, *ms, *vs)
    return [res[4 * k:4 * k + 4] for k in range(n)]


def _adamw(parts, w, m, v, name):
    r, c = w.shape
    tr = r if r * c <= 160 * 1024 else max(8, (160 * 1024 // c) // 8 * 8)
    while r % tr:
        tr -= 8
    body = functools.partial(_adam_update)
    spec = pl.BlockSpec((tr, c), lambda i: (i, 0))
    return pl.pallas_call(
        body, name=name, grid=(r // tr,),
        in_specs=[pl.BlockSpec((NDEV, tr, c), lambda i: (0, i, 0)), spec, spec, spec],
        out_specs=[spec] * 4, out_shape=[jax.ShapeDtypeStruct((r, c), F32)] * 4,
        compiler_params=_cp("parallel"),
    )(parts, w, m, v)


def _local_step(x, mem, tgt, gains, b_forget, w_pool, pool_scale, conv_b, w_in,
                mix_weights, ffn_weights, send_in_grad, send_mix_grads, send_ffn_grads):
    b_pad = jnp.pad(b_forget, ((0, 0), (0, 128 - FOX_HEADS)))
    wbd = jnp.zeros((D_POOL, D_POOL), F32)
    for g in range(4):
        wbd = wbd.at[64 * g:64 * g + 64, 64 * g:64 * g + 64].set(w_pool[g])
    wbd = wbd.astype(BF16)
    scale = pool_scale.reshape(1, D_POOL)

    h1, proj, fraw = _proj_in(x, gains["mix_pre"], w_in, "proj_in")
    flog, aq, ak = _gate_cumsum(fraw, b_pad, "gate_cumsum")
    ycat, aqb = _fox_fwd(proj, aq, ak, "fox_fwd")
    ycat = _pool_fwd(proj, wbd, scale, ycat, "pool_fwd")
    w_mix, w_xq, w_xo, w_xkv = mix_weights(ycat)
    y1, x1, h2 = _mm_rows(ycat, w_mix, "nn", 1024, "mix_out", [x], [gains["mix_post"], gains["xa_pre"]],
                          [F32, F32, BF16], _epi_resid)
    q2 = _mm(h2, w_xq, "nn", BF16, 2048, 1024, 1024, "xa_q")
    mem_n = _norm_fwd(mem, gains["mem"], "norm_mem")
    kv = _mm(mem_n, w_xkv, "nn", BF16, mem.shape[0], 256, 1024, "xa_kv", b_cols=256)
    o2 = _xattn_fwd(q2, kv, "xattn_fwd")
    y2, x2, h3 = _mm_rows(o2, w_xo, "nn", 1024, "xa_out", [x1], [gains["xa_post"], gains["ffn_pre"]],
                          [F32, F32, BF16], _epi_resid)
    w_up, w_down, cw = ffn_weights(h3)
    hid_g, hid_u, conv_g, conv_u, act, loss, dx3, dy3, dg_ffn_post = _ffn_fwd(
        h3, w_up, cw, conv_b, w_down, x2, tgt, gains["ffn_post"], "ffn_fwd")

    dhid_g, dhid_u, dcw_g, dcw_u, dcb_g, dcb_u = _ffn_bwd(dy3, w_down, hid_g, hid_u, conv_g, conv_u, cw, "ffn_bwd")
    d_w_down = _mm(act, dy3, "tn", BF16, 2048, 1024, 1024, "dw_down")
    d_w_up = _mm(h3, [dhid_g, dhid_u], "tn", BF16, 1024, 1024, 2048, "dw_up", out_cols=1024)
    sent = send_ffn_grads(d_w_up, d_w_down, jnp.concatenate([dcw_g, dcw_u], axis=1))
    dh3 = _mm([dhid_g, dhid_u], w_up, "nt", F32, 2048, 1024, 1024, "dh_ffn", b_cols=1024, after=sent)
    dx2, dg_ffn_pre, dy2, dg_xa_post = _norm_bwd(dh3, x2, dx3, gains["ffn_pre"], "norm_bwd_ffn",
                                                 prev=(y2, gains["xa_post"]))
    do2 = _mm(dy2, w_xo, "nt", BF16, 2048, 1024, 1024, "d_xa_out")
    d_w_xo = _mm(o2, dy2, "tn", BF16, 1024, 1024, 1024, "dw_xo")
    dq2, dkv = _xattn_bwd(q2, kv, do2, "xattn_bwd")
    dkv = dkv.astype(BF16)
    dx1, dg_xa_pre, dy1, dg_mix_post = _mm_rows(
        dq2, w_xq, "nt", 1024, "dh_xa", [x1, dx2, y1], [gains["xa_pre"], gains["mix_post"]],
        [F32, "sum", BF16, "sum"], _epi_norm_bwd)
    d_w_xq = _mm(h2, dq2, "tn", BF16, 1024, 1024, 1024, "dw_xq")
    dmem_n = _mm(dkv, w_xkv, "nt", F32, mem.shape[0], 1024, 256, "d_mem", b_cols=256)
    d_w_xkv = _mm(mem_n, dkv, "tn", BF16, 1024, 256, mem.shape[0], "dw_xkv", out_cols=256)
    _, dg_mem = _norm_bwd(dmem_n, mem, jnp.zeros_like(mem), gains["mem"], "norm_bwd_mem")
    dycat = _mm(dy1, w_mix, "nt", BF16, 2048, 1024, 1024, "d_mix_out")
    d_w_mix = _mm(ycat, dy1, "tn", BF16, 1024, 1024, 1024, "dw_mix")
    sent_mix = send_mix_grads(d_w_mix, d_w_xq, d_w_xo, d_w_xkv)
    ad = _fox_do_operand(dycat, ycat, sent_mix, "fox_do_operand")
    dq, dk, dv, qaux, kaux = _fox_bwd(proj, dycat, aqb, ak, ad, "fox_bwd")
    du, d_wbd, d_scale = _pool_bwd(proj, dycat, wbd, scale, "pool_bwd")
    df, db_f = _gate_bwd(qaux, kaux, flog, "gate_bwd")
    dproj = [du, dq, dk, dv, df]
    sent_in = send_in_grad(_dw_in(h1, dproj, "dw_in"))
    grad_x, dg_mix_pre = _mm_rows(dproj, w_in, "nt", None, "dh_mix", [x, dx1], [gains["mix_pre"]],
                                  [F32, "sum"], _epi_norm_bwd, after=sent_in)

    small = dict(
        mix_pre=dg_mix_pre, mix_post=dg_mix_post, mem=dg_mem, xa_pre=dg_xa_pre, xa_post=dg_xa_post,
        ffn_pre=dg_ffn_pre, ffn_post=dg_ffn_post,
        conv_b=jnp.concatenate([dcb_g, dcb_u], axis=1),
        w_pool=jnp.concatenate([d_wbd[64 * g:64 * g + 64, 64 * g:64 * g + 64] for g in range(4)], axis=0),
        pool_scale=d_scale.reshape(4, 64),
        b_forget=db_f[:, :FOX_HEADS],
    )
    return loss, grad_x, small


SMALL_ORDER = ("mix_pre", "mix_post", "mem", "xa_pre", "xa_post", "ffn_pre", "ffn_post", "conv_b",
               "w_pool", "pool_scale", "b_forget")


def kernel(x, mem, norm_mix_pre, norm_mix_post, w_in, b_forget, w_pool, pool_scale, w_mix_out, norm_mem, norm_xa_pre, norm_xa_post, w_xq, w_xkv, w_xo, norm_ffn_pre, norm_ffn_post, w_up, conv_w, conv_b, w_down, loss_target, m_norm_mix_pre, m_norm_mix_post, m_w_in, m_b_forget, m_w_pool, m_pool_scale, m_w_mix_out, m_norm_mem, m_norm_xa_pre, m_norm_xa_post, m_w_xq, m_w_xkv, m_w_xo, m_norm_ffn_pre, m_norm_ffn_post, m_w_up, m_conv_w, m_conv_b, m_w_down, v_norm_mix_pre, v_norm_mix_post, v_w_in, v_b_forget, v_w_pool, v_pool_scale, v_w_mix_out, v_norm_mem, v_norm_xa_pre, v_norm_xa_post, v_w_xq, v_w_xkv, v_w_xo, v_norm_ffn_pre, v_norm_ffn_post, v_w_up, v_conv_w, v_conv_b, v_w_down):
    names = ("norm_mix_pre", "norm_mix_post", "w_in", "b_forget", "w_pool", "pool_scale", "w_mix_out", "norm_mem",
             "norm_xa_pre", "norm_xa_post", "w_xq", "w_xkv", "w_xo", "norm_ffn_pre", "norm_ffn_post", "w_up",
             "conv_w", "conv_b", "w_down")
    w = dict(zip(names, (norm_mix_pre, norm_mix_post, w_in, b_forget, w_pool, pool_scale, w_mix_out, norm_mem,
                         norm_xa_pre, norm_xa_post, w_xq, w_xkv, w_xo, norm_ffn_pre, norm_ffn_post, w_up,
                         conv_w, conv_b, w_down)))
    mo = dict(zip(names, (m_norm_mix_pre, m_norm_mix_post, m_w_in, m_b_forget, m_w_pool, m_pool_scale, m_w_mix_out,
                          m_norm_mem, m_norm_xa_pre, m_norm_xa_post, m_w_xq, m_w_xkv, m_w_xo, m_norm_ffn_pre,
                          m_norm_ffn_post, m_w_up, m_conv_w, m_conv_b, m_w_down)))
    vo = dict(zip(names, (v_norm_mix_pre, v_norm_mix_post, v_w_in, v_b_forget, v_w_pool, v_pool_scale, v_w_mix_out,
                          v_norm_mem, v_norm_xa_pre, v_norm_xa_post, v_w_xq, v_w_xkv, v_w_xo, v_norm_ffn_pre,
                          v_norm_ffn_post, v_w_up, v_conv_w, v_conv_b, v_w_down)))

    big_names = ("w_in", "w_mix_out", "w_xq", "w_xo", "w_xkv", "w_up", "w_down")
    shards = {k: w[k][0].astype(BF16) for k in big_names}
    shards["w_in"] = jnp.pad(shards["w_in"], ((0, 0), (0, D_IN_PAD - shards["w_in"].shape[1])))
    conv_w_sh = jnp.pad(conv_w[0, :, 0, :], ((0, 5), (0, 0)))
    (g_in,) = _all_gather([shards["w_in"]], "gather_w_in")
    mix_srcs = [shards[k] for k in ("w_mix_out", "w_xq", "w_xo", "w_xkv")]
    mix_flight = _exchange_start(mix_srcs, [_own_slot(a) for a in mix_srcs], g_in, True, "gather_mix_start")
    ffn_srcs = [shards["w_up"], shards["w_down"], conv_w_sh]
    ffn_flight = _exchange_start(ffn_srcs, [_own_slot(a) for a in ffn_srcs], mix_flight[4], True, "gather_ffn_start")
    my_slot = 4 * lax.axis_index("x") + 2 * lax.axis_index("y") + lax.axis_index("c")
    own_block = lambda a: _own_slot(lax.dynamic_index_in_dim(a, my_slot, 0, keepdims=False))
    by_rows = lambda a: a.reshape(NDEV, a.shape[0] // NDEV, a.shape[1])
    by_cols = lambda a: a.reshape(a.shape[0], NDEV, a.shape[1] // NDEV).transpose(1, 0, 2)
    grad_flight = {}

    def mix_weights(after):
        g_mix, g_xq, g_xo, g_xkv = _exchange_wait(*mix_flight[:4], after, True, "gather_mix_wait")
        return (g_mix.reshape(D_MODEL, D_MODEL), g_xq.reshape(D_MODEL, D_MODEL), g_xo.reshape(D_MODEL, D_MODEL), g_xkv)

    def ffn_weights(after):
        g_up, g_down, g_cw = _exchange_wait(*ffn_flight[:4], after, True, "gather_ffn_wait")
        return g_up, g_down.reshape(D_FF, D_MODEL), g_cw.transpose(1, 0, 2).reshape(8, 2 * D_FF)

    def send_ffn_grads(d_w_up, d_w_down, d_cw):
        srcs = [d_w_up, by_rows(d_w_down), by_cols(d_cw)]
        grad_flight["ffn"] = _exchange_start(srcs, [own_block(a) for a in srcs], ffn_flight[4], False, "scatter_ffn_start")
        return grad_flight["ffn"][4]

    def send_mix_grads(d_w_mix, d_w_xq, d_w_xo, d_w_xkv):
        srcs = [by_rows(d_w_mix), by_rows(d_w_xq), by_rows(d_w_xo), d_w_xkv]
        grad_flight["mix"] = _exchange_start(srcs, [own_block(a) for a in srcs], ffn_flight[4], False, "scatter_mix_start")
        return grad_flight["mix"][4]

    def send_in_grad(d_w_in):
        srcs = [by_rows(d_w_in)]
        grad_flight["in"] = _exchange_start(srcs, [own_block(a) for a in srcs], ffn_flight[4], False, "scatter_in_start")
        return grad_flight["in"][4]

    gains = dict(mix_pre=norm_mix_pre + ffn_flight[4][0, 0], mix_post=norm_mix_post, mem=norm_mem, xa_pre=norm_xa_pre,
                 xa_post=norm_xa_post, ffn_pre=norm_ffn_pre, ffn_post=norm_ffn_post)
    loss, grad_x, small = _local_step(
        x[0], mem[0], loss_target[0], gains, b_forget, w_pool[0], pool_scale[0], conv_b,
        g_in.reshape(D_MODEL, D_IN_PAD), mix_weights, ffn_weights, send_in_grad, send_mix_grads, send_ffn_grads)

    p_up, p_down, p_cw = _exchange_wait(*grad_flight["ffn"][:4], grad_x, False, "scatter_ffn_wait")
    p_mix, p_xq, p_xo, p_xkv = _exchange_wait(*grad_flight["mix"][:4], grad_x, False, "scatter_mix_wait")
    parts = dict(w_mix_out=p_mix, w_xq=p_xq, w_xo=p_xo, w_xkv=p_xkv, w_up=p_up, w_down=p_down)
    *small_parts, loss_parts = _all_gather_small([small[k] for k in SMALL_ORDER] + [loss], "gather_small_grads")

    res = {k: [a[None] for a in _adamw(p, w[k][0], mo[k][0], vo[k][0], "adamw_" + k)] for k, p in parts.items()}
    pad_cw = lambda a: jnp.pad(a[0, :, 0, :], ((0, 5), (0, 0)))
    res["conv_w"] = [a[:3][None, :, None, :] for a in
                     _adamw(p_cw, pad_cw(conv_w), pad_cw(m_conv_w), pad_cw(v_conv_w), "adamw_conv_w")]
    key_of = dict(mix_pre="norm_mix_pre", mix_post="norm_mix_post", mem="norm_mem", xa_pre="norm_xa_pre",
                  xa_post="norm_xa_post", ffn_pre="norm_ffn_pre", ffn_post="norm_ffn_post", conv_b="conv_b",
                  w_pool="w_pool", pool_scale="pool_scale", b_forget="b_forget")
    flat2d = lambda src: [src[key_of[k]].reshape(small[k].shape) for k in SMALL_ORDER]
    small_out = _adamw_small(small_parts, flat2d(w), flat2d(mo), flat2d(vo), "adamw_small")
    for k, four in zip(SMALL_ORDER, small_out):
        res[key_of[k]] = [a.reshape(w[key_of[k]].shape) for a in four]
    (p_in,) = _exchange_wait(*grad_flight["in"][:4], res["w_up"][1], False, "scatter_in_wait")
    res["w_in"] = [a[None] for a in _adamw(p_in[:, :, :w_in.shape[2]], w["w_in"][0], mo["w_in"][0], vo["w_in"][0],
                                           "adamw_w_in")]

    outs = [jnp.sum(loss_parts[:, 0, 0]), grad_x[None]]
    for idx in range(4):
        outs += [res[k][idx] for k in names]
    return tuple(outs)
```

```python
import functools
import math

import jax
import jax.numpy as jnp
from jax import lax
from jax.experimental import pallas as pl
from jax.experimental.pallas import tpu as pltpu

F32 = jnp.float32
BF16 = jnp.bfloat16

NDEV = 8
D_MODEL = 1024
D_POOL = 256
D_FOX = 768
FOX_HEADS = 12
HEAD_PAIRS = FOX_HEADS // 2
XA_HEADS = 4
XA_DIM = 256
D_FF = 4096
D_IN_PAD = 2688
F_COL = 2560
POOL_HALO = 16
NORM_EPS = 1e-6
NEG = -1e30

ADAM_LR = 0.001
ADAM_B1 = 0.9
ADAM_B2 = 0.999
ADAM_EPS = 1e-08
ADAM_WD = 0.01
ADAM_STEP = 10

TM = 512
TQ = 512
TN_FF = 1024
VMEM_LIMIT = 56 * 1024 * 1024
MESH = pl.DeviceIdType.MESH


def _cp(*sem):
    return pltpu.CompilerParams(dimension_semantics=sem, vmem_limit_bytes=VMEM_LIMIT)


def _dot(a, b, dims):
    return lax.dot_general(a, b, (dims, ((), ())), preferred_element_type=F32)


NN = ((1,), (0,))
NT = ((1,), (1,))
TN = ((0,), (0,))


def _mm(a, b, mode, out_dtype, tm, tn, tk, name, b_cols=None, out_cols=None, after=None):
    a_list = list(a) if isinstance(a, (list, tuple)) else [a]
    b_list = list(b) if isinstance(b, (list, tuple)) else [b]
    assert len(a_list) == 1 or len(b_list) == 1
    if mode == "tn":
        K, M = a_list[0].shape
        assert len(a_list) == 1
        Ns = [x.shape[1] for x in b_list]
        N = sum(Ns)
        assert b_cols is None
    else:
        assert len(b_list) == 1
        M = a_list[0].shape[0]
        Ks = [x.shape[1] for x in a_list]
        K = sum(Ks)
        if b_cols is None:
            N = b_list[0].shape[0] if mode == "nt" else b_list[0].shape[1]
        else:
            N = b_list[0].shape[1] if mode == "nt" else NDEV * b_cols
    assert M % tm == 0 and N % tn == 0 and K % tk == 0, (name, M, N, K)
    grid = (M // tm, N // tn, K // tk)
    nk = grid[2]
    dims = {"nn": NN, "nt": NT, "tn": TN}[mode]

    in_specs = []
    if mode == "tn":
        in_specs.append(pl.BlockSpec((tk, tm), lambda i, j, k: (k, i)))
        if len(b_list) == 1:
            in_specs.append(pl.BlockSpec((tk, tn), lambda i, j, k: (k, j)))
        else:
            nj1 = Ns[0] // tn
            in_specs.append(pl.BlockSpec((tk, tn), lambda i, j, k: (k, jnp.minimum(j, nj1 - 1))))
            in_specs.append(pl.BlockSpec((tk, tn), lambda i, j, k: (k, jnp.maximum(j - nj1, 0))))
    else:
        if len(a_list) == 1:
            in_specs.append(pl.BlockSpec((tm, tk), lambda i, j, k: (i, k)))
        else:
            nk1 = Ks[0] // tk
            in_specs.append(pl.BlockSpec((tm, tk), lambda i, j, k: (i, jnp.minimum(k, nk1 - 1))))
            in_specs.append(pl.BlockSpec((tm, tk), lambda i, j, k: (i, jnp.maximum(k - nk1, 0))))
        if b_cols is None:
            if mode == "nn":
                in_specs.append(pl.BlockSpec((tk, tn), lambda i, j, k: (k, j)))
            else:
                in_specs.append(pl.BlockSpec((tn, tk), lambda i, j, k: (j, k)))
        else:
            if mode == "nn":
                per = b_cols // tn
                in_specs.append(pl.BlockSpec((None, tk, tn), lambda i, j, k: (j // per, k, j % per)))
            else:
                per = b_cols // tk
                in_specs.append(pl.BlockSpec((None, tn, tk), lambda i, j, k: (k // per, j, k % per)))
    if out_cols is None:
        out_spec = pl.BlockSpec((tm, tn), lambda i, j, k: (i, j))
        out_shape = jax.ShapeDtypeStruct((M, N), out_dtype)
    else:
        pero = out_cols // tn
        out_spec = pl.BlockSpec((None, tm, tn), lambda i, j, k: (j // pero, i, j % pero))
        out_shape = jax.ShapeDtypeStruct((NDEV, M, out_cols), out_dtype)

    two_a = len(a_list) == 2
    two_b = len(b_list) == 2
    extra = []
    if after is not None:
        in_specs.append(pl.BlockSpec(memory_space=pl.ANY))
        extra.append(after)

    def body(*refs):
        o_ref, acc_ref = refs[-2], refs[-1]
        j = pl.program_id(1)
        k = pl.program_id(2)

        @pl.when(k == 0)
        def _():
            acc_ref[...] = jnp.zeros_like(acc_ref)

        if two_a:
            a1, a2, b1 = refs[0], refs[1], refs[2]
            nk1_ = Ks[0] // tk

            @pl.when(k < nk1_)
            def _():
                acc_ref[...] += _dot(a1[...], b1[...], dims)

            @pl.when(k >= nk1_)
            def _():
                acc_ref[...] += _dot(a2[...], b1[...], dims)
        elif two_b:
            a1, b1, b2 = refs[0], refs[1], refs[2]
            nj1_ = Ns[0] // tn

            @pl.when(j < nj1_)
            def _():
                acc_ref[...] += _dot(a1[...], b1[...], dims)

            @pl.when(j >= nj1_)
            def _():
                acc_ref[...] += _dot(a1[...], b2[...], dims)
        else:
            acc_ref[...] += _dot(refs[0][...], refs[1][...], dims)

        @pl.when(k == nk - 1)
        def _():
            o_ref[...] = acc_ref[...].astype(o_ref.dtype)

    return pl.pallas_call(
        body, name=name, grid=grid, in_specs=in_specs, out_specs=out_spec, out_shape=out_shape,
        scratch_shapes=[pltpu.VMEM((tm, tn), F32)],
        compiler_params=_cp("parallel", "parallel", "arbitrary"),
    )(*a_list, *b_list, *extra)


def _rstd(x):
    return lax.rsqrt(jnp.mean(x * x, axis=-1, keepdims=True) + NORM_EPS)


def _norm_bwd_rows(dxn, xn, r):
    return r * (dxn - xn * jnp.mean(dxn * xn, axis=-1, keepdims=True))


def _row_spec(tm, d):
    return pl.BlockSpec((tm, d), lambda i: (i, 0))


def _vec_spec(d):
    return pl.BlockSpec((1, d), lambda i: (0, 0))


def _mm_rows(a, b, mode, tk, name, rows, vecs, outs, epilogue, b_cols=None, after=None):
    a_list = list(a) if isinstance(a, (list, tuple)) else [a]
    m = a_list[0].shape[0]
    ks = [x.shape[1] for x in a_list]
    n = D_MODEL
    pieces = tk is None
    nk = 1 if pieces else sum(ks) // tk
    dims = NN if mode == "nn" else NT
    if pieces:
        assert mode == "nt" and b_cols is None
        in_specs = [pl.BlockSpec((TM, kp), lambda i, k: (i, 0)) for kp in ks]
        tk = sum(ks)
    elif len(a_list) == 1:
        in_specs = [pl.BlockSpec((TM, tk), lambda i, k: (i, k))]
    else:
        nk1 = ks[0] // tk
        in_specs = [pl.BlockSpec((TM, tk), lambda i, k: (i, jnp.minimum(k, nk1 - 1))),
                    pl.BlockSpec((TM, tk), lambda i, k: (i, jnp.maximum(k - nk1, 0)))]
    if mode == "nn":
        in_specs.append(pl.BlockSpec((tk, n), lambda i, k: (k, 0)))
    elif b_cols is None:
        in_specs.append(pl.BlockSpec((n, tk), lambda i, k: (0, k)))
    else:
        per = b_cols // tk
        in_specs.append(pl.BlockSpec((None, n, tk), lambda i, k: (k // per, 0, k % per)))
    in_specs += [pl.BlockSpec((TM, n), lambda i, k: (i, 0))] * len(rows)
    in_specs += [pl.BlockSpec((1, n), lambda i, k: (0, 0))] * len(vecs)
    extra = []
    if after is not None:
        in_specs.append(pl.BlockSpec(memory_space=pl.ANY))
        extra.append(after)
    out_specs, out_shape = [], []
    for o in outs:
        if o == "sum":
            out_specs.append(pl.BlockSpec((1, n), lambda i, k: (0, 0)))
            out_shape.append(jax.ShapeDtypeStruct((1, n), F32))
        else:
            out_specs.append(pl.BlockSpec((TM, n), lambda i, k: (i, 0)))
            out_shape.append(jax.ShapeDtypeStruct((m, n), o))
    na, nr, nv = len(a_list), len(rows), len(vecs)

    def body(*refs):
        a_refs, b_ref = refs[:na], refs[na]
        row_refs = refs[na + 1:na + 1 + nr]
        vec_refs = refs[na + 1 + nr:na + 1 + nr + nv]
        out_refs = refs[len(refs) - 1 - len(outs):len(refs) - 1]
        acc_ref = refs[-1]
        i, k = pl.program_id(0), pl.program_id(1)

        @pl.when(k == 0)
        def _():
            acc_ref[...] = jnp.zeros_like(acc_ref)

        if pieces:
            off = 0
            for a_ref in a_refs:
                kp = a_ref.shape[1]
                acc_ref[...] += _dot(a_ref[...], b_ref[:, off:off + kp], dims)
                off += kp
        elif na == 1:
            acc_ref[...] += _dot(a_refs[0][...], b_ref[...], dims)
        else:
            nk1_ = ks[0] // tk

            @pl.when(k < nk1_)
            def _():
                acc_ref[...] += _dot(a_refs[0][...], b_ref[...], dims)

            @pl.when(k >= nk1_)
            def _():
                acc_ref[...] += _dot(a_refs[1][...], b_ref[...], dims)

        @pl.when(k == nk - 1)
        def _():
            vals = epilogue(acc_ref[...], [r[...] for r in row_refs], [v[...] for v in vec_refs])
            for o, ref, val in zip(outs, out_refs, vals):
                if o == "sum":
                    @pl.when(i == 0)
                    def _():
                        ref[...] = val

                    @pl.when(i > 0)
                    def _():
                        ref[...] += val
                else:
                    ref[...] = val.astype(o)

    return pl.pallas_call(
        body, name=name, grid=(m // TM, nk), in_specs=in_specs, out_specs=out_specs, out_shape=out_shape,
        scratch_shapes=[pltpu.VMEM((TM, n), F32)],
        compiler_params=_cp("arbitrary", "arbitrary"),
    )(*a_list, b, *rows, *vecs, *extra)


def _proj_in(x, g, w_in, name):
    s, d = x.shape
    n = w_in.shape[1]

    def body(x_ref, g_ref, w_ref, h_ref, p_ref, f_ref):
        xv = x_ref[...]
        h = (xv * _rstd(xv) * g_ref[...]).astype(BF16)
        h_ref[...] = h
        acc = _dot(h, w_ref[...], NN)
        p_ref[...] = acc.astype(BF16)
        f_ref[...] = acc[:, F_COL:]

    return pl.pallas_call(
        body, name=name, grid=(s // TM,),
        in_specs=[_row_spec(TM, d), _vec_spec(d), pl.BlockSpec((d, n), lambda i: (0, 0))],
        out_specs=[_row_spec(TM, d), _row_spec(TM, n), _row_spec(TM, n - F_COL)],
        out_shape=[jax.ShapeDtypeStruct((s, d), BF16), jax.ShapeDtypeStruct((s, n), BF16),
                   jax.ShapeDtypeStruct((s, n - F_COL), F32)],
        compiler_params=_cp("parallel"),
    )(x, g, w_in)


def _dw_in(h, pieces, name):
    s, d = h.shape
    n = sum(p.shape[1] for p in pieces)
    tk = 1024
    nk = s // tk

    def body(*refs):
        h_ref, piece_refs, o_ref, acc_ref = refs[0], refs[1:-2], refs[-2], refs[-1]
        k = pl.program_id(1)

        @pl.when(k == 0)
        def _():
            acc_ref[...] = jnp.zeros_like(acc_ref)

        off = 0
        for p_ref in piece_refs:
            w = p_ref.shape[1]
            acc_ref[:, off:off + w] += _dot(h_ref[...], p_ref[...], TN)
            off += w

        @pl.when(k == nk - 1)
        def _():
            o_ref[...] = acc_ref[...].astype(BF16)

    return pl.pallas_call(
        body, name=name, grid=(d // TM, nk),
        in_specs=[pl.BlockSpec((tk, TM), lambda i, k: (k, i))] +
                 [pl.BlockSpec((tk, p.shape[1]), lambda i, k: (k, 0)) for p in pieces],
        out_specs=pl.BlockSpec((TM, n), lambda i, k: (i, 0)),
        out_shape=jax.ShapeDtypeStruct((d, n), BF16),
        scratch_shapes=[pltpu.VMEM((TM, n), F32)],
        compiler_params=_cp("parallel", "arbitrary"),
    )(h, *pieces)


def _epi_resid(y, rows, vecs):
    (x_in,), (g_post, g_next) = rows, vecs
    xo = x_in + y * _rstd(y) * g_post
    return y, xo, xo * _rstd(xo) * g_next


def _epi_norm_bwd(dh, rows, vecs):
    x, dx_res = rows[0], rows[1]
    r = _rstd(x)
    xn = x * r
    dx = dx_res + _norm_bwd_rows(dh * vecs[0], xn, r)
    res = [dx, jnp.sum(dh * xn, axis=0, keepdims=True)]
    if len(rows) == 3:
        y = rows[2]
        r2 = _rstd(y)
        yn = y * r2
        res += [_norm_bwd_rows(dx * vecs[1], yn, r2), jnp.sum(dx * yn, axis=0, keepdims=True)]
    return res


def _norm_fwd(x, g, name):
    s, d = x.shape
    tm = min(TM, s)

    def body(x_ref, g_ref, h_ref):
        xv = x_ref[...]
        h_ref[...] = (xv * _rstd(xv) * g_ref[...]).astype(BF16)

    return pl.pallas_call(
        body, name=name, grid=(s // tm,), in_specs=[_row_spec(tm, d), _vec_spec(d)],
        out_specs=_row_spec(tm, d), out_shape=jax.ShapeDtypeStruct((s, d), BF16),
        compiler_params=_cp("parallel"),
    )(x, g)


def _norm_bwd(dh, x, dx_res, g_pre, name, prev=None):
    s, d = x.shape
    tm = min(TM, s)
    has_prev = prev is not None

    def body(*refs):
        if has_prev:
            dh_ref, x_ref, dr_ref, g_ref, y_ref, gp_ref, dx_ref, dg_ref, dy_ref, dgp_ref = refs
        else:
            dh_ref, x_ref, dr_ref, g_ref, dx_ref, dg_ref = refs
        i = pl.program_id(0)
        xv = x_ref[...]
        r = _rstd(xv)
        xn = xv * r
        dhv = dh_ref[...].astype(F32)
        dx = dr_ref[...] + _norm_bwd_rows(dhv * g_ref[...], xn, r)
        dx_ref[...] = dx
        dg = jnp.sum(dhv * xn, axis=0, keepdims=True)

        @pl.when(i == 0)
        def _():
            dg_ref[...] = dg

        @pl.when(i > 0)
        def _():
            dg_ref[...] += dg

        if has_prev:
            yv = y_ref[...]
            r2 = _rstd(yv)
            yn = yv * r2
            dy_ref[...] = _norm_bwd_rows(dx * gp_ref[...], yn, r2).astype(BF16)
            dgp = jnp.sum(dx * yn, axis=0, keepdims=True)

            @pl.when(i == 0)
            def _():
                dgp_ref[...] = dgp

            @pl.when(i > 0)
            def _():
                dgp_ref[...] += dgp

    in_specs = [_row_spec(tm, d), _row_spec(tm, d), _row_spec(tm, d), _vec_spec(d)]
    out_specs = [_row_spec(tm, d), _vec_spec(d)]
    out_shape = [jax.ShapeDtypeStruct((s, d), F32), jax.ShapeDtypeStruct((1, d), F32)]
    args = [dh, x, dx_res, g_pre]
    if has_prev:
        in_specs += [_row_spec(tm, d), _vec_spec(d)]
        out_specs += [_row_spec(tm, d), _vec_spec(d)]
        out_shape += [jax.ShapeDtypeStruct((s, d), BF16), jax.ShapeDtypeStruct((1, d), F32)]
        args += list(prev)
    return pl.pallas_call(
        body, name=name, grid=(s // tm,), in_specs=in_specs, out_specs=out_specs, out_shape=out_shape,
        compiler_params=_cp("arbitrary"),
    )(*args)


def _split3(v):
    hi = v.astype(BF16)
    r1 = v - hi.astype(F32)
    mid = r1.astype(BF16)
    lo = (r1 - mid.astype(F32)).astype(BF16)
    return hi, mid, lo


def _tri_dot(tri, v):
    hi, mid, lo = _split3(v)
    return _dot(tri, hi, NN) + _dot(tri, mid, NN) + _dot(tri, lo, NN)


def _gate_cumsum(fraw, b_pad, name):
    s = fraw.shape[0]
    width = HEAD_PAIRS * 128

    def body(f_ref, b_ref, flog_ref, aq_ref, ak_ref, carry_ref):
        i = pl.program_id(0)

        @pl.when(i == 0)
        def _():
            carry_ref[...] = jnp.zeros_like(carry_ref)

        flog = f_ref[...] + b_ref[...]
        flog_ref[...] = flog
        lf = jnp.minimum(flog, 0.0) - jnp.log(1.0 + jnp.exp(-jnp.abs(flog)))
        lane = lax.broadcasted_iota(jnp.int32, (1, 128), 1)
        lf = jnp.where(lane < FOX_HEADS, lf, 0.0)
        row = lax.broadcasted_iota(jnp.int32, (TM, TM), 0)
        col = lax.broadcasted_iota(jnp.int32, (TM, TM), 1)
        tri = (row >= col).astype(BF16)
        cum = _tri_dot(tri, lf) + carry_ref[...]
        carry_ref[...] = cum[TM - 1:TM, :]
        aq_ref[...], ak_ref[...] = _fox_operands(cum)

    return pl.pallas_call(
        body, name=name, grid=(s // TM,),
        in_specs=[_row_spec(TM, 128), _vec_spec(128)],
        out_specs=[_row_spec(TM, 128), _row_spec(TM, width), _row_spec(TM, width)],
        out_shape=[jax.ShapeDtypeStruct((s, 128), F32), jax.ShapeDtypeStruct((s, width), BF16),
                   jax.ShapeDtypeStruct((s, width), BF16)],
        scratch_shapes=[pltpu.VMEM((1, 128), F32)],
        compiler_params=_cp("arbitrary"),
    )(fraw, b_pad)


def _gate_bwd(qaux, kaux, flog, name):
    s = flog.shape[0]
    n = s // TM

    def body(qa_ref, ka_ref, fl_ref, dp_ref, db_ref, carry_ref):
        i = pl.program_id(0)

        @pl.when(i == 0)
        def _():
            carry_ref[...] = jnp.zeros_like(carry_ref)

        lane = lax.broadcasted_iota(jnp.int32, (1, 128), 1)
        moved = jnp.zeros((TM, 128), F32)
        for p in range(HEAD_PAIRS):
            d = qa_ref[p] - pltpu.roll(ka_ref[p], 128 - 3, 1)
            moved = moved + pltpu.roll(jnp.where((lane & 63) == 0, d, 0.0), 2 * p + 1, 1)
        odd = (lane & 1) == 1
        dcum = jnp.where(lane < FOX_HEADS, jnp.where(odd, moved, pltpu.roll(moved, 63, 1)), 0.0)
        row = lax.broadcasted_iota(jnp.int32, (TM, TM), 0)
        col = lax.broadcasted_iota(jnp.int32, (TM, TM), 1)
        tri = (row <= col).astype(BF16)
        dlf = _tri_dot(tri, dcum) + carry_ref[...]
        carry_ref[...] = dlf[0:1, :]
        df = jnp.where(lane < FOX_HEADS, dlf / (1.0 + jnp.exp(fl_ref[...])), 0.0)
        dp_ref[...] = df.astype(BF16)
        db = jnp.sum(df, axis=0, keepdims=True)

        @pl.when(i == 0)
        def _():
            db_ref[...] = db

        @pl.when(i > 0)
        def _():
            db_ref[...] += db

    rev = lambda i: (n - 1 - i, 0)
    return pl.pallas_call(
        body, name=name, grid=(n,),
        in_specs=[pl.BlockSpec((HEAD_PAIRS, TM, 128), lambda i: (0, n - 1 - i, 0)),
                  pl.BlockSpec((HEAD_PAIRS, TM, 128), lambda i: (0, n - 1 - i, 0)), pl.BlockSpec((TM, 128), rev)],
        out_specs=[pl.BlockSpec((TM, 128), rev), _vec_spec(128)],
        out_shape=[jax.ShapeDtypeStruct((s, 128), BF16), jax.ShapeDtypeStruct((1, 128), F32)],
        scratch_shapes=[pltpu.VMEM((1, 128), F32)],
        compiler_params=_cp("arbitrary"),
    )(qaux, kaux, flog)


def _pool_consts(i, rows):
    lane = lax.broadcasted_iota(jnp.int32, (rows, D_POOL), 1)
    t1 = lax.broadcasted_iota(jnp.int32, (rows, D_POOL), 0) + i * TM + 1
    win = jnp.where(lane < 64, 2, jnp.where(lane < 128, 4, jnp.where(lane < 192, 8, 16)))
    inv = 1.0 / jnp.minimum(t1, win).astype(F32)
    return lane, inv


def _by_group(lane, s2, s4, s8, s16):
    return jnp.where(lane < 64, s2, jnp.where(lane < 128, s4, jnp.where(lane < 192, s8, s16)))


def _pool_diff(i, u_ref, halo_ref):
    u = u_ref[...].astype(F32)
    halo = jnp.where(i > 0, halo_ref[...].astype(F32), 0.0)
    ext = jnp.concatenate([halo, u], axis=0)
    s2 = ext + pltpu.roll(ext, 1, 0)
    s4 = s2 + pltpu.roll(s2, 2, 0)
    s8 = s4 + pltpu.roll(s4, 4, 0)
    s16 = s8 + pltpu.roll(s8, 8, 0)
    lane, inv = _pool_consts(i, TM)
    sel = _by_group(lane, s2[POOL_HALO:], s4[POOL_HALO:], s8[POOL_HALO:], s16[POOL_HALO:])
    return sel * inv - u


def _pool_fwd(proj, wbd, scale, ycat, name):
    s = proj.shape[0]
    hb = TM // POOL_HALO

    def body(u_ref, halo_ref, w_ref, sc_ref, y_any, y_ref):
        del y_any
        i = pl.program_id(0)
        diff = _pool_diff(i, u_ref, halo_ref)
        mixed = _dot(diff.astype(BF16), w_ref[...], NN)
        y_ref[...] = (mixed * sc_ref[...]).astype(BF16)

    return pl.pallas_call(
        body, name=name, grid=(s // TM,),
        in_specs=[pl.BlockSpec((TM, D_POOL), lambda i: (i, 0)),
                  pl.BlockSpec((POOL_HALO, D_POOL), lambda i: (jnp.maximum(i * hb - 1, 0), 0)),
                  pl.BlockSpec((D_POOL, D_POOL), lambda i: (0, 0)), _vec_spec(D_POOL),
                  pl.BlockSpec(memory_space=pl.ANY)],
        out_specs=pl.BlockSpec((TM, D_POOL), lambda i: (i, 0)),
        out_shape=jax.ShapeDtypeStruct(ycat.shape, ycat.dtype),
        input_output_aliases={4: 0},
        compiler_params=_cp("parallel"),
    )(proj, proj, wbd, scale, ycat)


def _pool_bwd(proj, dycat, wbd, scale, name):
    s = proj.shape[0]
    n = s // TM
    hb = TM // POOL_HALO
    last_halo = s // POOL_HALO - 1

    def body(u_ref, halo_ref, dy_ref, dyp_ref, w_ref, sc_ref, dp_ref, dw_ref, dsc_ref):
        i = pl.program_id(0)
        diff = _pool_diff(i, u_ref, halo_ref)
        diff_b = diff.astype(BF16)
        mixed = _dot(diff_b, w_ref[...], NN)
        dy = dy_ref[...].astype(F32)
        dmix = (dy * sc_ref[...]).astype(BF16)
        dyp = jnp.where(i < n - 1, dyp_ref[...].astype(F32), 0.0)
        dmix_p = (dyp * sc_ref[...]).astype(BF16)
        dd = _dot(dmix, w_ref[...], NT)
        dd_p = _dot(dmix_p, w_ref[...], NT)
        lane, inv = _pool_consts(i, TM)
        _, inv_p = _pool_consts(i + 1, POOL_HALO)
        ext = jnp.concatenate([dd * inv, dd_p * inv_p], axis=0)
        rows = TM + POOL_HALO
        l2 = ext + pltpu.roll(ext, rows - 1, 0)
        l4 = l2 + pltpu.roll(l2, rows - 2, 0)
        l8 = l4 + pltpu.roll(l4, rows - 4, 0)
        l16 = l8 + pltpu.roll(l8, rows - 8, 0)
        du = _by_group(lane, l2[:TM], l4[:TM], l8[:TM], l16[:TM]) - dd
        dp_ref[...] = du.astype(BF16)
        dw = _dot(diff_b, dmix, TN)
        dsc = jnp.sum(dy * mixed, axis=0, keepdims=True)

        @pl.when(i == 0)
        def _():
            dw_ref[...] = dw
            dsc_ref[...] = dsc

        @pl.when(i > 0)
        def _():
            dw_ref[...] += dw
            dsc_ref[...] += dsc

    return pl.pallas_call(
        body, name=name, grid=(n,),
        in_specs=[pl.BlockSpec((TM, D_POOL), lambda i: (i, 0)),
                  pl.BlockSpec((POOL_HALO, D_POOL), lambda i: (jnp.maximum(i * hb - 1, 0), 0)),
                  pl.BlockSpec((TM, D_POOL), lambda i: (i, 0)),
                  pl.BlockSpec((POOL_HALO, D_POOL), lambda i: (jnp.minimum((i + 1) * hb, last_halo), 0)),
                  pl.BlockSpec((D_POOL, D_POOL), lambda i: (0, 0)), _vec_spec(D_POOL)],
        out_specs=[pl.BlockSpec((TM, D_POOL), lambda i: (i, 0)),
                   pl.BlockSpec((D_POOL, D_POOL), lambda i: (0, 0)), _vec_spec(D_POOL)],
        out_shape=[jax.ShapeDtypeStruct((s, D_POOL), BF16),
                   jax.ShapeDtypeStruct((D_POOL, D_POOL), F32), jax.ShapeDtypeStruct((1, D_POOL), F32)],
        compiler_params=_cp("arbitrary"),
    )(proj, proj, dycat, dycat, wbd, scale)


Q_BLK = D_POOL // 128
K_BLK = Q_BLK + D_FOX // 128
V_BLK = K_BLK + D_FOX // 128
STEP_PAIRS = 2
assert Q_BLK % STEP_PAIRS == 0 and K_BLK % STEP_PAIRS == 0 and V_BLK % STEP_PAIRS == 0 and HEAD_PAIRS % STEP_PAIRS == 0


def _operand_rows(v0, v1, ones_off):
    row = lax.broadcasted_iota(jnp.int32, (128, 1), 0)
    half = row & 63
    out = jnp.where(jnp.logical_and(half >= ones_off, half < ones_off + 3), 1.0, 0.0) + jnp.zeros_like(v0)
    for base, v in ((64, v0), (0, v1)):
        for j, piece in enumerate(_split3(v)):
            out = jnp.where(row == base + j, piece.astype(F32), out)
    return out


def _fox_operands(cum):
    width = HEAD_PAIRS * 128
    hi, mid, lo = _split3(cum)
    packed = (hi.astype(F32) + pltpu.roll(mid.astype(F32), 16, 1) + pltpu.roll(lo.astype(F32), 32, 1)).astype(BF16)
    row = lax.broadcasted_iota(jnp.int32, (128, width), 0)
    col = lax.broadcasted_iota(jnp.int32, (128, width), 1)
    head, j = row & 15, row >> 4
    base = (head >> 1) * 128 + (1 - (head & 1)) * 64
    used = jnp.logical_and(head < FOX_HEADS, j < 3)
    half = lax.broadcasted_iota(jnp.int32, (1, width), 1) & 63
    res = []
    for off, sign, ones_off in ((0, 1.0, 3), (3, -1.0, 0)):
        ones = jnp.where(jnp.logical_and(half >= ones_off, half < ones_off + 3), 1.0, 0.0)
        sel = jnp.where(jnp.logical_and(col == base + off + j, used), sign, 0.0).astype(BF16)
        res.append((ones + _dot(packed, sel, NN)).astype(BF16))
    return res


def _fox_do_operand(dycat, ycat, after, name):
    s = dycat.shape[0]

    rb = 1024
    nblk = D_FOX // D_POOL

    def body(*refs):
        do_refs, o_refs, ad_ref = refs[:nblk], refs[nblk:2 * nblk], refs[-1]
        src = lax.broadcasted_iota(jnp.int32, (D_POOL, D_POOL), 0)
        dst = lax.broadcasted_iota(jnp.int32, (D_POOL, D_POOL), 1)
        same_pair = (src >> 7) == (dst >> 7)
        s_in, d_in = src & 127, dst & 127
        hit = jnp.logical_and(same_pair, jnp.logical_or(
            jnp.logical_and(s_in < 64, jnp.logical_and(d_in >= 64, d_in < 67)), jnp.logical_and(s_in >= 64, d_in < 3)))
        sel = jnp.where(hit, 1.0, 0.0).astype(BF16)
        j = lax.broadcasted_iota(jnp.int32, (1, D_POOL), 1) & 63
        for b in range(nblk):
            dd = do_refs[b][...].astype(F32) * o_refs[b][...].astype(F32)
            dsum = jnp.zeros(dd.shape, F32)
            for piece in _split3(dd):
                dsum = dsum + _dot(piece, sel, NN)
            hi, mid, lo = _split3(-dsum)
            ad_ref[:, D_POOL * b:D_POOL * (b + 1)] = jnp.where(j == 0, hi, jnp.where(j == 1, mid, lo))

    blks = [pl.BlockSpec((rb, D_POOL), functools.partial(lambda i, b: (i, 1 + b), b=b)) for b in range(nblk)]
    return pl.pallas_call(
        body, name=name, grid=(s // rb,), in_specs=blks + blks + [pl.BlockSpec(memory_space=pl.ANY)],
        out_specs=pl.BlockSpec((rb, D_FOX), lambda i: (i, 0)),
        out_shape=jax.ShapeDtypeStruct((s, D_FOX), BF16),
        compiler_params=_cp("parallel"),
    )(*[dycat] * nblk, *[ycat] * nblk, after)


def _causal_pairs(nq, key_major):
    if key_major:
        pairs = [(q, k) for k in range(nq) for q in range(k, nq)]
    else:
        pairs = [(q, k) for q in range(nq) for k in range(q + 1)]
    return (jnp.asarray([p[0] for p in pairs], jnp.int32), jnp.asarray([p[1] for p in pairs], jnp.int32))


def _fox_fwd(proj, aq, ak, name):
    s = proj.shape[0]
    nq = s // TQ
    qi_arr, ki_arr = _causal_pairs(nq, key_major=False)

    wide = STEP_PAIRS * 128
    heads = [(pp, hh) for pp in range(STEP_PAIRS) for hh in range(2)]

    def body(qi_ref, ki_ref, q_ref, k_ref, v_ref, aq_ref, ak_ref, o_ref, aqb_ref, m_ref, acc_ref, aux_ref):
        t = pl.program_id(1)
        qi, ki = qi_ref[t], ki_ref[t]
        lane = lax.broadcasted_iota(jnp.int32, (1, 128), 1)
        masks = [lane < 64, lane >= 64]
        ones_v = jnp.where((lane & 63) == 8, 1.0, 0.0).astype(BF16)
        top = lax.broadcasted_iota(jnp.int32, (128, 1), 0) < 64

        @pl.when(ki == 0)
        def _():
            m_ref[...] = jnp.full_like(m_ref, NEG)
            acc_ref[...] = jnp.zeros_like(acc_ref)
            aux_ref[...] = jnp.zeros_like(aux_ref)

        def step(diag):
            q2s = q_ref[...] * 0.125
            k2, v2, aq2, ak2 = k_ref[...], v_ref[...], aq_ref[...], ak_ref[...]

            def operand(main, lanes, pp, hh):
                cols = slice(128 * pp, 128 * (pp + 1))
                return jnp.where(masks[hh], main[:, cols], lanes[:, cols])

            scs = [_dot(operand(k2, ak2, pp, hh), operand(q2s, aq2, pp, hh), NT) for pp, hh in heads]
            ps, alpha = [], []
            for n, sc in enumerate(scs):
                if diag:
                    key = lax.broadcasted_iota(jnp.int32, sc.shape, 0)
                    qry = lax.broadcasted_iota(jnp.int32, sc.shape, 1)
                    sc = jnp.where(qry >= key, sc, NEG)
                m_prev = m_ref[n]
                m_new = jnp.maximum(m_prev, jnp.max(sc, axis=0, keepdims=True))
                m_ref[n] = m_new
                alpha.append(jnp.exp(m_prev - m_new))
                ps.append(jnp.exp(sc - m_new).astype(BF16))
            ones2 = jnp.concatenate([ones_v] * STEP_PAIRS, axis=1)
            pv = [_dot(operand(v2, ones2, pp, hh), ps[n], TN) for n, (pp, hh) in enumerate(heads)]
            for pp in range(STEP_PAIRS):
                a0, a1, pv0, pv1 = alpha[2 * pp], alpha[2 * pp + 1], pv[2 * pp], pv[2 * pp + 1]
                acc_ref[pp] = acc_ref[pp] * jnp.where(top, a0, a1) + jnp.where(top, pv0, pv1)
                aux_ref[pp] = aux_ref[pp] * jnp.where(top, a1, a0) + jnp.where(top, pv1, pv0)

        @pl.when(ki < qi)
        def _():
            step(False)

        @pl.when(ki == qi)
        def _():
            step(True)
            for pp in range(STEP_PAIRS):
                cols = slice(128 * pp, 128 * (pp + 1))
                aux = aux_ref[pp]
                l0, l1 = aux[72:73, :], aux[8:9, :]
                o_ref[:, cols] = (acc_ref[pp] * jnp.where(top, 1.0 / l0, 1.0 / l1)).T.astype(BF16)
                aqt = aq_ref[:, cols].astype(F32).T
                cum0 = aqt[64:65, :] + aqt[65:66, :] + aqt[66:67, :]
                cum1 = aqt[0:1, :] + aqt[1:2, :] + aqt[2:3, :]
                aqb = _operand_rows(cum0 - (m_ref[2 * pp] + jnp.log(l0)), cum1 - (m_ref[2 * pp + 1] + jnp.log(l1)), 3)
                aqb_ref[:, cols] = aqb.T.astype(BF16)

    grid_spec = pltpu.PrefetchScalarGridSpec(
        num_scalar_prefetch=2, grid=(HEAD_PAIRS // STEP_PAIRS, int(qi_arr.shape[0])),
        in_specs=[pl.BlockSpec((TQ, wide), lambda p, t, qi, ki: (qi[t], Q_BLK // STEP_PAIRS + p)),
                  pl.BlockSpec((TQ, wide), lambda p, t, qi, ki: (ki[t], K_BLK // STEP_PAIRS + p)),
                  pl.BlockSpec((TQ, wide), lambda p, t, qi, ki: (ki[t], V_BLK // STEP_PAIRS + p)),
                  pl.BlockSpec((TQ, wide), lambda p, t, qi, ki: (qi[t], p)),
                  pl.BlockSpec((TQ, wide), lambda p, t, qi, ki: (ki[t], p))],
        out_specs=[pl.BlockSpec((TQ, wide), lambda p, t, qi, ki: (qi[t], Q_BLK // STEP_PAIRS + p)),
                   pl.BlockSpec((TQ, wide), lambda p, t, qi, ki: (qi[t], p))],
        scratch_shapes=[pltpu.VMEM((2 * STEP_PAIRS, 1, TQ), F32),
                        pltpu.VMEM((STEP_PAIRS, 128, TQ), F32), pltpu.VMEM((STEP_PAIRS, 128, TQ), F32)])
    return pl.pallas_call(
        body, name=name, grid_spec=grid_spec,
        out_shape=[jax.ShapeDtypeStruct((s, D_MODEL), BF16), jax.ShapeDtypeStruct((s, HEAD_PAIRS * 128), BF16)],
        compiler_params=_cp("parallel", "arbitrary"),
    )(qi_arr, ki_arr, proj, proj, proj, aq, ak)


def _fox_bwd(proj, dycat, aqb, ak, ad, name):
    s = proj.shape[0]
    nq = s // TQ
    qi_arr, ki_arr = _causal_pairs(nq, key_major=True)

    def body(qi_ref, ki_ref, q_ref, k_ref, v_ref, do_ref, aq_ref, ak_ref, ad_ref,
             dq_ref, dk_ref, dv_ref, qaux_ref, kaux_ref, dq_acc, qaux_acc, dk_acc, dv_acc, kaux_acc):
        t = pl.program_id(1)
        qi, ki = qi_ref[t], ki_ref[t]
        lane = lax.broadcasted_iota(jnp.int32, (1, 128), 1)
        masks = [lane < 64, lane >= 64]
        ones_v = jnp.where((lane & 63) < 3, 1.0, 0.0).astype(BF16)
        top = lax.broadcasted_iota(jnp.int32, (128, 1), 0) < 64

        @pl.when(qi == ki)
        def _():
            dk_acc[...] = jnp.zeros_like(dk_acc)
            dv_acc[...] = jnp.zeros_like(dv_acc)
            kaux_acc[...] = jnp.zeros_like(kaux_acc)

        @pl.when(ki == 0)
        def _():
            for pp in range(STEP_PAIRS):
                dq_acc[pp * nq + qi] = jnp.zeros((128, TQ), F32)
                qaux_acc[pp * nq + qi] = jnp.zeros((128, TQ), F32)

        def step(diag):
            q2s = q_ref[...] * 0.125
            k2, v2, do2 = k_ref[...], v_ref[...], do_ref[...]
            aq2, ak2, ad2 = aq_ref[...], ak_ref[...], ad_ref[...]
            ones2 = jnp.concatenate([ones_v] * STEP_PAIRS, axis=1)
            for pp in range(STEP_PAIRS):
                cols = slice(128 * pp, 128 * (pp + 1))
                dq, dk, dv = [], [], []
                for hh in range(2):
                    qh = jnp.where(masks[hh], q2s[:, cols], aq2[:, cols])
                    kh = jnp.where(masks[hh], k2[:, cols], ak2[:, cols])
                    doh = jnp.where(masks[hh], do2[:, cols], ad2[:, cols])
                    vh = jnp.where(masks[hh], v2[:, cols], ones2[:, cols])
                    sc = _dot(kh, qh, NT)
                    if diag:
                        key = lax.broadcasted_iota(jnp.int32, sc.shape, 0)
                        qry = lax.broadcasted_iota(jnp.int32, sc.shape, 1)
                        sc = jnp.where(qry >= key, sc, NEG)
                    p = jnp.exp(sc)
                    dsb = (p * _dot(vh, doh, NT)).astype(BF16)
                    dv.append(_dot(p.astype(BF16), doh, NN))
                    dk.append(_dot(dsb, qh, NN))
                    dq.append(_dot(kh, dsb, TN))
                dk_acc[pp] += jnp.where(masks[0], dk[0], dk[1])
                kaux_acc[pp] += jnp.where(masks[0], dk[1], dk[0])
                dv_acc[pp] += jnp.where(masks[0], dv[0], dv[1])
                dq_acc[pp * nq + qi] += jnp.where(top, dq[0], dq[1])
                qaux_acc[pp * nq + qi] += jnp.where(top, dq[1], dq[0])

        @pl.when(qi > ki)
        def _():
            step(False)

        @pl.when(qi == ki)
        def _():
            step(True)
            rows = pl.ds(pl.multiple_of(qi * TQ, TQ), TQ)
            for pp in range(STEP_PAIRS):
                dq_ref[rows, 128 * pp:128 * (pp + 1)] = (dq_acc[pp * nq + qi] * 0.125).T.astype(BF16)
                qaux_ref[pp, rows, :] = qaux_acc[pp * nq + qi].T

        @pl.when(qi == nq - 1)
        def _():
            for pp in range(STEP_PAIRS):
                dk_ref[:, 128 * pp:128 * (pp + 1)] = dk_acc[pp].astype(BF16)
                dv_ref[:, 128 * pp:128 * (pp + 1)] = dv_acc[pp].astype(BF16)
            kaux_ref[...] = kaux_acc[...]

    wide = STEP_PAIRS * 128
    grid_spec = pltpu.PrefetchScalarGridSpec(
        num_scalar_prefetch=2, grid=(HEAD_PAIRS // STEP_PAIRS, int(qi_arr.shape[0])),
        in_specs=[pl.BlockSpec((TQ, wide), lambda p, t, qi, ki: (qi[t], Q_BLK // STEP_PAIRS + p)),
                  pl.BlockSpec((TQ, wide), lambda p, t, qi, ki: (ki[t], K_BLK // STEP_PAIRS + p)),
                  pl.BlockSpec((TQ, wide), lambda p, t, qi, ki: (ki[t], V_BLK // STEP_PAIRS + p)),
                  pl.BlockSpec((TQ, wide), lambda p, t, qi, ki: (qi[t], Q_BLK // STEP_PAIRS + p)),
                  pl.BlockSpec((TQ, wide), lambda p, t, qi, ki: (qi[t], p)),
                  pl.BlockSpec((TQ, wide), lambda p, t, qi, ki: (ki[t], p)),
                  pl.BlockSpec((TQ, wide), lambda p, t, qi, ki: (qi[t], p))],
        out_specs=[pl.BlockSpec((s, wide), lambda p, t, qi, ki: (0, p)),
                   pl.BlockSpec((TQ, wide), lambda p, t, qi, ki: (ki[t], p)),
                   pl.BlockSpec((TQ, wide), lambda p, t, qi, ki: (ki[t], p)),
                   pl.BlockSpec((STEP_PAIRS, s, 128), lambda p, t, qi, ki: (p, 0, 0)),
                   pl.BlockSpec((STEP_PAIRS, TQ, 128), lambda p, t, qi, ki: (p, ki[t], 0))],
        scratch_shapes=[pltpu.VMEM((STEP_PAIRS * nq, 128, TQ), F32), pltpu.VMEM((STEP_PAIRS * nq, 128, TQ), F32),
                        pltpu.VMEM((STEP_PAIRS, TQ, 128), F32), pltpu.VMEM((STEP_PAIRS, TQ, 128), F32),
                        pltpu.VMEM((STEP_PAIRS, TQ, 128), F32)])
    return pl.pallas_call(
        body, name=name, grid_spec=grid_spec,
        out_shape=[jax.ShapeDtypeStruct((s, D_FOX), BF16)] * 3 + [jax.ShapeDtypeStruct((HEAD_PAIRS, s, 128), F32)] * 2,
        compiler_params=_cp("arbitrary", "arbitrary"),
    )(qi_arr, ki_arr, proj, proj, proj, dycat, aqb, ak, ad)


XA_SCALE = XA_DIM ** -0.5


def _xattn_fwd(q2, kv, name):
    s = q2.shape[0]
    m = kv.shape[0]

    def body(q_ref, kv_ref, o_ref):
        heads = [slice(h * XA_DIM, (h + 1) * XA_DIM) for h in range(XA_HEADS)]
        scs = [_dot(q_ref[:, cols], kv_ref[:, cols], NT) for cols in heads]
        for h in range(XA_HEADS):
            c0 = h * XA_DIM
            sc = scs[h] * XA_SCALE
            e = jnp.exp(sc - jnp.max(sc, axis=1, keepdims=True))
            p = e / jnp.sum(e, axis=1, keepdims=True)
            o_ref[:, c0:c0 + XA_DIM] = _dot(p.astype(BF16), kv_ref[:, D_MODEL + c0:D_MODEL + c0 + XA_DIM], NN).astype(BF16)

    return pl.pallas_call(
        body, name=name, grid=(s // TM,),
        in_specs=[_row_spec(TM, D_MODEL), pl.BlockSpec((m, 2 * D_MODEL), lambda i: (0, 0))],
        out_specs=_row_spec(TM, D_MODEL), out_shape=jax.ShapeDtypeStruct((s, D_MODEL), BF16),
        compiler_params=_cp("parallel"),
    )(q2, kv)


def _xattn_bwd(q2, kv, do2, name):
    s = q2.shape[0]
    m = kv.shape[0]

    def body(q_ref, kv_ref, do_ref, dq_ref, dkv_ref):
        i = pl.program_id(0)

        @pl.when(i == 0)
        def _():
            dkv_ref[...] = jnp.zeros_like(dkv_ref)

        heads = [slice(h * XA_DIM, (h + 1) * XA_DIM) for h in range(XA_HEADS)]
        scs = [_dot(kv_ref[:, cols], q_ref[:, cols], NT) for cols in heads]
        dps = [_dot(kv_ref[:, D_MODEL + cols.start:D_MODEL + cols.stop], do_ref[:, cols], NT) for cols in heads]
        for h in range(XA_HEADS):
            c0 = h * XA_DIM
            v0 = D_MODEL + c0
            qh = q_ref[:, c0:c0 + XA_DIM]
            kh = kv_ref[:, c0:c0 + XA_DIM]
            doh = do_ref[:, c0:c0 + XA_DIM]
            sc = scs[h] * XA_SCALE
            e = jnp.exp(sc - jnp.max(sc, axis=0, keepdims=True))
            p = e / jnp.sum(e, axis=0, keepdims=True)
            dp = dps[h]
            ds = p * (dp - jnp.sum(p * dp, axis=0, keepdims=True))
            dsb = (ds * XA_SCALE).astype(BF16)
            dq_ref[:, c0:c0 + XA_DIM] = _dot(kh, dsb, TN).T.astype(BF16)
            dkv_ref[:, c0:c0 + XA_DIM] += _dot(dsb, qh, NN)
            dkv_ref[:, v0:v0 + XA_DIM] += _dot(p.astype(BF16), doh, NN)

    return pl.pallas_call(
        body, name=name, grid=(s // TM,),
        in_specs=[_row_spec(TM, D_MODEL), pl.BlockSpec((m, 2 * D_MODEL), lambda i: (0, 0)), _row_spec(TM, D_MODEL)],
        out_specs=[_row_spec(TM, D_MODEL), pl.BlockSpec((m, 2 * D_MODEL), lambda i: (0, 0))],
        out_shape=[jax.ShapeDtypeStruct((s, D_MODEL), BF16), jax.ShapeDtypeStruct((m, 2 * D_MODEL), F32)],
        compiler_params=_cp("arbitrary"),
    )(q2, kv, do2)


GELU_C = math.sqrt(2.0 / math.pi)
GELU_A = 0.044715


def _gelu(x):
    return (0.5 * x) * (1.0 + jnp.tanh(x * (GELU_C * GELU_A * (x * x) + GELU_C)))


def _gelu_and_grad(x):
    x2 = x * x
    s = 1.0 + jnp.tanh(x * (GELU_C * GELU_A * x2 + GELU_C))
    hx = 0.5 * x
    return hx * s, s * (0.5 + hx * (2.0 - s) * (3.0 * GELU_C * GELU_A * x2 + GELU_C))


def _conv(h, s1, s2, w_ref, b_ref):
    return w_ref[0:1, :] * s2 + w_ref[1:2, :] * s1 + w_ref[2:3, :] * h + b_ref[...]


def _shift_down(main, prev8):
    row = lax.broadcasted_iota(jnp.int32, main.shape, 0)
    s1 = jnp.where(row == 0, prev8[7:8, :], pltpu.roll(main, 1, 0))
    s2 = jnp.where(row == 0, prev8[6:7, :], jnp.where(row == 1, prev8[7:8, :], pltpu.roll(main, 2, 0)))
    return s1, s2


def _shift_up(main, next8):
    n = main.shape[0]
    row = lax.broadcasted_iota(jnp.int32, main.shape, 0)
    u1 = jnp.where(row == n - 1, next8[0:1, :], pltpu.roll(main, n - 1, 0))
    u2 = jnp.where(row == n - 2, next8[0:1, :], jnp.where(row == n - 1, next8[1:2, :], pltpu.roll(main, n - 2, 0)))
    return u1, u2


def _ffn_fwd(h3, w_up, cw, cb, w_down, x2, tgt, g_post, name):
    s = h3.shape[0]
    tn = TN_FF
    nj = D_FF // tn
    per = D_MODEL // tn
    hb = TM // 8

    def body(h_ref, halo_ref, wg_ref, wu_ref, cwg_ref, cwu_ref, cbg_ref, cbu_ref, wd_ref, x_ref, t_ref, g_ref,
             hg_ref, hu_ref, cg_ref, cu_ref, a_ref, loss_ref, dx_ref, dy_ref, dg_ref, y_acc):
        i, j = pl.program_id(0), pl.program_id(1)

        @pl.when(j == 0)
        def _():
            y_acc[...] = jnp.zeros_like(y_acc)

        h = h_ref[...]
        halo = halo_ref[...]
        halo = jnp.where(i > 0, halo, jnp.zeros_like(halo))
        hid = [(_dot(h, w_ref[...], NN), _dot(halo, w_ref[...], NN)) for w_ref in (wg_ref, wu_ref)]
        conv = []
        for (hm, hm_halo), cw_ref, cb_ref, hid_ref, c_ref in zip(hid, (cwg_ref, cwu_ref), (cbg_ref, cbu_ref),
                                                                  (hg_ref, hu_ref), (cg_ref, cu_ref)):
            hid_ref[...] = hm.astype(BF16)
            s1, s2 = _shift_down(hm, hm_halo)
            c = _conv(hm, s1, s2, cw_ref, cb_ref)
            c_ref[...] = c.astype(BF16)
            conv.append(c)
        a = (_gelu(conv[0]) * conv[1]).astype(BF16)
        a_ref[...] = a
        y_acc[...] += _dot(a, wd_ref[...], NN)

        @pl.when(j == nj - 1)
        def _():
            yv = y_acc[...]
            r = _rstd(yv)
            yn = yv * r
            e = x_ref[...] + yn * g_ref[...] - t_ref[...]
            part = 0.5 * jnp.sum(jnp.mean(e * e, axis=-1, keepdims=True), axis=0, keepdims=True)
            part = jnp.broadcast_to(part, (1, 128))
            dx = e * (1.0 / D_MODEL)
            dx_ref[...] = dx
            dy_ref[...] = _norm_bwd_rows(dx * g_ref[...], yn, r).astype(BF16)
            dg = jnp.sum(dx * yn, axis=0, keepdims=True)

            @pl.when(i == 0)
            def _():
                dg_ref[...] = dg
                loss_ref[...] = part

            @pl.when(i > 0)
            def _():
                dg_ref[...] += dg
                loss_ref[...] += part

    rows = pl.BlockSpec((TM, D_MODEL), lambda i, j: (i, 0))
    tile = pl.BlockSpec((TM, tn), lambda i, j: (i, j))
    wide = jax.ShapeDtypeStruct((s, D_FF), BF16)
    return pl.pallas_call(
        body, name=name, grid=(s // TM, nj),
        in_specs=[rows,
                  pl.BlockSpec((8, D_MODEL), lambda i, j: (jnp.maximum(i * hb - 1, 0), 0)),
                  pl.BlockSpec((None, D_MODEL, tn), lambda i, j: (j // per, 0, j % per)),
                  pl.BlockSpec((None, D_MODEL, tn), lambda i, j: (NDEV // 2 + j // per, 0, j % per)),
                  pl.BlockSpec((8, tn), lambda i, j: (0, j)),
                  pl.BlockSpec((8, tn), lambda i, j: (0, nj + j)),
                  pl.BlockSpec((1, tn), lambda i, j: (0, j)),
                  pl.BlockSpec((1, tn), lambda i, j: (0, nj + j)),
                  pl.BlockSpec((tn, D_MODEL), lambda i, j: (j, 0)),
                  rows, rows, pl.BlockSpec((1, D_MODEL), lambda i, j: (0, 0))],
        out_specs=[tile, tile, tile, tile, tile,
                   pl.BlockSpec((1, 128), lambda i, j: (0, 0)), rows, rows,
                   pl.BlockSpec((1, D_MODEL), lambda i, j: (0, 0))],
        out_shape=[wide, wide, wide, wide, wide,
                   jax.ShapeDtypeStruct((1, 128), F32), jax.ShapeDtypeStruct((s, D_MODEL), F32),
                   jax.ShapeDtypeStruct((s, D_MODEL), BF16), jax.ShapeDtypeStruct((1, D_MODEL), F32)],
        scratch_shapes=[pltpu.VMEM((TM, D_MODEL), F32)],
        compiler_params=_cp("arbitrary", "arbitrary"),
    )(h3, h3, w_up, w_up, cw, cw, cb, cb, w_down, x2, tgt, g_post)


def _ffn_bwd(dy3, w_down, hid_g, hid_u, conv_g, conv_u, cw, name):
    s = dy3.shape[0]
    n = s // TM
    tn = TN_FF
    nj = D_FF // tn
    hb = TM // 8
    last8 = s // 8 - 1

    def body(dy_ref, dyn_ref, wd_ref, hg_ref, hu_ref, cg_ref, cgn_ref, cu_ref, cun_ref, cwg_ref, cwu_ref,
             dhg_ref, dhu_ref, dcwg_ref, dcwu_ref, dcbg_ref, dcbu_ref):
        i = pl.program_id(1)
        first, last = i == 0, i == n - 1
        @pl.when(first)
        def _():
            for ref in (dcwg_ref, dcwu_ref, dcbg_ref, dcbu_ref):
                ref[...] = jnp.zeros_like(ref)

        dyn = dyn_ref[...]
        dyn = jnp.where(last, jnp.zeros_like(dyn), dyn)
        wc = tn // 4
        chunks = [slice(c, c + wc) for c in range(0, tn, wc)]
        das = [(_dot(dy_ref[...], wd_ref[cols, :], NT), _dot(dyn, wd_ref[cols, :], NT)) for cols in chunks]
        row8 = lax.broadcasted_iota(jnp.int32, (8, wc), 0)
        for cols, (da, da_n) in zip(chunks, das):
            c_g, c_u = cg_ref[:, cols].astype(F32), cu_ref[:, cols].astype(F32)
            g, dg = _gelu_and_grad(c_g)
            gn, dgn = _gelu_and_grad(cgn_ref[:, cols].astype(F32))
            outs = ((da * c_u * dg, da_n * cun_ref[:, cols].astype(F32) * dgn, hg_ref, cwg_ref, dhg_ref, dcwg_ref, dcbg_ref),
                    (da * g, da_n * gn, hu_ref, cwu_ref, dhu_ref, dcwu_ref, dcbu_ref))
            for dc, dcn, h_ref, cw_ref, dh_ref, dcw_ref, dcb_ref in outs:
                u1, u2 = _shift_up(dc, dcn)
                dh_ref[:, cols] = (cw_ref[2:3, cols] * dc + cw_ref[1:2, cols] * u1 + cw_ref[0:1, cols] * u2).astype(BF16)
                hm = h_ref[:, cols].astype(F32)
                dcb = jnp.sum(dc, axis=0, keepdims=True)
                dcw = jnp.where(row8 == 0, jnp.sum(hm * u2, axis=0, keepdims=True),
                                jnp.where(row8 == 1, jnp.sum(hm * u1, axis=0, keepdims=True),
                                          jnp.where(row8 == 2, jnp.sum(hm * dc, axis=0, keepdims=True), 0.0)))
                dcw_ref[:, cols] += dcw
                dcb_ref[:, cols] += dcb

    next8 = lambda j, i: (jnp.minimum((i + 1) * hb, last8), j)
    blk = lambda j, i: (i, j)
    col = lambda j, i: (0, j)
    colu = lambda j, i: (0, nj + j)
    tile = pl.BlockSpec((TM, tn), blk)
    return pl.pallas_call(
        body, name=name, grid=(nj, n),
        in_specs=[pl.BlockSpec((TM, D_MODEL), lambda j, i: (i, 0)),
                  pl.BlockSpec((8, D_MODEL), lambda j, i: (jnp.minimum((i + 1) * hb, last8), 0)),
                  pl.BlockSpec((tn, D_MODEL), lambda j, i: (j, 0)),
                  tile, tile, tile, pl.BlockSpec((8, tn), next8), tile, pl.BlockSpec((8, tn), next8),
                  pl.BlockSpec((8, tn), col), pl.BlockSpec((8, tn), colu)],
        out_specs=[tile, tile, pl.BlockSpec((8, tn), col), pl.BlockSpec((8, tn), col),
                   pl.BlockSpec((1, tn), col), pl.BlockSpec((1, tn), col)],
        out_shape=[jax.ShapeDtypeStruct((s, D_FF), BF16), jax.ShapeDtypeStruct((s, D_FF), BF16),
                   jax.ShapeDtypeStruct((8, D_FF), F32), jax.ShapeDtypeStruct((8, D_FF), F32),
                   jax.ShapeDtypeStruct((1, D_FF), F32), jax.ShapeDtypeStruct((1, D_FF), F32)],
        compiler_params=_cp("parallel", "arbitrary"),
    )(dy3, dy3, w_down, hid_g, hid_u, conv_g, conv_g, conv_u, conv_u, cw, cw)


def _slot(p):
    return 4 * p[0] + 2 * p[1] + p[2]


def _all_gather(shards, name):
    n = len(shards)

    def body(*refs):
        ins, outs = refs[:n], refs[n:2 * n]
        send_sems, recv_sems, local_sems = refs[2 * n:]
        x, y, c = lax.axis_index("x"), lax.axis_index("y"), lax.axis_index("c")
        me, sibling = (x, y, c), (x, y, 1 - c)
        chips = [(1 - x, y), (x, 1 - y), (1 - x, 1 - y)]

        def copy(a, k, block, to, from_input=False):
            dst = outs[a].at[_slot(block)]
            return pltpu.make_async_remote_copy(
                src_ref=ins[a] if from_input else dst, dst_ref=dst,
                send_sem=send_sems.at[a, k], recv_sem=recv_sems.at[a, k],
                device_id=to, device_id_type=MESH)

        mine = [pltpu.make_async_copy(ins[a], outs[a].at[_slot(me)], local_sems.at[a]) for a in range(n)]
        for cp in mine:
            cp.start()
        first = []
        for a in range(n):
            first.append(copy(a, 0, me, sibling, True))
            first += [copy(a, 1 + j, me, (*chip, c), True) for j, chip in enumerate(chips)]
        for cp in first:
            cp.start()
        passed = []
        for j, chip in enumerate(chips):
            for a in range(n):
                copy(a, 1 + j, (*chip, c), me).wait_recv()
                fwd = copy(a, 4 + j, (*chip, c), sibling)
                fwd.start()
                passed.append(fwd)
        for a in range(n):
            copy(a, 0, sibling, me).wait_recv()
            for j, chip in enumerate(chips):
                copy(a, 4 + j, (*chip, 1 - c), me).wait_recv()
        for cp in first + passed:
            cp.wait_send()
        for cp in mine:
            cp.wait()

    any_spec = pl.BlockSpec(memory_space=pl.ANY)
    return pl.pallas_call(
        body, name=name,
        in_specs=[any_spec] * n, out_specs=[any_spec] * n,
        out_shape=[jax.ShapeDtypeStruct((NDEV,) + s.shape, s.dtype) for s in shards],
        scratch_shapes=[pltpu.SemaphoreType.DMA((n, 7)), pltpu.SemaphoreType.DMA((n, 7)),
                        pltpu.SemaphoreType.DMA((n,))],
    )(*shards)


def _peer_list(x, y, c):
    return [(1 - x if m & 4 else x, 1 - y if m & 2 else y, 1 - c if m & 1 else c) for m in range(1, NDEV)]


def _exchange_copies(src_refs, land_refs, send_sems, recv_sems, gather):
    x, y, c = lax.axis_index("x"), lax.axis_index("y"), lax.axis_index("c")
    me = (x, y, c)
    copies = []
    for m, peer in enumerate(_peer_list(x, y, c)):
        for a in range(len(src_refs)):
            copies.append(pltpu.make_async_remote_copy(
                src_ref=src_refs[a] if gather else src_refs[a].at[_slot(peer)], dst_ref=land_refs[a].at[_slot(me)],
                send_sem=send_sems.at[a * (NDEV - 1) + m], recv_sem=recv_sems.at[a * (NDEV - 1) + m],
                device_id=peer, device_id_type=MESH))
    return copies


def _all_gather_small(shards, name):
    n = len(shards)

    def body(*refs):
        ins, outs = refs[:n], refs[n:2 * n]
        send_sems, recv_sems, local_sems = refs[2 * n:]
        me = (lax.axis_index("x"), lax.axis_index("y"), lax.axis_index("c"))
        mine = [pltpu.make_async_copy(ins[a], outs[a].at[_slot(me)], local_sems.at[a]) for a in range(n)]
        copies = _exchange_copies(ins, outs, send_sems, recv_sems, True)
        for cp in mine + copies:
            cp.start()
        for cp in copies + mine:
            cp.wait()

    any_spec = pl.BlockSpec(memory_space=pl.ANY)
    return pl.pallas_call(
        body, name=name,
        in_specs=[any_spec] * n, out_specs=[any_spec] * n,
        out_shape=[jax.ShapeDtypeStruct((NDEV,) + s.shape, s.dtype) for s in shards],
        scratch_shapes=[pltpu.SemaphoreType.DMA((n * (NDEV - 1),)), pltpu.SemaphoreType.DMA((n * (NDEV - 1),)),
                        pltpu.SemaphoreType.DMA((n,))],
    )(*shards)


def _exchange_start(srcs, lands, after, gather, name):
    n = len(srcs)
    hbm = pl.BlockSpec(memory_space=pltpu.HBM)

    def body(*refs):
        for cp in _exchange_copies(refs[:n], refs[n:2 * n], refs[2 * n + 1], refs[2 * n + 2], gather):
            cp.start()
        token = refs[-1]
        token[...] = jnp.zeros_like(token)

    outs = pl.pallas_call(
        body, name=name,
        out_shape=(pltpu.SemaphoreType.DMA((n * (NDEV - 1),)), pltpu.SemaphoreType.DMA((n * (NDEV - 1),)),
                   *[pltpu.HBM(a.shape, a.dtype) for a in list(srcs) + list(lands)],
                   jax.ShapeDtypeStruct((8, 128), F32)),
        in_specs=[hbm] * (2 * n) + [pl.BlockSpec(memory_space=pl.ANY)],
        out_specs=(pl.BlockSpec(memory_space=pltpu.SEMAPHORE), pl.BlockSpec(memory_space=pltpu.SEMAPHORE),
                   *[hbm] * (2 * n), pl.BlockSpec(memory_space=pltpu.VMEM)),
        input_output_aliases={i: 2 + i for i in range(2 * n)},
        compiler_params=pltpu.CompilerParams(has_side_effects=pltpu.SideEffectType.DATAFLOW_SIDE_EFFECTING),
    )(*[pltpu.with_memory_space_constraint(a, pltpu.HBM) for a in list(srcs) + list(lands)], after)
    return outs[0], outs[1], outs[2:2 + n], outs[2 + n:2 + 2 * n], outs[-1]


def _exchange_wait(send_sems, recv_sems, srcs, lands, after, gather, name):
    n = len(srcs)
    hbm = pl.BlockSpec(memory_space=pltpu.HBM)

    def body(*refs):
        for cp in _exchange_copies(refs[:n], refs[n:2 * n], refs[2 * n], refs[2 * n + 1], gather):
            cp.wait_send()
            cp.wait_recv()

    outs = pl.pallas_call(
        body, name=name,
        out_shape=tuple(pltpu.HBM(a.shape, a.dtype) for a in list(srcs) + list(lands)),
        in_specs=[hbm] * (2 * n) + [pl.BlockSpec(memory_space=pltpu.SEMAPHORE)] * 2 + [pl.BlockSpec(memory_space=pl.ANY)],
        out_specs=tuple([hbm] * (2 * n)),
        input_output_aliases={i: i for i in range(2 * n)},
        compiler_params=pltpu.CompilerParams(has_side_effects=pltpu.SideEffectType.DATAFLOW_SIDE_EFFECTING),
    )(*srcs, *lands, send_sems, recv_sems, after)
    return outs[n:]


def _own_slot(block):
    me = 4 * lax.axis_index("x") + 2 * lax.axis_index("y") + lax.axis_index("c")
    return lax.dynamic_update_slice(lax.empty((NDEV,) + block.shape, block.dtype), block[None], (me, 0, 0))


def _adam_update(p_ref, w_ref, m_ref, v_ref, g_ref, d_ref, mo_ref, vo_ref):
    bc1 = 1.0 - ADAM_B1 ** ADAM_STEP
    bc2 = 1.0 - ADAM_B2 ** ADAM_STEP
    g = p_ref[0].astype(F32)
    for d in range(1, NDEV):
        g = g + p_ref[d].astype(F32)
    g_ref[...] = g
    mn = ADAM_B1 * m_ref[...] + (1.0 - ADAM_B1) * g
    vn = ADAM_B2 * v_ref[...] + (1.0 - ADAM_B2) * (g * g)
    mo_ref[...] = mn
    vo_ref[...] = vn
    d_ref[...] = -ADAM_LR * ((mn / bc1) / (jnp.sqrt(vn / bc2) + ADAM_EPS) + ADAM_WD * w_ref[...])


def _adamw_small(parts, ws, ms, vs, name):
    n = len(ws)

    def body(*refs):
        ins, outs = refs[:4 * n], refs[4 * n:]
        for k in range(n):
            _adam_update(ins[k], ins[n + k], ins[2 * n + k], ins[3 * n + k], *outs[4 * k:4 * k + 4])

    whole = pl.BlockSpec(memory_space=pltpu.VMEM)
    res = pl.pallas_call(
        body, name=name, in_specs=[whole] * (4 * n), out_specs=[whole] * (4 * n),
        out_shape=[jax.ShapeDtypeStruct(a.shape, F32) for a in ws for _ in range(4)],
    )(*parts, *ws, *ms, *vs)
    return [res[4 * k:4 * k + 4] for k in range(n)]


def _adamw(parts, w, m, v, name):
    r, c = w.shape
    tr = r if r * c <= 160 * 1024 else max(8, (160 * 1024 // c) // 8 * 8)
    while r % tr:
        tr -= 8
    body = functools.partial(_adam_update)
    spec = pl.BlockSpec((tr, c), lambda i: (i, 0))
    return pl.pallas_call(
        body, name=name, grid=(r // tr,),
        in_specs=[pl.BlockSpec((NDEV, tr, c), lambda i: (0, i, 0)), spec, spec, spec],
        out_specs=[spec] * 4, out_shape=[jax.ShapeDtypeStruct((r, c), F32)] * 4,
        compiler_params=_cp("parallel"),
    )(parts, w, m, v)


def _local_step(x, mem, tgt, gains, b_forget, w_pool, pool_scale, conv_b, w_in,
                mix_weights, ffn_weights, send_in_grad, send_mix_grads, send_ffn_grads):
    b_pad = jnp.pad(b_forget, ((0, 0), (0, 128 - FOX_HEADS)))
    wbd = jnp.zeros((D_POOL, D_POOL), F32)
    for g in range(4):
        wbd = wbd.at[64 * g:64 * g + 64, 64 * g:64 * g + 64].set(w_pool[g])
    wbd = wbd.astype(BF16)
    scale = pool_scale.reshape(1, D_POOL)

    h1, proj, fraw = _proj_in(x, gains["mix_pre"], w_in, "proj_in")
    flog, aq, ak = _gate_cumsum(fraw, b_pad, "gate_cumsum")
    ycat, aqb = _fox_fwd(proj, aq, ak, "fox_fwd")
    ycat = _pool_fwd(proj, wbd, scale, ycat, "pool_fwd")
    w_mix, w_xq, w_xo, w_xkv = mix_weights(ycat)
    y1, x1, h2 = _mm_rows(ycat, w_mix, "nn", 1024, "mix_out", [x], [gains["mix_post"], gains["xa_pre"]],
                          [F32, F32, BF16], _epi_resid)
    q2 = _mm(h2, w_xq, "nn", BF16, 2048, 1024, 1024, "xa_q")
    mem_n = _norm_fwd(mem, gains["mem"], "norm_mem")
    kv = _mm(mem_n, w_xkv, "nn", BF16, mem.shape[0], 256, 1024, "xa_kv", b_cols=256)
    o2 = _xattn_fwd(q2, kv, "xattn_fwd")
    y2, x2, h3 = _mm_rows(o2, w_xo, "nn", 1024, "xa_out", [x1], [gains["xa_post"], gains["ffn_pre"]],
                          [F32, F32, BF16], _epi_resid)
    w_up, w_down, cw = ffn_weights(h3)
    hid_g, hid_u, conv_g, conv_u, act, loss, dx3, dy3, dg_ffn_post = _ffn_fwd(
        h3, w_up, cw, conv_b, w_down, x2, tgt, gains["ffn_post"], "ffn_fwd")

    dhid_g, dhid_u, dcw_g, dcw_u, dcb_g, dcb_u = _ffn_bwd(dy3, w_down, hid_g, hid_u, conv_g, conv_u, cw, "ffn_bwd")
    d_w_down = _mm(act, dy3, "tn", BF16, 2048, 1024, 1024, "dw_down")
    d_w_up = _mm(h3, [dhid_g, dhid_u], "tn", BF16, 1024, 1024, 2048, "dw_up", out_cols=1024)
    sent = send_ffn_grads(d_w_up, d_w_down, jnp.concatenate([dcw_g, dcw_u], axis=1))
    dh3 = _mm([dhid_g, dhid_u], w_up, "nt", F32, 2048, 1024, 1024, "dh_ffn", b_cols=1024, after=sent)
    dx2, dg_ffn_pre, dy2, dg_xa_post = _norm_bwd(dh3, x2, dx3, gains["ffn_pre"], "norm_bwd_ffn",
                                                 prev=(y2, gains["xa_post"]))
    do2 = _mm(dy2, w_xo, "nt", BF16, 2048, 1024, 1024, "d_xa_out")
    d_w_xo = _mm(o2, dy2, "tn", BF16, 1024, 1024, 1024, "dw_xo")
    dq2, dkv = _xattn_bwd(q2, kv, do2, "xattn_bwd")
    dkv = dkv.astype(BF16)
    dx1, dg_xa_pre, dy1, dg_mix_post = _mm_rows(
        dq2, w_xq, "nt", 1024, "dh_xa", [x1, dx2, y1], [gains["xa_pre"], gains["mix_post"]],
        [F32, "sum", BF16, "sum"], _epi_norm_bwd)
    d_w_xq = _mm(h2, dq2, "tn", BF16, 1024, 1024, 1024, "dw_xq")
    dmem_n = _mm(dkv, w_xkv, "nt", F32, mem.shape[0], 1024, 256, "d_mem", b_cols=256)
    d_w_xkv = _mm(mem_n, dkv, "tn", BF16, 1024, 256, mem.shape[0], "dw_xkv", out_cols=256)
    _, dg_mem = _norm_bwd(dmem_n, mem, jnp.zeros_like(mem), gains["mem"], "norm_bwd_mem")
    dycat = _mm(dy1, w_mix, "nt", BF16, 2048, 1024, 1024, "d_mix_out")
    d_w_mix = _mm(ycat, dy1, "tn", BF16, 1024, 1024, 1024, "dw_mix")
    sent_mix = send_mix_grads(d_w_mix, d_w_xq, d_w_xo, d_w_xkv)
    ad = _fox_do_operand(dycat, ycat, sent_mix, "fox_do_operand")
    dq, dk, dv, qaux, kaux = _fox_bwd(proj, dycat, aqb, ak, ad, "fox_bwd")
    du, d_wbd, d_scale = _pool_bwd(proj, dycat, wbd, scale, "pool_bwd")
    df, db_f = _gate_bwd(qaux, kaux, flog, "gate_bwd")
    dproj = [du, dq, dk, dv, df]
    sent_in = send_in_grad(_dw_in(h1, dproj, "dw_in"))
    grad_x, dg_mix_pre = _mm_rows(dproj, w_in, "nt", None, "dh_mix", [x, dx1], [gains["mix_pre"]],
                                  [F32, "sum"], _epi_norm_bwd, after=sent_in)

    small = dict(
        mix_pre=dg_mix_pre, mix_post=dg_mix_post, mem=dg_mem, xa_pre=dg_xa_pre, xa_post=dg_xa_post,
        ffn_pre=dg_ffn_pre, ffn_post=dg_ffn_post,
        conv_b=jnp.concatenate([dcb_g, dcb_u], axis=1),
        w_pool=jnp.concatenate([d_wbd[64 * g:64 * g + 64, 64 * g:64 * g + 64] for g in range(4)], axis=0),
        pool_scale=d_scale.reshape(4, 64),
        b_forget=db_f[:, :FOX_HEADS],
    )
    return loss, grad_x, small


SMALL_ORDER = ("mix_pre", "mix_post", "mem", "xa_pre", "xa_post", "ffn_pre", "ffn_post", "conv_b",
               "w_pool", "pool_scale", "b_forget")


def kernel(x, mem, norm_mix_pre, norm_mix_post, w_in, b_forget, w_pool, pool_scale, w_mix_out, norm_mem, norm_xa_pre, norm_xa_post, w_xq, w_xkv, w_xo, norm_ffn_pre, norm_ffn_post, w_up, conv_w, conv_b, w_down, loss_target, m_norm_mix_pre, m_norm_mix_post, m_w_in, m_b_forget, m_w_pool, m_pool_scale, m_w_mix_out, m_norm_mem, m_norm_xa_pre, m_norm_xa_post, m_w_xq, m_w_xkv, m_w_xo, m_norm_ffn_pre, m_norm_ffn_post, m_w_up, m_conv_w, m_conv_b, m_w_down, v_norm_mix_pre, v_norm_mix_post, v_w_in, v_b_forget, v_w_pool, v_pool_scale, v_w_mix_out, v_norm_mem, v_norm_xa_pre, v_norm_xa_post, v_w_xq, v_w_xkv, v_w_xo, v_norm_ffn_pre, v_norm_ffn_post, v_w_up, v_conv_w, v_conv_b, v_w_down):
    names = ("norm_mix_pre", "norm_mix_post", "w_in", "b_forget", "w_pool", "pool_scale", "w_mix_out", "norm_mem",
             "norm_xa_pre", "norm_xa_post", "w_xq", "w_xkv", "w_xo", "norm_ffn_pre", "norm_ffn_post", "w_up",
             "conv_w", "conv_b", "w_down")
    w = dict(zip(names, (norm_mix_pre, norm_mix_post, w_in, b_forget, w_pool, pool_scale, w_mix_out, norm_mem,
                         norm_xa_pre, norm_xa_post, w_xq, w_xkv, w_xo, norm_ffn_pre, norm_ffn_post, w_up,
                         conv_w, conv_b, w_down)))
    mo = dict(zip(names, (m_norm_mix_pre, m_norm_mix_post, m_w_in, m_b_forget, m_w_pool, m_pool_scale, m_w_mix_out,
                          m_norm_mem, m_norm_xa_pre, m_norm_xa_post, m_w_xq, m_w_xkv, m_w_xo, m_norm_ffn_pre,
                          m_norm_ffn_post, m_w_up, m_conv_w, m_conv_b, m_w_down)))
    vo = dict(zip(names, (v_norm_mix_pre, v_norm_mix_post, v_w_in, v_b_forget, v_w_pool, v_pool_scale, v_w_mix_out,
                          v_norm_mem, v_norm_xa_pre, v_norm_xa_post, v_w_xq, v_w_xkv, v_w_xo, v_norm_ffn_pre,
                          v_norm_ffn_post, v_w_up, v_conv_w, v_conv_b, v_w_down)))

    big_names = ("w_in", "w_mix_out", "w_xq", "w_xo", "w_xkv", "w_up", "w_down")
    shards = {k: w[k][0].astype(BF16) for k in big_names}
    shards["w_in"] = jnp.pad(shards["w_in"], ((0, 0), (0, D_IN_PAD - shards["w_in"].shape[1])))
    conv_w_sh = jnp.pad(conv_w[0, :, 0, :], ((0, 5), (0, 0)))
    (g_in,) = _all_gather([shards["w_in"]], "gather_w_in")
    mix_srcs = [shards[k] for k in ("w_mix_out", "w_xq", "w_xo", "w_xkv")]
    mix_flight = _exchange_start(mix_srcs, [_own_slot(a) for a in mix_srcs], g_in, True, "gather_mix_start")
    ffn_srcs = [shards["w_up"], shards["w_down"], conv_w_sh]
    ffn_flight = _exchange_start(ffn_srcs, [_own_slot(a) for a in ffn_srcs], mix_flight[4], True, "gather_ffn_start")
    my_slot = 4 * lax.axis_index("x") + 2 * lax.axis_index("y") + lax.axis_index("c")
    own_block = lambda a: _own_slot(lax.dynamic_index_in_dim(a, my_slot, 0, keepdims=False))
    by_rows = lambda a: a.reshape(NDEV, a.shape[0] // NDEV, a.shape[1])
    by_cols = lambda a: a.reshape(a.shape[0], NDEV, a.shape[1] // NDEV).transpose(1, 0, 2)
    grad_flight = {}

    def mix_weights(after):
        g_mix, g_xq, g_xo, g_xkv = _exchange_wait(*mix_flight[:4], after, True, "gather_mix_wait")
        return (g_mix.reshape(D_MODEL, D_MODEL), g_xq.reshape(D_MODEL, D_MODEL), g_xo.reshape(D_MODEL, D_MODEL), g_xkv)

    def ffn_weights(after):
        g_up, g_down, g_cw = _exchange_wait(*ffn_flight[:4], after, True, "gather_ffn_wait")
        return g_up, g_down.reshape(D_FF, D_MODEL), g_cw.transpose(1, 0, 2).reshape(8, 2 * D_FF)

    def send_ffn_grads(d_w_up, d_w_down, d_cw):
        srcs = [d_w_up, by_rows(d_w_down), by_cols(d_cw)]
        grad_flight["ffn"] = _exchange_start(srcs, [own_block(a) for a in srcs], ffn_flight[4], False, "scatter_ffn_start")
        return grad_flight["ffn"][4]

    def send_mix_grads(d_w_mix, d_w_xq, d_w_xo, d_w_xkv):
        srcs = [by_rows(d_w_mix), by_rows(d_w_xq), by_rows(d_w_xo), d_w_xkv]
        grad_flight["mix"] = _exchange_start(srcs, [own_block(a) for a in srcs], ffn_flight[4], False, "scatter_mix_start")
        return grad_flight["mix"][4]

    def send_in_grad(d_w_in):
        srcs = [by_rows(d_w_in)]
        grad_flight["in"] = _exchange_start(srcs, [own_block(a) for a in srcs], ffn_flight[4], False, "scatter_in_start")
        return grad_flight["in"][4]

    gains = dict(mix_pre=norm_mix_pre + ffn_flight[4][0, 0], mix_post=norm_mix_post, mem=norm_mem, xa_pre=norm_xa_pre,
                 xa_post=norm_xa_post, ffn_pre=norm_ffn_pre, ffn_post=norm_ffn_post)
    loss, grad_x, small = _local_step(
        x[0], mem[0], loss_target[0], gains, b_forget, w_pool[0], pool_scale[0], conv_b,
        g_in.reshape(D_MODEL, D_IN_PAD), mix_weights, ffn_weights, send_in_grad, send_mix_grads, send_ffn_grads)

    small_srcs = [small[k] for k in SMALL_ORDER] + [loss]
    small_flight = _exchange_start(small_srcs, [_own_slot(a) for a in small_srcs], grad_x, True, "gather_small_start")
    p_up, p_down, p_cw = _exchange_wait(*grad_flight["ffn"][:4], small_flight[4], False, "scatter_ffn_wait")
    p_mix, p_xq, p_xo, p_xkv = _exchange_wait(*grad_flight["mix"][:4], small_flight[4], False, "scatter_mix_wait")
    parts = dict(w_mix_out=p_mix, w_xq=p_xq, w_xo=p_xo, w_xkv=p_xkv, w_up=p_up, w_down=p_down)

    res = {k: [a[None] for a in _adamw(p, w[k][0], mo[k][0], vo[k][0], "adamw_" + k)] for k, p in parts.items()}
    pad_cw = lambda a: jnp.pad(a[0, :, 0, :], ((0, 5), (0, 0)))
    res["conv_w"] = [a[:3][None, :, None, :] for a in
                     _adamw(p_cw, pad_cw(conv_w), pad_cw(m_conv_w), pad_cw(v_conv_w), "adamw_conv_w")]
    key_of = dict(mix_pre="norm_mix_pre", mix_post="norm_mix_post", mem="norm_mem", xa_pre="norm_xa_pre",
                  xa_post="norm_xa_post", ffn_pre="norm_ffn_pre", ffn_post="norm_ffn_post", conv_b="conv_b",
                  w_pool="w_pool", pool_scale="pool_scale", b_forget="b_forget")
    flat2d = lambda src: [src[key_of[k]].reshape(small[k].shape) for k in SMALL_ORDER]
    *small_parts, loss_parts = _exchange_wait(*small_flight[:4], res["w_down"][1], True, "gather_small_wait")
    small_out =_adamw_small(small_parts, flat2d(w), flat2d(mo), flat2d(vo), "adamw_small")
    for k, four in zip(SMALL_ORDER, small_out):
        res[key_of[k]] = [a.reshape(w[key_of[k]].shape) for a in four]
    (p_in,) = _exchange_wait(*grad_flight["in"][:4], res["w_up"][1], False, "scatter_in_wait")
    res["w_in"] = [a[None] for a in _adamw(p_in[:, :, :w_in.shape[2]], w["w_in"][0], mo["w_in"][0], vo["w_in"][0],
                                           "adamw_w_in")]

    outs = [jnp.sum(loss_parts[:, 0, 0]), grad_x[None]]
    for idx in range(4):
        outs += [res[k][idx] for k in names]
    return tuple(outs)
```
